```python
import jax, jax.numpy as jnp
from jax import lax
import numpy as np

D_MODEL = 1024
BATCH = 16
SEQ = 2048
DEPTH = 1

CTX_LEN = 256
GRID_W = 64
RET_HEADS = 4
RET_DIM = 128
RET_WIDTH = RET_HEADS * RET_DIM
RET_CHUNK = 128
NA_HEADS = 8
NA_DIM = 64
NA_WIDTH = NA_HEADS * NA_DIM
NA_KH = 8
NA_KW = 16
MIX_WIDTH = RET_WIDTH + NA_WIDTH
IN_SPLITS = (RET_WIDTH, RET_WIDTH, RET_WIDTH, RET_WIDTH, NA_WIDTH, NA_WIDTH, NA_WIDTH)
IN_WIDTH = 4 * RET_WIDTH + 3 * NA_WIDTH
D_FF = 4 * D_MODEL
ROPE_BASE = 10000.0
NORM_EPS = 1e-6
N_MOD = 6
NEG_INF = -1e30

kernel_name = 'hybrid_retention_natten_dit_block'


def rmsnorm(x, g):
    xf = x.astype(jnp.float32)
    y = xf * lax.rsqrt(jnp.mean(xf * xf, axis=-1, keepdims=True) + NORM_EPS)
    return (y * g.astype(jnp.float32)).astype(x.dtype)


def modulations(cvec, w_ada, b_ada):
    return jnp.split(jax.nn.silu(cvec) @ w_ada + b_ada, N_MOD, axis=-1)


def to_heads(t, n_heads):
    b, l, _ = t.shape
    return t.reshape(b, l, n_heads, -1).transpose(0, 2, 1, 3)


def axial_rope(x, pos_r, pos_c):
    d_axis = x.shape[-1] // 2
    n_freq = d_axis // 2
    inv = ROPE_BASE ** (-jnp.arange(n_freq, dtype=jnp.float32) / n_freq)

    def rot(seg, pos):
        ang = pos[:, None] * inv[None, :]
        cos, sin = jnp.cos(ang), jnp.sin(ang)
        s1 = seg[..., :n_freq].astype(jnp.float32)
        s2 = seg[..., n_freq:].astype(jnp.float32)
        return jnp.concatenate([s1 * cos - s2 * sin, s1 * sin + s2 * cos], axis=-1)

    out = jnp.concatenate([rot(x[..., :d_axis], pos_r), rot(x[..., d_axis:], pos_c)], axis=-1)
    return out.astype(x.dtype)


def retention_scan(q, k, v, log_gamma, state0, strict):
    b, h, l, dk = q.shape
    dv = v.shape[-1]
    nc = l // RET_CHUNK
    qc = (q * dk ** -0.5).reshape(b, h, nc, RET_CHUNK, dk)
    kc = k.reshape(b, h, nc, RET_CHUNK, dk)
    vc = v.reshape(b, h, nc, RET_CHUNK, dv)
    lg = log_gamma.astype(jnp.float32)[:, None]
    i = jnp.arange(RET_CHUNK, dtype=jnp.float32)
    diff = i[:, None] - i[None, :]
    mask = (diff > 0) if strict else (diff >= 0)
    decay = jnp.where(mask, jnp.exp(lg[:, :, None] * jnp.where(mask, diff, 0.0)), 0.0)
    scores = jnp.einsum('bhnid,bhnjd->bhnij', qc, kc) * decay[:, None]
    inner = jnp.einsum('bhnij,bhnje->bhnie', scores, vc)
    k_w = kc * jnp.exp(lg * (RET_CHUNK - 1.0 - i))[:, None, :, None]
    upd = jnp.einsum('bhnjd,bhnje->nbhde', k_w, vc).astype(jnp.float32)
    chunk_decay = jnp.exp(lg * RET_CHUNK)[:, :, None]

    def step(state, u):
        return chunk_decay * state + u, state

    state_final, state_prev = lax.scan(step, state0.astype(jnp.float32), upd)
    q_w = qc * jnp.exp(lg * (i + 1.0))[:, None, :, None]
    cross = jnp.einsum('bhnid,nbhde->bhnie', q_w, state_prev)
    out = (inner + cross).reshape(b, h, l, dv).astype(v.dtype)
    return out, state_final


def head_layernorm(o, w):
    of = o.astype(jnp.float32)
    mu = jnp.mean(of, axis=-1, keepdims=True)
    var = jnp.mean(jnp.square(of - mu), axis=-1, keepdims=True)
    y = (of - mu) * lax.rsqrt(var + NORM_EPS)
    b, h, l, dv = o.shape
    y = y.transpose(0, 2, 1, 3).reshape(b, l, h * dv)
    return (y * w.astype(jnp.float32)).astype(o.dtype)


def retention_mixer(q, k, v, g, qc, kc, vc, gc, log_gammas, gn_w, with_ctx_out):
    n = q.shape[1]
    tok = jnp.arange(n)
    pos_r = (tok // GRID_W).astype(jnp.float32)
    pos_c = (tok % GRID_W).astype(jnp.float32)
    q = axial_rope(to_heads(q, RET_HEADS), pos_r, pos_c)
    k = axial_rope(to_heads(k, RET_HEADS), pos_r, pos_c)
    v = to_heads(v, RET_HEADS)
    qc, kc, vc = to_heads(qc, RET_HEADS), to_heads(kc, RET_HEADS), to_heads(vc, RET_HEADS)
    b = q.shape[0]
    zero = jnp.zeros((b, RET_HEADS, RET_DIM, RET_DIM), jnp.float32)

    def flip(t):
        return jnp.flip(t, axis=2)

    ctx_f, s_f = retention_scan(qc, kc, vc, log_gammas[0], zero, False)
    lat_f, _ = retention_scan(q, k, v, log_gammas[0], s_f, False)
    ctx_b, s_b = retention_scan(flip(qc), flip(kc), flip(vc), log_gammas[1], zero, True)
    lat_b, _ = retention_scan(flip(q), flip(k), flip(v), log_gammas[1], s_b, True)
    lat = head_layernorm(lat_f + flip(lat_b), gn_w) * jax.nn.silu(g)
    ctx_out = None
    if with_ctx_out:
        ctx_out = head_layernorm(ctx_f + flip(ctx_b), gn_w) * jax.nn.silu(gc)
    return lat, ctx_out


def neighbourhood_attention(q, k, v, kc, vc, rpb):
    b, n, _ = q.shape
    rows = n // GRID_W
    kh = min(NA_KH, rows)

    def grid(t):
        return t.reshape(b, rows, GRID_W, NA_HEADS, NA_DIM).transpose(0, 3, 1, 2, 4)

    qg = grid(q) * NA_DIM ** -0.5
    kg, vg = grid(k), grid(v)
    kc, vc = to_heads(kc, NA_HEADS), to_heads(vc, NA_HEADS)
    r = jnp.arange(rows)
    row_idx = jnp.clip(r - kh // 2, 0, rows - kh)[:, None] + jnp.arange(kh)[None, :]
    nk = kh * GRID_W
    k_band = kg[:, :, row_idx].reshape(b, NA_HEADS, rows, nk, NA_DIM)
    v_band = vg[:, :, row_idx].reshape(b, NA_HEADS, rows, nk, NA_DIM)
    col = jnp.arange(GRID_W)
    col_start = jnp.clip(col - NA_KW // 2, 0, GRID_W - NA_KW)
    key_col = jnp.tile(col, kh)
    key_row = jnp.repeat(row_idx, GRID_W, axis=1)
    valid = (key_col[None, :] >= col_start[:, None]) & (key_col[None, :] < col_start[:, None] + NA_KW)
    dr = key_row - r[:, None] + (NA_KH - 1)
    dc = jnp.clip(key_col[None, :] - col[:, None] + (NA_KW - 1), 0, 2 * NA_KW - 2)
    bias = rpb[:, dr[:, None, :], dc[None, :, :]].astype(jnp.float32)
    bias = jnp.where(valid[None, None], bias, NEG_INF)
    s_loc = jnp.einsum('bhrqd,bhrkd->bhrqk', qg, k_band).astype(jnp.float32) + bias
    s_ctx = jnp.einsum('bhrqd,bhkd->bhrqk', qg, kc).astype(jnp.float32)
    p = jax.nn.softmax(jnp.concatenate([s_loc, s_ctx], axis=-1), axis=-1).astype(v.dtype)
    out = (jnp.einsum('bhrqk,bhrkd->bhrqd', p[..., :nk], v_band)
           + jnp.einsum('bhrqk,bhkd->bhrqd', p[..., nk:], vc))
    return out.transpose(0, 2, 3, 1, 4).reshape(b, n, NA_WIDTH)


def context_attention(qc, kc, vc):
    b, l, _ = qc.shape
    q, k, v = to_heads(qc, NA_HEADS), to_heads(kc, NA_HEADS), to_heads(vc, NA_HEADS)
    s = jnp.einsum('bhqd,bhkd->bhqk', q * NA_DIM ** -0.5, k).astype(jnp.float32)
    p = jax.nn.softmax(s, axis=-1).astype(v.dtype)
    o = jnp.einsum('bhqk,bhkd->bhqd', p, v)
    return o.transpose(0, 2, 1, 3).reshape(b, l, NA_WIDTH)


def squared_relu_mlp(h, w1, w2):
    return jnp.square(jax.nn.relu(h @ w1)) @ w2


def hybrid_layer(x, ctx, c, c_ctx, w_ada, b_ada, g_pre_mix, g_post_mix, g_pre_mlp, g_post_mlp,
                 w_in, ret_decay, ret_gn, na_rpb, w_out, w_mlp1, w_mlp2, update_ctx):
    sh1, sc1, gt1, sh2, sc2, gt2 = modulations(c[:, None, :], w_ada, b_ada)
    csh1, csc1, cgt1, csh2, csc2, cgt2 = modulations(c_ctx, w_ada, b_ada)
    split_at = [int(s) for s in np.cumsum(IN_SPLITS)[:-1]]
    h = rmsnorm(x, g_pre_mix) * (1.0 + sc1) + sh1
    hc = rmsnorm(ctx, g_pre_mix) * (1.0 + csc1) + csh1
    rq, rk, rv, rg, nq, nk, nv = jnp.split(h @ w_in, split_at, axis=-1)
    crq, crk, crv, crg, cnq, cnk, cnv = jnp.split(hc @ w_in, split_at, axis=-1)
    log_gammas = jax.nn.log_sigmoid(ret_decay.astype(jnp.float32))
    ret_lat, ret_ctx = retention_mixer(rq, rk, rv, rg, crq, crk, crv, crg, log_gammas, ret_gn, update_ctx)
    na_lat = neighbourhood_attention(nq, nk, nv, cnk, cnv, na_rpb)
    mix = jnp.concatenate([ret_lat, na_lat], axis=-1) @ w_out
    x = x + gt1 * rmsnorm(mix, g_post_mix)
    h2 = rmsnorm(x, g_pre_mlp) * (1.0 + sc2) + sh2
    x = x + gt2 * rmsnorm(squared_relu_mlp(h2, w_mlp1, w_mlp2), g_post_mlp)
    if update_ctx:
        na_ctx = context_attention(cnq, cnk, cnv)
        mix_c = jnp.concatenate([ret_ctx, na_ctx], axis=-1) @ w_out
        ctx = ctx + cgt1 * rmsnorm(mix_c, g_post_mix)
        hc2 = rmsnorm(ctx, g_pre_mlp) * (1.0 + csc2) + csh2
        ctx = ctx + cgt2 * rmsnorm(squared_relu_mlp(hc2, w_mlp1, w_mlp2), g_post_mlp)
    return x, ctx


def _fwd_setup_inputs(seed: int = 0) -> dict:
    key = jax.random.key(seed)
    ks = jax.random.split(key, 17)

    def nrm(k, shape, s):
        return jax.random.normal(k, shape, jnp.float32) * s

    base_logit = jnp.log(2.0 ** (5.0 + jnp.arange(RET_HEADS, dtype=jnp.float32)) - 1.0)
    return {
        'x': nrm(ks[0], (BATCH, SEQ, D_MODEL), 1.0),
        'c': nrm(ks[1], (BATCH, D_MODEL), 1.0),
        'ctx': nrm(ks[2], (BATCH, CTX_LEN, D_MODEL), 1.0),
        'c_ctx': nrm(ks[3], (D_MODEL,), 1.0),
        'w_ada': nrm(ks[4], (DEPTH, D_MODEL, N_MOD * D_MODEL), D_MODEL ** -0.5),
        'b_ada': nrm(ks[5], (DEPTH, N_MOD * D_MODEL), 0.02),
        'g_pre_mix': 1.0 + nrm(ks[6], (DEPTH, D_MODEL), 0.02),
        'g_post_mix': 1.0 + nrm(ks[7], (DEPTH, D_MODEL), 0.02),
        'g_pre_mlp': 1.0 + nrm(ks[8], (DEPTH, D_MODEL), 0.02),
        'g_post_mlp': 1.0 + nrm(ks[9], (DEPTH, D_MODEL), 0.02),
        'w_in': nrm(ks[10], (DEPTH, D_MODEL, IN_WIDTH), D_MODEL ** -0.5),
        'ret_decay': base_logit[None, None, :] + nrm(ks[11], (DEPTH, 2, RET_HEADS), 0.1),
        'ret_gn': 1.0 + nrm(ks[12], (DEPTH, RET_WIDTH), 0.02),
        'na_rpb': nrm(ks[13], (DEPTH, NA_HEADS, 2 * NA_KH - 1, 2 * NA_KW - 1), 0.1),
        'w_out': nrm(ks[14], (DEPTH, MIX_WIDTH, D_MODEL), MIX_WIDTH ** -0.5),
        'w_mlp1': nrm(ks[15], (DEPTH, D_MODEL, D_FF), D_MODEL ** -0.5),
        'w_mlp2': nrm(ks[16], (DEPTH, D_FF, D_MODEL), D_FF ** -0.5),
    }


def _fwd_reference(x, c, ctx, c_ctx, w_ada, b_ada, g_pre_mix, g_post_mix, g_pre_mlp, g_post_mlp,
              w_in, ret_decay, ret_gn, na_rpb, w_out, w_mlp1, w_mlp2):
    for layer in range(DEPTH):
        x, ctx = hybrid_layer(x, ctx, c, c_ctx, w_ada[layer], b_ada[layer], g_pre_mix[layer],
                              g_post_mix[layer], g_pre_mlp[layer], g_post_mlp[layer], w_in[layer],
                              ret_decay[layer], ret_gn[layer], na_rpb[layer], w_out[layer],
                              w_mlp1[layer], w_mlp2[layer], update_ctx=(layer + 1 < DEPTH))
    return x


import jax as _jax
import jax.numpy as _jnp

TWIN_FORMAT = 'train_step'
FWD_PARAMS = ['x', 'c', 'ctx', 'c_ctx', 'w_ada', 'b_ada', 'g_pre_mix', 'g_post_mix', 'g_pre_mlp', 'g_post_mlp', 'w_in', 'ret_decay', 'ret_gn', 'na_rpb', 'w_out', 'w_mlp1', 'w_mlp2']
TWIN_WEIGHTS = ['c_ctx', 'w_ada', 'b_ada', 'g_pre_mix', 'g_post_mix', 'g_pre_mlp', 'g_post_mlp', 'w_in', 'ret_decay', 'ret_gn', 'na_rpb', 'w_out', 'w_mlp1', 'w_mlp2']
TWIN_DIFF_INPUT = 'x'
TWIN_INPUTS = ['x', 'c', 'ctx', 'c_ctx', 'w_ada', 'b_ada', 'g_pre_mix', 'g_post_mix', 'g_pre_mlp', 'g_post_mlp', 'w_in', 'ret_decay', 'ret_gn', 'na_rpb', 'w_out', 'w_mlp1', 'w_mlp2', 'loss_target', 'm_c_ctx', 'm_w_ada', 'm_b_ada', 'm_g_pre_mix', 'm_g_post_mix', 'm_g_pre_mlp', 'm_g_post_mlp', 'm_w_in', 'm_ret_decay', 'm_ret_gn', 'm_na_rpb', 'm_w_out', 'm_w_mlp1', 'm_w_mlp2', 'v_c_ctx', 'v_w_ada', 'v_b_ada', 'v_g_pre_mix', 'v_g_post_mix', 'v_g_pre_mlp', 'v_g_post_mlp', 'v_w_in', 'v_ret_decay', 'v_ret_gn', 'v_na_rpb', 'v_w_out', 'v_w_mlp1', 'v_w_mlp2']
TWIN_OUTPUTS = ['loss', 'grad_x', 'grad_c_ctx', 'grad_w_ada', 'grad_b_ada', 'grad_g_pre_mix', 'grad_g_post_mix', 'grad_g_pre_mlp', 'grad_g_post_mlp', 'grad_w_in', 'grad_ret_decay', 'grad_ret_gn', 'grad_na_rpb', 'grad_w_out', 'grad_w_mlp1', 'grad_w_mlp2', 'delta_c_ctx', 'delta_w_ada', 'delta_b_ada', 'delta_g_pre_mix', 'delta_g_post_mix', 'delta_g_pre_mlp', 'delta_g_post_mlp', 'delta_w_in', 'delta_ret_decay', 'delta_ret_gn', 'delta_na_rpb', 'delta_w_out', 'delta_w_mlp1', 'delta_w_mlp2', 'new_m_c_ctx', 'new_m_w_ada', 'new_m_b_ada', 'new_m_g_pre_mix', 'new_m_g_post_mix', 'new_m_g_pre_mlp', 'new_m_g_post_mlp', 'new_m_w_in', 'new_m_ret_decay', 'new_m_ret_gn', 'new_m_na_rpb', 'new_m_w_out', 'new_m_w_mlp1', 'new_m_w_mlp2', 'new_v_c_ctx', 'new_v_w_ada', 'new_v_b_ada', 'new_v_g_pre_mix', 'new_v_g_post_mix', 'new_v_g_pre_mlp', 'new_v_g_post_mlp', 'new_v_w_in', 'new_v_ret_decay', 'new_v_ret_gn', 'new_v_na_rpb', 'new_v_w_out', 'new_v_w_mlp1', 'new_v_w_mlp2']
TWIN_LEAF_KINDS = {'loss': 'loss', 'grad_x': 'grad_x', 'grad_c_ctx': 'grad_w', 'grad_w_ada': 'grad_w', 'grad_b_ada': 'grad_w', 'grad_g_pre_mix': 'grad_w', 'grad_g_post_mix': 'grad_w', 'grad_g_pre_mlp': 'grad_w', 'grad_g_post_mlp': 'grad_w', 'grad_w_in': 'grad_w', 'grad_ret_decay': 'grad_w', 'grad_ret_gn': 'grad_w', 'grad_na_rpb': 'grad_w', 'grad_w_out': 'grad_w', 'grad_w_mlp1': 'grad_w', 'grad_w_mlp2': 'grad_w', 'delta_c_ctx': 'delta_w', 'delta_w_ada': 'delta_w', 'delta_b_ada': 'delta_w', 'delta_g_pre_mix': 'delta_w', 'delta_g_post_mix': 'delta_w', 'delta_g_pre_mlp': 'delta_w', 'delta_g_post_mlp': 'delta_w', 'delta_w_in': 'delta_w', 'delta_ret_decay': 'delta_w', 'delta_ret_gn': 'delta_w', 'delta_na_rpb': 'delta_w', 'delta_w_out': 'delta_w', 'delta_w_mlp1': 'delta_w', 'delta_w_mlp2': 'delta_w', 'new_m_c_ctx': 'new_m', 'new_m_w_ada': 'new_m', 'new_m_b_ada': 'new_m', 'new_m_g_pre_mix': 'new_m', 'new_m_g_post_mix': 'new_m', 'new_m_g_pre_mlp': 'new_m', 'new_m_g_post_mlp': 'new_m', 'new_m_w_in': 'new_m', 'new_m_ret_decay': 'new_m', 'new_m_ret_gn': 'new_m', 'new_m_na_rpb': 'new_m', 'new_m_w_out': 'new_m', 'new_m_w_mlp1': 'new_m', 'new_m_w_mlp2': 'new_m', 'new_v_c_ctx': 'new_v', 'new_v_w_ada': 'new_v', 'new_v_b_ada': 'new_v', 'new_v_g_pre_mix': 'new_v', 'new_v_g_post_mix': 'new_v', 'new_v_g_pre_mlp': 'new_v', 'new_v_g_post_mlp': 'new_v', 'new_v_w_in': 'new_v', 'new_v_ret_decay': 'new_v', 'new_v_ret_gn': 'new_v', 'new_v_na_rpb': 'new_v', 'new_v_w_out': 'new_v', 'new_v_w_mlp1': 'new_v', 'new_v_w_mlp2': 'new_v'}


def _forward(args):
    return _fwd_reference(*[args[k] for k in FWD_PARAMS])


def _output_shape():
    out = _jax.eval_shape(lambda: _forward(_fwd_setup_inputs(0)))
    return out.shape, out.dtype

N_MICROBATCH = 1
ADAM_LR = 0.001
ADAM_B1 = 0.9
ADAM_B2 = 0.999
ADAM_EPS = 1e-08
ADAM_WD = 0.01
ADAM_STEP = 10
PER_EXAMPLE_BATCH_AXIS = {'x': 0, 'c': 0, 'ctx': 0, 'loss_target': 0}
SHARED_INPUTS = []
_WEIGHT_DTYPES = {'c_ctx': _jnp.float32, 'w_ada': _jnp.float32, 'b_ada': _jnp.float32, 'g_pre_mix': _jnp.float32, 'g_post_mix': _jnp.float32, 'g_pre_mlp': _jnp.float32, 'g_post_mlp': _jnp.float32, 'w_in': _jnp.float32, 'ret_decay': _jnp.float32, 'ret_gn': _jnp.float32, 'na_rpb': _jnp.float32, 'w_out': _jnp.float32, 'w_mlp1': _jnp.float32, 'w_mlp2': _jnp.float32}
MOMENT_SCALE = {'c_ctx': 1.648031e+00, 'w_ada': 4.224942e+00, 'b_ada': 7.888499e+00, 'g_pre_mix': 7.612643e-01, 'g_post_mix': 1.531820e+01, 'g_pre_mlp': 9.024067e-01, 'g_post_mlp': 1.655617e+01, 'w_in': 1.612050e+00, 'ret_decay': 1.204434e+00, 'ret_gn': 1.940852e+00, 'na_rpb': 2.410993e-02, 'w_out': 2.492879e+00, 'w_mlp1': 1.075358e+00, 'w_mlp2': 3.179633e+00}


def _to_microbatches(a, axis):
    t = _jnp.moveaxis(a, axis, 0)
    t = t.reshape((N_MICROBATCH, t.shape[0] // N_MICROBATCH) + t.shape[1:])
    return _jnp.moveaxis(t, 1, axis + 1)


def setup_inputs(seed: int = 0) -> dict:
    inp = _fwd_setup_inputs(seed)
    key = _jax.random.fold_in(_jax.random.key(seed), 7919)
    shape, _ = _output_shape()
    out = dict(inp)
    out["loss_target"] = _jax.random.normal(_jax.random.fold_in(key, 0), shape, _jnp.float32)
    for i, name in enumerate(TWIN_WEIGHTS):
        w = inp[name].astype(_jnp.float32)
        if MOMENT_SCALE is None:
            s = _jnp.sqrt(_jnp.mean(_jnp.square(w)) + 1e-30)
        else:
            s = MOMENT_SCALE[name]
        km, kv = _jax.random.split(_jax.random.fold_in(key, i + 1))
        out[name] = w
        out["m_" + name] = s * _jax.random.normal(km, w.shape, _jnp.float32)
        out["v_" + name] = (s * s) * _jax.random.uniform(kv, w.shape, _jnp.float32, 0.5, 1.5)
    if N_MICROBATCH > 1:
        for name, axis in PER_EXAMPLE_BATCH_AXIS.items():
            out[name] = _to_microbatches(out[name], axis)
    return {'x': out['x'], 'c': out['c'], 'ctx': out['ctx'], 'c_ctx': out['c_ctx'], 'w_ada': out['w_ada'], 'b_ada': out['b_ada'], 'g_pre_mix': out['g_pre_mix'], 'g_post_mix': out['g_post_mix'], 'g_pre_mlp': out['g_pre_mlp'], 'g_post_mlp': out['g_post_mlp'], 'w_in': out['w_in'], 'ret_decay': out['ret_decay'], 'ret_gn': out['ret_gn'], 'na_rpb': out['na_rpb'], 'w_out': out['w_out'], 'w_mlp1': out['w_mlp1'], 'w_mlp2': out['w_mlp2'], 'loss_target': out['loss_target'], 'm_c_ctx': out['m_c_ctx'], 'm_w_ada': out['m_w_ada'], 'm_b_ada': out['m_b_ada'], 'm_g_pre_mix': out['m_g_pre_mix'], 'm_g_post_mix': out['m_g_post_mix'], 'm_g_pre_mlp': out['m_g_pre_mlp'], 'm_g_post_mlp': out['m_g_post_mlp'], 'm_w_in': out['m_w_in'], 'm_ret_decay': out['m_ret_decay'], 'm_ret_gn': out['m_ret_gn'], 'm_na_rpb': out['m_na_rpb'], 'm_w_out': out['m_w_out'], 'm_w_mlp1': out['m_w_mlp1'], 'm_w_mlp2': out['m_w_mlp2'], 'v_c_ctx': out['v_c_ctx'], 'v_w_ada': out['v_w_ada'], 'v_b_ada': out['v_b_ada'], 'v_g_pre_mix': out['v_g_pre_mix'], 'v_g_post_mix': out['v_g_post_mix'], 'v_g_pre_mlp': out['v_g_pre_mlp'], 'v_g_post_mlp': out['v_g_post_mlp'], 'v_w_in': out['v_w_in'], 'v_ret_decay': out['v_ret_decay'], 'v_ret_gn': out['v_ret_gn'], 'v_na_rpb': out['v_na_rpb'], 'v_w_out': out['v_w_out'], 'v_w_mlp1': out['v_w_mlp1'], 'v_w_mlp2': out['v_w_mlp2']}


def _loss(weights, diff, rest, loss_target):
    with _jax.named_scope("forward"):
        args = {**rest, TWIN_DIFF_INPUT: diff, **{k: w.astype(_WEIGHT_DTYPES[k]) for k, w in weights.items()}}
        y = _forward(args)
    with _jax.named_scope("loss_head"):
        err = _jnp.square(y.astype(_jnp.float32) - loss_target)
        return 0.5 * _jnp.sum(_jnp.mean(err, axis=-1)) if err.ndim else 0.5 * err


def _adamw(w, g, m, v):
    m = ADAM_B1 * m + (1.0 - ADAM_B1) * g
    v = ADAM_B2 * v + (1.0 - ADAM_B2) * _jnp.square(g)
    m_hat = m / (1.0 - ADAM_B1 ** ADAM_STEP)
    v_hat = v / (1.0 - ADAM_B2 ** ADAM_STEP)
    delta = -ADAM_LR * (m_hat / (_jnp.sqrt(v_hat) + ADAM_EPS) + ADAM_WD * w)
    return delta, m, v


def reference(x, c, ctx, c_ctx, w_ada, b_ada, g_pre_mix, g_post_mix, g_pre_mlp, g_post_mlp, w_in, ret_decay, ret_gn, na_rpb, w_out, w_mlp1, w_mlp2, loss_target, m_c_ctx, m_w_ada, m_b_ada, m_g_pre_mix, m_g_post_mix, m_g_pre_mlp, m_g_post_mlp, m_w_in, m_ret_decay, m_ret_gn, m_na_rpb, m_w_out, m_w_mlp1, m_w_mlp2, v_c_ctx, v_w_ada, v_b_ada, v_g_pre_mix, v_g_post_mix, v_g_pre_mlp, v_g_post_mlp, v_w_in, v_ret_decay, v_ret_gn, v_na_rpb, v_w_out, v_w_mlp1, v_w_mlp2):
    given = dict(x=x, c=c, ctx=ctx, c_ctx=c_ctx, w_ada=w_ada, b_ada=b_ada, g_pre_mix=g_pre_mix, g_post_mix=g_post_mix, g_pre_mlp=g_pre_mlp, g_post_mlp=g_post_mlp, w_in=w_in, ret_decay=ret_decay, ret_gn=ret_gn, na_rpb=na_rpb, w_out=w_out, w_mlp1=w_mlp1, w_mlp2=w_mlp2, loss_target=loss_target, m_c_ctx=m_c_ctx, m_w_ada=m_w_ada, m_b_ada=m_b_ada, m_g_pre_mix=m_g_pre_mix, m_g_post_mix=m_g_post_mix, m_g_pre_mlp=m_g_pre_mlp, m_g_post_mlp=m_g_post_mlp, m_w_in=m_w_in, m_ret_decay=m_ret_decay, m_ret_gn=m_ret_gn, m_na_rpb=m_na_rpb, m_w_out=m_w_out, m_w_mlp1=m_w_mlp1, m_w_mlp2=m_w_mlp2, v_c_ctx=v_c_ctx, v_w_ada=v_w_ada, v_b_ada=v_b_ada, v_g_pre_mix=v_g_pre_mix, v_g_post_mix=v_g_post_mix, v_g_pre_mlp=v_g_pre_mlp, v_g_post_mlp=v_g_post_mlp, v_w_in=v_w_in, v_ret_decay=v_ret_decay, v_ret_gn=v_ret_gn, v_na_rpb=v_na_rpb, v_w_out=v_w_out, v_w_mlp1=v_w_mlp1, v_w_mlp2=v_w_mlp2)
    weights = {n: given[n] for n in TWIN_WEIGHTS}
    shared = {n: given[n] for n in SHARED_INPUTS}
    per_example = {n: given[n] for n in ['x', 'c', 'ctx']}
    grad_fn = _jax.value_and_grad(_loss, argnums=(0, 1))

    def one_microbatch(ex, loss_target):
        ex = dict(ex)
        diff = ex.pop(TWIN_DIFF_INPUT)
        return grad_fn(weights, diff, {**shared, **ex}, loss_target)

    if N_MICROBATCH == 1:
        loss, (grad_w, grad_x) = one_microbatch(per_example, given["loss_target"])
    else:
        def body(carry, xs):
            loss_sum, grad_sum = carry
            l_k, (gw_k, gx_k) = one_microbatch(xs[0], xs[1])
            with _jax.named_scope("update"):
                return (loss_sum + l_k, _jax.tree.map(_jnp.add, grad_sum, gw_k)), gx_k

        init = (_jnp.zeros((), _jnp.float32), _jax.tree.map(_jnp.zeros_like, weights))
        (loss, grad_w), grad_x = _jax.lax.scan(body, init, (per_example, given["loss_target"]))
    with _jax.named_scope("update"):
        delta_w, new_m, new_v = {}, {}, {}
        for n in TWIN_WEIGHTS:
            delta_w[n], new_m[n], new_v[n] = _adamw(weights[n], grad_w[n], given["m_" + n], given["v_" + n])
    return (loss, grad_x, *[grad_w[n] for n in TWIN_WEIGHTS], *[delta_w[n] for n in TWIN_WEIGHTS],
            *[new_m[n] for n in TWIN_WEIGHTS], *[new_v[n] for n in TWIN_WEIGHTS])
```

```python
import functools
import math

import numpy as np
import jax
import jax.numpy as jnp
from jax import lax
from jax.experimental import pallas as pl
from jax.experimental.pallas import tpu as pltpu

F32 = jnp.float32
BF16 = jnp.bfloat16
MESH = pl.DeviceIdType.MESH

N_DEV = 8
LANES = 128
SUBLANES = 8
VMEM_LIMIT = 60 * 1024 * 1024

GRID_W = 64
RET_HEADS = 4
RET_DIM = 128
RET_WIDTH = RET_HEADS * RET_DIM
NA_HEADS = 8
NA_DIM = 64
NA_WIDTH = NA_HEADS * NA_DIM
NA_PAIRS = NA_HEADS // 2
NA_KH = 8
NA_KW = 16
SEG = 512
ROPE_BASE = 10000.0
NORM_EPS = 1e-6
NEG_INF = -1e30
N_MOD = 6

ADAM_LR = 0.001
ADAM_B1 = 0.9
ADAM_B2 = 0.999
ADAM_EPS = 1e-08
ADAM_WD = 0.01
ADAM_STEP = 10


def _dot(a, b):
    return lax.dot_general(a, b, (((1,), (0,)), ((), ())), preferred_element_type=F32)


def _dot_nt(a, b):
    return lax.dot_general(a, b, (((1,), (1,)), ((), ())), preferred_element_type=F32)


def _dot_tn(a, b):
    return lax.dot_general(a, b, (((0,), (0,)), ((), ())), preferred_element_type=F32)


def _sigmoid(x):
    return 1.0 / (1.0 + jnp.exp(-x))


def _div_tile(n, cap, mult):
    if n <= cap:
        return n
    for t in range(cap - cap % mult, 0, -mult):
        if n % t == 0:
            return t
    raise ValueError(f"no tile for {n}")


def _params(*sem):
    return pltpu.CompilerParams(dimension_semantics=tuple(sem) if sem else None,
                                vmem_limit_bytes=VMEM_LIMIT)


def _vmem():
    return pl.BlockSpec(memory_space=pltpu.VMEM)


def _any():
    return pl.BlockSpec(memory_space=pl.ANY)


def _me_and_peers():
    x, y, c = lax.axis_index("x"), lax.axis_index("y"), lax.axis_index("c")
    me = 4 * x + 2 * y + c
    peers = []
    for m in range(1, N_DEV):
        px = 1 - x if (m >> 2) & 1 else x
        py = 1 - y if (m >> 1) & 1 else y
        pc = 1 - c if m & 1 else c
        peers.append(((px, py, pc), 4 * px + 2 * py + pc))
    return me, peers


def _exchange(src_for, dst_from, send_sems, recv_sems):
    me, peers = _me_and_peers()
    sent = []
    for i, (dev, pid) in enumerate(peers):
        cp = pltpu.make_async_remote_copy(src_ref=src_for(pid), dst_ref=dst_from(me),
                                          send_sem=send_sems.at[i], recv_sem=recv_sems.at[i],
                                          device_id=dev, device_id_type=MESH)
        cp.start()
        sent.append(cp)
    for i, (dev, pid) in enumerate(peers):
        pltpu.make_async_remote_copy(src_ref=src_for(pid), dst_ref=dst_from(pid),
                                     send_sem=send_sems.at[i], recv_sem=recv_sems.at[i],
                                     device_id=dev, device_id_type=MESH).wait_recv()
    for cp in sent:
        cp.wait_send()


def _mod_gather(c, c_ctx, w_ada, b_ada):
    B, D = c.shape
    ncol = w_ada.shape[1]
    rows = SUBLANES * N_DEV + SUBLANES

    def body(c_ref, cc_ref, w_ref, b_ref, s_ref, m_ref, send1, recv1, send2, recv2):
        me, _ = _me_and_peers()
        cv = c_ref[...]
        slot = jnp.concatenate([cv * _sigmoid(cv), jnp.zeros((SUBLANES - B, D), F32)], axis=0)
        my_rows = pl.ds(pl.multiple_of(me * SUBLANES, SUBLANES), SUBLANES)
        s_ref[my_rows, :] = slot
        ccv = cc_ref[...]
        s_ref[SUBLANES * N_DEV:, :] = jnp.concatenate(
            [ccv * _sigmoid(ccv), jnp.zeros((SUBLANES - 1, D), F32)], axis=0)

        def rows_of(p):
            return s_ref.at[pl.ds(pl.multiple_of(p * SUBLANES, SUBLANES), SUBLANES), :]

        _exchange(lambda p: rows_of(me), rows_of, send1, recv1)
        b_loc = b_ref[:, pl.ds(pl.multiple_of(me * ncol, ncol), ncol)]
        m_ref[me] = _dot(s_ref[...], w_ref[...]) + b_loc
        _exchange(lambda p: m_ref.at[me], lambda p: m_ref.at[p], send2, recv2)

    return pl.pallas_call(
        body, name="mod_gather",
        out_shape=(jax.ShapeDtypeStruct((rows, D), F32), jax.ShapeDtypeStruct((N_DEV, rows, ncol), F32)),
        in_specs=[_vmem()] * 4, out_specs=(_vmem(), _vmem()),
        scratch_shapes=[pltpu.SemaphoreType.DMA((N_DEV - 1,))] * 4,
        compiler_params=pltpu.CompilerParams(vmem_limit_bytes=VMEM_LIMIT),
    )(c, c_ctx.reshape(1, D), w_ada, b_ada)


def _w_gather(w_in, w_out, w1, w2):
    D, cin = w_in.shape
    rout = w_out.shape[0]
    c1 = w1.shape[1]
    r2 = w2.shape[0]

    def body(win_ref, wout_ref, w1_ref, w2_ref, gin_ref, gout_ref, g1_ref, g2_ref,
             bin_, bout, b1, b2, send_sems, recv_sems, loc_sems):
        me, _ = _me_and_peers()
        bin_[...] = win_ref[...].astype(BF16)
        bout[...] = wout_ref[...].astype(BF16)
        b1[...] = w1_ref[...].astype(BF16)
        b2[...] = w2_ref[...].astype(BF16)

        def d_in(p):
            return gin_ref.at[p]

        def d_out(p):
            return gout_ref.at[pl.ds(pl.multiple_of(p * rout, rout), rout), :]

        def d_1(p):
            return g1_ref.at[:, pl.ds(pl.multiple_of(p * c1, c1), c1)]

        def d_2(p):
            return g2_ref.at[pl.ds(pl.multiple_of(p * r2, r2), r2), :]

        srcs = (bin_, bout, b1, b2)
        dsts = (d_in, d_out, d_1, d_2)
        local = [pltpu.make_async_copy(s, d(me), loc_sems.at[k]) for k, (s, d) in enumerate(zip(srcs, dsts))]
        for cp in local:
            cp.start()
        _, peers = _me_and_peers()
        sent = []
        for k, (s, d) in enumerate(zip(srcs, dsts)):
            for i, (dev, pid) in enumerate(peers):
                cp = pltpu.make_async_remote_copy(src_ref=s, dst_ref=d(me), send_sem=send_sems.at[k, i],
                                                  recv_sem=recv_sems.at[k, i], device_id=dev, device_id_type=MESH)
                cp.start()
                sent.append(cp)
        for k, (s, d) in enumerate(zip(srcs, dsts)):
            for i, (dev, pid) in enumerate(peers):
                pltpu.make_async_remote_copy(src_ref=s, dst_ref=d(pid), send_sem=send_sems.at[k, i],
                                             recv_sem=recv_sems.at[k, i], device_id=dev,
                                             device_id_type=MESH).wait_recv()
        for cp in sent:
            cp.wait_send()
        for cp in local:
            cp.wait()

    return pl.pallas_call(
        body, name="w_gather",
        out_shape=(jax.ShapeDtypeStruct((N_DEV, D, cin), BF16),
                   jax.ShapeDtypeStruct((N_DEV * rout, w_out.shape[1]), BF16),
                   jax.ShapeDtypeStruct((D, N_DEV * c1), BF16),
                   jax.ShapeDtypeStruct((N_DEV * r2, w2.shape[1]), BF16)),
        in_specs=[_vmem()] * 4, out_specs=(_any(),) * 4,
        scratch_shapes=[pltpu.VMEM(w_in.shape, BF16), pltpu.VMEM(w_out.shape, BF16),
                        pltpu.VMEM(w1.shape, BF16), pltpu.VMEM(w2.shape, BF16),
                        pltpu.SemaphoreType.DMA((4, N_DEV - 1)), pltpu.SemaphoreType.DMA((4, N_DEV - 1)),
                        pltpu.SemaphoreType.DMA((4,))],
        compiler_params=pltpu.CompilerParams(vmem_limit_bytes=VMEM_LIMIT),
    )(w_in, w_out, w1, w2)


def _inproj_fwd(x_all, modl, g1, w_in, n_ctx):
    B, T, D = x_all.shape
    nw = w_in.shape[1]
    tm = _div_tile(n_ctx, 256, 16)
    nct = n_ctx // tm

    def body(x_ref, sh_ref, sc_ref, g_ref, w_ref, h_ref, p_ref):
        x = x_ref[...]
        r = lax.rsqrt(jnp.mean(x * x, axis=-1, keepdims=True) + NORM_EPS)
        h = ((x * r) * g_ref[...]) * (1.0 + sc_ref[...]) + sh_ref[...]
        hb = h.astype(BF16)
        h_ref[...] = hb
        p_ref[...] = _dot(hb, w_ref[...])

    def mrow(b, t):
        return jnp.where(t < nct, B, b)

    return pl.pallas_call(
        body, name="inproj_fwd", grid=(B, T // tm),
        out_shape=(jax.ShapeDtypeStruct((B, T, D), BF16), jax.ShapeDtypeStruct((B, T, nw), F32)),
        in_specs=[pl.BlockSpec((None, tm, D), lambda b, t: (b, t, 0)),
                  pl.BlockSpec((None, None, 1, D), lambda b, t: (mrow(b, t), 0, 0, 0)),
                  pl.BlockSpec((None, None, 1, D), lambda b, t: (mrow(b, t), 1, 0, 0)),
                  pl.BlockSpec((1, D), lambda b, t: (0, 0)),
                  pl.BlockSpec((D, nw), lambda b, t: (0, 0))],
        out_specs=(pl.BlockSpec((None, tm, D), lambda b, t: (b, t, 0)),
                   pl.BlockSpec((None, tm, nw), lambda b, t: (b, t, 0))),
        compiler_params=_params("parallel", "arbitrary"),
    )(x_all, modl, modl, g1, w_in)


def _swap32(x):
    lane = lax.broadcasted_iota(jnp.int32, x.shape, 1)
    return jnp.where((lane % 64) < 32, pltpu.roll(x, 96, 1), pltpu.roll(x, 32, 1))


def _rope(x, cos, sin):
    return x * cos + _swap32(x) * sin


def _unrope(dy, cos, sin):
    return dy * cos + _swap32(dy * sin)


def _ret_weights(lgf, lgb, dist):
    return jnp.exp(jnp.where(dist >= 0.0, lgf * dist, -lgb * dist))


def _ret_fwd(proj, cos, sin, lg, gn, n_ctx):
    B, T, _ = proj.shape
    C = n_ctx
    N = T - C
    tq = _div_tile(N, 256, 16)
    tk = tq
    scale = RET_DIM ** -0.5

    def body(lg_ref, q_ref, k_ref, v_ref, g_ref, cos_ref, sin_ref, gn_ref, o_ref, lat_ref, qs, ks, vs):
        h = pl.program_id(1)
        lgf = lg_ref[0, h]
        lgb = lg_ref[1, h]
        cosv = cos_ref[...]
        sinv = sin_ref[...]
        qs[...] = (_rope(q_ref[...], cosv, sinv) * scale).astype(BF16)
        ks[...] = _rope(k_ref[...], cosv, sinv).astype(BF16)
        vs[...] = v_ref[...].astype(BF16)
        gnv = gn_ref[...]
        rc = (lax.broadcasted_iota(jnp.int32, (tq, tk), 0) - lax.broadcasted_iota(jnp.int32, (tq, tk), 1)).astype(F32)
        ri = lax.broadcasted_iota(jnp.int32, (tq, C), 0).astype(F32)
        ti = lax.broadcasted_iota(jnp.int32, (tq, C), 1).astype(F32)

        def q_tile(qi, carry):
            i0 = pl.multiple_of(qi * tq, tq)
            i0f = (qi * tq).astype(F32)
            qt = qs[pl.ds(C + i0, tq), :]
            s = _dot_nt(qt, ks[0:C, :])
            w = jnp.exp(lgf * (ri + (i0f + C) - ti)) + jnp.exp(lgb * ((N - i0f) - ri + ti))
            acc = _dot((s * w).astype(BF16), vs[0:C, :])

            def k_tile(kj, acc):
                j0 = pl.multiple_of(kj * tk, tk)
                kt = ks[pl.ds(C + j0, tk), :]
                s = _dot_nt(qt, kt)
                w = _ret_weights(lgf, lgb, rc + (i0f - (kj * tk).astype(F32)))
                return acc + _dot((s * w).astype(BF16), vs[pl.ds(C + j0, tk), :])

            o = lax.fori_loop(0, N // tk, k_tile, acc)
            o_ref[pl.ds(i0, tq), :] = o
            mu = jnp.mean(o, axis=-1, keepdims=True)
            oc = o - mu
            var = jnp.mean(oc * oc, axis=-1, keepdims=True)
            yh = oc * lax.rsqrt(var + NORM_EPS)
            g = g_ref[pl.ds(C + i0, tq), :]
            lat_ref[pl.ds(i0, tq), :] = ((yh * gnv) * (g * _sigmoid(g))).astype(BF16)
            return carry

        lax.fori_loop(0, N // tq, q_tile, 0)

    def col(seg):
        return pl.BlockSpec((None, T, RET_DIM), lambda b, h, seg=seg: (b, 0, seg * RET_HEADS + h))

    return pl.pallas_call(
        body, name="ret_fwd", grid=(B, RET_HEADS),
        out_shape=(jax.ShapeDtypeStruct((B, N, RET_WIDTH), F32), jax.ShapeDtypeStruct((B, N, RET_WIDTH), BF16)),
        in_specs=[pl.BlockSpec(memory_space=pltpu.SMEM), col(0), col(1), col(2), col(3),
                  pl.BlockSpec((T, RET_DIM), lambda b, h: (0, 0)), pl.BlockSpec((T, RET_DIM), lambda b, h: (0, 0)),
                  pl.BlockSpec((1, RET_DIM), lambda b, h: (0, h))],
        out_specs=(pl.BlockSpec((None, N, RET_DIM), lambda b, h: (b, 0, h)),
                   pl.BlockSpec((None, N, RET_DIM), lambda b, h: (b, 0, h))),
        scratch_shapes=[pltpu.VMEM((T, RET_DIM), BF16)] * 3,
        compiler_params=_params("parallel", "arbitrary"),
    )(lg, proj, proj, proj, proj, cos, sin, gn)


def _ret_bwd(proj, cos, sin, lg, gn, o, dlat, n_ctx):
    B, T, _ = proj.shape
    C = n_ctx
    N = T - C
    tq = _div_tile(N, 256, 16)
    tk = tq
    scale = RET_DIM ** -0.5

    def body(lg_ref, q_ref, k_ref, v_ref, g_ref, cos_ref, sin_ref, gn_ref, o_ref, dl_ref,
             d_ref, dgn_ref, dlg_ref, qs, ks, vs, dos):
        h = pl.program_id(1)
        lgf = lg_ref[0, h]
        lgb = lg_ref[1, h]
        cosv = cos_ref[...]
        sinv = sin_ref[...]
        qs[...] = (_rope(q_ref[...], cosv, sinv) * scale).astype(BF16)
        ks[...] = _rope(k_ref[...], cosv, sinv).astype(BF16)
        vs[...] = v_ref[...].astype(BF16)
        gnv = gn_ref[...]

        ov = o_ref[...]
        mu = jnp.mean(ov, axis=-1, keepdims=True)
        oc = ov - mu
        var = jnp.mean(oc * oc, axis=-1, keepdims=True)
        rstd = lax.rsqrt(var + NORM_EPS)
        yh = oc * rstd
        g = g_ref[C:, :]
        sg = _sigmoid(g)
        silu = g * sg
        dl = dl_ref[...]
        d_ref[3, 0:C, :] = jnp.zeros((C, RET_DIM), F32)
        d_ref[3, C:, :] = dl * (yh * gnv) * (sg * (1.0 + g * (1.0 - sg)))
        dls = dl * silu
        dgn = jnp.sum(dls * yh, axis=0, keepdims=True)
        dgn_ref[...] = jnp.concatenate([dgn, jnp.zeros((SUBLANES - 1, RET_DIM), F32)], axis=0)
        dyh = dls * gnv
        do = rstd * (dyh - jnp.mean(dyh, axis=-1, keepdims=True) - yh * jnp.mean(dyh * yh, axis=-1, keepdims=True))
        dos[...] = do.astype(BF16)

        d_ref[0, 0:C, :] = jnp.zeros((C, RET_DIM), F32)
        d_ref[1] = jnp.zeros((T, RET_DIM), F32)
        d_ref[2] = jnp.zeros((T, RET_DIM), F32)

        rc = (lax.broadcasted_iota(jnp.int32, (tq, tk), 0) - lax.broadcasted_iota(jnp.int32, (tq, tk), 1)).astype(F32)
        ri = lax.broadcasted_iota(jnp.int32, (tq, C), 0).astype(F32)
        ti = lax.broadcasted_iota(jnp.int32, (tq, C), 1).astype(F32)

        def fold(a):
            return jnp.sum(a.reshape(a.shape[0] // SUBLANES, SUBLANES, a.shape[1]), axis=0)

        def q_tile(qi, carry):
            gf, gb = carry
            i0 = pl.multiple_of(qi * tq, tq)
            i0f = (qi * tq).astype(F32)
            qt = qs[pl.ds(C + i0, tq), :]
            dot = dos[pl.ds(i0, tq), :]
            kc = ks[0:C, :]
            vc = vs[0:C, :]
            s = _dot_nt(qt, kc)
            dp = _dot_nt(dot, vc)
            ef = ri + (i0f + C) - ti
            eb = (N - i0f) - ri + ti
            wf = jnp.exp(lgf * ef)
            wb = jnp.exp(lgb * eb)
            w = wf + wb
            d_ref[2, 0:C, :] += _dot_tn((s * w).astype(BF16), dot)
            ds = (dp * w).astype(BF16)
            dq = _dot(ds, kc)
            d_ref[1, 0:C, :] += _dot_tn(ds, qt)
            gs = dp * s
            gfc = fold(gs * wf * ef)
            gbc = fold(gs * wb * eb)

            def k_tile(kj, carry):
                dq, gf, gb = carry
                j0 = pl.multiple_of(kj * tk, tk)
                rows = pl.ds(C + j0, tk)
                kt = ks[rows, :]
                vt = vs[rows, :]
                s = _dot_nt(qt, kt)
                dp = _dot_nt(dot, vt)
                dist = rc + (i0f - (kj * tk).astype(F32))
                w = _ret_weights(lgf, lgb, dist)
                d_ref[2, rows, :] += _dot_tn((s * w).astype(BF16), dot)
                ds = (dp * w).astype(BF16)
                dq = dq + _dot(ds, kt)
                d_ref[1, rows, :] += _dot_tn(ds, qt)
                gw = dp * s * w * dist
                gf = gf + fold(jnp.where(dist >= 0.0, gw, 0.0))
                gb = gb - fold(jnp.where(dist < 0.0, gw, 0.0))
                return dq, gf, gb

            dq, gf2, gb2 = lax.fori_loop(0, N // tk, k_tile, (dq, jnp.zeros((SUBLANES, tk), F32),
                                                            jnp.zeros((SUBLANES, tk), F32)))
            qrows = pl.ds(C + i0, tq)
            d_ref[0, qrows, :] = _unrope(dq * scale, cos_ref[qrows, :], sin_ref[qrows, :])
            return gf + jnp.sum(gf2) + jnp.sum(gfc), gb + jnp.sum(gb2) + jnp.sum(gbc)

        gf, gb = lax.fori_loop(0, N // tq, q_tile, (jnp.zeros((), F32), jnp.zeros((), F32)))
        d_ref[1] = _unrope(d_ref[1], cosv, sinv)
        row = lax.broadcasted_iota(jnp.int32, (SUBLANES, LANES), 0)
        dlg_ref[...] = jnp.where(row == 0, gf, jnp.where(row == 1, gb, 0.0))

    def col(seg):
        return pl.BlockSpec((None, T, RET_DIM), lambda b, h, seg=seg: (b, 0, seg * RET_HEADS + h))

    return pl.pallas_call(
        body, name="ret_bwd", grid=(B, RET_HEADS),
        out_shape=(jax.ShapeDtypeStruct((B, 4, T, RET_WIDTH), F32),
                   jax.ShapeDtypeStruct((B, SUBLANES, RET_WIDTH), F32),
                   jax.ShapeDtypeStruct((B, RET_HEADS, SUBLANES, LANES), F32)),
        in_specs=[pl.BlockSpec(memory_space=pltpu.SMEM), col(0), col(1), col(2), col(3),
                  pl.BlockSpec((T, RET_DIM), lambda b, h: (0, 0)), pl.BlockSpec((T, RET_DIM), lambda b, h: (0, 0)),
                  pl.BlockSpec((1, RET_DIM), lambda b, h: (0, h)),
                  pl.BlockSpec((None, N, RET_DIM), lambda b, h: (b, 0, h)),
                  pl.BlockSpec((None, N, RET_DIM), lambda b, h: (b, 0, h))],
        out_specs=(pl.BlockSpec((None, 4, T, RET_DIM), lambda b, h: (b, 0, 0, h)),
                   pl.BlockSpec((None, SUBLANES, RET_DIM), lambda b, h: (b, 0, h)),
                   pl.BlockSpec((None, None, SUBLANES, LANES), lambda b, h: (b, h, 0, 0))),
        scratch_shapes=[pltpu.VMEM((T, RET_DIM), BF16)] * 3 + [pltpu.VMEM((N, RET_DIM), BF16)],
        compiler_params=_params("parallel", "arbitrary"),
    )(lg, proj, proj, proj, proj, cos, sin, gn, o, dlat)


def _na_geometry(rows):
    kh = min(NA_KH, rows)
    return kh, kh * GRID_W


def _na_scores(qm, kb, kc, bias_ref, e, dr0, kh):
    band = jnp.concatenate([bias_ref[e, pl.ds(dr0 + 2 * m, 1)].reshape(GRID_W, LANES) for m in range(kh // 2)], axis=1)
    s_loc = _dot_nt(qm, kb) + band
    s_ctx = _dot_nt(qm, kc)
    mx = jnp.maximum(jnp.max(s_loc, axis=-1, keepdims=True), jnp.max(s_ctx, axis=-1, keepdims=True))
    p_loc = jnp.exp(s_loc - mx)
    p_ctx = jnp.exp(s_ctx - mx)
    den = jnp.sum(p_loc, axis=-1, keepdims=True) + jnp.sum(p_ctx, axis=-1, keepdims=True)
    return p_loc, p_ctx, den


def _na_fwd(proj, bias2, n_ctx):
    B, T, _ = proj.shape
    C = n_ctx
    N = T - C
    R = N // GRID_W
    kh, nk = _na_geometry(R)
    scale = NA_DIM ** -0.5
    base = (4 * RET_WIDTH) // LANES

    def body(q_ref, k_ref, v_ref, bias_ref, out_ref, kb16, vb16):
        kb16[...] = k_ref[...].astype(BF16)
        vb16[...] = v_ref[...].astype(BF16)
        kc = kb16[0:C, :]
        vc = vb16[0:C, :]
        lane = lax.broadcasted_iota(jnp.int32, (GRID_W, LANES), 1)

        def row(r, carry):
            bs = jnp.clip(r - kh // 2, 0, R - kh)
            dr0 = bs - r + (NA_KH - 1)
            q = q_ref[pl.ds(pl.multiple_of(C + r * GRID_W, GRID_W), GRID_W), :] * scale
            band = pl.ds(pl.multiple_of(C + bs * GRID_W, GRID_W), nk)
            kb = kb16[band, :]
            vb = vb16[band, :]
            outs = []
            for e in range(2):
                qm = jnp.where((lane >= NA_DIM) == (e == 1), q, 0.0).astype(BF16)
                p_loc, p_ctx, den = _na_scores(qm, kb, kc, bias_ref, e, dr0, kh)
                oe = _dot(p_loc.astype(BF16), vb) + _dot(p_ctx.astype(BF16), vc)
                outs.append(oe / den)
            out_ref[pl.ds(pl.multiple_of(r * GRID_W, GRID_W), GRID_W), :] = jnp.where(
                lane < NA_DIM, outs[0], outs[1]).astype(BF16)
            return carry

        lax.fori_loop(0, R, row, 0)

    def col(seg):
        return pl.BlockSpec((None, T, LANES), lambda b, p, seg=seg: (b, 0, base + seg * NA_PAIRS + p))

    return pl.pallas_call(
        body, name="na_fwd", grid=(B, NA_PAIRS),
        out_shape=jax.ShapeDtypeStruct((B, N, NA_WIDTH), BF16),
        in_specs=[col(0), col(1), col(2),
                  pl.BlockSpec((2, 2 * NA_KH - 2, GRID_W, LANES), lambda b, p: (p, 0, 0, 0))],
        out_specs=pl.BlockSpec((None, N, LANES), lambda b, p: (b, 0, p)),
        scratch_shapes=[pltpu.VMEM((T, LANES), BF16)] * 2,
        compiler_params=_params("parallel", "arbitrary"),
    )(proj, proj, proj, bias2)


def _na_bwd(proj, bias2, dlat, n_ctx):
    B, T, _ = proj.shape
    C = n_ctx
    N = T - C
    R = N // GRID_W
    kh, nk = _na_geometry(R)
    scale = NA_DIM ** -0.5
    base = (4 * RET_WIDTH) // LANES

    def body(q_ref, k_ref, v_ref, bias_ref, dl_ref, d_ref, db_ref, kb16, vb16):
        b = pl.program_id(1)
        kb16[...] = k_ref[...].astype(BF16)
        vb16[...] = v_ref[...].astype(BF16)
        kc = kb16[0:C, :]
        vc = vb16[0:C, :]
        lane = lax.broadcasted_iota(jnp.int32, (GRID_W, LANES), 1)
        d_ref[...] = jnp.zeros(d_ref.shape, F32)

        @pl.when(b == 0)
        def _():
            db_ref[...] = jnp.zeros(db_ref.shape, F32)

        def row(r, carry):
            bs = jnp.clip(r - kh // 2, 0, R - kh)
            dr0 = bs - r + (NA_KH - 1)
            q = q_ref[pl.ds(pl.multiple_of(C + r * GRID_W, GRID_W), GRID_W), :] * scale
            do = dl_ref[pl.ds(pl.multiple_of(r * GRID_W, GRID_W), GRID_W), :]
            band = pl.ds(pl.multiple_of(C + bs * GRID_W, GRID_W), nk)
            kb = kb16[band, :]
            vb = vb16[band, :]
            dq = jnp.zeros((GRID_W, LANES), F32)
            for e in range(2):
                sel = (lane >= NA_DIM) == (e == 1)
                qm = jnp.where(sel, q, 0.0).astype(BF16)
                dom = jnp.where(sel, do, 0.0).astype(BF16)
                p_loc, p_ctx, den = _na_scores(qm, kb, kc, bias_ref, e, dr0, kh)
                inv = 1.0 / den
                p_loc = p_loc * inv
                p_ctx = p_ctx * inv
                dp_loc = _dot_nt(dom, vb)
                dp_ctx = _dot_nt(dom, vc)
                delta = (jnp.sum(p_loc * dp_loc, axis=-1, keepdims=True)
                         + jnp.sum(p_ctx * dp_ctx, axis=-1, keepdims=True))
                ds_loc = p_loc * (dp_loc - delta)
                ds_ctx = p_ctx * (dp_ctx - delta)
                dsb_loc = ds_loc.astype(BF16)
                dsb_ctx = ds_ctx.astype(BF16)
                dq = dq + jnp.where(sel, _dot(dsb_loc, kb) + _dot(dsb_ctx, kc), 0.0)
                d_ref[1, band, :] += _dot_tn(dsb_loc, qm)
                d_ref[2, band, :] += _dot_tn(p_loc.astype(BF16), dom)
                d_ref[1, 0:C, :] += _dot_tn(dsb_ctx, qm)
                d_ref[2, 0:C, :] += _dot_tn(p_ctx.astype(BF16), dom)
                for m in range(kh // 2):
                    db_ref[e, pl.ds(dr0 + 2 * m, 1)] += ds_loc[:, m * LANES:(m + 1) * LANES].reshape(1, GRID_W, LANES)
            d_ref[0, pl.ds(pl.multiple_of(C + r * GRID_W, GRID_W), GRID_W), :] = dq * scale
            return carry

        lax.fori_loop(0, R, row, 0)

    def col(seg):
        return pl.BlockSpec((None, T, LANES), lambda p, b, seg=seg: (b, 0, base + seg * NA_PAIRS + p))

    return pl.pallas_call(
        body, name="na_bwd", grid=(NA_PAIRS, B),
        out_shape=(jax.ShapeDtypeStruct((B, 3, T, NA_WIDTH), F32),
                   jax.ShapeDtypeStruct((NA_HEADS, 2 * NA_KH - 2, GRID_W, LANES), F32)),
        in_specs=[col(0), col(1), col(2),
                  pl.BlockSpec((2, 2 * NA_KH - 2, GRID_W, LANES), lambda p, b: (p, 0, 0, 0)),
                  pl.BlockSpec((None, N, LANES), lambda p, b: (b, 0, p))],
        out_specs=(pl.BlockSpec((None, 3, T, LANES), lambda p, b: (b, 0, 0, p)),
                   pl.BlockSpec((2, 2 * NA_KH - 2, GRID_W, LANES), lambda p, b: (p, 0, 0, 0))),
        scratch_shapes=[pltpu.VMEM((T, LANES), BF16)] * 2,
        compiler_params=_params("parallel", "arbitrary"),
    )(proj, proj, proj, bias2, dlat)


def _split3(a):
    hi = a.astype(BF16)
    r1 = a - hi.astype(F32)
    mid = r1.astype(BF16)
    lo = (r1 - mid.astype(F32)).astype(BF16)
    return hi, mid, lo


def _rpb_reduce(dbias2, onehot2):
    rows = dbias2.shape[0] * dbias2.shape[1]
    flat = dbias2.reshape(rows, GRID_W * LANES)

    def body(a_ref, oh_ref, o_ref):
        hi, mid, lo = _split3(a_ref[...])
        oh = oh_ref[...]
        o_ref[...] = _dot(hi, oh) + _dot(mid, oh) + _dot(lo, oh)

    return pl.pallas_call(
        body, name="rpb_reduce", out_shape=jax.ShapeDtypeStruct((rows, LANES), F32),
        in_specs=[_vmem(), _vmem()], out_specs=_vmem(),
        compiler_params=pltpu.CompilerParams(vmem_limit_bytes=VMEM_LIMIT),
    )(flat, onehot2)


def _dense_core(lat_ret, lat_na, x, tgt, modl, g_post_mix, g_pre_mlp, g_post_mlp, w_out, w1, w2):
    B, N, D = x.shape
    F = w1.shape[1]
    mixw = w_out.shape[0]
    half = mixw // 2
    tm = _div_tile(N, 256, 16)
    nt = N // tm
    fc = _div_tile(F, 1024, LANES)

    def body(lr_ref, ln_ref, x_ref, t_ref, gt1_ref, sh2_ref, sc2_ref, gt2_ref, gpm_ref, gpre_ref, gpo_ref,
             wout_hbm, w1_hbm, w2_hbm,
             dy1_ref, dlr_ref, dln_ref, dmix_ref, h2_ref, a_ref, du_ref, dz_ref, red_ref,
             wout_v, w1_v, w2_v, u_s, sems):
        @pl.when((pl.program_id(0) == 0) & (pl.program_id(1) == 0))
        def _():
            cps = [pltpu.make_async_copy(wout_hbm, wout_v, sems.at[0]),
                   pltpu.make_async_copy(w1_hbm, w1_v, sems.at[1]),
                   pltpu.make_async_copy(w2_hbm, w2_v, sems.at[2])]
            for cp in cps:
                cp.start()
            for cp in cps:
                cp.wait()

        gt1 = gt1_ref[...]
        sh2 = sh2_ref[...]
        sc2 = sc2_ref[...]
        gt2 = gt2_ref[...]
        gpm = gpm_ref[...]
        gpre = gpre_ref[...]
        gpo = gpo_ref[...]

        def rowmean(a):
            return jnp.mean(a, axis=-1, keepdims=True)

        def colsum(a):
            return jnp.sum(a, axis=0, keepdims=True)

        mix = _dot(lr_ref[...], wout_v[0:half, :]) + _dot(ln_ref[...], wout_v[half:, :])
        x = x_ref[...]
        rm = lax.rsqrt(rowmean(mix * mix) + NORM_EPS)
        mh = mix * rm
        nm = mh * gpm
        y1 = x + gt1 * nm
        r1 = lax.rsqrt(rowmean(y1 * y1) + NORM_EPS)
        xh = y1 * r1
        n1 = xh * gpre
        h2b = (n1 * (1.0 + sc2) + sh2).astype(BF16)
        h2_ref[...] = h2b
        z = jnp.zeros((tm, D), F32)
        for c0 in range(0, F, fc):
            u = _dot(h2b, w1_v[:, c0:c0 + fc])
            u_s[:, c0:c0 + fc] = u
            ru = jnp.maximum(u, 0.0)
            ab = (ru * ru).astype(BF16)
            a_ref[:, c0:c0 + fc] = ab
            z = z + _dot(ab, w2_v[c0:c0 + fc, :])
        r2 = lax.rsqrt(rowmean(z * z) + NORM_EPS)
        zh = z * r2
        n2 = zh * gpo
        y2 = y1 + gt2 * n2
        err = y2 - t_ref[...]
        loss = 0.5 * jnp.sum(rowmean(err * err))
        dy2 = err * (1.0 / D)
        red_ref[2:3, :] = colsum(dy2 * n2)
        dn2 = dy2 * gt2
        red_ref[6:7, :] = colsum(dn2 * zh)
        dzh = dn2 * gpo
        dz = r2 * (dzh - zh * rowmean(dzh * zh))
        dzb = dz.astype(BF16)
        dz_ref[...] = dzb
        dh2 = jnp.zeros((tm, D), F32)
        for c0 in range(0, F, fc):
            da = _dot_nt(dzb, w2_v[c0:c0 + fc, :])
            dub = (da * (2.0 * jnp.maximum(u_s[:, c0:c0 + fc], 0.0))).astype(BF16)
            du_ref[:, c0:c0 + fc] = dub
            dh2 = dh2 + _dot_nt(dub, w1_v[:, c0:c0 + fc])
        red_ref[3:4, :] = colsum(dh2 * n1)
        red_ref[4:5, :] = colsum(dh2)
        dn1 = dh2 * (1.0 + sc2)
        red_ref[5:6, :] = colsum(dn1 * xh)
        dxh = dn1 * gpre
        dy1 = dy2 + r1 * (dxh - xh * rowmean(dxh * xh))
        dy1_ref[...] = dy1
        red_ref[0:1, :] = colsum(dy1 * nm)
        dnm = dy1 * gt1
        red_ref[1:2, :] = colsum(dnm * mh)
        dmh = dnm * gpm
        dmix = (rm * (dmh - mh * rowmean(dmh * mh))).astype(BF16)
        dmix_ref[...] = dmix
        dlr_ref[...] = _dot_nt(dmix, wout_v[0:half, :])
        dln_ref[...] = _dot_nt(dmix, wout_v[half:, :])
        red_ref[7:8, :] = jnp.zeros((1, D), F32) + loss

    def tok(w):
        return pl.BlockSpec((None, tm, w), lambda b, t: (b, t, 0))

    def mod(k):
        return pl.BlockSpec((None, None, 1, D), lambda b, t, k=k: (b, k, 0, 0))

    def vec():
        return pl.BlockSpec((1, D), lambda b, t: (0, 0))

    return pl.pallas_call(
        body, name="dense_core", grid=(B, nt),
        out_shape=(jax.ShapeDtypeStruct((B, N, D), F32), jax.ShapeDtypeStruct((B, N, half), F32),
                   jax.ShapeDtypeStruct((B, N, half), F32), jax.ShapeDtypeStruct((B, N, D), BF16),
                   jax.ShapeDtypeStruct((B, N, D), BF16), jax.ShapeDtypeStruct((B, N, F), BF16),
                   jax.ShapeDtypeStruct((B, N, F), BF16), jax.ShapeDtypeStruct((B, N, D), BF16),
                   jax.ShapeDtypeStruct((B, nt, SUBLANES, D), F32)),
        in_specs=[tok(half), tok(half), tok(D), tok(D), mod(2), mod(3), mod(4), mod(5), vec(), vec(), vec(),
                  _any(), _any(), _any()],
        out_specs=(tok(D), tok(half), tok(half), tok(D), tok(D), tok(F), tok(F), tok(D),
                   pl.BlockSpec((None, None, SUBLANES, D), lambda b, t: (b, t, 0, 0))),
        scratch_shapes=[pltpu.VMEM((mixw, D), BF16), pltpu.VMEM((D, F), BF16), pltpu.VMEM((F, D), BF16),
                        pltpu.VMEM((tm, F), F32), pltpu.SemaphoreType.DMA((3,))],
        compiler_params=_params("arbitrary", "arbitrary"),
    )(lat_ret, lat_na, x, tgt, modl, modl, modl, modl, g_post_mix, g_pre_mlp, g_post_mlp, w_out, w1, w2)


def _inproj_bwd(dret, dna, x_all, dy1, modl, g1, w_in, n_ctx):
    B, T, D = x_all.shape
    N = T - n_ctx
    tm = _div_tile(n_ctx, 256, 16)
    nct = n_ctx // tm
    nt = T // tm
    nseg_r = dret.shape[1]
    nseg_n = dna.shape[1]
    nw = w_in.shape[1]

    def body(*refs):
        seg_refs = refs[:nseg_r + nseg_n]
        x_ref, dy1_ref, sc_ref, g_ref, w_ref, dx_ref, red_ref = refs[nseg_r + nseg_n:]
        t = pl.program_id(1)
        dh = jnp.zeros((tm, D), F32)
        for s, ref in enumerate(seg_refs):
            dh = dh + _dot_nt(ref[...].astype(BF16), w_ref[:, s * SEG:(s + 1) * SEG])
        x = x_ref[...]
        g = g_ref[...]
        r = lax.rsqrt(jnp.mean(x * x, axis=-1, keepdims=True) + NORM_EPS)
        xh = x * r
        red_ref[0:1, :] = jnp.sum(dh, axis=0, keepdims=True)
        red_ref[1:2, :] = jnp.sum(dh * (xh * g), axis=0, keepdims=True)
        dn = dh * (1.0 + sc_ref[...])
        red_ref[2:3, :] = jnp.sum(dn * xh, axis=0, keepdims=True)
        red_ref[3:, :] = jnp.zeros((SUBLANES - 3, D), F32)
        dxh = dn * g
        dx = r * (dxh - xh * jnp.mean(dxh * xh, axis=-1, keepdims=True))
        dx_ref[...] = dx + jnp.where(t >= nct, dy1_ref[...], 0.0)

    def mrow(b, t):
        return jnp.where(t < nct, B, b)

    def seg(s):
        return pl.BlockSpec((None, None, tm, SEG), lambda b, t, s=s: (b, s, t, 0))

    def lat_tile():
        return pl.BlockSpec((None, tm, D), lambda b, t: (b, jnp.maximum(t - nct, 0), 0))

    return pl.pallas_call(
        body, name="inproj_bwd", grid=(B, nt),
        out_shape=(jax.ShapeDtypeStruct((B, N, D), F32), jax.ShapeDtypeStruct((B, nt, SUBLANES, D), F32)),
        in_specs=[seg(s) for s in range(nseg_r)] + [seg(s) for s in range(nseg_n)]
                 + [pl.BlockSpec((None, tm, D), lambda b, t: (b, t, 0)), lat_tile(),
                    pl.BlockSpec((None, None, 1, D), lambda b, t: (mrow(b, t), 1, 0, 0)),
                    pl.BlockSpec((1, D), lambda b, t: (0, 0)),
                    pl.BlockSpec((D, nw), lambda b, t: (0, 0))],
        out_specs=(lat_tile(), pl.BlockSpec((None, None, SUBLANES, D), lambda b, t: (b, t, 0, 0))),
        compiler_params=_params("arbitrary", "arbitrary"),
    )(*([dret] * nseg_r), *([dna] * nseg_n), x_all, dy1, modl, g1, w_in)


def _tn_matmul(lhs, rhs, name):
    B, S, T, W = lhs.shape
    nn = rhs.shape[-1]
    tk = _div_tile(T, 1024, LANES)
    bm = _div_tile(W, 1024, LANES)
    bn = _div_tile(nn, 1024, LANES)
    nkt = T // tk
    nk = B * nkt

    def body(l_ref, r_ref, o_ref):
        @pl.when(pl.program_id(3) == 0)
        def _():
            o_ref[...] = jnp.zeros(o_ref.shape, F32)

        o_ref[...] += _dot_tn(l_ref[...].astype(BF16), r_ref[...].astype(BF16))

    nwb = W // bm
    return pl.pallas_call(
        functools.partial(body), name=name, grid=(S, nwb, nn // bn, nk),
        out_shape=jax.ShapeDtypeStruct((S * W, nn), F32),
        in_specs=[pl.BlockSpec((None, None, tk, bm), lambda s, i, j, k: (k // nkt, s, k % nkt, i)),
                  pl.BlockSpec((None, tk, bn), lambda s, i, j, k: (k // nkt, k % nkt, j))],
        out_specs=pl.BlockSpec((bm, bn), lambda s, i, j, k: (s * nwb + i, j)),
        compiler_params=_params("parallel", "parallel", "parallel", "arbitrary"),
    )(lhs, rhs)


def _grad_rs(gin_t, gout, g1, g2):
    rin = gin_t.shape[0] // N_DEV
    rout = gout.shape[0] // N_DEV
    c1 = g1.shape[1] // N_DEV
    r2 = g2.shape[0] // N_DEV

    def body(gin_ref, gout_ref, g1_ref, g2_ref, bin_ref, bout_ref, b1_ref, b2_ref, send_sems, recv_sems, loc_sems):
        me, peers = _me_and_peers()

        def s_in(p):
            return gin_ref.at[pl.ds(pl.multiple_of(p * rin, SUBLANES), rin), :]

        def s_out(p):
            return gout_ref.at[pl.ds(pl.multiple_of(p * rout, SUBLANES), rout), :]

        def s_1(p):
            return g1_ref.at[:, pl.ds(pl.multiple_of(p * c1, c1), c1)]

        def s_2(p):
            return g2_ref.at[pl.ds(pl.multiple_of(p * r2, SUBLANES), r2), :]

        srcs = (s_in, s_out, s_1, s_2)
        bufs = (bin_ref, bout_ref, b1_ref, b2_ref)
        local = [pltpu.make_async_copy(s(me), b.at[me], loc_sems.at[k]) for k, (s, b) in enumerate(zip(srcs, bufs))]
        for cp in local:
            cp.start()
        sent = []
        for k, (s, b) in enumerate(zip(srcs, bufs)):
            for i, (dev, pid) in enumerate(peers):
                cp = pltpu.make_async_remote_copy(src_ref=s(pid), dst_ref=b.at[me], send_sem=send_sems.at[k, i],
                                                  recv_sem=recv_sems.at[k, i], device_id=dev, device_id_type=MESH)
                cp.start()
                sent.append(cp)
        for k, (s, b) in enumerate(zip(srcs, bufs)):
            for i, (dev, pid) in enumerate(peers):
                pltpu.make_async_remote_copy(src_ref=s(pid), dst_ref=b.at[pid], send_sem=send_sems.at[k, i],
                                             recv_sem=recv_sems.at[k, i], device_id=dev,
                                             device_id_type=MESH).wait_recv()
        for cp in sent:
            cp.wait_send()
        for cp in local:
            cp.wait()

    return pl.pallas_call(
        body, name="grad_rs",
        out_shape=(jax.ShapeDtypeStruct((N_DEV, rin, gin_t.shape[1]), F32),
                   jax.ShapeDtypeStruct((N_DEV, rout, gout.shape[1]), F32),
                   jax.ShapeDtypeStruct((N_DEV, g1.shape[0], c1), F32),
                   jax.ShapeDtypeStruct((N_DEV, r2, g2.shape[1]), F32)),
        in_specs=[_any()] * 4, out_specs=(_any(),) * 4,
        scratch_shapes=[pltpu.SemaphoreType.DMA((4, N_DEV - 1)), pltpu.SemaphoreType.DMA((4, N_DEV - 1)),
                        pltpu.SemaphoreType.DMA((4,))],
        compiler_params=pltpu.CompilerParams(vmem_limit_bytes=VMEM_LIMIT),
    )(gin_t, gout, g1, g2)


def _sum_slots(buf, name):
    _, rows, cols = buf.shape
    tr = _div_tile(rows, 256, SUBLANES)

    def body(b_ref, o_ref):
        acc = b_ref[0]
        for k in range(1, N_DEV):
            acc = acc + b_ref[k]
        o_ref[...] = acc

    return pl.pallas_call(
        functools.partial(body), name=name, grid=(rows // tr,),
        out_shape=jax.ShapeDtypeStruct((rows, cols), F32),
        in_specs=[pl.BlockSpec((N_DEV, tr, cols), lambda i: (0, i, 0))],
        out_specs=pl.BlockSpec((tr, cols), lambda i: (i, 0)),
        compiler_params=_params("parallel"),
    )(buf)


def _small_ar(vec, dmods, silu_all, w_ada, c_ctx):
    rv = vec.shape[0]
    D = silu_all.shape[1]
    ncol = w_ada.shape[1]
    nm = dmods.shape[1]
    srows = silu_all.shape[0]

    def body(vec_ref, dm_ref, s_ref, w_ref, cc_ref, tot_ref, gb_ref, gw_ref, gc_ref,
             vbuf, mbuf, tbuf, dmx, send1, recv1, send2, recv2, send3, recv3):
        me, _ = _me_and_peers()
        vbuf[me] = vec_ref[...]
        mbuf[me] = dm_ref[...]
        _exchange(lambda p: vbuf.at[me], lambda p: vbuf.at[p], send1, recv1)
        _exchange(lambda p: mbuf.at[me], lambda p: mbuf.at[p], send2, recv2)
        tot = vbuf[0]
        msum = mbuf[0]
        for k in range(1, N_DEV):
            tot = tot + vbuf[k]
            msum = msum + mbuf[k]
        tot_ref[...] = tot
        gb_ref[...] = jnp.sum(msum, axis=0, keepdims=True)
        loc = pl.ds(pl.multiple_of(me * ncol, ncol), ncol)
        for k in range(N_DEV):
            dmx[k * SUBLANES:(k + 1) * SUBLANES, :] = mbuf[k, :, loc]
        cm = msum[2:3, :]
        mbuf[0, 2:3, :] = cm
        cm_loc = mbuf[0, 2:3, loc]
        dmx[N_DEV * SUBLANES:, :] = jnp.concatenate([cm_loc, jnp.zeros((SUBLANES - 1, ncol), F32)], axis=0)
        gw_ref[...] = _dot_tn(s_ref[...], dmx[...])
        tbuf[me] = _dot_nt(dmx[N_DEV * SUBLANES:, :], w_ref[...])
        _exchange(lambda p: tbuf.at[me], lambda p: tbuf.at[p], send3, recv3)
        tsum = tbuf[0]
        for k in range(1, N_DEV):
            tsum = tsum + tbuf[k]
        cc = cc_ref[...]
        sg = _sigmoid(cc)
        gc_ref[...] = tsum[0:1, :] * (sg * (1.0 + cc * (1.0 - sg)))

    return pl.pallas_call(
        body, name="small_ar",
        out_shape=(jax.ShapeDtypeStruct((rv, LANES), F32), jax.ShapeDtypeStruct((1, nm), F32),
                   jax.ShapeDtypeStruct((D, ncol), F32), jax.ShapeDtypeStruct((1, D), F32)),
        in_specs=[_vmem()] * 5, out_specs=(_vmem(),) * 4,
        scratch_shapes=[pltpu.VMEM((N_DEV, rv, LANES), F32), pltpu.VMEM((N_DEV, SUBLANES, nm), F32),
                        pltpu.VMEM((N_DEV, SUBLANES, D), F32), pltpu.VMEM((srows, ncol), F32)]
                       + [pltpu.SemaphoreType.DMA((N_DEV - 1,))] * 6,
        compiler_params=pltpu.CompilerParams(vmem_limit_bytes=VMEM_LIMIT),
    )(vec, dmods, silu_all, w_ada, c_ctx.reshape(1, D))


def _adamw(w, g, m, v, name):
    rows, cols = w.shape
    tr = _div_tile(rows, 256, SUBLANES) if rows * cols > 65536 else rows

    def body(w_ref, g_ref, m_ref, v_ref, d_ref, nm_ref, nv_ref):
        gv = g_ref[...]
        mn = ADAM_B1 * m_ref[...] + (1.0 - ADAM_B1) * gv
        vn = ADAM_B2 * v_ref[...] + (1.0 - ADAM_B2) * (gv * gv)
        m_hat = mn / (1.0 - ADAM_B1 ** ADAM_STEP)
        v_hat = vn / (1.0 - ADAM_B2 ** ADAM_STEP)
        d_ref[...] = -ADAM_LR * (m_hat / (jnp.sqrt(v_hat) + ADAM_EPS) + ADAM_WD * w_ref[...])
        nm_ref[...] = mn
        nv_ref[...] = vn

    spec = pl.BlockSpec((tr, cols), lambda i: (i, 0))
    return pl.pallas_call(
        functools.partial(body), name=name, grid=(rows // tr,),
        out_shape=(jax.ShapeDtypeStruct((rows, cols), F32),) * 3,
        in_specs=[spec] * 4, out_specs=(spec,) * 3,
        compiler_params=_params("parallel"),
    )(w, g, m, v)


def _rope_tables(n_ctx, n):
    n_freq = RET_DIM // 4
    inv = ROPE_BASE ** (-jnp.arange(n_freq, dtype=F32) / n_freq)
    tok = jnp.arange(n)
    pos_r = (tok // GRID_W).astype(F32)
    pos_c = (tok % GRID_W).astype(F32)
    ang_r = pos_r[:, None] * inv[None, :]
    ang_c = pos_c[:, None] * inv[None, :]
    cos = jnp.concatenate([jnp.cos(ang_r), jnp.cos(ang_r), jnp.cos(ang_c), jnp.cos(ang_c)], axis=-1)
    sin = jnp.concatenate([-jnp.sin(ang_r), jnp.sin(ang_r), -jnp.sin(ang_c), jnp.sin(ang_c)], axis=-1)
    cos = jnp.concatenate([jnp.ones((n_ctx, RET_DIM), F32), cos], axis=0)
    sin = jnp.concatenate([jnp.zeros((n_ctx, RET_DIM), F32), sin], axis=0)
    return cos, sin


def _na_tables():
    q = np.arange(GRID_W)[:, None]
    k = np.arange(GRID_W)[None, :]
    start = np.clip(q - NA_KW // 2, 0, GRID_W - NA_KW)
    valid = (k >= start) & (k < start + NA_KW)
    dc = np.clip(k - q + (NA_KW - 1), 0, 2 * NA_KW - 2)
    ncls = 2 * NA_KW - 1
    onehot = (dc[None] == np.arange(ncls)[:, None, None]) & valid[None]
    oh2 = np.zeros((GRID_W, LANES, LANES), np.float32)
    for c in range(ncls):
        oh2[:, :GRID_W, c] = onehot[c]
        oh2[:, GRID_W:, 32 + c] = onehot[c]
    return onehot.astype(np.float32), valid, oh2.reshape(GRID_W * LANES, LANES)


def _paired_bias(rpb, onehot, valid):
    t = jnp.einsum("hdc,cqk->hdqk", rpb, jnp.asarray(onehot), precision=lax.Precision.HIGHEST)
    t = jnp.where(jnp.asarray(valid)[None, None], t, NEG_INF)
    return jnp.concatenate([t[:, :-1], t[:, 1:]], axis=-1)


def kernel(x, c, ctx, c_ctx, w_ada, b_ada, g_pre_mix, g_post_mix, g_pre_mlp, g_post_mlp, w_in, ret_decay, ret_gn, na_rpb, w_out, w_mlp1, w_mlp2, loss_target, m_c_ctx, m_w_ada, m_b_ada, m_g_pre_mix, m_g_post_mix, m_g_pre_mlp, m_g_post_mlp, m_w_in, m_ret_decay, m_ret_gn, m_na_rpb, m_w_out, m_w_mlp1, m_w_mlp2, v_c_ctx, v_w_ada, v_b_ada, v_g_pre_mix, v_g_post_mix, v_g_pre_mlp, v_g_post_mlp, v_w_in, v_ret_decay, v_ret_gn, v_na_rpb, v_w_out, v_w_mlp1, v_w_mlp2):
    B, N, D = x.shape
    C = ctx.shape[1]
    T = C + N
    me = 4 * lax.axis_index("x") + 2 * lax.axis_index("y") + lax.axis_index("c")

    silu_all, mods_g = _mod_gather(c, c_ctx, w_ada[0], b_ada)
    mods_full = mods_g.transpose(1, 0, 2).reshape(mods_g.shape[1], N_MOD * D)
    mine = lax.dynamic_slice_in_dim(mods_full, me * SUBLANES, B, axis=0)
    modl = jnp.concatenate([mine, mods_full[N_DEV * SUBLANES:N_DEV * SUBLANES + 1]], axis=0)
    modl = modl.reshape(B + 1, N_MOD, 1, D)

    gin, wout_b, w1_b, w2_b = _w_gather(w_in[0], w_out[0], w_mlp1[0], w_mlp2[0])
    win_b = gin.transpose(1, 0, 2).reshape(D, N_DEV * gin.shape[2])

    cos, sin = _rope_tables(C, N)
    onehot, valid, oh2 = _na_tables()
    bias2 = _paired_bias(na_rpb[0], onehot, valid)
    lg = jax.nn.log_sigmoid(ret_decay[0].astype(F32))

    x_all = jnp.concatenate([ctx, x], axis=1)
    h_all, proj = _inproj_fwd(x_all, modl, g_pre_mix, win_b, C)
    o_ret, lat_ret = _ret_fwd(proj, cos, sin, lg, ret_gn, C)
    lat_na = _na_fwd(proj, bias2, C)

    (dy1, dlat_ret, dlat_na, dmix, h2, act, du, dz, red_d) = _dense_core(
        lat_ret, lat_na, x, loss_target, modl, g_post_mix, g_pre_mlp, g_post_mlp, wout_b, w1_b, w2_b)

    dret, dgn_p, dlg_p = _ret_bwd(proj, cos, sin, lg, ret_gn, o_ret, dlat_ret, C)
    dna, dbias2 = _na_bwd(proj, bias2, dlat_na, C)
    grad_x, red_i = _inproj_bwd(dret, dna, x_all, dy1, modl, g_pre_mix, win_b, C)

    gw_out_p = jnp.concatenate([_tn_matmul(lat_ret[:, None], dmix, "gw_out_ret"),
                                _tn_matmul(lat_na[:, None], dmix, "gw_out_na")], axis=0)
    gw1_p = _tn_matmul(h2[:, None], du, "gw_mlp1")
    gw2_p = _tn_matmul(act[:, None], dz, "gw_mlp2")
    gwin_t_p = jnp.concatenate([_tn_matmul(dret, h_all, "gw_in_ret"), _tn_matmul(dna, h_all, "gw_in_na")], axis=0)

    bin_, bout, b1, b2 = _grad_rs(gwin_t_p, gw_out_p, gw1_p, gw2_p)
    g_w_in = _sum_slots(bin_, "sum_w_in").T
    g_w_out = _sum_slots(bout, "sum_w_out")
    g_w1 = _sum_slots(b1, "sum_w_mlp1")
    g_w2 = _sum_slots(b2, "sum_w_mlp2")

    rd = red_d.sum(axis=1)[:, :, :]
    ri = red_i
    nct = ri.shape[1] * C // T
    ri_ctx = ri[:, :nct].sum(axis=(0, 1))
    ri_lat = ri[:, nct:].sum(axis=1)
    d_mods = jnp.concatenate([ri_lat[:, 0], ri_lat[:, 1], rd[:, 0], rd[:, 4], rd[:, 3], rd[:, 2]], axis=-1)
    d_cmods = jnp.concatenate([ri_ctx[0], ri_ctx[1], jnp.zeros(((N_MOD - 2) * D,), F32)])[None]
    dm_slot = jnp.concatenate([d_mods, d_cmods, jnp.zeros((SUBLANES - B - 1, N_MOD * D), F32)], axis=0)
    dg_pre_mix = ri_lat[:, 2].sum(axis=0) + ri_ctx[2]
    dg_post_mix = rd[:, 1].sum(axis=0)
    dg_pre_mlp = rd[:, 5].sum(axis=0)
    dg_post_mlp = rd[:, 6].sum(axis=0)
    loss_p = rd[:, 7, 0].sum()
    d_gn = dgn_p[:, 0].sum(axis=0)
    d_lg = dlg_p[:, :, :2, 0].sum(axis=0).T
    d_decay = d_lg * jax.nn.sigmoid(-ret_decay[0].astype(F32))
    rr = _rpb_reduce(dbias2, jnp.asarray(oh2, BF16)).reshape(NA_HEADS, 2 * NA_KH - 2, LANES)
    ncls = 2 * NA_KW - 1
    d_rpb = (jnp.pad(rr[:, :, :ncls], ((0, 0), (0, 1), (0, 0))) + jnp.pad(rr[:, :, 32:32 + ncls], ((0, 0), (1, 0), (0, 0))))
    d_rpb32 = jnp.pad(d_rpb, ((0, 0), (0, 0), (0, 32 - ncls)))
    pieces = [dg_pre_mix, dg_post_mix, dg_pre_mlp, dg_post_mlp, d_gn, d_rpb32.reshape(-1),
              jnp.pad(d_decay.reshape(-1), (0, LANES - d_decay.size)), jnp.full((LANES,), loss_p, F32)]
    vec = jnp.concatenate(pieces)
    pad = (-vec.shape[0]) % (SUBLANES * LANES)
    vec = jnp.pad(vec, (0, pad)).reshape(-1, LANES)
    tot, g_b_ada, g_w_ada, g_c_ctx = _small_ar(vec, dm_slot, silu_all, w_ada[0], c_ctx)
    flat = tot.reshape(-1)
    o0 = 0
    g_pre_mix_g = flat[o0:o0 + D]; o0 += D
    g_post_mix_g = flat[o0:o0 + D]; o0 += D
    g_pre_mlp_g = flat[o0:o0 + D]; o0 += D
    g_post_mlp_g = flat[o0:o0 + D]; o0 += D
    g_gn = flat[o0:o0 + RET_WIDTH]; o0 += RET_WIDTH
    nrpb = NA_HEADS * (2 * NA_KH - 1) * 32
    g_rpb = flat[o0:o0 + nrpb].reshape(NA_HEADS, 2 * NA_KH - 1, 32)[:, :, :ncls]; o0 += nrpb
    g_decay = flat[o0:o0 + 2 * RET_HEADS].reshape(2, RET_HEADS); o0 += LANES
    loss = flat[o0]

    grads = {
        "c_ctx": g_c_ctx.reshape(c_ctx.shape), "w_ada": g_w_ada[None], "b_ada": g_b_ada.reshape(b_ada.shape),
        "g_pre_mix": g_pre_mix_g[None], "g_post_mix": g_post_mix_g[None], "g_pre_mlp": g_pre_mlp_g[None],
        "g_post_mlp": g_post_mlp_g[None], "w_in": g_w_in[None], "ret_decay": g_decay[None], "ret_gn": g_gn[None],
        "na_rpb": g_rpb[None], "w_out": g_w_out[None], "w_mlp1": g_w1[None], "w_mlp2": g_w2[None],
    }
    weights = dict(c_ctx=c_ctx, w_ada=w_ada, b_ada=b_ada, g_pre_mix=g_pre_mix, g_post_mix=g_post_mix,
                   g_pre_mlp=g_pre_mlp, g_post_mlp=g_post_mlp, w_in=w_in, ret_decay=ret_decay, ret_gn=ret_gn,
                   na_rpb=na_rpb, w_out=w_out, w_mlp1=w_mlp1, w_mlp2=w_mlp2)
    m_in = dict(c_ctx=m_c_ctx, w_ada=m_w_ada, b_ada=m_b_ada, g_pre_mix=m_g_pre_mix, g_post_mix=m_g_post_mix,
                g_pre_mlp=m_g_pre_mlp, g_post_mlp=m_g_post_mlp, w_in=m_w_in, ret_decay=m_ret_decay,
                ret_gn=m_ret_gn, na_rpb=m_na_rpb, w_out=m_w_out, w_mlp1=m_w_mlp1, w_mlp2=m_w_mlp2)
    v_in = dict(c_ctx=v_c_ctx, w_ada=v_w_ada, b_ada=v_b_ada, g_pre_mix=v_g_pre_mix, g_post_mix=v_g_post_mix,
                g_pre_mlp=v_g_pre_mlp, g_post_mlp=v_g_post_mlp, w_in=v_w_in, ret_decay=v_ret_decay,
                ret_gn=v_ret_gn, na_rpb=v_na_rpb, w_out=v_w_out, w_mlp1=v_w_mlp1, w_mlp2=v_w_mlp2)
    names = list(weights)
    deltas, new_m, new_v = {}, {}, {}
    for n in names:
        shp = weights[n].shape
        two_d = (-1, shp[-1]) if len(shp) > 1 else (1, shp[0])
        d, nm, nv = _adamw(weights[n].reshape(two_d), grads[n].reshape(two_d), m_in[n].reshape(two_d),
                           v_in[n].reshape(two_d), "adamw_" + n)
        deltas[n], new_m[n], new_v[n] = d.reshape(shp), nm.reshape(shp), nv.reshape(shp)
    return (loss, grad_x, *[grads[n] for n in names], *[deltas[n] for n in names],
            *[new_m[n] for n in names], *[new_v[n] for n in names])
```

```python
import functools
import math

import numpy as np
import jax
import jax.numpy as jnp
from jax import lax
from jax.experimental import pallas as pl
from jax.experimental.pallas import tpu as pltpu

F32 = jnp.float32
BF16 = jnp.bfloat16
MESH = pl.DeviceIdType.MESH

N_DEV = 8
LANES = 128
SUBLANES = 8
VMEM_LIMIT = 60 * 1024 * 1024

GRID_W = 64
RET_HEADS = 4
RET_DIM = 128
RET_WIDTH = RET_HEADS * RET_DIM
NA_HEADS = 8
NA_DIM = 64
NA_WIDTH = NA_HEADS * NA_DIM
NA_PAIRS = NA_HEADS // 2
NA_KH = 8
NA_KW = 16
NA_GROUP = 4
SEG = 512
ROPE_BASE = 10000.0
NORM_EPS = 1e-6
NEG_INF = -1e30
N_MOD = 6

ADAM_LR = 0.001
ADAM_B1 = 0.9
ADAM_B2 = 0.999
ADAM_EPS = 1e-08
ADAM_WD = 0.01
ADAM_STEP = 10


def _dot(a, b):
    return lax.dot_general(a, b, (((1,), (0,)), ((), ())), preferred_element_type=F32)


def _dot_nt(a, b):
    return lax.dot_general(a, b, (((1,), (1,)), ((), ())), preferred_element_type=F32)


def _dot_tn(a, b):
    return lax.dot_general(a, b, (((0,), (0,)), ((), ())), preferred_element_type=F32)


def _sigmoid(x):
    return 1.0 / (1.0 + jnp.exp(-x))


def _div_tile(n, cap, mult):
    if n <= cap:
        return n
    for t in range(cap - cap % mult, 0, -mult):
        if n % t == 0:
            return t
    raise ValueError(f"no tile for {n}")


def _params(*sem):
    return pltpu.CompilerParams(dimension_semantics=tuple(sem) if sem else None,
                                vmem_limit_bytes=VMEM_LIMIT)


def _vmem():
    return pl.BlockSpec(memory_space=pltpu.VMEM)


def _any():
    return pl.BlockSpec(memory_space=pl.ANY)


def _me_and_peers():
    x, y, c = lax.axis_index("x"), lax.axis_index("y"), lax.axis_index("c")
    me = 4 * x + 2 * y + c
    peers = []
    for m in range(1, N_DEV):
        px = 1 - x if (m >> 2) & 1 else x
        py = 1 - y if (m >> 1) & 1 else y
        pc = 1 - c if m & 1 else c
        peers.append(((px, py, pc), 4 * px + 2 * py + pc))
    return me, peers


def _exchange(src_for, dst_from, send_sems, recv_sems):
    me, peers = _me_and_peers()
    sent = []
    for i, (dev, pid) in enumerate(peers):
        cp = pltpu.make_async_remote_copy(src_ref=src_for(pid), dst_ref=dst_from(me),
                                          send_sem=send_sems.at[i], recv_sem=recv_sems.at[i],
                                          device_id=dev, device_id_type=MESH)
        cp.start()
        sent.append(cp)
    for i, (dev, pid) in enumerate(peers):
        pltpu.make_async_remote_copy(src_ref=src_for(pid), dst_ref=dst_from(pid),
                                     send_sem=send_sems.at[i], recv_sem=recv_sems.at[i],
                                     device_id=dev, device_id_type=MESH).wait_recv()
    for cp in sent:
        cp.wait_send()


def _mod_gather(c, c_ctx, w_ada, b_ada):
    B, D = c.shape
    ncol = w_ada.shape[1]
    rows = SUBLANES * N_DEV + SUBLANES

    def body(c_ref, cc_ref, w_ref, b_ref, s_ref, m_ref, send1, recv1, send2, recv2):
        me, _ = _me_and_peers()
        cv = c_ref[...]
        slot = jnp.concatenate([cv * _sigmoid(cv), jnp.zeros((SUBLANES - B, D), F32)], axis=0)
        my_rows = pl.ds(pl.multiple_of(me * SUBLANES, SUBLANES), SUBLANES)
        s_ref[my_rows, :] = slot
        ccv = cc_ref[...]
        s_ref[SUBLANES * N_DEV:, :] = jnp.concatenate(
            [ccv * _sigmoid(ccv), jnp.zeros((SUBLANES - 1, D), F32)], axis=0)

        def rows_of(p):
            return s_ref.at[pl.ds(pl.multiple_of(p * SUBLANES, SUBLANES), SUBLANES), :]

        _exchange(lambda p: rows_of(me), rows_of, send1, recv1)
        b_loc = b_ref[:, pl.ds(pl.multiple_of(me * ncol, ncol), ncol)]
        m_ref[me] = _dot(s_ref[...], w_ref[...]) + b_loc
        _exchange(lambda p: m_ref.at[me], lambda p: m_ref.at[p], send2, recv2)

    return pl.pallas_call(
        body, name="mod_gather",
        out_shape=(jax.ShapeDtypeStruct((rows, D), F32), jax.ShapeDtypeStruct((N_DEV, rows, ncol), F32)),
        in_specs=[_vmem()] * 4, out_specs=(_vmem(), _vmem()),
        scratch_shapes=[pltpu.SemaphoreType.DMA((N_DEV - 1,))] * 4,
        compiler_params=pltpu.CompilerParams(vmem_limit_bytes=VMEM_LIMIT),
    )(c, c_ctx.reshape(1, D), w_ada, b_ada)


def _w_gather(w_in, w_out, w1, w2):
    D, cin = w_in.shape
    rout = w_out.shape[0]
    c1 = w1.shape[1]
    r2 = w2.shape[0]

    def body(win_ref, wout_ref, w1_ref, w2_ref, gin_ref, gout_ref, g1_ref, g2_ref,
             bin_, bout, b1, b2, send_sems, recv_sems, loc_sems):
        me, _ = _me_and_peers()
        bin_[...] = win_ref[...].astype(BF16)
        bout[...] = wout_ref[...].astype(BF16)
        b1[...] = w1_ref[...].astype(BF16)
        b2[...] = w2_ref[...].astype(BF16)

        def d_in(p):
            return gin_ref.at[p]

        def d_out(p):
            return gout_ref.at[pl.ds(pl.multiple_of(p * rout, rout), rout), :]

        def d_1(p):
            return g1_ref.at[:, pl.ds(pl.multiple_of(p * c1, c1), c1)]

        def d_2(p):
            return g2_ref.at[pl.ds(pl.multiple_of(p * r2, r2), r2), :]

        srcs = (bin_, bout, b1, b2)
        dsts = (d_in, d_out, d_1, d_2)
        local = [pltpu.make_async_copy(s, d(me), loc_sems.at[k]) for k, (s, d) in enumerate(zip(srcs, dsts))]
        for cp in local:
            cp.start()
        _, peers = _me_and_peers()
        sent = []
        for k, (s, d) in enumerate(zip(srcs, dsts)):
            for i, (dev, pid) in enumerate(peers):
                cp = pltpu.make_async_remote_copy(src_ref=s, dst_ref=d(me), send_sem=send_sems.at[k, i],
                                                  recv_sem=recv_sems.at[k, i], device_id=dev, device_id_type=MESH)
                cp.start()
                sent.append(cp)
        for k, (s, d) in enumerate(zip(srcs, dsts)):
            for i, (dev, pid) in enumerate(peers):
                pltpu.make_async_remote_copy(src_ref=s, dst_ref=d(pid), send_sem=send_sems.at[k, i],
                                             recv_sem=recv_sems.at[k, i], device_id=dev,
                                             device_id_type=MESH).wait_recv()
        for cp in sent:
            cp.wait_send()
        for cp in local:
            cp.wait()

    return pl.pallas_call(
        body, name="w_gather",
        out_shape=(jax.ShapeDtypeStruct((N_DEV, D, cin), BF16),
                   jax.ShapeDtypeStruct((N_DEV * rout, w_out.shape[1]), BF16),
                   jax.ShapeDtypeStruct((D, N_DEV * c1), BF16),
                   jax.ShapeDtypeStruct((N_DEV * r2, w2.shape[1]), BF16)),
        in_specs=[_vmem()] * 4, out_specs=(_any(),) * 4,
        scratch_shapes=[pltpu.VMEM(w_in.shape, BF16), pltpu.VMEM(w_out.shape, BF16),
                        pltpu.VMEM(w1.shape, BF16), pltpu.VMEM(w2.shape, BF16),
                        pltpu.SemaphoreType.DMA((4, N_DEV - 1)), pltpu.SemaphoreType.DMA((4, N_DEV - 1)),
                        pltpu.SemaphoreType.DMA((4,))],
        compiler_params=pltpu.CompilerParams(vmem_limit_bytes=VMEM_LIMIT),
    )(w_in, w_out, w1, w2)


def _inproj_fwd(x_all, modl, g1, w_in, n_ctx):
    B, T, D = x_all.shape
    nw = w_in.shape[1]
    tm = _div_tile(n_ctx, 256, 16)
    nct = n_ctx // tm

    def body(x_ref, sh_ref, sc_ref, g_ref, w_ref, h_ref, p_ref):
        x = x_ref[...]
        r = lax.rsqrt(jnp.mean(x * x, axis=-1, keepdims=True) + NORM_EPS)
        h = ((x * r) * g_ref[...]) * (1.0 + sc_ref[...]) + sh_ref[...]
        hb = h.astype(BF16)
        h_ref[...] = hb
        p_ref[...] = _dot(hb, w_ref[...])

    def mrow(b, t):
        return jnp.where(t < nct, B, b)

    return pl.pallas_call(
        body, name="inproj_fwd", grid=(B, T // tm),
        out_shape=(jax.ShapeDtypeStruct((B, T, D), BF16), jax.ShapeDtypeStruct((B, T, nw), F32)),
        in_specs=[pl.BlockSpec((None, tm, D), lambda b, t: (b, t, 0)),
                  pl.BlockSpec((None, None, 1, D), lambda b, t: (mrow(b, t), 0, 0, 0)),
                  pl.BlockSpec((None, None, 1, D), lambda b, t: (mrow(b, t), 1, 0, 0)),
                  pl.BlockSpec((1, D), lambda b, t: (0, 0)),
                  pl.BlockSpec((D, nw), lambda b, t: (0, 0))],
        out_specs=(pl.BlockSpec((None, tm, D), lambda b, t: (b, t, 0)),
                   pl.BlockSpec((None, tm, nw), lambda b, t: (b, t, 0))),
        compiler_params=_params("parallel", "arbitrary"),
    )(x_all, modl, modl, g1, w_in)


def _swap32(x):
    lane = lax.broadcasted_iota(jnp.int32, x.shape, 1)
    return jnp.where((lane % 64) < 32, pltpu.roll(x, 96, 1), pltpu.roll(x, 32, 1))


def _rope(x, cos, sin):
    return x * cos + _swap32(x) * sin


def _unrope(dy, cos, sin):
    return dy * cos + _swap32(dy * sin)


def _ret_weights(lgf, lgb, dist):
    return jnp.exp(jnp.where(dist >= 0.0, lgf * dist, -lgb * dist))


def _ret_fwd(proj, cos, sin, lg, gn, n_ctx):
    B, T, _ = proj.shape
    C = n_ctx
    N = T - C
    tq = _div_tile(N, 256, 16)
    tk = tq
    scale = RET_DIM ** -0.5

    def body(lg_ref, q_ref, k_ref, v_ref, g_ref, cos_ref, sin_ref, gn_ref, o_ref, lat_ref, qs, ks, vs):
        h = pl.program_id(1)
        lgf = lg_ref[0, h]
        lgb = lg_ref[1, h]
        cosv = cos_ref[...]
        sinv = sin_ref[...]
        qs[...] = (_rope(q_ref[...], cosv, sinv) * scale).astype(BF16)
        ks[...] = _rope(k_ref[...], cosv, sinv).astype(BF16)
        vs[...] = v_ref[...].astype(BF16)
        gnv = gn_ref[...]
        rc = (lax.broadcasted_iota(jnp.int32, (tq, tk), 0) - lax.broadcasted_iota(jnp.int32, (tq, tk), 1)).astype(F32)
        ri = lax.broadcasted_iota(jnp.int32, (tq, C), 0).astype(F32)
        ti = lax.broadcasted_iota(jnp.int32, (tq, C), 1).astype(F32)

        def q_tile(qi, carry):
            i0 = pl.multiple_of(qi * tq, tq)
            i0f = (qi * tq).astype(F32)
            qt = qs[pl.ds(C + i0, tq), :]
            s = _dot_nt(qt, ks[0:C, :])
            w = jnp.exp(lgf * (ri + (i0f + C) - ti)) + jnp.exp(lgb * ((N - i0f) - ri + ti))
            acc = _dot((s * w).astype(BF16), vs[0:C, :])

            def k_tile(kj, acc):
                j0 = pl.multiple_of(kj * tk, tk)
                kt = ks[pl.ds(C + j0, tk), :]
                s = _dot_nt(qt, kt)
                w = _ret_weights(lgf, lgb, rc + (i0f - (kj * tk).astype(F32)))
                return acc + _dot((s * w).astype(BF16), vs[pl.ds(C + j0, tk), :])

            o = lax.fori_loop(0, N // tk, k_tile, acc, unroll=True)
            o_ref[pl.ds(i0, tq), :] = o
            mu = jnp.mean(o, axis=-1, keepdims=True)
            oc = o - mu
            var = jnp.mean(oc * oc, axis=-1, keepdims=True)
            yh = oc * lax.rsqrt(var + NORM_EPS)
            g = g_ref[pl.ds(C + i0, tq), :]
            lat_ref[pl.ds(i0, tq), :] = ((yh * gnv) * (g * _sigmoid(g))).astype(BF16)
            return carry

        lax.fori_loop(0, N // tq, q_tile, 0)

    def col(seg):
        return pl.BlockSpec((None, T, RET_DIM), lambda b, h, seg=seg: (b, 0, seg * RET_HEADS + h))

    return pl.pallas_call(
        body, name="ret_fwd", grid=(B, RET_HEADS),
        out_shape=(jax.ShapeDtypeStruct((B, N, RET_WIDTH), F32), jax.ShapeDtypeStruct((B, N, RET_WIDTH), BF16)),
        in_specs=[pl.BlockSpec(memory_space=pltpu.SMEM), col(0), col(1), col(2), col(3),
                  pl.BlockSpec((T, RET_DIM), lambda b, h: (0, 0)), pl.BlockSpec((T, RET_DIM), lambda b, h: (0, 0)),
                  pl.BlockSpec((1, RET_DIM), lambda b, h: (0, h))],
        out_specs=(pl.BlockSpec((None, N, RET_DIM), lambda b, h: (b, 0, h)),
                   pl.BlockSpec((None, N, RET_DIM), lambda b, h: (b, 0, h))),
        scratch_shapes=[pltpu.VMEM((T, RET_DIM), BF16)] * 3,
        compiler_params=_params("parallel", "arbitrary"),
    )(lg, proj, proj, proj, proj, cos, sin, gn)


def _ret_bwd(proj, cos, sin, lg, gn, o, dlat, n_ctx):
    B, T, _ = proj.shape
    C = n_ctx
    N = T - C
    tq = _div_tile(N, 256, 16)
    tk = tq
    scale = RET_DIM ** -0.5

    def body(lg_ref, q_ref, k_ref, v_ref, g_ref, cos_ref, sin_ref, gn_ref, o_ref, dl_ref,
             d_ref, dgn_ref, dlg_ref, qs, ks, vs, dos):
        h = pl.program_id(1)
        lgf = lg_ref[0, h]
        lgb = lg_ref[1, h]
        cosv = cos_ref[...]
        sinv = sin_ref[...]
        qs[...] = (_rope(q_ref[...], cosv, sinv) * scale).astype(BF16)
        ks[...] = _rope(k_ref[...], cosv, sinv).astype(BF16)
        vs[...] = v_ref[...].astype(BF16)
        gnv = gn_ref[...]

        ov = o_ref[...]
        mu = jnp.mean(ov, axis=-1, keepdims=True)
        oc = ov - mu
        var = jnp.mean(oc * oc, axis=-1, keepdims=True)
        rstd = lax.rsqrt(var + NORM_EPS)
        yh = oc * rstd
        g = g_ref[C:, :]
        sg = _sigmoid(g)
        silu = g * sg
        dl = dl_ref[...]
        d_ref[3, 0:C, :] = jnp.zeros((C, RET_DIM), F32)
        d_ref[3, C:, :] = dl * (yh * gnv) * (sg * (1.0 + g * (1.0 - sg)))
        dls = dl * silu
        dgn = jnp.sum(dls * yh, axis=0, keepdims=True)
        dgn_ref[...] = jnp.concatenate([dgn, jnp.zeros((SUBLANES - 1, RET_DIM), F32)], axis=0)
        dyh = dls * gnv
        do = rstd * (dyh - jnp.mean(dyh, axis=-1, keepdims=True) - yh * jnp.mean(dyh * yh, axis=-1, keepdims=True))
        dos[...] = do.astype(BF16)

        d_ref[0, 0:C, :] = jnp.zeros((C, RET_DIM), F32)
        d_ref[1] = jnp.zeros((T, RET_DIM), F32)
        d_ref[2] = jnp.zeros((T, RET_DIM), F32)

        rc = (lax.broadcasted_iota(jnp.int32, (tq, tk), 0) - lax.broadcasted_iota(jnp.int32, (tq, tk), 1)).astype(F32)
        ri = lax.broadcasted_iota(jnp.int32, (tq, C), 0).astype(F32)
        ti = lax.broadcasted_iota(jnp.int32, (tq, C), 1).astype(F32)

        def fold(a):
            return jnp.sum(a.reshape(a.shape[0] // SUBLANES, SUBLANES, a.shape[1]), axis=0)

        def q_tile(qi, carry):
            gf, gb = carry
            i0 = pl.multiple_of(qi * tq, tq)
            i0f = (qi * tq).astype(F32)
            qt = qs[pl.ds(C + i0, tq), :]
            dot = dos[pl.ds(i0, tq), :]
            kc = ks[0:C, :]
            vc = vs[0:C, :]
            s = _dot_nt(qt, kc)
            dp = _dot_nt(dot, vc)
            ef = ri + (i0f + C) - ti
            eb = (N - i0f) - ri + ti
            wf = jnp.exp(lgf * ef)
            wb = jnp.exp(lgb * eb)
            w = wf + wb
            d_ref[2, 0:C, :] += _dot_tn((s * w).astype(BF16), dot)
            ds = (dp * w).astype(BF16)
            dq = _dot(ds, kc)
            d_ref[1, 0:C, :] += _dot_tn(ds, qt)
            gs = dp * s
            gfc = fold(gs * wf * ef)
            gbc = fold(gs * wb * eb)

            def k_tile(kj, carry):
                dq, gf, gb = carry
                j0 = pl.multiple_of(kj * tk, tk)
                rows = pl.ds(C + j0, tk)
                kt = ks[rows, :]
                vt = vs[rows, :]
                s = _dot_nt(qt, kt)
                dp = _dot_nt(dot, vt)
                dist = rc + (i0f - (kj * tk).astype(F32))
                w = _ret_weights(lgf, lgb, dist)
                d_ref[2, rows, :] += _dot_tn((s * w).astype(BF16), dot)
                ds = (dp * w).astype(BF16)
                dq = dq + _dot(ds, kt)
                d_ref[1, rows, :] += _dot_tn(ds, qt)
                gw = dp * s * w * dist
                gf = gf + fold(jnp.where(dist >= 0.0, gw, 0.0))
                gb = gb - fold(jnp.where(dist < 0.0, gw, 0.0))
                return dq, gf, gb

            dq, gf2, gb2 = lax.fori_loop(0, N // tk, k_tile, (dq, jnp.zeros((SUBLANES, tk), F32),
                                                            jnp.zeros((SUBLANES, tk), F32)), unroll=True)
            qrows = pl.ds(C + i0, tq)
            d_ref[0, qrows, :] = _unrope(dq * scale, cos_ref[qrows, :], sin_ref[qrows, :])
            return gf + jnp.sum(gf2) + jnp.sum(gfc), gb + jnp.sum(gb2) + jnp.sum(gbc)

        gf, gb = lax.fori_loop(0, N // tq, q_tile, (jnp.zeros((), F32), jnp.zeros((), F32)))
        d_ref[1] = _unrope(d_ref[1], cosv, sinv)
        row = lax.broadcasted_iota(jnp.int32, (SUBLANES, LANES), 0)
        dlg_ref[...] = jnp.where(row == 0, gf, jnp.where(row == 1, gb, 0.0))

    def col(seg):
        return pl.BlockSpec((None, T, RET_DIM), lambda b, h, seg=seg: (b, 0, seg * RET_HEADS + h))

    return pl.pallas_call(
        body, name="ret_bwd", grid=(B, RET_HEADS),
        out_shape=(jax.ShapeDtypeStruct((B, 4, T, RET_WIDTH), F32),
                   jax.ShapeDtypeStruct((B, SUBLANES, RET_WIDTH), F32),
                   jax.ShapeDtypeStruct((B, RET_HEADS, SUBLANES, LANES), F32)),
        in_specs=[pl.BlockSpec(memory_space=pltpu.SMEM), col(0), col(1), col(2), col(3),
                  pl.BlockSpec((T, RET_DIM), lambda b, h: (0, 0)), pl.BlockSpec((T, RET_DIM), lambda b, h: (0, 0)),
                  pl.BlockSpec((1, RET_DIM), lambda b, h: (0, h)),
                  pl.BlockSpec((None, N, RET_DIM), lambda b, h: (b, 0, h)),
                  pl.BlockSpec((None, N, RET_DIM), lambda b, h: (b, 0, h))],
        out_specs=(pl.BlockSpec((None, 4, T, RET_DIM), lambda b, h: (b, 0, 0, h)),
                   pl.BlockSpec((None, SUBLANES, RET_DIM), lambda b, h: (b, 0, h)),
                   pl.BlockSpec((None, None, SUBLANES, LANES), lambda b, h: (b, h, 0, 0))),
        scratch_shapes=[pltpu.VMEM((T, RET_DIM), BF16)] * 3 + [pltpu.VMEM((N, RET_DIM), BF16)],
        compiler_params=_params("parallel", "arbitrary"),
    )(lg, proj, proj, proj, proj, cos, sin, gn, o, dlat)


def _na_geometry(rows):
    kh = min(NA_KH, rows)
    return kh, kh * GRID_W


def _pair_select():
    lane = lax.broadcasted_iota(jnp.int32, (2 * GRID_W, LANES), 1)
    row = lax.broadcasted_iota(jnp.int32, (2 * GRID_W, LANES), 0)
    return (lane >= NA_DIM) == (row >= GRID_W)


def _pair_bias(bias_ref, dr0, kh):
    return jnp.concatenate(
        [jnp.concatenate([bias_ref[e, pl.ds(dr0 + 2 * m, 1)].reshape(GRID_W, LANES) for m in range(kh // 2)], axis=1)
         for e in range(2)], axis=0)


def _na_softmax(s_loc, s_ctx):
    mx = jnp.maximum(jnp.max(s_loc, axis=-1, keepdims=True), jnp.max(s_ctx, axis=-1, keepdims=True))
    p_loc = jnp.exp(s_loc - mx)
    p_ctx = jnp.exp(s_ctx - mx)
    den = jnp.sum(p_loc, axis=-1, keepdims=True) + jnp.sum(p_ctx, axis=-1, keepdims=True)
    return p_loc, p_ctx, den


def _na_fwd(proj, bias2, n_ctx):
    B, T, _ = proj.shape
    C = n_ctx
    N = T - C
    R = N // GRID_W
    kh, nk = _na_geometry(R)
    scale = NA_DIM ** -0.5
    base = (4 * RET_WIDTH) // LANES

    def body(q_ref, k_ref, v_ref, bias_ref, out_ref, kb16, vb16):
        kb16[...] = k_ref[...].astype(BF16)
        vb16[...] = v_ref[...].astype(BF16)
        kc = kb16[0:C, :]
        vc = vb16[0:C, :]
        lane = lax.broadcasted_iota(jnp.int32, (GRID_W, LANES), 1)
        sel2 = _pair_select()

        def group(gi, carry):
            pre = []
            for u in range(NA_GROUP):
                r = gi * NA_GROUP + u
                bs = jnp.clip(r - kh // 2, 0, R - kh)
                dr0 = bs - r + (NA_KH - 1)
                q = q_ref[pl.ds(pl.multiple_of(C + r * GRID_W, GRID_W), GRID_W), :] * scale
                q2 = jnp.where(sel2, jnp.concatenate([q, q], axis=0), 0.0).astype(BF16)
                band = pl.ds(pl.multiple_of(C + bs * GRID_W, GRID_W), nk)
                s_loc = _dot_nt(q2, kb16[band, :]) + _pair_bias(bias_ref, dr0, kh)
                s_ctx = _dot_nt(q2, kc)
                pre.append((r, band, s_loc, s_ctx))
            mid = [(r, band) + _na_softmax(s_loc, s_ctx) for r, band, s_loc, s_ctx in pre]
            for r, band, p_loc, p_ctx, den in mid:
                o2 = (_dot(p_loc.astype(BF16), vb16[band, :]) + _dot(p_ctx.astype(BF16), vc)) / den
                out_ref[pl.ds(pl.multiple_of(r * GRID_W, GRID_W), GRID_W), :] = jnp.where(
                    lane < NA_DIM, o2[:GRID_W], o2[GRID_W:]).astype(BF16)
            return carry

        lax.fori_loop(0, R // NA_GROUP, group, 0)

    def col(seg):
        return pl.BlockSpec((None, T, LANES), lambda b, p, seg=seg: (b, 0, base + seg * NA_PAIRS + p))

    return pl.pallas_call(
        body, name="na_fwd", grid=(B, NA_PAIRS),
        out_shape=jax.ShapeDtypeStruct((B, N, NA_WIDTH), BF16),
        in_specs=[col(0), col(1), col(2),
                  pl.BlockSpec((2, 2 * NA_KH - 2, GRID_W, LANES), lambda b, p: (p, 0, 0, 0))],
        out_specs=pl.BlockSpec((None, N, LANES), lambda b, p: (b, 0, p)),
        scratch_shapes=[pltpu.VMEM((T, LANES), BF16)] * 2,
        compiler_params=_params("parallel", "arbitrary"),
    )(proj, proj, proj, bias2)


def _na_bwd(proj, bias2, dlat, n_ctx):
    B, T, _ = proj.shape
    C = n_ctx
    N = T - C
    R = N // GRID_W
    kh, nk = _na_geometry(R)
    scale = NA_DIM ** -0.5
    base = (4 * RET_WIDTH) // LANES

    def body(q_ref, k_ref, v_ref, bias_ref, dl_ref, d_ref, db_ref, kb16, vb16):
        b = pl.program_id(1)
        kb16[...] = k_ref[...].astype(BF16)
        vb16[...] = v_ref[...].astype(BF16)
        kc = kb16[0:C, :]
        vc = vb16[0:C, :]
        lane = lax.broadcasted_iota(jnp.int32, (GRID_W, LANES), 1)
        d_ref[...] = jnp.zeros(d_ref.shape, F32)

        @pl.when(b == 0)
        def _():
            db_ref[...] = jnp.zeros(db_ref.shape, F32)

        sel2 = _pair_select()

        def group(gi, carry):
            pre = []
            for u in range(NA_GROUP):
                r = gi * NA_GROUP + u
                bs = jnp.clip(r - kh // 2, 0, R - kh)
                dr0 = bs - r + (NA_KH - 1)
                q = q_ref[pl.ds(pl.multiple_of(C + r * GRID_W, GRID_W), GRID_W), :] * scale
                do = dl_ref[pl.ds(pl.multiple_of(r * GRID_W, GRID_W), GRID_W), :]
                q2 = jnp.where(sel2, jnp.concatenate([q, q], axis=0), 0.0).astype(BF16)
                do2 = jnp.where(sel2, jnp.concatenate([do, do], axis=0), 0.0).astype(BF16)
                band = pl.ds(pl.multiple_of(C + bs * GRID_W, GRID_W), nk)
                s_loc = _dot_nt(q2, kb16[band, :]) + _pair_bias(bias_ref, dr0, kh)
                s_ctx = _dot_nt(q2, kc)
                dp_loc = _dot_nt(do2, vb16[band, :])
                dp_ctx = _dot_nt(do2, vc)
                pre.append((r, dr0, band, q2, do2, s_loc, s_ctx, dp_loc, dp_ctx))
            mid = []
            for r, dr0, band, q2, do2, s_loc, s_ctx, dp_loc, dp_ctx in pre:
                p_loc, p_ctx, den = _na_softmax(s_loc, s_ctx)
                inv = 1.0 / den
                p_loc = p_loc * inv
                p_ctx = p_ctx * inv
                delta = (jnp.sum(p_loc * dp_loc, axis=-1, keepdims=True)
                         + jnp.sum(p_ctx * dp_ctx, axis=-1, keepdims=True))
                ds_loc = p_loc * (dp_loc - delta)
                ds_ctx = p_ctx * (dp_ctx - delta)
                mid.append((r, dr0, band, q2, do2, p_loc.astype(BF16), p_ctx.astype(BF16), ds_loc, ds_ctx))
            for r, dr0, band, q2, do2, pb_loc, pb_ctx, ds_loc, ds_ctx in mid:
                dsb_loc = ds_loc.astype(BF16)
                dsb_ctx = ds_ctx.astype(BF16)
                dq2 = _dot(dsb_loc, kb16[band, :]) + _dot(dsb_ctx, kc)
                d_ref[0, pl.ds(pl.multiple_of(C + r * GRID_W, GRID_W), GRID_W), :] = jnp.where(
                    lane < NA_DIM, dq2[:GRID_W], dq2[GRID_W:]) * scale
                d_ref[1, band, :] += _dot_tn(dsb_loc, q2)
                d_ref[2, band, :] += _dot_tn(pb_loc, do2)
                d_ref[1, 0:C, :] += _dot_tn(dsb_ctx, q2)
                d_ref[2, 0:C, :] += _dot_tn(pb_ctx, do2)
                for e in range(2):
                    for m in range(kh // 2):
                        db_ref[e, pl.ds(dr0 + 2 * m, 1)] += ds_loc[e * GRID_W:(e + 1) * GRID_W,
                                                                   m * LANES:(m + 1) * LANES].reshape(1, GRID_W, LANES)
            return carry

        lax.fori_loop(0, R // NA_GROUP, group, 0)

    def col(seg):
        return pl.BlockSpec((None, T, LANES), lambda p, b, seg=seg: (b, 0, base + seg * NA_PAIRS + p))

    return pl.pallas_call(
        body, name="na_bwd", grid=(NA_PAIRS, B),
        out_shape=(jax.ShapeDtypeStruct((B, 3, T, NA_WIDTH), F32),
                   jax.ShapeDtypeStruct((NA_HEADS, 2 * NA_KH - 2, GRID_W, LANES), F32)),
        in_specs=[col(0), col(1), col(2),
                  pl.BlockSpec((2, 2 * NA_KH - 2, GRID_W, LANES), lambda p, b: (p, 0, 0, 0)),
                  pl.BlockSpec((None, N, LANES), lambda p, b: (b, 0, p))],
        out_specs=(pl.BlockSpec((None, 3, T, LANES), lambda p, b: (b, 0, 0, p)),
                   pl.BlockSpec((2, 2 * NA_KH - 2, GRID_W, LANES), lambda p, b: (p, 0, 0, 0))),
        scratch_shapes=[pltpu.VMEM((T, LANES), BF16)] * 2,
        compiler_params=_params("parallel", "arbitrary"),
    )(proj, proj, proj, bias2, dlat)


def _split3(a):
    hi = a.astype(BF16)
    r1 = a - hi.astype(F32)
    mid = r1.astype(BF16)
    lo = (r1 - mid.astype(F32)).astype(BF16)
    return hi, mid, lo


def _rpb_reduce(dbias2, onehot2):
    rows = dbias2.shape[0] * dbias2.shape[1]
    flat = dbias2.reshape(rows, GRID_W * LANES)

    def body(a_ref, oh_ref, o_ref):
        hi, mid, lo = _split3(a_ref[...])
        oh = oh_ref[...]
        o_ref[...] = _dot(hi, oh) + _dot(mid, oh) + _dot(lo, oh)

    return pl.pallas_call(
        body, name="rpb_reduce", out_shape=jax.ShapeDtypeStruct((rows, LANES), F32),
        in_specs=[_vmem(), _vmem()], out_specs=_vmem(),
        compiler_params=pltpu.CompilerParams(vmem_limit_bytes=VMEM_LIMIT),
    )(flat, onehot2)


def _dense_core(lat_ret, lat_na, x, tgt, modl, g_post_mix, g_pre_mlp, g_post_mlp, w_out, w1, w2):
    B, N, D = x.shape
    F = w1.shape[1]
    mixw = w_out.shape[0]
    half = mixw // 2
    tm = _div_tile(N, 256, 16)
    nt = N // tm
    fc = _div_tile(F, 1024, LANES)

    def body(lr_ref, ln_ref, x_ref, t_ref, gt1_ref, sh2_ref, sc2_ref, gt2_ref, gpm_ref, gpre_ref, gpo_ref,
             wout_hbm, w1_hbm, w2_hbm,
             dy1_ref, dlr_ref, dln_ref, dmix_ref, h2_ref, a_ref, du_ref, dz_ref, red_ref,
             wout_v, w1_v, w2_v, u_s, sems):
        @pl.when((pl.program_id(0) == 0) & (pl.program_id(1) == 0))
        def _():
            cps = [pltpu.make_async_copy(wout_hbm, wout_v, sems.at[0]),
                   pltpu.make_async_copy(w1_hbm, w1_v, sems.at[1]),
                   pltpu.make_async_copy(w2_hbm, w2_v, sems.at[2])]
            for cp in cps:
                cp.start()
            for cp in cps:
                cp.wait()

        gt1 = gt1_ref[...]
        sh2 = sh2_ref[...]
        sc2 = sc2_ref[...]
        gt2 = gt2_ref[...]
        gpm = gpm_ref[...]
        gpre = gpre_ref[...]
        gpo = gpo_ref[...]

        def rowmean(a):
            return jnp.mean(a, axis=-1, keepdims=True)

        def colsum(a):
            return jnp.sum(a, axis=0, keepdims=True)

        mix = _dot(lr_ref[...], wout_v[0:half, :]) + _dot(ln_ref[...], wout_v[half:, :])
        x = x_ref[...]
        rm = lax.rsqrt(rowmean(mix * mix) + NORM_EPS)
        mh = mix * rm
        nm = mh * gpm
        y1 = x + gt1 * nm
        r1 = lax.rsqrt(rowmean(y1 * y1) + NORM_EPS)
        xh = y1 * r1
        n1 = xh * gpre
        h2b = (n1 * (1.0 + sc2) + sh2).astype(BF16)
        h2_ref[...] = h2b
        z = jnp.zeros((tm, D), F32)
        for c0 in range(0, F, fc):
            u = _dot(h2b, w1_v[:, c0:c0 + fc])
            u_s[:, c0:c0 + fc] = u
            ru = jnp.maximum(u, 0.0)
            ab = (ru * ru).astype(BF16)
            a_ref[:, c0:c0 + fc] = ab
            z = z + _dot(ab, w2_v[c0:c0 + fc, :])
        r2 = lax.rsqrt(rowmean(z * z) + NORM_EPS)
        zh = z * r2
        n2 = zh * gpo
        y2 = y1 + gt2 * n2
        err = y2 - t_ref[...]
        loss = 0.5 * jnp.sum(rowmean(err * err))
        dy2 = err * (1.0 / D)
        red_ref[2:3, :] = colsum(dy2 * n2)
        dn2 = dy2 * gt2
        red_ref[6:7, :] = colsum(dn2 * zh)
        dzh = dn2 * gpo
        dz = r2 * (dzh - zh * rowmean(dzh * zh))
        dzb = dz.astype(BF16)
        dz_ref[...] = dzb
        dh2 = jnp.zeros((tm, D), F32)
        for c0 in range(0, F, fc):
            da = _dot_nt(dzb, w2_v[c0:c0 + fc, :])
            dub = (da * (2.0 * jnp.maximum(u_s[:, c0:c0 + fc], 0.0))).astype(BF16)
            du_ref[:, c0:c0 + fc] = dub
            dh2 = dh2 + _dot_nt(dub, w1_v[:, c0:c0 + fc])
        red_ref[3:4, :] = colsum(dh2 * n1)
        red_ref[4:5, :] = colsum(dh2)
        dn1 = dh2 * (1.0 + sc2)
        red_ref[5:6, :] = colsum(dn1 * xh)
        dxh = dn1 * gpre
        dy1 = dy2 + r1 * (dxh - xh * rowmean(dxh * xh))
        dy1_ref[...] = dy1
        red_ref[0:1, :] = colsum(dy1 * nm)
        dnm = dy1 * gt1
        red_ref[1:2, :] = colsum(dnm * mh)
        dmh = dnm * gpm
        dmix = (rm * (dmh - mh * rowmean(dmh * mh))).astype(BF16)
        dmix_ref[...] = dmix
        dlr_ref[...] = _dot_nt(dmix, wout_v[0:half, :])
        dln_ref[...] = _dot_nt(dmix, wout_v[half:, :])
        red_ref[7:8, :] = jnp.zeros((1, D), F32) + loss

    def tok(w):
        return pl.BlockSpec((None, tm, w), lambda b, t: (b, t, 0))

    def mod(k):
        return pl.BlockSpec((None, None, 1, D), lambda b, t, k=k: (b, k, 0, 0))

    def vec():
        return pl.BlockSpec((1, D), lambda b, t: (0, 0))

    return pl.pallas_call(
        body, name="dense_core", grid=(B, nt),
        out_shape=(jax.ShapeDtypeStruct((B, N, D), F32), jax.ShapeDtypeStruct((B, N, half), F32),
                   jax.ShapeDtypeStruct((B, N, half), F32), jax.ShapeDtypeStruct((B, N, D), BF16),
                   jax.ShapeDtypeStruct((B, N, D), BF16), jax.ShapeDtypeStruct((B, N, F), BF16),
                   jax.ShapeDtypeStruct((B, N, F), BF16), jax.ShapeDtypeStruct((B, N, D), BF16),
                   jax.ShapeDtypeStruct((B, nt, SUBLANES, D), F32)),
        in_specs=[tok(half), tok(half), tok(D), tok(D), mod(2), mod(3), mod(4), mod(5), vec(), vec(), vec(),
                  _any(), _any(), _any()],
        out_specs=(tok(D), tok(half), tok(half), tok(D), tok(D), tok(F), tok(F), tok(D),
                   pl.BlockSpec((None, None, SUBLANES, D), lambda b, t: (b, t, 0, 0))),
        scratch_shapes=[pltpu.VMEM((mixw, D), BF16), pltpu.VMEM((D, F), BF16), pltpu.VMEM((F, D), BF16),
                        pltpu.VMEM((tm, F), F32), pltpu.SemaphoreType.DMA((3,))],
        compiler_params=_params("arbitrary", "arbitrary"),
    )(lat_ret, lat_na, x, tgt, modl, modl, modl, modl, g_post_mix, g_pre_mlp, g_post_mlp, w_out, w1, w2)


def _inproj_bwd(dret, dna, x_all, dy1, modl, g1, w_in, n_ctx):
    B, T, D = x_all.shape
    N = T - n_ctx
    tm = _div_tile(n_ctx, 256, 16)
    nct = n_ctx // tm
    nt = T // tm
    nseg_r = dret.shape[1]
    nseg_n = dna.shape[1]
    nw = w_in.shape[1]

    def body(*refs):
        seg_refs = refs[:nseg_r + nseg_n]
        x_ref, dy1_ref, sc_ref, g_ref, w_ref, dx_ref, red_ref = refs[nseg_r + nseg_n:]
        t = pl.program_id(1)
        dh = jnp.zeros((tm, D), F32)
        for s, ref in enumerate(seg_refs):
            dh = dh + _dot_nt(ref[...].astype(BF16), w_ref[:, s * SEG:(s + 1) * SEG])
        x = x_ref[...]
        g = g_ref[...]
        r = lax.rsqrt(jnp.mean(x * x, axis=-1, keepdims=True) + NORM_EPS)
        xh = x * r
        red_ref[0:1, :] = jnp.sum(dh, axis=0, keepdims=True)
        red_ref[1:2, :] = jnp.sum(dh * (xh * g), axis=0, keepdims=True)
        dn = dh * (1.0 + sc_ref[...])
        red_ref[2:3, :] = jnp.sum(dn * xh, axis=0, keepdims=True)
        red_ref[3:, :] = jnp.zeros((SUBLANES - 3, D), F32)
        dxh = dn * g
        dx = r * (dxh - xh * jnp.mean(dxh * xh, axis=-1, keepdims=True))
        dx_ref[...] = dx + jnp.where(t >= nct, dy1_ref[...], 0.0)

    def mrow(b, t):
        return jnp.where(t < nct, B, b)

    def seg(s):
        return pl.BlockSpec((None, None, tm, SEG), lambda b, t, s=s: (b, s, t, 0))

    def lat_tile():
        return pl.BlockSpec((None, tm, D), lambda b, t: (b, jnp.maximum(t - nct, 0), 0))

    return pl.pallas_call(
        body, name="inproj_bwd", grid=(B, nt),
        out_shape=(jax.ShapeDtypeStruct((B, N, D), F32), jax.ShapeDtypeStruct((B, nt, SUBLANES, D), F32)),
        in_specs=[seg(s) for s in range(nseg_r)] + [seg(s) for s in range(nseg_n)]
                 + [pl.BlockSpec((None, tm, D), lambda b, t: (b, t, 0)), lat_tile(),
                    pl.BlockSpec((None, None, 1, D), lambda b, t: (mrow(b, t), 1, 0, 0)),
                    pl.BlockSpec((1, D), lambda b, t: (0, 0)),
                    pl.BlockSpec((D, nw), lambda b, t: (0, 0))],
        out_specs=(lat_tile(), pl.BlockSpec((None, None, SUBLANES, D), lambda b, t: (b, t, 0, 0))),
        compiler_params=_params("arbitrary", "arbitrary"),
    )(*([dret] * nseg_r), *([dna] * nseg_n), x_all, dy1, modl, g1, w_in)


def _tn_matmul(lhs, rhs, name):
    B, S, T, W = lhs.shape
    nn = rhs.shape[-1]
    tk = _div_tile(T, 1024, LANES)
    bm = _div_tile(W, 1024, LANES)
    bn = _div_tile(nn, 1024, LANES)
    nkt = T // tk
    nk = B * nkt

    def body(l_ref, r_ref, o_ref, acc):
        k = pl.program_id(3)

        @pl.when(k == 0)
        def _():
            acc[...] = jnp.zeros(acc.shape, F32)

        acc[...] += _dot_tn(l_ref[...].astype(BF16), r_ref[...].astype(BF16))

        @pl.when(k == nk - 1)
        def _():
            o_ref[...] = acc[...].astype(BF16)

    nwb = W // bm
    return pl.pallas_call(
        functools.partial(body), name=name, grid=(S, nwb, nn // bn, nk),
        out_shape=jax.ShapeDtypeStruct((S * W, nn), BF16),
        in_specs=[pl.BlockSpec((None, None, tk, bm), lambda s, i, j, k: (k // nkt, s, k % nkt, i)),
                  pl.BlockSpec((None, tk, bn), lambda s, i, j, k: (k // nkt, k % nkt, j))],
        out_specs=pl.BlockSpec((bm, bn), lambda s, i, j, k: (s * nwb + i, j)),
        scratch_shapes=[pltpu.VMEM((bm, bn), F32)],
        compiler_params=_params("parallel", "parallel", "parallel", "arbitrary"),
    )(lhs, rhs)


def _grad_rs(gin_t, gout, g1, g2):
    rin = gin_t.shape[0] // N_DEV
    rout = gout.shape[0] // N_DEV
    c1 = g1.shape[1] // N_DEV
    r2 = g2.shape[0] // N_DEV

    def body(gin_ref, gout_ref, g1_ref, g2_ref, bin_ref, bout_ref, b1_ref, b2_ref, send_sems, recv_sems, loc_sems):
        me, peers = _me_and_peers()

        def s_in(p):
            return gin_ref.at[pl.ds(pl.multiple_of(p * rin, 2 * SUBLANES), rin), :]

        def s_out(p):
            return gout_ref.at[pl.ds(pl.multiple_of(p * rout, 2 * SUBLANES), rout), :]

        def s_1(p):
            return g1_ref.at[:, pl.ds(pl.multiple_of(p * c1, c1), c1)]

        def s_2(p):
            return g2_ref.at[pl.ds(pl.multiple_of(p * r2, 2 * SUBLANES), r2), :]

        srcs = (s_in, s_out, s_1, s_2)
        bufs = (bin_ref, bout_ref, b1_ref, b2_ref)
        local = [pltpu.make_async_copy(s(me), b.at[me], loc_sems.at[k]) for k, (s, b) in enumerate(zip(srcs, bufs))]
        for cp in local:
            cp.start()
        sent = []
        for k, (s, b) in enumerate(zip(srcs, bufs)):
            for i, (dev, pid) in enumerate(peers):
                cp = pltpu.make_async_remote_copy(src_ref=s(pid), dst_ref=b.at[me], send_sem=send_sems.at[k, i],
                                                  recv_sem=recv_sems.at[k, i], device_id=dev, device_id_type=MESH)
                cp.start()
                sent.append(cp)
        for k, (s, b) in enumerate(zip(srcs, bufs)):
            for i, (dev, pid) in enumerate(peers):
                pltpu.make_async_remote_copy(src_ref=s(pid), dst_ref=b.at[pid], send_sem=send_sems.at[k, i],
                                             recv_sem=recv_sems.at[k, i], device_id=dev,
                                             device_id_type=MESH).wait_recv()
        for cp in sent:
            cp.wait_send()
        for cp in local:
            cp.wait()

    return pl.pallas_call(
        body, name="grad_rs",
        out_shape=(jax.ShapeDtypeStruct((N_DEV, rin, gin_t.shape[1]), gin_t.dtype),
                   jax.ShapeDtypeStruct((N_DEV, rout, gout.shape[1]), gout.dtype),
                   jax.ShapeDtypeStruct((N_DEV, g1.shape[0], c1), g1.dtype),
                   jax.ShapeDtypeStruct((N_DEV, r2, g2.shape[1]), g2.dtype)),
        in_specs=[_any()] * 4, out_specs=(_any(),) * 4,
        scratch_shapes=[pltpu.SemaphoreType.DMA((4, N_DEV - 1)), pltpu.SemaphoreType.DMA((4, N_DEV - 1)),
                        pltpu.SemaphoreType.DMA((4,))],
        compiler_params=pltpu.CompilerParams(vmem_limit_bytes=VMEM_LIMIT),
    )(gin_t, gout, g1, g2)


def _sum_slots(buf, name):
    _, rows, cols = buf.shape
    tr = _div_tile(rows, 256, 2 * SUBLANES)

    def body(b_ref, o_ref):
        acc = b_ref[0].astype(F32)
        for k in range(1, N_DEV):
            acc = acc + b_ref[k].astype(F32)
        o_ref[...] = acc

    return pl.pallas_call(
        functools.partial(body), name=name, grid=(rows // tr,),
        out_shape=jax.ShapeDtypeStruct((rows, cols), F32),
        in_specs=[pl.BlockSpec((N_DEV, tr, cols), lambda i: (0, i, 0))],
        out_specs=pl.BlockSpec((tr, cols), lambda i: (i, 0)),
        compiler_params=_params("parallel"),
    )(buf)


def _small_ar(vec, dmods, silu_all, w_ada, c_ctx):
    rv = vec.shape[0]
    D = silu_all.shape[1]
    ncol = w_ada.shape[1]
    nm = dmods.shape[1]
    srows = silu_all.shape[0]

    def body(vec_ref, dm_ref, s_ref, w_ref, cc_ref, tot_ref, gb_ref, gw_ref, gc_ref,
             vbuf, mbuf, tbuf, dmx, send1, recv1, send2, recv2, send3, recv3):
        me, _ = _me_and_peers()
        vbuf[me] = vec_ref[...]
        mbuf[me] = dm_ref[...]
        _exchange(lambda p: vbuf.at[me], lambda p: vbuf.at[p], send1, recv1)
        _exchange(lambda p: mbuf.at[me], lambda p: mbuf.at[p], send2, recv2)
        tot = vbuf[0]
        msum = mbuf[0]
        for k in range(1, N_DEV):
            tot = tot + vbuf[k]
            msum = msum + mbuf[k]
        tot_ref[...] = tot
        gb_ref[...] = jnp.sum(msum, axis=0, keepdims=True)
        loc = pl.ds(pl.multiple_of(me * ncol, ncol), ncol)
        for k in range(N_DEV):
            dmx[k * SUBLANES:(k + 1) * SUBLANES, :] = mbuf[k, :, loc]
        cm = msum[2:3, :]
        mbuf[0, 2:3, :] = cm
        cm_loc = mbuf[0, 2:3, loc]
        dmx[N_DEV * SUBLANES:, :] = jnp.concatenate([cm_loc, jnp.zeros((SUBLANES - 1, ncol), F32)], axis=0)
        gw_ref[...] = _dot_tn(s_ref[...], dmx[...])
        tbuf[me] = _dot_nt(dmx[N_DEV * SUBLANES:, :], w_ref[...])
        _exchange(lambda p: tbuf.at[me], lambda p: tbuf.at[p], send3, recv3)
        tsum = tbuf[0]
        for k in range(1, N_DEV):
            tsum = tsum + tbuf[k]
        cc = cc_ref[...]
        sg = _sigmoid(cc)
        gc_ref[...] = tsum[0:1, :] * (sg * (1.0 + cc * (1.0 - sg)))

    return pl.pallas_call(
        body, name="small_ar",
        out_shape=(jax.ShapeDtypeStruct((rv, LANES), F32), jax.ShapeDtypeStruct((1, nm), F32),
                   jax.ShapeDtypeStruct((D, ncol), F32), jax.ShapeDtypeStruct((1, D), F32)),
        in_specs=[_vmem()] * 5, out_specs=(_vmem(),) * 4,
        scratch_shapes=[pltpu.VMEM((N_DEV, rv, LANES), F32), pltpu.VMEM((N_DEV, SUBLANES, nm), F32),
                        pltpu.VMEM((N_DEV, SUBLANES, D), F32), pltpu.VMEM((srows, ncol), F32)]
                       + [pltpu.SemaphoreType.DMA((N_DEV - 1,))] * 6,
        compiler_params=pltpu.CompilerParams(vmem_limit_bytes=VMEM_LIMIT),
    )(vec, dmods, silu_all, w_ada, c_ctx.reshape(1, D))


def _adamw(w, g, m, v, name):
    rows, cols = w.shape
    tr = _div_tile(rows, 256, SUBLANES) if rows * cols > 65536 else rows

    def body(w_ref, g_ref, m_ref, v_ref, d_ref, nm_ref, nv_ref):
        gv = g_ref[...]
        mn = ADAM_B1 * m_ref[...] + (1.0 - ADAM_B1) * gv
        vn = ADAM_B2 * v_ref[...] + (1.0 - ADAM_B2) * (gv * gv)
        m_hat = mn / (1.0 - ADAM_B1 ** ADAM_STEP)
        v_hat = vn / (1.0 - ADAM_B2 ** ADAM_STEP)
        d_ref[...] = -ADAM_LR * (m_hat / (jnp.sqrt(v_hat) + ADAM_EPS) + ADAM_WD * w_ref[...])
        nm_ref[...] = mn
        nv_ref[...] = vn

    spec = pl.BlockSpec((tr, cols), lambda i: (i, 0))
    return pl.pallas_call(
        functools.partial(body), name=name, grid=(rows // tr,),
        out_shape=(jax.ShapeDtypeStruct((rows, cols), F32),) * 3,
        in_specs=[spec] * 4, out_specs=(spec,) * 3,
        compiler_params=_params("parallel"),
    )(w, g, m, v)


def _rope_tables(n_ctx, n):
    n_freq = RET_DIM // 4
    inv = ROPE_BASE ** (-jnp.arange(n_freq, dtype=F32) / n_freq)
    tok = jnp.arange(n)
    pos_r = (tok // GRID_W).astype(F32)
    pos_c = (tok % GRID_W).astype(F32)
    ang_r = pos_r[:, None] * inv[None, :]
    ang_c = pos_c[:, None] * inv[None, :]
    cos = jnp.concatenate([jnp.cos(ang_r), jnp.cos(ang_r), jnp.cos(ang_c), jnp.cos(ang_c)], axis=-1)
    sin = jnp.concatenate([-jnp.sin(ang_r), jnp.sin(ang_r), -jnp.sin(ang_c), jnp.sin(ang_c)], axis=-1)
    cos = jnp.concatenate([jnp.ones((n_ctx, RET_DIM), F32), cos], axis=0)
    sin = jnp.concatenate([jnp.zeros((n_ctx, RET_DIM), F32), sin], axis=0)
    return cos, sin


def _na_tables():
    q = np.arange(GRID_W)[:, None]
    k = np.arange(GRID_W)[None, :]
    start = np.clip(q - NA_KW // 2, 0, GRID_W - NA_KW)
    valid = (k >= start) & (k < start + NA_KW)
    dc = np.clip(k - q + (NA_KW - 1), 0, 2 * NA_KW - 2)
    ncls = 2 * NA_KW - 1
    onehot = (dc[None] == np.arange(ncls)[:, None, None]) & valid[None]
    oh2 = np.zeros((GRID_W, LANES, LANES), np.float32)
    for c in range(ncls):
        oh2[:, :GRID_W, c] = onehot[c]
        oh2[:, GRID_W:, 32 + c] = onehot[c]
    return onehot.astype(np.float32), valid, oh2.reshape(GRID_W * LANES, LANES)


def _paired_bias(rpb, onehot, valid):
    t = jnp.einsum("hdc,cqk->hdqk", rpb, jnp.asarray(onehot), precision=lax.Precision.HIGHEST)
    t = jnp.where(jnp.asarray(valid)[None, None], t, NEG_INF)
    return jnp.concatenate([t[:, :-1], t[:, 1:]], axis=-1)


def kernel(x, c, ctx, c_ctx, w_ada, b_ada, g_pre_mix, g_post_mix, g_pre_mlp, g_post_mlp, w_in, ret_decay, ret_gn, na_rpb, w_out, w_mlp1, w_mlp2, loss_target, m_c_ctx, m_w_ada, m_b_ada, m_g_pre_mix, m_g_post_mix, m_g_pre_mlp, m_g_post_mlp, m_w_in, m_ret_decay, m_ret_gn, m_na_rpb, m_w_out, m_w_mlp1, m_w_mlp2, v_c_ctx, v_w_ada, v_b_ada, v_g_pre_mix, v_g_post_mix, v_g_pre_mlp, v_g_post_mlp, v_w_in, v_ret_decay, v_ret_gn, v_na_rpb, v_w_out, v_w_mlp1, v_w_mlp2):
    B, N, D = x.shape
    C = ctx.shape[1]
    T = C + N
    me = 4 * lax.axis_index("x") + 2 * lax.axis_index("y") + lax.axis_index("c")

    silu_all, mods_g = _mod_gather(c, c_ctx, w_ada[0], b_ada)
    mods_full = mods_g.transpose(1, 0, 2).reshape(mods_g.shape[1], N_MOD * D)
    mine = lax.dynamic_slice_in_dim(mods_full, me * SUBLANES, B, axis=0)
    modl = jnp.concatenate([mine, mods_full[N_DEV * SUBLANES:N_DEV * SUBLANES + 1]], axis=0)
    modl = modl.reshape(B + 1, N_MOD, 1, D)

    gin, wout_b, w1_b, w2_b = _w_gather(w_in[0], w_out[0], w_mlp1[0], w_mlp2[0])
    win_b = gin.transpose(1, 0, 2).reshape(D, N_DEV * gin.shape[2])

    cos, sin = _rope_tables(C, N)
    onehot, valid, oh2 = _na_tables()
    bias2 = _paired_bias(na_rpb[0], onehot, valid)
    lg = jax.nn.log_sigmoid(ret_decay[0].astype(F32))

    x_all = jnp.concatenate([ctx, x], axis=1)
    h_all, proj = _inproj_fwd(x_all, modl, g_pre_mix, win_b, C)
    o_ret, lat_ret = _ret_fwd(proj, cos, sin, lg, ret_gn, C)
    lat_na = _na_fwd(proj, bias2, C)

    (dy1, dlat_ret, dlat_na, dmix, h2, act, du, dz, red_d) = _dense_core(
        lat_ret, lat_na, x, loss_target, modl, g_post_mix, g_pre_mlp, g_post_mlp, wout_b, w1_b, w2_b)

    dret, dgn_p, dlg_p = _ret_bwd(proj, cos, sin, lg, ret_gn, o_ret, dlat_ret, C)
    dna, dbias2 = _na_bwd(proj, bias2, dlat_na, C)
    grad_x, red_i = _inproj_bwd(dret, dna, x_all, dy1, modl, g_pre_mix, win_b, C)

    gw_out_p = jnp.concatenate([_tn_matmul(lat_ret[:, None], dmix, "gw_out_ret"),
                                _tn_matmul(lat_na[:, None], dmix, "gw_out_na")], axis=0)
    gw1_p = _tn_matmul(h2[:, None], du, "gw_mlp1")
    gw2_p = _tn_matmul(act[:, None], dz, "gw_mlp2")
    gwin_t_p = jnp.concatenate([_tn_matmul(dret, h_all, "gw_in_ret"), _tn_matmul(dna, h_all, "gw_in_na")], axis=0)

    bin_, bout, b1, b2 = _grad_rs(gwin_t_p, gw_out_p, gw1_p, gw2_p)
    g_w_in = _sum_slots(bin_, "sum_w_in").T
    g_w_out = _sum_slots(bout, "sum_w_out")
    g_w1 = _sum_slots(b1, "sum_w_mlp1")
    g_w2 = _sum_slots(b2, "sum_w_mlp2")

    rd = red_d.sum(axis=1)[:, :, :]
    ri = red_i
    nct = ri.shape[1] * C // T
    ri_ctx = ri[:, :nct].sum(axis=(0, 1))
    ri_lat = ri[:, nct:].sum(axis=1)
    d_mods = jnp.concatenate([ri_lat[:, 0], ri_lat[:, 1], rd[:, 0], rd[:, 4], rd[:, 3], rd[:, 2]], axis=-1)
    d_cmods = jnp.concatenate([ri_ctx[0], ri_ctx[1], jnp.zeros(((N_MOD - 2) * D,), F32)])[None]
    dm_slot = jnp.concatenate([d_mods, d_cmods, jnp.zeros((SUBLANES - B - 1, N_MOD * D), F32)], axis=0)
    dg_pre_mix = ri_lat[:, 2].sum(axis=0) + ri_ctx[2]
    dg_post_mix = rd[:, 1].sum(axis=0)
    dg_pre_mlp = rd[:, 5].sum(axis=0)
    dg_post_mlp = rd[:, 6].sum(axis=0)
    loss_p = rd[:, 7, 0].sum()
    d_gn = dgn_p[:, 0].sum(axis=0)
    d_lg = dlg_p[:, :, :2, 0].sum(axis=0).T
    d_decay = d_lg * jax.nn.sigmoid(-ret_decay[0].astype(F32))
    rr = _rpb_reduce(dbias2, jnp.asarray(oh2, BF16)).reshape(NA_HEADS, 2 * NA_KH - 2, LANES)
    ncls = 2 * NA_KW - 1
    d_rpb = (jnp.pad(rr[:, :, :ncls], ((0, 0), (0, 1), (0, 0))) + jnp.pad(rr[:, :, 32:32 + ncls], ((0, 0), (1, 0), (0, 0))))
    d_rpb32 = jnp.pad(d_rpb, ((0, 0), (0, 0), (0, 32 - ncls)))
    pieces = [dg_pre_mix, dg_post_mix, dg_pre_mlp, dg_post_mlp, d_gn, d_rpb32.reshape(-1),
              jnp.pad(d_decay.reshape(-1), (0, LANES - d_decay.size)), jnp.full((LANES,), loss_p, F32)]
    vec = jnp.concatenate(pieces)
    pad = (-vec.shape[0]) % (SUBLANES * LANES)
    vec = jnp.pad(vec, (0, pad)).reshape(-1, LANES)
    tot, g_b_ada, g_w_ada, g_c_ctx = _small_ar(vec, dm_slot, silu_all, w_ada[0], c_ctx)
    flat = tot.reshape(-1)
    o0 = 0
    g_pre_mix_g = flat[o0:o0 + D]; o0 += D
    g_post_mix_g = flat[o0:o0 + D]; o0 += D
    g_pre_mlp_g = flat[o0:o0 + D]; o0 += D
    g_post_mlp_g = flat[o0:o0 + D]; o0 += D
    g_gn = flat[o0:o0 + RET_WIDTH]; o0 += RET_WIDTH
    nrpb = NA_HEADS * (2 * NA_KH - 1) * 32
    g_rpb = flat[o0:o0 + nrpb].reshape(NA_HEADS, 2 * NA_KH - 1, 32)[:, :, :ncls]; o0 += nrpb
    g_decay = flat[o0:o0 + 2 * RET_HEADS].reshape(2, RET_HEADS); o0 += LANES
    loss = flat[o0]

    grads = {
        "c_ctx": g_c_ctx.reshape(c_ctx.shape), "w_ada": g_w_ada[None], "b_ada": g_b_ada.reshape(b_ada.shape),
        "g_pre_mix": g_pre_mix_g[None], "g_post_mix": g_post_mix_g[None], "g_pre_mlp": g_pre_mlp_g[None],
        "g_post_mlp": g_post_mlp_g[None], "w_in": g_w_in[None], "ret_decay": g_decay[None], "ret_gn": g_gn[None],
        "na_rpb": g_rpb[None], "w_out": g_w_out[None], "w_mlp1": g_w1[None], "w_mlp2": g_w2[None],
    }
    weights = dict(c_ctx=c_ctx, w_ada=w_ada, b_ada=b_ada, g_pre_mix=g_pre_mix, g_post_mix=g_post_mix,
                   g_pre_mlp=g_pre_mlp, g_post_mlp=g_post_mlp, w_in=w_in, ret_decay=ret_decay, ret_gn=ret_gn,
                   na_rpb=na_rpb, w_out=w_out, w_mlp1=w_mlp1, w_mlp2=w_mlp2)
    m_in = dict(c_ctx=m_c_ctx, w_ada=m_w_ada, b_ada=m_b_ada, g_pre_mix=m_g_pre_mix, g_post_mix=m_g_post_mix,
                g_pre_mlp=m_g_pre_mlp, g_post_mlp=m_g_post_mlp, w_in=m_w_in, ret_decay=m_ret_decay,
                ret_gn=m_ret_gn, na_rpb=m_na_rpb, w_out=m_w_out, w_mlp1=m_w_mlp1, w_mlp2=m_w_mlp2)
    v_in = dict(c_ctx=v_c_ctx, w_ada=v_w_ada, b_ada=v_b_ada, g_pre_mix=v_g_pre_mix, g_post_mix=v_g_post_mix,
                g_pre_mlp=v_g_pre_mlp, g_post_mlp=v_g_post_mlp, w_in=v_w_in, ret_decay=v_ret_decay,
                ret_gn=v_ret_gn, na_rpb=v_na_rpb, w_out=v_w_out, w_mlp1=v_w_mlp1, w_mlp2=v_w_mlp2)
    names = list(weights)
    deltas, new_m, new_v = {}, {}, {}
    for n in names:
        shp = weights[n].shape
        two_d = (-1, shp[-1]) if len(shp) > 1 else (1, shp[0])
        d, nm, nv = _adamw(weights[n].reshape(two_d), grads[n].reshape(two_d), m_in[n].reshape(two_d),
                           v_in[n].reshape(two_d), "adamw_" + n)
        deltas[n], new_m[n], new_v[n] = d.reshape(shp), nm.reshape(shp), nv.reshape(shp)
    return (loss, grad_x, *[grads[n] for n in names], *[deltas[n] for n in names],
            *[new_m[n] for n in names], *[new_v[n] for n in names])
```

```python
import functools
import math

import numpy as np
import jax
import jax.numpy as jnp
from jax import lax
from jax.experimental import pallas as pl
from jax.experimental.pallas import tpu as pltpu

F32 = jnp.float32
BF16 = jnp.bfloat16
MESH = pl.DeviceIdType.MESH

N_DEV = 8
LANES = 128
SUBLANES = 8
VMEM_LIMIT = 60 * 1024 * 1024

GRID_W = 64
RET_HEADS = 4
RET_DIM = 128
RET_WIDTH = RET_HEADS * RET_DIM
NA_HEADS = 8
NA_DIM = 64
NA_WIDTH = NA_HEADS * NA_DIM
NA_PAIRS = NA_HEADS // 2
NA_KH = 8
NA_KW = 16
NA_GROUP = 4
SEG = 512
ROPE_BASE = 10000.0
NORM_EPS = 1e-6
NEG_INF = -1e30
N_MOD = 6

ADAM_LR = 0.001
ADAM_B1 = 0.9
ADAM_B2 = 0.999
ADAM_EPS = 1e-08
ADAM_WD = 0.01
ADAM_STEP = 10


def _dot(a, b):
    return lax.dot_general(a, b, (((1,), (0,)), ((), ())), preferred_element_type=F32)


def _dot_nt(a, b):
    return lax.dot_general(a, b, (((1,), (1,)), ((), ())), preferred_element_type=F32)


def _dot_tn(a, b):
    return lax.dot_general(a, b, (((0,), (0,)), ((), ())), preferred_element_type=F32)


def _sigmoid(x):
    return 1.0 / (1.0 + jnp.exp(-x))


def _div_tile(n, cap, mult):
    if n <= cap:
        return n
    for t in range(cap - cap % mult, 0, -mult):
        if n % t == 0:
            return t
    raise ValueError(f"no tile for {n}")


def _params(*sem):
    return pltpu.CompilerParams(dimension_semantics=tuple(sem) if sem else None,
                                vmem_limit_bytes=VMEM_LIMIT)


def _vmem():
    return pl.BlockSpec(memory_space=pltpu.VMEM)


def _any():
    return pl.BlockSpec(memory_space=pl.ANY)


def _me_and_peers():
    x, y, c = lax.axis_index("x"), lax.axis_index("y"), lax.axis_index("c")
    me = 4 * x + 2 * y + c
    peers = []
    for m in range(1, N_DEV):
        px = 1 - x if (m >> 2) & 1 else x
        py = 1 - y if (m >> 1) & 1 else y
        pc = 1 - c if m & 1 else c
        peers.append(((px, py, pc), 4 * px + 2 * py + pc))
    return me, peers


def _exchange(src_for, dst_from, send_sems, recv_sems):
    me, peers = _me_and_peers()
    sent = []
    for i, (dev, pid) in enumerate(peers):
        cp = pltpu.make_async_remote_copy(src_ref=src_for(pid), dst_ref=dst_from(me),
                                          send_sem=send_sems.at[i], recv_sem=recv_sems.at[i],
                                          device_id=dev, device_id_type=MESH)
        cp.start()
        sent.append(cp)
    for i, (dev, pid) in enumerate(peers):
        pltpu.make_async_remote_copy(src_ref=src_for(pid), dst_ref=dst_from(pid),
                                     send_sem=send_sems.at[i], recv_sem=recv_sems.at[i],
                                     device_id=dev, device_id_type=MESH).wait_recv()
    for cp in sent:
        cp.wait_send()


SIBLING = (1,)
ICI_SAME_CORE = (2, 4, 6)
ALL_PEERS = tuple(range(1, N_DEV))


def _remote(src, dst, send_sem, recv_sem, dev):
    return pltpu.make_async_remote_copy(src_ref=src, dst_ref=dst, send_sem=send_sem, recv_sem=recv_sem,
                                        device_id=dev, device_id_type=MESH)


def _push_start(items, masks, send_sems, recv_sems):
    me, peers = _me_and_peers()
    for k, (src_for, dst_from) in enumerate(items):
        for m in masks:
            dev, pid = peers[m - 1]
            _remote(src_for(pid), dst_from(me), send_sems.at[k, m - 1], recv_sems.at[k, m - 1], dev).start()


def _push_wait_recv(items, masks, send_sems, recv_sems):
    me, peers = _me_and_peers()
    for k, (src_for, dst_from) in enumerate(items):
        for m in masks:
            dev, pid = peers[m - 1]
            _remote(src_for(pid), dst_from(pid), send_sems.at[k, m - 1], recv_sems.at[k, m - 1], dev).wait_recv()


def _push_wait_send(items, masks, send_sems, recv_sems):
    me, peers = _me_and_peers()
    for k, (src_for, dst_from) in enumerate(items):
        for m in masks:
            dev, pid = peers[m - 1]
            _remote(src_for(pid), dst_from(me), send_sems.at[k, m - 1], recv_sems.at[k, m - 1], dev).wait_send()


def _forward_start(items, send_sems, recv_sems):
    me, peers = _me_and_peers()
    sib = peers[0][0]
    for k, (blk_in, blk_out) in enumerate(items):
        for j, m in enumerate(ICI_SAME_CORE):
            pid = peers[m - 1][1]
            _remote(blk_in(pid), blk_out(pid), send_sems.at[k, j], recv_sems.at[k, j], sib).start()


def _forward_wait(items, send_sems, recv_sems):
    me, peers = _me_and_peers()
    sib = peers[0][0]
    for k, (blk_in, blk_out) in enumerate(items):
        for j, m in enumerate(ICI_SAME_CORE):
            got = peers[(m | 1) - 1][1]
            _remote(blk_in(got), blk_out(got), send_sems.at[k, j], recv_sems.at[k, j], sib).wait_recv()
    for k, (blk_in, blk_out) in enumerate(items):
        for j, m in enumerate(ICI_SAME_CORE):
            pid = peers[m - 1][1]
            _remote(blk_in(pid), blk_out(pid), send_sems.at[k, j], recv_sems.at[k, j], sib).wait_send()


def _mod_gather(c, c_ctx, w_ada, b_ada, w_in, w_out, w1, w2):
    B, D = c.shape
    ncol = w_ada.shape[1]
    rows = SUBLANES * N_DEV + SUBLANES

    def body(c_ref, cc_ref, w_ref, b_ref, win_ref, wout_ref, w1_ref, w2_ref,
             s_ref, m_ref, gin_ref, wout_b, w1_b, w2_b,
             win_b, send1, recv1, send2, recv2, wsend, wrecv, fsend, frecv, lsem):
        me, _ = _me_and_peers()
        win_b[...] = win_ref[...].astype(BF16)
        gather = [(lambda p: win_b, lambda p: gin_ref.at[p])]
        own = pltpu.make_async_copy(win_b, gin_ref.at[me], lsem.at[0])
        own.start()
        _push_start(gather, SIBLING + ICI_SAME_CORE, wsend, wrecv)
        wout_b[...] = wout_ref[...].astype(BF16)
        w1_b[...] = w1_ref[...].astype(BF16)
        w2_b[...] = w2_ref[...].astype(BF16)
        cv = c_ref[...]
        slot = jnp.concatenate([cv * _sigmoid(cv), jnp.zeros((SUBLANES - B, D), F32)], axis=0)
        my_rows = pl.ds(pl.multiple_of(me * SUBLANES, SUBLANES), SUBLANES)
        s_ref[my_rows, :] = slot
        ccv = cc_ref[...]
        s_ref[SUBLANES * N_DEV:, :] = jnp.concatenate(
            [ccv * _sigmoid(ccv), jnp.zeros((SUBLANES - 1, D), F32)], axis=0)

        def rows_of(p):
            return s_ref.at[pl.ds(pl.multiple_of(p * SUBLANES, SUBLANES), SUBLANES), :]

        _exchange(lambda p: rows_of(me), rows_of, send1, recv1)
        b_loc = b_ref[:, pl.ds(pl.multiple_of(me * ncol, ncol), ncol)]
        m_ref[me] = _dot(s_ref[...], w_ref[...]) + b_loc
        _exchange(lambda p: m_ref.at[me], lambda p: m_ref.at[p], send2, recv2)
        _push_wait_recv(gather, ICI_SAME_CORE, wsend, wrecv)
        relay = [(lambda p: gin_ref.at[p], lambda p: gin_ref.at[p])]
        _forward_start(relay, fsend, frecv)
        _push_wait_recv(gather, SIBLING, wsend, wrecv)
        _forward_wait(relay, fsend, frecv)
        _push_wait_send(gather, SIBLING + ICI_SAME_CORE, wsend, wrecv)
        own.wait()

    return pl.pallas_call(
        body, name="mod_gather",
        out_shape=(jax.ShapeDtypeStruct((rows, D), F32), jax.ShapeDtypeStruct((N_DEV, rows, ncol), F32),
                   jax.ShapeDtypeStruct((N_DEV,) + w_in.shape, BF16),
                   jax.ShapeDtypeStruct(w_out.shape, BF16), jax.ShapeDtypeStruct(w1.shape, BF16),
                   jax.ShapeDtypeStruct(w2.shape, BF16)),
        in_specs=[_vmem()] * 8, out_specs=(_vmem(), _vmem(), _any(), _vmem(), _vmem(), _vmem()),
        scratch_shapes=[pltpu.VMEM(w_in.shape, BF16)] + [pltpu.SemaphoreType.DMA((N_DEV - 1,))] * 4
                       + [pltpu.SemaphoreType.DMA((1, N_DEV - 1))] * 2 + [pltpu.SemaphoreType.DMA((1, 3))] * 2
                       + [pltpu.SemaphoreType.DMA((1,))],
        compiler_params=pltpu.CompilerParams(vmem_limit_bytes=VMEM_LIMIT),
    )(c, c_ctx.reshape(1, D), w_ada, b_ada, w_in, w_out, w1, w2)


def _row_block(ref, rows):
    return lambda p: ref.at[pl.ds(pl.multiple_of(p * rows, 2 * SUBLANES), rows), :]


def _col_block(ref, cols):
    return lambda p: ref.at[:, pl.ds(pl.multiple_of(p * cols, LANES), cols)]


def _slot(ref):
    return lambda p: ref.at[p]


class _Hosted:
    def __init__(self, kind, masks, operands, block_of, out_shapes, with_own):
        self.kind, self.masks, self.operands = kind, masks, list(operands)
        self.block_of, self.out_shapes, self.with_own = block_of, list(out_shapes), with_own
        self.n = len(self.operands)

    def scratch(self):
        return [pltpu.SemaphoreType.DMA((self.n, N_DEV - 1)), pltpu.SemaphoreType.DMA((self.n, N_DEV - 1)),
                pltpu.SemaphoreType.DMA((self.n,))]

    def _items(self, in_refs, out_refs):
        items = []
        for k in range(self.n):
            if self.kind == "gather":
                items.append((lambda p, k=k: in_refs[k], self.block_of[k](out_refs[k])))
            else:
                items.append((self.block_of[k](in_refs[k]), _slot(out_refs[k])))
        return items

    def _own(self, in_refs, out_refs, lsem):
        me, _ = _me_and_peers()
        items = self._items(in_refs, out_refs)
        return [pltpu.make_async_copy(src_for(me), dst_from(me), lsem.at[k])
                for k, (src_for, dst_from) in enumerate(items)]

    def start(self, in_refs, out_refs, sems):
        send, recv, lsem = sems
        if self.with_own:
            for cp in self._own(in_refs, out_refs, lsem):
                cp.start()
        _push_start(self._items(in_refs, out_refs), self.masks, send, recv)

    def wait(self, in_refs, out_refs, sems):
        send, recv, lsem = sems
        items = self._items(in_refs, out_refs)
        _push_wait_recv(items, self.masks, send, recv)
        _push_wait_send(items, self.masks, send, recv)
        if self.with_own:
            for cp in self._own(in_refs, out_refs, lsem):
                cp.wait()


class _HostedRelay:
    def __init__(self, arrays, block_of):
        self.operands, self.block_of = list(arrays), block_of
        self.out_shapes = [jax.ShapeDtypeStruct(a.shape, a.dtype) for a in arrays]
        self.n = len(self.operands)

    def scratch(self):
        return [pltpu.SemaphoreType.DMA((self.n, 3)), pltpu.SemaphoreType.DMA((self.n, 3))]

    def _items(self, in_refs, out_refs):
        return [(self.block_of[k](in_refs[k]), self.block_of[k](out_refs[k])) for k in range(self.n)]

    def start(self, in_refs, out_refs, sems):
        _forward_start(self._items(in_refs, out_refs), *sems)

    def wait(self, in_refs, out_refs, sems):
        _forward_wait(self._items(in_refs, out_refs), *sems)


def _call_hosting(body, hosted, *, name, grid, out_shape, in_specs, out_specs, scratch_shapes, args):
    n_in, n_out, n_scr, hn = len(in_specs), len(out_shape), len(scratch_shapes), hosted.n
    relay = isinstance(hosted, _HostedRelay)

    def wrapped(*refs):
        ins = refs[:n_in]
        h_in = refs[n_in:n_in + hn]
        outs = refs[n_in + hn:n_in + hn + n_out]
        h_out = refs[n_in + hn + n_out:n_in + 2 * hn + n_out]
        scr = refs[n_in + 2 * hn + n_out:n_in + 2 * hn + n_out + n_scr]
        sems = refs[n_in + 2 * hn + n_out + n_scr:]
        ids = [pl.program_id(i) for i in range(len(grid))]
        first = functools.reduce(jnp.logical_and, [i == 0 for i in ids])
        last = functools.reduce(jnp.logical_and, [i == g - 1 for i, g in zip(ids, grid)])

        @pl.when(first)
        def _():
            hosted.start(h_in, h_out, sems)

        body(*ins, *outs, *scr)

        @pl.when(last)
        def _():
            hosted.wait(h_in, h_out, sems)

    return pl.pallas_call(
        wrapped, name=name, grid=grid,
        out_shape=tuple(out_shape) + tuple(hosted.out_shapes),
        in_specs=list(in_specs) + [_any()] * hn,
        out_specs=tuple(out_specs) + (_any(),) * hn,
        scratch_shapes=list(scratch_shapes) + hosted.scratch(),
        input_output_aliases={n_in + k: n_out + k for k in range(hn)} if relay else {},
        compiler_params=_params(*(("arbitrary",) * len(grid))),
    )(*args, *hosted.operands)


def _inproj_fwd(x_all, modl, g1, w_in, n_ctx, hosted):
    B, T, D = x_all.shape
    nw = w_in.shape[1]
    tm = _div_tile(n_ctx, 256, 16)
    nct = n_ctx // tm

    def body(x_ref, sh_ref, sc_ref, g_ref, w_ref, h_ref, p_ref):
        x = x_ref[...]
        r = lax.rsqrt(jnp.mean(x * x, axis=-1, keepdims=True) + NORM_EPS)
        h = ((x * r) * g_ref[...]) * (1.0 + sc_ref[...]) + sh_ref[...]
        hb = h.astype(BF16)
        h_ref[...] = hb
        p_ref[...] = _dot(hb, w_ref[...])

    def mrow(b, t):
        return jnp.where(t < nct, B, b)

    return _call_hosting(
        body, hosted, name="inproj_fwd", grid=(B, T // tm),
        out_shape=(jax.ShapeDtypeStruct((B, T, D), BF16), jax.ShapeDtypeStruct((B, T, nw), F32)),
        in_specs=[pl.BlockSpec((None, tm, D), lambda b, t: (b, t, 0)),
                  pl.BlockSpec((None, None, 1, D), lambda b, t: (mrow(b, t), 0, 0, 0)),
                  pl.BlockSpec((None, None, 1, D), lambda b, t: (mrow(b, t), 1, 0, 0)),
                  pl.BlockSpec((1, D), lambda b, t: (0, 0)),
                  pl.BlockSpec((D, nw), lambda b, t: (0, 0))],
        out_specs=(pl.BlockSpec((None, tm, D), lambda b, t: (b, t, 0)),
                   pl.BlockSpec((None, tm, nw), lambda b, t: (b, t, 0))),
        scratch_shapes=[], args=(x_all, modl, modl, g1, w_in))


def _swap32(x):
    lane = lax.broadcasted_iota(jnp.int32, x.shape, 1)
    return jnp.where((lane % 64) < 32, pltpu.roll(x, 96, 1), pltpu.roll(x, 32, 1))


def _rope(x, cos, sin):
    return x * cos + _swap32(x) * sin


def _unrope(dy, cos, sin):
    return dy * cos + _swap32(dy * sin)


def _ret_weights(lgf, lgb, dist):
    return jnp.exp(jnp.where(dist >= 0.0, lgf * dist, -lgb * dist))


def _ret_fwd(proj, cos, sin, lg, gn, n_ctx, hosted):
    B, T, _ = proj.shape
    C = n_ctx
    N = T - C
    tq = _div_tile(N, 256, 16)
    tk = tq
    scale = RET_DIM ** -0.5

    def body(lg_ref, q_ref, k_ref, v_ref, g_ref, cos_ref, sin_ref, gn_ref, o_ref, lat_ref, qs, ks, vs):
        h = pl.program_id(1)
        lgf = lg_ref[0, h]
        lgb = lg_ref[1, h]
        cosv = cos_ref[...]
        sinv = sin_ref[...]
        qs[...] = (_rope(q_ref[...], cosv, sinv) * scale).astype(BF16)
        ks[...] = _rope(k_ref[...], cosv, sinv).astype(BF16)
        vs[...] = v_ref[...].astype(BF16)
        gnv = gn_ref[...]
        rc = (lax.broadcasted_iota(jnp.int32, (tq, tk), 0) - lax.broadcasted_iota(jnp.int32, (tq, tk), 1)).astype(F32)
        ri = lax.broadcasted_iota(jnp.int32, (tq, C), 0).astype(F32)
        ti = lax.broadcasted_iota(jnp.int32, (tq, C), 1).astype(F32)

        def q_tile(qi, carry):
            i0 = pl.multiple_of(qi * tq, tq)
            i0f = (qi * tq).astype(F32)
            qt = qs[pl.ds(C + i0, tq), :]
            s = _dot_nt(qt, ks[0:C, :])
            w = jnp.exp(lgf * (ri + (i0f + C) - ti)) + jnp.exp(lgb * ((N - i0f) - ri + ti))
            acc = _dot((s * w).astype(BF16), vs[0:C, :])

            def k_tile(kj, acc):
                j0 = pl.multiple_of(kj * tk, tk)
                kt = ks[pl.ds(C + j0, tk), :]
                s = _dot_nt(qt, kt)
                w = _ret_weights(lgf, lgb, rc + (i0f - (kj * tk).astype(F32)))
                return acc + _dot((s * w).astype(BF16), vs[pl.ds(C + j0, tk), :])

            o = lax.fori_loop(0, N // tk, k_tile, acc, unroll=True)
            o_ref[pl.ds(i0, tq), :] = o
            mu = jnp.mean(o, axis=-1, keepdims=True)
            oc = o - mu
            var = jnp.mean(oc * oc, axis=-1, keepdims=True)
            yh = oc * lax.rsqrt(var + NORM_EPS)
            g = g_ref[pl.ds(C + i0, tq), :]
            lat_ref[pl.ds(i0, tq), :] = ((yh * gnv) * (g * _sigmoid(g))).astype(BF16)
            return carry

        lax.fori_loop(0, N // tq, q_tile, 0)

    def col(seg):
        return pl.BlockSpec((None, T, RET_DIM), lambda b, h, seg=seg: (b, 0, seg * RET_HEADS + h))

    return _call_hosting(
        body, hosted, name="ret_fwd", grid=(B, RET_HEADS),
        out_shape=(jax.ShapeDtypeStruct((B, N, RET_WIDTH), F32), jax.ShapeDtypeStruct((B, N, RET_WIDTH), BF16)),
        in_specs=[pl.BlockSpec(memory_space=pltpu.SMEM), col(0), col(1), col(2), col(3),
                  pl.BlockSpec((T, RET_DIM), lambda b, h: (0, 0)), pl.BlockSpec((T, RET_DIM), lambda b, h: (0, 0)),
                  pl.BlockSpec((1, RET_DIM), lambda b, h: (0, h))],
        out_specs=(pl.BlockSpec((None, N, RET_DIM), lambda b, h: (b, 0, h)),
                   pl.BlockSpec((None, N, RET_DIM), lambda b, h: (b, 0, h))),
        scratch_shapes=[pltpu.VMEM((T, RET_DIM), BF16)] * 3,
        args=(lg, proj, proj, proj, proj, cos, sin, gn))


def _ret_bwd(proj, cos, sin, lg, gn, o, dlat, n_ctx, hosted):
    B, T, _ = proj.shape
    C = n_ctx
    N = T - C
    tq = _div_tile(N, 256, 16)
    tk = tq
    scale = RET_DIM ** -0.5

    def body(lg_ref, q_ref, k_ref, v_ref, g_ref, cos_ref, sin_ref, gn_ref, o_ref, dl_ref,
             d_ref, dgn_ref, dlg_ref, qs, ks, vs, dos):
        h = pl.program_id(1)
        lgf = lg_ref[0, h]
        lgb = lg_ref[1, h]
        cosv = cos_ref[...]
        sinv = sin_ref[...]
        qs[...] = (_rope(q_ref[...], cosv, sinv) * scale).astype(BF16)
        ks[...] = _rope(k_ref[...], cosv, sinv).astype(BF16)
        vs[...] = v_ref[...].astype(BF16)
        gnv = gn_ref[...]

        ov = o_ref[...]
        mu = jnp.mean(ov, axis=-1, keepdims=True)
        oc = ov - mu
        var = jnp.mean(oc * oc, axis=-1, keepdims=True)
        rstd = lax.rsqrt(var + NORM_EPS)
        yh = oc * rstd
        g = g_ref[C:, :]
        sg = _sigmoid(g)
        silu = g * sg
        dl = dl_ref[...]
        d_ref[3, 0:C, :] = jnp.zeros((C, RET_DIM), F32)
        d_ref[3, C:, :] = dl * (yh * gnv) * (sg * (1.0 + g * (1.0 - sg)))
        dls = dl * silu
        dgn = jnp.sum(dls * yh, axis=0, keepdims=True)
        dgn_ref[...] = jnp.concatenate([dgn, jnp.zeros((SUBLANES - 1, RET_DIM), F32)], axis=0)
        dyh = dls * gnv
        do = rstd * (dyh - jnp.mean(dyh, axis=-1, keepdims=True) - yh * jnp.mean(dyh * yh, axis=-1, keepdims=True))
        dos[...] = do.astype(BF16)

        d_ref[0, 0:C, :] = jnp.zeros((C, RET_DIM), F32)
        d_ref[1] = jnp.zeros((T, RET_DIM), F32)
        d_ref[2] = jnp.zeros((T, RET_DIM), F32)

        rc = (lax.broadcasted_iota(jnp.int32, (tq, tk), 0) - lax.broadcasted_iota(jnp.int32, (tq, tk), 1)).astype(F32)
        ri = lax.broadcasted_iota(jnp.int32, (tq, C), 0).astype(F32)
        ti = lax.broadcasted_iota(jnp.int32, (tq, C), 1).astype(F32)

        def fold(a):
            return jnp.sum(a.reshape(a.shape[0] // SUBLANES, SUBLANES, a.shape[1]), axis=0)

        def q_tile(qi, carry):
            gf, gb = carry
            i0 = pl.multiple_of(qi * tq, tq)
            i0f = (qi * tq).astype(F32)
            qt = qs[pl.ds(C + i0, tq), :]
            dot = dos[pl.ds(i0, tq), :]
            kc = ks[0:C, :]
            vc = vs[0:C, :]
            s = _dot_nt(qt, kc)
            dp = _dot_nt(dot, vc)
            ef = ri + (i0f + C) - ti
            eb = (N - i0f) - ri + ti
            wf = jnp.exp(lgf * ef)
            wb = jnp.exp(lgb * eb)
            w = wf + wb
            d_ref[2, 0:C, :] += _dot_tn((s * w).astype(BF16), dot)
            ds = (dp * w).astype(BF16)
            dq = _dot(ds, kc)
            d_ref[1, 0:C, :] += _dot_tn(ds, qt)
            gs = dp * s
            gfc = fold(gs * wf * ef)
            gbc = fold(gs * wb * eb)

            def k_tile(kj, carry):
                dq, gf, gb = carry
                j0 = pl.multiple_of(kj * tk, tk)
                rows = pl.ds(C + j0, tk)
                kt = ks[rows, :]
                vt = vs[rows, :]
                s = _dot_nt(qt, kt)
                dp = _dot_nt(dot, vt)
                dist = rc + (i0f - (kj * tk).astype(F32))
                w = _ret_weights(lgf, lgb, dist)
                d_ref[2, rows, :] += _dot_tn((s * w).astype(BF16), dot)
                ds = (dp * w).astype(BF16)
                dq = dq + _dot(ds, kt)
                d_ref[1, rows, :] += _dot_tn(ds, qt)
                gw = dp * s * w * dist
                gf = gf + fold(jnp.where(dist >= 0.0, gw, 0.0))
                gb = gb - fold(jnp.where(dist < 0.0, gw, 0.0))
                return dq, gf, gb

            dq, gf2, gb2 = lax.fori_loop(0, N // tk, k_tile, (dq, jnp.zeros((SUBLANES, tk), F32),
                                                            jnp.zeros((SUBLANES, tk), F32)), unroll=True)
            qrows = pl.ds(C + i0, tq)
            d_ref[0, qrows, :] = _unrope(dq * scale, cos_ref[qrows, :], sin_ref[qrows, :])
            return gf + jnp.sum(gf2) + jnp.sum(gfc), gb + jnp.sum(gb2) + jnp.sum(gbc)

        gf, gb = lax.fori_loop(0, N // tq, q_tile, (jnp.zeros((), F32), jnp.zeros((), F32)))
        d_ref[1] = _unrope(d_ref[1], cosv, sinv)
        row = lax.broadcasted_iota(jnp.int32, (SUBLANES, LANES), 0)
        dlg_ref[...] = jnp.where(row == 0, gf, jnp.where(row == 1, gb, 0.0))

    def col(seg):
        return pl.BlockSpec((None, T, RET_DIM), lambda b, h, seg=seg: (b, 0, seg * RET_HEADS + h))

    return _call_hosting(
        body, hosted, name="ret_bwd", grid=(B, RET_HEADS),
        out_shape=(jax.ShapeDtypeStruct((B, 4, T, RET_WIDTH), F32),
                   jax.ShapeDtypeStruct((B, SUBLANES, RET_WIDTH), F32),
                   jax.ShapeDtypeStruct((B, RET_HEADS, SUBLANES, LANES), F32)),
        in_specs=[pl.BlockSpec(memory_space=pltpu.SMEM), col(0), col(1), col(2), col(3),
                  pl.BlockSpec((T, RET_DIM), lambda b, h: (0, 0)), pl.BlockSpec((T, RET_DIM), lambda b, h: (0, 0)),
                  pl.BlockSpec((1, RET_DIM), lambda b, h: (0, h)),
                  pl.BlockSpec((None, N, RET_DIM), lambda b, h: (b, 0, h)),
                  pl.BlockSpec((None, N, RET_DIM), lambda b, h: (b, 0, h))],
        out_specs=(pl.BlockSpec((None, 4, T, RET_DIM), lambda b, h: (b, 0, 0, h)),
                   pl.BlockSpec((None, SUBLANES, RET_DIM), lambda b, h: (b, 0, h)),
                   pl.BlockSpec((None, None, SUBLANES, LANES), lambda b, h: (b, h, 0, 0))),
        scratch_shapes=[pltpu.VMEM((T, RET_DIM), BF16)] * 3 + [pltpu.VMEM((N, RET_DIM), BF16)],
        args=(lg, proj, proj, proj, proj, cos, sin, gn, o, dlat))


def _na_geometry(rows):
    kh = min(NA_KH, rows)
    return kh, kh * GRID_W


def _pair_select():
    lane = lax.broadcasted_iota(jnp.int32, (2 * GRID_W, LANES), 1)
    row = lax.broadcasted_iota(jnp.int32, (2 * GRID_W, LANES), 0)
    return (lane >= NA_DIM) == (row >= GRID_W)


def _pair_bias(bias_ref, dr0, kh):
    return jnp.concatenate(
        [jnp.concatenate([bias_ref[e, pl.ds(dr0 + 2 * m, 1)].reshape(GRID_W, LANES) for m in range(kh // 2)], axis=1)
         for e in range(2)], axis=0)


def _na_softmax(s_loc, s_ctx):
    mx = jnp.maximum(jnp.max(s_loc, axis=-1, keepdims=True), jnp.max(s_ctx, axis=-1, keepdims=True))
    p_loc = jnp.exp(s_loc - mx)
    p_ctx = jnp.exp(s_ctx - mx)
    den = jnp.sum(p_loc, axis=-1, keepdims=True) + jnp.sum(p_ctx, axis=-1, keepdims=True)
    return p_loc, p_ctx, den


def _na_fwd(proj, bias2, n_ctx, hosted):
    B, T, _ = proj.shape
    C = n_ctx
    N = T - C
    R = N // GRID_W
    kh, nk = _na_geometry(R)
    scale = NA_DIM ** -0.5
    base = (4 * RET_WIDTH) // LANES

    def body(q_ref, k_ref, v_ref, bias_ref, out_ref, kb16, vb16):
        kb16[...] = k_ref[...].astype(BF16)
        vb16[...] = v_ref[...].astype(BF16)
        kc = kb16[0:C, :]
        vc = vb16[0:C, :]
        lane = lax.broadcasted_iota(jnp.int32, (GRID_W, LANES), 1)
        sel2 = _pair_select()

        def group(gi, carry):
            pre = []
            for u in range(NA_GROUP):
                r = gi * NA_GROUP + u
                bs = jnp.clip(r - kh // 2, 0, R - kh)
                dr0 = bs - r + (NA_KH - 1)
                q = q_ref[pl.ds(pl.multiple_of(C + r * GRID_W, GRID_W), GRID_W), :] * scale
                q2 = jnp.where(sel2, jnp.concatenate([q, q], axis=0), 0.0).astype(BF16)
                band = pl.ds(pl.multiple_of(C + bs * GRID_W, GRID_W), nk)
                s_loc = _dot_nt(q2, kb16[band, :]) + _pair_bias(bias_ref, dr0, kh)
                s_ctx = _dot_nt(q2, kc)
                pre.append((r, band, s_loc, s_ctx))
            mid = [(r, band) + _na_softmax(s_loc, s_ctx) for r, band, s_loc, s_ctx in pre]
            for r, band, p_loc, p_ctx, den in mid:
                o2 = (_dot(p_loc.astype(BF16), vb16[band, :]) + _dot(p_ctx.astype(BF16), vc)) / den
                out_ref[pl.ds(pl.multiple_of(r * GRID_W, GRID_W), GRID_W), :] = jnp.where(
                    lane < NA_DIM, o2[:GRID_W], o2[GRID_W:]).astype(BF16)
            return carry

        lax.fori_loop(0, R // NA_GROUP, group, 0)

    def col(seg):
        return pl.BlockSpec((None, T, LANES), lambda b, p, seg=seg: (b, 0, base + seg * NA_PAIRS + p))

    return _call_hosting(
        body, hosted, name="na_fwd", grid=(B, NA_PAIRS),
        out_shape=(jax.ShapeDtypeStruct((B, N, NA_WIDTH), BF16),),
        in_specs=[col(0), col(1), col(2),
                  pl.BlockSpec((2, 2 * NA_KH - 2, GRID_W, LANES), lambda b, p: (p, 0, 0, 0))],
        out_specs=(pl.BlockSpec((None, N, LANES), lambda b, p: (b, 0, p)),),
        scratch_shapes=[pltpu.VMEM((T, LANES), BF16)] * 2,
        args=(proj, proj, proj, bias2))


def _na_bwd(proj, bias2, dlat, n_ctx):
    B, T, _ = proj.shape
    C = n_ctx
    N = T - C
    R = N // GRID_W
    kh, nk = _na_geometry(R)
    scale = NA_DIM ** -0.5
    base = (4 * RET_WIDTH) // LANES

    def body(q_ref, k_ref, v_ref, bias_ref, dl_ref, d_ref, db_ref, kb16, vb16):
        b = pl.program_id(1)
        kb16[...] = k_ref[...].astype(BF16)
        vb16[...] = v_ref[...].astype(BF16)
        kc = kb16[0:C, :]
        vc = vb16[0:C, :]
        lane = lax.broadcasted_iota(jnp.int32, (GRID_W, LANES), 1)
        d_ref[...] = jnp.zeros(d_ref.shape, F32)

        @pl.when(b == 0)
        def _():
            db_ref[...] = jnp.zeros(db_ref.shape, F32)

        sel2 = _pair_select()

        def group(gi, carry):
            pre = []
            for u in range(NA_GROUP):
                r = gi * NA_GROUP + u
                bs = jnp.clip(r - kh // 2, 0, R - kh)
                dr0 = bs - r + (NA_KH - 1)
                q = q_ref[pl.ds(pl.multiple_of(C + r * GRID_W, GRID_W), GRID_W), :] * scale
                do = dl_ref[pl.ds(pl.multiple_of(r * GRID_W, GRID_W), GRID_W), :]
                q2 = jnp.where(sel2, jnp.concatenate([q, q], axis=0), 0.0).astype(BF16)
                do2 = jnp.where(sel2, jnp.concatenate([do, do], axis=0), 0.0).astype(BF16)
                band = pl.ds(pl.multiple_of(C + bs * GRID_W, GRID_W), nk)
                s_loc = _dot_nt(q2, kb16[band, :]) + _pair_bias(bias_ref, dr0, kh)
                s_ctx = _dot_nt(q2, kc)
                dp_loc = _dot_nt(do2, vb16[band, :])
                dp_ctx = _dot_nt(do2, vc)
                pre.append((r, dr0, band, q2, do2, s_loc, s_ctx, dp_loc, dp_ctx))
            mid = []
            for r, dr0, band, q2, do2, s_loc, s_ctx, dp_loc, dp_ctx in pre:
                p_loc, p_ctx, den = _na_softmax(s_loc, s_ctx)
                inv = 1.0 / den
                p_loc = p_loc * inv
                p_ctx = p_ctx * inv
                delta = (jnp.sum(p_loc * dp_loc, axis=-1, keepdims=True)
                         + jnp.sum(p_ctx * dp_ctx, axis=-1, keepdims=True))
                ds_loc = p_loc * (dp_loc - delta)
                ds_ctx = p_ctx * (dp_ctx - delta)
                mid.append((r, dr0, band, q2, do2, p_loc.astype(BF16), p_ctx.astype(BF16), ds_loc, ds_ctx))
            for r, dr0, band, q2, do2, pb_loc, pb_ctx, ds_loc, ds_ctx in mid:
                dsb_loc = ds_loc.astype(BF16)
                dsb_ctx = ds_ctx.astype(BF16)
                dq2 = _dot(dsb_loc, kb16[band, :]) + _dot(dsb_ctx, kc)
                d_ref[0, pl.ds(pl.multiple_of(C + r * GRID_W, GRID_W), GRID_W), :] = jnp.where(
                    lane < NA_DIM, dq2[:GRID_W], dq2[GRID_W:]) * scale
                d_ref[1, band, :] += _dot_tn(dsb_loc, q2)
                d_ref[2, band, :] += _dot_tn(pb_loc, do2)
                d_ref[1, 0:C, :] += _dot_tn(dsb_ctx, q2)
                d_ref[2, 0:C, :] += _dot_tn(pb_ctx, do2)
                for e in range(2):
                    for m in range(kh // 2):
                        db_ref[e, pl.ds(dr0 + 2 * m, 1)] += ds_loc[e * GRID_W:(e + 1) * GRID_W,
                                                                   m * LANES:(m + 1) * LANES].reshape(1, GRID_W, LANES)
            return carry

        lax.fori_loop(0, R // NA_GROUP, group, 0)

    def col(seg):
        return pl.BlockSpec((None, T, LANES), lambda p, b, seg=seg: (b, 0, base + seg * NA_PAIRS + p))

    return pl.pallas_call(
        body, name="na_bwd", grid=(NA_PAIRS, B),
        out_shape=(jax.ShapeDtypeStruct((B, 3, T, NA_WIDTH), F32),
                   jax.ShapeDtypeStruct((NA_HEADS, 2 * NA_KH - 2, GRID_W, LANES), F32)),
        in_specs=[col(0), col(1), col(2),
                  pl.BlockSpec((2, 2 * NA_KH - 2, GRID_W, LANES), lambda p, b: (p, 0, 0, 0)),
                  pl.BlockSpec((None, N, LANES), lambda p, b: (b, 0, p))],
        out_specs=(pl.BlockSpec((None, 3, T, LANES), lambda p, b: (b, 0, 0, p)),
                   pl.BlockSpec((2, 2 * NA_KH - 2, GRID_W, LANES), lambda p, b: (p, 0, 0, 0))),
        scratch_shapes=[pltpu.VMEM((T, LANES), BF16)] * 2,
        compiler_params=_params("parallel", "arbitrary"),
    )(proj, proj, proj, bias2, dlat)


def _split3(a):
    hi = a.astype(BF16)
    r1 = a - hi.astype(F32)
    mid = r1.astype(BF16)
    lo = (r1 - mid.astype(F32)).astype(BF16)
    return hi, mid, lo


def _rpb_reduce(dbias2, onehot2):
    rows = dbias2.shape[0] * dbias2.shape[1]
    flat = dbias2.reshape(rows, GRID_W * LANES)

    def body(a_ref, oh_ref, o_ref):
        hi, mid, lo = _split3(a_ref[...])
        oh = oh_ref[...]
        o_ref[...] = _dot(hi, oh) + _dot(mid, oh) + _dot(lo, oh)

    return pl.pallas_call(
        body, name="rpb_reduce", out_shape=jax.ShapeDtypeStruct((rows, LANES), F32),
        in_specs=[_vmem(), _vmem()], out_specs=_vmem(),
        compiler_params=pltpu.CompilerParams(vmem_limit_bytes=VMEM_LIMIT),
    )(flat, onehot2)


def _dense_core(lat_ret, lat_na, x, tgt, modl, g_post_mix, g_pre_mlp, g_post_mlp, w_out, w1, w2):
    B, N, D = x.shape
    F = w1.shape[1]
    mixw = w_out.shape[0]
    half = mixw // 2
    tm = _div_tile(N, 256, 16)
    nt = N // tm
    fc = _div_tile(F, 1024, LANES)

    def body(lr_ref, ln_ref, x_ref, t_ref, gt1_ref, sh2_ref, sc2_ref, gt2_ref, gpm_ref, gpre_ref, gpo_ref,
             wout_hbm, w1_hbm, w2_hbm,
             dy1_ref, dlr_ref, dln_ref, dmix_ref, h2_ref, a_ref, du_ref, dz_ref, red_ref,
             wout_v, w1_v, w2_v, u_s, sems):
        @pl.when((pl.program_id(0) == 0) & (pl.program_id(1) == 0))
        def _():
            cps = [pltpu.make_async_copy(wout_hbm, wout_v, sems.at[0]),
                   pltpu.make_async_copy(w1_hbm, w1_v, sems.at[1]),
                   pltpu.make_async_copy(w2_hbm, w2_v, sems.at[2])]
            for cp in cps:
                cp.start()
            for cp in cps:
                cp.wait()

        gt1 = gt1_ref[...]
        sh2 = sh2_ref[...]
        sc2 = sc2_ref[...]
        gt2 = gt2_ref[...]
        gpm = gpm_ref[...]
        gpre = gpre_ref[...]
        gpo = gpo_ref[...]

        def rowmean(a):
            return jnp.mean(a, axis=-1, keepdims=True)

        def colsum(a):
            return jnp.sum(a, axis=0, keepdims=True)

        mix = _dot(lr_ref[...], wout_v[0:half, :]) + _dot(ln_ref[...], wout_v[half:, :])
        x = x_ref[...]
        rm = lax.rsqrt(rowmean(mix * mix) + NORM_EPS)
        mh = mix * rm
        nm = mh * gpm
        y1 = x + gt1 * nm
        r1 = lax.rsqrt(rowmean(y1 * y1) + NORM_EPS)
        xh = y1 * r1
        n1 = xh * gpre
        h2b = (n1 * (1.0 + sc2) + sh2).astype(BF16)
        h2_ref[...] = h2b
        z = jnp.zeros((tm, D), F32)
        for c0 in range(0, F, fc):
            u = _dot(h2b, w1_v[:, c0:c0 + fc])
            u_s[:, c0:c0 + fc] = u
            ru = jnp.maximum(u, 0.0)
            ab = (ru * ru).astype(BF16)
            a_ref[:, c0:c0 + fc] = ab
            z = z + _dot(ab, w2_v[c0:c0 + fc, :])
        r2 = lax.rsqrt(rowmean(z * z) + NORM_EPS)
        zh = z * r2
        n2 = zh * gpo
        y2 = y1 + gt2 * n2
        err = y2 - t_ref[...]
        loss = 0.5 * jnp.sum(rowmean(err * err))
        dy2 = err * (1.0 / D)
        red_ref[2:3, :] = colsum(dy2 * n2)
        dn2 = dy2 * gt2
        red_ref[6:7, :] = colsum(dn2 * zh)
        dzh = dn2 * gpo
        dz = r2 * (dzh - zh * rowmean(dzh * zh))
        dzb = dz.astype(BF16)
        dz_ref[...] = dzb
        dh2 = jnp.zeros((tm, D), F32)
        for c0 in range(0, F, fc):
            da = _dot_nt(dzb, w2_v[c0:c0 + fc, :])
            dub = (da * (2.0 * jnp.maximum(u_s[:, c0:c0 + fc], 0.0))).astype(BF16)
            du_ref[:, c0:c0 + fc] = dub
            dh2 = dh2 + _dot_nt(dub, w1_v[:, c0:c0 + fc])
        red_ref[3:4, :] = colsum(dh2 * n1)
        red_ref[4:5, :] = colsum(dh2)
        dn1 = dh2 * (1.0 + sc2)
        red_ref[5:6, :] = colsum(dn1 * xh)
        dxh = dn1 * gpre
        dy1 = dy2 + r1 * (dxh - xh * rowmean(dxh * xh))
        dy1_ref[...] = dy1
        red_ref[0:1, :] = colsum(dy1 * nm)
        dnm = dy1 * gt1
        red_ref[1:2, :] = colsum(dnm * mh)
        dmh = dnm * gpm
        dmix = (rm * (dmh - mh * rowmean(dmh * mh))).astype(BF16)
        dmix_ref[...] = dmix
        dlr_ref[...] = _dot_nt(dmix, wout_v[0:half, :])
        dln_ref[...] = _dot_nt(dmix, wout_v[half:, :])
        red_ref[7:8, :] = jnp.zeros((1, D), F32) + loss

    def tok(w):
        return pl.BlockSpec((None, tm, w), lambda b, t: (b, t, 0))

    def mod(k):
        return pl.BlockSpec((None, None, 1, D), lambda b, t, k=k: (b, k, 0, 0))

    def vec():
        return pl.BlockSpec((1, D), lambda b, t: (0, 0))

    return pl.pallas_call(
        body, name="dense_core", grid=(B, nt),
        out_shape=(jax.ShapeDtypeStruct((B, N, D), F32), jax.ShapeDtypeStruct((B, N, half), F32),
                   jax.ShapeDtypeStruct((B, N, half), F32), jax.ShapeDtypeStruct((B, N, D), BF16),
                   jax.ShapeDtypeStruct((B, N, D), BF16), jax.ShapeDtypeStruct((B, N, F), BF16),
                   jax.ShapeDtypeStruct((B, N, F), BF16), jax.ShapeDtypeStruct((B, N, D), BF16),
                   jax.ShapeDtypeStruct((B, nt, SUBLANES, D), F32)),
        in_specs=[tok(half), tok(half), tok(D), tok(D), mod(2), mod(3), mod(4), mod(5), vec(), vec(), vec(),
                  _any(), _any(), _any()],
        out_specs=(tok(D), tok(half), tok(half), tok(D), tok(D), tok(F), tok(F), tok(D),
                   pl.BlockSpec((None, None, SUBLANES, D), lambda b, t: (b, t, 0, 0))),
        scratch_shapes=[pltpu.VMEM((mixw, D), BF16), pltpu.VMEM((D, F), BF16), pltpu.VMEM((F, D), BF16),
                        pltpu.VMEM((tm, F), F32), pltpu.SemaphoreType.DMA((3,))],
        compiler_params=_params("arbitrary", "arbitrary"),
    )(lat_ret, lat_na, x, tgt, modl, modl, modl, modl, g_post_mix, g_pre_mlp, g_post_mlp, w_out, w1, w2)


def _inproj_bwd(dret, dna, x_all, dy1, modl, g1, w_in, n_ctx, hosted):
    B, T, D = x_all.shape
    N = T - n_ctx
    tm = _div_tile(n_ctx, 256, 16)
    nct = n_ctx // tm
    nt = T // tm
    nseg_r = dret.shape[1]
    nseg_n = dna.shape[1]
    nw = w_in.shape[1]

    def body(*refs):
        seg_refs = refs[:nseg_r + nseg_n]
        x_ref, dy1_ref, sc_ref, g_ref, w_ref, dx_ref, red_ref = refs[nseg_r + nseg_n:]
        t = pl.program_id(1)
        dh = jnp.zeros((tm, D), F32)
        for s, ref in enumerate(seg_refs):
            dh = dh + _dot_nt(ref[...].astype(BF16), w_ref[:, s * SEG:(s + 1) * SEG])
        x = x_ref[...]
        g = g_ref[...]
        r = lax.rsqrt(jnp.mean(x * x, axis=-1, keepdims=True) + NORM_EPS)
        xh = x * r
        red_ref[0:1, :] = jnp.sum(dh, axis=0, keepdims=True)
        red_ref[1:2, :] = jnp.sum(dh * (xh * g), axis=0, keepdims=True)
        dn = dh * (1.0 + sc_ref[...])
        red_ref[2:3, :] = jnp.sum(dn * xh, axis=0, keepdims=True)
        red_ref[3:, :] = jnp.zeros((SUBLANES - 3, D), F32)
        dxh = dn * g
        dx = r * (dxh - xh * jnp.mean(dxh * xh, axis=-1, keepdims=True))
        dx_ref[...] = dx + jnp.where(t >= nct, dy1_ref[...], 0.0)

    def mrow(b, t):
        return jnp.where(t < nct, B, b)

    def seg(s):
        return pl.BlockSpec((None, None, tm, SEG), lambda b, t, s=s: (b, s, t, 0))

    def lat_tile():
        return pl.BlockSpec((None, tm, D), lambda b, t: (b, jnp.maximum(t - nct, 0), 0))

    return _call_hosting(
        body, hosted, name="inproj_bwd", grid=(B, nt),
        out_shape=(jax.ShapeDtypeStruct((B, N, D), F32), jax.ShapeDtypeStruct((B, nt, SUBLANES, D), F32)),
        in_specs=[seg(s) for s in range(nseg_r)] + [seg(s) for s in range(nseg_n)]
                 + [pl.BlockSpec((None, tm, D), lambda b, t: (b, t, 0)), lat_tile(),
                    pl.BlockSpec((None, None, 1, D), lambda b, t: (mrow(b, t), 1, 0, 0)),
                    pl.BlockSpec((1, D), lambda b, t: (0, 0)),
                    pl.BlockSpec((D, nw), lambda b, t: (0, 0))],
        out_specs=(lat_tile(), pl.BlockSpec((None, None, SUBLANES, D), lambda b, t: (b, t, 0, 0))),
        scratch_shapes=[], args=(*([dret] * nseg_r), *([dna] * nseg_n), x_all, dy1, modl, g1, w_in))


def _tn_matmul(lhs, rhs, name):
    B, S, T, W = lhs.shape
    nn = rhs.shape[-1]
    tk = _div_tile(T, 1024, LANES)
    bm = _div_tile(W, 1024, LANES)
    bn = _div_tile(nn, 1024, LANES)
    nkt = T // tk
    nk = B * nkt

    def body(l_ref, r_ref, o_ref, acc):
        k = pl.program_id(3)

        @pl.when(k == 0)
        def _():
            acc[...] = jnp.zeros(acc.shape, F32)

        acc[...] += _dot_tn(l_ref[...].astype(BF16), r_ref[...].astype(BF16))

        @pl.when(k == nk - 1)
        def _():
            o_ref[...] = acc[...].astype(BF16)

    nwb = W // bm
    return pl.pallas_call(
        functools.partial(body), name=name, grid=(S, nwb, nn // bn, nk),
        out_shape=jax.ShapeDtypeStruct((S * W, nn), BF16),
        in_specs=[pl.BlockSpec((None, None, tk, bm), lambda s, i, j, k: (k // nkt, s, k % nkt, i)),
                  pl.BlockSpec((None, tk, bn), lambda s, i, j, k: (k // nkt, k % nkt, j))],
        out_specs=pl.BlockSpec((bm, bn), lambda s, i, j, k: (s * nwb + i, j)),
        scratch_shapes=[pltpu.VMEM((bm, bn), F32)],
        compiler_params=_params("parallel", "parallel", "parallel", "arbitrary"),
    )(lhs, rhs)


def _sum_slots(buf, name):
    _, rows, cols = buf.shape
    tr = _div_tile(rows, 256, 2 * SUBLANES)

    def body(b_ref, o_ref):
        acc = b_ref[0].astype(F32)
        for k in range(1, N_DEV):
            acc = acc + b_ref[k].astype(F32)
        o_ref[...] = acc

    return pl.pallas_call(
        functools.partial(body), name=name, grid=(rows // tr,),
        out_shape=jax.ShapeDtypeStruct((rows, cols), F32),
        in_specs=[pl.BlockSpec((N_DEV, tr, cols), lambda i: (0, i, 0))],
        out_specs=pl.BlockSpec((tr, cols), lambda i: (i, 0)),
        compiler_params=_params("parallel"),
    )(buf)


def _small_ar(vec, dmods, silu_all, w_ada, c_ctx):
    rv = vec.shape[0]
    D = silu_all.shape[1]
    ncol = w_ada.shape[1]
    nm = dmods.shape[1]
    srows = silu_all.shape[0]

    def body(vec_ref, dm_ref, s_ref, w_ref, cc_ref, tot_ref, gb_ref, gw_ref, gc_ref,
             vbuf, mbuf, tbuf, dmx, send1, recv1, send2, recv2, send3, recv3):
        me, _ = _me_and_peers()
        vbuf[me] = vec_ref[...]
        mbuf[me] = dm_ref[...]
        _exchange(lambda p: vbuf.at[me], lambda p: vbuf.at[p], send1, recv1)
        _exchange(lambda p: mbuf.at[me], lambda p: mbuf.at[p], send2, recv2)
        tot = vbuf[0]
        msum = mbuf[0]
        for k in range(1, N_DEV):
            tot = tot + vbuf[k]
            msum = msum + mbuf[k]
        tot_ref[...] = tot
        gb_ref[...] = jnp.sum(msum, axis=0, keepdims=True)
        loc = pl.ds(pl.multiple_of(me * ncol, ncol), ncol)
        for k in range(N_DEV):
            dmx[k * SUBLANES:(k + 1) * SUBLANES, :] = mbuf[k, :, loc]
        cm = msum[2:3, :]
        mbuf[0, 2:3, :] = cm
        cm_loc = mbuf[0, 2:3, loc]
        dmx[N_DEV * SUBLANES:, :] = jnp.concatenate([cm_loc, jnp.zeros((SUBLANES - 1, ncol), F32)], axis=0)
        gw_ref[...] = _dot_tn(s_ref[...], dmx[...])
        tbuf[me] = _dot_nt(dmx[N_DEV * SUBLANES:, :], w_ref[...])
        _exchange(lambda p: tbuf.at[me], lambda p: tbuf.at[p], send3, recv3)
        tsum = tbuf[0]
        for k in range(1, N_DEV):
            tsum = tsum + tbuf[k]
        cc = cc_ref[...]
        sg = _sigmoid(cc)
        gc_ref[...] = tsum[0:1, :] * (sg * (1.0 + cc * (1.0 - sg)))

    return pl.pallas_call(
        body, name="small_ar",
        out_shape=(jax.ShapeDtypeStruct((rv, LANES), F32), jax.ShapeDtypeStruct((1, nm), F32),
                   jax.ShapeDtypeStruct((D, ncol), F32), jax.ShapeDtypeStruct((1, D), F32)),
        in_specs=[_vmem()] * 5, out_specs=(_vmem(),) * 4,
        scratch_shapes=[pltpu.VMEM((N_DEV, rv, LANES), F32), pltpu.VMEM((N_DEV, SUBLANES, nm), F32),
                        pltpu.VMEM((N_DEV, SUBLANES, D), F32), pltpu.VMEM((srows, ncol), F32)]
                       + [pltpu.SemaphoreType.DMA((N_DEV - 1,))] * 6,
        compiler_params=pltpu.CompilerParams(vmem_limit_bytes=VMEM_LIMIT),
    )(vec, dmods, silu_all, w_ada, c_ctx.reshape(1, D))


def _adamw(w, g, m, v, name):
    rows, cols = w.shape
    tr = _div_tile(rows, 256, SUBLANES) if rows * cols > 65536 else rows

    def body(w_ref, g_ref, m_ref, v_ref, d_ref, nm_ref, nv_ref):
        gv = g_ref[...]
        mn = ADAM_B1 * m_ref[...] + (1.0 - ADAM_B1) * gv
        vn = ADAM_B2 * v_ref[...] + (1.0 - ADAM_B2) * (gv * gv)
        m_hat = mn / (1.0 - ADAM_B1 ** ADAM_STEP)
        v_hat = vn / (1.0 - ADAM_B2 ** ADAM_STEP)
        d_ref[...] = -ADAM_LR * (m_hat / (jnp.sqrt(v_hat) + ADAM_EPS) + ADAM_WD * w_ref[...])
        nm_ref[...] = mn
        nv_ref[...] = vn

    spec = pl.BlockSpec((tr, cols), lambda i: (i, 0))
    return pl.pallas_call(
        functools.partial(body), name=name, grid=(rows // tr,),
        out_shape=(jax.ShapeDtypeStruct((rows, cols), F32),) * 3,
        in_specs=[spec] * 4, out_specs=(spec,) * 3,
        compiler_params=_params("parallel"),
    )(w, g, m, v)


def _rope_tables(n_ctx, n):
    n_freq = RET_DIM // 4
    inv = ROPE_BASE ** (-jnp.arange(n_freq, dtype=F32) / n_freq)
    tok = jnp.arange(n)
    pos_r = (tok // GRID_W).astype(F32)
    pos_c = (tok % GRID_W).astype(F32)
    ang_r = pos_r[:, None] * inv[None, :]
    ang_c = pos_c[:, None] * inv[None, :]
    cos = jnp.concatenate([jnp.cos(ang_r), jnp.cos(ang_r), jnp.cos(ang_c), jnp.cos(ang_c)], axis=-1)
    sin = jnp.concatenate([-jnp.sin(ang_r), jnp.sin(ang_r), -jnp.sin(ang_c), jnp.sin(ang_c)], axis=-1)
    cos = jnp.concatenate([jnp.ones((n_ctx, RET_DIM), F32), cos], axis=0)
    sin = jnp.concatenate([jnp.zeros((n_ctx, RET_DIM), F32), sin], axis=0)
    return cos, sin


def _na_tables():
    q = np.arange(GRID_W)[:, None]
    k = np.arange(GRID_W)[None, :]
    start = np.clip(q - NA_KW // 2, 0, GRID_W - NA_KW)
    valid = (k >= start) & (k < start + NA_KW)
    dc = np.clip(k - q + (NA_KW - 1), 0, 2 * NA_KW - 2)
    ncls = 2 * NA_KW - 1
    onehot = (dc[None] == np.arange(ncls)[:, None, None]) & valid[None]
    oh2 = np.zeros((GRID_W, LANES, LANES), np.float32)
    for c in range(ncls):
        oh2[:, :GRID_W, c] = onehot[c]
        oh2[:, GRID_W:, 32 + c] = onehot[c]
    return onehot.astype(np.float32), valid, oh2.reshape(GRID_W * LANES, LANES)


def _paired_bias(rpb, onehot, valid):
    t = jnp.einsum("hdc,cqk->hdqk", rpb, jnp.asarray(onehot), precision=lax.Precision.HIGHEST)
    t = jnp.where(jnp.asarray(valid)[None, None], t, NEG_INF)
    return jnp.concatenate([t[:, :-1], t[:, 1:]], axis=-1)


def kernel(x, c, ctx, c_ctx, w_ada, b_ada, g_pre_mix, g_post_mix, g_pre_mlp, g_post_mlp, w_in, ret_decay, ret_gn, na_rpb, w_out, w_mlp1, w_mlp2, loss_target, m_c_ctx, m_w_ada, m_b_ada, m_g_pre_mix, m_g_post_mix, m_g_pre_mlp, m_g_post_mlp, m_w_in, m_ret_decay, m_ret_gn, m_na_rpb, m_w_out, m_w_mlp1, m_w_mlp2, v_c_ctx, v_w_ada, v_b_ada, v_g_pre_mix, v_g_post_mix, v_g_pre_mlp, v_g_post_mlp, v_w_in, v_ret_decay, v_ret_gn, v_na_rpb, v_w_out, v_w_mlp1, v_w_mlp2):
    B, N, D = x.shape
    C = ctx.shape[1]
    T = C + N
    me = 4 * lax.axis_index("x") + 2 * lax.axis_index("y") + lax.axis_index("c")

    silu_all, mods_g, gin, wout_l, w1_l, w2_l = _mod_gather(c, c_ctx, w_ada[0], b_ada, w_in[0], w_out[0],
                                                           w_mlp1[0], w_mlp2[0])
    mods_full = mods_g.transpose(1, 0, 2).reshape(mods_g.shape[1], N_MOD * D)
    mine = lax.dynamic_slice_in_dim(mods_full, me * SUBLANES, B, axis=0)
    modl = jnp.concatenate([mine, mods_full[N_DEV * SUBLANES:N_DEV * SUBLANES + 1]], axis=0)
    modl = modl.reshape(B + 1, N_MOD, 1, D)
    win_b = gin.transpose(1, 0, 2).reshape(D, N_DEV * gin.shape[2])
    rin = win_b.shape[1] // N_DEV
    rout, c1, r2 = wout_l.shape[0], w1_l.shape[1], w2_l.shape[0]

    def rows_of(n):
        return lambda ref: _row_block(ref, n)

    def cols_of(n):
        return lambda ref: _col_block(ref, n)

    cos, sin = _rope_tables(C, N)
    onehot, valid, oh2 = _na_tables()
    bias2 = _paired_bias(na_rpb[0], onehot, valid)
    lg = jax.nn.log_sigmoid(ret_decay[0].astype(F32))

    x_all = jnp.concatenate([ctx, x], axis=1)
    h_all, proj, wout_b = _inproj_fwd(
        x_all, modl, g_pre_mix, win_b, C,
        _Hosted("gather", ALL_PEERS, [wout_l], [rows_of(rout)],
                [jax.ShapeDtypeStruct((N_DEV * rout, D), BF16)], True))
    o_ret, lat_ret, w1_part, w2_part = _ret_fwd(
        proj, cos, sin, lg, ret_gn, C,
        _Hosted("gather", SIBLING + ICI_SAME_CORE, [w1_l, w2_l], [cols_of(c1), rows_of(r2)],
                [jax.ShapeDtypeStruct((D, N_DEV * c1), BF16), jax.ShapeDtypeStruct((N_DEV * r2, D), BF16)], True))
    lat_na, w1_b, w2_b = _na_fwd(proj, bias2, C, _HostedRelay([w1_part, w2_part], [cols_of(c1), rows_of(r2)]))

    (dy1, dlat_ret, dlat_na, dmix, h2, act, du, dz, red_d) = _dense_core(
        lat_ret, lat_na, x, loss_target, modl, g_post_mix, g_pre_mlp, g_post_mlp, wout_b, w1_b, w2_b)

    gw_out_p = jnp.concatenate([_tn_matmul(lat_ret[:, None], dmix, "gw_out_ret"),
                                _tn_matmul(lat_na[:, None], dmix, "gw_out_na")], axis=0)
    gw1_p = _tn_matmul(h2[:, None], du, "gw_mlp1")
    gw2_p = _tn_matmul(act[:, None], dz, "gw_mlp2")

    dret, dgn_p, dlg_p, bout, b1, b2 = _ret_bwd(
        proj, cos, sin, lg, ret_gn, o_ret, dlat_ret, C,
        _Hosted("scatter", ALL_PEERS, [gw_out_p, gw1_p, gw2_p], [rows_of(rout), cols_of(c1), rows_of(r2)],
                [jax.ShapeDtypeStruct((N_DEV, rout, D), BF16), jax.ShapeDtypeStruct((N_DEV, D, c1), BF16),
                 jax.ShapeDtypeStruct((N_DEV, r2, D), BF16)], True))
    dna, dbias2 = _na_bwd(proj, bias2, dlat_na, C)
    gwin_t_p = jnp.concatenate([_tn_matmul(dret, h_all, "gw_in_ret"), _tn_matmul(dna, h_all, "gw_in_na")], axis=0)
    grad_x, red_i, bin_ = _inproj_bwd(
        dret, dna, x_all, dy1, modl, g_pre_mix, win_b, C,
        _Hosted("scatter", ALL_PEERS, [gwin_t_p], [rows_of(rin)],
                [jax.ShapeDtypeStruct((N_DEV, rin, D), BF16)], True))

    g_w_in = _sum_slots(bin_, "sum_w_in").T
    g_w_out = _sum_slots(bout, "sum_w_out")
    g_w1 = _sum_slots(b1, "sum_w_mlp1")
    g_w2 = _sum_slots(b2, "sum_w_mlp2")

    rd = red_d.sum(axis=1)[:, :, :]
    ri = red_i
    nct = ri.shape[1] * C // T
    ri_ctx = ri[:, :nct].sum(axis=(0, 1))
    ri_lat = ri[:, nct:].sum(axis=1)
    d_mods = jnp.concatenate([ri_lat[:, 0], ri_lat[:, 1], rd[:, 0], rd[:, 4], rd[:, 3], rd[:, 2]], axis=-1)
    d_cmods = jnp.concatenate([ri_ctx[0], ri_ctx[1], jnp.zeros(((N_MOD - 2) * D,), F32)])[None]
    dm_slot = jnp.concatenate([d_mods, d_cmods, jnp.zeros((SUBLANES - B - 1, N_MOD * D), F32)], axis=0)
    dg_pre_mix = ri_lat[:, 2].sum(axis=0) + ri_ctx[2]
    dg_post_mix = rd[:, 1].sum(axis=0)
    dg_pre_mlp = rd[:, 5].sum(axis=0)
    dg_post_mlp = rd[:, 6].sum(axis=0)
    loss_p = rd[:, 7, 0].sum()
    d_gn = dgn_p[:, 0].sum(axis=0)
    d_lg = dlg_p[:, :, :2, 0].sum(axis=0).T
    d_decay = d_lg * jax.nn.sigmoid(-ret_decay[0].astype(F32))
    rr = _rpb_reduce(dbias2, jnp.asarray(oh2, BF16)).reshape(NA_HEADS, 2 * NA_KH - 2, LANES)
    ncls = 2 * NA_KW - 1
    d_rpb = (jnp.pad(rr[:, :, :ncls], ((0, 0), (0, 1), (0, 0))) + jnp.pad(rr[:, :, 32:32 + ncls], ((0, 0), (1, 0), (0, 0))))
    d_rpb32 = jnp.pad(d_rpb, ((0, 0), (0, 0), (0, 32 - ncls)))
    pieces = [dg_pre_mix, dg_post_mix, dg_pre_mlp, dg_post_mlp, d_gn, d_rpb32.reshape(-1),
              jnp.pad(d_decay.reshape(-1), (0, LANES - d_decay.size)), jnp.full((LANES,), loss_p, F32)]
    vec = jnp.concatenate(pieces)
    pad = (-vec.shape[0]) % (SUBLANES * LANES)
    vec = jnp.pad(vec, (0, pad)).reshape(-1, LANES)
    tot, g_b_ada, g_w_ada, g_c_ctx = _small_ar(vec, dm_slot, silu_all, w_ada[0], c_ctx)
    flat = tot.reshape(-1)
    o0 = 0
    g_pre_mix_g = flat[o0:o0 + D]; o0 += D
    g_post_mix_g = flat[o0:o0 + D]; o0 += D
    g_pre_mlp_g = flat[o0:o0 + D]; o0 += D
    g_post_mlp_g = flat[o0:o0 + D]; o0 += D
    g_gn = flat[o0:o0 + RET_WIDTH]; o0 += RET_WIDTH
    nrpb = NA_HEADS * (2 * NA_KH - 1) * 32
    g_rpb = flat[o0:o0 + nrpb].reshape(NA_HEADS, 2 * NA_KH - 1, 32)[:, :, :ncls]; o0 += nrpb
    g_decay = flat[o0:o0 + 2 * RET_HEADS].reshape(2, RET_HEADS); o0 += LANES
    loss = flat[o0]

    grads = {
        "c_ctx": g_c_ctx.reshape(c_ctx.shape), "w_ada": g_w_ada[None], "b_ada": g_b_ada.reshape(b_ada.shape),
        "g_pre_mix": g_pre_mix_g[None], "g_post_mix": g_post_mix_g[None], "g_pre_mlp": g_pre_mlp_g[None],
        "g_post_mlp": g_post_mlp_g[None], "w_in": g_w_in[None], "ret_decay": g_decay[None], "ret_gn": g_gn[None],
        "na_rpb": g_rpb[None], "w_out": g_w_out[None], "w_mlp1": g_w1[None], "w_mlp2": g_w2[None],
    }
    weights = dict(c_ctx=c_ctx, w_ada=w_ada, b_ada=b_ada, g_pre_mix=g_pre_mix, g_post_mix=g_post_mix,
                   g_pre_mlp=g_pre_mlp, g_post_mlp=g_post_mlp, w_in=w_in, ret_decay=ret_decay, ret_gn=ret_gn,
                   na_rpb=na_rpb, w_out=w_out, w_mlp1=w_mlp1, w_mlp2=w_mlp2)
    m_in = dict(c_ctx=m_c_ctx, w_ada=m_w_ada, b_ada=m_b_ada, g_pre_mix=m_g_pre_mix, g_post_mix=m_g_post_mix,
                g_pre_mlp=m_g_pre_mlp, g_post_mlp=m_g_post_mlp, w_in=m_w_in, ret_decay=m_ret_decay,
                ret_gn=m_ret_gn, na_rpb=m_na_rpb, w_out=m_w_out, w_mlp1=m_w_mlp1, w_mlp2=m_w_mlp2)
    v_in = dict(c_ctx=v_c_ctx, w_ada=v_w_ada, b_ada=v_b_ada, g_pre_mix=v_g_pre_mix, g_post_mix=v_g_post_mix,
                g_pre_mlp=v_g_pre_mlp, g_post_mlp=v_g_post_mlp, w_in=v_w_in, ret_decay=v_ret_decay,
                ret_gn=v_ret_gn, na_rpb=v_na_rpb, w_out=v_w_out, w_mlp1=v_w_mlp1, w_mlp2=v_w_mlp2)
    names = list(weights)
    deltas, new_m, new_v = {}, {}, {}
    for n in names:
        shp = weights[n].shape
        two_d = (-1, shp[-1]) if len(shp) > 1 else (1, shp[0])
        d, nm, nv = _adamw(weights[n].reshape(two_d), grads[n].reshape(two_d), m_in[n].reshape(two_d),
                           v_in[n].reshape(two_d), "adamw_" + n)
        deltas[n], new_m[n], new_v[n] = d.reshape(shp), nm.reshape(shp), nv.reshape(shp)
    return (loss, grad_x, *[grads[n] for n in names], *[deltas[n] for n in names],
            *[new_m[n] for n in names], *[new_v[n] for n in names])
```

```python
import functools
import math

import numpy as np
import jax
import jax.numpy as jnp
from jax import lax
from jax.experimental import pallas as pl
from jax.experimental.pallas import tpu as pltpu

F32 = jnp.float32
BF16 = jnp.bfloat16
MESH = pl.DeviceIdType.MESH

N_DEV = 8
LANES = 128
SUBLANES = 8
VMEM_LIMIT = 60 * 1024 * 1024

GRID_W = 64
RET_HEADS = 4
RET_DIM = 128
RET_WIDTH = RET_HEADS * RET_DIM
NA_HEADS = 8
NA_DIM = 64
NA_WIDTH = NA_HEADS * NA_DIM
NA_PAIRS = NA_HEADS // 2
NA_KH = 8
NA_KW = 16
NA_GROUP = 4
SEG = 512
ROPE_BASE = 10000.0
NORM_EPS = 1e-6
NEG_INF = -1e30
N_MOD = 6

ADAM_LR = 0.001
ADAM_B1 = 0.9
ADAM_B2 = 0.999
ADAM_EPS = 1e-08
ADAM_WD = 0.01
ADAM_STEP = 10


def _dot(a, b):
    return lax.dot_general(a, b, (((1,), (0,)), ((), ())), preferred_element_type=F32)


def _dot_nt(a, b):
    return lax.dot_general(a, b, (((1,), (1,)), ((), ())), preferred_element_type=F32)


def _dot_tn(a, b):
    return lax.dot_general(a, b, (((0,), (0,)), ((), ())), preferred_element_type=F32)


def _sigmoid(x):
    return 1.0 / (1.0 + jnp.exp(-x))


def _div_tile(n, cap, mult):
    if n <= cap:
        return n
    for t in range(cap - cap % mult, 0, -mult):
        if n % t == 0:
            return t
    raise ValueError(f"no tile for {n}")


def _params(*sem):
    return pltpu.CompilerParams(dimension_semantics=tuple(sem) if sem else None,
                                vmem_limit_bytes=VMEM_LIMIT)


def _vmem():
    return pl.BlockSpec(memory_space=pltpu.VMEM)


def _any():
    return pl.BlockSpec(memory_space=pl.ANY)


def _me_and_peers():
    x, y, c = lax.axis_index("x"), lax.axis_index("y"), lax.axis_index("c")
    me = 4 * x + 2 * y + c
    peers = []
    for m in range(1, N_DEV):
        px = 1 - x if (m >> 2) & 1 else x
        py = 1 - y if (m >> 1) & 1 else y
        pc = 1 - c if m & 1 else c
        peers.append(((px, py, pc), 4 * px + 2 * py + pc))
    return me, peers


def _exchange(src_for, dst_from, send_sems, recv_sems):
    me, peers = _me_and_peers()
    sent = []
    for i, (dev, pid) in enumerate(peers):
        cp = pltpu.make_async_remote_copy(src_ref=src_for(pid), dst_ref=dst_from(me),
                                          send_sem=send_sems.at[i], recv_sem=recv_sems.at[i],
                                          device_id=dev, device_id_type=MESH)
        cp.start()
        sent.append(cp)
    for i, (dev, pid) in enumerate(peers):
        pltpu.make_async_remote_copy(src_ref=src_for(pid), dst_ref=dst_from(pid),
                                     send_sem=send_sems.at[i], recv_sem=recv_sems.at[i],
                                     device_id=dev, device_id_type=MESH).wait_recv()
    for cp in sent:
        cp.wait_send()


SIBLING = (1,)
ICI_SAME_CORE = (2, 4, 6)
ALL_PEERS = tuple(range(1, N_DEV))


def _remote(src, dst, send_sem, recv_sem, dev):
    return pltpu.make_async_remote_copy(src_ref=src, dst_ref=dst, send_sem=send_sem, recv_sem=recv_sem,
                                        device_id=dev, device_id_type=MESH)


def _push_start(items, masks, send_sems, recv_sems):
    me, peers = _me_and_peers()
    for k, (src_for, dst_from) in enumerate(items):
        for m in masks:
            dev, pid = peers[m - 1]
            _remote(src_for(pid), dst_from(me), send_sems.at[k, m - 1], recv_sems.at[k, m - 1], dev).start()


def _push_wait_recv(items, masks, send_sems, recv_sems):
    me, peers = _me_and_peers()
    for k, (src_for, dst_from) in enumerate(items):
        for m in masks:
            dev, pid = peers[m - 1]
            _remote(src_for(pid), dst_from(pid), send_sems.at[k, m - 1], recv_sems.at[k, m - 1], dev).wait_recv()


def _push_wait_send(items, masks, send_sems, recv_sems):
    me, peers = _me_and_peers()
    for k, (src_for, dst_from) in enumerate(items):
        for m in masks:
            dev, pid = peers[m - 1]
            _remote(src_for(pid), dst_from(me), send_sems.at[k, m - 1], recv_sems.at[k, m - 1], dev).wait_send()


def _forward_start(items, send_sems, recv_sems):
    me, peers = _me_and_peers()
    sib = peers[0][0]
    for k, (blk_in, blk_out) in enumerate(items):
        for j, m in enumerate(ICI_SAME_CORE):
            pid = peers[m - 1][1]
            _remote(blk_in(pid), blk_out(pid), send_sems.at[k, j], recv_sems.at[k, j], sib).start()


def _forward_wait(items, send_sems, recv_sems):
    me, peers = _me_and_peers()
    sib = peers[0][0]
    for k, (blk_in, blk_out) in enumerate(items):
        for j, m in enumerate(ICI_SAME_CORE):
            got = peers[(m | 1) - 1][1]
            _remote(blk_in(got), blk_out(got), send_sems.at[k, j], recv_sems.at[k, j], sib).wait_recv()
    for k, (blk_in, blk_out) in enumerate(items):
        for j, m in enumerate(ICI_SAME_CORE):
            pid = peers[m - 1][1]
            _remote(blk_in(pid), blk_out(pid), send_sems.at[k, j], recv_sems.at[k, j], sib).wait_send()


def _mod_gather(c, c_ctx, w_ada, b_ada, w_in, w_out, w1, w2):
    B, D = c.shape
    ncol = w_ada.shape[1]
    rows = SUBLANES * N_DEV + SUBLANES

    def body(c_ref, cc_ref, w_ref, b_ref, win_ref, wout_ref, w1_ref, w2_ref,
             s_ref, m_ref, gin_ref, wout_b, w1_b, w2_b,
             win_b, send1, recv1, send2, recv2, wsend, wrecv, fsend, frecv, lsem):
        me, _ = _me_and_peers()
        win_b[...] = win_ref[...].astype(BF16)
        gather = [(lambda p: win_b, lambda p: gin_ref.at[p])]
        own = pltpu.make_async_copy(win_b, gin_ref.at[me], lsem.at[0])
        own.start()
        _push_start(gather, SIBLING + ICI_SAME_CORE, wsend, wrecv)
        wout_b[...] = wout_ref[...].astype(BF16)
        w1_b[...] = w1_ref[...].astype(BF16)
        w2_b[...] = w2_ref[...].astype(BF16)
        cv = c_ref[...]
        slot = jnp.concatenate([cv * _sigmoid(cv), jnp.zeros((SUBLANES - B, D), F32)], axis=0)
        my_rows = pl.ds(pl.multiple_of(me * SUBLANES, SUBLANES), SUBLANES)
        s_ref[my_rows, :] = slot
        ccv = cc_ref[...]
        s_ref[SUBLANES * N_DEV:, :] = jnp.concatenate(
            [ccv * _sigmoid(ccv), jnp.zeros((SUBLANES - 1, D), F32)], axis=0)

        def rows_of(p):
            return s_ref.at[pl.ds(pl.multiple_of(p * SUBLANES, SUBLANES), SUBLANES), :]

        _exchange(lambda p: rows_of(me), rows_of, send1, recv1)
        b_loc = b_ref[:, pl.ds(pl.multiple_of(me * ncol, ncol), ncol)]
        m_ref[me] = _dot(s_ref[...], w_ref[...]) + b_loc
        _exchange(lambda p: m_ref.at[me], lambda p: m_ref.at[p], send2, recv2)
        _push_wait_recv(gather, ICI_SAME_CORE, wsend, wrecv)
        relay = [(lambda p: gin_ref.at[p], lambda p: gin_ref.at[p])]
        _forward_start(relay, fsend, frecv)
        _push_wait_recv(gather, SIBLING, wsend, wrecv)
        _forward_wait(relay, fsend, frecv)
        _push_wait_send(gather, SIBLING + ICI_SAME_CORE, wsend, wrecv)
        own.wait()

    return pl.pallas_call(
        body, name="mod_gather",
        out_shape=(jax.ShapeDtypeStruct((rows, D), F32), jax.ShapeDtypeStruct((N_DEV, rows, ncol), F32),
                   jax.ShapeDtypeStruct((N_DEV,) + w_in.shape, BF16),
                   jax.ShapeDtypeStruct(w_out.shape, BF16), jax.ShapeDtypeStruct(w1.shape, BF16),
                   jax.ShapeDtypeStruct(w2.shape, BF16)),
        in_specs=[_vmem()] * 8, out_specs=(_vmem(), _vmem(), _any(), _vmem(), _vmem(), _vmem()),
        scratch_shapes=[pltpu.VMEM(w_in.shape, BF16)] + [pltpu.SemaphoreType.DMA((N_DEV - 1,))] * 4
                       + [pltpu.SemaphoreType.DMA((1, N_DEV - 1))] * 2 + [pltpu.SemaphoreType.DMA((1, 3))] * 2
                       + [pltpu.SemaphoreType.DMA((1,))],
        compiler_params=pltpu.CompilerParams(vmem_limit_bytes=VMEM_LIMIT),
    )(c, c_ctx.reshape(1, D), w_ada, b_ada, w_in, w_out, w1, w2)


def _row_block(ref, rows):
    return lambda p: ref.at[pl.ds(pl.multiple_of(p * rows, 2 * SUBLANES), rows), :]


def _col_block(ref, cols):
    return lambda p: ref.at[:, pl.ds(pl.multiple_of(p * cols, LANES), cols)]


def _slot(ref):
    return lambda p: ref.at[p]


class _Hosted:
    def __init__(self, kind, masks, operands, block_of, out_shapes, with_own):
        self.kind, self.masks, self.operands = kind, masks, list(operands)
        self.block_of, self.out_shapes, self.with_own = block_of, list(out_shapes), with_own
        self.n = len(self.operands)

    def scratch(self):
        return [pltpu.SemaphoreType.DMA((self.n, N_DEV - 1)), pltpu.SemaphoreType.DMA((self.n, N_DEV - 1)),
                pltpu.SemaphoreType.DMA((self.n,))]

    def _items(self, in_refs, out_refs):
        items = []
        for k in range(self.n):
            if self.kind == "gather":
                items.append((lambda p, k=k: in_refs[k], self.block_of[k](out_refs[k])))
            else:
                items.append((self.block_of[k](in_refs[k]), _slot(out_refs[k])))
        return items

    def _own(self, in_refs, out_refs, lsem):
        me, _ = _me_and_peers()
        items = self._items(in_refs, out_refs)
        return [pltpu.make_async_copy(src_for(me), dst_from(me), lsem.at[k])
                for k, (src_for, dst_from) in enumerate(items)]

    def start(self, in_refs, out_refs, sems):
        send, recv, lsem = sems
        if self.with_own:
            for cp in self._own(in_refs, out_refs, lsem):
                cp.start()
        _push_start(self._items(in_refs, out_refs), self.masks, send, recv)

    def wait(self, in_refs, out_refs, sems):
        send, recv, lsem = sems
        items = self._items(in_refs, out_refs)
        _push_wait_recv(items, self.masks, send, recv)
        _push_wait_send(items, self.masks, send, recv)
        if self.with_own:
            for cp in self._own(in_refs, out_refs, lsem):
                cp.wait()


class _HostedRelay:
    def __init__(self, arrays, block_of):
        self.operands, self.block_of = list(arrays), block_of
        self.out_shapes = [jax.ShapeDtypeStruct(a.shape, a.dtype) for a in arrays]
        self.n = len(self.operands)

    def scratch(self):
        return [pltpu.SemaphoreType.DMA((self.n, 3)), pltpu.SemaphoreType.DMA((self.n, 3))]

    def _items(self, in_refs, out_refs):
        return [(self.block_of[k](in_refs[k]), self.block_of[k](out_refs[k])) for k in range(self.n)]

    def start(self, in_refs, out_refs, sems):
        _forward_start(self._items(in_refs, out_refs), *sems)

    def wait(self, in_refs, out_refs, sems):
        _forward_wait(self._items(in_refs, out_refs), *sems)


def _call_hosting(body, hosted, *, name, grid, out_shape, in_specs, out_specs, scratch_shapes, args):
    n_in, n_out, n_scr = len(in_specs), len(out_shape), len(scratch_shapes)
    hn = sum(hs.n for hs in hosted)
    n_sem = [len(hs.scratch()) for hs in hosted]

    def wrapped(*refs):
        ins = refs[:n_in]
        h_in = refs[n_in:n_in + hn]
        outs = refs[n_in + hn:n_in + hn + n_out]
        h_out = refs[n_in + hn + n_out:n_in + 2 * hn + n_out]
        scr = refs[n_in + 2 * hn + n_out:n_in + 2 * hn + n_out + n_scr]
        sems = refs[n_in + 2 * hn + n_out + n_scr:]
        ids = [pl.program_id(i) for i in range(len(grid))]
        first = functools.reduce(jnp.logical_and, [i == 0 for i in ids])
        last = functools.reduce(jnp.logical_and, [i == g - 1 for i, g in zip(ids, grid)])
        parts, o0, s0 = [], 0, 0
        for hs, ns in zip(hosted, n_sem):
            parts.append((hs, h_in[o0:o0 + hs.n], h_out[o0:o0 + hs.n], sems[s0:s0 + ns]))
            o0 += hs.n
            s0 += ns

        @pl.when(first)
        def _():
            for hs, hi, ho, se in parts:
                hs.start(hi, ho, se)

        body(*ins, *outs, *scr)

        @pl.when(last)
        def _():
            for hs, hi, ho, se in parts:
                hs.wait(hi, ho, se)

    aliases, o0 = {}, 0
    for hs in hosted:
        if isinstance(hs, _HostedRelay):
            aliases.update({n_in + o0 + k: n_out + o0 + k for k in range(hs.n)})
        o0 += hs.n
    return pl.pallas_call(
        wrapped, name=name, grid=grid,
        out_shape=tuple(out_shape) + tuple(s for hs in hosted for s in hs.out_shapes),
        in_specs=list(in_specs) + [_any()] * hn,
        out_specs=tuple(out_specs) + (_any(),) * hn,
        scratch_shapes=list(scratch_shapes) + [s for hs in hosted for s in hs.scratch()],
        input_output_aliases=aliases,
        compiler_params=_params(*(("arbitrary",) * len(grid))),
    )(*args, *[a for hs in hosted for a in hs.operands])


def _inproj_fwd(x_all, modl, g1, w_in, n_ctx, hosted):
    B, T, D = x_all.shape
    nw = w_in.shape[1]
    tm = _div_tile(n_ctx, 256, 16)
    nct = n_ctx // tm

    def body(x_ref, sh_ref, sc_ref, g_ref, w_ref, h_ref, p_ref):
        x = x_ref[...]
        r = lax.rsqrt(jnp.mean(x * x, axis=-1, keepdims=True) + NORM_EPS)
        h = ((x * r) * g_ref[...]) * (1.0 + sc_ref[...]) + sh_ref[...]
        hb = h.astype(BF16)
        h_ref[...] = hb
        p_ref[...] = _dot(hb, w_ref[...])

    def mrow(b, t):
        return jnp.where(t < nct, B, b)

    return _call_hosting(
        body, hosted, name="inproj_fwd", grid=(B, T // tm),
        out_shape=(jax.ShapeDtypeStruct((B, T, D), BF16), jax.ShapeDtypeStruct((B, T, nw), F32)),
        in_specs=[pl.BlockSpec((None, tm, D), lambda b, t: (b, t, 0)),
                  pl.BlockSpec((None, None, 1, D), lambda b, t: (mrow(b, t), 0, 0, 0)),
                  pl.BlockSpec((None, None, 1, D), lambda b, t: (mrow(b, t), 1, 0, 0)),
                  pl.BlockSpec((1, D), lambda b, t: (0, 0)),
                  pl.BlockSpec((D, nw), lambda b, t: (0, 0))],
        out_specs=(pl.BlockSpec((None, tm, D), lambda b, t: (b, t, 0)),
                   pl.BlockSpec((None, tm, nw), lambda b, t: (b, t, 0))),
        scratch_shapes=[], args=(x_all, modl, modl, g1, w_in))


def _swap32(x):
    lane = lax.broadcasted_iota(jnp.int32, x.shape, 1)
    return jnp.where((lane % 64) < 32, pltpu.roll(x, 96, 1), pltpu.roll(x, 32, 1))


def _rope(x, cos, sin):
    return x * cos + _swap32(x) * sin


def _unrope(dy, cos, sin):
    return dy * cos + _swap32(dy * sin)


def _ret_weights(lgf, lgb, dist):
    return jnp.exp(jnp.where(dist >= 0.0, lgf * dist, -lgb * dist))


class _RetDecay:
    def __init__(self, lgf, lgb, rows):
        r = lax.broadcasted_iota(jnp.int32, (rows, RET_DIM), 0).astype(F32)
        self.head = r + 1.0
        self.tail = (rows - 1.0) - r
        self.q_f = jnp.exp(lgf * self.head)
        self.k_f = jnp.exp(lgf * self.tail)
        self.q_b = jnp.exp(lgb * self.tail)
        self.k_b = jnp.exp(lgb * self.head)


def _ret_states(kf32, vs, lgf, lgb, C, c, nt, hf, hb, hfa=None, hba=None):
    dec = _RetDecay(lgf, lgb, c)
    dec_c = _RetDecay(lgf, lgb, C)
    step_f = jnp.exp(jnp.zeros((RET_DIM, RET_DIM), F32) + lgf * c)
    step_b = jnp.exp(jnp.zeros((RET_DIM, RET_DIM), F32) + lgb * c)

    def upd(rows, kdec):
        return _dot_tn((kf32[rows, :] * kdec).astype(BF16), vs[rows, :])

    def lat(t):
        return slice(C + t * c, C + (t + 1) * c)

    state = upd(slice(0, C), dec_c.k_f)
    aged = jnp.zeros_like(state)
    for t in range(nt):
        hf[t] = state.astype(BF16)
        if hfa is not None:
            hfa[t] = aged
        if t < nt - 1:
            aged = step_f * (aged + c * state)
            state = step_f * state + upd(lat(t), dec.k_f)
    state = upd(slice(0, C), dec_c.k_b)
    aged = jnp.zeros_like(state)
    for t in range(nt - 1, -1, -1):
        hb[t] = state.astype(BF16)
        if hba is not None:
            hba[t] = aged
        if t > 0:
            aged = step_b * (aged + c * state)
            state = step_b * state + upd(lat(t), dec.k_b)
    return dec, dec_c, step_f, step_b


def _ret_fwd(proj, cos, sin, lg, gn, n_ctx, hosted):
    B, T, _ = proj.shape
    C = n_ctx
    N = T - C
    c = _div_tile(N, 256, 16)
    nt = N // c
    scale = RET_DIM ** -0.5

    def body(lg_ref, q_ref, k_ref, v_ref, g_ref, cos_ref, sin_ref, gn_ref, o_ref, lat_ref, qs, ks, vs, kf32, hf, hb):
        h = pl.program_id(1)
        lgf = lg_ref[0, h]
        lgb = lg_ref[1, h]
        for rows in [slice(0, C)] + [slice(C + t * c, C + (t + 1) * c) for t in range(nt)]:
            cosb = cos_ref[rows, :]
            sinb = sin_ref[rows, :]
            qs[rows, :] = (_rope(q_ref[rows, :], cosb, sinb) * scale).astype(BF16)
            kr = _rope(k_ref[rows, :], cosb, sinb)
            kf32[rows, :] = kr
            ks[rows, :] = kr.astype(BF16)
            vs[rows, :] = v_ref[rows, :].astype(BF16)
        gnv = gn_ref[...]
        dec, _, _, _ = _ret_states(kf32, vs, lgf, lgb, C, c, nt, hf, hb)
        rc = (lax.broadcasted_iota(jnp.int32, (c, c), 0) - lax.broadcasted_iota(jnp.int32, (c, c), 1)).astype(F32)
        w_diag = _ret_weights(lgf, lgb, rc)
        for t in range(nt):
            rows = slice(C + t * c, C + (t + 1) * c)
            qt = qs[rows, :]
            s = _dot_nt(qt, ks[rows, :])
            o = (_dot((s * w_diag).astype(BF16), vs[rows, :])
                 + dec.q_f * _dot(qt, hf[t]) + dec.q_b * _dot(qt, hb[t]))
            o_ref[t * c:(t + 1) * c, :] = o
            mu = jnp.mean(o, axis=-1, keepdims=True)
            oc = o - mu
            var = jnp.mean(oc * oc, axis=-1, keepdims=True)
            yh = oc * lax.rsqrt(var + NORM_EPS)
            g = g_ref[rows, :]
            lat_ref[t * c:(t + 1) * c, :] = ((yh * gnv) * (g * _sigmoid(g))).astype(BF16)

    def col(seg):
        return pl.BlockSpec((None, T, RET_DIM), lambda b, h, seg=seg: (b, 0, seg * RET_HEADS + h))

    return _call_hosting(
        body, hosted, name="ret_fwd", grid=(B, RET_HEADS),
        out_shape=(jax.ShapeDtypeStruct((B, N, RET_WIDTH), F32), jax.ShapeDtypeStruct((B, N, RET_WIDTH), BF16)),
        in_specs=[pl.BlockSpec(memory_space=pltpu.SMEM), col(0), col(1), col(2), col(3),
                  pl.BlockSpec((T, RET_DIM), lambda b, h: (0, 0)), pl.BlockSpec((T, RET_DIM), lambda b, h: (0, 0)),
                  pl.BlockSpec((1, RET_DIM), lambda b, h: (0, h))],
        out_specs=(pl.BlockSpec((None, N, RET_DIM), lambda b, h: (b, 0, h)),
                   pl.BlockSpec((None, N, RET_DIM), lambda b, h: (b, 0, h))),
        scratch_shapes=[pltpu.VMEM((T, RET_DIM), BF16)] * 3 + [pltpu.VMEM((T, RET_DIM), F32)]
                       + [pltpu.VMEM((nt, RET_DIM, RET_DIM), BF16)] * 2,
        args=(lg, proj, proj, proj, proj, cos, sin, gn))


def _ret_bwd(proj, cos, sin, lg, gn, o, dlat, n_ctx, hosted):
    B, T, _ = proj.shape
    C = n_ctx
    N = T - C
    c = _div_tile(N, 256, 16)
    nt = N // c
    scale = RET_DIM ** -0.5

    def lat(t):
        return slice(C + t * c, C + (t + 1) * c)

    def body(lg_ref, q_ref, k_ref, v_ref, g_ref, cos_ref, sin_ref, gn_ref, o_ref, dl_ref,
             d_ref, dgn_ref, dlg_ref, qs, ks, vs, dos, qf32, kf32, hf, hb, hfa, hba, gf_s, gb_s):
        h = pl.program_id(1)
        lgf = lg_ref[0, h]
        lgb = lg_ref[1, h]
        gnv = gn_ref[...]

        def fold(a):
            return jnp.sum(a.reshape(a.shape[0] // SUBLANES, SUBLANES, a.shape[1]), axis=0)

        for rows in [slice(0, C)] + [lat(t) for t in range(nt)]:
            cosb = cos_ref[rows, :]
            sinb = sin_ref[rows, :]
            qr = _rope(q_ref[rows, :], cosb, sinb) * scale
            qf32[rows, :] = qr
            qs[rows, :] = qr.astype(BF16)
            kr = _rope(k_ref[rows, :], cosb, sinb)
            kf32[rows, :] = kr
            ks[rows, :] = kr.astype(BF16)
            vs[rows, :] = v_ref[rows, :].astype(BF16)

        dgn = jnp.zeros((1, RET_DIM), F32)
        for t in range(nt):
            lrows = slice(t * c, (t + 1) * c)
            ov = o_ref[lrows, :]
            mu = jnp.mean(ov, axis=-1, keepdims=True)
            oc = ov - mu
            var = jnp.mean(oc * oc, axis=-1, keepdims=True)
            rstd = lax.rsqrt(var + NORM_EPS)
            yh = oc * rstd
            g = g_ref[lat(t), :]
            sg = _sigmoid(g)
            dl = dl_ref[lrows, :]
            d_ref[3, lat(t), :] = dl * (yh * gnv) * (sg * (1.0 + g * (1.0 - sg)))
            dls = dl * (g * sg)
            dgn = dgn + jnp.sum(dls * yh, axis=0, keepdims=True)
            dyh = dls * gnv
            do = rstd * (dyh - jnp.mean(dyh, axis=-1, keepdims=True)
                         - yh * jnp.mean(dyh * yh, axis=-1, keepdims=True))
            dos[lrows, :] = do.astype(BF16)
        dgn_ref[...] = jnp.concatenate([dgn, jnp.zeros((SUBLANES - 1, RET_DIM), F32)], axis=0)
        d_ref[3, 0:C, :] = jnp.zeros((C, RET_DIM), F32)
        d_ref[0, 0:C, :] = jnp.zeros((C, RET_DIM), F32)

        dec, dec_c, step_f, step_b = _ret_states(kf32, vs, lgf, lgb, C, c, nt, hf, hb, hfa, hba)

        def zmat(t, qdec):
            return _dot_tn((qf32[lat(t), :] * qdec).astype(BF16), dos[t * c:(t + 1) * c, :])

        acc3f = jnp.zeros((RET_DIM, RET_DIM), F32)
        acc3b = jnp.zeros((RET_DIM, RET_DIM), F32)
        state = jnp.zeros((RET_DIM, RET_DIM), F32)
        for t in range(nt - 1, -1, -1):
            gf_s[t] = state.astype(BF16)
            z = zmat(t, dec.q_f)
            acc3f = acc3f + hfa[t] * z
            state = step_f * state + z
        gctx_f = state.astype(BF16)
        state = jnp.zeros((RET_DIM, RET_DIM), F32)
        for t in range(nt):
            gb_s[t] = state.astype(BF16)
            z = zmat(t, dec.q_b)
            acc3b = acc3b + hba[t] * z
            state = step_b * state + z
        gctx_b = state.astype(BF16)

        rc = (lax.broadcasted_iota(jnp.int32, (c, c), 0) - lax.broadcasted_iota(jnp.int32, (c, c), 1)).astype(F32)
        w_diag = _ret_weights(lgf, lgb, rc)
        wg_f = jnp.where(rc >= 0.0, w_diag * rc, 0.0)
        wg_b = jnp.where(rc < 0.0, -w_diag * rc, 0.0)
        accf = jnp.zeros((SUBLANES, RET_DIM), F32)
        accb = jnp.zeros((SUBLANES, RET_DIM), F32)
        gdf = jnp.zeros((SUBLANES, c), F32)
        gdb = jnp.zeros((SUBLANES, c), F32)
        for t in range(nt):
            rows = lat(t)
            qt = qs[rows, :]
            kt = ks[rows, :]
            vt = vs[rows, :]
            dot = dos[t * c:(t + 1) * c, :]
            s = _dot_nt(qt, kt)
            dp = _dot_nt(dot, vt)
            dv = _dot_tn((s * w_diag).astype(BF16), dot)
            ds = (dp * w_diag).astype(BF16)
            dq = _dot(ds, kt)
            dk = _dot_tn(ds, qt)
            gs = dp * s
            gdf = gdf + fold(gs * wg_f)
            gdb = gdb + fold(gs * wg_b)
            qv = qf32[rows, :]
            kv = kf32[rows, :]
            dq_f = dec.q_f * _dot_nt(dot, hf[t])
            dq_b = dec.q_b * _dot_nt(dot, hb[t])
            dk_f = dec.k_f * _dot_nt(vt, gf_s[t])
            dk_b = dec.k_b * _dot_nt(vt, gb_s[t])
            accf = accf + fold(dec.head * dq_f * qv) + fold(dec.tail * dk_f * kv)
            accb = accb + fold(dec.tail * dq_b * qv) + fold(dec.head * dk_b * kv)
            dv = dv + dec.k_f * _dot(kt, gf_s[t]) + dec.k_b * _dot(kt, gb_s[t])
            cosb = cos_ref[rows, :]
            sinb = sin_ref[rows, :]
            d_ref[0, rows, :] = _unrope((dq + dq_f + dq_b) * scale, cosb, sinb)
            d_ref[1, rows, :] = _unrope(dk + dk_f + dk_b, cosb, sinb)
            d_ref[2, rows, :] = dv
        kc = ks[0:C, :]
        vc = vs[0:C, :]
        kcv = kf32[0:C, :]
        dkc_f = dec_c.k_f * _dot_nt(vc, gctx_f)
        dkc_b = dec_c.k_b * _dot_nt(vc, gctx_b)
        accf = accf + fold(dec_c.tail * dkc_f * kcv)
        accb = accb + fold(dec_c.head * dkc_b * kcv)
        d_ref[1, 0:C, :] = dkc_f + dkc_b
        d_ref[2, 0:C, :] = dec_c.k_f * _dot(kc, gctx_f) + dec_c.k_b * _dot(kc, gctx_b)
        gf = jnp.sum(gdf) + jnp.sum(accf) + jnp.sum(acc3f)
        gb = jnp.sum(gdb) + jnp.sum(accb) + jnp.sum(acc3b)
        row = lax.broadcasted_iota(jnp.int32, (SUBLANES, LANES), 0)
        dlg_ref[...] = jnp.where(row == 0, gf, jnp.where(row == 1, gb, 0.0))

    def col(seg):
        return pl.BlockSpec((None, T, RET_DIM), lambda b, h, seg=seg: (b, 0, seg * RET_HEADS + h))

    return _call_hosting(
        body, hosted, name="ret_bwd", grid=(B, RET_HEADS),
        out_shape=(jax.ShapeDtypeStruct((B, 4, T, RET_WIDTH), F32),
                   jax.ShapeDtypeStruct((B, SUBLANES, RET_WIDTH), F32),
                   jax.ShapeDtypeStruct((B, RET_HEADS, SUBLANES, LANES), F32)),
        in_specs=[pl.BlockSpec(memory_space=pltpu.SMEM), col(0), col(1), col(2), col(3),
                  pl.BlockSpec((T, RET_DIM), lambda b, h: (0, 0)), pl.BlockSpec((T, RET_DIM), lambda b, h: (0, 0)),
                  pl.BlockSpec((1, RET_DIM), lambda b, h: (0, h)),
                  pl.BlockSpec((None, N, RET_DIM), lambda b, h: (b, 0, h)),
                  pl.BlockSpec((None, N, RET_DIM), lambda b, h: (b, 0, h))],
        out_specs=(pl.BlockSpec((None, 4, T, RET_DIM), lambda b, h: (b, 0, 0, h)),
                   pl.BlockSpec((None, SUBLANES, RET_DIM), lambda b, h: (b, 0, h)),
                   pl.BlockSpec((None, None, SUBLANES, LANES), lambda b, h: (b, h, 0, 0))),
        scratch_shapes=[pltpu.VMEM((T, RET_DIM), BF16)] * 3 + [pltpu.VMEM((N, RET_DIM), BF16)]
                       + [pltpu.VMEM((T, RET_DIM), F32)] * 2
                       + [pltpu.VMEM((nt, RET_DIM, RET_DIM), BF16)] * 2 + [pltpu.VMEM((nt, RET_DIM, RET_DIM), F32)] * 2
                       + [pltpu.VMEM((nt, RET_DIM, RET_DIM), BF16)] * 2,
        args=(lg, proj, proj, proj, proj, cos, sin, gn, o, dlat))


def _na_geometry(rows):
    kh = min(NA_KH, rows)
    return kh, kh * GRID_W


def _pair_select():
    lane = lax.broadcasted_iota(jnp.int32, (2 * GRID_W, LANES), 1)
    row = lax.broadcasted_iota(jnp.int32, (2 * GRID_W, LANES), 0)
    return (lane >= NA_DIM) == (row >= GRID_W)


def _pair_bias(bias_ref, dr0, kh):
    return jnp.concatenate(
        [jnp.concatenate([bias_ref[e, pl.ds(dr0 + 2 * m, 1)].reshape(GRID_W, LANES) for m in range(kh // 2)], axis=1)
         for e in range(2)], axis=0)


def _na_softmax(s_loc, s_ctx):
    mx = jnp.maximum(jnp.max(s_loc, axis=-1, keepdims=True), jnp.max(s_ctx, axis=-1, keepdims=True))
    p_loc = jnp.exp(s_loc - mx)
    p_ctx = jnp.exp(s_ctx - mx)
    den = jnp.sum(p_loc, axis=-1, keepdims=True) + jnp.sum(p_ctx, axis=-1, keepdims=True)
    return p_loc, p_ctx, den


def _na_fwd(proj, bias2, n_ctx, hosted):
    B, T, _ = proj.shape
    C = n_ctx
    N = T - C
    R = N // GRID_W
    kh, nk = _na_geometry(R)
    scale = NA_DIM ** -0.5
    base = (4 * RET_WIDTH) // LANES

    def body(q_ref, k_ref, v_ref, bias_ref, out_ref, kb16, vb16):
        kb16[...] = k_ref[...].astype(BF16)
        vb16[...] = v_ref[...].astype(BF16)
        kc = kb16[0:C, :]
        vc = vb16[0:C, :]
        lane = lax.broadcasted_iota(jnp.int32, (GRID_W, LANES), 1)
        sel2 = _pair_select()

        def group(gi, carry):
            pre = []
            for u in range(NA_GROUP):
                r = gi * NA_GROUP + u
                bs = jnp.clip(r - kh // 2, 0, R - kh)
                dr0 = bs - r + (NA_KH - 1)
                q = q_ref[pl.ds(pl.multiple_of(C + r * GRID_W, GRID_W), GRID_W), :] * scale
                q2 = jnp.where(sel2, jnp.concatenate([q, q], axis=0), 0.0).astype(BF16)
                band = pl.ds(pl.multiple_of(C + bs * GRID_W, GRID_W), nk)
                s_loc = _dot_nt(q2, kb16[band, :]) + _pair_bias(bias_ref, dr0, kh)
                s_ctx = _dot_nt(q2, kc)
                pre.append((r, band, s_loc, s_ctx))
            mid = [(r, band) + _na_softmax(s_loc, s_ctx) for r, band, s_loc, s_ctx in pre]
            for r, band, p_loc, p_ctx, den in mid:
                o2 = (_dot(p_loc.astype(BF16), vb16[band, :]) + _dot(p_ctx.astype(BF16), vc)) / den
                out_ref[pl.ds(pl.multiple_of(r * GRID_W, GRID_W), GRID_W), :] = jnp.where(
                    lane < NA_DIM, o2[:GRID_W], o2[GRID_W:]).astype(BF16)
            return carry

        lax.fori_loop(0, R // NA_GROUP, group, 0)

    def col(seg):
        return pl.BlockSpec((None, T, LANES), lambda b, p, seg=seg: (b, 0, base + seg * NA_PAIRS + p))

    return _call_hosting(
        body, hosted, name="na_fwd", grid=(B, NA_PAIRS),
        out_shape=(jax.ShapeDtypeStruct((B, N, NA_WIDTH), BF16),),
        in_specs=[col(0), col(1), col(2),
                  pl.BlockSpec((2, 2 * NA_KH - 2, GRID_W, LANES), lambda b, p: (p, 0, 0, 0))],
        out_specs=(pl.BlockSpec((None, N, LANES), lambda b, p: (b, 0, p)),),
        scratch_shapes=[pltpu.VMEM((T, LANES), BF16)] * 2,
        args=(proj, proj, proj, bias2))


def _na_bwd(proj, bias2, dlat, n_ctx, hosted):
    B, T, _ = proj.shape
    C = n_ctx
    N = T - C
    R = N // GRID_W
    kh, nk = _na_geometry(R)
    scale = NA_DIM ** -0.5
    base = (4 * RET_WIDTH) // LANES

    def body(q_ref, k_ref, v_ref, bias_ref, dl_ref, d_ref, db_ref, kb16, vb16):
        b = pl.program_id(1)
        kb16[...] = k_ref[...].astype(BF16)
        vb16[...] = v_ref[...].astype(BF16)
        kc = kb16[0:C, :]
        vc = vb16[0:C, :]
        lane = lax.broadcasted_iota(jnp.int32, (GRID_W, LANES), 1)
        d_ref[...] = jnp.zeros(d_ref.shape, F32)

        @pl.when(b == 0)
        def _():
            db_ref[...] = jnp.zeros(db_ref.shape, F32)

        sel2 = _pair_select()

        def group(gi, carry):
            pre = []
            for u in range(NA_GROUP):
                r = gi * NA_GROUP + u
                bs = jnp.clip(r - kh // 2, 0, R - kh)
                dr0 = bs - r + (NA_KH - 1)
                q = q_ref[pl.ds(pl.multiple_of(C + r * GRID_W, GRID_W), GRID_W), :] * scale
                do = dl_ref[pl.ds(pl.multiple_of(r * GRID_W, GRID_W), GRID_W), :]
                q2 = jnp.where(sel2, jnp.concatenate([q, q], axis=0), 0.0).astype(BF16)
                do2 = jnp.where(sel2, jnp.concatenate([do, do], axis=0), 0.0).astype(BF16)
                band = pl.ds(pl.multiple_of(C + bs * GRID_W, GRID_W), nk)
                s_loc = _dot_nt(q2, kb16[band, :]) + _pair_bias(bias_ref, dr0, kh)
                s_ctx = _dot_nt(q2, kc)
                dp_loc = _dot_nt(do2, vb16[band, :])
                dp_ctx = _dot_nt(do2, vc)
                pre.append((r, dr0, band, q2, do2, s_loc, s_ctx, dp_loc, dp_ctx))
            mid = []
            for r, dr0, band, q2, do2, s_loc, s_ctx, dp_loc, dp_ctx in pre:
                p_loc, p_ctx, den = _na_softmax(s_loc, s_ctx)
                inv = 1.0 / den
                p_loc = p_loc * inv
                p_ctx = p_ctx * inv
                delta = (jnp.sum(p_loc * dp_loc, axis=-1, keepdims=True)
                         + jnp.sum(p_ctx * dp_ctx, axis=-1, keepdims=True))
                ds_loc = p_loc * (dp_loc - delta)
                ds_ctx = p_ctx * (dp_ctx - delta)
                mid.append((r, dr0, band, q2, do2, p_loc.astype(BF16), p_ctx.astype(BF16), ds_loc, ds_ctx))
            for r, dr0, band, q2, do2, pb_loc, pb_ctx, ds_loc, ds_ctx in mid:
                dsb_loc = ds_loc.astype(BF16)
                dsb_ctx = ds_ctx.astype(BF16)
                dq2 = _dot(dsb_loc, kb16[band, :]) + _dot(dsb_ctx, kc)
                d_ref[0, pl.ds(pl.multiple_of(C + r * GRID_W, GRID_W), GRID_W), :] = jnp.where(
                    lane < NA_DIM, dq2[:GRID_W], dq2[GRID_W:]) * scale
                d_ref[1, band, :] += _dot_tn(dsb_loc, q2)
                d_ref[2, band, :] += _dot_tn(pb_loc, do2)
                d_ref[1, 0:C, :] += _dot_tn(dsb_ctx, q2)
                d_ref[2, 0:C, :] += _dot_tn(pb_ctx, do2)
                for e in range(2):
                    for m in range(kh // 2):
                        db_ref[e, pl.ds(dr0 + 2 * m, 1)] += ds_loc[e * GRID_W:(e + 1) * GRID_W,
                                                                   m * LANES:(m + 1) * LANES].reshape(1, GRID_W, LANES)
            return carry

        lax.fori_loop(0, R // NA_GROUP, group, 0)

    def col(seg):
        return pl.BlockSpec((None, T, LANES), lambda p, b, seg=seg: (b, 0, base + seg * NA_PAIRS + p))

    return _call_hosting(
        body, hosted, name="na_bwd", grid=(NA_PAIRS, B),
        out_shape=(jax.ShapeDtypeStruct((B, 3, T, NA_WIDTH), F32),
                   jax.ShapeDtypeStruct((NA_HEADS, 2 * NA_KH - 2, GRID_W, LANES), F32)),
        in_specs=[col(0), col(1), col(2),
                  pl.BlockSpec((2, 2 * NA_KH - 2, GRID_W, LANES), lambda p, b: (p, 0, 0, 0)),
                  pl.BlockSpec((None, N, LANES), lambda p, b: (b, 0, p))],
        out_specs=(pl.BlockSpec((None, 3, T, LANES), lambda p, b: (b, 0, 0, p)),
                   pl.BlockSpec((2, 2 * NA_KH - 2, GRID_W, LANES), lambda p, b: (p, 0, 0, 0))),
        scratch_shapes=[pltpu.VMEM((T, LANES), BF16)] * 2,
        args=(proj, proj, proj, bias2, dlat))


def _split3(a):
    hi = a.astype(BF16)
    r1 = a - hi.astype(F32)
    mid = r1.astype(BF16)
    lo = (r1 - mid.astype(F32)).astype(BF16)
    return hi, mid, lo


def _rpb_reduce(dbias2, onehot2):
    rows = dbias2.shape[0] * dbias2.shape[1]
    flat = dbias2.reshape(rows, GRID_W * LANES)

    def body(a_ref, oh_ref, o_ref):
        hi, mid, lo = _split3(a_ref[...])
        oh = oh_ref[...]
        o_ref[...] = _dot(hi, oh) + _dot(mid, oh) + _dot(lo, oh)

    return pl.pallas_call(
        body, name="rpb_reduce", out_shape=jax.ShapeDtypeStruct((rows, LANES), F32),
        in_specs=[_vmem(), _vmem()], out_specs=_vmem(),
        compiler_params=pltpu.CompilerParams(vmem_limit_bytes=VMEM_LIMIT),
    )(flat, onehot2)


def _dense_core(lat_ret, lat_na, x, tgt, modl, g_post_mix, g_pre_mlp, g_post_mlp, w_out, w1, w2):
    B, N, D = x.shape
    F = w1.shape[1]
    mixw = w_out.shape[0]
    half = mixw // 2
    tm = _div_tile(N, 256, 16)
    nt = N // tm
    fc = _div_tile(F, 1024, LANES)

    def body(lr_ref, ln_ref, x_ref, t_ref, gt1_ref, sh2_ref, sc2_ref, gt2_ref, gpm_ref, gpre_ref, gpo_ref,
             wout_hbm, w1_hbm, w2_hbm,
             dy1_ref, dlr_ref, dln_ref, dmix_ref, h2_ref, a_ref, du_ref, dz_ref, red_ref,
             wout_v, w1_v, w2_v, u_s, sems):
        @pl.when((pl.program_id(0) == 0) & (pl.program_id(1) == 0))
        def _():
            cps = [pltpu.make_async_copy(wout_hbm, wout_v, sems.at[0]),
                   pltpu.make_async_copy(w1_hbm, w1_v, sems.at[1]),
                   pltpu.make_async_copy(w2_hbm, w2_v, sems.at[2])]
            for cp in cps:
                cp.start()
            for cp in cps:
                cp.wait()

        gt1 = gt1_ref[...]
        sh2 = sh2_ref[...]
        sc2 = sc2_ref[...]
        gt2 = gt2_ref[...]
        gpm = gpm_ref[...]
        gpre = gpre_ref[...]
        gpo = gpo_ref[...]

        def rowmean(a):
            return jnp.mean(a, axis=-1, keepdims=True)

        def colsum(a):
            return jnp.sum(a, axis=0, keepdims=True)

        mix = _dot(lr_ref[...], wout_v[0:half, :]) + _dot(ln_ref[...], wout_v[half:, :])
        x = x_ref[...]
        rm = lax.rsqrt(rowmean(mix * mix) + NORM_EPS)
        mh = mix * rm
        nm = mh * gpm
        y1 = x + gt1 * nm
        r1 = lax.rsqrt(rowmean(y1 * y1) + NORM_EPS)
        xh = y1 * r1
        n1 = xh * gpre
        h2b = (n1 * (1.0 + sc2) + sh2).astype(BF16)
        h2_ref[...] = h2b
        z = jnp.zeros((tm, D), F32)
        for c0 in range(0, F, fc):
            u = _dot(h2b, w1_v[:, c0:c0 + fc])
            u_s[:, c0:c0 + fc] = u
            ru = jnp.maximum(u, 0.0)
            ab = (ru * ru).astype(BF16)
            a_ref[:, c0:c0 + fc] = ab
            z = z + _dot(ab, w2_v[c0:c0 + fc, :])
        r2 = lax.rsqrt(rowmean(z * z) + NORM_EPS)
        zh = z * r2
        n2 = zh * gpo
        y2 = y1 + gt2 * n2
        err = y2 - t_ref[...]
        loss = 0.5 * jnp.sum(rowmean(err * err))
        dy2 = err * (1.0 / D)
        red_ref[2:3, :] = colsum(dy2 * n2)
        dn2 = dy2 * gt2
        red_ref[6:7, :] = colsum(dn2 * zh)
        dzh = dn2 * gpo
        dz = r2 * (dzh - zh * rowmean(dzh * zh))
        dzb = dz.astype(BF16)
        dz_ref[...] = dzb
        dh2 = jnp.zeros((tm, D), F32)
        for c0 in range(0, F, fc):
            da = _dot_nt(dzb, w2_v[c0:c0 + fc, :])
            dub = (da * (2.0 * jnp.maximum(u_s[:, c0:c0 + fc], 0.0))).astype(BF16)
            du_ref[:, c0:c0 + fc] = dub
            dh2 = dh2 + _dot_nt(dub, w1_v[:, c0:c0 + fc])
        red_ref[3:4, :] = colsum(dh2 * n1)
        red_ref[4:5, :] = colsum(dh2)
        dn1 = dh2 * (1.0 + sc2)
        red_ref[5:6, :] = colsum(dn1 * xh)
        dxh = dn1 * gpre
        dy1 = dy2 + r1 * (dxh - xh * rowmean(dxh * xh))
        dy1_ref[...] = dy1
        red_ref[0:1, :] = colsum(dy1 * nm)
        dnm = dy1 * gt1
        red_ref[1:2, :] = colsum(dnm * mh)
        dmh = dnm * gpm
        dmix = (rm * (dmh - mh * rowmean(dmh * mh))).astype(BF16)
        dmix_ref[...] = dmix
        dlr_ref[...] = _dot_nt(dmix, wout_v[0:half, :])
        dln_ref[...] = _dot_nt(dmix, wout_v[half:, :])
        red_ref[7:8, :] = jnp.zeros((1, D), F32) + loss

    def tok(w):
        return pl.BlockSpec((None, tm, w), lambda b, t: (b, t, 0))

    def mod(k):
        return pl.BlockSpec((None, None, 1, D), lambda b, t, k=k: (b, k, 0, 0))

    def vec():
        return pl.BlockSpec((1, D), lambda b, t: (0, 0))

    return pl.pallas_call(
        body, name="dense_core", grid=(B, nt),
        out_shape=(jax.ShapeDtypeStruct((B, N, D), F32), jax.ShapeDtypeStruct((B, N, half), F32),
                   jax.ShapeDtypeStruct((B, N, half), F32), jax.ShapeDtypeStruct((B, N, D), BF16),
                   jax.ShapeDtypeStruct((B, N, D), BF16), jax.ShapeDtypeStruct((B, N, F), BF16),
                   jax.ShapeDtypeStruct((B, N, F), BF16), jax.ShapeDtypeStruct((B, N, D), BF16),
                   jax.ShapeDtypeStruct((B, nt, SUBLANES, D), F32)),
        in_specs=[tok(half), tok(half), tok(D), tok(D), mod(2), mod(3), mod(4), mod(5), vec(), vec(), vec(),
                  _any(), _any(), _any()],
        out_specs=(tok(D), tok(half), tok(half), tok(D), tok(D), tok(F), tok(F), tok(D),
                   pl.BlockSpec((None, None, SUBLANES, D), lambda b, t: (b, t, 0, 0))),
        scratch_shapes=[pltpu.VMEM((mixw, D), BF16), pltpu.VMEM((D, F), BF16), pltpu.VMEM((F, D), BF16),
                        pltpu.VMEM((tm, F), F32), pltpu.SemaphoreType.DMA((3,))],
        compiler_params=_params("arbitrary", "arbitrary"),
    )(lat_ret, lat_na, x, tgt, modl, modl, modl, modl, g_post_mix, g_pre_mlp, g_post_mlp, w_out, w1, w2)


def _inproj_bwd(dret, dna, x_all, dy1, modl, g1, w_in, n_ctx, hosted):
    B, T, D = x_all.shape
    N = T - n_ctx
    tm = _div_tile(n_ctx, 256, 16)
    nct = n_ctx // tm
    nt = T // tm
    nseg_r = dret.shape[1]
    nseg_n = dna.shape[1]
    nw = w_in.shape[1]

    def body(*refs):
        seg_refs = refs[:nseg_r + nseg_n]
        x_ref, dy1_ref, sc_ref, g_ref, w_ref, dx_ref, red_ref = refs[nseg_r + nseg_n:]
        t = pl.program_id(1)
        dh = jnp.zeros((tm, D), F32)
        for s, ref in enumerate(seg_refs):
            dh = dh + _dot_nt(ref[...].astype(BF16), w_ref[:, s * SEG:(s + 1) * SEG])
        x = x_ref[...]
        g = g_ref[...]
        r = lax.rsqrt(jnp.mean(x * x, axis=-1, keepdims=True) + NORM_EPS)
        xh = x * r
        red_ref[0:1, :] = jnp.sum(dh, axis=0, keepdims=True)
        red_ref[1:2, :] = jnp.sum(dh * (xh * g), axis=0, keepdims=True)
        dn = dh * (1.0 + sc_ref[...])
        red_ref[2:3, :] = jnp.sum(dn * xh, axis=0, keepdims=True)
        red_ref[3:, :] = jnp.zeros((SUBLANES - 3, D), F32)
        dxh = dn * g
        dx = r * (dxh - xh * jnp.mean(dxh * xh, axis=-1, keepdims=True))
        dx_ref[...] = dx + jnp.where(t >= nct, dy1_ref[...], 0.0)

    def mrow(b, t):
        return jnp.where(t < nct, B, b)

    def seg(s):
        return pl.BlockSpec((None, None, tm, SEG), lambda b, t, s=s: (b, s, t, 0))

    def lat_tile():
        return pl.BlockSpec((None, tm, D), lambda b, t: (b, jnp.maximum(t - nct, 0), 0))

    return _call_hosting(
        body, hosted, name="inproj_bwd", grid=(B, nt),
        out_shape=(jax.ShapeDtypeStruct((B, N, D), F32), jax.ShapeDtypeStruct((B, nt, SUBLANES, D), F32)),
        in_specs=[seg(s) for s in range(nseg_r)] + [seg(s) for s in range(nseg_n)]
                 + [pl.BlockSpec((None, tm, D), lambda b, t: (b, t, 0)), lat_tile(),
                    pl.BlockSpec((None, None, 1, D), lambda b, t: (mrow(b, t), 1, 0, 0)),
                    pl.BlockSpec((1, D), lambda b, t: (0, 0)),
                    pl.BlockSpec((D, nw), lambda b, t: (0, 0))],
        out_specs=(lat_tile(), pl.BlockSpec((None, None, SUBLANES, D), lambda b, t: (b, t, 0, 0))),
        scratch_shapes=[], args=(*([dret] * nseg_r), *([dna] * nseg_n), x_all, dy1, modl, g1, w_in))


def _tn_matmul(lhs, rhs, name):
    B, S, T, W = lhs.shape
    nn = rhs.shape[-1]
    tk = _div_tile(T, 1024, LANES)
    bm = _div_tile(W, 1024, LANES)
    bn = _div_tile(nn, 1024, LANES)
    nkt = T // tk
    nk = B * nkt

    def body(l_ref, r_ref, o_ref, acc):
        k = pl.program_id(3)

        @pl.when(k == 0)
        def _():
            acc[...] = jnp.zeros(acc.shape, F32)

        acc[...] += _dot_tn(l_ref[...].astype(BF16), r_ref[...].astype(BF16))

        @pl.when(k == nk - 1)
        def _():
            o_ref[...] = acc[...].astype(BF16)

    nwb = W // bm
    return pl.pallas_call(
        functools.partial(body), name=name, grid=(S, nwb, nn // bn, nk),
        out_shape=jax.ShapeDtypeStruct((S * W, nn), BF16),
        in_specs=[pl.BlockSpec((None, None, tk, bm), lambda s, i, j, k: (k // nkt, s, k % nkt, i)),
                  pl.BlockSpec((None, tk, bn), lambda s, i, j, k: (k // nkt, k % nkt, j))],
        out_specs=pl.BlockSpec((bm, bn), lambda s, i, j, k: (s * nwb + i, j)),
        scratch_shapes=[pltpu.VMEM((bm, bn), F32)],
        compiler_params=_params("parallel", "parallel", "parallel", "arbitrary"),
    )(lhs, rhs)


def _sum_slots(buf, name):
    _, rows, cols = buf.shape
    tr = _div_tile(rows, 256, 2 * SUBLANES)

    def body(b_ref, o_ref):
        acc = b_ref[0].astype(F32)
        for k in range(1, N_DEV):
            acc = acc + b_ref[k].astype(F32)
        o_ref[...] = acc

    return pl.pallas_call(
        functools.partial(body), name=name, grid=(rows // tr,),
        out_shape=jax.ShapeDtypeStruct((rows, cols), F32),
        in_specs=[pl.BlockSpec((N_DEV, tr, cols), lambda i: (0, i, 0))],
        out_specs=pl.BlockSpec((tr, cols), lambda i: (i, 0)),
        compiler_params=_params("parallel"),
    )(buf)


def _small_ar(vec, dmods, silu_all, w_ada, c_ctx):
    rv = vec.shape[0]
    D = silu_all.shape[1]
    ncol = w_ada.shape[1]
    nm = dmods.shape[1]
    srows = silu_all.shape[0]

    def body(vec_ref, dm_ref, s_ref, w_ref, cc_ref, tot_ref, gb_ref, gw_ref, gc_ref,
             vbuf, mbuf, tbuf, dmx, send1, recv1, send2, recv2, send3, recv3):
        me, _ = _me_and_peers()
        vbuf[me] = vec_ref[...]
        mbuf[me] = dm_ref[...]
        _exchange(lambda p: vbuf.at[me], lambda p: vbuf.at[p], send1, recv1)
        _exchange(lambda p: mbuf.at[me], lambda p: mbuf.at[p], send2, recv2)
        tot = vbuf[0]
        msum = mbuf[0]
        for k in range(1, N_DEV):
            tot = tot + vbuf[k]
            msum = msum + mbuf[k]
        tot_ref[...] = tot
        gb_ref[...] = jnp.sum(msum, axis=0, keepdims=True)
        loc = pl.ds(pl.multiple_of(me * ncol, ncol), ncol)
        for k in range(N_DEV):
            dmx[k * SUBLANES:(k + 1) * SUBLANES, :] = mbuf[k, :, loc]
        cm = msum[2:3, :]
        mbuf[0, 2:3, :] = cm
        cm_loc = mbuf[0, 2:3, loc]
        dmx[N_DEV * SUBLANES:, :] = jnp.concatenate([cm_loc, jnp.zeros((SUBLANES - 1, ncol), F32)], axis=0)
        gw_ref[...] = _dot_tn(s_ref[...], dmx[...])
        tbuf[me] = _dot_nt(dmx[N_DEV * SUBLANES:, :], w_ref[...])
        _exchange(lambda p: tbuf.at[me], lambda p: tbuf.at[p], send3, recv3)
        tsum = tbuf[0]
        for k in range(1, N_DEV):
            tsum = tsum + tbuf[k]
        cc = cc_ref[...]
        sg = _sigmoid(cc)
        gc_ref[...] = tsum[0:1, :] * (sg * (1.0 + cc * (1.0 - sg)))

    return pl.pallas_call(
        body, name="small_ar",
        out_shape=(jax.ShapeDtypeStruct((rv, LANES), F32), jax.ShapeDtypeStruct((1, nm), F32),
                   jax.ShapeDtypeStruct((D, ncol), F32), jax.ShapeDtypeStruct((1, D), F32)),
        in_specs=[_vmem()] * 5, out_specs=(_vmem(),) * 4,
        scratch_shapes=[pltpu.VMEM((N_DEV, rv, LANES), F32), pltpu.VMEM((N_DEV, SUBLANES, nm), F32),
                        pltpu.VMEM((N_DEV, SUBLANES, D), F32), pltpu.VMEM((srows, ncol), F32)]
                       + [pltpu.SemaphoreType.DMA((N_DEV - 1,))] * 6,
        compiler_params=pltpu.CompilerParams(vmem_limit_bytes=VMEM_LIMIT),
    )(vec, dmods, silu_all, w_ada, c_ctx.reshape(1, D))


def _adamw(w, g, m, v, name):
    rows, cols = w.shape
    tr = _div_tile(rows, 256, SUBLANES) if rows * cols > 65536 else rows

    def body(w_ref, g_ref, m_ref, v_ref, d_ref, nm_ref, nv_ref):
        gv = g_ref[...]
        mn = ADAM_B1 * m_ref[...] + (1.0 - ADAM_B1) * gv
        vn = ADAM_B2 * v_ref[...] + (1.0 - ADAM_B2) * (gv * gv)
        m_hat = mn / (1.0 - ADAM_B1 ** ADAM_STEP)
        v_hat = vn / (1.0 - ADAM_B2 ** ADAM_STEP)
        d_ref[...] = -ADAM_LR * (m_hat / (jnp.sqrt(v_hat) + ADAM_EPS) + ADAM_WD * w_ref[...])
        nm_ref[...] = mn
        nv_ref[...] = vn

    spec = pl.BlockSpec((tr, cols), lambda i: (i, 0))
    return pl.pallas_call(
        functools.partial(body), name=name, grid=(rows // tr,),
        out_shape=(jax.ShapeDtypeStruct((rows, cols), F32),) * 3,
        in_specs=[spec] * 4, out_specs=(spec,) * 3,
        compiler_params=_params("parallel"),
    )(w, g, m, v)


def _rope_tables(n_ctx, n):
    n_freq = RET_DIM // 4
    inv = ROPE_BASE ** (-jnp.arange(n_freq, dtype=F32) / n_freq)
    tok = jnp.arange(n)
    pos_r = (tok // GRID_W).astype(F32)
    pos_c = (tok % GRID_W).astype(F32)
    ang_r = pos_r[:, None] * inv[None, :]
    ang_c = pos_c[:, None] * inv[None, :]
    cos = jnp.concatenate([jnp.cos(ang_r), jnp.cos(ang_r), jnp.cos(ang_c), jnp.cos(ang_c)], axis=-1)
    sin = jnp.concatenate([-jnp.sin(ang_r), jnp.sin(ang_r), -jnp.sin(ang_c), jnp.sin(ang_c)], axis=-1)
    cos = jnp.concatenate([jnp.ones((n_ctx, RET_DIM), F32), cos], axis=0)
    sin = jnp.concatenate([jnp.zeros((n_ctx, RET_DIM), F32), sin], axis=0)
    return cos, sin


def _na_tables():
    q = np.arange(GRID_W)[:, None]
    k = np.arange(GRID_W)[None, :]
    start = np.clip(q - NA_KW // 2, 0, GRID_W - NA_KW)
    valid = (k >= start) & (k < start + NA_KW)
    dc = np.clip(k - q + (NA_KW - 1), 0, 2 * NA_KW - 2)
    ncls = 2 * NA_KW - 1
    onehot = (dc[None] == np.arange(ncls)[:, None, None]) & valid[None]
    oh2 = np.zeros((GRID_W, LANES, LANES), np.float32)
    for c in range(ncls):
        oh2[:, :GRID_W, c] = onehot[c]
        oh2[:, GRID_W:, 32 + c] = onehot[c]
    return onehot.astype(np.float32), valid, oh2.reshape(GRID_W * LANES, LANES)


def _paired_bias(rpb, onehot, valid):
    t = jnp.einsum("hdc,cqk->hdqk", rpb, jnp.asarray(onehot), precision=lax.Precision.HIGHEST)
    t = jnp.where(jnp.asarray(valid)[None, None], t, NEG_INF)
    return jnp.concatenate([t[:, :-1], t[:, 1:]], axis=-1)


def kernel(x, c, ctx, c_ctx, w_ada, b_ada, g_pre_mix, g_post_mix, g_pre_mlp, g_post_mlp, w_in, ret_decay, ret_gn, na_rpb, w_out, w_mlp1, w_mlp2, loss_target, m_c_ctx, m_w_ada, m_b_ada, m_g_pre_mix, m_g_post_mix, m_g_pre_mlp, m_g_post_mlp, m_w_in, m_ret_decay, m_ret_gn, m_na_rpb, m_w_out, m_w_mlp1, m_w_mlp2, v_c_ctx, v_w_ada, v_b_ada, v_g_pre_mix, v_g_post_mix, v_g_pre_mlp, v_g_post_mlp, v_w_in, v_ret_decay, v_ret_gn, v_na_rpb, v_w_out, v_w_mlp1, v_w_mlp2):
    B, N, D = x.shape
    C = ctx.shape[1]
    T = C + N
    me = 4 * lax.axis_index("x") + 2 * lax.axis_index("y") + lax.axis_index("c")

    silu_all, mods_g, gin, wout_l, w1_l, w2_l = _mod_gather(c, c_ctx, w_ada[0], b_ada, w_in[0], w_out[0],
                                                           w_mlp1[0], w_mlp2[0])
    mods_full = mods_g.transpose(1, 0, 2).reshape(mods_g.shape[1], N_MOD * D)
    mine = lax.dynamic_slice_in_dim(mods_full, me * SUBLANES, B, axis=0)
    modl = jnp.concatenate([mine, mods_full[N_DEV * SUBLANES:N_DEV * SUBLANES + 1]], axis=0)
    modl = modl.reshape(B + 1, N_MOD, 1, D)
    win_b = gin.transpose(1, 0, 2).reshape(D, N_DEV * gin.shape[2])
    rin = win_b.shape[1] // N_DEV
    rout, c1, r2 = wout_l.shape[0], w1_l.shape[1], w2_l.shape[0]

    def rows_of(n):
        return lambda ref: _row_block(ref, n)

    def cols_of(n):
        return lambda ref: _col_block(ref, n)

    cos, sin = _rope_tables(C, N)
    onehot, valid, oh2 = _na_tables()
    bias2 = _paired_bias(na_rpb[0], onehot, valid)
    lg = jax.nn.log_sigmoid(ret_decay[0].astype(F32))

    x_all = jnp.concatenate([ctx, x], axis=1)
    level_one = SIBLING + ICI_SAME_CORE
    h_all, proj, w1_part = _inproj_fwd(
        x_all, modl, g_pre_mix, win_b, C,
        [_Hosted("gather", level_one, [w1_l], [cols_of(c1)], [jax.ShapeDtypeStruct((D, N_DEV * c1), BF16)], True)])
    o_ret, lat_ret, w2_part = _ret_fwd(
        proj, cos, sin, lg, ret_gn, C,
        [_Hosted("gather", level_one, [w2_l], [rows_of(r2)], [jax.ShapeDtypeStruct((N_DEV * r2, D), BF16)], True)])
    lat_na, w1_b, w2_b, wout_b = _na_fwd(
        proj, bias2, C,
        [_HostedRelay([w1_part, w2_part], [cols_of(c1), rows_of(r2)]),
         _Hosted("gather", ALL_PEERS, [wout_l], [rows_of(rout)], [jax.ShapeDtypeStruct((N_DEV * rout, D), BF16)], True)])

    (dy1, dlat_ret, dlat_na, dmix, h2, act, du, dz, red_d) = _dense_core(
        lat_ret, lat_na, x, loss_target, modl, g_post_mix, g_pre_mlp, g_post_mlp, wout_b, w1_b, w2_b)

    gw_out_p = jnp.concatenate([_tn_matmul(lat_ret[:, None], dmix, "gw_out_ret"),
                                _tn_matmul(lat_na[:, None], dmix, "gw_out_na")], axis=0)
    gw1_p = _tn_matmul(h2[:, None], du, "gw_mlp1")
    gw2_p = _tn_matmul(act[:, None], dz, "gw_mlp2")

    dret, dgn_p, dlg_p, b1 = _ret_bwd(
        proj, cos, sin, lg, ret_gn, o_ret, dlat_ret, C,
        [_Hosted("scatter", ALL_PEERS, [gw1_p], [cols_of(c1)], [jax.ShapeDtypeStruct((N_DEV, D, c1), BF16)], True)])
    dna, dbias2, b2, bout = _na_bwd(
        proj, bias2, dlat_na, C,
        [_Hosted("scatter", ALL_PEERS, [gw2_p, gw_out_p], [rows_of(r2), rows_of(rout)],
                 [jax.ShapeDtypeStruct((N_DEV, r2, D), BF16), jax.ShapeDtypeStruct((N_DEV, rout, D), BF16)], True)])
    gwin_t_p = jnp.concatenate([_tn_matmul(dret, h_all, "gw_in_ret"), _tn_matmul(dna, h_all, "gw_in_na")], axis=0)
    grad_x, red_i, bin_ = _inproj_bwd(
        dret, dna, x_all, dy1, modl, g_pre_mix, win_b, C,
        [_Hosted("scatter", ALL_PEERS, [gwin_t_p], [rows_of(rin)], [jax.ShapeDtypeStruct((N_DEV, rin, D), BF16)], True)])

    g_w_in = _sum_slots(bin_, "sum_w_in").T
    g_w_out = _sum_slots(bout, "sum_w_out")
    g_w1 = _sum_slots(b1, "sum_w_mlp1")
    g_w2 = _sum_slots(b2, "sum_w_mlp2")

    rd = red_d.sum(axis=1)[:, :, :]
    ri = red_i
    nct = ri.shape[1] * C // T
    ri_ctx = ri[:, :nct].sum(axis=(0, 1))
    ri_lat = ri[:, nct:].sum(axis=1)
    d_mods = jnp.concatenate([ri_lat[:, 0], ri_lat[:, 1], rd[:, 0], rd[:, 4], rd[:, 3], rd[:, 2]], axis=-1)
    d_cmods = jnp.concatenate([ri_ctx[0], ri_ctx[1], jnp.zeros(((N_MOD - 2) * D,), F32)])[None]
    dm_slot = jnp.concatenate([d_mods, d_cmods, jnp.zeros((SUBLANES - B - 1, N_MOD * D), F32)], axis=0)
    dg_pre_mix = ri_lat[:, 2].sum(axis=0) + ri_ctx[2]
    dg_post_mix = rd[:, 1].sum(axis=0)
    dg_pre_mlp = rd[:, 5].sum(axis=0)
    dg_post_mlp = rd[:, 6].sum(axis=0)
    loss_p = rd[:, 7, 0].sum()
    d_gn = dgn_p[:, 0].sum(axis=0)
    d_lg = dlg_p[:, :, :2, 0].sum(axis=0).T
    d_decay = d_lg * jax.nn.sigmoid(-ret_decay[0].astype(F32))
    rr = _rpb_reduce(dbias2, jnp.asarray(oh2, BF16)).reshape(NA_HEADS, 2 * NA_KH - 2, LANES)
    ncls = 2 * NA_KW - 1
    d_rpb = (jnp.pad(rr[:, :, :ncls], ((0, 0), (0, 1), (0, 0))) + jnp.pad(rr[:, :, 32:32 + ncls], ((0, 0), (1, 0), (0, 0))))
    d_rpb32 = jnp.pad(d_rpb, ((0, 0), (0, 0), (0, 32 - ncls)))
    pieces = [dg_pre_mix, dg_post_mix, dg_pre_mlp, dg_post_mlp, d_gn, d_rpb32.reshape(-1),
              jnp.pad(d_decay.reshape(-1), (0, LANES - d_decay.size)), jnp.full((LANES,), loss_p, F32)]
    vec = jnp.concatenate(pieces)
    pad = (-vec.shape[0]) % (SUBLANES * LANES)
    vec = jnp.pad(vec, (0, pad)).reshape(-1, LANES)
    tot, g_b_ada, g_w_ada, g_c_ctx = _small_ar(vec, dm_slot, silu_all, w_ada[0], c_ctx)
    flat = tot.reshape(-1)
    o0 = 0
    g_pre_mix_g = flat[o0:o0 + D]; o0 += D
    g_post_mix_g = flat[o0:o0 + D]; o0 += D
    g_pre_mlp_g = flat[o0:o0 + D]; o0 += D
    g_post_mlp_g = flat[o0:o0 + D]; o0 += D
    g_gn = flat[o0:o0 + RET_WIDTH]; o0 += RET_WIDTH
    nrpb = NA_HEADS * (2 * NA_KH - 1) * 32
    g_rpb = flat[o0:o0 + nrpb].reshape(NA_HEADS, 2 * NA_KH - 1, 32)[:, :, :ncls]; o0 += nrpb
    g_decay = flat[o0:o0 + 2 * RET_HEADS].reshape(2, RET_HEADS); o0 += LANES
    loss = flat[o0]

    grads = {
        "c_ctx": g_c_ctx.reshape(c_ctx.shape), "w_ada": g_w_ada[None], "b_ada": g_b_ada.reshape(b_ada.shape),
        "g_pre_mix": g_pre_mix_g[None], "g_post_mix": g_post_mix_g[None], "g_pre_mlp": g_pre_mlp_g[None],
        "g_post_mlp": g_post_mlp_g[None], "w_in": g_w_in[None], "ret_decay": g_decay[None], "ret_gn": g_gn[None],
        "na_rpb": g_rpb[None], "w_out": g_w_out[None], "w_mlp1": g_w1[None], "w_mlp2": g_w2[None],
    }
    weights = dict(c_ctx=c_ctx, w_ada=w_ada, b_ada=b_ada, g_pre_mix=g_pre_mix, g_post_mix=g_post_mix,
                   g_pre_mlp=g_pre_mlp, g_post_mlp=g_post_mlp, w_in=w_in, ret_decay=ret_decay, ret_gn=ret_gn,
                   na_rpb=na_rpb, w_out=w_out, w_mlp1=w_mlp1, w_mlp2=w_mlp2)
    m_in = dict(c_ctx=m_c_ctx, w_ada=m_w_ada, b_ada=m_b_ada, g_pre_mix=m_g_pre_mix, g_post_mix=m_g_post_mix,
                g_pre_mlp=m_g_pre_mlp, g_post_mlp=m_g_post_mlp, w_in=m_w_in, ret_decay=m_ret_decay,
                ret_gn=m_ret_gn, na_rpb=m_na_rpb, w_out=m_w_out, w_mlp1=m_w_mlp1, w_mlp2=m_w_mlp2)
    v_in = dict(c_ctx=v_c_ctx, w_ada=v_w_ada, b_ada=v_b_ada, g_pre_mix=v_g_pre_mix, g_post_mix=v_g_post_mix,
                g_pre_mlp=v_g_pre_mlp, g_post_mlp=v_g_post_mlp, w_in=v_w_in, ret_decay=v_ret_decay,
                ret_gn=v_ret_gn, na_rpb=v_na_rpb, w_out=v_w_out, w_mlp1=v_w_mlp1, w_mlp2=v_w_mlp2)
    names = list(weights)
    deltas, new_m, new_v = {}, {}, {}
    for n in names:
        shp = weights[n].shape
        two_d = (-1, shp[-1]) if len(shp) > 1 else (1, shp[0])
        d, nm, nv = _adamw(weights[n].reshape(two_d), grads[n].reshape(two_d), m_in[n].reshape(two_d),
                           v_in[n].reshape(two_d), "adamw_" + n)
        deltas[n], new_m[n], new_v[n] = d.reshape(shp), nm.reshape(shp), nv.reshape(shp)
    return (loss, grad_x, *[grads[n] for n in names], *[deltas[n] for n in names],
            *[new_m[n] for n in names], *[new_v[n] for n in names])
```

```python
import functools
import math

import numpy as np
import jax
import jax.numpy as jnp
from jax import lax
from jax.experimental import pallas as pl
from jax.experimental.pallas import tpu as pltpu

F32 = jnp.float32
BF16 = jnp.bfloat16
MESH = pl.DeviceIdType.MESH

N_DEV = 8
LANES = 128
SUBLANES = 8
VMEM_LIMIT = 60 * 1024 * 1024

GRID_W = 64
RET_HEADS = 4
RET_DIM = 128
RET_WIDTH = RET_HEADS * RET_DIM
NA_HEADS = 8
NA_DIM = 64
NA_WIDTH = NA_HEADS * NA_DIM
NA_PAIRS = NA_HEADS // 2
NA_KH = 8
NA_KW = 16
NA_GROUP = 4
SEG = 512
ROPE_BASE = 10000.0
NORM_EPS = 1e-6
NEG_INF = -1e30
N_MOD = 6

ADAM_LR = 0.001
ADAM_B1 = 0.9
ADAM_B2 = 0.999
ADAM_EPS = 1e-08
ADAM_WD = 0.01
ADAM_STEP = 10


def _dot(a, b):
    return lax.dot_general(a, b, (((1,), (0,)), ((), ())), preferred_element_type=F32)


def _dot_nt(a, b):
    return lax.dot_general(a, b, (((1,), (1,)), ((), ())), preferred_element_type=F32)


def _dot_tn(a, b):
    return lax.dot_general(a, b, (((0,), (0,)), ((), ())), preferred_element_type=F32)


def _sigmoid(x):
    return 1.0 / (1.0 + jnp.exp(-x))


def _div_tile(n, cap, mult):
    if n <= cap:
        return n
    for t in range(cap - cap % mult, 0, -mult):
        if n % t == 0:
            return t
    raise ValueError(f"no tile for {n}")


def _params(*sem):
    return pltpu.CompilerParams(dimension_semantics=tuple(sem) if sem else None,
                                vmem_limit_bytes=VMEM_LIMIT)


def _vmem():
    return pl.BlockSpec(memory_space=pltpu.VMEM)


def _any():
    return pl.BlockSpec(memory_space=pl.ANY)


def _me_and_peers():
    x, y, c = lax.axis_index("x"), lax.axis_index("y"), lax.axis_index("c")
    me = 4 * x + 2 * y + c
    peers = []
    for m in range(1, N_DEV):
        px = 1 - x if (m >> 2) & 1 else x
        py = 1 - y if (m >> 1) & 1 else y
        pc = 1 - c if m & 1 else c
        peers.append(((px, py, pc), 4 * px + 2 * py + pc))
    return me, peers


def _exchange(src_for, dst_from, send_sems, recv_sems):
    me, peers = _me_and_peers()
    sent = []
    for i, (dev, pid) in enumerate(peers):
        cp = pltpu.make_async_remote_copy(src_ref=src_for(pid), dst_ref=dst_from(me),
                                          send_sem=send_sems.at[i], recv_sem=recv_sems.at[i],
                                          device_id=dev, device_id_type=MESH)
        cp.start()
        sent.append(cp)
    for i, (dev, pid) in enumerate(peers):
        pltpu.make_async_remote_copy(src_ref=src_for(pid), dst_ref=dst_from(pid),
                                     send_sem=send_sems.at[i], recv_sem=recv_sems.at[i],
                                     device_id=dev, device_id_type=MESH).wait_recv()
    for cp in sent:
        cp.wait_send()


SIBLING = (1,)
ICI_SAME_CORE = (2, 4, 6)
ALL_PEERS = tuple(range(1, N_DEV))


def _remote(src, dst, send_sem, recv_sem, dev):
    return pltpu.make_async_remote_copy(src_ref=src, dst_ref=dst, send_sem=send_sem, recv_sem=recv_sem,
                                        device_id=dev, device_id_type=MESH)


def _push_start(items, masks, send_sems, recv_sems):
    me, peers = _me_and_peers()
    for k, (src_for, dst_from) in enumerate(items):
        for m in masks:
            dev, pid = peers[m - 1]
            _remote(src_for(pid), dst_from(me), send_sems.at[k, m - 1], recv_sems.at[k, m - 1], dev).start()


def _push_wait_recv(items, masks, send_sems, recv_sems):
    me, peers = _me_and_peers()
    for k, (src_for, dst_from) in enumerate(items):
        for m in masks:
            dev, pid = peers[m - 1]
            _remote(src_for(pid), dst_from(pid), send_sems.at[k, m - 1], recv_sems.at[k, m - 1], dev).wait_recv()


def _push_wait_send(items, masks, send_sems, recv_sems):
    me, peers = _me_and_peers()
    for k, (src_for, dst_from) in enumerate(items):
        for m in masks:
            dev, pid = peers[m - 1]
            _remote(src_for(pid), dst_from(me), send_sems.at[k, m - 1], recv_sems.at[k, m - 1], dev).wait_send()


def _forward_start(items, send_sems, recv_sems):
    me, peers = _me_and_peers()
    sib = peers[0][0]
    for k, (blk_in, blk_out) in enumerate(items):
        for j, m in enumerate(ICI_SAME_CORE):
            pid = peers[m - 1][1]
            _remote(blk_in(pid), blk_out(pid), send_sems.at[k, j], recv_sems.at[k, j], sib).start()


def _forward_wait(items, send_sems, recv_sems):
    me, peers = _me_and_peers()
    sib = peers[0][0]
    for k, (blk_in, blk_out) in enumerate(items):
        for j, m in enumerate(ICI_SAME_CORE):
            got = peers[(m | 1) - 1][1]
            _remote(blk_in(got), blk_out(got), send_sems.at[k, j], recv_sems.at[k, j], sib).wait_recv()
    for k, (blk_in, blk_out) in enumerate(items):
        for j, m in enumerate(ICI_SAME_CORE):
            pid = peers[m - 1][1]
            _remote(blk_in(pid), blk_out(pid), send_sems.at[k, j], recv_sems.at[k, j], sib).wait_send()


def _mod_gather(c, c_ctx, w_ada, b_ada, w_in_t, w_out, w1, w2):
    B, D = c.shape
    ncol = w_ada.shape[1]
    rows = SUBLANES * N_DEV + SUBLANES

    def body(c_ref, cc_ref, w_ref, b_ref, win_ref, wout_ref, w1_ref, w2_ref,
             s_ref, m_ref, gin_ref, wout_b, w1_b, w2_b,
             win_b, send1, recv1, send2, recv2, wsend, wrecv, fsend, frecv, lsem):
        me, _ = _me_and_peers()
        win_b[...] = win_ref[...].astype(BF16)
        block = _row_block(gin_ref, w_in_t.shape[0])
        gather = [(lambda p: win_b, block)]
        own = pltpu.make_async_copy(win_b, block(me), lsem.at[0])
        own.start()
        _push_start(gather, SIBLING + ICI_SAME_CORE, wsend, wrecv)
        wout_b[...] = wout_ref[...].astype(BF16)
        w1_b[...] = w1_ref[...].astype(BF16)
        w2_b[...] = w2_ref[...].astype(BF16)
        cv = c_ref[...]
        slot = jnp.concatenate([cv * _sigmoid(cv), jnp.zeros((SUBLANES - B, D), F32)], axis=0)
        my_rows = pl.ds(pl.multiple_of(me * SUBLANES, SUBLANES), SUBLANES)
        s_ref[my_rows, :] = slot
        ccv = cc_ref[...]
        s_ref[SUBLANES * N_DEV:, :] = jnp.concatenate(
            [ccv * _sigmoid(ccv), jnp.zeros((SUBLANES - 1, D), F32)], axis=0)

        def rows_of(p):
            return s_ref.at[pl.ds(pl.multiple_of(p * SUBLANES, SUBLANES), SUBLANES), :]

        _exchange(lambda p: rows_of(me), rows_of, send1, recv1)
        b_loc = b_ref[:, pl.ds(pl.multiple_of(me * ncol, ncol), ncol)]
        m_ref[me] = _dot(s_ref[...], w_ref[...]) + b_loc
        _exchange(lambda p: m_ref.at[me], lambda p: m_ref.at[p], send2, recv2)
        _push_wait_recv(gather, ICI_SAME_CORE, wsend, wrecv)
        relay = [(block, block)]
        _forward_start(relay, fsend, frecv)
        _push_wait_recv(gather, SIBLING, wsend, wrecv)
        _forward_wait(relay, fsend, frecv)
        _push_wait_send(gather, SIBLING + ICI_SAME_CORE, wsend, wrecv)
        own.wait()

    return pl.pallas_call(
        body, name="mod_gather",
        out_shape=(jax.ShapeDtypeStruct((rows, D), F32), jax.ShapeDtypeStruct((N_DEV, rows, ncol), F32),
                   jax.ShapeDtypeStruct((N_DEV * w_in_t.shape[0], D), BF16),
                   jax.ShapeDtypeStruct(w_out.shape, BF16), jax.ShapeDtypeStruct(w1.shape, BF16),
                   jax.ShapeDtypeStruct(w2.shape, BF16)),
        in_specs=[_vmem()] * 8, out_specs=(_vmem(), _vmem(), _any(), _vmem(), _vmem(), _vmem()),
        scratch_shapes=[pltpu.VMEM(w_in_t.shape, BF16)] + [pltpu.SemaphoreType.DMA((N_DEV - 1,))] * 4
                       + [pltpu.SemaphoreType.DMA((1, N_DEV - 1))] * 2 + [pltpu.SemaphoreType.DMA((1, 3))] * 2
                       + [pltpu.SemaphoreType.DMA((1,))],
        compiler_params=pltpu.CompilerParams(vmem_limit_bytes=VMEM_LIMIT),
    )(c, c_ctx.reshape(1, D), w_ada, b_ada, w_in_t, w_out, w1, w2)


def _row_block(ref, rows):
    return lambda p: ref.at[pl.ds(pl.multiple_of(p * rows, 2 * SUBLANES), rows), :]


def _col_block(ref, cols):
    return lambda p: ref.at[:, pl.ds(pl.multiple_of(p * cols, LANES), cols)]


def _slot(ref):
    return lambda p: ref.at[p]


class _Hosted:
    def __init__(self, kind, masks, operands, block_of, out_shapes, with_own):
        self.kind, self.masks, self.operands = kind, masks, list(operands)
        self.block_of, self.out_shapes, self.with_own = block_of, list(out_shapes), with_own
        self.n = len(self.operands)

    def scratch(self):
        return [pltpu.SemaphoreType.DMA((self.n, N_DEV - 1)), pltpu.SemaphoreType.DMA((self.n, N_DEV - 1)),
                pltpu.SemaphoreType.DMA((self.n,))]

    def _items(self, in_refs, out_refs):
        items = []
        for k in range(self.n):
            if self.kind == "gather":
                items.append((lambda p, k=k: in_refs[k], self.block_of[k](out_refs[k])))
            else:
                items.append((self.block_of[k](in_refs[k]), _slot(out_refs[k])))
        return items

    def _own(self, in_refs, out_refs, lsem):
        me, _ = _me_and_peers()
        items = self._items(in_refs, out_refs)
        return [pltpu.make_async_copy(src_for(me), dst_from(me), lsem.at[k])
                for k, (src_for, dst_from) in enumerate(items)]

    def start(self, in_refs, out_refs, sems):
        send, recv, lsem = sems
        if self.with_own:
            for cp in self._own(in_refs, out_refs, lsem):
                cp.start()
        _push_start(self._items(in_refs, out_refs), self.masks, send, recv)

    def wait(self, in_refs, out_refs, sems):
        send, recv, lsem = sems
        items = self._items(in_refs, out_refs)
        _push_wait_recv(items, self.masks, send, recv)
        _push_wait_send(items, self.masks, send, recv)
        if self.with_own:
            for cp in self._own(in_refs, out_refs, lsem):
                cp.wait()


class _HostedRelay:
    def __init__(self, arrays, block_of):
        self.operands, self.block_of = list(arrays), block_of
        self.out_shapes = [jax.ShapeDtypeStruct(a.shape, a.dtype) for a in arrays]
        self.n = len(self.operands)

    def scratch(self):
        return [pltpu.SemaphoreType.DMA((self.n, 3)), pltpu.SemaphoreType.DMA((self.n, 3))]

    def _items(self, in_refs, out_refs):
        return [(self.block_of[k](in_refs[k]), self.block_of[k](out_refs[k])) for k in range(self.n)]

    def start(self, in_refs, out_refs, sems):
        _forward_start(self._items(in_refs, out_refs), *sems)

    def wait(self, in_refs, out_refs, sems):
        _forward_wait(self._items(in_refs, out_refs), *sems)


def _call_hosting(body, hosted, *, name, grid, out_shape, in_specs, out_specs, scratch_shapes, args):
    n_in, n_out, n_scr = len(in_specs), len(out_shape), len(scratch_shapes)
    hn = sum(hs.n for hs in hosted)
    n_sem = [len(hs.scratch()) for hs in hosted]

    def wrapped(*refs):
        ins = refs[:n_in]
        h_in = refs[n_in:n_in + hn]
        outs = refs[n_in + hn:n_in + hn + n_out]
        h_out = refs[n_in + hn + n_out:n_in + 2 * hn + n_out]
        scr = refs[n_in + 2 * hn + n_out:n_in + 2 * hn + n_out + n_scr]
        sems = refs[n_in + 2 * hn + n_out + n_scr:]
        ids = [pl.program_id(i) for i in range(len(grid))]
        first = functools.reduce(jnp.logical_and, [i == 0 for i in ids])
        last = functools.reduce(jnp.logical_and, [i == g - 1 for i, g in zip(ids, grid)])
        parts, o0, s0 = [], 0, 0
        for hs, ns in zip(hosted, n_sem):
            parts.append((hs, h_in[o0:o0 + hs.n], h_out[o0:o0 + hs.n], sems[s0:s0 + ns]))
            o0 += hs.n
            s0 += ns

        @pl.when(first)
        def _():
            for hs, hi, ho, se in parts:
                hs.start(hi, ho, se)

        body(*ins, *outs, *scr)

        @pl.when(last)
        def _():
            for hs, hi, ho, se in parts:
                hs.wait(hi, ho, se)

    aliases, o0 = {}, 0
    for hs in hosted:
        if isinstance(hs, _HostedRelay):
            aliases.update({n_in + o0 + k: n_out + o0 + k for k in range(hs.n)})
        o0 += hs.n
    return pl.pallas_call(
        wrapped, name=name, grid=grid,
        out_shape=tuple(out_shape) + tuple(s for hs in hosted for s in hs.out_shapes),
        in_specs=list(in_specs) + [_any()] * hn,
        out_specs=tuple(out_specs) + (_any(),) * hn,
        scratch_shapes=list(scratch_shapes) + [s for hs in hosted for s in hs.scratch()],
        input_output_aliases=aliases,
        compiler_params=_params(*(("arbitrary",) * len(grid))),
    )(*args, *[a for hs in hosted for a in hs.operands])


def _inproj_fwd(x_all, modl, g1, w_in_t, n_ctx, hosted):
    B, T, D = x_all.shape
    nw = w_in_t.shape[0]
    tm = _div_tile(n_ctx, 256, 16)
    nct = n_ctx // tm

    def body(x_ref, sh_ref, sc_ref, g_ref, w_ref, h_ref, p_ref):
        x = x_ref[...]
        r = lax.rsqrt(jnp.mean(x * x, axis=-1, keepdims=True) + NORM_EPS)
        h = ((x * r) * g_ref[...]) * (1.0 + sc_ref[...]) + sh_ref[...]
        hb = h.astype(BF16)
        h_ref[...] = hb
        p_ref[...] = _dot_nt(hb, w_ref[...])

    def mrow(b, t):
        return jnp.where(t < nct, B, b)

    return _call_hosting(
        body, hosted, name="inproj_fwd", grid=(B, T // tm),
        out_shape=(jax.ShapeDtypeStruct((B, T, D), BF16), jax.ShapeDtypeStruct((B, T, nw), F32)),
        in_specs=[pl.BlockSpec((None, tm, D), lambda b, t: (b, t, 0)),
                  pl.BlockSpec((None, None, 1, D), lambda b, t: (mrow(b, t), 0, 0, 0)),
                  pl.BlockSpec((None, None, 1, D), lambda b, t: (mrow(b, t), 1, 0, 0)),
                  pl.BlockSpec((1, D), lambda b, t: (0, 0)),
                  pl.BlockSpec((nw, D), lambda b, t: (0, 0))],
        out_specs=(pl.BlockSpec((None, tm, D), lambda b, t: (b, t, 0)),
                   pl.BlockSpec((None, tm, nw), lambda b, t: (b, t, 0))),
        scratch_shapes=[], args=(x_all, modl, modl, g1, w_in_t))


def _swap32(x):
    lane = lax.broadcasted_iota(jnp.int32, x.shape, 1)
    return jnp.where((lane % 64) < 32, pltpu.roll(x, 96, 1), pltpu.roll(x, 32, 1))


def _rope(x, cos, sin):
    return x * cos + _swap32(x) * sin


def _unrope(dy, cos, sin):
    return dy * cos + _swap32(dy * sin)


def _ret_weights(lgf, lgb, dist):
    return jnp.exp(jnp.where(dist >= 0.0, lgf * dist, -lgb * dist))


class _RetDecay:
    def __init__(self, lgf, lgb, rows):
        r = lax.broadcasted_iota(jnp.int32, (rows, RET_DIM), 0).astype(F32)
        self.head = r + 1.0
        self.tail = (rows - 1.0) - r
        self.q_f = jnp.exp(lgf * self.head)
        self.k_f = jnp.exp(lgf * self.tail)
        self.q_b = jnp.exp(lgb * self.tail)
        self.k_b = jnp.exp(lgb * self.head)


def _ret_states(kf32, vs, lgf, lgb, C, c, nt, hf, hb, hfa=None, hba=None):
    dec = _RetDecay(lgf, lgb, c)
    dec_c = _RetDecay(lgf, lgb, C)
    step_f = jnp.exp(jnp.zeros((RET_DIM, RET_DIM), F32) + lgf * c)
    step_b = jnp.exp(jnp.zeros((RET_DIM, RET_DIM), F32) + lgb * c)

    def upd(rows, kdec):
        return _dot_tn((kf32[rows, :] * kdec).astype(BF16), vs[rows, :])

    def lat(t):
        return slice(C + t * c, C + (t + 1) * c)

    state = upd(slice(0, C), dec_c.k_f)
    aged = jnp.zeros_like(state)
    for t in range(nt):
        hf[t] = state.astype(BF16)
        if hfa is not None:
            hfa[t] = aged
        if t < nt - 1:
            aged = step_f * (aged + c * state)
            state = step_f * state + upd(lat(t), dec.k_f)
    state = upd(slice(0, C), dec_c.k_b)
    aged = jnp.zeros_like(state)
    for t in range(nt - 1, -1, -1):
        hb[t] = state.astype(BF16)
        if hba is not None:
            hba[t] = aged
        if t > 0:
            aged = step_b * (aged + c * state)
            state = step_b * state + upd(lat(t), dec.k_b)
    return dec, dec_c, step_f, step_b


def _ret_fwd(proj, cos, sin, lg, gn, n_ctx, hosted):
    B, T, _ = proj.shape
    C = n_ctx
    N = T - C
    c = _div_tile(N, 256, 16)
    nt = N // c
    scale = RET_DIM ** -0.5

    def body(lg_ref, q_ref, k_ref, v_ref, g_ref, cos_ref, sin_ref, gn_ref, o_ref, lat_ref, qs, ks, vs, kf32, hf, hb):
        h = pl.program_id(1)
        lgf = lg_ref[0, h]
        lgb = lg_ref[1, h]
        for rows in [slice(0, C)] + [slice(C + t * c, C + (t + 1) * c) for t in range(nt)]:
            cosb = cos_ref[rows, :]
            sinb = sin_ref[rows, :]
            qs[rows, :] = (_rope(q_ref[rows, :], cosb, sinb) * scale).astype(BF16)
            kr = _rope(k_ref[rows, :], cosb, sinb)
            kf32[rows, :] = kr
            ks[rows, :] = kr.astype(BF16)
            vs[rows, :] = v_ref[rows, :].astype(BF16)
        gnv = gn_ref[...]
        dec, _, _, _ = _ret_states(kf32, vs, lgf, lgb, C, c, nt, hf, hb)
        rc = (lax.broadcasted_iota(jnp.int32, (c, c), 0) - lax.broadcasted_iota(jnp.int32, (c, c), 1)).astype(F32)
        w_diag = _ret_weights(lgf, lgb, rc)
        for t in range(nt):
            rows = slice(C + t * c, C + (t + 1) * c)
            qt = qs[rows, :]
            s = _dot_nt(qt, ks[rows, :])
            o = (_dot((s * w_diag).astype(BF16), vs[rows, :])
                 + dec.q_f * _dot(qt, hf[t]) + dec.q_b * _dot(qt, hb[t]))
            o_ref[t * c:(t + 1) * c, :] = o
            mu = jnp.mean(o, axis=-1, keepdims=True)
            oc = o - mu
            var = jnp.mean(oc * oc, axis=-1, keepdims=True)
            yh = oc * lax.rsqrt(var + NORM_EPS)
            g = g_ref[rows, :]
            lat_ref[t * c:(t + 1) * c, :] = ((yh * gnv) * (g * _sigmoid(g))).astype(BF16)

    def col(seg):
        return pl.BlockSpec((None, T, RET_DIM), lambda b, h, seg=seg: (b, 0, seg * RET_HEADS + h))

    return _call_hosting(
        body, hosted, name="ret_fwd", grid=(B, RET_HEADS),
        out_shape=(jax.ShapeDtypeStruct((B, N, RET_WIDTH), F32), jax.ShapeDtypeStruct((B, N, RET_WIDTH), BF16)),
        in_specs=[pl.BlockSpec(memory_space=pltpu.SMEM), col(0), col(1), col(2), col(3),
                  pl.BlockSpec((T, RET_DIM), lambda b, h: (0, 0)), pl.BlockSpec((T, RET_DIM), lambda b, h: (0, 0)),
                  pl.BlockSpec((1, RET_DIM), lambda b, h: (0, h))],
        out_specs=(pl.BlockSpec((None, N, RET_DIM), lambda b, h: (b, 0, h)),
                   pl.BlockSpec((None, N, RET_DIM), lambda b, h: (b, 0, h))),
        scratch_shapes=[pltpu.VMEM((T, RET_DIM), BF16)] * 3 + [pltpu.VMEM((T, RET_DIM), F32)]
                       + [pltpu.VMEM((nt, RET_DIM, RET_DIM), BF16)] * 2,
        args=(lg, proj, proj, proj, proj, cos, sin, gn))


def _ret_bwd(proj, cos, sin, lg, gn, o, dlat, n_ctx, hosted):
    B, T, _ = proj.shape
    C = n_ctx
    N = T - C
    c = _div_tile(N, 256, 16)
    nt = N // c
    scale = RET_DIM ** -0.5

    def lat(t):
        return slice(C + t * c, C + (t + 1) * c)

    def body(lg_ref, q_ref, k_ref, v_ref, g_ref, cos_ref, sin_ref, gn_ref, o_ref, dl_ref,
             d_ref, dgn_ref, dlg_ref, qs, ks, vs, dos, qf32, kf32, hf, hb, hfa, hba, gf_s, gb_s):
        h = pl.program_id(1)
        lgf = lg_ref[0, h]
        lgb = lg_ref[1, h]
        gnv = gn_ref[...]

        def fold(a):
            return jnp.sum(a.reshape(a.shape[0] // SUBLANES, SUBLANES, a.shape[1]), axis=0)

        for rows in [slice(0, C)] + [lat(t) for t in range(nt)]:
            cosb = cos_ref[rows, :]
            sinb = sin_ref[rows, :]
            qr = _rope(q_ref[rows, :], cosb, sinb) * scale
            qf32[rows, :] = qr
            qs[rows, :] = qr.astype(BF16)
            kr = _rope(k_ref[rows, :], cosb, sinb)
            kf32[rows, :] = kr
            ks[rows, :] = kr.astype(BF16)
            vs[rows, :] = v_ref[rows, :].astype(BF16)

        dgn = jnp.zeros((1, RET_DIM), F32)
        for t in range(nt):
            lrows = slice(t * c, (t + 1) * c)
            ov = o_ref[lrows, :]
            mu = jnp.mean(ov, axis=-1, keepdims=True)
            oc = ov - mu
            var = jnp.mean(oc * oc, axis=-1, keepdims=True)
            rstd = lax.rsqrt(var + NORM_EPS)
            yh = oc * rstd
            g = g_ref[lat(t), :]
            sg = _sigmoid(g)
            dl = dl_ref[lrows, :]
            d_ref[3, lat(t), :] = (dl * (yh * gnv) * (sg * (1.0 + g * (1.0 - sg)))).astype(BF16)
            dls = dl * (g * sg)
            dgn = dgn + jnp.sum(dls * yh, axis=0, keepdims=True)
            dyh = dls * gnv
            do = rstd * (dyh - jnp.mean(dyh, axis=-1, keepdims=True)
                         - yh * jnp.mean(dyh * yh, axis=-1, keepdims=True))
            dos[lrows, :] = do.astype(BF16)
        dgn_ref[...] = jnp.concatenate([dgn, jnp.zeros((SUBLANES - 1, RET_DIM), F32)], axis=0)
        d_ref[3, 0:C, :] = jnp.zeros((C, RET_DIM), BF16)
        d_ref[0, 0:C, :] = jnp.zeros((C, RET_DIM), BF16)

        dec, dec_c, step_f, step_b = _ret_states(kf32, vs, lgf, lgb, C, c, nt, hf, hb, hfa, hba)

        def zmat(t, qdec):
            return _dot_tn((qf32[lat(t), :] * qdec).astype(BF16), dos[t * c:(t + 1) * c, :])

        acc3f = jnp.zeros((RET_DIM, RET_DIM), F32)
        acc3b = jnp.zeros((RET_DIM, RET_DIM), F32)
        state = jnp.zeros((RET_DIM, RET_DIM), F32)
        for t in range(nt - 1, -1, -1):
            gf_s[t] = state.astype(BF16)
            z = zmat(t, dec.q_f)
            acc3f = acc3f + hfa[t] * z
            state = step_f * state + z
        gctx_f = state.astype(BF16)
        state = jnp.zeros((RET_DIM, RET_DIM), F32)
        for t in range(nt):
            gb_s[t] = state.astype(BF16)
            z = zmat(t, dec.q_b)
            acc3b = acc3b + hba[t] * z
            state = step_b * state + z
        gctx_b = state.astype(BF16)

        rc = (lax.broadcasted_iota(jnp.int32, (c, c), 0) - lax.broadcasted_iota(jnp.int32, (c, c), 1)).astype(F32)
        w_diag = _ret_weights(lgf, lgb, rc)
        wg_f = jnp.where(rc >= 0.0, w_diag * rc, 0.0)
        wg_b = jnp.where(rc < 0.0, -w_diag * rc, 0.0)
        accf = jnp.zeros((SUBLANES, RET_DIM), F32)
        accb = jnp.zeros((SUBLANES, RET_DIM), F32)
        gdf = jnp.zeros((SUBLANES, c), F32)
        gdb = jnp.zeros((SUBLANES, c), F32)
        for t in range(nt):
            rows = lat(t)
            qt = qs[rows, :]
            kt = ks[rows, :]
            vt = vs[rows, :]
            dot = dos[t * c:(t + 1) * c, :]
            s = _dot_nt(qt, kt)
            dp = _dot_nt(dot, vt)
            dv = _dot_tn((s * w_diag).astype(BF16), dot)
            ds = (dp * w_diag).astype(BF16)
            dq = _dot(ds, kt)
            dk = _dot_tn(ds, qt)
            gs = dp * s
            gdf = gdf + fold(gs * wg_f)
            gdb = gdb + fold(gs * wg_b)
            qv = qf32[rows, :]
            kv = kf32[rows, :]
            dq_f = dec.q_f * _dot_nt(dot, hf[t])
            dq_b = dec.q_b * _dot_nt(dot, hb[t])
            dk_f = dec.k_f * _dot_nt(vt, gf_s[t])
            dk_b = dec.k_b * _dot_nt(vt, gb_s[t])
            accf = accf + fold(dec.head * dq_f * qv) + fold(dec.tail * dk_f * kv)
            accb = accb + fold(dec.tail * dq_b * qv) + fold(dec.head * dk_b * kv)
            dv = dv + dec.k_f * _dot(kt, gf_s[t]) + dec.k_b * _dot(kt, gb_s[t])
            cosb = cos_ref[rows, :]
            sinb = sin_ref[rows, :]
            d_ref[0, rows, :] = _unrope((dq + dq_f + dq_b) * scale, cosb, sinb).astype(BF16)
            d_ref[1, rows, :] = _unrope(dk + dk_f + dk_b, cosb, sinb).astype(BF16)
            d_ref[2, rows, :] = dv.astype(BF16)
        kc = ks[0:C, :]
        vc = vs[0:C, :]
        kcv = kf32[0:C, :]
        dkc_f = dec_c.k_f * _dot_nt(vc, gctx_f)
        dkc_b = dec_c.k_b * _dot_nt(vc, gctx_b)
        accf = accf + fold(dec_c.tail * dkc_f * kcv)
        accb = accb + fold(dec_c.head * dkc_b * kcv)
        d_ref[1, 0:C, :] = (dkc_f + dkc_b).astype(BF16)
        d_ref[2, 0:C, :] = (dec_c.k_f * _dot(kc, gctx_f) + dec_c.k_b * _dot(kc, gctx_b)).astype(BF16)
        gf = jnp.sum(gdf) + jnp.sum(accf) + jnp.sum(acc3f)
        gb = jnp.sum(gdb) + jnp.sum(accb) + jnp.sum(acc3b)
        row = lax.broadcasted_iota(jnp.int32, (SUBLANES, LANES), 0)
        dlg_ref[...] = jnp.where(row == 0, gf, jnp.where(row == 1, gb, 0.0))

    def col(seg):
        return pl.BlockSpec((None, T, RET_DIM), lambda b, h, seg=seg: (b, 0, seg * RET_HEADS + h))

    return _call_hosting(
        body, hosted, name="ret_bwd", grid=(B, RET_HEADS),
        out_shape=(jax.ShapeDtypeStruct((B, 4, T, RET_WIDTH), BF16),
                   jax.ShapeDtypeStruct((B, SUBLANES, RET_WIDTH), F32),
                   jax.ShapeDtypeStruct((B, RET_HEADS, SUBLANES, LANES), F32)),
        in_specs=[pl.BlockSpec(memory_space=pltpu.SMEM), col(0), col(1), col(2), col(3),
                  pl.BlockSpec((T, RET_DIM), lambda b, h: (0, 0)), pl.BlockSpec((T, RET_DIM), lambda b, h: (0, 0)),
                  pl.BlockSpec((1, RET_DIM), lambda b, h: (0, h)),
                  pl.BlockSpec((None, N, RET_DIM), lambda b, h: (b, 0, h)),
                  pl.BlockSpec((None, N, RET_DIM), lambda b, h: (b, 0, h))],
        out_specs=(pl.BlockSpec((None, 4, T, RET_DIM), lambda b, h: (b, 0, 0, h)),
                   pl.BlockSpec((None, SUBLANES, RET_DIM), lambda b, h: (b, 0, h)),
                   pl.BlockSpec((None, None, SUBLANES, LANES), lambda b, h: (b, h, 0, 0))),
        scratch_shapes=[pltpu.VMEM((T, RET_DIM), BF16)] * 3 + [pltpu.VMEM((N, RET_DIM), BF16)]
                       + [pltpu.VMEM((T, RET_DIM), F32)] * 2
                       + [pltpu.VMEM((nt, RET_DIM, RET_DIM), BF16)] * 2 + [pltpu.VMEM((nt, RET_DIM, RET_DIM), F32)] * 2
                       + [pltpu.VMEM((nt, RET_DIM, RET_DIM), BF16)] * 2,
        args=(lg, proj, proj, proj, proj, cos, sin, gn, o, dlat))


def _na_geometry(rows):
    kh = min(NA_KH, rows)
    return kh, kh * GRID_W


def _pair_select():
    lane = lax.broadcasted_iota(jnp.int32, (2 * GRID_W, LANES), 1)
    row = lax.broadcasted_iota(jnp.int32, (2 * GRID_W, LANES), 0)
    return (lane >= NA_DIM) == (row >= GRID_W)


def _pair_bias(bias_ref, dr0, kh):
    return jnp.concatenate(
        [jnp.concatenate([bias_ref[e, pl.ds(dr0 + 2 * m, 1)].reshape(GRID_W, LANES) for m in range(kh // 2)], axis=1)
         for e in range(2)], axis=0)


def _na_softmax(s_loc, s_ctx):
    mx = jnp.maximum(jnp.max(s_loc, axis=-1, keepdims=True), jnp.max(s_ctx, axis=-1, keepdims=True))
    p_loc = jnp.exp(s_loc - mx)
    p_ctx = jnp.exp(s_ctx - mx)
    den = jnp.sum(p_loc, axis=-1, keepdims=True) + jnp.sum(p_ctx, axis=-1, keepdims=True)
    return p_loc, p_ctx, den


def _na_fwd(proj, bias2, n_ctx, hosted):
    B, T, _ = proj.shape
    C = n_ctx
    N = T - C
    R = N // GRID_W
    kh, nk = _na_geometry(R)
    scale = NA_DIM ** -0.5
    base = (4 * RET_WIDTH) // LANES

    def body(q_ref, k_ref, v_ref, bias_ref, out_ref, kb16, vb16):
        kb16[...] = k_ref[...].astype(BF16)
        vb16[...] = v_ref[...].astype(BF16)
        kc = kb16[0:C, :]
        vc = vb16[0:C, :]
        lane = lax.broadcasted_iota(jnp.int32, (GRID_W, LANES), 1)
        sel2 = _pair_select()

        def group(gi, carry):
            pre = []
            for u in range(NA_GROUP):
                r = gi * NA_GROUP + u
                bs = jnp.clip(r - kh // 2, 0, R - kh)
                dr0 = bs - r + (NA_KH - 1)
                q = q_ref[pl.ds(pl.multiple_of(C + r * GRID_W, GRID_W), GRID_W), :] * scale
                q2 = jnp.where(sel2, jnp.concatenate([q, q], axis=0), 0.0).astype(BF16)
                band = pl.ds(pl.multiple_of(C + bs * GRID_W, GRID_W), nk)
                s_loc = _dot_nt(q2, kb16[band, :]) + _pair_bias(bias_ref, dr0, kh)
                s_ctx = _dot_nt(q2, kc)
                pre.append((r, band, s_loc, s_ctx))
            mid = [(r, band) + _na_softmax(s_loc, s_ctx) for r, band, s_loc, s_ctx in pre]
            for r, band, p_loc, p_ctx, den in mid:
                o2 = (_dot(p_loc.astype(BF16), vb16[band, :]) + _dot(p_ctx.astype(BF16), vc)) / den
                out_ref[pl.ds(pl.multiple_of(r * GRID_W, GRID_W), GRID_W), :] = jnp.where(
                    lane < NA_DIM, o2[:GRID_W], o2[GRID_W:]).astype(BF16)
            return carry

        lax.fori_loop(0, R // NA_GROUP, group, 0)

    def col(seg):
        return pl.BlockSpec((None, T, LANES), lambda b, p, seg=seg: (b, 0, base + seg * NA_PAIRS + p))

    return _call_hosting(
        body, hosted, name="na_fwd", grid=(B, NA_PAIRS),
        out_shape=(jax.ShapeDtypeStruct((B, N, NA_WIDTH), BF16),),
        in_specs=[col(0), col(1), col(2),
                  pl.BlockSpec((2, 2 * NA_KH - 2, GRID_W, LANES), lambda b, p: (p, 0, 0, 0))],
        out_specs=(pl.BlockSpec((None, N, LANES), lambda b, p: (b, 0, p)),),
        scratch_shapes=[pltpu.VMEM((T, LANES), BF16)] * 2,
        args=(proj, proj, proj, bias2))


def _na_bwd(proj, bias2, dlat, n_ctx, hosted):
    B, T, _ = proj.shape
    C = n_ctx
    N = T - C
    R = N // GRID_W
    kh, nk = _na_geometry(R)
    scale = NA_DIM ** -0.5
    base = (4 * RET_WIDTH) // LANES

    def body(q_ref, k_ref, v_ref, bias_ref, dl_ref, d_ref, db_ref, kb16, vb16, dkv):
        b = pl.program_id(1)
        kb16[...] = k_ref[...].astype(BF16)
        vb16[...] = v_ref[...].astype(BF16)
        kc = kb16[0:C, :]
        vc = vb16[0:C, :]
        lane = lax.broadcasted_iota(jnp.int32, (GRID_W, LANES), 1)
        dkv[...] = jnp.zeros(dkv.shape, F32)
        d_ref[0, 0:C, :] = jnp.zeros((C, LANES), BF16)

        @pl.when(b == 0)
        def _():
            db_ref[...] = jnp.zeros(db_ref.shape, F32)

        sel2 = _pair_select()

        def group(gi, carry):
            pre = []
            for u in range(NA_GROUP):
                r = gi * NA_GROUP + u
                bs = jnp.clip(r - kh // 2, 0, R - kh)
                dr0 = bs - r + (NA_KH - 1)
                q = q_ref[pl.ds(pl.multiple_of(C + r * GRID_W, GRID_W), GRID_W), :] * scale
                do = dl_ref[pl.ds(pl.multiple_of(r * GRID_W, GRID_W), GRID_W), :]
                q2 = jnp.where(sel2, jnp.concatenate([q, q], axis=0), 0.0).astype(BF16)
                do2 = jnp.where(sel2, jnp.concatenate([do, do], axis=0), 0.0).astype(BF16)
                band = pl.ds(pl.multiple_of(C + bs * GRID_W, GRID_W), nk)
                s_loc = _dot_nt(q2, kb16[band, :]) + _pair_bias(bias_ref, dr0, kh)
                s_ctx = _dot_nt(q2, kc)
                dp_loc = _dot_nt(do2, vb16[band, :])
                dp_ctx = _dot_nt(do2, vc)
                pre.append((r, dr0, band, q2, do2, s_loc, s_ctx, dp_loc, dp_ctx))
            mid = []
            for r, dr0, band, q2, do2, s_loc, s_ctx, dp_loc, dp_ctx in pre:
                p_loc, p_ctx, den = _na_softmax(s_loc, s_ctx)
                inv = 1.0 / den
                p_loc = p_loc * inv
                p_ctx = p_ctx * inv
                delta = (jnp.sum(p_loc * dp_loc, axis=-1, keepdims=True)
                         + jnp.sum(p_ctx * dp_ctx, axis=-1, keepdims=True))
                ds_loc = p_loc * (dp_loc - delta)
                ds_ctx = p_ctx * (dp_ctx - delta)
                mid.append((r, dr0, band, q2, do2, p_loc.astype(BF16), p_ctx.astype(BF16), ds_loc, ds_ctx))
            for r, dr0, band, q2, do2, pb_loc, pb_ctx, ds_loc, ds_ctx in mid:
                dsb_loc = ds_loc.astype(BF16)
                dsb_ctx = ds_ctx.astype(BF16)
                dq2 = _dot(dsb_loc, kb16[band, :]) + _dot(dsb_ctx, kc)
                d_ref[0, pl.ds(pl.multiple_of(C + r * GRID_W, GRID_W), GRID_W), :] = (jnp.where(
                    lane < NA_DIM, dq2[:GRID_W], dq2[GRID_W:]) * scale).astype(BF16)
                dkv[0, band, :] += _dot_tn(dsb_loc, q2)
                dkv[1, band, :] += _dot_tn(pb_loc, do2)
                dkv[0, 0:C, :] += _dot_tn(dsb_ctx, q2)
                dkv[1, 0:C, :] += _dot_tn(pb_ctx, do2)
                for e in range(2):
                    for m in range(kh // 2):
                        db_ref[e, pl.ds(dr0 + 2 * m, 1)] += ds_loc[e * GRID_W:(e + 1) * GRID_W,
                                                                   m * LANES:(m + 1) * LANES].reshape(1, GRID_W, LANES)
            return carry

        lax.fori_loop(0, R // NA_GROUP, group, 0)
        d_ref[1] = dkv[0].astype(BF16)
        d_ref[2] = dkv[1].astype(BF16)

    def col(seg):
        return pl.BlockSpec((None, T, LANES), lambda p, b, seg=seg: (b, 0, base + seg * NA_PAIRS + p))

    return _call_hosting(
        body, hosted, name="na_bwd", grid=(NA_PAIRS, B),
        out_shape=(jax.ShapeDtypeStruct((B, 3, T, NA_WIDTH), BF16),
                   jax.ShapeDtypeStruct((NA_HEADS, 2 * NA_KH - 2, GRID_W, LANES), F32)),
        in_specs=[col(0), col(1), col(2),
                  pl.BlockSpec((2, 2 * NA_KH - 2, GRID_W, LANES), lambda p, b: (p, 0, 0, 0)),
                  pl.BlockSpec((None, N, LANES), lambda p, b: (b, 0, p))],
        out_specs=(pl.BlockSpec((None, 3, T, LANES), lambda p, b: (b, 0, 0, p)),
                   pl.BlockSpec((2, 2 * NA_KH - 2, GRID_W, LANES), lambda p, b: (p, 0, 0, 0))),
        scratch_shapes=[pltpu.VMEM((T, LANES), BF16)] * 2 + [pltpu.VMEM((2, T, LANES), F32)],
        args=(proj, proj, proj, bias2, dlat))


def _split3(a):
    hi = a.astype(BF16)
    r1 = a - hi.astype(F32)
    mid = r1.astype(BF16)
    lo = (r1 - mid.astype(F32)).astype(BF16)
    return hi, mid, lo


def _rpb_reduce(dbias2, onehot2):
    rows = dbias2.shape[0] * dbias2.shape[1]
    flat = dbias2.reshape(rows, GRID_W * LANES)

    def body(a_ref, oh_ref, o_ref):
        hi, mid, lo = _split3(a_ref[...])
        oh = oh_ref[...]
        o_ref[...] = _dot(hi, oh) + _dot(mid, oh) + _dot(lo, oh)

    return pl.pallas_call(
        body, name="rpb_reduce", out_shape=jax.ShapeDtypeStruct((rows, LANES), F32),
        in_specs=[_vmem(), _vmem()], out_specs=_vmem(),
        compiler_params=pltpu.CompilerParams(vmem_limit_bytes=VMEM_LIMIT),
    )(flat, onehot2)


def _dense_core(lat_ret, lat_na, x, tgt, modl, g_post_mix, g_pre_mlp, g_post_mlp, w_out, w1, w2):
    B, N, D = x.shape
    F = w1.shape[1]
    mixw = w_out.shape[0]
    half = mixw // 2
    tm = _div_tile(N, 256, 16)
    nt = N // tm
    fc = _div_tile(F, 1024, LANES)

    def body(lr_ref, ln_ref, x_ref, t_ref, gt1_ref, sh2_ref, sc2_ref, gt2_ref, gpm_ref, gpre_ref, gpo_ref,
             wout_hbm, w1_hbm, w2_hbm,
             dy1_ref, dlr_ref, dln_ref, dmix_ref, h2_ref, a_ref, du_ref, dz_ref, red_ref,
             wout_v, w1_v, w2_v, u_s, sems):
        @pl.when((pl.program_id(0) == 0) & (pl.program_id(1) == 0))
        def _():
            cps = [pltpu.make_async_copy(wout_hbm, wout_v, sems.at[0]),
                   pltpu.make_async_copy(w1_hbm, w1_v, sems.at[1]),
                   pltpu.make_async_copy(w2_hbm, w2_v, sems.at[2])]
            for cp in cps:
                cp.start()
            for cp in cps:
                cp.wait()

        gt1 = gt1_ref[...]
        sh2 = sh2_ref[...]
        sc2 = sc2_ref[...]
        gt2 = gt2_ref[...]
        gpm = gpm_ref[...]
        gpre = gpre_ref[...]
        gpo = gpo_ref[...]

        def rowmean(a):
            return jnp.mean(a, axis=-1, keepdims=True)

        def colsum(a):
            return jnp.sum(a, axis=0, keepdims=True)

        mix = _dot(lr_ref[...], wout_v[0:half, :]) + _dot(ln_ref[...], wout_v[half:, :])
        x = x_ref[...]
        rm = lax.rsqrt(rowmean(mix * mix) + NORM_EPS)
        mh = mix * rm
        nm = mh * gpm
        y1 = x + gt1 * nm
        r1 = lax.rsqrt(rowmean(y1 * y1) + NORM_EPS)
        xh = y1 * r1
        n1 = xh * gpre
        h2b = (n1 * (1.0 + sc2) + sh2).astype(BF16)
        h2_ref[...] = h2b
        z = jnp.zeros((tm, D), F32)
        for c0 in range(0, F, fc):
            u = _dot(h2b, w1_v[:, c0:c0 + fc])
            u_s[:, c0:c0 + fc] = u
            ru = jnp.maximum(u, 0.0)
            ab = (ru * ru).astype(BF16)
            a_ref[:, c0:c0 + fc] = ab
            z = z + _dot(ab, w2_v[c0:c0 + fc, :])
        r2 = lax.rsqrt(rowmean(z * z) + NORM_EPS)
        zh = z * r2
        n2 = zh * gpo
        y2 = y1 + gt2 * n2
        err = y2 - t_ref[...]
        loss = 0.5 * jnp.sum(rowmean(err * err))
        dy2 = err * (1.0 / D)
        red_ref[2:3, :] = colsum(dy2 * n2)
        dn2 = dy2 * gt2
        red_ref[6:7, :] = colsum(dn2 * zh)
        dzh = dn2 * gpo
        dz = r2 * (dzh - zh * rowmean(dzh * zh))
        dzb = dz.astype(BF16)
        dz_ref[...] = dzb
        dh2 = jnp.zeros((tm, D), F32)
        for c0 in range(0, F, fc):
            da = _dot_nt(dzb, w2_v[c0:c0 + fc, :])
            dub = (da * (2.0 * jnp.maximum(u_s[:, c0:c0 + fc], 0.0))).astype(BF16)
            du_ref[:, c0:c0 + fc] = dub
            dh2 = dh2 + _dot_nt(dub, w1_v[:, c0:c0 + fc])
        red_ref[3:4, :] = colsum(dh2 * n1)
        red_ref[4:5, :] = colsum(dh2)
        dn1 = dh2 * (1.0 + sc2)
        red_ref[5:6, :] = colsum(dn1 * xh)
        dxh = dn1 * gpre
        dy1 = dy2 + r1 * (dxh - xh * rowmean(dxh * xh))
        dy1_ref[...] = dy1
        red_ref[0:1, :] = colsum(dy1 * nm)
        dnm = dy1 * gt1
        red_ref[1:2, :] = colsum(dnm * mh)
        dmh = dnm * gpm
        dmix = (rm * (dmh - mh * rowmean(dmh * mh))).astype(BF16)
        dmix_ref[...] = dmix
        dlr_ref[...] = _dot_nt(dmix, wout_v[0:half, :])
        dln_ref[...] = _dot_nt(dmix, wout_v[half:, :])
        red_ref[7:8, :] = jnp.zeros((1, D), F32) + loss

    def tok(w):
        return pl.BlockSpec((None, tm, w), lambda b, t: (b, t, 0))

    def mod(k):
        return pl.BlockSpec((None, None, 1, D), lambda b, t, k=k: (b, k, 0, 0))

    def vec():
        return pl.BlockSpec((1, D), lambda b, t: (0, 0))

    return pl.pallas_call(
        body, name="dense_core", grid=(B, nt),
        out_shape=(jax.ShapeDtypeStruct((B, N, D), F32), jax.ShapeDtypeStruct((B, N, half), F32),
                   jax.ShapeDtypeStruct((B, N, half), F32), jax.ShapeDtypeStruct((B, N, D), BF16),
                   jax.ShapeDtypeStruct((B, N, D), BF16), jax.ShapeDtypeStruct((B, N, F), BF16),
                   jax.ShapeDtypeStruct((B, N, F), BF16), jax.ShapeDtypeStruct((B, N, D), BF16),
                   jax.ShapeDtypeStruct((B, nt, SUBLANES, D), F32)),
        in_specs=[tok(half), tok(half), tok(D), tok(D), mod(2), mod(3), mod(4), mod(5), vec(), vec(), vec(),
                  _any(), _any(), _any()],
        out_specs=(tok(D), tok(half), tok(half), tok(D), tok(D), tok(F), tok(F), tok(D),
                   pl.BlockSpec((None, None, SUBLANES, D), lambda b, t: (b, t, 0, 0))),
        scratch_shapes=[pltpu.VMEM((mixw, D), BF16), pltpu.VMEM((D, F), BF16), pltpu.VMEM((F, D), BF16),
                        pltpu.VMEM((tm, F), F32), pltpu.SemaphoreType.DMA((3,))],
        compiler_params=_params("arbitrary", "arbitrary"),
    )(lat_ret, lat_na, x, tgt, modl, modl, modl, modl, g_post_mix, g_pre_mlp, g_post_mlp, w_out, w1, w2)


def _inproj_bwd(dret, dna, x_all, dy1, modl, g1, w_in_t, n_ctx, hosted):
    B, T, D = x_all.shape
    N = T - n_ctx
    tm = _div_tile(n_ctx, 256, 16)
    nct = n_ctx // tm
    nt = T // tm
    nseg_r = dret.shape[1]
    nseg_n = dna.shape[1]
    nw = w_in_t.shape[0]

    def body(*refs):
        seg_refs = refs[:nseg_r + nseg_n]
        x_ref, dy1_ref, sc_ref, g_ref, w_ref, dx_ref, red_ref = refs[nseg_r + nseg_n:]
        t = pl.program_id(1)
        dh = jnp.zeros((tm, D), F32)
        for s, ref in enumerate(seg_refs):
            dh = dh + _dot(ref[...], w_ref[s * SEG:(s + 1) * SEG, :])
        x = x_ref[...]
        g = g_ref[...]
        r = lax.rsqrt(jnp.mean(x * x, axis=-1, keepdims=True) + NORM_EPS)
        xh = x * r
        red_ref[0:1, :] = jnp.sum(dh, axis=0, keepdims=True)
        red_ref[1:2, :] = jnp.sum(dh * (xh * g), axis=0, keepdims=True)
        dn = dh * (1.0 + sc_ref[...])
        red_ref[2:3, :] = jnp.sum(dn * xh, axis=0, keepdims=True)
        red_ref[3:, :] = jnp.zeros((SUBLANES - 3, D), F32)
        dxh = dn * g
        dx = r * (dxh - xh * jnp.mean(dxh * xh, axis=-1, keepdims=True))
        dx_ref[...] = dx + jnp.where(t >= nct, dy1_ref[...], 0.0)

    def mrow(b, t):
        return jnp.where(t < nct, B, b)

    def seg(s):
        return pl.BlockSpec((None, None, tm, SEG), lambda b, t, s=s: (b, s, t, 0))

    def lat_tile():
        return pl.BlockSpec((None, tm, D), lambda b, t: (b, jnp.maximum(t - nct, 0), 0))

    return _call_hosting(
        body, hosted, name="inproj_bwd", grid=(B, nt),
        out_shape=(jax.ShapeDtypeStruct((B, N, D), F32), jax.ShapeDtypeStruct((B, nt, SUBLANES, D), F32)),
        in_specs=[seg(s) for s in range(nseg_r)] + [seg(s) for s in range(nseg_n)]
                 + [pl.BlockSpec((None, tm, D), lambda b, t: (b, t, 0)), lat_tile(),
                    pl.BlockSpec((None, None, 1, D), lambda b, t: (mrow(b, t), 1, 0, 0)),
                    pl.BlockSpec((1, D), lambda b, t: (0, 0)),
                    pl.BlockSpec((nw, D), lambda b, t: (0, 0))],
        out_specs=(lat_tile(), pl.BlockSpec((None, None, SUBLANES, D), lambda b, t: (b, t, 0, 0))),
        scratch_shapes=[], args=(*([dret] * nseg_r), *([dna] * nseg_n), x_all, dy1, modl, g1, w_in_t))


def _tn_matmul(lhs, rhs, name):
    B, S, T, W = lhs.shape
    nn = rhs.shape[-1]
    tk = _div_tile(T, 1024, LANES)
    bm = _div_tile(W, 1024, LANES)
    bn = _div_tile(nn, 1024, LANES)
    nkt = T // tk
    nk = B * nkt

    def body(l_ref, r_ref, o_ref, acc):
        k = pl.program_id(3)

        @pl.when(k == 0)
        def _():
            acc[...] = jnp.zeros(acc.shape, F32)

        acc[...] += _dot_tn(l_ref[...].astype(BF16), r_ref[...].astype(BF16))

        @pl.when(k == nk - 1)
        def _():
            o_ref[...] = acc[...].astype(BF16)

    nwb = W // bm
    return pl.pallas_call(
        functools.partial(body), name=name, grid=(S, nwb, nn // bn, nk),
        out_shape=jax.ShapeDtypeStruct((S * W, nn), BF16),
        in_specs=[pl.BlockSpec((None, None, tk, bm), lambda s, i, j, k: (k // nkt, s, k % nkt, i)),
                  pl.BlockSpec((None, tk, bn), lambda s, i, j, k: (k // nkt, k % nkt, j))],
        out_specs=pl.BlockSpec((bm, bn), lambda s, i, j, k: (s * nwb + i, j)),
        scratch_shapes=[pltpu.VMEM((bm, bn), F32)],
        compiler_params=_params("parallel", "parallel", "parallel", "arbitrary"),
    )(lhs, rhs)


def _sum_slots(buf, name):
    _, rows, cols = buf.shape
    tr = _div_tile(rows, 256, 2 * SUBLANES)

    def body(b_ref, o_ref):
        acc = b_ref[0].astype(F32)
        for k in range(1, N_DEV):
            acc = acc + b_ref[k].astype(F32)
        o_ref[...] = acc

    return pl.pallas_call(
        functools.partial(body), name=name, grid=(rows // tr,),
        out_shape=jax.ShapeDtypeStruct((rows, cols), F32),
        in_specs=[pl.BlockSpec((N_DEV, tr, cols), lambda i: (0, i, 0))],
        out_specs=pl.BlockSpec((tr, cols), lambda i: (i, 0)),
        compiler_params=_params("parallel"),
    )(buf)


def _small_ar(vec, dmods, silu_all, w_ada, c_ctx):
    rv = vec.shape[0]
    D = silu_all.shape[1]
    ncol = w_ada.shape[1]
    nm = dmods.shape[1]
    srows = silu_all.shape[0]

    def body(vec_ref, dm_ref, s_ref, w_ref, cc_ref, tot_ref, gb_ref, gw_ref, gc_ref,
             vbuf, mbuf, tbuf, dmx, send1, recv1, send3, recv3):
        me, _ = _me_and_peers()
        vbuf[me] = vec_ref[...]
        mbuf[me] = dm_ref[...]
        both = [(lambda p: vbuf.at[me], lambda p: vbuf.at[p]), (lambda p: mbuf.at[me], lambda p: mbuf.at[p])]
        _push_start(both, ALL_PEERS, send1, recv1)
        _push_wait_recv(both, ALL_PEERS, send1, recv1)
        _push_wait_send(both, ALL_PEERS, send1, recv1)
        tot = vbuf[0]
        msum = mbuf[0]
        for k in range(1, N_DEV):
            tot = tot + vbuf[k]
            msum = msum + mbuf[k]
        tot_ref[...] = tot
        gb_ref[...] = jnp.sum(msum, axis=0, keepdims=True)
        loc = pl.ds(pl.multiple_of(me * ncol, ncol), ncol)
        for k in range(N_DEV):
            dmx[k * SUBLANES:(k + 1) * SUBLANES, :] = mbuf[k, :, loc]
        cm = msum[2:3, :]
        mbuf[0, 2:3, :] = cm
        cm_loc = mbuf[0, 2:3, loc]
        dmx[N_DEV * SUBLANES:, :] = jnp.concatenate([cm_loc, jnp.zeros((SUBLANES - 1, ncol), F32)], axis=0)
        gw_ref[...] = _dot_tn(s_ref[...], dmx[...])
        tbuf[me] = _dot_nt(dmx[N_DEV * SUBLANES:, :], w_ref[...])
        _exchange(lambda p: tbuf.at[me], lambda p: tbuf.at[p], send3, recv3)
        tsum = tbuf[0]
        for k in range(1, N_DEV):
            tsum = tsum + tbuf[k]
        cc = cc_ref[...]
        sg = _sigmoid(cc)
        gc_ref[...] = tsum[0:1, :] * (sg * (1.0 + cc * (1.0 - sg)))

    return pl.pallas_call(
        body, name="small_ar",
        out_shape=(jax.ShapeDtypeStruct((rv, LANES), F32), jax.ShapeDtypeStruct((1, nm), F32),
                   jax.ShapeDtypeStruct((D, ncol), F32), jax.ShapeDtypeStruct((1, D), F32)),
        in_specs=[_vmem()] * 5, out_specs=(_vmem(),) * 4,
        scratch_shapes=[pltpu.VMEM((N_DEV, rv, LANES), F32), pltpu.VMEM((N_DEV, SUBLANES, nm), F32),
                        pltpu.VMEM((N_DEV, SUBLANES, D), F32), pltpu.VMEM((srows, ncol), F32)]
                       + [pltpu.SemaphoreType.DMA((2, N_DEV - 1))] * 2 + [pltpu.SemaphoreType.DMA((N_DEV - 1,))] * 2,
        compiler_params=pltpu.CompilerParams(vmem_limit_bytes=VMEM_LIMIT),
    )(vec, dmods, silu_all, w_ada, c_ctx.reshape(1, D))


def _adam_update(w, g, m, v):
    mn = ADAM_B1 * m + (1.0 - ADAM_B1) * g
    vn = ADAM_B2 * v + (1.0 - ADAM_B2) * (g * g)
    m_hat = mn / (1.0 - ADAM_B1 ** ADAM_STEP)
    v_hat = vn / (1.0 - ADAM_B2 ** ADAM_STEP)
    return -ADAM_LR * (m_hat / (jnp.sqrt(v_hat) + ADAM_EPS) + ADAM_WD * w), mn, vn


def _adamw(w, g, m, v, name):
    rows, cols = w.shape
    tr = _div_tile(rows, 256, SUBLANES) if rows * cols > 65536 else rows

    def body(w_ref, g_ref, m_ref, v_ref, d_ref, nm_ref, nv_ref):
        d_ref[...], nm_ref[...], nv_ref[...] = _adam_update(w_ref[...], g_ref[...], m_ref[...], v_ref[...])

    spec = pl.BlockSpec((tr, cols), lambda i: (i, 0))
    return pl.pallas_call(
        functools.partial(body), name=name, grid=(rows // tr,),
        out_shape=(jax.ShapeDtypeStruct((rows, cols), F32),) * 3,
        in_specs=[spec] * 4, out_specs=(spec,) * 3,
        compiler_params=_params("parallel"),
    )(w, g, m, v)


def _sum_adamw(buf, w, m, v, name):
    _, rows, cols = buf.shape
    tr = _div_tile(rows, 256, 2 * SUBLANES)

    def body(b_ref, w_ref, m_ref, v_ref, g_ref, d_ref, nm_ref, nv_ref):
        g = b_ref[0].astype(F32)
        for k in range(1, N_DEV):
            g = g + b_ref[k].astype(F32)
        g_ref[...] = g
        d_ref[...], nm_ref[...], nv_ref[...] = _adam_update(w_ref[...], g, m_ref[...], v_ref[...])

    spec = pl.BlockSpec((tr, cols), lambda i: (i, 0))
    return pl.pallas_call(
        functools.partial(body), name=name, grid=(rows // tr,),
        out_shape=(jax.ShapeDtypeStruct((rows, cols), F32),) * 4,
        in_specs=[pl.BlockSpec((N_DEV, tr, cols), lambda i: (0, i, 0))] + [spec] * 3, out_specs=(spec,) * 4,
        compiler_params=_params("parallel"),
    )(buf, w, m, v)


def _rope_tables(n_ctx, n):
    n_freq = RET_DIM // 4
    inv = ROPE_BASE ** (-jnp.arange(n_freq, dtype=F32) / n_freq)
    tok = jnp.arange(n)
    pos_r = (tok // GRID_W).astype(F32)
    pos_c = (tok % GRID_W).astype(F32)
    ang_r = pos_r[:, None] * inv[None, :]
    ang_c = pos_c[:, None] * inv[None, :]
    cos = jnp.concatenate([jnp.cos(ang_r), jnp.cos(ang_r), jnp.cos(ang_c), jnp.cos(ang_c)], axis=-1)
    sin = jnp.concatenate([-jnp.sin(ang_r), jnp.sin(ang_r), -jnp.sin(ang_c), jnp.sin(ang_c)], axis=-1)
    cos = jnp.concatenate([jnp.ones((n_ctx, RET_DIM), F32), cos], axis=0)
    sin = jnp.concatenate([jnp.zeros((n_ctx, RET_DIM), F32), sin], axis=0)
    return cos, sin


def _na_tables():
    q = np.arange(GRID_W)[:, None]
    k = np.arange(GRID_W)[None, :]
    start = np.clip(q - NA_KW // 2, 0, GRID_W - NA_KW)
    valid = (k >= start) & (k < start + NA_KW)
    dc = np.clip(k - q + (NA_KW - 1), 0, 2 * NA_KW - 2)
    ncls = 2 * NA_KW - 1
    onehot = (dc[None] == np.arange(ncls)[:, None, None]) & valid[None]
    oh2 = np.zeros((GRID_W, LANES, LANES), np.float32)
    for c in range(ncls):
        oh2[:, :GRID_W, c] = onehot[c]
        oh2[:, GRID_W:, 32 + c] = onehot[c]
    return onehot.astype(np.float32), valid, oh2.reshape(GRID_W * LANES, LANES)


def _paired_bias(rpb, onehot, valid):
    t = jnp.einsum("hdc,cqk->hdqk", rpb, jnp.asarray(onehot), precision=lax.Precision.HIGHEST)
    t = jnp.where(jnp.asarray(valid)[None, None], t, NEG_INF)
    return jnp.concatenate([t[:, :-1], t[:, 1:]], axis=-1)


def kernel(x, c, ctx, c_ctx, w_ada, b_ada, g_pre_mix, g_post_mix, g_pre_mlp, g_post_mlp, w_in, ret_decay, ret_gn, na_rpb, w_out, w_mlp1, w_mlp2, loss_target, m_c_ctx, m_w_ada, m_b_ada, m_g_pre_mix, m_g_post_mix, m_g_pre_mlp, m_g_post_mlp, m_w_in, m_ret_decay, m_ret_gn, m_na_rpb, m_w_out, m_w_mlp1, m_w_mlp2, v_c_ctx, v_w_ada, v_b_ada, v_g_pre_mix, v_g_post_mix, v_g_pre_mlp, v_g_post_mlp, v_w_in, v_ret_decay, v_ret_gn, v_na_rpb, v_w_out, v_w_mlp1, v_w_mlp2):
    B, N, D = x.shape
    C = ctx.shape[1]
    T = C + N
    me = 4 * lax.axis_index("x") + 2 * lax.axis_index("y") + lax.axis_index("c")

    silu_all, mods_g, win_b, wout_l, w1_l, w2_l = _mod_gather(c, c_ctx, w_ada[0], b_ada, w_in[0].T, w_out[0],
                                                             w_mlp1[0], w_mlp2[0])
    mods_full = mods_g.transpose(1, 0, 2).reshape(mods_g.shape[1], N_MOD * D)
    mine = lax.dynamic_slice_in_dim(mods_full, me * SUBLANES, B, axis=0)
    modl = jnp.concatenate([mine, mods_full[N_DEV * SUBLANES:N_DEV * SUBLANES + 1]], axis=0)
    modl = modl.reshape(B + 1, N_MOD, 1, D)
    rin = w_in.shape[2]
    rout, c1, r2 = wout_l.shape[0], w1_l.shape[1], w2_l.shape[0]

    def rows_of(n):
        return lambda ref: _row_block(ref, n)

    def cols_of(n):
        return lambda ref: _col_block(ref, n)

    cos, sin = _rope_tables(C, N)
    onehot, valid, oh2 = _na_tables()
    bias2 = _paired_bias(na_rpb[0], onehot, valid)
    lg = jax.nn.log_sigmoid(ret_decay[0].astype(F32))

    x_all = jnp.concatenate([ctx, x], axis=1)
    level_one = SIBLING + ICI_SAME_CORE
    h_all, proj, w1_part = _inproj_fwd(
        x_all, modl, g_pre_mix, win_b, C,
        [_Hosted("gather", level_one, [w1_l], [cols_of(c1)], [jax.ShapeDtypeStruct((D, N_DEV * c1), BF16)], True)])
    o_ret, lat_ret, w2_part = _ret_fwd(
        proj, cos, sin, lg, ret_gn, C,
        [_Hosted("gather", level_one, [w2_l], [rows_of(r2)], [jax.ShapeDtypeStruct((N_DEV * r2, D), BF16)], True)])
    lat_na, w1_b, w2_b, wout_b = _na_fwd(
        proj, bias2, C,
        [_HostedRelay([w1_part, w2_part], [cols_of(c1), rows_of(r2)]),
         _Hosted("gather", ALL_PEERS, [wout_l], [rows_of(rout)], [jax.ShapeDtypeStruct((N_DEV * rout, D), BF16)], True)])

    (dy1, dlat_ret, dlat_na, dmix, h2, act, du, dz, red_d) = _dense_core(
        lat_ret, lat_na, x, loss_target, modl, g_post_mix, g_pre_mlp, g_post_mlp, wout_b, w1_b, w2_b)

    gw_out_p = jnp.concatenate([_tn_matmul(lat_ret[:, None], dmix, "gw_out_ret"),
                                _tn_matmul(lat_na[:, None], dmix, "gw_out_na")], axis=0)
    gw1_p = _tn_matmul(h2[:, None], du, "gw_mlp1")
    gw2_p = _tn_matmul(act[:, None], dz, "gw_mlp2")

    dret, dgn_p, dlg_p, b1 = _ret_bwd(
        proj, cos, sin, lg, ret_gn, o_ret, dlat_ret, C,
        [_Hosted("scatter", ALL_PEERS, [gw1_p], [cols_of(c1)], [jax.ShapeDtypeStruct((N_DEV, D, c1), BF16)], True)])
    dna, dbias2, b2, bout = _na_bwd(
        proj, bias2, dlat_na, C,
        [_Hosted("scatter", ALL_PEERS, [gw2_p, gw_out_p], [rows_of(r2), rows_of(rout)],
                 [jax.ShapeDtypeStruct((N_DEV, r2, D), BF16), jax.ShapeDtypeStruct((N_DEV, rout, D), BF16)], True)])
    gwin_t_p = jnp.concatenate([_tn_matmul(dret, h_all, "gw_in_ret"), _tn_matmul(dna, h_all, "gw_in_na")], axis=0)
    grad_x, red_i, bin_ = _inproj_bwd(
        dret, dna, x_all, dy1, modl, g_pre_mix, win_b, C,
        [_Hosted("scatter", ALL_PEERS, [gwin_t_p], [rows_of(rin)], [jax.ShapeDtypeStruct((N_DEV, rin, D), BF16)], True)])

    g_w_in = _sum_slots(bin_, "sum_w_in").T
    fused = {"w_out": _sum_adamw(bout, w_out[0], m_w_out[0], v_w_out[0], "sum_adamw_w_out"),
             "w_mlp1": _sum_adamw(b1, w_mlp1[0], m_w_mlp1[0], v_w_mlp1[0], "sum_adamw_w_mlp1"),
             "w_mlp2": _sum_adamw(b2, w_mlp2[0], m_w_mlp2[0], v_w_mlp2[0], "sum_adamw_w_mlp2")}

    rd = red_d.sum(axis=1)[:, :, :]
    ri = red_i
    nct = ri.shape[1] * C // T
    ri_ctx = ri[:, :nct].sum(axis=(0, 1))
    ri_lat = ri[:, nct:].sum(axis=1)
    d_mods = jnp.concatenate([ri_lat[:, 0], ri_lat[:, 1], rd[:, 0], rd[:, 4], rd[:, 3], rd[:, 2]], axis=-1)
    d_cmods = jnp.concatenate([ri_ctx[0], ri_ctx[1], jnp.zeros(((N_MOD - 2) * D,), F32)])[None]
    dm_slot = jnp.concatenate([d_mods, d_cmods, jnp.zeros((SUBLANES - B - 1, N_MOD * D), F32)], axis=0)
    dg_pre_mix = ri_lat[:, 2].sum(axis=0) + ri_ctx[2]
    dg_post_mix = rd[:, 1].sum(axis=0)
    dg_pre_mlp = rd[:, 5].sum(axis=0)
    dg_post_mlp = rd[:, 6].sum(axis=0)
    loss_p = rd[:, 7, 0].sum()
    d_gn = dgn_p[:, 0].sum(axis=0)
    d_lg = dlg_p[:, :, :2, 0].sum(axis=0).T
    d_decay = d_lg * jax.nn.sigmoid(-ret_decay[0].astype(F32))
    rr = _rpb_reduce(dbias2, jnp.asarray(oh2, BF16)).reshape(NA_HEADS, 2 * NA_KH - 2, LANES)
    ncls = 2 * NA_KW - 1
    d_rpb = (jnp.pad(rr[:, :, :ncls], ((0, 0), (0, 1), (0, 0))) + jnp.pad(rr[:, :, 32:32 + ncls], ((0, 0), (1, 0), (0, 0))))
    d_rpb32 = jnp.pad(d_rpb, ((0, 0), (0, 0), (0, 32 - ncls)))
    pieces = [dg_pre_mix, dg_post_mix, dg_pre_mlp, dg_post_mlp, d_gn, d_rpb32.reshape(-1),
              jnp.pad(d_decay.reshape(-1), (0, LANES - d_decay.size)), jnp.full((LANES,), loss_p, F32)]
    vec = jnp.concatenate(pieces)
    pad = (-vec.shape[0]) % (SUBLANES * LANES)
    vec = jnp.pad(vec, (0, pad)).reshape(-1, LANES)
    tot, g_b_ada, g_w_ada, g_c_ctx = _small_ar(vec, dm_slot, silu_all, w_ada[0], c_ctx)
    flat = tot.reshape(-1)
    o0 = 0
    g_pre_mix_g = flat[o0:o0 + D]; o0 += D
    g_post_mix_g = flat[o0:o0 + D]; o0 += D
    g_pre_mlp_g = flat[o0:o0 + D]; o0 += D
    g_post_mlp_g = flat[o0:o0 + D]; o0 += D
    g_gn = flat[o0:o0 + RET_WIDTH]; o0 += RET_WIDTH
    nrpb = NA_HEADS * (2 * NA_KH - 1) * 32
    g_rpb = flat[o0:o0 + nrpb].reshape(NA_HEADS, 2 * NA_KH - 1, 32)[:, :, :ncls]; o0 += nrpb
    g_decay = flat[o0:o0 + 2 * RET_HEADS].reshape(2, RET_HEADS); o0 += LANES
    loss = flat[o0]

    grads = {
        "c_ctx": g_c_ctx.reshape(c_ctx.shape), "w_ada": g_w_ada[None], "b_ada": g_b_ada.reshape(b_ada.shape),
        "g_pre_mix": g_pre_mix_g[None], "g_post_mix": g_post_mix_g[None], "g_pre_mlp": g_pre_mlp_g[None],
        "g_post_mlp": g_post_mlp_g[None], "w_in": g_w_in[None], "ret_decay": g_decay[None], "ret_gn": g_gn[None],
        "na_rpb": g_rpb[None], "w_out": fused["w_out"][0][None], "w_mlp1": fused["w_mlp1"][0][None],
        "w_mlp2": fused["w_mlp2"][0][None],
    }
    weights = dict(c_ctx=c_ctx, w_ada=w_ada, b_ada=b_ada, g_pre_mix=g_pre_mix, g_post_mix=g_post_mix,
                   g_pre_mlp=g_pre_mlp, g_post_mlp=g_post_mlp, w_in=w_in, ret_decay=ret_decay, ret_gn=ret_gn,
                   na_rpb=na_rpb, w_out=w_out, w_mlp1=w_mlp1, w_mlp2=w_mlp2)
    m_in = dict(c_ctx=m_c_ctx, w_ada=m_w_ada, b_ada=m_b_ada, g_pre_mix=m_g_pre_mix, g_post_mix=m_g_post_mix,
                g_pre_mlp=m_g_pre_mlp, g_post_mlp=m_g_post_mlp, w_in=m_w_in, ret_decay=m_ret_decay,
                ret_gn=m_ret_gn, na_rpb=m_na_rpb, w_out=m_w_out, w_mlp1=m_w_mlp1, w_mlp2=m_w_mlp2)
    v_in = dict(c_ctx=v_c_ctx, w_ada=v_w_ada, b_ada=v_b_ada, g_pre_mix=v_g_pre_mix, g_post_mix=v_g_post_mix,
                g_pre_mlp=v_g_pre_mlp, g_post_mlp=v_g_post_mlp, w_in=v_w_in, ret_decay=v_ret_decay,
                ret_gn=v_ret_gn, na_rpb=v_na_rpb, w_out=v_w_out, w_mlp1=v_w_mlp1, w_mlp2=v_w_mlp2)
    names = list(weights)
    deltas, new_m, new_v = {}, {}, {}
    for n in names:
        shp = weights[n].shape
        if n in fused:
            deltas[n], new_m[n], new_v[n] = (a.reshape(shp) for a in fused[n][1:])
            continue
        two_d = (-1, shp[-1]) if len(shp) > 1 else (1, shp[0])
        d, nm, nv = _adamw(weights[n].reshape(two_d), grads[n].reshape(two_d), m_in[n].reshape(two_d),
                           v_in[n].reshape(two_d), "adamw_" + n)
        deltas[n], new_m[n], new_v[n] = d.reshape(shp), nm.reshape(shp), nv.reshape(shp)
    return (loss, grad_x, *[grads[n] for n in names], *[deltas[n] for n in names],
            *[new_m[n] for n in names], *[new_v[n] for n in names])
```

```python
import functools
import math

import numpy as np
import jax
import jax.numpy as jnp
from jax import lax
from jax.experimental import pallas as pl
from jax.experimental.pallas import tpu as pltpu

F32 = jnp.float32
BF16 = jnp.bfloat16
MESH = pl.DeviceIdType.MESH

N_DEV = 8
LANES = 128
SUBLANES = 8
VMEM_LIMIT = 60 * 1024 * 1024

GRID_W = 64
RET_HEADS = 4
RET_DIM = 128
RET_WIDTH = RET_HEADS * RET_DIM
NA_HEADS = 8
NA_DIM = 64
NA_WIDTH = NA_HEADS * NA_DIM
NA_PAIRS = NA_HEADS // 2
NA_KH = 8
NA_KW = 16
NA_GROUP = 4
SEG = 512
ROPE_BASE = 10000.0
NORM_EPS = 1e-6
NEG_INF = -1e30
N_MOD = 6

ADAM_LR = 0.001
ADAM_B1 = 0.9
ADAM_B2 = 0.999
ADAM_EPS = 1e-08
ADAM_WD = 0.01
ADAM_STEP = 10


def _dot(a, b):
    return lax.dot_general(a, b, (((1,), (0,)), ((), ())), preferred_element_type=F32)


def _dot_nt(a, b):
    return lax.dot_general(a, b, (((1,), (1,)), ((), ())), preferred_element_type=F32)


def _dot_tn(a, b):
    return lax.dot_general(a, b, (((0,), (0,)), ((), ())), preferred_element_type=F32)


def _sigmoid(x):
    return 1.0 / (1.0 + jnp.exp(-x))


def _div_tile(n, cap, mult):
    if n <= cap:
        return n
    for t in range(cap - cap % mult, 0, -mult):
        if n % t == 0:
            return t
    raise ValueError(f"no tile for {n}")


def _params(*sem):
    return pltpu.CompilerParams(dimension_semantics=tuple(sem) if sem else None,
                                vmem_limit_bytes=VMEM_LIMIT)


def _vmem():
    return pl.BlockSpec(memory_space=pltpu.VMEM)


def _any():
    return pl.BlockSpec(memory_space=pl.ANY)


def _me_and_peers():
    x, y, c = lax.axis_index("x"), lax.axis_index("y"), lax.axis_index("c")
    me = 4 * x + 2 * y + c
    peers = []
    for m in range(1, N_DEV):
        px = 1 - x if (m >> 2) & 1 else x
        py = 1 - y if (m >> 1) & 1 else y
        pc = 1 - c if m & 1 else c
        peers.append(((px, py, pc), 4 * px + 2 * py + pc))
    return me, peers


def _exchange(src_for, dst_from, send_sems, recv_sems):
    me, peers = _me_and_peers()
    sent = []
    for i, (dev, pid) in enumerate(peers):
        cp = pltpu.make_async_remote_copy(src_ref=src_for(pid), dst_ref=dst_from(me),
                                          send_sem=send_sems.at[i], recv_sem=recv_sems.at[i],
                                          device_id=dev, device_id_type=MESH)
        cp.start()
        sent.append(cp)
    for i, (dev, pid) in enumerate(peers):
        pltpu.make_async_remote_copy(src_ref=src_for(pid), dst_ref=dst_from(pid),
                                     send_sem=send_sems.at[i], recv_sem=recv_sems.at[i],
                                     device_id=dev, device_id_type=MESH).wait_recv()
    for cp in sent:
        cp.wait_send()


SIBLING = (1,)
ICI_SAME_CORE = (2, 4, 6)
ALL_PEERS = tuple(range(1, N_DEV))


def _remote(src, dst, send_sem, recv_sem, dev):
    return pltpu.make_async_remote_copy(src_ref=src, dst_ref=dst, send_sem=send_sem, recv_sem=recv_sem,
                                        device_id=dev, device_id_type=MESH)


def _push_start(items, masks, send_sems, recv_sems):
    me, peers = _me_and_peers()
    for k, (src_for, dst_from) in enumerate(items):
        for m in masks:
            dev, pid = peers[m - 1]
            _remote(src_for(pid), dst_from(me), send_sems.at[k, m - 1], recv_sems.at[k, m - 1], dev).start()


def _push_wait_recv(items, masks, send_sems, recv_sems):
    me, peers = _me_and_peers()
    for k, (src_for, dst_from) in enumerate(items):
        for m in masks:
            dev, pid = peers[m - 1]
            _remote(src_for(pid), dst_from(pid), send_sems.at[k, m - 1], recv_sems.at[k, m - 1], dev).wait_recv()


def _push_wait_send(items, masks, send_sems, recv_sems):
    me, peers = _me_and_peers()
    for k, (src_for, dst_from) in enumerate(items):
        for m in masks:
            dev, pid = peers[m - 1]
            _remote(src_for(pid), dst_from(me), send_sems.at[k, m - 1], recv_sems.at[k, m - 1], dev).wait_send()


def _forward_start(items, send_sems, recv_sems):
    me, peers = _me_and_peers()
    sib = peers[0][0]
    for k, (blk_in, blk_out) in enumerate(items):
        for j, m in enumerate(ICI_SAME_CORE):
            pid = peers[m - 1][1]
            _remote(blk_in(pid), blk_out(pid), send_sems.at[k, j], recv_sems.at[k, j], sib).start()


def _forward_wait(items, send_sems, recv_sems):
    me, peers = _me_and_peers()
    sib = peers[0][0]
    for k, (blk_in, blk_out) in enumerate(items):
        for j, m in enumerate(ICI_SAME_CORE):
            got = peers[(m | 1) - 1][1]
            _remote(blk_in(got), blk_out(got), send_sems.at[k, j], recv_sems.at[k, j], sib).wait_recv()
    for k, (blk_in, blk_out) in enumerate(items):
        for j, m in enumerate(ICI_SAME_CORE):
            pid = peers[m - 1][1]
            _remote(blk_in(pid), blk_out(pid), send_sems.at[k, j], recv_sems.at[k, j], sib).wait_send()


def _mod_gather(c, c_ctx, w_ada, b_ada, w_in_t, w_out, w1, w2):
    B, D = c.shape
    ncol = w_ada.shape[1]
    rows = SUBLANES * N_DEV + SUBLANES

    def body(c_ref, cc_ref, w_ref, b_ref, win_ref, wout_ref, w1_ref, w2_ref,
             s_ref, m_ref, gin_ref, wout_b, w1_b, w2_b,
             win_b, msend, send1, recv1, send2, recv2, wsend, wrecv, fsend, frecv, lsem):
        me, _ = _me_and_peers()
        win_b[...] = win_ref[...].astype(BF16)
        block = _row_block(gin_ref, w_in_t.shape[0])
        gather = [(lambda p: win_b, block)]
        own = pltpu.make_async_copy(win_b, block(me), lsem.at[0])
        own.start()
        _push_start(gather, SIBLING + ICI_SAME_CORE, wsend, wrecv)
        wout_b[...] = wout_ref[...].astype(BF16)
        w1_b[...] = w1_ref[...].astype(BF16)
        w2_b[...] = w2_ref[...].astype(BF16)
        cv = c_ref[...]
        slot = jnp.concatenate([cv * _sigmoid(cv), jnp.zeros((SUBLANES - B, D), F32)], axis=0)
        my_rows = pl.ds(pl.multiple_of(me * SUBLANES, SUBLANES), SUBLANES)
        s_ref[my_rows, :] = slot
        ccv = cc_ref[...]
        s_ref[SUBLANES * N_DEV:, :] = jnp.concatenate(
            [ccv * _sigmoid(ccv), jnp.zeros((SUBLANES - 1, D), F32)], axis=0)

        def rows_of(p):
            return s_ref.at[pl.ds(pl.multiple_of(p * SUBLANES, SUBLANES), SUBLANES), :]

        _exchange(lambda p: rows_of(me), rows_of, send1, recv1)
        b_loc = b_ref[:, pl.ds(pl.multiple_of(me * ncol, ncol), ncol)]
        mods = _dot(s_ref[...], w_ref[...]) + b_loc
        for p in range(N_DEV):
            msend[p] = jnp.concatenate([mods[p * SUBLANES:(p + 1) * SUBLANES], mods[N_DEV * SUBLANES:]], axis=0)
        m_ref[me] = msend[me]
        _exchange(lambda p: msend.at[p], lambda p: m_ref.at[p], send2, recv2)
        _push_wait_recv(gather, ICI_SAME_CORE, wsend, wrecv)
        relay = [(block, block)]
        _forward_start(relay, fsend, frecv)
        _push_wait_recv(gather, SIBLING, wsend, wrecv)
        _forward_wait(relay, fsend, frecv)
        _push_wait_send(gather, SIBLING + ICI_SAME_CORE, wsend, wrecv)
        own.wait()

    return pl.pallas_call(
        body, name="mod_gather",
        out_shape=(jax.ShapeDtypeStruct((rows, D), F32), jax.ShapeDtypeStruct((N_DEV, 2 * SUBLANES, ncol), F32),
                   jax.ShapeDtypeStruct((N_DEV * w_in_t.shape[0], D), BF16),
                   jax.ShapeDtypeStruct(w_out.shape, BF16), jax.ShapeDtypeStruct(w1.shape, BF16),
                   jax.ShapeDtypeStruct(w2.shape, BF16)),
        in_specs=[_vmem()] * 8, out_specs=(_vmem(), _vmem(), _any(), _vmem(), _vmem(), _vmem()),
        scratch_shapes=[pltpu.VMEM(w_in_t.shape, BF16), pltpu.VMEM((N_DEV, 2 * SUBLANES, ncol), F32)]
                       + [pltpu.SemaphoreType.DMA((N_DEV - 1,))] * 4
                       + [pltpu.SemaphoreType.DMA((1, N_DEV - 1))] * 2 + [pltpu.SemaphoreType.DMA((1, 3))] * 2
                       + [pltpu.SemaphoreType.DMA((1,))],
        compiler_params=pltpu.CompilerParams(vmem_limit_bytes=VMEM_LIMIT),
    )(c, c_ctx.reshape(1, D), w_ada, b_ada, w_in_t, w_out, w1, w2)


def _row_block(ref, rows):
    return lambda p: ref.at[pl.ds(pl.multiple_of(p * rows, 2 * SUBLANES), rows), :]


def _col_block(ref, cols):
    return lambda p: ref.at[:, pl.ds(pl.multiple_of(p * cols, LANES), cols)]


def _slot(ref):
    return lambda p: ref.at[p]


class _Hosted:
    def __init__(self, kind, masks, operands, block_of, out_shapes, with_own):
        self.kind, self.masks, self.operands = kind, masks, list(operands)
        self.block_of, self.out_shapes, self.with_own = block_of, list(out_shapes), with_own
        self.n = len(self.operands)

    def scratch(self):
        return [pltpu.SemaphoreType.DMA((self.n, N_DEV - 1)), pltpu.SemaphoreType.DMA((self.n, N_DEV - 1)),
                pltpu.SemaphoreType.DMA((self.n,))]

    def _items(self, in_refs, out_refs):
        items = []
        for k in range(self.n):
            if self.kind == "gather":
                items.append((lambda p, k=k: in_refs[k], self.block_of[k](out_refs[k])))
            else:
                items.append((self.block_of[k](in_refs[k]), _slot(out_refs[k])))
        return items

    def _own(self, in_refs, out_refs, lsem):
        me, _ = _me_and_peers()
        items = self._items(in_refs, out_refs)
        return [pltpu.make_async_copy(src_for(me), dst_from(me), lsem.at[k])
                for k, (src_for, dst_from) in enumerate(items)]

    def start(self, in_refs, out_refs, sems):
        send, recv, lsem = sems
        if self.with_own:
            for cp in self._own(in_refs, out_refs, lsem):
                cp.start()
        _push_start(self._items(in_refs, out_refs), self.masks, send, recv)

    def wait(self, in_refs, out_refs, sems):
        send, recv, lsem = sems
        items = self._items(in_refs, out_refs)
        _push_wait_recv(items, self.masks, send, recv)
        _push_wait_send(items, self.masks, send, recv)
        if self.with_own:
            for cp in self._own(in_refs, out_refs, lsem):
                cp.wait()


class _HostedRelay:
    def __init__(self, arrays, block_of):
        self.operands, self.block_of = list(arrays), block_of
        self.out_shapes = [jax.ShapeDtypeStruct(a.shape, a.dtype) for a in arrays]
        self.n = len(self.operands)

    def scratch(self):
        return [pltpu.SemaphoreType.DMA((self.n, 3)), pltpu.SemaphoreType.DMA((self.n, 3))]

    def _items(self, in_refs, out_refs):
        return [(self.block_of[k](in_refs[k]), self.block_of[k](out_refs[k])) for k in range(self.n)]

    def start(self, in_refs, out_refs, sems):
        _forward_start(self._items(in_refs, out_refs), *sems)

    def wait(self, in_refs, out_refs, sems):
        _forward_wait(self._items(in_refs, out_refs), *sems)


def _call_hosting(body, hosted, *, name, grid, out_shape, in_specs, out_specs, scratch_shapes, args):
    n_in, n_out, n_scr = len(in_specs), len(out_shape), len(scratch_shapes)
    hn = sum(hs.n for hs in hosted)
    n_sem = [len(hs.scratch()) for hs in hosted]

    def wrapped(*refs):
        ins = refs[:n_in]
        h_in = refs[n_in:n_in + hn]
        outs = refs[n_in + hn:n_in + hn + n_out]
        h_out = refs[n_in + hn + n_out:n_in + 2 * hn + n_out]
        scr = refs[n_in + 2 * hn + n_out:n_in + 2 * hn + n_out + n_scr]
        sems = refs[n_in + 2 * hn + n_out + n_scr:]
        ids = [pl.program_id(i) for i in range(len(grid))]
        first = functools.reduce(jnp.logical_and, [i == 0 for i in ids])
        last = functools.reduce(jnp.logical_and, [i == g - 1 for i, g in zip(ids, grid)])
        parts, o0, s0 = [], 0, 0
        for hs, ns in zip(hosted, n_sem):
            parts.append((hs, h_in[o0:o0 + hs.n], h_out[o0:o0 + hs.n], sems[s0:s0 + ns]))
            o0 += hs.n
            s0 += ns

        @pl.when(first)
        def _():
            for hs, hi, ho, se in parts:
                hs.start(hi, ho, se)

        body(*ins, *outs, *scr)

        @pl.when(last)
        def _():
            for hs, hi, ho, se in parts:
                hs.wait(hi, ho, se)

    aliases, o0 = {}, 0
    for hs in hosted:
        if isinstance(hs, _HostedRelay):
            aliases.update({n_in + o0 + k: n_out + o0 + k for k in range(hs.n)})
        o0 += hs.n
    return pl.pallas_call(
        wrapped, name=name, grid=grid,
        out_shape=tuple(out_shape) + tuple(s for hs in hosted for s in hs.out_shapes),
        in_specs=list(in_specs) + [_any()] * hn,
        out_specs=tuple(out_specs) + (_any(),) * hn,
        scratch_shapes=list(scratch_shapes) + [s for hs in hosted for s in hs.scratch()],
        input_output_aliases=aliases,
        compiler_params=_params(*(("arbitrary",) * len(grid))),
    )(*args, *[a for hs in hosted for a in hs.operands])


def _token_tiles(n_ctx, tm):
    nct = n_ctx // tm

    def ctx_spec(D):
        return pl.BlockSpec((None, tm, D), lambda b, t: (b, jnp.minimum(t, nct - 1), 0))

    def lat_spec(D):
        return pl.BlockSpec((None, tm, D), lambda b, t: (b, jnp.maximum(t - nct, 0), 0))

    return nct, ctx_spec, lat_spec


def _inproj_fwd(x, ctx, modl, g1, w_in_t, hosted):
    B, N, D = x.shape
    n_ctx = ctx.shape[1]
    T = n_ctx + N
    nw = w_in_t.shape[0]
    tm = _div_tile(n_ctx, 256, 16)
    nct, ctx_spec, lat_spec = _token_tiles(n_ctx, tm)

    def body(c_ref, x_ref, sh_ref, sc_ref, g_ref, w_ref, h_ref, p_ref):
        x = jnp.where(pl.program_id(1) < nct, c_ref[...], x_ref[...])
        r = lax.rsqrt(jnp.mean(x * x, axis=-1, keepdims=True) + NORM_EPS)
        h = ((x * r) * g_ref[...]) * (1.0 + sc_ref[...]) + sh_ref[...]
        hb = h.astype(BF16)
        h_ref[...] = hb
        p_ref[...] = _dot_nt(hb, w_ref[...])

    def mrow(b, t):
        return jnp.where(t < nct, B, b)

    return _call_hosting(
        body, hosted, name="inproj_fwd", grid=(B, T // tm),
        out_shape=(jax.ShapeDtypeStruct((B, T, D), BF16), jax.ShapeDtypeStruct((B, T, nw), F32)),
        in_specs=[ctx_spec(D), lat_spec(D),
                  pl.BlockSpec((None, None, 1, D), lambda b, t: (mrow(b, t), 0, 0, 0)),
                  pl.BlockSpec((None, None, 1, D), lambda b, t: (mrow(b, t), 1, 0, 0)),
                  pl.BlockSpec((1, D), lambda b, t: (0, 0)),
                  pl.BlockSpec((nw, D), lambda b, t: (0, 0))],
        out_specs=(pl.BlockSpec((None, tm, D), lambda b, t: (b, t, 0)),
                   pl.BlockSpec((None, tm, nw), lambda b, t: (b, t, 0))),
        scratch_shapes=[], args=(ctx, x, modl, modl, g1, w_in_t))


def _swap32(x):
    lane = lax.broadcasted_iota(jnp.int32, x.shape, 1)
    return jnp.where((lane % 64) < 32, pltpu.roll(x, 96, 1), pltpu.roll(x, 32, 1))


def _rope(x, cos, sin):
    return x * cos + _swap32(x) * sin


def _unrope(dy, cos, sin):
    return dy * cos + _swap32(dy * sin)


def _ret_weights(lgf, lgb, dist):
    return jnp.exp(jnp.where(dist >= 0.0, lgf * dist, -lgb * dist))


class _RetDecay:
    def __init__(self, lgf, lgb, rows):
        r = lax.broadcasted_iota(jnp.int32, (rows, RET_DIM), 0).astype(F32)
        self.head = r + 1.0
        self.tail = (rows - 1.0) - r
        self.q_f = jnp.exp(lgf * self.head)
        self.k_f = jnp.exp(lgf * self.tail)
        self.q_b = jnp.exp(lgb * self.tail)
        self.k_b = jnp.exp(lgb * self.head)


def _ret_states(kf32, vs, lgf, lgb, C, c, nt, hf, hb, hfa=None, hba=None):
    dec = _RetDecay(lgf, lgb, c)
    dec_c = _RetDecay(lgf, lgb, C)
    step_f = jnp.exp(jnp.zeros((RET_DIM, RET_DIM), F32) + lgf * c)
    step_b = jnp.exp(jnp.zeros((RET_DIM, RET_DIM), F32) + lgb * c)

    def upd(rows, kdec):
        return _dot_tn((kf32[rows, :] * kdec).astype(BF16), vs[rows, :])

    def lat(t):
        return slice(C + t * c, C + (t + 1) * c)

    state = upd(slice(0, C), dec_c.k_f)
    aged = jnp.zeros_like(state)
    for t in range(nt):
        hf[t] = state.astype(BF16)
        if hfa is not None:
            hfa[t] = aged
        if t < nt - 1:
            aged = step_f * (aged + c * state)
            state = step_f * state + upd(lat(t), dec.k_f)
    state = upd(slice(0, C), dec_c.k_b)
    aged = jnp.zeros_like(state)
    for t in range(nt - 1, -1, -1):
        hb[t] = state.astype(BF16)
        if hba is not None:
            hba[t] = aged
        if t > 0:
            aged = step_b * (aged + c * state)
            state = step_b * state + upd(lat(t), dec.k_b)
    return dec, dec_c, step_f, step_b


def _ret_fwd(proj, cos, sin, lg, gn, n_ctx, hosted):
    B, T, _ = proj.shape
    C = n_ctx
    N = T - C
    c = _div_tile(N, 256, 16)
    nt = N // c
    scale = RET_DIM ** -0.5

    def body(lg_ref, q_ref, k_ref, v_ref, g_ref, cos_ref, sin_ref, gn_ref, o_ref, lat_ref, qs, ks, vs, kf32, hf, hb):
        h = pl.program_id(1)
        lgf = lg_ref[0, h]
        lgb = lg_ref[1, h]
        for rows in [slice(0, C)] + [slice(C + t * c, C + (t + 1) * c) for t in range(nt)]:
            cosb = cos_ref[rows, :]
            sinb = sin_ref[rows, :]
            qs[rows, :] = (_rope(q_ref[rows, :], cosb, sinb) * scale).astype(BF16)
            kr = _rope(k_ref[rows, :], cosb, sinb)
            kf32[rows, :] = kr
            ks[rows, :] = kr.astype(BF16)
            vs[rows, :] = v_ref[rows, :].astype(BF16)
        gnv = gn_ref[...]
        dec, _, _, _ = _ret_states(kf32, vs, lgf, lgb, C, c, nt, hf, hb)
        rc = (lax.broadcasted_iota(jnp.int32, (c, c), 0) - lax.broadcasted_iota(jnp.int32, (c, c), 1)).astype(F32)
        w_diag = _ret_weights(lgf, lgb, rc)
        for t in range(nt):
            rows = slice(C + t * c, C + (t + 1) * c)
            qt = qs[rows, :]
            s = _dot_nt(qt, ks[rows, :])
            o = (_dot((s * w_diag).astype(BF16), vs[rows, :])
                 + dec.q_f * _dot(qt, hf[t]) + dec.q_b * _dot(qt, hb[t]))
            o_ref[t * c:(t + 1) * c, :] = o
            mu = jnp.mean(o, axis=-1, keepdims=True)
            oc = o - mu
            var = jnp.mean(oc * oc, axis=-1, keepdims=True)
            yh = oc * lax.rsqrt(var + NORM_EPS)
            g = g_ref[rows, :]
            lat_ref[t * c:(t + 1) * c, :] = ((yh * gnv) * (g * _sigmoid(g))).astype(BF16)

    def col(seg):
        return pl.BlockSpec((None, T, RET_DIM), lambda b, h, seg=seg: (b, 0, seg * RET_HEADS + h))

    return _call_hosting(
        body, hosted, name="ret_fwd", grid=(B, RET_HEADS),
        out_shape=(jax.ShapeDtypeStruct((B, N, RET_WIDTH), F32), jax.ShapeDtypeStruct((B, N, RET_WIDTH), BF16)),
        in_specs=[pl.BlockSpec(memory_space=pltpu.SMEM), col(0), col(1), col(2), col(3),
                  pl.BlockSpec((T, RET_DIM), lambda b, h: (0, 0)), pl.BlockSpec((T, RET_DIM), lambda b, h: (0, 0)),
                  pl.BlockSpec((1, RET_DIM), lambda b, h: (0, h))],
        out_specs=(pl.BlockSpec((None, N, RET_DIM), lambda b, h: (b, 0, h)),
                   pl.BlockSpec((None, N, RET_DIM), lambda b, h: (b, 0, h))),
        scratch_shapes=[pltpu.VMEM((T, RET_DIM), BF16)] * 3 + [pltpu.VMEM((T, RET_DIM), F32)]
                       + [pltpu.VMEM((nt, RET_DIM, RET_DIM), BF16)] * 2,
        args=(lg, proj, proj, proj, proj, cos, sin, gn))


def _ret_bwd(proj, cos, sin, lg, gn, o, dlat, n_ctx, hosted):
    B, T, _ = proj.shape
    C = n_ctx
    N = T - C
    c = _div_tile(N, 256, 16)
    nt = N // c
    scale = RET_DIM ** -0.5

    def lat(t):
        return slice(C + t * c, C + (t + 1) * c)

    def body(lg_ref, q_ref, k_ref, v_ref, g_ref, cos_ref, sin_ref, gn_ref, o_ref, dl_ref,
             d_ref, dgn_ref, dlg_ref, qs, ks, vs, dos, qf32, kf32, hf, hb, hfa, hba, gf_s, gb_s):
        h = pl.program_id(1)
        lgf = lg_ref[0, h]
        lgb = lg_ref[1, h]
        gnv = gn_ref[...]

        def fold(a):
            return jnp.sum(a.reshape(a.shape[0] // SUBLANES, SUBLANES, a.shape[1]), axis=0)

        for rows in [slice(0, C)] + [lat(t) for t in range(nt)]:
            cosb = cos_ref[rows, :]
            sinb = sin_ref[rows, :]
            qr = _rope(q_ref[rows, :], cosb, sinb) * scale
            qf32[rows, :] = qr
            qs[rows, :] = qr.astype(BF16)
            kr = _rope(k_ref[rows, :], cosb, sinb)
            kf32[rows, :] = kr
            ks[rows, :] = kr.astype(BF16)
            vs[rows, :] = v_ref[rows, :].astype(BF16)

        dgn = jnp.zeros((1, RET_DIM), F32)
        for t in range(nt):
            lrows = slice(t * c, (t + 1) * c)
            ov = o_ref[lrows, :]
            mu = jnp.mean(ov, axis=-1, keepdims=True)
            oc = ov - mu
            var = jnp.mean(oc * oc, axis=-1, keepdims=True)
            rstd = lax.rsqrt(var + NORM_EPS)
            yh = oc * rstd
            g = g_ref[lat(t), :]
            sg = _sigmoid(g)
            dl = dl_ref[lrows, :]
            d_ref[3, lat(t), :] = (dl * (yh * gnv) * (sg * (1.0 + g * (1.0 - sg)))).astype(BF16)
            dls = dl * (g * sg)
            dgn = dgn + jnp.sum(dls * yh, axis=0, keepdims=True)
            dyh = dls * gnv
            do = rstd * (dyh - jnp.mean(dyh, axis=-1, keepdims=True)
                         - yh * jnp.mean(dyh * yh, axis=-1, keepdims=True))
            dos[lrows, :] = do.astype(BF16)
        dgn_ref[...] = jnp.concatenate([dgn, jnp.zeros((SUBLANES - 1, RET_DIM), F32)], axis=0)
        d_ref[3, 0:C, :] = jnp.zeros((C, RET_DIM), BF16)
        d_ref[0, 0:C, :] = jnp.zeros((C, RET_DIM), BF16)

        dec, dec_c, step_f, step_b = _ret_states(kf32, vs, lgf, lgb, C, c, nt, hf, hb, hfa, hba)

        def zmat(t, qdec):
            return _dot_tn((qf32[lat(t), :] * qdec).astype(BF16), dos[t * c:(t + 1) * c, :])

        acc3f = jnp.zeros((RET_DIM, RET_DIM), F32)
        acc3b = jnp.zeros((RET_DIM, RET_DIM), F32)
        state = jnp.zeros((RET_DIM, RET_DIM), F32)
        for t in range(nt - 1, -1, -1):
            gf_s[t] = state.astype(BF16)
            z = zmat(t, dec.q_f)
            acc3f = acc3f + hfa[t] * z
            state = step_f * state + z
        gctx_f = state.astype(BF16)
        state = jnp.zeros((RET_DIM, RET_DIM), F32)
        for t in range(nt):
            gb_s[t] = state.astype(BF16)
            z = zmat(t, dec.q_b)
            acc3b = acc3b + hba[t] * z
            state = step_b * state + z
        gctx_b = state.astype(BF16)

        rc = (lax.broadcasted_iota(jnp.int32, (c, c), 0) - lax.broadcasted_iota(jnp.int32, (c, c), 1)).astype(F32)
        w_diag = _ret_weights(lgf, lgb, rc)
        wg_f = jnp.where(rc >= 0.0, w_diag * rc, 0.0)
        wg_b = jnp.where(rc < 0.0, -w_diag * rc, 0.0)
        accf = jnp.zeros((SUBLANES, RET_DIM), F32)
        accb = jnp.zeros((SUBLANES, RET_DIM), F32)
        gdf = jnp.zeros((SUBLANES, c), F32)
        gdb = jnp.zeros((SUBLANES, c), F32)
        for t in range(nt):
            rows = lat(t)
            qt = qs[rows, :]
            kt = ks[rows, :]
            vt = vs[rows, :]
            dot = dos[t * c:(t + 1) * c, :]
            s = _dot_nt(qt, kt)
            dp = _dot_nt(dot, vt)
            dv = _dot_tn((s * w_diag).astype(BF16), dot)
            ds = (dp * w_diag).astype(BF16)
            dq = _dot(ds, kt)
            dk = _dot_tn(ds, qt)
            gs = dp * s
            gdf = gdf + fold(gs * wg_f)
            gdb = gdb + fold(gs * wg_b)
            qv = qf32[rows, :]
            kv = kf32[rows, :]
            dq_f = dec.q_f * _dot_nt(dot, hf[t])
            dq_b = dec.q_b * _dot_nt(dot, hb[t])
            dk_f = dec.k_f * _dot_nt(vt, gf_s[t])
            dk_b = dec.k_b * _dot_nt(vt, gb_s[t])
            accf = accf + fold(dec.head * dq_f * qv) + fold(dec.tail * dk_f * kv)
            accb = accb + fold(dec.tail * dq_b * qv) + fold(dec.head * dk_b * kv)
            dv = dv + dec.k_f * _dot(kt, gf_s[t]) + dec.k_b * _dot(kt, gb_s[t])
            cosb = cos_ref[rows, :]
            sinb = sin_ref[rows, :]
            d_ref[0, rows, :] = _unrope((dq + dq_f + dq_b) * scale, cosb, sinb).astype(BF16)
            d_ref[1, rows, :] = _unrope(dk + dk_f + dk_b, cosb, sinb).astype(BF16)
            d_ref[2, rows, :] = dv.astype(BF16)
        kc = ks[0:C, :]
        vc = vs[0:C, :]
        kcv = kf32[0:C, :]
        dkc_f = dec_c.k_f * _dot_nt(vc, gctx_f)
        dkc_b = dec_c.k_b * _dot_nt(vc, gctx_b)
        accf = accf + fold(dec_c.tail * dkc_f * kcv)
        accb = accb + fold(dec_c.head * dkc_b * kcv)
        d_ref[1, 0:C, :] = (dkc_f + dkc_b).astype(BF16)
        d_ref[2, 0:C, :] = (dec_c.k_f * _dot(kc, gctx_f) + dec_c.k_b * _dot(kc, gctx_b)).astype(BF16)
        gf = jnp.sum(gdf) + jnp.sum(accf) + jnp.sum(acc3f)
        gb = jnp.sum(gdb) + jnp.sum(accb) + jnp.sum(acc3b)
        row = lax.broadcasted_iota(jnp.int32, (SUBLANES, LANES), 0)
        dlg_ref[...] = jnp.where(row == 0, gf, jnp.where(row == 1, gb, 0.0))

    def col(seg):
        return pl.BlockSpec((None, T, RET_DIM), lambda b, h, seg=seg: (b, 0, seg * RET_HEADS + h))

    return _call_hosting(
        body, hosted, name="ret_bwd", grid=(B, RET_HEADS),
        out_shape=(jax.ShapeDtypeStruct((B, 4, T, RET_WIDTH), BF16),
                   jax.ShapeDtypeStruct((B, SUBLANES, RET_WIDTH), F32),
                   jax.ShapeDtypeStruct((B, RET_HEADS, SUBLANES, LANES), F32)),
        in_specs=[pl.BlockSpec(memory_space=pltpu.SMEM), col(0), col(1), col(2), col(3),
                  pl.BlockSpec((T, RET_DIM), lambda b, h: (0, 0)), pl.BlockSpec((T, RET_DIM), lambda b, h: (0, 0)),
                  pl.BlockSpec((1, RET_DIM), lambda b, h: (0, h)),
                  pl.BlockSpec((None, N, RET_DIM), lambda b, h: (b, 0, h)),
                  pl.BlockSpec((None, N, RET_DIM), lambda b, h: (b, 0, h))],
        out_specs=(pl.BlockSpec((None, 4, T, RET_DIM), lambda b, h: (b, 0, 0, h)),
                   pl.BlockSpec((None, SUBLANES, RET_DIM), lambda b, h: (b, 0, h)),
                   pl.BlockSpec((None, None, SUBLANES, LANES), lambda b, h: (b, h, 0, 0))),
        scratch_shapes=[pltpu.VMEM((T, RET_DIM), BF16)] * 3 + [pltpu.VMEM((N, RET_DIM), BF16)]
                       + [pltpu.VMEM((T, RET_DIM), F32)] * 2
                       + [pltpu.VMEM((nt, RET_DIM, RET_DIM), BF16)] * 2 + [pltpu.VMEM((nt, RET_DIM, RET_DIM), F32)] * 2
                       + [pltpu.VMEM((nt, RET_DIM, RET_DIM), BF16)] * 2,
        args=(lg, proj, proj, proj, proj, cos, sin, gn, o, dlat))


def _na_geometry(rows):
    kh = min(NA_KH, rows)
    return kh, kh * GRID_W


def _pair_select():
    lane = lax.broadcasted_iota(jnp.int32, (2 * GRID_W, LANES), 1)
    row = lax.broadcasted_iota(jnp.int32, (2 * GRID_W, LANES), 0)
    return (lane >= NA_DIM) == (row >= GRID_W)


def _pair_bias(bias_ref, dr0, kh):
    return jnp.concatenate(
        [jnp.concatenate([bias_ref[e, pl.ds(dr0 + 2 * m, 1)].reshape(GRID_W, LANES) for m in range(kh // 2)], axis=1)
         for e in range(2)], axis=0)


def _na_softmax(s_loc, s_ctx):
    mx = jnp.maximum(jnp.max(s_loc, axis=-1, keepdims=True), jnp.max(s_ctx, axis=-1, keepdims=True))
    p_loc = jnp.exp(s_loc - mx)
    p_ctx = jnp.exp(s_ctx - mx)
    den = jnp.sum(p_loc, axis=-1, keepdims=True) + jnp.sum(p_ctx, axis=-1, keepdims=True)
    return p_loc, p_ctx, den


def _na_fwd(proj, bias2, n_ctx, hosted):
    B, T, _ = proj.shape
    C = n_ctx
    N = T - C
    R = N // GRID_W
    kh, nk = _na_geometry(R)
    scale = NA_DIM ** -0.5
    base = (4 * RET_WIDTH) // LANES

    def body(q_ref, k_ref, v_ref, bias_ref, out_ref, kb16, vb16):
        kb16[...] = k_ref[...].astype(BF16)
        vb16[...] = v_ref[...].astype(BF16)
        kc = kb16[0:C, :]
        vc = vb16[0:C, :]
        lane = lax.broadcasted_iota(jnp.int32, (GRID_W, LANES), 1)
        sel2 = _pair_select()

        def group(gi, carry):
            pre = []
            for u in range(NA_GROUP):
                r = gi * NA_GROUP + u
                bs = jnp.clip(r - kh // 2, 0, R - kh)
                dr0 = bs - r + (NA_KH - 1)
                q = q_ref[pl.ds(pl.multiple_of(C + r * GRID_W, GRID_W), GRID_W), :] * scale
                q2 = jnp.where(sel2, jnp.concatenate([q, q], axis=0), 0.0).astype(BF16)
                band = pl.ds(pl.multiple_of(C + bs * GRID_W, GRID_W), nk)
                s_loc = _dot_nt(q2, kb16[band, :]) + _pair_bias(bias_ref, dr0, kh)
                s_ctx = _dot_nt(q2, kc)
                pre.append((r, band, s_loc, s_ctx))
            mid = [(r, band) + _na_softmax(s_loc, s_ctx) for r, band, s_loc, s_ctx in pre]
            for r, band, p_loc, p_ctx, den in mid:
                o2 = (_dot(p_loc.astype(BF16), vb16[band, :]) + _dot(p_ctx.astype(BF16), vc)) / den
                out_ref[pl.ds(pl.multiple_of(r * GRID_W, GRID_W), GRID_W), :] = jnp.where(
                    lane < NA_DIM, o2[:GRID_W], o2[GRID_W:]).astype(BF16)
            return carry

        lax.fori_loop(0, R // NA_GROUP, group, 0)

    def col(seg):
        return pl.BlockSpec((None, T, LANES), lambda b, p, seg=seg: (b, 0, base + seg * NA_PAIRS + p))

    return _call_hosting(
        body, hosted, name="na_fwd", grid=(B, NA_PAIRS),
        out_shape=(jax.ShapeDtypeStruct((B, N, NA_WIDTH), BF16),),
        in_specs=[col(0), col(1), col(2),
                  pl.BlockSpec((2, 2 * NA_KH - 2, GRID_W, LANES), lambda b, p: (p, 0, 0, 0))],
        out_specs=(pl.BlockSpec((None, N, LANES), lambda b, p: (b, 0, p)),),
        scratch_shapes=[pltpu.VMEM((T, LANES), BF16)] * 2,
        args=(proj, proj, proj, bias2))


def _na_bwd(proj, bias2, dlat, n_ctx, hosted):
    B, T, _ = proj.shape
    C = n_ctx
    N = T - C
    R = N // GRID_W
    kh, nk = _na_geometry(R)
    scale = NA_DIM ** -0.5
    base = (4 * RET_WIDTH) // LANES

    def body(q_ref, k_ref, v_ref, bias_ref, dl_ref, d_ref, db_ref, kb16, vb16, dkv):
        b = pl.program_id(1)
        kb16[...] = k_ref[...].astype(BF16)
        vb16[...] = v_ref[...].astype(BF16)
        kc = kb16[0:C, :]
        vc = vb16[0:C, :]
        lane = lax.broadcasted_iota(jnp.int32, (GRID_W, LANES), 1)
        dkv[...] = jnp.zeros(dkv.shape, F32)
        d_ref[0, 0:C, :] = jnp.zeros((C, LANES), BF16)

        @pl.when(b == 0)
        def _():
            db_ref[...] = jnp.zeros(db_ref.shape, F32)

        sel2 = _pair_select()

        def group(gi, carry):
            pre = []
            for u in range(NA_GROUP):
                r = gi * NA_GROUP + u
                bs = jnp.clip(r - kh // 2, 0, R - kh)
                dr0 = bs - r + (NA_KH - 1)
                q = q_ref[pl.ds(pl.multiple_of(C + r * GRID_W, GRID_W), GRID_W), :] * scale
                do = dl_ref[pl.ds(pl.multiple_of(r * GRID_W, GRID_W), GRID_W), :]
                q2 = jnp.where(sel2, jnp.concatenate([q, q], axis=0), 0.0).astype(BF16)
                do2 = jnp.where(sel2, jnp.concatenate([do, do], axis=0), 0.0).astype(BF16)
                band = pl.ds(pl.multiple_of(C + bs * GRID_W, GRID_W), nk)
                s_loc = _dot_nt(q2, kb16[band, :]) + _pair_bias(bias_ref, dr0, kh)
                s_ctx = _dot_nt(q2, kc)
                dp_loc = _dot_nt(do2, vb16[band, :])
                dp_ctx = _dot_nt(do2, vc)
                pre.append((r, dr0, band, q2, do2, s_loc, s_ctx, dp_loc, dp_ctx))
            mid = []
            for r, dr0, band, q2, do2, s_loc, s_ctx, dp_loc, dp_ctx in pre:
                p_loc, p_ctx, den = _na_softmax(s_loc, s_ctx)
                inv = 1.0 / den
                p_loc = p_loc * inv
                p_ctx = p_ctx * inv
                delta = (jnp.sum(p_loc * dp_loc, axis=-1, keepdims=True)
                         + jnp.sum(p_ctx * dp_ctx, axis=-1, keepdims=True))
                ds_loc = p_loc * (dp_loc - delta)
                ds_ctx = p_ctx * (dp_ctx - delta)
                mid.append((r, dr0, band, q2, do2, p_loc.astype(BF16), p_ctx.astype(BF16), ds_loc, ds_ctx))
            for r, dr0, band, q2, do2, pb_loc, pb_ctx, ds_loc, ds_ctx in mid:
                dsb_loc = ds_loc.astype(BF16)
                dsb_ctx = ds_ctx.astype(BF16)
                dq2 = _dot(dsb_loc, kb16[band, :]) + _dot(dsb_ctx, kc)
                d_ref[0, pl.ds(pl.multiple_of(C + r * GRID_W, GRID_W), GRID_W), :] = (jnp.where(
                    lane < NA_DIM, dq2[:GRID_W], dq2[GRID_W:]) * scale).astype(BF16)
                dkv[0, band, :] += _dot_tn(dsb_loc, q2)
                dkv[1, band, :] += _dot_tn(pb_loc, do2)
                dkv[0, 0:C, :] += _dot_tn(dsb_ctx, q2)
                dkv[1, 0:C, :] += _dot_tn(pb_ctx, do2)
                for e in range(2):
                    for m in range(kh // 2):
                        db_ref[e, pl.ds(dr0 + 2 * m, 1)] += ds_loc[e * GRID_W:(e + 1) * GRID_W,
                                                                   m * LANES:(m + 1) * LANES].reshape(1, GRID_W, LANES)
            return carry

        lax.fori_loop(0, R // NA_GROUP, group, 0)
        d_ref[1] = dkv[0].astype(BF16)
        d_ref[2] = dkv[1].astype(BF16)

    def col(seg):
        return pl.BlockSpec((None, T, LANES), lambda p, b, seg=seg: (b, 0, base + seg * NA_PAIRS + p))

    return _call_hosting(
        body, hosted, name="na_bwd", grid=(NA_PAIRS, B),
        out_shape=(jax.ShapeDtypeStruct((B, 3, T, NA_WIDTH), BF16),
                   jax.ShapeDtypeStruct((NA_HEADS, 2 * NA_KH - 2, GRID_W, LANES), F32)),
        in_specs=[col(0), col(1), col(2),
                  pl.BlockSpec((2, 2 * NA_KH - 2, GRID_W, LANES), lambda p, b: (p, 0, 0, 0)),
                  pl.BlockSpec((None, N, LANES), lambda p, b: (b, 0, p))],
        out_specs=(pl.BlockSpec((None, 3, T, LANES), lambda p, b: (b, 0, 0, p)),
                   pl.BlockSpec((2, 2 * NA_KH - 2, GRID_W, LANES), lambda p, b: (p, 0, 0, 0))),
        scratch_shapes=[pltpu.VMEM((T, LANES), BF16)] * 2 + [pltpu.VMEM((2, T, LANES), F32)],
        args=(proj, proj, proj, bias2, dlat))


def _split3(a):
    hi = a.astype(BF16)
    r1 = a - hi.astype(F32)
    mid = r1.astype(BF16)
    lo = (r1 - mid.astype(F32)).astype(BF16)
    return hi, mid, lo


def _rpb_reduce(dbias2, onehot2):
    rows = dbias2.shape[0] * dbias2.shape[1]
    flat = dbias2.reshape(rows, GRID_W * LANES)

    def body(a_ref, oh_ref, o_ref):
        hi, mid, lo = _split3(a_ref[...])
        oh = oh_ref[...]
        o_ref[...] = _dot(hi, oh) + _dot(mid, oh) + _dot(lo, oh)

    return pl.pallas_call(
        body, name="rpb_reduce", out_shape=jax.ShapeDtypeStruct((rows, LANES), F32),
        in_specs=[_vmem(), _vmem()], out_specs=_vmem(),
        compiler_params=pltpu.CompilerParams(vmem_limit_bytes=VMEM_LIMIT),
    )(flat, onehot2)


def _dense_core(lat_ret, lat_na, x, tgt, modl, g_post_mix, g_pre_mlp, g_post_mlp, w_out, w1, w2):
    B, N, D = x.shape
    F = w1.shape[1]
    w2_rows = w2.shape[0] // N_DEV
    mixw = w_out.shape[0]
    half = mixw // 2
    tm = _div_tile(N, 256, 16)
    nt = N // tm
    fc = _div_tile(F, 1024, LANES)

    def body(lr_ref, ln_ref, x_ref, t_ref, gt1_ref, sh2_ref, sc2_ref, gt2_ref, gpm_ref, gpre_ref, gpo_ref,
             wout_hbm, w1_hbm, w2_part,
             dy1_ref, dlr_ref, dln_ref, dmix_ref, h2_ref, a_ref, du_ref, dz_ref, red_ref, w2_hbm,
             wout_v, w1_v, w2_v, u_s, sems, fsend, frecv):
        @pl.when((pl.program_id(0) == 0) & (pl.program_id(1) == 0))
        def _():
            relay = [(_row_block(w2_part, w2_rows), _row_block(w2_hbm, w2_rows))]
            _forward_start(relay, fsend, frecv)
            cps = [pltpu.make_async_copy(wout_hbm, wout_v, sems.at[0]),
                   pltpu.make_async_copy(w1_hbm, w1_v, sems.at[1])]
            for cp in cps:
                cp.start()
            _forward_wait(relay, fsend, frecv)
            cps.append(pltpu.make_async_copy(w2_hbm, w2_v, sems.at[2]))
            cps[2].start()
            for cp in cps:
                cp.wait()

        gt1 = gt1_ref[...]
        sh2 = sh2_ref[...]
        sc2 = sc2_ref[...]
        gt2 = gt2_ref[...]
        gpm = gpm_ref[...]
        gpre = gpre_ref[...]
        gpo = gpo_ref[...]

        def rowmean(a):
            return jnp.mean(a, axis=-1, keepdims=True)

        def colsum(a):
            return jnp.sum(a, axis=0, keepdims=True)

        mix = _dot(lr_ref[...], wout_v[0:half, :]) + _dot(ln_ref[...], wout_v[half:, :])
        x = x_ref[...]
        rm = lax.rsqrt(rowmean(mix * mix) + NORM_EPS)
        mh = mix * rm
        nm = mh * gpm
        y1 = x + gt1 * nm
        r1 = lax.rsqrt(rowmean(y1 * y1) + NORM_EPS)
        xh = y1 * r1
        n1 = xh * gpre
        h2b = (n1 * (1.0 + sc2) + sh2).astype(BF16)
        h2_ref[...] = h2b
        z = jnp.zeros((tm, D), F32)
        for c0 in range(0, F, fc):
            u = _dot(h2b, w1_v[:, c0:c0 + fc])
            u_s[:, c0:c0 + fc] = u
            ru = jnp.maximum(u, 0.0)
            ab = (ru * ru).astype(BF16)
            a_ref[:, c0:c0 + fc] = ab
            z = z + _dot(ab, w2_v[c0:c0 + fc, :])
        r2 = lax.rsqrt(rowmean(z * z) + NORM_EPS)
        zh = z * r2
        n2 = zh * gpo
        y2 = y1 + gt2 * n2
        err = y2 - t_ref[...]
        loss = 0.5 * jnp.sum(rowmean(err * err))
        dy2 = err * (1.0 / D)
        red_ref[2:3, :] = colsum(dy2 * n2)
        dn2 = dy2 * gt2
        red_ref[6:7, :] = colsum(dn2 * zh)
        dzh = dn2 * gpo
        dz = r2 * (dzh - zh * rowmean(dzh * zh))
        dzb = dz.astype(BF16)
        dz_ref[...] = dzb
        dh2 = jnp.zeros((tm, D), F32)
        for c0 in range(0, F, fc):
            da = _dot_nt(dzb, w2_v[c0:c0 + fc, :])
            dub = (da * (2.0 * jnp.maximum(u_s[:, c0:c0 + fc], 0.0))).astype(BF16)
            du_ref[:, c0:c0 + fc] = dub
            dh2 = dh2 + _dot_nt(dub, w1_v[:, c0:c0 + fc])
        red_ref[3:4, :] = colsum(dh2 * n1)
        red_ref[4:5, :] = colsum(dh2)
        dn1 = dh2 * (1.0 + sc2)
        red_ref[5:6, :] = colsum(dn1 * xh)
        dxh = dn1 * gpre
        dy1 = dy2 + r1 * (dxh - xh * rowmean(dxh * xh))
        dy1_ref[...] = dy1
        red_ref[0:1, :] = colsum(dy1 * nm)
        dnm = dy1 * gt1
        red_ref[1:2, :] = colsum(dnm * mh)
        dmh = dnm * gpm
        dmix = (rm * (dmh - mh * rowmean(dmh * mh))).astype(BF16)
        dmix_ref[...] = dmix
        dlr_ref[...] = _dot_nt(dmix, wout_v[0:half, :])
        dln_ref[...] = _dot_nt(dmix, wout_v[half:, :])
        red_ref[7:8, :] = jnp.zeros((1, D), F32) + loss

    def tok(w):
        return pl.BlockSpec((None, tm, w), lambda b, t: (b, t, 0))

    def mod(k):
        return pl.BlockSpec((None, None, 1, D), lambda b, t, k=k: (b, k, 0, 0))

    def vec():
        return pl.BlockSpec((1, D), lambda b, t: (0, 0))

    return pl.pallas_call(
        body, name="dense_core", grid=(B, nt),
        out_shape=(jax.ShapeDtypeStruct((B, N, D), F32), jax.ShapeDtypeStruct((B, N, half), F32),
                   jax.ShapeDtypeStruct((B, N, half), F32), jax.ShapeDtypeStruct((B, N, D), BF16),
                   jax.ShapeDtypeStruct((B, N, D), BF16), jax.ShapeDtypeStruct((B, N, F), BF16),
                   jax.ShapeDtypeStruct((B, N, F), BF16), jax.ShapeDtypeStruct((B, N, D), BF16),
                   jax.ShapeDtypeStruct((B, nt, SUBLANES, D), F32),
                   jax.ShapeDtypeStruct(w2.shape, w2.dtype)),
        in_specs=[tok(half), tok(half), tok(D), tok(D), mod(2), mod(3), mod(4), mod(5), vec(), vec(), vec(),
                  _any(), _any(), _any()],
        out_specs=(tok(D), tok(half), tok(half), tok(D), tok(D), tok(F), tok(F), tok(D),
                   pl.BlockSpec((None, None, SUBLANES, D), lambda b, t: (b, t, 0, 0)), _any()),
        scratch_shapes=[pltpu.VMEM((mixw, D), BF16), pltpu.VMEM((D, F), BF16), pltpu.VMEM((F, D), BF16),
                        pltpu.VMEM((tm, F), F32), pltpu.SemaphoreType.DMA((3,)),
                        pltpu.SemaphoreType.DMA((1, 3)), pltpu.SemaphoreType.DMA((1, 3))],
        input_output_aliases={13: 9},
        compiler_params=_params("arbitrary", "arbitrary"),
    )(lat_ret, lat_na, x, tgt, modl, modl, modl, modl, g_post_mix, g_pre_mlp, g_post_mlp, w_out, w1, w2)[:9]


def _inproj_bwd(dret, dna, x, ctx, dy1, modl, g1, w_in_t, hosted):
    B, N, D = x.shape
    n_ctx = ctx.shape[1]
    T = n_ctx + N
    tm = _div_tile(n_ctx, 256, 16)
    nct, ctx_spec, lat_spec = _token_tiles(n_ctx, tm)
    nt = T // tm
    nseg_r = dret.shape[1]
    nseg_n = dna.shape[1]
    nw = w_in_t.shape[0]

    def body(*refs):
        seg_refs = refs[:nseg_r + nseg_n]
        c_ref, x_ref, dy1_ref, sc_ref, g_ref, w_ref, dx_ref, red_ref = refs[nseg_r + nseg_n:]
        t = pl.program_id(1)
        dh = jnp.zeros((tm, D), F32)
        for s, ref in enumerate(seg_refs):
            dh = dh + _dot(ref[...], w_ref[s * SEG:(s + 1) * SEG, :])
        x = jnp.where(t < nct, c_ref[...], x_ref[...])
        g = g_ref[...]
        r = lax.rsqrt(jnp.mean(x * x, axis=-1, keepdims=True) + NORM_EPS)
        xh = x * r
        red_ref[0:1, :] = jnp.sum(dh, axis=0, keepdims=True)
        red_ref[1:2, :] = jnp.sum(dh * (xh * g), axis=0, keepdims=True)
        dn = dh * (1.0 + sc_ref[...])
        red_ref[2:3, :] = jnp.sum(dn * xh, axis=0, keepdims=True)
        red_ref[3:, :] = jnp.zeros((SUBLANES - 3, D), F32)
        dxh = dn * g
        dx = r * (dxh - xh * jnp.mean(dxh * xh, axis=-1, keepdims=True))
        dx_ref[...] = dx + jnp.where(t >= nct, dy1_ref[...], 0.0)

    def mrow(b, t):
        return jnp.where(t < nct, B, b)

    def seg(s):
        return pl.BlockSpec((None, None, tm, SEG), lambda b, t, s=s: (b, s, t, 0))

    return _call_hosting(
        body, hosted, name="inproj_bwd", grid=(B, nt),
        out_shape=(jax.ShapeDtypeStruct((B, N, D), F32), jax.ShapeDtypeStruct((B, nt, SUBLANES, D), F32)),
        in_specs=[seg(s) for s in range(nseg_r)] + [seg(s) for s in range(nseg_n)]
                 + [ctx_spec(D), lat_spec(D), lat_spec(D),
                    pl.BlockSpec((None, None, 1, D), lambda b, t: (mrow(b, t), 1, 0, 0)),
                    pl.BlockSpec((1, D), lambda b, t: (0, 0)),
                    pl.BlockSpec((nw, D), lambda b, t: (0, 0))],
        out_specs=(lat_spec(D), pl.BlockSpec((None, None, SUBLANES, D), lambda b, t: (b, t, 0, 0))),
        scratch_shapes=[], args=(*([dret] * nseg_r), *([dna] * nseg_n), ctx, x, dy1, modl, g1, w_in_t))


def _tn_matmul(lhs, rhs, name):
    B, S, T, W = lhs.shape
    nn = rhs.shape[-1]
    tk = _div_tile(T, 1024, LANES)
    bm = _div_tile(W, 1024, LANES)
    bn = _div_tile(nn, 1024, LANES)
    nkt = T // tk
    nk = B * nkt

    def body(l_ref, r_ref, o_ref, acc):
        k = pl.program_id(3)

        @pl.when(k == 0)
        def _():
            acc[...] = jnp.zeros(acc.shape, F32)

        acc[...] += _dot_tn(l_ref[...].astype(BF16), r_ref[...].astype(BF16))

        @pl.when(k == nk - 1)
        def _():
            o_ref[...] = acc[...].astype(BF16)

    nwb = W // bm
    return pl.pallas_call(
        functools.partial(body), name=name, grid=(S, nwb, nn // bn, nk),
        out_shape=jax.ShapeDtypeStruct((S * W, nn), BF16),
        in_specs=[pl.BlockSpec((None, None, tk, bm), lambda s, i, j, k: (k // nkt, s, k % nkt, i)),
                  pl.BlockSpec((None, tk, bn), lambda s, i, j, k: (k // nkt, k % nkt, j))],
        out_specs=pl.BlockSpec((bm, bn), lambda s, i, j, k: (s * nwb + i, j)),
        scratch_shapes=[pltpu.VMEM((bm, bn), F32)],
        compiler_params=_params("parallel", "parallel", "parallel", "arbitrary"),
    )(lhs, rhs)


def _sum_slots(buf, name):
    _, rows, cols = buf.shape
    tr = _div_tile(rows, 256, 2 * SUBLANES)

    def body(b_ref, o_ref):
        acc = b_ref[0].astype(F32)
        for k in range(1, N_DEV):
            acc = acc + b_ref[k].astype(F32)
        o_ref[...] = acc

    return pl.pallas_call(
        functools.partial(body), name=name, grid=(rows // tr,),
        out_shape=jax.ShapeDtypeStruct((rows, cols), F32),
        in_specs=[pl.BlockSpec((N_DEV, tr, cols), lambda i: (0, i, 0))],
        out_specs=pl.BlockSpec((tr, cols), lambda i: (i, 0)),
        compiler_params=_params("parallel"),
    )(buf)


def _small_ar(vec, dmods, silu_all, w_ada, c_ctx):
    rv = vec.shape[0]
    D = silu_all.shape[1]
    ncol = w_ada.shape[1]
    nm = dmods.shape[1]
    srows = silu_all.shape[0]

    def body(vec_ref, dm_ref, s_ref, w_ref, cc_ref, tot_ref, gb_ref, gw_ref, gc_ref,
             vbuf, mbuf, tbuf, dmx, send1, recv1, send3, recv3):
        me, _ = _me_and_peers()
        vbuf[me] = vec_ref[...]
        mbuf[me] = dm_ref[...]
        both = [(lambda p: vbuf.at[me], lambda p: vbuf.at[p]), (lambda p: mbuf.at[me], lambda p: mbuf.at[p])]
        _push_start(both, ALL_PEERS, send1, recv1)
        _push_wait_recv(both, ALL_PEERS, send1, recv1)
        _push_wait_send(both, ALL_PEERS, send1, recv1)
        tot = vbuf[0]
        msum = mbuf[0]
        for k in range(1, N_DEV):
            tot = tot + vbuf[k]
            msum = msum + mbuf[k]
        tot_ref[...] = tot
        gb_ref[...] = jnp.sum(msum, axis=0, keepdims=True)
        loc = pl.ds(pl.multiple_of(me * ncol, ncol), ncol)
        for k in range(N_DEV):
            dmx[k * SUBLANES:(k + 1) * SUBLANES, :] = mbuf[k, :, loc]
        cm = msum[2:3, :]
        mbuf[0, 2:3, :] = cm
        cm_loc = mbuf[0, 2:3, loc]
        dmx[N_DEV * SUBLANES:, :] = jnp.concatenate([cm_loc, jnp.zeros((SUBLANES - 1, ncol), F32)], axis=0)
        gw_ref[...] = _dot_tn(s_ref[...], dmx[...])
        tbuf[me] = _dot_nt(dmx[N_DEV * SUBLANES:, :], w_ref[...])
        _exchange(lambda p: tbuf.at[me], lambda p: tbuf.at[p], send3, recv3)
        tsum = tbuf[0]
        for k in range(1, N_DEV):
            tsum = tsum + tbuf[k]
        cc = cc_ref[...]
        sg = _sigmoid(cc)
        gc_ref[...] = tsum[0:1, :] * (sg * (1.0 + cc * (1.0 - sg)))

    return pl.pallas_call(
        body, name="small_ar",
        out_shape=(jax.ShapeDtypeStruct((rv, LANES), F32), jax.ShapeDtypeStruct((1, nm), F32),
                   jax.ShapeDtypeStruct((D, ncol), F32), jax.ShapeDtypeStruct((1, D), F32)),
        in_specs=[_vmem()] * 5, out_specs=(_vmem(),) * 4,
        scratch_shapes=[pltpu.VMEM((N_DEV, rv, LANES), F32), pltpu.VMEM((N_DEV, SUBLANES, nm), F32),
                        pltpu.VMEM((N_DEV, SUBLANES, D), F32), pltpu.VMEM((srows, ncol), F32)]
                       + [pltpu.SemaphoreType.DMA((2, N_DEV - 1))] * 2 + [pltpu.SemaphoreType.DMA((N_DEV - 1,))] * 2,
        compiler_params=pltpu.CompilerParams(vmem_limit_bytes=VMEM_LIMIT),
    )(vec, dmods, silu_all, w_ada, c_ctx.reshape(1, D))


def _adam_update(w, g, m, v):
    mn = ADAM_B1 * m + (1.0 - ADAM_B1) * g
    vn = ADAM_B2 * v + (1.0 - ADAM_B2) * (g * g)
    m_hat = mn / (1.0 - ADAM_B1 ** ADAM_STEP)
    v_hat = vn / (1.0 - ADAM_B2 ** ADAM_STEP)
    return -ADAM_LR * (m_hat / (jnp.sqrt(v_hat) + ADAM_EPS) + ADAM_WD * w), mn, vn


def _adamw(w, g, m, v, name):
    rows, cols = w.shape
    tr = _div_tile(rows, 256, SUBLANES) if rows * cols > 65536 else rows

    def body(w_ref, g_ref, m_ref, v_ref, d_ref, nm_ref, nv_ref):
        d_ref[...], nm_ref[...], nv_ref[...] = _adam_update(w_ref[...], g_ref[...], m_ref[...], v_ref[...])

    spec = pl.BlockSpec((tr, cols), lambda i: (i, 0))
    return pl.pallas_call(
        functools.partial(body), name=name, grid=(rows // tr,),
        out_shape=(jax.ShapeDtypeStruct((rows, cols), F32),) * 3,
        in_specs=[spec] * 4, out_specs=(spec,) * 3,
        compiler_params=_params("parallel"),
    )(w, g, m, v)


def _sum_adamw(buf, w, m, v, name):
    _, rows, cols = buf.shape
    tr = _div_tile(rows, 256, 2 * SUBLANES)

    def body(b_ref, w_ref, m_ref, v_ref, g_ref, d_ref, nm_ref, nv_ref):
        g = b_ref[0].astype(F32)
        for k in range(1, N_DEV):
            g = g + b_ref[k].astype(F32)
        g_ref[...] = g
        d_ref[...], nm_ref[...], nv_ref[...] = _adam_update(w_ref[...], g, m_ref[...], v_ref[...])

    spec = pl.BlockSpec((tr, cols), lambda i: (i, 0))
    return pl.pallas_call(
        functools.partial(body), name=name, grid=(rows // tr,),
        out_shape=(jax.ShapeDtypeStruct((rows, cols), F32),) * 4,
        in_specs=[pl.BlockSpec((N_DEV, tr, cols), lambda i: (0, i, 0))] + [spec] * 3, out_specs=(spec,) * 4,
        compiler_params=_params("parallel"),
    )(buf, w, m, v)


def _rope_tables(n_ctx, n):
    n_freq = RET_DIM // 4
    inv = np.float32(ROPE_BASE) ** (-np.arange(n_freq, dtype=np.float32) / np.float32(n_freq))
    tok = np.arange(n)
    pos_r = (tok // GRID_W).astype(np.float32)
    pos_c = (tok % GRID_W).astype(np.float32)
    ang_r = (pos_r[:, None] * inv[None, :]).astype(np.float32)
    ang_c = (pos_c[:, None] * inv[None, :]).astype(np.float32)
    cos = np.concatenate([np.cos(ang_r), np.cos(ang_r), np.cos(ang_c), np.cos(ang_c)], axis=-1)
    sin = np.concatenate([-np.sin(ang_r), np.sin(ang_r), -np.sin(ang_c), np.sin(ang_c)], axis=-1)
    cos = np.concatenate([np.ones((n_ctx, RET_DIM), np.float32), cos], axis=0)
    sin = np.concatenate([np.zeros((n_ctx, RET_DIM), np.float32), sin], axis=0)
    return jnp.asarray(cos, F32), jnp.asarray(sin, F32)


def _na_tables():
    q = np.arange(GRID_W)[:, None]
    k = np.arange(GRID_W)[None, :]
    start = np.clip(q - NA_KW // 2, 0, GRID_W - NA_KW)
    valid = (k >= start) & (k < start + NA_KW)
    dc = np.clip(k - q + (NA_KW - 1), 0, 2 * NA_KW - 2)
    ncls = 2 * NA_KW - 1
    onehot = (dc[None] == np.arange(ncls)[:, None, None]) & valid[None]
    oh2 = np.zeros((GRID_W, LANES, LANES), np.float32)
    for c in range(ncls):
        oh2[:, :GRID_W, c] = onehot[c]
        oh2[:, GRID_W:, 32 + c] = onehot[c]
    return onehot.astype(np.float32), valid, oh2.reshape(GRID_W * LANES, LANES)


def _paired_bias(rpb, onehot, valid):
    t = jnp.einsum("hdc,cqk->hdqk", rpb, jnp.asarray(onehot), precision=lax.Precision.HIGHEST)
    t = jnp.where(jnp.asarray(valid)[None, None], t, NEG_INF)
    return jnp.concatenate([t[:, :-1], t[:, 1:]], axis=-1)


def kernel(x, c, ctx, c_ctx, w_ada, b_ada, g_pre_mix, g_post_mix, g_pre_mlp, g_post_mlp, w_in, ret_decay, ret_gn, na_rpb, w_out, w_mlp1, w_mlp2, loss_target, m_c_ctx, m_w_ada, m_b_ada, m_g_pre_mix, m_g_post_mix, m_g_pre_mlp, m_g_post_mlp, m_w_in, m_ret_decay, m_ret_gn, m_na_rpb, m_w_out, m_w_mlp1, m_w_mlp2, v_c_ctx, v_w_ada, v_b_ada, v_g_pre_mix, v_g_post_mix, v_g_pre_mlp, v_g_post_mlp, v_w_in, v_ret_decay, v_ret_gn, v_na_rpb, v_w_out, v_w_mlp1, v_w_mlp2):
    B, N, D = x.shape
    C = ctx.shape[1]
    T = C + N

    silu_all, mods_g, win_b, wout_l, w1_l, w2_l = _mod_gather(c, c_ctx, w_ada[0], b_ada, w_in[0].T, w_out[0],
                                                             w_mlp1[0], w_mlp2[0])
    mods_mine = mods_g.transpose(1, 0, 2).reshape(mods_g.shape[1], N_MOD * D)
    modl = jnp.concatenate([mods_mine[:B], mods_mine[SUBLANES:SUBLANES + 1]], axis=0)
    modl = modl.reshape(B + 1, N_MOD, 1, D)
    rin = w_in.shape[2]
    rout, c1, r2 = wout_l.shape[0], w1_l.shape[1], w2_l.shape[0]

    def rows_of(n):
        return lambda ref: _row_block(ref, n)

    def cols_of(n):
        return lambda ref: _col_block(ref, n)

    cos, sin = _rope_tables(C, N)
    onehot, valid, oh2 = _na_tables()
    bias2 = _paired_bias(na_rpb[0], onehot, valid)
    lg = jax.nn.log_sigmoid(ret_decay[0].astype(F32))

    level_one = SIBLING + ICI_SAME_CORE
    h_all, proj, w1_part = _inproj_fwd(
        x, ctx, modl, g_pre_mix, win_b,
        [_Hosted("gather", level_one, [w1_l], [cols_of(c1)], [jax.ShapeDtypeStruct((D, N_DEV * c1), BF16)], True)])
    o_ret, lat_ret, wout_b = _ret_fwd(
        proj, cos, sin, lg, ret_gn, C,
        [_Hosted("gather", ALL_PEERS, [wout_l], [rows_of(rout)], [jax.ShapeDtypeStruct((N_DEV * rout, D), BF16)], True)])
    lat_na, w1_b, w2_part = _na_fwd(
        proj, bias2, C,
        [_HostedRelay([w1_part], [cols_of(c1)]),
         _Hosted("gather", level_one, [w2_l], [rows_of(r2)], [jax.ShapeDtypeStruct((N_DEV * r2, D), BF16)], True)])

    (dy1, dlat_ret, dlat_na, dmix, h2, act, du, dz, red_d) = _dense_core(
        lat_ret, lat_na, x, loss_target, modl, g_post_mix, g_pre_mlp, g_post_mlp, wout_b, w1_b, w2_part)

    gw_out_p = jnp.concatenate([_tn_matmul(lat_ret[:, None], dmix, "gw_out_ret"),
                                _tn_matmul(lat_na[:, None], dmix, "gw_out_na")], axis=0)
    gw1_p = _tn_matmul(h2[:, None], du, "gw_mlp1")
    gw2_p = _tn_matmul(act[:, None], dz, "gw_mlp2")

    dret, dgn_p, dlg_p, b1 = _ret_bwd(
        proj, cos, sin, lg, ret_gn, o_ret, dlat_ret, C,
        [_Hosted("scatter", ALL_PEERS, [gw1_p], [cols_of(c1)], [jax.ShapeDtypeStruct((N_DEV, D, c1), BF16)], True)])
    dna, dbias2, b2, bout = _na_bwd(
        proj, bias2, dlat_na, C,
        [_Hosted("scatter", ALL_PEERS, [gw2_p, gw_out_p], [rows_of(r2), rows_of(rout)],
                 [jax.ShapeDtypeStruct((N_DEV, r2, D), BF16), jax.ShapeDtypeStruct((N_DEV, rout, D), BF16)], True)])
    gwin_t_p = jnp.concatenate([_tn_matmul(dret, h_all, "gw_in_ret"), _tn_matmul(dna, h_all, "gw_in_na")], axis=0)
    grad_x, red_i, bin_ = _inproj_bwd(
        dret, dna, x, ctx, dy1, modl, g_pre_mix, win_b,
        [_Hosted("scatter", ALL_PEERS, [gwin_t_p], [rows_of(rin)], [jax.ShapeDtypeStruct((N_DEV, rin, D), BF16)], True)])

    g_w_in = _sum_slots(bin_, "sum_w_in").T
    fused = {"w_out": _sum_adamw(bout, w_out[0], m_w_out[0], v_w_out[0], "sum_adamw_w_out"),
             "w_mlp1": _sum_adamw(b1, w_mlp1[0], m_w_mlp1[0], v_w_mlp1[0], "sum_adamw_w_mlp1"),
             "w_mlp2": _sum_adamw(b2, w_mlp2[0], m_w_mlp2[0], v_w_mlp2[0], "sum_adamw_w_mlp2")}

    rd = red_d.sum(axis=1)[:, :, :]
    ri = red_i
    nct = ri.shape[1] * C // T
    ri_ctx = ri[:, :nct].sum(axis=(0, 1))
    ri_lat = ri[:, nct:].sum(axis=1)
    d_mods = jnp.concatenate([ri_lat[:, 0], ri_lat[:, 1], rd[:, 0], rd[:, 4], rd[:, 3], rd[:, 2]], axis=-1)
    d_cmods = jnp.concatenate([ri_ctx[0], ri_ctx[1], jnp.zeros(((N_MOD - 2) * D,), F32)])[None]
    dm_slot = jnp.concatenate([d_mods, d_cmods, jnp.zeros((SUBLANES - B - 1, N_MOD * D), F32)], axis=0)
    dg_pre_mix = ri_lat[:, 2].sum(axis=0) + ri_ctx[2]
    dg_post_mix = rd[:, 1].sum(axis=0)
    dg_pre_mlp = rd[:, 5].sum(axis=0)
    dg_post_mlp = rd[:, 6].sum(axis=0)
    loss_p = rd[:, 7, 0].sum()
    d_gn = dgn_p[:, 0].sum(axis=0)
    d_lg = dlg_p[:, :, :2, 0].sum(axis=0).T
    d_decay = d_lg * jax.nn.sigmoid(-ret_decay[0].astype(F32))
    rr = _rpb_reduce(dbias2, jnp.asarray(oh2, BF16)).reshape(NA_HEADS, 2 * NA_KH - 2, LANES)
    ncls = 2 * NA_KW - 1
    d_rpb = (jnp.pad(rr[:, :, :ncls], ((0, 0), (0, 1), (0, 0))) + jnp.pad(rr[:, :, 32:32 + ncls], ((0, 0), (1, 0), (0, 0))))
    d_rpb32 = jnp.pad(d_rpb, ((0, 0), (0, 0), (0, 32 - ncls)))
    pieces = [dg_pre_mix, dg_post_mix, dg_pre_mlp, dg_post_mlp, d_gn, d_rpb32.reshape(-1),
              jnp.pad(d_decay.reshape(-1), (0, LANES - d_decay.size)), jnp.full((LANES,), loss_p, F32)]
    vec = jnp.concatenate(pieces)
    pad = (-vec.shape[0]) % (SUBLANES * LANES)
    vec = jnp.pad(vec, (0, pad)).reshape(-1, LANES)
    tot, g_b_ada, g_w_ada, g_c_ctx = _small_ar(vec, dm_slot, silu_all, w_ada[0], c_ctx)
    flat = tot.reshape(-1)
    o0 = 0
    g_pre_mix_g = flat[o0:o0 + D]; o0 += D
    g_post_mix_g = flat[o0:o0 + D]; o0 += D
    g_pre_mlp_g = flat[o0:o0 + D]; o0 += D
    g_post_mlp_g = flat[o0:o0 + D]; o0 += D
    g_gn = flat[o0:o0 + RET_WIDTH]; o0 += RET_WIDTH
    nrpb = NA_HEADS * (2 * NA_KH - 1) * 32
    g_rpb = flat[o0:o0 + nrpb].reshape(NA_HEADS, 2 * NA_KH - 1, 32)[:, :, :ncls]; o0 += nrpb
    g_decay = flat[o0:o0 + 2 * RET_HEADS].reshape(2, RET_HEADS); o0 += LANES
    loss = flat[o0]

    grads = {
        "c_ctx": g_c_ctx.reshape(c_ctx.shape), "w_ada": g_w_ada[None], "b_ada": g_b_ada.reshape(b_ada.shape),
        "g_pre_mix": g_pre_mix_g[None], "g_post_mix": g_post_mix_g[None], "g_pre_mlp": g_pre_mlp_g[None],
        "g_post_mlp": g_post_mlp_g[None], "w_in": g_w_in[None], "ret_decay": g_decay[None], "ret_gn": g_gn[None],
        "na_rpb": g_rpb[None], "w_out": fused["w_out"][0][None], "w_mlp1": fused["w_mlp1"][0][None],
        "w_mlp2": fused["w_mlp2"][0][None],
    }
    weights = dict(c_ctx=c_ctx, w_ada=w_ada, b_ada=b_ada, g_pre_mix=g_pre_mix, g_post_mix=g_post_mix,
                   g_pre_mlp=g_pre_mlp, g_post_mlp=g_post_mlp, w_in=w_in, ret_decay=ret_decay, ret_gn=ret_gn,
                   na_rpb=na_rpb, w_out=w_out, w_mlp1=w_mlp1, w_mlp2=w_mlp2)
    m_in = dict(c_ctx=m_c_ctx, w_ada=m_w_ada, b_ada=m_b_ada, g_pre_mix=m_g_pre_mix, g_post_mix=m_g_post_mix,
                g_pre_mlp=m_g_pre_mlp, g_post_mlp=m_g_post_mlp, w_in=m_w_in, ret_decay=m_ret_decay,
                ret_gn=m_ret_gn, na_rpb=m_na_rpb, w_out=m_w_out, w_mlp1=m_w_mlp1, w_mlp2=m_w_mlp2)
    v_in = dict(c_ctx=v_c_ctx, w_ada=v_w_ada, b_ada=v_b_ada, g_pre_mix=v_g_pre_mix, g_post_mix=v_g_post_mix,
                g_pre_mlp=v_g_pre_mlp, g_post_mlp=v_g_post_mlp, w_in=v_w_in, ret_decay=v_ret_decay,
                ret_gn=v_ret_gn, na_rpb=v_na_rpb, w_out=v_w_out, w_mlp1=v_w_mlp1, w_mlp2=v_w_mlp2)
    names = list(weights)
    deltas, new_m, new_v = {}, {}, {}
    for n in names:
        shp = weights[n].shape
        if n in fused:
            deltas[n], new_m[n], new_v[n] = (a.reshape(shp) for a in fused[n][1:])
            continue
        two_d = (-1, shp[-1]) if len(shp) > 1 else (1, shp[0])
        d, nm, nv = _adamw(weights[n].reshape(two_d), grads[n].reshape(two_d), m_in[n].reshape(two_d),
                           v_in[n].reshape(two_d), "adamw_" + n)
        deltas[n], new_m[n], new_v[n] = d.reshape(shp), nm.reshape(shp), nv.reshape(shp)
    return (loss, grad_x, *[grads[n] for n in names], *[deltas[n] for n in names],
            *[new_m[n] for n in names], *[new_v[n] for n in names])
```

```python
import functools
import math

import numpy as np
import jax
import jax.numpy as jnp
from jax import lax
from jax.experimental import pallas as pl
from jax.experimental.pallas import tpu as pltpu

F32 = jnp.float32
BF16 = jnp.bfloat16
MESH = pl.DeviceIdType.MESH

N_DEV = 8
LANES = 128
SUBLANES = 8
VMEM_LIMIT = 60 * 1024 * 1024

GRID_W = 64
RET_HEADS = 4
RET_DIM = 128
RET_WIDTH = RET_HEADS * RET_DIM
NA_HEADS = 8
NA_DIM = 64
NA_WIDTH = NA_HEADS * NA_DIM
NA_PAIRS = NA_HEADS // 2
NA_KH = 8
NA_KW = 16
NA_GROUP = 8
SEG = 512
ROPE_BASE = 10000.0
NORM_EPS = 1e-6
NEG_INF = -1e30
N_MOD = 6

ADAM_LR = 0.001
ADAM_B1 = 0.9
ADAM_B2 = 0.999
ADAM_EPS = 1e-08
ADAM_WD = 0.01
ADAM_STEP = 10


def _dot(a, b):
    return lax.dot_general(a, b, (((1,), (0,)), ((), ())), preferred_element_type=F32)


def _dot_nt(a, b):
    return lax.dot_general(a, b, (((1,), (1,)), ((), ())), preferred_element_type=F32)


def _dot_tn(a, b):
    return lax.dot_general(a, b, (((0,), (0,)), ((), ())), preferred_element_type=F32)


def _sigmoid(x):
    return 1.0 / (1.0 + jnp.exp(-x))


def _div_tile(n, cap, mult):
    if n <= cap:
        return n
    for t in range(cap - cap % mult, 0, -mult):
        if n % t == 0:
            return t
    raise ValueError(f"no tile for {n}")


def _params(*sem):
    return pltpu.CompilerParams(dimension_semantics=tuple(sem) if sem else None,
                                vmem_limit_bytes=VMEM_LIMIT)


def _vmem():
    return pl.BlockSpec(memory_space=pltpu.VMEM)


def _any():
    return pl.BlockSpec(memory_space=pl.ANY)


def _me_and_peers():
    x, y, c = lax.axis_index("x"), lax.axis_index("y"), lax.axis_index("c")
    me = 4 * x + 2 * y + c
    peers = []
    for m in range(1, N_DEV):
        px = 1 - x if (m >> 2) & 1 else x
        py = 1 - y if (m >> 1) & 1 else y
        pc = 1 - c if m & 1 else c
        peers.append(((px, py, pc), 4 * px + 2 * py + pc))
    return me, peers


def _exchange(src_for, dst_from, send_sems, recv_sems):
    me, peers = _me_and_peers()
    sent = []
    for i, (dev, pid) in enumerate(peers):
        cp = pltpu.make_async_remote_copy(src_ref=src_for(pid), dst_ref=dst_from(me),
                                          send_sem=send_sems.at[i], recv_sem=recv_sems.at[i],
                                          device_id=dev, device_id_type=MESH)
        cp.start()
        sent.append(cp)
    for i, (dev, pid) in enumerate(peers):
        pltpu.make_async_remote_copy(src_ref=src_for(pid), dst_ref=dst_from(pid),
                                     send_sem=send_sems.at[i], recv_sem=recv_sems.at[i],
                                     device_id=dev, device_id_type=MESH).wait_recv()
    for cp in sent:
        cp.wait_send()


SIBLING = (1,)
ICI_SAME_CORE = (2, 4, 6)
ALL_PEERS = tuple(range(1, N_DEV))


def _remote(src, dst, send_sem, recv_sem, dev):
    return pltpu.make_async_remote_copy(src_ref=src, dst_ref=dst, send_sem=send_sem, recv_sem=recv_sem,
                                        device_id=dev, device_id_type=MESH)


def _push_start(items, masks, send_sems, recv_sems):
    me, peers = _me_and_peers()
    for k, (src_for, dst_from) in enumerate(items):
        for m in masks:
            dev, pid = peers[m - 1]
            _remote(src_for(pid), dst_from(me), send_sems.at[k, m - 1], recv_sems.at[k, m - 1], dev).start()


def _push_wait_recv(items, masks, send_sems, recv_sems):
    me, peers = _me_and_peers()
    for k, (src_for, dst_from) in enumerate(items):
        for m in masks:
            dev, pid = peers[m - 1]
            _remote(src_for(pid), dst_from(pid), send_sems.at[k, m - 1], recv_sems.at[k, m - 1], dev).wait_recv()


def _push_wait_send(items, masks, send_sems, recv_sems):
    me, peers = _me_and_peers()
    for k, (src_for, dst_from) in enumerate(items):
        for m in masks:
            dev, pid = peers[m - 1]
            _remote(src_for(pid), dst_from(me), send_sems.at[k, m - 1], recv_sems.at[k, m - 1], dev).wait_send()


def _forward_start(items, send_sems, recv_sems):
    me, peers = _me_and_peers()
    sib = peers[0][0]
    for k, (blk_in, blk_out) in enumerate(items):
        for j, m in enumerate(ICI_SAME_CORE):
            pid = peers[m - 1][1]
            _remote(blk_in(pid), blk_out(pid), send_sems.at[k, j], recv_sems.at[k, j], sib).start()


def _forward_wait(items, send_sems, recv_sems):
    me, peers = _me_and_peers()
    sib = peers[0][0]
    for k, (blk_in, blk_out) in enumerate(items):
        for j, m in enumerate(ICI_SAME_CORE):
            got = peers[(m | 1) - 1][1]
            _remote(blk_in(got), blk_out(got), send_sems.at[k, j], recv_sems.at[k, j], sib).wait_recv()
    for k, (blk_in, blk_out) in enumerate(items):
        for j, m in enumerate(ICI_SAME_CORE):
            pid = peers[m - 1][1]
            _remote(blk_in(pid), blk_out(pid), send_sems.at[k, j], recv_sems.at[k, j], sib).wait_send()


def _mod_gather(c, c_ctx, w_ada, b_ada, w_in_t, w_out, w1, w2):
    B, D = c.shape
    ncol = w_ada.shape[1]
    rows = SUBLANES * N_DEV + SUBLANES

    def body(c_ref, cc_ref, w_ref, b_ref, win_ref, wout_ref, w1_ref, w2_ref,
             s_ref, m_ref, gin_ref, wout_b, w1_b, w2_b,
             win_b, msend, send1, recv1, send2, recv2, wsend, wrecv, fsend, frecv, lsem):
        me, _ = _me_and_peers()
        win_b[...] = win_ref[...].astype(BF16)
        block = _row_block(gin_ref, w_in_t.shape[0])
        gather = [(lambda p: win_b, block)]
        own = pltpu.make_async_copy(win_b, block(me), lsem.at[0])
        own.start()
        _push_start(gather, SIBLING + ICI_SAME_CORE, wsend, wrecv)
        wout_b[...] = wout_ref[...].astype(BF16)
        w1_b[...] = w1_ref[...].astype(BF16)
        w2_b[...] = w2_ref[...].astype(BF16)
        cv = c_ref[...]
        slot = jnp.concatenate([cv * _sigmoid(cv), jnp.zeros((SUBLANES - B, D), F32)], axis=0)
        my_rows = pl.ds(pl.multiple_of(me * SUBLANES, SUBLANES), SUBLANES)
        s_ref[my_rows, :] = slot
        ccv = cc_ref[...]
        s_ref[SUBLANES * N_DEV:, :] = jnp.concatenate(
            [ccv * _sigmoid(ccv), jnp.zeros((SUBLANES - 1, D), F32)], axis=0)

        def rows_of(p):
            return s_ref.at[pl.ds(pl.multiple_of(p * SUBLANES, SUBLANES), SUBLANES), :]

        _exchange(lambda p: rows_of(me), rows_of, send1, recv1)
        b_loc = b_ref[:, pl.ds(pl.multiple_of(me * ncol, ncol), ncol)]
        mods = _dot(s_ref[...], w_ref[...]) + b_loc
        for p in range(N_DEV):
            msend[p] = jnp.concatenate([mods[p * SUBLANES:(p + 1) * SUBLANES], mods[N_DEV * SUBLANES:]], axis=0)
        m_ref[me] = msend[me]
        _exchange(lambda p: msend.at[p], lambda p: m_ref.at[p], send2, recv2)
        _push_wait_recv(gather, ICI_SAME_CORE, wsend, wrecv)
        relay = [(block, block)]
        _forward_start(relay, fsend, frecv)
        _push_wait_recv(gather, SIBLING, wsend, wrecv)
        _forward_wait(relay, fsend, frecv)
        _push_wait_send(gather, SIBLING + ICI_SAME_CORE, wsend, wrecv)
        own.wait()

    return pl.pallas_call(
        body, name="mod_gather",
        out_shape=(jax.ShapeDtypeStruct((rows, D), F32), jax.ShapeDtypeStruct((N_DEV, 2 * SUBLANES, ncol), F32),
                   jax.ShapeDtypeStruct((N_DEV * w_in_t.shape[0], D), BF16),
                   jax.ShapeDtypeStruct(w_out.shape, BF16), jax.ShapeDtypeStruct(w1.shape, BF16),
                   jax.ShapeDtypeStruct(w2.shape, BF16)),
        in_specs=[_vmem()] * 8, out_specs=(_vmem(), _vmem(), _any(), _vmem(), _vmem(), _vmem()),
        scratch_shapes=[pltpu.VMEM(w_in_t.shape, BF16), pltpu.VMEM((N_DEV, 2 * SUBLANES, ncol), F32)]
                       + [pltpu.SemaphoreType.DMA((N_DEV - 1,))] * 4
                       + [pltpu.SemaphoreType.DMA((1, N_DEV - 1))] * 2 + [pltpu.SemaphoreType.DMA((1, 3))] * 2
                       + [pltpu.SemaphoreType.DMA((1,))],
        compiler_params=pltpu.CompilerParams(vmem_limit_bytes=VMEM_LIMIT),
    )(c, c_ctx.reshape(1, D), w_ada, b_ada, w_in_t, w_out, w1, w2)


def _row_block(ref, rows):
    return lambda p: ref.at[pl.ds(pl.multiple_of(p * rows, 2 * SUBLANES), rows), :]


def _col_block(ref, cols):
    return lambda p: ref.at[:, pl.ds(pl.multiple_of(p * cols, LANES), cols)]


def _slot(ref):
    return lambda p: ref.at[p]


class _Hosted:
    def __init__(self, kind, masks, operands, block_of, out_shapes, with_own):
        self.kind, self.masks, self.operands = kind, masks, list(operands)
        self.block_of, self.out_shapes, self.with_own = block_of, list(out_shapes), with_own
        self.n = len(self.operands)

    def scratch(self):
        return [pltpu.SemaphoreType.DMA((self.n, N_DEV - 1)), pltpu.SemaphoreType.DMA((self.n, N_DEV - 1)),
                pltpu.SemaphoreType.DMA((self.n,))]

    def _items(self, in_refs, out_refs):
        items = []
        for k in range(self.n):
            if self.kind == "gather":
                items.append((lambda p, k=k: in_refs[k], self.block_of[k](out_refs[k])))
            else:
                items.append((self.block_of[k](in_refs[k]), _slot(out_refs[k])))
        return items

    def _own(self, in_refs, out_refs, lsem):
        me, _ = _me_and_peers()
        items = self._items(in_refs, out_refs)
        return [pltpu.make_async_copy(src_for(me), dst_from(me), lsem.at[k])
                for k, (src_for, dst_from) in enumerate(items)]

    def start(self, in_refs, out_refs, sems):
        send, recv, lsem = sems
        if self.with_own:
            for cp in self._own(in_refs, out_refs, lsem):
                cp.start()
        _push_start(self._items(in_refs, out_refs), self.masks, send, recv)

    def wait(self, in_refs, out_refs, sems):
        send, recv, lsem = sems
        items = self._items(in_refs, out_refs)
        _push_wait_recv(items, self.masks, send, recv)
        _push_wait_send(items, self.masks, send, recv)
        if self.with_own:
            for cp in self._own(in_refs, out_refs, lsem):
                cp.wait()


class _HostedRelay:
    def __init__(self, arrays, block_of):
        self.operands, self.block_of = list(arrays), block_of
        self.out_shapes = [jax.ShapeDtypeStruct(a.shape, a.dtype) for a in arrays]
        self.n = len(self.operands)

    def scratch(self):
        return [pltpu.SemaphoreType.DMA((self.n, 3)), pltpu.SemaphoreType.DMA((self.n, 3))]

    def _items(self, in_refs, out_refs):
        return [(self.block_of[k](in_refs[k]), self.block_of[k](out_refs[k])) for k in range(self.n)]

    def start(self, in_refs, out_refs, sems):
        _forward_start(self._items(in_refs, out_refs), *sems)

    def wait(self, in_refs, out_refs, sems):
        _forward_wait(self._items(in_refs, out_refs), *sems)


def _call_hosting(body, hosted, *, name, grid, out_shape, in_specs, out_specs, scratch_shapes, args):
    n_in, n_out, n_scr = len(in_specs), len(out_shape), len(scratch_shapes)
    hn = sum(hs.n for hs in hosted)
    n_sem = [len(hs.scratch()) for hs in hosted]

    def wrapped(*refs):
        ins = refs[:n_in]
        h_in = refs[n_in:n_in + hn]
        outs = refs[n_in + hn:n_in + hn + n_out]
        h_out = refs[n_in + hn + n_out:n_in + 2 * hn + n_out]
        scr = refs[n_in + 2 * hn + n_out:n_in + 2 * hn + n_out + n_scr]
        sems = refs[n_in + 2 * hn + n_out + n_scr:]
        ids = [pl.program_id(i) for i in range(len(grid))]
        first = functools.reduce(jnp.logical_and, [i == 0 for i in ids])
        last = functools.reduce(jnp.logical_and, [i == g - 1 for i, g in zip(ids, grid)])
        parts, o0, s0 = [], 0, 0
        for hs, ns in zip(hosted, n_sem):
            parts.append((hs, h_in[o0:o0 + hs.n], h_out[o0:o0 + hs.n], sems[s0:s0 + ns]))
            o0 += hs.n
            s0 += ns

        @pl.when(first)
        def _():
            for hs, hi, ho, se in parts:
                hs.start(hi, ho, se)

        body(*ins, *outs, *scr)

        @pl.when(last)
        def _():
            for hs, hi, ho, se in parts:
                hs.wait(hi, ho, se)

    aliases, o0 = {}, 0
    for hs in hosted:
        if isinstance(hs, _HostedRelay):
            aliases.update({n_in + o0 + k: n_out + o0 + k for k in range(hs.n)})
        o0 += hs.n
    return pl.pallas_call(
        wrapped, name=name, grid=grid,
        out_shape=tuple(out_shape) + tuple(s for hs in hosted for s in hs.out_shapes),
        in_specs=list(in_specs) + [_any()] * hn,
        out_specs=tuple(out_specs) + (_any(),) * hn,
        scratch_shapes=list(scratch_shapes) + [s for hs in hosted for s in hs.scratch()],
        input_output_aliases=aliases,
        compiler_params=_params(*(("arbitrary",) * len(grid))),
    )(*args, *[a for hs in hosted for a in hs.operands])


def _token_tiles(n_ctx, tm):
    nct = n_ctx // tm

    def ctx_spec(D):
        return pl.BlockSpec((None, tm, D), lambda b, t: (b, jnp.minimum(t, nct - 1), 0))

    def lat_spec(D):
        return pl.BlockSpec((None, tm, D), lambda b, t: (b, jnp.maximum(t - nct, 0), 0))

    return nct, ctx_spec, lat_spec


def _inproj_fwd(x, ctx, modl, g1, w_in_t, hosted):
    B, N, D = x.shape
    n_ctx = ctx.shape[1]
    T = n_ctx + N
    nw = w_in_t.shape[0]
    tm = _div_tile(n_ctx, 256, 16)
    nct, ctx_spec, lat_spec = _token_tiles(n_ctx, tm)

    def body(c_ref, x_ref, sh_ref, sc_ref, g_ref, w_ref, h_ref, p_ref):
        x = jnp.where(pl.program_id(1) < nct, c_ref[...], x_ref[...])
        r = lax.rsqrt(jnp.mean(x * x, axis=-1, keepdims=True) + NORM_EPS)
        h = ((x * r) * g_ref[...]) * (1.0 + sc_ref[...]) + sh_ref[...]
        hb = h.astype(BF16)
        h_ref[...] = hb
        p_ref[...] = _dot_nt(hb, w_ref[...])

    def mrow(b, t):
        return jnp.where(t < nct, B, b)

    return _call_hosting(
        body, hosted, name="inproj_fwd", grid=(B, T // tm),
        out_shape=(jax.ShapeDtypeStruct((B, T, D), BF16), jax.ShapeDtypeStruct((B, T, nw), F32)),
        in_specs=[ctx_spec(D), lat_spec(D),
                  pl.BlockSpec((None, None, 1, D), lambda b, t: (mrow(b, t), 0, 0, 0)),
                  pl.BlockSpec((None, None, 1, D), lambda b, t: (mrow(b, t), 1, 0, 0)),
                  pl.BlockSpec((1, D), lambda b, t: (0, 0)),
                  pl.BlockSpec((nw, D), lambda b, t: (0, 0))],
        out_specs=(pl.BlockSpec((None, tm, D), lambda b, t: (b, t, 0)),
                   pl.BlockSpec((None, tm, nw), lambda b, t: (b, t, 0))),
        scratch_shapes=[], args=(ctx, x, modl, modl, g1, w_in_t))


def _swap32(x):
    lane = lax.broadcasted_iota(jnp.int32, x.shape, 1)
    return jnp.where((lane % 64) < 32, pltpu.roll(x, 96, 1), pltpu.roll(x, 32, 1))


def _rope(x, cos, sin):
    return x * cos + _swap32(x) * sin


def _unrope(dy, cos, sin):
    return dy * cos + _swap32(dy * sin)


def _ret_weights(lgf, lgb, dist):
    return jnp.exp(jnp.where(dist >= 0.0, lgf * dist, -lgb * dist))


class _RetDecay:
    def __init__(self, lgf, lgb, rows):
        r = lax.broadcasted_iota(jnp.int32, (rows, RET_DIM), 0).astype(F32)
        self.head = r + 1.0
        self.tail = (rows - 1.0) - r
        self.q_f = jnp.exp(lgf * self.head)
        self.k_f = jnp.exp(lgf * self.tail)
        self.q_b = jnp.exp(lgb * self.tail)
        self.k_b = jnp.exp(lgb * self.head)


def _ret_states(kf32, vs, lgf, lgb, C, c, nt, hf, hb, hfa=None, hba=None):
    dec = _RetDecay(lgf, lgb, c)
    dec_c = _RetDecay(lgf, lgb, C)
    step_f = jnp.exp(jnp.zeros((RET_DIM, RET_DIM), F32) + lgf * c)
    step_b = jnp.exp(jnp.zeros((RET_DIM, RET_DIM), F32) + lgb * c)

    def upd(rows, kdec):
        return _dot_tn((kf32[rows, :] * kdec).astype(BF16), vs[rows, :])

    def lat(t):
        return slice(C + t * c, C + (t + 1) * c)

    state = upd(slice(0, C), dec_c.k_f)
    aged = jnp.zeros_like(state)
    for t in range(nt):
        hf[t] = state.astype(BF16)
        if hfa is not None:
            hfa[t] = aged
        if t < nt - 1:
            aged = step_f * (aged + c * state)
            state = step_f * state + upd(lat(t), dec.k_f)
    state = upd(slice(0, C), dec_c.k_b)
    aged = jnp.zeros_like(state)
    for t in range(nt - 1, -1, -1):
        hb[t] = state.astype(BF16)
        if hba is not None:
            hba[t] = aged
        if t > 0:
            aged = step_b * (aged + c * state)
            state = step_b * state + upd(lat(t), dec.k_b)
    return dec, dec_c, step_f, step_b


def _ret_fwd(proj, cos, sin, lg, gn, n_ctx, hosted):
    B, T, _ = proj.shape
    C = n_ctx
    N = T - C
    c = _div_tile(N, 256, 16)
    nt = N // c
    scale = RET_DIM ** -0.5

    def body(lg_ref, q_ref, k_ref, v_ref, g_ref, cos_ref, sin_ref, gn_ref, o_ref, lat_ref, qs, ks, vs, kf32, hf, hb):
        h = pl.program_id(1)
        lgf = lg_ref[0, h]
        lgb = lg_ref[1, h]
        for rows in [slice(0, C)] + [slice(C + t * c, C + (t + 1) * c) for t in range(nt)]:
            cosb = cos_ref[rows, :]
            sinb = sin_ref[rows, :]
            qs[rows, :] = (_rope(q_ref[rows, :], cosb, sinb) * scale).astype(BF16)
            kr = _rope(k_ref[rows, :], cosb, sinb)
            kf32[rows, :] = kr
            ks[rows, :] = kr.astype(BF16)
            vs[rows, :] = v_ref[rows, :].astype(BF16)
        gnv = gn_ref[...]
        dec, _, _, _ = _ret_states(kf32, vs, lgf, lgb, C, c, nt, hf, hb)
        rc = (lax.broadcasted_iota(jnp.int32, (c, c), 0) - lax.broadcasted_iota(jnp.int32, (c, c), 1)).astype(F32)
        w_diag = _ret_weights(lgf, lgb, rc)
        for t in range(nt):
            rows = slice(C + t * c, C + (t + 1) * c)
            qt = qs[rows, :]
            s = _dot_nt(qt, ks[rows, :])
            o = (_dot((s * w_diag).astype(BF16), vs[rows, :])
                 + dec.q_f * _dot(qt, hf[t]) + dec.q_b * _dot(qt, hb[t]))
            o_ref[t * c:(t + 1) * c, :] = o
            mu = jnp.mean(o, axis=-1, keepdims=True)
            oc = o - mu
            var = jnp.mean(oc * oc, axis=-1, keepdims=True)
            yh = oc * lax.rsqrt(var + NORM_EPS)
            g = g_ref[rows, :]
            lat_ref[t * c:(t + 1) * c, :] = ((yh * gnv) * (g * _sigmoid(g))).astype(BF16)

    def col(seg):
        return pl.BlockSpec((None, T, RET_DIM), lambda b, h, seg=seg: (b, 0, seg * RET_HEADS + h))

    return _call_hosting(
        body, hosted, name="ret_fwd", grid=(B, RET_HEADS),
        out_shape=(jax.ShapeDtypeStruct((B, N, RET_WIDTH), F32), jax.ShapeDtypeStruct((B, N, RET_WIDTH), BF16)),
        in_specs=[pl.BlockSpec(memory_space=pltpu.SMEM), col(0), col(1), col(2), col(3),
                  pl.BlockSpec((T, RET_DIM), lambda b, h: (0, 0)), pl.BlockSpec((T, RET_DIM), lambda b, h: (0, 0)),
                  pl.BlockSpec((1, RET_DIM), lambda b, h: (0, h))],
        out_specs=(pl.BlockSpec((None, N, RET_DIM), lambda b, h: (b, 0, h)),
                   pl.BlockSpec((None, N, RET_DIM), lambda b, h: (b, 0, h))),
        scratch_shapes=[pltpu.VMEM((T, RET_DIM), BF16)] * 3 + [pltpu.VMEM((T, RET_DIM), F32)]
                       + [pltpu.VMEM((nt, RET_DIM, RET_DIM), BF16)] * 2,
        args=(lg, proj, proj, proj, proj, cos, sin, gn))


def _ret_bwd(proj, cos, sin, lg, gn, o, dlat, n_ctx, hosted):
    B, T, _ = proj.shape
    C = n_ctx
    N = T - C
    c = _div_tile(N, 256, 16)
    nt = N // c
    scale = RET_DIM ** -0.5

    def lat(t):
        return slice(C + t * c, C + (t + 1) * c)

    def body(lg_ref, q_ref, k_ref, v_ref, g_ref, cos_ref, sin_ref, gn_ref, o_ref, dl_ref,
             d_ref, dgn_ref, dlg_ref, qs, ks, vs, dos, qf32, kf32, hf, hb, hfa, hba, gf_s, gb_s):
        h = pl.program_id(1)
        lgf = lg_ref[0, h]
        lgb = lg_ref[1, h]
        gnv = gn_ref[...]

        def fold(a):
            return jnp.sum(a.reshape(a.shape[0] // SUBLANES, SUBLANES, a.shape[1]), axis=0)

        for rows in [slice(0, C)] + [lat(t) for t in range(nt)]:
            cosb = cos_ref[rows, :]
            sinb = sin_ref[rows, :]
            qr = _rope(q_ref[rows, :], cosb, sinb) * scale
            qf32[rows, :] = qr
            qs[rows, :] = qr.astype(BF16)
            kr = _rope(k_ref[rows, :], cosb, sinb)
            kf32[rows, :] = kr
            ks[rows, :] = kr.astype(BF16)
            vs[rows, :] = v_ref[rows, :].astype(BF16)

        dgn = jnp.zeros((1, RET_DIM), F32)
        for t in range(nt):
            lrows = slice(t * c, (t + 1) * c)
            ov = o_ref[lrows, :]
            mu = jnp.mean(ov, axis=-1, keepdims=True)
            oc = ov - mu
            var = jnp.mean(oc * oc, axis=-1, keepdims=True)
            rstd = lax.rsqrt(var + NORM_EPS)
            yh = oc * rstd
            g = g_ref[lat(t), :]
            sg = _sigmoid(g)
            dl = dl_ref[lrows, :]
            d_ref[3, lat(t), :] = (dl * (yh * gnv) * (sg * (1.0 + g * (1.0 - sg)))).astype(BF16)
            dls = dl * (g * sg)
            dgn = dgn + jnp.sum(dls * yh, axis=0, keepdims=True)
            dyh = dls * gnv
            do = rstd * (dyh - jnp.mean(dyh, axis=-1, keepdims=True)
                         - yh * jnp.mean(dyh * yh, axis=-1, keepdims=True))
            dos[lrows, :] = do.astype(BF16)
        dgn_ref[...] = jnp.concatenate([dgn, jnp.zeros((SUBLANES - 1, RET_DIM), F32)], axis=0)
        d_ref[3, 0:C, :] = jnp.zeros((C, RET_DIM), BF16)
        d_ref[0, 0:C, :] = jnp.zeros((C, RET_DIM), BF16)

        dec, dec_c, step_f, step_b = _ret_states(kf32, vs, lgf, lgb, C, c, nt, hf, hb, hfa, hba)

        def zmat(t, qdec):
            return _dot_tn((qf32[lat(t), :] * qdec).astype(BF16), dos[t * c:(t + 1) * c, :])

        acc3f = jnp.zeros((RET_DIM, RET_DIM), F32)
        acc3b = jnp.zeros((RET_DIM, RET_DIM), F32)
        state = jnp.zeros((RET_DIM, RET_DIM), F32)
        for t in range(nt - 1, -1, -1):
            gf_s[t] = state.astype(BF16)
            z = zmat(t, dec.q_f)
            acc3f = acc3f + hfa[t] * z
            state = step_f * state + z
        gctx_f = state.astype(BF16)
        state = jnp.zeros((RET_DIM, RET_DIM), F32)
        for t in range(nt):
            gb_s[t] = state.astype(BF16)
            z = zmat(t, dec.q_b)
            acc3b = acc3b + hba[t] * z
            state = step_b * state + z
        gctx_b = state.astype(BF16)

        rc = (lax.broadcasted_iota(jnp.int32, (c, c), 0) - lax.broadcasted_iota(jnp.int32, (c, c), 1)).astype(F32)
        w_diag = _ret_weights(lgf, lgb, rc)
        wg_f = jnp.where(rc >= 0.0, w_diag * rc, 0.0)
        wg_b = jnp.where(rc < 0.0, -w_diag * rc, 0.0)
        accf = jnp.zeros((SUBLANES, RET_DIM), F32)
        accb = jnp.zeros((SUBLANES, RET_DIM), F32)
        gdf = jnp.zeros((SUBLANES, c), F32)
        gdb = jnp.zeros((SUBLANES, c), F32)
        for t in range(nt):
            rows = lat(t)
            qt = qs[rows, :]
            kt = ks[rows, :]
            vt = vs[rows, :]
            dot = dos[t * c:(t + 1) * c, :]
            s = _dot_nt(qt, kt)
            dp = _dot_nt(dot, vt)
            dv = _dot_tn((s * w_diag).astype(BF16), dot)
            ds = (dp * w_diag).astype(BF16)
            dq = _dot(ds, kt)
            dk = _dot_tn(ds, qt)
            gs = dp * s
            gdf = gdf + fold(gs * wg_f)
            gdb = gdb + fold(gs * wg_b)
            qv = qf32[rows, :]
            kv = kf32[rows, :]
            dq_f = dec.q_f * _dot_nt(dot, hf[t])
            dq_b = dec.q_b * _dot_nt(dot, hb[t])
            dk_f = dec.k_f * _dot_nt(vt, gf_s[t])
            dk_b = dec.k_b * _dot_nt(vt, gb_s[t])
            accf = accf + fold(dec.head * dq_f * qv) + fold(dec.tail * dk_f * kv)
            accb = accb + fold(dec.tail * dq_b * qv) + fold(dec.head * dk_b * kv)
            dv = dv + dec.k_f * _dot(kt, gf_s[t]) + dec.k_b * _dot(kt, gb_s[t])
            cosb = cos_ref[rows, :]
            sinb = sin_ref[rows, :]
            d_ref[0, rows, :] = _unrope((dq + dq_f + dq_b) * scale, cosb, sinb).astype(BF16)
            d_ref[1, rows, :] = _unrope(dk + dk_f + dk_b, cosb, sinb).astype(BF16)
            d_ref[2, rows, :] = dv.astype(BF16)
        kc = ks[0:C, :]
        vc = vs[0:C, :]
        kcv = kf32[0:C, :]
        dkc_f = dec_c.k_f * _dot_nt(vc, gctx_f)
        dkc_b = dec_c.k_b * _dot_nt(vc, gctx_b)
        accf = accf + fold(dec_c.tail * dkc_f * kcv)
        accb = accb + fold(dec_c.head * dkc_b * kcv)
        d_ref[1, 0:C, :] = (dkc_f + dkc_b).astype(BF16)
        d_ref[2, 0:C, :] = (dec_c.k_f * _dot(kc, gctx_f) + dec_c.k_b * _dot(kc, gctx_b)).astype(BF16)
        gf = jnp.sum(gdf) + jnp.sum(accf) + jnp.sum(acc3f)
        gb = jnp.sum(gdb) + jnp.sum(accb) + jnp.sum(acc3b)
        row = lax.broadcasted_iota(jnp.int32, (SUBLANES, LANES), 0)
        dlg_ref[...] = jnp.where(row == 0, gf, jnp.where(row == 1, gb, 0.0))

    def col(seg):
        return pl.BlockSpec((None, T, RET_DIM), lambda b, h, seg=seg: (b, 0, seg * RET_HEADS + h))

    return _call_hosting(
        body, hosted, name="ret_bwd", grid=(B, RET_HEADS),
        out_shape=(jax.ShapeDtypeStruct((B, 4, T, RET_WIDTH), BF16),
                   jax.ShapeDtypeStruct((B, SUBLANES, RET_WIDTH), F32),
                   jax.ShapeDtypeStruct((B, RET_HEADS, SUBLANES, LANES), F32)),
        in_specs=[pl.BlockSpec(memory_space=pltpu.SMEM), col(0), col(1), col(2), col(3),
                  pl.BlockSpec((T, RET_DIM), lambda b, h: (0, 0)), pl.BlockSpec((T, RET_DIM), lambda b, h: (0, 0)),
                  pl.BlockSpec((1, RET_DIM), lambda b, h: (0, h)),
                  pl.BlockSpec((None, N, RET_DIM), lambda b, h: (b, 0, h)),
                  pl.BlockSpec((None, N, RET_DIM), lambda b, h: (b, 0, h))],
        out_specs=(pl.BlockSpec((None, 4, T, RET_DIM), lambda b, h: (b, 0, 0, h)),
                   pl.BlockSpec((None, SUBLANES, RET_DIM), lambda b, h: (b, 0, h)),
                   pl.BlockSpec((None, None, SUBLANES, LANES), lambda b, h: (b, h, 0, 0))),
        scratch_shapes=[pltpu.VMEM((T, RET_DIM), BF16)] * 3 + [pltpu.VMEM((N, RET_DIM), BF16)]
                       + [pltpu.VMEM((T, RET_DIM), F32)] * 2
                       + [pltpu.VMEM((nt, RET_DIM, RET_DIM), BF16)] * 2 + [pltpu.VMEM((nt, RET_DIM, RET_DIM), F32)] * 2
                       + [pltpu.VMEM((nt, RET_DIM, RET_DIM), BF16)] * 2,
        args=(lg, proj, proj, proj, proj, cos, sin, gn, o, dlat))


def _na_geometry(rows):
    kh = min(NA_KH, rows)
    return kh, kh * GRID_W


def _pair_select():
    lane = lax.broadcasted_iota(jnp.int32, (2 * GRID_W, LANES), 1)
    row = lax.broadcasted_iota(jnp.int32, (2 * GRID_W, LANES), 0)
    return (lane >= NA_DIM) == (row >= GRID_W)


def _pair_bias(bias_ref, dr0, kh):
    return jnp.concatenate(
        [jnp.concatenate([bias_ref[e, pl.ds(dr0 + 2 * m, 1)].reshape(GRID_W, LANES) for m in range(kh // 2)], axis=1)
         for e in range(2)], axis=0)


def _na_softmax(s_loc, s_ctx):
    mx = jnp.maximum(jnp.max(s_loc, axis=-1, keepdims=True), jnp.max(s_ctx, axis=-1, keepdims=True))
    p_loc = jnp.exp(s_loc - mx)
    p_ctx = jnp.exp(s_ctx - mx)
    den = jnp.sum(p_loc, axis=-1, keepdims=True) + jnp.sum(p_ctx, axis=-1, keepdims=True)
    return p_loc, p_ctx, den


def _na_fwd(proj, bias2, n_ctx, hosted):
    B, T, _ = proj.shape
    C = n_ctx
    N = T - C
    R = N // GRID_W
    kh, nk = _na_geometry(R)
    scale = NA_DIM ** -0.5
    base = (4 * RET_WIDTH) // LANES

    def body(q_ref, k_ref, v_ref, bias_ref, out_ref, kb16, vb16):
        kb16[...] = k_ref[...].astype(BF16)
        vb16[...] = v_ref[...].astype(BF16)
        kc = kb16[0:C, :]
        vc = vb16[0:C, :]
        lane = lax.broadcasted_iota(jnp.int32, (GRID_W, LANES), 1)
        sel2 = _pair_select()

        def group(gi, carry):
            pre = []
            for u in range(NA_GROUP):
                r = gi * NA_GROUP + u
                bs = jnp.clip(r - kh // 2, 0, R - kh)
                dr0 = bs - r + (NA_KH - 1)
                q = q_ref[pl.ds(pl.multiple_of(C + r * GRID_W, GRID_W), GRID_W), :] * scale
                q2 = jnp.where(sel2, jnp.concatenate([q, q], axis=0), 0.0).astype(BF16)
                band = pl.ds(pl.multiple_of(C + bs * GRID_W, GRID_W), nk)
                s_loc = _dot_nt(q2, kb16[band, :]) + _pair_bias(bias_ref, dr0, kh)
                s_ctx = _dot_nt(q2, kc)
                pre.append((r, band, s_loc, s_ctx))
            mid = [(r, band) + _na_softmax(s_loc, s_ctx) for r, band, s_loc, s_ctx in pre]
            for r, band, p_loc, p_ctx, den in mid:
                o2 = (_dot(p_loc.astype(BF16), vb16[band, :]) + _dot(p_ctx.astype(BF16), vc)) / den
                out_ref[pl.ds(pl.multiple_of(r * GRID_W, GRID_W), GRID_W), :] = jnp.where(
                    lane < NA_DIM, o2[:GRID_W], o2[GRID_W:]).astype(BF16)
            return carry

        lax.fori_loop(0, R // NA_GROUP, group, 0)

    def col(seg):
        return pl.BlockSpec((None, T, LANES), lambda b, p, seg=seg: (b, 0, base + seg * NA_PAIRS + p))

    return _call_hosting(
        body, hosted, name="na_fwd", grid=(B, NA_PAIRS),
        out_shape=(jax.ShapeDtypeStruct((B, N, NA_WIDTH), BF16),),
        in_specs=[col(0), col(1), col(2),
                  pl.BlockSpec((2, 2 * NA_KH - 2, GRID_W, LANES), lambda b, p: (p, 0, 0, 0))],
        out_specs=(pl.BlockSpec((None, N, LANES), lambda b, p: (b, 0, p)),),
        scratch_shapes=[pltpu.VMEM((T, LANES), BF16)] * 2,
        args=(proj, proj, proj, bias2))


def _na_bwd(proj, bias2, dlat, n_ctx, hosted):
    B, T, _ = proj.shape
    C = n_ctx
    N = T - C
    R = N // GRID_W
    kh, nk = _na_geometry(R)
    scale = NA_DIM ** -0.5
    base = (4 * RET_WIDTH) // LANES

    def body(q_ref, k_ref, v_ref, bias_ref, dl_ref, d_ref, db_ref, kb16, vb16, dkv):
        b = pl.program_id(1)
        kb16[...] = k_ref[...].astype(BF16)
        vb16[...] = v_ref[...].astype(BF16)
        kc = kb16[0:C, :]
        vc = vb16[0:C, :]
        lane = lax.broadcasted_iota(jnp.int32, (GRID_W, LANES), 1)
        dkv[...] = jnp.zeros(dkv.shape, F32)
        d_ref[0, 0:C, :] = jnp.zeros((C, LANES), BF16)

        @pl.when(b == 0)
        def _():
            db_ref[...] = jnp.zeros(db_ref.shape, F32)

        sel2 = _pair_select()

        def group(gi, carry):
            pre = []
            for u in range(NA_GROUP):
                r = gi * NA_GROUP + u
                bs = jnp.clip(r - kh // 2, 0, R - kh)
                dr0 = bs - r + (NA_KH - 1)
                q = q_ref[pl.ds(pl.multiple_of(C + r * GRID_W, GRID_W), GRID_W), :] * scale
                do = dl_ref[pl.ds(pl.multiple_of(r * GRID_W, GRID_W), GRID_W), :]
                q2 = jnp.where(sel2, jnp.concatenate([q, q], axis=0), 0.0).astype(BF16)
                do2 = jnp.where(sel2, jnp.concatenate([do, do], axis=0), 0.0).astype(BF16)
                band = pl.ds(pl.multiple_of(C + bs * GRID_W, GRID_W), nk)
                s_loc = _dot_nt(q2, kb16[band, :]) + _pair_bias(bias_ref, dr0, kh)
                s_ctx = _dot_nt(q2, kc)
                dp_loc = _dot_nt(do2, vb16[band, :])
                dp_ctx = _dot_nt(do2, vc)
                pre.append((r, dr0, band, q2, do2, s_loc, s_ctx, dp_loc, dp_ctx))
            mid = []
            for r, dr0, band, q2, do2, s_loc, s_ctx, dp_loc, dp_ctx in pre:
                p_loc, p_ctx, den = _na_softmax(s_loc, s_ctx)
                inv = 1.0 / den
                p_loc = p_loc * inv
                p_ctx = p_ctx * inv
                delta = (jnp.sum(p_loc * dp_loc, axis=-1, keepdims=True)
                         + jnp.sum(p_ctx * dp_ctx, axis=-1, keepdims=True))
                ds_loc = p_loc * (dp_loc - delta)
                ds_ctx = p_ctx * (dp_ctx - delta)
                mid.append((r, dr0, band, q2, do2, p_loc.astype(BF16), p_ctx.astype(BF16), ds_loc, ds_ctx))
            for r, dr0, band, q2, do2, pb_loc, pb_ctx, ds_loc, ds_ctx in mid:
                dsb_loc = ds_loc.astype(BF16)
                dsb_ctx = ds_ctx.astype(BF16)
                dq2 = _dot(dsb_loc, kb16[band, :]) + _dot(dsb_ctx, kc)
                d_ref[0, pl.ds(pl.multiple_of(C + r * GRID_W, GRID_W), GRID_W), :] = (jnp.where(
                    lane < NA_DIM, dq2[:GRID_W], dq2[GRID_W:]) * scale).astype(BF16)
                dkv[0, band, :] += _dot_tn(dsb_loc, q2)
                dkv[1, band, :] += _dot_tn(pb_loc, do2)
                dkv[0, 0:C, :] += _dot_tn(dsb_ctx, q2)
                dkv[1, 0:C, :] += _dot_tn(pb_ctx, do2)
                for e in range(2):
                    for m in range(kh // 2):
                        db_ref[e, pl.ds(dr0 + 2 * m, 1)] += ds_loc[e * GRID_W:(e + 1) * GRID_W,
                                                                   m * LANES:(m + 1) * LANES].reshape(1, GRID_W, LANES)
            return carry

        lax.fori_loop(0, R // NA_GROUP, group, 0)
        d_ref[1] = dkv[0].astype(BF16)
        d_ref[2] = dkv[1].astype(BF16)

    def col(seg):
        return pl.BlockSpec((None, T, LANES), lambda p, b, seg=seg: (b, 0, base + seg * NA_PAIRS + p))

    return _call_hosting(
        body, hosted, name="na_bwd", grid=(NA_PAIRS, B),
        out_shape=(jax.ShapeDtypeStruct((B, 3, T, NA_WIDTH), BF16),
                   jax.ShapeDtypeStruct((NA_HEADS, 2 * NA_KH - 2, GRID_W, LANES), F32)),
        in_specs=[col(0), col(1), col(2),
                  pl.BlockSpec((2, 2 * NA_KH - 2, GRID_W, LANES), lambda p, b: (p, 0, 0, 0)),
                  pl.BlockSpec((None, N, LANES), lambda p, b: (b, 0, p))],
        out_specs=(pl.BlockSpec((None, 3, T, LANES), lambda p, b: (b, 0, 0, p)),
                   pl.BlockSpec((2, 2 * NA_KH - 2, GRID_W, LANES), lambda p, b: (p, 0, 0, 0))),
        scratch_shapes=[pltpu.VMEM((T, LANES), BF16)] * 2 + [pltpu.VMEM((2, T, LANES), F32)],
        args=(proj, proj, proj, bias2, dlat))


def _split3(a):
    hi = a.astype(BF16)
    r1 = a - hi.astype(F32)
    mid = r1.astype(BF16)
    lo = (r1 - mid.astype(F32)).astype(BF16)
    return hi, mid, lo


def _rpb_reduce(dbias2, onehot2):
    rows = dbias2.shape[0] * dbias2.shape[1]
    flat = dbias2.reshape(rows, GRID_W * LANES)

    def body(a_ref, oh_ref, o_ref):
        hi, mid, lo = _split3(a_ref[...])
        oh = oh_ref[...]
        o_ref[...] = _dot(hi, oh) + _dot(mid, oh) + _dot(lo, oh)

    return pl.pallas_call(
        body, name="rpb_reduce", out_shape=jax.ShapeDtypeStruct((rows, LANES), F32),
        in_specs=[_vmem(), _vmem()], out_specs=_vmem(),
        compiler_params=pltpu.CompilerParams(vmem_limit_bytes=VMEM_LIMIT),
    )(flat, onehot2)


def _dense_core(lat_ret, lat_na, x, tgt, modl, g_post_mix, g_pre_mlp, g_post_mlp, w_out, w1, w2):
    B, N, D = x.shape
    F = w1.shape[1]
    w2_rows = w2.shape[0] // N_DEV
    mixw = w_out.shape[0]
    half = mixw // 2
    tm = _div_tile(N, 256, 16)
    nt = N // tm
    fc = _div_tile(F, 1024, LANES)

    def body(lr_ref, ln_ref, x_ref, t_ref, gt1_ref, sh2_ref, sc2_ref, gt2_ref, gpm_ref, gpre_ref, gpo_ref,
             wout_hbm, w1_hbm, w2_part,
             dy1_ref, dlr_ref, dln_ref, dmix_ref, h2_ref, a_ref, du_ref, dz_ref, red_ref, w2_hbm,
             wout_v, w1_v, w2_v, u_s, sems, fsend, frecv):
        @pl.when((pl.program_id(0) == 0) & (pl.program_id(1) == 0))
        def _():
            relay = [(_row_block(w2_part, w2_rows), _row_block(w2_hbm, w2_rows))]
            _forward_start(relay, fsend, frecv)
            cps = [pltpu.make_async_copy(wout_hbm, wout_v, sems.at[0]),
                   pltpu.make_async_copy(w1_hbm, w1_v, sems.at[1])]
            for cp in cps:
                cp.start()
            _forward_wait(relay, fsend, frecv)
            cps.append(pltpu.make_async_copy(w2_hbm, w2_v, sems.at[2]))
            cps[2].start()
            for cp in cps:
                cp.wait()

        gt1 = gt1_ref[...]
        sh2 = sh2_ref[...]
        sc2 = sc2_ref[...]
        gt2 = gt2_ref[...]
        gpm = gpm_ref[...]
        gpre = gpre_ref[...]
        gpo = gpo_ref[...]

        def rowmean(a):
            return jnp.mean(a, axis=-1, keepdims=True)

        def colsum(a):
            return jnp.sum(a, axis=0, keepdims=True)

        mix = _dot(lr_ref[...], wout_v[0:half, :]) + _dot(ln_ref[...], wout_v[half:, :])
        x = x_ref[...]
        rm = lax.rsqrt(rowmean(mix * mix) + NORM_EPS)
        mh = mix * rm
        nm = mh * gpm
        y1 = x + gt1 * nm
        r1 = lax.rsqrt(rowmean(y1 * y1) + NORM_EPS)
        xh = y1 * r1
        n1 = xh * gpre
        h2b = (n1 * (1.0 + sc2) + sh2).astype(BF16)
        h2_ref[...] = h2b
        z = jnp.zeros((tm, D), F32)
        for c0 in range(0, F, fc):
            u = _dot(h2b, w1_v[:, c0:c0 + fc])
            u_s[:, c0:c0 + fc] = u
            ru = jnp.maximum(u, 0.0)
            ab = (ru * ru).astype(BF16)
            a_ref[:, c0:c0 + fc] = ab
            z = z + _dot(ab, w2_v[c0:c0 + fc, :])
        r2 = lax.rsqrt(rowmean(z * z) + NORM_EPS)
        zh = z * r2
        n2 = zh * gpo
        y2 = y1 + gt2 * n2
        err = y2 - t_ref[...]
        loss = 0.5 * jnp.sum(rowmean(err * err))
        dy2 = err * (1.0 / D)
        red_ref[2:3, :] = colsum(dy2 * n2)
        dn2 = dy2 * gt2
        red_ref[6:7, :] = colsum(dn2 * zh)
        dzh = dn2 * gpo
        dz = r2 * (dzh - zh * rowmean(dzh * zh))
        dzb = dz.astype(BF16)
        dz_ref[...] = dzb
        dh2 = jnp.zeros((tm, D), F32)
        for c0 in range(0, F, fc):
            da = _dot_nt(dzb, w2_v[c0:c0 + fc, :])
            dub = (da * (2.0 * jnp.maximum(u_s[:, c0:c0 + fc], 0.0))).astype(BF16)
            du_ref[:, c0:c0 + fc] = dub
            dh2 = dh2 + _dot_nt(dub, w1_v[:, c0:c0 + fc])
        red_ref[3:4, :] = colsum(dh2 * n1)
        red_ref[4:5, :] = colsum(dh2)
        dn1 = dh2 * (1.0 + sc2)
        red_ref[5:6, :] = colsum(dn1 * xh)
        dxh = dn1 * gpre
        dy1 = dy2 + r1 * (dxh - xh * rowmean(dxh * xh))
        dy1_ref[...] = dy1
        red_ref[0:1, :] = colsum(dy1 * nm)
        dnm = dy1 * gt1
        red_ref[1:2, :] = colsum(dnm * mh)
        dmh = dnm * gpm
        dmix = (rm * (dmh - mh * rowmean(dmh * mh))).astype(BF16)
        dmix_ref[...] = dmix
        dlr_ref[...] = _dot_nt(dmix, wout_v[0:half, :])
        dln_ref[...] = _dot_nt(dmix, wout_v[half:, :])
        red_ref[7:8, :] = jnp.zeros((1, D), F32) + loss

    def tok(w):
        return pl.BlockSpec((None, tm, w), lambda b, t: (b, t, 0))

    def mod(k):
        return pl.BlockSpec((None, None, 1, D), lambda b, t, k=k: (b, k, 0, 0))

    def vec():
        return pl.BlockSpec((1, D), lambda b, t: (0, 0))

    return pl.pallas_call(
        body, name="dense_core", grid=(B, nt),
        out_shape=(jax.ShapeDtypeStruct((B, N, D), F32), jax.ShapeDtypeStruct((B, N, half), F32),
                   jax.ShapeDtypeStruct((B, N, half), F32), jax.ShapeDtypeStruct((B, N, D), BF16),
                   jax.ShapeDtypeStruct((B, N, D), BF16), jax.ShapeDtypeStruct((B, N, F), BF16),
                   jax.ShapeDtypeStruct((B, N, F), BF16), jax.ShapeDtypeStruct((B, N, D), BF16),
                   jax.ShapeDtypeStruct((B, nt, SUBLANES, D), F32),
                   jax.ShapeDtypeStruct(w2.shape, w2.dtype)),
        in_specs=[tok(half), tok(half), tok(D), tok(D), mod(2), mod(3), mod(4), mod(5), vec(), vec(), vec(),
                  _any(), _any(), _any()],
        out_specs=(tok(D), tok(half), tok(half), tok(D), tok(D), tok(F), tok(F), tok(D),
                   pl.BlockSpec((None, None, SUBLANES, D), lambda b, t: (b, t, 0, 0)), _any()),
        scratch_shapes=[pltpu.VMEM((mixw, D), BF16), pltpu.VMEM((D, F), BF16), pltpu.VMEM((F, D), BF16),
                        pltpu.VMEM((tm, F), F32), pltpu.SemaphoreType.DMA((3,)),
                        pltpu.SemaphoreType.DMA((1, 3)), pltpu.SemaphoreType.DMA((1, 3))],
        input_output_aliases={13: 9},
        compiler_params=_params("arbitrary", "arbitrary"),
    )(lat_ret, lat_na, x, tgt, modl, modl, modl, modl, g_post_mix, g_pre_mlp, g_post_mlp, w_out, w1, w2)[:9]


def _inproj_bwd(dret, dna, x, ctx, dy1, modl, g1, w_in_t, hosted):
    B, N, D = x.shape
    n_ctx = ctx.shape[1]
    T = n_ctx + N
    tm = _div_tile(n_ctx, 256, 16)
    nct, ctx_spec, lat_spec = _token_tiles(n_ctx, tm)
    nt = T // tm
    nseg_r = dret.shape[1]
    nseg_n = dna.shape[1]
    nw = w_in_t.shape[0]

    def body(*refs):
        seg_refs = refs[:nseg_r + nseg_n]
        c_ref, x_ref, dy1_ref, sc_ref, g_ref, w_ref, dx_ref, red_ref = refs[nseg_r + nseg_n:]
        t = pl.program_id(1)
        dh = jnp.zeros((tm, D), F32)
        for s, ref in enumerate(seg_refs):
            dh = dh + _dot(ref[...], w_ref[s * SEG:(s + 1) * SEG, :])
        x = jnp.where(t < nct, c_ref[...], x_ref[...])
        g = g_ref[...]
        r = lax.rsqrt(jnp.mean(x * x, axis=-1, keepdims=True) + NORM_EPS)
        xh = x * r
        red_ref[0:1, :] = jnp.sum(dh, axis=0, keepdims=True)
        red_ref[1:2, :] = jnp.sum(dh * (xh * g), axis=0, keepdims=True)
        dn = dh * (1.0 + sc_ref[...])
        red_ref[2:3, :] = jnp.sum(dn * xh, axis=0, keepdims=True)
        red_ref[3:, :] = jnp.zeros((SUBLANES - 3, D), F32)
        dxh = dn * g
        dx = r * (dxh - xh * jnp.mean(dxh * xh, axis=-1, keepdims=True))
        dx_ref[...] = dx + jnp.where(t >= nct, dy1_ref[...], 0.0)

    def mrow(b, t):
        return jnp.where(t < nct, B, b)

    def seg(s):
        return pl.BlockSpec((None, None, tm, SEG), lambda b, t, s=s: (b, s, t, 0))

    return _call_hosting(
        body, hosted, name="inproj_bwd", grid=(B, nt),
        out_shape=(jax.ShapeDtypeStruct((B, N, D), F32), jax.ShapeDtypeStruct((B, nt, SUBLANES, D), F32)),
        in_specs=[seg(s) for s in range(nseg_r)] + [seg(s) for s in range(nseg_n)]
                 + [ctx_spec(D), lat_spec(D), lat_spec(D),
                    pl.BlockSpec((None, None, 1, D), lambda b, t: (mrow(b, t), 1, 0, 0)),
                    pl.BlockSpec((1, D), lambda b, t: (0, 0)),
                    pl.BlockSpec((nw, D), lambda b, t: (0, 0))],
        out_specs=(lat_spec(D), pl.BlockSpec((None, None, SUBLANES, D), lambda b, t: (b, t, 0, 0))),
        scratch_shapes=[], args=(*([dret] * nseg_r), *([dna] * nseg_n), ctx, x, dy1, modl, g1, w_in_t))


def _tn_matmul(lhs, rhs, name):
    B, S, T, W = lhs.shape
    nn = rhs.shape[-1]
    tk = _div_tile(T, 1024, LANES)
    bm = _div_tile(W, 1024, LANES)
    bn = _div_tile(nn, 1024, LANES)
    nkt = T // tk
    nk = B * nkt

    def body(l_ref, r_ref, o_ref, acc):
        k = pl.program_id(3)

        @pl.when(k == 0)
        def _():
            acc[...] = jnp.zeros(acc.shape, F32)

        acc[...] += _dot_tn(l_ref[...].astype(BF16), r_ref[...].astype(BF16))

        @pl.when(k == nk - 1)
        def _():
            o_ref[...] = acc[...].astype(BF16)

    nwb = W // bm
    return pl.pallas_call(
        functools.partial(body), name=name, grid=(S, nwb, nn // bn, nk),
        out_shape=jax.ShapeDtypeStruct((S * W, nn), BF16),
        in_specs=[pl.BlockSpec((None, None, tk, bm), lambda s, i, j, k: (k // nkt, s, k % nkt, i)),
                  pl.BlockSpec((None, tk, bn), lambda s, i, j, k: (k // nkt, k % nkt, j))],
        out_specs=pl.BlockSpec((bm, bn), lambda s, i, j, k: (s * nwb + i, j)),
        scratch_shapes=[pltpu.VMEM((bm, bn), F32)],
        compiler_params=_params("parallel", "parallel", "parallel", "arbitrary"),
    )(lhs, rhs)


def _sum_slots(buf, name):
    _, rows, cols = buf.shape
    tr = _div_tile(rows, 256, 2 * SUBLANES)

    def body(b_ref, o_ref):
        acc = b_ref[0].astype(F32)
        for k in range(1, N_DEV):
            acc = acc + b_ref[k].astype(F32)
        o_ref[...] = acc

    return pl.pallas_call(
        functools.partial(body), name=name, grid=(rows // tr,),
        out_shape=jax.ShapeDtypeStruct((rows, cols), F32),
        in_specs=[pl.BlockSpec((N_DEV, tr, cols), lambda i: (0, i, 0))],
        out_specs=pl.BlockSpec((tr, cols), lambda i: (i, 0)),
        compiler_params=_params("parallel"),
    )(buf)


def _small_ar(vec, dmods, silu_all, w_ada, c_ctx):
    rv = vec.shape[0]
    D = silu_all.shape[1]
    ncol = w_ada.shape[1]
    nm = dmods.shape[1]
    srows = silu_all.shape[0]

    def body(vec_ref, dm_ref, s_ref, w_ref, cc_ref, tot_ref, gb_ref, gw_ref, gc_ref,
             vbuf, mbuf, tbuf, dmx, send1, recv1, send3, recv3):
        me, _ = _me_and_peers()
        vbuf[me] = vec_ref[...]
        mbuf[me] = dm_ref[...]
        both = [(lambda p: vbuf.at[me], lambda p: vbuf.at[p]), (lambda p: mbuf.at[me], lambda p: mbuf.at[p])]
        _push_start(both, ALL_PEERS, send1, recv1)
        _push_wait_recv(both, ALL_PEERS, send1, recv1)
        _push_wait_send(both, ALL_PEERS, send1, recv1)
        tot = vbuf[0]
        msum = mbuf[0]
        for k in range(1, N_DEV):
            tot = tot + vbuf[k]
            msum = msum + mbuf[k]
        tot_ref[...] = tot
        gb_ref[...] = jnp.sum(msum, axis=0, keepdims=True)
        loc = pl.ds(pl.multiple_of(me * ncol, ncol), ncol)
        for k in range(N_DEV):
            dmx[k * SUBLANES:(k + 1) * SUBLANES, :] = mbuf[k, :, loc]
        cm = msum[2:3, :]
        mbuf[0, 2:3, :] = cm
        cm_loc = mbuf[0, 2:3, loc]
        dmx[N_DEV * SUBLANES:, :] = jnp.concatenate([cm_loc, jnp.zeros((SUBLANES - 1, ncol), F32)], axis=0)
        gw_ref[...] = _dot_tn(s_ref[...], dmx[...])
        tbuf[me] = _dot_nt(dmx[N_DEV * SUBLANES:, :], w_ref[...])
        _exchange(lambda p: tbuf.at[me], lambda p: tbuf.at[p], send3, recv3)
        tsum = tbuf[0]
        for k in range(1, N_DEV):
            tsum = tsum + tbuf[k]
        cc = cc_ref[...]
        sg = _sigmoid(cc)
        gc_ref[...] = tsum[0:1, :] * (sg * (1.0 + cc * (1.0 - sg)))

    return pl.pallas_call(
        body, name="small_ar",
        out_shape=(jax.ShapeDtypeStruct((rv, LANES), F32), jax.ShapeDtypeStruct((1, nm), F32),
                   jax.ShapeDtypeStruct((D, ncol), F32), jax.ShapeDtypeStruct((1, D), F32)),
        in_specs=[_vmem()] * 5, out_specs=(_vmem(),) * 4,
        scratch_shapes=[pltpu.VMEM((N_DEV, rv, LANES), F32), pltpu.VMEM((N_DEV, SUBLANES, nm), F32),
                        pltpu.VMEM((N_DEV, SUBLANES, D), F32), pltpu.VMEM((srows, ncol), F32)]
                       + [pltpu.SemaphoreType.DMA((2, N_DEV - 1))] * 2 + [pltpu.SemaphoreType.DMA((N_DEV - 1,))] * 2,
        compiler_params=pltpu.CompilerParams(vmem_limit_bytes=VMEM_LIMIT),
    )(vec, dmods, silu_all, w_ada, c_ctx.reshape(1, D))


def _adam_update(w, g, m, v):
    mn = ADAM_B1 * m + (1.0 - ADAM_B1) * g
    vn = ADAM_B2 * v + (1.0 - ADAM_B2) * (g * g)
    m_hat = mn / (1.0 - ADAM_B1 ** ADAM_STEP)
    v_hat = vn / (1.0 - ADAM_B2 ** ADAM_STEP)
    return -ADAM_LR * (m_hat / (jnp.sqrt(v_hat) + ADAM_EPS) + ADAM_WD * w), mn, vn


def _adamw(w, g, m, v, name):
    rows, cols = w.shape
    tr = _div_tile(rows, 256, SUBLANES) if rows * cols > 65536 else rows

    def body(w_ref, g_ref, m_ref, v_ref, d_ref, nm_ref, nv_ref):
        d_ref[...], nm_ref[...], nv_ref[...] = _adam_update(w_ref[...], g_ref[...], m_ref[...], v_ref[...])

    spec = pl.BlockSpec((tr, cols), lambda i: (i, 0))
    return pl.pallas_call(
        functools.partial(body), name=name, grid=(rows // tr,),
        out_shape=(jax.ShapeDtypeStruct((rows, cols), F32),) * 3,
        in_specs=[spec] * 4, out_specs=(spec,) * 3,
        compiler_params=_params("parallel"),
    )(w, g, m, v)


def _sum_adamw(buf, w, m, v, name):
    _, rows, cols = buf.shape
    tr = _div_tile(rows, 256, 2 * SUBLANES)

    def body(b_ref, w_ref, m_ref, v_ref, g_ref, d_ref, nm_ref, nv_ref):
        g = b_ref[0].astype(F32)
        for k in range(1, N_DEV):
            g = g + b_ref[k].astype(F32)
        g_ref[...] = g
        d_ref[...], nm_ref[...], nv_ref[...] = _adam_update(w_ref[...], g, m_ref[...], v_ref[...])

    spec = pl.BlockSpec((tr, cols), lambda i: (i, 0))
    return pl.pallas_call(
        functools.partial(body), name=name, grid=(rows // tr,),
        out_shape=(jax.ShapeDtypeStruct((rows, cols), F32),) * 4,
        in_specs=[pl.BlockSpec((N_DEV, tr, cols), lambda i: (0, i, 0))] + [spec] * 3, out_specs=(spec,) * 4,
        compiler_params=_params("parallel"),
    )(buf, w, m, v)


def _rope_tables(n_ctx, n):
    n_freq = RET_DIM // 4
    inv = np.float32(ROPE_BASE) ** (-np.arange(n_freq, dtype=np.float32) / np.float32(n_freq))
    tok = np.arange(n)
    pos_r = (tok // GRID_W).astype(np.float32)
    pos_c = (tok % GRID_W).astype(np.float32)
    ang_r = (pos_r[:, None] * inv[None, :]).astype(np.float32)
    ang_c = (pos_c[:, None] * inv[None, :]).astype(np.float32)
    cos = np.concatenate([np.cos(ang_r), np.cos(ang_r), np.cos(ang_c), np.cos(ang_c)], axis=-1)
    sin = np.concatenate([-np.sin(ang_r), np.sin(ang_r), -np.sin(ang_c), np.sin(ang_c)], axis=-1)
    cos = np.concatenate([np.ones((n_ctx, RET_DIM), np.float32), cos], axis=0)
    sin = np.concatenate([np.zeros((n_ctx, RET_DIM), np.float32), sin], axis=0)
    return jnp.asarray(cos, F32), jnp.asarray(sin, F32)


def _na_tables():
    q = np.arange(GRID_W)[:, None]
    k = np.arange(GRID_W)[None, :]
    start = np.clip(q - NA_KW // 2, 0, GRID_W - NA_KW)
    valid = (k >= start) & (k < start + NA_KW)
    dc = np.clip(k - q + (NA_KW - 1), 0, 2 * NA_KW - 2)
    ncls = 2 * NA_KW - 1
    onehot = (dc[None] == np.arange(ncls)[:, None, None]) & valid[None]
    oh2 = np.zeros((GRID_W, LANES, LANES), np.float32)
    for c in range(ncls):
        oh2[:, :GRID_W, c] = onehot[c]
        oh2[:, GRID_W:, 32 + c] = onehot[c]
    return onehot.astype(np.float32), valid, oh2.reshape(GRID_W * LANES, LANES)


def _paired_bias(rpb, onehot, valid):
    t = jnp.einsum("hdc,cqk->hdqk", rpb, jnp.asarray(onehot), precision=lax.Precision.HIGHEST)
    t = jnp.where(jnp.asarray(valid)[None, None], t, NEG_INF)
    return jnp.concatenate([t[:, :-1], t[:, 1:]], axis=-1)


def kernel(x, c, ctx, c_ctx, w_ada, b_ada, g_pre_mix, g_post_mix, g_pre_mlp, g_post_mlp, w_in, ret_decay, ret_gn, na_rpb, w_out, w_mlp1, w_mlp2, loss_target, m_c_ctx, m_w_ada, m_b_ada, m_g_pre_mix, m_g_post_mix, m_g_pre_mlp, m_g_post_mlp, m_w_in, m_ret_decay, m_ret_gn, m_na_rpb, m_w_out, m_w_mlp1, m_w_mlp2, v_c_ctx, v_w_ada, v_b_ada, v_g_pre_mix, v_g_post_mix, v_g_pre_mlp, v_g_post_mlp, v_w_in, v_ret_decay, v_ret_gn, v_na_rpb, v_w_out, v_w_mlp1, v_w_mlp2):
    B, N, D = x.shape
    C = ctx.shape[1]
    T = C + N

    silu_all, mods_g, win_b, wout_l, w1_l, w2_l = _mod_gather(c, c_ctx, w_ada[0], b_ada, w_in[0].T, w_out[0],
                                                             w_mlp1[0], w_mlp2[0])
    mods_mine = mods_g.transpose(1, 0, 2).reshape(mods_g.shape[1], N_MOD * D)
    modl = jnp.concatenate([mods_mine[:B], mods_mine[SUBLANES:SUBLANES + 1]], axis=0)
    modl = modl.reshape(B + 1, N_MOD, 1, D)
    rin = w_in.shape[2]
    rout, c1, r2 = wout_l.shape[0], w1_l.shape[1], w2_l.shape[0]

    def rows_of(n):
        return lambda ref: _row_block(ref, n)

    def cols_of(n):
        return lambda ref: _col_block(ref, n)

    cos, sin = _rope_tables(C, N)
    onehot, valid, oh2 = _na_tables()
    bias2 = _paired_bias(na_rpb[0], onehot, valid)
    lg = jax.nn.log_sigmoid(ret_decay[0].astype(F32))

    level_one = SIBLING + ICI_SAME_CORE
    h_all, proj, w1_part = _inproj_fwd(
        x, ctx, modl, g_pre_mix, win_b,
        [_Hosted("gather", level_one, [w1_l], [cols_of(c1)], [jax.ShapeDtypeStruct((D, N_DEV * c1), BF16)], True)])
    o_ret, lat_ret, wout_b = _ret_fwd(
        proj, cos, sin, lg, ret_gn, C,
        [_Hosted("gather", ALL_PEERS, [wout_l], [rows_of(rout)], [jax.ShapeDtypeStruct((N_DEV * rout, D), BF16)], True)])
    lat_na, w1_b, w2_part = _na_fwd(
        proj, bias2, C,
        [_HostedRelay([w1_part], [cols_of(c1)]),
         _Hosted("gather", level_one, [w2_l], [rows_of(r2)], [jax.ShapeDtypeStruct((N_DEV * r2, D), BF16)], True)])

    (dy1, dlat_ret, dlat_na, dmix, h2, act, du, dz, red_d) = _dense_core(
        lat_ret, lat_na, x, loss_target, modl, g_post_mix, g_pre_mlp, g_post_mlp, wout_b, w1_b, w2_part)

    gw_out_p = jnp.concatenate([_tn_matmul(lat_ret[:, None], dmix, "gw_out_ret"),
                                _tn_matmul(lat_na[:, None], dmix, "gw_out_na")], axis=0)
    gw1_p = _tn_matmul(h2[:, None], du, "gw_mlp1")
    gw2_p = _tn_matmul(act[:, None], dz, "gw_mlp2")

    dret, dgn_p, dlg_p, b1 = _ret_bwd(
        proj, cos, sin, lg, ret_gn, o_ret, dlat_ret, C,
        [_Hosted("scatter", ALL_PEERS, [gw1_p], [cols_of(c1)], [jax.ShapeDtypeStruct((N_DEV, D, c1), BF16)], True)])
    dna, dbias2, b2, bout = _na_bwd(
        proj, bias2, dlat_na, C,
        [_Hosted("scatter", ALL_PEERS, [gw2_p, gw_out_p], [rows_of(r2), rows_of(rout)],
                 [jax.ShapeDtypeStruct((N_DEV, r2, D), BF16), jax.ShapeDtypeStruct((N_DEV, rout, D), BF16)], True)])
    gwin_t_p = jnp.concatenate([_tn_matmul(dret, h_all, "gw_in_ret"), _tn_matmul(dna, h_all, "gw_in_na")], axis=0)
    grad_x, red_i, bin_ = _inproj_bwd(
        dret, dna, x, ctx, dy1, modl, g_pre_mix, win_b,
        [_Hosted("scatter", ALL_PEERS, [gwin_t_p], [rows_of(rin)], [jax.ShapeDtypeStruct((N_DEV, rin, D), BF16)], True)])

    g_w_in = _sum_slots(bin_, "sum_w_in").T
    fused = {"w_out": _sum_adamw(bout, w_out[0], m_w_out[0], v_w_out[0], "sum_adamw_w_out"),
             "w_mlp1": _sum_adamw(b1, w_mlp1[0], m_w_mlp1[0], v_w_mlp1[0], "sum_adamw_w_mlp1"),
             "w_mlp2": _sum_adamw(b2, w_mlp2[0], m_w_mlp2[0], v_w_mlp2[0], "sum_adamw_w_mlp2")}

    rd = red_d.sum(axis=1)[:, :, :]
    ri = red_i
    nct = ri.shape[1] * C // T
    ri_ctx = ri[:, :nct].sum(axis=(0, 1))
    ri_lat = ri[:, nct:].sum(axis=1)
    d_mods = jnp.concatenate([ri_lat[:, 0], ri_lat[:, 1], rd[:, 0], rd[:, 4], rd[:, 3], rd[:, 2]], axis=-1)
    d_cmods = jnp.concatenate([ri_ctx[0], ri_ctx[1], jnp.zeros(((N_MOD - 2) * D,), F32)])[None]
    dm_slot = jnp.concatenate([d_mods, d_cmods, jnp.zeros((SUBLANES - B - 1, N_MOD * D), F32)], axis=0)
    dg_pre_mix = ri_lat[:, 2].sum(axis=0) + ri_ctx[2]
    dg_post_mix = rd[:, 1].sum(axis=0)
    dg_pre_mlp = rd[:, 5].sum(axis=0)
    dg_post_mlp = rd[:, 6].sum(axis=0)
    loss_p = rd[:, 7, 0].sum()
    d_gn = dgn_p[:, 0].sum(axis=0)
    d_lg = dlg_p[:, :, :2, 0].sum(axis=0).T
    d_decay = d_lg * jax.nn.sigmoid(-ret_decay[0].astype(F32))
    rr = _rpb_reduce(dbias2, jnp.asarray(oh2, BF16)).reshape(NA_HEADS, 2 * NA_KH - 2, LANES)
    ncls = 2 * NA_KW - 1
    d_rpb = (jnp.pad(rr[:, :, :ncls], ((0, 0), (0, 1), (0, 0))) + jnp.pad(rr[:, :, 32:32 + ncls], ((0, 0), (1, 0), (0, 0))))
    d_rpb32 = jnp.pad(d_rpb, ((0, 0), (0, 0), (0, 32 - ncls)))
    pieces = [dg_pre_mix, dg_post_mix, dg_pre_mlp, dg_post_mlp, d_gn, d_rpb32.reshape(-1),
              jnp.pad(d_decay.reshape(-1), (0, LANES - d_decay.size)), jnp.full((LANES,), loss_p, F32)]
    vec = jnp.concatenate(pieces)
    pad = (-vec.shape[0]) % (SUBLANES * LANES)
    vec = jnp.pad(vec, (0, pad)).reshape(-1, LANES)
    tot, g_b_ada, g_w_ada, g_c_ctx = _small_ar(vec, dm_slot, silu_all, w_ada[0], c_ctx)
    flat = tot.reshape(-1)
    o0 = 0
    g_pre_mix_g = flat[o0:o0 + D]; o0 += D
    g_post_mix_g = flat[o0:o0 + D]; o0 += D
    g_pre_mlp_g = flat[o0:o0 + D]; o0 += D
    g_post_mlp_g = flat[o0:o0 + D]; o0 += D
    g_gn = flat[o0:o0 + RET_WIDTH]; o0 += RET_WIDTH
    nrpb = NA_HEADS * (2 * NA_KH - 1) * 32
    g_rpb = flat[o0:o0 + nrpb].reshape(NA_HEADS, 2 * NA_KH - 1, 32)[:, :, :ncls]; o0 += nrpb
    g_decay = flat[o0:o0 + 2 * RET_HEADS].reshape(2, RET_HEADS); o0 += LANES
    loss = flat[o0]

    grads = {
        "c_ctx": g_c_ctx.reshape(c_ctx.shape), "w_ada": g_w_ada[None], "b_ada": g_b_ada.reshape(b_ada.shape),
        "g_pre_mix": g_pre_mix_g[None], "g_post_mix": g_post_mix_g[None], "g_pre_mlp": g_pre_mlp_g[None],
        "g_post_mlp": g_post_mlp_g[None], "w_in": g_w_in[None], "ret_decay": g_decay[None], "ret_gn": g_gn[None],
        "na_rpb": g_rpb[None], "w_out": fused["w_out"][0][None], "w_mlp1": fused["w_mlp1"][0][None],
        "w_mlp2": fused["w_mlp2"][0][None],
    }
    weights = dict(c_ctx=c_ctx, w_ada=w_ada, b_ada=b_ada, g_pre_mix=g_pre_mix, g_post_mix=g_post_mix,
                   g_pre_mlp=g_pre_mlp, g_post_mlp=g_post_mlp, w_in=w_in, ret_decay=ret_decay, ret_gn=ret_gn,
                   na_rpb=na_rpb, w_out=w_out, w_mlp1=w_mlp1, w_mlp2=w_mlp2)
    m_in = dict(c_ctx=m_c_ctx, w_ada=m_w_ada, b_ada=m_b_ada, g_pre_mix=m_g_pre_mix, g_post_mix=m_g_post_mix,
                g_pre_mlp=m_g_pre_mlp, g_post_mlp=m_g_post_mlp, w_in=m_w_in, ret_decay=m_ret_decay,
                ret_gn=m_ret_gn, na_rpb=m_na_rpb, w_out=m_w_out, w_mlp1=m_w_mlp1, w_mlp2=m_w_mlp2)
    v_in = dict(c_ctx=v_c_ctx, w_ada=v_w_ada, b_ada=v_b_ada, g_pre_mix=v_g_pre_mix, g_post_mix=v_g_post_mix,
                g_pre_mlp=v_g_pre_mlp, g_post_mlp=v_g_post_mlp, w_in=v_w_in, ret_decay=v_ret_decay,
                ret_gn=v_ret_gn, na_rpb=v_na_rpb, w_out=v_w_out, w_mlp1=v_w_mlp1, w_mlp2=v_w_mlp2)
    names = list(weights)
    deltas, new_m, new_v = {}, {}, {}
    for n in names:
        shp = weights[n].shape
        if n in fused:
            deltas[n], new_m[n], new_v[n] = (a.reshape(shp) for a in fused[n][1:])
            continue
        two_d = (-1, shp[-1]) if len(shp) > 1 else (1, shp[0])
        d, nm, nv = _adamw(weights[n].reshape(two_d), grads[n].reshape(two_d), m_in[n].reshape(two_d),
                           v_in[n].reshape(two_d), "adamw_" + n)
        deltas[n], new_m[n], new_v[n] = d.reshape(shp), nm.reshape(shp), nv.reshape(shp)
    return (loss, grad_x, *[grads[n] for n in names], *[deltas[n] for n in names],
            *[new_m[n] for n in names], *[new_v[n] for n in names])
```

```python
import functools
import math

import numpy as np
import jax
import jax.numpy as jnp
from jax import lax
from jax.experimental import pallas as pl
from jax.experimental.pallas import tpu as pltpu

F32 = jnp.float32
BF16 = jnp.bfloat16
MESH = pl.DeviceIdType.MESH

N_DEV = 8
LANES = 128
SUBLANES = 8
VMEM_LIMIT = 60 * 1024 * 1024

GRID_W = 64
RET_HEADS = 4
RET_DIM = 128
RET_WIDTH = RET_HEADS * RET_DIM
NA_HEADS = 8
NA_DIM = 64
NA_WIDTH = NA_HEADS * NA_DIM
NA_PAIRS = NA_HEADS // 2
NA_KH = 8
NA_KW = 16
NA_GROUP = 8
SEG = 512
ROPE_BASE = 10000.0
NORM_EPS = 1e-6
NEG_INF = -1e30
N_MOD = 6

ADAM_LR = 0.001
ADAM_B1 = 0.9
ADAM_B2 = 0.999
ADAM_EPS = 1e-08
ADAM_WD = 0.01
ADAM_STEP = 10


def _dot(a, b):
    return lax.dot_general(a, b, (((1,), (0,)), ((), ())), preferred_element_type=F32)


def _dot_nt(a, b):
    return lax.dot_general(a, b, (((1,), (1,)), ((), ())), preferred_element_type=F32)


def _dot_tn(a, b):
    return lax.dot_general(a, b, (((0,), (0,)), ((), ())), preferred_element_type=F32)


def _sigmoid(x):
    return 1.0 / (1.0 + jnp.exp(-x))


def _div_tile(n, cap, mult):
    if n <= cap:
        return n
    for t in range(cap - cap % mult, 0, -mult):
        if n % t == 0:
            return t
    raise ValueError(f"no tile for {n}")


def _params(*sem):
    return pltpu.CompilerParams(dimension_semantics=tuple(sem) if sem else None,
                                vmem_limit_bytes=VMEM_LIMIT)


def _vmem():
    return pl.BlockSpec(memory_space=pltpu.VMEM)


def _any():
    return pl.BlockSpec(memory_space=pl.ANY)


def _me_and_peers():
    x, y, c = lax.axis_index("x"), lax.axis_index("y"), lax.axis_index("c")
    me = 4 * x + 2 * y + c
    peers = []
    for m in range(1, N_DEV):
        px = 1 - x if (m >> 2) & 1 else x
        py = 1 - y if (m >> 1) & 1 else y
        pc = 1 - c if m & 1 else c
        peers.append(((px, py, pc), 4 * px + 2 * py + pc))
    return me, peers


def _exchange(src_for, dst_from, send_sems, recv_sems):
    me, peers = _me_and_peers()
    sent = []
    for i, (dev, pid) in enumerate(peers):
        cp = pltpu.make_async_remote_copy(src_ref=src_for(pid), dst_ref=dst_from(me),
                                          send_sem=send_sems.at[i], recv_sem=recv_sems.at[i],
                                          device_id=dev, device_id_type=MESH)
        cp.start()
        sent.append(cp)
    for i, (dev, pid) in enumerate(peers):
        pltpu.make_async_remote_copy(src_ref=src_for(pid), dst_ref=dst_from(pid),
                                     send_sem=send_sems.at[i], recv_sem=recv_sems.at[i],
                                     device_id=dev, device_id_type=MESH).wait_recv()
    for cp in sent:
        cp.wait_send()


SIBLING = (1,)
ICI_SAME_CORE = (2, 4, 6)
ALL_PEERS = tuple(range(1, N_DEV))


def _remote(src, dst, send_sem, recv_sem, dev):
    return pltpu.make_async_remote_copy(src_ref=src, dst_ref=dst, send_sem=send_sem, recv_sem=recv_sem,
                                        device_id=dev, device_id_type=MESH)


def _push_start(items, masks, send_sems, recv_sems):
    me, peers = _me_and_peers()
    for k, (src_for, dst_from) in enumerate(items):
        for m in masks:
            dev, pid = peers[m - 1]
            _remote(src_for(pid), dst_from(me), send_sems.at[k, m - 1], recv_sems.at[k, m - 1], dev).start()


def _push_wait_recv(items, masks, send_sems, recv_sems):
    me, peers = _me_and_peers()
    for k, (src_for, dst_from) in enumerate(items):
        for m in masks:
            dev, pid = peers[m - 1]
            _remote(src_for(pid), dst_from(pid), send_sems.at[k, m - 1], recv_sems.at[k, m - 1], dev).wait_recv()


def _push_wait_send(items, masks, send_sems, recv_sems):
    me, peers = _me_and_peers()
    for k, (src_for, dst_from) in enumerate(items):
        for m in masks:
            dev, pid = peers[m - 1]
            _remote(src_for(pid), dst_from(me), send_sems.at[k, m - 1], recv_sems.at[k, m - 1], dev).wait_send()


def _forward_start(items, send_sems, recv_sems):
    me, peers = _me_and_peers()
    sib = peers[0][0]
    for k, (blk_in, blk_out) in enumerate(items):
        for j, m in enumerate(ICI_SAME_CORE):
            pid = peers[m - 1][1]
            _remote(blk_in(pid), blk_out(pid), send_sems.at[k, j], recv_sems.at[k, j], sib).start()


def _forward_wait(items, send_sems, recv_sems):
    me, peers = _me_and_peers()
    sib = peers[0][0]
    for k, (blk_in, blk_out) in enumerate(items):
        for j, m in enumerate(ICI_SAME_CORE):
            got = peers[(m | 1) - 1][1]
            _remote(blk_in(got), blk_out(got), send_sems.at[k, j], recv_sems.at[k, j], sib).wait_recv()
    for k, (blk_in, blk_out) in enumerate(items):
        for j, m in enumerate(ICI_SAME_CORE):
            pid = peers[m - 1][1]
            _remote(blk_in(pid), blk_out(pid), send_sems.at[k, j], recv_sems.at[k, j], sib).wait_send()


def _mod_gather(c, c_ctx, w_ada, b_ada, w_in_t, w_out, w1, w2):
    B, D = c.shape
    ncol = w_ada.shape[1]
    rows = SUBLANES * N_DEV + SUBLANES

    def body(c_ref, cc_ref, w_ref, b_ref, win_ref, wout_ref, w1_ref, w2_ref,
             s_ref, m_ref, gin_ref, wout_b, w1_b, w2_b,
             win_b, msend, send1, recv1, send2, recv2, wsend, wrecv, fsend, frecv, lsem):
        me, _ = _me_and_peers()
        win_b[...] = win_ref[...].astype(BF16)
        block = _row_block(gin_ref, w_in_t.shape[0])
        gather = [(lambda p: win_b, block)]
        own = pltpu.make_async_copy(win_b, block(me), lsem.at[0])
        own.start()
        _push_start(gather, SIBLING + ICI_SAME_CORE, wsend, wrecv)
        wout_b[...] = wout_ref[...].astype(BF16)
        w1_b[...] = w1_ref[...].astype(BF16)
        w2_b[...] = w2_ref[...].astype(BF16)
        cv = c_ref[...]
        slot = jnp.concatenate([cv * _sigmoid(cv), jnp.zeros((SUBLANES - B, D), F32)], axis=0)
        my_rows = pl.ds(pl.multiple_of(me * SUBLANES, SUBLANES), SUBLANES)
        s_ref[my_rows, :] = slot
        ccv = cc_ref[...]
        s_ref[SUBLANES * N_DEV:, :] = jnp.concatenate(
            [ccv * _sigmoid(ccv), jnp.zeros((SUBLANES - 1, D), F32)], axis=0)

        def rows_of(p):
            return s_ref.at[pl.ds(pl.multiple_of(p * SUBLANES, SUBLANES), SUBLANES), :]

        _exchange(lambda p: rows_of(me), rows_of, send1, recv1)
        b_loc = b_ref[:, pl.ds(pl.multiple_of(me * ncol, ncol), ncol)]
        mods = _dot(s_ref[...], w_ref[...]) + b_loc
        for p in range(N_DEV):
            msend[p] = jnp.concatenate([mods[p * SUBLANES:(p + 1) * SUBLANES], mods[N_DEV * SUBLANES:]], axis=0)
        m_ref[me] = msend[me]
        _exchange(lambda p: msend.at[p], lambda p: m_ref.at[p], send2, recv2)
        _push_wait_recv(gather, ICI_SAME_CORE, wsend, wrecv)
        relay = [(block, block)]
        _forward_start(relay, fsend, frecv)
        _push_wait_recv(gather, SIBLING, wsend, wrecv)
        _forward_wait(relay, fsend, frecv)
        _push_wait_send(gather, SIBLING + ICI_SAME_CORE, wsend, wrecv)
        own.wait()

    return pl.pallas_call(
        body, name="mod_gather",
        out_shape=(jax.ShapeDtypeStruct((rows, D), F32), jax.ShapeDtypeStruct((N_DEV, 2 * SUBLANES, ncol), F32),
                   jax.ShapeDtypeStruct((N_DEV * w_in_t.shape[0], D), BF16),
                   jax.ShapeDtypeStruct(w_out.shape, BF16), jax.ShapeDtypeStruct(w1.shape, BF16),
                   jax.ShapeDtypeStruct(w2.shape, BF16)),
        in_specs=[_vmem()] * 8, out_specs=(_vmem(), _vmem(), _any(), _vmem(), _vmem(), _vmem()),
        scratch_shapes=[pltpu.VMEM(w_in_t.shape, BF16), pltpu.VMEM((N_DEV, 2 * SUBLANES, ncol), F32)]
                       + [pltpu.SemaphoreType.DMA((N_DEV - 1,))] * 4
                       + [pltpu.SemaphoreType.DMA((1, N_DEV - 1))] * 2 + [pltpu.SemaphoreType.DMA((1, 3))] * 2
                       + [pltpu.SemaphoreType.DMA((1,))],
        compiler_params=pltpu.CompilerParams(vmem_limit_bytes=VMEM_LIMIT),
    )(c, c_ctx.reshape(1, D), w_ada, b_ada, w_in_t, w_out, w1, w2)


def _row_block(ref, rows):
    return lambda p: ref.at[pl.ds(pl.multiple_of(p * rows, 2 * SUBLANES), rows), :]


def _col_block(ref, cols):
    return lambda p: ref.at[:, pl.ds(pl.multiple_of(p * cols, LANES), cols)]


def _slot(ref):
    return lambda p: ref.at[p]


class _Hosted:
    def __init__(self, kind, masks, operands, block_of, out_shapes, with_own):
        self.kind, self.masks, self.operands = kind, masks, list(operands)
        self.block_of, self.out_shapes, self.with_own = block_of, list(out_shapes), with_own
        self.n = len(self.operands)

    def scratch(self):
        return [pltpu.SemaphoreType.DMA((self.n, N_DEV - 1)), pltpu.SemaphoreType.DMA((self.n, N_DEV - 1)),
                pltpu.SemaphoreType.DMA((self.n,))]

    def _items(self, in_refs, out_refs):
        items = []
        for k in range(self.n):
            if self.kind == "gather":
                items.append((lambda p, k=k: in_refs[k], self.block_of[k](out_refs[k])))
            else:
                items.append((self.block_of[k](in_refs[k]), _slot(out_refs[k])))
        return items

    def _own(self, in_refs, out_refs, lsem):
        me, _ = _me_and_peers()
        items = self._items(in_refs, out_refs)
        return [pltpu.make_async_copy(src_for(me), dst_from(me), lsem.at[k])
                for k, (src_for, dst_from) in enumerate(items)]

    def start(self, in_refs, out_refs, sems):
        send, recv, lsem = sems
        if self.with_own:
            for cp in self._own(in_refs, out_refs, lsem):
                cp.start()
        _push_start(self._items(in_refs, out_refs), self.masks, send, recv)

    def wait(self, in_refs, out_refs, sems):
        send, recv, lsem = sems
        items = self._items(in_refs, out_refs)
        _push_wait_recv(items, self.masks, send, recv)
        _push_wait_send(items, self.masks, send, recv)
        if self.with_own:
            for cp in self._own(in_refs, out_refs, lsem):
                cp.wait()


class _HostedRelay:
    def __init__(self, arrays, block_of):
        self.operands, self.block_of = list(arrays), block_of
        self.out_shapes = [jax.ShapeDtypeStruct(a.shape, a.dtype) for a in arrays]
        self.n = len(self.operands)

    def scratch(self):
        return [pltpu.SemaphoreType.DMA((self.n, 3)), pltpu.SemaphoreType.DMA((self.n, 3))]

    def _items(self, in_refs, out_refs):
        return [(self.block_of[k](in_refs[k]), self.block_of[k](out_refs[k])) for k in range(self.n)]

    def start(self, in_refs, out_refs, sems):
        _forward_start(self._items(in_refs, out_refs), *sems)

    def wait(self, in_refs, out_refs, sems):
        _forward_wait(self._items(in_refs, out_refs), *sems)


def _call_hosting(body, hosted, *, name, grid, out_shape, in_specs, out_specs, scratch_shapes, args):
    n_in, n_out, n_scr = len(in_specs), len(out_shape), len(scratch_shapes)
    hn = sum(hs.n for hs in hosted)
    n_sem = [len(hs.scratch()) for hs in hosted]

    def wrapped(*refs):
        ins = refs[:n_in]
        h_in = refs[n_in:n_in + hn]
        outs = refs[n_in + hn:n_in + hn + n_out]
        h_out = refs[n_in + hn + n_out:n_in + 2 * hn + n_out]
        scr = refs[n_in + 2 * hn + n_out:n_in + 2 * hn + n_out + n_scr]
        sems = refs[n_in + 2 * hn + n_out + n_scr:]
        ids = [pl.program_id(i) for i in range(len(grid))]
        first = functools.reduce(jnp.logical_and, [i == 0 for i in ids])
        last = functools.reduce(jnp.logical_and, [i == g - 1 for i, g in zip(ids, grid)])
        parts, o0, s0 = [], 0, 0
        for hs, ns in zip(hosted, n_sem):
            parts.append((hs, h_in[o0:o0 + hs.n], h_out[o0:o0 + hs.n], sems[s0:s0 + ns]))
            o0 += hs.n
            s0 += ns

        @pl.when(first)
        def _():
            for hs, hi, ho, se in parts:
                hs.start(hi, ho, se)

        body(*ins, *outs, *scr)

        @pl.when(last)
        def _():
            for hs, hi, ho, se in parts:
                hs.wait(hi, ho, se)

    aliases, o0 = {}, 0
    for hs in hosted:
        if isinstance(hs, _HostedRelay):
            aliases.update({n_in + o0 + k: n_out + o0 + k for k in range(hs.n)})
        o0 += hs.n
    return pl.pallas_call(
        wrapped, name=name, grid=grid,
        out_shape=tuple(out_shape) + tuple(s for hs in hosted for s in hs.out_shapes),
        in_specs=list(in_specs) + [_any()] * hn,
        out_specs=tuple(out_specs) + (_any(),) * hn,
        scratch_shapes=list(scratch_shapes) + [s for hs in hosted for s in hs.scratch()],
        input_output_aliases=aliases,
        compiler_params=_params(*(("arbitrary",) * len(grid))),
    )(*args, *[a for hs in hosted for a in hs.operands])


def _token_tiles(n_ctx, tm):
    nct = n_ctx // tm

    def ctx_spec(D):
        return pl.BlockSpec((None, tm, D), lambda b, t: (b, jnp.minimum(t, nct - 1), 0))

    def lat_spec(D):
        return pl.BlockSpec((None, tm, D), lambda b, t: (b, jnp.maximum(t - nct, 0), 0))

    return nct, ctx_spec, lat_spec


def _inproj_fwd(x, ctx, modl, g1, w_in_t, hosted):
    B, N, D = x.shape
    n_ctx = ctx.shape[1]
    T = n_ctx + N
    nw = w_in_t.shape[0]
    tm = _div_tile(n_ctx, 256, 16)
    nct, ctx_spec, lat_spec = _token_tiles(n_ctx, tm)

    def body(c_ref, x_ref, sh_ref, sc_ref, g_ref, w_ref, h_ref, p_ref):
        x = jnp.where(pl.program_id(1) < nct, c_ref[...], x_ref[...])
        r = lax.rsqrt(jnp.mean(x * x, axis=-1, keepdims=True) + NORM_EPS)
        h = ((x * r) * g_ref[...]) * (1.0 + sc_ref[...]) + sh_ref[...]
        hb = h.astype(BF16)
        h_ref[...] = hb
        p_ref[...] = _dot_nt(hb, w_ref[...])

    def mrow(b, t):
        return jnp.where(t < nct, B, b)

    return _call_hosting(
        body, hosted, name="inproj_fwd", grid=(B, T // tm),
        out_shape=(jax.ShapeDtypeStruct((B, T, D), BF16), jax.ShapeDtypeStruct((B, T, nw), F32)),
        in_specs=[ctx_spec(D), lat_spec(D),
                  pl.BlockSpec((None, None, 1, D), lambda b, t: (mrow(b, t), 0, 0, 0)),
                  pl.BlockSpec((None, None, 1, D), lambda b, t: (mrow(b, t), 1, 0, 0)),
                  pl.BlockSpec((1, D), lambda b, t: (0, 0)),
                  pl.BlockSpec((nw, D), lambda b, t: (0, 0))],
        out_specs=(pl.BlockSpec((None, tm, D), lambda b, t: (b, t, 0)),
                   pl.BlockSpec((None, tm, nw), lambda b, t: (b, t, 0))),
        scratch_shapes=[], args=(ctx, x, modl, modl, g1, w_in_t))


def _swap32(x):
    lane = lax.broadcasted_iota(jnp.int32, x.shape, 1)
    return jnp.where((lane % 64) < 32, pltpu.roll(x, 96, 1), pltpu.roll(x, 32, 1))


def _rope(x, cos, sin):
    return x * cos + _swap32(x) * sin


def _unrope(dy, cos, sin):
    return dy * cos + _swap32(dy * sin)


def _ret_weights(lgf, lgb, dist):
    return jnp.exp(jnp.where(dist >= 0.0, lgf * dist, -lgb * dist))


class _RetDecay:
    def __init__(self, lgf, lgb, rows):
        r = lax.broadcasted_iota(jnp.int32, (rows, RET_DIM), 0).astype(F32)
        self.head = r + 1.0
        self.tail = (rows - 1.0) - r
        self.q_f = jnp.exp(lgf * self.head)
        self.k_f = jnp.exp(lgf * self.tail)
        self.q_b = jnp.exp(lgb * self.tail)
        self.k_b = jnp.exp(lgb * self.head)


def _ret_states(kf32, vs, lgf, lgb, C, c, nt, hf, hb, hfa=None, hba=None):
    dec = _RetDecay(lgf, lgb, c)
    dec_c = _RetDecay(lgf, lgb, C)
    step_f = jnp.exp(jnp.zeros((RET_DIM, RET_DIM), F32) + lgf * c)
    step_b = jnp.exp(jnp.zeros((RET_DIM, RET_DIM), F32) + lgb * c)

    def upd(rows, kdec):
        return _dot_tn((kf32[rows, :] * kdec).astype(BF16), vs[rows, :])

    def lat(t):
        return slice(C + t * c, C + (t + 1) * c)

    state = upd(slice(0, C), dec_c.k_f)
    aged = jnp.zeros_like(state)
    for t in range(nt):
        hf[t] = state.astype(BF16)
        if hfa is not None:
            hfa[t] = aged
        if t < nt - 1:
            aged = step_f * (aged + c * state)
            state = step_f * state + upd(lat(t), dec.k_f)
    state = upd(slice(0, C), dec_c.k_b)
    aged = jnp.zeros_like(state)
    for t in range(nt - 1, -1, -1):
        hb[t] = state.astype(BF16)
        if hba is not None:
            hba[t] = aged
        if t > 0:
            aged = step_b * (aged + c * state)
            state = step_b * state + upd(lat(t), dec.k_b)
    return dec, dec_c, step_f, step_b


def _ret_fwd(proj, cos, sin, lg, gn, n_ctx, hosted):
    B, T, _ = proj.shape
    C = n_ctx
    N = T - C
    c = _div_tile(N, 256, 16)
    nt = N // c
    scale = RET_DIM ** -0.5

    def body(lg_ref, q_ref, k_ref, v_ref, g_ref, cos_ref, sin_ref, gn_ref, o_ref, lat_ref, qs, ks, vs, kf32, hf, hb):
        h = pl.program_id(1)
        lgf = lg_ref[0, h]
        lgb = lg_ref[1, h]
        for rows in [slice(0, C)] + [slice(C + t * c, C + (t + 1) * c) for t in range(nt)]:
            cosb = cos_ref[rows, :]
            sinb = sin_ref[rows, :]
            qs[rows, :] = (_rope(q_ref[rows, :], cosb, sinb) * scale).astype(BF16)
            kr = _rope(k_ref[rows, :], cosb, sinb)
            kf32[rows, :] = kr
            ks[rows, :] = kr.astype(BF16)
            vs[rows, :] = v_ref[rows, :].astype(BF16)
        gnv = gn_ref[...]
        dec, _, _, _ = _ret_states(kf32, vs, lgf, lgb, C, c, nt, hf, hb)
        rc = (lax.broadcasted_iota(jnp.int32, (c, c), 0) - lax.broadcasted_iota(jnp.int32, (c, c), 1)).astype(F32)
        w_diag = _ret_weights(lgf, lgb, rc)
        for t in range(nt):
            rows = slice(C + t * c, C + (t + 1) * c)
            qt = qs[rows, :]
            s = _dot_nt(qt, ks[rows, :])
            o = (_dot((s * w_diag).astype(BF16), vs[rows, :])
                 + dec.q_f * _dot(qt, hf[t]) + dec.q_b * _dot(qt, hb[t]))
            o_ref[t * c:(t + 1) * c, :] = o
            mu = jnp.mean(o, axis=-1, keepdims=True)
            oc = o - mu
            var = jnp.mean(oc * oc, axis=-1, keepdims=True)
            yh = oc * lax.rsqrt(var + NORM_EPS)
            g = g_ref[rows, :]
            lat_ref[t * c:(t + 1) * c, :] = ((yh * gnv) * (g * _sigmoid(g))).astype(BF16)

    def col(seg):
        return pl.BlockSpec((None, T, RET_DIM), lambda b, h, seg=seg: (b, 0, seg * RET_HEADS + h))

    return _call_hosting(
        body, hosted, name="ret_fwd", grid=(B, RET_HEADS),
        out_shape=(jax.ShapeDtypeStruct((B, N, RET_WIDTH), F32), jax.ShapeDtypeStruct((B, N, RET_WIDTH), BF16)),
        in_specs=[pl.BlockSpec(memory_space=pltpu.SMEM), col(0), col(1), col(2), col(3),
                  pl.BlockSpec((T, RET_DIM), lambda b, h: (0, 0)), pl.BlockSpec((T, RET_DIM), lambda b, h: (0, 0)),
                  pl.BlockSpec((1, RET_DIM), lambda b, h: (0, h))],
        out_specs=(pl.BlockSpec((None, N, RET_DIM), lambda b, h: (b, 0, h)),
                   pl.BlockSpec((None, N, RET_DIM), lambda b, h: (b, 0, h))),
        scratch_shapes=[pltpu.VMEM((T, RET_DIM), BF16)] * 3 + [pltpu.VMEM((T, RET_DIM), F32)]
                       + [pltpu.VMEM((nt, RET_DIM, RET_DIM), BF16)] * 2,
        args=(lg, proj, proj, proj, proj, cos, sin, gn))


def _ret_bwd(proj, cos, sin, lg, gn, o, dlat, n_ctx, hosted):
    B, T, _ = proj.shape
    C = n_ctx
    N = T - C
    c = _div_tile(N, 256, 16)
    nt = N // c
    scale = RET_DIM ** -0.5

    def lat(t):
        return slice(C + t * c, C + (t + 1) * c)

    def body(lg_ref, q_ref, k_ref, v_ref, g_ref, cos_ref, sin_ref, gn_ref, o_ref, dl_ref,
             d_ref, dgn_ref, dlg_ref, qs, ks, vs, dos, qf32, kf32, hf, hb, hfa, hba, gf_s, gb_s):
        h = pl.program_id(1)
        lgf = lg_ref[0, h]
        lgb = lg_ref[1, h]
        gnv = gn_ref[...]

        def fold(a):
            return jnp.sum(a.reshape(a.shape[0] // SUBLANES, SUBLANES, a.shape[1]), axis=0)

        for rows in [slice(0, C)] + [lat(t) for t in range(nt)]:
            cosb = cos_ref[rows, :]
            sinb = sin_ref[rows, :]
            qr = _rope(q_ref[rows, :], cosb, sinb) * scale
            qf32[rows, :] = qr
            qs[rows, :] = qr.astype(BF16)
            kr = _rope(k_ref[rows, :], cosb, sinb)
            kf32[rows, :] = kr
            ks[rows, :] = kr.astype(BF16)
            vs[rows, :] = v_ref[rows, :].astype(BF16)

        dgn = jnp.zeros((1, RET_DIM), F32)
        for t in range(nt):
            lrows = slice(t * c, (t + 1) * c)
            ov = o_ref[lrows, :]
            mu = jnp.mean(ov, axis=-1, keepdims=True)
            oc = ov - mu
            var = jnp.mean(oc * oc, axis=-1, keepdims=True)
            rstd = lax.rsqrt(var + NORM_EPS)
            yh = oc * rstd
            g = g_ref[lat(t), :]
            sg = _sigmoid(g)
            dl = dl_ref[lrows, :]
            d_ref[3, lat(t), :] = (dl * (yh * gnv) * (sg * (1.0 + g * (1.0 - sg)))).astype(BF16)
            dls = dl * (g * sg)
            dgn = dgn + jnp.sum(dls * yh, axis=0, keepdims=True)
            dyh = dls * gnv
            do = rstd * (dyh - jnp.mean(dyh, axis=-1, keepdims=True)
                         - yh * jnp.mean(dyh * yh, axis=-1, keepdims=True))
            dos[lrows, :] = do.astype(BF16)
        dgn_ref[...] = jnp.concatenate([dgn, jnp.zeros((SUBLANES - 1, RET_DIM), F32)], axis=0)
        d_ref[3, 0:C, :] = jnp.zeros((C, RET_DIM), BF16)
        d_ref[0, 0:C, :] = jnp.zeros((C, RET_DIM), BF16)

        dec, dec_c, step_f, step_b = _ret_states(kf32, vs, lgf, lgb, C, c, nt, hf, hb, hfa, hba)

        def zmat(t, qdec):
            return _dot_tn((qf32[lat(t), :] * qdec).astype(BF16), dos[t * c:(t + 1) * c, :])

        acc3f = jnp.zeros((RET_DIM, RET_DIM), F32)
        acc3b = jnp.zeros((RET_DIM, RET_DIM), F32)
        state = jnp.zeros((RET_DIM, RET_DIM), F32)
        for t in range(nt - 1, -1, -1):
            gf_s[t] = state.astype(BF16)
            z = zmat(t, dec.q_f)
            acc3f = acc3f + hfa[t] * z
            state = step_f * state + z
        gctx_f = state.astype(BF16)
        state = jnp.zeros((RET_DIM, RET_DIM), F32)
        for t in range(nt):
            gb_s[t] = state.astype(BF16)
            z = zmat(t, dec.q_b)
            acc3b = acc3b + hba[t] * z
            state = step_b * state + z
        gctx_b = state.astype(BF16)

        rc = (lax.broadcasted_iota(jnp.int32, (c, c), 0) - lax.broadcasted_iota(jnp.int32, (c, c), 1)).astype(F32)
        w_diag = _ret_weights(lgf, lgb, rc)
        wg_f = jnp.where(rc >= 0.0, w_diag * rc, 0.0)
        wg_b = jnp.where(rc < 0.0, -w_diag * rc, 0.0)
        accf = jnp.zeros((SUBLANES, RET_DIM), F32)
        accb = jnp.zeros((SUBLANES, RET_DIM), F32)
        gdf = jnp.zeros((SUBLANES, c), F32)
        gdb = jnp.zeros((SUBLANES, c), F32)
        for t in range(nt):
            rows = lat(t)
            qt = qs[rows, :]
            kt = ks[rows, :]
            vt = vs[rows, :]
            dot = dos[t * c:(t + 1) * c, :]
            s = _dot_nt(qt, kt)
            dp = _dot_nt(dot, vt)
            dv = _dot_tn((s * w_diag).astype(BF16), dot)
            ds = (dp * w_diag).astype(BF16)
            dq = _dot(ds, kt)
            dk = _dot_tn(ds, qt)
            gs = dp * s
            gdf = gdf + fold(gs * wg_f)
            gdb = gdb + fold(gs * wg_b)
            qv = qf32[rows, :]
            kv = kf32[rows, :]
            dq_f = dec.q_f * _dot_nt(dot, hf[t])
            dq_b = dec.q_b * _dot_nt(dot, hb[t])
            dk_f = dec.k_f * _dot_nt(vt, gf_s[t])
            dk_b = dec.k_b * _dot_nt(vt, gb_s[t])
            accf = accf + fold(dec.head * dq_f * qv) + fold(dec.tail * dk_f * kv)
            accb = accb + fold(dec.tail * dq_b * qv) + fold(dec.head * dk_b * kv)
            dv = dv + dec.k_f * _dot(kt, gf_s[t]) + dec.k_b * _dot(kt, gb_s[t])
            cosb = cos_ref[rows, :]
            sinb = sin_ref[rows, :]
            d_ref[0, rows, :] = _unrope((dq + dq_f + dq_b) * scale, cosb, sinb).astype(BF16)
            d_ref[1, rows, :] = _unrope(dk + dk_f + dk_b, cosb, sinb).astype(BF16)
            d_ref[2, rows, :] = dv.astype(BF16)
        kc = ks[0:C, :]
        vc = vs[0:C, :]
        kcv = kf32[0:C, :]
        dkc_f = dec_c.k_f * _dot_nt(vc, gctx_f)
        dkc_b = dec_c.k_b * _dot_nt(vc, gctx_b)
        accf = accf + fold(dec_c.tail * dkc_f * kcv)
        accb = accb + fold(dec_c.head * dkc_b * kcv)
        d_ref[1, 0:C, :] = (dkc_f + dkc_b).astype(BF16)
        d_ref[2, 0:C, :] = (dec_c.k_f * _dot(kc, gctx_f) + dec_c.k_b * _dot(kc, gctx_b)).astype(BF16)
        gf = jnp.sum(gdf) + jnp.sum(accf) + jnp.sum(acc3f)
        gb = jnp.sum(gdb) + jnp.sum(accb) + jnp.sum(acc3b)
        row = lax.broadcasted_iota(jnp.int32, (SUBLANES, LANES), 0)
        dlg_ref[...] = jnp.where(row == 0, gf, jnp.where(row == 1, gb, 0.0))

    def col(seg):
        return pl.BlockSpec((None, T, RET_DIM), lambda b, h, seg=seg: (b, 0, seg * RET_HEADS + h))

    return _call_hosting(
        body, hosted, name="ret_bwd", grid=(B, RET_HEADS),
        out_shape=(jax.ShapeDtypeStruct((B, 4, T, RET_WIDTH), BF16),
                   jax.ShapeDtypeStruct((B, SUBLANES, RET_WIDTH), F32),
                   jax.ShapeDtypeStruct((B, RET_HEADS, SUBLANES, LANES), F32)),
        in_specs=[pl.BlockSpec(memory_space=pltpu.SMEM), col(0), col(1), col(2), col(3),
                  pl.BlockSpec((T, RET_DIM), lambda b, h: (0, 0)), pl.BlockSpec((T, RET_DIM), lambda b, h: (0, 0)),
                  pl.BlockSpec((1, RET_DIM), lambda b, h: (0, h)),
                  pl.BlockSpec((None, N, RET_DIM), lambda b, h: (b, 0, h)),
                  pl.BlockSpec((None, N, RET_DIM), lambda b, h: (b, 0, h))],
        out_specs=(pl.BlockSpec((None, 4, T, RET_DIM), lambda b, h: (b, 0, 0, h)),
                   pl.BlockSpec((None, SUBLANES, RET_DIM), lambda b, h: (b, 0, h)),
                   pl.BlockSpec((None, None, SUBLANES, LANES), lambda b, h: (b, h, 0, 0))),
        scratch_shapes=[pltpu.VMEM((T, RET_DIM), BF16)] * 3 + [pltpu.VMEM((N, RET_DIM), BF16)]
                       + [pltpu.VMEM((T, RET_DIM), F32)] * 2
                       + [pltpu.VMEM((nt, RET_DIM, RET_DIM), BF16)] * 2 + [pltpu.VMEM((nt, RET_DIM, RET_DIM), F32)] * 2
                       + [pltpu.VMEM((nt, RET_DIM, RET_DIM), BF16)] * 2,
        args=(lg, proj, proj, proj, proj, cos, sin, gn, o, dlat))


def _na_geometry(rows):
    kh = min(NA_KH, rows)
    return kh, kh * GRID_W


def _pair_select():
    lane = lax.broadcasted_iota(jnp.int32, (2 * GRID_W, LANES), 1)
    row = lax.broadcasted_iota(jnp.int32, (2 * GRID_W, LANES), 0)
    return (lane >= NA_DIM) == (row >= GRID_W)


def _pair_bias(bias_ref, dr0, kh):
    return jnp.concatenate(
        [jnp.concatenate([bias_ref[e, pl.ds(dr0 + 2 * m, 1)].reshape(GRID_W, LANES) for m in range(kh // 2)], axis=1)
         for e in range(2)], axis=0)


def _na_softmax(s_loc, s_ctx):
    mx = jnp.maximum(jnp.max(s_loc, axis=-1, keepdims=True), jnp.max(s_ctx, axis=-1, keepdims=True))
    p_loc = jnp.exp(s_loc - mx)
    p_ctx = jnp.exp(s_ctx - mx)
    den = jnp.sum(p_loc, axis=-1, keepdims=True) + jnp.sum(p_ctx, axis=-1, keepdims=True)
    return p_loc, p_ctx, den


def _na_fwd(proj, bias2, n_ctx, hosted):
    B, T, _ = proj.shape
    C = n_ctx
    N = T - C
    R = N // GRID_W
    kh, nk = _na_geometry(R)
    scale = NA_DIM ** -0.5
    base = (4 * RET_WIDTH) // LANES

    def body(q_ref, k_ref, v_ref, bias_ref, out_ref, kb16, vb16):
        kb16[...] = k_ref[...].astype(BF16)
        vb16[...] = v_ref[...].astype(BF16)
        kc = kb16[0:C, :]
        vc = vb16[0:C, :]
        lane = lax.broadcasted_iota(jnp.int32, (GRID_W, LANES), 1)
        sel2 = _pair_select()

        def group(gi, carry):
            pre = []
            for u in range(NA_GROUP):
                r = gi * NA_GROUP + u
                bs = jnp.clip(r - kh // 2, 0, R - kh)
                dr0 = bs - r + (NA_KH - 1)
                q = q_ref[pl.ds(pl.multiple_of(C + r * GRID_W, GRID_W), GRID_W), :] * scale
                q2 = jnp.where(sel2, jnp.concatenate([q, q], axis=0), 0.0).astype(BF16)
                band = pl.ds(pl.multiple_of(C + bs * GRID_W, GRID_W), nk)
                s_loc = _dot_nt(q2, kb16[band, :]) + _pair_bias(bias_ref, dr0, kh)
                s_ctx = _dot_nt(q2, kc)
                pre.append((r, band, s_loc, s_ctx))
            mid = [(r, band) + _na_softmax(s_loc, s_ctx) for r, band, s_loc, s_ctx in pre]
            for r, band, p_loc, p_ctx, den in mid:
                o2 = (_dot(p_loc.astype(BF16), vb16[band, :]) + _dot(p_ctx.astype(BF16), vc)) / den
                out_ref[pl.ds(pl.multiple_of(r * GRID_W, GRID_W), GRID_W), :] = jnp.where(
                    lane < NA_DIM, o2[:GRID_W], o2[GRID_W:]).astype(BF16)
            return carry

        lax.fori_loop(0, R // NA_GROUP, group, 0)

    def col(seg):
        return pl.BlockSpec((None, T, LANES), lambda b, p, seg=seg: (b, 0, base + seg * NA_PAIRS + p))

    return _call_hosting(
        body, hosted, name="na_fwd", grid=(B, NA_PAIRS),
        out_shape=(jax.ShapeDtypeStruct((B, N, NA_WIDTH), BF16),),
        in_specs=[col(0), col(1), col(2),
                  pl.BlockSpec((2, 2 * NA_KH - 2, GRID_W, LANES), lambda b, p: (p, 0, 0, 0))],
        out_specs=(pl.BlockSpec((None, N, LANES), lambda b, p: (b, 0, p)),),
        scratch_shapes=[pltpu.VMEM((T, LANES), BF16)] * 2,
        args=(proj, proj, proj, bias2))


def _na_bwd(proj, bias2, dlat, n_ctx, hosted):
    B, T, _ = proj.shape
    C = n_ctx
    N = T - C
    R = N // GRID_W
    kh, nk = _na_geometry(R)
    scale = NA_DIM ** -0.5
    base = (4 * RET_WIDTH) // LANES

    def body(q_ref, k_ref, v_ref, bias_ref, dl_ref, d_ref, db_ref, kb16, vb16, dkv):
        b = pl.program_id(1)
        kb16[...] = k_ref[...].astype(BF16)
        vb16[...] = v_ref[...].astype(BF16)
        kc = kb16[0:C, :]
        vc = vb16[0:C, :]
        lane = lax.broadcasted_iota(jnp.int32, (GRID_W, LANES), 1)
        dkv[...] = jnp.zeros(dkv.shape, F32)
        d_ref[0, 0:C, :] = jnp.zeros((C, LANES), BF16)

        @pl.when(b == 0)
        def _():
            db_ref[...] = jnp.zeros(db_ref.shape, F32)

        sel2 = _pair_select()

        def group(gi, carry):
            pre = []
            for u in range(NA_GROUP):
                r = gi * NA_GROUP + u
                bs = jnp.clip(r - kh // 2, 0, R - kh)
                dr0 = bs - r + (NA_KH - 1)
                q = q_ref[pl.ds(pl.multiple_of(C + r * GRID_W, GRID_W), GRID_W), :] * scale
                do = dl_ref[pl.ds(pl.multiple_of(r * GRID_W, GRID_W), GRID_W), :]
                q2 = jnp.where(sel2, jnp.concatenate([q, q], axis=0), 0.0).astype(BF16)
                do2 = jnp.where(sel2, jnp.concatenate([do, do], axis=0), 0.0).astype(BF16)
                band = pl.ds(pl.multiple_of(C + bs * GRID_W, GRID_W), nk)
                s_loc = _dot_nt(q2, kb16[band, :]) + _pair_bias(bias_ref, dr0, kh)
                s_ctx = _dot_nt(q2, kc)
                dp_loc = _dot_nt(do2, vb16[band, :])
                dp_ctx = _dot_nt(do2, vc)
                pre.append((r, dr0, band, q2, do2, s_loc, s_ctx, dp_loc, dp_ctx))
            mid = []
            for r, dr0, band, q2, do2, s_loc, s_ctx, dp_loc, dp_ctx in pre:
                p_loc, p_ctx, den = _na_softmax(s_loc, s_ctx)
                inv = 1.0 / den
                p_loc = p_loc * inv
                p_ctx = p_ctx * inv
                delta = (jnp.sum(p_loc * dp_loc, axis=-1, keepdims=True)
                         + jnp.sum(p_ctx * dp_ctx, axis=-1, keepdims=True))
                ds_loc = p_loc * (dp_loc - delta)
                ds_ctx = p_ctx * (dp_ctx - delta)
                mid.append((r, dr0, band, q2, do2, p_loc.astype(BF16), p_ctx.astype(BF16), ds_loc, ds_ctx))
            for r, dr0, band, q2, do2, pb_loc, pb_ctx, ds_loc, ds_ctx in mid:
                dsb_loc = ds_loc.astype(BF16)
                dsb_ctx = ds_ctx.astype(BF16)
                dq2 = _dot(dsb_loc, kb16[band, :]) + _dot(dsb_ctx, kc)
                d_ref[0, pl.ds(pl.multiple_of(C + r * GRID_W, GRID_W), GRID_W), :] = (jnp.where(
                    lane < NA_DIM, dq2[:GRID_W], dq2[GRID_W:]) * scale).astype(BF16)
                dkv[0, band, :] += _dot_tn(dsb_loc, q2)
                dkv[1, band, :] += _dot_tn(pb_loc, do2)
                dkv[0, 0:C, :] += _dot_tn(dsb_ctx, q2)
                dkv[1, 0:C, :] += _dot_tn(pb_ctx, do2)
                for e in range(2):
                    for m in range(kh // 2):
                        db_ref[e, pl.ds(dr0 + 2 * m, 1)] += ds_loc[e * GRID_W:(e + 1) * GRID_W,
                                                                   m * LANES:(m + 1) * LANES].reshape(1, GRID_W, LANES)
            return carry

        lax.fori_loop(0, R // NA_GROUP, group, 0)
        d_ref[1] = dkv[0].astype(BF16)
        d_ref[2] = dkv[1].astype(BF16)

    def col(seg):
        return pl.BlockSpec((None, T, LANES), lambda p, b, seg=seg: (b, 0, base + seg * NA_PAIRS + p))

    return _call_hosting(
        body, hosted, name="na_bwd", grid=(NA_PAIRS, B),
        out_shape=(jax.ShapeDtypeStruct((B, 3, T, NA_WIDTH), BF16),
                   jax.ShapeDtypeStruct((NA_HEADS, 2 * NA_KH - 2, GRID_W, LANES), F32)),
        in_specs=[col(0), col(1), col(2),
                  pl.BlockSpec((2, 2 * NA_KH - 2, GRID_W, LANES), lambda p, b: (p, 0, 0, 0)),
                  pl.BlockSpec((None, N, LANES), lambda p, b: (b, 0, p))],
        out_specs=(pl.BlockSpec((None, 3, T, LANES), lambda p, b: (b, 0, 0, p)),
                   pl.BlockSpec((2, 2 * NA_KH - 2, GRID_W, LANES), lambda p, b: (p, 0, 0, 0))),
        scratch_shapes=[pltpu.VMEM((T, LANES), BF16)] * 2 + [pltpu.VMEM((2, T, LANES), F32)],
        args=(proj, proj, proj, bias2, dlat))


def _split3(a):
    hi = a.astype(BF16)
    r1 = a - hi.astype(F32)
    mid = r1.astype(BF16)
    lo = (r1 - mid.astype(F32)).astype(BF16)
    return hi, mid, lo


def _rpb_reduce(dbias2, onehot2):
    rows = dbias2.shape[0] * dbias2.shape[1]
    flat = dbias2.reshape(rows, GRID_W * LANES)

    def body(a_ref, oh_ref, o_ref):
        hi, mid, lo = _split3(a_ref[...])
        oh = oh_ref[...]
        o_ref[...] = _dot(hi, oh) + _dot(mid, oh) + _dot(lo, oh)

    return pl.pallas_call(
        body, name="rpb_reduce", out_shape=jax.ShapeDtypeStruct((rows, LANES), F32),
        in_specs=[_vmem(), _vmem()], out_specs=_vmem(),
        compiler_params=pltpu.CompilerParams(vmem_limit_bytes=VMEM_LIMIT),
    )(flat, onehot2)


def _dense_core(lat_ret, lat_na, x, tgt, modl, g_post_mix, g_pre_mlp, g_post_mlp, w_out, w1, w2):
    B, N, D = x.shape
    F = w1.shape[1]
    w2_rows = w2.shape[0] // N_DEV
    mixw = w_out.shape[0]
    half = mixw // 2
    tm = _div_tile(N, 256, 16)
    nt = N // tm
    fc = _div_tile(F, 1024, LANES)

    def body(lr_ref, ln_ref, x_ref, t_ref, gt1_ref, sh2_ref, sc2_ref, gt2_ref, gpm_ref, gpre_ref, gpo_ref,
             wout_hbm, w1_hbm, w2_part,
             dy1_ref, dlr_ref, dln_ref, dmix_ref, h2_ref, a_ref, du_ref, dz_ref, red_ref, w2_hbm,
             wout_v, w1_v, w2_v, u_s, sems, fsend, frecv):
        @pl.when((pl.program_id(0) == 0) & (pl.program_id(1) == 0))
        def _():
            relay = [(_row_block(w2_part, w2_rows), _row_block(w2_hbm, w2_rows))]
            _forward_start(relay, fsend, frecv)
            cps = [pltpu.make_async_copy(wout_hbm, wout_v, sems.at[0]),
                   pltpu.make_async_copy(w1_hbm, w1_v, sems.at[1])]
            for cp in cps:
                cp.start()
            _forward_wait(relay, fsend, frecv)
            cps.append(pltpu.make_async_copy(w2_hbm, w2_v, sems.at[2]))
            cps[2].start()
            for cp in cps:
                cp.wait()

        gt1 = gt1_ref[...]
        sh2 = sh2_ref[...]
        sc2 = sc2_ref[...]
        gt2 = gt2_ref[...]
        gpm = gpm_ref[...]
        gpre = gpre_ref[...]
        gpo = gpo_ref[...]

        def rowmean(a):
            return jnp.mean(a, axis=-1, keepdims=True)

        def colsum(a):
            return jnp.sum(a, axis=0, keepdims=True)

        mix = _dot(lr_ref[...], wout_v[0:half, :]) + _dot(ln_ref[...], wout_v[half:, :])
        x = x_ref[...]
        rm = lax.rsqrt(rowmean(mix * mix) + NORM_EPS)
        mh = mix * rm
        nm = mh * gpm
        y1 = x + gt1 * nm
        r1 = lax.rsqrt(rowmean(y1 * y1) + NORM_EPS)
        xh = y1 * r1
        n1 = xh * gpre
        h2b = (n1 * (1.0 + sc2) + sh2).astype(BF16)
        h2_ref[...] = h2b
        z = jnp.zeros((tm, D), F32)
        for c0 in range(0, F, fc):
            u = _dot(h2b, w1_v[:, c0:c0 + fc])
            u_s[:, c0:c0 + fc] = u
            ru = jnp.maximum(u, 0.0)
            ab = (ru * ru).astype(BF16)
            a_ref[:, c0:c0 + fc] = ab
            z = z + _dot(ab, w2_v[c0:c0 + fc, :])
        r2 = lax.rsqrt(rowmean(z * z) + NORM_EPS)
        zh = z * r2
        n2 = zh * gpo
        y2 = y1 + gt2 * n2
        err = y2 - t_ref[...]
        loss = 0.5 * jnp.sum(rowmean(err * err))
        dy2 = err * (1.0 / D)
        red_ref[2:3, :] = colsum(dy2 * n2)
        dn2 = dy2 * gt2
        red_ref[6:7, :] = colsum(dn2 * zh)
        dzh = dn2 * gpo
        dz = r2 * (dzh - zh * rowmean(dzh * zh))
        dzb = dz.astype(BF16)
        dz_ref[...] = dzb
        dh2 = jnp.zeros((tm, D), F32)
        for c0 in range(0, F, fc):
            da = _dot_nt(dzb, w2_v[c0:c0 + fc, :])
            dub = (da * (2.0 * jnp.maximum(u_s[:, c0:c0 + fc], 0.0))).astype(BF16)
            du_ref[:, c0:c0 + fc] = dub
            dh2 = dh2 + _dot_nt(dub, w1_v[:, c0:c0 + fc])
        red_ref[3:4, :] = colsum(dh2 * n1)
        red_ref[4:5, :] = colsum(dh2)
        dn1 = dh2 * (1.0 + sc2)
        red_ref[5:6, :] = colsum(dn1 * xh)
        dxh = dn1 * gpre
        dy1 = dy2 + r1 * (dxh - xh * rowmean(dxh * xh))
        dy1_ref[...] = dy1
        red_ref[0:1, :] = colsum(dy1 * nm)
        dnm = dy1 * gt1
        red_ref[1:2, :] = colsum(dnm * mh)
        dmh = dnm * gpm
        dmix = (rm * (dmh - mh * rowmean(dmh * mh))).astype(BF16)
        dmix_ref[...] = dmix
        dlr_ref[...] = _dot_nt(dmix, wout_v[0:half, :])
        dln_ref[...] = _dot_nt(dmix, wout_v[half:, :])
        red_ref[7:8, :] = jnp.zeros((1, D), F32) + loss

    def tok(w):
        return pl.BlockSpec((None, tm, w), lambda b, t: (b, t, 0))

    def mod(k):
        return pl.BlockSpec((None, None, 1, D), lambda b, t, k=k: (b, k, 0, 0))

    def vec():
        return pl.BlockSpec((1, D), lambda b, t: (0, 0))

    return pl.pallas_call(
        body, name="dense_core", grid=(B, nt),
        out_shape=(jax.ShapeDtypeStruct((B, N, D), F32), jax.ShapeDtypeStruct((B, N, half), F32),
                   jax.ShapeDtypeStruct((B, N, half), F32), jax.ShapeDtypeStruct((B, N, D), BF16),
                   jax.ShapeDtypeStruct((B, N, D), BF16), jax.ShapeDtypeStruct((B, N, F), BF16),
                   jax.ShapeDtypeStruct((B, N, F), BF16), jax.ShapeDtypeStruct((B, N, D), BF16),
                   jax.ShapeDtypeStruct((B, nt, SUBLANES, D), F32),
                   jax.ShapeDtypeStruct(w2.shape, w2.dtype)),
        in_specs=[tok(half), tok(half), tok(D), tok(D), mod(2), mod(3), mod(4), mod(5), vec(), vec(), vec(),
                  _any(), _any(), _any()],
        out_specs=(tok(D), tok(half), tok(half), tok(D), tok(D), tok(F), tok(F), tok(D),
                   pl.BlockSpec((None, None, SUBLANES, D), lambda b, t: (b, t, 0, 0)), _any()),
        scratch_shapes=[pltpu.VMEM((mixw, D), BF16), pltpu.VMEM((D, F), BF16), pltpu.VMEM((F, D), BF16),
                        pltpu.VMEM((tm, F), F32), pltpu.SemaphoreType.DMA((3,)),
                        pltpu.SemaphoreType.DMA((1, 3)), pltpu.SemaphoreType.DMA((1, 3))],
        input_output_aliases={13: 9},
        compiler_params=_params("arbitrary", "arbitrary"),
    )(lat_ret, lat_na, x, tgt, modl, modl, modl, modl, g_post_mix, g_pre_mlp, g_post_mlp, w_out, w1, w2)[:9]


def _inproj_bwd(dret, dna, x, ctx, dy1, modl, g1, w_in_t, hosted):
    B, N, D = x.shape
    n_ctx = ctx.shape[1]
    T = n_ctx + N
    tm = _div_tile(n_ctx, 256, 16)
    nct, ctx_spec, lat_spec = _token_tiles(n_ctx, tm)
    nt = T // tm
    nseg_r = dret.shape[1]
    nseg_n = dna.shape[1]
    nw = w_in_t.shape[0]

    def body(*refs):
        seg_refs = refs[:nseg_r + nseg_n]
        c_ref, x_ref, dy1_ref, sc_ref, g_ref, w_ref, dx_ref, red_ref = refs[nseg_r + nseg_n:]
        t = pl.program_id(1)
        dh = jnp.zeros((tm, D), F32)
        for s, ref in enumerate(seg_refs):
            dh = dh + _dot(ref[...], w_ref[s * SEG:(s + 1) * SEG, :])
        x = jnp.where(t < nct, c_ref[...], x_ref[...])
        g = g_ref[...]
        r = lax.rsqrt(jnp.mean(x * x, axis=-1, keepdims=True) + NORM_EPS)
        xh = x * r
        red_ref[0:1, :] = jnp.sum(dh, axis=0, keepdims=True)
        red_ref[1:2, :] = jnp.sum(dh * (xh * g), axis=0, keepdims=True)
        dn = dh * (1.0 + sc_ref[...])
        red_ref[2:3, :] = jnp.sum(dn * xh, axis=0, keepdims=True)
        red_ref[3:, :] = jnp.zeros((SUBLANES - 3, D), F32)
        dxh = dn * g
        dx = r * (dxh - xh * jnp.mean(dxh * xh, axis=-1, keepdims=True))
        dx_ref[...] = dx + jnp.where(t >= nct, dy1_ref[...], 0.0)

    def mrow(b, t):
        return jnp.where(t < nct, B, b)

    def seg(s):
        return pl.BlockSpec((None, None, tm, SEG), lambda b, t, s=s: (b, s, t, 0))

    return _call_hosting(
        body, hosted, name="inproj_bwd", grid=(B, nt),
        out_shape=(jax.ShapeDtypeStruct((B, N, D), F32), jax.ShapeDtypeStruct((B, nt, SUBLANES, D), F32)),
        in_specs=[seg(s) for s in range(nseg_r)] + [seg(s) for s in range(nseg_n)]
                 + [ctx_spec(D), lat_spec(D), lat_spec(D),
                    pl.BlockSpec((None, None, 1, D), lambda b, t: (mrow(b, t), 1, 0, 0)),
                    pl.BlockSpec((1, D), lambda b, t: (0, 0)),
                    pl.BlockSpec((nw, D), lambda b, t: (0, 0))],
        out_specs=(lat_spec(D), pl.BlockSpec((None, None, SUBLANES, D), lambda b, t: (b, t, 0, 0))),
        scratch_shapes=[], args=(*([dret] * nseg_r), *([dna] * nseg_n), ctx, x, dy1, modl, g1, w_in_t))


def _tn_matmul(lhs, rhs, name):
    B, S, T, W = lhs.shape
    nn = rhs.shape[-1]
    tk = _div_tile(T, 1024, LANES)
    bm = _div_tile(W, 1024, LANES)
    bn = _div_tile(nn, 1024, LANES)
    nkt = T // tk
    nk = B * nkt

    def body(l_ref, r_ref, o_ref, acc):
        k = pl.program_id(3)

        @pl.when(k == 0)
        def _():
            acc[...] = jnp.zeros(acc.shape, F32)

        acc[...] += _dot_tn(l_ref[...].astype(BF16), r_ref[...].astype(BF16))

        @pl.when(k == nk - 1)
        def _():
            o_ref[...] = acc[...].astype(BF16)

    nwb = W // bm
    return pl.pallas_call(
        functools.partial(body), name=name, grid=(S, nwb, nn // bn, nk),
        out_shape=jax.ShapeDtypeStruct((S * W, nn), BF16),
        in_specs=[pl.BlockSpec((None, None, tk, bm), lambda s, i, j, k: (k // nkt, s, k % nkt, i)),
                  pl.BlockSpec((None, tk, bn), lambda s, i, j, k: (k // nkt, k % nkt, j))],
        out_specs=pl.BlockSpec((bm, bn), lambda s, i, j, k: (s * nwb + i, j)),
        scratch_shapes=[pltpu.VMEM((bm, bn), F32)],
        compiler_params=_params("parallel", "parallel", "parallel", "arbitrary"),
    )(lhs, rhs)


def _scatter_start(g, rows, name):
    land_shape = (N_DEV, rows, g.shape[1])
    hbm = pl.BlockSpec(memory_space=pltpu.HBM)
    sem = pl.BlockSpec(memory_space=pltpu.SEMAPHORE)

    def body(g_ref, land_ref, send_sems, recv_sems, g_thru, land_thru, token):
        me, peers = _me_and_peers()
        src = _row_block(g_ref, rows)
        for i, (dev, pid) in enumerate(peers):
            _remote(src(pid), land_ref.at[me], send_sems.at[i], recv_sems.at[i], dev).start()
        token[...] = jnp.zeros_like(token)

    return pl.pallas_call(
        body, name=name,
        out_shape=(pltpu.SemaphoreType.DMA((N_DEV - 1,)), pltpu.SemaphoreType.DMA((N_DEV - 1,)),
                   pltpu.HBM(g.shape, g.dtype), pltpu.HBM(land_shape, g.dtype),
                   jax.ShapeDtypeStruct((SUBLANES, LANES), F32)),
        in_specs=(hbm, hbm), out_specs=(sem, sem, hbm, hbm, _vmem()), input_output_aliases={0: 2, 1: 3},
        compiler_params=pltpu.CompilerParams(has_side_effects=pltpu.SideEffectType.DATAFLOW_SIDE_EFFECTING),
    )(pltpu.with_memory_space_constraint(g, pltpu.HBM),
      pltpu.with_memory_space_constraint(lax.empty(land_shape, g.dtype), pltpu.HBM))


def _scatter_wait(send_sems, recv_sems, g_thru, land_thru, after, rows, name):
    hbm = pl.BlockSpec(memory_space=pltpu.HBM)
    sem = pl.BlockSpec(memory_space=pltpu.SEMAPHORE)

    def body(g_ref, land_ref, send_sems, recv_sems, after_ref, g_out, land_out):
        me, peers = _me_and_peers()
        src = _row_block(g_ref, rows)
        for i, (dev, pid) in enumerate(peers):
            cp = _remote(src(pid), land_ref.at[pid], send_sems.at[i], recv_sems.at[i], dev)
            cp.wait_send()
            cp.wait_recv()

    return pl.pallas_call(
        body, name=name,
        out_shape=(pltpu.HBM(g_thru.shape, g_thru.dtype), pltpu.HBM(land_thru.shape, land_thru.dtype)),
        in_specs=(hbm, hbm, sem, sem, pl.BlockSpec(memory_space=pl.ANY)), out_specs=(hbm, hbm),
        input_output_aliases={0: 0, 1: 1},
        compiler_params=pltpu.CompilerParams(has_side_effects=pltpu.SideEffectType.DATAFLOW_SIDE_EFFECTING),
    )(g_thru, land_thru, send_sems, recv_sems, after)


def _sum_slots(buf, name):
    _, rows, cols = buf.shape
    tr = _div_tile(rows, 256, 2 * SUBLANES)

    def body(b_ref, o_ref):
        acc = b_ref[0].astype(F32)
        for k in range(1, N_DEV):
            acc = acc + b_ref[k].astype(F32)
        o_ref[...] = acc

    return pl.pallas_call(
        functools.partial(body), name=name, grid=(rows // tr,),
        out_shape=jax.ShapeDtypeStruct((rows, cols), F32),
        in_specs=[pl.BlockSpec((N_DEV, tr, cols), lambda i: (0, i, 0))],
        out_specs=pl.BlockSpec((tr, cols), lambda i: (i, 0)),
        compiler_params=_params("parallel"),
    )(buf)


def _small_ar(vec, dmods, silu_all, w_ada, c_ctx):
    rv = vec.shape[0]
    D = silu_all.shape[1]
    ncol = w_ada.shape[1]
    nm = dmods.shape[1]
    srows = silu_all.shape[0]

    def body(vec_ref, dm_ref, s_ref, w_ref, cc_ref, tot_ref, gb_ref, gw_ref, gc_ref,
             vbuf, mbuf, tbuf, dmx, send1, recv1, send3, recv3):
        me, _ = _me_and_peers()
        vbuf[me] = vec_ref[...]
        mbuf[me] = dm_ref[...]
        both = [(lambda p: vbuf.at[me], lambda p: vbuf.at[p]), (lambda p: mbuf.at[me], lambda p: mbuf.at[p])]
        _push_start(both, ALL_PEERS, send1, recv1)
        _push_wait_recv(both, ALL_PEERS, send1, recv1)
        _push_wait_send(both, ALL_PEERS, send1, recv1)
        tot = vbuf[0]
        msum = mbuf[0]
        for k in range(1, N_DEV):
            tot = tot + vbuf[k]
            msum = msum + mbuf[k]
        tot_ref[...] = tot
        gb_ref[...] = jnp.sum(msum, axis=0, keepdims=True)
        loc = pl.ds(pl.multiple_of(me * ncol, ncol), ncol)
        for k in range(N_DEV):
            dmx[k * SUBLANES:(k + 1) * SUBLANES, :] = mbuf[k, :, loc]
        cm = msum[2:3, :]
        mbuf[0, 2:3, :] = cm
        cm_loc = mbuf[0, 2:3, loc]
        dmx[N_DEV * SUBLANES:, :] = jnp.concatenate([cm_loc, jnp.zeros((SUBLANES - 1, ncol), F32)], axis=0)
        gw_ref[...] = _dot_tn(s_ref[...], dmx[...])
        tbuf[me] = _dot_nt(dmx[N_DEV * SUBLANES:, :], w_ref[...])
        _exchange(lambda p: tbuf.at[me], lambda p: tbuf.at[p], send3, recv3)
        tsum = tbuf[0]
        for k in range(1, N_DEV):
            tsum = tsum + tbuf[k]
        cc = cc_ref[...]
        sg = _sigmoid(cc)
        gc_ref[...] = tsum[0:1, :] * (sg * (1.0 + cc * (1.0 - sg)))

    return pl.pallas_call(
        body, name="small_ar",
        out_shape=(jax.ShapeDtypeStruct((rv, LANES), F32), jax.ShapeDtypeStruct((1, nm), F32),
                   jax.ShapeDtypeStruct((D, ncol), F32), jax.ShapeDtypeStruct((1, D), F32)),
        in_specs=[_vmem()] * 5, out_specs=(_vmem(),) * 4,
        scratch_shapes=[pltpu.VMEM((N_DEV, rv, LANES), F32), pltpu.VMEM((N_DEV, SUBLANES, nm), F32),
                        pltpu.VMEM((N_DEV, SUBLANES, D), F32), pltpu.VMEM((srows, ncol), F32)]
                       + [pltpu.SemaphoreType.DMA((2, N_DEV - 1))] * 2 + [pltpu.SemaphoreType.DMA((N_DEV - 1,))] * 2,
        compiler_params=pltpu.CompilerParams(vmem_limit_bytes=VMEM_LIMIT),
    )(vec, dmods, silu_all, w_ada, c_ctx.reshape(1, D))


def _adam_update(w, g, m, v):
    mn = ADAM_B1 * m + (1.0 - ADAM_B1) * g
    vn = ADAM_B2 * v + (1.0 - ADAM_B2) * (g * g)
    m_hat = mn / (1.0 - ADAM_B1 ** ADAM_STEP)
    v_hat = vn / (1.0 - ADAM_B2 ** ADAM_STEP)
    return -ADAM_LR * (m_hat / (jnp.sqrt(v_hat) + ADAM_EPS) + ADAM_WD * w), mn, vn


def _adamw(w, g, m, v, name):
    rows, cols = w.shape
    tr = _div_tile(rows, 256, SUBLANES) if rows * cols > 65536 else rows

    def body(w_ref, g_ref, m_ref, v_ref, d_ref, nm_ref, nv_ref):
        d_ref[...], nm_ref[...], nv_ref[...] = _adam_update(w_ref[...], g_ref[...], m_ref[...], v_ref[...])

    spec = pl.BlockSpec((tr, cols), lambda i: (i, 0))
    return pl.pallas_call(
        functools.partial(body), name=name, grid=(rows // tr,),
        out_shape=(jax.ShapeDtypeStruct((rows, cols), F32),) * 3,
        in_specs=[spec] * 4, out_specs=(spec,) * 3,
        compiler_params=_params("parallel"),
    )(w, g, m, v)


def _sum_adamw(buf, w, m, v, name):
    _, rows, cols = buf.shape
    tr = _div_tile(rows, 256, 2 * SUBLANES)

    def body(b_ref, w_ref, m_ref, v_ref, g_ref, d_ref, nm_ref, nv_ref):
        g = b_ref[0].astype(F32)
        for k in range(1, N_DEV):
            g = g + b_ref[k].astype(F32)
        g_ref[...] = g
        d_ref[...], nm_ref[...], nv_ref[...] = _adam_update(w_ref[...], g, m_ref[...], v_ref[...])

    spec = pl.BlockSpec((tr, cols), lambda i: (i, 0))
    return pl.pallas_call(
        functools.partial(body), name=name, grid=(rows // tr,),
        out_shape=(jax.ShapeDtypeStruct((rows, cols), F32),) * 4,
        in_specs=[pl.BlockSpec((N_DEV, tr, cols), lambda i: (0, i, 0))] + [spec] * 3, out_specs=(spec,) * 4,
        compiler_params=_params("parallel"),
    )(buf, w, m, v)


def _rope_tables(n_ctx, n):
    n_freq = RET_DIM // 4
    inv = np.float32(ROPE_BASE) ** (-np.arange(n_freq, dtype=np.float32) / np.float32(n_freq))
    tok = np.arange(n)
    pos_r = (tok // GRID_W).astype(np.float32)
    pos_c = (tok % GRID_W).astype(np.float32)
    ang_r = (pos_r[:, None] * inv[None, :]).astype(np.float32)
    ang_c = (pos_c[:, None] * inv[None, :]).astype(np.float32)
    cos = np.concatenate([np.cos(ang_r), np.cos(ang_r), np.cos(ang_c), np.cos(ang_c)], axis=-1)
    sin = np.concatenate([-np.sin(ang_r), np.sin(ang_r), -np.sin(ang_c), np.sin(ang_c)], axis=-1)
    cos = np.concatenate([np.ones((n_ctx, RET_DIM), np.float32), cos], axis=0)
    sin = np.concatenate([np.zeros((n_ctx, RET_DIM), np.float32), sin], axis=0)
    return jnp.asarray(cos, F32), jnp.asarray(sin, F32)


def _na_tables():
    q = np.arange(GRID_W)[:, None]
    k = np.arange(GRID_W)[None, :]
    start = np.clip(q - NA_KW // 2, 0, GRID_W - NA_KW)
    valid = (k >= start) & (k < start + NA_KW)
    dc = np.clip(k - q + (NA_KW - 1), 0, 2 * NA_KW - 2)
    ncls = 2 * NA_KW - 1
    onehot = (dc[None] == np.arange(ncls)[:, None, None]) & valid[None]
    oh2 = np.zeros((GRID_W, LANES, LANES), np.float32)
    for c in range(ncls):
        oh2[:, :GRID_W, c] = onehot[c]
        oh2[:, GRID_W:, 32 + c] = onehot[c]
    return onehot.astype(np.float32), valid, oh2.reshape(GRID_W * LANES, LANES)


def _paired_bias(rpb, onehot, valid):
    t = jnp.einsum("hdc,cqk->hdqk", rpb, jnp.asarray(onehot), precision=lax.Precision.HIGHEST)
    t = jnp.where(jnp.asarray(valid)[None, None], t, NEG_INF)
    return jnp.concatenate([t[:, :-1], t[:, 1:]], axis=-1)


def kernel(x, c, ctx, c_ctx, w_ada, b_ada, g_pre_mix, g_post_mix, g_pre_mlp, g_post_mlp, w_in, ret_decay, ret_gn, na_rpb, w_out, w_mlp1, w_mlp2, loss_target, m_c_ctx, m_w_ada, m_b_ada, m_g_pre_mix, m_g_post_mix, m_g_pre_mlp, m_g_post_mlp, m_w_in, m_ret_decay, m_ret_gn, m_na_rpb, m_w_out, m_w_mlp1, m_w_mlp2, v_c_ctx, v_w_ada, v_b_ada, v_g_pre_mix, v_g_post_mix, v_g_pre_mlp, v_g_post_mlp, v_w_in, v_ret_decay, v_ret_gn, v_na_rpb, v_w_out, v_w_mlp1, v_w_mlp2):
    B, N, D = x.shape
    C = ctx.shape[1]
    T = C + N

    silu_all, mods_g, win_b, wout_l, w1_l, w2_l = _mod_gather(c, c_ctx, w_ada[0], b_ada, w_in[0].T, w_out[0],
                                                             w_mlp1[0], w_mlp2[0])
    mods_mine = mods_g.transpose(1, 0, 2).reshape(mods_g.shape[1], N_MOD * D)
    modl = jnp.concatenate([mods_mine[:B], mods_mine[SUBLANES:SUBLANES + 1]], axis=0)
    modl = modl.reshape(B + 1, N_MOD, 1, D)
    rin = w_in.shape[2]
    rout, c1, r2 = wout_l.shape[0], w1_l.shape[1], w2_l.shape[0]

    def rows_of(n):
        return lambda ref: _row_block(ref, n)

    def cols_of(n):
        return lambda ref: _col_block(ref, n)

    cos, sin = _rope_tables(C, N)
    onehot, valid, oh2 = _na_tables()
    bias2 = _paired_bias(na_rpb[0], onehot, valid)
    lg = jax.nn.log_sigmoid(ret_decay[0].astype(F32))

    level_one = SIBLING + ICI_SAME_CORE
    h_all, proj, w1_part = _inproj_fwd(
        x, ctx, modl, g_pre_mix, win_b,
        [_Hosted("gather", level_one, [w1_l], [cols_of(c1)], [jax.ShapeDtypeStruct((D, N_DEV * c1), BF16)], True)])
    o_ret, lat_ret, wout_b = _ret_fwd(
        proj, cos, sin, lg, ret_gn, C,
        [_Hosted("gather", ALL_PEERS, [wout_l], [rows_of(rout)], [jax.ShapeDtypeStruct((N_DEV * rout, D), BF16)], True)])
    lat_na, w1_b, w2_part = _na_fwd(
        proj, bias2, C,
        [_HostedRelay([w1_part], [cols_of(c1)]),
         _Hosted("gather", level_one, [w2_l], [rows_of(r2)], [jax.ShapeDtypeStruct((N_DEV * r2, D), BF16)], True)])

    (dy1, dlat_ret, dlat_na, dmix, h2, act, du, dz, red_d) = _dense_core(
        lat_ret, lat_na, x, loss_target, modl, g_post_mix, g_pre_mlp, g_post_mlp, wout_b, w1_b, w2_part)

    gw_out_p = jnp.concatenate([_tn_matmul(lat_ret[:, None], dmix, "gw_out_ret"),
                                _tn_matmul(lat_na[:, None], dmix, "gw_out_na")], axis=0)
    gw1_p = _tn_matmul(h2[:, None], du, "gw_mlp1")
    gw2_p = _tn_matmul(act[:, None], dz, "gw_mlp2")

    dret, dgn_p, dlg_p, b1 = _ret_bwd(
        proj, cos, sin, lg, ret_gn, o_ret, dlat_ret, C,
        [_Hosted("scatter", ALL_PEERS, [gw1_p], [cols_of(c1)], [jax.ShapeDtypeStruct((N_DEV, D, c1), BF16)], True)])
    dna, dbias2, b2, bout = _na_bwd(
        proj, bias2, dlat_na, C,
        [_Hosted("scatter", ALL_PEERS, [gw2_p, gw_out_p], [rows_of(r2), rows_of(rout)],
                 [jax.ShapeDtypeStruct((N_DEV, r2, D), BF16), jax.ShapeDtypeStruct((N_DEV, rout, D), BF16)], True)])
    gwin_t_p = jnp.concatenate([_tn_matmul(dret, h_all, "gw_in_ret"), _tn_matmul(dna, h_all, "gw_in_na")], axis=0)
    rs_send, rs_recv, gwin_thru, land_thru, token = _scatter_start(gwin_t_p, rin, "rs_w_in_start")
    grad_x, red_i = _inproj_bwd(dret, dna, x, ctx, dy1, modl, g_pre_mix + token[0, 0], win_b, [])

    fused = {"w_out": _sum_adamw(bout, w_out[0], m_w_out[0], v_w_out[0], "sum_adamw_w_out"),
             "w_mlp1": _sum_adamw(b1, w_mlp1[0], m_w_mlp1[0], v_w_mlp1[0], "sum_adamw_w_mlp1"),
             "w_mlp2": _sum_adamw(b2, w_mlp2[0], m_w_mlp2[0], v_w_mlp2[0], "sum_adamw_w_mlp2")}

    rd = red_d.sum(axis=1)[:, :, :]
    ri = red_i
    nct = ri.shape[1] * C // T
    ri_ctx = ri[:, :nct].sum(axis=(0, 1))
    ri_lat = ri[:, nct:].sum(axis=1)
    d_mods = jnp.concatenate([ri_lat[:, 0], ri_lat[:, 1], rd[:, 0], rd[:, 4], rd[:, 3], rd[:, 2]], axis=-1)
    d_cmods = jnp.concatenate([ri_ctx[0], ri_ctx[1], jnp.zeros(((N_MOD - 2) * D,), F32)])[None]
    dm_slot = jnp.concatenate([d_mods, d_cmods, jnp.zeros((SUBLANES - B - 1, N_MOD * D), F32)], axis=0)
    dg_pre_mix = ri_lat[:, 2].sum(axis=0) + ri_ctx[2]
    dg_post_mix = rd[:, 1].sum(axis=0)
    dg_pre_mlp = rd[:, 5].sum(axis=0)
    dg_post_mlp = rd[:, 6].sum(axis=0)
    loss_p = rd[:, 7, 0].sum()
    d_gn = dgn_p[:, 0].sum(axis=0)
    d_lg = dlg_p[:, :, :2, 0].sum(axis=0).T
    d_decay = d_lg * jax.nn.sigmoid(-ret_decay[0].astype(F32))
    rr = _rpb_reduce(dbias2, jnp.asarray(oh2, BF16)).reshape(NA_HEADS, 2 * NA_KH - 2, LANES)
    ncls = 2 * NA_KW - 1
    d_rpb = (jnp.pad(rr[:, :, :ncls], ((0, 0), (0, 1), (0, 0))) + jnp.pad(rr[:, :, 32:32 + ncls], ((0, 0), (1, 0), (0, 0))))
    d_rpb32 = jnp.pad(d_rpb, ((0, 0), (0, 0), (0, 32 - ncls)))
    pieces = [dg_pre_mix, dg_post_mix, dg_pre_mlp, dg_post_mlp, d_gn, d_rpb32.reshape(-1),
              jnp.pad(d_decay.reshape(-1), (0, LANES - d_decay.size)), jnp.full((LANES,), loss_p, F32)]
    vec = jnp.concatenate(pieces)
    pad = (-vec.shape[0]) % (SUBLANES * LANES)
    vec = jnp.pad(vec, (0, pad)).reshape(-1, LANES)
    tot, g_b_ada, g_w_ada, g_c_ctx = _small_ar(vec, dm_slot, silu_all, w_ada[0], c_ctx)
    gwin_done, land = _scatter_wait(rs_send, rs_recv, gwin_thru, land_thru, tot, rin, "rs_w_in_wait")
    me = 4 * lax.axis_index("x") + 2 * lax.axis_index("y") + lax.axis_index("c")
    own = lax.dynamic_slice_in_dim(gwin_done, me * rin, rin, axis=0)
    g_w_in = _sum_slots(lax.dynamic_update_slice_in_dim(land, own[None], me, axis=0), "sum_w_in").T
    flat = tot.reshape(-1)
    o0 = 0
    g_pre_mix_g = flat[o0:o0 + D]; o0 += D
    g_post_mix_g = flat[o0:o0 + D]; o0 += D
    g_pre_mlp_g = flat[o0:o0 + D]; o0 += D
    g_post_mlp_g = flat[o0:o0 + D]; o0 += D
    g_gn = flat[o0:o0 + RET_WIDTH]; o0 += RET_WIDTH
    nrpb = NA_HEADS * (2 * NA_KH - 1) * 32
    g_rpb = flat[o0:o0 + nrpb].reshape(NA_HEADS, 2 * NA_KH - 1, 32)[:, :, :ncls]; o0 += nrpb
    g_decay = flat[o0:o0 + 2 * RET_HEADS].reshape(2, RET_HEADS); o0 += LANES
    loss = flat[o0]

    grads = {
        "c_ctx": g_c_ctx.reshape(c_ctx.shape), "w_ada": g_w_ada[None], "b_ada": g_b_ada.reshape(b_ada.shape),
        "g_pre_mix": g_pre_mix_g[None], "g_post_mix": g_post_mix_g[None], "g_pre_mlp": g_pre_mlp_g[None],
        "g_post_mlp": g_post_mlp_g[None], "w_in": g_w_in[None], "ret_decay": g_decay[None], "ret_gn": g_gn[None],
        "na_rpb": g_rpb[None], "w_out": fused["w_out"][0][None], "w_mlp1": fused["w_mlp1"][0][None],
        "w_mlp2": fused["w_mlp2"][0][None],
    }
    weights = dict(c_ctx=c_ctx, w_ada=w_ada, b_ada=b_ada, g_pre_mix=g_pre_mix, g_post_mix=g_post_mix,
                   g_pre_mlp=g_pre_mlp, g_post_mlp=g_post_mlp, w_in=w_in, ret_decay=ret_decay, ret_gn=ret_gn,
                   na_rpb=na_rpb, w_out=w_out, w_mlp1=w_mlp1, w_mlp2=w_mlp2)
    m_in = dict(c_ctx=m_c_ctx, w_ada=m_w_ada, b_ada=m_b_ada, g_pre_mix=m_g_pre_mix, g_post_mix=m_g_post_mix,
                g_pre_mlp=m_g_pre_mlp, g_post_mlp=m_g_post_mlp, w_in=m_w_in, ret_decay=m_ret_decay,
                ret_gn=m_ret_gn, na_rpb=m_na_rpb, w_out=m_w_out, w_mlp1=m_w_mlp1, w_mlp2=m_w_mlp2)
    v_in = dict(c_ctx=v_c_ctx, w_ada=v_w_ada, b_ada=v_b_ada, g_pre_mix=v_g_pre_mix, g_post_mix=v_g_post_mix,
                g_pre_mlp=v_g_pre_mlp, g_post_mlp=v_g_post_mlp, w_in=v_w_in, ret_decay=v_ret_decay,
                ret_gn=v_ret_gn, na_rpb=v_na_rpb, w_out=v_w_out, w_mlp1=v_w_mlp1, w_mlp2=v_w_mlp2)
    names = list(weights)
    deltas, new_m, new_v = {}, {}, {}
    for n in names:
        shp = weights[n].shape
        if n in fused:
            deltas[n], new_m[n], new_v[n] = (a.reshape(shp) for a in fused[n][1:])
            continue
        two_d = (-1, shp[-1]) if len(shp) > 1 else (1, shp[0])
        d, nm, nv = _adamw(weights[n].reshape(two_d), grads[n].reshape(two_d), m_in[n].reshape(two_d),
                           v_in[n].reshape(two_d), "adamw_" + n)
        deltas[n], new_m[n], new_v[n] = d.reshape(shp), nm.reshape(shp), nv.reshape(shp)
    return (loss, grad_x, *[grads[n] for n in names], *[deltas[n] for n in names],
            *[new_m[n] for n in names], *[new_v[n] for n in names])
```

```python
import functools
import math

import numpy as np
import jax
import jax.numpy as jnp
from jax import lax
from jax.experimental import pallas as pl
from jax.experimental.pallas import tpu as pltpu

F32 = jnp.float32
BF16 = jnp.bfloat16
MESH = pl.DeviceIdType.MESH

N_DEV = 8
LANES = 128
SUBLANES = 8
VMEM_LIMIT = 60 * 1024 * 1024

GRID_W = 64
RET_HEADS = 4
RET_DIM = 128
RET_WIDTH = RET_HEADS * RET_DIM
NA_HEADS = 8
NA_DIM = 64
NA_WIDTH = NA_HEADS * NA_DIM
NA_PAIRS = NA_HEADS // 2
NA_KH = 8
NA_KW = 16
NA_GROUP = 8
SEG = 512
ROPE_BASE = 10000.0
NORM_EPS = 1e-6
NEG_INF = -1e30
N_MOD = 6

ADAM_LR = 0.001
ADAM_B1 = 0.9
ADAM_B2 = 0.999
ADAM_EPS = 1e-08
ADAM_WD = 0.01
ADAM_STEP = 10


def _dot(a, b):
    return lax.dot_general(a, b, (((1,), (0,)), ((), ())), preferred_element_type=F32)


def _dot_nt(a, b):
    return lax.dot_general(a, b, (((1,), (1,)), ((), ())), preferred_element_type=F32)


def _dot_tn(a, b):
    return lax.dot_general(a, b, (((0,), (0,)), ((), ())), preferred_element_type=F32)


def _sigmoid(x):
    return 1.0 / (1.0 + jnp.exp(-x))


def _div_tile(n, cap, mult):
    if n <= cap:
        return n
    for t in range(cap - cap % mult, 0, -mult):
        if n % t == 0:
            return t
    raise ValueError(f"no tile for {n}")


def _params(*sem):
    return pltpu.CompilerParams(dimension_semantics=tuple(sem) if sem else None,
                                vmem_limit_bytes=VMEM_LIMIT)


def _vmem():
    return pl.BlockSpec(memory_space=pltpu.VMEM)


def _any():
    return pl.BlockSpec(memory_space=pl.ANY)


def _me_and_peers():
    x, y, c = lax.axis_index("x"), lax.axis_index("y"), lax.axis_index("c")
    me = 4 * x + 2 * y + c
    peers = []
    for m in range(1, N_DEV):
        px = 1 - x if (m >> 2) & 1 else x
        py = 1 - y if (m >> 1) & 1 else y
        pc = 1 - c if m & 1 else c
        peers.append(((px, py, pc), 4 * px + 2 * py + pc))
    return me, peers


def _exchange(src_for, dst_from, send_sems, recv_sems):
    me, peers = _me_and_peers()
    sent = []
    for i, (dev, pid) in enumerate(peers):
        cp = pltpu.make_async_remote_copy(src_ref=src_for(pid), dst_ref=dst_from(me),
                                          send_sem=send_sems.at[i], recv_sem=recv_sems.at[i],
                                          device_id=dev, device_id_type=MESH)
        cp.start()
        sent.append(cp)
    for i, (dev, pid) in enumerate(peers):
        pltpu.make_async_remote_copy(src_ref=src_for(pid), dst_ref=dst_from(pid),
                                     send_sem=send_sems.at[i], recv_sem=recv_sems.at[i],
                                     device_id=dev, device_id_type=MESH).wait_recv()
    for cp in sent:
        cp.wait_send()


SIBLING = (1,)
ICI_SAME_CORE = (2, 4, 6)
ALL_PEERS = tuple(range(1, N_DEV))


def _remote(src, dst, send_sem, recv_sem, dev):
    return pltpu.make_async_remote_copy(src_ref=src, dst_ref=dst, send_sem=send_sem, recv_sem=recv_sem,
                                        device_id=dev, device_id_type=MESH)


def _push_start(items, masks, send_sems, recv_sems):
    me, peers = _me_and_peers()
    for k, (src_for, dst_from) in enumerate(items):
        for m in masks:
            dev, pid = peers[m - 1]
            _remote(src_for(pid), dst_from(me), send_sems.at[k, m - 1], recv_sems.at[k, m - 1], dev).start()


def _push_wait_recv(items, masks, send_sems, recv_sems):
    me, peers = _me_and_peers()
    for k, (src_for, dst_from) in enumerate(items):
        for m in masks:
            dev, pid = peers[m - 1]
            _remote(src_for(pid), dst_from(pid), send_sems.at[k, m - 1], recv_sems.at[k, m - 1], dev).wait_recv()


def _push_wait_send(items, masks, send_sems, recv_sems):
    me, peers = _me_and_peers()
    for k, (src_for, dst_from) in enumerate(items):
        for m in masks:
            dev, pid = peers[m - 1]
            _remote(src_for(pid), dst_from(me), send_sems.at[k, m - 1], recv_sems.at[k, m - 1], dev).wait_send()


def _forward_start(items, send_sems, recv_sems):
    me, peers = _me_and_peers()
    sib = peers[0][0]
    for k, (blk_in, blk_out) in enumerate(items):
        for j, m in enumerate(ICI_SAME_CORE):
            pid = peers[m - 1][1]
            _remote(blk_in(pid), blk_out(pid), send_sems.at[k, j], recv_sems.at[k, j], sib).start()


def _forward_wait(items, send_sems, recv_sems):
    me, peers = _me_and_peers()
    sib = peers[0][0]
    for k, (blk_in, blk_out) in enumerate(items):
        for j, m in enumerate(ICI_SAME_CORE):
            got = peers[(m | 1) - 1][1]
            _remote(blk_in(got), blk_out(got), send_sems.at[k, j], recv_sems.at[k, j], sib).wait_recv()
    for k, (blk_in, blk_out) in enumerate(items):
        for j, m in enumerate(ICI_SAME_CORE):
            pid = peers[m - 1][1]
            _remote(blk_in(pid), blk_out(pid), send_sems.at[k, j], recv_sems.at[k, j], sib).wait_send()


def _mod_gather(c, c_ctx, w_ada, b_ada, w_in_t, w_out, w1, w2):
    B, D = c.shape
    ncol = w_ada.shape[1]
    rows = SUBLANES * N_DEV + SUBLANES

    def body(c_ref, cc_ref, w_ref, b_ref, win_ref, wout_ref, w1_ref, w2_ref,
             s_ref, m_ref, gin_ref, wout_b, w1_b, w2_b,
             win_b, msend, send1, recv1, send2, recv2, wsend, wrecv, fsend, frecv, lsem):
        me, _ = _me_and_peers()
        win_b[...] = win_ref[...].astype(BF16)
        block = _row_block(gin_ref, w_in_t.shape[0])
        gather = [(lambda p: win_b, block)]
        own = pltpu.make_async_copy(win_b, block(me), lsem.at[0])
        own.start()
        _push_start(gather, SIBLING + ICI_SAME_CORE, wsend, wrecv)
        wout_b[...] = wout_ref[...].astype(BF16)
        w1_b[...] = w1_ref[...].astype(BF16)
        w2_b[...] = w2_ref[...].astype(BF16)
        cv = c_ref[...]
        slot = jnp.concatenate([cv * _sigmoid(cv), jnp.zeros((SUBLANES - B, D), F32)], axis=0)
        my_rows = pl.ds(pl.multiple_of(me * SUBLANES, SUBLANES), SUBLANES)
        s_ref[my_rows, :] = slot
        ccv = cc_ref[...]
        s_ref[SUBLANES * N_DEV:, :] = jnp.concatenate(
            [ccv * _sigmoid(ccv), jnp.zeros((SUBLANES - 1, D), F32)], axis=0)

        def rows_of(p):
            return s_ref.at[pl.ds(pl.multiple_of(p * SUBLANES, SUBLANES), SUBLANES), :]

        _exchange(lambda p: rows_of(me), rows_of, send1, recv1)
        b_loc = b_ref[:, pl.ds(pl.multiple_of(me * ncol, ncol), ncol)]
        mods = _dot(s_ref[...], w_ref[...]) + b_loc
        for p in range(N_DEV):
            msend[p] = jnp.concatenate([mods[p * SUBLANES:(p + 1) * SUBLANES], mods[N_DEV * SUBLANES:]], axis=0)
        m_ref[me] = msend[me]
        _exchange(lambda p: msend.at[p], lambda p: m_ref.at[p], send2, recv2)
        _push_wait_recv(gather, ICI_SAME_CORE, wsend, wrecv)
        relay = [(block, block)]
        _forward_start(relay, fsend, frecv)
        _push_wait_recv(gather, SIBLING, wsend, wrecv)
        _forward_wait(relay, fsend, frecv)
        _push_wait_send(gather, SIBLING + ICI_SAME_CORE, wsend, wrecv)
        own.wait()

    return pl.pallas_call(
        body, name="mod_gather",
        out_shape=(jax.ShapeDtypeStruct((rows, D), F32), jax.ShapeDtypeStruct((N_DEV, 2 * SUBLANES, ncol), F32),
                   jax.ShapeDtypeStruct((N_DEV * w_in_t.shape[0], D), BF16),
                   jax.ShapeDtypeStruct(w_out.shape, BF16), jax.ShapeDtypeStruct(w1.shape, BF16),
                   jax.ShapeDtypeStruct(w2.shape, BF16)),
        in_specs=[_vmem()] * 8, out_specs=(_vmem(), _vmem(), _any(), _vmem(), _vmem(), _vmem()),
        scratch_shapes=[pltpu.VMEM(w_in_t.shape, BF16), pltpu.VMEM((N_DEV, 2 * SUBLANES, ncol), F32)]
                       + [pltpu.SemaphoreType.DMA((N_DEV - 1,))] * 4
                       + [pltpu.SemaphoreType.DMA((1, N_DEV - 1))] * 2 + [pltpu.SemaphoreType.DMA((1, 3))] * 2
                       + [pltpu.SemaphoreType.DMA((1,))],
        compiler_params=pltpu.CompilerParams(vmem_limit_bytes=VMEM_LIMIT),
    )(c, c_ctx.reshape(1, D), w_ada, b_ada, w_in_t, w_out, w1, w2)


def _row_block(ref, rows):
    return lambda p: ref.at[pl.ds(pl.multiple_of(p * rows, 2 * SUBLANES), rows), :]


def _col_block(ref, cols):
    return lambda p: ref.at[:, pl.ds(pl.multiple_of(p * cols, LANES), cols)]


def _slot(ref):
    return lambda p: ref.at[p]


class _Hosted:
    def __init__(self, kind, masks, operands, block_of, out_shapes, with_own):
        self.kind, self.masks, self.operands = kind, masks, list(operands)
        self.block_of, self.out_shapes, self.with_own = block_of, list(out_shapes), with_own
        self.n = len(self.operands)

    def scratch(self):
        return [pltpu.SemaphoreType.DMA((self.n, N_DEV - 1)), pltpu.SemaphoreType.DMA((self.n, N_DEV - 1)),
                pltpu.SemaphoreType.DMA((self.n,))]

    def _items(self, in_refs, out_refs):
        items = []
        for k in range(self.n):
            if self.kind == "gather":
                items.append((lambda p, k=k: in_refs[k], self.block_of[k](out_refs[k])))
            else:
                items.append((self.block_of[k](in_refs[k]), _slot(out_refs[k])))
        return items

    def _own(self, in_refs, out_refs, lsem):
        me, _ = _me_and_peers()
        items = self._items(in_refs, out_refs)
        return [pltpu.make_async_copy(src_for(me), dst_from(me), lsem.at[k])
                for k, (src_for, dst_from) in enumerate(items)]

    def start(self, in_refs, out_refs, sems):
        send, recv, lsem = sems
        if self.with_own:
            for cp in self._own(in_refs, out_refs, lsem):
                cp.start()
        _push_start(self._items(in_refs, out_refs), self.masks, send, recv)

    def wait(self, in_refs, out_refs, sems):
        send, recv, lsem = sems
        items = self._items(in_refs, out_refs)
        _push_wait_recv(items, self.masks, send, recv)
        _push_wait_send(items, self.masks, send, recv)
        if self.with_own:
            for cp in self._own(in_refs, out_refs, lsem):
                cp.wait()


class _HostedRelay:
    def __init__(self, arrays, block_of):
        self.operands, self.block_of = list(arrays), block_of
        self.out_shapes = [jax.ShapeDtypeStruct(a.shape, a.dtype) for a in arrays]
        self.n = len(self.operands)

    def scratch(self):
        return [pltpu.SemaphoreType.DMA((self.n, 3)), pltpu.SemaphoreType.DMA((self.n, 3))]

    def _items(self, in_refs, out_refs):
        return [(self.block_of[k](in_refs[k]), self.block_of[k](out_refs[k])) for k in range(self.n)]

    def start(self, in_refs, out_refs, sems):
        _forward_start(self._items(in_refs, out_refs), *sems)

    def wait(self, in_refs, out_refs, sems):
        _forward_wait(self._items(in_refs, out_refs), *sems)


def _call_hosting(body, hosted, *, name, grid, out_shape, in_specs, out_specs, scratch_shapes, args):
    n_in, n_out, n_scr = len(in_specs), len(out_shape), len(scratch_shapes)
    hn = sum(hs.n for hs in hosted)
    n_sem = [len(hs.scratch()) for hs in hosted]

    def wrapped(*refs):
        ins = refs[:n_in]
        h_in = refs[n_in:n_in + hn]
        outs = refs[n_in + hn:n_in + hn + n_out]
        h_out = refs[n_in + hn + n_out:n_in + 2 * hn + n_out]
        scr = refs[n_in + 2 * hn + n_out:n_in + 2 * hn + n_out + n_scr]
        sems = refs[n_in + 2 * hn + n_out + n_scr:]
        ids = [pl.program_id(i) for i in range(len(grid))]
        first = functools.reduce(jnp.logical_and, [i == 0 for i in ids])
        last = functools.reduce(jnp.logical_and, [i == g - 1 for i, g in zip(ids, grid)])
        parts, o0, s0 = [], 0, 0
        for hs, ns in zip(hosted, n_sem):
            parts.append((hs, h_in[o0:o0 + hs.n], h_out[o0:o0 + hs.n], sems[s0:s0 + ns]))
            o0 += hs.n
            s0 += ns

        @pl.when(first)
        def _():
            for hs, hi, ho, se in parts:
                hs.start(hi, ho, se)

        body(*ins, *outs, *scr)

        @pl.when(last)
        def _():
            for hs, hi, ho, se in parts:
                hs.wait(hi, ho, se)

    aliases, o0 = {}, 0
    for hs in hosted:
        if isinstance(hs, _HostedRelay):
            aliases.update({n_in + o0 + k: n_out + o0 + k for k in range(hs.n)})
        o0 += hs.n
    return pl.pallas_call(
        wrapped, name=name, grid=grid,
        out_shape=tuple(out_shape) + tuple(s for hs in hosted for s in hs.out_shapes),
        in_specs=list(in_specs) + [_any()] * hn,
        out_specs=tuple(out_specs) + (_any(),) * hn,
        scratch_shapes=list(scratch_shapes) + [s for hs in hosted for s in hs.scratch()],
        input_output_aliases=aliases,
        compiler_params=_params(*(("arbitrary",) * len(grid))),
    )(*args, *[a for hs in hosted for a in hs.operands])


def _token_tiles(n_ctx, tm):
    nct = n_ctx // tm

    def ctx_spec(D):
        return pl.BlockSpec((None, tm, D), lambda b, t: (b, jnp.minimum(t, nct - 1), 0))

    def lat_spec(D):
        return pl.BlockSpec((None, tm, D), lambda b, t: (b, jnp.maximum(t - nct, 0), 0))

    return nct, ctx_spec, lat_spec


def _inproj_fwd(x, ctx, modl, g1, w_in_t, hosted):
    B, N, D = x.shape
    n_ctx = ctx.shape[1]
    T = n_ctx + N
    nw = w_in_t.shape[0]
    tm = _div_tile(n_ctx, 256, 16)
    nct, ctx_spec, lat_spec = _token_tiles(n_ctx, tm)

    def body(c_ref, x_ref, sh_ref, sc_ref, g_ref, w_ref, h_ref, p_ref):
        x = jnp.where(pl.program_id(1) < nct, c_ref[...], x_ref[...])
        r = lax.rsqrt(jnp.mean(x * x, axis=-1, keepdims=True) + NORM_EPS)
        h = ((x * r) * g_ref[...]) * (1.0 + sc_ref[...]) + sh_ref[...]
        hb = h.astype(BF16)
        h_ref[...] = hb
        p_ref[...] = _dot_nt(hb, w_ref[...])

    def mrow(b, t):
        return jnp.where(t < nct, B, b)

    return _call_hosting(
        body, hosted, name="inproj_fwd", grid=(B, T // tm),
        out_shape=(jax.ShapeDtypeStruct((B, T, D), BF16), jax.ShapeDtypeStruct((B, T, nw), F32)),
        in_specs=[ctx_spec(D), lat_spec(D),
                  pl.BlockSpec((None, None, 1, D), lambda b, t: (mrow(b, t), 0, 0, 0)),
                  pl.BlockSpec((None, None, 1, D), lambda b, t: (mrow(b, t), 1, 0, 0)),
                  pl.BlockSpec((1, D), lambda b, t: (0, 0)),
                  pl.BlockSpec((nw, D), lambda b, t: (0, 0))],
        out_specs=(pl.BlockSpec((None, tm, D), lambda b, t: (b, t, 0)),
                   pl.BlockSpec((None, tm, nw), lambda b, t: (b, t, 0))),
        scratch_shapes=[], args=(ctx, x, modl, modl, g1, w_in_t))


def _swap32(x):
    lane = lax.broadcasted_iota(jnp.int32, x.shape, 1)
    return jnp.where((lane % 64) < 32, pltpu.roll(x, 96, 1), pltpu.roll(x, 32, 1))


def _rope(x, cos, sin):
    return x * cos + _swap32(x) * sin


def _unrope(dy, cos, sin):
    return dy * cos + _swap32(dy * sin)


def _ret_weights(lgf, lgb, dist):
    return jnp.exp(jnp.where(dist >= 0.0, lgf * dist, -lgb * dist))


class _RetDecay:
    def __init__(self, lgf, lgb, rows):
        r = lax.broadcasted_iota(jnp.int32, (rows, RET_DIM), 0).astype(F32)
        self.head = r + 1.0
        self.tail = (rows - 1.0) - r
        self.q_f = jnp.exp(lgf * self.head)
        self.k_f = jnp.exp(lgf * self.tail)
        self.q_b = jnp.exp(lgb * self.tail)
        self.k_b = jnp.exp(lgb * self.head)


def _ret_states(kf32, vs, lgf, lgb, C, c, nt, hf, hb, hfa=None, hba=None):
    dec = _RetDecay(lgf, lgb, c)
    dec_c = _RetDecay(lgf, lgb, C)
    step_f = jnp.exp(jnp.zeros((RET_DIM, RET_DIM), F32) + lgf * c)
    step_b = jnp.exp(jnp.zeros((RET_DIM, RET_DIM), F32) + lgb * c)

    def upd(rows, kdec):
        return _dot_tn((kf32[rows, :] * kdec).astype(BF16), vs[rows, :])

    def lat(t):
        return slice(C + t * c, C + (t + 1) * c)

    state = upd(slice(0, C), dec_c.k_f)
    aged = jnp.zeros_like(state)
    for t in range(nt):
        hf[t] = state.astype(BF16)
        if hfa is not None:
            hfa[t] = aged
        if t < nt - 1:
            aged = step_f * (aged + c * state)
            state = step_f * state + upd(lat(t), dec.k_f)
    state = upd(slice(0, C), dec_c.k_b)
    aged = jnp.zeros_like(state)
    for t in range(nt - 1, -1, -1):
        hb[t] = state.astype(BF16)
        if hba is not None:
            hba[t] = aged
        if t > 0:
            aged = step_b * (aged + c * state)
            state = step_b * state + upd(lat(t), dec.k_b)
    return dec, dec_c, step_f, step_b


def _ret_fwd(proj, cos, sin, lg, gn, n_ctx, hosted):
    B, T, _ = proj.shape
    C = n_ctx
    N = T - C
    c = _div_tile(N, 256, 16)
    nt = N // c
    scale = RET_DIM ** -0.5

    def body(lg_ref, q_ref, k_ref, v_ref, g_ref, cos_ref, sin_ref, gn_ref, o_ref, lat_ref, qs, ks, vs, kf32, hf, hb):
        h = pl.program_id(1)
        lgf = lg_ref[0, h]
        lgb = lg_ref[1, h]
        for rows in [slice(0, C)] + [slice(C + t * c, C + (t + 1) * c) for t in range(nt)]:
            cosb = cos_ref[rows, :]
            sinb = sin_ref[rows, :]
            qs[rows, :] = (_rope(q_ref[rows, :], cosb, sinb) * scale).astype(BF16)
            kr = _rope(k_ref[rows, :], cosb, sinb)
            kf32[rows, :] = kr
            ks[rows, :] = kr.astype(BF16)
            vs[rows, :] = v_ref[rows, :].astype(BF16)
        gnv = gn_ref[...]
        dec, _, _, _ = _ret_states(kf32, vs, lgf, lgb, C, c, nt, hf, hb)
        rc = (lax.broadcasted_iota(jnp.int32, (c, c), 0) - lax.broadcasted_iota(jnp.int32, (c, c), 1)).astype(F32)
        w_diag = _ret_weights(lgf, lgb, rc)
        for t in range(nt):
            rows = slice(C + t * c, C + (t + 1) * c)
            qt = qs[rows, :]
            s = _dot_nt(qt, ks[rows, :])
            o = (_dot((s * w_diag).astype(BF16), vs[rows, :])
                 + dec.q_f * _dot(qt, hf[t]) + dec.q_b * _dot(qt, hb[t]))
            o_ref[t * c:(t + 1) * c, :] = o
            mu = jnp.mean(o, axis=-1, keepdims=True)
            oc = o - mu
            var = jnp.mean(oc * oc, axis=-1, keepdims=True)
            yh = oc * lax.rsqrt(var + NORM_EPS)
            g = g_ref[rows, :]
            lat_ref[t * c:(t + 1) * c, :] = ((yh * gnv) * (g * _sigmoid(g))).astype(BF16)

    def col(seg):
        return pl.BlockSpec((None, T, RET_DIM), lambda b, h, seg=seg: (b, 0, seg * RET_HEADS + h))

    return _call_hosting(
        body, hosted, name="ret_fwd", grid=(B, RET_HEADS),
        out_shape=(jax.ShapeDtypeStruct((B, N, RET_WIDTH), F32), jax.ShapeDtypeStruct((B, N, RET_WIDTH), BF16)),
        in_specs=[pl.BlockSpec(memory_space=pltpu.SMEM), col(0), col(1), col(2), col(3),
                  pl.BlockSpec((T, RET_DIM), lambda b, h: (0, 0)), pl.BlockSpec((T, RET_DIM), lambda b, h: (0, 0)),
                  pl.BlockSpec((1, RET_DIM), lambda b, h: (0, h))],
        out_specs=(pl.BlockSpec((None, N, RET_DIM), lambda b, h: (b, 0, h)),
                   pl.BlockSpec((None, N, RET_DIM), lambda b, h: (b, 0, h))),
        scratch_shapes=[pltpu.VMEM((T, RET_DIM), BF16)] * 3 + [pltpu.VMEM((T, RET_DIM), F32)]
                       + [pltpu.VMEM((nt, RET_DIM, RET_DIM), BF16)] * 2,
        args=(lg, proj, proj, proj, proj, cos, sin, gn))


def _ret_bwd(proj, cos, sin, lg, gn, o, dlat, n_ctx, hosted):
    B, T, _ = proj.shape
    C = n_ctx
    N = T - C
    c = _div_tile(N, 256, 16)
    nt = N // c
    scale = RET_DIM ** -0.5

    def lat(t):
        return slice(C + t * c, C + (t + 1) * c)

    def body(lg_ref, q_ref, k_ref, v_ref, g_ref, cos_ref, sin_ref, gn_ref, o_ref, dl_ref,
             d_ref, dgn_ref, dlg_ref, qs, ks, vs, dos, qf32, kf32, hf, hb, hfa, hba, gf_s, gb_s):
        h = pl.program_id(1)
        lgf = lg_ref[0, h]
        lgb = lg_ref[1, h]
        gnv = gn_ref[...]

        def fold(a):
            return jnp.sum(a.reshape(a.shape[0] // SUBLANES, SUBLANES, a.shape[1]), axis=0)

        for rows in [slice(0, C)] + [lat(t) for t in range(nt)]:
            cosb = cos_ref[rows, :]
            sinb = sin_ref[rows, :]
            qr = _rope(q_ref[rows, :], cosb, sinb) * scale
            qf32[rows, :] = qr
            qs[rows, :] = qr.astype(BF16)
            kr = _rope(k_ref[rows, :], cosb, sinb)
            kf32[rows, :] = kr
            ks[rows, :] = kr.astype(BF16)
            vs[rows, :] = v_ref[rows, :].astype(BF16)

        dgn = jnp.zeros((1, RET_DIM), F32)
        for t in range(nt):
            lrows = slice(t * c, (t + 1) * c)
            ov = o_ref[lrows, :]
            mu = jnp.mean(ov, axis=-1, keepdims=True)
            oc = ov - mu
            var = jnp.mean(oc * oc, axis=-1, keepdims=True)
            rstd = lax.rsqrt(var + NORM_EPS)
            yh = oc * rstd
            g = g_ref[lat(t), :]
            sg = _sigmoid(g)
            dl = dl_ref[lrows, :]
            d_ref[3, lat(t), :] = (dl * (yh * gnv) * (sg * (1.0 + g * (1.0 - sg)))).astype(BF16)
            dls = dl * (g * sg)
            dgn = dgn + jnp.sum(dls * yh, axis=0, keepdims=True)
            dyh = dls * gnv
            do = rstd * (dyh - jnp.mean(dyh, axis=-1, keepdims=True)
                         - yh * jnp.mean(dyh * yh, axis=-1, keepdims=True))
            dos[lrows, :] = do.astype(BF16)
        dgn_ref[...] = jnp.concatenate([dgn, jnp.zeros((SUBLANES - 1, RET_DIM), F32)], axis=0)
        d_ref[3, 0:C, :] = jnp.zeros((C, RET_DIM), BF16)
        d_ref[0, 0:C, :] = jnp.zeros((C, RET_DIM), BF16)

        dec, dec_c, step_f, step_b = _ret_states(kf32, vs, lgf, lgb, C, c, nt, hf, hb, hfa, hba)

        def zmat(t, qdec):
            return _dot_tn((qf32[lat(t), :] * qdec).astype(BF16), dos[t * c:(t + 1) * c, :])

        acc3f = jnp.zeros((RET_DIM, RET_DIM), F32)
        acc3b = jnp.zeros((RET_DIM, RET_DIM), F32)
        state = jnp.zeros((RET_DIM, RET_DIM), F32)
        for t in range(nt - 1, -1, -1):
            gf_s[t] = state.astype(BF16)
            z = zmat(t, dec.q_f)
            acc3f = acc3f + hfa[t] * z
            state = step_f * state + z
        gctx_f = state.astype(BF16)
        state = jnp.zeros((RET_DIM, RET_DIM), F32)
        for t in range(nt):
            gb_s[t] = state.astype(BF16)
            z = zmat(t, dec.q_b)
            acc3b = acc3b + hba[t] * z
            state = step_b * state + z
        gctx_b = state.astype(BF16)

        rc = (lax.broadcasted_iota(jnp.int32, (c, c), 0) - lax.broadcasted_iota(jnp.int32, (c, c), 1)).astype(F32)
        w_diag = _ret_weights(lgf, lgb, rc)
        wg_f = jnp.where(rc >= 0.0, w_diag * rc, 0.0)
        wg_b = jnp.where(rc < 0.0, -w_diag * rc, 0.0)
        accf = jnp.zeros((SUBLANES, RET_DIM), F32)
        accb = jnp.zeros((SUBLANES, RET_DIM), F32)
        gdf = jnp.zeros((SUBLANES, c), F32)
        gdb = jnp.zeros((SUBLANES, c), F32)
        for t in range(nt):
            rows = lat(t)
            qt = qs[rows, :]
            kt = ks[rows, :]
            vt = vs[rows, :]
            dot = dos[t * c:(t + 1) * c, :]
            s = _dot_nt(qt, kt)
            dp = _dot_nt(dot, vt)
            dv = _dot_tn((s * w_diag).astype(BF16), dot)
            ds = (dp * w_diag).astype(BF16)
            dq = _dot(ds, kt)
            dk = _dot_tn(ds, qt)
            gs = dp * s
            gdf = gdf + fold(gs * wg_f)
            gdb = gdb + fold(gs * wg_b)
            qv = qf32[rows, :]
            kv = kf32[rows, :]
            dq_f = dec.q_f * _dot_nt(dot, hf[t])
            dq_b = dec.q_b * _dot_nt(dot, hb[t])
            dk_f = dec.k_f * _dot_nt(vt, gf_s[t])
            dk_b = dec.k_b * _dot_nt(vt, gb_s[t])
            accf = accf + fold(dec.head * dq_f * qv) + fold(dec.tail * dk_f * kv)
            accb = accb + fold(dec.tail * dq_b * qv) + fold(dec.head * dk_b * kv)
            dv = dv + dec.k_f * _dot(kt, gf_s[t]) + dec.k_b * _dot(kt, gb_s[t])
            cosb = cos_ref[rows, :]
            sinb = sin_ref[rows, :]
            d_ref[0, rows, :] = _unrope((dq + dq_f + dq_b) * scale, cosb, sinb).astype(BF16)
            d_ref[1, rows, :] = _unrope(dk + dk_f + dk_b, cosb, sinb).astype(BF16)
            d_ref[2, rows, :] = dv.astype(BF16)
        kc = ks[0:C, :]
        vc = vs[0:C, :]
        kcv = kf32[0:C, :]
        dkc_f = dec_c.k_f * _dot_nt(vc, gctx_f)
        dkc_b = dec_c.k_b * _dot_nt(vc, gctx_b)
        accf = accf + fold(dec_c.tail * dkc_f * kcv)
        accb = accb + fold(dec_c.head * dkc_b * kcv)
        d_ref[1, 0:C, :] = (dkc_f + dkc_b).astype(BF16)
        d_ref[2, 0:C, :] = (dec_c.k_f * _dot(kc, gctx_f) + dec_c.k_b * _dot(kc, gctx_b)).astype(BF16)
        gf = jnp.sum(gdf) + jnp.sum(accf) + jnp.sum(acc3f)
        gb = jnp.sum(gdb) + jnp.sum(accb) + jnp.sum(acc3b)
        row = lax.broadcasted_iota(jnp.int32, (SUBLANES, LANES), 0)
        dlg_ref[...] = jnp.where(row == 0, gf, jnp.where(row == 1, gb, 0.0))

    def col(seg):
        return pl.BlockSpec((None, T, RET_DIM), lambda b, h, seg=seg: (b, 0, seg * RET_HEADS + h))

    return _call_hosting(
        body, hosted, name="ret_bwd", grid=(B, RET_HEADS),
        out_shape=(jax.ShapeDtypeStruct((B, 4, T, RET_WIDTH), BF16),
                   jax.ShapeDtypeStruct((B, SUBLANES, RET_WIDTH), F32),
                   jax.ShapeDtypeStruct((B, RET_HEADS, SUBLANES, LANES), F32)),
        in_specs=[pl.BlockSpec(memory_space=pltpu.SMEM), col(0), col(1), col(2), col(3),
                  pl.BlockSpec((T, RET_DIM), lambda b, h: (0, 0)), pl.BlockSpec((T, RET_DIM), lambda b, h: (0, 0)),
                  pl.BlockSpec((1, RET_DIM), lambda b, h: (0, h)),
                  pl.BlockSpec((None, N, RET_DIM), lambda b, h: (b, 0, h)),
                  pl.BlockSpec((None, N, RET_DIM), lambda b, h: (b, 0, h))],
        out_specs=(pl.BlockSpec((None, 4, T, RET_DIM), lambda b, h: (b, 0, 0, h)),
                   pl.BlockSpec((None, SUBLANES, RET_DIM), lambda b, h: (b, 0, h)),
                   pl.BlockSpec((None, None, SUBLANES, LANES), lambda b, h: (b, h, 0, 0))),
        scratch_shapes=[pltpu.VMEM((T, RET_DIM), BF16)] * 3 + [pltpu.VMEM((N, RET_DIM), BF16)]
                       + [pltpu.VMEM((T, RET_DIM), F32)] * 2
                       + [pltpu.VMEM((nt, RET_DIM, RET_DIM), BF16)] * 2 + [pltpu.VMEM((nt, RET_DIM, RET_DIM), F32)] * 2
                       + [pltpu.VMEM((nt, RET_DIM, RET_DIM), BF16)] * 2,
        args=(lg, proj, proj, proj, proj, cos, sin, gn, o, dlat))


def _na_geometry(rows):
    kh = min(NA_KH, rows)
    return kh, kh * GRID_W


def _pair_select():
    lane = lax.broadcasted_iota(jnp.int32, (2 * GRID_W, LANES), 1)
    row = lax.broadcasted_iota(jnp.int32, (2 * GRID_W, LANES), 0)
    return (lane >= NA_DIM) == (row >= GRID_W)


def _pair_bias(bias_ref, dr0, kh):
    return jnp.concatenate(
        [jnp.concatenate([bias_ref[e, pl.ds(dr0 + 2 * m, 1)].reshape(GRID_W, LANES) for m in range(kh // 2)], axis=1)
         for e in range(2)], axis=0)


def _na_softmax(s_loc, s_ctx):
    mx = jnp.maximum(jnp.max(s_loc, axis=-1, keepdims=True), jnp.max(s_ctx, axis=-1, keepdims=True))
    p_loc = jnp.exp(s_loc - mx)
    p_ctx = jnp.exp(s_ctx - mx)
    den = jnp.sum(p_loc, axis=-1, keepdims=True) + jnp.sum(p_ctx, axis=-1, keepdims=True)
    return p_loc, p_ctx, den


def _na_fwd(proj, bias2, n_ctx, hosted):
    B, T, _ = proj.shape
    C = n_ctx
    N = T - C
    R = N // GRID_W
    kh, nk = _na_geometry(R)
    scale = NA_DIM ** -0.5
    base = (4 * RET_WIDTH) // LANES

    def body(q_ref, k_ref, v_ref, bias_ref, out_ref, kb16, vb16):
        kb16[...] = k_ref[...].astype(BF16)
        vb16[...] = v_ref[...].astype(BF16)
        kc = kb16[0:C, :]
        vc = vb16[0:C, :]
        lane = lax.broadcasted_iota(jnp.int32, (GRID_W, LANES), 1)
        sel2 = _pair_select()

        def group(gi, carry):
            pre = []
            for u in range(NA_GROUP):
                r = gi * NA_GROUP + u
                bs = jnp.clip(r - kh // 2, 0, R - kh)
                dr0 = bs - r + (NA_KH - 1)
                q = q_ref[pl.ds(pl.multiple_of(C + r * GRID_W, GRID_W), GRID_W), :] * scale
                q2 = jnp.where(sel2, jnp.concatenate([q, q], axis=0), 0.0).astype(BF16)
                band = pl.ds(pl.multiple_of(C + bs * GRID_W, GRID_W), nk)
                s_loc = _dot_nt(q2, kb16[band, :]) + _pair_bias(bias_ref, dr0, kh)
                s_ctx = _dot_nt(q2, kc)
                pre.append((r, band, s_loc, s_ctx))
            mid = [(r, band) + _na_softmax(s_loc, s_ctx) for r, band, s_loc, s_ctx in pre]
            for r, band, p_loc, p_ctx, den in mid:
                o2 = (_dot(p_loc.astype(BF16), vb16[band, :]) + _dot(p_ctx.astype(BF16), vc)) / den
                out_ref[pl.ds(pl.multiple_of(r * GRID_W, GRID_W), GRID_W), :] = jnp.where(
                    lane < NA_DIM, o2[:GRID_W], o2[GRID_W:]).astype(BF16)
            return carry

        lax.fori_loop(0, R // NA_GROUP, group, 0)

    def col(seg):
        return pl.BlockSpec((None, T, LANES), lambda b, p, seg=seg: (b, 0, base + seg * NA_PAIRS + p))

    return _call_hosting(
        body, hosted, name="na_fwd", grid=(B, NA_PAIRS),
        out_shape=(jax.ShapeDtypeStruct((B, N, NA_WIDTH), BF16),),
        in_specs=[col(0), col(1), col(2),
                  pl.BlockSpec((2, 2 * NA_KH - 2, GRID_W, LANES), lambda b, p: (p, 0, 0, 0))],
        out_specs=(pl.BlockSpec((None, N, LANES), lambda b, p: (b, 0, p)),),
        scratch_shapes=[pltpu.VMEM((T, LANES), BF16)] * 2,
        args=(proj, proj, proj, bias2))


def _na_bwd(proj, bias2, dlat, n_ctx, hosted):
    B, T, _ = proj.shape
    C = n_ctx
    N = T - C
    R = N // GRID_W
    kh, nk = _na_geometry(R)
    scale = NA_DIM ** -0.5
    base = (4 * RET_WIDTH) // LANES

    def body(q_ref, k_ref, v_ref, bias_ref, dl_ref, d_ref, db_ref, kb16, vb16, dkv):
        b = pl.program_id(1)
        kb16[...] = k_ref[...].astype(BF16)
        vb16[...] = v_ref[...].astype(BF16)
        kc = kb16[0:C, :]
        vc = vb16[0:C, :]
        lane = lax.broadcasted_iota(jnp.int32, (GRID_W, LANES), 1)
        dkv[...] = jnp.zeros(dkv.shape, F32)
        d_ref[0, 0:C, :] = jnp.zeros((C, LANES), BF16)

        @pl.when(b == 0)
        def _():
            db_ref[...] = jnp.zeros(db_ref.shape, F32)

        sel2 = _pair_select()

        def group(gi, carry):
            pre = []
            for u in range(NA_GROUP):
                r = gi * NA_GROUP + u
                bs = jnp.clip(r - kh // 2, 0, R - kh)
                dr0 = bs - r + (NA_KH - 1)
                q = q_ref[pl.ds(pl.multiple_of(C + r * GRID_W, GRID_W), GRID_W), :] * scale
                do = dl_ref[pl.ds(pl.multiple_of(r * GRID_W, GRID_W), GRID_W), :]
                q2 = jnp.where(sel2, jnp.concatenate([q, q], axis=0), 0.0).astype(BF16)
                do2 = jnp.where(sel2, jnp.concatenate([do, do], axis=0), 0.0).astype(BF16)
                band = pl.ds(pl.multiple_of(C + bs * GRID_W, GRID_W), nk)
                s_loc = _dot_nt(q2, kb16[band, :]) + _pair_bias(bias_ref, dr0, kh)
                s_ctx = _dot_nt(q2, kc)
                dp_loc = _dot_nt(do2, vb16[band, :])
                dp_ctx = _dot_nt(do2, vc)
                pre.append((r, dr0, band, q2, do2, s_loc, s_ctx, dp_loc, dp_ctx))
            mid = []
            for r, dr0, band, q2, do2, s_loc, s_ctx, dp_loc, dp_ctx in pre:
                p_loc, p_ctx, den = _na_softmax(s_loc, s_ctx)
                inv = 1.0 / den
                p_loc = p_loc * inv
                p_ctx = p_ctx * inv
                delta = (jnp.sum(p_loc * dp_loc, axis=-1, keepdims=True)
                         + jnp.sum(p_ctx * dp_ctx, axis=-1, keepdims=True))
                ds_loc = p_loc * (dp_loc - delta)
                ds_ctx = p_ctx * (dp_ctx - delta)
                mid.append((r, dr0, band, q2, do2, p_loc.astype(BF16), p_ctx.astype(BF16), ds_loc, ds_ctx))
            for r, dr0, band, q2, do2, pb_loc, pb_ctx, ds_loc, ds_ctx in mid:
                dsb_loc = ds_loc.astype(BF16)
                dsb_ctx = ds_ctx.astype(BF16)
                dq2 = _dot(dsb_loc, kb16[band, :]) + _dot(dsb_ctx, kc)
                d_ref[0, pl.ds(pl.multiple_of(C + r * GRID_W, GRID_W), GRID_W), :] = (jnp.where(
                    lane < NA_DIM, dq2[:GRID_W], dq2[GRID_W:]) * scale).astype(BF16)
                dkv[0, band, :] += _dot_tn(dsb_loc, q2)
                dkv[1, band, :] += _dot_tn(pb_loc, do2)
                dkv[0, 0:C, :] += _dot_tn(dsb_ctx, q2)
                dkv[1, 0:C, :] += _dot_tn(pb_ctx, do2)
                for e in range(2):
                    for m in range(kh // 2):
                        db_ref[e, pl.ds(dr0 + 2 * m, 1)] += ds_loc[e * GRID_W:(e + 1) * GRID_W,
                                                                   m * LANES:(m + 1) * LANES].reshape(1, GRID_W, LANES)
            return carry

        lax.fori_loop(0, R // NA_GROUP, group, 0)
        d_ref[1] = dkv[0].astype(BF16)
        d_ref[2] = dkv[1].astype(BF16)

    def col(seg):
        return pl.BlockSpec((None, T, LANES), lambda p, b, seg=seg: (b, 0, base + seg * NA_PAIRS + p))

    return _call_hosting(
        body, hosted, name="na_bwd", grid=(NA_PAIRS, B),
        out_shape=(jax.ShapeDtypeStruct((B, 3, T, NA_WIDTH), BF16),
                   jax.ShapeDtypeStruct((NA_HEADS, 2 * NA_KH - 2, GRID_W, LANES), F32)),
        in_specs=[col(0), col(1), col(2),
                  pl.BlockSpec((2, 2 * NA_KH - 2, GRID_W, LANES), lambda p, b: (p, 0, 0, 0)),
                  pl.BlockSpec((None, N, LANES), lambda p, b: (b, 0, p))],
        out_specs=(pl.BlockSpec((None, 3, T, LANES), lambda p, b: (b, 0, 0, p)),
                   pl.BlockSpec((2, 2 * NA_KH - 2, GRID_W, LANES), lambda p, b: (p, 0, 0, 0))),
        scratch_shapes=[pltpu.VMEM((T, LANES), BF16)] * 2 + [pltpu.VMEM((2, T, LANES), F32)],
        args=(proj, proj, proj, bias2, dlat))


def _split3(a):
    hi = a.astype(BF16)
    r1 = a - hi.astype(F32)
    mid = r1.astype(BF16)
    lo = (r1 - mid.astype(F32)).astype(BF16)
    return hi, mid, lo


def _rpb_reduce(dbias2, onehot2):
    rows = dbias2.shape[0] * dbias2.shape[1]
    flat = dbias2.reshape(rows, GRID_W * LANES)

    def body(a_ref, oh_ref, o_ref):
        hi, mid, lo = _split3(a_ref[...])
        oh = oh_ref[...]
        o_ref[...] = _dot(hi, oh) + _dot(mid, oh) + _dot(lo, oh)

    return pl.pallas_call(
        body, name="rpb_reduce", out_shape=jax.ShapeDtypeStruct((rows, LANES), F32),
        in_specs=[_vmem(), _vmem()], out_specs=_vmem(),
        compiler_params=pltpu.CompilerParams(vmem_limit_bytes=VMEM_LIMIT),
    )(flat, onehot2)


def _dense_core(lat_ret, lat_na, x, tgt, modl, g_post_mix, g_pre_mlp, g_post_mlp, w_out, w1, w2):
    B, N, D = x.shape
    F = w1.shape[1]
    w2_rows = w2.shape[0] // N_DEV
    mixw = w_out.shape[0]
    half = mixw // 2
    tm = _div_tile(N, 256, 16)
    nt = N // tm
    fc = _div_tile(F, 1024, LANES)

    def body(lr_ref, ln_ref, x_ref, t_ref, gt1_ref, sh2_ref, sc2_ref, gt2_ref, gpm_ref, gpre_ref, gpo_ref,
             wout_hbm, w1_hbm, w2_part,
             dy1_ref, dlr_ref, dln_ref, dmix_ref, h2_ref, a_ref, du_ref, dz_ref, red_ref, w2_hbm,
             wout_v, w1_v, w2_v, u_s, sems, fsend, frecv):
        @pl.when((pl.program_id(0) == 0) & (pl.program_id(1) == 0))
        def _():
            relay = [(_row_block(w2_part, w2_rows), _row_block(w2_hbm, w2_rows))]
            _forward_start(relay, fsend, frecv)
            cps = [pltpu.make_async_copy(wout_hbm, wout_v, sems.at[0]),
                   pltpu.make_async_copy(w1_hbm, w1_v, sems.at[1])]
            for cp in cps:
                cp.start()
            _forward_wait(relay, fsend, frecv)
            cps.append(pltpu.make_async_copy(w2_hbm, w2_v, sems.at[2]))
            cps[2].start()
            for cp in cps:
                cp.wait()

        gt1 = gt1_ref[...]
        sh2 = sh2_ref[...]
        sc2 = sc2_ref[...]
        gt2 = gt2_ref[...]
        gpm = gpm_ref[...]
        gpre = gpre_ref[...]
        gpo = gpo_ref[...]

        def rowmean(a):
            return jnp.mean(a, axis=-1, keepdims=True)

        def colsum(a):
            return jnp.sum(a, axis=0, keepdims=True)

        mix = _dot(lr_ref[...], wout_v[0:half, :]) + _dot(ln_ref[...], wout_v[half:, :])
        x = x_ref[...]
        rm = lax.rsqrt(rowmean(mix * mix) + NORM_EPS)
        mh = mix * rm
        nm = mh * gpm
        y1 = x + gt1 * nm
        r1 = lax.rsqrt(rowmean(y1 * y1) + NORM_EPS)
        xh = y1 * r1
        n1 = xh * gpre
        h2b = (n1 * (1.0 + sc2) + sh2).astype(BF16)
        h2_ref[...] = h2b
        z = jnp.zeros((tm, D), F32)
        for c0 in range(0, F, fc):
            u = _dot(h2b, w1_v[:, c0:c0 + fc])
            u_s[:, c0:c0 + fc] = u
            ru = jnp.maximum(u, 0.0)
            ab = (ru * ru).astype(BF16)
            a_ref[:, c0:c0 + fc] = ab
            z = z + _dot(ab, w2_v[c0:c0 + fc, :])
        r2 = lax.rsqrt(rowmean(z * z) + NORM_EPS)
        zh = z * r2
        n2 = zh * gpo
        y2 = y1 + gt2 * n2
        err = y2 - t_ref[...]
        loss = 0.5 * jnp.sum(rowmean(err * err))
        dy2 = err * (1.0 / D)
        red_ref[2:3, :] = colsum(dy2 * n2)
        dn2 = dy2 * gt2
        red_ref[6:7, :] = colsum(dn2 * zh)
        dzh = dn2 * gpo
        dz = r2 * (dzh - zh * rowmean(dzh * zh))
        dzb = dz.astype(BF16)
        dz_ref[...] = dzb
        dh2 = jnp.zeros((tm, D), F32)
        for c0 in range(0, F, fc):
            da = _dot_nt(dzb, w2_v[c0:c0 + fc, :])
            dub = (da * (2.0 * jnp.maximum(u_s[:, c0:c0 + fc], 0.0))).astype(BF16)
            du_ref[:, c0:c0 + fc] = dub
            dh2 = dh2 + _dot_nt(dub, w1_v[:, c0:c0 + fc])
        red_ref[3:4, :] = colsum(dh2 * n1)
        red_ref[4:5, :] = colsum(dh2)
        dn1 = dh2 * (1.0 + sc2)
        red_ref[5:6, :] = colsum(dn1 * xh)
        dxh = dn1 * gpre
        dy1 = dy2 + r1 * (dxh - xh * rowmean(dxh * xh))
        dy1_ref[...] = dy1
        red_ref[0:1, :] = colsum(dy1 * nm)
        dnm = dy1 * gt1
        red_ref[1:2, :] = colsum(dnm * mh)
        dmh = dnm * gpm
        dmix = (rm * (dmh - mh * rowmean(dmh * mh))).astype(BF16)
        dmix_ref[...] = dmix
        dlr_ref[...] = _dot_nt(dmix, wout_v[0:half, :])
        dln_ref[...] = _dot_nt(dmix, wout_v[half:, :])
        red_ref[7:8, :] = jnp.zeros((1, D), F32) + loss

    def tok(w):
        return pl.BlockSpec((None, tm, w), lambda b, t: (b, t, 0))

    def mod(k):
        return pl.BlockSpec((None, None, 1, D), lambda b, t, k=k: (b, k, 0, 0))

    def vec():
        return pl.BlockSpec((1, D), lambda b, t: (0, 0))

    return pl.pallas_call(
        body, name="dense_core", grid=(B, nt),
        out_shape=(jax.ShapeDtypeStruct((B, N, D), F32), jax.ShapeDtypeStruct((B, N, half), F32),
                   jax.ShapeDtypeStruct((B, N, half), F32), jax.ShapeDtypeStruct((B, N, D), BF16),
                   jax.ShapeDtypeStruct((B, N, D), BF16), jax.ShapeDtypeStruct((B, N, F), BF16),
                   jax.ShapeDtypeStruct((B, N, F), BF16), jax.ShapeDtypeStruct((B, N, D), BF16),
                   jax.ShapeDtypeStruct((B, nt, SUBLANES, D), F32),
                   jax.ShapeDtypeStruct(w2.shape, w2.dtype)),
        in_specs=[tok(half), tok(half), tok(D), tok(D), mod(2), mod(3), mod(4), mod(5), vec(), vec(), vec(),
                  _any(), _any(), _any()],
        out_specs=(tok(D), tok(half), tok(half), tok(D), tok(D), tok(F), tok(F), tok(D),
                   pl.BlockSpec((None, None, SUBLANES, D), lambda b, t: (b, t, 0, 0)), _any()),
        scratch_shapes=[pltpu.VMEM((mixw, D), BF16), pltpu.VMEM((D, F), BF16), pltpu.VMEM((F, D), BF16),
                        pltpu.VMEM((tm, F), F32), pltpu.SemaphoreType.DMA((3,)),
                        pltpu.SemaphoreType.DMA((1, 3)), pltpu.SemaphoreType.DMA((1, 3))],
        input_output_aliases={13: 9},
        compiler_params=_params("arbitrary", "arbitrary"),
    )(lat_ret, lat_na, x, tgt, modl, modl, modl, modl, g_post_mix, g_pre_mlp, g_post_mlp, w_out, w1, w2)[:9]


def _inproj_bwd(dret, dna, x, ctx, dy1, modl, g1, w_in_t, hosted):
    B, N, D = x.shape
    n_ctx = ctx.shape[1]
    T = n_ctx + N
    tm = _div_tile(n_ctx, 256, 16)
    nct, ctx_spec, lat_spec = _token_tiles(n_ctx, tm)
    nt = T // tm
    nseg_r = dret.shape[1]
    nseg_n = dna.shape[1]
    nw = w_in_t.shape[0]

    def body(*refs):
        seg_refs = refs[:nseg_r + nseg_n]
        c_ref, x_ref, dy1_ref, sc_ref, g_ref, w_ref, dx_ref, red_ref = refs[nseg_r + nseg_n:]
        t = pl.program_id(1)
        dh = jnp.zeros((tm, D), F32)
        for s, ref in enumerate(seg_refs):
            dh = dh + _dot(ref[...], w_ref[s * SEG:(s + 1) * SEG, :])
        x = jnp.where(t < nct, c_ref[...], x_ref[...])
        g = g_ref[...]
        r = lax.rsqrt(jnp.mean(x * x, axis=-1, keepdims=True) + NORM_EPS)
        xh = x * r
        red_ref[0:1, :] = jnp.sum(dh, axis=0, keepdims=True)
        red_ref[1:2, :] = jnp.sum(dh * (xh * g), axis=0, keepdims=True)
        dn = dh * (1.0 + sc_ref[...])
        red_ref[2:3, :] = jnp.sum(dn * xh, axis=0, keepdims=True)
        red_ref[3:, :] = jnp.zeros((SUBLANES - 3, D), F32)
        dxh = dn * g
        dx = r * (dxh - xh * jnp.mean(dxh * xh, axis=-1, keepdims=True))
        dx_ref[...] = dx + jnp.where(t >= nct, dy1_ref[...], 0.0)

    def mrow(b, t):
        return jnp.where(t < nct, B, b)

    def seg(s):
        return pl.BlockSpec((None, None, tm, SEG), lambda b, t, s=s: (b, s, t, 0))

    return _call_hosting(
        body, hosted, name="inproj_bwd", grid=(B, nt),
        out_shape=(jax.ShapeDtypeStruct((B, N, D), F32), jax.ShapeDtypeStruct((B, nt, SUBLANES, D), F32)),
        in_specs=[seg(s) for s in range(nseg_r)] + [seg(s) for s in range(nseg_n)]
                 + [ctx_spec(D), lat_spec(D), lat_spec(D),
                    pl.BlockSpec((None, None, 1, D), lambda b, t: (mrow(b, t), 1, 0, 0)),
                    pl.BlockSpec((1, D), lambda b, t: (0, 0)),
                    pl.BlockSpec((nw, D), lambda b, t: (0, 0))],
        out_specs=(lat_spec(D), pl.BlockSpec((None, None, SUBLANES, D), lambda b, t: (b, t, 0, 0))),
        scratch_shapes=[], args=(*([dret] * nseg_r), *([dna] * nseg_n), ctx, x, dy1, modl, g1, w_in_t))


def _tn_matmul(lhs, rhs, name):
    B, S, T, W = lhs.shape
    nn = rhs.shape[-1]
    tk = _div_tile(T, 1024, LANES)
    bm = _div_tile(W, 1024, LANES)
    bn = _div_tile(nn, 1024, LANES)
    nkt = T // tk
    nk = B * nkt

    def body(l_ref, r_ref, o_ref, acc):
        k = pl.program_id(3)

        @pl.when(k == 0)
        def _():
            acc[...] = jnp.zeros(acc.shape, F32)

        acc[...] += _dot_tn(l_ref[...].astype(BF16), r_ref[...].astype(BF16))

        @pl.when(k == nk - 1)
        def _():
            o_ref[...] = acc[...].astype(BF16)

    nwb = W // bm
    return pl.pallas_call(
        functools.partial(body), name=name, grid=(S, nwb, nn // bn, nk),
        out_shape=jax.ShapeDtypeStruct((S * W, nn), BF16),
        in_specs=[pl.BlockSpec((None, None, tk, bm), lambda s, i, j, k: (k // nkt, s, k % nkt, i)),
                  pl.BlockSpec((None, tk, bn), lambda s, i, j, k: (k // nkt, k % nkt, j))],
        out_specs=pl.BlockSpec((bm, bn), lambda s, i, j, k: (s * nwb + i, j)),
        scratch_shapes=[pltpu.VMEM((bm, bn), F32)],
        compiler_params=_params("parallel", "parallel", "parallel", "arbitrary"),
    )(lhs, rhs)


def _scatter_start(g, block_of, block_shape, name):
    land_shape = (N_DEV,) + tuple(block_shape)
    hbm = pl.BlockSpec(memory_space=pltpu.HBM)
    sem = pl.BlockSpec(memory_space=pltpu.SEMAPHORE)

    def body(g_ref, land_ref, send_sems, recv_sems, g_thru, land_thru, token):
        me, peers = _me_and_peers()
        src = block_of(g_ref)
        for i, (dev, pid) in enumerate(peers):
            _remote(src(pid), land_ref.at[me], send_sems.at[i], recv_sems.at[i], dev).start()
        token[...] = jnp.zeros_like(token)

    return pl.pallas_call(
        body, name=name,
        out_shape=(pltpu.SemaphoreType.DMA((N_DEV - 1,)), pltpu.SemaphoreType.DMA((N_DEV - 1,)),
                   pltpu.HBM(g.shape, g.dtype), pltpu.HBM(land_shape, g.dtype),
                   jax.ShapeDtypeStruct((SUBLANES, LANES), F32)),
        in_specs=(hbm, hbm), out_specs=(sem, sem, hbm, hbm, _vmem()), input_output_aliases={0: 2, 1: 3},
        compiler_params=pltpu.CompilerParams(has_side_effects=pltpu.SideEffectType.DATAFLOW_SIDE_EFFECTING),
    )(pltpu.with_memory_space_constraint(g, pltpu.HBM),
      pltpu.with_memory_space_constraint(lax.empty(land_shape, g.dtype), pltpu.HBM))


def _scatter_wait(send_sems, recv_sems, g_thru, land_thru, after, block_of, name):
    hbm = pl.BlockSpec(memory_space=pltpu.HBM)
    sem = pl.BlockSpec(memory_space=pltpu.SEMAPHORE)

    def body(g_ref, land_ref, send_sems, recv_sems, after_ref, g_out, land_out):
        me, peers = _me_and_peers()
        src = block_of(g_ref)
        for i, (dev, pid) in enumerate(peers):
            cp = _remote(src(pid), land_ref.at[pid], send_sems.at[i], recv_sems.at[i], dev)
            cp.wait_send()
            cp.wait_recv()

    return pl.pallas_call(
        body, name=name,
        out_shape=(pltpu.HBM(g_thru.shape, g_thru.dtype), pltpu.HBM(land_thru.shape, land_thru.dtype)),
        in_specs=(hbm, hbm, sem, sem, pl.BlockSpec(memory_space=pl.ANY)), out_specs=(hbm, hbm),
        input_output_aliases={0: 0, 1: 1},
        compiler_params=pltpu.CompilerParams(has_side_effects=pltpu.SideEffectType.DATAFLOW_SIDE_EFFECTING),
    )(g_thru, land_thru, send_sems, recv_sems, after)


def _sum_slots(buf, name):
    _, rows, cols = buf.shape
    tr = _div_tile(rows, 256, 2 * SUBLANES)

    def body(b_ref, o_ref):
        acc = b_ref[0].astype(F32)
        for k in range(1, N_DEV):
            acc = acc + b_ref[k].astype(F32)
        o_ref[...] = acc

    return pl.pallas_call(
        functools.partial(body), name=name, grid=(rows // tr,),
        out_shape=jax.ShapeDtypeStruct((rows, cols), F32),
        in_specs=[pl.BlockSpec((N_DEV, tr, cols), lambda i: (0, i, 0))],
        out_specs=pl.BlockSpec((tr, cols), lambda i: (i, 0)),
        compiler_params=_params("parallel"),
    )(buf)


def _small_ar(vec, dmods, silu_all, w_ada, c_ctx):
    rv = vec.shape[0]
    D = silu_all.shape[1]
    ncol = w_ada.shape[1]
    nm = dmods.shape[1]
    srows = silu_all.shape[0]

    def body(vec_ref, dm_ref, s_ref, w_ref, cc_ref, tot_ref, gb_ref, gw_ref, gc_ref,
             vbuf, mbuf, tbuf, dmx, send1, recv1, send3, recv3):
        me, _ = _me_and_peers()
        vbuf[me] = vec_ref[...]
        mbuf[me] = dm_ref[...]
        both = [(lambda p: vbuf.at[me], lambda p: vbuf.at[p]), (lambda p: mbuf.at[me], lambda p: mbuf.at[p])]
        _push_start(both, ALL_PEERS, send1, recv1)
        _push_wait_recv(both, ALL_PEERS, send1, recv1)
        _push_wait_send(both, ALL_PEERS, send1, recv1)
        tot = vbuf[0]
        msum = mbuf[0]
        for k in range(1, N_DEV):
            tot = tot + vbuf[k]
            msum = msum + mbuf[k]
        tot_ref[...] = tot
        gb_ref[...] = jnp.sum(msum, axis=0, keepdims=True)
        loc = pl.ds(pl.multiple_of(me * ncol, ncol), ncol)
        for k in range(N_DEV):
            dmx[k * SUBLANES:(k + 1) * SUBLANES, :] = mbuf[k, :, loc]
        cm = msum[2:3, :]
        mbuf[0, 2:3, :] = cm
        cm_loc = mbuf[0, 2:3, loc]
        dmx[N_DEV * SUBLANES:, :] = jnp.concatenate([cm_loc, jnp.zeros((SUBLANES - 1, ncol), F32)], axis=0)
        gw_ref[...] = _dot_tn(s_ref[...], dmx[...])
        tbuf[me] = _dot_nt(dmx[N_DEV * SUBLANES:, :], w_ref[...])
        _exchange(lambda p: tbuf.at[me], lambda p: tbuf.at[p], send3, recv3)
        tsum = tbuf[0]
        for k in range(1, N_DEV):
            tsum = tsum + tbuf[k]
        cc = cc_ref[...]
        sg = _sigmoid(cc)
        gc_ref[...] = tsum[0:1, :] * (sg * (1.0 + cc * (1.0 - sg)))

    return pl.pallas_call(
        body, name="small_ar",
        out_shape=(jax.ShapeDtypeStruct((rv, LANES), F32), jax.ShapeDtypeStruct((1, nm), F32),
                   jax.ShapeDtypeStruct((D, ncol), F32), jax.ShapeDtypeStruct((1, D), F32)),
        in_specs=[_vmem()] * 5, out_specs=(_vmem(),) * 4,
        scratch_shapes=[pltpu.VMEM((N_DEV, rv, LANES), F32), pltpu.VMEM((N_DEV, SUBLANES, nm), F32),
                        pltpu.VMEM((N_DEV, SUBLANES, D), F32), pltpu.VMEM((srows, ncol), F32)]
                       + [pltpu.SemaphoreType.DMA((2, N_DEV - 1))] * 2 + [pltpu.SemaphoreType.DMA((N_DEV - 1,))] * 2,
        compiler_params=pltpu.CompilerParams(vmem_limit_bytes=VMEM_LIMIT),
    )(vec, dmods, silu_all, w_ada, c_ctx.reshape(1, D))


def _adam_update(w, g, m, v):
    mn = ADAM_B1 * m + (1.0 - ADAM_B1) * g
    vn = ADAM_B2 * v + (1.0 - ADAM_B2) * (g * g)
    m_hat = mn / (1.0 - ADAM_B1 ** ADAM_STEP)
    v_hat = vn / (1.0 - ADAM_B2 ** ADAM_STEP)
    return -ADAM_LR * (m_hat / (jnp.sqrt(v_hat) + ADAM_EPS) + ADAM_WD * w), mn, vn


def _adamw(w, g, m, v, name):
    rows, cols = w.shape
    tr = _div_tile(rows, 256, SUBLANES) if rows * cols > 65536 else rows

    def body(w_ref, g_ref, m_ref, v_ref, d_ref, nm_ref, nv_ref):
        d_ref[...], nm_ref[...], nv_ref[...] = _adam_update(w_ref[...], g_ref[...], m_ref[...], v_ref[...])

    spec = pl.BlockSpec((tr, cols), lambda i: (i, 0))
    return pl.pallas_call(
        functools.partial(body), name=name, grid=(rows // tr,),
        out_shape=(jax.ShapeDtypeStruct((rows, cols), F32),) * 3,
        in_specs=[spec] * 4, out_specs=(spec,) * 3,
        compiler_params=_params("parallel"),
    )(w, g, m, v)


def _sum_adamw(buf, w, m, v, name):
    _, rows, cols = buf.shape
    tr = _div_tile(rows, 256, 2 * SUBLANES)

    def body(b_ref, w_ref, m_ref, v_ref, g_ref, d_ref, nm_ref, nv_ref):
        g = b_ref[0].astype(F32)
        for k in range(1, N_DEV):
            g = g + b_ref[k].astype(F32)
        g_ref[...] = g
        d_ref[...], nm_ref[...], nv_ref[...] = _adam_update(w_ref[...], g, m_ref[...], v_ref[...])

    spec = pl.BlockSpec((tr, cols), lambda i: (i, 0))
    return pl.pallas_call(
        functools.partial(body), name=name, grid=(rows // tr,),
        out_shape=(jax.ShapeDtypeStruct((rows, cols), F32),) * 4,
        in_specs=[pl.BlockSpec((N_DEV, tr, cols), lambda i: (0, i, 0))] + [spec] * 3, out_specs=(spec,) * 4,
        compiler_params=_params("parallel"),
    )(buf, w, m, v)


def _rope_tables(n_ctx, n):
    n_freq = RET_DIM // 4
    inv = np.float32(ROPE_BASE) ** (-np.arange(n_freq, dtype=np.float32) / np.float32(n_freq))
    tok = np.arange(n)
    pos_r = (tok // GRID_W).astype(np.float32)
    pos_c = (tok % GRID_W).astype(np.float32)
    ang_r = (pos_r[:, None] * inv[None, :]).astype(np.float32)
    ang_c = (pos_c[:, None] * inv[None, :]).astype(np.float32)
    cos = np.concatenate([np.cos(ang_r), np.cos(ang_r), np.cos(ang_c), np.cos(ang_c)], axis=-1)
    sin = np.concatenate([-np.sin(ang_r), np.sin(ang_r), -np.sin(ang_c), np.sin(ang_c)], axis=-1)
    cos = np.concatenate([np.ones((n_ctx, RET_DIM), np.float32), cos], axis=0)
    sin = np.concatenate([np.zeros((n_ctx, RET_DIM), np.float32), sin], axis=0)
    return jnp.asarray(cos, F32), jnp.asarray(sin, F32)


def _na_tables():
    q = np.arange(GRID_W)[:, None]
    k = np.arange(GRID_W)[None, :]
    start = np.clip(q - NA_KW // 2, 0, GRID_W - NA_KW)
    valid = (k >= start) & (k < start + NA_KW)
    dc = np.clip(k - q + (NA_KW - 1), 0, 2 * NA_KW - 2)
    ncls = 2 * NA_KW - 1
    onehot = (dc[None] == np.arange(ncls)[:, None, None]) & valid[None]
    oh2 = np.zeros((GRID_W, LANES, LANES), np.float32)
    for c in range(ncls):
        oh2[:, :GRID_W, c] = onehot[c]
        oh2[:, GRID_W:, 32 + c] = onehot[c]
    return onehot.astype(np.float32), valid, oh2.reshape(GRID_W * LANES, LANES)


def _paired_bias(rpb, onehot, valid):
    t = jnp.einsum("hdc,cqk->hdqk", rpb, jnp.asarray(onehot), precision=lax.Precision.HIGHEST)
    t = jnp.where(jnp.asarray(valid)[None, None], t, NEG_INF)
    return jnp.concatenate([t[:, :-1], t[:, 1:]], axis=-1)


def kernel(x, c, ctx, c_ctx, w_ada, b_ada, g_pre_mix, g_post_mix, g_pre_mlp, g_post_mlp, w_in, ret_decay, ret_gn, na_rpb, w_out, w_mlp1, w_mlp2, loss_target, m_c_ctx, m_w_ada, m_b_ada, m_g_pre_mix, m_g_post_mix, m_g_pre_mlp, m_g_post_mlp, m_w_in, m_ret_decay, m_ret_gn, m_na_rpb, m_w_out, m_w_mlp1, m_w_mlp2, v_c_ctx, v_w_ada, v_b_ada, v_g_pre_mix, v_g_post_mix, v_g_pre_mlp, v_g_post_mlp, v_w_in, v_ret_decay, v_ret_gn, v_na_rpb, v_w_out, v_w_mlp1, v_w_mlp2):
    B, N, D = x.shape
    C = ctx.shape[1]
    T = C + N

    silu_all, mods_g, win_b, wout_l, w1_l, w2_l = _mod_gather(c, c_ctx, w_ada[0], b_ada, w_in[0].T, w_out[0],
                                                             w_mlp1[0], w_mlp2[0])
    mods_mine = mods_g.transpose(1, 0, 2).reshape(mods_g.shape[1], N_MOD * D)
    modl = jnp.concatenate([mods_mine[:B], mods_mine[SUBLANES:SUBLANES + 1]], axis=0)
    modl = modl.reshape(B + 1, N_MOD, 1, D)
    rin = w_in.shape[2]
    rout, c1, r2 = wout_l.shape[0], w1_l.shape[1], w2_l.shape[0]

    def rows_of(n):
        return lambda ref: _row_block(ref, n)

    def cols_of(n):
        return lambda ref: _col_block(ref, n)

    cos, sin = _rope_tables(C, N)
    onehot, valid, oh2 = _na_tables()
    bias2 = _paired_bias(na_rpb[0], onehot, valid)
    lg = jax.nn.log_sigmoid(ret_decay[0].astype(F32))

    level_one = SIBLING + ICI_SAME_CORE
    h_all, proj, w1_part = _inproj_fwd(
        x, ctx, modl, g_pre_mix, win_b,
        [_Hosted("gather", level_one, [w1_l], [cols_of(c1)], [jax.ShapeDtypeStruct((D, N_DEV * c1), BF16)], True)])
    o_ret, lat_ret, wout_b = _ret_fwd(
        proj, cos, sin, lg, ret_gn, C,
        [_Hosted("gather", ALL_PEERS, [wout_l], [rows_of(rout)], [jax.ShapeDtypeStruct((N_DEV * rout, D), BF16)], True)])
    lat_na, w1_b, w2_part = _na_fwd(
        proj, bias2, C,
        [_HostedRelay([w1_part], [cols_of(c1)]),
         _Hosted("gather", level_one, [w2_l], [rows_of(r2)], [jax.ShapeDtypeStruct((N_DEV * r2, D), BF16)], True)])

    (dy1, dlat_ret, dlat_na, dmix, h2, act, du, dz, red_d) = _dense_core(
        lat_ret, lat_na, x, loss_target, modl, g_post_mix, g_pre_mlp, g_post_mlp, wout_b, w1_b, w2_part)

    gw_out_p = jnp.concatenate([_tn_matmul(lat_ret[:, None], dmix, "gw_out_ret"),
                                _tn_matmul(lat_na[:, None], dmix, "gw_out_na")], axis=0)
    gw1_p = _tn_matmul(h2[:, None], du, "gw_mlp1")
    gw2_p = _tn_matmul(act[:, None], dz, "gw_mlp2")
    rs_out = _scatter_start(gw_out_p, rows_of(rout), (rout, D), "rs_w_out_start")
    rs_1 = _scatter_start(gw1_p, cols_of(c1), (D, c1), "rs_w_mlp1_start")
    rs_2 = _scatter_start(gw2_p, rows_of(r2), (r2, D), "rs_w_mlp2_start")
    started = rs_out[4][0, 0] + rs_1[4][0, 0] + rs_2[4][0, 0]

    dret, dgn_p, dlg_p = _ret_bwd(proj, cos, sin, lg, ret_gn + started, o_ret, dlat_ret, C, [])
    dna, dbias2 = _na_bwd(proj, bias2, dlat_na, C, [])
    gwin_t_p = jnp.concatenate([_tn_matmul(dret, h_all, "gw_in_ret"), _tn_matmul(dna, h_all, "gw_in_na")], axis=0)
    rs_in = _scatter_start(gwin_t_p, rows_of(rin), (rin, D), "rs_w_in_start")
    grad_x, red_i = _inproj_bwd(dret, dna, x, ctx, dy1, modl, g_pre_mix + rs_in[4][0, 0], win_b, [])

    rd = red_d.sum(axis=1)[:, :, :]
    ri = red_i
    nct = ri.shape[1] * C // T
    ri_ctx = ri[:, :nct].sum(axis=(0, 1))
    ri_lat = ri[:, nct:].sum(axis=1)
    d_mods = jnp.concatenate([ri_lat[:, 0], ri_lat[:, 1], rd[:, 0], rd[:, 4], rd[:, 3], rd[:, 2]], axis=-1)
    d_cmods = jnp.concatenate([ri_ctx[0], ri_ctx[1], jnp.zeros(((N_MOD - 2) * D,), F32)])[None]
    dm_slot = jnp.concatenate([d_mods, d_cmods, jnp.zeros((SUBLANES - B - 1, N_MOD * D), F32)], axis=0)
    dg_pre_mix = ri_lat[:, 2].sum(axis=0) + ri_ctx[2]
    dg_post_mix = rd[:, 1].sum(axis=0)
    dg_pre_mlp = rd[:, 5].sum(axis=0)
    dg_post_mlp = rd[:, 6].sum(axis=0)
    loss_p = rd[:, 7, 0].sum()
    d_gn = dgn_p[:, 0].sum(axis=0)
    d_lg = dlg_p[:, :, :2, 0].sum(axis=0).T
    d_decay = d_lg * jax.nn.sigmoid(-ret_decay[0].astype(F32))
    rr = _rpb_reduce(dbias2, jnp.asarray(oh2, BF16)).reshape(NA_HEADS, 2 * NA_KH - 2, LANES)
    ncls = 2 * NA_KW - 1
    d_rpb = (jnp.pad(rr[:, :, :ncls], ((0, 0), (0, 1), (0, 0))) + jnp.pad(rr[:, :, 32:32 + ncls], ((0, 0), (1, 0), (0, 0))))
    d_rpb32 = jnp.pad(d_rpb, ((0, 0), (0, 0), (0, 32 - ncls)))
    pieces = [dg_pre_mix, dg_post_mix, dg_pre_mlp, dg_post_mlp, d_gn, d_rpb32.reshape(-1),
              jnp.pad(d_decay.reshape(-1), (0, LANES - d_decay.size)), jnp.full((LANES,), loss_p, F32)]
    vec = jnp.concatenate(pieces)
    pad = (-vec.shape[0]) % (SUBLANES * LANES)
    vec = jnp.pad(vec, (0, pad)).reshape(-1, LANES)
    tot, g_b_ada, g_w_ada, g_c_ctx = _small_ar(vec, dm_slot, silu_all, w_ada[0], c_ctx)
    me = 4 * lax.axis_index("x") + 2 * lax.axis_index("y") + lax.axis_index("c")

    def landed(parts, block_of, size, axis, name):
        done, land = _scatter_wait(*parts[:4], tot, block_of, name)
        own = lax.dynamic_slice_in_dim(done, me * size, size, axis=axis)
        return lax.dynamic_update_slice_in_dim(land, own[None], me, axis=0)

    g_w_in = _sum_slots(landed(rs_in, rows_of(rin), rin, 0, "rs_w_in_wait"), "sum_w_in").T
    fused = {"w_out": _sum_adamw(landed(rs_out, rows_of(rout), rout, 0, "rs_w_out_wait"),
                                 w_out[0], m_w_out[0], v_w_out[0], "sum_adamw_w_out"),
             "w_mlp1": _sum_adamw(landed(rs_1, cols_of(c1), c1, 1, "rs_w_mlp1_wait"),
                                  w_mlp1[0], m_w_mlp1[0], v_w_mlp1[0], "sum_adamw_w_mlp1"),
             "w_mlp2": _sum_adamw(landed(rs_2, rows_of(r2), r2, 0, "rs_w_mlp2_wait"),
                                  w_mlp2[0], m_w_mlp2[0], v_w_mlp2[0], "sum_adamw_w_mlp2")}
    flat = tot.reshape(-1)
    o0 = 0
    g_pre_mix_g = flat[o0:o0 + D]; o0 += D
    g_post_mix_g = flat[o0:o0 + D]; o0 += D
    g_pre_mlp_g = flat[o0:o0 + D]; o0 += D
    g_post_mlp_g = flat[o0:o0 + D]; o0 += D
    g_gn = flat[o0:o0 + RET_WIDTH]; o0 += RET_WIDTH
    nrpb = NA_HEADS * (2 * NA_KH - 1) * 32
    g_rpb = flat[o0:o0 + nrpb].reshape(NA_HEADS, 2 * NA_KH - 1, 32)[:, :, :ncls]; o0 += nrpb
    g_decay = flat[o0:o0 + 2 * RET_HEADS].reshape(2, RET_HEADS); o0 += LANES
    loss = flat[o0]

    grads = {
        "c_ctx": g_c_ctx.reshape(c_ctx.shape), "w_ada": g_w_ada[None], "b_ada": g_b_ada.reshape(b_ada.shape),
        "g_pre_mix": g_pre_mix_g[None], "g_post_mix": g_post_mix_g[None], "g_pre_mlp": g_pre_mlp_g[None],
        "g_post_mlp": g_post_mlp_g[None], "w_in": g_w_in[None], "ret_decay": g_decay[None], "ret_gn": g_gn[None],
        "na_rpb": g_rpb[None], "w_out": fused["w_out"][0][None], "w_mlp1": fused["w_mlp1"][0][None],
        "w_mlp2": fused["w_mlp2"][0][None],
    }
    weights = dict(c_ctx=c_ctx, w_ada=w_ada, b_ada=b_ada, g_pre_mix=g_pre_mix, g_post_mix=g_post_mix,
                   g_pre_mlp=g_pre_mlp, g_post_mlp=g_post_mlp, w_in=w_in, ret_decay=ret_decay, ret_gn=ret_gn,
                   na_rpb=na_rpb, w_out=w_out, w_mlp1=w_mlp1, w_mlp2=w_mlp2)
    m_in = dict(c_ctx=m_c_ctx, w_ada=m_w_ada, b_ada=m_b_ada, g_pre_mix=m_g_pre_mix, g_post_mix=m_g_post_mix,
                g_pre_mlp=m_g_pre_mlp, g_post_mlp=m_g_post_mlp, w_in=m_w_in, ret_decay=m_ret_decay,
                ret_gn=m_ret_gn, na_rpb=m_na_rpb, w_out=m_w_out, w_mlp1=m_w_mlp1, w_mlp2=m_w_mlp2)
    v_in = dict(c_ctx=v_c_ctx, w_ada=v_w_ada, b_ada=v_b_ada, g_pre_mix=v_g_pre_mix, g_post_mix=v_g_post_mix,
                g_pre_mlp=v_g_pre_mlp, g_post_mlp=v_g_post_mlp, w_in=v_w_in, ret_decay=v_ret_decay,
                ret_gn=v_ret_gn, na_rpb=v_na_rpb, w_out=v_w_out, w_mlp1=v_w_mlp1, w_mlp2=v_w_mlp2)
    names = list(weights)
    deltas, new_m, new_v = {}, {}, {}
    for n in names:
        shp = weights[n].shape
        if n in fused:
            deltas[n], new_m[n], new_v[n] = (a.reshape(shp) for a in fused[n][1:])
            continue
        two_d = (-1, shp[-1]) if len(shp) > 1 else (1, shp[0])
        d, nm, nv = _adamw(weights[n].reshape(two_d), grads[n].reshape(two_d), m_in[n].reshape(two_d),
                           v_in[n].reshape(two_d), "adamw_" + n)
        deltas[n], new_m[n], new_v[n] = d.reshape(shp), nm.reshape(shp), nv.reshape(shp)
    return (loss, grad_x, *[grads[n] for n in names], *[deltas[n] for n in names],
            *[new_m[n] for n in names], *[new_v[n] for n in names])
```

```python
import functools
import math

import numpy as np
import jax
import jax.numpy as jnp
from jax import lax
from jax.experimental import pallas as pl
from jax.experimental.pallas import tpu as pltpu

F32 = jnp.float32
BF16 = jnp.bfloat16
MESH = pl.DeviceIdType.MESH

N_DEV = 8
LANES = 128
SUBLANES = 8
VMEM_LIMIT = 60 * 1024 * 1024

GRID_W = 64
RET_HEADS = 4
RET_DIM = 128
RET_WIDTH = RET_HEADS * RET_DIM
NA_HEADS = 8
NA_DIM = 64
NA_WIDTH = NA_HEADS * NA_DIM
NA_PAIRS = NA_HEADS // 2
NA_KH = 8
NA_KW = 16
NA_GROUP = 8
SEG = 512
ROPE_BASE = 10000.0
NORM_EPS = 1e-6
NEG_INF = -1e30
N_MOD = 6

ADAM_LR = 0.001
ADAM_B1 = 0.9
ADAM_B2 = 0.999
ADAM_EPS = 1e-08
ADAM_WD = 0.01
ADAM_STEP = 10


def _dot(a, b):
    return lax.dot_general(a, b, (((1,), (0,)), ((), ())), preferred_element_type=F32)


def _dot_nt(a, b):
    return lax.dot_general(a, b, (((1,), (1,)), ((), ())), preferred_element_type=F32)


def _dot_tn(a, b):
    return lax.dot_general(a, b, (((0,), (0,)), ((), ())), preferred_element_type=F32)


def _sigmoid(x):
    return 1.0 / (1.0 + jnp.exp(-x))


def _div_tile(n, cap, mult):
    if n <= cap:
        return n
    for t in range(cap - cap % mult, 0, -mult):
        if n % t == 0:
            return t
    raise ValueError(f"no tile for {n}")


def _params(*sem):
    return pltpu.CompilerParams(dimension_semantics=tuple(sem) if sem else None,
                                vmem_limit_bytes=VMEM_LIMIT)


def _vmem():
    return pl.BlockSpec(memory_space=pltpu.VMEM)


def _any():
    return pl.BlockSpec(memory_space=pl.ANY)


def _me_and_peers():
    x, y, c = lax.axis_index("x"), lax.axis_index("y"), lax.axis_index("c")
    me = 4 * x + 2 * y + c
    peers = []
    for m in range(1, N_DEV):
        px = 1 - x if (m >> 2) & 1 else x
        py = 1 - y if (m >> 1) & 1 else y
        pc = 1 - c if m & 1 else c
        peers.append(((px, py, pc), 4 * px + 2 * py + pc))
    return me, peers


def _exchange(src_for, dst_from, send_sems, recv_sems):
    me, peers = _me_and_peers()
    sent = []
    for i, (dev, pid) in enumerate(peers):
        cp = pltpu.make_async_remote_copy(src_ref=src_for(pid), dst_ref=dst_from(me),
                                          send_sem=send_sems.at[i], recv_sem=recv_sems.at[i],
                                          device_id=dev, device_id_type=MESH)
        cp.start()
        sent.append(cp)
    for i, (dev, pid) in enumerate(peers):
        pltpu.make_async_remote_copy(src_ref=src_for(pid), dst_ref=dst_from(pid),
                                     send_sem=send_sems.at[i], recv_sem=recv_sems.at[i],
                                     device_id=dev, device_id_type=MESH).wait_recv()
    for cp in sent:
        cp.wait_send()


SIBLING = (1,)
ICI_SAME_CORE = (2, 4, 6)
ALL_PEERS = tuple(range(1, N_DEV))


def _remote(src, dst, send_sem, recv_sem, dev):
    return pltpu.make_async_remote_copy(src_ref=src, dst_ref=dst, send_sem=send_sem, recv_sem=recv_sem,
                                        device_id=dev, device_id_type=MESH)


def _push_start(items, masks, send_sems, recv_sems):
    me, peers = _me_and_peers()
    for k, (src_for, dst_from) in enumerate(items):
        for m in masks:
            dev, pid = peers[m - 1]
            _remote(src_for(pid), dst_from(me), send_sems.at[k, m - 1], recv_sems.at[k, m - 1], dev).start()


def _push_wait_recv(items, masks, send_sems, recv_sems):
    me, peers = _me_and_peers()
    for k, (src_for, dst_from) in enumerate(items):
        for m in masks:
            dev, pid = peers[m - 1]
            _remote(src_for(pid), dst_from(pid), send_sems.at[k, m - 1], recv_sems.at[k, m - 1], dev).wait_recv()


def _push_wait_send(items, masks, send_sems, recv_sems):
    me, peers = _me_and_peers()
    for k, (src_for, dst_from) in enumerate(items):
        for m in masks:
            dev, pid = peers[m - 1]
            _remote(src_for(pid), dst_from(me), send_sems.at[k, m - 1], recv_sems.at[k, m - 1], dev).wait_send()


def _forward_start(items, send_sems, recv_sems):
    me, peers = _me_and_peers()
    sib = peers[0][0]
    for k, (blk_in, blk_out) in enumerate(items):
        for j, m in enumerate(ICI_SAME_CORE):
            pid = peers[m - 1][1]
            _remote(blk_in(pid), blk_out(pid), send_sems.at[k, j], recv_sems.at[k, j], sib).start()


def _forward_wait(items, send_sems, recv_sems):
    me, peers = _me_and_peers()
    sib = peers[0][0]
    for k, (blk_in, blk_out) in enumerate(items):
        for j, m in enumerate(ICI_SAME_CORE):
            got = peers[(m | 1) - 1][1]
            _remote(blk_in(got), blk_out(got), send_sems.at[k, j], recv_sems.at[k, j], sib).wait_recv()
    for k, (blk_in, blk_out) in enumerate(items):
        for j, m in enumerate(ICI_SAME_CORE):
            pid = peers[m - 1][1]
            _remote(blk_in(pid), blk_out(pid), send_sems.at[k, j], recv_sems.at[k, j], sib).wait_send()


def _mod_gather(c, c_ctx, w_ada, b_ada, w_in_t, w_out, w1, w2):
    B, D = c.shape
    ncol = w_ada.shape[1]
    rows = SUBLANES * N_DEV + SUBLANES

    def body(c_ref, cc_ref, w_ref, b_ref, win_ref, wout_ref, w1_ref, w2_ref,
             s_ref, m_ref, gin_ref, wout_b, w1_b, w2_b,
             win_b, msend, send1, recv1, send2, recv2, wsend, wrecv, fsend, frecv, lsem):
        me, _ = _me_and_peers()
        win_b[...] = win_ref[...].astype(BF16)
        block = _row_block(gin_ref, w_in_t.shape[0])
        gather = [(lambda p: win_b, block)]
        own = pltpu.make_async_copy(win_b, block(me), lsem.at[0])
        own.start()
        _push_start(gather, SIBLING + ICI_SAME_CORE, wsend, wrecv)
        wout_b[...] = wout_ref[...].astype(BF16)
        w1_b[...] = w1_ref[...].astype(BF16)
        w2_b[...] = w2_ref[...].astype(BF16)
        cv = c_ref[...]
        slot = jnp.concatenate([cv * _sigmoid(cv), jnp.zeros((SUBLANES - B, D), F32)], axis=0)
        my_rows = pl.ds(pl.multiple_of(me * SUBLANES, SUBLANES), SUBLANES)
        s_ref[my_rows, :] = slot
        ccv = cc_ref[...]
        s_ref[SUBLANES * N_DEV:, :] = jnp.concatenate(
            [ccv * _sigmoid(ccv), jnp.zeros((SUBLANES - 1, D), F32)], axis=0)

        def rows_of(p):
            return s_ref.at[pl.ds(pl.multiple_of(p * SUBLANES, SUBLANES), SUBLANES), :]

        _exchange(lambda p: rows_of(me), rows_of, send1, recv1)
        b_loc = b_ref[:, pl.ds(pl.multiple_of(me * ncol, ncol), ncol)]
        mods = _dot(s_ref[...], w_ref[...]) + b_loc
        for p in range(N_DEV):
            msend[p] = jnp.concatenate([mods[p * SUBLANES:(p + 1) * SUBLANES], mods[N_DEV * SUBLANES:]], axis=0)
        m_ref[me] = msend[me]
        _exchange(lambda p: msend.at[p], lambda p: m_ref.at[p], send2, recv2)
        _push_wait_recv(gather, ICI_SAME_CORE, wsend, wrecv)
        relay = [(block, block)]
        _forward_start(relay, fsend, frecv)
        _push_wait_recv(gather, SIBLING, wsend, wrecv)
        _forward_wait(relay, fsend, frecv)
        _push_wait_send(gather, SIBLING + ICI_SAME_CORE, wsend, wrecv)
        own.wait()

    return pl.pallas_call(
        body, name="mod_gather",
        out_shape=(jax.ShapeDtypeStruct((rows, D), F32), jax.ShapeDtypeStruct((N_DEV, 2 * SUBLANES, ncol), F32),
                   jax.ShapeDtypeStruct((N_DEV * w_in_t.shape[0], D), BF16),
                   jax.ShapeDtypeStruct(w_out.shape, BF16), jax.ShapeDtypeStruct(w1.shape, BF16),
                   jax.ShapeDtypeStruct(w2.shape, BF16)),
        in_specs=[_vmem()] * 8, out_specs=(_vmem(), _vmem(), _any(), _vmem(), _vmem(), _vmem()),
        scratch_shapes=[pltpu.VMEM(w_in_t.shape, BF16), pltpu.VMEM((N_DEV, 2 * SUBLANES, ncol), F32)]
                       + [pltpu.SemaphoreType.DMA((N_DEV - 1,))] * 4
                       + [pltpu.SemaphoreType.DMA((1, N_DEV - 1))] * 2 + [pltpu.SemaphoreType.DMA((1, 3))] * 2
                       + [pltpu.SemaphoreType.DMA((1,))],
        compiler_params=pltpu.CompilerParams(vmem_limit_bytes=VMEM_LIMIT),
    )(c, c_ctx.reshape(1, D), w_ada, b_ada, w_in_t, w_out, w1, w2)


def _row_block(ref, rows):
    return lambda p: ref.at[pl.ds(pl.multiple_of(p * rows, 2 * SUBLANES), rows), :]


def _col_block(ref, cols):
    return lambda p: ref.at[:, pl.ds(pl.multiple_of(p * cols, LANES), cols)]


def _slot(ref):
    return lambda p: ref.at[p]


class _Hosted:
    def __init__(self, kind, masks, operands, block_of, out_shapes, with_own):
        self.kind, self.masks, self.operands = kind, masks, list(operands)
        self.block_of, self.out_shapes, self.with_own = block_of, list(out_shapes), with_own
        self.n = len(self.operands)

    def scratch(self):
        return [pltpu.SemaphoreType.DMA((self.n, N_DEV - 1)), pltpu.SemaphoreType.DMA((self.n, N_DEV - 1)),
                pltpu.SemaphoreType.DMA((self.n,))]

    def _items(self, in_refs, out_refs):
        items = []
        for k in range(self.n):
            if self.kind == "gather":
                items.append((lambda p, k=k: in_refs[k], self.block_of[k](out_refs[k])))
            else:
                items.append((self.block_of[k](in_refs[k]), _slot(out_refs[k])))
        return items

    def _own(self, in_refs, out_refs, lsem):
        me, _ = _me_and_peers()
        items = self._items(in_refs, out_refs)
        return [pltpu.make_async_copy(src_for(me), dst_from(me), lsem.at[k])
                for k, (src_for, dst_from) in enumerate(items)]

    def start(self, in_refs, out_refs, sems):
        send, recv, lsem = sems
        if self.with_own:
            for cp in self._own(in_refs, out_refs, lsem):
                cp.start()
        _push_start(self._items(in_refs, out_refs), self.masks, send, recv)

    def wait(self, in_refs, out_refs, sems):
        send, recv, lsem = sems
        items = self._items(in_refs, out_refs)
        _push_wait_recv(items, self.masks, send, recv)
        _push_wait_send(items, self.masks, send, recv)
        if self.with_own:
            for cp in self._own(in_refs, out_refs, lsem):
                cp.wait()


class _HostedRelay:
    def __init__(self, arrays, block_of):
        self.operands, self.block_of = list(arrays), block_of
        self.out_shapes = [jax.ShapeDtypeStruct(a.shape, a.dtype) for a in arrays]
        self.n = len(self.operands)

    def scratch(self):
        return [pltpu.SemaphoreType.DMA((self.n, 3)), pltpu.SemaphoreType.DMA((self.n, 3))]

    def _items(self, in_refs, out_refs):
        return [(self.block_of[k](in_refs[k]), self.block_of[k](out_refs[k])) for k in range(self.n)]

    def start(self, in_refs, out_refs, sems):
        _forward_start(self._items(in_refs, out_refs), *sems)

    def wait(self, in_refs, out_refs, sems):
        _forward_wait(self._items(in_refs, out_refs), *sems)


def _call_hosting(body, hosted, *, name, grid, out_shape, in_specs, out_specs, scratch_shapes, args):
    n_in, n_out, n_scr = len(in_specs), len(out_shape), len(scratch_shapes)
    hn = sum(hs.n for hs in hosted)
    n_sem = [len(hs.scratch()) for hs in hosted]

    def wrapped(*refs):
        ins = refs[:n_in]
        h_in = refs[n_in:n_in + hn]
        outs = refs[n_in + hn:n_in + hn + n_out]
        h_out = refs[n_in + hn + n_out:n_in + 2 * hn + n_out]
        scr = refs[n_in + 2 * hn + n_out:n_in + 2 * hn + n_out + n_scr]
        sems = refs[n_in + 2 * hn + n_out + n_scr:]
        ids = [pl.program_id(i) for i in range(len(grid))]
        first = functools.reduce(jnp.logical_and, [i == 0 for i in ids])
        last = functools.reduce(jnp.logical_and, [i == g - 1 for i, g in zip(ids, grid)])
        parts, o0, s0 = [], 0, 0
        for hs, ns in zip(hosted, n_sem):
            parts.append((hs, h_in[o0:o0 + hs.n], h_out[o0:o0 + hs.n], sems[s0:s0 + ns]))
            o0 += hs.n
            s0 += ns

        @pl.when(first)
        def _():
            for hs, hi, ho, se in parts:
                hs.start(hi, ho, se)

        body(*ins, *outs, *scr)

        @pl.when(last)
        def _():
            for hs, hi, ho, se in parts:
                hs.wait(hi, ho, se)

    aliases, o0 = {}, 0
    for hs in hosted:
        if isinstance(hs, _HostedRelay):
            aliases.update({n_in + o0 + k: n_out + o0 + k for k in range(hs.n)})
        o0 += hs.n
    return pl.pallas_call(
        wrapped, name=name, grid=grid,
        out_shape=tuple(out_shape) + tuple(s for hs in hosted for s in hs.out_shapes),
        in_specs=list(in_specs) + [_any()] * hn,
        out_specs=tuple(out_specs) + (_any(),) * hn,
        scratch_shapes=list(scratch_shapes) + [s for hs in hosted for s in hs.scratch()],
        input_output_aliases=aliases,
        compiler_params=_params(*(("arbitrary",) * len(grid))),
    )(*args, *[a for hs in hosted for a in hs.operands])


def _token_tiles(n_ctx, tm):
    nct = n_ctx // tm

    def ctx_spec(D):
        return pl.BlockSpec((None, tm, D), lambda b, t: (b, jnp.minimum(t, nct - 1), 0))

    def lat_spec(D):
        return pl.BlockSpec((None, tm, D), lambda b, t: (b, jnp.maximum(t - nct, 0), 0))

    return nct, ctx_spec, lat_spec


def _inproj_fwd(x, ctx, modl, g1, w_in_t, hosted):
    B, N, D = x.shape
    n_ctx = ctx.shape[1]
    T = n_ctx + N
    nw = w_in_t.shape[0]
    tm = _div_tile(n_ctx, 256, 16)
    nct, ctx_spec, lat_spec = _token_tiles(n_ctx, tm)

    def body(c_ref, x_ref, sh_ref, sc_ref, g_ref, w_ref, h_ref, p_ref):
        x = jnp.where(pl.program_id(1) < nct, c_ref[...], x_ref[...])
        r = lax.rsqrt(jnp.mean(x * x, axis=-1, keepdims=True) + NORM_EPS)
        h = ((x * r) * g_ref[...]) * (1.0 + sc_ref[...]) + sh_ref[...]
        hb = h.astype(BF16)
        h_ref[...] = hb
        p_ref[...] = _dot_nt(hb, w_ref[...])

    def mrow(b, t):
        return jnp.where(t < nct, B, b)

    return _call_hosting(
        body, hosted, name="inproj_fwd", grid=(B, T // tm),
        out_shape=(jax.ShapeDtypeStruct((B, T, D), BF16), jax.ShapeDtypeStruct((B, T, nw), F32)),
        in_specs=[ctx_spec(D), lat_spec(D),
                  pl.BlockSpec((None, None, 1, D), lambda b, t: (mrow(b, t), 0, 0, 0)),
                  pl.BlockSpec((None, None, 1, D), lambda b, t: (mrow(b, t), 1, 0, 0)),
                  pl.BlockSpec((1, D), lambda b, t: (0, 0)),
                  pl.BlockSpec((nw, D), lambda b, t: (0, 0))],
        out_specs=(pl.BlockSpec((None, tm, D), lambda b, t: (b, t, 0)),
                   pl.BlockSpec((None, tm, nw), lambda b, t: (b, t, 0))),
        scratch_shapes=[], args=(ctx, x, modl, modl, g1, w_in_t))


def _swap32(x):
    lane = lax.broadcasted_iota(jnp.int32, x.shape, 1)
    return jnp.where((lane % 64) < 32, pltpu.roll(x, 96, 1), pltpu.roll(x, 32, 1))


def _rope(x, cos, sin):
    return x * cos + _swap32(x) * sin


def _unrope(dy, cos, sin):
    return dy * cos + _swap32(dy * sin)


def _ret_weights(lgf, lgb, dist):
    return jnp.exp(jnp.where(dist >= 0.0, lgf * dist, -lgb * dist))


class _RetDecay:
    def __init__(self, lgf, lgb, rows):
        r = lax.broadcasted_iota(jnp.int32, (rows, RET_DIM), 0).astype(F32)
        self.head = r + 1.0
        self.tail = (rows - 1.0) - r
        self.q_f = jnp.exp(lgf * self.head)
        self.k_f = jnp.exp(lgf * self.tail)
        self.q_b = jnp.exp(lgb * self.tail)
        self.k_b = jnp.exp(lgb * self.head)


def _ret_states(kf32, vs, lgf, lgb, C, c, nt, hf, hb, hfa=None, hba=None):
    dec = _RetDecay(lgf, lgb, c)
    dec_c = _RetDecay(lgf, lgb, C)
    step_f = jnp.exp(jnp.zeros((RET_DIM, RET_DIM), F32) + lgf * c)
    step_b = jnp.exp(jnp.zeros((RET_DIM, RET_DIM), F32) + lgb * c)

    def upd(rows, kdec):
        return _dot_tn((kf32[rows, :] * kdec).astype(BF16), vs[rows, :])

    def lat(t):
        return slice(C + t * c, C + (t + 1) * c)

    state = upd(slice(0, C), dec_c.k_f)
    aged = jnp.zeros_like(state)
    for t in range(nt):
        hf[t] = state.astype(BF16)
        if hfa is not None:
            hfa[t] = aged
        if t < nt - 1:
            aged = step_f * (aged + c * state)
            state = step_f * state + upd(lat(t), dec.k_f)
    state = upd(slice(0, C), dec_c.k_b)
    aged = jnp.zeros_like(state)
    for t in range(nt - 1, -1, -1):
        hb[t] = state.astype(BF16)
        if hba is not None:
            hba[t] = aged
        if t > 0:
            aged = step_b * (aged + c * state)
            state = step_b * state + upd(lat(t), dec.k_b)
    return dec, dec_c, step_f, step_b


def _ret_fwd(proj, cos, sin, lg, gn, n_ctx, hosted):
    B, T, _ = proj.shape
    C = n_ctx
    N = T - C
    c = _div_tile(N, 256, 16)
    nt = N // c
    scale = RET_DIM ** -0.5

    def body(lg_ref, q_ref, k_ref, v_ref, g_ref, cos_ref, sin_ref, gn_ref, o_ref, lat_ref, qs, ks, vs, kf32, hf, hb):
        h = pl.program_id(1)
        lgf = lg_ref[0, h]
        lgb = lg_ref[1, h]
        for rows in [slice(0, C)] + [slice(C + t * c, C + (t + 1) * c) for t in range(nt)]:
            cosb = cos_ref[rows, :]
            sinb = sin_ref[rows, :]
            qs[rows, :] = (_rope(q_ref[rows, :], cosb, sinb) * scale).astype(BF16)
            kr = _rope(k_ref[rows, :], cosb, sinb)
            kf32[rows, :] = kr
            ks[rows, :] = kr.astype(BF16)
            vs[rows, :] = v_ref[rows, :].astype(BF16)
        gnv = gn_ref[...]
        dec, _, _, _ = _ret_states(kf32, vs, lgf, lgb, C, c, nt, hf, hb)
        rc = (lax.broadcasted_iota(jnp.int32, (c, c), 0) - lax.broadcasted_iota(jnp.int32, (c, c), 1)).astype(F32)
        w_diag = _ret_weights(lgf, lgb, rc)
        for t in range(nt):
            rows = slice(C + t * c, C + (t + 1) * c)
            qt = qs[rows, :]
            s = _dot_nt(qt, ks[rows, :])
            o = (_dot((s * w_diag).astype(BF16), vs[rows, :])
                 + dec.q_f * _dot(qt, hf[t]) + dec.q_b * _dot(qt, hb[t]))
            o_ref[t * c:(t + 1) * c, :] = o
            mu = jnp.mean(o, axis=-1, keepdims=True)
            oc = o - mu
            var = jnp.mean(oc * oc, axis=-1, keepdims=True)
            yh = oc * lax.rsqrt(var + NORM_EPS)
            g = g_ref[rows, :]
            lat_ref[t * c:(t + 1) * c, :] = ((yh * gnv) * (g * _sigmoid(g))).astype(BF16)

    def col(seg):
        return pl.BlockSpec((None, T, RET_DIM), lambda b, h, seg=seg: (b, 0, seg * RET_HEADS + h))

    return _call_hosting(
        body, hosted, name="ret_fwd", grid=(B, RET_HEADS),
        out_shape=(jax.ShapeDtypeStruct((B, N, RET_WIDTH), F32), jax.ShapeDtypeStruct((B, N, RET_WIDTH), BF16)),
        in_specs=[pl.BlockSpec(memory_space=pltpu.SMEM), col(0), col(1), col(2), col(3),
                  pl.BlockSpec((T, RET_DIM), lambda b, h: (0, 0)), pl.BlockSpec((T, RET_DIM), lambda b, h: (0, 0)),
                  pl.BlockSpec((1, RET_DIM), lambda b, h: (0, h))],
        out_specs=(pl.BlockSpec((None, N, RET_DIM), lambda b, h: (b, 0, h)),
                   pl.BlockSpec((None, N, RET_DIM), lambda b, h: (b, 0, h))),
        scratch_shapes=[pltpu.VMEM((T, RET_DIM), BF16)] * 3 + [pltpu.VMEM((T, RET_DIM), F32)]
                       + [pltpu.VMEM((nt, RET_DIM, RET_DIM), BF16)] * 2,
        args=(lg, proj, proj, proj, proj, cos, sin, gn))


def _ret_bwd(proj, cos, sin, lg, gn, o, dlat, n_ctx, hosted):
    B, T, _ = proj.shape
    C = n_ctx
    N = T - C
    c = _div_tile(N, 256, 16)
    nt = N // c
    scale = RET_DIM ** -0.5

    def lat(t):
        return slice(C + t * c, C + (t + 1) * c)

    def body(lg_ref, q_ref, k_ref, v_ref, g_ref, cos_ref, sin_ref, gn_ref, o_ref, dl_ref,
             d_ref, dgn_ref, dlg_ref, qs, ks, vs, dos, qf32, kf32, hf, hb, hfa, hba, gf_s, gb_s):
        h = pl.program_id(1)
        lgf = lg_ref[0, h]
        lgb = lg_ref[1, h]
        gnv = gn_ref[...]

        def fold(a):
            return jnp.sum(a.reshape(a.shape[0] // SUBLANES, SUBLANES, a.shape[1]), axis=0)

        for rows in [slice(0, C)] + [lat(t) for t in range(nt)]:
            cosb = cos_ref[rows, :]
            sinb = sin_ref[rows, :]
            qr = _rope(q_ref[rows, :], cosb, sinb) * scale
            qf32[rows, :] = qr
            qs[rows, :] = qr.astype(BF16)
            kr = _rope(k_ref[rows, :], cosb, sinb)
            kf32[rows, :] = kr
            ks[rows, :] = kr.astype(BF16)
            vs[rows, :] = v_ref[rows, :].astype(BF16)

        dgn = jnp.zeros((1, RET_DIM), F32)
        for t in range(nt):
            lrows = slice(t * c, (t + 1) * c)
            ov = o_ref[lrows, :]
            mu = jnp.mean(ov, axis=-1, keepdims=True)
            oc = ov - mu
            var = jnp.mean(oc * oc, axis=-1, keepdims=True)
            rstd = lax.rsqrt(var + NORM_EPS)
            yh = oc * rstd
            g = g_ref[lat(t), :]
            sg = _sigmoid(g)
            dl = dl_ref[lrows, :]
            d_ref[3, lat(t), :] = (dl * (yh * gnv) * (sg * (1.0 + g * (1.0 - sg)))).astype(BF16)
            dls = dl * (g * sg)
            dgn = dgn + jnp.sum(dls * yh, axis=0, keepdims=True)
            dyh = dls * gnv
            do = rstd * (dyh - jnp.mean(dyh, axis=-1, keepdims=True)
                         - yh * jnp.mean(dyh * yh, axis=-1, keepdims=True))
            dos[lrows, :] = do.astype(BF16)
        dgn_ref[...] = jnp.concatenate([dgn, jnp.zeros((SUBLANES - 1, RET_DIM), F32)], axis=0)
        d_ref[3, 0:C, :] = jnp.zeros((C, RET_DIM), BF16)
        d_ref[0, 0:C, :] = jnp.zeros((C, RET_DIM), BF16)

        dec, dec_c, step_f, step_b = _ret_states(kf32, vs, lgf, lgb, C, c, nt, hf, hb, hfa, hba)

        def zmat(t, qdec):
            return _dot_tn((qf32[lat(t), :] * qdec).astype(BF16), dos[t * c:(t + 1) * c, :])

        acc3f = jnp.zeros((RET_DIM, RET_DIM), F32)
        acc3b = jnp.zeros((RET_DIM, RET_DIM), F32)
        state = jnp.zeros((RET_DIM, RET_DIM), F32)
        for t in range(nt - 1, -1, -1):
            gf_s[t] = state.astype(BF16)
            z = zmat(t, dec.q_f)
            acc3f = acc3f + hfa[t] * z
            state = step_f * state + z
        gctx_f = state.astype(BF16)
        state = jnp.zeros((RET_DIM, RET_DIM), F32)
        for t in range(nt):
            gb_s[t] = state.astype(BF16)
            z = zmat(t, dec.q_b)
            acc3b = acc3b + hba[t] * z
            state = step_b * state + z
        gctx_b = state.astype(BF16)

        rc = (lax.broadcasted_iota(jnp.int32, (c, c), 0) - lax.broadcasted_iota(jnp.int32, (c, c), 1)).astype(F32)
        w_diag = _ret_weights(lgf, lgb, rc)
        wg_f = jnp.where(rc >= 0.0, w_diag * rc, 0.0)
        wg_b = jnp.where(rc < 0.0, -w_diag * rc, 0.0)
        accf = jnp.zeros((SUBLANES, RET_DIM), F32)
        accb = jnp.zeros((SUBLANES, RET_DIM), F32)
        gdf = jnp.zeros((SUBLANES, c), F32)
        gdb = jnp.zeros((SUBLANES, c), F32)
        for t in range(nt):
            rows = lat(t)
            qt = qs[rows, :]
            kt = ks[rows, :]
            vt = vs[rows, :]
            dot = dos[t * c:(t + 1) * c, :]
            s = _dot_nt(qt, kt)
            dp = _dot_nt(dot, vt)
            dv = _dot_tn((s * w_diag).astype(BF16), dot)
            ds = (dp * w_diag).astype(BF16)
            dq = _dot(ds, kt)
            dk = _dot_tn(ds, qt)
            gs = dp * s
            gdf = gdf + fold(gs * wg_f)
            gdb = gdb + fold(gs * wg_b)
            qv = qf32[rows, :]
            kv = kf32[rows, :]
            dq_f = dec.q_f * _dot_nt(dot, hf[t])
            dq_b = dec.q_b * _dot_nt(dot, hb[t])
            dk_f = dec.k_f * _dot_nt(vt, gf_s[t])
            dk_b = dec.k_b * _dot_nt(vt, gb_s[t])
            accf = accf + fold(dec.head * dq_f * qv) + fold(dec.tail * dk_f * kv)
            accb = accb + fold(dec.tail * dq_b * qv) + fold(dec.head * dk_b * kv)
            dv = dv + dec.k_f * _dot(kt, gf_s[t]) + dec.k_b * _dot(kt, gb_s[t])
            cosb = cos_ref[rows, :]
            sinb = sin_ref[rows, :]
            d_ref[0, rows, :] = _unrope((dq + dq_f + dq_b) * scale, cosb, sinb).astype(BF16)
            d_ref[1, rows, :] = _unrope(dk + dk_f + dk_b, cosb, sinb).astype(BF16)
            d_ref[2, rows, :] = dv.astype(BF16)
        kc = ks[0:C, :]
        vc = vs[0:C, :]
        kcv = kf32[0:C, :]
        dkc_f = dec_c.k_f * _dot_nt(vc, gctx_f)
        dkc_b = dec_c.k_b * _dot_nt(vc, gctx_b)
        accf = accf + fold(dec_c.tail * dkc_f * kcv)
        accb = accb + fold(dec_c.head * dkc_b * kcv)
        d_ref[1, 0:C, :] = (dkc_f + dkc_b).astype(BF16)
        d_ref[2, 0:C, :] = (dec_c.k_f * _dot(kc, gctx_f) + dec_c.k_b * _dot(kc, gctx_b)).astype(BF16)
        gf = jnp.sum(gdf) + jnp.sum(accf) + jnp.sum(acc3f)
        gb = jnp.sum(gdb) + jnp.sum(accb) + jnp.sum(acc3b)
        row = lax.broadcasted_iota(jnp.int32, (SUBLANES, LANES), 0)
        dlg_ref[...] = jnp.where(row == 0, gf, jnp.where(row == 1, gb, 0.0))

    def col(seg):
        return pl.BlockSpec((None, T, RET_DIM), lambda b, h, seg=seg: (b, 0, seg * RET_HEADS + h))

    return _call_hosting(
        body, hosted, name="ret_bwd", grid=(B, RET_HEADS),
        out_shape=(jax.ShapeDtypeStruct((B, 4, T, RET_WIDTH), BF16),
                   jax.ShapeDtypeStruct((B, SUBLANES, RET_WIDTH), F32),
                   jax.ShapeDtypeStruct((B, RET_HEADS, SUBLANES, LANES), F32)),
        in_specs=[pl.BlockSpec(memory_space=pltpu.SMEM), col(0), col(1), col(2), col(3),
                  pl.BlockSpec((T, RET_DIM), lambda b, h: (0, 0)), pl.BlockSpec((T, RET_DIM), lambda b, h: (0, 0)),
                  pl.BlockSpec((1, RET_DIM), lambda b, h: (0, h)),
                  pl.BlockSpec((None, N, RET_DIM), lambda b, h: (b, 0, h)),
                  pl.BlockSpec((None, N, RET_DIM), lambda b, h: (b, 0, h))],
        out_specs=(pl.BlockSpec((None, 4, T, RET_DIM), lambda b, h: (b, 0, 0, h)),
                   pl.BlockSpec((None, SUBLANES, RET_DIM), lambda b, h: (b, 0, h)),
                   pl.BlockSpec((None, None, SUBLANES, LANES), lambda b, h: (b, h, 0, 0))),
        scratch_shapes=[pltpu.VMEM((T, RET_DIM), BF16)] * 3 + [pltpu.VMEM((N, RET_DIM), BF16)]
                       + [pltpu.VMEM((T, RET_DIM), F32)] * 2
                       + [pltpu.VMEM((nt, RET_DIM, RET_DIM), BF16)] * 2 + [pltpu.VMEM((nt, RET_DIM, RET_DIM), F32)] * 2
                       + [pltpu.VMEM((nt, RET_DIM, RET_DIM), BF16)] * 2,
        args=(lg, proj, proj, proj, proj, cos, sin, gn, o, dlat))


def _na_geometry(rows):
    kh = min(NA_KH, rows)
    return kh, kh * GRID_W


def _pair_select():
    lane = lax.broadcasted_iota(jnp.int32, (2 * GRID_W, LANES), 1)
    row = lax.broadcasted_iota(jnp.int32, (2 * GRID_W, LANES), 0)
    return (lane >= NA_DIM) == (row >= GRID_W)


def _pair_bias(bias_ref, dr0, kh):
    return jnp.concatenate(
        [jnp.concatenate([bias_ref[e, pl.ds(dr0 + 2 * m, 1)].reshape(GRID_W, LANES) for m in range(kh // 2)], axis=1)
         for e in range(2)], axis=0)


def _na_softmax(s_loc, s_ctx):
    mx = jnp.maximum(jnp.max(s_loc, axis=-1, keepdims=True), jnp.max(s_ctx, axis=-1, keepdims=True))
    p_loc = jnp.exp(s_loc - mx)
    p_ctx = jnp.exp(s_ctx - mx)
    den = jnp.sum(p_loc, axis=-1, keepdims=True) + jnp.sum(p_ctx, axis=-1, keepdims=True)
    return p_loc, p_ctx, den


def _na_fwd(proj, bias2, n_ctx, hosted):
    B, T, _ = proj.shape
    C = n_ctx
    N = T - C
    R = N // GRID_W
    kh, nk = _na_geometry(R)
    scale = NA_DIM ** -0.5
    base = (4 * RET_WIDTH) // LANES

    def body(q_ref, k_ref, v_ref, bias_ref, out_ref, kb16, vb16):
        kb16[...] = k_ref[...].astype(BF16)
        vb16[...] = v_ref[...].astype(BF16)
        kc = kb16[0:C, :]
        vc = vb16[0:C, :]
        lane = lax.broadcasted_iota(jnp.int32, (GRID_W, LANES), 1)
        sel2 = _pair_select()

        def group(gi, carry):
            pre = []
            for u in range(NA_GROUP):
                r = gi * NA_GROUP + u
                bs = jnp.clip(r - kh // 2, 0, R - kh)
                dr0 = bs - r + (NA_KH - 1)
                q = q_ref[pl.ds(pl.multiple_of(C + r * GRID_W, GRID_W), GRID_W), :] * scale
                q2 = jnp.where(sel2, jnp.concatenate([q, q], axis=0), 0.0).astype(BF16)
                band = pl.ds(pl.multiple_of(C + bs * GRID_W, GRID_W), nk)
                s_loc = _dot_nt(q2, kb16[band, :]) + _pair_bias(bias_ref, dr0, kh)
                s_ctx = _dot_nt(q2, kc)
                pre.append((r, band, s_loc, s_ctx))
            mid = [(r, band) + _na_softmax(s_loc, s_ctx) for r, band, s_loc, s_ctx in pre]
            for r, band, p_loc, p_ctx, den in mid:
                o2 = (_dot(p_loc.astype(BF16), vb16[band, :]) + _dot(p_ctx.astype(BF16), vc)) / den
                out_ref[pl.ds(pl.multiple_of(r * GRID_W, GRID_W), GRID_W), :] = jnp.where(
                    lane < NA_DIM, o2[:GRID_W], o2[GRID_W:]).astype(BF16)
            return carry

        lax.fori_loop(0, R // NA_GROUP, group, 0)

    def col(seg):
        return pl.BlockSpec((None, T, LANES), lambda b, p, seg=seg: (b, 0, base + seg * NA_PAIRS + p))

    return _call_hosting(
        body, hosted, name="na_fwd", grid=(B, NA_PAIRS),
        out_shape=(jax.ShapeDtypeStruct((B, N, NA_WIDTH), BF16),),
        in_specs=[col(0), col(1), col(2),
                  pl.BlockSpec((2, 2 * NA_KH - 2, GRID_W, LANES), lambda b, p: (p, 0, 0, 0))],
        out_specs=(pl.BlockSpec((None, N, LANES), lambda b, p: (b, 0, p)),),
        scratch_shapes=[pltpu.VMEM((T, LANES), BF16)] * 2,
        args=(proj, proj, proj, bias2))


def _na_bwd(proj, bias2, dlat, n_ctx, hosted):
    B, T, _ = proj.shape
    C = n_ctx
    N = T - C
    R = N // GRID_W
    kh, nk = _na_geometry(R)
    scale = NA_DIM ** -0.5
    base = (4 * RET_WIDTH) // LANES

    def body(q_ref, k_ref, v_ref, bias_ref, dl_ref, d_ref, db_ref, kb16, vb16, dkv):
        b = pl.program_id(1)
        kb16[...] = k_ref[...].astype(BF16)
        vb16[...] = v_ref[...].astype(BF16)
        kc = kb16[0:C, :]
        vc = vb16[0:C, :]
        lane = lax.broadcasted_iota(jnp.int32, (GRID_W, LANES), 1)
        dkv[...] = jnp.zeros(dkv.shape, F32)
        d_ref[0, 0:C, :] = jnp.zeros((C, LANES), BF16)

        @pl.when(b == 0)
        def _():
            db_ref[...] = jnp.zeros(db_ref.shape, F32)

        sel2 = _pair_select()

        def group(gi, carry):
            pre = []
            for u in range(NA_GROUP):
                r = gi * NA_GROUP + u
                bs = jnp.clip(r - kh // 2, 0, R - kh)
                dr0 = bs - r + (NA_KH - 1)
                q = q_ref[pl.ds(pl.multiple_of(C + r * GRID_W, GRID_W), GRID_W), :] * scale
                do = dl_ref[pl.ds(pl.multiple_of(r * GRID_W, GRID_W), GRID_W), :]
                q2 = jnp.where(sel2, jnp.concatenate([q, q], axis=0), 0.0).astype(BF16)
                do2 = jnp.where(sel2, jnp.concatenate([do, do], axis=0), 0.0).astype(BF16)
                band = pl.ds(pl.multiple_of(C + bs * GRID_W, GRID_W), nk)
                s_loc = _dot_nt(q2, kb16[band, :]) + _pair_bias(bias_ref, dr0, kh)
                s_ctx = _dot_nt(q2, kc)
                dp_loc = _dot_nt(do2, vb16[band, :])
                dp_ctx = _dot_nt(do2, vc)
                pre.append((r, dr0, band, q2, do2, s_loc, s_ctx, dp_loc, dp_ctx))
            mid = []
            for r, dr0, band, q2, do2, s_loc, s_ctx, dp_loc, dp_ctx in pre:
                p_loc, p_ctx, den = _na_softmax(s_loc, s_ctx)
                inv = 1.0 / den
                p_loc = p_loc * inv
                p_ctx = p_ctx * inv
                delta = (jnp.sum(p_loc * dp_loc, axis=-1, keepdims=True)
                         + jnp.sum(p_ctx * dp_ctx, axis=-1, keepdims=True))
                ds_loc = p_loc * (dp_loc - delta)
                ds_ctx = p_ctx * (dp_ctx - delta)
                mid.append((r, dr0, band, q2, do2, p_loc.astype(BF16), p_ctx.astype(BF16), ds_loc, ds_ctx))
            for r, dr0, band, q2, do2, pb_loc, pb_ctx, ds_loc, ds_ctx in mid:
                dsb_loc = ds_loc.astype(BF16)
                dsb_ctx = ds_ctx.astype(BF16)
                dq2 = _dot(dsb_loc, kb16[band, :]) + _dot(dsb_ctx, kc)
                d_ref[0, pl.ds(pl.multiple_of(C + r * GRID_W, GRID_W), GRID_W), :] = (jnp.where(
                    lane < NA_DIM, dq2[:GRID_W], dq2[GRID_W:]) * scale).astype(BF16)
                dkv[0, band, :] += _dot_tn(dsb_loc, q2)
                dkv[1, band, :] += _dot_tn(pb_loc, do2)
                dkv[0, 0:C, :] += _dot_tn(dsb_ctx, q2)
                dkv[1, 0:C, :] += _dot_tn(pb_ctx, do2)
                for e in range(2):
                    for m in range(kh // 2):
                        db_ref[e, pl.ds(dr0 + 2 * m, 1)] += ds_loc[e * GRID_W:(e + 1) * GRID_W,
                                                                   m * LANES:(m + 1) * LANES].reshape(1, GRID_W, LANES)
            return carry

        lax.fori_loop(0, R // NA_GROUP, group, 0)
        d_ref[1] = dkv[0].astype(BF16)
        d_ref[2] = dkv[1].astype(BF16)

    def col(seg):
        return pl.BlockSpec((None, T, LANES), lambda p, b, seg=seg: (b, 0, base + seg * NA_PAIRS + p))

    return _call_hosting(
        body, hosted, name="na_bwd", grid=(NA_PAIRS, B),
        out_shape=(jax.ShapeDtypeStruct((B, 3, T, NA_WIDTH), BF16),
                   jax.ShapeDtypeStruct((NA_HEADS, 2 * NA_KH - 2, GRID_W, LANES), F32)),
        in_specs=[col(0), col(1), col(2),
                  pl.BlockSpec((2, 2 * NA_KH - 2, GRID_W, LANES), lambda p, b: (p, 0, 0, 0)),
                  pl.BlockSpec((None, N, LANES), lambda p, b: (b, 0, p))],
        out_specs=(pl.BlockSpec((None, 3, T, LANES), lambda p, b: (b, 0, 0, p)),
                   pl.BlockSpec((2, 2 * NA_KH - 2, GRID_W, LANES), lambda p, b: (p, 0, 0, 0))),
        scratch_shapes=[pltpu.VMEM((T, LANES), BF16)] * 2 + [pltpu.VMEM((2, T, LANES), F32)],
        args=(proj, proj, proj, bias2, dlat))


def _split3(a):
    hi = a.astype(BF16)
    r1 = a - hi.astype(F32)
    mid = r1.astype(BF16)
    lo = (r1 - mid.astype(F32)).astype(BF16)
    return hi, mid, lo


def _rpb_reduce(dbias2, onehot2):
    rows = dbias2.shape[0] * dbias2.shape[1]
    flat = dbias2.reshape(rows, GRID_W * LANES)

    def body(a_ref, oh_ref, o_ref):
        hi, mid, lo = _split3(a_ref[...])
        oh = oh_ref[...]
        o_ref[...] = _dot(hi, oh) + _dot(mid, oh) + _dot(lo, oh)

    return pl.pallas_call(
        body, name="rpb_reduce", out_shape=jax.ShapeDtypeStruct((rows, LANES), F32),
        in_specs=[_vmem(), _vmem()], out_specs=_vmem(),
        compiler_params=pltpu.CompilerParams(vmem_limit_bytes=VMEM_LIMIT),
    )(flat, onehot2)


def _dense_core(lat_ret, lat_na, x, tgt, modl, g_post_mix, g_pre_mlp, g_post_mlp, w_out, w1, w2):
    B, N, D = x.shape
    F = w1.shape[1]
    w2_rows = w2.shape[0] // N_DEV
    mixw = w_out.shape[0]
    half = mixw // 2
    tm = _div_tile(N, 256, 16)
    nt = N // tm
    fc = _div_tile(F, 1024, LANES)

    def body(lr_ref, ln_ref, x_ref, t_ref, gt1_ref, sh2_ref, sc2_ref, gt2_ref, gpm_ref, gpre_ref, gpo_ref,
             wout_hbm, w1_hbm, w2_part,
             dy1_ref, dlr_ref, dln_ref, dmix_ref, h2_ref, a_ref, du_ref, dz_ref, red_ref, w2_hbm,
             wout_v, w1_v, w2_v, u_s, sems, fsend, frecv):
        @pl.when((pl.program_id(0) == 0) & (pl.program_id(1) == 0))
        def _():
            relay = [(_row_block(w2_part, w2_rows), _row_block(w2_hbm, w2_rows))]
            _forward_start(relay, fsend, frecv)
            cps = [pltpu.make_async_copy(wout_hbm, wout_v, sems.at[0]),
                   pltpu.make_async_copy(w1_hbm, w1_v, sems.at[1])]
            for cp in cps:
                cp.start()
            _forward_wait(relay, fsend, frecv)
            cps.append(pltpu.make_async_copy(w2_hbm, w2_v, sems.at[2]))
            cps[2].start()
            for cp in cps:
                cp.wait()

        gt1 = gt1_ref[...]
        sh2 = sh2_ref[...]
        sc2 = sc2_ref[...]
        gt2 = gt2_ref[...]
        gpm = gpm_ref[...]
        gpre = gpre_ref[...]
        gpo = gpo_ref[...]

        def rowmean(a):
            return jnp.mean(a, axis=-1, keepdims=True)

        def colsum(a):
            return jnp.sum(a, axis=0, keepdims=True)

        mix = _dot(lr_ref[...], wout_v[0:half, :]) + _dot(ln_ref[...], wout_v[half:, :])
        x = x_ref[...]
        rm = lax.rsqrt(rowmean(mix * mix) + NORM_EPS)
        mh = mix * rm
        nm = mh * gpm
        y1 = x + gt1 * nm
        r1 = lax.rsqrt(rowmean(y1 * y1) + NORM_EPS)
        xh = y1 * r1
        n1 = xh * gpre
        h2b = (n1 * (1.0 + sc2) + sh2).astype(BF16)
        h2_ref[...] = h2b
        z = jnp.zeros((tm, D), F32)
        for c0 in range(0, F, fc):
            u = _dot(h2b, w1_v[:, c0:c0 + fc])
            u_s[:, c0:c0 + fc] = u
            ru = jnp.maximum(u, 0.0)
            ab = (ru * ru).astype(BF16)
            a_ref[:, c0:c0 + fc] = ab
            z = z + _dot(ab, w2_v[c0:c0 + fc, :])
        r2 = lax.rsqrt(rowmean(z * z) + NORM_EPS)
        zh = z * r2
        n2 = zh * gpo
        y2 = y1 + gt2 * n2
        err = y2 - t_ref[...]
        loss = 0.5 * jnp.sum(rowmean(err * err))
        dy2 = err * (1.0 / D)
        red_ref[2:3, :] = colsum(dy2 * n2)
        dn2 = dy2 * gt2
        red_ref[6:7, :] = colsum(dn2 * zh)
        dzh = dn2 * gpo
        dz = r2 * (dzh - zh * rowmean(dzh * zh))
        dzb = dz.astype(BF16)
        dz_ref[...] = dzb
        dh2 = jnp.zeros((tm, D), F32)
        for c0 in range(0, F, fc):
            da = _dot_nt(dzb, w2_v[c0:c0 + fc, :])
            dub = (da * (2.0 * jnp.maximum(u_s[:, c0:c0 + fc], 0.0))).astype(BF16)
            du_ref[:, c0:c0 + fc] = dub
            dh2 = dh2 + _dot_nt(dub, w1_v[:, c0:c0 + fc])
        red_ref[3:4, :] = colsum(dh2 * n1)
        red_ref[4:5, :] = colsum(dh2)
        dn1 = dh2 * (1.0 + sc2)
        red_ref[5:6, :] = colsum(dn1 * xh)
        dxh = dn1 * gpre
        dy1 = dy2 + r1 * (dxh - xh * rowmean(dxh * xh))
        dy1_ref[...] = dy1
        red_ref[0:1, :] = colsum(dy1 * nm)
        dnm = dy1 * gt1
        red_ref[1:2, :] = colsum(dnm * mh)
        dmh = dnm * gpm
        dmix = (rm * (dmh - mh * rowmean(dmh * mh))).astype(BF16)
        dmix_ref[...] = dmix
        dlr_ref[...] = _dot_nt(dmix, wout_v[0:half, :])
        dln_ref[...] = _dot_nt(dmix, wout_v[half:, :])
        red_ref[7:8, :] = jnp.zeros((1, D), F32) + loss

    def tok(w):
        return pl.BlockSpec((None, tm, w), lambda b, t: (b, t, 0))

    def mod(k):
        return pl.BlockSpec((None, None, 1, D), lambda b, t, k=k: (b, k, 0, 0))

    def vec():
        return pl.BlockSpec((1, D), lambda b, t: (0, 0))

    return pl.pallas_call(
        body, name="dense_core", grid=(B, nt),
        out_shape=(jax.ShapeDtypeStruct((B, N, D), F32), jax.ShapeDtypeStruct((B, N, half), F32),
                   jax.ShapeDtypeStruct((B, N, half), F32), jax.ShapeDtypeStruct((B, N, D), BF16),
                   jax.ShapeDtypeStruct((B, N, D), BF16), jax.ShapeDtypeStruct((B, N, F), BF16),
                   jax.ShapeDtypeStruct((B, N, F), BF16), jax.ShapeDtypeStruct((B, N, D), BF16),
                   jax.ShapeDtypeStruct((B, nt, SUBLANES, D), F32),
                   jax.ShapeDtypeStruct(w2.shape, w2.dtype)),
        in_specs=[tok(half), tok(half), tok(D), tok(D), mod(2), mod(3), mod(4), mod(5), vec(), vec(), vec(),
                  _any(), _any(), _any()],
        out_specs=(tok(D), tok(half), tok(half), tok(D), tok(D), tok(F), tok(F), tok(D),
                   pl.BlockSpec((None, None, SUBLANES, D), lambda b, t: (b, t, 0, 0)), _any()),
        scratch_shapes=[pltpu.VMEM((mixw, D), BF16), pltpu.VMEM((D, F), BF16), pltpu.VMEM((F, D), BF16),
                        pltpu.VMEM((tm, F), F32), pltpu.SemaphoreType.DMA((3,)),
                        pltpu.SemaphoreType.DMA((1, 3)), pltpu.SemaphoreType.DMA((1, 3))],
        input_output_aliases={13: 9},
        compiler_params=_params("arbitrary", "arbitrary"),
    )(lat_ret, lat_na, x, tgt, modl, modl, modl, modl, g_post_mix, g_pre_mlp, g_post_mlp, w_out, w1, w2)[:9]


def _inproj_bwd(dret, dna, x, ctx, dy1, modl, g1, w_in_t, hosted):
    B, N, D = x.shape
    n_ctx = ctx.shape[1]
    T = n_ctx + N
    tm = _div_tile(n_ctx, 256, 16)
    nct, ctx_spec, lat_spec = _token_tiles(n_ctx, tm)
    nt = T // tm
    nseg_r = dret.shape[1]
    nseg_n = dna.shape[1]
    nw = w_in_t.shape[0]

    def body(*refs):
        seg_refs = refs[:nseg_r + nseg_n]
        c_ref, x_ref, dy1_ref, sc_ref, g_ref, w_ref, dx_ref, red_ref = refs[nseg_r + nseg_n:]
        t = pl.program_id(1)
        dh = jnp.zeros((tm, D), F32)
        for s, ref in enumerate(seg_refs):
            dh = dh + _dot(ref[...], w_ref[s * SEG:(s + 1) * SEG, :])
        x = jnp.where(t < nct, c_ref[...], x_ref[...])
        g = g_ref[...]
        r = lax.rsqrt(jnp.mean(x * x, axis=-1, keepdims=True) + NORM_EPS)
        xh = x * r
        red_ref[0:1, :] = jnp.sum(dh, axis=0, keepdims=True)
        red_ref[1:2, :] = jnp.sum(dh * (xh * g), axis=0, keepdims=True)
        dn = dh * (1.0 + sc_ref[...])
        red_ref[2:3, :] = jnp.sum(dn * xh, axis=0, keepdims=True)
        red_ref[3:, :] = jnp.zeros((SUBLANES - 3, D), F32)
        dxh = dn * g
        dx = r * (dxh - xh * jnp.mean(dxh * xh, axis=-1, keepdims=True))
        dx_ref[...] = dx + jnp.where(t >= nct, dy1_ref[...], 0.0)

    def mrow(b, t):
        return jnp.where(t < nct, B, b)

    def seg(s):
        return pl.BlockSpec((None, None, tm, SEG), lambda b, t, s=s: (b, s, t, 0))

    return _call_hosting(
        body, hosted, name="inproj_bwd", grid=(B, nt),
        out_shape=(jax.ShapeDtypeStruct((B, N, D), F32), jax.ShapeDtypeStruct((B, nt, SUBLANES, D), F32)),
        in_specs=[seg(s) for s in range(nseg_r)] + [seg(s) for s in range(nseg_n)]
                 + [ctx_spec(D), lat_spec(D), lat_spec(D),
                    pl.BlockSpec((None, None, 1, D), lambda b, t: (mrow(b, t), 1, 0, 0)),
                    pl.BlockSpec((1, D), lambda b, t: (0, 0)),
                    pl.BlockSpec((nw, D), lambda b, t: (0, 0))],
        out_specs=(lat_spec(D), pl.BlockSpec((None, None, SUBLANES, D), lambda b, t: (b, t, 0, 0))),
        scratch_shapes=[], args=(*([dret] * nseg_r), *([dna] * nseg_n), ctx, x, dy1, modl, g1, w_in_t))


def _tn_matmul(lhs, rhs, name):
    B, S, T, W = lhs.shape
    nn = rhs.shape[-1]
    tk = _div_tile(T, 1024, LANES)
    bm = _div_tile(W, 1024, LANES)
    bn = _div_tile(nn, 1024, LANES)
    nkt = T // tk
    nk = B * nkt

    def body(l_ref, r_ref, o_ref, acc):
        k = pl.program_id(3)

        @pl.when(k == 0)
        def _():
            acc[...] = jnp.zeros(acc.shape, F32)

        acc[...] += _dot_tn(l_ref[...].astype(BF16), r_ref[...].astype(BF16))

        @pl.when(k == nk - 1)
        def _():
            o_ref[...] = acc[...].astype(BF16)

    nwb = W // bm
    return pl.pallas_call(
        functools.partial(body), name=name, grid=(S, nwb, nn // bn, nk),
        out_shape=jax.ShapeDtypeStruct((S * W, nn), BF16),
        in_specs=[pl.BlockSpec((None, None, tk, bm), lambda s, i, j, k: (k // nkt, s, k % nkt, i)),
                  pl.BlockSpec((None, tk, bn), lambda s, i, j, k: (k // nkt, k % nkt, j))],
        out_specs=pl.BlockSpec((bm, bn), lambda s, i, j, k: (s * nwb + i, j)),
        scratch_shapes=[pltpu.VMEM((bm, bn), F32)],
        compiler_params=_params("parallel", "parallel", "parallel", "arbitrary"),
    )(lhs, rhs)


class _SplitScatter:
    def __init__(self, gs, block_ofs, block_shapes, name):
        self.n = n = len(gs)
        self.block_ofs = block_ofs
        land_shapes = [(N_DEV,) + tuple(bs) for bs in block_shapes]
        hbm = pl.BlockSpec(memory_space=pltpu.HBM)
        sem = pl.BlockSpec(memory_space=pltpu.SEMAPHORE)

        def body(*refs):
            g_refs, land_refs = refs[:n], refs[n:2 * n]
            send_sems, recv_sems, own_sems = refs[2 * n:2 * n + 3]
            token = refs[-1]
            for own, pushes in self._copies(g_refs, land_refs, send_sems, recv_sems, own_sems, landing="sender"):
                own.start()
                for cp in pushes:
                    cp.start()
            token[...] = jnp.zeros_like(token)

        outs = pl.pallas_call(
            body, name=name,
            out_shape=(pltpu.SemaphoreType.DMA((n * (N_DEV - 1),)), pltpu.SemaphoreType.DMA((n * (N_DEV - 1),)),
                       pltpu.SemaphoreType.DMA((n,)))
                      + tuple(pltpu.HBM(g.shape, g.dtype) for g in gs)
                      + tuple(pltpu.HBM(s, g.dtype) for s, g in zip(land_shapes, gs))
                      + (jax.ShapeDtypeStruct((SUBLANES, LANES), F32),),
            in_specs=(hbm,) * (2 * n), out_specs=(sem,) * 3 + (hbm,) * (2 * n) + (_vmem(),),
            input_output_aliases={k: 3 + k for k in range(2 * n)},
            compiler_params=pltpu.CompilerParams(has_side_effects=pltpu.SideEffectType.DATAFLOW_SIDE_EFFECTING),
        )(*[pltpu.with_memory_space_constraint(g, pltpu.HBM) for g in gs],
          *[pltpu.with_memory_space_constraint(lax.empty(s, g.dtype), pltpu.HBM) for s, g in zip(land_shapes, gs)])
        self.sems, self.thru, self.token = outs[:3], outs[3:3 + 2 * n], outs[-1]

    def _copies(self, g_refs, land_refs, send_sems, recv_sems, own_sems, landing):
        me, peers = _me_and_peers()
        out = []
        for k in range(self.n):
            src = self.block_ofs[k](g_refs[k])
            own = pltpu.make_async_copy(src(me), land_refs[k].at[me], own_sems.at[k])
            pushes = [_remote(src(pid), land_refs[k].at[me if landing == "sender" else pid],
                              send_sems.at[k * (N_DEV - 1) + i], recv_sems.at[k * (N_DEV - 1) + i], dev)
                      for i, (dev, pid) in enumerate(peers)]
            out.append((own, pushes))
        return out


def _scatter_wait(scatters, after, name):
    hbm = pl.BlockSpec(memory_space=pltpu.HBM)
    sem = pl.BlockSpec(memory_space=pltpu.SEMAPHORE)
    n_arr = [2 * sc.n for sc in scatters]
    total = sum(n_arr)

    def body(*refs):
        arrs, sems = refs[:total], refs[total:total + 3 * len(scatters)]
        a0 = 0
        for j, sc in enumerate(scatters):
            g_refs, land_refs = arrs[a0:a0 + sc.n], arrs[a0 + sc.n:a0 + 2 * sc.n]
            a0 += 2 * sc.n
            send_sems, recv_sems, own_sems = sems[3 * j:3 * j + 3]
            for (own, sent), (_, got) in zip(sc._copies(g_refs, land_refs, send_sems, recv_sems, own_sems, "sender"),
                                             sc._copies(g_refs, land_refs, send_sems, recv_sems, own_sems, "receiver")):
                own.wait()
                for cp in sent:
                    cp.wait_send()
                for cp in got:
                    cp.wait_recv()

    operands = [a for sc in scatters for a in sc.thru]
    outs = pl.pallas_call(
        body, name=name,
        out_shape=tuple(pltpu.HBM(a.shape, a.dtype) for a in operands),
        in_specs=(hbm,) * total + (sem,) * (3 * len(scatters)) + (pl.BlockSpec(memory_space=pl.ANY),),
        out_specs=(hbm,) * total, input_output_aliases={k: k for k in range(total)},
        compiler_params=pltpu.CompilerParams(has_side_effects=pltpu.SideEffectType.DATAFLOW_SIDE_EFFECTING),
    )(*operands, *[s for sc in scatters for s in sc.sems], after)
    lands, a0 = [], 0
    for sc in scatters:
        lands.extend(outs[a0 + sc.n:a0 + 2 * sc.n])
        a0 += 2 * sc.n
    return lands


def _sum_slots(buf, name):
    _, rows, cols = buf.shape
    tr = _div_tile(rows, 256, 2 * SUBLANES)

    def body(b_ref, o_ref):
        acc = b_ref[0].astype(F32)
        for k in range(1, N_DEV):
            acc = acc + b_ref[k].astype(F32)
        o_ref[...] = acc

    return pl.pallas_call(
        functools.partial(body), name=name, grid=(rows // tr,),
        out_shape=jax.ShapeDtypeStruct((rows, cols), F32),
        in_specs=[pl.BlockSpec((N_DEV, tr, cols), lambda i: (0, i, 0))],
        out_specs=pl.BlockSpec((tr, cols), lambda i: (i, 0)),
        compiler_params=_params("parallel"),
    )(buf)


def _small_ar(vec, dmods, silu_all, w_ada, c_ctx):
    rv = vec.shape[0]
    D = silu_all.shape[1]
    ncol = w_ada.shape[1]
    nm = dmods.shape[1]
    srows = silu_all.shape[0]

    def body(vec_ref, dm_ref, s_ref, w_ref, cc_ref, tot_ref, gb_ref, gw_ref, gc_ref,
             vbuf, mbuf, tbuf, dmx, send1, recv1, send3, recv3):
        me, _ = _me_and_peers()
        vbuf[me] = vec_ref[...]
        mbuf[me] = dm_ref[...]
        both = [(lambda p: vbuf.at[me], lambda p: vbuf.at[p]), (lambda p: mbuf.at[me], lambda p: mbuf.at[p])]
        _push_start(both, ALL_PEERS, send1, recv1)
        _push_wait_recv(both, ALL_PEERS, send1, recv1)
        _push_wait_send(both, ALL_PEERS, send1, recv1)
        tot = vbuf[0]
        msum = mbuf[0]
        for k in range(1, N_DEV):
            tot = tot + vbuf[k]
            msum = msum + mbuf[k]
        tot_ref[...] = tot
        gb_ref[...] = jnp.sum(msum, axis=0, keepdims=True)
        loc = pl.ds(pl.multiple_of(me * ncol, ncol), ncol)
        for k in range(N_DEV):
            dmx[k * SUBLANES:(k + 1) * SUBLANES, :] = mbuf[k, :, loc]
        cm = msum[2:3, :]
        mbuf[0, 2:3, :] = cm
        cm_loc = mbuf[0, 2:3, loc]
        dmx[N_DEV * SUBLANES:, :] = jnp.concatenate([cm_loc, jnp.zeros((SUBLANES - 1, ncol), F32)], axis=0)
        gw_ref[...] = _dot_tn(s_ref[...], dmx[...])
        tbuf[me] = _dot_nt(dmx[N_DEV * SUBLANES:, :], w_ref[...])
        _exchange(lambda p: tbuf.at[me], lambda p: tbuf.at[p], send3, recv3)
        tsum = tbuf[0]
        for k in range(1, N_DEV):
            tsum = tsum + tbuf[k]
        cc = cc_ref[...]
        sg = _sigmoid(cc)
        gc_ref[...] = tsum[0:1, :] * (sg * (1.0 + cc * (1.0 - sg)))

    return pl.pallas_call(
        body, name="small_ar",
        out_shape=(jax.ShapeDtypeStruct((rv, LANES), F32), jax.ShapeDtypeStruct((1, nm), F32),
                   jax.ShapeDtypeStruct((D, ncol), F32), jax.ShapeDtypeStruct((1, D), F32)),
        in_specs=[_vmem()] * 5, out_specs=(_vmem(),) * 4,
        scratch_shapes=[pltpu.VMEM((N_DEV, rv, LANES), F32), pltpu.VMEM((N_DEV, SUBLANES, nm), F32),
                        pltpu.VMEM((N_DEV, SUBLANES, D), F32), pltpu.VMEM((srows, ncol), F32)]
                       + [pltpu.SemaphoreType.DMA((2, N_DEV - 1))] * 2 + [pltpu.SemaphoreType.DMA((N_DEV - 1,))] * 2,
        compiler_params=pltpu.CompilerParams(vmem_limit_bytes=VMEM_LIMIT),
    )(vec, dmods, silu_all, w_ada, c_ctx.reshape(1, D))


def _adam_update(w, g, m, v):
    mn = ADAM_B1 * m + (1.0 - ADAM_B1) * g
    vn = ADAM_B2 * v + (1.0 - ADAM_B2) * (g * g)
    m_hat = mn / (1.0 - ADAM_B1 ** ADAM_STEP)
    v_hat = vn / (1.0 - ADAM_B2 ** ADAM_STEP)
    return -ADAM_LR * (m_hat / (jnp.sqrt(v_hat) + ADAM_EPS) + ADAM_WD * w), mn, vn


def _adamw(w, g, m, v, name):
    rows, cols = w.shape
    tr = _div_tile(rows, 256, SUBLANES) if rows * cols > 65536 else rows

    def body(w_ref, g_ref, m_ref, v_ref, d_ref, nm_ref, nv_ref):
        d_ref[...], nm_ref[...], nv_ref[...] = _adam_update(w_ref[...], g_ref[...], m_ref[...], v_ref[...])

    spec = pl.BlockSpec((tr, cols), lambda i: (i, 0))
    return pl.pallas_call(
        functools.partial(body), name=name, grid=(rows // tr,),
        out_shape=(jax.ShapeDtypeStruct((rows, cols), F32),) * 3,
        in_specs=[spec] * 4, out_specs=(spec,) * 3,
        compiler_params=_params("parallel"),
    )(w, g, m, v)


def _sum_adamw(buf, w, m, v, name):
    _, rows, cols = buf.shape
    tr = _div_tile(rows, 256, 2 * SUBLANES)

    def body(b_ref, w_ref, m_ref, v_ref, g_ref, d_ref, nm_ref, nv_ref):
        g = b_ref[0].astype(F32)
        for k in range(1, N_DEV):
            g = g + b_ref[k].astype(F32)
        g_ref[...] = g
        d_ref[...], nm_ref[...], nv_ref[...] = _adam_update(w_ref[...], g, m_ref[...], v_ref[...])

    spec = pl.BlockSpec((tr, cols), lambda i: (i, 0))
    return pl.pallas_call(
        functools.partial(body), name=name, grid=(rows // tr,),
        out_shape=(jax.ShapeDtypeStruct((rows, cols), F32),) * 4,
        in_specs=[pl.BlockSpec((N_DEV, tr, cols), lambda i: (0, i, 0))] + [spec] * 3, out_specs=(spec,) * 4,
        compiler_params=_params("parallel"),
    )(buf, w, m, v)


def _rope_tables(n_ctx, n):
    n_freq = RET_DIM // 4
    inv = np.float32(ROPE_BASE) ** (-np.arange(n_freq, dtype=np.float32) / np.float32(n_freq))
    tok = np.arange(n)
    pos_r = (tok // GRID_W).astype(np.float32)
    pos_c = (tok % GRID_W).astype(np.float32)
    ang_r = (pos_r[:, None] * inv[None, :]).astype(np.float32)
    ang_c = (pos_c[:, None] * inv[None, :]).astype(np.float32)
    cos = np.concatenate([np.cos(ang_r), np.cos(ang_r), np.cos(ang_c), np.cos(ang_c)], axis=-1)
    sin = np.concatenate([-np.sin(ang_r), np.sin(ang_r), -np.sin(ang_c), np.sin(ang_c)], axis=-1)
    cos = np.concatenate([np.ones((n_ctx, RET_DIM), np.float32), cos], axis=0)
    sin = np.concatenate([np.zeros((n_ctx, RET_DIM), np.float32), sin], axis=0)
    return jnp.asarray(cos, F32), jnp.asarray(sin, F32)


def _na_tables():
    q = np.arange(GRID_W)[:, None]
    k = np.arange(GRID_W)[None, :]
    start = np.clip(q - NA_KW // 2, 0, GRID_W - NA_KW)
    valid = (k >= start) & (k < start + NA_KW)
    dc = np.clip(k - q + (NA_KW - 1), 0, 2 * NA_KW - 2)
    ncls = 2 * NA_KW - 1
    onehot = (dc[None] == np.arange(ncls)[:, None, None]) & valid[None]
    oh2 = np.zeros((GRID_W, LANES, LANES), np.float32)
    for c in range(ncls):
        oh2[:, :GRID_W, c] = onehot[c]
        oh2[:, GRID_W:, 32 + c] = onehot[c]
    return onehot.astype(np.float32), valid, oh2.reshape(GRID_W * LANES, LANES)


def _paired_bias(rpb, onehot, valid):
    t = jnp.einsum("hdc,cqk->hdqk", rpb, jnp.asarray(onehot), precision=lax.Precision.HIGHEST)
    t = jnp.where(jnp.asarray(valid)[None, None], t, NEG_INF)
    return jnp.concatenate([t[:, :-1], t[:, 1:]], axis=-1)


def kernel(x, c, ctx, c_ctx, w_ada, b_ada, g_pre_mix, g_post_mix, g_pre_mlp, g_post_mlp, w_in, ret_decay, ret_gn, na_rpb, w_out, w_mlp1, w_mlp2, loss_target, m_c_ctx, m_w_ada, m_b_ada, m_g_pre_mix, m_g_post_mix, m_g_pre_mlp, m_g_post_mlp, m_w_in, m_ret_decay, m_ret_gn, m_na_rpb, m_w_out, m_w_mlp1, m_w_mlp2, v_c_ctx, v_w_ada, v_b_ada, v_g_pre_mix, v_g_post_mix, v_g_pre_mlp, v_g_post_mlp, v_w_in, v_ret_decay, v_ret_gn, v_na_rpb, v_w_out, v_w_mlp1, v_w_mlp2):
    B, N, D = x.shape
    C = ctx.shape[1]
    T = C + N

    silu_all, mods_g, win_b, wout_l, w1_l, w2_l = _mod_gather(c, c_ctx, w_ada[0], b_ada, w_in[0].T, w_out[0],
                                                             w_mlp1[0], w_mlp2[0])
    mods_mine = mods_g.transpose(1, 0, 2).reshape(mods_g.shape[1], N_MOD * D)
    modl = jnp.concatenate([mods_mine[:B], mods_mine[SUBLANES:SUBLANES + 1]], axis=0)
    modl = modl.reshape(B + 1, N_MOD, 1, D)
    rin = w_in.shape[2]
    rout, c1, r2 = wout_l.shape[0], w1_l.shape[1], w2_l.shape[0]

    def rows_of(n):
        return lambda ref: _row_block(ref, n)

    def cols_of(n):
        return lambda ref: _col_block(ref, n)

    cos, sin = _rope_tables(C, N)
    onehot, valid, oh2 = _na_tables()
    bias2 = _paired_bias(na_rpb[0], onehot, valid)
    lg = jax.nn.log_sigmoid(ret_decay[0].astype(F32))

    level_one = SIBLING + ICI_SAME_CORE
    h_all, proj, w1_part = _inproj_fwd(
        x, ctx, modl, g_pre_mix, win_b,
        [_Hosted("gather", level_one, [w1_l], [cols_of(c1)], [jax.ShapeDtypeStruct((D, N_DEV * c1), BF16)], True)])
    o_ret, lat_ret, wout_b = _ret_fwd(
        proj, cos, sin, lg, ret_gn, C,
        [_Hosted("gather", ALL_PEERS, [wout_l], [rows_of(rout)], [jax.ShapeDtypeStruct((N_DEV * rout, D), BF16)], True)])
    lat_na, w1_b, w2_part = _na_fwd(
        proj, bias2, C,
        [_HostedRelay([w1_part], [cols_of(c1)]),
         _Hosted("gather", level_one, [w2_l], [rows_of(r2)], [jax.ShapeDtypeStruct((N_DEV * r2, D), BF16)], True)])

    (dy1, dlat_ret, dlat_na, dmix, h2, act, du, dz, red_d) = _dense_core(
        lat_ret, lat_na, x, loss_target, modl, g_post_mix, g_pre_mlp, g_post_mlp, wout_b, w1_b, w2_part)

    gw_out_p = jnp.concatenate([_tn_matmul(lat_ret[:, None], dmix, "gw_out_ret"),
                                _tn_matmul(lat_na[:, None], dmix, "gw_out_na")], axis=0)
    gw1_p = _tn_matmul(h2[:, None], du, "gw_mlp1")
    gw2_p = _tn_matmul(act[:, None], dz, "gw_mlp2")
    rs_mlp = _SplitScatter([gw_out_p, gw1_p, gw2_p], [rows_of(rout), cols_of(c1), rows_of(r2)],
                           [(rout, D), (D, c1), (r2, D)], "rs_mlp_start")

    dret, dgn_p, dlg_p = _ret_bwd(proj, cos, sin, lg, ret_gn + rs_mlp.token[0, 0], o_ret, dlat_ret, C, [])
    dna, dbias2 = _na_bwd(proj, bias2, dlat_na, C, [])
    gwin_t_p = jnp.concatenate([_tn_matmul(dret, h_all, "gw_in_ret"), _tn_matmul(dna, h_all, "gw_in_na")], axis=0)
    rs_in = _SplitScatter([gwin_t_p], [rows_of(rin)], [(rin, D)], "rs_w_in_start")
    grad_x, red_i = _inproj_bwd(dret, dna, x, ctx, dy1, modl, g_pre_mix + rs_in.token[0, 0], win_b, [])

    rd = red_d.sum(axis=1)[:, :, :]
    ri = red_i
    nct = ri.shape[1] * C // T
    ri_ctx = ri[:, :nct].sum(axis=(0, 1))
    ri_lat = ri[:, nct:].sum(axis=1)
    d_mods = jnp.concatenate([ri_lat[:, 0], ri_lat[:, 1], rd[:, 0], rd[:, 4], rd[:, 3], rd[:, 2]], axis=-1)
    d_cmods = jnp.concatenate([ri_ctx[0], ri_ctx[1], jnp.zeros(((N_MOD - 2) * D,), F32)])[None]
    dm_slot = jnp.concatenate([d_mods, d_cmods, jnp.zeros((SUBLANES - B - 1, N_MOD * D), F32)], axis=0)
    dg_pre_mix = ri_lat[:, 2].sum(axis=0) + ri_ctx[2]
    dg_post_mix = rd[:, 1].sum(axis=0)
    dg_pre_mlp = rd[:, 5].sum(axis=0)
    dg_post_mlp = rd[:, 6].sum(axis=0)
    loss_p = rd[:, 7, 0].sum()
    d_gn = dgn_p[:, 0].sum(axis=0)
    d_lg = dlg_p[:, :, :2, 0].sum(axis=0).T
    d_decay = d_lg * jax.nn.sigmoid(-ret_decay[0].astype(F32))
    rr = _rpb_reduce(dbias2, jnp.asarray(oh2, BF16)).reshape(NA_HEADS, 2 * NA_KH - 2, LANES)
    ncls = 2 * NA_KW - 1
    d_rpb = (jnp.pad(rr[:, :, :ncls], ((0, 0), (0, 1), (0, 0))) + jnp.pad(rr[:, :, 32:32 + ncls], ((0, 0), (1, 0), (0, 0))))
    d_rpb32 = jnp.pad(d_rpb, ((0, 0), (0, 0), (0, 32 - ncls)))
    pieces = [dg_pre_mix, dg_post_mix, dg_pre_mlp, dg_post_mlp, d_gn, d_rpb32.reshape(-1),
              jnp.pad(d_decay.reshape(-1), (0, LANES - d_decay.size)), jnp.full((LANES,), loss_p, F32)]
    vec = jnp.concatenate(pieces)
    pad = (-vec.shape[0]) % (SUBLANES * LANES)
    vec = jnp.pad(vec, (0, pad)).reshape(-1, LANES)
    tot, g_b_ada, g_w_ada, g_c_ctx = _small_ar(vec, dm_slot, silu_all, w_ada[0], c_ctx)
    land_out, land_1, land_2, land_in = _scatter_wait([rs_mlp, rs_in], tot, "rs_wait")
    g_w_in = _sum_slots(land_in, "sum_w_in").T
    fused = {"w_out": _sum_adamw(land_out, w_out[0], m_w_out[0], v_w_out[0], "sum_adamw_w_out"),
             "w_mlp1": _sum_adamw(land_1, w_mlp1[0], m_w_mlp1[0], v_w_mlp1[0], "sum_adamw_w_mlp1"),
             "w_mlp2": _sum_adamw(land_2, w_mlp2[0], m_w_mlp2[0], v_w_mlp2[0], "sum_adamw_w_mlp2")}
    flat = tot.reshape(-1)
    o0 = 0
    g_pre_mix_g = flat[o0:o0 + D]; o0 += D
    g_post_mix_g = flat[o0:o0 + D]; o0 += D
    g_pre_mlp_g = flat[o0:o0 + D]; o0 += D
    g_post_mlp_g = flat[o0:o0 + D]; o0 += D
    g_gn = flat[o0:o0 + RET_WIDTH]; o0 += RET_WIDTH
    nrpb = NA_HEADS * (2 * NA_KH - 1) * 32
    g_rpb = flat[o0:o0 + nrpb].reshape(NA_HEADS, 2 * NA_KH - 1, 32)[:, :, :ncls]; o0 += nrpb
    g_decay = flat[o0:o0 + 2 * RET_HEADS].reshape(2, RET_HEADS); o0 += LANES
    loss = flat[o0]

    grads = {
        "c_ctx": g_c_ctx.reshape(c_ctx.shape), "w_ada": g_w_ada[None], "b_ada": g_b_ada.reshape(b_ada.shape),
        "g_pre_mix": g_pre_mix_g[None], "g_post_mix": g_post_mix_g[None], "g_pre_mlp": g_pre_mlp_g[None],
        "g_post_mlp": g_post_mlp_g[None], "w_in": g_w_in[None], "ret_decay": g_decay[None], "ret_gn": g_gn[None],
        "na_rpb": g_rpb[None], "w_out": fused["w_out"][0][None], "w_mlp1": fused["w_mlp1"][0][None],
        "w_mlp2": fused["w_mlp2"][0][None],
    }
    weights = dict(c_ctx=c_ctx, w_ada=w_ada, b_ada=b_ada, g_pre_mix=g_pre_mix, g_post_mix=g_post_mix,
                   g_pre_mlp=g_pre_mlp, g_post_mlp=g_post_mlp, w_in=w_in, ret_decay=ret_decay, ret_gn=ret_gn,
                   na_rpb=na_rpb, w_out=w_out, w_mlp1=w_mlp1, w_mlp2=w_mlp2)
    m_in = dict(c_ctx=m_c_ctx, w_ada=m_w_ada, b_ada=m_b_ada, g_pre_mix=m_g_pre_mix, g_post_mix=m_g_post_mix,
                g_pre_mlp=m_g_pre_mlp, g_post_mlp=m_g_post_mlp, w_in=m_w_in, ret_decay=m_ret_decay,
                ret_gn=m_ret_gn, na_rpb=m_na_rpb, w_out=m_w_out, w_mlp1=m_w_mlp1, w_mlp2=m_w_mlp2)
    v_in = dict(c_ctx=v_c_ctx, w_ada=v_w_ada, b_ada=v_b_ada, g_pre_mix=v_g_pre_mix, g_post_mix=v_g_post_mix,
                g_pre_mlp=v_g_pre_mlp, g_post_mlp=v_g_post_mlp, w_in=v_w_in, ret_decay=v_ret_decay,
                ret_gn=v_ret_gn, na_rpb=v_na_rpb, w_out=v_w_out, w_mlp1=v_w_mlp1, w_mlp2=v_w_mlp2)
    names = list(weights)
    deltas, new_m, new_v = {}, {}, {}
    for n in names:
        shp = weights[n].shape
        if n in fused:
            deltas[n], new_m[n], new_v[n] = (a.reshape(shp) for a in fused[n][1:])
            continue
        two_d = (-1, shp[-1]) if len(shp) > 1 else (1, shp[0])
        d, nm, nv = _adamw(weights[n].reshape(two_d), grads[n].reshape(two_d), m_in[n].reshape(two_d),
                           v_in[n].reshape(two_d), "adamw_" + n)
        deltas[n], new_m[n], new_v[n] = d.reshape(shp), nm.reshape(shp), nv.reshape(shp)
    return (loss, grad_x, *[grads[n] for n in names], *[deltas[n] for n in names],
            *[new_m[n] for n in names], *[new_v[n] for n in names])
```

```python
import functools
import math

import numpy as np
import jax
import jax.numpy as jnp
from jax import lax
from jax.experimental import pallas as pl
from jax.experimental.pallas import tpu as pltpu

F32 = jnp.float32
BF16 = jnp.bfloat16
MESH = pl.DeviceIdType.MESH

N_DEV = 8
LANES = 128
SUBLANES = 8
VMEM_LIMIT = 60 * 1024 * 1024

GRID_W = 64
RET_HEADS = 4
RET_DIM = 128
RET_WIDTH = RET_HEADS * RET_DIM
NA_HEADS = 8
NA_DIM = 64
NA_WIDTH = NA_HEADS * NA_DIM
NA_PAIRS = NA_HEADS // 2
NA_KH = 8
NA_KW = 16
NA_GROUP = 8
SEG = 512
ROPE_BASE = 10000.0
NORM_EPS = 1e-6
NEG_INF = -1e30
N_MOD = 6

ADAM_LR = 0.001
ADAM_B1 = 0.9
ADAM_B2 = 0.999
ADAM_EPS = 1e-08
ADAM_WD = 0.01
ADAM_STEP = 10


def _dot(a, b):
    return lax.dot_general(a, b, (((1,), (0,)), ((), ())), preferred_element_type=F32)


def _dot_nt(a, b):
    return lax.dot_general(a, b, (((1,), (1,)), ((), ())), preferred_element_type=F32)


def _dot_tn(a, b):
    return lax.dot_general(a, b, (((0,), (0,)), ((), ())), preferred_element_type=F32)


def _sigmoid(x):
    return 1.0 / (1.0 + jnp.exp(-x))


def _div_tile(n, cap, mult):
    if n <= cap:
        return n
    for t in range(cap - cap % mult, 0, -mult):
        if n % t == 0:
            return t
    raise ValueError(f"no tile for {n}")


def _params(*sem):
    return pltpu.CompilerParams(dimension_semantics=tuple(sem) if sem else None,
                                vmem_limit_bytes=VMEM_LIMIT)


def _vmem():
    return pl.BlockSpec(memory_space=pltpu.VMEM)


def _any():
    return pl.BlockSpec(memory_space=pl.ANY)


def _me_and_peers():
    x, y, c = lax.axis_index("x"), lax.axis_index("y"), lax.axis_index("c")
    me = 4 * x + 2 * y + c
    peers = []
    for m in range(1, N_DEV):
        px = 1 - x if (m >> 2) & 1 else x
        py = 1 - y if (m >> 1) & 1 else y
        pc = 1 - c if m & 1 else c
        peers.append(((px, py, pc), 4 * px + 2 * py + pc))
    return me, peers


def _exchange(src_for, dst_from, send_sems, recv_sems):
    me, peers = _me_and_peers()
    sent = []
    for i, (dev, pid) in enumerate(peers):
        cp = pltpu.make_async_remote_copy(src_ref=src_for(pid), dst_ref=dst_from(me),
                                          send_sem=send_sems.at[i], recv_sem=recv_sems.at[i],
                                          device_id=dev, device_id_type=MESH)
        cp.start()
        sent.append(cp)
    for i, (dev, pid) in enumerate(peers):
        pltpu.make_async_remote_copy(src_ref=src_for(pid), dst_ref=dst_from(pid),
                                     send_sem=send_sems.at[i], recv_sem=recv_sems.at[i],
                                     device_id=dev, device_id_type=MESH).wait_recv()
    for cp in sent:
        cp.wait_send()


SIBLING = (1,)
ICI_SAME_CORE = (2, 4, 6)
ALL_PEERS = tuple(range(1, N_DEV))


def _remote(src, dst, send_sem, recv_sem, dev):
    return pltpu.make_async_remote_copy(src_ref=src, dst_ref=dst, send_sem=send_sem, recv_sem=recv_sem,
                                        device_id=dev, device_id_type=MESH)


def _push_start(items, masks, send_sems, recv_sems):
    me, peers = _me_and_peers()
    for k, (src_for, dst_from) in enumerate(items):
        for m in masks:
            dev, pid = peers[m - 1]
            _remote(src_for(pid), dst_from(me), send_sems.at[k, m - 1], recv_sems.at[k, m - 1], dev).start()


def _push_wait_recv(items, masks, send_sems, recv_sems):
    me, peers = _me_and_peers()
    for k, (src_for, dst_from) in enumerate(items):
        for m in masks:
            dev, pid = peers[m - 1]
            _remote(src_for(pid), dst_from(pid), send_sems.at[k, m - 1], recv_sems.at[k, m - 1], dev).wait_recv()


def _push_wait_send(items, masks, send_sems, recv_sems):
    me, peers = _me_and_peers()
    for k, (src_for, dst_from) in enumerate(items):
        for m in masks:
            dev, pid = peers[m - 1]
            _remote(src_for(pid), dst_from(me), send_sems.at[k, m - 1], recv_sems.at[k, m - 1], dev).wait_send()


def _forward_start(items, send_sems, recv_sems):
    me, peers = _me_and_peers()
    sib = peers[0][0]
    for k, (blk_in, blk_out) in enumerate(items):
        for j, m in enumerate(ICI_SAME_CORE):
            pid = peers[m - 1][1]
            _remote(blk_in(pid), blk_out(pid), send_sems.at[k, j], recv_sems.at[k, j], sib).start()


def _forward_wait(items, send_sems, recv_sems):
    me, peers = _me_and_peers()
    sib = peers[0][0]
    for k, (blk_in, blk_out) in enumerate(items):
        for j, m in enumerate(ICI_SAME_CORE):
            got = peers[(m | 1) - 1][1]
            _remote(blk_in(got), blk_out(got), send_sems.at[k, j], recv_sems.at[k, j], sib).wait_recv()
    for k, (blk_in, blk_out) in enumerate(items):
        for j, m in enumerate(ICI_SAME_CORE):
            pid = peers[m - 1][1]
            _remote(blk_in(pid), blk_out(pid), send_sems.at[k, j], recv_sems.at[k, j], sib).wait_send()


def _mod_gather(c, c_ctx, w_ada, b_ada, w_in_t, w_out, w1, w2):
    B, D = c.shape
    ncol = w_ada.shape[1]
    rows = SUBLANES * N_DEV + SUBLANES

    def body(c_ref, cc_ref, w_ref, b_ref, win_ref, wout_ref, w1_ref, w2_ref,
             s_ref, m_ref, gin_ref, wout_b, w1_b, w2_b,
             win_b, msend, send1, recv1, send2, recv2, wsend, wrecv, fsend, frecv, lsem):
        me, _ = _me_and_peers()
        win_b[...] = win_ref[...].astype(BF16)
        block = _row_block(gin_ref, w_in_t.shape[0])
        gather = [(lambda p: win_b, block)]
        own = pltpu.make_async_copy(win_b, block(me), lsem.at[0])
        own.start()
        _push_start(gather, SIBLING + ICI_SAME_CORE, wsend, wrecv)
        wout_b[...] = wout_ref[...].astype(BF16)
        w1_b[...] = w1_ref[...].astype(BF16)
        w2_b[...] = w2_ref[...].astype(BF16)
        cv = c_ref[...]
        slot = jnp.concatenate([cv * _sigmoid(cv), jnp.zeros((SUBLANES - B, D), F32)], axis=0)
        my_rows = pl.ds(pl.multiple_of(me * SUBLANES, SUBLANES), SUBLANES)
        s_ref[my_rows, :] = slot
        ccv = cc_ref[...]
        s_ref[SUBLANES * N_DEV:, :] = jnp.concatenate(
            [ccv * _sigmoid(ccv), jnp.zeros((SUBLANES - 1, D), F32)], axis=0)

        def rows_of(p):
            return s_ref.at[pl.ds(pl.multiple_of(p * SUBLANES, SUBLANES), SUBLANES), :]

        _exchange(lambda p: rows_of(me), rows_of, send1, recv1)
        b_loc = b_ref[:, pl.ds(pl.multiple_of(me * ncol, ncol), ncol)]
        mods = _dot(s_ref[...], w_ref[...]) + b_loc
        for p in range(N_DEV):
            msend[p] = jnp.concatenate([mods[p * SUBLANES:(p + 1) * SUBLANES], mods[N_DEV * SUBLANES:]], axis=0)
        m_ref[me] = msend[me]
        _exchange(lambda p: msend.at[p], lambda p: m_ref.at[p], send2, recv2)
        _push_wait_recv(gather, ICI_SAME_CORE, wsend, wrecv)
        relay = [(block, block)]
        _forward_start(relay, fsend, frecv)
        _push_wait_recv(gather, SIBLING, wsend, wrecv)
        _forward_wait(relay, fsend, frecv)
        _push_wait_send(gather, SIBLING + ICI_SAME_CORE, wsend, wrecv)
        own.wait()

    return pl.pallas_call(
        body, name="mod_gather",
        out_shape=(jax.ShapeDtypeStruct((rows, D), F32), jax.ShapeDtypeStruct((N_DEV, 2 * SUBLANES, ncol), F32),
                   jax.ShapeDtypeStruct((N_DEV * w_in_t.shape[0], D), BF16),
                   jax.ShapeDtypeStruct(w_out.shape, BF16), jax.ShapeDtypeStruct(w1.shape, BF16),
                   jax.ShapeDtypeStruct(w2.shape, BF16)),
        in_specs=[_vmem()] * 8, out_specs=(_vmem(), _vmem(), _any(), _vmem(), _vmem(), _vmem()),
        scratch_shapes=[pltpu.VMEM(w_in_t.shape, BF16), pltpu.VMEM((N_DEV, 2 * SUBLANES, ncol), F32)]
                       + [pltpu.SemaphoreType.DMA((N_DEV - 1,))] * 4
                       + [pltpu.SemaphoreType.DMA((1, N_DEV - 1))] * 2 + [pltpu.SemaphoreType.DMA((1, 3))] * 2
                       + [pltpu.SemaphoreType.DMA((1,))],
        compiler_params=pltpu.CompilerParams(vmem_limit_bytes=VMEM_LIMIT),
    )(c, c_ctx.reshape(1, D), w_ada, b_ada, w_in_t, w_out, w1, w2)


def _row_block(ref, rows):
    return lambda p: ref.at[pl.ds(pl.multiple_of(p * rows, 2 * SUBLANES), rows), :]


def _col_block(ref, cols):
    return lambda p: ref.at[:, pl.ds(pl.multiple_of(p * cols, LANES), cols)]


def _slot(ref):
    return lambda p: ref.at[p]


class _Hosted:
    def __init__(self, kind, masks, operands, block_of, out_shapes, with_own):
        self.kind, self.masks, self.operands = kind, masks, list(operands)
        self.block_of, self.out_shapes, self.with_own = block_of, list(out_shapes), with_own
        self.n = len(self.operands)

    def scratch(self):
        return [pltpu.SemaphoreType.DMA((self.n, N_DEV - 1)), pltpu.SemaphoreType.DMA((self.n, N_DEV - 1)),
                pltpu.SemaphoreType.DMA((self.n,))]

    def _items(self, in_refs, out_refs):
        items = []
        for k in range(self.n):
            if self.kind == "gather":
                items.append((lambda p, k=k: in_refs[k], self.block_of[k](out_refs[k])))
            else:
                items.append((self.block_of[k](in_refs[k]), _slot(out_refs[k])))
        return items

    def _own(self, in_refs, out_refs, lsem):
        me, _ = _me_and_peers()
        items = self._items(in_refs, out_refs)
        return [pltpu.make_async_copy(src_for(me), dst_from(me), lsem.at[k])
                for k, (src_for, dst_from) in enumerate(items)]

    def start(self, in_refs, out_refs, sems):
        send, recv, lsem = sems
        if self.with_own:
            for cp in self._own(in_refs, out_refs, lsem):
                cp.start()
        _push_start(self._items(in_refs, out_refs), self.masks, send, recv)

    def wait(self, in_refs, out_refs, sems):
        send, recv, lsem = sems
        items = self._items(in_refs, out_refs)
        _push_wait_recv(items, self.masks, send, recv)
        _push_wait_send(items, self.masks, send, recv)
        if self.with_own:
            for cp in self._own(in_refs, out_refs, lsem):
                cp.wait()


class _HostedRelay:
    def __init__(self, arrays, block_of):
        self.operands, self.block_of = list(arrays), block_of
        self.out_shapes = [jax.ShapeDtypeStruct(a.shape, a.dtype) for a in arrays]
        self.n = len(self.operands)

    def scratch(self):
        return [pltpu.SemaphoreType.DMA((self.n, 3)), pltpu.SemaphoreType.DMA((self.n, 3))]

    def _items(self, in_refs, out_refs):
        return [(self.block_of[k](in_refs[k]), self.block_of[k](out_refs[k])) for k in range(self.n)]

    def start(self, in_refs, out_refs, sems):
        _forward_start(self._items(in_refs, out_refs), *sems)

    def wait(self, in_refs, out_refs, sems):
        _forward_wait(self._items(in_refs, out_refs), *sems)


def _call_hosting(body, hosted, *, name, grid, out_shape, in_specs, out_specs, scratch_shapes, args):
    n_in, n_out, n_scr = len(in_specs), len(out_shape), len(scratch_shapes)
    hn = sum(hs.n for hs in hosted)
    n_sem = [len(hs.scratch()) for hs in hosted]

    def wrapped(*refs):
        ins = refs[:n_in]
        h_in = refs[n_in:n_in + hn]
        outs = refs[n_in + hn:n_in + hn + n_out]
        h_out = refs[n_in + hn + n_out:n_in + 2 * hn + n_out]
        scr = refs[n_in + 2 * hn + n_out:n_in + 2 * hn + n_out + n_scr]
        sems = refs[n_in + 2 * hn + n_out + n_scr:]
        ids = [pl.program_id(i) for i in range(len(grid))]
        first = functools.reduce(jnp.logical_and, [i == 0 for i in ids])
        last = functools.reduce(jnp.logical_and, [i == g - 1 for i, g in zip(ids, grid)])
        parts, o0, s0 = [], 0, 0
        for hs, ns in zip(hosted, n_sem):
            parts.append((hs, h_in[o0:o0 + hs.n], h_out[o0:o0 + hs.n], sems[s0:s0 + ns]))
            o0 += hs.n
            s0 += ns

        @pl.when(first)
        def _():
            for hs, hi, ho, se in parts:
                hs.start(hi, ho, se)

        body(*ins, *outs, *scr)

        @pl.when(last)
        def _():
            for hs, hi, ho, se in parts:
                hs.wait(hi, ho, se)

    aliases, o0 = {}, 0
    for hs in hosted:
        if isinstance(hs, _HostedRelay):
            aliases.update({n_in + o0 + k: n_out + o0 + k for k in range(hs.n)})
        o0 += hs.n
    return pl.pallas_call(
        wrapped, name=name, grid=grid,
        out_shape=tuple(out_shape) + tuple(s for hs in hosted for s in hs.out_shapes),
        in_specs=list(in_specs) + [_any()] * hn,
        out_specs=tuple(out_specs) + (_any(),) * hn,
        scratch_shapes=list(scratch_shapes) + [s for hs in hosted for s in hs.scratch()],
        input_output_aliases=aliases,
        compiler_params=_params(*(("arbitrary",) * len(grid))),
    )(*args, *[a for hs in hosted for a in hs.operands])


def _token_tiles(n_ctx, tm):
    nct = n_ctx // tm

    def ctx_spec(D):
        return pl.BlockSpec((None, tm, D), lambda b, t: (b, jnp.minimum(t, nct - 1), 0))

    def lat_spec(D):
        return pl.BlockSpec((None, tm, D), lambda b, t: (b, jnp.maximum(t - nct, 0), 0))

    return nct, ctx_spec, lat_spec


def _inproj_fwd(x, ctx, modl, g1, w_in_t, hosted):
    B, N, D = x.shape
    n_ctx = ctx.shape[1]
    T = n_ctx + N
    nw = w_in_t.shape[0]
    tm = _div_tile(n_ctx, 256, 16)
    nct, ctx_spec, lat_spec = _token_tiles(n_ctx, tm)

    def body(c_ref, x_ref, sh_ref, sc_ref, g_ref, w_ref, h_ref, p_ref):
        x = jnp.where(pl.program_id(1) < nct, c_ref[...], x_ref[...])
        r = lax.rsqrt(jnp.mean(x * x, axis=-1, keepdims=True) + NORM_EPS)
        h = ((x * r) * g_ref[...]) * (1.0 + sc_ref[...]) + sh_ref[...]
        hb = h.astype(BF16)
        h_ref[...] = hb
        p_ref[...] = _dot_nt(hb, w_ref[...])

    def mrow(b, t):
        return jnp.where(t < nct, B, b)

    return _call_hosting(
        body, hosted, name="inproj_fwd", grid=(B, T // tm),
        out_shape=(jax.ShapeDtypeStruct((B, T, D), BF16), jax.ShapeDtypeStruct((B, T, nw), F32)),
        in_specs=[ctx_spec(D), lat_spec(D),
                  pl.BlockSpec((None, None, 1, D), lambda b, t: (mrow(b, t), 0, 0, 0)),
                  pl.BlockSpec((None, None, 1, D), lambda b, t: (mrow(b, t), 1, 0, 0)),
                  pl.BlockSpec((1, D), lambda b, t: (0, 0)),
                  pl.BlockSpec((nw, D), lambda b, t: (0, 0))],
        out_specs=(pl.BlockSpec((None, tm, D), lambda b, t: (b, t, 0)),
                   pl.BlockSpec((None, tm, nw), lambda b, t: (b, t, 0))),
        scratch_shapes=[], args=(ctx, x, modl, modl, g1, w_in_t))


def _swap32(x):
    lane = lax.broadcasted_iota(jnp.int32, x.shape, 1)
    return jnp.where((lane % 64) < 32, pltpu.roll(x, 96, 1), pltpu.roll(x, 32, 1))


def _rope(x, cos, sin):
    return x * cos + _swap32(x) * sin


def _unrope(dy, cos, sin):
    return dy * cos + _swap32(dy * sin)


def _ret_weights(lgf, lgb, dist):
    return jnp.exp(jnp.where(dist >= 0.0, lgf * dist, -lgb * dist))


class _RetDecay:
    def __init__(self, lgf, lgb, rows):
        r = lax.broadcasted_iota(jnp.int32, (rows, RET_DIM), 0).astype(F32)
        self.head = r + 1.0
        self.tail = (rows - 1.0) - r
        self.q_f = jnp.exp(lgf * self.head)
        self.k_f = jnp.exp(lgf * self.tail)
        self.q_b = jnp.exp(lgb * self.tail)
        self.k_b = jnp.exp(lgb * self.head)


def _ret_states(kf32, vs, lgf, lgb, C, c, nt, hf, hb, hfa=None, hba=None):
    dec = _RetDecay(lgf, lgb, c)
    dec_c = _RetDecay(lgf, lgb, C)
    step_f = jnp.exp(jnp.zeros((RET_DIM, RET_DIM), F32) + lgf * c)
    step_b = jnp.exp(jnp.zeros((RET_DIM, RET_DIM), F32) + lgb * c)

    def upd(rows, kdec):
        return _dot_tn((kf32[rows, :] * kdec).astype(BF16), vs[rows, :])

    def lat(t):
        return slice(C + t * c, C + (t + 1) * c)

    state = upd(slice(0, C), dec_c.k_f)
    aged = jnp.zeros_like(state)
    for t in range(nt):
        hf[t] = state.astype(BF16)
        if hfa is not None:
            hfa[t] = aged
        if t < nt - 1:
            aged = step_f * (aged + c * state)
            state = step_f * state + upd(lat(t), dec.k_f)
    state = upd(slice(0, C), dec_c.k_b)
    aged = jnp.zeros_like(state)
    for t in range(nt - 1, -1, -1):
        hb[t] = state.astype(BF16)
        if hba is not None:
            hba[t] = aged
        if t > 0:
            aged = step_b * (aged + c * state)
            state = step_b * state + upd(lat(t), dec.k_b)
    return dec, dec_c, step_f, step_b


def _ret_fwd(proj, cos, sin, lg, gn, n_ctx, hosted):
    B, T, _ = proj.shape
    C = n_ctx
    N = T - C
    c = _div_tile(N, 256, 16)
    nt = N // c
    scale = RET_DIM ** -0.5

    def body(lg_ref, q_ref, k_ref, v_ref, g_ref, cos_ref, sin_ref, gn_ref, o_ref, lat_ref, qs, ks, vs, kf32, hf, hb):
        h = pl.program_id(1)
        lgf = lg_ref[0, h]
        lgb = lg_ref[1, h]
        for rows in [slice(0, C)] + [slice(C + t * c, C + (t + 1) * c) for t in range(nt)]:
            cosb = cos_ref[rows, :]
            sinb = sin_ref[rows, :]
            qs[rows, :] = (_rope(q_ref[rows, :], cosb, sinb) * scale).astype(BF16)
            kr = _rope(k_ref[rows, :], cosb, sinb)
            kf32[rows, :] = kr
            ks[rows, :] = kr.astype(BF16)
            vs[rows, :] = v_ref[rows, :].astype(BF16)
        gnv = gn_ref[...]
        dec, _, _, _ = _ret_states(kf32, vs, lgf, lgb, C, c, nt, hf, hb)
        rc = (lax.broadcasted_iota(jnp.int32, (c, c), 0) - lax.broadcasted_iota(jnp.int32, (c, c), 1)).astype(F32)
        w_diag = _ret_weights(lgf, lgb, rc)
        for t in range(nt):
            rows = slice(C + t * c, C + (t + 1) * c)
            qt = qs[rows, :]
            s = _dot_nt(qt, ks[rows, :])
            o = (_dot((s * w_diag).astype(BF16), vs[rows, :])
                 + dec.q_f * _dot(qt, hf[t]) + dec.q_b * _dot(qt, hb[t]))
            o_ref[t * c:(t + 1) * c, :] = o
            mu = jnp.mean(o, axis=-1, keepdims=True)
            oc = o - mu
            var = jnp.mean(oc * oc, axis=-1, keepdims=True)
            yh = oc * lax.rsqrt(var + NORM_EPS)
            g = g_ref[rows, :]
            lat_ref[t * c:(t + 1) * c, :] = ((yh * gnv) * (g * _sigmoid(g))).astype(BF16)

    def col(seg):
        return pl.BlockSpec((None, T, RET_DIM), lambda b, h, seg=seg: (b, 0, seg * RET_HEADS + h))

    return _call_hosting(
        body, hosted, name="ret_fwd", grid=(B, RET_HEADS),
        out_shape=(jax.ShapeDtypeStruct((B, N, RET_WIDTH), F32), jax.ShapeDtypeStruct((B, N, RET_WIDTH), BF16)),
        in_specs=[pl.BlockSpec(memory_space=pltpu.SMEM), col(0), col(1), col(2), col(3),
                  pl.BlockSpec((T, RET_DIM), lambda b, h: (0, 0)), pl.BlockSpec((T, RET_DIM), lambda b, h: (0, 0)),
                  pl.BlockSpec((1, RET_DIM), lambda b, h: (0, h))],
        out_specs=(pl.BlockSpec((None, N, RET_DIM), lambda b, h: (b, 0, h)),
                   pl.BlockSpec((None, N, RET_DIM), lambda b, h: (b, 0, h))),
        scratch_shapes=[pltpu.VMEM((T, RET_DIM), BF16)] * 3 + [pltpu.VMEM((T, RET_DIM), F32)]
                       + [pltpu.VMEM((nt, RET_DIM, RET_DIM), BF16)] * 2,
        args=(lg, proj, proj, proj, proj, cos, sin, gn))


def _ret_bwd(proj, cos, sin, lg, gn, o, dlat, n_ctx, hosted):
    B, T, _ = proj.shape
    C = n_ctx
    N = T - C
    c = _div_tile(N, 256, 16)
    nt = N // c
    scale = RET_DIM ** -0.5

    def lat(t):
        return slice(C + t * c, C + (t + 1) * c)

    def body(lg_ref, q_ref, k_ref, v_ref, g_ref, cos_ref, sin_ref, gn_ref, o_ref, dl_ref,
             d_ref, dgn_ref, dlg_ref, qs, ks, vs, dos, qf32, kf32, hf, hb, hfa, hba, gf_s, gb_s):
        h = pl.program_id(1)
        lgf = lg_ref[0, h]
        lgb = lg_ref[1, h]
        gnv = gn_ref[...]

        def fold(a):
            return jnp.sum(a.reshape(a.shape[0] // SUBLANES, SUBLANES, a.shape[1]), axis=0)

        for rows in [slice(0, C)] + [lat(t) for t in range(nt)]:
            cosb = cos_ref[rows, :]
            sinb = sin_ref[rows, :]
            qr = _rope(q_ref[rows, :], cosb, sinb) * scale
            qf32[rows, :] = qr
            qs[rows, :] = qr.astype(BF16)
            kr = _rope(k_ref[rows, :], cosb, sinb)
            kf32[rows, :] = kr
            ks[rows, :] = kr.astype(BF16)
            vs[rows, :] = v_ref[rows, :].astype(BF16)

        dgn = jnp.zeros((1, RET_DIM), F32)
        for t in range(nt):
            lrows = slice(t * c, (t + 1) * c)
            ov = o_ref[lrows, :]
            mu = jnp.mean(ov, axis=-1, keepdims=True)
            oc = ov - mu
            var = jnp.mean(oc * oc, axis=-1, keepdims=True)
            rstd = lax.rsqrt(var + NORM_EPS)
            yh = oc * rstd
            g = g_ref[lat(t), :]
            sg = _sigmoid(g)
            dl = dl_ref[lrows, :]
            d_ref[3, lat(t), :] = (dl * (yh * gnv) * (sg * (1.0 + g * (1.0 - sg)))).astype(BF16)
            dls = dl * (g * sg)
            dgn = dgn + jnp.sum(dls * yh, axis=0, keepdims=True)
            dyh = dls * gnv
            do = rstd * (dyh - jnp.mean(dyh, axis=-1, keepdims=True)
                         - yh * jnp.mean(dyh * yh, axis=-1, keepdims=True))
            dos[lrows, :] = do.astype(BF16)
        dgn_ref[...] = jnp.concatenate([dgn, jnp.zeros((SUBLANES - 1, RET_DIM), F32)], axis=0)
        d_ref[3, 0:C, :] = jnp.zeros((C, RET_DIM), BF16)
        d_ref[0, 0:C, :] = jnp.zeros((C, RET_DIM), BF16)

        dec, dec_c, step_f, step_b = _ret_states(kf32, vs, lgf, lgb, C, c, nt, hf, hb, hfa, hba)

        def zmat(t, qdec):
            return _dot_tn((qf32[lat(t), :] * qdec).astype(BF16), dos[t * c:(t + 1) * c, :])

        acc3f = jnp.zeros((RET_DIM, RET_DIM), F32)
        acc3b = jnp.zeros((RET_DIM, RET_DIM), F32)
        state = jnp.zeros((RET_DIM, RET_DIM), F32)
        for t in range(nt - 1, -1, -1):
            gf_s[t] = state.astype(BF16)
            z = zmat(t, dec.q_f)
            acc3f = acc3f + hfa[t] * z
            state = step_f * state + z
        gctx_f = state.astype(BF16)
        state = jnp.zeros((RET_DIM, RET_DIM), F32)
        for t in range(nt):
            gb_s[t] = state.astype(BF16)
            z = zmat(t, dec.q_b)
            acc3b = acc3b + hba[t] * z
            state = step_b * state + z
        gctx_b = state.astype(BF16)

        rc = (lax.broadcasted_iota(jnp.int32, (c, c), 0) - lax.broadcasted_iota(jnp.int32, (c, c), 1)).astype(F32)
        w_diag = _ret_weights(lgf, lgb, rc)
        wg_f = jnp.where(rc >= 0.0, w_diag * rc, 0.0)
        wg_b = jnp.where(rc < 0.0, -w_diag * rc, 0.0)
        accf = jnp.zeros((SUBLANES, RET_DIM), F32)
        accb = jnp.zeros((SUBLANES, RET_DIM), F32)
        gdf = jnp.zeros((SUBLANES, c), F32)
        gdb = jnp.zeros((SUBLANES, c), F32)
        for t in range(nt):
            rows = lat(t)
            qt = qs[rows, :]
            kt = ks[rows, :]
            vt = vs[rows, :]
            dot = dos[t * c:(t + 1) * c, :]
            s = _dot_nt(qt, kt)
            dp = _dot_nt(dot, vt)
            dv = _dot_tn((s * w_diag).astype(BF16), dot)
            ds = (dp * w_diag).astype(BF16)
            dq = _dot(ds, kt)
            dk = _dot_tn(ds, qt)
            gs = dp * s
            gdf = gdf + fold(gs * wg_f)
            gdb = gdb + fold(gs * wg_b)
            qv = qf32[rows, :]
            kv = kf32[rows, :]
            dq_f = dec.q_f * _dot_nt(dot, hf[t])
            dq_b = dec.q_b * _dot_nt(dot, hb[t])
            dk_f = dec.k_f * _dot_nt(vt, gf_s[t])
            dk_b = dec.k_b * _dot_nt(vt, gb_s[t])
            accf = accf + fold(dec.head * dq_f * qv) + fold(dec.tail * dk_f * kv)
            accb = accb + fold(dec.tail * dq_b * qv) + fold(dec.head * dk_b * kv)
            dv = dv + dec.k_f * _dot(kt, gf_s[t]) + dec.k_b * _dot(kt, gb_s[t])
            cosb = cos_ref[rows, :]
            sinb = sin_ref[rows, :]
            d_ref[0, rows, :] = _unrope((dq + dq_f + dq_b) * scale, cosb, sinb).astype(BF16)
            d_ref[1, rows, :] = _unrope(dk + dk_f + dk_b, cosb, sinb).astype(BF16)
            d_ref[2, rows, :] = dv.astype(BF16)
        kc = ks[0:C, :]
        vc = vs[0:C, :]
        kcv = kf32[0:C, :]
        dkc_f = dec_c.k_f * _dot_nt(vc, gctx_f)
        dkc_b = dec_c.k_b * _dot_nt(vc, gctx_b)
        accf = accf + fold(dec_c.tail * dkc_f * kcv)
        accb = accb + fold(dec_c.head * dkc_b * kcv)
        d_ref[1, 0:C, :] = (dkc_f + dkc_b).astype(BF16)
        d_ref[2, 0:C, :] = (dec_c.k_f * _dot(kc, gctx_f) + dec_c.k_b * _dot(kc, gctx_b)).astype(BF16)
        gf = jnp.sum(gdf) + jnp.sum(accf) + jnp.sum(acc3f)
        gb = jnp.sum(gdb) + jnp.sum(accb) + jnp.sum(acc3b)
        row = lax.broadcasted_iota(jnp.int32, (SUBLANES, LANES), 0)
        dlg_ref[...] = jnp.where(row == 0, gf, jnp.where(row == 1, gb, 0.0))

    def col(seg):
        return pl.BlockSpec((None, T, RET_DIM), lambda b, h, seg=seg: (b, 0, seg * RET_HEADS + h))

    return _call_hosting(
        body, hosted, name="ret_bwd", grid=(B, RET_HEADS),
        out_shape=(jax.ShapeDtypeStruct((B, 4, T, RET_WIDTH), BF16),
                   jax.ShapeDtypeStruct((B, SUBLANES, RET_WIDTH), F32),
                   jax.ShapeDtypeStruct((B, RET_HEADS, SUBLANES, LANES), F32)),
        in_specs=[pl.BlockSpec(memory_space=pltpu.SMEM), col(0), col(1), col(2), col(3),
                  pl.BlockSpec((T, RET_DIM), lambda b, h: (0, 0)), pl.BlockSpec((T, RET_DIM), lambda b, h: (0, 0)),
                  pl.BlockSpec((1, RET_DIM), lambda b, h: (0, h)),
                  pl.BlockSpec((None, N, RET_DIM), lambda b, h: (b, 0, h)),
                  pl.BlockSpec((None, N, RET_DIM), lambda b, h: (b, 0, h))],
        out_specs=(pl.BlockSpec((None, 4, T, RET_DIM), lambda b, h: (b, 0, 0, h)),
                   pl.BlockSpec((None, SUBLANES, RET_DIM), lambda b, h: (b, 0, h)),
                   pl.BlockSpec((None, None, SUBLANES, LANES), lambda b, h: (b, h, 0, 0))),
        scratch_shapes=[pltpu.VMEM((T, RET_DIM), BF16)] * 3 + [pltpu.VMEM((N, RET_DIM), BF16)]
                       + [pltpu.VMEM((T, RET_DIM), F32)] * 2
                       + [pltpu.VMEM((nt, RET_DIM, RET_DIM), BF16)] * 2 + [pltpu.VMEM((nt, RET_DIM, RET_DIM), F32)] * 2
                       + [pltpu.VMEM((nt, RET_DIM, RET_DIM), BF16)] * 2,
        args=(lg, proj, proj, proj, proj, cos, sin, gn, o, dlat))


def _na_geometry(rows):
    kh = min(NA_KH, rows)
    return kh, kh * GRID_W


def _pair_select():
    lane = lax.broadcasted_iota(jnp.int32, (2 * GRID_W, LANES), 1)
    row = lax.broadcasted_iota(jnp.int32, (2 * GRID_W, LANES), 0)
    return (lane >= NA_DIM) == (row >= GRID_W)


def _pair_bias(bias_ref, dr0, kh):
    return jnp.concatenate(
        [jnp.concatenate([bias_ref[e, pl.ds(dr0 + 2 * m, 1)].reshape(GRID_W, LANES) for m in range(kh // 2)], axis=1)
         for e in range(2)], axis=0)


def _na_softmax(s_loc, s_ctx):
    mx = jnp.maximum(jnp.max(s_loc, axis=-1, keepdims=True), jnp.max(s_ctx, axis=-1, keepdims=True))
    p_loc = jnp.exp(s_loc - mx)
    p_ctx = jnp.exp(s_ctx - mx)
    den = jnp.sum(p_loc, axis=-1, keepdims=True) + jnp.sum(p_ctx, axis=-1, keepdims=True)
    return p_loc, p_ctx, den


def _na_fwd(proj, bias2, n_ctx, hosted):
    B, T, _ = proj.shape
    C = n_ctx
    N = T - C
    R = N // GRID_W
    kh, nk = _na_geometry(R)
    scale = NA_DIM ** -0.5
    base = (4 * RET_WIDTH) // LANES

    def body(q_ref, k_ref, v_ref, bias_ref, out_ref, kb16, vb16):
        kb16[...] = k_ref[...].astype(BF16)
        vb16[...] = v_ref[...].astype(BF16)
        kc = kb16[0:C, :]
        vc = vb16[0:C, :]
        lane = lax.broadcasted_iota(jnp.int32, (GRID_W, LANES), 1)
        sel2 = _pair_select()

        def group(gi, carry):
            pre = []
            for u in range(NA_GROUP):
                r = gi * NA_GROUP + u
                bs = jnp.clip(r - kh // 2, 0, R - kh)
                dr0 = bs - r + (NA_KH - 1)
                q = q_ref[pl.ds(pl.multiple_of(C + r * GRID_W, GRID_W), GRID_W), :] * scale
                q2 = jnp.where(sel2, jnp.concatenate([q, q], axis=0), 0.0).astype(BF16)
                band = pl.ds(pl.multiple_of(C + bs * GRID_W, GRID_W), nk)
                s_loc = _dot_nt(q2, kb16[band, :]) + _pair_bias(bias_ref, dr0, kh)
                s_ctx = _dot_nt(q2, kc)
                pre.append((r, band, s_loc, s_ctx))
            mid = [(r, band) + _na_softmax(s_loc, s_ctx) for r, band, s_loc, s_ctx in pre]
            for r, band, p_loc, p_ctx, den in mid:
                o2 = (_dot(p_loc.astype(BF16), vb16[band, :]) + _dot(p_ctx.astype(BF16), vc)) / den
                out_ref[pl.ds(pl.multiple_of(r * GRID_W, GRID_W), GRID_W), :] = jnp.where(
                    lane < NA_DIM, o2[:GRID_W], o2[GRID_W:]).astype(BF16)
            return carry

        lax.fori_loop(0, R // NA_GROUP, group, 0)

    def col(seg):
        return pl.BlockSpec((None, T, LANES), lambda b, p, seg=seg: (b, 0, base + seg * NA_PAIRS + p))

    return _call_hosting(
        body, hosted, name="na_fwd", grid=(B, NA_PAIRS),
        out_shape=(jax.ShapeDtypeStruct((B, N, NA_WIDTH), BF16),),
        in_specs=[col(0), col(1), col(2),
                  pl.BlockSpec((2, 2 * NA_KH - 2, GRID_W, LANES), lambda b, p: (p, 0, 0, 0))],
        out_specs=(pl.BlockSpec((None, N, LANES), lambda b, p: (b, 0, p)),),
        scratch_shapes=[pltpu.VMEM((T, LANES), BF16)] * 2,
        args=(proj, proj, proj, bias2))


def _na_bwd(proj, bias2, dlat, n_ctx, hosted):
    B, T, _ = proj.shape
    C = n_ctx
    N = T - C
    R = N // GRID_W
    kh, nk = _na_geometry(R)
    scale = NA_DIM ** -0.5
    base = (4 * RET_WIDTH) // LANES

    def body(q_ref, k_ref, v_ref, bias_ref, dl_ref, d_ref, db_ref, kb16, vb16, dkv):
        b = pl.program_id(1)
        kb16[...] = k_ref[...].astype(BF16)
        vb16[...] = v_ref[...].astype(BF16)
        kc = kb16[0:C, :]
        vc = vb16[0:C, :]
        lane = lax.broadcasted_iota(jnp.int32, (GRID_W, LANES), 1)
        dkv[...] = jnp.zeros(dkv.shape, F32)
        d_ref[0, 0:C, :] = jnp.zeros((C, LANES), BF16)

        @pl.when(b == 0)
        def _():
            db_ref[...] = jnp.zeros(db_ref.shape, F32)

        sel2 = _pair_select()

        def group(gi, carry):
            pre = []
            for u in range(NA_GROUP):
                r = gi * NA_GROUP + u
                bs = jnp.clip(r - kh // 2, 0, R - kh)
                dr0 = bs - r + (NA_KH - 1)
                q = q_ref[pl.ds(pl.multiple_of(C + r * GRID_W, GRID_W), GRID_W), :] * scale
                do = dl_ref[pl.ds(pl.multiple_of(r * GRID_W, GRID_W), GRID_W), :]
                q2 = jnp.where(sel2, jnp.concatenate([q, q], axis=0), 0.0).astype(BF16)
                do2 = jnp.where(sel2, jnp.concatenate([do, do], axis=0), 0.0).astype(BF16)
                band = pl.ds(pl.multiple_of(C + bs * GRID_W, GRID_W), nk)
                s_loc = _dot_nt(q2, kb16[band, :]) + _pair_bias(bias_ref, dr0, kh)
                s_ctx = _dot_nt(q2, kc)
                dp_loc = _dot_nt(do2, vb16[band, :])
                dp_ctx = _dot_nt(do2, vc)
                pre.append((r, dr0, band, q2, do2, s_loc, s_ctx, dp_loc, dp_ctx))
            mid = []
            for r, dr0, band, q2, do2, s_loc, s_ctx, dp_loc, dp_ctx in pre:
                p_loc, p_ctx, den = _na_softmax(s_loc, s_ctx)
                inv = 1.0 / den
                p_loc = p_loc * inv
                p_ctx = p_ctx * inv
                delta = (jnp.sum(p_loc * dp_loc, axis=-1, keepdims=True)
                         + jnp.sum(p_ctx * dp_ctx, axis=-1, keepdims=True))
                ds_loc = p_loc * (dp_loc - delta)
                ds_ctx = p_ctx * (dp_ctx - delta)
                mid.append((r, dr0, band, q2, do2, p_loc.astype(BF16), p_ctx.astype(BF16), ds_loc, ds_ctx))
            for r, dr0, band, q2, do2, pb_loc, pb_ctx, ds_loc, ds_ctx in mid:
                dsb_loc = ds_loc.astype(BF16)
                dsb_ctx = ds_ctx.astype(BF16)
                dq2 = _dot(dsb_loc, kb16[band, :]) + _dot(dsb_ctx, kc)
                d_ref[0, pl.ds(pl.multiple_of(C + r * GRID_W, GRID_W), GRID_W), :] = (jnp.where(
                    lane < NA_DIM, dq2[:GRID_W], dq2[GRID_W:]) * scale).astype(BF16)
                dkv[0, band, :] += _dot_tn(dsb_loc, q2)
                dkv[1, band, :] += _dot_tn(pb_loc, do2)
                dkv[0, 0:C, :] += _dot_tn(dsb_ctx, q2)
                dkv[1, 0:C, :] += _dot_tn(pb_ctx, do2)
                for e in range(2):
                    for m in range(kh // 2):
                        db_ref[e, pl.ds(dr0 + 2 * m, 1)] += ds_loc[e * GRID_W:(e + 1) * GRID_W,
                                                                   m * LANES:(m + 1) * LANES].reshape(1, GRID_W, LANES)
            return carry

        lax.fori_loop(0, R // NA_GROUP, group, 0)
        d_ref[1] = dkv[0].astype(BF16)
        d_ref[2] = dkv[1].astype(BF16)

    def col(seg):
        return pl.BlockSpec((None, T, LANES), lambda p, b, seg=seg: (b, 0, base + seg * NA_PAIRS + p))

    return _call_hosting(
        body, hosted, name="na_bwd", grid=(NA_PAIRS, B),
        out_shape=(jax.ShapeDtypeStruct((B, 3, T, NA_WIDTH), BF16),
                   jax.ShapeDtypeStruct((NA_HEADS, 2 * NA_KH - 2, GRID_W, LANES), F32)),
        in_specs=[col(0), col(1), col(2),
                  pl.BlockSpec((2, 2 * NA_KH - 2, GRID_W, LANES), lambda p, b: (p, 0, 0, 0)),
                  pl.BlockSpec((None, N, LANES), lambda p, b: (b, 0, p))],
        out_specs=(pl.BlockSpec((None, 3, T, LANES), lambda p, b: (b, 0, 0, p)),
                   pl.BlockSpec((2, 2 * NA_KH - 2, GRID_W, LANES), lambda p, b: (p, 0, 0, 0))),
        scratch_shapes=[pltpu.VMEM((T, LANES), BF16)] * 2 + [pltpu.VMEM((2, T, LANES), F32)],
        args=(proj, proj, proj, bias2, dlat))


def _split3(a):
    hi = a.astype(BF16)
    r1 = a - hi.astype(F32)
    mid = r1.astype(BF16)
    lo = (r1 - mid.astype(F32)).astype(BF16)
    return hi, mid, lo


def _rpb_reduce(dbias2, onehot2):
    rows = dbias2.shape[0] * dbias2.shape[1]
    flat = dbias2.reshape(rows, GRID_W * LANES)

    def body(a_ref, oh_ref, o_ref):
        hi, mid, lo = _split3(a_ref[...])
        oh = oh_ref[...]
        o_ref[...] = _dot(hi, oh) + _dot(mid, oh) + _dot(lo, oh)

    return pl.pallas_call(
        body, name="rpb_reduce", out_shape=jax.ShapeDtypeStruct((rows, LANES), F32),
        in_specs=[_vmem(), _vmem()], out_specs=_vmem(),
        compiler_params=pltpu.CompilerParams(vmem_limit_bytes=VMEM_LIMIT),
    )(flat, onehot2)


def _dense_core(lat_ret, lat_na, x, tgt, modl, g_post_mix, g_pre_mlp, g_post_mlp, w_out, w1, w2):
    B, N, D = x.shape
    F = w1.shape[1]
    w2_rows = w2.shape[0] // N_DEV
    mixw = w_out.shape[0]
    half = mixw // 2
    tm = _div_tile(N, 256, 16)
    nt = N // tm
    fc = _div_tile(F, 1024, LANES)

    def body(lr_ref, ln_ref, x_ref, t_ref, gt1_ref, sh2_ref, sc2_ref, gt2_ref, gpm_ref, gpre_ref, gpo_ref,
             wout_hbm, w1_hbm, w2_part,
             dy1_ref, dlr_ref, dln_ref, dmix_ref, h2_ref, a_ref, du_ref, dz_ref, red_ref, w2_hbm,
             wout_v, w1_v, w2_v, u_s, sems, fsend, frecv):
        @pl.when((pl.program_id(0) == 0) & (pl.program_id(1) == 0))
        def _():
            relay = [(_row_block(w2_part, w2_rows), _row_block(w2_hbm, w2_rows))]
            _forward_start(relay, fsend, frecv)
            cps = [pltpu.make_async_copy(wout_hbm, wout_v, sems.at[0]),
                   pltpu.make_async_copy(w1_hbm, w1_v, sems.at[1])]
            for cp in cps:
                cp.start()
            _forward_wait(relay, fsend, frecv)
            cps.append(pltpu.make_async_copy(w2_hbm, w2_v, sems.at[2]))
            cps[2].start()
            for cp in cps:
                cp.wait()

        gt1 = gt1_ref[...]
        sh2 = sh2_ref[...]
        sc2 = sc2_ref[...]
        gt2 = gt2_ref[...]
        gpm = gpm_ref[...]
        gpre = gpre_ref[...]
        gpo = gpo_ref[...]

        def rowmean(a):
            return jnp.mean(a, axis=-1, keepdims=True)

        def colsum(a):
            return jnp.sum(a, axis=0, keepdims=True)

        mix = _dot(lr_ref[...], wout_v[0:half, :]) + _dot(ln_ref[...], wout_v[half:, :])
        x = x_ref[...]
        rm = lax.rsqrt(rowmean(mix * mix) + NORM_EPS)
        mh = mix * rm
        nm = mh * gpm
        y1 = x + gt1 * nm
        r1 = lax.rsqrt(rowmean(y1 * y1) + NORM_EPS)
        xh = y1 * r1
        n1 = xh * gpre
        h2b = (n1 * (1.0 + sc2) + sh2).astype(BF16)
        h2_ref[...] = h2b
        z = jnp.zeros((tm, D), F32)
        for c0 in range(0, F, fc):
            u = _dot(h2b, w1_v[:, c0:c0 + fc])
            u_s[:, c0:c0 + fc] = u
            ru = jnp.maximum(u, 0.0)
            ab = (ru * ru).astype(BF16)
            a_ref[:, c0:c0 + fc] = ab
            z = z + _dot(ab, w2_v[c0:c0 + fc, :])
        r2 = lax.rsqrt(rowmean(z * z) + NORM_EPS)
        zh = z * r2
        n2 = zh * gpo
        y2 = y1 + gt2 * n2
        err = y2 - t_ref[...]
        loss = 0.5 * jnp.sum(rowmean(err * err))
        dy2 = err * (1.0 / D)
        red_ref[2:3, :] = colsum(dy2 * n2)
        dn2 = dy2 * gt2
        red_ref[6:7, :] = colsum(dn2 * zh)
        dzh = dn2 * gpo
        dz = r2 * (dzh - zh * rowmean(dzh * zh))
        dzb = dz.astype(BF16)
        dz_ref[...] = dzb
        dh2 = jnp.zeros((tm, D), F32)
        for c0 in range(0, F, fc):
            da = _dot_nt(dzb, w2_v[c0:c0 + fc, :])
            dub = (da * (2.0 * jnp.maximum(u_s[:, c0:c0 + fc], 0.0))).astype(BF16)
            du_ref[:, c0:c0 + fc] = dub
            dh2 = dh2 + _dot_nt(dub, w1_v[:, c0:c0 + fc])
        red_ref[3:4, :] = colsum(dh2 * n1)
        red_ref[4:5, :] = colsum(dh2)
        dn1 = dh2 * (1.0 + sc2)
        red_ref[5:6, :] = colsum(dn1 * xh)
        dxh = dn1 * gpre
        dy1 = dy2 + r1 * (dxh - xh * rowmean(dxh * xh))
        dy1_ref[...] = dy1
        red_ref[0:1, :] = colsum(dy1 * nm)
        dnm = dy1 * gt1
        red_ref[1:2, :] = colsum(dnm * mh)
        dmh = dnm * gpm
        dmix = (rm * (dmh - mh * rowmean(dmh * mh))).astype(BF16)
        dmix_ref[...] = dmix
        dlr_ref[...] = _dot_nt(dmix, wout_v[0:half, :])
        dln_ref[...] = _dot_nt(dmix, wout_v[half:, :])
        red_ref[7:8, :] = jnp.zeros((1, D), F32) + loss

    def tok(w):
        return pl.BlockSpec((None, tm, w), lambda b, t: (b, t, 0))

    def mod(k):
        return pl.BlockSpec((None, None, 1, D), lambda b, t, k=k: (b, k, 0, 0))

    def vec():
        return pl.BlockSpec((1, D), lambda b, t: (0, 0))

    return pl.pallas_call(
        body, name="dense_core", grid=(B, nt),
        out_shape=(jax.ShapeDtypeStruct((B, N, D), F32), jax.ShapeDtypeStruct((B, N, half), F32),
                   jax.ShapeDtypeStruct((B, N, half), F32), jax.ShapeDtypeStruct((B, N, D), BF16),
                   jax.ShapeDtypeStruct((B, N, D), BF16), jax.ShapeDtypeStruct((B, N, F), BF16),
                   jax.ShapeDtypeStruct((B, N, F), BF16), jax.ShapeDtypeStruct((B, N, D), BF16),
                   jax.ShapeDtypeStruct((B, nt, SUBLANES, D), F32),
                   jax.ShapeDtypeStruct(w2.shape, w2.dtype)),
        in_specs=[tok(half), tok(half), tok(D), tok(D), mod(2), mod(3), mod(4), mod(5), vec(), vec(), vec(),
                  _any(), _any(), _any()],
        out_specs=(tok(D), tok(half), tok(half), tok(D), tok(D), tok(F), tok(F), tok(D),
                   pl.BlockSpec((None, None, SUBLANES, D), lambda b, t: (b, t, 0, 0)), _any()),
        scratch_shapes=[pltpu.VMEM((mixw, D), BF16), pltpu.VMEM((D, F), BF16), pltpu.VMEM((F, D), BF16),
                        pltpu.VMEM((tm, F), F32), pltpu.SemaphoreType.DMA((3,)),
                        pltpu.SemaphoreType.DMA((1, 3)), pltpu.SemaphoreType.DMA((1, 3))],
        input_output_aliases={13: 9},
        compiler_params=_params("arbitrary", "arbitrary"),
    )(lat_ret, lat_na, x, tgt, modl, modl, modl, modl, g_post_mix, g_pre_mlp, g_post_mlp, w_out, w1, w2)[:9]


def _inproj_bwd(dret, dna, x, ctx, dy1, modl, g1, w_in_t, hosted):
    B, N, D = x.shape
    n_ctx = ctx.shape[1]
    T = n_ctx + N
    tm = _div_tile(n_ctx, 256, 16)
    nct, ctx_spec, lat_spec = _token_tiles(n_ctx, tm)
    nt = T // tm
    nseg_r = dret.shape[1]
    nseg_n = dna.shape[1]
    nw = w_in_t.shape[0]

    def body(*refs):
        seg_refs = refs[:nseg_r + nseg_n]
        c_ref, x_ref, dy1_ref, sc_ref, g_ref, w_ref, dx_ref, red_ref = refs[nseg_r + nseg_n:]
        t = pl.program_id(1)
        dh = jnp.zeros((tm, D), F32)
        for s, ref in enumerate(seg_refs):
            dh = dh + _dot(ref[...], w_ref[s * SEG:(s + 1) * SEG, :])
        x = jnp.where(t < nct, c_ref[...], x_ref[...])
        g = g_ref[...]
        r = lax.rsqrt(jnp.mean(x * x, axis=-1, keepdims=True) + NORM_EPS)
        xh = x * r
        red_ref[0:1, :] = jnp.sum(dh, axis=0, keepdims=True)
        red_ref[1:2, :] = jnp.sum(dh * (xh * g), axis=0, keepdims=True)
        dn = dh * (1.0 + sc_ref[...])
        red_ref[2:3, :] = jnp.sum(dn * xh, axis=0, keepdims=True)
        red_ref[3:, :] = jnp.zeros((SUBLANES - 3, D), F32)
        dxh = dn * g
        dx = r * (dxh - xh * jnp.mean(dxh * xh, axis=-1, keepdims=True))
        dx_ref[...] = dx + jnp.where(t >= nct, dy1_ref[...], 0.0)

    def mrow(b, t):
        return jnp.where(t < nct, B, b)

    def seg(s):
        return pl.BlockSpec((None, None, tm, SEG), lambda b, t, s=s: (b, s, t, 0))

    return _call_hosting(
        body, hosted, name="inproj_bwd", grid=(B, nt),
        out_shape=(jax.ShapeDtypeStruct((B, N, D), F32), jax.ShapeDtypeStruct((B, nt, SUBLANES, D), F32)),
        in_specs=[seg(s) for s in range(nseg_r)] + [seg(s) for s in range(nseg_n)]
                 + [ctx_spec(D), lat_spec(D), lat_spec(D),
                    pl.BlockSpec((None, None, 1, D), lambda b, t: (mrow(b, t), 1, 0, 0)),
                    pl.BlockSpec((1, D), lambda b, t: (0, 0)),
                    pl.BlockSpec((nw, D), lambda b, t: (0, 0))],
        out_specs=(lat_spec(D), pl.BlockSpec((None, None, SUBLANES, D), lambda b, t: (b, t, 0, 0))),
        scratch_shapes=[], args=(*([dret] * nseg_r), *([dna] * nseg_n), ctx, x, dy1, modl, g1, w_in_t))


def _tn_matmul(lhs, rhs, name, rows_before=0, rows_after=0, into=None):
    B, S, T, W = lhs.shape
    nn = rhs.shape[-1]
    tk = _div_tile(T, 2304, LANES)
    bm = _div_tile(W, 1024, LANES)
    bn = _div_tile(nn, 1024, LANES)
    nkt = T // tk
    nk = B * nkt

    def body(l_ref, r_ref, *rest):
        o_ref, acc = rest[-2:]
        k = pl.program_id(3)

        @pl.when(k == 0)
        def _():
            acc[...] = jnp.zeros(acc.shape, F32)

        acc[...] += _dot_tn(l_ref[...].astype(BF16), r_ref[...].astype(BF16))

        @pl.when(k == nk - 1)
        def _():
            o_ref[...] = acc[...].astype(BF16)

    nwb = W // bm
    first = rows_before // bm
    return pl.pallas_call(
        functools.partial(body), name=name, grid=(S, nwb, nn // bn, nk),
        out_shape=jax.ShapeDtypeStruct((rows_before + S * W + rows_after, nn), BF16),
        in_specs=[pl.BlockSpec((None, None, tk, bm), lambda s, i, j, k: (k // nkt, s, k % nkt, i)),
                  pl.BlockSpec((None, tk, bn), lambda s, i, j, k: (k // nkt, k % nkt, j))]
                 + ([] if into is None else [_any()]),
        out_specs=pl.BlockSpec((bm, bn), lambda s, i, j, k: (first + s * nwb + i, j)),
        scratch_shapes=[pltpu.VMEM((bm, bn), F32)],
        input_output_aliases={} if into is None else {2: 0},
        compiler_params=_params("parallel", "parallel", "parallel", "arbitrary"),
    )(lhs, rhs, *([] if into is None else [into]))


class _SplitScatter:
    def __init__(self, gs, block_ofs, block_shapes, name):
        self.n = n = len(gs)
        self.block_ofs = block_ofs
        land_shapes = [(N_DEV,) + tuple(bs) for bs in block_shapes]
        hbm = pl.BlockSpec(memory_space=pltpu.HBM)
        sem = pl.BlockSpec(memory_space=pltpu.SEMAPHORE)

        def body(*refs):
            g_refs, land_refs = refs[:n], refs[n:2 * n]
            send_sems, recv_sems, own_sems = refs[2 * n:2 * n + 3]
            token = refs[-1]
            for own, pushes in self._copies(g_refs, land_refs, send_sems, recv_sems, own_sems, landing="sender"):
                own.start()
                for cp in pushes:
                    cp.start()
            token[...] = jnp.zeros_like(token)

        outs = pl.pallas_call(
            body, name=name,
            out_shape=(pltpu.SemaphoreType.DMA((n * (N_DEV - 1),)), pltpu.SemaphoreType.DMA((n * (N_DEV - 1),)),
                       pltpu.SemaphoreType.DMA((n,)))
                      + tuple(pltpu.HBM(g.shape, g.dtype) for g in gs)
                      + tuple(pltpu.HBM(s, g.dtype) for s, g in zip(land_shapes, gs))
                      + (jax.ShapeDtypeStruct((SUBLANES, LANES), F32),),
            in_specs=(hbm,) * (2 * n), out_specs=(sem,) * 3 + (hbm,) * (2 * n) + (_vmem(),),
            input_output_aliases={k: 3 + k for k in range(2 * n)},
            compiler_params=pltpu.CompilerParams(has_side_effects=pltpu.SideEffectType.DATAFLOW_SIDE_EFFECTING),
        )(*[pltpu.with_memory_space_constraint(g, pltpu.HBM) for g in gs],
          *[pltpu.with_memory_space_constraint(lax.empty(s, g.dtype), pltpu.HBM) for s, g in zip(land_shapes, gs)])
        self.sems, self.thru, self.token = outs[:3], outs[3:3 + 2 * n], outs[-1]

    def _copies(self, g_refs, land_refs, send_sems, recv_sems, own_sems, landing):
        me, peers = _me_and_peers()
        out = []
        for k in range(self.n):
            src = self.block_ofs[k](g_refs[k])
            own = pltpu.make_async_copy(src(me), land_refs[k].at[me], own_sems.at[k])
            pushes = [_remote(src(pid), land_refs[k].at[me if landing == "sender" else pid],
                              send_sems.at[k * (N_DEV - 1) + i], recv_sems.at[k * (N_DEV - 1) + i], dev)
                      for i, (dev, pid) in enumerate(peers)]
            out.append((own, pushes))
        return out


def _scatter_wait(scatters, after, name):
    hbm = pl.BlockSpec(memory_space=pltpu.HBM)
    sem = pl.BlockSpec(memory_space=pltpu.SEMAPHORE)
    n_arr = [2 * sc.n for sc in scatters]
    total = sum(n_arr)

    def body(*refs):
        arrs, sems = refs[:total], refs[total:total + 3 * len(scatters)]
        a0 = 0
        for j, sc in enumerate(scatters):
            g_refs, land_refs = arrs[a0:a0 + sc.n], arrs[a0 + sc.n:a0 + 2 * sc.n]
            a0 += 2 * sc.n
            send_sems, recv_sems, own_sems = sems[3 * j:3 * j + 3]
            for (own, sent), (_, got) in zip(sc._copies(g_refs, land_refs, send_sems, recv_sems, own_sems, "sender"),
                                             sc._copies(g_refs, land_refs, send_sems, recv_sems, own_sems, "receiver")):
                own.wait()
                for cp in sent:
                    cp.wait_send()
                for cp in got:
                    cp.wait_recv()

    operands = [a for sc in scatters for a in sc.thru]
    outs = pl.pallas_call(
        body, name=name,
        out_shape=tuple(pltpu.HBM(a.shape, a.dtype) for a in operands),
        in_specs=(hbm,) * total + (sem,) * (3 * len(scatters)) + (pl.BlockSpec(memory_space=pl.ANY),),
        out_specs=(hbm,) * total, input_output_aliases={k: k for k in range(total)},
        compiler_params=pltpu.CompilerParams(has_side_effects=pltpu.SideEffectType.DATAFLOW_SIDE_EFFECTING),
    )(*operands, *[s for sc in scatters for s in sc.sems], after)
    lands, a0 = [], 0
    for sc in scatters:
        lands.extend(outs[a0 + sc.n:a0 + 2 * sc.n])
        a0 += 2 * sc.n
    return lands


def _sum_slots(buf, name):
    _, rows, cols = buf.shape
    tr = _div_tile(rows, 256, 2 * SUBLANES)

    def body(b_ref, o_ref):
        acc = b_ref[0].astype(F32)
        for k in range(1, N_DEV):
            acc = acc + b_ref[k].astype(F32)
        o_ref[...] = acc

    return pl.pallas_call(
        functools.partial(body), name=name, grid=(rows // tr,),
        out_shape=jax.ShapeDtypeStruct((rows, cols), F32),
        in_specs=[pl.BlockSpec((N_DEV, tr, cols), lambda i: (0, i, 0))],
        out_specs=pl.BlockSpec((tr, cols), lambda i: (i, 0)),
        compiler_params=_params("parallel"),
    )(buf)


def _small_ar(vec, dmods, silu_all, w_ada, c_ctx):
    rv = vec.shape[0]
    D = silu_all.shape[1]
    ncol = w_ada.shape[1]
    nm = dmods.shape[1]
    srows = silu_all.shape[0]

    def body(vec_ref, dm_ref, s_ref, w_ref, cc_ref, tot_ref, gb_ref, gw_ref, gc_ref,
             vbuf, mbuf, tbuf, dmx, send1, recv1, send3, recv3):
        me, _ = _me_and_peers()
        vbuf[me] = vec_ref[...]
        mbuf[me] = dm_ref[...]
        both = [(lambda p: vbuf.at[me], lambda p: vbuf.at[p]), (lambda p: mbuf.at[me], lambda p: mbuf.at[p])]
        _push_start(both, ALL_PEERS, send1, recv1)
        _push_wait_recv(both, ALL_PEERS, send1, recv1)
        _push_wait_send(both, ALL_PEERS, send1, recv1)
        tot = vbuf[0]
        msum = mbuf[0]
        for k in range(1, N_DEV):
            tot = tot + vbuf[k]
            msum = msum + mbuf[k]
        tot_ref[...] = tot
        gb_ref[...] = jnp.sum(msum, axis=0, keepdims=True)
        loc = pl.ds(pl.multiple_of(me * ncol, ncol), ncol)
        for k in range(N_DEV):
            dmx[k * SUBLANES:(k + 1) * SUBLANES, :] = mbuf[k, :, loc]
        cm = msum[2:3, :]
        mbuf[0, 2:3, :] = cm
        cm_loc = mbuf[0, 2:3, loc]
        dmx[N_DEV * SUBLANES:, :] = jnp.concatenate([cm_loc, jnp.zeros((SUBLANES - 1, ncol), F32)], axis=0)
        gw_ref[...] = _dot_tn(s_ref[...], dmx[...])
        tbuf[me] = _dot_nt(dmx[N_DEV * SUBLANES:, :], w_ref[...])
        _exchange(lambda p: tbuf.at[me], lambda p: tbuf.at[p], send3, recv3)
        tsum = tbuf[0]
        for k in range(1, N_DEV):
            tsum = tsum + tbuf[k]
        cc = cc_ref[...]
        sg = _sigmoid(cc)
        gc_ref[...] = tsum[0:1, :] * (sg * (1.0 + cc * (1.0 - sg)))

    return pl.pallas_call(
        body, name="small_ar",
        out_shape=(jax.ShapeDtypeStruct((rv, LANES), F32), jax.ShapeDtypeStruct((1, nm), F32),
                   jax.ShapeDtypeStruct((D, ncol), F32), jax.ShapeDtypeStruct((1, D), F32)),
        in_specs=[_vmem()] * 5, out_specs=(_vmem(),) * 4,
        scratch_shapes=[pltpu.VMEM((N_DEV, rv, LANES), F32), pltpu.VMEM((N_DEV, SUBLANES, nm), F32),
                        pltpu.VMEM((N_DEV, SUBLANES, D), F32), pltpu.VMEM((srows, ncol), F32)]
                       + [pltpu.SemaphoreType.DMA((2, N_DEV - 1))] * 2 + [pltpu.SemaphoreType.DMA((N_DEV - 1,))] * 2,
        compiler_params=pltpu.CompilerParams(vmem_limit_bytes=VMEM_LIMIT),
    )(vec, dmods, silu_all, w_ada, c_ctx.reshape(1, D))


def _adam_update(w, g, m, v):
    mn = ADAM_B1 * m + (1.0 - ADAM_B1) * g
    vn = ADAM_B2 * v + (1.0 - ADAM_B2) * (g * g)
    m_hat = mn / (1.0 - ADAM_B1 ** ADAM_STEP)
    v_hat = vn / (1.0 - ADAM_B2 ** ADAM_STEP)
    return -ADAM_LR * (m_hat / (jnp.sqrt(v_hat) + ADAM_EPS) + ADAM_WD * w), mn, vn


def _adamw(w, g, m, v, name):
    rows, cols = w.shape
    tr = _div_tile(rows, 256, SUBLANES) if rows * cols > 65536 else rows

    def body(w_ref, g_ref, m_ref, v_ref, d_ref, nm_ref, nv_ref):
        d_ref[...], nm_ref[...], nv_ref[...] = _adam_update(w_ref[...], g_ref[...], m_ref[...], v_ref[...])

    spec = pl.BlockSpec((tr, cols), lambda i: (i, 0))
    return pl.pallas_call(
        functools.partial(body), name=name, grid=(rows // tr,),
        out_shape=(jax.ShapeDtypeStruct((rows, cols), F32),) * 3,
        in_specs=[spec] * 4, out_specs=(spec,) * 3,
        compiler_params=_params("parallel"),
    )(w, g, m, v)


def _adamw_small(items, name):
    n = len(items)

    def body(*refs):
        ins, outs = refs[:4 * n], refs[4 * n:]
        for i in range(n):
            w_ref, g_ref, m_ref, v_ref = ins[4 * i:4 * i + 4]
            outs[3 * i][...], outs[3 * i + 1][...], outs[3 * i + 2][...] = _adam_update(
                w_ref[...], g_ref[...], m_ref[...], v_ref[...])

    outs = pl.pallas_call(
        body, name=name,
        out_shape=tuple(jax.ShapeDtypeStruct(it[0].shape, F32) for it in items for _ in range(3)),
        in_specs=[_vmem()] * (4 * n), out_specs=(_vmem(),) * (3 * n),
        compiler_params=pltpu.CompilerParams(vmem_limit_bytes=VMEM_LIMIT),
    )(*[a for it in items for a in it])
    return [tuple(outs[3 * i:3 * i + 3]) for i in range(n)]


def _sum_adamw(buf, w, m, v, name):
    _, rows, cols = buf.shape
    tr = _div_tile(rows, 256, 2 * SUBLANES)

    def body(b_ref, w_ref, m_ref, v_ref, g_ref, d_ref, nm_ref, nv_ref):
        g = b_ref[0].astype(F32)
        for k in range(1, N_DEV):
            g = g + b_ref[k].astype(F32)
        g_ref[...] = g
        d_ref[...], nm_ref[...], nv_ref[...] = _adam_update(w_ref[...], g, m_ref[...], v_ref[...])

    spec = pl.BlockSpec((tr, cols), lambda i: (i, 0))
    return pl.pallas_call(
        functools.partial(body), name=name, grid=(rows // tr,),
        out_shape=(jax.ShapeDtypeStruct((rows, cols), F32),) * 4,
        in_specs=[pl.BlockSpec((N_DEV, tr, cols), lambda i: (0, i, 0))] + [spec] * 3, out_specs=(spec,) * 4,
        compiler_params=_params("parallel"),
    )(buf, w, m, v)


def _rope_tables(n_ctx, n):
    n_freq = RET_DIM // 4
    inv = np.float32(ROPE_BASE) ** (-np.arange(n_freq, dtype=np.float32) / np.float32(n_freq))
    tok = np.arange(n)
    pos_r = (tok // GRID_W).astype(np.float32)
    pos_c = (tok % GRID_W).astype(np.float32)
    ang_r = (pos_r[:, None] * inv[None, :]).astype(np.float32)
    ang_c = (pos_c[:, None] * inv[None, :]).astype(np.float32)
    cos = np.concatenate([np.cos(ang_r), np.cos(ang_r), np.cos(ang_c), np.cos(ang_c)], axis=-1)
    sin = np.concatenate([-np.sin(ang_r), np.sin(ang_r), -np.sin(ang_c), np.sin(ang_c)], axis=-1)
    cos = np.concatenate([np.ones((n_ctx, RET_DIM), np.float32), cos], axis=0)
    sin = np.concatenate([np.zeros((n_ctx, RET_DIM), np.float32), sin], axis=0)
    return jnp.asarray(cos, F32), jnp.asarray(sin, F32)


def _na_tables():
    q = np.arange(GRID_W)[:, None]
    k = np.arange(GRID_W)[None, :]
    start = np.clip(q - NA_KW // 2, 0, GRID_W - NA_KW)
    valid = (k >= start) & (k < start + NA_KW)
    dc = np.clip(k - q + (NA_KW - 1), 0, 2 * NA_KW - 2)
    ncls = 2 * NA_KW - 1
    onehot = (dc[None] == np.arange(ncls)[:, None, None]) & valid[None]
    oh2 = np.zeros((GRID_W, LANES, LANES), np.float32)
    for c in range(ncls):
        oh2[:, :GRID_W, c] = onehot[c]
        oh2[:, GRID_W:, 32 + c] = onehot[c]
    return onehot.astype(np.float32), valid, oh2.reshape(GRID_W * LANES, LANES)


def _paired_bias(rpb, onehot, valid):
    t = jnp.einsum("hdc,cqk->hdqk", rpb, jnp.asarray(onehot), precision=lax.Precision.HIGHEST)
    t = jnp.where(jnp.asarray(valid)[None, None], t, NEG_INF)
    return jnp.concatenate([t[:, :-1], t[:, 1:]], axis=-1)


def kernel(x, c, ctx, c_ctx, w_ada, b_ada, g_pre_mix, g_post_mix, g_pre_mlp, g_post_mlp, w_in, ret_decay, ret_gn, na_rpb, w_out, w_mlp1, w_mlp2, loss_target, m_c_ctx, m_w_ada, m_b_ada, m_g_pre_mix, m_g_post_mix, m_g_pre_mlp, m_g_post_mlp, m_w_in, m_ret_decay, m_ret_gn, m_na_rpb, m_w_out, m_w_mlp1, m_w_mlp2, v_c_ctx, v_w_ada, v_b_ada, v_g_pre_mix, v_g_post_mix, v_g_pre_mlp, v_g_post_mlp, v_w_in, v_ret_decay, v_ret_gn, v_na_rpb, v_w_out, v_w_mlp1, v_w_mlp2):
    B, N, D = x.shape
    C = ctx.shape[1]
    T = C + N

    silu_all, mods_g, win_b, wout_l, w1_l, w2_l = _mod_gather(c, c_ctx, w_ada[0], b_ada, w_in[0].T, w_out[0],
                                                             w_mlp1[0], w_mlp2[0])
    mods_mine = mods_g.transpose(1, 0, 2).reshape(mods_g.shape[1], N_MOD * D)
    modl = jnp.concatenate([mods_mine[:B], mods_mine[SUBLANES:SUBLANES + 1]], axis=0)
    modl = modl.reshape(B + 1, N_MOD, 1, D)
    rin = w_in.shape[2]
    rout, c1, r2 = wout_l.shape[0], w1_l.shape[1], w2_l.shape[0]

    def rows_of(n):
        return lambda ref: _row_block(ref, n)

    def cols_of(n):
        return lambda ref: _col_block(ref, n)

    cos, sin = _rope_tables(C, N)
    onehot, valid, oh2 = _na_tables()
    bias2 = _paired_bias(na_rpb[0], onehot, valid)
    lg = jax.nn.log_sigmoid(ret_decay[0].astype(F32))

    level_one = SIBLING + ICI_SAME_CORE
    h_all, proj, w1_part = _inproj_fwd(
        x, ctx, modl, g_pre_mix, win_b,
        [_Hosted("gather", level_one, [w1_l], [cols_of(c1)], [jax.ShapeDtypeStruct((D, N_DEV * c1), BF16)], True)])
    o_ret, lat_ret, wout_b = _ret_fwd(
        proj, cos, sin, lg, ret_gn, C,
        [_Hosted("gather", ALL_PEERS, [wout_l], [rows_of(rout)], [jax.ShapeDtypeStruct((N_DEV * rout, D), BF16)], True)])
    lat_na, w1_b, w2_part = _na_fwd(
        proj, bias2, C,
        [_HostedRelay([w1_part], [cols_of(c1)]),
         _Hosted("gather", level_one, [w2_l], [rows_of(r2)], [jax.ShapeDtypeStruct((N_DEV * r2, D), BF16)], True)])

    (dy1, dlat_ret, dlat_na, dmix, h2, act, du, dz, red_d) = _dense_core(
        lat_ret, lat_na, x, loss_target, modl, g_post_mix, g_pre_mlp, g_post_mlp, wout_b, w1_b, w2_part)

    gw_out_p = _tn_matmul(lat_ret[:, None], dmix, "gw_out_ret", rows_after=lat_na.shape[-1])
    gw_out_p = _tn_matmul(lat_na[:, None], dmix, "gw_out_na", rows_before=lat_ret.shape[-1], into=gw_out_p)
    gw1_p = _tn_matmul(h2[:, None], du, "gw_mlp1")
    gw2_p = _tn_matmul(act[:, None], dz, "gw_mlp2")
    rs_mlp = _SplitScatter([gw_out_p, gw1_p, gw2_p], [rows_of(rout), cols_of(c1), rows_of(r2)],
                           [(rout, D), (D, c1), (r2, D)], "rs_mlp_start")

    dret, dgn_p, dlg_p = _ret_bwd(proj, cos, sin, lg, ret_gn + rs_mlp.token[0, 0], o_ret, dlat_ret, C, [])
    dna, dbias2 = _na_bwd(proj, bias2, dlat_na, C, [])
    ret_cols, na_cols = dret.shape[1] * dret.shape[3], dna.shape[1] * dna.shape[3]
    gwin_t_p = _tn_matmul(dret, h_all, "gw_in_ret", rows_after=na_cols)
    gwin_t_p = _tn_matmul(dna, h_all, "gw_in_na", rows_before=ret_cols, into=gwin_t_p)
    rs_in = _SplitScatter([gwin_t_p], [rows_of(rin)], [(rin, D)], "rs_w_in_start")
    grad_x, red_i = _inproj_bwd(dret, dna, x, ctx, dy1, modl, g_pre_mix + rs_in.token[0, 0], win_b, [])

    rd = red_d.sum(axis=1)[:, :, :]
    ri = red_i
    nct = ri.shape[1] * C // T
    ri_ctx = ri[:, :nct].sum(axis=(0, 1))
    ri_lat = ri[:, nct:].sum(axis=1)
    d_mods = jnp.concatenate([ri_lat[:, 0], ri_lat[:, 1], rd[:, 0], rd[:, 4], rd[:, 3], rd[:, 2]], axis=-1)
    d_cmods = jnp.concatenate([ri_ctx[0], ri_ctx[1], jnp.zeros(((N_MOD - 2) * D,), F32)])[None]
    dm_slot = jnp.concatenate([d_mods, d_cmods, jnp.zeros((SUBLANES - B - 1, N_MOD * D), F32)], axis=0)
    dg_pre_mix = ri_lat[:, 2].sum(axis=0) + ri_ctx[2]
    dg_post_mix = rd[:, 1].sum(axis=0)
    dg_pre_mlp = rd[:, 5].sum(axis=0)
    dg_post_mlp = rd[:, 6].sum(axis=0)
    loss_p = rd[:, 7, 0].sum()
    d_gn = dgn_p[:, 0].sum(axis=0)
    d_lg = dlg_p[:, :, :2, 0].sum(axis=0).T
    d_decay = d_lg * jax.nn.sigmoid(-ret_decay[0].astype(F32))
    rr = _rpb_reduce(dbias2, jnp.asarray(oh2, BF16)).reshape(NA_HEADS, 2 * NA_KH - 2, LANES)
    ncls = 2 * NA_KW - 1
    d_rpb = (jnp.pad(rr[:, :, :ncls], ((0, 0), (0, 1), (0, 0))) + jnp.pad(rr[:, :, 32:32 + ncls], ((0, 0), (1, 0), (0, 0))))
    d_rpb32 = jnp.pad(d_rpb, ((0, 0), (0, 0), (0, 32 - ncls)))
    pieces = [dg_pre_mix, dg_post_mix, dg_pre_mlp, dg_post_mlp, d_gn, d_rpb32.reshape(-1),
              jnp.pad(d_decay.reshape(-1), (0, LANES - d_decay.size)), jnp.full((LANES,), loss_p, F32)]
    vec = jnp.concatenate(pieces)
    pad = (-vec.shape[0]) % (SUBLANES * LANES)
    vec = jnp.pad(vec, (0, pad)).reshape(-1, LANES)
    tot, g_b_ada, g_w_ada, g_c_ctx = _small_ar(vec, dm_slot, silu_all, w_ada[0], c_ctx)
    land_out, land_1, land_2, land_in = _scatter_wait([rs_mlp, rs_in], tot, "rs_wait")
    g_w_in = _sum_slots(land_in, "sum_w_in").T
    fused = {"w_out": _sum_adamw(land_out, w_out[0], m_w_out[0], v_w_out[0], "sum_adamw_w_out"),
             "w_mlp1": _sum_adamw(land_1, w_mlp1[0], m_w_mlp1[0], v_w_mlp1[0], "sum_adamw_w_mlp1"),
             "w_mlp2": _sum_adamw(land_2, w_mlp2[0], m_w_mlp2[0], v_w_mlp2[0], "sum_adamw_w_mlp2")}
    flat = tot.reshape(-1)
    o0 = 0
    g_pre_mix_g = flat[o0:o0 + D]; o0 += D
    g_post_mix_g = flat[o0:o0 + D]; o0 += D
    g_pre_mlp_g = flat[o0:o0 + D]; o0 += D
    g_post_mlp_g = flat[o0:o0 + D]; o0 += D
    g_gn = flat[o0:o0 + RET_WIDTH]; o0 += RET_WIDTH
    nrpb = NA_HEADS * (2 * NA_KH - 1) * 32
    g_rpb = flat[o0:o0 + nrpb].reshape(NA_HEADS, 2 * NA_KH - 1, 32)[:, :, :ncls]; o0 += nrpb
    g_decay = flat[o0:o0 + 2 * RET_HEADS].reshape(2, RET_HEADS); o0 += LANES
    loss = flat[o0]

    grads = {
        "c_ctx": g_c_ctx.reshape(c_ctx.shape), "w_ada": g_w_ada[None], "b_ada": g_b_ada.reshape(b_ada.shape),
        "g_pre_mix": g_pre_mix_g[None], "g_post_mix": g_post_mix_g[None], "g_pre_mlp": g_pre_mlp_g[None],
        "g_post_mlp": g_post_mlp_g[None], "w_in": g_w_in[None], "ret_decay": g_decay[None], "ret_gn": g_gn[None],
        "na_rpb": g_rpb[None], "w_out": fused["w_out"][0][None], "w_mlp1": fused["w_mlp1"][0][None],
        "w_mlp2": fused["w_mlp2"][0][None],
    }
    weights = dict(c_ctx=c_ctx, w_ada=w_ada, b_ada=b_ada, g_pre_mix=g_pre_mix, g_post_mix=g_post_mix,
                   g_pre_mlp=g_pre_mlp, g_post_mlp=g_post_mlp, w_in=w_in, ret_decay=ret_decay, ret_gn=ret_gn,
                   na_rpb=na_rpb, w_out=w_out, w_mlp1=w_mlp1, w_mlp2=w_mlp2)
    m_in = dict(c_ctx=m_c_ctx, w_ada=m_w_ada, b_ada=m_b_ada, g_pre_mix=m_g_pre_mix, g_post_mix=m_g_post_mix,
                g_pre_mlp=m_g_pre_mlp, g_post_mlp=m_g_post_mlp, w_in=m_w_in, ret_decay=m_ret_decay,
                ret_gn=m_ret_gn, na_rpb=m_na_rpb, w_out=m_w_out, w_mlp1=m_w_mlp1, w_mlp2=m_w_mlp2)
    v_in = dict(c_ctx=v_c_ctx, w_ada=v_w_ada, b_ada=v_b_ada, g_pre_mix=v_g_pre_mix, g_post_mix=v_g_post_mix,
                g_pre_mlp=v_g_pre_mlp, g_post_mlp=v_g_post_mlp, w_in=v_w_in, ret_decay=v_ret_decay,
                ret_gn=v_ret_gn, na_rpb=v_na_rpb, w_out=v_w_out, w_mlp1=v_w_mlp1, w_mlp2=v_w_mlp2)
    names = list(weights)
    deltas, new_m, new_v = {}, {}, {}
    def as_2d(n):
        shp = weights[n].shape
        two_d = (-1, shp[-1]) if len(shp) > 1 else (1, shp[0])
        return [a.reshape(two_d) for a in (weights[n], grads[n], m_in[n], v_in[n])]

    small = [n for n in names if n not in fused and weights[n].size <= 65536]
    updated = dict(zip(small, _adamw_small([as_2d(n) for n in small], "adamw_small")))
    for n in names:
        if n in fused:
            updated[n] = fused[n][1:]
        elif n not in updated:
            updated[n] = _adamw(*as_2d(n), "adamw_" + n)
        deltas[n], new_m[n], new_v[n] = (a.reshape(weights[n].shape) for a in updated[n])
    return (loss, grad_x, *[grads[n] for n in names], *[deltas[n] for n in names],
            *[new_m[n] for n in names], *[new_v[n] for n in names])
```

```python
import functools
import math

import numpy as np
import jax
import jax.numpy as jnp
from jax import lax
from jax.experimental import pallas as pl
from jax.experimental.pallas import tpu as pltpu

F32 = jnp.float32
BF16 = jnp.bfloat16
MESH = pl.DeviceIdType.MESH

N_DEV = 8
LANES = 128
SUBLANES = 8
VMEM_LIMIT = 60 * 1024 * 1024

GRID_W = 64
RET_HEADS = 4
RET_DIM = 128
RET_WIDTH = RET_HEADS * RET_DIM
NA_HEADS = 8
NA_DIM = 64
NA_WIDTH = NA_HEADS * NA_DIM
NA_PAIRS = NA_HEADS // 2
NA_KH = 8
NA_KW = 16
NA_GROUP = 8
SEG = 512
ROPE_BASE = 10000.0
NORM_EPS = 1e-6
NEG_INF = -1e30
N_MOD = 6

ADAM_LR = 0.001
ADAM_B1 = 0.9
ADAM_B2 = 0.999
ADAM_EPS = 1e-08
ADAM_WD = 0.01
ADAM_STEP = 10


def _dot(a, b):
    return lax.dot_general(a, b, (((1,), (0,)), ((), ())), preferred_element_type=F32)


def _dot_nt(a, b):
    return lax.dot_general(a, b, (((1,), (1,)), ((), ())), preferred_element_type=F32)


def _dot_tn(a, b):
    return lax.dot_general(a, b, (((0,), (0,)), ((), ())), preferred_element_type=F32)


def _sigmoid(x):
    return 1.0 / (1.0 + jnp.exp(-x))


def _div_tile(n, cap, mult):
    if n <= cap:
        return n
    for t in range(cap - cap % mult, 0, -mult):
        if n % t == 0:
            return t
    raise ValueError(f"no tile for {n}")


def _params(*sem):
    return pltpu.CompilerParams(dimension_semantics=tuple(sem) if sem else None,
                                vmem_limit_bytes=VMEM_LIMIT)


def _vmem():
    return pl.BlockSpec(memory_space=pltpu.VMEM)


def _any():
    return pl.BlockSpec(memory_space=pl.ANY)


def _me_and_peers():
    x, y, c = lax.axis_index("x"), lax.axis_index("y"), lax.axis_index("c")
    me = 4 * x + 2 * y + c
    peers = []
    for m in range(1, N_DEV):
        px = 1 - x if (m >> 2) & 1 else x
        py = 1 - y if (m >> 1) & 1 else y
        pc = 1 - c if m & 1 else c
        peers.append(((px, py, pc), 4 * px + 2 * py + pc))
    return me, peers


def _exchange(src_for, dst_from, send_sems, recv_sems):
    me, peers = _me_and_peers()
    sent = []
    for i, (dev, pid) in enumerate(peers):
        cp = pltpu.make_async_remote_copy(src_ref=src_for(pid), dst_ref=dst_from(me),
                                          send_sem=send_sems.at[i], recv_sem=recv_sems.at[i],
                                          device_id=dev, device_id_type=MESH)
        cp.start()
        sent.append(cp)
    for i, (dev, pid) in enumerate(peers):
        pltpu.make_async_remote_copy(src_ref=src_for(pid), dst_ref=dst_from(pid),
                                     send_sem=send_sems.at[i], recv_sem=recv_sems.at[i],
                                     device_id=dev, device_id_type=MESH).wait_recv()
    for cp in sent:
        cp.wait_send()


SIBLING = (1,)
ICI_SAME_CORE = (2, 4, 6)
ALL_PEERS = tuple(range(1, N_DEV))


def _remote(src, dst, send_sem, recv_sem, dev):
    return pltpu.make_async_remote_copy(src_ref=src, dst_ref=dst, send_sem=send_sem, recv_sem=recv_sem,
                                        device_id=dev, device_id_type=MESH)


def _push_start(items, masks, send_sems, recv_sems):
    me, peers = _me_and_peers()
    for k, (src_for, dst_from) in enumerate(items):
        for m in masks:
            dev, pid = peers[m - 1]
            _remote(src_for(pid), dst_from(me), send_sems.at[k, m - 1], recv_sems.at[k, m - 1], dev).start()


def _push_wait_recv(items, masks, send_sems, recv_sems):
    me, peers = _me_and_peers()
    for k, (src_for, dst_from) in enumerate(items):
        for m in masks:
            dev, pid = peers[m - 1]
            _remote(src_for(pid), dst_from(pid), send_sems.at[k, m - 1], recv_sems.at[k, m - 1], dev).wait_recv()


def _push_wait_send(items, masks, send_sems, recv_sems):
    me, peers = _me_and_peers()
    for k, (src_for, dst_from) in enumerate(items):
        for m in masks:
            dev, pid = peers[m - 1]
            _remote(src_for(pid), dst_from(me), send_sems.at[k, m - 1], recv_sems.at[k, m - 1], dev).wait_send()


def _forward_start(items, send_sems, recv_sems):
    me, peers = _me_and_peers()
    sib = peers[0][0]
    for k, (blk_in, blk_out) in enumerate(items):
        for j, m in enumerate(ICI_SAME_CORE):
            pid = peers[m - 1][1]
            _remote(blk_in(pid), blk_out(pid), send_sems.at[k, j], recv_sems.at[k, j], sib).start()


def _forward_wait(items, send_sems, recv_sems):
    me, peers = _me_and_peers()
    sib = peers[0][0]
    for k, (blk_in, blk_out) in enumerate(items):
        for j, m in enumerate(ICI_SAME_CORE):
            got = peers[(m | 1) - 1][1]
            _remote(blk_in(got), blk_out(got), send_sems.at[k, j], recv_sems.at[k, j], sib).wait_recv()
    for k, (blk_in, blk_out) in enumerate(items):
        for j, m in enumerate(ICI_SAME_CORE):
            pid = peers[m - 1][1]
            _remote(blk_in(pid), blk_out(pid), send_sems.at[k, j], recv_sems.at[k, j], sib).wait_send()


def _mod_gather(c, c_ctx, w_ada, b_ada, w_in_t, w_out, w1, w2):
    B, D = c.shape
    ncol = w_ada.shape[1]
    rows = SUBLANES * N_DEV + SUBLANES

    def body(c_ref, cc_ref, w_ref, b_ref, win_ref, wout_ref, w1_ref, w2_ref,
             s_ref, m_ref, gin_ref, wout_b, w1_b, w2_b,
             win_b, msend, send1, recv1, send2, recv2, wsend, wrecv, fsend, frecv, lsem):
        me, _ = _me_and_peers()
        win_b[...] = win_ref[...].astype(BF16)
        block = _row_block(gin_ref, w_in_t.shape[0])
        gather = [(lambda p: win_b, block)]
        own = pltpu.make_async_copy(win_b, block(me), lsem.at[0])
        cv = c_ref[...]
        slot = jnp.concatenate([cv * _sigmoid(cv), jnp.zeros((SUBLANES - B, D), F32)], axis=0)
        my_rows = pl.ds(pl.multiple_of(me * SUBLANES, SUBLANES), SUBLANES)
        s_ref[my_rows, :] = slot
        ccv = cc_ref[...]
        s_ref[SUBLANES * N_DEV:, :] = jnp.concatenate(
            [ccv * _sigmoid(ccv), jnp.zeros((SUBLANES - 1, D), F32)], axis=0)

        def rows_of(p):
            return s_ref.at[pl.ds(pl.multiple_of(p * SUBLANES, SUBLANES), SUBLANES), :]

        _exchange(lambda p: rows_of(me), rows_of, send1, recv1)
        own.start()
        _push_start(gather, SIBLING + ICI_SAME_CORE, wsend, wrecv)
        wout_b[...] = wout_ref[...].astype(BF16)
        w1_b[...] = w1_ref[...].astype(BF16)
        w2_b[...] = w2_ref[...].astype(BF16)
        b_loc = b_ref[:, pl.ds(pl.multiple_of(me * ncol, ncol), ncol)]
        mods = _dot(s_ref[...], w_ref[...]) + b_loc
        for p in range(N_DEV):
            msend[p] = jnp.concatenate([mods[p * SUBLANES:(p + 1) * SUBLANES], mods[N_DEV * SUBLANES:]], axis=0)
        m_ref[me] = msend[me]
        columns = [(lambda p: msend.at[p], lambda p: m_ref.at[p])]
        _push_start(columns, ALL_PEERS, send2, recv2)
        _push_wait_recv(gather, ICI_SAME_CORE, wsend, wrecv)
        relay = [(block, block)]
        _forward_start(relay, fsend, frecv)
        _push_wait_recv(columns, ALL_PEERS, send2, recv2)
        _push_wait_recv(gather, SIBLING, wsend, wrecv)
        _forward_wait(relay, fsend, frecv)
        _push_wait_send(columns, ALL_PEERS, send2, recv2)
        _push_wait_send(gather, SIBLING + ICI_SAME_CORE, wsend, wrecv)
        own.wait()

    return pl.pallas_call(
        body, name="mod_gather",
        out_shape=(jax.ShapeDtypeStruct((rows, D), F32), jax.ShapeDtypeStruct((N_DEV, 2 * SUBLANES, ncol), F32),
                   jax.ShapeDtypeStruct((N_DEV * w_in_t.shape[0], D), BF16),
                   jax.ShapeDtypeStruct(w_out.shape, BF16), jax.ShapeDtypeStruct(w1.shape, BF16),
                   jax.ShapeDtypeStruct(w2.shape, BF16)),
        in_specs=[_vmem()] * 8, out_specs=(_vmem(), _vmem(), _any(), _vmem(), _vmem(), _vmem()),
        scratch_shapes=[pltpu.VMEM(w_in_t.shape, BF16), pltpu.VMEM((N_DEV, 2 * SUBLANES, ncol), F32)]
                       + [pltpu.SemaphoreType.DMA((N_DEV - 1,))] * 2
                       + [pltpu.SemaphoreType.DMA((1, N_DEV - 1))] * 4 + [pltpu.SemaphoreType.DMA((1, 3))] * 2
                       + [pltpu.SemaphoreType.DMA((1,))],
        compiler_params=pltpu.CompilerParams(vmem_limit_bytes=VMEM_LIMIT),
    )(c, c_ctx.reshape(1, D), w_ada, b_ada, w_in_t, w_out, w1, w2)


def _row_block(ref, rows):
    return lambda p: ref.at[pl.ds(pl.multiple_of(p * rows, 2 * SUBLANES), rows), :]


def _col_block(ref, cols):
    return lambda p: ref.at[:, pl.ds(pl.multiple_of(p * cols, LANES), cols)]


def _slot(ref):
    return lambda p: ref.at[p]


class _Hosted:
    def __init__(self, kind, masks, operands, block_of, out_shapes, with_own):
        self.kind, self.masks, self.operands = kind, masks, list(operands)
        self.block_of, self.out_shapes, self.with_own = block_of, list(out_shapes), with_own
        self.n = len(self.operands)

    def scratch(self):
        return [pltpu.SemaphoreType.DMA((self.n, N_DEV - 1)), pltpu.SemaphoreType.DMA((self.n, N_DEV - 1)),
                pltpu.SemaphoreType.DMA((self.n,))]

    def _items(self, in_refs, out_refs):
        items = []
        for k in range(self.n):
            if self.kind == "gather":
                items.append((lambda p, k=k: in_refs[k], self.block_of[k](out_refs[k])))
            else:
                items.append((self.block_of[k](in_refs[k]), _slot(out_refs[k])))
        return items

    def _own(self, in_refs, out_refs, lsem):
        me, _ = _me_and_peers()
        items = self._items(in_refs, out_refs)
        return [pltpu.make_async_copy(src_for(me), dst_from(me), lsem.at[k])
                for k, (src_for, dst_from) in enumerate(items)]

    def start(self, in_refs, out_refs, sems):
        send, recv, lsem = sems
        if self.with_own:
            for cp in self._own(in_refs, out_refs, lsem):
                cp.start()
        _push_start(self._items(in_refs, out_refs), self.masks, send, recv)

    def wait(self, in_refs, out_refs, sems):
        send, recv, lsem = sems
        items = self._items(in_refs, out_refs)
        _push_wait_recv(items, self.masks, send, recv)
        _push_wait_send(items, self.masks, send, recv)
        if self.with_own:
            for cp in self._own(in_refs, out_refs, lsem):
                cp.wait()


class _HostedRelay:
    def __init__(self, arrays, block_of):
        self.operands, self.block_of = list(arrays), block_of
        self.out_shapes = [jax.ShapeDtypeStruct(a.shape, a.dtype) for a in arrays]
        self.n = len(self.operands)

    def scratch(self):
        return [pltpu.SemaphoreType.DMA((self.n, 3)), pltpu.SemaphoreType.DMA((self.n, 3))]

    def _items(self, in_refs, out_refs):
        return [(self.block_of[k](in_refs[k]), self.block_of[k](out_refs[k])) for k in range(self.n)]

    def start(self, in_refs, out_refs, sems):
        _forward_start(self._items(in_refs, out_refs), *sems)

    def wait(self, in_refs, out_refs, sems):
        _forward_wait(self._items(in_refs, out_refs), *sems)


def _call_hosting(body, hosted, *, name, grid, out_shape, in_specs, out_specs, scratch_shapes, args):
    n_in, n_out, n_scr = len(in_specs), len(out_shape), len(scratch_shapes)
    hn = sum(hs.n for hs in hosted)
    n_sem = [len(hs.scratch()) for hs in hosted]

    def wrapped(*refs):
        ins = refs[:n_in]
        h_in = refs[n_in:n_in + hn]
        outs = refs[n_in + hn:n_in + hn + n_out]
        h_out = refs[n_in + hn + n_out:n_in + 2 * hn + n_out]
        scr = refs[n_in + 2 * hn + n_out:n_in + 2 * hn + n_out + n_scr]
        sems = refs[n_in + 2 * hn + n_out + n_scr:]
        ids = [pl.program_id(i) for i in range(len(grid))]
        first = functools.reduce(jnp.logical_and, [i == 0 for i in ids])
        last = functools.reduce(jnp.logical_and, [i == g - 1 for i, g in zip(ids, grid)])
        parts, o0, s0 = [], 0, 0
        for hs, ns in zip(hosted, n_sem):
            parts.append((hs, h_in[o0:o0 + hs.n], h_out[o0:o0 + hs.n], sems[s0:s0 + ns]))
            o0 += hs.n
            s0 += ns

        @pl.when(first)
        def _():
            for hs, hi, ho, se in parts:
                hs.start(hi, ho, se)

        body(*ins, *outs, *scr)

        @pl.when(last)
        def _():
            for hs, hi, ho, se in parts:
                hs.wait(hi, ho, se)

    aliases, o0 = {}, 0
    for hs in hosted:
        if isinstance(hs, _HostedRelay):
            aliases.update({n_in + o0 + k: n_out + o0 + k for k in range(hs.n)})
        o0 += hs.n
    return pl.pallas_call(
        wrapped, name=name, grid=grid,
        out_shape=tuple(out_shape) + tuple(s for hs in hosted for s in hs.out_shapes),
        in_specs=list(in_specs) + [_any()] * hn,
        out_specs=tuple(out_specs) + (_any(),) * hn,
        scratch_shapes=list(scratch_shapes) + [s for hs in hosted for s in hs.scratch()],
        input_output_aliases=aliases,
        compiler_params=_params(*(("arbitrary",) * len(grid))),
    )(*args, *[a for hs in hosted for a in hs.operands])


def _token_tiles(n_ctx, tm):
    nct = n_ctx // tm

    def ctx_spec(D):
        return pl.BlockSpec((None, tm, D), lambda b, t: (b, jnp.minimum(t, nct - 1), 0))

    def lat_spec(D):
        return pl.BlockSpec((None, tm, D), lambda b, t: (b, jnp.maximum(t - nct, 0), 0))

    return nct, ctx_spec, lat_spec


def _inproj_fwd(x, ctx, modl, g1, w_in_t, hosted):
    B, N, D = x.shape
    n_ctx = ctx.shape[1]
    T = n_ctx + N
    nw = w_in_t.shape[0]
    tm = _div_tile(n_ctx, 256, 16)
    nct, ctx_spec, lat_spec = _token_tiles(n_ctx, tm)

    def body(c_ref, x_ref, sh_ref, sc_ref, g_ref, w_ref, h_ref, p_ref):
        x = jnp.where(pl.program_id(1) < nct, c_ref[...], x_ref[...])
        r = lax.rsqrt(jnp.mean(x * x, axis=-1, keepdims=True) + NORM_EPS)
        h = ((x * r) * g_ref[...]) * (1.0 + sc_ref[...]) + sh_ref[...]
        hb = h.astype(BF16)
        h_ref[...] = hb
        p_ref[...] = _dot_nt(hb, w_ref[...])

    def mrow(b, t):
        return jnp.where(t < nct, B, b)

    return _call_hosting(
        body, hosted, name="inproj_fwd", grid=(B, T // tm),
        out_shape=(jax.ShapeDtypeStruct((B, T, D), BF16), jax.ShapeDtypeStruct((B, T, nw), F32)),
        in_specs=[ctx_spec(D), lat_spec(D),
                  pl.BlockSpec((None, None, 1, D), lambda b, t: (mrow(b, t), 0, 0, 0)),
                  pl.BlockSpec((None, None, 1, D), lambda b, t: (mrow(b, t), 1, 0, 0)),
                  pl.BlockSpec((1, D), lambda b, t: (0, 0)),
                  pl.BlockSpec((nw, D), lambda b, t: (0, 0))],
        out_specs=(pl.BlockSpec((None, tm, D), lambda b, t: (b, t, 0)),
                   pl.BlockSpec((None, tm, nw), lambda b, t: (b, t, 0))),
        scratch_shapes=[], args=(ctx, x, modl, modl, g1, w_in_t))


def _swap32(x):
    lane = lax.broadcasted_iota(jnp.int32, x.shape, 1)
    return jnp.where((lane % 64) < 32, pltpu.roll(x, 96, 1), pltpu.roll(x, 32, 1))


def _rope(x, cos, sin):
    return x * cos + _swap32(x) * sin


def _unrope(dy, cos, sin):
    return dy * cos + _swap32(dy * sin)


def _ret_weights(lgf, lgb, dist):
    return jnp.exp(jnp.where(dist >= 0.0, lgf * dist, -lgb * dist))


class _RetDecay:
    def __init__(self, lgf, lgb, rows):
        r = lax.broadcasted_iota(jnp.int32, (rows, RET_DIM), 0).astype(F32)
        self.head = r + 1.0
        self.tail = (rows - 1.0) - r
        self.q_f = jnp.exp(lgf * self.head)
        self.k_f = jnp.exp(lgf * self.tail)
        self.q_b = jnp.exp(lgb * self.tail)
        self.k_b = jnp.exp(lgb * self.head)


def _ret_states(kf32, vs, lgf, lgb, C, c, nt, hf, hb, hfa=None, hba=None):
    dec = _RetDecay(lgf, lgb, c)
    dec_c = _RetDecay(lgf, lgb, C)
    step_f = jnp.exp(jnp.zeros((RET_DIM, RET_DIM), F32) + lgf * c)
    step_b = jnp.exp(jnp.zeros((RET_DIM, RET_DIM), F32) + lgb * c)

    def upd(rows, kdec):
        return _dot_tn((kf32[rows, :] * kdec).astype(BF16), vs[rows, :])

    def lat(t):
        return slice(C + t * c, C + (t + 1) * c)

    state = upd(slice(0, C), dec_c.k_f)
    aged = jnp.zeros_like(state)
    for t in range(nt):
        hf[t] = state.astype(BF16)
        if hfa is not None:
            hfa[t] = aged
        if t < nt - 1:
            aged = step_f * (aged + c * state)
            state = step_f * state + upd(lat(t), dec.k_f)
    state = upd(slice(0, C), dec_c.k_b)
    aged = jnp.zeros_like(state)
    for t in range(nt - 1, -1, -1):
        hb[t] = state.astype(BF16)
        if hba is not None:
            hba[t] = aged
        if t > 0:
            aged = step_b * (aged + c * state)
            state = step_b * state + upd(lat(t), dec.k_b)
    return dec, dec_c, step_f, step_b


def _ret_fwd(proj, cos, sin, lg, gn, n_ctx, hosted):
    B, T, _ = proj.shape
    C = n_ctx
    N = T - C
    c = _div_tile(N, 256, 16)
    nt = N // c
    scale = RET_DIM ** -0.5

    def body(lg_ref, q_ref, k_ref, v_ref, g_ref, cos_ref, sin_ref, gn_ref, o_ref, lat_ref, qs, ks, vs, kf32, hf, hb):
        h = pl.program_id(1)
        lgf = lg_ref[0, h]
        lgb = lg_ref[1, h]
        for rows in [slice(0, C)] + [slice(C + t * c, C + (t + 1) * c) for t in range(nt)]:
            cosb = cos_ref[rows, :]
            sinb = sin_ref[rows, :]
            qs[rows, :] = (_rope(q_ref[rows, :], cosb, sinb) * scale).astype(BF16)
            kr = _rope(k_ref[rows, :], cosb, sinb)
            kf32[rows, :] = kr
            ks[rows, :] = kr.astype(BF16)
            vs[rows, :] = v_ref[rows, :].astype(BF16)
        gnv = gn_ref[...]
        dec, _, _, _ = _ret_states(kf32, vs, lgf, lgb, C, c, nt, hf, hb)
        rc = (lax.broadcasted_iota(jnp.int32, (c, c), 0) - lax.broadcasted_iota(jnp.int32, (c, c), 1)).astype(F32)
        w_diag = _ret_weights(lgf, lgb, rc)
        for t in range(nt):
            rows = slice(C + t * c, C + (t + 1) * c)
            qt = qs[rows, :]
            s = _dot_nt(qt, ks[rows, :])
            o = (_dot((s * w_diag).astype(BF16), vs[rows, :])
                 + dec.q_f * _dot(qt, hf[t]) + dec.q_b * _dot(qt, hb[t]))
            o_ref[t * c:(t + 1) * c, :] = o
            mu = jnp.mean(o, axis=-1, keepdims=True)
            oc = o - mu
            var = jnp.mean(oc * oc, axis=-1, keepdims=True)
            yh = oc * lax.rsqrt(var + NORM_EPS)
            g = g_ref[rows, :]
            lat_ref[t * c:(t + 1) * c, :] = ((yh * gnv) * (g * _sigmoid(g))).astype(BF16)

    def col(seg):
        return pl.BlockSpec((None, T, RET_DIM), lambda b, h, seg=seg: (b, 0, seg * RET_HEADS + h))

    return _call_hosting(
        body, hosted, name="ret_fwd", grid=(B, RET_HEADS),
        out_shape=(jax.ShapeDtypeStruct((B, N, RET_WIDTH), F32), jax.ShapeDtypeStruct((B, N, RET_WIDTH), BF16)),
        in_specs=[pl.BlockSpec(memory_space=pltpu.SMEM), col(0), col(1), col(2), col(3),
                  pl.BlockSpec((T, RET_DIM), lambda b, h: (0, 0)), pl.BlockSpec((T, RET_DIM), lambda b, h: (0, 0)),
                  pl.BlockSpec((1, RET_DIM), lambda b, h: (0, h))],
        out_specs=(pl.BlockSpec((None, N, RET_DIM), lambda b, h: (b, 0, h)),
                   pl.BlockSpec((None, N, RET_DIM), lambda b, h: (b, 0, h))),
        scratch_shapes=[pltpu.VMEM((T, RET_DIM), BF16)] * 3 + [pltpu.VMEM((T, RET_DIM), F32)]
                       + [pltpu.VMEM((nt, RET_DIM, RET_DIM), BF16)] * 2,
        args=(lg, proj, proj, proj, proj, cos, sin, gn))


def _ret_bwd(proj, cos, sin, lg, gn, o, dlat, n_ctx, hosted):
    B, T, _ = proj.shape
    C = n_ctx
    N = T - C
    c = _div_tile(N, 256, 16)
    nt = N // c
    scale = RET_DIM ** -0.5

    def lat(t):
        return slice(C + t * c, C + (t + 1) * c)

    def body(lg_ref, q_ref, k_ref, v_ref, g_ref, cos_ref, sin_ref, gn_ref, o_ref, dl_ref,
             d_ref, dgn_ref, dlg_ref, qs, ks, vs, dos, qf32, kf32, hf, hb, hfa, hba, gf_s, gb_s):
        h = pl.program_id(1)
        lgf = lg_ref[0, h]
        lgb = lg_ref[1, h]
        gnv = gn_ref[...]

        def fold(a):
            return jnp.sum(a.reshape(a.shape[0] // SUBLANES, SUBLANES, a.shape[1]), axis=0)

        for rows in [slice(0, C)] + [lat(t) for t in range(nt)]:
            cosb = cos_ref[rows, :]
            sinb = sin_ref[rows, :]
            qr = _rope(q_ref[rows, :], cosb, sinb) * scale
            qf32[rows, :] = qr
            qs[rows, :] = qr.astype(BF16)
            kr = _rope(k_ref[rows, :], cosb, sinb)
            kf32[rows, :] = kr
            ks[rows, :] = kr.astype(BF16)
            vs[rows, :] = v_ref[rows, :].astype(BF16)

        dgn = jnp.zeros((1, RET_DIM), F32)
        for t in range(nt):
            lrows = slice(t * c, (t + 1) * c)
            ov = o_ref[lrows, :]
            mu = jnp.mean(ov, axis=-1, keepdims=True)
            oc = ov - mu
            var = jnp.mean(oc * oc, axis=-1, keepdims=True)
            rstd = lax.rsqrt(var + NORM_EPS)
            yh = oc * rstd
            g = g_ref[lat(t), :]
            sg = _sigmoid(g)
            dl = dl_ref[lrows, :]
            d_ref[3, lat(t), :] = (dl * (yh * gnv) * (sg * (1.0 + g * (1.0 - sg)))).astype(BF16)
            dls = dl * (g * sg)
            dgn = dgn + jnp.sum(dls * yh, axis=0, keepdims=True)
            dyh = dls * gnv
            do = rstd * (dyh - jnp.mean(dyh, axis=-1, keepdims=True)
                         - yh * jnp.mean(dyh * yh, axis=-1, keepdims=True))
            dos[lrows, :] = do.astype(BF16)
        dgn_ref[...] = jnp.concatenate([dgn, jnp.zeros((SUBLANES - 1, RET_DIM), F32)], axis=0)
        d_ref[3, 0:C, :] = jnp.zeros((C, RET_DIM), BF16)
        d_ref[0, 0:C, :] = jnp.zeros((C, RET_DIM), BF16)

        dec, dec_c, step_f, step_b = _ret_states(kf32, vs, lgf, lgb, C, c, nt, hf, hb, hfa, hba)

        def zmat(t, qdec):
            return _dot_tn((qf32[lat(t), :] * qdec).astype(BF16), dos[t * c:(t + 1) * c, :])

        acc3f = jnp.zeros((RET_DIM, RET_DIM), F32)
        acc3b = jnp.zeros((RET_DIM, RET_DIM), F32)
        state = jnp.zeros((RET_DIM, RET_DIM), F32)
        for t in range(nt - 1, -1, -1):
            gf_s[t] = state.astype(BF16)
            z = zmat(t, dec.q_f)
            acc3f = acc3f + hfa[t] * z
            state = step_f * state + z
        gctx_f = state.astype(BF16)
        state = jnp.zeros((RET_DIM, RET_DIM), F32)
        for t in range(nt):
            gb_s[t] = state.astype(BF16)
            z = zmat(t, dec.q_b)
            acc3b = acc3b + hba[t] * z
            state = step_b * state + z
        gctx_b = state.astype(BF16)

        rc = (lax.broadcasted_iota(jnp.int32, (c, c), 0) - lax.broadcasted_iota(jnp.int32, (c, c), 1)).astype(F32)
        w_diag = _ret_weights(lgf, lgb, rc)
        wg_f = jnp.where(rc >= 0.0, w_diag * rc, 0.0)
        wg_b = jnp.where(rc < 0.0, -w_diag * rc, 0.0)
        accf = jnp.zeros((SUBLANES, RET_DIM), F32)
        accb = jnp.zeros((SUBLANES, RET_DIM), F32)
        gdf = jnp.zeros((SUBLANES, c), F32)
        gdb = jnp.zeros((SUBLANES, c), F32)
        for t in range(nt):
            rows = lat(t)
            qt = qs[rows, :]
            kt = ks[rows, :]
            vt = vs[rows, :]
            dot = dos[t * c:(t + 1) * c, :]
            s = _dot_nt(qt, kt)
            dp = _dot_nt(dot, vt)
            dv = _dot_tn((s * w_diag).astype(BF16), dot)
            ds = (dp * w_diag).astype(BF16)
            dq = _dot(ds, kt)
            dk = _dot_tn(ds, qt)
            gs = dp * s
            gdf = gdf + fold(gs * wg_f)
            gdb = gdb + fold(gs * wg_b)
            qv = qf32[rows, :]
            kv = kf32[rows, :]
            dq_f = dec.q_f * _dot_nt(dot, hf[t])
            dq_b = dec.q_b * _dot_nt(dot, hb[t])
            dk_f = dec.k_f * _dot_nt(vt, gf_s[t])
            dk_b = dec.k_b * _dot_nt(vt, gb_s[t])
            accf = accf + fold(dec.head * dq_f * qv) + fold(dec.tail * dk_f * kv)
            accb = accb + fold(dec.tail * dq_b * qv) + fold(dec.head * dk_b * kv)
            dv = dv + dec.k_f * _dot(kt, gf_s[t]) + dec.k_b * _dot(kt, gb_s[t])
            cosb = cos_ref[rows, :]
            sinb = sin_ref[rows, :]
            d_ref[0, rows, :] = _unrope((dq + dq_f + dq_b) * scale, cosb, sinb).astype(BF16)
            d_ref[1, rows, :] = _unrope(dk + dk_f + dk_b, cosb, sinb).astype(BF16)
            d_ref[2, rows, :] = dv.astype(BF16)
        kc = ks[0:C, :]
        vc = vs[0:C, :]
        kcv = kf32[0:C, :]
        dkc_f = dec_c.k_f * _dot_nt(vc, gctx_f)
        dkc_b = dec_c.k_b * _dot_nt(vc, gctx_b)
        accf = accf + fold(dec_c.tail * dkc_f * kcv)
        accb = accb + fold(dec_c.head * dkc_b * kcv)
        d_ref[1, 0:C, :] = (dkc_f + dkc_b).astype(BF16)
        d_ref[2, 0:C, :] = (dec_c.k_f * _dot(kc, gctx_f) + dec_c.k_b * _dot(kc, gctx_b)).astype(BF16)
        gf = jnp.sum(gdf) + jnp.sum(accf) + jnp.sum(acc3f)
        gb = jnp.sum(gdb) + jnp.sum(accb) + jnp.sum(acc3b)
        row = lax.broadcasted_iota(jnp.int32, (SUBLANES, LANES), 0)
        dlg_ref[...] = jnp.where(row == 0, gf, jnp.where(row == 1, gb, 0.0))

    def col(seg):
        return pl.BlockSpec((None, T, RET_DIM), lambda b, h, seg=seg: (b, 0, seg * RET_HEADS + h))

    return _call_hosting(
        body, hosted, name="ret_bwd", grid=(B, RET_HEADS),
        out_shape=(jax.ShapeDtypeStruct((B, 4, T, RET_WIDTH), BF16),
                   jax.ShapeDtypeStruct((B, SUBLANES, RET_WIDTH), F32),
                   jax.ShapeDtypeStruct((B, RET_HEADS, SUBLANES, LANES), F32)),
        in_specs=[pl.BlockSpec(memory_space=pltpu.SMEM), col(0), col(1), col(2), col(3),
                  pl.BlockSpec((T, RET_DIM), lambda b, h: (0, 0)), pl.BlockSpec((T, RET_DIM), lambda b, h: (0, 0)),
                  pl.BlockSpec((1, RET_DIM), lambda b, h: (0, h)),
                  pl.BlockSpec((None, N, RET_DIM), lambda b, h: (b, 0, h)),
                  pl.BlockSpec((None, N, RET_DIM), lambda b, h: (b, 0, h))],
        out_specs=(pl.BlockSpec((None, 4, T, RET_DIM), lambda b, h: (b, 0, 0, h)),
                   pl.BlockSpec((None, SUBLANES, RET_DIM), lambda b, h: (b, 0, h)),
                   pl.BlockSpec((None, None, SUBLANES, LANES), lambda b, h: (b, h, 0, 0))),
        scratch_shapes=[pltpu.VMEM((T, RET_DIM), BF16)] * 3 + [pltpu.VMEM((N, RET_DIM), BF16)]
                       + [pltpu.VMEM((T, RET_DIM), F32)] * 2
                       + [pltpu.VMEM((nt, RET_DIM, RET_DIM), BF16)] * 2 + [pltpu.VMEM((nt, RET_DIM, RET_DIM), F32)] * 2
                       + [pltpu.VMEM((nt, RET_DIM, RET_DIM), BF16)] * 2,
        args=(lg, proj, proj, proj, proj, cos, sin, gn, o, dlat))


def _na_geometry(rows):
    kh = min(NA_KH, rows)
    return kh, kh * GRID_W


def _pair_select():
    lane = lax.broadcasted_iota(jnp.int32, (2 * GRID_W, LANES), 1)
    row = lax.broadcasted_iota(jnp.int32, (2 * GRID_W, LANES), 0)
    return (lane >= NA_DIM) == (row >= GRID_W)


def _pair_bias(bias_ref, dr0, kh):
    return jnp.concatenate(
        [jnp.concatenate([bias_ref[e, pl.ds(dr0 + 2 * m, 1)].reshape(GRID_W, LANES) for m in range(kh // 2)], axis=1)
         for e in range(2)], axis=0)


def _na_softmax(s_loc, s_ctx):
    mx = jnp.maximum(jnp.max(s_loc, axis=-1, keepdims=True), jnp.max(s_ctx, axis=-1, keepdims=True))
    p_loc = jnp.exp(s_loc - mx)
    p_ctx = jnp.exp(s_ctx - mx)
    den = jnp.sum(p_loc, axis=-1, keepdims=True) + jnp.sum(p_ctx, axis=-1, keepdims=True)
    return p_loc, p_ctx, den


def _na_fwd(proj, bias2, n_ctx, hosted):
    B, T, _ = proj.shape
    C = n_ctx
    N = T - C
    R = N // GRID_W
    kh, nk = _na_geometry(R)
    scale = NA_DIM ** -0.5
    base = (4 * RET_WIDTH) // LANES

    def body(q_ref, k_ref, v_ref, bias_ref, out_ref, kb16, vb16):
        kb16[...] = k_ref[...].astype(BF16)
        vb16[...] = v_ref[...].astype(BF16)
        kc = kb16[0:C, :]
        vc = vb16[0:C, :]
        lane = lax.broadcasted_iota(jnp.int32, (GRID_W, LANES), 1)
        sel2 = _pair_select()

        def group(gi, carry):
            pre = []
            for u in range(NA_GROUP):
                r = gi * NA_GROUP + u
                bs = jnp.clip(r - kh // 2, 0, R - kh)
                dr0 = bs - r + (NA_KH - 1)
                q = q_ref[pl.ds(pl.multiple_of(C + r * GRID_W, GRID_W), GRID_W), :] * scale
                q2 = jnp.where(sel2, jnp.concatenate([q, q], axis=0), 0.0).astype(BF16)
                band = pl.ds(pl.multiple_of(C + bs * GRID_W, GRID_W), nk)
                s_loc = _dot_nt(q2, kb16[band, :]) + _pair_bias(bias_ref, dr0, kh)
                s_ctx = _dot_nt(q2, kc)
                pre.append((r, band, s_loc, s_ctx))
            mid = [(r, band) + _na_softmax(s_loc, s_ctx) for r, band, s_loc, s_ctx in pre]
            for r, band, p_loc, p_ctx, den in mid:
                o2 = (_dot(p_loc.astype(BF16), vb16[band, :]) + _dot(p_ctx.astype(BF16), vc)) / den
                out_ref[pl.ds(pl.multiple_of(r * GRID_W, GRID_W), GRID_W), :] = jnp.where(
                    lane < NA_DIM, o2[:GRID_W], o2[GRID_W:]).astype(BF16)
            return carry

        lax.fori_loop(0, R // NA_GROUP, group, 0)

    def col(seg):
        return pl.BlockSpec((None, T, LANES), lambda b, p, seg=seg: (b, 0, base + seg * NA_PAIRS + p))

    return _call_hosting(
        body, hosted, name="na_fwd", grid=(B, NA_PAIRS),
        out_shape=(jax.ShapeDtypeStruct((B, N, NA_WIDTH), BF16),),
        in_specs=[col(0), col(1), col(2),
                  pl.BlockSpec((2, 2 * NA_KH - 2, GRID_W, LANES), lambda b, p: (p, 0, 0, 0))],
        out_specs=(pl.BlockSpec((None, N, LANES), lambda b, p: (b, 0, p)),),
        scratch_shapes=[pltpu.VMEM((T, LANES), BF16)] * 2,
        args=(proj, proj, proj, bias2))


def _na_bwd(proj, bias2, dlat, n_ctx, hosted):
    B, T, _ = proj.shape
    C = n_ctx
    N = T - C
    R = N // GRID_W
    kh, nk = _na_geometry(R)
    scale = NA_DIM ** -0.5
    base = (4 * RET_WIDTH) // LANES

    def body(q_ref, k_ref, v_ref, bias_ref, dl_ref, d_ref, db_ref, kb16, vb16, dkv):
        b = pl.program_id(1)
        kb16[...] = k_ref[...].astype(BF16)
        vb16[...] = v_ref[...].astype(BF16)
        kc = kb16[0:C, :]
        vc = vb16[0:C, :]
        lane = lax.broadcasted_iota(jnp.int32, (GRID_W, LANES), 1)
        dkv[...] = jnp.zeros(dkv.shape, F32)
        d_ref[0, 0:C, :] = jnp.zeros((C, LANES), BF16)

        @pl.when(b == 0)
        def _():
            db_ref[...] = jnp.zeros(db_ref.shape, F32)

        sel2 = _pair_select()

        def group(gi, carry):
            pre = []
            for u in range(NA_GROUP):
                r = gi * NA_GROUP + u
                bs = jnp.clip(r - kh // 2, 0, R - kh)
                dr0 = bs - r + (NA_KH - 1)
                q = q_ref[pl.ds(pl.multiple_of(C + r * GRID_W, GRID_W), GRID_W), :] * scale
                do = dl_ref[pl.ds(pl.multiple_of(r * GRID_W, GRID_W), GRID_W), :]
                q2 = jnp.where(sel2, jnp.concatenate([q, q], axis=0), 0.0).astype(BF16)
                do2 = jnp.where(sel2, jnp.concatenate([do, do], axis=0), 0.0).astype(BF16)
                band = pl.ds(pl.multiple_of(C + bs * GRID_W, GRID_W), nk)
                s_loc = _dot_nt(q2, kb16[band, :]) + _pair_bias(bias_ref, dr0, kh)
                s_ctx = _dot_nt(q2, kc)
                dp_loc = _dot_nt(do2, vb16[band, :])
                dp_ctx = _dot_nt(do2, vc)
                pre.append((r, dr0, band, q2, do2, s_loc, s_ctx, dp_loc, dp_ctx))
            mid = []
            for r, dr0, band, q2, do2, s_loc, s_ctx, dp_loc, dp_ctx in pre:
                p_loc, p_ctx, den = _na_softmax(s_loc, s_ctx)
                inv = 1.0 / den
                p_loc = p_loc * inv
                p_ctx = p_ctx * inv
                delta = (jnp.sum(p_loc * dp_loc, axis=-1, keepdims=True)
                         + jnp.sum(p_ctx * dp_ctx, axis=-1, keepdims=True))
                ds_loc = p_loc * (dp_loc - delta)
                ds_ctx = p_ctx * (dp_ctx - delta)
                mid.append((r, dr0, band, q2, do2, p_loc.astype(BF16), p_ctx.astype(BF16), ds_loc, ds_ctx))
            for r, dr0, band, q2, do2, pb_loc, pb_ctx, ds_loc, ds_ctx in mid:
                dsb_loc = ds_loc.astype(BF16)
                dsb_ctx = ds_ctx.astype(BF16)
                dq2 = _dot(dsb_loc, kb16[band, :]) + _dot(dsb_ctx, kc)
                d_ref[0, pl.ds(pl.multiple_of(C + r * GRID_W, GRID_W), GRID_W), :] = (jnp.where(
                    lane < NA_DIM, dq2[:GRID_W], dq2[GRID_W:]) * scale).astype(BF16)
                dkv[0, band, :] += _dot_tn(dsb_loc, q2)
                dkv[1, band, :] += _dot_tn(pb_loc, do2)
                dkv[0, 0:C, :] += _dot_tn(dsb_ctx, q2)
                dkv[1, 0:C, :] += _dot_tn(pb_ctx, do2)
                for e in range(2):
                    for m in range(kh // 2):
                        db_ref[e, pl.ds(dr0 + 2 * m, 1)] += ds_loc[e * GRID_W:(e + 1) * GRID_W,
                                                                   m * LANES:(m + 1) * LANES].reshape(1, GRID_W, LANES)
            return carry

        lax.fori_loop(0, R // NA_GROUP, group, 0)
        d_ref[1] = dkv[0].astype(BF16)
        d_ref[2] = dkv[1].astype(BF16)

    def col(seg):
        return pl.BlockSpec((None, T, LANES), lambda p, b, seg=seg: (b, 0, base + seg * NA_PAIRS + p))

    return _call_hosting(
        body, hosted, name="na_bwd", grid=(NA_PAIRS, B),
        out_shape=(jax.ShapeDtypeStruct((B, 3, T, NA_WIDTH), BF16),
                   jax.ShapeDtypeStruct((NA_HEADS, 2 * NA_KH - 2, GRID_W, LANES), F32)),
        in_specs=[col(0), col(1), col(2),
                  pl.BlockSpec((2, 2 * NA_KH - 2, GRID_W, LANES), lambda p, b: (p, 0, 0, 0)),
                  pl.BlockSpec((None, N, LANES), lambda p, b: (b, 0, p))],
        out_specs=(pl.BlockSpec((None, 3, T, LANES), lambda p, b: (b, 0, 0, p)),
                   pl.BlockSpec((2, 2 * NA_KH - 2, GRID_W, LANES), lambda p, b: (p, 0, 0, 0))),
        scratch_shapes=[pltpu.VMEM((T, LANES), BF16)] * 2 + [pltpu.VMEM((2, T, LANES), F32)],
        args=(proj, proj, proj, bias2, dlat))


def _split3(a):
    hi = a.astype(BF16)
    r1 = a - hi.astype(F32)
    mid = r1.astype(BF16)
    lo = (r1 - mid.astype(F32)).astype(BF16)
    return hi, mid, lo


def _rpb_reduce(dbias2, onehot2):
    rows = dbias2.shape[0] * dbias2.shape[1]
    flat = dbias2.reshape(rows, GRID_W * LANES)

    def body(a_ref, oh_ref, o_ref):
        hi, mid, lo = _split3(a_ref[...])
        oh = oh_ref[...]
        o_ref[...] = _dot(hi, oh) + _dot(mid, oh) + _dot(lo, oh)

    return pl.pallas_call(
        body, name="rpb_reduce", out_shape=jax.ShapeDtypeStruct((rows, LANES), F32),
        in_specs=[_vmem(), _vmem()], out_specs=_vmem(),
        compiler_params=pltpu.CompilerParams(vmem_limit_bytes=VMEM_LIMIT),
    )(flat, onehot2)


def _dense_core(lat_ret, lat_na, x, tgt, modl, g_post_mix, g_pre_mlp, g_post_mlp, w_out, w1, w2):
    B, N, D = x.shape
    F = w1.shape[1]
    w2_rows = w2.shape[0] // N_DEV
    mixw = w_out.shape[0]
    half = mixw // 2
    tm = _div_tile(N, 256, 16)
    nt = N // tm
    fc = _div_tile(F, 1024, LANES)

    def body(lr_ref, ln_ref, x_ref, t_ref, gt1_ref, sh2_ref, sc2_ref, gt2_ref, gpm_ref, gpre_ref, gpo_ref,
             wout_hbm, w1_hbm, w2_part,
             dy1_ref, dlr_ref, dln_ref, dmix_ref, h2_ref, a_ref, du_ref, dz_ref, red_ref, w2_hbm,
             wout_v, w1_v, w2_v, u_s, sems, fsend, frecv):
        @pl.when((pl.program_id(0) == 0) & (pl.program_id(1) == 0))
        def _():
            relay = [(_row_block(w2_part, w2_rows), _row_block(w2_hbm, w2_rows))]
            _forward_start(relay, fsend, frecv)
            cps = [pltpu.make_async_copy(wout_hbm, wout_v, sems.at[0]),
                   pltpu.make_async_copy(w1_hbm, w1_v, sems.at[1])]
            for cp in cps:
                cp.start()
            _forward_wait(relay, fsend, frecv)
            cps.append(pltpu.make_async_copy(w2_hbm, w2_v, sems.at[2]))
            cps[2].start()
            for cp in cps:
                cp.wait()

        @pl.when(pl.program_id(1) == 0)
        def _():
            red_ref[...] = jnp.zeros(red_ref.shape, F32)

        gt1 = gt1_ref[...]
        sh2 = sh2_ref[...]
        sc2 = sc2_ref[...]
        gt2 = gt2_ref[...]
        gpm = gpm_ref[...]
        gpre = gpre_ref[...]
        gpo = gpo_ref[...]

        def rowmean(a):
            return jnp.mean(a, axis=-1, keepdims=True)

        def colsum(a):
            return jnp.sum(a, axis=0, keepdims=True)

        mix = _dot(lr_ref[...], wout_v[0:half, :]) + _dot(ln_ref[...], wout_v[half:, :])
        x = x_ref[...]
        rm = lax.rsqrt(rowmean(mix * mix) + NORM_EPS)
        mh = mix * rm
        nm = mh * gpm
        y1 = x + gt1 * nm
        r1 = lax.rsqrt(rowmean(y1 * y1) + NORM_EPS)
        xh = y1 * r1
        n1 = xh * gpre
        h2b = (n1 * (1.0 + sc2) + sh2).astype(BF16)
        h2_ref[...] = h2b
        z = jnp.zeros((tm, D), F32)
        for c0 in range(0, F, fc):
            u = _dot(h2b, w1_v[:, c0:c0 + fc])
            u_s[:, c0:c0 + fc] = u
            ru = jnp.maximum(u, 0.0)
            ab = (ru * ru).astype(BF16)
            a_ref[:, c0:c0 + fc] = ab
            z = z + _dot(ab, w2_v[c0:c0 + fc, :])
        r2 = lax.rsqrt(rowmean(z * z) + NORM_EPS)
        zh = z * r2
        n2 = zh * gpo
        y2 = y1 + gt2 * n2
        err = y2 - t_ref[...]
        loss = 0.5 * jnp.sum(rowmean(err * err))
        dy2 = err * (1.0 / D)
        red_ref[2:3, :] += colsum(dy2 * n2)
        dn2 = dy2 * gt2
        red_ref[6:7, :] += colsum(dn2 * zh)
        dzh = dn2 * gpo
        dz = r2 * (dzh - zh * rowmean(dzh * zh))
        dzb = dz.astype(BF16)
        dz_ref[...] = dzb
        dh2 = jnp.zeros((tm, D), F32)
        for c0 in range(0, F, fc):
            da = _dot_nt(dzb, w2_v[c0:c0 + fc, :])
            dub = (da * (2.0 * jnp.maximum(u_s[:, c0:c0 + fc], 0.0))).astype(BF16)
            du_ref[:, c0:c0 + fc] = dub
            dh2 = dh2 + _dot_nt(dub, w1_v[:, c0:c0 + fc])
        red_ref[3:4, :] += colsum(dh2 * n1)
        red_ref[4:5, :] += colsum(dh2)
        dn1 = dh2 * (1.0 + sc2)
        red_ref[5:6, :] += colsum(dn1 * xh)
        dxh = dn1 * gpre
        dy1 = dy2 + r1 * (dxh - xh * rowmean(dxh * xh))
        dy1_ref[...] = dy1
        red_ref[0:1, :] += colsum(dy1 * nm)
        dnm = dy1 * gt1
        red_ref[1:2, :] += colsum(dnm * mh)
        dmh = dnm * gpm
        dmix = (rm * (dmh - mh * rowmean(dmh * mh))).astype(BF16)
        dmix_ref[...] = dmix
        dlr_ref[...] = _dot_nt(dmix, wout_v[0:half, :])
        dln_ref[...] = _dot_nt(dmix, wout_v[half:, :])
        red_ref[7:8, :] += jnp.zeros((1, D), F32) + loss

    def tok(w):
        return pl.BlockSpec((None, tm, w), lambda b, t: (b, t, 0))

    def mod(k):
        return pl.BlockSpec((None, None, 1, D), lambda b, t, k=k: (b, k, 0, 0))

    def vec():
        return pl.BlockSpec((1, D), lambda b, t: (0, 0))

    return pl.pallas_call(
        body, name="dense_core", grid=(B, nt),
        out_shape=(jax.ShapeDtypeStruct((B, N, D), F32), jax.ShapeDtypeStruct((B, N, half), F32),
                   jax.ShapeDtypeStruct((B, N, half), F32), jax.ShapeDtypeStruct((B, N, D), BF16),
                   jax.ShapeDtypeStruct((B, N, D), BF16), jax.ShapeDtypeStruct((B, N, F), BF16),
                   jax.ShapeDtypeStruct((B, N, F), BF16), jax.ShapeDtypeStruct((B, N, D), BF16),
                   jax.ShapeDtypeStruct((B, SUBLANES, D), F32),
                   jax.ShapeDtypeStruct(w2.shape, w2.dtype)),
        in_specs=[tok(half), tok(half), tok(D), tok(D), mod(2), mod(3), mod(4), mod(5), vec(), vec(), vec(),
                  _any(), _any(), _any()],
        out_specs=(tok(D), tok(half), tok(half), tok(D), tok(D), tok(F), tok(F), tok(D),
                   pl.BlockSpec((None, SUBLANES, D), lambda b, t: (b, 0, 0)), _any()),
        scratch_shapes=[pltpu.VMEM((mixw, D), BF16), pltpu.VMEM((D, F), BF16), pltpu.VMEM((F, D), BF16),
                        pltpu.VMEM((tm, F), F32), pltpu.SemaphoreType.DMA((3,)),
                        pltpu.SemaphoreType.DMA((1, 3)), pltpu.SemaphoreType.DMA((1, 3))],
        input_output_aliases={13: 9},
        compiler_params=_params("arbitrary", "arbitrary"),
    )(lat_ret, lat_na, x, tgt, modl, modl, modl, modl, g_post_mix, g_pre_mlp, g_post_mlp, w_out, w1, w2)[:9]


def _inproj_bwd(dret, dna, x, ctx, dy1, modl, g1, w_in_t, hosted):
    B, N, D = x.shape
    n_ctx = ctx.shape[1]
    T = n_ctx + N
    tm = _div_tile(n_ctx, 256, 16)
    nct, ctx_spec, lat_spec = _token_tiles(n_ctx, tm)
    nt = T // tm
    nseg_r = dret.shape[1]
    nseg_n = dna.shape[1]
    nw = w_in_t.shape[0]

    def body(*refs):
        seg_refs = refs[:nseg_r + nseg_n]
        c_ref, x_ref, dy1_ref, sc_ref, g_ref, w_ref, dx_ref, red_ref = refs[nseg_r + nseg_n:]
        t = pl.program_id(1)
        dh = jnp.zeros((tm, D), F32)
        for s, ref in enumerate(seg_refs):
            dh = dh + _dot(ref[...], w_ref[s * SEG:(s + 1) * SEG, :])
        x = jnp.where(t < nct, c_ref[...], x_ref[...])
        g = g_ref[...]
        r = lax.rsqrt(jnp.mean(x * x, axis=-1, keepdims=True) + NORM_EPS)
        xh = x * r
        @pl.when((t == 0) | (t == nct))
        def _():
            red_ref[...] = jnp.zeros(red_ref.shape, F32)

        red_ref[0:1, :] += jnp.sum(dh, axis=0, keepdims=True)
        red_ref[1:2, :] += jnp.sum(dh * (xh * g), axis=0, keepdims=True)
        dn = dh * (1.0 + sc_ref[...])
        red_ref[2:3, :] += jnp.sum(dn * xh, axis=0, keepdims=True)
        dxh = dn * g
        dx = r * (dxh - xh * jnp.mean(dxh * xh, axis=-1, keepdims=True))
        dx_ref[...] = dx + jnp.where(t >= nct, dy1_ref[...], 0.0)

    def mrow(b, t):
        return jnp.where(t < nct, B, b)

    def seg(s):
        return pl.BlockSpec((None, None, tm, SEG), lambda b, t, s=s: (b, s, t, 0))

    return _call_hosting(
        body, hosted, name="inproj_bwd", grid=(B, nt),
        out_shape=(jax.ShapeDtypeStruct((B, N, D), F32), jax.ShapeDtypeStruct((B, 2, SUBLANES, D), F32)),
        in_specs=[seg(s) for s in range(nseg_r)] + [seg(s) for s in range(nseg_n)]
                 + [ctx_spec(D), lat_spec(D), lat_spec(D),
                    pl.BlockSpec((None, None, 1, D), lambda b, t: (mrow(b, t), 1, 0, 0)),
                    pl.BlockSpec((1, D), lambda b, t: (0, 0)),
                    pl.BlockSpec((nw, D), lambda b, t: (0, 0))],
        out_specs=(lat_spec(D),
                   pl.BlockSpec((None, None, SUBLANES, D), lambda b, t: (b, jnp.where(t < nct, 0, 1), 0, 0))),
        scratch_shapes=[], args=(*([dret] * nseg_r), *([dna] * nseg_n), ctx, x, dy1, modl, g1, w_in_t))


def _tn_matmul(lhs, rhs, name, rows_before=0, rows_after=0, into=None):
    B, S, T, W = lhs.shape
    nn = rhs.shape[-1]
    tk = _div_tile(T, 2304, LANES)
    bm = _div_tile(W, 1024, LANES)
    bn = _div_tile(nn, 1024, LANES)
    nkt = T // tk
    nk = B * nkt

    def body(l_ref, r_ref, *rest):
        o_ref, acc = rest[-2:]
        k = pl.program_id(3)

        @pl.when(k == 0)
        def _():
            acc[...] = jnp.zeros(acc.shape, F32)

        acc[...] += _dot_tn(l_ref[...].astype(BF16), r_ref[...].astype(BF16))

        @pl.when(k == nk - 1)
        def _():
            o_ref[...] = acc[...].astype(BF16)

    nwb = W // bm
    first = rows_before // bm
    return pl.pallas_call(
        functools.partial(body), name=name, grid=(S, nwb, nn // bn, nk),
        out_shape=jax.ShapeDtypeStruct((rows_before + S * W + rows_after, nn), BF16),
        in_specs=[pl.BlockSpec((None, None, tk, bm), lambda s, i, j, k: (k // nkt, s, k % nkt, i)),
                  pl.BlockSpec((None, tk, bn), lambda s, i, j, k: (k // nkt, k % nkt, j))]
                 + ([] if into is None else [_any()]),
        out_specs=pl.BlockSpec((bm, bn), lambda s, i, j, k: (first + s * nwb + i, j)),
        scratch_shapes=[pltpu.VMEM((bm, bn), F32)],
        input_output_aliases={} if into is None else {2: 0},
        compiler_params=_params("parallel", "parallel", "parallel", "arbitrary"),
    )(lhs, rhs, *([] if into is None else [into]))


class _SplitScatter:
    def __init__(self, gs, block_ofs, block_shapes, name):
        self.n = n = len(gs)
        self.block_ofs = block_ofs
        land_shapes = [(N_DEV,) + tuple(bs) for bs in block_shapes]
        hbm = pl.BlockSpec(memory_space=pltpu.HBM)
        sem = pl.BlockSpec(memory_space=pltpu.SEMAPHORE)

        def body(*refs):
            g_refs, land_refs = refs[:n], refs[n:2 * n]
            send_sems, recv_sems, own_sems = refs[2 * n:2 * n + 3]
            token = refs[-1]
            for own, pushes in self._copies(g_refs, land_refs, send_sems, recv_sems, own_sems, landing="sender"):
                own.start()
                for cp in pushes:
                    cp.start()
            token[...] = jnp.zeros_like(token)

        outs = pl.pallas_call(
            body, name=name,
            out_shape=(pltpu.SemaphoreType.DMA((n * (N_DEV - 1),)), pltpu.SemaphoreType.DMA((n * (N_DEV - 1),)),
                       pltpu.SemaphoreType.DMA((n,)))
                      + tuple(pltpu.HBM(g.shape, g.dtype) for g in gs)
                      + tuple(pltpu.HBM(s, g.dtype) for s, g in zip(land_shapes, gs))
                      + (jax.ShapeDtypeStruct((SUBLANES, LANES), F32),),
            in_specs=(hbm,) * (2 * n), out_specs=(sem,) * 3 + (hbm,) * (2 * n) + (_vmem(),),
            input_output_aliases={k: 3 + k for k in range(2 * n)},
            compiler_params=pltpu.CompilerParams(has_side_effects=pltpu.SideEffectType.DATAFLOW_SIDE_EFFECTING),
        )(*[pltpu.with_memory_space_constraint(g, pltpu.HBM) for g in gs],
          *[pltpu.with_memory_space_constraint(lax.empty(s, g.dtype), pltpu.HBM) for s, g in zip(land_shapes, gs)])
        self.sems, self.thru, self.token = outs[:3], outs[3:3 + 2 * n], outs[-1]

    def _copies(self, g_refs, land_refs, send_sems, recv_sems, own_sems, landing):
        me, peers = _me_and_peers()
        out = []
        for k in range(self.n):
            src = self.block_ofs[k](g_refs[k])
            own = pltpu.make_async_copy(src(me), land_refs[k].at[me], own_sems.at[k])
            pushes = [_remote(src(pid), land_refs[k].at[me if landing == "sender" else pid],
                              send_sems.at[k * (N_DEV - 1) + i], recv_sems.at[k * (N_DEV - 1) + i], dev)
                      for i, (dev, pid) in enumerate(peers)]
            out.append((own, pushes))
        return out


def _scatter_wait(scatters, after, name):
    hbm = pl.BlockSpec(memory_space=pltpu.HBM)
    sem = pl.BlockSpec(memory_space=pltpu.SEMAPHORE)
    n_arr = [2 * sc.n for sc in scatters]
    total = sum(n_arr)

    def body(*refs):
        arrs, sems = refs[:total], refs[total:total + 3 * len(scatters)]
        a0 = 0
        for j, sc in enumerate(scatters):
            g_refs, land_refs = arrs[a0:a0 + sc.n], arrs[a0 + sc.n:a0 + 2 * sc.n]
            a0 += 2 * sc.n
            send_sems, recv_sems, own_sems = sems[3 * j:3 * j + 3]
            for (own, sent), (_, got) in zip(sc._copies(g_refs, land_refs, send_sems, recv_sems, own_sems, "sender"),
                                             sc._copies(g_refs, land_refs, send_sems, recv_sems, own_sems, "receiver")):
                own.wait()
                for cp in sent:
                    cp.wait_send()
                for cp in got:
                    cp.wait_recv()

    operands = [a for sc in scatters for a in sc.thru]
    outs = pl.pallas_call(
        body, name=name,
        out_shape=tuple(pltpu.HBM(a.shape, a.dtype) for a in operands),
        in_specs=(hbm,) * total + (sem,) * (3 * len(scatters)) + (pl.BlockSpec(memory_space=pl.ANY),),
        out_specs=(hbm,) * total, input_output_aliases={k: k for k in range(total)},
        compiler_params=pltpu.CompilerParams(has_side_effects=pltpu.SideEffectType.DATAFLOW_SIDE_EFFECTING),
    )(*operands, *[s for sc in scatters for s in sc.sems], after)
    lands, a0 = [], 0
    for sc in scatters:
        lands.extend(outs[a0 + sc.n:a0 + 2 * sc.n])
        a0 += 2 * sc.n
    return lands


def _sum_slots(buf, name):
    _, rows, cols = buf.shape
    tr = _div_tile(rows, 256, 2 * SUBLANES)

    def body(b_ref, o_ref):
        acc = b_ref[0].astype(F32)
        for k in range(1, N_DEV):
            acc = acc + b_ref[k].astype(F32)
        o_ref[...] = acc

    return pl.pallas_call(
        functools.partial(body), name=name, grid=(rows // tr,),
        out_shape=jax.ShapeDtypeStruct((rows, cols), F32),
        in_specs=[pl.BlockSpec((N_DEV, tr, cols), lambda i: (0, i, 0))],
        out_specs=pl.BlockSpec((tr, cols), lambda i: (i, 0)),
        compiler_params=_params("parallel"),
    )(buf)


def _small_ar(vec, dmods, silu_all, w_ada, c_ctx):
    rv = vec.shape[0]
    D = silu_all.shape[1]
    ncol = w_ada.shape[1]
    nm = dmods.shape[1]
    srows = silu_all.shape[0]

    def body(vec_ref, dm_ref, s_ref, w_ref, cc_ref, tot_ref, gb_ref, gw_ref, gc_ref,
             vbuf, mbuf, tbuf, dmx, send1, recv1, send3, recv3):
        me, _ = _me_and_peers()
        vbuf[me] = vec_ref[...]
        mbuf[me] = dm_ref[...]
        both = [(lambda p: vbuf.at[me], lambda p: vbuf.at[p]), (lambda p: mbuf.at[me], lambda p: mbuf.at[p])]
        _push_start(both, ALL_PEERS, send1, recv1)
        _push_wait_recv(both, ALL_PEERS, send1, recv1)
        _push_wait_send(both, ALL_PEERS, send1, recv1)
        tot = vbuf[0]
        msum = mbuf[0]
        for k in range(1, N_DEV):
            tot = tot + vbuf[k]
            msum = msum + mbuf[k]
        tot_ref[...] = tot
        gb_ref[...] = jnp.sum(msum, axis=0, keepdims=True)
        loc = pl.ds(pl.multiple_of(me * ncol, ncol), ncol)
        for k in range(N_DEV):
            dmx[k * SUBLANES:(k + 1) * SUBLANES, :] = mbuf[k, :, loc]
        cm = msum[2:3, :]
        mbuf[0, 2:3, :] = cm
        cm_loc = mbuf[0, 2:3, loc]
        dmx[N_DEV * SUBLANES:, :] = jnp.concatenate([cm_loc, jnp.zeros((SUBLANES - 1, ncol), F32)], axis=0)
        gw_ref[...] = _dot_tn(s_ref[...], dmx[...])
        tbuf[me] = _dot_nt(dmx[N_DEV * SUBLANES:, :], w_ref[...])
        _exchange(lambda p: tbuf.at[me], lambda p: tbuf.at[p], send3, recv3)
        tsum = tbuf[0]
        for k in range(1, N_DEV):
            tsum = tsum + tbuf[k]
        cc = cc_ref[...]
        sg = _sigmoid(cc)
        gc_ref[...] = tsum[0:1, :] * (sg * (1.0 + cc * (1.0 - sg)))

    return pl.pallas_call(
        body, name="small_ar",
        out_shape=(jax.ShapeDtypeStruct((rv, LANES), F32), jax.ShapeDtypeStruct((1, nm), F32),
                   jax.ShapeDtypeStruct((D, ncol), F32), jax.ShapeDtypeStruct((1, D), F32)),
        in_specs=[_vmem()] * 5, out_specs=(_vmem(),) * 4,
        scratch_shapes=[pltpu.VMEM((N_DEV, rv, LANES), F32), pltpu.VMEM((N_DEV, SUBLANES, nm), F32),
                        pltpu.VMEM((N_DEV, SUBLANES, D), F32), pltpu.VMEM((srows, ncol), F32)]
                       + [pltpu.SemaphoreType.DMA((2, N_DEV - 1))] * 2 + [pltpu.SemaphoreType.DMA((N_DEV - 1,))] * 2,
        compiler_params=pltpu.CompilerParams(vmem_limit_bytes=VMEM_LIMIT),
    )(vec, dmods, silu_all, w_ada, c_ctx.reshape(1, D))


def _adam_update(w, g, m, v):
    mn = ADAM_B1 * m + (1.0 - ADAM_B1) * g
    vn = ADAM_B2 * v + (1.0 - ADAM_B2) * (g * g)
    m_hat = mn / (1.0 - ADAM_B1 ** ADAM_STEP)
    v_hat = vn / (1.0 - ADAM_B2 ** ADAM_STEP)
    return -ADAM_LR * (m_hat / (jnp.sqrt(v_hat) + ADAM_EPS) + ADAM_WD * w), mn, vn


def _adamw(w, g, m, v, name):
    rows, cols = w.shape
    tr = _div_tile(rows, 256, SUBLANES) if rows * cols > 65536 else rows

    def body(w_ref, g_ref, m_ref, v_ref, d_ref, nm_ref, nv_ref):
        d_ref[...], nm_ref[...], nv_ref[...] = _adam_update(w_ref[...], g_ref[...], m_ref[...], v_ref[...])

    spec = pl.BlockSpec((tr, cols), lambda i: (i, 0))
    return pl.pallas_call(
        functools.partial(body), name=name, grid=(rows // tr,),
        out_shape=(jax.ShapeDtypeStruct((rows, cols), F32),) * 3,
        in_specs=[spec] * 4, out_specs=(spec,) * 3,
        compiler_params=_params("parallel"),
    )(w, g, m, v)


def _adamw_small(items, name):
    n = len(items)

    def body(*refs):
        ins, outs = refs[:4 * n], refs[4 * n:]
        for i in range(n):
            w_ref, g_ref, m_ref, v_ref = ins[4 * i:4 * i + 4]
            outs[3 * i][...], outs[3 * i + 1][...], outs[3 * i + 2][...] = _adam_update(
                w_ref[...], g_ref[...], m_ref[...], v_ref[...])

    outs = pl.pallas_call(
        body, name=name,
        out_shape=tuple(jax.ShapeDtypeStruct(it[0].shape, F32) for it in items for _ in range(3)),
        in_specs=[_vmem()] * (4 * n), out_specs=(_vmem(),) * (3 * n),
        compiler_params=pltpu.CompilerParams(vmem_limit_bytes=VMEM_LIMIT),
    )(*[a for it in items for a in it])
    return [tuple(outs[3 * i:3 * i + 3]) for i in range(n)]


def _sum_adamw(buf, w, m, v, name):
    _, rows, cols = buf.shape
    tr = _div_tile(rows, 256, 2 * SUBLANES)

    def body(b_ref, w_ref, m_ref, v_ref, g_ref, d_ref, nm_ref, nv_ref):
        g = b_ref[0].astype(F32)
        for k in range(1, N_DEV):
            g = g + b_ref[k].astype(F32)
        g_ref[...] = g
        d_ref[...], nm_ref[...], nv_ref[...] = _adam_update(w_ref[...], g, m_ref[...], v_ref[...])

    spec = pl.BlockSpec((tr, cols), lambda i: (i, 0))
    return pl.pallas_call(
        functools.partial(body), name=name, grid=(rows // tr,),
        out_shape=(jax.ShapeDtypeStruct((rows, cols), F32),) * 4,
        in_specs=[pl.BlockSpec((N_DEV, tr, cols), lambda i: (0, i, 0))] + [spec] * 3, out_specs=(spec,) * 4,
        compiler_params=_params("parallel"),
    )(buf, w, m, v)


def _rope_tables(n_ctx, n):
    n_freq = RET_DIM // 4
    inv = np.float32(ROPE_BASE) ** (-np.arange(n_freq, dtype=np.float32) / np.float32(n_freq))
    tok = np.arange(n)
    pos_r = (tok // GRID_W).astype(np.float32)
    pos_c = (tok % GRID_W).astype(np.float32)
    ang_r = (pos_r[:, None] * inv[None, :]).astype(np.float32)
    ang_c = (pos_c[:, None] * inv[None, :]).astype(np.float32)
    cos = np.concatenate([np.cos(ang_r), np.cos(ang_r), np.cos(ang_c), np.cos(ang_c)], axis=-1)
    sin = np.concatenate([-np.sin(ang_r), np.sin(ang_r), -np.sin(ang_c), np.sin(ang_c)], axis=-1)
    cos = np.concatenate([np.ones((n_ctx, RET_DIM), np.float32), cos], axis=0)
    sin = np.concatenate([np.zeros((n_ctx, RET_DIM), np.float32), sin], axis=0)
    return jnp.asarray(cos, F32), jnp.asarray(sin, F32)


def _na_tables():
    q = np.arange(GRID_W)[:, None]
    k = np.arange(GRID_W)[None, :]
    start = np.clip(q - NA_KW // 2, 0, GRID_W - NA_KW)
    valid = (k >= start) & (k < start + NA_KW)
    dc = np.clip(k - q + (NA_KW - 1), 0, 2 * NA_KW - 2)
    ncls = 2 * NA_KW - 1
    onehot = (dc[None] == np.arange(ncls)[:, None, None]) & valid[None]
    oh2 = np.zeros((GRID_W, LANES, LANES), np.float32)
    for c in range(ncls):
        oh2[:, :GRID_W, c] = onehot[c]
        oh2[:, GRID_W:, 32 + c] = onehot[c]
    return onehot.astype(np.float32), valid, oh2.reshape(GRID_W * LANES, LANES)


def _paired_bias(rpb, onehot, valid):
    t = jnp.einsum("hdc,cqk->hdqk", rpb, jnp.asarray(onehot), precision=lax.Precision.HIGHEST)
    t = jnp.where(jnp.asarray(valid)[None, None], t, NEG_INF)
    return jnp.concatenate([t[:, :-1], t[:, 1:]], axis=-1)


def kernel(x, c, ctx, c_ctx, w_ada, b_ada, g_pre_mix, g_post_mix, g_pre_mlp, g_post_mlp, w_in, ret_decay, ret_gn, na_rpb, w_out, w_mlp1, w_mlp2, loss_target, m_c_ctx, m_w_ada, m_b_ada, m_g_pre_mix, m_g_post_mix, m_g_pre_mlp, m_g_post_mlp, m_w_in, m_ret_decay, m_ret_gn, m_na_rpb, m_w_out, m_w_mlp1, m_w_mlp2, v_c_ctx, v_w_ada, v_b_ada, v_g_pre_mix, v_g_post_mix, v_g_pre_mlp, v_g_post_mlp, v_w_in, v_ret_decay, v_ret_gn, v_na_rpb, v_w_out, v_w_mlp1, v_w_mlp2):
    B, N, D = x.shape
    C = ctx.shape[1]
    T = C + N

    silu_all, mods_g, win_b, wout_l, w1_l, w2_l = _mod_gather(c, c_ctx, w_ada[0], b_ada, w_in[0].T, w_out[0],
                                                             w_mlp1[0], w_mlp2[0])
    mods_mine = mods_g.transpose(1, 0, 2).reshape(mods_g.shape[1], N_MOD * D)
    modl = jnp.concatenate([mods_mine[:B], mods_mine[SUBLANES:SUBLANES + 1]], axis=0)
    modl = modl.reshape(B + 1, N_MOD, 1, D)
    rin = w_in.shape[2]
    rout, c1, r2 = wout_l.shape[0], w1_l.shape[1], w2_l.shape[0]

    def rows_of(n):
        return lambda ref: _row_block(ref, n)

    def cols_of(n):
        return lambda ref: _col_block(ref, n)

    cos, sin = _rope_tables(C, N)
    onehot, valid, oh2 = _na_tables()
    bias2 = _paired_bias(na_rpb[0], onehot, valid)
    lg = jax.nn.log_sigmoid(ret_decay[0].astype(F32))

    level_one = SIBLING + ICI_SAME_CORE
    h_all, proj, w1_part = _inproj_fwd(
        x, ctx, modl, g_pre_mix, win_b,
        [_Hosted("gather", level_one, [w1_l], [cols_of(c1)], [jax.ShapeDtypeStruct((D, N_DEV * c1), BF16)], True)])
    o_ret, lat_ret, wout_b = _ret_fwd(
        proj, cos, sin, lg, ret_gn, C,
        [_Hosted("gather", ALL_PEERS, [wout_l], [rows_of(rout)], [jax.ShapeDtypeStruct((N_DEV * rout, D), BF16)], True)])
    lat_na, w1_b, w2_part = _na_fwd(
        proj, bias2, C,
        [_HostedRelay([w1_part], [cols_of(c1)]),
         _Hosted("gather", level_one, [w2_l], [rows_of(r2)], [jax.ShapeDtypeStruct((N_DEV * r2, D), BF16)], True)])

    (dy1, dlat_ret, dlat_na, dmix, h2, act, du, dz, red_d) = _dense_core(
        lat_ret, lat_na, x, loss_target, modl, g_post_mix, g_pre_mlp, g_post_mlp, wout_b, w1_b, w2_part)

    gw_out_p = _tn_matmul(lat_ret[:, None], dmix, "gw_out_ret", rows_after=lat_na.shape[-1])
    gw_out_p = _tn_matmul(lat_na[:, None], dmix, "gw_out_na", rows_before=lat_ret.shape[-1], into=gw_out_p)
    gw1_p = _tn_matmul(h2[:, None], du, "gw_mlp1")
    gw2_p = _tn_matmul(act[:, None], dz, "gw_mlp2")
    rs_mlp = _SplitScatter([gw_out_p, gw1_p, gw2_p], [rows_of(rout), cols_of(c1), rows_of(r2)],
                           [(rout, D), (D, c1), (r2, D)], "rs_mlp_start")

    dret, dgn_p, dlg_p = _ret_bwd(proj, cos, sin, lg, ret_gn + rs_mlp.token[0, 0], o_ret, dlat_ret, C, [])
    dna, dbias2 = _na_bwd(proj, bias2, dlat_na, C, [])
    ret_cols, na_cols = dret.shape[1] * dret.shape[3], dna.shape[1] * dna.shape[3]
    gwin_t_p = _tn_matmul(dret, h_all, "gw_in_ret", rows_after=na_cols)
    gwin_t_p = _tn_matmul(dna, h_all, "gw_in_na", rows_before=ret_cols, into=gwin_t_p)
    rs_in = _SplitScatter([gwin_t_p], [rows_of(rin)], [(rin, D)], "rs_w_in_start")
    grad_x, red_i = _inproj_bwd(dret, dna, x, ctx, dy1, modl, g_pre_mix + rs_in.token[0, 0], win_b, [])

    rd = red_d
    ri_ctx = red_i[:, 0].sum(axis=0)
    ri_lat = red_i[:, 1]
    d_mods = jnp.concatenate([ri_lat[:, 0], ri_lat[:, 1], rd[:, 0], rd[:, 4], rd[:, 3], rd[:, 2]], axis=-1)
    d_cmods = jnp.concatenate([ri_ctx[0], ri_ctx[1], jnp.zeros(((N_MOD - 2) * D,), F32)])[None]
    dm_slot = jnp.concatenate([d_mods, d_cmods, jnp.zeros((SUBLANES - B - 1, N_MOD * D), F32)], axis=0)
    dg_pre_mix = ri_lat[:, 2].sum(axis=0) + ri_ctx[2]
    dg_post_mix = rd[:, 1].sum(axis=0)
    dg_pre_mlp = rd[:, 5].sum(axis=0)
    dg_post_mlp = rd[:, 6].sum(axis=0)
    loss_p = rd[:, 7, 0].sum()
    d_gn = dgn_p[:, 0].sum(axis=0)
    d_lg = dlg_p[:, :, :2, 0].sum(axis=0).T
    d_decay = d_lg * jax.nn.sigmoid(-ret_decay[0].astype(F32))
    rr = _rpb_reduce(dbias2, jnp.asarray(oh2, BF16)).reshape(NA_HEADS, 2 * NA_KH - 2, LANES)
    ncls = 2 * NA_KW - 1
    d_rpb = (jnp.pad(rr[:, :, :ncls], ((0, 0), (0, 1), (0, 0))) + jnp.pad(rr[:, :, 32:32 + ncls], ((0, 0), (1, 0), (0, 0))))
    d_rpb32 = jnp.pad(d_rpb, ((0, 0), (0, 0), (0, 32 - ncls)))
    pieces = [dg_pre_mix, dg_post_mix, dg_pre_mlp, dg_post_mlp, d_gn, d_rpb32.reshape(-1),
              jnp.pad(d_decay.reshape(-1), (0, LANES - d_decay.size)), jnp.full((LANES,), loss_p, F32)]
    vec = jnp.concatenate(pieces)
    pad = (-vec.shape[0]) % (SUBLANES * LANES)
    vec = jnp.pad(vec, (0, pad)).reshape(-1, LANES)
    tot, g_b_ada, g_w_ada, g_c_ctx = _small_ar(vec, dm_slot, silu_all, w_ada[0], c_ctx)
    land_out, land_1, land_2, land_in = _scatter_wait([rs_mlp, rs_in], tot, "rs_wait")
    g_w_in = _sum_slots(land_in, "sum_w_in").T
    fused = {"w_out": _sum_adamw(land_out, w_out[0], m_w_out[0], v_w_out[0], "sum_adamw_w_out"),
             "w_mlp1": _sum_adamw(land_1, w_mlp1[0], m_w_mlp1[0], v_w_mlp1[0], "sum_adamw_w_mlp1"),
             "w_mlp2": _sum_adamw(land_2, w_mlp2[0], m_w_mlp2[0], v_w_mlp2[0], "sum_adamw_w_mlp2")}
    flat = tot.reshape(-1)
    o0 = 0
    g_pre_mix_g = flat[o0:o0 + D]; o0 += D
    g_post_mix_g = flat[o0:o0 + D]; o0 += D
    g_pre_mlp_g = flat[o0:o0 + D]; o0 += D
    g_post_mlp_g = flat[o0:o0 + D]; o0 += D
    g_gn = flat[o0:o0 + RET_WIDTH]; o0 += RET_WIDTH
    nrpb = NA_HEADS * (2 * NA_KH - 1) * 32
    g_rpb = flat[o0:o0 + nrpb].reshape(NA_HEADS, 2 * NA_KH - 1, 32)[:, :, :ncls]; o0 += nrpb
    g_decay = flat[o0:o0 + 2 * RET_HEADS].reshape(2, RET_HEADS); o0 += LANES
    loss = flat[o0]

    grads = {
        "c_ctx": g_c_ctx.reshape(c_ctx.shape), "w_ada": g_w_ada[None], "b_ada": g_b_ada.reshape(b_ada.shape),
        "g_pre_mix": g_pre_mix_g[None], "g_post_mix": g_post_mix_g[None], "g_pre_mlp": g_pre_mlp_g[None],
        "g_post_mlp": g_post_mlp_g[None], "w_in": g_w_in[None], "ret_decay": g_decay[None], "ret_gn": g_gn[None],
        "na_rpb": g_rpb[None], "w_out": fused["w_out"][0][None], "w_mlp1": fused["w_mlp1"][0][None],
        "w_mlp2": fused["w_mlp2"][0][None],
    }
    weights = dict(c_ctx=c_ctx, w_ada=w_ada, b_ada=b_ada, g_pre_mix=g_pre_mix, g_post_mix=g_post_mix,
                   g_pre_mlp=g_pre_mlp, g_post_mlp=g_post_mlp, w_in=w_in, ret_decay=ret_decay, ret_gn=ret_gn,
                   na_rpb=na_rpb, w_out=w_out, w_mlp1=w_mlp1, w_mlp2=w_mlp2)
    m_in = dict(c_ctx=m_c_ctx, w_ada=m_w_ada, b_ada=m_b_ada, g_pre_mix=m_g_pre_mix, g_post_mix=m_g_post_mix,
                g_pre_mlp=m_g_pre_mlp, g_post_mlp=m_g_post_mlp, w_in=m_w_in, ret_decay=m_ret_decay,
                ret_gn=m_ret_gn, na_rpb=m_na_rpb, w_out=m_w_out, w_mlp1=m_w_mlp1, w_mlp2=m_w_mlp2)
    v_in = dict(c_ctx=v_c_ctx, w_ada=v_w_ada, b_ada=v_b_ada, g_pre_mix=v_g_pre_mix, g_post_mix=v_g_post_mix,
                g_pre_mlp=v_g_pre_mlp, g_post_mlp=v_g_post_mlp, w_in=v_w_in, ret_decay=v_ret_decay,
                ret_gn=v_ret_gn, na_rpb=v_na_rpb, w_out=v_w_out, w_mlp1=v_w_mlp1, w_mlp2=v_w_mlp2)
    names = list(weights)
    deltas, new_m, new_v = {}, {}, {}
    def as_2d(n):
        shp = weights[n].shape
        two_d = (-1, shp[-1]) if len(shp) > 1 else (1, shp[0])
        return [a.reshape(two_d) for a in (weights[n], grads[n], m_in[n], v_in[n])]

    small = [n for n in names if n not in fused and weights[n].size <= 65536]
    updated = dict(zip(small, _adamw_small([as_2d(n) for n in small], "adamw_small")))
    for n in names:
        if n in fused:
            updated[n] = fused[n][1:]
        elif n not in updated:
            updated[n] = _adamw(*as_2d(n), "adamw_" + n)
        deltas[n], new_m[n], new_v[n] = (a.reshape(weights[n].shape) for a in updated[n])
    return (loss, grad_x, *[grads[n] for n in names], *[deltas[n] for n in names],
            *[new_m[n] for n in names], *[new_v[n] for n in names])
```

```python
import functools
import math

import numpy as np
import jax
import jax.numpy as jnp
from jax import lax
from jax.experimental import pallas as pl
from jax.experimental.pallas import tpu as pltpu

F32 = jnp.float32
BF16 = jnp.bfloat16
MESH = pl.DeviceIdType.MESH

N_DEV = 8
LANES = 128
SUBLANES = 8
VMEM_LIMIT = 60 * 1024 * 1024

GRID_W = 64
RET_HEADS = 4
RET_DIM = 128
RET_WIDTH = RET_HEADS * RET_DIM
NA_HEADS = 8
NA_DIM = 64
NA_WIDTH = NA_HEADS * NA_DIM
NA_PAIRS = NA_HEADS // 2
NA_KH = 8
NA_KW = 16
NA_GROUP = 8
SEG = 512
ROPE_BASE = 10000.0
NORM_EPS = 1e-6
NEG_INF = -1e30
N_MOD = 6

ADAM_LR = 0.001
ADAM_B1 = 0.9
ADAM_B2 = 0.999
ADAM_EPS = 1e-08
ADAM_WD = 0.01
ADAM_STEP = 10


def _dot(a, b):
    return lax.dot_general(a, b, (((1,), (0,)), ((), ())), preferred_element_type=F32)


def _dot_nt(a, b):
    return lax.dot_general(a, b, (((1,), (1,)), ((), ())), preferred_element_type=F32)


def _dot_tn(a, b):
    return lax.dot_general(a, b, (((0,), (0,)), ((), ())), preferred_element_type=F32)


def _sigmoid(x):
    return 1.0 / (1.0 + jnp.exp(-x))


def _div_tile(n, cap, mult):
    if n <= cap:
        return n
    for t in range(cap - cap % mult, 0, -mult):
        if n % t == 0:
            return t
    raise ValueError(f"no tile for {n}")


def _params(*sem):
    return pltpu.CompilerParams(dimension_semantics=tuple(sem) if sem else None,
                                vmem_limit_bytes=VMEM_LIMIT)


def _vmem():
    return pl.BlockSpec(memory_space=pltpu.VMEM)


def _any():
    return pl.BlockSpec(memory_space=pl.ANY)


def _me_and_peers():
    x, y, c = lax.axis_index("x"), lax.axis_index("y"), lax.axis_index("c")
    me = 4 * x + 2 * y + c
    peers = []
    for m in range(1, N_DEV):
        px = 1 - x if (m >> 2) & 1 else x
        py = 1 - y if (m >> 1) & 1 else y
        pc = 1 - c if m & 1 else c
        peers.append(((px, py, pc), 4 * px + 2 * py + pc))
    return me, peers


def _exchange(src_for, dst_from, send_sems, recv_sems):
    me, peers = _me_and_peers()
    sent = []
    for i, (dev, pid) in enumerate(peers):
        cp = pltpu.make_async_remote_copy(src_ref=src_for(pid), dst_ref=dst_from(me),
                                          send_sem=send_sems.at[i], recv_sem=recv_sems.at[i],
                                          device_id=dev, device_id_type=MESH)
        cp.start()
        sent.append(cp)
    for i, (dev, pid) in enumerate(peers):
        pltpu.make_async_remote_copy(src_ref=src_for(pid), dst_ref=dst_from(pid),
                                     send_sem=send_sems.at[i], recv_sem=recv_sems.at[i],
                                     device_id=dev, device_id_type=MESH).wait_recv()
    for cp in sent:
        cp.wait_send()


SIBLING = (1,)
ICI_SAME_CORE = (2, 4, 6)
ALL_PEERS = tuple(range(1, N_DEV))


def _remote(src, dst, send_sem, recv_sem, dev):
    return pltpu.make_async_remote_copy(src_ref=src, dst_ref=dst, send_sem=send_sem, recv_sem=recv_sem,
                                        device_id=dev, device_id_type=MESH)


def _push_start(items, masks, send_sems, recv_sems):
    me, peers = _me_and_peers()
    for k, (src_for, dst_from) in enumerate(items):
        for m in masks:
            dev, pid = peers[m - 1]
            _remote(src_for(pid), dst_from(me), send_sems.at[k, m - 1], recv_sems.at[k, m - 1], dev).start()


def _push_wait_recv(items, masks, send_sems, recv_sems):
    me, peers = _me_and_peers()
    for k, (src_for, dst_from) in enumerate(items):
        for m in masks:
            dev, pid = peers[m - 1]
            _remote(src_for(pid), dst_from(pid), send_sems.at[k, m - 1], recv_sems.at[k, m - 1], dev).wait_recv()


def _push_wait_send(items, masks, send_sems, recv_sems):
    me, peers = _me_and_peers()
    for k, (src_for, dst_from) in enumerate(items):
        for m in masks:
            dev, pid = peers[m - 1]
            _remote(src_for(pid), dst_from(me), send_sems.at[k, m - 1], recv_sems.at[k, m - 1], dev).wait_send()


def _forward_start(items, send_sems, recv_sems):
    me, peers = _me_and_peers()
    sib = peers[0][0]
    for k, (blk_in, blk_out) in enumerate(items):
        for j, m in enumerate(ICI_SAME_CORE):
            pid = peers[m - 1][1]
            _remote(blk_in(pid), blk_out(pid), send_sems.at[k, j], recv_sems.at[k, j], sib).start()


def _forward_wait(items, send_sems, recv_sems):
    me, peers = _me_and_peers()
    sib = peers[0][0]
    for k, (blk_in, blk_out) in enumerate(items):
        for j, m in enumerate(ICI_SAME_CORE):
            got = peers[(m | 1) - 1][1]
            _remote(blk_in(got), blk_out(got), send_sems.at[k, j], recv_sems.at[k, j], sib).wait_recv()
    for k, (blk_in, blk_out) in enumerate(items):
        for j, m in enumerate(ICI_SAME_CORE):
            pid = peers[m - 1][1]
            _remote(blk_in(pid), blk_out(pid), send_sems.at[k, j], recv_sems.at[k, j], sib).wait_send()


def _mod_gather(c, c_ctx, w_ada, b_ada, w_in_t, w_out, w1, w2):
    B, D = c.shape
    ncol = w_ada.shape[1]
    rows = SUBLANES * N_DEV + SUBLANES

    def body(c_ref, cc_ref, w_ref, b_ref, win_ref, wout_ref, w1_ref, w2_ref,
             s_ref, m_ref, gin_ref, wout_b, w1_b, w2_b,
             win_b, msend, send1, recv1, send2, recv2, wsend, wrecv, fsend, frecv, lsem):
        me, _ = _me_and_peers()
        win_b[...] = win_ref[...].astype(BF16)
        block = _row_block(gin_ref, w_in_t.shape[0])
        gather = [(lambda p: win_b, block)]
        own = pltpu.make_async_copy(win_b, block(me), lsem.at[0])
        cv = c_ref[...]
        slot = jnp.concatenate([cv * _sigmoid(cv), jnp.zeros((SUBLANES - B, D), F32)], axis=0)
        my_rows = pl.ds(pl.multiple_of(me * SUBLANES, SUBLANES), SUBLANES)
        s_ref[my_rows, :] = slot
        ccv = cc_ref[...]
        s_ref[SUBLANES * N_DEV:, :] = jnp.concatenate(
            [ccv * _sigmoid(ccv), jnp.zeros((SUBLANES - 1, D), F32)], axis=0)

        def rows_of(p):
            return s_ref.at[pl.ds(pl.multiple_of(p * SUBLANES, SUBLANES), SUBLANES), :]

        _exchange(lambda p: rows_of(me), rows_of, send1, recv1)
        own.start()
        _push_start(gather, SIBLING + ICI_SAME_CORE, wsend, wrecv)
        wout_b[...] = wout_ref[...].astype(BF16)
        w1_b[...] = w1_ref[...].astype(BF16)
        w2_b[...] = w2_ref[...].astype(BF16)
        b_loc = b_ref[:, pl.ds(pl.multiple_of(me * ncol, ncol), ncol)]
        mods = _dot(s_ref[...], w_ref[...]) + b_loc
        for p in range(N_DEV):
            msend[p] = jnp.concatenate([mods[p * SUBLANES:(p + 1) * SUBLANES], mods[N_DEV * SUBLANES:]], axis=0)
        m_ref[me] = msend[me]
        columns = [(lambda p: msend.at[p], lambda p: m_ref.at[p])]
        _push_start(columns, ALL_PEERS, send2, recv2)
        _push_wait_recv(gather, ICI_SAME_CORE, wsend, wrecv)
        relay = [(block, block)]
        _forward_start(relay, fsend, frecv)
        _push_wait_recv(columns, ALL_PEERS, send2, recv2)
        _push_wait_recv(gather, SIBLING, wsend, wrecv)
        _forward_wait(relay, fsend, frecv)
        _push_wait_send(columns, ALL_PEERS, send2, recv2)
        _push_wait_send(gather, SIBLING + ICI_SAME_CORE, wsend, wrecv)
        own.wait()

    return pl.pallas_call(
        body, name="mod_gather",
        out_shape=(jax.ShapeDtypeStruct((rows, D), F32), jax.ShapeDtypeStruct((N_DEV, 2 * SUBLANES, ncol), F32),
                   jax.ShapeDtypeStruct((N_DEV * w_in_t.shape[0], D), BF16),
                   jax.ShapeDtypeStruct(w_out.shape, BF16), jax.ShapeDtypeStruct(w1.shape, BF16),
                   jax.ShapeDtypeStruct(w2.shape, BF16)),
        in_specs=[_vmem()] * 8, out_specs=(_vmem(), _vmem(), _any(), _vmem(), _vmem(), _vmem()),
        scratch_shapes=[pltpu.VMEM(w_in_t.shape, BF16), pltpu.VMEM((N_DEV, 2 * SUBLANES, ncol), F32)]
                       + [pltpu.SemaphoreType.DMA((N_DEV - 1,))] * 2
                       + [pltpu.SemaphoreType.DMA((1, N_DEV - 1))] * 4 + [pltpu.SemaphoreType.DMA((1, 3))] * 2
                       + [pltpu.SemaphoreType.DMA((1,))],
        compiler_params=pltpu.CompilerParams(vmem_limit_bytes=VMEM_LIMIT),
    )(c, c_ctx.reshape(1, D), w_ada, b_ada, w_in_t, w_out, w1, w2)


def _row_block(ref, rows):
    return lambda p: ref.at[pl.ds(pl.multiple_of(p * rows, 2 * SUBLANES), rows), :]


def _col_block(ref, cols):
    return lambda p: ref.at[:, pl.ds(pl.multiple_of(p * cols, LANES), cols)]


def _slot(ref):
    return lambda p: ref.at[p]


class _Hosted:
    def __init__(self, kind, masks, operands, block_of, out_shapes, with_own):
        self.kind, self.masks, self.operands = kind, masks, list(operands)
        self.block_of, self.out_shapes, self.with_own = block_of, list(out_shapes), with_own
        self.n = len(self.operands)

    def scratch(self):
        return [pltpu.SemaphoreType.DMA((self.n, N_DEV - 1)), pltpu.SemaphoreType.DMA((self.n, N_DEV - 1)),
                pltpu.SemaphoreType.DMA((self.n,))]

    def _items(self, in_refs, out_refs):
        items = []
        for k in range(self.n):
            if self.kind == "gather":
                items.append((lambda p, k=k: in_refs[k], self.block_of[k](out_refs[k])))
            else:
                items.append((self.block_of[k](in_refs[k]), _slot(out_refs[k])))
        return items

    def _own(self, in_refs, out_refs, lsem):
        me, _ = _me_and_peers()
        items = self._items(in_refs, out_refs)
        return [pltpu.make_async_copy(src_for(me), dst_from(me), lsem.at[k])
                for k, (src_for, dst_from) in enumerate(items)]

    def start(self, in_refs, out_refs, sems):
        send, recv, lsem = sems
        if self.with_own:
            for cp in self._own(in_refs, out_refs, lsem):
                cp.start()
        _push_start(self._items(in_refs, out_refs), self.masks, send, recv)

    def wait(self, in_refs, out_refs, sems):
        send, recv, lsem = sems
        items = self._items(in_refs, out_refs)
        _push_wait_recv(items, self.masks, send, recv)
        _push_wait_send(items, self.masks, send, recv)
        if self.with_own:
            for cp in self._own(in_refs, out_refs, lsem):
                cp.wait()


class _HostedRelay:
    def __init__(self, arrays, block_of):
        self.operands, self.block_of = list(arrays), block_of
        self.out_shapes = [jax.ShapeDtypeStruct(a.shape, a.dtype) for a in arrays]
        self.n = len(self.operands)

    def scratch(self):
        return [pltpu.SemaphoreType.DMA((self.n, 3)), pltpu.SemaphoreType.DMA((self.n, 3))]

    def _items(self, in_refs, out_refs):
        return [(self.block_of[k](in_refs[k]), self.block_of[k](out_refs[k])) for k in range(self.n)]

    def start(self, in_refs, out_refs, sems):
        _forward_start(self._items(in_refs, out_refs), *sems)

    def wait(self, in_refs, out_refs, sems):
        _forward_wait(self._items(in_refs, out_refs), *sems)


def _call_hosting(body, hosted, *, name, grid, out_shape, in_specs, out_specs, scratch_shapes, args):
    n_in, n_out, n_scr = len(in_specs), len(out_shape), len(scratch_shapes)
    hn = sum(hs.n for hs in hosted)
    n_sem = [len(hs.scratch()) for hs in hosted]

    def wrapped(*refs):
        ins = refs[:n_in]
        h_in = refs[n_in:n_in + hn]
        outs = refs[n_in + hn:n_in + hn + n_out]
        h_out = refs[n_in + hn + n_out:n_in + 2 * hn + n_out]
        scr = refs[n_in + 2 * hn + n_out:n_in + 2 * hn + n_out + n_scr]
        sems = refs[n_in + 2 * hn + n_out + n_scr:]
        ids = [pl.program_id(i) for i in range(len(grid))]
        first = functools.reduce(jnp.logical_and, [i == 0 for i in ids])
        last = functools.reduce(jnp.logical_and, [i == g - 1 for i, g in zip(ids, grid)])
        parts, o0, s0 = [], 0, 0
        for hs, ns in zip(hosted, n_sem):
            parts.append((hs, h_in[o0:o0 + hs.n], h_out[o0:o0 + hs.n], sems[s0:s0 + ns]))
            o0 += hs.n
            s0 += ns

        @pl.when(first)
        def _():
            for hs, hi, ho, se in parts:
                hs.start(hi, ho, se)

        body(*ins, *outs, *scr)

        @pl.when(last)
        def _():
            for hs, hi, ho, se in parts:
                hs.wait(hi, ho, se)

    aliases, o0 = {}, 0
    for hs in hosted:
        if isinstance(hs, _HostedRelay):
            aliases.update({n_in + o0 + k: n_out + o0 + k for k in range(hs.n)})
        o0 += hs.n
    return pl.pallas_call(
        wrapped, name=name, grid=grid,
        out_shape=tuple(out_shape) + tuple(s for hs in hosted for s in hs.out_shapes),
        in_specs=list(in_specs) + [_any()] * hn,
        out_specs=tuple(out_specs) + (_any(),) * hn,
        scratch_shapes=list(scratch_shapes) + [s for hs in hosted for s in hs.scratch()],
        input_output_aliases=aliases,
        compiler_params=_params(*(("arbitrary",) * len(grid))),
    )(*args, *[a for hs in hosted for a in hs.operands])


def _token_tiles(n_ctx, tm):
    nct = n_ctx // tm

    def ctx_spec(D):
        return pl.BlockSpec((None, tm, D), lambda b, t: (b, jnp.minimum(t, nct - 1), 0))

    def lat_spec(D):
        return pl.BlockSpec((None, tm, D), lambda b, t: (b, jnp.maximum(t - nct, 0), 0))

    return nct, ctx_spec, lat_spec


def _inproj_fwd(x, ctx, modl, g1, w_in_t, hosted):
    B, N, D = x.shape
    n_ctx = ctx.shape[1]
    T = n_ctx + N
    nw = w_in_t.shape[0]
    tm = _div_tile(n_ctx, 256, 16)
    nct, ctx_spec, lat_spec = _token_tiles(n_ctx, tm)

    def body(c_ref, x_ref, sh_ref, sc_ref, g_ref, w_ref, h_ref, p_ref):
        x = jnp.where(pl.program_id(1) < nct, c_ref[...], x_ref[...])
        r = lax.rsqrt(jnp.mean(x * x, axis=-1, keepdims=True) + NORM_EPS)
        h = ((x * r) * g_ref[...]) * (1.0 + sc_ref[...]) + sh_ref[...]
        hb = h.astype(BF16)
        h_ref[...] = hb
        p_ref[...] = _dot_nt(hb, w_ref[...])

    def mrow(b, t):
        return jnp.where(t < nct, B, b)

    return _call_hosting(
        body, hosted, name="inproj_fwd", grid=(B, T // tm),
        out_shape=(jax.ShapeDtypeStruct((B, T, D), BF16), jax.ShapeDtypeStruct((B, T, nw), F32)),
        in_specs=[ctx_spec(D), lat_spec(D),
                  pl.BlockSpec((None, None, 1, D), lambda b, t: (mrow(b, t), 0, 0, 0)),
                  pl.BlockSpec((None, None, 1, D), lambda b, t: (mrow(b, t), 1, 0, 0)),
                  pl.BlockSpec((1, D), lambda b, t: (0, 0)),
                  pl.BlockSpec((nw, D), lambda b, t: (0, 0))],
        out_specs=(pl.BlockSpec((None, tm, D), lambda b, t: (b, t, 0)),
                   pl.BlockSpec((None, tm, nw), lambda b, t: (b, t, 0))),
        scratch_shapes=[], args=(ctx, x, modl, modl, g1, w_in_t))


def _swap32(x):
    lane = lax.broadcasted_iota(jnp.int32, x.shape, 1)
    return jnp.where((lane % 64) < 32, pltpu.roll(x, 96, 1), pltpu.roll(x, 32, 1))


def _rope(x, cos, sin):
    return x * cos + _swap32(x) * sin


def _unrope(dy, cos, sin):
    return dy * cos + _swap32(dy * sin)


def _ret_weights(lgf, lgb, dist):
    return jnp.exp(jnp.where(dist >= 0.0, lgf * dist, -lgb * dist))


class _RetDecay:
    def __init__(self, lgf, lgb, rows):
        r = lax.broadcasted_iota(jnp.int32, (rows, RET_DIM), 0).astype(F32)
        self.head = r + 1.0
        self.tail = (rows - 1.0) - r
        self.q_f = jnp.exp(lgf * self.head)
        self.k_f = jnp.exp(lgf * self.tail)
        self.q_b = jnp.exp(lgb * self.tail)
        self.k_b = jnp.exp(lgb * self.head)


def _ret_states(kf32, vs, lgf, lgb, C, c, nt, hf, hb, hfa=None, hba=None):
    dec = _RetDecay(lgf, lgb, c)
    dec_c = _RetDecay(lgf, lgb, C)
    step_f = jnp.exp(jnp.zeros((RET_DIM, RET_DIM), F32) + lgf * c)
    step_b = jnp.exp(jnp.zeros((RET_DIM, RET_DIM), F32) + lgb * c)

    def upd(rows, kdec):
        return _dot_tn((kf32[rows, :] * kdec).astype(BF16), vs[rows, :])

    def lat(t):
        return slice(C + t * c, C + (t + 1) * c)

    state = upd(slice(0, C), dec_c.k_f)
    aged = jnp.zeros_like(state)
    for t in range(nt):
        hf[t] = state.astype(BF16)
        if hfa is not None:
            hfa[t] = aged
        if t < nt - 1:
            aged = step_f * (aged + c * state)
            state = step_f * state + upd(lat(t), dec.k_f)
    state = upd(slice(0, C), dec_c.k_b)
    aged = jnp.zeros_like(state)
    for t in range(nt - 1, -1, -1):
        hb[t] = state.astype(BF16)
        if hba is not None:
            hba[t] = aged
        if t > 0:
            aged = step_b * (aged + c * state)
            state = step_b * state + upd(lat(t), dec.k_b)
    return dec, dec_c, step_f, step_b


def _ret_fwd(proj, cos, sin, lg, gn, n_ctx, hosted):
    B, T, _ = proj.shape
    C = n_ctx
    N = T - C
    c = _div_tile(N, 256, 16)
    nt = N // c
    scale = RET_DIM ** -0.5

    def body(lg_ref, q_ref, k_ref, v_ref, g_ref, cos_ref, sin_ref, gn_ref, o_ref, lat_ref, qs, ks, vs, kf32, hf, hb):
        h = pl.program_id(1)
        lgf = lg_ref[0, h]
        lgb = lg_ref[1, h]
        for rows in [slice(0, C)] + [slice(C + t * c, C + (t + 1) * c) for t in range(nt)]:
            cosb = cos_ref[rows, :]
            sinb = sin_ref[rows, :]
            qs[rows, :] = (_rope(q_ref[rows, :], cosb, sinb) * scale).astype(BF16)
            kr = _rope(k_ref[rows, :], cosb, sinb)
            kf32[rows, :] = kr
            ks[rows, :] = kr.astype(BF16)
            vs[rows, :] = v_ref[rows, :].astype(BF16)
        gnv = gn_ref[...]
        dec, _, _, _ = _ret_states(kf32, vs, lgf, lgb, C, c, nt, hf, hb)
        rc = (lax.broadcasted_iota(jnp.int32, (c, c), 0) - lax.broadcasted_iota(jnp.int32, (c, c), 1)).astype(F32)
        w_diag = _ret_weights(lgf, lgb, rc)
        for t in range(nt):
            rows = slice(C + t * c, C + (t + 1) * c)
            qt = qs[rows, :]
            s = _dot_nt(qt, ks[rows, :])
            o = (_dot((s * w_diag).astype(BF16), vs[rows, :])
                 + dec.q_f * _dot(qt, hf[t]) + dec.q_b * _dot(qt, hb[t]))
            o_ref[t * c:(t + 1) * c, :] = o
            mu = jnp.mean(o, axis=-1, keepdims=True)
            oc = o - mu
            var = jnp.mean(oc * oc, axis=-1, keepdims=True)
            yh = oc * lax.rsqrt(var + NORM_EPS)
            g = g_ref[rows, :]
            lat_ref[t * c:(t + 1) * c, :] = ((yh * gnv) * (g * _sigmoid(g))).astype(BF16)

    def col(seg):
        return pl.BlockSpec((None, T, RET_DIM), lambda b, h, seg=seg: (b, 0, seg * RET_HEADS + h))

    return _call_hosting(
        body, hosted, name="ret_fwd", grid=(B, RET_HEADS),
        out_shape=(jax.ShapeDtypeStruct((B, N, RET_WIDTH), F32), jax.ShapeDtypeStruct((B, N, RET_WIDTH), BF16)),
        in_specs=[pl.BlockSpec(memory_space=pltpu.SMEM), col(0), col(1), col(2), col(3),
                  pl.BlockSpec((T, RET_DIM), lambda b, h: (0, 0)), pl.BlockSpec((T, RET_DIM), lambda b, h: (0, 0)),
                  pl.BlockSpec((1, RET_DIM), lambda b, h: (0, h))],
        out_specs=(pl.BlockSpec((None, N, RET_DIM), lambda b, h: (b, 0, h)),
                   pl.BlockSpec((None, N, RET_DIM), lambda b, h: (b, 0, h))),
        scratch_shapes=[pltpu.VMEM((T, RET_DIM), BF16)] * 3 + [pltpu.VMEM((T, RET_DIM), F32)]
                       + [pltpu.VMEM((nt, RET_DIM, RET_DIM), BF16)] * 2,
        args=(lg, proj, proj, proj, proj, cos, sin, gn))


def _ret_bwd(proj, cos, sin, lg, gn, o, dlat, n_ctx, hosted):
    B, T, _ = proj.shape
    C = n_ctx
    N = T - C
    c = _div_tile(N, 256, 16)
    nt = N // c
    scale = RET_DIM ** -0.5

    def lat(t):
        return slice(C + t * c, C + (t + 1) * c)

    def body(lg_ref, q_ref, k_ref, v_ref, g_ref, cos_ref, sin_ref, gn_ref, o_ref, dl_ref,
             d_ref, dgn_ref, dlg_ref, qs, ks, vs, dos, qf32, kf32, hf, hb, hfa, hba, gf_s, gb_s):
        h = pl.program_id(1)
        lgf = lg_ref[0, h]
        lgb = lg_ref[1, h]
        gnv = gn_ref[...]

        def fold(a):
            return jnp.sum(a.reshape(a.shape[0] // SUBLANES, SUBLANES, a.shape[1]), axis=0)

        for rows in [slice(0, C)] + [lat(t) for t in range(nt)]:
            cosb = cos_ref[rows, :]
            sinb = sin_ref[rows, :]
            qr = _rope(q_ref[rows, :], cosb, sinb) * scale
            qf32[rows, :] = qr
            qs[rows, :] = qr.astype(BF16)
            kr = _rope(k_ref[rows, :], cosb, sinb)
            kf32[rows, :] = kr
            ks[rows, :] = kr.astype(BF16)
            vs[rows, :] = v_ref[rows, :].astype(BF16)

        dgn = jnp.zeros((1, RET_DIM), F32)
        for t in range(nt):
            lrows = slice(t * c, (t + 1) * c)
            ov = o_ref[lrows, :]
            mu = jnp.mean(ov, axis=-1, keepdims=True)
            oc = ov - mu
            var = jnp.mean(oc * oc, axis=-1, keepdims=True)
            rstd = lax.rsqrt(var + NORM_EPS)
            yh = oc * rstd
            g = g_ref[lat(t), :]
            sg = _sigmoid(g)
            dl = dl_ref[lrows, :]
            d_ref[3, lat(t), :] = (dl * (yh * gnv) * (sg * (1.0 + g * (1.0 - sg)))).astype(BF16)
            dls = dl * (g * sg)
            dgn = dgn + jnp.sum(dls * yh, axis=0, keepdims=True)
            dyh = dls * gnv
            do = rstd * (dyh - jnp.mean(dyh, axis=-1, keepdims=True)
                         - yh * jnp.mean(dyh * yh, axis=-1, keepdims=True))
            dos[lrows, :] = do.astype(BF16)
        dgn_ref[...] = jnp.concatenate([dgn, jnp.zeros((SUBLANES - 1, RET_DIM), F32)], axis=0)
        d_ref[3, 0:C, :] = jnp.zeros((C, RET_DIM), BF16)
        d_ref[0, 0:C, :] = jnp.zeros((C, RET_DIM), BF16)

        dec, dec_c, step_f, step_b = _ret_states(kf32, vs, lgf, lgb, C, c, nt, hf, hb, hfa, hba)

        def zmat(t, qdec):
            return _dot_tn((qf32[lat(t), :] * qdec).astype(BF16), dos[t * c:(t + 1) * c, :])

        acc3f = jnp.zeros((RET_DIM, RET_DIM), F32)
        acc3b = jnp.zeros((RET_DIM, RET_DIM), F32)
        state = jnp.zeros((RET_DIM, RET_DIM), F32)
        for t in range(nt - 1, -1, -1):
            gf_s[t] = state.astype(BF16)
            z = zmat(t, dec.q_f)
            acc3f = acc3f + hfa[t] * z
            state = step_f * state + z
        gctx_f = state.astype(BF16)
        state = jnp.zeros((RET_DIM, RET_DIM), F32)
        for t in range(nt):
            gb_s[t] = state.astype(BF16)
            z = zmat(t, dec.q_b)
            acc3b = acc3b + hba[t] * z
            state = step_b * state + z
        gctx_b = state.astype(BF16)

        rc = (lax.broadcasted_iota(jnp.int32, (c, c), 0) - lax.broadcasted_iota(jnp.int32, (c, c), 1)).astype(F32)
        w_diag = _ret_weights(lgf, lgb, rc)
        wg_f = jnp.where(rc >= 0.0, w_diag * rc, 0.0)
        wg_b = jnp.where(rc < 0.0, -w_diag * rc, 0.0)
        accf = jnp.zeros((SUBLANES, RET_DIM), F32)
        accb = jnp.zeros((SUBLANES, RET_DIM), F32)
        gdf = jnp.zeros((SUBLANES, c), F32)
        gdb = jnp.zeros((SUBLANES, c), F32)
        for t in range(nt):
            rows = lat(t)
            qt = qs[rows, :]
            kt = ks[rows, :]
            vt = vs[rows, :]
            dot = dos[t * c:(t + 1) * c, :]
            s = _dot_nt(qt, kt)
            dp = _dot_nt(dot, vt)
            dv = _dot_tn((s * w_diag).astype(BF16), dot)
            ds = (dp * w_diag).astype(BF16)
            dq = _dot(ds, kt)
            dk = _dot_tn(ds, qt)
            gs = dp * s
            gdf = gdf + fold(gs * wg_f)
            gdb = gdb + fold(gs * wg_b)
            qv = qf32[rows, :]
            kv = kf32[rows, :]
            dq_f = dec.q_f * _dot_nt(dot, hf[t])
            dq_b = dec.q_b * _dot_nt(dot, hb[t])
            dk_f = dec.k_f * _dot_nt(vt, gf_s[t])
            dk_b = dec.k_b * _dot_nt(vt, gb_s[t])
            accf = accf + fold(dec.head * dq_f * qv) + fold(dec.tail * dk_f * kv)
            accb = accb + fold(dec.tail * dq_b * qv) + fold(dec.head * dk_b * kv)
            dv = dv + dec.k_f * _dot(kt, gf_s[t]) + dec.k_b * _dot(kt, gb_s[t])
            cosb = cos_ref[rows, :]
            sinb = sin_ref[rows, :]
            d_ref[0, rows, :] = _unrope((dq + dq_f + dq_b) * scale, cosb, sinb).astype(BF16)
            d_ref[1, rows, :] = _unrope(dk + dk_f + dk_b, cosb, sinb).astype(BF16)
            d_ref[2, rows, :] = dv.astype(BF16)
        kc = ks[0:C, :]
        vc = vs[0:C, :]
        kcv = kf32[0:C, :]
        dkc_f = dec_c.k_f * _dot_nt(vc, gctx_f)
        dkc_b = dec_c.k_b * _dot_nt(vc, gctx_b)
        accf = accf + fold(dec_c.tail * dkc_f * kcv)
        accb = accb + fold(dec_c.head * dkc_b * kcv)
        d_ref[1, 0:C, :] = (dkc_f + dkc_b).astype(BF16)
        d_ref[2, 0:C, :] = (dec_c.k_f * _dot(kc, gctx_f) + dec_c.k_b * _dot(kc, gctx_b)).astype(BF16)
        gf = jnp.sum(gdf) + jnp.sum(accf) + jnp.sum(acc3f)
        gb = jnp.sum(gdb) + jnp.sum(accb) + jnp.sum(acc3b)
        row = lax.broadcasted_iota(jnp.int32, (SUBLANES, LANES), 0)
        dlg_ref[...] = jnp.where(row == 0, gf, jnp.where(row == 1, gb, 0.0))

    def col(seg):
        return pl.BlockSpec((None, T, RET_DIM), lambda b, h, seg=seg: (b, 0, seg * RET_HEADS + h))

    return _call_hosting(
        body, hosted, name="ret_bwd", grid=(B, RET_HEADS),
        out_shape=(jax.ShapeDtypeStruct((B, 4, T, RET_WIDTH), BF16),
                   jax.ShapeDtypeStruct((B, SUBLANES, RET_WIDTH), F32),
                   jax.ShapeDtypeStruct((B, RET_HEADS, SUBLANES, LANES), F32)),
        in_specs=[pl.BlockSpec(memory_space=pltpu.SMEM), col(0), col(1), col(2), col(3),
                  pl.BlockSpec((T, RET_DIM), lambda b, h: (0, 0)), pl.BlockSpec((T, RET_DIM), lambda b, h: (0, 0)),
                  pl.BlockSpec((1, RET_DIM), lambda b, h: (0, h)),
                  pl.BlockSpec((None, N, RET_DIM), lambda b, h: (b, 0, h)),
                  pl.BlockSpec((None, N, RET_DIM), lambda b, h: (b, 0, h))],
        out_specs=(pl.BlockSpec((None, 4, T, RET_DIM), lambda b, h: (b, 0, 0, h)),
                   pl.BlockSpec((None, SUBLANES, RET_DIM), lambda b, h: (b, 0, h)),
                   pl.BlockSpec((None, None, SUBLANES, LANES), lambda b, h: (b, h, 0, 0))),
        scratch_shapes=[pltpu.VMEM((T, RET_DIM), BF16)] * 3 + [pltpu.VMEM((N, RET_DIM), BF16)]
                       + [pltpu.VMEM((T, RET_DIM), F32)] * 2
                       + [pltpu.VMEM((nt, RET_DIM, RET_DIM), BF16)] * 2 + [pltpu.VMEM((nt, RET_DIM, RET_DIM), F32)] * 2
                       + [pltpu.VMEM((nt, RET_DIM, RET_DIM), BF16)] * 2,
        args=(lg, proj, proj, proj, proj, cos, sin, gn, o, dlat))


def _na_geometry(rows):
    kh = min(NA_KH, rows)
    return kh, kh * GRID_W


def _pair_select():
    lane = lax.broadcasted_iota(jnp.int32, (2 * GRID_W, LANES), 1)
    row = lax.broadcasted_iota(jnp.int32, (2 * GRID_W, LANES), 0)
    return (lane >= NA_DIM) == (row >= GRID_W)


def _pair_bias(bias_ref, dr0, kh):
    return jnp.concatenate(
        [jnp.concatenate([bias_ref[e, pl.ds(dr0 + 2 * m, 1)].reshape(GRID_W, LANES) for m in range(kh // 2)], axis=1)
         for e in range(2)], axis=0)


def _na_softmax(s_loc, s_ctx):
    mx = jnp.maximum(jnp.max(s_loc, axis=-1, keepdims=True), jnp.max(s_ctx, axis=-1, keepdims=True))
    p_loc = jnp.exp(s_loc - mx)
    p_ctx = jnp.exp(s_ctx - mx)
    den = jnp.sum(p_loc, axis=-1, keepdims=True) + jnp.sum(p_ctx, axis=-1, keepdims=True)
    return p_loc, p_ctx, den


def _na_fwd(proj, bias2, n_ctx, hosted):
    B, T, _ = proj.shape
    C = n_ctx
    N = T - C
    R = N // GRID_W
    kh, nk = _na_geometry(R)
    scale = NA_DIM ** -0.5
    base = (4 * RET_WIDTH) // LANES

    def body(q_ref, k_ref, v_ref, bias_ref, out_ref, kb16, vb16):
        kb16[...] = k_ref[...].astype(BF16)
        vb16[...] = v_ref[...].astype(BF16)
        kc = kb16[0:C, :]
        vc = vb16[0:C, :]
        lane = lax.broadcasted_iota(jnp.int32, (GRID_W, LANES), 1)
        sel2 = _pair_select()

        def group(gi, carry):
            pre = []
            for u in range(NA_GROUP):
                r = gi * NA_GROUP + u
                bs = jnp.clip(r - kh // 2, 0, R - kh)
                dr0 = bs - r + (NA_KH - 1)
                q = q_ref[pl.ds(pl.multiple_of(C + r * GRID_W, GRID_W), GRID_W), :] * scale
                q2 = jnp.where(sel2, jnp.concatenate([q, q], axis=0), 0.0).astype(BF16)
                band = pl.ds(pl.multiple_of(C + bs * GRID_W, GRID_W), nk)
                s_loc = _dot_nt(q2, kb16[band, :]) + _pair_bias(bias_ref, dr0, kh)
                s_ctx = _dot_nt(q2, kc)
                pre.append((r, band, s_loc, s_ctx))
            mid = [(r, band) + _na_softmax(s_loc, s_ctx) for r, band, s_loc, s_ctx in pre]
            for r, band, p_loc, p_ctx, den in mid:
                o2 = (_dot(p_loc.astype(BF16), vb16[band, :]) + _dot(p_ctx.astype(BF16), vc)) / den
                out_ref[pl.ds(pl.multiple_of(r * GRID_W, GRID_W), GRID_W), :] = jnp.where(
                    lane < NA_DIM, o2[:GRID_W], o2[GRID_W:]).astype(BF16)
            return carry

        lax.fori_loop(0, R // NA_GROUP, group, 0)

    def col(seg):
        return pl.BlockSpec((None, T, LANES), lambda b, p, seg=seg: (b, 0, base + seg * NA_PAIRS + p))

    return _call_hosting(
        body, hosted, name="na_fwd", grid=(B, NA_PAIRS),
        out_shape=(jax.ShapeDtypeStruct((B, N, NA_WIDTH), BF16),),
        in_specs=[col(0), col(1), col(2),
                  pl.BlockSpec((2, 2 * NA_KH - 2, GRID_W, LANES), lambda b, p: (p, 0, 0, 0))],
        out_specs=(pl.BlockSpec((None, N, LANES), lambda b, p: (b, 0, p)),),
        scratch_shapes=[pltpu.VMEM((T, LANES), BF16)] * 2,
        args=(proj, proj, proj, bias2))


def _na_bwd(proj, bias2, dlat, n_ctx, hosted):
    B, T, _ = proj.shape
    C = n_ctx
    N = T - C
    R = N // GRID_W
    kh, nk = _na_geometry(R)
    scale = NA_DIM ** -0.5
    base = (4 * RET_WIDTH) // LANES

    def body(q_ref, k_ref, v_ref, bias_ref, dl_ref, d_ref, db_ref, kb16, vb16, dkv):
        b = pl.program_id(1)
        kb16[...] = k_ref[...].astype(BF16)
        vb16[...] = v_ref[...].astype(BF16)
        kc = kb16[0:C, :]
        vc = vb16[0:C, :]
        lane = lax.broadcasted_iota(jnp.int32, (GRID_W, LANES), 1)
        dkv[...] = jnp.zeros(dkv.shape, F32)
        d_ref[0, 0:C, :] = jnp.zeros((C, LANES), BF16)

        @pl.when(b == 0)
        def _():
            db_ref[...] = jnp.zeros(db_ref.shape, F32)

        sel2 = _pair_select()

        def group(gi, carry):
            pre = []
            for u in range(NA_GROUP):
                r = gi * NA_GROUP + u
                bs = jnp.clip(r - kh // 2, 0, R - kh)
                dr0 = bs - r + (NA_KH - 1)
                q = q_ref[pl.ds(pl.multiple_of(C + r * GRID_W, GRID_W), GRID_W), :] * scale
                do = dl_ref[pl.ds(pl.multiple_of(r * GRID_W, GRID_W), GRID_W), :]
                q2 = jnp.where(sel2, jnp.concatenate([q, q], axis=0), 0.0).astype(BF16)
                do2 = jnp.where(sel2, jnp.concatenate([do, do], axis=0), 0.0).astype(BF16)
                band = pl.ds(pl.multiple_of(C + bs * GRID_W, GRID_W), nk)
                s_loc = _dot_nt(q2, kb16[band, :]) + _pair_bias(bias_ref, dr0, kh)
                s_ctx = _dot_nt(q2, kc)
                dp_loc = _dot_nt(do2, vb16[band, :])
                dp_ctx = _dot_nt(do2, vc)
                pre.append((r, dr0, band, q2, do2, s_loc, s_ctx, dp_loc, dp_ctx))
            mid = []
            for r, dr0, band, q2, do2, s_loc, s_ctx, dp_loc, dp_ctx in pre:
                p_loc, p_ctx, den = _na_softmax(s_loc, s_ctx)
                inv = 1.0 / den
                p_loc = p_loc * inv
                p_ctx = p_ctx * inv
                delta = (jnp.sum(p_loc * dp_loc, axis=-1, keepdims=True)
                         + jnp.sum(p_ctx * dp_ctx, axis=-1, keepdims=True))
                ds_loc = p_loc * (dp_loc - delta)
                ds_ctx = p_ctx * (dp_ctx - delta)
                mid.append((r, dr0, band, q2, do2, p_loc.astype(BF16), p_ctx.astype(BF16), ds_loc, ds_ctx))
            for r, dr0, band, q2, do2, pb_loc, pb_ctx, ds_loc, ds_ctx in mid:
                dsb_loc = ds_loc.astype(BF16)
                dsb_ctx = ds_ctx.astype(BF16)
                dq2 = _dot(dsb_loc, kb16[band, :]) + _dot(dsb_ctx, kc)
                d_ref[0, pl.ds(pl.multiple_of(C + r * GRID_W, GRID_W), GRID_W), :] = (jnp.where(
                    lane < NA_DIM, dq2[:GRID_W], dq2[GRID_W:]) * scale).astype(BF16)
                dkv[0, band, :] += _dot_tn(dsb_loc, q2)
                dkv[1, band, :] += _dot_tn(pb_loc, do2)
                dkv[0, 0:C, :] += _dot_tn(dsb_ctx, q2)
                dkv[1, 0:C, :] += _dot_tn(pb_ctx, do2)
                for e in range(2):
                    for m in range(kh // 2):
                        db_ref[e, pl.ds(dr0 + 2 * m, 1)] += ds_loc[e * GRID_W:(e + 1) * GRID_W,
                                                                   m * LANES:(m + 1) * LANES].reshape(1, GRID_W, LANES)
            return carry

        lax.fori_loop(0, R // NA_GROUP, group, 0)
        d_ref[1] = dkv[0].astype(BF16)
        d_ref[2] = dkv[1].astype(BF16)

    def col(seg):
        return pl.BlockSpec((None, T, LANES), lambda p, b, seg=seg: (b, 0, base + seg * NA_PAIRS + p))

    return _call_hosting(
        body, hosted, name="na_bwd", grid=(NA_PAIRS, B),
        out_shape=(jax.ShapeDtypeStruct((B, 3, T, NA_WIDTH), BF16),
                   jax.ShapeDtypeStruct((NA_HEADS, 2 * NA_KH - 2, GRID_W, LANES), F32)),
        in_specs=[col(0), col(1), col(2),
                  pl.BlockSpec((2, 2 * NA_KH - 2, GRID_W, LANES), lambda p, b: (p, 0, 0, 0)),
                  pl.BlockSpec((None, N, LANES), lambda p, b: (b, 0, p))],
        out_specs=(pl.BlockSpec((None, 3, T, LANES), lambda p, b: (b, 0, 0, p)),
                   pl.BlockSpec((2, 2 * NA_KH - 2, GRID_W, LANES), lambda p, b: (p, 0, 0, 0))),
        scratch_shapes=[pltpu.VMEM((T, LANES), BF16)] * 2 + [pltpu.VMEM((2, T, LANES), F32)],
        args=(proj, proj, proj, bias2, dlat))


def _split3(a):
    hi = a.astype(BF16)
    r1 = a - hi.astype(F32)
    mid = r1.astype(BF16)
    lo = (r1 - mid.astype(F32)).astype(BF16)
    return hi, mid, lo


def _rpb_reduce(dbias2, onehot2):
    rows = dbias2.shape[0] * dbias2.shape[1]
    flat = dbias2.reshape(rows, GRID_W * LANES)

    def body(a_ref, oh_ref, o_ref):
        hi, mid, lo = _split3(a_ref[...])
        oh = oh_ref[...]
        o_ref[...] = _dot(hi, oh) + _dot(mid, oh) + _dot(lo, oh)

    return pl.pallas_call(
        body, name="rpb_reduce", out_shape=jax.ShapeDtypeStruct((rows, LANES), F32),
        in_specs=[_vmem(), _vmem()], out_specs=_vmem(),
        compiler_params=pltpu.CompilerParams(vmem_limit_bytes=VMEM_LIMIT),
    )(flat, onehot2)


def _dense_core(lat_ret, lat_na, x, tgt, modl, g_post_mix, g_pre_mlp, g_post_mlp, w_out, w1, w2):
    B, N, D = x.shape
    F = w1.shape[1]
    wout_rows, w1_cols, w2_rows = w_out.shape[0] // N_DEV, w1.shape[1] // N_DEV, w2.shape[0] // N_DEV
    mixw = w_out.shape[0]
    half = mixw // 2
    tm = _div_tile(N, 256, 16)
    nt = N // tm
    fc = _div_tile(F, 1024, LANES)

    def body(lr_ref, ln_ref, x_ref, t_ref, gt1_ref, sh2_ref, sc2_ref, gt2_ref, gpm_ref, gpre_ref, gpo_ref,
             wout_part, w1_part, w2_part,
             dy1_ref, dlr_ref, dln_ref, dmix_ref, h2_ref, a_ref, du_ref, dz_ref, red_ref, wout_hbm, w1_hbm, w2_hbm,
             wout_v, w1_v, w2_v, u_s, sems, fsend, frecv):
        @pl.when((pl.program_id(0) == 0) & (pl.program_id(1) == 0))
        def _():
            relay = [(_row_block(wout_part, wout_rows), _row_block(wout_hbm, wout_rows)),
                     (_col_block(w1_part, w1_cols), _col_block(w1_hbm, w1_cols)),
                     (_row_block(w2_part, w2_rows), _row_block(w2_hbm, w2_rows))]
            _forward_start(relay, fsend, frecv)
            _forward_wait(relay, fsend, frecv)
            cps = [pltpu.make_async_copy(wout_hbm, wout_v, sems.at[0]),
                   pltpu.make_async_copy(w1_hbm, w1_v, sems.at[1]),
                   pltpu.make_async_copy(w2_hbm, w2_v, sems.at[2])]
            for cp in cps:
                cp.start()
            for cp in cps:
                cp.wait()

        @pl.when(pl.program_id(1) == 0)
        def _():
            red_ref[...] = jnp.zeros(red_ref.shape, F32)

        gt1 = gt1_ref[...]
        sh2 = sh2_ref[...]
        sc2 = sc2_ref[...]
        gt2 = gt2_ref[...]
        gpm = gpm_ref[...]
        gpre = gpre_ref[...]
        gpo = gpo_ref[...]

        def rowmean(a):
            return jnp.mean(a, axis=-1, keepdims=True)

        def colsum(a):
            return jnp.sum(a, axis=0, keepdims=True)

        mix = _dot(lr_ref[...], wout_v[0:half, :]) + _dot(ln_ref[...], wout_v[half:, :])
        x = x_ref[...]
        rm = lax.rsqrt(rowmean(mix * mix) + NORM_EPS)
        mh = mix * rm
        nm = mh * gpm
        y1 = x + gt1 * nm
        r1 = lax.rsqrt(rowmean(y1 * y1) + NORM_EPS)
        xh = y1 * r1
        n1 = xh * gpre
        h2b = (n1 * (1.0 + sc2) + sh2).astype(BF16)
        h2_ref[...] = h2b
        z = jnp.zeros((tm, D), F32)
        for c0 in range(0, F, fc):
            u = _dot(h2b, w1_v[:, c0:c0 + fc])
            u_s[:, c0:c0 + fc] = u
            ru = jnp.maximum(u, 0.0)
            ab = (ru * ru).astype(BF16)
            a_ref[:, c0:c0 + fc] = ab
            z = z + _dot(ab, w2_v[c0:c0 + fc, :])
        r2 = lax.rsqrt(rowmean(z * z) + NORM_EPS)
        zh = z * r2
        n2 = zh * gpo
        y2 = y1 + gt2 * n2
        err = y2 - t_ref[...]
        loss = 0.5 * jnp.sum(rowmean(err * err))
        dy2 = err * (1.0 / D)
        red_ref[2:3, :] += colsum(dy2 * n2)
        dn2 = dy2 * gt2
        red_ref[6:7, :] += colsum(dn2 * zh)
        dzh = dn2 * gpo
        dz = r2 * (dzh - zh * rowmean(dzh * zh))
        dzb = dz.astype(BF16)
        dz_ref[...] = dzb
        dh2 = jnp.zeros((tm, D), F32)
        for c0 in range(0, F, fc):
            da = _dot_nt(dzb, w2_v[c0:c0 + fc, :])
            dub = (da * (2.0 * jnp.maximum(u_s[:, c0:c0 + fc], 0.0))).astype(BF16)
            du_ref[:, c0:c0 + fc] = dub
            dh2 = dh2 + _dot_nt(dub, w1_v[:, c0:c0 + fc])
        red_ref[3:4, :] += colsum(dh2 * n1)
        red_ref[4:5, :] += colsum(dh2)
        dn1 = dh2 * (1.0 + sc2)
        red_ref[5:6, :] += colsum(dn1 * xh)
        dxh = dn1 * gpre
        dy1 = dy2 + r1 * (dxh - xh * rowmean(dxh * xh))
        dy1_ref[...] = dy1
        red_ref[0:1, :] += colsum(dy1 * nm)
        dnm = dy1 * gt1
        red_ref[1:2, :] += colsum(dnm * mh)
        dmh = dnm * gpm
        dmix = (rm * (dmh - mh * rowmean(dmh * mh))).astype(BF16)
        dmix_ref[...] = dmix
        dlr_ref[...] = _dot_nt(dmix, wout_v[0:half, :])
        dln_ref[...] = _dot_nt(dmix, wout_v[half:, :])
        red_ref[7:8, :] += jnp.zeros((1, D), F32) + loss

    def tok(w):
        return pl.BlockSpec((None, tm, w), lambda b, t: (b, t, 0))

    def mod(k):
        return pl.BlockSpec((None, None, 1, D), lambda b, t, k=k: (b, k, 0, 0))

    def vec():
        return pl.BlockSpec((1, D), lambda b, t: (0, 0))

    return pl.pallas_call(
        body, name="dense_core", grid=(B, nt),
        out_shape=(jax.ShapeDtypeStruct((B, N, D), F32), jax.ShapeDtypeStruct((B, N, half), F32),
                   jax.ShapeDtypeStruct((B, N, half), F32), jax.ShapeDtypeStruct((B, N, D), BF16),
                   jax.ShapeDtypeStruct((B, N, D), BF16), jax.ShapeDtypeStruct((B, N, F), BF16),
                   jax.ShapeDtypeStruct((B, N, F), BF16), jax.ShapeDtypeStruct((B, N, D), BF16),
                   jax.ShapeDtypeStruct((B, SUBLANES, D), F32),
                   jax.ShapeDtypeStruct(w_out.shape, w_out.dtype), jax.ShapeDtypeStruct(w1.shape, w1.dtype),
                   jax.ShapeDtypeStruct(w2.shape, w2.dtype)),
        in_specs=[tok(half), tok(half), tok(D), tok(D), mod(2), mod(3), mod(4), mod(5), vec(), vec(), vec(),
                  _any(), _any(), _any()],
        out_specs=(tok(D), tok(half), tok(half), tok(D), tok(D), tok(F), tok(F), tok(D),
                   pl.BlockSpec((None, SUBLANES, D), lambda b, t: (b, 0, 0)), _any(), _any(), _any()),
        scratch_shapes=[pltpu.VMEM((mixw, D), BF16), pltpu.VMEM((D, F), BF16), pltpu.VMEM((F, D), BF16),
                        pltpu.VMEM((tm, F), F32), pltpu.SemaphoreType.DMA((3,)),
                        pltpu.SemaphoreType.DMA((3, 3)), pltpu.SemaphoreType.DMA((3, 3))],
        input_output_aliases={11: 9, 12: 10, 13: 11},
        compiler_params=_params("arbitrary", "arbitrary"),
    )(lat_ret, lat_na, x, tgt, modl, modl, modl, modl, g_post_mix, g_pre_mlp, g_post_mlp, w_out, w1, w2)[:9]


def _inproj_bwd(dret, dna, x, ctx, dy1, modl, g1, w_in_t, hosted):
    B, N, D = x.shape
    n_ctx = ctx.shape[1]
    T = n_ctx + N
    tm = _div_tile(n_ctx, 256, 16)
    nct, ctx_spec, lat_spec = _token_tiles(n_ctx, tm)
    nt = T // tm
    nseg_r = dret.shape[1]
    nseg_n = dna.shape[1]
    nw = w_in_t.shape[0]

    def body(*refs):
        seg_refs = refs[:nseg_r + nseg_n]
        c_ref, x_ref, dy1_ref, sc_ref, g_ref, w_ref, dx_ref, red_ref = refs[nseg_r + nseg_n:]
        t = pl.program_id(1)
        dh = jnp.zeros((tm, D), F32)
        for s, ref in enumerate(seg_refs):
            dh = dh + _dot(ref[...], w_ref[s * SEG:(s + 1) * SEG, :])
        x = jnp.where(t < nct, c_ref[...], x_ref[...])
        g = g_ref[...]
        r = lax.rsqrt(jnp.mean(x * x, axis=-1, keepdims=True) + NORM_EPS)
        xh = x * r
        red_ref[0:1, :] = jnp.sum(dh, axis=0, keepdims=True)
        red_ref[1:2, :] = jnp.sum(dh * (xh * g), axis=0, keepdims=True)
        dn = dh * (1.0 + sc_ref[...])
        red_ref[2:3, :] = jnp.sum(dn * xh, axis=0, keepdims=True)
        red_ref[3:, :] = jnp.zeros((SUBLANES - 3, D), F32)
        dxh = dn * g
        dx = r * (dxh - xh * jnp.mean(dxh * xh, axis=-1, keepdims=True))
        dx_ref[...] = dx + jnp.where(t >= nct, dy1_ref[...], 0.0)

    def mrow(b, t):
        return jnp.where(t < nct, B, b)

    def seg(s):
        return pl.BlockSpec((None, None, tm, SEG), lambda b, t, s=s: (b, s, t, 0))

    return _call_hosting(
        body, hosted, name="inproj_bwd", grid=(B, nt),
        out_shape=(jax.ShapeDtypeStruct((B, N, D), F32), jax.ShapeDtypeStruct((B, nt, SUBLANES, D), F32)),
        in_specs=[seg(s) for s in range(nseg_r)] + [seg(s) for s in range(nseg_n)]
                 + [ctx_spec(D), lat_spec(D), lat_spec(D),
                    pl.BlockSpec((None, None, 1, D), lambda b, t: (mrow(b, t), 1, 0, 0)),
                    pl.BlockSpec((1, D), lambda b, t: (0, 0)),
                    pl.BlockSpec((nw, D), lambda b, t: (0, 0))],
        out_specs=(lat_spec(D), pl.BlockSpec((None, None, SUBLANES, D), lambda b, t: (b, t, 0, 0))),
        scratch_shapes=[], args=(*([dret] * nseg_r), *([dna] * nseg_n), ctx, x, dy1, modl, g1, w_in_t))


def _tn_matmul(lhs, rhs, name, rows_before=0, rows_after=0, into=None):
    B, S, T, W = lhs.shape
    nn = rhs.shape[-1]
    tk = _div_tile(T, 2304, LANES)
    bm = _div_tile(W, 1024, LANES)
    bn = _div_tile(nn, 1024, LANES)
    nkt = T // tk
    nk = B * nkt

    def body(l_ref, r_ref, *rest):
        o_ref, acc = rest[-2:]
        k = pl.program_id(3)

        @pl.when(k == 0)
        def _():
            acc[...] = jnp.zeros(acc.shape, F32)

        acc[...] += _dot_tn(l_ref[...].astype(BF16), r_ref[...].astype(BF16))

        @pl.when(k == nk - 1)
        def _():
            o_ref[...] = acc[...].astype(BF16)

    nwb = W // bm
    first = rows_before // bm
    return pl.pallas_call(
        functools.partial(body), name=name, grid=(S, nwb, nn // bn, nk),
        out_shape=jax.ShapeDtypeStruct((rows_before + S * W + rows_after, nn), BF16),
        in_specs=[pl.BlockSpec((None, None, tk, bm), lambda s, i, j, k: (k // nkt, s, k % nkt, i)),
                  pl.BlockSpec((None, tk, bn), lambda s, i, j, k: (k // nkt, k % nkt, j))]
                 + ([] if into is None else [_any()]),
        out_specs=pl.BlockSpec((bm, bn), lambda s, i, j, k: (first + s * nwb + i, j)),
        scratch_shapes=[pltpu.VMEM((bm, bn), F32)],
        input_output_aliases={} if into is None else {2: 0},
        compiler_params=_params("parallel", "parallel", "parallel", "arbitrary"),
    )(lhs, rhs, *([] if into is None else [into]))


class _SplitScatter:
    def __init__(self, gs, block_ofs, land_shapes, name, kind="scatter", masks=ALL_PEERS):
        self.n = n = len(gs)
        self.block_ofs, self.kind, self.masks = block_ofs, kind, masks
        if kind == "scatter":
            land_shapes = [(N_DEV,) + tuple(bs) for bs in land_shapes]
        hbm = pl.BlockSpec(memory_space=pltpu.HBM)
        sem = pl.BlockSpec(memory_space=pltpu.SEMAPHORE)

        def body(*refs):
            g_refs, land_refs = refs[:n], refs[n:2 * n]
            send_sems, recv_sems, own_sems = refs[2 * n:2 * n + 3]
            token = refs[-1]
            for own, pushes in self._copies(g_refs, land_refs, send_sems, recv_sems, own_sems, landing="sender"):
                own.start()
                for cp in pushes:
                    cp.start()
            token[...] = jnp.zeros_like(token)

        outs = pl.pallas_call(
            body, name=name,
            out_shape=(pltpu.SemaphoreType.DMA((n * (N_DEV - 1),)), pltpu.SemaphoreType.DMA((n * (N_DEV - 1),)),
                       pltpu.SemaphoreType.DMA((n,)))
                      + tuple(pltpu.HBM(g.shape, g.dtype) for g in gs)
                      + tuple(pltpu.HBM(s, g.dtype) for s, g in zip(land_shapes, gs))
                      + (jax.ShapeDtypeStruct((SUBLANES, LANES), F32),),
            in_specs=(hbm,) * (2 * n), out_specs=(sem,) * 3 + (hbm,) * (2 * n) + (_vmem(),),
            input_output_aliases={k: 3 + k for k in range(2 * n)},
            compiler_params=pltpu.CompilerParams(has_side_effects=pltpu.SideEffectType.DATAFLOW_SIDE_EFFECTING),
        )(*[pltpu.with_memory_space_constraint(g, pltpu.HBM) for g in gs],
          *[pltpu.with_memory_space_constraint(lax.empty(s, g.dtype), pltpu.HBM) for s, g in zip(land_shapes, gs)])
        self.sems, self.thru, self.token = outs[:3], outs[3:3 + 2 * n], outs[-1]

    def _copies(self, g_refs, land_refs, send_sems, recv_sems, own_sems, landing):
        me, peers = _me_and_peers()
        out = []
        for k in range(self.n):
            if self.kind == "scatter":
                src, dst = self.block_ofs[k](g_refs[k]), _slot(land_refs[k])
            else:
                src, dst = (lambda p, k=k: g_refs[k]), self.block_ofs[k](land_refs[k])
            own = pltpu.make_async_copy(src(me), dst(me), own_sems.at[k])
            pushes = []
            for m in self.masks:
                dev, pid = peers[m - 1]
                i = k * (N_DEV - 1) + m - 1
                pushes.append(_remote(src(pid), dst(me if landing == "sender" else pid),
                                      send_sems.at[i], recv_sems.at[i], dev))
            out.append((own, pushes))
        return out


def _scatter_wait(scatters, after, name):
    hbm = pl.BlockSpec(memory_space=pltpu.HBM)
    sem = pl.BlockSpec(memory_space=pltpu.SEMAPHORE)
    n_arr = [2 * sc.n for sc in scatters]
    total = sum(n_arr)

    def body(*refs):
        arrs, sems = refs[:total], refs[total:total + 3 * len(scatters)]
        a0 = 0
        for j, sc in enumerate(scatters):
            g_refs, land_refs = arrs[a0:a0 + sc.n], arrs[a0 + sc.n:a0 + 2 * sc.n]
            a0 += 2 * sc.n
            send_sems, recv_sems, own_sems = sems[3 * j:3 * j + 3]
            for (own, sent), (_, got) in zip(sc._copies(g_refs, land_refs, send_sems, recv_sems, own_sems, "sender"),
                                             sc._copies(g_refs, land_refs, send_sems, recv_sems, own_sems, "receiver")):
                own.wait()
                for cp in sent:
                    cp.wait_send()
                for cp in got:
                    cp.wait_recv()

    operands = [a for sc in scatters for a in sc.thru]
    outs = pl.pallas_call(
        body, name=name,
        out_shape=tuple(pltpu.HBM(a.shape, a.dtype) for a in operands),
        in_specs=(hbm,) * total + (sem,) * (3 * len(scatters)) + (pl.BlockSpec(memory_space=pl.ANY),),
        out_specs=(hbm,) * total, input_output_aliases={k: k for k in range(total)},
        compiler_params=pltpu.CompilerParams(has_side_effects=pltpu.SideEffectType.DATAFLOW_SIDE_EFFECTING),
    )(*operands, *[s for sc in scatters for s in sc.sems], after)
    lands, a0 = [], 0
    for sc in scatters:
        lands.extend(outs[a0 + sc.n:a0 + 2 * sc.n])
        a0 += 2 * sc.n
    return lands


def _sum_slots(buf, name):
    _, rows, cols = buf.shape
    tr = _div_tile(rows, 256, 2 * SUBLANES)

    def body(b_ref, o_ref):
        acc = b_ref[0].astype(F32)
        for k in range(1, N_DEV):
            acc = acc + b_ref[k].astype(F32)
        o_ref[...] = acc

    return pl.pallas_call(
        functools.partial(body), name=name, grid=(rows // tr,),
        out_shape=jax.ShapeDtypeStruct((rows, cols), F32),
        in_specs=[pl.BlockSpec((N_DEV, tr, cols), lambda i: (0, i, 0))],
        out_specs=pl.BlockSpec((tr, cols), lambda i: (i, 0)),
        compiler_params=_params("parallel"),
    )(buf)


def _small_ar(vec, dmods, silu_all, w_ada, c_ctx):
    rv = vec.shape[0]
    D = silu_all.shape[1]
    ncol = w_ada.shape[1]
    nm = dmods.shape[1]
    srows = silu_all.shape[0]

    def body(vec_ref, dm_ref, s_ref, w_ref, cc_ref, tot_ref, gb_ref, gw_ref, gc_ref,
             vbuf, mbuf, tbuf, dmx, send1, recv1, send3, recv3):
        me, _ = _me_and_peers()
        vbuf[me] = vec_ref[...]
        mbuf[me] = dm_ref[...]
        both = [(lambda p: vbuf.at[me], lambda p: vbuf.at[p]), (lambda p: mbuf.at[me], lambda p: mbuf.at[p])]
        _push_start(both, ALL_PEERS, send1, recv1)
        _push_wait_recv(both, ALL_PEERS, send1, recv1)
        _push_wait_send(both, ALL_PEERS, send1, recv1)
        tot = vbuf[0]
        msum = mbuf[0]
        for k in range(1, N_DEV):
            tot = tot + vbuf[k]
            msum = msum + mbuf[k]
        tot_ref[...] = tot
        gb_ref[...] = jnp.sum(msum, axis=0, keepdims=True)
        loc = pl.ds(pl.multiple_of(me * ncol, ncol), ncol)
        for k in range(N_DEV):
            dmx[k * SUBLANES:(k + 1) * SUBLANES, :] = mbuf[k, :, loc]
        cm = msum[2:3, :]
        mbuf[0, 2:3, :] = cm
        cm_loc = mbuf[0, 2:3, loc]
        dmx[N_DEV * SUBLANES:, :] = jnp.concatenate([cm_loc, jnp.zeros((SUBLANES - 1, ncol), F32)], axis=0)
        gw_ref[...] = _dot_tn(s_ref[...], dmx[...])
        tbuf[me] = _dot_nt(dmx[N_DEV * SUBLANES:, :], w_ref[...])
        _exchange(lambda p: tbuf.at[me], lambda p: tbuf.at[p], send3, recv3)
        tsum = tbuf[0]
        for k in range(1, N_DEV):
            tsum = tsum + tbuf[k]
        cc = cc_ref[...]
        sg = _sigmoid(cc)
        gc_ref[...] = tsum[0:1, :] * (sg * (1.0 + cc * (1.0 - sg)))

    return pl.pallas_call(
        body, name="small_ar",
        out_shape=(jax.ShapeDtypeStruct((rv, LANES), F32), jax.ShapeDtypeStruct((1, nm), F32),
                   jax.ShapeDtypeStruct((D, ncol), F32), jax.ShapeDtypeStruct((1, D), F32)),
        in_specs=[_vmem()] * 5, out_specs=(_vmem(),) * 4,
        scratch_shapes=[pltpu.VMEM((N_DEV, rv, LANES), F32), pltpu.VMEM((N_DEV, SUBLANES, nm), F32),
                        pltpu.VMEM((N_DEV, SUBLANES, D), F32), pltpu.VMEM((srows, ncol), F32)]
                       + [pltpu.SemaphoreType.DMA((2, N_DEV - 1))] * 2 + [pltpu.SemaphoreType.DMA((N_DEV - 1,))] * 2,
        compiler_params=pltpu.CompilerParams(vmem_limit_bytes=VMEM_LIMIT),
    )(vec, dmods, silu_all, w_ada, c_ctx.reshape(1, D))


def _adam_update(w, g, m, v):
    mn = ADAM_B1 * m + (1.0 - ADAM_B1) * g
    vn = ADAM_B2 * v + (1.0 - ADAM_B2) * (g * g)
    m_hat = mn / (1.0 - ADAM_B1 ** ADAM_STEP)
    v_hat = vn / (1.0 - ADAM_B2 ** ADAM_STEP)
    return -ADAM_LR * (m_hat / (jnp.sqrt(v_hat) + ADAM_EPS) + ADAM_WD * w), mn, vn


def _adamw(w, g, m, v, name):
    rows, cols = w.shape
    tr = _div_tile(rows, 256, SUBLANES) if rows * cols > 65536 else rows

    def body(w_ref, g_ref, m_ref, v_ref, d_ref, nm_ref, nv_ref):
        d_ref[...], nm_ref[...], nv_ref[...] = _adam_update(w_ref[...], g_ref[...], m_ref[...], v_ref[...])

    spec = pl.BlockSpec((tr, cols), lambda i: (i, 0))
    return pl.pallas_call(
        functools.partial(body), name=name, grid=(rows // tr,),
        out_shape=(jax.ShapeDtypeStruct((rows, cols), F32),) * 3,
        in_specs=[spec] * 4, out_specs=(spec,) * 3,
        compiler_params=_params("parallel"),
    )(w, g, m, v)


def _adamw_small(items, name):
    n = len(items)

    def body(*refs):
        ins, outs = refs[:4 * n], refs[4 * n:]
        for i in range(n):
            w_ref, g_ref, m_ref, v_ref = ins[4 * i:4 * i + 4]
            outs[3 * i][...], outs[3 * i + 1][...], outs[3 * i + 2][...] = _adam_update(
                w_ref[...], g_ref[...], m_ref[...], v_ref[...])

    outs = pl.pallas_call(
        body, name=name,
        out_shape=tuple(jax.ShapeDtypeStruct(it[0].shape, F32) for it in items for _ in range(3)),
        in_specs=[_vmem()] * (4 * n), out_specs=(_vmem(),) * (3 * n),
        compiler_params=pltpu.CompilerParams(vmem_limit_bytes=VMEM_LIMIT),
    )(*[a for it in items for a in it])
    return [tuple(outs[3 * i:3 * i + 3]) for i in range(n)]


def _sum_adamw(buf, w, m, v, name):
    _, rows, cols = buf.shape
    tr = _div_tile(rows, 256, 2 * SUBLANES)

    def body(b_ref, w_ref, m_ref, v_ref, g_ref, d_ref, nm_ref, nv_ref):
        g = b_ref[0].astype(F32)
        for k in range(1, N_DEV):
            g = g + b_ref[k].astype(F32)
        g_ref[...] = g
        d_ref[...], nm_ref[...], nv_ref[...] = _adam_update(w_ref[...], g, m_ref[...], v_ref[...])

    spec = pl.BlockSpec((tr, cols), lambda i: (i, 0))
    return pl.pallas_call(
        functools.partial(body), name=name, grid=(rows // tr,),
        out_shape=(jax.ShapeDtypeStruct((rows, cols), F32),) * 4,
        in_specs=[pl.BlockSpec((N_DEV, tr, cols), lambda i: (0, i, 0))] + [spec] * 3, out_specs=(spec,) * 4,
        compiler_params=_params("parallel"),
    )(buf, w, m, v)


def _rope_tables(n_ctx, n):
    n_freq = RET_DIM // 4
    inv = np.float32(ROPE_BASE) ** (-np.arange(n_freq, dtype=np.float32) / np.float32(n_freq))
    tok = np.arange(n)
    pos_r = (tok // GRID_W).astype(np.float32)
    pos_c = (tok % GRID_W).astype(np.float32)
    ang_r = (pos_r[:, None] * inv[None, :]).astype(np.float32)
    ang_c = (pos_c[:, None] * inv[None, :]).astype(np.float32)
    cos = np.concatenate([np.cos(ang_r), np.cos(ang_r), np.cos(ang_c), np.cos(ang_c)], axis=-1)
    sin = np.concatenate([-np.sin(ang_r), np.sin(ang_r), -np.sin(ang_c), np.sin(ang_c)], axis=-1)
    cos = np.concatenate([np.ones((n_ctx, RET_DIM), np.float32), cos], axis=0)
    sin = np.concatenate([np.zeros((n_ctx, RET_DIM), np.float32), sin], axis=0)
    return jnp.asarray(cos, F32), jnp.asarray(sin, F32)


def _na_tables():
    q = np.arange(GRID_W)[:, None]
    k = np.arange(GRID_W)[None, :]
    start = np.clip(q - NA_KW // 2, 0, GRID_W - NA_KW)
    valid = (k >= start) & (k < start + NA_KW)
    dc = np.clip(k - q + (NA_KW - 1), 0, 2 * NA_KW - 2)
    ncls = 2 * NA_KW - 1
    onehot = (dc[None] == np.arange(ncls)[:, None, None]) & valid[None]
    oh2 = np.zeros((GRID_W, LANES, LANES), np.float32)
    for c in range(ncls):
        oh2[:, :GRID_W, c] = onehot[c]
        oh2[:, GRID_W:, 32 + c] = onehot[c]
    return onehot.astype(np.float32), valid, oh2.reshape(GRID_W * LANES, LANES)


def _paired_bias(rpb, onehot, valid):
    t = jnp.einsum("hdc,cqk->hdqk", rpb, jnp.asarray(onehot), precision=lax.Precision.HIGHEST)
    t = jnp.where(jnp.asarray(valid)[None, None], t, NEG_INF)
    return jnp.concatenate([t[:, :-1], t[:, 1:]], axis=-1)


def kernel(x, c, ctx, c_ctx, w_ada, b_ada, g_pre_mix, g_post_mix, g_pre_mlp, g_post_mlp, w_in, ret_decay, ret_gn, na_rpb, w_out, w_mlp1, w_mlp2, loss_target, m_c_ctx, m_w_ada, m_b_ada, m_g_pre_mix, m_g_post_mix, m_g_pre_mlp, m_g_post_mlp, m_w_in, m_ret_decay, m_ret_gn, m_na_rpb, m_w_out, m_w_mlp1, m_w_mlp2, v_c_ctx, v_w_ada, v_b_ada, v_g_pre_mix, v_g_post_mix, v_g_pre_mlp, v_g_post_mlp, v_w_in, v_ret_decay, v_ret_gn, v_na_rpb, v_w_out, v_w_mlp1, v_w_mlp2):
    B, N, D = x.shape
    C = ctx.shape[1]
    T = C + N

    silu_all, mods_g, win_b, wout_l, w1_l, w2_l = _mod_gather(c, c_ctx, w_ada[0], b_ada, w_in[0].T, w_out[0],
                                                             w_mlp1[0], w_mlp2[0])
    mods_mine = mods_g.transpose(1, 0, 2).reshape(mods_g.shape[1], N_MOD * D)
    modl = jnp.concatenate([mods_mine[:B], mods_mine[SUBLANES:SUBLANES + 1]], axis=0)
    modl = modl.reshape(B + 1, N_MOD, 1, D)
    rin = w_in.shape[2]
    rout, c1, r2 = wout_l.shape[0], w1_l.shape[1], w2_l.shape[0]

    def rows_of(n):
        return lambda ref: _row_block(ref, n)

    def cols_of(n):
        return lambda ref: _col_block(ref, n)

    cos, sin = _rope_tables(C, N)
    onehot, valid, oh2 = _na_tables()
    bias2 = _paired_bias(na_rpb[0], onehot, valid)
    lg = jax.nn.log_sigmoid(ret_decay[0].astype(F32))

    ag = _SplitScatter([wout_l, w1_l, w2_l], [rows_of(rout), cols_of(c1), rows_of(r2)],
                       [(N_DEV * rout, D), (D, N_DEV * c1), (N_DEV * r2, D)], "ag_mlp_start",
                       kind="gather", masks=SIBLING + ICI_SAME_CORE)
    h_all, proj = _inproj_fwd(x, ctx, modl, g_pre_mix + ag.token[0, 0], win_b, [])
    o_ret, lat_ret = _ret_fwd(proj, cos, sin, lg, ret_gn, C, [])
    (lat_na,) = _na_fwd(proj, bias2, C, [])
    wout_part, w1_part, w2_part = _scatter_wait([ag], lat_na, "ag_mlp_wait")

    (dy1, dlat_ret, dlat_na, dmix, h2, act, du, dz, red_d) = _dense_core(
        lat_ret, lat_na, x, loss_target, modl, g_post_mix, g_pre_mlp, g_post_mlp, wout_part, w1_part, w2_part)

    gw_out_p = _tn_matmul(lat_ret[:, None], dmix, "gw_out_ret", rows_after=lat_na.shape[-1])
    gw_out_p = _tn_matmul(lat_na[:, None], dmix, "gw_out_na", rows_before=lat_ret.shape[-1], into=gw_out_p)
    gw1_p = _tn_matmul(h2[:, None], du, "gw_mlp1")
    gw2_p = _tn_matmul(act[:, None], dz, "gw_mlp2")
    rs_mlp = _SplitScatter([gw_out_p, gw1_p, gw2_p], [rows_of(rout), cols_of(c1), rows_of(r2)],
                           [(rout, D), (D, c1), (r2, D)], "rs_mlp_start")

    dret, dgn_p, dlg_p = _ret_bwd(proj, cos, sin, lg, ret_gn + rs_mlp.token[0, 0], o_ret, dlat_ret, C, [])
    dna, dbias2 = _na_bwd(proj, bias2, dlat_na, C, [])
    ret_cols, na_cols = dret.shape[1] * dret.shape[3], dna.shape[1] * dna.shape[3]
    gwin_t_p = _tn_matmul(dret, h_all, "gw_in_ret", rows_after=na_cols)
    gwin_t_p = _tn_matmul(dna, h_all, "gw_in_na", rows_before=ret_cols, into=gwin_t_p)
    rs_in = _SplitScatter([gwin_t_p], [rows_of(rin)], [(rin, D)], "rs_w_in_start")
    grad_x, red_i = _inproj_bwd(dret, dna, x, ctx, dy1, modl, g_pre_mix + rs_in.token[0, 0], win_b, [])

    rd = red_d
    nct = red_i.shape[1] * C // T
    ri_ctx = red_i[:, :nct].sum(axis=(0, 1))
    ri_lat = red_i[:, nct:].sum(axis=1)
    d_mods = jnp.concatenate([ri_lat[:, 0], ri_lat[:, 1], rd[:, 0], rd[:, 4], rd[:, 3], rd[:, 2]], axis=-1)
    d_cmods = jnp.concatenate([ri_ctx[0], ri_ctx[1], jnp.zeros(((N_MOD - 2) * D,), F32)])[None]
    dm_slot = jnp.concatenate([d_mods, d_cmods, jnp.zeros((SUBLANES - B - 1, N_MOD * D), F32)], axis=0)
    dg_pre_mix = ri_lat[:, 2].sum(axis=0) + ri_ctx[2]
    dg_post_mix = rd[:, 1].sum(axis=0)
    dg_pre_mlp = rd[:, 5].sum(axis=0)
    dg_post_mlp = rd[:, 6].sum(axis=0)
    loss_p = rd[:, 7, 0].sum()
    d_gn = dgn_p[:, 0].sum(axis=0)
    d_lg = dlg_p[:, :, :2, 0].sum(axis=0).T
    d_decay = d_lg * jax.nn.sigmoid(-ret_decay[0].astype(F32))
    rr = _rpb_reduce(dbias2, jnp.asarray(oh2, BF16)).reshape(NA_HEADS, 2 * NA_KH - 2, LANES)
    ncls = 2 * NA_KW - 1
    d_rpb = (jnp.pad(rr[:, :, :ncls], ((0, 0), (0, 1), (0, 0))) + jnp.pad(rr[:, :, 32:32 + ncls], ((0, 0), (1, 0), (0, 0))))
    d_rpb32 = jnp.pad(d_rpb, ((0, 0), (0, 0), (0, 32 - ncls)))
    pieces = [dg_pre_mix, dg_post_mix, dg_pre_mlp, dg_post_mlp, d_gn, d_rpb32.reshape(-1),
              jnp.pad(d_decay.reshape(-1), (0, LANES - d_decay.size)), jnp.full((LANES,), loss_p, F32)]
    vec = jnp.concatenate(pieces)
    pad = (-vec.shape[0]) % (SUBLANES * LANES)
    vec = jnp.pad(vec, (0, pad)).reshape(-1, LANES)
    tot, g_b_ada, g_w_ada, g_c_ctx = _small_ar(vec, dm_slot, silu_all, w_ada[0], c_ctx)
    land_out, land_1, land_2, land_in = _scatter_wait([rs_mlp, rs_in], tot, "rs_wait")
    g_w_in = _sum_slots(land_in, "sum_w_in").T
    fused = {"w_out": _sum_adamw(land_out, w_out[0], m_w_out[0], v_w_out[0], "sum_adamw_w_out"),
             "w_mlp1": _sum_adamw(land_1, w_mlp1[0], m_w_mlp1[0], v_w_mlp1[0], "sum_adamw_w_mlp1"),
             "w_mlp2": _sum_adamw(land_2, w_mlp2[0], m_w_mlp2[0], v_w_mlp2[0], "sum_adamw_w_mlp2")}
    flat = tot.reshape(-1)
    o0 = 0
    g_pre_mix_g = flat[o0:o0 + D]; o0 += D
    g_post_mix_g = flat[o0:o0 + D]; o0 += D
    g_pre_mlp_g = flat[o0:o0 + D]; o0 += D
    g_post_mlp_g = flat[o0:o0 + D]; o0 += D
    g_gn = flat[o0:o0 + RET_WIDTH]; o0 += RET_WIDTH
    nrpb = NA_HEADS * (2 * NA_KH - 1) * 32
    g_rpb = flat[o0:o0 + nrpb].reshape(NA_HEADS, 2 * NA_KH - 1, 32)[:, :, :ncls]; o0 += nrpb
    g_decay = flat[o0:o0 + 2 * RET_HEADS].reshape(2, RET_HEADS); o0 += LANES
    loss = flat[o0]

    grads = {
        "c_ctx": g_c_ctx.reshape(c_ctx.shape), "w_ada": g_w_ada[None], "b_ada": g_b_ada.reshape(b_ada.shape),
        "g_pre_mix": g_pre_mix_g[None], "g_post_mix": g_post_mix_g[None], "g_pre_mlp": g_pre_mlp_g[None],
        "g_post_mlp": g_post_mlp_g[None], "w_in": g_w_in[None], "ret_decay": g_decay[None], "ret_gn": g_gn[None],
        "na_rpb": g_rpb[None], "w_out": fused["w_out"][0][None], "w_mlp1": fused["w_mlp1"][0][None],
        "w_mlp2": fused["w_mlp2"][0][None],
    }
    weights = dict(c_ctx=c_ctx, w_ada=w_ada, b_ada=b_ada, g_pre_mix=g_pre_mix, g_post_mix=g_post_mix,
                   g_pre_mlp=g_pre_mlp, g_post_mlp=g_post_mlp, w_in=w_in, ret_decay=ret_decay, ret_gn=ret_gn,
                   na_rpb=na_rpb, w_out=w_out, w_mlp1=w_mlp1, w_mlp2=w_mlp2)
    m_in = dict(c_ctx=m_c_ctx, w_ada=m_w_ada, b_ada=m_b_ada, g_pre_mix=m_g_pre_mix, g_post_mix=m_g_post_mix,
                g_pre_mlp=m_g_pre_mlp, g_post_mlp=m_g_post_mlp, w_in=m_w_in, ret_decay=m_ret_decay,
                ret_gn=m_ret_gn, na_rpb=m_na_rpb, w_out=m_w_out, w_mlp1=m_w_mlp1, w_mlp2=m_w_mlp2)
    v_in = dict(c_ctx=v_c_ctx, w_ada=v_w_ada, b_ada=v_b_ada, g_pre_mix=v_g_pre_mix, g_post_mix=v_g_post_mix,
                g_pre_mlp=v_g_pre_mlp, g_post_mlp=v_g_post_mlp, w_in=v_w_in, ret_decay=v_ret_decay,
                ret_gn=v_ret_gn, na_rpb=v_na_rpb, w_out=v_w_out, w_mlp1=v_w_mlp1, w_mlp2=v_w_mlp2)
    names = list(weights)
    deltas, new_m, new_v = {}, {}, {}
    def as_2d(n):
        shp = weights[n].shape
        two_d = (-1, shp[-1]) if len(shp) > 1 else (1, shp[0])
        return [a.reshape(two_d) for a in (weights[n], grads[n], m_in[n], v_in[n])]

    small = [n for n in names if n not in fused and weights[n].size <= 65536]
    updated = dict(zip(small, _adamw_small([as_2d(n) for n in small], "adamw_small")))
    for n in names:
        if n in fused:
            updated[n] = fused[n][1:]
        elif n not in updated:
            updated[n] = _adamw(*as_2d(n), "adamw_" + n)
        deltas[n], new_m[n], new_v[n] = (a.reshape(weights[n].shape) for a in updated[n])
    return (loss, grad_x, *[grads[n] for n in names], *[deltas[n] for n in names],
            *[new_m[n] for n in names], *[new_v[n] for n in names])
```

```python
import functools
import math

import numpy as np
import jax
import jax.numpy as jnp
from jax import lax
from jax.experimental import pallas as pl
from jax.experimental.pallas import tpu as pltpu

F32 = jnp.float32
BF16 = jnp.bfloat16
MESH = pl.DeviceIdType.MESH

N_DEV = 8
LANES = 128
SUBLANES = 8
VMEM_LIMIT = 60 * 1024 * 1024

GRID_W = 64
RET_HEADS = 4
RET_DIM = 128
RET_WIDTH = RET_HEADS * RET_DIM
NA_HEADS = 8
NA_DIM = 64
NA_WIDTH = NA_HEADS * NA_DIM
NA_PAIRS = NA_HEADS // 2
NA_KH = 8
NA_KW = 16
NA_GROUP = 8
SEG = 512
ROPE_BASE = 10000.0
NORM_EPS = 1e-6
NEG_INF = -1e30
N_MOD = 6

ADAM_LR = 0.001
ADAM_B1 = 0.9
ADAM_B2 = 0.999
ADAM_EPS = 1e-08
ADAM_WD = 0.01
ADAM_STEP = 10


def _dot(a, b):
    return lax.dot_general(a, b, (((1,), (0,)), ((), ())), preferred_element_type=F32)


def _dot_nt(a, b):
    return lax.dot_general(a, b, (((1,), (1,)), ((), ())), preferred_element_type=F32)


def _dot_tn(a, b):
    return lax.dot_general(a, b, (((0,), (0,)), ((), ())), preferred_element_type=F32)


def _sigmoid(x):
    return 1.0 / (1.0 + jnp.exp(-x))


def _div_tile(n, cap, mult):
    if n <= cap:
        return n
    for t in range(cap - cap % mult, 0, -mult):
        if n % t == 0:
            return t
    raise ValueError(f"no tile for {n}")


def _params(*sem):
    return pltpu.CompilerParams(dimension_semantics=tuple(sem) if sem else None,
                                vmem_limit_bytes=VMEM_LIMIT)


def _vmem():
    return pl.BlockSpec(memory_space=pltpu.VMEM)


def _any():
    return pl.BlockSpec(memory_space=pl.ANY)


def _me_and_peers():
    x, y, c = lax.axis_index("x"), lax.axis_index("y"), lax.axis_index("c")
    me = 4 * x + 2 * y + c
    peers = []
    for m in range(1, N_DEV):
        px = 1 - x if (m >> 2) & 1 else x
        py = 1 - y if (m >> 1) & 1 else y
        pc = 1 - c if m & 1 else c
        peers.append(((px, py, pc), 4 * px + 2 * py + pc))
    return me, peers


def _exchange(src_for, dst_from, send_sems, recv_sems):
    me, peers = _me_and_peers()
    sent = []
    for i, (dev, pid) in enumerate(peers):
        cp = pltpu.make_async_remote_copy(src_ref=src_for(pid), dst_ref=dst_from(me),
                                          send_sem=send_sems.at[i], recv_sem=recv_sems.at[i],
                                          device_id=dev, device_id_type=MESH)
        cp.start()
        sent.append(cp)
    for i, (dev, pid) in enumerate(peers):
        pltpu.make_async_remote_copy(src_ref=src_for(pid), dst_ref=dst_from(pid),
                                     send_sem=send_sems.at[i], recv_sem=recv_sems.at[i],
                                     device_id=dev, device_id_type=MESH).wait_recv()
    for cp in sent:
        cp.wait_send()


SIBLING = (1,)
ICI_SAME_CORE = (2, 4, 6)
ALL_PEERS = tuple(range(1, N_DEV))


def _remote(src, dst, send_sem, recv_sem, dev):
    return pltpu.make_async_remote_copy(src_ref=src, dst_ref=dst, send_sem=send_sem, recv_sem=recv_sem,
                                        device_id=dev, device_id_type=MESH)


def _push_start(items, masks, send_sems, recv_sems):
    me, peers = _me_and_peers()
    for k, (src_for, dst_from) in enumerate(items):
        for m in masks:
            dev, pid = peers[m - 1]
            _remote(src_for(pid), dst_from(me), send_sems.at[k, m - 1], recv_sems.at[k, m - 1], dev).start()


def _push_wait_recv(items, masks, send_sems, recv_sems):
    me, peers = _me_and_peers()
    for k, (src_for, dst_from) in enumerate(items):
        for m in masks:
            dev, pid = peers[m - 1]
            _remote(src_for(pid), dst_from(pid), send_sems.at[k, m - 1], recv_sems.at[k, m - 1], dev).wait_recv()


def _push_wait_send(items, masks, send_sems, recv_sems):
    me, peers = _me_and_peers()
    for k, (src_for, dst_from) in enumerate(items):
        for m in masks:
            dev, pid = peers[m - 1]
            _remote(src_for(pid), dst_from(me), send_sems.at[k, m - 1], recv_sems.at[k, m - 1], dev).wait_send()


def _forward_start(items, send_sems, recv_sems):
    me, peers = _me_and_peers()
    sib = peers[0][0]
    for k, (blk_in, blk_out) in enumerate(items):
        for j, m in enumerate(ICI_SAME_CORE):
            pid = peers[m - 1][1]
            _remote(blk_in(pid), blk_out(pid), send_sems.at[k, j], recv_sems.at[k, j], sib).start()


def _forward_wait(items, send_sems, recv_sems):
    me, peers = _me_and_peers()
    sib = peers[0][0]
    for k, (blk_in, blk_out) in enumerate(items):
        for j, m in enumerate(ICI_SAME_CORE):
            got = peers[(m | 1) - 1][1]
            _remote(blk_in(got), blk_out(got), send_sems.at[k, j], recv_sems.at[k, j], sib).wait_recv()
    for k, (blk_in, blk_out) in enumerate(items):
        for j, m in enumerate(ICI_SAME_CORE):
            pid = peers[m - 1][1]
            _remote(blk_in(pid), blk_out(pid), send_sems.at[k, j], recv_sems.at[k, j], sib).wait_send()


def _mod_gather(c, c_ctx, w_ada, b_ada, w_in_t, w_out, w1, w2):
    B, D = c.shape
    ncol = w_ada.shape[1]
    rows = SUBLANES * N_DEV + SUBLANES

    def body(c_ref, cc_ref, w_ref, b_ref, win_ref, wout_ref, w1_ref, w2_ref,
             s_ref, m_ref, gin_ref, wout_b, w1_b, w2_b,
             win_b, msend, send1, recv1, send2, recv2, wsend, wrecv, fsend, frecv, lsem):
        me, _ = _me_and_peers()
        win_b[...] = win_ref[...].astype(BF16)
        block = _row_block(gin_ref, w_in_t.shape[0])
        gather = [(lambda p: win_b, block)]
        own = pltpu.make_async_copy(win_b, block(me), lsem.at[0])
        cv = c_ref[...]
        slot = jnp.concatenate([cv * _sigmoid(cv), jnp.zeros((SUBLANES - B, D), F32)], axis=0)
        my_rows = pl.ds(pl.multiple_of(me * SUBLANES, SUBLANES), SUBLANES)
        s_ref[my_rows, :] = slot
        ccv = cc_ref[...]
        s_ref[SUBLANES * N_DEV:, :] = jnp.concatenate(
            [ccv * _sigmoid(ccv), jnp.zeros((SUBLANES - 1, D), F32)], axis=0)

        def rows_of(p):
            return s_ref.at[pl.ds(pl.multiple_of(p * SUBLANES, SUBLANES), SUBLANES), :]

        _exchange(lambda p: rows_of(me), rows_of, send1, recv1)
        own.start()
        _push_start(gather, SIBLING + ICI_SAME_CORE, wsend, wrecv)
        wout_b[...] = wout_ref[...].astype(BF16)
        w1_b[...] = w1_ref[...].astype(BF16)
        w2_b[...] = w2_ref[...].astype(BF16)
        b_loc = b_ref[:, pl.ds(pl.multiple_of(me * ncol, ncol), ncol)]
        mods = _dot(s_ref[...], w_ref[...]) + b_loc
        for p in range(N_DEV):
            msend[p] = jnp.concatenate([mods[p * SUBLANES:(p + 1) * SUBLANES], mods[N_DEV * SUBLANES:]], axis=0)
        m_ref[me] = msend[me]
        columns = [(lambda p: msend.at[p], lambda p: m_ref.at[p])]
        _push_start(columns, ALL_PEERS, send2, recv2)
        _push_wait_recv(gather, ICI_SAME_CORE, wsend, wrecv)
        relay = [(block, block)]
        _forward_start(relay, fsend, frecv)
        _push_wait_recv(columns, ALL_PEERS, send2, recv2)
        _push_wait_recv(gather, SIBLING, wsend, wrecv)
        _forward_wait(relay, fsend, frecv)
        _push_wait_send(columns, ALL_PEERS, send2, recv2)
        _push_wait_send(gather, SIBLING + ICI_SAME_CORE, wsend, wrecv)
        own.wait()

    return pl.pallas_call(
        body, name="mod_gather",
        out_shape=(jax.ShapeDtypeStruct((rows, D), F32), jax.ShapeDtypeStruct((N_DEV, 2 * SUBLANES, ncol), F32),
                   jax.ShapeDtypeStruct((N_DEV * w_in_t.shape[0], D), BF16),
                   jax.ShapeDtypeStruct(w_out.shape, BF16), jax.ShapeDtypeStruct(w1.shape, BF16),
                   jax.ShapeDtypeStruct(w2.shape, BF16)),
        in_specs=[_vmem()] * 8, out_specs=(_vmem(), _vmem(), _any(), _vmem(), _vmem(), _vmem()),
        scratch_shapes=[pltpu.VMEM(w_in_t.shape, BF16), pltpu.VMEM((N_DEV, 2 * SUBLANES, ncol), F32)]
                       + [pltpu.SemaphoreType.DMA((N_DEV - 1,))] * 2
                       + [pltpu.SemaphoreType.DMA((1, N_DEV - 1))] * 4 + [pltpu.SemaphoreType.DMA((1, 3))] * 2
                       + [pltpu.SemaphoreType.DMA((1,))],
        compiler_params=pltpu.CompilerParams(vmem_limit_bytes=VMEM_LIMIT),
    )(c, c_ctx.reshape(1, D), w_ada, b_ada, w_in_t, w_out, w1, w2)


def _row_block(ref, rows):
    return lambda p: ref.at[pl.ds(pl.multiple_of(p * rows, 2 * SUBLANES), rows), :]


def _col_block(ref, cols):
    return lambda p: ref.at[:, pl.ds(pl.multiple_of(p * cols, LANES), cols)]


def _slot(ref):
    return lambda p: ref.at[p]


def _grid_call(body, *, name, grid, out_shape, in_specs, out_specs, scratch_shapes, args):
    return pl.pallas_call(
        body, name=name, grid=grid, out_shape=tuple(out_shape), in_specs=list(in_specs), out_specs=tuple(out_specs),
        scratch_shapes=list(scratch_shapes), compiler_params=_params(*(("arbitrary",) * len(grid))),
    )(*args)


def _token_tiles(n_ctx, tm):
    nct = n_ctx // tm

    def ctx_spec(D):
        return pl.BlockSpec((None, tm, D), lambda b, t: (b, jnp.minimum(t, nct - 1), 0))

    def lat_spec(D):
        return pl.BlockSpec((None, tm, D), lambda b, t: (b, jnp.maximum(t - nct, 0), 0))

    return nct, ctx_spec, lat_spec


def _inproj_fwd(x, ctx, modl, g1, w_in_t):
    B, N, D = x.shape
    n_ctx = ctx.shape[1]
    T = n_ctx + N
    nw = w_in_t.shape[0]
    tm = _div_tile(n_ctx, 256, 16)
    nct, ctx_spec, lat_spec = _token_tiles(n_ctx, tm)

    def body(c_ref, x_ref, sh_ref, sc_ref, g_ref, w_ref, h_ref, p_ref):
        x = jnp.where(pl.program_id(1) < nct, c_ref[...], x_ref[...])
        r = lax.rsqrt(jnp.mean(x * x, axis=-1, keepdims=True) + NORM_EPS)
        h = ((x * r) * g_ref[...]) * (1.0 + sc_ref[...]) + sh_ref[...]
        hb = h.astype(BF16)
        h_ref[...] = hb
        p_ref[...] = _dot_nt(hb, w_ref[...]).astype(BF16)

    def mrow(b, t):
        return jnp.where(t < nct, B, b)

    return _grid_call(
        body, name="inproj_fwd", grid=(B, T // tm),
        out_shape=(jax.ShapeDtypeStruct((B, T, D), BF16), jax.ShapeDtypeStruct((B, T, nw), BF16)),
        in_specs=[ctx_spec(D), lat_spec(D),
                  pl.BlockSpec((None, None, 1, D), lambda b, t: (mrow(b, t), 0, 0, 0)),
                  pl.BlockSpec((None, None, 1, D), lambda b, t: (mrow(b, t), 1, 0, 0)),
                  pl.BlockSpec((1, D), lambda b, t: (0, 0)),
                  pl.BlockSpec((nw, D), lambda b, t: (0, 0))],
        out_specs=(pl.BlockSpec((None, tm, D), lambda b, t: (b, t, 0)),
                   pl.BlockSpec((None, tm, nw), lambda b, t: (b, t, 0))),
        scratch_shapes=[], args=(ctx, x, modl, modl, g1, w_in_t))


def _swap32(x):
    lane = lax.broadcasted_iota(jnp.int32, x.shape, 1)
    return jnp.where((lane % 64) < 32, pltpu.roll(x, 96, 1), pltpu.roll(x, 32, 1))


def _rope(x, cos, sin):
    return x * cos + _swap32(x) * sin


def _unrope(dy, cos, sin):
    return dy * cos + _swap32(dy * sin)


def _ret_weights(lgf, lgb, dist):
    return jnp.exp(jnp.where(dist >= 0.0, lgf * dist, -lgb * dist))


class _RetDecay:
    def __init__(self, lgf, lgb, rows):
        r = lax.broadcasted_iota(jnp.int32, (rows, RET_DIM), 0).astype(F32)
        self.head = r + 1.0
        self.tail = (rows - 1.0) - r
        self.q_f = jnp.exp(lgf * self.head)
        self.k_f = jnp.exp(lgf * self.tail)
        self.q_b = jnp.exp(lgb * self.tail)
        self.k_b = jnp.exp(lgb * self.head)


def _ret_states(kf32, vs, lgf, lgb, C, c, nt, hf, hb, hfa=None, hba=None):
    dec = _RetDecay(lgf, lgb, c)
    dec_c = _RetDecay(lgf, lgb, C)
    step_f = jnp.exp(jnp.zeros((RET_DIM, RET_DIM), F32) + lgf * c)
    step_b = jnp.exp(jnp.zeros((RET_DIM, RET_DIM), F32) + lgb * c)

    def upd(rows, kdec):
        return _dot_tn((kf32[rows, :] * kdec).astype(BF16), vs[rows, :])

    def lat(t):
        return slice(C + t * c, C + (t + 1) * c)

    state = upd(slice(0, C), dec_c.k_f)
    aged = jnp.zeros_like(state)
    for t in range(nt):
        hf[t] = state.astype(BF16)
        if hfa is not None:
            hfa[t] = aged
        if t < nt - 1:
            aged = step_f * (aged + c * state)
            state = step_f * state + upd(lat(t), dec.k_f)
    state = upd(slice(0, C), dec_c.k_b)
    aged = jnp.zeros_like(state)
    for t in range(nt - 1, -1, -1):
        hb[t] = state.astype(BF16)
        if hba is not None:
            hba[t] = aged
        if t > 0:
            aged = step_b * (aged + c * state)
            state = step_b * state + upd(lat(t), dec.k_b)
    return dec, dec_c, step_f, step_b


def _ret_fwd(proj, cos, sin, lg, gn, n_ctx):
    B, T, _ = proj.shape
    C = n_ctx
    N = T - C
    c = _div_tile(N, 256, 16)
    nt = N // c
    scale = RET_DIM ** -0.5

    def body(lg_ref, q_ref, k_ref, v_ref, g_ref, cos_ref, sin_ref, gn_ref, o_ref, lat_ref, qs, ks, vs, kf32, hf, hb):
        h = pl.program_id(1)
        lgf = lg_ref[0, h]
        lgb = lg_ref[1, h]
        for rows in [slice(0, C)] + [slice(C + t * c, C + (t + 1) * c) for t in range(nt)]:
            cosb = cos_ref[rows, :]
            sinb = sin_ref[rows, :]
            qs[rows, :] = (_rope(q_ref[rows, :].astype(F32), cosb, sinb) * scale).astype(BF16)
            kr = _rope(k_ref[rows, :].astype(F32), cosb, sinb)
            kf32[rows, :] = kr
            ks[rows, :] = kr.astype(BF16)
            vs[rows, :] = v_ref[rows, :].astype(BF16)
        gnv = gn_ref[...]
        dec, _, _, _ = _ret_states(kf32, vs, lgf, lgb, C, c, nt, hf, hb)
        rc = (lax.broadcasted_iota(jnp.int32, (c, c), 0) - lax.broadcasted_iota(jnp.int32, (c, c), 1)).astype(F32)
        w_diag = _ret_weights(lgf, lgb, rc)
        for t in range(nt):
            rows = slice(C + t * c, C + (t + 1) * c)
            qt = qs[rows, :]
            s = _dot_nt(qt, ks[rows, :])
            o = (_dot((s * w_diag).astype(BF16), vs[rows, :])
                 + dec.q_f * _dot(qt, hf[t]) + dec.q_b * _dot(qt, hb[t]))
            o_ref[t * c:(t + 1) * c, :] = o
            mu = jnp.mean(o, axis=-1, keepdims=True)
            oc = o - mu
            var = jnp.mean(oc * oc, axis=-1, keepdims=True)
            yh = oc * lax.rsqrt(var + NORM_EPS)
            g = g_ref[rows, :].astype(F32)
            lat_ref[t * c:(t + 1) * c, :] = ((yh * gnv) * (g * _sigmoid(g))).astype(BF16)

    def col(seg):
        return pl.BlockSpec((None, T, RET_DIM), lambda b, h, seg=seg: (b, 0, seg * RET_HEADS + h))

    return _grid_call(
        body, name="ret_fwd", grid=(B, RET_HEADS),
        out_shape=(jax.ShapeDtypeStruct((B, N, RET_WIDTH), F32), jax.ShapeDtypeStruct((B, N, RET_WIDTH), BF16)),
        in_specs=[pl.BlockSpec(memory_space=pltpu.SMEM), col(0), col(1), col(2), col(3),
                  pl.BlockSpec((T, RET_DIM), lambda b, h: (0, 0)), pl.BlockSpec((T, RET_DIM), lambda b, h: (0, 0)),
                  pl.BlockSpec((1, RET_DIM), lambda b, h: (0, h))],
        out_specs=(pl.BlockSpec((None, N, RET_DIM), lambda b, h: (b, 0, h)),
                   pl.BlockSpec((None, N, RET_DIM), lambda b, h: (b, 0, h))),
        scratch_shapes=[pltpu.VMEM((T, RET_DIM), BF16)] * 3 + [pltpu.VMEM((T, RET_DIM), F32)]
                       + [pltpu.VMEM((nt, RET_DIM, RET_DIM), BF16)] * 2,
        args=(lg, proj, proj, proj, proj, cos, sin, gn))


def _ret_bwd(proj, cos, sin, lg, gn, o, dlat, n_ctx):
    B, T, _ = proj.shape
    C = n_ctx
    N = T - C
    c = _div_tile(N, 256, 16)
    nt = N // c
    scale = RET_DIM ** -0.5

    def lat(t):
        return slice(C + t * c, C + (t + 1) * c)

    def body(lg_ref, q_ref, k_ref, v_ref, g_ref, cos_ref, sin_ref, gn_ref, o_ref, dl_ref,
             d_ref, dgn_ref, dlg_ref, qs, ks, vs, dos, qf32, kf32, hf, hb, hfa, hba, gf_s, gb_s):
        h = pl.program_id(1)
        lgf = lg_ref[0, h]
        lgb = lg_ref[1, h]
        gnv = gn_ref[...]

        def fold(a):
            return jnp.sum(a.reshape(a.shape[0] // SUBLANES, SUBLANES, a.shape[1]), axis=0)

        for rows in [slice(0, C)] + [lat(t) for t in range(nt)]:
            cosb = cos_ref[rows, :]
            sinb = sin_ref[rows, :]
            qr = _rope(q_ref[rows, :].astype(F32), cosb, sinb) * scale
            qf32[rows, :] = qr
            qs[rows, :] = qr.astype(BF16)
            kr = _rope(k_ref[rows, :].astype(F32), cosb, sinb)
            kf32[rows, :] = kr
            ks[rows, :] = kr.astype(BF16)
            vs[rows, :] = v_ref[rows, :].astype(BF16)

        dgn = jnp.zeros((1, RET_DIM), F32)
        for t in range(nt):
            lrows = slice(t * c, (t + 1) * c)
            ov = o_ref[lrows, :]
            mu = jnp.mean(ov, axis=-1, keepdims=True)
            oc = ov - mu
            var = jnp.mean(oc * oc, axis=-1, keepdims=True)
            rstd = lax.rsqrt(var + NORM_EPS)
            yh = oc * rstd
            g = g_ref[lat(t), :].astype(F32)
            sg = _sigmoid(g)
            dl = dl_ref[lrows, :]
            d_ref[3, lat(t), :] = (dl * (yh * gnv) * (sg * (1.0 + g * (1.0 - sg)))).astype(BF16)
            dls = dl * (g * sg)
            dgn = dgn + jnp.sum(dls * yh, axis=0, keepdims=True)
            dyh = dls * gnv
            do = rstd * (dyh - jnp.mean(dyh, axis=-1, keepdims=True)
                         - yh * jnp.mean(dyh * yh, axis=-1, keepdims=True))
            dos[lrows, :] = do.astype(BF16)
        dgn_ref[...] = jnp.concatenate([dgn, jnp.zeros((SUBLANES - 1, RET_DIM), F32)], axis=0)
        d_ref[3, 0:C, :] = jnp.zeros((C, RET_DIM), BF16)
        d_ref[0, 0:C, :] = jnp.zeros((C, RET_DIM), BF16)

        dec, dec_c, step_f, step_b = _ret_states(kf32, vs, lgf, lgb, C, c, nt, hf, hb, hfa, hba)

        def zmat(t, qdec):
            return _dot_tn((qf32[lat(t), :] * qdec).astype(BF16), dos[t * c:(t + 1) * c, :])

        acc3f = jnp.zeros((RET_DIM, RET_DIM), F32)
        acc3b = jnp.zeros((RET_DIM, RET_DIM), F32)
        state = jnp.zeros((RET_DIM, RET_DIM), F32)
        for t in range(nt - 1, -1, -1):
            gf_s[t] = state.astype(BF16)
            z = zmat(t, dec.q_f)
            acc3f = acc3f + hfa[t] * z
            state = step_f * state + z
        gctx_f = state.astype(BF16)
        state = jnp.zeros((RET_DIM, RET_DIM), F32)
        for t in range(nt):
            gb_s[t] = state.astype(BF16)
            z = zmat(t, dec.q_b)
            acc3b = acc3b + hba[t] * z
            state = step_b * state + z
        gctx_b = state.astype(BF16)

        rc = (lax.broadcasted_iota(jnp.int32, (c, c), 0) - lax.broadcasted_iota(jnp.int32, (c, c), 1)).astype(F32)
        w_diag = _ret_weights(lgf, lgb, rc)
        wg_f = jnp.where(rc >= 0.0, w_diag * rc, 0.0)
        wg_b = jnp.where(rc < 0.0, -w_diag * rc, 0.0)
        accf = jnp.zeros((SUBLANES, RET_DIM), F32)
        accb = jnp.zeros((SUBLANES, RET_DIM), F32)
        gdf = jnp.zeros((SUBLANES, c), F32)
        gdb = jnp.zeros((SUBLANES, c), F32)
        for t in range(nt):
            rows = lat(t)
            qt = qs[rows, :]
            kt = ks[rows, :]
            vt = vs[rows, :]
            dot = dos[t * c:(t + 1) * c, :]
            s = _dot_nt(qt, kt)
            dp = _dot_nt(dot, vt)
            dv = _dot_tn((s * w_diag).astype(BF16), dot)
            ds = (dp * w_diag).astype(BF16)
            dq = _dot(ds, kt)
            dk = _dot_tn(ds, qt)
            gs = dp * s
            gdf = gdf + fold(gs * wg_f)
            gdb = gdb + fold(gs * wg_b)
            qv = qf32[rows, :]
            kv = kf32[rows, :]
            dq_f = dec.q_f * _dot_nt(dot, hf[t])
            dq_b = dec.q_b * _dot_nt(dot, hb[t])
            dk_f = dec.k_f * _dot_nt(vt, gf_s[t])
            dk_b = dec.k_b * _dot_nt(vt, gb_s[t])
            accf = accf + fold(dec.head * dq_f * qv) + fold(dec.tail * dk_f * kv)
            accb = accb + fold(dec.tail * dq_b * qv) + fold(dec.head * dk_b * kv)
            dv = dv + dec.k_f * _dot(kt, gf_s[t]) + dec.k_b * _dot(kt, gb_s[t])
            cosb = cos_ref[rows, :]
            sinb = sin_ref[rows, :]
            d_ref[0, rows, :] = _unrope((dq + dq_f + dq_b) * scale, cosb, sinb).astype(BF16)
            d_ref[1, rows, :] = _unrope(dk + dk_f + dk_b, cosb, sinb).astype(BF16)
            d_ref[2, rows, :] = dv.astype(BF16)
        kc = ks[0:C, :]
        vc = vs[0:C, :]
        kcv = kf32[0:C, :]
        dkc_f = dec_c.k_f * _dot_nt(vc, gctx_f)
        dkc_b = dec_c.k_b * _dot_nt(vc, gctx_b)
        accf = accf + fold(dec_c.tail * dkc_f * kcv)
        accb = accb + fold(dec_c.head * dkc_b * kcv)
        d_ref[1, 0:C, :] = (dkc_f + dkc_b).astype(BF16)
        d_ref[2, 0:C, :] = (dec_c.k_f * _dot(kc, gctx_f) + dec_c.k_b * _dot(kc, gctx_b)).astype(BF16)
        gf = jnp.sum(gdf) + jnp.sum(accf) + jnp.sum(acc3f)
        gb = jnp.sum(gdb) + jnp.sum(accb) + jnp.sum(acc3b)
        row = lax.broadcasted_iota(jnp.int32, (SUBLANES, LANES), 0)
        dlg_ref[...] = jnp.where(row == 0, gf, jnp.where(row == 1, gb, 0.0))

    def col(seg):
        return pl.BlockSpec((None, T, RET_DIM), lambda b, h, seg=seg: (b, 0, seg * RET_HEADS + h))

    return _grid_call(
        body, name="ret_bwd", grid=(B, RET_HEADS),
        out_shape=(jax.ShapeDtypeStruct((B, 4, T, RET_WIDTH), BF16),
                   jax.ShapeDtypeStruct((B, SUBLANES, RET_WIDTH), F32),
                   jax.ShapeDtypeStruct((B, RET_HEADS, SUBLANES, LANES), F32)),
        in_specs=[pl.BlockSpec(memory_space=pltpu.SMEM), col(0), col(1), col(2), col(3),
                  pl.BlockSpec((T, RET_DIM), lambda b, h: (0, 0)), pl.BlockSpec((T, RET_DIM), lambda b, h: (0, 0)),
                  pl.BlockSpec((1, RET_DIM), lambda b, h: (0, h)),
                  pl.BlockSpec((None, N, RET_DIM), lambda b, h: (b, 0, h)),
                  pl.BlockSpec((None, N, RET_DIM), lambda b, h: (b, 0, h))],
        out_specs=(pl.BlockSpec((None, 4, T, RET_DIM), lambda b, h: (b, 0, 0, h)),
                   pl.BlockSpec((None, SUBLANES, RET_DIM), lambda b, h: (b, 0, h)),
                   pl.BlockSpec((None, None, SUBLANES, LANES), lambda b, h: (b, h, 0, 0))),
        scratch_shapes=[pltpu.VMEM((T, RET_DIM), BF16)] * 3 + [pltpu.VMEM((N, RET_DIM), BF16)]
                       + [pltpu.VMEM((T, RET_DIM), F32)] * 2
                       + [pltpu.VMEM((nt, RET_DIM, RET_DIM), BF16)] * 2 + [pltpu.VMEM((nt, RET_DIM, RET_DIM), F32)] * 2
                       + [pltpu.VMEM((nt, RET_DIM, RET_DIM), BF16)] * 2,
        args=(lg, proj, proj, proj, proj, cos, sin, gn, o, dlat))


def _na_geometry(rows):
    kh = min(NA_KH, rows)
    return kh, kh * GRID_W


def _pair_select():
    lane = lax.broadcasted_iota(jnp.int32, (2 * GRID_W, LANES), 1)
    row = lax.broadcasted_iota(jnp.int32, (2 * GRID_W, LANES), 0)
    return (lane >= NA_DIM) == (row >= GRID_W)


def _pair_bias(bias_ref, dr0, kh):
    return jnp.concatenate(
        [jnp.concatenate([bias_ref[e, pl.ds(dr0 + 2 * m, 1)].reshape(GRID_W, LANES) for m in range(kh // 2)], axis=1)
         for e in range(2)], axis=0)


def _na_softmax(s_loc, s_ctx):
    mx = jnp.maximum(jnp.max(s_loc, axis=-1, keepdims=True), jnp.max(s_ctx, axis=-1, keepdims=True))
    p_loc = jnp.exp(s_loc - mx)
    p_ctx = jnp.exp(s_ctx - mx)
    den = jnp.sum(p_loc, axis=-1, keepdims=True) + jnp.sum(p_ctx, axis=-1, keepdims=True)
    return p_loc, p_ctx, den


def _na_fwd(proj, bias2, n_ctx):
    B, T, _ = proj.shape
    C = n_ctx
    N = T - C
    R = N // GRID_W
    kh, nk = _na_geometry(R)
    scale = NA_DIM ** -0.5
    base = (4 * RET_WIDTH) // LANES

    def body(q_ref, k_ref, v_ref, bias_ref, out_ref, kb16, vb16):
        kb16[...] = k_ref[...].astype(BF16)
        vb16[...] = v_ref[...].astype(BF16)
        kc = kb16[0:C, :]
        vc = vb16[0:C, :]
        lane = lax.broadcasted_iota(jnp.int32, (GRID_W, LANES), 1)
        sel2 = _pair_select()

        def group(gi, carry):
            pre = []
            for u in range(NA_GROUP):
                r = gi * NA_GROUP + u
                bs = jnp.clip(r - kh // 2, 0, R - kh)
                dr0 = bs - r + (NA_KH - 1)
                q = q_ref[pl.ds(pl.multiple_of(C + r * GRID_W, GRID_W), GRID_W), :].astype(F32) * scale
                q2 = jnp.where(sel2, jnp.concatenate([q, q], axis=0), 0.0).astype(BF16)
                band = pl.ds(pl.multiple_of(C + bs * GRID_W, GRID_W), nk)
                s_loc = _dot_nt(q2, kb16[band, :]) + _pair_bias(bias_ref, dr0, kh)
                s_ctx = _dot_nt(q2, kc)
                pre.append((r, band, s_loc, s_ctx))
            mid = [(r, band) + _na_softmax(s_loc, s_ctx) for r, band, s_loc, s_ctx in pre]
            for r, band, p_loc, p_ctx, den in mid:
                o2 = (_dot(p_loc.astype(BF16), vb16[band, :]) + _dot(p_ctx.astype(BF16), vc)) / den
                out_ref[pl.ds(pl.multiple_of(r * GRID_W, GRID_W), GRID_W), :] = jnp.where(
                    lane < NA_DIM, o2[:GRID_W], o2[GRID_W:]).astype(BF16)
            return carry

        lax.fori_loop(0, R // NA_GROUP, group, 0)

    def col(seg):
        return pl.BlockSpec((None, T, LANES), lambda b, p, seg=seg: (b, 0, base + seg * NA_PAIRS + p))

    return _grid_call(
        body, name="na_fwd", grid=(B, NA_PAIRS),
        out_shape=(jax.ShapeDtypeStruct((B, N, NA_WIDTH), BF16),),
        in_specs=[col(0), col(1), col(2),
                  pl.BlockSpec((2, 2 * NA_KH - 2, GRID_W, LANES), lambda b, p: (p, 0, 0, 0))],
        out_specs=(pl.BlockSpec((None, N, LANES), lambda b, p: (b, 0, p)),),
        scratch_shapes=[pltpu.VMEM((T, LANES), BF16)] * 2,
        args=(proj, proj, proj, bias2))


def _na_bwd(proj, bias2, dlat, n_ctx):
    B, T, _ = proj.shape
    C = n_ctx
    N = T - C
    R = N // GRID_W
    kh, nk = _na_geometry(R)
    scale = NA_DIM ** -0.5
    base = (4 * RET_WIDTH) // LANES

    def body(q_ref, k_ref, v_ref, bias_ref, dl_ref, d_ref, db_ref, kb16, vb16, dkv):
        b = pl.program_id(1)
        kb16[...] = k_ref[...].astype(BF16)
        vb16[...] = v_ref[...].astype(BF16)
        kc = kb16[0:C, :]
        vc = vb16[0:C, :]
        lane = lax.broadcasted_iota(jnp.int32, (GRID_W, LANES), 1)
        dkv[...] = jnp.zeros(dkv.shape, F32)
        d_ref[0, 0:C, :] = jnp.zeros((C, LANES), BF16)

        @pl.when(b == 0)
        def _():
            db_ref[...] = jnp.zeros(db_ref.shape, F32)

        sel2 = _pair_select()

        def group(gi, carry):
            pre = []
            for u in range(NA_GROUP):
                r = gi * NA_GROUP + u
                bs = jnp.clip(r - kh // 2, 0, R - kh)
                dr0 = bs - r + (NA_KH - 1)
                q = q_ref[pl.ds(pl.multiple_of(C + r * GRID_W, GRID_W), GRID_W), :].astype(F32) * scale
                do = dl_ref[pl.ds(pl.multiple_of(r * GRID_W, GRID_W), GRID_W), :]
                q2 = jnp.where(sel2, jnp.concatenate([q, q], axis=0), 0.0).astype(BF16)
                do2 = jnp.where(sel2, jnp.concatenate([do, do], axis=0), 0.0).astype(BF16)
                band = pl.ds(pl.multiple_of(C + bs * GRID_W, GRID_W), nk)
                s_loc = _dot_nt(q2, kb16[band, :]) + _pair_bias(bias_ref, dr0, kh)
                s_ctx = _dot_nt(q2, kc)
                dp_loc = _dot_nt(do2, vb16[band, :])
                dp_ctx = _dot_nt(do2, vc)
                pre.append((r, dr0, band, q2, do2, s_loc, s_ctx, dp_loc, dp_ctx))
            mid = []
            for r, dr0, band, q2, do2, s_loc, s_ctx, dp_loc, dp_ctx in pre:
                p_loc, p_ctx, den = _na_softmax(s_loc, s_ctx)
                inv = 1.0 / den
                p_loc = p_loc * inv
                p_ctx = p_ctx * inv
                delta = (jnp.sum(p_loc * dp_loc, axis=-1, keepdims=True)
                         + jnp.sum(p_ctx * dp_ctx, axis=-1, keepdims=True))
                ds_loc = p_loc * (dp_loc - delta)
                ds_ctx = p_ctx * (dp_ctx - delta)
                mid.append((r, dr0, band, q2, do2, p_loc.astype(BF16), p_ctx.astype(BF16), ds_loc, ds_ctx))
            for r, dr0, band, q2, do2, pb_loc, pb_ctx, ds_loc, ds_ctx in mid:
                dsb_loc = ds_loc.astype(BF16)
                dsb_ctx = ds_ctx.astype(BF16)
                dq2 = _dot(dsb_loc, kb16[band, :]) + _dot(dsb_ctx, kc)
                d_ref[0, pl.ds(pl.multiple_of(C + r * GRID_W, GRID_W), GRID_W), :] = (jnp.where(
                    lane < NA_DIM, dq2[:GRID_W], dq2[GRID_W:]) * scale).astype(BF16)
                dkv[0, band, :] += _dot_tn(dsb_loc, q2)
                dkv[1, band, :] += _dot_tn(pb_loc, do2)
                dkv[0, 0:C, :] += _dot_tn(dsb_ctx, q2)
                dkv[1, 0:C, :] += _dot_tn(pb_ctx, do2)
                for e in range(2):
                    for m in range(kh // 2):
                        db_ref[e, pl.ds(dr0 + 2 * m, 1)] += ds_loc[e * GRID_W:(e + 1) * GRID_W,
                                                                   m * LANES:(m + 1) * LANES].reshape(1, GRID_W, LANES)
            return carry

        lax.fori_loop(0, R // NA_GROUP, group, 0)
        d_ref[1] = dkv[0].astype(BF16)
        d_ref[2] = dkv[1].astype(BF16)

    def col(seg):
        return pl.BlockSpec((None, T, LANES), lambda p, b, seg=seg: (b, 0, base + seg * NA_PAIRS + p))

    return _grid_call(
        body, name="na_bwd", grid=(NA_PAIRS, B),
        out_shape=(jax.ShapeDtypeStruct((B, 3, T, NA_WIDTH), BF16),
                   jax.ShapeDtypeStruct((NA_HEADS, 2 * NA_KH - 2, GRID_W, LANES), F32)),
        in_specs=[col(0), col(1), col(2),
                  pl.BlockSpec((2, 2 * NA_KH - 2, GRID_W, LANES), lambda p, b: (p, 0, 0, 0)),
                  pl.BlockSpec((None, N, LANES), lambda p, b: (b, 0, p))],
        out_specs=(pl.BlockSpec((None, 3, T, LANES), lambda p, b: (b, 0, 0, p)),
                   pl.BlockSpec((2, 2 * NA_KH - 2, GRID_W, LANES), lambda p, b: (p, 0, 0, 0))),
        scratch_shapes=[pltpu.VMEM((T, LANES), BF16)] * 2 + [pltpu.VMEM((2, T, LANES), F32)],
        args=(proj, proj, proj, bias2, dlat))


def _split3(a):
    hi = a.astype(BF16)
    r1 = a - hi.astype(F32)
    mid = r1.astype(BF16)
    lo = (r1 - mid.astype(F32)).astype(BF16)
    return hi, mid, lo


def _rpb_reduce(dbias2, onehot2):
    rows = dbias2.shape[0] * dbias2.shape[1]
    flat = dbias2.reshape(rows, GRID_W * LANES)

    def body(a_ref, oh_ref, o_ref):
        hi, mid, lo = _split3(a_ref[...])
        oh = oh_ref[...]
        o_ref[...] = _dot(hi, oh) + _dot(mid, oh) + _dot(lo, oh)

    return pl.pallas_call(
        body, name="rpb_reduce", out_shape=jax.ShapeDtypeStruct((rows, LANES), F32),
        in_specs=[_vmem(), _vmem()], out_specs=_vmem(),
        compiler_params=pltpu.CompilerParams(vmem_limit_bytes=VMEM_LIMIT),
    )(flat, onehot2)


def _dense_core(lat_ret, lat_na, x, tgt, modl, g_post_mix, g_pre_mlp, g_post_mlp, w_out, w1, w2):
    B, N, D = x.shape
    F = w1.shape[1]
    wout_rows, w1_cols, w2_rows = w_out.shape[0] // N_DEV, w1.shape[1] // N_DEV, w2.shape[0] // N_DEV
    mixw = w_out.shape[0]
    half = mixw // 2
    tm = _div_tile(N, 256, 16)
    nt = N // tm
    fc = _div_tile(F, 1024, LANES)

    def body(lr_ref, ln_ref, x_ref, t_ref, gt1_ref, sh2_ref, sc2_ref, gt2_ref, gpm_ref, gpre_ref, gpo_ref,
             wout_part, w1_part, w2_part,
             dy1_ref, dlr_ref, dln_ref, dmix_ref, h2_ref, a_ref, du_ref, dz_ref, red_ref, wout_hbm, w1_hbm, w2_hbm,
             wout_v, w1_v, w2_v, u_s, sems, fsend, frecv):
        @pl.when((pl.program_id(0) == 0) & (pl.program_id(1) == 0))
        def _():
            relay = [(_row_block(wout_part, wout_rows), _row_block(wout_hbm, wout_rows)),
                     (_col_block(w1_part, w1_cols), _col_block(w1_hbm, w1_cols)),
                     (_row_block(w2_part, w2_rows), _row_block(w2_hbm, w2_rows))]
            _forward_start(relay, fsend, frecv)
            _forward_wait(relay, fsend, frecv)
            cps = [pltpu.make_async_copy(wout_hbm, wout_v, sems.at[0]),
                   pltpu.make_async_copy(w1_hbm, w1_v, sems.at[1]),
                   pltpu.make_async_copy(w2_hbm, w2_v, sems.at[2])]
            for cp in cps:
                cp.start()
            for cp in cps:
                cp.wait()

        @pl.when(pl.program_id(1) == 0)
        def _():
            red_ref[...] = jnp.zeros(red_ref.shape, F32)

        gt1 = gt1_ref[...]
        sh2 = sh2_ref[...]
        sc2 = sc2_ref[...]
        gt2 = gt2_ref[...]
        gpm = gpm_ref[...]
        gpre = gpre_ref[...]
        gpo = gpo_ref[...]

        def rowmean(a):
            return jnp.mean(a, axis=-1, keepdims=True)

        def colsum(a):
            return jnp.sum(a, axis=0, keepdims=True)

        mix = _dot(lr_ref[...], wout_v[0:half, :]) + _dot(ln_ref[...], wout_v[half:, :])
        x = x_ref[...]
        rm = lax.rsqrt(rowmean(mix * mix) + NORM_EPS)
        mh = mix * rm
        nm = mh * gpm
        y1 = x + gt1 * nm
        r1 = lax.rsqrt(rowmean(y1 * y1) + NORM_EPS)
        xh = y1 * r1
        n1 = xh * gpre
        h2b = (n1 * (1.0 + sc2) + sh2).astype(BF16)
        h2_ref[...] = h2b
        z = jnp.zeros((tm, D), F32)
        for c0 in range(0, F, fc):
            u = _dot(h2b, w1_v[:, c0:c0 + fc])
            u_s[:, c0:c0 + fc] = u
            ru = jnp.maximum(u, 0.0)
            ab = (ru * ru).astype(BF16)
            a_ref[:, c0:c0 + fc] = ab
            z = z + _dot(ab, w2_v[c0:c0 + fc, :])
        r2 = lax.rsqrt(rowmean(z * z) + NORM_EPS)
        zh = z * r2
        n2 = zh * gpo
        y2 = y1 + gt2 * n2
        err = y2 - t_ref[...]
        loss = 0.5 * jnp.sum(rowmean(err * err))
        dy2 = err * (1.0 / D)
        red_ref[2:3, :] += colsum(dy2 * n2)
        dn2 = dy2 * gt2
        red_ref[6:7, :] += colsum(dn2 * zh)
        dzh = dn2 * gpo
        dz = r2 * (dzh - zh * rowmean(dzh * zh))
        dzb = dz.astype(BF16)
        dz_ref[...] = dzb
        dh2 = jnp.zeros((tm, D), F32)
        for c0 in range(0, F, fc):
            da = _dot_nt(dzb, w2_v[c0:c0 + fc, :])
            dub = (da * (2.0 * jnp.maximum(u_s[:, c0:c0 + fc], 0.0))).astype(BF16)
            du_ref[:, c0:c0 + fc] = dub
            dh2 = dh2 + _dot_nt(dub, w1_v[:, c0:c0 + fc])
        red_ref[3:4, :] += colsum(dh2 * n1)
        red_ref[4:5, :] += colsum(dh2)
        dn1 = dh2 * (1.0 + sc2)
        red_ref[5:6, :] += colsum(dn1 * xh)
        dxh = dn1 * gpre
        dy1 = dy2 + r1 * (dxh - xh * rowmean(dxh * xh))
        dy1_ref[...] = dy1
        red_ref[0:1, :] += colsum(dy1 * nm)
        dnm = dy1 * gt1
        red_ref[1:2, :] += colsum(dnm * mh)
        dmh = dnm * gpm
        dmix = (rm * (dmh - mh * rowmean(dmh * mh))).astype(BF16)
        dmix_ref[...] = dmix
        dlr_ref[...] = _dot_nt(dmix, wout_v[0:half, :])
        dln_ref[...] = _dot_nt(dmix, wout_v[half:, :])
        red_ref[7:8, :] += jnp.zeros((1, D), F32) + loss

    def tok(w):
        return pl.BlockSpec((None, tm, w), lambda b, t: (b, t, 0))

    def mod(k):
        return pl.BlockSpec((None, None, 1, D), lambda b, t, k=k: (b, k, 0, 0))

    def vec():
        return pl.BlockSpec((1, D), lambda b, t: (0, 0))

    return pl.pallas_call(
        body, name="dense_core", grid=(B, nt),
        out_shape=(jax.ShapeDtypeStruct((B, N, D), F32), jax.ShapeDtypeStruct((B, N, half), F32),
                   jax.ShapeDtypeStruct((B, N, half), F32), jax.ShapeDtypeStruct((B, N, D), BF16),
                   jax.ShapeDtypeStruct((B, N, D), BF16), jax.ShapeDtypeStruct((B, N, F), BF16),
                   jax.ShapeDtypeStruct((B, N, F), BF16), jax.ShapeDtypeStruct((B, N, D), BF16),
                   jax.ShapeDtypeStruct((B, SUBLANES, D), F32),
                   jax.ShapeDtypeStruct(w_out.shape, w_out.dtype), jax.ShapeDtypeStruct(w1.shape, w1.dtype),
                   jax.ShapeDtypeStruct(w2.shape, w2.dtype)),
        in_specs=[tok(half), tok(half), tok(D), tok(D), mod(2), mod(3), mod(4), mod(5), vec(), vec(), vec(),
                  _any(), _any(), _any()],
        out_specs=(tok(D), tok(half), tok(half), tok(D), tok(D), tok(F), tok(F), tok(D),
                   pl.BlockSpec((None, SUBLANES, D), lambda b, t: (b, 0, 0)), _any(), _any(), _any()),
        scratch_shapes=[pltpu.VMEM((mixw, D), BF16), pltpu.VMEM((D, F), BF16), pltpu.VMEM((F, D), BF16),
                        pltpu.VMEM((tm, F), F32), pltpu.SemaphoreType.DMA((3,)),
                        pltpu.SemaphoreType.DMA((3, 3)), pltpu.SemaphoreType.DMA((3, 3))],
        input_output_aliases={11: 9, 12: 10, 13: 11},
        compiler_params=_params("arbitrary", "arbitrary"),
    )(lat_ret, lat_na, x, tgt, modl, modl, modl, modl, g_post_mix, g_pre_mlp, g_post_mlp, w_out, w1, w2)[:9]


def _inproj_bwd(dret, dna, x, ctx, dy1, modl, g1, w_in_t):
    B, N, D = x.shape
    n_ctx = ctx.shape[1]
    T = n_ctx + N
    tm = _div_tile(n_ctx, 256, 16)
    nct, ctx_spec, lat_spec = _token_tiles(n_ctx, tm)
    nt = T // tm
    nseg_r = dret.shape[1]
    nseg_n = dna.shape[1]
    nw = w_in_t.shape[0]

    def body(*refs):
        seg_refs = refs[:nseg_r + nseg_n]
        c_ref, x_ref, dy1_ref, sc_ref, g_ref, w_ref, dx_ref, red_ref = refs[nseg_r + nseg_n:]
        t = pl.program_id(1)
        dh = jnp.zeros((tm, D), F32)
        for s, ref in enumerate(seg_refs):
            dh = dh + _dot(ref[...], w_ref[s * SEG:(s + 1) * SEG, :])
        x = jnp.where(t < nct, c_ref[...], x_ref[...])
        g = g_ref[...]
        r = lax.rsqrt(jnp.mean(x * x, axis=-1, keepdims=True) + NORM_EPS)
        xh = x * r
        red_ref[0:1, :] = jnp.sum(dh, axis=0, keepdims=True)
        red_ref[1:2, :] = jnp.sum(dh * (xh * g), axis=0, keepdims=True)
        dn = dh * (1.0 + sc_ref[...])
        red_ref[2:3, :] = jnp.sum(dn * xh, axis=0, keepdims=True)
        red_ref[3:, :] = jnp.zeros((SUBLANES - 3, D), F32)
        dxh = dn * g
        dx = r * (dxh - xh * jnp.mean(dxh * xh, axis=-1, keepdims=True))
        dx_ref[...] = dx + jnp.where(t >= nct, dy1_ref[...], 0.0)

    def mrow(b, t):
        return jnp.where(t < nct, B, b)

    def seg(s):
        return pl.BlockSpec((None, None, tm, SEG), lambda b, t, s=s: (b, s, t, 0))

    return _grid_call(
        body, name="inproj_bwd", grid=(B, nt),
        out_shape=(jax.ShapeDtypeStruct((B, N, D), F32), jax.ShapeDtypeStruct((B, nt, SUBLANES, D), F32)),
        in_specs=[seg(s) for s in range(nseg_r)] + [seg(s) for s in range(nseg_n)]
                 + [ctx_spec(D), lat_spec(D), lat_spec(D),
                    pl.BlockSpec((None, None, 1, D), lambda b, t: (mrow(b, t), 1, 0, 0)),
                    pl.BlockSpec((1, D), lambda b, t: (0, 0)),
                    pl.BlockSpec((nw, D), lambda b, t: (0, 0))],
        out_specs=(lat_spec(D), pl.BlockSpec((None, None, SUBLANES, D), lambda b, t: (b, t, 0, 0))),
        scratch_shapes=[], args=(*([dret] * nseg_r), *([dna] * nseg_n), ctx, x, dy1, modl, g1, w_in_t))


def _tn_matmul(lhs, rhs, name, rows_before=0, rows_after=0, into=None):
    B, S, T, W = lhs.shape
    nn = rhs.shape[-1]
    tk = _div_tile(T, 2304, LANES)
    bm = _div_tile(W, 1024, LANES)
    bn = _div_tile(nn, 1024, LANES)
    nkt = T // tk
    nk = B * nkt

    def body(l_ref, r_ref, *rest):
        o_ref, acc = rest[-2:]
        k = pl.program_id(3)

        @pl.when(k == 0)
        def _():
            acc[...] = jnp.zeros(acc.shape, F32)

        acc[...] += _dot_tn(l_ref[...].astype(BF16), r_ref[...].astype(BF16))

        @pl.when(k == nk - 1)
        def _():
            o_ref[...] = acc[...].astype(BF16)

    nwb = W // bm
    first = rows_before // bm
    return pl.pallas_call(
        functools.partial(body), name=name, grid=(S, nwb, nn // bn, nk),
        out_shape=jax.ShapeDtypeStruct((rows_before + S * W + rows_after, nn), BF16),
        in_specs=[pl.BlockSpec((None, None, tk, bm), lambda s, i, j, k: (k // nkt, s, k % nkt, i)),
                  pl.BlockSpec((None, tk, bn), lambda s, i, j, k: (k // nkt, k % nkt, j))]
                 + ([] if into is None else [_any()]),
        out_specs=pl.BlockSpec((bm, bn), lambda s, i, j, k: (first + s * nwb + i, j)),
        scratch_shapes=[pltpu.VMEM((bm, bn), F32)],
        input_output_aliases={} if into is None else {2: 0},
        compiler_params=_params("parallel", "parallel", "parallel", "arbitrary"),
    )(lhs, rhs, *([] if into is None else [into]))


class _SplitScatter:
    def __init__(self, gs, block_ofs, land_shapes, name, kind="scatter", masks=ALL_PEERS):
        self.n = n = len(gs)
        self.block_ofs, self.kind, self.masks = block_ofs, kind, masks
        if kind == "scatter":
            land_shapes = [(N_DEV,) + tuple(bs) for bs in land_shapes]
        hbm = pl.BlockSpec(memory_space=pltpu.HBM)
        sem = pl.BlockSpec(memory_space=pltpu.SEMAPHORE)

        def body(*refs):
            g_refs, land_refs = refs[:n], refs[n:2 * n]
            send_sems, recv_sems, own_sems = refs[2 * n:2 * n + 3]
            token = refs[-1]
            for own, pushes in self._copies(g_refs, land_refs, send_sems, recv_sems, own_sems, landing="sender"):
                own.start()
                for cp in pushes:
                    cp.start()
            token[...] = jnp.zeros_like(token)

        outs = pl.pallas_call(
            body, name=name,
            out_shape=(pltpu.SemaphoreType.DMA((n * (N_DEV - 1),)), pltpu.SemaphoreType.DMA((n * (N_DEV - 1),)),
                       pltpu.SemaphoreType.DMA((n,)))
                      + tuple(pltpu.HBM(g.shape, g.dtype) for g in gs)
                      + tuple(pltpu.HBM(s, g.dtype) for s, g in zip(land_shapes, gs))
                      + (jax.ShapeDtypeStruct((SUBLANES, LANES), F32),),
            in_specs=(hbm,) * (2 * n), out_specs=(sem,) * 3 + (hbm,) * (2 * n) + (_vmem(),),
            input_output_aliases={k: 3 + k for k in range(2 * n)},
            compiler_params=pltpu.CompilerParams(has_side_effects=pltpu.SideEffectType.DATAFLOW_SIDE_EFFECTING),
        )(*[pltpu.with_memory_space_constraint(g, pltpu.HBM) for g in gs],
          *[pltpu.with_memory_space_constraint(lax.empty(s, g.dtype), pltpu.HBM) for s, g in zip(land_shapes, gs)])
        self.sems, self.thru, self.token = outs[:3], outs[3:3 + 2 * n], outs[-1]

    def _copies(self, g_refs, land_refs, send_sems, recv_sems, own_sems, landing):
        me, peers = _me_and_peers()
        out = []
        for k in range(self.n):
            if self.kind == "scatter":
                src, dst = self.block_ofs[k](g_refs[k]), _slot(land_refs[k])
            else:
                src, dst = (lambda p, k=k: g_refs[k]), self.block_ofs[k](land_refs[k])
            own = pltpu.make_async_copy(src(me), dst(me), own_sems.at[k])
            pushes = []
            for m in self.masks:
                dev, pid = peers[m - 1]
                i = k * (N_DEV - 1) + m - 1
                pushes.append(_remote(src(pid), dst(me if landing == "sender" else pid),
                                      send_sems.at[i], recv_sems.at[i], dev))
            out.append((own, pushes))
        return out


def _scatter_wait(scatters, after, name):
    hbm = pl.BlockSpec(memory_space=pltpu.HBM)
    sem = pl.BlockSpec(memory_space=pltpu.SEMAPHORE)
    n_arr = [2 * sc.n for sc in scatters]
    total = sum(n_arr)

    def body(*refs):
        arrs, sems = refs[:total], refs[total:total + 3 * len(scatters)]
        a0 = 0
        for j, sc in enumerate(scatters):
            g_refs, land_refs = arrs[a0:a0 + sc.n], arrs[a0 + sc.n:a0 + 2 * sc.n]
            a0 += 2 * sc.n
            send_sems, recv_sems, own_sems = sems[3 * j:3 * j + 3]
            for (own, sent), (_, got) in zip(sc._copies(g_refs, land_refs, send_sems, recv_sems, own_sems, "sender"),
                                             sc._copies(g_refs, land_refs, send_sems, recv_sems, own_sems, "receiver")):
                own.wait()
                for cp in sent:
                    cp.wait_send()
                for cp in got:
                    cp.wait_recv()

    operands = [a for sc in scatters for a in sc.thru]
    outs = pl.pallas_call(
        body, name=name,
        out_shape=tuple(pltpu.HBM(a.shape, a.dtype) for a in operands),
        in_specs=(hbm,) * total + (sem,) * (3 * len(scatters)) + (pl.BlockSpec(memory_space=pl.ANY),),
        out_specs=(hbm,) * total, input_output_aliases={k: k for k in range(total)},
        compiler_params=pltpu.CompilerParams(has_side_effects=pltpu.SideEffectType.DATAFLOW_SIDE_EFFECTING),
    )(*operands, *[s for sc in scatters for s in sc.sems], after)
    lands, a0 = [], 0
    for sc in scatters:
        lands.extend(outs[a0 + sc.n:a0 + 2 * sc.n])
        a0 += 2 * sc.n
    return lands


def _sum_slots(buf, name):
    _, rows, cols = buf.shape
    tr = _div_tile(rows, 256, 2 * SUBLANES)

    def body(b_ref, o_ref):
        acc = b_ref[0].astype(F32)
        for k in range(1, N_DEV):
            acc = acc + b_ref[k].astype(F32)
        o_ref[...] = acc

    return pl.pallas_call(
        functools.partial(body), name=name, grid=(rows // tr,),
        out_shape=jax.ShapeDtypeStruct((rows, cols), F32),
        in_specs=[pl.BlockSpec((N_DEV, tr, cols), lambda i: (0, i, 0))],
        out_specs=pl.BlockSpec((tr, cols), lambda i: (i, 0)),
        compiler_params=_params("parallel"),
    )(buf)


def _small_ar(vec, dmods, silu_all, w_ada, c_ctx):
    rv = vec.shape[0]
    D = silu_all.shape[1]
    ncol = w_ada.shape[1]
    nm = dmods.shape[1]
    srows = silu_all.shape[0]

    def body(vec_ref, dm_ref, s_ref, w_ref, cc_ref, tot_ref, gb_ref, gw_ref, gc_ref,
             vbuf, mbuf, tbuf, dmx, send1, recv1, send3, recv3):
        me, _ = _me_and_peers()
        vbuf[me] = vec_ref[...]
        mbuf[me] = dm_ref[...]
        both = [(lambda p: vbuf.at[me], lambda p: vbuf.at[p]), (lambda p: mbuf.at[me], lambda p: mbuf.at[p])]
        _push_start(both, ALL_PEERS, send1, recv1)
        _push_wait_recv(both, ALL_PEERS, send1, recv1)
        _push_wait_send(both, ALL_PEERS, send1, recv1)
        tot = vbuf[0]
        msum = mbuf[0]
        for k in range(1, N_DEV):
            tot = tot + vbuf[k]
            msum = msum + mbuf[k]
        tot_ref[...] = tot
        gb_ref[...] = jnp.sum(msum, axis=0, keepdims=True)
        loc = pl.ds(pl.multiple_of(me * ncol, ncol), ncol)
        for k in range(N_DEV):
            dmx[k * SUBLANES:(k + 1) * SUBLANES, :] = mbuf[k, :, loc]
        cm = msum[2:3, :]
        mbuf[0, 2:3, :] = cm
        cm_loc = mbuf[0, 2:3, loc]
        dmx[N_DEV * SUBLANES:, :] = jnp.concatenate([cm_loc, jnp.zeros((SUBLANES - 1, ncol), F32)], axis=0)
        gw_ref[...] = _dot_tn(s_ref[...], dmx[...])
        tbuf[me] = _dot_nt(dmx[N_DEV * SUBLANES:, :], w_ref[...])
        _exchange(lambda p: tbuf.at[me], lambda p: tbuf.at[p], send3, recv3)
        tsum = tbuf[0]
        for k in range(1, N_DEV):
            tsum = tsum + tbuf[k]
        cc = cc_ref[...]
        sg = _sigmoid(cc)
        gc_ref[...] = tsum[0:1, :] * (sg * (1.0 + cc * (1.0 - sg)))

    return pl.pallas_call(
        body, name="small_ar",
        out_shape=(jax.ShapeDtypeStruct((rv, LANES), F32), jax.ShapeDtypeStruct((1, nm), F32),
                   jax.ShapeDtypeStruct((D, ncol), F32), jax.ShapeDtypeStruct((1, D), F32)),
        in_specs=[_vmem()] * 5, out_specs=(_vmem(),) * 4,
        scratch_shapes=[pltpu.VMEM((N_DEV, rv, LANES), F32), pltpu.VMEM((N_DEV, SUBLANES, nm), F32),
                        pltpu.VMEM((N_DEV, SUBLANES, D), F32), pltpu.VMEM((srows, ncol), F32)]
                       + [pltpu.SemaphoreType.DMA((2, N_DEV - 1))] * 2 + [pltpu.SemaphoreType.DMA((N_DEV - 1,))] * 2,
        compiler_params=pltpu.CompilerParams(vmem_limit_bytes=VMEM_LIMIT),
    )(vec, dmods, silu_all, w_ada, c_ctx.reshape(1, D))


def _adam_update(w, g, m, v):
    mn = ADAM_B1 * m + (1.0 - ADAM_B1) * g
    vn = ADAM_B2 * v + (1.0 - ADAM_B2) * (g * g)
    m_hat = mn / (1.0 - ADAM_B1 ** ADAM_STEP)
    v_hat = vn / (1.0 - ADAM_B2 ** ADAM_STEP)
    return -ADAM_LR * (m_hat / (jnp.sqrt(v_hat) + ADAM_EPS) + ADAM_WD * w), mn, vn


def _adamw(w, g, m, v, name):
    rows, cols = w.shape
    tr = _div_tile(rows, 256, SUBLANES) if rows * cols > 65536 else rows

    def body(w_ref, g_ref, m_ref, v_ref, d_ref, nm_ref, nv_ref):
        d_ref[...], nm_ref[...], nv_ref[...] = _adam_update(w_ref[...], g_ref[...], m_ref[...], v_ref[...])

    spec = pl.BlockSpec((tr, cols), lambda i: (i, 0))
    return pl.pallas_call(
        functools.partial(body), name=name, grid=(rows // tr,),
        out_shape=(jax.ShapeDtypeStruct((rows, cols), F32),) * 3,
        in_specs=[spec] * 4, out_specs=(spec,) * 3,
        compiler_params=_params("parallel"),
    )(w, g, m, v)


def _adamw_small(items, name):
    n = len(items)

    def body(*refs):
        ins, outs = refs[:4 * n], refs[4 * n:]
        for i in range(n):
            w_ref, g_ref, m_ref, v_ref = ins[4 * i:4 * i + 4]
            outs[3 * i][...], outs[3 * i + 1][...], outs[3 * i + 2][...] = _adam_update(
                w_ref[...], g_ref[...], m_ref[...], v_ref[...])

    outs = pl.pallas_call(
        body, name=name,
        out_shape=tuple(jax.ShapeDtypeStruct(it[0].shape, F32) for it in items for _ in range(3)),
        in_specs=[_vmem()] * (4 * n), out_specs=(_vmem(),) * (3 * n),
        compiler_params=pltpu.CompilerParams(vmem_limit_bytes=VMEM_LIMIT),
    )(*[a for it in items for a in it])
    return [tuple(outs[3 * i:3 * i + 3]) for i in range(n)]


def _sum_adamw(buf, w, m, v, name):
    _, rows, cols = buf.shape
    tr = _div_tile(rows, 256, 2 * SUBLANES)

    def body(b_ref, w_ref, m_ref, v_ref, g_ref, d_ref, nm_ref, nv_ref):
        g = b_ref[0].astype(F32)
        for k in range(1, N_DEV):
            g = g + b_ref[k].astype(F32)
        g_ref[...] = g
        d_ref[...], nm_ref[...], nv_ref[...] = _adam_update(w_ref[...], g, m_ref[...], v_ref[...])

    spec = pl.BlockSpec((tr, cols), lambda i: (i, 0))
    return pl.pallas_call(
        functools.partial(body), name=name, grid=(rows // tr,),
        out_shape=(jax.ShapeDtypeStruct((rows, cols), F32),) * 4,
        in_specs=[pl.BlockSpec((N_DEV, tr, cols), lambda i: (0, i, 0))] + [spec] * 3, out_specs=(spec,) * 4,
        compiler_params=_params("parallel"),
    )(buf, w, m, v)


def _rope_tables(n_ctx, n):
    n_freq = RET_DIM // 4
    inv = np.float32(ROPE_BASE) ** (-np.arange(n_freq, dtype=np.float32) / np.float32(n_freq))
    tok = np.arange(n)
    pos_r = (tok // GRID_W).astype(np.float32)
    pos_c = (tok % GRID_W).astype(np.float32)
    ang_r = (pos_r[:, None] * inv[None, :]).astype(np.float32)
    ang_c = (pos_c[:, None] * inv[None, :]).astype(np.float32)
    cos = np.concatenate([np.cos(ang_r), np.cos(ang_r), np.cos(ang_c), np.cos(ang_c)], axis=-1)
    sin = np.concatenate([-np.sin(ang_r), np.sin(ang_r), -np.sin(ang_c), np.sin(ang_c)], axis=-1)
    cos = np.concatenate([np.ones((n_ctx, RET_DIM), np.float32), cos], axis=0)
    sin = np.concatenate([np.zeros((n_ctx, RET_DIM), np.float32), sin], axis=0)
    return jnp.asarray(cos, F32), jnp.asarray(sin, F32)


def _na_tables():
    q = np.arange(GRID_W)[:, None]
    k = np.arange(GRID_W)[None, :]
    start = np.clip(q - NA_KW // 2, 0, GRID_W - NA_KW)
    valid = (k >= start) & (k < start + NA_KW)
    dc = np.clip(k - q + (NA_KW - 1), 0, 2 * NA_KW - 2)
    ncls = 2 * NA_KW - 1
    onehot = (dc[None] == np.arange(ncls)[:, None, None]) & valid[None]
    oh2 = np.zeros((GRID_W, LANES, LANES), np.float32)
    for c in range(ncls):
        oh2[:, :GRID_W, c] = onehot[c]
        oh2[:, GRID_W:, 32 + c] = onehot[c]
    return onehot.astype(np.float32), valid, oh2.reshape(GRID_W * LANES, LANES)


def _paired_bias(rpb, onehot, valid):
    t = jnp.einsum("hdc,cqk->hdqk", rpb, jnp.asarray(onehot), precision=lax.Precision.HIGHEST)
    t = jnp.where(jnp.asarray(valid)[None, None], t, NEG_INF)
    return jnp.concatenate([t[:, :-1], t[:, 1:]], axis=-1)


def kernel(x, c, ctx, c_ctx, w_ada, b_ada, g_pre_mix, g_post_mix, g_pre_mlp, g_post_mlp, w_in, ret_decay, ret_gn, na_rpb, w_out, w_mlp1, w_mlp2, loss_target, m_c_ctx, m_w_ada, m_b_ada, m_g_pre_mix, m_g_post_mix, m_g_pre_mlp, m_g_post_mlp, m_w_in, m_ret_decay, m_ret_gn, m_na_rpb, m_w_out, m_w_mlp1, m_w_mlp2, v_c_ctx, v_w_ada, v_b_ada, v_g_pre_mix, v_g_post_mix, v_g_pre_mlp, v_g_post_mlp, v_w_in, v_ret_decay, v_ret_gn, v_na_rpb, v_w_out, v_w_mlp1, v_w_mlp2):
    B, N, D = x.shape
    C = ctx.shape[1]
    T = C + N

    silu_all, mods_g, win_b, wout_l, w1_l, w2_l = _mod_gather(c, c_ctx, w_ada[0], b_ada, w_in[0].T, w_out[0],
                                                             w_mlp1[0], w_mlp2[0])
    mods_mine = mods_g.transpose(1, 0, 2).reshape(mods_g.shape[1], N_MOD * D)
    modl = jnp.concatenate([mods_mine[:B], mods_mine[SUBLANES:SUBLANES + 1]], axis=0)
    modl = modl.reshape(B + 1, N_MOD, 1, D)
    rin = w_in.shape[2]
    rout, c1, r2 = wout_l.shape[0], w1_l.shape[1], w2_l.shape[0]

    def rows_of(n):
        return lambda ref: _row_block(ref, n)

    def cols_of(n):
        return lambda ref: _col_block(ref, n)

    cos, sin = _rope_tables(C, N)
    onehot, valid, oh2 = _na_tables()
    bias2 = _paired_bias(na_rpb[0], onehot, valid)
    lg = jax.nn.log_sigmoid(ret_decay[0].astype(F32))

    ag = _SplitScatter([wout_l, w1_l, w2_l], [rows_of(rout), cols_of(c1), rows_of(r2)],
                       [(N_DEV * rout, D), (D, N_DEV * c1), (N_DEV * r2, D)], "ag_mlp_start",
                       kind="gather", masks=SIBLING + ICI_SAME_CORE)
    h_all, proj = _inproj_fwd(x, ctx, modl, g_pre_mix + ag.token[0, 0], win_b)
    o_ret, lat_ret = _ret_fwd(proj, cos, sin, lg, ret_gn, C)
    (lat_na,) = _na_fwd(proj, bias2, C)
    wout_part, w1_part, w2_part = _scatter_wait([ag], lat_na, "ag_mlp_wait")

    (dy1, dlat_ret, dlat_na, dmix, h2, act, du, dz, red_d) = _dense_core(
        lat_ret, lat_na, x, loss_target, modl, g_post_mix, g_pre_mlp, g_post_mlp, wout_part, w1_part, w2_part)

    gw_out_p = _tn_matmul(lat_ret[:, None], dmix, "gw_out_ret", rows_after=lat_na.shape[-1])
    gw_out_p = _tn_matmul(lat_na[:, None], dmix, "gw_out_na", rows_before=lat_ret.shape[-1], into=gw_out_p)
    gw1_p = _tn_matmul(h2[:, None], du, "gw_mlp1")
    gw2_p = _tn_matmul(act[:, None], dz, "gw_mlp2")
    rs_mlp = _SplitScatter([gw_out_p, gw1_p, gw2_p], [rows_of(rout), cols_of(c1), rows_of(r2)],
                           [(rout, D), (D, c1), (r2, D)], "rs_mlp_start")

    dret, dgn_p, dlg_p = _ret_bwd(proj, cos, sin, lg, ret_gn + rs_mlp.token[0, 0], o_ret, dlat_ret, C)
    dna, dbias2 = _na_bwd(proj, bias2, dlat_na, C)
    ret_cols, na_cols = dret.shape[1] * dret.shape[3], dna.shape[1] * dna.shape[3]
    gwin_t_p = _tn_matmul(dret, h_all, "gw_in_ret", rows_after=na_cols)
    gwin_t_p = _tn_matmul(dna, h_all, "gw_in_na", rows_before=ret_cols, into=gwin_t_p)
    rs_in = _SplitScatter([gwin_t_p], [rows_of(rin)], [(rin, D)], "rs_w_in_start")
    grad_x, red_i = _inproj_bwd(dret, dna, x, ctx, dy1, modl, g_pre_mix + rs_in.token[0, 0], win_b)

    rd = red_d
    nct = red_i.shape[1] * C // T
    ri_ctx = red_i[:, :nct].sum(axis=(0, 1))
    ri_lat = red_i[:, nct:].sum(axis=1)
    d_mods = jnp.concatenate([ri_lat[:, 0], ri_lat[:, 1], rd[:, 0], rd[:, 4], rd[:, 3], rd[:, 2]], axis=-1)
    d_cmods = jnp.concatenate([ri_ctx[0], ri_ctx[1], jnp.zeros(((N_MOD - 2) * D,), F32)])[None]
    dm_slot = jnp.concatenate([d_mods, d_cmods, jnp.zeros((SUBLANES - B - 1, N_MOD * D), F32)], axis=0)
    dg_pre_mix = ri_lat[:, 2].sum(axis=0) + ri_ctx[2]
    dg_post_mix = rd[:, 1].sum(axis=0)
    dg_pre_mlp = rd[:, 5].sum(axis=0)
    dg_post_mlp = rd[:, 6].sum(axis=0)
    loss_p = rd[:, 7, 0].sum()
    d_gn = dgn_p[:, 0].sum(axis=0)
    d_lg = dlg_p[:, :, :2, 0].sum(axis=0).T
    d_decay = d_lg * jax.nn.sigmoid(-ret_decay[0].astype(F32))
    rr = _rpb_reduce(dbias2, jnp.asarray(oh2, BF16)).reshape(NA_HEADS, 2 * NA_KH - 2, LANES)
    ncls = 2 * NA_KW - 1
    d_rpb = (jnp.pad(rr[:, :, :ncls], ((0, 0), (0, 1), (0, 0))) + jnp.pad(rr[:, :, 32:32 + ncls], ((0, 0), (1, 0), (0, 0))))
    d_rpb32 = jnp.pad(d_rpb, ((0, 0), (0, 0), (0, 32 - ncls)))
    pieces = [dg_pre_mix, dg_post_mix, dg_pre_mlp, dg_post_mlp, d_gn, d_rpb32.reshape(-1),
              jnp.pad(d_decay.reshape(-1), (0, LANES - d_decay.size)), jnp.full((LANES,), loss_p, F32)]
    vec = jnp.concatenate(pieces)
    pad = (-vec.shape[0]) % (SUBLANES * LANES)
    vec = jnp.pad(vec, (0, pad)).reshape(-1, LANES)
    tot, g_b_ada, g_w_ada, g_c_ctx = _small_ar(vec, dm_slot, silu_all, w_ada[0], c_ctx)
    land_out, land_1, land_2, land_in = _scatter_wait([rs_mlp, rs_in], tot, "rs_wait")
    g_w_in = _sum_slots(land_in, "sum_w_in").T
    fused = {"w_out": _sum_adamw(land_out, w_out[0], m_w_out[0], v_w_out[0], "sum_adamw_w_out"),
             "w_mlp1": _sum_adamw(land_1, w_mlp1[0], m_w_mlp1[0], v_w_mlp1[0], "sum_adamw_w_mlp1"),
             "w_mlp2": _sum_adamw(land_2, w_mlp2[0], m_w_mlp2[0], v_w_mlp2[0], "sum_adamw_w_mlp2")}
    flat = tot.reshape(-1)
    o0 = 0
    g_pre_mix_g = flat[o0:o0 + D]; o0 += D
    g_post_mix_g = flat[o0:o0 + D]; o0 += D
    g_pre_mlp_g = flat[o0:o0 + D]; o0 += D
    g_post_mlp_g = flat[o0:o0 + D]; o0 += D
    g_gn = flat[o0:o0 + RET_WIDTH]; o0 += RET_WIDTH
    nrpb = NA_HEADS * (2 * NA_KH - 1) * 32
    g_rpb = flat[o0:o0 + nrpb].reshape(NA_HEADS, 2 * NA_KH - 1, 32)[:, :, :ncls]; o0 += nrpb
    g_decay = flat[o0:o0 + 2 * RET_HEADS].reshape(2, RET_HEADS); o0 += LANES
    loss = flat[o0]

    grads = {
        "c_ctx": g_c_ctx.reshape(c_ctx.shape), "w_ada": g_w_ada[None], "b_ada": g_b_ada.reshape(b_ada.shape),
        "g_pre_mix": g_pre_mix_g[None], "g_post_mix": g_post_mix_g[None], "g_pre_mlp": g_pre_mlp_g[None],
        "g_post_mlp": g_post_mlp_g[None], "w_in": g_w_in[None], "ret_decay": g_decay[None], "ret_gn": g_gn[None],
        "na_rpb": g_rpb[None], "w_out": fused["w_out"][0][None], "w_mlp1": fused["w_mlp1"][0][None],
        "w_mlp2": fused["w_mlp2"][0][None],
    }
    weights = dict(c_ctx=c_ctx, w_ada=w_ada, b_ada=b_ada, g_pre_mix=g_pre_mix, g_post_mix=g_post_mix,
                   g_pre_mlp=g_pre_mlp, g_post_mlp=g_post_mlp, w_in=w_in, ret_decay=ret_decay, ret_gn=ret_gn,
                   na_rpb=na_rpb, w_out=w_out, w_mlp1=w_mlp1, w_mlp2=w_mlp2)
    m_in = dict(c_ctx=m_c_ctx, w_ada=m_w_ada, b_ada=m_b_ada, g_pre_mix=m_g_pre_mix, g_post_mix=m_g_post_mix,
                g_pre_mlp=m_g_pre_mlp, g_post_mlp=m_g_post_mlp, w_in=m_w_in, ret_decay=m_ret_decay,
                ret_gn=m_ret_gn, na_rpb=m_na_rpb, w_out=m_w_out, w_mlp1=m_w_mlp1, w_mlp2=m_w_mlp2)
    v_in = dict(c_ctx=v_c_ctx, w_ada=v_w_ada, b_ada=v_b_ada, g_pre_mix=v_g_pre_mix, g_post_mix=v_g_post_mix,
                g_pre_mlp=v_g_pre_mlp, g_post_mlp=v_g_post_mlp, w_in=v_w_in, ret_decay=v_ret_decay,
                ret_gn=v_ret_gn, na_rpb=v_na_rpb, w_out=v_w_out, w_mlp1=v_w_mlp1, w_mlp2=v_w_mlp2)
    names = list(weights)
    deltas, new_m, new_v = {}, {}, {}
    def as_2d(n):
        shp = weights[n].shape
        two_d = (-1, shp[-1]) if len(shp) > 1 else (1, shp[0])
        return [a.reshape(two_d) for a in (weights[n], grads[n], m_in[n], v_in[n])]

    small = [n for n in names if n not in fused and weights[n].size <= 65536]
    updated = dict(zip(small, _adamw_small([as_2d(n) for n in small], "adamw_small")))
    for n in names:
        if n in fused:
            updated[n] = fused[n][1:]
        elif n not in updated:
            updated[n] = _adamw(*as_2d(n), "adamw_" + n)
        deltas[n], new_m[n], new_v[n] = (a.reshape(weights[n].shape) for a in updated[n])
    return (loss, grad_x, *[grads[n] for n in names], *[deltas[n] for n in names],
            *[new_m[n] for n in names], *[new_v[n] for n in names])
```

```python
import functools
import math

import numpy as np
import jax
import jax.numpy as jnp
from jax import lax
from jax.experimental import pallas as pl
from jax.experimental.pallas import tpu as pltpu

F32 = jnp.float32
BF16 = jnp.bfloat16
MESH = pl.DeviceIdType.MESH

N_DEV = 8
LANES = 128
SUBLANES = 8
VMEM_LIMIT = 60 * 1024 * 1024

GRID_W = 64
RET_HEADS = 4
RET_DIM = 128
RET_WIDTH = RET_HEADS * RET_DIM
NA_HEADS = 8
NA_DIM = 64
NA_WIDTH = NA_HEADS * NA_DIM
NA_PAIRS = NA_HEADS // 2
NA_KH = 8
NA_KW = 16
NA_GROUP = 8
SEG = 512
ROPE_BASE = 10000.0
NORM_EPS = 1e-6
NEG_INF = -1e30
N_MOD = 6

ADAM_LR = 0.001
ADAM_B1 = 0.9
ADAM_B2 = 0.999
ADAM_EPS = 1e-08
ADAM_WD = 0.01
ADAM_STEP = 10


def _dot(a, b):
    return lax.dot_general(a, b, (((1,), (0,)), ((), ())), preferred_element_type=F32)


def _dot_nt(a, b):
    return lax.dot_general(a, b, (((1,), (1,)), ((), ())), preferred_element_type=F32)


def _dot_tn(a, b):
    return lax.dot_general(a, b, (((0,), (0,)), ((), ())), preferred_element_type=F32)


def _sigmoid(x):
    return 1.0 / (1.0 + jnp.exp(-x))


def _div_tile(n, cap, mult):
    if n <= cap:
        return n
    for t in range(cap - cap % mult, 0, -mult):
        if n % t == 0:
            return t
    raise ValueError(f"no tile for {n}")


def _params(*sem):
    return pltpu.CompilerParams(dimension_semantics=tuple(sem) if sem else None,
                                vmem_limit_bytes=VMEM_LIMIT)


def _vmem():
    return pl.BlockSpec(memory_space=pltpu.VMEM)


def _any():
    return pl.BlockSpec(memory_space=pl.ANY)


def _me_and_peers():
    x, y, c = lax.axis_index("x"), lax.axis_index("y"), lax.axis_index("c")
    me = 4 * x + 2 * y + c
    peers = []
    for m in range(1, N_DEV):
        px = 1 - x if (m >> 2) & 1 else x
        py = 1 - y if (m >> 1) & 1 else y
        pc = 1 - c if m & 1 else c
        peers.append(((px, py, pc), 4 * px + 2 * py + pc))
    return me, peers


def _exchange(src_for, dst_from, send_sems, recv_sems):
    me, peers = _me_and_peers()
    sent = []
    for i, (dev, pid) in enumerate(peers):
        cp = pltpu.make_async_remote_copy(src_ref=src_for(pid), dst_ref=dst_from(me),
                                          send_sem=send_sems.at[i], recv_sem=recv_sems.at[i],
                                          device_id=dev, device_id_type=MESH)
        cp.start()
        sent.append(cp)
    for i, (dev, pid) in enumerate(peers):
        pltpu.make_async_remote_copy(src_ref=src_for(pid), dst_ref=dst_from(pid),
                                     send_sem=send_sems.at[i], recv_sem=recv_sems.at[i],
                                     device_id=dev, device_id_type=MESH).wait_recv()
    for cp in sent:
        cp.wait_send()


SIBLING = (1,)
ICI_SAME_CORE = (2, 4, 6)
ALL_PEERS = tuple(range(1, N_DEV))


def _remote(src, dst, send_sem, recv_sem, dev):
    return pltpu.make_async_remote_copy(src_ref=src, dst_ref=dst, send_sem=send_sem, recv_sem=recv_sem,
                                        device_id=dev, device_id_type=MESH)


def _push_start(items, masks, send_sems, recv_sems):
    me, peers = _me_and_peers()
    for k, (src_for, dst_from) in enumerate(items):
        for m in masks:
            dev, pid = peers[m - 1]
            _remote(src_for(pid), dst_from(me), send_sems.at[k, m - 1], recv_sems.at[k, m - 1], dev).start()


def _push_wait_recv(items, masks, send_sems, recv_sems):
    me, peers = _me_and_peers()
    for k, (src_for, dst_from) in enumerate(items):
        for m in masks:
            dev, pid = peers[m - 1]
            _remote(src_for(pid), dst_from(pid), send_sems.at[k, m - 1], recv_sems.at[k, m - 1], dev).wait_recv()


def _push_wait_send(items, masks, send_sems, recv_sems):
    me, peers = _me_and_peers()
    for k, (src_for, dst_from) in enumerate(items):
        for m in masks:
            dev, pid = peers[m - 1]
            _remote(src_for(pid), dst_from(me), send_sems.at[k, m - 1], recv_sems.at[k, m - 1], dev).wait_send()


def _forward_start(items, send_sems, recv_sems):
    me, peers = _me_and_peers()
    sib = peers[0][0]
    for k, (blk_in, blk_out) in enumerate(items):
        for j, m in enumerate(ICI_SAME_CORE):
            pid = peers[m - 1][1]
            _remote(blk_in(pid), blk_out(pid), send_sems.at[k, j], recv_sems.at[k, j], sib).start()


def _forward_wait(items, send_sems, recv_sems):
    me, peers = _me_and_peers()
    sib = peers[0][0]
    for k, (blk_in, blk_out) in enumerate(items):
        for j, m in enumerate(ICI_SAME_CORE):
            got = peers[(m | 1) - 1][1]
            _remote(blk_in(got), blk_out(got), send_sems.at[k, j], recv_sems.at[k, j], sib).wait_recv()
    for k, (blk_in, blk_out) in enumerate(items):
        for j, m in enumerate(ICI_SAME_CORE):
            pid = peers[m - 1][1]
            _remote(blk_in(pid), blk_out(pid), send_sems.at[k, j], recv_sems.at[k, j], sib).wait_send()


def _mod_gather(c, c_ctx, w_ada, b_ada, w_in_t, w_out, w1, w2):
    B, D = c.shape
    ncol = w_ada.shape[1]
    rows = SUBLANES * N_DEV + SUBLANES

    def body(c_ref, cc_ref, w_ref, b_ref, win_ref, wout_ref, w1_ref, w2_ref,
             s_ref, m_ref, gin_ref, wout_b, w1_b, w2_b,
             win_b, msend, send1, recv1, send2, recv2, wsend, wrecv, fsend, frecv, lsem):
        me, _ = _me_and_peers()
        win_b[...] = win_ref[...].astype(BF16)
        block = _row_block(gin_ref, w_in_t.shape[0])
        gather = [(lambda p: win_b, block)]
        own = pltpu.make_async_copy(win_b, block(me), lsem.at[0])
        cv = c_ref[...]
        slot = jnp.concatenate([cv * _sigmoid(cv), jnp.zeros((SUBLANES - B, D), F32)], axis=0)
        my_rows = pl.ds(pl.multiple_of(me * SUBLANES, SUBLANES), SUBLANES)
        s_ref[my_rows, :] = slot
        ccv = cc_ref[...]
        s_ref[SUBLANES * N_DEV:, :] = jnp.concatenate(
            [ccv * _sigmoid(ccv), jnp.zeros((SUBLANES - 1, D), F32)], axis=0)

        def rows_of(p):
            return s_ref.at[pl.ds(pl.multiple_of(p * SUBLANES, SUBLANES), SUBLANES), :]

        _exchange(lambda p: rows_of(me), rows_of, send1, recv1)
        own.start()
        _push_start(gather, SIBLING + ICI_SAME_CORE, wsend, wrecv)
        wout_b[...] = wout_ref[...].astype(BF16)
        w1_b[...] = w1_ref[...].astype(BF16)
        w2_b[...] = w2_ref[...].astype(BF16)
        b_loc = b_ref[:, pl.ds(pl.multiple_of(me * ncol, ncol), ncol)]
        mods = _dot(s_ref[...], w_ref[...]) + b_loc
        for p in range(N_DEV):
            msend[p] = jnp.concatenate([mods[p * SUBLANES:(p + 1) * SUBLANES], mods[N_DEV * SUBLANES:]], axis=0)
        m_ref[me] = msend[me]
        columns = [(lambda p: msend.at[p], lambda p: m_ref.at[p])]
        _push_start(columns, ALL_PEERS, send2, recv2)
        _push_wait_recv(gather, ICI_SAME_CORE, wsend, wrecv)
        relay = [(block, block)]
        _forward_start(relay, fsend, frecv)
        _push_wait_recv(columns, ALL_PEERS, send2, recv2)
        _push_wait_recv(gather, SIBLING, wsend, wrecv)
        _forward_wait(relay, fsend, frecv)
        _push_wait_send(columns, ALL_PEERS, send2, recv2)
        _push_wait_send(gather, SIBLING + ICI_SAME_CORE, wsend, wrecv)
        own.wait()

    return pl.pallas_call(
        body, name="mod_gather",
        out_shape=(jax.ShapeDtypeStruct((rows, D), F32), jax.ShapeDtypeStruct((N_DEV, 2 * SUBLANES, ncol), F32),
                   jax.ShapeDtypeStruct((N_DEV * w_in_t.shape[0], D), BF16),
                   jax.ShapeDtypeStruct(w_out.shape, BF16), jax.ShapeDtypeStruct(w1.shape, BF16),
                   jax.ShapeDtypeStruct(w2.shape, BF16)),
        in_specs=[_vmem()] * 8, out_specs=(_vmem(), _vmem(), _any(), _vmem(), _vmem(), _vmem()),
        scratch_shapes=[pltpu.VMEM(w_in_t.shape, BF16), pltpu.VMEM((N_DEV, 2 * SUBLANES, ncol), F32)]
                       + [pltpu.SemaphoreType.DMA((N_DEV - 1,))] * 2
                       + [pltpu.SemaphoreType.DMA((1, N_DEV - 1))] * 4 + [pltpu.SemaphoreType.DMA((1, 3))] * 2
                       + [pltpu.SemaphoreType.DMA((1,))],
        compiler_params=pltpu.CompilerParams(vmem_limit_bytes=VMEM_LIMIT),
    )(c, c_ctx.reshape(1, D), w_ada, b_ada, w_in_t, w_out, w1, w2)


def _row_block(ref, rows):
    return lambda p: ref.at[pl.ds(pl.multiple_of(p * rows, 2 * SUBLANES), rows), :]


def _col_block(ref, cols):
    return lambda p: ref.at[:, pl.ds(pl.multiple_of(p * cols, LANES), cols)]


def _slot(ref):
    return lambda p: ref.at[p]


def _grid_call(body, *, name, grid, out_shape, in_specs, out_specs, scratch_shapes, args):
    return pl.pallas_call(
        body, name=name, grid=grid, out_shape=tuple(out_shape), in_specs=list(in_specs), out_specs=tuple(out_specs),
        scratch_shapes=list(scratch_shapes), compiler_params=_params(*(("arbitrary",) * len(grid))),
    )(*args)


def _token_tiles(n_ctx, tm):
    nct = n_ctx // tm

    def ctx_spec(D):
        return pl.BlockSpec((None, tm, D), lambda b, t: (b, jnp.minimum(t, nct - 1), 0))

    def lat_spec(D):
        return pl.BlockSpec((None, tm, D), lambda b, t: (b, jnp.maximum(t - nct, 0), 0))

    return nct, ctx_spec, lat_spec


def _inproj_fwd(x, ctx, modl, g1, w_in_t):
    B, N, D = x.shape
    n_ctx = ctx.shape[1]
    T = n_ctx + N
    nw = w_in_t.shape[0]
    tm = _div_tile(n_ctx, 256, 16)
    nct, ctx_spec, lat_spec = _token_tiles(n_ctx, tm)

    def body(c_ref, x_ref, sh_ref, sc_ref, g_ref, w_ref, h_ref, p_ref):
        x = jnp.where(pl.program_id(1) < nct, c_ref[...], x_ref[...])
        r = lax.rsqrt(jnp.mean(x * x, axis=-1, keepdims=True) + NORM_EPS)
        h = ((x * r) * g_ref[...]) * (1.0 + sc_ref[...]) + sh_ref[...]
        hb = h.astype(BF16)
        h_ref[...] = hb
        p_ref[...] = _dot_nt(hb, w_ref[...]).astype(BF16)

    def mrow(b, t):
        return jnp.where(t < nct, B, b)

    return _grid_call(
        body, name="inproj_fwd", grid=(B, T // tm),
        out_shape=(jax.ShapeDtypeStruct((B, T, D), BF16), jax.ShapeDtypeStruct((B, T, nw), BF16)),
        in_specs=[ctx_spec(D), lat_spec(D),
                  pl.BlockSpec((None, None, 1, D), lambda b, t: (mrow(b, t), 0, 0, 0)),
                  pl.BlockSpec((None, None, 1, D), lambda b, t: (mrow(b, t), 1, 0, 0)),
                  pl.BlockSpec((1, D), lambda b, t: (0, 0)),
                  pl.BlockSpec((nw, D), lambda b, t: (0, 0))],
        out_specs=(pl.BlockSpec((None, tm, D), lambda b, t: (b, t, 0)),
                   pl.BlockSpec((None, tm, nw), lambda b, t: (b, t, 0))),
        scratch_shapes=[], args=(ctx, x, modl, modl, g1, w_in_t))


def _swap32(x):
    lane = lax.broadcasted_iota(jnp.int32, x.shape, 1)
    return jnp.where((lane % 64) < 32, pltpu.roll(x, 96, 1), pltpu.roll(x, 32, 1))


def _rope(x, cos, sin):
    return x * cos + _swap32(x) * sin


def _unrope(dy, cos, sin):
    return dy * cos + _swap32(dy * sin)


def _ret_weights(lgf, lgb, dist):
    return jnp.exp(jnp.where(dist >= 0.0, lgf * dist, -lgb * dist))


class _RetDecay:
    def __init__(self, lgf, lgb, rows):
        r = lax.broadcasted_iota(jnp.int32, (rows, RET_DIM), 0).astype(F32)
        self.head = r + 1.0
        self.tail = (rows - 1.0) - r
        self.q_f = jnp.exp(lgf * self.head)
        self.k_f = jnp.exp(lgf * self.tail)
        self.q_b = jnp.exp(lgb * self.tail)
        self.k_b = jnp.exp(lgb * self.head)


def _ret_states(kf32, vs, lgf, lgb, C, c, nt, hf, hb, hfa=None, hba=None):
    dec = _RetDecay(lgf, lgb, c)
    dec_c = _RetDecay(lgf, lgb, C)
    step_f = jnp.exp(jnp.zeros((RET_DIM, RET_DIM), F32) + lgf * c)
    step_b = jnp.exp(jnp.zeros((RET_DIM, RET_DIM), F32) + lgb * c)

    def upd(rows, kdec):
        return _dot_tn((kf32[rows, :] * kdec).astype(BF16), vs[rows, :])

    def lat(t):
        return slice(C + t * c, C + (t + 1) * c)

    state = upd(slice(0, C), dec_c.k_f)
    aged = jnp.zeros_like(state)
    for t in range(nt):
        hf[t] = state.astype(BF16)
        if hfa is not None:
            hfa[t] = aged
        if t < nt - 1:
            aged = step_f * (aged + c * state)
            state = step_f * state + upd(lat(t), dec.k_f)
    state = upd(slice(0, C), dec_c.k_b)
    aged = jnp.zeros_like(state)
    for t in range(nt - 1, -1, -1):
        hb[t] = state.astype(BF16)
        if hba is not None:
            hba[t] = aged
        if t > 0:
            aged = step_b * (aged + c * state)
            state = step_b * state + upd(lat(t), dec.k_b)
    return dec, dec_c, step_f, step_b


def _ret_fwd(proj, cos, sin, lg, gn, n_ctx):
    B, T, _ = proj.shape
    C = n_ctx
    N = T - C
    c = _div_tile(N, 256, 16)
    nt = N // c
    scale = RET_DIM ** -0.5

    def body(lg_ref, q_ref, k_ref, v_ref, g_ref, cos_ref, sin_ref, gn_ref, o_ref, lat_ref, qs, ks, vs, kf32, hf, hb):
        h = pl.program_id(1)
        lgf = lg_ref[0, h]
        lgb = lg_ref[1, h]
        for rows in [slice(0, C)] + [slice(C + t * c, C + (t + 1) * c) for t in range(nt)]:
            cosb = cos_ref[rows, :]
            sinb = sin_ref[rows, :]
            qs[rows, :] = (_rope(q_ref[rows, :].astype(F32), cosb, sinb) * scale).astype(BF16)
            kr = _rope(k_ref[rows, :].astype(F32), cosb, sinb)
            kf32[rows, :] = kr
            ks[rows, :] = kr.astype(BF16)
            vs[rows, :] = v_ref[rows, :].astype(BF16)
        gnv = gn_ref[...]
        dec, _, _, _ = _ret_states(kf32, vs, lgf, lgb, C, c, nt, hf, hb)
        rc = (lax.broadcasted_iota(jnp.int32, (c, c), 0) - lax.broadcasted_iota(jnp.int32, (c, c), 1)).astype(F32)
        w_diag = _ret_weights(lgf, lgb, rc)
        for t in range(nt):
            rows = slice(C + t * c, C + (t + 1) * c)
            qt = qs[rows, :]
            s = _dot_nt(qt, ks[rows, :])
            o = (_dot((s * w_diag).astype(BF16), vs[rows, :])
                 + dec.q_f * _dot(qt, hf[t]) + dec.q_b * _dot(qt, hb[t]))
            o_ref[t * c:(t + 1) * c, :] = o
            mu = jnp.mean(o, axis=-1, keepdims=True)
            oc = o - mu
            var = jnp.mean(oc * oc, axis=-1, keepdims=True)
            yh = oc * lax.rsqrt(var + NORM_EPS)
            g = g_ref[rows, :].astype(F32)
            lat_ref[t * c:(t + 1) * c, :] = ((yh * gnv) * (g * _sigmoid(g))).astype(BF16)

    def col(seg):
        return pl.BlockSpec((None, T, RET_DIM), lambda b, h, seg=seg: (b, 0, seg * RET_HEADS + h))

    return _grid_call(
        body, name="ret_fwd", grid=(B, RET_HEADS),
        out_shape=(jax.ShapeDtypeStruct((B, N, RET_WIDTH), F32), jax.ShapeDtypeStruct((B, N, RET_WIDTH), BF16)),
        in_specs=[pl.BlockSpec(memory_space=pltpu.SMEM), col(0), col(1), col(2), col(3),
                  pl.BlockSpec((T, RET_DIM), lambda b, h: (0, 0)), pl.BlockSpec((T, RET_DIM), lambda b, h: (0, 0)),
                  pl.BlockSpec((1, RET_DIM), lambda b, h: (0, h))],
        out_specs=(pl.BlockSpec((None, N, RET_DIM), lambda b, h: (b, 0, h)),
                   pl.BlockSpec((None, N, RET_DIM), lambda b, h: (b, 0, h))),
        scratch_shapes=[pltpu.VMEM((T, RET_DIM), BF16)] * 3 + [pltpu.VMEM((T, RET_DIM), F32)]
                       + [pltpu.VMEM((nt, RET_DIM, RET_DIM), BF16)] * 2,
        args=(lg, proj, proj, proj, proj, cos, sin, gn))


def _ret_bwd(proj, cos, sin, lg, gn, o, dlat, n_ctx):
    B, T, _ = proj.shape
    C = n_ctx
    N = T - C
    c = _div_tile(N, 256, 16)
    nt = N // c
    scale = RET_DIM ** -0.5

    def lat(t):
        return slice(C + t * c, C + (t + 1) * c)

    def body(lg_ref, q_ref, k_ref, v_ref, g_ref, cos_ref, sin_ref, gn_ref, o_ref, dl_ref,
             d_ref, dgn_ref, dlg_ref, qs, ks, vs, dos, qf32, kf32, hf, hb, hfa, hba, gf_s, gb_s):
        h = pl.program_id(1)
        lgf = lg_ref[0, h]
        lgb = lg_ref[1, h]
        gnv = gn_ref[...]

        def fold(a):
            return jnp.sum(a.reshape(a.shape[0] // SUBLANES, SUBLANES, a.shape[1]), axis=0)

        for rows in [slice(0, C)] + [lat(t) for t in range(nt)]:
            cosb = cos_ref[rows, :]
            sinb = sin_ref[rows, :]
            qr = _rope(q_ref[rows, :].astype(F32), cosb, sinb) * scale
            qf32[rows, :] = qr
            qs[rows, :] = qr.astype(BF16)
            kr = _rope(k_ref[rows, :].astype(F32), cosb, sinb)
            kf32[rows, :] = kr
            ks[rows, :] = kr.astype(BF16)
            vs[rows, :] = v_ref[rows, :].astype(BF16)

        dgn = jnp.zeros((1, RET_DIM), F32)
        for t in range(nt):
            lrows = slice(t * c, (t + 1) * c)
            ov = o_ref[lrows, :]
            mu = jnp.mean(ov, axis=-1, keepdims=True)
            oc = ov - mu
            var = jnp.mean(oc * oc, axis=-1, keepdims=True)
            rstd = lax.rsqrt(var + NORM_EPS)
            yh = oc * rstd
            g = g_ref[lat(t), :].astype(F32)
            sg = _sigmoid(g)
            dl = dl_ref[lrows, :]
            d_ref[3, lat(t), :] = (dl * (yh * gnv) * (sg * (1.0 + g * (1.0 - sg)))).astype(BF16)
            dls = dl * (g * sg)
            dgn = dgn + jnp.sum(dls * yh, axis=0, keepdims=True)
            dyh = dls * gnv
            do = rstd * (dyh - jnp.mean(dyh, axis=-1, keepdims=True)
                         - yh * jnp.mean(dyh * yh, axis=-1, keepdims=True))
            dos[lrows, :] = do.astype(BF16)
        dgn_ref[...] = jnp.concatenate([dgn, jnp.zeros((SUBLANES - 1, RET_DIM), F32)], axis=0)
        d_ref[3, 0:C, :] = jnp.zeros((C, RET_DIM), BF16)
        d_ref[0, 0:C, :] = jnp.zeros((C, RET_DIM), BF16)

        dec, dec_c, step_f, step_b = _ret_states(kf32, vs, lgf, lgb, C, c, nt, hf, hb, hfa, hba)

        def zmat(t, qdec):
            return _dot_tn((qf32[lat(t), :] * qdec).astype(BF16), dos[t * c:(t + 1) * c, :])

        acc3f = jnp.zeros((RET_DIM, RET_DIM), F32)
        acc3b = jnp.zeros((RET_DIM, RET_DIM), F32)
        state = jnp.zeros((RET_DIM, RET_DIM), F32)
        for t in range(nt - 1, -1, -1):
            gf_s[t] = state.astype(BF16)
            z = zmat(t, dec.q_f)
            acc3f = acc3f + hfa[t] * z
            state = step_f * state + z
        gctx_f = state.astype(BF16)
        state = jnp.zeros((RET_DIM, RET_DIM), F32)
        for t in range(nt):
            gb_s[t] = state.astype(BF16)
            z = zmat(t, dec.q_b)
            acc3b = acc3b + hba[t] * z
            state = step_b * state + z
        gctx_b = state.astype(BF16)

        rc = (lax.broadcasted_iota(jnp.int32, (c, c), 0) - lax.broadcasted_iota(jnp.int32, (c, c), 1)).astype(F32)
        w_diag = _ret_weights(lgf, lgb, rc)
        wg_f = jnp.where(rc >= 0.0, w_diag * rc, 0.0)
        wg_b = jnp.where(rc < 0.0, -w_diag * rc, 0.0)
        accf = jnp.zeros((SUBLANES, RET_DIM), F32)
        accb = jnp.zeros((SUBLANES, RET_DIM), F32)
        gdf = jnp.zeros((SUBLANES, c), F32)
        gdb = jnp.zeros((SUBLANES, c), F32)
        for t in range(nt):
            rows = lat(t)
            qt = qs[rows, :]
            kt = ks[rows, :]
            vt = vs[rows, :]
            dot = dos[t * c:(t + 1) * c, :]
            s = _dot_nt(qt, kt)
            dp = _dot_nt(dot, vt)
            dv = _dot_tn((s * w_diag).astype(BF16), dot)
            ds = (dp * w_diag).astype(BF16)
            dq = _dot(ds, kt)
            dk = _dot_tn(ds, qt)
            gs = dp * s
            gdf = gdf + fold(gs * wg_f)
            gdb = gdb + fold(gs * wg_b)
            qv = qf32[rows, :]
            kv = kf32[rows, :]
            dq_f = dec.q_f * _dot_nt(dot, hf[t])
            dq_b = dec.q_b * _dot_nt(dot, hb[t])
            dk_f = dec.k_f * _dot_nt(vt, gf_s[t])
            dk_b = dec.k_b * _dot_nt(vt, gb_s[t])
            accf = accf + fold(dec.head * dq_f * qv) + fold(dec.tail * dk_f * kv)
            accb = accb + fold(dec.tail * dq_b * qv) + fold(dec.head * dk_b * kv)
            dv = dv + dec.k_f * _dot(kt, gf_s[t]) + dec.k_b * _dot(kt, gb_s[t])
            cosb = cos_ref[rows, :]
            sinb = sin_ref[rows, :]
            d_ref[0, rows, :] = _unrope((dq + dq_f + dq_b) * scale, cosb, sinb).astype(BF16)
            d_ref[1, rows, :] = _unrope(dk + dk_f + dk_b, cosb, sinb).astype(BF16)
            d_ref[2, rows, :] = dv.astype(BF16)
        kc = ks[0:C, :]
        vc = vs[0:C, :]
        kcv = kf32[0:C, :]
        dkc_f = dec_c.k_f * _dot_nt(vc, gctx_f)
        dkc_b = dec_c.k_b * _dot_nt(vc, gctx_b)
        accf = accf + fold(dec_c.tail * dkc_f * kcv)
        accb = accb + fold(dec_c.head * dkc_b * kcv)
        d_ref[1, 0:C, :] = (dkc_f + dkc_b).astype(BF16)
        d_ref[2, 0:C, :] = (dec_c.k_f * _dot(kc, gctx_f) + dec_c.k_b * _dot(kc, gctx_b)).astype(BF16)
        gf = jnp.sum(gdf) + jnp.sum(accf) + jnp.sum(acc3f)
        gb = jnp.sum(gdb) + jnp.sum(accb) + jnp.sum(acc3b)
        row = lax.broadcasted_iota(jnp.int32, (SUBLANES, LANES), 0)
        dlg_ref[...] = jnp.where(row == 0, gf, jnp.where(row == 1, gb, 0.0))

    def col(seg):
        return pl.BlockSpec((None, T, RET_DIM), lambda b, h, seg=seg: (b, 0, seg * RET_HEADS + h))

    return _grid_call(
        body, name="ret_bwd", grid=(B, RET_HEADS),
        out_shape=(jax.ShapeDtypeStruct((B, 4, T, RET_WIDTH), BF16),
                   jax.ShapeDtypeStruct((B, SUBLANES, RET_WIDTH), F32),
                   jax.ShapeDtypeStruct((B, RET_HEADS, SUBLANES, LANES), F32)),
        in_specs=[pl.BlockSpec(memory_space=pltpu.SMEM), col(0), col(1), col(2), col(3),
                  pl.BlockSpec((T, RET_DIM), lambda b, h: (0, 0)), pl.BlockSpec((T, RET_DIM), lambda b, h: (0, 0)),
                  pl.BlockSpec((1, RET_DIM), lambda b, h: (0, h)),
                  pl.BlockSpec((None, N, RET_DIM), lambda b, h: (b, 0, h)),
                  pl.BlockSpec((None, N, RET_DIM), lambda b, h: (b, 0, h))],
        out_specs=(pl.BlockSpec((None, 4, T, RET_DIM), lambda b, h: (b, 0, 0, h)),
                   pl.BlockSpec((None, SUBLANES, RET_DIM), lambda b, h: (b, 0, h)),
                   pl.BlockSpec((None, None, SUBLANES, LANES), lambda b, h: (b, h, 0, 0))),
        scratch_shapes=[pltpu.VMEM((T, RET_DIM), BF16)] * 3 + [pltpu.VMEM((N, RET_DIM), BF16)]
                       + [pltpu.VMEM((T, RET_DIM), F32)] * 2
                       + [pltpu.VMEM((nt, RET_DIM, RET_DIM), BF16)] * 2 + [pltpu.VMEM((nt, RET_DIM, RET_DIM), F32)] * 2
                       + [pltpu.VMEM((nt, RET_DIM, RET_DIM), BF16)] * 2,
        args=(lg, proj, proj, proj, proj, cos, sin, gn, o, dlat))


def _na_geometry(rows):
    kh = min(NA_KH, rows)
    return kh, kh * GRID_W


def _pair_select():
    lane = lax.broadcasted_iota(jnp.int32, (2 * GRID_W, LANES), 1)
    row = lax.broadcasted_iota(jnp.int32, (2 * GRID_W, LANES), 0)
    return (lane >= NA_DIM) == (row >= GRID_W)


def _pair_bias(bias_ref, dr0, kh):
    return jnp.concatenate(
        [jnp.concatenate([bias_ref[e, pl.ds(dr0 + 2 * m, 1)].reshape(GRID_W, LANES) for m in range(kh // 2)], axis=1)
         for e in range(2)], axis=0)


def _na_softmax(s_loc, s_ctx):
    mx = jnp.maximum(jnp.max(s_loc, axis=-1, keepdims=True), jnp.max(s_ctx, axis=-1, keepdims=True))
    p_loc = jnp.exp(s_loc - mx)
    p_ctx = jnp.exp(s_ctx - mx)
    den = jnp.sum(p_loc, axis=-1, keepdims=True) + jnp.sum(p_ctx, axis=-1, keepdims=True)
    return p_loc, p_ctx, den


def _na_fwd(proj, bias2, n_ctx):
    B, T, _ = proj.shape
    C = n_ctx
    N = T - C
    R = N // GRID_W
    kh, nk = _na_geometry(R)
    scale = NA_DIM ** -0.5
    base = (4 * RET_WIDTH) // LANES

    def body(q_ref, k_ref, v_ref, bias_ref, out_ref, kb16, vb16):
        kb16[...] = k_ref[...].astype(BF16)
        vb16[...] = v_ref[...].astype(BF16)
        kc = kb16[0:C, :]
        vc = vb16[0:C, :]
        lane = lax.broadcasted_iota(jnp.int32, (GRID_W, LANES), 1)
        sel2 = _pair_select()

        def group(gi, carry):
            pre = []
            for u in range(NA_GROUP):
                r = gi * NA_GROUP + u
                bs = jnp.clip(r - kh // 2, 0, R - kh)
                dr0 = bs - r + (NA_KH - 1)
                q = q_ref[pl.ds(pl.multiple_of(C + r * GRID_W, GRID_W), GRID_W), :].astype(F32) * scale
                q2 = jnp.where(sel2, jnp.concatenate([q, q], axis=0), 0.0).astype(BF16)
                band = pl.ds(pl.multiple_of(C + bs * GRID_W, GRID_W), nk)
                s_loc = _dot_nt(q2, kb16[band, :]) + _pair_bias(bias_ref, dr0, kh)
                s_ctx = _dot_nt(q2, kc)
                pre.append((r, band, s_loc, s_ctx))
            mid = [(r, band) + _na_softmax(s_loc, s_ctx) for r, band, s_loc, s_ctx in pre]
            for r, band, p_loc, p_ctx, den in mid:
                o2 = (_dot(p_loc.astype(BF16), vb16[band, :]) + _dot(p_ctx.astype(BF16), vc)) / den
                out_ref[pl.ds(pl.multiple_of(r * GRID_W, GRID_W), GRID_W), :] = jnp.where(
                    lane < NA_DIM, o2[:GRID_W], o2[GRID_W:]).astype(BF16)
            return carry

        lax.fori_loop(0, R // NA_GROUP, group, 0)

    def col(seg):
        return pl.BlockSpec((None, T, LANES), lambda b, p, seg=seg: (b, 0, base + seg * NA_PAIRS + p))

    return _grid_call(
        body, name="na_fwd", grid=(B, NA_PAIRS),
        out_shape=(jax.ShapeDtypeStruct((B, N, NA_WIDTH), BF16),),
        in_specs=[col(0), col(1), col(2),
                  pl.BlockSpec((2, 2 * NA_KH - 2, GRID_W, LANES), lambda b, p: (p, 0, 0, 0))],
        out_specs=(pl.BlockSpec((None, N, LANES), lambda b, p: (b, 0, p)),),
        scratch_shapes=[pltpu.VMEM((T, LANES), BF16)] * 2,
        args=(proj, proj, proj, bias2))


def _na_bwd(proj, bias2, dlat, n_ctx):
    B, T, _ = proj.shape
    C = n_ctx
    N = T - C
    R = N // GRID_W
    kh, nk = _na_geometry(R)
    scale = NA_DIM ** -0.5
    base = (4 * RET_WIDTH) // LANES

    def body(q_ref, k_ref, v_ref, bias_ref, dl_ref, d_ref, db_ref, kb16, vb16, dkv):
        b = pl.program_id(1)
        kb16[...] = k_ref[...].astype(BF16)
        vb16[...] = v_ref[...].astype(BF16)
        kc = kb16[0:C, :]
        vc = vb16[0:C, :]
        lane = lax.broadcasted_iota(jnp.int32, (GRID_W, LANES), 1)
        dkv[...] = jnp.zeros(dkv.shape, F32)
        d_ref[0, 0:C, :] = jnp.zeros((C, LANES), BF16)

        @pl.when(b == 0)
        def _():
            db_ref[...] = jnp.zeros(db_ref.shape, F32)

        sel2 = _pair_select()

        def group(gi, carry):
            pre = []
            for u in range(NA_GROUP):
                r = gi * NA_GROUP + u
                bs = jnp.clip(r - kh // 2, 0, R - kh)
                dr0 = bs - r + (NA_KH - 1)
                q = q_ref[pl.ds(pl.multiple_of(C + r * GRID_W, GRID_W), GRID_W), :].astype(F32) * scale
                do = dl_ref[pl.ds(pl.multiple_of(r * GRID_W, GRID_W), GRID_W), :]
                q2 = jnp.where(sel2, jnp.concatenate([q, q], axis=0), 0.0).astype(BF16)
                do2 = jnp.where(sel2, jnp.concatenate([do, do], axis=0), 0.0).astype(BF16)
                band = pl.ds(pl.multiple_of(C + bs * GRID_W, GRID_W), nk)
                s_loc = _dot_nt(q2, kb16[band, :]) + _pair_bias(bias_ref, dr0, kh)
                s_ctx = _dot_nt(q2, kc)
                dp_loc = _dot_nt(do2, vb16[band, :])
                dp_ctx = _dot_nt(do2, vc)
                pre.append((r, dr0, band, q2, do2, s_loc, s_ctx, dp_loc, dp_ctx))
            mid = []
            for r, dr0, band, q2, do2, s_loc, s_ctx, dp_loc, dp_ctx in pre:
                p_loc, p_ctx, den = _na_softmax(s_loc, s_ctx)
                inv = 1.0 / den
                p_loc = p_loc * inv
                p_ctx = p_ctx * inv
                delta = (jnp.sum(p_loc * dp_loc, axis=-1, keepdims=True)
                         + jnp.sum(p_ctx * dp_ctx, axis=-1, keepdims=True))
                ds_loc = p_loc * (dp_loc - delta)
                ds_ctx = p_ctx * (dp_ctx - delta)
                mid.append((r, dr0, band, q2, do2, p_loc.astype(BF16), p_ctx.astype(BF16), ds_loc, ds_ctx))
            for r, dr0, band, q2, do2, pb_loc, pb_ctx, ds_loc, ds_ctx in mid:
                dsb_loc = ds_loc.astype(BF16)
                dsb_ctx = ds_ctx.astype(BF16)
                dq2 = _dot(dsb_loc, kb16[band, :]) + _dot(dsb_ctx, kc)
                d_ref[0, pl.ds(pl.multiple_of(C + r * GRID_W, GRID_W), GRID_W), :] = (jnp.where(
                    lane < NA_DIM, dq2[:GRID_W], dq2[GRID_W:]) * scale).astype(BF16)
                dkv[0, band, :] += _dot_tn(dsb_loc, q2)
                dkv[1, band, :] += _dot_tn(pb_loc, do2)
                dkv[0, 0:C, :] += _dot_tn(dsb_ctx, q2)
                dkv[1, 0:C, :] += _dot_tn(pb_ctx, do2)
                for e in range(2):
                    for m in range(kh // 2):
                        db_ref[e, pl.ds(dr0 + 2 * m, 1)] += ds_loc[e * GRID_W:(e + 1) * GRID_W,
                                                                   m * LANES:(m + 1) * LANES].reshape(1, GRID_W, LANES)
            return carry

        lax.fori_loop(0, R // NA_GROUP, group, 0)
        d_ref[1] = dkv[0].astype(BF16)
        d_ref[2] = dkv[1].astype(BF16)

    def col(seg):
        return pl.BlockSpec((None, T, LANES), lambda p, b, seg=seg: (b, 0, base + seg * NA_PAIRS + p))

    return _grid_call(
        body, name="na_bwd", grid=(NA_PAIRS, B),
        out_shape=(jax.ShapeDtypeStruct((B, 3, T, NA_WIDTH), BF16),
                   jax.ShapeDtypeStruct((NA_HEADS, 2 * NA_KH - 2, GRID_W, LANES), F32)),
        in_specs=[col(0), col(1), col(2),
                  pl.BlockSpec((2, 2 * NA_KH - 2, GRID_W, LANES), lambda p, b: (p, 0, 0, 0)),
                  pl.BlockSpec((None, N, LANES), lambda p, b: (b, 0, p))],
        out_specs=(pl.BlockSpec((None, 3, T, LANES), lambda p, b: (b, 0, 0, p)),
                   pl.BlockSpec((2, 2 * NA_KH - 2, GRID_W, LANES), lambda p, b: (p, 0, 0, 0))),
        scratch_shapes=[pltpu.VMEM((T, LANES), BF16)] * 2 + [pltpu.VMEM((2, T, LANES), F32)],
        args=(proj, proj, proj, bias2, dlat))


def _split3(a):
    hi = a.astype(BF16)
    r1 = a - hi.astype(F32)
    mid = r1.astype(BF16)
    lo = (r1 - mid.astype(F32)).astype(BF16)
    return hi, mid, lo


def _rpb_reduce(dbias2, onehot2):
    rows = dbias2.shape[0] * dbias2.shape[1]
    flat = dbias2.reshape(rows, GRID_W * LANES)

    def body(a_ref, oh_ref, o_ref):
        hi, mid, lo = _split3(a_ref[...])
        oh = oh_ref[...]
        o_ref[...] = _dot(hi, oh) + _dot(mid, oh) + _dot(lo, oh)

    return pl.pallas_call(
        body, name="rpb_reduce", out_shape=jax.ShapeDtypeStruct((rows, LANES), F32),
        in_specs=[_vmem(), _vmem()], out_specs=_vmem(),
        compiler_params=pltpu.CompilerParams(vmem_limit_bytes=VMEM_LIMIT),
    )(flat, onehot2)


def _dense_core(lat_ret, lat_na, x, tgt, modl, g_post_mix, g_pre_mlp, g_post_mlp, w_out, w1, w2):
    B, N, D = x.shape
    F = w1.shape[1]
    wout_rows, w1_cols, w2_rows = w_out.shape[0] // N_DEV, w1.shape[1] // N_DEV, w2.shape[0] // N_DEV
    mixw = w_out.shape[0]
    half = mixw // 2
    tm = _div_tile(N, 256, 16)
    nt = N // tm
    fc = _div_tile(F, 1024, LANES)

    def body(lr_ref, ln_ref, x_ref, t_ref, gt1_ref, sh2_ref, sc2_ref, gt2_ref, gpm_ref, gpre_ref, gpo_ref,
             wout_part, w1_part, w2_part,
             dy1_ref, dlr_ref, dln_ref, dmix_ref, h2_ref, a_ref, du_ref, dz_ref, red_ref, wout_hbm, w1_hbm, w2_hbm,
             wout_v, w1_v, w2_v, u_s, sems, fsend, frecv):
        @pl.when((pl.program_id(0) == 0) & (pl.program_id(1) == 0))
        def _():
            relay = [(_row_block(wout_part, wout_rows), _row_block(wout_hbm, wout_rows)),
                     (_col_block(w1_part, w1_cols), _col_block(w1_hbm, w1_cols)),
                     (_row_block(w2_part, w2_rows), _row_block(w2_hbm, w2_rows))]
            _forward_start(relay, fsend, frecv)
            _forward_wait(relay, fsend, frecv)
            cps = [pltpu.make_async_copy(wout_hbm, wout_v, sems.at[0]),
                   pltpu.make_async_copy(w1_hbm, w1_v, sems.at[1]),
                   pltpu.make_async_copy(w2_hbm, w2_v, sems.at[2])]
            for cp in cps:
                cp.start()
            for cp in cps:
                cp.wait()

        @pl.when(pl.program_id(1) == 0)
        def _():
            red_ref[...] = jnp.zeros(red_ref.shape, F32)

        gt1 = gt1_ref[...]
        sh2 = sh2_ref[...]
        sc2 = sc2_ref[...]
        gt2 = gt2_ref[...]
        gpm = gpm_ref[...]
        gpre = gpre_ref[...]
        gpo = gpo_ref[...]

        def rowmean(a):
            return jnp.mean(a, axis=-1, keepdims=True)

        def colsum(a):
            return jnp.sum(a, axis=0, keepdims=True)

        mix = _dot(lr_ref[...], wout_v[0:half, :]) + _dot(ln_ref[...], wout_v[half:, :])
        x = x_ref[...]
        rm = lax.rsqrt(rowmean(mix * mix) + NORM_EPS)
        mh = mix * rm
        nm = mh * gpm
        y1 = x + gt1 * nm
        r1 = lax.rsqrt(rowmean(y1 * y1) + NORM_EPS)
        xh = y1 * r1
        n1 = xh * gpre
        h2b = (n1 * (1.0 + sc2) + sh2).astype(BF16)
        h2_ref[...] = h2b
        z = jnp.zeros((tm, D), F32)
        for c0 in range(0, F, fc):
            u = _dot(h2b, w1_v[:, c0:c0 + fc])
            u_s[:, c0:c0 + fc] = u
            ru = jnp.maximum(u, 0.0)
            ab = (ru * ru).astype(BF16)
            a_ref[:, c0:c0 + fc] = ab
            z = z + _dot(ab, w2_v[c0:c0 + fc, :])
        r2 = lax.rsqrt(rowmean(z * z) + NORM_EPS)
        zh = z * r2
        n2 = zh * gpo
        y2 = y1 + gt2 * n2
        err = y2 - t_ref[...]
        loss = 0.5 * jnp.sum(rowmean(err * err))
        dy2 = err * (1.0 / D)
        red_ref[2:3, :] += colsum(dy2 * n2)
        dn2 = dy2 * gt2
        red_ref[6:7, :] += colsum(dn2 * zh)
        dzh = dn2 * gpo
        dz = r2 * (dzh - zh * rowmean(dzh * zh))
        dzb = dz.astype(BF16)
        dz_ref[...] = dzb
        dh2 = jnp.zeros((tm, D), F32)
        for c0 in range(0, F, fc):
            da = _dot_nt(dzb, w2_v[c0:c0 + fc, :])
            dub = (da * (2.0 * jnp.maximum(u_s[:, c0:c0 + fc], 0.0))).astype(BF16)
            du_ref[:, c0:c0 + fc] = dub
            dh2 = dh2 + _dot_nt(dub, w1_v[:, c0:c0 + fc])
        red_ref[3:4, :] += colsum(dh2 * n1)
        red_ref[4:5, :] += colsum(dh2)
        dn1 = dh2 * (1.0 + sc2)
        red_ref[5:6, :] += colsum(dn1 * xh)
        dxh = dn1 * gpre
        dy1 = dy2 + r1 * (dxh - xh * rowmean(dxh * xh))
        dy1_ref[...] = dy1
        red_ref[0:1, :] += colsum(dy1 * nm)
        dnm = dy1 * gt1
        red_ref[1:2, :] += colsum(dnm * mh)
        dmh = dnm * gpm
        dmix = (rm * (dmh - mh * rowmean(dmh * mh))).astype(BF16)
        dmix_ref[...] = dmix
        dlr_ref[...] = _dot_nt(dmix, wout_v[0:half, :])
        dln_ref[...] = _dot_nt(dmix, wout_v[half:, :])
        red_ref[7:8, :] += jnp.zeros((1, D), F32) + loss

    def tok(w):
        return pl.BlockSpec((None, tm, w), lambda b, t: (b, t, 0))

    def mod(k):
        return pl.BlockSpec((None, None, 1, D), lambda b, t, k=k: (b, k, 0, 0))

    def vec():
        return pl.BlockSpec((1, D), lambda b, t: (0, 0))

    return pl.pallas_call(
        body, name="dense_core", grid=(B, nt),
        out_shape=(jax.ShapeDtypeStruct((B, N, D), F32), jax.ShapeDtypeStruct((B, N, half), F32),
                   jax.ShapeDtypeStruct((B, N, half), F32), jax.ShapeDtypeStruct((B, N, D), BF16),
                   jax.ShapeDtypeStruct((B, N, D), BF16), jax.ShapeDtypeStruct((B, N, F), BF16),
                   jax.ShapeDtypeStruct((B, N, F), BF16), jax.ShapeDtypeStruct((B, N, D), BF16),
                   jax.ShapeDtypeStruct((B, SUBLANES, D), F32),
                   jax.ShapeDtypeStruct(w_out.shape, w_out.dtype), jax.ShapeDtypeStruct(w1.shape, w1.dtype),
                   jax.ShapeDtypeStruct(w2.shape, w2.dtype)),
        in_specs=[tok(half), tok(half), tok(D), tok(D), mod(2), mod(3), mod(4), mod(5), vec(), vec(), vec(),
                  _any(), _any(), _any()],
        out_specs=(tok(D), tok(half), tok(half), tok(D), tok(D), tok(F), tok(F), tok(D),
                   pl.BlockSpec((None, SUBLANES, D), lambda b, t: (b, 0, 0)), _any(), _any(), _any()),
        scratch_shapes=[pltpu.VMEM((mixw, D), BF16), pltpu.VMEM((D, F), BF16), pltpu.VMEM((F, D), BF16),
                        pltpu.VMEM((tm, F), F32), pltpu.SemaphoreType.DMA((3,)),
                        pltpu.SemaphoreType.DMA((3, 3)), pltpu.SemaphoreType.DMA((3, 3))],
        input_output_aliases={11: 9, 12: 10, 13: 11},
        compiler_params=_params("arbitrary", "arbitrary"),
    )(lat_ret, lat_na, x, tgt, modl, modl, modl, modl, g_post_mix, g_pre_mlp, g_post_mlp, w_out, w1, w2)[:9]


def _inproj_bwd(dret, dna, x, ctx, dy1, modl, g1, w_in_t):
    B, N, D = x.shape
    n_ctx = ctx.shape[1]
    T = n_ctx + N
    tm = _div_tile(n_ctx, 256, 16)
    nct, ctx_spec, lat_spec = _token_tiles(n_ctx, tm)
    nt = T // tm
    nseg_r = dret.shape[1]
    nseg_n = dna.shape[1]
    nw = w_in_t.shape[0]

    def body(*refs):
        seg_refs = refs[:nseg_r + nseg_n]
        c_ref, x_ref, dy1_ref, sc_ref, g_ref, w_ref, dx_ref, red_ref = refs[nseg_r + nseg_n:]
        t = pl.program_id(1)
        dh = jnp.zeros((tm, D), F32)
        for s, ref in enumerate(seg_refs):
            dh = dh + _dot(ref[...], w_ref[s * SEG:(s + 1) * SEG, :])
        x = jnp.where(t < nct, c_ref[...], x_ref[...])
        g = g_ref[...]
        r = lax.rsqrt(jnp.mean(x * x, axis=-1, keepdims=True) + NORM_EPS)
        xh = x * r
        red_ref[0:1, :] = jnp.sum(dh, axis=0, keepdims=True)
        red_ref[1:2, :] = jnp.sum(dh * (xh * g), axis=0, keepdims=True)
        dn = dh * (1.0 + sc_ref[...])
        red_ref[2:3, :] = jnp.sum(dn * xh, axis=0, keepdims=True)
        red_ref[3:, :] = jnp.zeros((SUBLANES - 3, D), F32)
        dxh = dn * g
        dx = r * (dxh - xh * jnp.mean(dxh * xh, axis=-1, keepdims=True))
        dx_ref[...] = dx + jnp.where(t >= nct, dy1_ref[...], 0.0)

    def mrow(b, t):
        return jnp.where(t < nct, B, b)

    def seg(s):
        return pl.BlockSpec((None, None, tm, SEG), lambda b, t, s=s: (b, s, t, 0))

    return _grid_call(
        body, name="inproj_bwd", grid=(B, nt),
        out_shape=(jax.ShapeDtypeStruct((B, N, D), F32), jax.ShapeDtypeStruct((B, nt, SUBLANES, D), F32)),
        in_specs=[seg(s) for s in range(nseg_r)] + [seg(s) for s in range(nseg_n)]
                 + [ctx_spec(D), lat_spec(D), lat_spec(D),
                    pl.BlockSpec((None, None, 1, D), lambda b, t: (mrow(b, t), 1, 0, 0)),
                    pl.BlockSpec((1, D), lambda b, t: (0, 0)),
                    pl.BlockSpec((nw, D), lambda b, t: (0, 0))],
        out_specs=(lat_spec(D), pl.BlockSpec((None, None, SUBLANES, D), lambda b, t: (b, t, 0, 0))),
        scratch_shapes=[], args=(*([dret] * nseg_r), *([dna] * nseg_n), ctx, x, dy1, modl, g1, w_in_t))


def _tn_matmul(lhs, rhs, name, rows_before=0, rows_after=0, into=None):
    B, S, T, W = lhs.shape
    nn = rhs.shape[-1]
    tk = _div_tile(T, 2304, LANES)
    bm = _div_tile(W, 1024, LANES)
    bn = _div_tile(nn, 1024, LANES)
    nkt = T // tk
    nk = B * nkt

    def body(l_ref, r_ref, *rest):
        o_ref, acc = rest[-2:]
        k = pl.program_id(3)

        @pl.when(k == 0)
        def _():
            acc[...] = jnp.zeros(acc.shape, F32)

        acc[...] += _dot_tn(l_ref[...].astype(BF16), r_ref[...].astype(BF16))

        @pl.when(k == nk - 1)
        def _():
            o_ref[...] = acc[...].astype(BF16)

    nwb = W // bm
    first = rows_before // bm
    return pl.pallas_call(
        functools.partial(body), name=name, grid=(S, nwb, nn // bn, nk),
        out_shape=jax.ShapeDtypeStruct((rows_before + S * W + rows_after, nn), BF16),
        in_specs=[pl.BlockSpec((None, None, tk, bm), lambda s, i, j, k: (k // nkt, s, k % nkt, i)),
                  pl.BlockSpec((None, tk, bn), lambda s, i, j, k: (k // nkt, k % nkt, j))]
                 + ([] if into is None else [_any()]),
        out_specs=pl.BlockSpec((bm, bn), lambda s, i, j, k: (first + s * nwb + i, j)),
        scratch_shapes=[pltpu.VMEM((bm, bn), F32)],
        input_output_aliases={} if into is None else {2: 0},
        compiler_params=_params("parallel", "parallel", "parallel", "arbitrary"),
    )(lhs, rhs, *([] if into is None else [into]))


class _SplitScatter:
    def __init__(self, gs, block_ofs, land_shapes, name, kind="scatter", masks=ALL_PEERS):
        self.n = n = len(gs)
        self.block_ofs, self.kind, self.masks = block_ofs, kind, masks
        if kind == "scatter":
            land_shapes = [(N_DEV,) + tuple(bs) for bs in land_shapes]
        hbm = pl.BlockSpec(memory_space=pltpu.HBM)
        sem = pl.BlockSpec(memory_space=pltpu.SEMAPHORE)

        def body(*refs):
            g_refs, land_refs = refs[:n], refs[n:2 * n]
            send_sems, recv_sems, own_sems = refs[2 * n:2 * n + 3]
            token = refs[-1]
            for own, pushes in self._copies(g_refs, land_refs, send_sems, recv_sems, own_sems, landing="sender"):
                own.start()
                for cp in pushes:
                    cp.start()
            token[...] = jnp.zeros_like(token)

        outs = pl.pallas_call(
            body, name=name,
            out_shape=(pltpu.SemaphoreType.DMA((n * (N_DEV - 1),)), pltpu.SemaphoreType.DMA((n * (N_DEV - 1),)),
                       pltpu.SemaphoreType.DMA((n,)))
                      + tuple(pltpu.HBM(g.shape, g.dtype) for g in gs)
                      + tuple(pltpu.HBM(s, g.dtype) for s, g in zip(land_shapes, gs))
                      + (jax.ShapeDtypeStruct((SUBLANES, LANES), F32),),
            in_specs=(hbm,) * (2 * n), out_specs=(sem,) * 3 + (hbm,) * (2 * n) + (_vmem(),),
            input_output_aliases={k: 3 + k for k in range(2 * n)},
            compiler_params=pltpu.CompilerParams(has_side_effects=pltpu.SideEffectType.DATAFLOW_SIDE_EFFECTING),
        )(*[pltpu.with_memory_space_constraint(g, pltpu.HBM) for g in gs],
          *[pltpu.with_memory_space_constraint(lax.empty(s, g.dtype), pltpu.HBM) for s, g in zip(land_shapes, gs)])
        self.sems, self.thru, self.token = outs[:3], outs[3:3 + 2 * n], outs[-1]

    def _copies(self, g_refs, land_refs, send_sems, recv_sems, own_sems, landing):
        me, peers = _me_and_peers()
        out = []
        for k in range(self.n):
            if self.kind == "scatter":
                src, dst = self.block_ofs[k](g_refs[k]), _slot(land_refs[k])
            else:
                src, dst = (lambda p, k=k: g_refs[k]), self.block_ofs[k](land_refs[k])
            own = pltpu.make_async_copy(src(me), dst(me), own_sems.at[k])
            pushes = []
            for m in self.masks:
                dev, pid = peers[m - 1]
                i = k * (N_DEV - 1) + m - 1
                pushes.append(_remote(src(pid), dst(me if landing == "sender" else pid),
                                      send_sems.at[i], recv_sems.at[i], dev))
            out.append((own, pushes))
        return out


def _scatter_wait(scatters, after, name):
    hbm = pl.BlockSpec(memory_space=pltpu.HBM)
    sem = pl.BlockSpec(memory_space=pltpu.SEMAPHORE)
    n_arr = [2 * sc.n for sc in scatters]
    total = sum(n_arr)

    def body(*refs):
        arrs, sems = refs[:total], refs[total:total + 3 * len(scatters)]
        a0 = 0
        for j, sc in enumerate(scatters):
            g_refs, land_refs = arrs[a0:a0 + sc.n], arrs[a0 + sc.n:a0 + 2 * sc.n]
            a0 += 2 * sc.n
            send_sems, recv_sems, own_sems = sems[3 * j:3 * j + 3]
            for (own, sent), (_, got) in zip(sc._copies(g_refs, land_refs, send_sems, recv_sems, own_sems, "sender"),
                                             sc._copies(g_refs, land_refs, send_sems, recv_sems, own_sems, "receiver")):
                own.wait()
                for cp in sent:
                    cp.wait_send()
                for cp in got:
                    cp.wait_recv()

    operands = [a for sc in scatters for a in sc.thru]
    outs = pl.pallas_call(
        body, name=name,
        out_shape=tuple(pltpu.HBM(a.shape, a.dtype) for a in operands),
        in_specs=(hbm,) * total + (sem,) * (3 * len(scatters)) + (pl.BlockSpec(memory_space=pl.ANY),),
        out_specs=(hbm,) * total, input_output_aliases={k: k for k in range(total)},
        compiler_params=pltpu.CompilerParams(has_side_effects=pltpu.SideEffectType.DATAFLOW_SIDE_EFFECTING),
    )(*operands, *[s for sc in scatters for s in sc.sems], after)
    lands, a0 = [], 0
    for sc in scatters:
        lands.extend(outs[a0 + sc.n:a0 + 2 * sc.n])
        a0 += 2 * sc.n
    return lands


def _sum_slots(buf, name):
    _, rows, cols = buf.shape
    tr = _div_tile(rows, 256, 2 * SUBLANES)

    def body(b_ref, o_ref):
        acc = b_ref[0].astype(F32)
        for k in range(1, N_DEV):
            acc = acc + b_ref[k].astype(F32)
        o_ref[...] = acc

    return pl.pallas_call(
        functools.partial(body), name=name, grid=(rows // tr,),
        out_shape=jax.ShapeDtypeStruct((rows, cols), F32),
        in_specs=[pl.BlockSpec((N_DEV, tr, cols), lambda i: (0, i, 0))],
        out_specs=pl.BlockSpec((tr, cols), lambda i: (i, 0)),
        compiler_params=_params("parallel"),
    )(buf)


def _small_ar(vbuf, mbuf, silu_all, w_ada, c_ctx):
    rv = vbuf.shape[1]
    D = silu_all.shape[1]
    ncol = w_ada.shape[1]
    nm = mbuf.shape[2]
    srows = silu_all.shape[0]

    def body(vbuf, mbuf, s_ref, w_ref, cc_ref, tot_ref, gb_ref, gw_ref, gc_ref, tbuf, dmx, cmrow, send3, recv3):
        me, _ = _me_and_peers()
        tot = vbuf[0]
        msum = mbuf[0]
        for k in range(1, N_DEV):
            tot = tot + vbuf[k]
            msum = msum + mbuf[k]
        tot_ref[...] = tot
        gb_ref[...] = jnp.sum(msum, axis=0, keepdims=True)
        loc = pl.ds(pl.multiple_of(me * ncol, ncol), ncol)
        for k in range(N_DEV):
            dmx[k * SUBLANES:(k + 1) * SUBLANES, :] = mbuf[k, :, loc]
        cmrow[...] = msum
        cm_loc = cmrow[2:3, loc]
        dmx[N_DEV * SUBLANES:, :] = jnp.concatenate([cm_loc, jnp.zeros((SUBLANES - 1, ncol), F32)], axis=0)
        gw_ref[...] = _dot_tn(s_ref[...], dmx[...])
        tbuf[me] = _dot_nt(dmx[N_DEV * SUBLANES:, :], w_ref[...])
        _exchange(lambda p: tbuf.at[me], lambda p: tbuf.at[p], send3, recv3)
        tsum = tbuf[0]
        for k in range(1, N_DEV):
            tsum = tsum + tbuf[k]
        cc = cc_ref[...]
        sg = _sigmoid(cc)
        gc_ref[...] = tsum[0:1, :] * (sg * (1.0 + cc * (1.0 - sg)))

    return pl.pallas_call(
        body, name="small_ar",
        out_shape=(jax.ShapeDtypeStruct((rv, LANES), F32), jax.ShapeDtypeStruct((1, nm), F32),
                   jax.ShapeDtypeStruct((D, ncol), F32), jax.ShapeDtypeStruct((1, D), F32)),
        in_specs=[_vmem()] * 5, out_specs=(_vmem(),) * 4,
        scratch_shapes=[pltpu.VMEM((N_DEV, SUBLANES, D), F32), pltpu.VMEM((srows, ncol), F32),
                        pltpu.VMEM((SUBLANES, nm), F32)] + [pltpu.SemaphoreType.DMA((N_DEV - 1,))] * 2,
        compiler_params=pltpu.CompilerParams(vmem_limit_bytes=VMEM_LIMIT),
    )(vbuf, mbuf, silu_all, w_ada, c_ctx.reshape(1, D))


def _adam_update(w, g, m, v):
    mn = ADAM_B1 * m + (1.0 - ADAM_B1) * g
    vn = ADAM_B2 * v + (1.0 - ADAM_B2) * (g * g)
    m_hat = mn / (1.0 - ADAM_B1 ** ADAM_STEP)
    v_hat = vn / (1.0 - ADAM_B2 ** ADAM_STEP)
    return -ADAM_LR * (m_hat / (jnp.sqrt(v_hat) + ADAM_EPS) + ADAM_WD * w), mn, vn


def _adamw(w, g, m, v, name):
    rows, cols = w.shape
    tr = _div_tile(rows, 256, SUBLANES) if rows * cols > 65536 else rows

    def body(w_ref, g_ref, m_ref, v_ref, d_ref, nm_ref, nv_ref):
        d_ref[...], nm_ref[...], nv_ref[...] = _adam_update(w_ref[...], g_ref[...], m_ref[...], v_ref[...])

    spec = pl.BlockSpec((tr, cols), lambda i: (i, 0))
    return pl.pallas_call(
        functools.partial(body), name=name, grid=(rows // tr,),
        out_shape=(jax.ShapeDtypeStruct((rows, cols), F32),) * 3,
        in_specs=[spec] * 4, out_specs=(spec,) * 3,
        compiler_params=_params("parallel"),
    )(w, g, m, v)


def _adamw_small(items, name):
    n = len(items)

    def body(*refs):
        ins, outs = refs[:4 * n], refs[4 * n:]
        for i in range(n):
            w_ref, g_ref, m_ref, v_ref = ins[4 * i:4 * i + 4]
            outs[3 * i][...], outs[3 * i + 1][...], outs[3 * i + 2][...] = _adam_update(
                w_ref[...], g_ref[...], m_ref[...], v_ref[...])

    outs = pl.pallas_call(
        body, name=name,
        out_shape=tuple(jax.ShapeDtypeStruct(it[0].shape, F32) for it in items for _ in range(3)),
        in_specs=[_vmem()] * (4 * n), out_specs=(_vmem(),) * (3 * n),
        compiler_params=pltpu.CompilerParams(vmem_limit_bytes=VMEM_LIMIT),
    )(*[a for it in items for a in it])
    return [tuple(outs[3 * i:3 * i + 3]) for i in range(n)]


def _sum_adamw(buf, w, m, v, name):
    _, rows, cols = buf.shape
    tr = _div_tile(rows, 256, 2 * SUBLANES)

    def body(b_ref, w_ref, m_ref, v_ref, g_ref, d_ref, nm_ref, nv_ref):
        g = b_ref[0].astype(F32)
        for k in range(1, N_DEV):
            g = g + b_ref[k].astype(F32)
        g_ref[...] = g
        d_ref[...], nm_ref[...], nv_ref[...] = _adam_update(w_ref[...], g, m_ref[...], v_ref[...])

    spec = pl.BlockSpec((tr, cols), lambda i: (i, 0))
    return pl.pallas_call(
        functools.partial(body), name=name, grid=(rows // tr,),
        out_shape=(jax.ShapeDtypeStruct((rows, cols), F32),) * 4,
        in_specs=[pl.BlockSpec((N_DEV, tr, cols), lambda i: (0, i, 0))] + [spec] * 3, out_specs=(spec,) * 4,
        compiler_params=_params("parallel"),
    )(buf, w, m, v)


def _rope_tables(n_ctx, n):
    n_freq = RET_DIM // 4
    inv = np.float32(ROPE_BASE) ** (-np.arange(n_freq, dtype=np.float32) / np.float32(n_freq))
    tok = np.arange(n)
    pos_r = (tok // GRID_W).astype(np.float32)
    pos_c = (tok % GRID_W).astype(np.float32)
    ang_r = (pos_r[:, None] * inv[None, :]).astype(np.float32)
    ang_c = (pos_c[:, None] * inv[None, :]).astype(np.float32)
    cos = np.concatenate([np.cos(ang_r), np.cos(ang_r), np.cos(ang_c), np.cos(ang_c)], axis=-1)
    sin = np.concatenate([-np.sin(ang_r), np.sin(ang_r), -np.sin(ang_c), np.sin(ang_c)], axis=-1)
    cos = np.concatenate([np.ones((n_ctx, RET_DIM), np.float32), cos], axis=0)
    sin = np.concatenate([np.zeros((n_ctx, RET_DIM), np.float32), sin], axis=0)
    return jnp.asarray(cos, F32), jnp.asarray(sin, F32)


def _na_tables():
    q = np.arange(GRID_W)[:, None]
    k = np.arange(GRID_W)[None, :]
    start = np.clip(q - NA_KW // 2, 0, GRID_W - NA_KW)
    valid = (k >= start) & (k < start + NA_KW)
    dc = np.clip(k - q + (NA_KW - 1), 0, 2 * NA_KW - 2)
    ncls = 2 * NA_KW - 1
    onehot = (dc[None] == np.arange(ncls)[:, None, None]) & valid[None]
    oh2 = np.zeros((GRID_W, LANES, LANES), np.float32)
    for c in range(ncls):
        oh2[:, :GRID_W, c] = onehot[c]
        oh2[:, GRID_W:, 32 + c] = onehot[c]
    return onehot.astype(np.float32), valid, oh2.reshape(GRID_W * LANES, LANES)


def _paired_bias(rpb, onehot, valid):
    t = jnp.einsum("hdc,cqk->hdqk", rpb, jnp.asarray(onehot), precision=lax.Precision.HIGHEST)
    t = jnp.where(jnp.asarray(valid)[None, None], t, NEG_INF)
    return jnp.concatenate([t[:, :-1], t[:, 1:]], axis=-1)


def kernel(x, c, ctx, c_ctx, w_ada, b_ada, g_pre_mix, g_post_mix, g_pre_mlp, g_post_mlp, w_in, ret_decay, ret_gn, na_rpb, w_out, w_mlp1, w_mlp2, loss_target, m_c_ctx, m_w_ada, m_b_ada, m_g_pre_mix, m_g_post_mix, m_g_pre_mlp, m_g_post_mlp, m_w_in, m_ret_decay, m_ret_gn, m_na_rpb, m_w_out, m_w_mlp1, m_w_mlp2, v_c_ctx, v_w_ada, v_b_ada, v_g_pre_mix, v_g_post_mix, v_g_pre_mlp, v_g_post_mlp, v_w_in, v_ret_decay, v_ret_gn, v_na_rpb, v_w_out, v_w_mlp1, v_w_mlp2):
    B, N, D = x.shape
    C = ctx.shape[1]
    T = C + N

    silu_all, mods_g, win_b, wout_l, w1_l, w2_l = _mod_gather(c, c_ctx, w_ada[0], b_ada, w_in[0].T, w_out[0],
                                                             w_mlp1[0], w_mlp2[0])
    mods_mine = mods_g.transpose(1, 0, 2).reshape(mods_g.shape[1], N_MOD * D)
    modl = jnp.concatenate([mods_mine[:B], mods_mine[SUBLANES:SUBLANES + 1]], axis=0)
    modl = modl.reshape(B + 1, N_MOD, 1, D)
    rin = w_in.shape[2]
    rout, c1, r2 = wout_l.shape[0], w1_l.shape[1], w2_l.shape[0]

    def rows_of(n):
        return lambda ref: _row_block(ref, n)

    def cols_of(n):
        return lambda ref: _col_block(ref, n)

    cos, sin = _rope_tables(C, N)
    onehot, valid, oh2 = _na_tables()
    bias2 = _paired_bias(na_rpb[0], onehot, valid)
    lg = jax.nn.log_sigmoid(ret_decay[0].astype(F32))

    ag = _SplitScatter([wout_l, w1_l, w2_l], [rows_of(rout), cols_of(c1), rows_of(r2)],
                       [(N_DEV * rout, D), (D, N_DEV * c1), (N_DEV * r2, D)], "ag_mlp_start",
                       kind="gather", masks=SIBLING + ICI_SAME_CORE)
    h_all, proj = _inproj_fwd(x, ctx, modl, g_pre_mix + ag.token[0, 0], win_b)
    o_ret, lat_ret = _ret_fwd(proj, cos, sin, lg, ret_gn, C)
    (lat_na,) = _na_fwd(proj, bias2, C)
    wout_part, w1_part, w2_part = _scatter_wait([ag], lat_na, "ag_mlp_wait")

    (dy1, dlat_ret, dlat_na, dmix, h2, act, du, dz, red_d) = _dense_core(
        lat_ret, lat_na, x, loss_target, modl, g_post_mix, g_pre_mlp, g_post_mlp, wout_part, w1_part, w2_part)

    gw_out_p = _tn_matmul(lat_ret[:, None], dmix, "gw_out_ret", rows_after=lat_na.shape[-1])
    gw_out_p = _tn_matmul(lat_na[:, None], dmix, "gw_out_na", rows_before=lat_ret.shape[-1], into=gw_out_p)
    gw1_p = _tn_matmul(h2[:, None], du, "gw_mlp1")
    gw2_p = _tn_matmul(act[:, None], dz, "gw_mlp2")
    rs_mlp = _SplitScatter([gw_out_p, gw1_p, gw2_p], [rows_of(rout), cols_of(c1), rows_of(r2)],
                           [(rout, D), (D, c1), (r2, D)], "rs_mlp_start")

    dret, dgn_p, dlg_p = _ret_bwd(proj, cos, sin, lg, ret_gn + rs_mlp.token[0, 0], o_ret, dlat_ret, C)
    dna, dbias2 = _na_bwd(proj, bias2, dlat_na, C)
    ret_cols, na_cols = dret.shape[1] * dret.shape[3], dna.shape[1] * dna.shape[3]
    gwin_t_p = _tn_matmul(dret, h_all, "gw_in_ret", rows_after=na_cols)
    gwin_t_p = _tn_matmul(dna, h_all, "gw_in_na", rows_before=ret_cols, into=gwin_t_p)
    rs_in = _SplitScatter([gwin_t_p], [rows_of(rin)], [(rin, D)], "rs_w_in_start")
    grad_x, red_i = _inproj_bwd(dret, dna, x, ctx, dy1, modl, g_pre_mix + rs_in.token[0, 0], win_b)

    rd = red_d
    nct = red_i.shape[1] * C // T
    ri_ctx = red_i[:, :nct].sum(axis=(0, 1))
    ri_lat = red_i[:, nct:].sum(axis=1)
    d_mods = jnp.concatenate([ri_lat[:, 0], ri_lat[:, 1], rd[:, 0], rd[:, 4], rd[:, 3], rd[:, 2]], axis=-1)
    d_cmods = jnp.concatenate([ri_ctx[0], ri_ctx[1], jnp.zeros(((N_MOD - 2) * D,), F32)])[None]
    dm_slot = jnp.concatenate([d_mods, d_cmods, jnp.zeros((SUBLANES - B - 1, N_MOD * D), F32)], axis=0)
    dg_pre_mix = ri_lat[:, 2].sum(axis=0) + ri_ctx[2]
    dg_post_mix = rd[:, 1].sum(axis=0)
    dg_pre_mlp = rd[:, 5].sum(axis=0)
    dg_post_mlp = rd[:, 6].sum(axis=0)
    loss_p = rd[:, 7, 0].sum()
    d_gn = dgn_p[:, 0].sum(axis=0)
    d_lg = dlg_p[:, :, :2, 0].sum(axis=0).T
    d_decay = d_lg * jax.nn.sigmoid(-ret_decay[0].astype(F32))
    rr = _rpb_reduce(dbias2, jnp.asarray(oh2, BF16)).reshape(NA_HEADS, 2 * NA_KH - 2, LANES)
    ncls = 2 * NA_KW - 1
    d_rpb = (jnp.pad(rr[:, :, :ncls], ((0, 0), (0, 1), (0, 0))) + jnp.pad(rr[:, :, 32:32 + ncls], ((0, 0), (1, 0), (0, 0))))
    d_rpb32 = jnp.pad(d_rpb, ((0, 0), (0, 0), (0, 32 - ncls)))
    pieces = [dg_pre_mix, dg_post_mix, dg_pre_mlp, dg_post_mlp, d_gn, d_rpb32.reshape(-1),
              jnp.pad(d_decay.reshape(-1), (0, LANES - d_decay.size)), jnp.full((LANES,), loss_p, F32)]
    vec = jnp.concatenate(pieces)
    pad = (-vec.shape[0]) % (SUBLANES * LANES)
    vec = jnp.pad(vec, (0, pad)).reshape(-1, LANES)
    def whole(ref):
        return lambda p: ref

    small = _SplitScatter([vec, dm_slot], [whole, whole], [vec.shape, dm_slot.shape], "small_start")
    land_out, land_1, land_2, land_in = _scatter_wait([rs_mlp, rs_in], small.token, "rs_wait")
    g_w_in = _sum_slots(land_in, "sum_w_in").T
    fused = {"w_out": _sum_adamw(land_out, w_out[0], m_w_out[0], v_w_out[0], "sum_adamw_w_out"),
             "w_mlp1": _sum_adamw(land_1, w_mlp1[0], m_w_mlp1[0], v_w_mlp1[0], "sum_adamw_w_mlp1"),
             "w_mlp2": _sum_adamw(land_2, w_mlp2[0], m_w_mlp2[0], v_w_mlp2[0], "sum_adamw_w_mlp2")}
    vbuf, mbuf = _scatter_wait([small], fused["w_mlp2"][0], "small_wait")
    tot, g_b_ada, g_w_ada, g_c_ctx = _small_ar(vbuf, mbuf, silu_all, w_ada[0], c_ctx)
    flat = tot.reshape(-1)
    o0 = 0
    g_pre_mix_g = flat[o0:o0 + D]; o0 += D
    g_post_mix_g = flat[o0:o0 + D]; o0 += D
    g_pre_mlp_g = flat[o0:o0 + D]; o0 += D
    g_post_mlp_g = flat[o0:o0 + D]; o0 += D
    g_gn = flat[o0:o0 + RET_WIDTH]; o0 += RET_WIDTH
    nrpb = NA_HEADS * (2 * NA_KH - 1) * 32
    g_rpb = flat[o0:o0 + nrpb].reshape(NA_HEADS, 2 * NA_KH - 1, 32)[:, :, :ncls]; o0 += nrpb
    g_decay = flat[o0:o0 + 2 * RET_HEADS].reshape(2, RET_HEADS); o0 += LANES
    loss = flat[o0]

    grads = {
        "c_ctx": g_c_ctx.reshape(c_ctx.shape), "w_ada": g_w_ada[None], "b_ada": g_b_ada.reshape(b_ada.shape),
        "g_pre_mix": g_pre_mix_g[None], "g_post_mix": g_post_mix_g[None], "g_pre_mlp": g_pre_mlp_g[None],
        "g_post_mlp": g_post_mlp_g[None], "w_in": g_w_in[None], "ret_decay": g_decay[None], "ret_gn": g_gn[None],
        "na_rpb": g_rpb[None], "w_out": fused["w_out"][0][None], "w_mlp1": fused["w_mlp1"][0][None],
        "w_mlp2": fused["w_mlp2"][0][None],
    }
    weights = dict(c_ctx=c_ctx, w_ada=w_ada, b_ada=b_ada, g_pre_mix=g_pre_mix, g_post_mix=g_post_mix,
                   g_pre_mlp=g_pre_mlp, g_post_mlp=g_post_mlp, w_in=w_in, ret_decay=ret_decay, ret_gn=ret_gn,
                   na_rpb=na_rpb, w_out=w_out, w_mlp1=w_mlp1, w_mlp2=w_mlp2)
    m_in = dict(c_ctx=m_c_ctx, w_ada=m_w_ada, b_ada=m_b_ada, g_pre_mix=m_g_pre_mix, g_post_mix=m_g_post_mix,
                g_pre_mlp=m_g_pre_mlp, g_post_mlp=m_g_post_mlp, w_in=m_w_in, ret_decay=m_ret_decay,
                ret_gn=m_ret_gn, na_rpb=m_na_rpb, w_out=m_w_out, w_mlp1=m_w_mlp1, w_mlp2=m_w_mlp2)
    v_in = dict(c_ctx=v_c_ctx, w_ada=v_w_ada, b_ada=v_b_ada, g_pre_mix=v_g_pre_mix, g_post_mix=v_g_post_mix,
                g_pre_mlp=v_g_pre_mlp, g_post_mlp=v_g_post_mlp, w_in=v_w_in, ret_decay=v_ret_decay,
                ret_gn=v_ret_gn, na_rpb=v_na_rpb, w_out=v_w_out, w_mlp1=v_w_mlp1, w_mlp2=v_w_mlp2)
    names = list(weights)
    deltas, new_m, new_v = {}, {}, {}
    def as_2d(n):
        shp = weights[n].shape
        two_d = (-1, shp[-1]) if len(shp) > 1 else (1, shp[0])
        return [a.reshape(two_d) for a in (weights[n], grads[n], m_in[n], v_in[n])]

    small = [n for n in names if n not in fused and weights[n].size <= 65536]
    updated = dict(zip(small, _adamw_small([as_2d(n) for n in small], "adamw_small")))
    for n in names:
        if n in fused:
            updated[n] = fused[n][1:]
        elif n not in updated:
            updated[n] = _adamw(*as_2d(n), "adamw_" + n)
        deltas[n], new_m[n], new_v[n] = (a.reshape(weights[n].shape) for a in updated[n])
    return (loss, grad_x, *[grads[n] for n in names], *[deltas[n] for n in names],
            *[new_m[n] for n in names], *[new_v[n] for n in names])
```

```python
import functools
import math

import numpy as np
import jax
import jax.numpy as jnp
from jax import lax
from jax.experimental import pallas as pl
from jax.experimental.pallas import tpu as pltpu

F32 = jnp.float32
BF16 = jnp.bfloat16
MESH = pl.DeviceIdType.MESH

N_DEV = 8
LANES = 128
SUBLANES = 8
VMEM_LIMIT = 60 * 1024 * 1024

GRID_W = 64
RET_HEADS = 4
RET_DIM = 128
RET_WIDTH = RET_HEADS * RET_DIM
NA_HEADS = 8
NA_DIM = 64
NA_WIDTH = NA_HEADS * NA_DIM
NA_PAIRS = NA_HEADS // 2
NA_KH = 8
NA_KW = 16
NA_GROUP = 8
SEG = 512
ROPE_BASE = 10000.0
NORM_EPS = 1e-6
NEG_INF = -1e30
N_MOD = 6

ADAM_LR = 0.001
ADAM_B1 = 0.9
ADAM_B2 = 0.999
ADAM_EPS = 1e-08
ADAM_WD = 0.01
ADAM_STEP = 10


def _dot(a, b):
    return lax.dot_general(a, b, (((1,), (0,)), ((), ())), preferred_element_type=F32)


def _dot_nt(a, b):
    return lax.dot_general(a, b, (((1,), (1,)), ((), ())), preferred_element_type=F32)


def _dot_tn(a, b):
    return lax.dot_general(a, b, (((0,), (0,)), ((), ())), preferred_element_type=F32)


def _sigmoid(x):
    return 1.0 / (1.0 + jnp.exp(-x))


def _div_tile(n, cap, mult):
    if n <= cap:
        return n
    for t in range(cap - cap % mult, 0, -mult):
        if n % t == 0:
            return t
    raise ValueError(f"no tile for {n}")


def _params(*sem):
    return pltpu.CompilerParams(dimension_semantics=tuple(sem) if sem else None,
                                vmem_limit_bytes=VMEM_LIMIT)


def _vmem():
    return pl.BlockSpec(memory_space=pltpu.VMEM)


def _any():
    return pl.BlockSpec(memory_space=pl.ANY)


def _me_and_peers():
    x, y, c = lax.axis_index("x"), lax.axis_index("y"), lax.axis_index("c")
    me = 4 * x + 2 * y + c
    peers = []
    for m in range(1, N_DEV):
        px = 1 - x if (m >> 2) & 1 else x
        py = 1 - y if (m >> 1) & 1 else y
        pc = 1 - c if m & 1 else c
        peers.append(((px, py, pc), 4 * px + 2 * py + pc))
    return me, peers


def _exchange(src_for, dst_from, send_sems, recv_sems):
    me, peers = _me_and_peers()
    sent = []
    for i, (dev, pid) in enumerate(peers):
        cp = pltpu.make_async_remote_copy(src_ref=src_for(pid), dst_ref=dst_from(me),
                                          send_sem=send_sems.at[i], recv_sem=recv_sems.at[i],
                                          device_id=dev, device_id_type=MESH)
        cp.start()
        sent.append(cp)
    for i, (dev, pid) in enumerate(peers):
        pltpu.make_async_remote_copy(src_ref=src_for(pid), dst_ref=dst_from(pid),
                                     send_sem=send_sems.at[i], recv_sem=recv_sems.at[i],
                                     device_id=dev, device_id_type=MESH).wait_recv()
    for cp in sent:
        cp.wait_send()


SIBLING = (1,)
ICI_SAME_CORE = (2, 4, 6)
ALL_PEERS = tuple(range(1, N_DEV))


def _remote(src, dst, send_sem, recv_sem, dev):
    return pltpu.make_async_remote_copy(src_ref=src, dst_ref=dst, send_sem=send_sem, recv_sem=recv_sem,
                                        device_id=dev, device_id_type=MESH)


def _push_start(items, masks, send_sems, recv_sems):
    me, peers = _me_and_peers()
    for k, (src_for, dst_from) in enumerate(items):
        for m in masks:
            dev, pid = peers[m - 1]
            _remote(src_for(pid), dst_from(me), send_sems.at[k, m - 1], recv_sems.at[k, m - 1], dev).start()


def _push_wait_recv(items, masks, send_sems, recv_sems):
    me, peers = _me_and_peers()
    for k, (src_for, dst_from) in enumerate(items):
        for m in masks:
            dev, pid = peers[m - 1]
            _remote(src_for(pid), dst_from(pid), send_sems.at[k, m - 1], recv_sems.at[k, m - 1], dev).wait_recv()


def _push_wait_send(items, masks, send_sems, recv_sems):
    me, peers = _me_and_peers()
    for k, (src_for, dst_from) in enumerate(items):
        for m in masks:
            dev, pid = peers[m - 1]
            _remote(src_for(pid), dst_from(me), send_sems.at[k, m - 1], recv_sems.at[k, m - 1], dev).wait_send()


def _forward_start(items, send_sems, recv_sems):
    me, peers = _me_and_peers()
    sib = peers[0][0]
    for k, (blk_in, blk_out) in enumerate(items):
        for j, m in enumerate(ICI_SAME_CORE):
            pid = peers[m - 1][1]
            _remote(blk_in(pid), blk_out(pid), send_sems.at[k, j], recv_sems.at[k, j], sib).start()


def _forward_wait(items, send_sems, recv_sems):
    me, peers = _me_and_peers()
    sib = peers[0][0]
    for k, (blk_in, blk_out) in enumerate(items):
        for j, m in enumerate(ICI_SAME_CORE):
            got = peers[(m | 1) - 1][1]
            _remote(blk_in(got), blk_out(got), send_sems.at[k, j], recv_sems.at[k, j], sib).wait_recv()
    for k, (blk_in, blk_out) in enumerate(items):
        for j, m in enumerate(ICI_SAME_CORE):
            pid = peers[m - 1][1]
            _remote(blk_in(pid), blk_out(pid), send_sems.at[k, j], recv_sems.at[k, j], sib).wait_send()


def _mod_gather(c, c_ctx, w_ada, b_ada, w_in_t, w_out, w1, w2):
    B, D = c.shape
    ncol = w_ada.shape[1]
    rows = SUBLANES * N_DEV + SUBLANES

    def body(c_ref, cc_ref, w_ref, b_ref, win_ref, wout_ref, w1_ref, w2_ref,
             s_ref, m_ref, gin_ref, wout_b, w1_b, w2_b,
             win_b, msend, send1, recv1, send2, recv2, wsend, wrecv, fsend, frecv, lsem):
        me, _ = _me_and_peers()
        win_b[...] = win_ref[...].astype(BF16)
        block = _row_block(gin_ref, w_in_t.shape[0])
        gather = [(lambda p: win_b, block)]
        own = pltpu.make_async_copy(win_b, block(me), lsem.at[0])
        cv = c_ref[...]
        slot = jnp.concatenate([cv * _sigmoid(cv), jnp.zeros((SUBLANES - B, D), F32)], axis=0)
        my_rows = pl.ds(pl.multiple_of(me * SUBLANES, SUBLANES), SUBLANES)
        s_ref[my_rows, :] = slot
        ccv = cc_ref[...]
        s_ref[SUBLANES * N_DEV:, :] = jnp.concatenate(
            [ccv * _sigmoid(ccv), jnp.zeros((SUBLANES - 1, D), F32)], axis=0)

        def rows_of(p):
            return s_ref.at[pl.ds(pl.multiple_of(p * SUBLANES, SUBLANES), SUBLANES), :]

        _exchange(lambda p: rows_of(me), rows_of, send1, recv1)
        own.start()
        _push_start(gather, SIBLING + ICI_SAME_CORE, wsend, wrecv)
        wout_b[...] = wout_ref[...].astype(BF16)
        w1_b[...] = w1_ref[...].astype(BF16)
        w2_b[...] = w2_ref[...].astype(BF16)
        b_loc = b_ref[:, pl.ds(pl.multiple_of(me * ncol, ncol), ncol)]
        mods = _dot(s_ref[...], w_ref[...]) + b_loc
        for p in range(N_DEV):
            msend[p] = jnp.concatenate([mods[p * SUBLANES:(p + 1) * SUBLANES], mods[N_DEV * SUBLANES:]], axis=0)
        m_ref[me] = msend[me]
        columns = [(lambda p: msend.at[p], lambda p: m_ref.at[p])]
        _push_start(columns, ALL_PEERS, send2, recv2)
        _push_wait_recv(gather, ICI_SAME_CORE, wsend, wrecv)
        relay = [(block, block)]
        _forward_start(relay, fsend, frecv)
        _push_wait_recv(columns, ALL_PEERS, send2, recv2)
        _push_wait_recv(gather, SIBLING, wsend, wrecv)
        _forward_wait(relay, fsend, frecv)
        _push_wait_send(columns, ALL_PEERS, send2, recv2)
        _push_wait_send(gather, SIBLING + ICI_SAME_CORE, wsend, wrecv)
        own.wait()

    return pl.pallas_call(
        body, name="mod_gather",
        out_shape=(jax.ShapeDtypeStruct((rows, D), F32), jax.ShapeDtypeStruct((N_DEV, 2 * SUBLANES, ncol), F32),
                   jax.ShapeDtypeStruct((N_DEV * w_in_t.shape[0], D), BF16),
                   jax.ShapeDtypeStruct(w_out.shape, BF16), jax.ShapeDtypeStruct(w1.shape, BF16),
                   jax.ShapeDtypeStruct(w2.shape, BF16)),
        in_specs=[_vmem()] * 8, out_specs=(_vmem(), _vmem(), _any(), _vmem(), _vmem(), _vmem()),
        scratch_shapes=[pltpu.VMEM(w_in_t.shape, BF16), pltpu.VMEM((N_DEV, 2 * SUBLANES, ncol), F32)]
                       + [pltpu.SemaphoreType.DMA((N_DEV - 1,))] * 2
                       + [pltpu.SemaphoreType.DMA((1, N_DEV - 1))] * 4 + [pltpu.SemaphoreType.DMA((1, 3))] * 2
                       + [pltpu.SemaphoreType.DMA((1,))],
        compiler_params=pltpu.CompilerParams(vmem_limit_bytes=VMEM_LIMIT),
    )(c, c_ctx.reshape(1, D), w_ada, b_ada, w_in_t, w_out, w1, w2)


def _row_block(ref, rows):
    return lambda p: ref.at[pl.ds(pl.multiple_of(p * rows, 2 * SUBLANES), rows), :]


def _col_block(ref, cols):
    return lambda p: ref.at[:, pl.ds(pl.multiple_of(p * cols, LANES), cols)]


def _slot(ref):
    return lambda p: ref.at[p]


def _grid_call(body, *, name, grid, out_shape, in_specs, out_specs, scratch_shapes, args):
    return pl.pallas_call(
        body, name=name, grid=grid, out_shape=tuple(out_shape), in_specs=list(in_specs), out_specs=tuple(out_specs),
        scratch_shapes=list(scratch_shapes), compiler_params=_params(*(("arbitrary",) * len(grid))),
    )(*args)


def _token_tiles(n_ctx, tm):
    nct = n_ctx // tm

    def ctx_spec(D):
        return pl.BlockSpec((None, tm, D), lambda b, t: (b, jnp.minimum(t, nct - 1), 0))

    def lat_spec(D):
        return pl.BlockSpec((None, tm, D), lambda b, t: (b, jnp.maximum(t - nct, 0), 0))

    return nct, ctx_spec, lat_spec


def _inproj_fwd(x, ctx, modl, g1, w_in_t):
    B, N, D = x.shape
    n_ctx = ctx.shape[1]
    T = n_ctx + N
    nw = w_in_t.shape[0]
    tm = _div_tile(n_ctx, 256, 16)
    nct, ctx_spec, lat_spec = _token_tiles(n_ctx, tm)

    def body(c_ref, x_ref, sh_ref, sc_ref, g_ref, w_ref, h_ref, p_ref):
        x = jnp.where(pl.program_id(1) < nct, c_ref[...], x_ref[...])
        r = lax.rsqrt(jnp.mean(x * x, axis=-1, keepdims=True) + NORM_EPS)
        h = ((x * r) * g_ref[...]) * (1.0 + sc_ref[...]) + sh_ref[...]
        hb = h.astype(BF16)
        h_ref[...] = hb
        p_ref[...] = _dot_nt(hb, w_ref[...]).astype(BF16)

    def mrow(b, t):
        return jnp.where(t < nct, B, b)

    return _grid_call(
        body, name="inproj_fwd", grid=(B, T // tm),
        out_shape=(jax.ShapeDtypeStruct((B, T, D), BF16), jax.ShapeDtypeStruct((B, T, nw), BF16)),
        in_specs=[ctx_spec(D), lat_spec(D),
                  pl.BlockSpec((None, None, 1, D), lambda b, t: (mrow(b, t), 0, 0, 0)),
                  pl.BlockSpec((None, None, 1, D), lambda b, t: (mrow(b, t), 1, 0, 0)),
                  pl.BlockSpec((1, D), lambda b, t: (0, 0)),
                  pl.BlockSpec((nw, D), lambda b, t: (0, 0))],
        out_specs=(pl.BlockSpec((None, tm, D), lambda b, t: (b, t, 0)),
                   pl.BlockSpec((None, tm, nw), lambda b, t: (b, t, 0))),
        scratch_shapes=[], args=(ctx, x, modl, modl, g1, w_in_t))


def _swap32(x):
    lane = lax.broadcasted_iota(jnp.int32, x.shape, 1)
    return jnp.where((lane % 64) < 32, pltpu.roll(x, 96, 1), pltpu.roll(x, 32, 1))


def _rope(x, cos, sin):
    return x * cos + _swap32(x) * sin


def _unrope(dy, cos, sin):
    return dy * cos + _swap32(dy * sin)


def _ret_weights(lgf, lgb, dist):
    return jnp.exp(jnp.where(dist >= 0.0, lgf * dist, -lgb * dist))


class _RetDecay:
    def __init__(self, lgf, lgb, rows):
        r = lax.broadcasted_iota(jnp.int32, (rows, RET_DIM), 0).astype(F32)
        self.head = r + 1.0
        self.tail = (rows - 1.0) - r
        self.q_f = jnp.exp(lgf * self.head)
        self.k_f = jnp.exp(lgf * self.tail)
        self.q_b = jnp.exp(lgb * self.tail)
        self.k_b = jnp.exp(lgb * self.head)


def _ret_states(kf32, vs, lgf, lgb, C, c, nt, hf, hb, hfa=None, hba=None):
    dec = _RetDecay(lgf, lgb, c)
    dec_c = _RetDecay(lgf, lgb, C)
    step_f = jnp.exp(jnp.zeros((RET_DIM, RET_DIM), F32) + lgf * c)
    step_b = jnp.exp(jnp.zeros((RET_DIM, RET_DIM), F32) + lgb * c)

    def upd(rows, kdec):
        return _dot_tn((kf32[rows, :] * kdec).astype(BF16), vs[rows, :])

    def lat(t):
        return slice(C + t * c, C + (t + 1) * c)

    state = upd(slice(0, C), dec_c.k_f)
    aged = jnp.zeros_like(state)
    for t in range(nt):
        hf[t] = state.astype(BF16)
        if hfa is not None:
            hfa[t] = aged
        if t < nt - 1:
            aged = step_f * (aged + c * state)
            state = step_f * state + upd(lat(t), dec.k_f)
    state = upd(slice(0, C), dec_c.k_b)
    aged = jnp.zeros_like(state)
    for t in range(nt - 1, -1, -1):
        hb[t] = state.astype(BF16)
        if hba is not None:
            hba[t] = aged
        if t > 0:
            aged = step_b * (aged + c * state)
            state = step_b * state + upd(lat(t), dec.k_b)
    return dec, dec_c, step_f, step_b


def _ret_fwd(proj, cos, sin, lg, gn, n_ctx):
    B, T, _ = proj.shape
    C = n_ctx
    N = T - C
    c = _div_tile(N, 256, 16)
    nt = N // c
    scale = RET_DIM ** -0.5

    def body(lg_ref, q_ref, k_ref, v_ref, g_ref, cos_ref, sin_ref, gn_ref, o_ref, lat_ref, qs, ks, vs, kf32, hf, hb):
        h = pl.program_id(1)
        lgf = lg_ref[0, h]
        lgb = lg_ref[1, h]
        for rows in [slice(0, C)] + [slice(C + t * c, C + (t + 1) * c) for t in range(nt)]:
            cosb = cos_ref[rows, :]
            sinb = sin_ref[rows, :]
            qs[rows, :] = (_rope(q_ref[rows, :].astype(F32), cosb, sinb) * scale).astype(BF16)
            kr = _rope(k_ref[rows, :].astype(F32), cosb, sinb)
            kf32[rows, :] = kr
            ks[rows, :] = kr.astype(BF16)
            vs[rows, :] = v_ref[rows, :].astype(BF16)
        gnv = gn_ref[...]
        dec, _, _, _ = _ret_states(kf32, vs, lgf, lgb, C, c, nt, hf, hb)
        rc = (lax.broadcasted_iota(jnp.int32, (c, c), 0) - lax.broadcasted_iota(jnp.int32, (c, c), 1)).astype(F32)
        w_diag = _ret_weights(lgf, lgb, rc)
        for t in range(nt):
            rows = slice(C + t * c, C + (t + 1) * c)
            qt = qs[rows, :]
            s = _dot_nt(qt, ks[rows, :])
            o = (_dot((s * w_diag).astype(BF16), vs[rows, :])
                 + dec.q_f * _dot(qt, hf[t]) + dec.q_b * _dot(qt, hb[t]))
            o_ref[t * c:(t + 1) * c, :] = o
            mu = jnp.mean(o, axis=-1, keepdims=True)
            oc = o - mu
            var = jnp.mean(oc * oc, axis=-1, keepdims=True)
            yh = oc * lax.rsqrt(var + NORM_EPS)
            g = g_ref[rows, :].astype(F32)
            lat_ref[t * c:(t + 1) * c, :] = ((yh * gnv) * (g * _sigmoid(g))).astype(BF16)

    def col(seg):
        return pl.BlockSpec((None, T, RET_DIM), lambda b, h, seg=seg: (b, 0, seg * RET_HEADS + h))

    return _grid_call(
        body, name="ret_fwd", grid=(B, RET_HEADS),
        out_shape=(jax.ShapeDtypeStruct((B, N, RET_WIDTH), F32), jax.ShapeDtypeStruct((B, N, RET_WIDTH), BF16)),
        in_specs=[pl.BlockSpec(memory_space=pltpu.SMEM), col(0), col(1), col(2), col(3),
                  pl.BlockSpec((T, RET_DIM), lambda b, h: (0, 0)), pl.BlockSpec((T, RET_DIM), lambda b, h: (0, 0)),
                  pl.BlockSpec((1, RET_DIM), lambda b, h: (0, h))],
        out_specs=(pl.BlockSpec((None, N, RET_DIM), lambda b, h: (b, 0, h)),
                   pl.BlockSpec((None, N, RET_DIM), lambda b, h: (b, 0, h))),
        scratch_shapes=[pltpu.VMEM((T, RET_DIM), BF16)] * 3 + [pltpu.VMEM((T, RET_DIM), F32)]
                       + [pltpu.VMEM((nt, RET_DIM, RET_DIM), BF16)] * 2,
        args=(lg, proj, proj, proj, proj, cos, sin, gn))


def _ret_bwd(proj, cos, sin, lg, gn, o, dlat, n_ctx):
    B, T, _ = proj.shape
    C = n_ctx
    N = T - C
    c = _div_tile(N, 256, 16)
    nt = N // c
    scale = RET_DIM ** -0.5

    def lat(t):
        return slice(C + t * c, C + (t + 1) * c)

    def body(lg_ref, q_ref, k_ref, v_ref, g_ref, cos_ref, sin_ref, gn_ref, o_ref, dl_ref,
             d_ref, dgn_ref, dlg_ref, qs, ks, vs, dos, qf32, kf32, hf, hb, hfa, hba, gf_s, gb_s):
        h = pl.program_id(1)
        lgf = lg_ref[0, h]
        lgb = lg_ref[1, h]
        gnv = gn_ref[...]

        def fold(a):
            return jnp.sum(a.reshape(a.shape[0] // SUBLANES, SUBLANES, a.shape[1]), axis=0)

        for rows in [slice(0, C)] + [lat(t) for t in range(nt)]:
            cosb = cos_ref[rows, :]
            sinb = sin_ref[rows, :]
            qr = _rope(q_ref[rows, :].astype(F32), cosb, sinb) * scale
            qf32[rows, :] = qr
            qs[rows, :] = qr.astype(BF16)
            kr = _rope(k_ref[rows, :].astype(F32), cosb, sinb)
            kf32[rows, :] = kr
            ks[rows, :] = kr.astype(BF16)
            vs[rows, :] = v_ref[rows, :].astype(BF16)

        dgn = jnp.zeros((1, RET_DIM), F32)
        for t in range(nt):
            lrows = slice(t * c, (t + 1) * c)
            ov = o_ref[lrows, :]
            mu = jnp.mean(ov, axis=-1, keepdims=True)
            oc = ov - mu
            var = jnp.mean(oc * oc, axis=-1, keepdims=True)
            rstd = lax.rsqrt(var + NORM_EPS)
            yh = oc * rstd
            g = g_ref[lat(t), :].astype(F32)
            sg = _sigmoid(g)
            dl = dl_ref[lrows, :]
            d_ref[3, lat(t), :] = (dl * (yh * gnv) * (sg * (1.0 + g * (1.0 - sg)))).astype(BF16)
            dls = dl * (g * sg)
            dgn = dgn + jnp.sum(dls * yh, axis=0, keepdims=True)
            dyh = dls * gnv
            do = rstd * (dyh - jnp.mean(dyh, axis=-1, keepdims=True)
                         - yh * jnp.mean(dyh * yh, axis=-1, keepdims=True))
            dos[lrows, :] = do.astype(BF16)
        dgn_ref[...] = jnp.concatenate([dgn, jnp.zeros((SUBLANES - 1, RET_DIM), F32)], axis=0)
        d_ref[3, 0:C, :] = jnp.zeros((C, RET_DIM), BF16)
        d_ref[0, 0:C, :] = jnp.zeros((C, RET_DIM), BF16)

        dec, dec_c, step_f, step_b = _ret_states(kf32, vs, lgf, lgb, C, c, nt, hf, hb, hfa, hba)

        def zmat(t, qdec):
            return _dot_tn((qf32[lat(t), :] * qdec).astype(BF16), dos[t * c:(t + 1) * c, :])

        acc3f = jnp.zeros((RET_DIM, RET_DIM), F32)
        acc3b = jnp.zeros((RET_DIM, RET_DIM), F32)
        state = jnp.zeros((RET_DIM, RET_DIM), F32)
        for t in range(nt - 1, -1, -1):
            gf_s[t] = state.astype(BF16)
            z = zmat(t, dec.q_f)
            acc3f = acc3f + hfa[t] * z
            state = step_f * state + z
        gctx_f = state.astype(BF16)
        state = jnp.zeros((RET_DIM, RET_DIM), F32)
        for t in range(nt):
            gb_s[t] = state.astype(BF16)
            z = zmat(t, dec.q_b)
            acc3b = acc3b + hba[t] * z
            state = step_b * state + z
        gctx_b = state.astype(BF16)

        rc = (lax.broadcasted_iota(jnp.int32, (c, c), 0) - lax.broadcasted_iota(jnp.int32, (c, c), 1)).astype(F32)
        w_diag = _ret_weights(lgf, lgb, rc)
        wg_f = jnp.where(rc >= 0.0, w_diag * rc, 0.0)
        wg_b = jnp.where(rc < 0.0, -w_diag * rc, 0.0)
        accf = jnp.zeros((SUBLANES, RET_DIM), F32)
        accb = jnp.zeros((SUBLANES, RET_DIM), F32)
        gdf = jnp.zeros((SUBLANES, c), F32)
        gdb = jnp.zeros((SUBLANES, c), F32)
        for t in range(nt):
            rows = lat(t)
            qt = qs[rows, :]
            kt = ks[rows, :]
            vt = vs[rows, :]
            dot = dos[t * c:(t + 1) * c, :]
            s = _dot_nt(qt, kt)
            dp = _dot_nt(dot, vt)
            dv = _dot_tn((s * w_diag).astype(BF16), dot)
            ds = (dp * w_diag).astype(BF16)
            dq = _dot(ds, kt)
            dk = _dot_tn(ds, qt)
            gs = dp * s
            gdf = gdf + fold(gs * wg_f)
            gdb = gdb + fold(gs * wg_b)
            qv = qf32[rows, :]
            kv = kf32[rows, :]
            dq_f = dec.q_f * _dot_nt(dot, hf[t])
            dq_b = dec.q_b * _dot_nt(dot, hb[t])
            dk_f = dec.k_f * _dot_nt(vt, gf_s[t])
            dk_b = dec.k_b * _dot_nt(vt, gb_s[t])
            accf = accf + fold(dec.head * dq_f * qv) + fold(dec.tail * dk_f * kv)
            accb = accb + fold(dec.tail * dq_b * qv) + fold(dec.head * dk_b * kv)
            dv = dv + dec.k_f * _dot(kt, gf_s[t]) + dec.k_b * _dot(kt, gb_s[t])
            cosb = cos_ref[rows, :]
            sinb = sin_ref[rows, :]
            d_ref[0, rows, :] = _unrope((dq + dq_f + dq_b) * scale, cosb, sinb).astype(BF16)
            d_ref[1, rows, :] = _unrope(dk + dk_f + dk_b, cosb, sinb).astype(BF16)
            d_ref[2, rows, :] = dv.astype(BF16)
        kc = ks[0:C, :]
        vc = vs[0:C, :]
        kcv = kf32[0:C, :]
        dkc_f = dec_c.k_f * _dot_nt(vc, gctx_f)
        dkc_b = dec_c.k_b * _dot_nt(vc, gctx_b)
        accf = accf + fold(dec_c.tail * dkc_f * kcv)
        accb = accb + fold(dec_c.head * dkc_b * kcv)
        d_ref[1, 0:C, :] = (dkc_f + dkc_b).astype(BF16)
        d_ref[2, 0:C, :] = (dec_c.k_f * _dot(kc, gctx_f) + dec_c.k_b * _dot(kc, gctx_b)).astype(BF16)
        gf = jnp.sum(gdf) + jnp.sum(accf) + jnp.sum(acc3f)
        gb = jnp.sum(gdb) + jnp.sum(accb) + jnp.sum(acc3b)
        row = lax.broadcasted_iota(jnp.int32, (SUBLANES, LANES), 0)
        dlg_ref[...] = jnp.where(row == 0, gf, jnp.where(row == 1, gb, 0.0))

    def col(seg):
        return pl.BlockSpec((None, T, RET_DIM), lambda b, h, seg=seg: (b, 0, seg * RET_HEADS + h))

    return _grid_call(
        body, name="ret_bwd", grid=(B, RET_HEADS),
        out_shape=(jax.ShapeDtypeStruct((B, 4, T, RET_WIDTH), BF16),
                   jax.ShapeDtypeStruct((B, SUBLANES, RET_WIDTH), F32),
                   jax.ShapeDtypeStruct((B, RET_HEADS, SUBLANES, LANES), F32)),
        in_specs=[pl.BlockSpec(memory_space=pltpu.SMEM), col(0), col(1), col(2), col(3),
                  pl.BlockSpec((T, RET_DIM), lambda b, h: (0, 0)), pl.BlockSpec((T, RET_DIM), lambda b, h: (0, 0)),
                  pl.BlockSpec((1, RET_DIM), lambda b, h: (0, h)),
                  pl.BlockSpec((None, N, RET_DIM), lambda b, h: (b, 0, h)),
                  pl.BlockSpec((None, N, RET_DIM), lambda b, h: (b, 0, h))],
        out_specs=(pl.BlockSpec((None, 4, T, RET_DIM), lambda b, h: (b, 0, 0, h)),
                   pl.BlockSpec((None, SUBLANES, RET_DIM), lambda b, h: (b, 0, h)),
                   pl.BlockSpec((None, None, SUBLANES, LANES), lambda b, h: (b, h, 0, 0))),
        scratch_shapes=[pltpu.VMEM((T, RET_DIM), BF16)] * 3 + [pltpu.VMEM((N, RET_DIM), BF16)]
                       + [pltpu.VMEM((T, RET_DIM), F32)] * 2
                       + [pltpu.VMEM((nt, RET_DIM, RET_DIM), BF16)] * 2 + [pltpu.VMEM((nt, RET_DIM, RET_DIM), F32)] * 2
                       + [pltpu.VMEM((nt, RET_DIM, RET_DIM), BF16)] * 2,
        args=(lg, proj, proj, proj, proj, cos, sin, gn, o, dlat))


def _na_geometry(rows):
    kh = min(NA_KH, rows)
    return kh, kh * GRID_W


def _pair_select():
    lane = lax.broadcasted_iota(jnp.int32, (2 * GRID_W, LANES), 1)
    row = lax.broadcasted_iota(jnp.int32, (2 * GRID_W, LANES), 0)
    return (lane >= NA_DIM) == (row >= GRID_W)


def _pair_bias(bias_ref, dr0, kh):
    return jnp.concatenate(
        [jnp.concatenate([bias_ref[e, pl.ds(dr0 + 2 * m, 1)].reshape(GRID_W, LANES) for m in range(kh // 2)], axis=1)
         for e in range(2)], axis=0)


def _na_softmax(s_loc, s_ctx):
    mx = jnp.maximum(jnp.max(s_loc, axis=-1, keepdims=True), jnp.max(s_ctx, axis=-1, keepdims=True))
    p_loc = jnp.exp(s_loc - mx)
    p_ctx = jnp.exp(s_ctx - mx)
    den = jnp.sum(p_loc, axis=-1, keepdims=True) + jnp.sum(p_ctx, axis=-1, keepdims=True)
    return p_loc, p_ctx, den


def _na_fwd(proj, bias2, n_ctx):
    B, T, _ = proj.shape
    C = n_ctx
    N = T - C
    R = N // GRID_W
    kh, nk = _na_geometry(R)
    scale = NA_DIM ** -0.5
    base = (4 * RET_WIDTH) // LANES

    def body(q_ref, k_ref, v_ref, bias_ref, out_ref, kb16, vb16):
        kb16[...] = k_ref[...].astype(BF16)
        vb16[...] = v_ref[...].astype(BF16)
        kc = kb16[0:C, :]
        vc = vb16[0:C, :]
        lane = lax.broadcasted_iota(jnp.int32, (GRID_W, LANES), 1)
        sel2 = _pair_select()

        def group(gi, carry):
            pre = []
            for u in range(NA_GROUP):
                r = gi * NA_GROUP + u
                bs = jnp.clip(r - kh // 2, 0, R - kh)
                dr0 = bs - r + (NA_KH - 1)
                q = q_ref[pl.ds(pl.multiple_of(C + r * GRID_W, GRID_W), GRID_W), :].astype(F32) * scale
                q2 = jnp.where(sel2, jnp.concatenate([q, q], axis=0), 0.0).astype(BF16)
                band = pl.ds(pl.multiple_of(C + bs * GRID_W, GRID_W), nk)
                s_loc = _dot_nt(q2, kb16[band, :]) + _pair_bias(bias_ref, dr0, kh)
                s_ctx = _dot_nt(q2, kc)
                pre.append((r, band, s_loc, s_ctx))
            mid = [(r, band) + _na_softmax(s_loc, s_ctx) for r, band, s_loc, s_ctx in pre]
            for r, band, p_loc, p_ctx, den in mid:
                o2 = (_dot(p_loc.astype(BF16), vb16[band, :]) + _dot(p_ctx.astype(BF16), vc)) / den
                out_ref[pl.ds(pl.multiple_of(r * GRID_W, GRID_W), GRID_W), :] = jnp.where(
                    lane < NA_DIM, o2[:GRID_W], o2[GRID_W:]).astype(BF16)
            return carry

        lax.fori_loop(0, R // NA_GROUP, group, 0)

    def col(seg):
        return pl.BlockSpec((None, T, LANES), lambda b, p, seg=seg: (b, 0, base + seg * NA_PAIRS + p))

    return _grid_call(
        body, name="na_fwd", grid=(B, NA_PAIRS),
        out_shape=(jax.ShapeDtypeStruct((B, N, NA_WIDTH), BF16),),
        in_specs=[col(0), col(1), col(2),
                  pl.BlockSpec((2, 2 * NA_KH - 2, GRID_W, LANES), lambda b, p: (p, 0, 0, 0))],
        out_specs=(pl.BlockSpec((None, N, LANES), lambda b, p: (b, 0, p)),),
        scratch_shapes=[pltpu.VMEM((T, LANES), BF16)] * 2,
        args=(proj, proj, proj, bias2))


def _na_bwd(proj, bias2, dlat, n_ctx):
    B, T, _ = proj.shape
    C = n_ctx
    N = T - C
    R = N // GRID_W
    kh, nk = _na_geometry(R)
    scale = NA_DIM ** -0.5
    base = (4 * RET_WIDTH) // LANES

    def body(q_ref, k_ref, v_ref, bias_ref, dl_ref, d_ref, db_ref, kb16, vb16, dkv):
        b = pl.program_id(1)
        kb16[...] = k_ref[...].astype(BF16)
        vb16[...] = v_ref[...].astype(BF16)
        kc = kb16[0:C, :]
        vc = vb16[0:C, :]
        lane = lax.broadcasted_iota(jnp.int32, (GRID_W, LANES), 1)
        dkv[...] = jnp.zeros(dkv.shape, F32)
        d_ref[0, 0:C, :] = jnp.zeros((C, LANES), BF16)

        @pl.when(b == 0)
        def _():
            db_ref[...] = jnp.zeros(db_ref.shape, F32)

        sel2 = _pair_select()

        def group(gi, carry):
            pre = []
            for u in range(NA_GROUP):
                r = gi * NA_GROUP + u
                bs = jnp.clip(r - kh // 2, 0, R - kh)
                dr0 = bs - r + (NA_KH - 1)
                q = q_ref[pl.ds(pl.multiple_of(C + r * GRID_W, GRID_W), GRID_W), :].astype(F32) * scale
                do = dl_ref[pl.ds(pl.multiple_of(r * GRID_W, GRID_W), GRID_W), :]
                q2 = jnp.where(sel2, jnp.concatenate([q, q], axis=0), 0.0).astype(BF16)
                do2 = jnp.where(sel2, jnp.concatenate([do, do], axis=0), 0.0).astype(BF16)
                band = pl.ds(pl.multiple_of(C + bs * GRID_W, GRID_W), nk)
                s_loc = _dot_nt(q2, kb16[band, :]) + _pair_bias(bias_ref, dr0, kh)
                s_ctx = _dot_nt(q2, kc)
                dp_loc = _dot_nt(do2, vb16[band, :])
                dp_ctx = _dot_nt(do2, vc)
                pre.append((r, dr0, band, q2, do2, s_loc, s_ctx, dp_loc, dp_ctx))
            mid = []
            for r, dr0, band, q2, do2, s_loc, s_ctx, dp_loc, dp_ctx in pre:
                p_loc, p_ctx, den = _na_softmax(s_loc, s_ctx)
                inv = 1.0 / den
                p_loc = p_loc * inv
                p_ctx = p_ctx * inv
                delta = (jnp.sum(p_loc * dp_loc, axis=-1, keepdims=True)
                         + jnp.sum(p_ctx * dp_ctx, axis=-1, keepdims=True))
                ds_loc = p_loc * (dp_loc - delta)
                ds_ctx = p_ctx * (dp_ctx - delta)
                mid.append((r, dr0, band, q2, do2, p_loc.astype(BF16), p_ctx.astype(BF16), ds_loc, ds_ctx))
            for r, dr0, band, q2, do2, pb_loc, pb_ctx, ds_loc, ds_ctx in mid:
                dsb_loc = ds_loc.astype(BF16)
                dsb_ctx = ds_ctx.astype(BF16)
                dq2 = _dot(dsb_loc, kb16[band, :]) + _dot(dsb_ctx, kc)
                d_ref[0, pl.ds(pl.multiple_of(C + r * GRID_W, GRID_W), GRID_W), :] = (jnp.where(
                    lane < NA_DIM, dq2[:GRID_W], dq2[GRID_W:]) * scale).astype(BF16)
                dkv[0, band, :] += _dot_tn(dsb_loc, q2)
                dkv[1, band, :] += _dot_tn(pb_loc, do2)
                dkv[0, 0:C, :] += _dot_tn(dsb_ctx, q2)
                dkv[1, 0:C, :] += _dot_tn(pb_ctx, do2)
                for e in range(2):
                    for m in range(kh // 2):
                        db_ref[e, pl.ds(dr0 + 2 * m, 1)] += ds_loc[e * GRID_W:(e + 1) * GRID_W,
                                                                   m * LANES:(m + 1) * LANES].reshape(1, GRID_W, LANES)
            return carry

        lax.fori_loop(0, R // NA_GROUP, group, 0)
        d_ref[1] = dkv[0].astype(BF16)
        d_ref[2] = dkv[1].astype(BF16)

    def col(seg):
        return pl.BlockSpec((None, T, LANES), lambda p, b, seg=seg: (b, 0, base + seg * NA_PAIRS + p))

    return _grid_call(
        body, name="na_bwd", grid=(NA_PAIRS, B),
        out_shape=(jax.ShapeDtypeStruct((B, 3, T, NA_WIDTH), BF16),
                   jax.ShapeDtypeStruct((NA_HEADS, 2 * NA_KH - 2, GRID_W, LANES), F32)),
        in_specs=[col(0), col(1), col(2),
                  pl.BlockSpec((2, 2 * NA_KH - 2, GRID_W, LANES), lambda p, b: (p, 0, 0, 0)),
                  pl.BlockSpec((None, N, LANES), lambda p, b: (b, 0, p))],
        out_specs=(pl.BlockSpec((None, 3, T, LANES), lambda p, b: (b, 0, 0, p)),
                   pl.BlockSpec((2, 2 * NA_KH - 2, GRID_W, LANES), lambda p, b: (p, 0, 0, 0))),
        scratch_shapes=[pltpu.VMEM((T, LANES), BF16)] * 2 + [pltpu.VMEM((2, T, LANES), F32)],
        args=(proj, proj, proj, bias2, dlat))


def _split3(a):
    hi = a.astype(BF16)
    r1 = a - hi.astype(F32)
    mid = r1.astype(BF16)
    lo = (r1 - mid.astype(F32)).astype(BF16)
    return hi, mid, lo


def _rpb_reduce(dbias2, onehot2):
    rows = dbias2.shape[0] * dbias2.shape[1]
    flat = dbias2.reshape(rows, GRID_W * LANES)

    def body(a_ref, oh_ref, o_ref):
        hi, mid, lo = _split3(a_ref[...])
        oh = oh_ref[...]
        o_ref[...] = _dot(hi, oh) + _dot(mid, oh) + _dot(lo, oh)

    return pl.pallas_call(
        body, name="rpb_reduce", out_shape=jax.ShapeDtypeStruct((rows, LANES), F32),
        in_specs=[_vmem(), _vmem()], out_specs=_vmem(),
        compiler_params=pltpu.CompilerParams(vmem_limit_bytes=VMEM_LIMIT),
    )(flat, onehot2)


def _dense_core(lat_ret, lat_na, x, tgt, modl, g_post_mix, g_pre_mlp, g_post_mlp, w_out, w1, w2):
    B, N, D = x.shape
    F = w1.shape[1]
    wout_rows, w1_cols, w2_rows = w_out.shape[0] // N_DEV, w1.shape[1] // N_DEV, w2.shape[0] // N_DEV
    mixw = w_out.shape[0]
    half = mixw // 2
    tm = _div_tile(N, 256, 16)
    nt = N // tm
    fc = _div_tile(F, 1024, LANES)

    def body(lr_ref, ln_ref, x_ref, t_ref, gt1_ref, sh2_ref, sc2_ref, gt2_ref, gpm_ref, gpre_ref, gpo_ref,
             wout_part, w1_part, w2_part,
             dy1_ref, dlr_ref, dln_ref, dmix_ref, h2_ref, a_ref, du_ref, dz_ref, red_ref, wout_hbm, w1_hbm, w2_hbm,
             wout_v, w1_v, w2_v, u_s, sems, fsend, frecv):
        @pl.when((pl.program_id(0) == 0) & (pl.program_id(1) == 0))
        def _():
            relay = [(_row_block(wout_part, wout_rows), _row_block(wout_hbm, wout_rows)),
                     (_col_block(w1_part, w1_cols), _col_block(w1_hbm, w1_cols)),
                     (_row_block(w2_part, w2_rows), _row_block(w2_hbm, w2_rows))]
            _forward_start(relay, fsend, frecv)
            _forward_wait(relay, fsend, frecv)
            cps = [pltpu.make_async_copy(wout_hbm, wout_v, sems.at[0]),
                   pltpu.make_async_copy(w1_hbm, w1_v, sems.at[1]),
                   pltpu.make_async_copy(w2_hbm, w2_v, sems.at[2])]
            for cp in cps:
                cp.start()
            for cp in cps:
                cp.wait()

        @pl.when(pl.program_id(1) == 0)
        def _():
            red_ref[...] = jnp.zeros(red_ref.shape, F32)

        gt1 = gt1_ref[...]
        sh2 = sh2_ref[...]
        sc2 = sc2_ref[...]
        gt2 = gt2_ref[...]
        gpm = gpm_ref[...]
        gpre = gpre_ref[...]
        gpo = gpo_ref[...]

        def rowmean(a):
            return jnp.mean(a, axis=-1, keepdims=True)

        def colsum(a):
            return jnp.sum(a, axis=0, keepdims=True)

        mix = _dot(lr_ref[...], wout_v[0:half, :]) + _dot(ln_ref[...], wout_v[half:, :])
        x = x_ref[...]
        rm = lax.rsqrt(rowmean(mix * mix) + NORM_EPS)
        mh = mix * rm
        nm = mh * gpm
        y1 = x + gt1 * nm
        r1 = lax.rsqrt(rowmean(y1 * y1) + NORM_EPS)
        xh = y1 * r1
        n1 = xh * gpre
        h2b = (n1 * (1.0 + sc2) + sh2).astype(BF16)
        h2_ref[...] = h2b
        z = jnp.zeros((tm, D), F32)
        for c0 in range(0, F, fc):
            u = _dot(h2b, w1_v[:, c0:c0 + fc])
            u_s[:, c0:c0 + fc] = u
            ru = jnp.maximum(u, 0.0)
            ab = (ru * ru).astype(BF16)
            a_ref[:, c0:c0 + fc] = ab
            z = z + _dot(ab, w2_v[c0:c0 + fc, :])
        r2 = lax.rsqrt(rowmean(z * z) + NORM_EPS)
        zh = z * r2
        n2 = zh * gpo
        y2 = y1 + gt2 * n2
        err = y2 - t_ref[...]
        loss = 0.5 * jnp.sum(rowmean(err * err))
        dy2 = err * (1.0 / D)
        red_ref[2:3, :] += colsum(dy2 * n2)
        dn2 = dy2 * gt2
        red_ref[6:7, :] += colsum(dn2 * zh)
        dzh = dn2 * gpo
        dz = r2 * (dzh - zh * rowmean(dzh * zh))
        dzb = dz.astype(BF16)
        dz_ref[...] = dzb
        dh2 = jnp.zeros((tm, D), F32)
        for c0 in range(0, F, fc):
            da = _dot_nt(dzb, w2_v[c0:c0 + fc, :])
            dub = (da * (2.0 * jnp.maximum(u_s[:, c0:c0 + fc], 0.0))).astype(BF16)
            du_ref[:, c0:c0 + fc] = dub
            dh2 = dh2 + _dot_nt(dub, w1_v[:, c0:c0 + fc])
        red_ref[3:4, :] += colsum(dh2 * n1)
        red_ref[4:5, :] += colsum(dh2)
        dn1 = dh2 * (1.0 + sc2)
        red_ref[5:6, :] += colsum(dn1 * xh)
        dxh = dn1 * gpre
        dy1 = dy2 + r1 * (dxh - xh * rowmean(dxh * xh))
        dy1_ref[...] = dy1
        red_ref[0:1, :] += colsum(dy1 * nm)
        dnm = dy1 * gt1
        red_ref[1:2, :] += colsum(dnm * mh)
        dmh = dnm * gpm
        dmix = (rm * (dmh - mh * rowmean(dmh * mh))).astype(BF16)
        dmix_ref[...] = dmix
        dlr_ref[...] = _dot_nt(dmix, wout_v[0:half, :])
        dln_ref[...] = _dot_nt(dmix, wout_v[half:, :])
        red_ref[7:8, :] += jnp.zeros((1, D), F32) + loss

    def tok(w):
        return pl.BlockSpec((None, tm, w), lambda b, t: (b, t, 0))

    def mod(k):
        return pl.BlockSpec((None, None, 1, D), lambda b, t, k=k: (b, k, 0, 0))

    def vec():
        return pl.BlockSpec((1, D), lambda b, t: (0, 0))

    return pl.pallas_call(
        body, name="dense_core", grid=(B, nt),
        out_shape=(jax.ShapeDtypeStruct((B, N, D), F32), jax.ShapeDtypeStruct((B, N, half), F32),
                   jax.ShapeDtypeStruct((B, N, half), F32), jax.ShapeDtypeStruct((B, N, D), BF16),
                   jax.ShapeDtypeStruct((B, N, D), BF16), jax.ShapeDtypeStruct((B, N, F), BF16),
                   jax.ShapeDtypeStruct((B, N, F), BF16), jax.ShapeDtypeStruct((B, N, D), BF16),
                   jax.ShapeDtypeStruct((B, SUBLANES, D), F32),
                   jax.ShapeDtypeStruct(w_out.shape, w_out.dtype), jax.ShapeDtypeStruct(w1.shape, w1.dtype),
                   jax.ShapeDtypeStruct(w2.shape, w2.dtype)),
        in_specs=[tok(half), tok(half), tok(D), tok(D), mod(2), mod(3), mod(4), mod(5), vec(), vec(), vec(),
                  _any(), _any(), _any()],
        out_specs=(tok(D), tok(half), tok(half), tok(D), tok(D), tok(F), tok(F), tok(D),
                   pl.BlockSpec((None, SUBLANES, D), lambda b, t: (b, 0, 0)), _any(), _any(), _any()),
        scratch_shapes=[pltpu.VMEM((mixw, D), BF16), pltpu.VMEM((D, F), BF16), pltpu.VMEM((F, D), BF16),
                        pltpu.VMEM((tm, F), F32), pltpu.SemaphoreType.DMA((3,)),
                        pltpu.SemaphoreType.DMA((3, 3)), pltpu.SemaphoreType.DMA((3, 3))],
        input_output_aliases={11: 9, 12: 10, 13: 11},
        compiler_params=_params("arbitrary", "arbitrary"),
    )(lat_ret, lat_na, x, tgt, modl, modl, modl, modl, g_post_mix, g_pre_mlp, g_post_mlp, w_out, w1, w2)[:9]


def _inproj_bwd(dret, dna, x, ctx, dy1, modl, g1, w_in_t):
    B, N, D = x.shape
    n_ctx = ctx.shape[1]
    T = n_ctx + N
    tm = _div_tile(n_ctx, 256, 16)
    nct, ctx_spec, lat_spec = _token_tiles(n_ctx, tm)
    nt = T // tm
    nseg_r = dret.shape[1]
    nseg_n = dna.shape[1]
    nw = w_in_t.shape[0]

    def body(*refs):
        seg_refs = refs[:nseg_r + nseg_n]
        c_ref, x_ref, dy1_ref, sc_ref, g_ref, w_ref, dx_ref, red_ref = refs[nseg_r + nseg_n:]
        t = pl.program_id(1)
        dh = jnp.zeros((tm, D), F32)
        for s, ref in enumerate(seg_refs):
            dh = dh + _dot(ref[...], w_ref[s * SEG:(s + 1) * SEG, :])
        x = jnp.where(t < nct, c_ref[...], x_ref[...])
        g = g_ref[...]
        r = lax.rsqrt(jnp.mean(x * x, axis=-1, keepdims=True) + NORM_EPS)
        xh = x * r
        red_ref[0:1, :] = jnp.sum(dh, axis=0, keepdims=True)
        red_ref[1:2, :] = jnp.sum(dh * (xh * g), axis=0, keepdims=True)
        dn = dh * (1.0 + sc_ref[...])
        red_ref[2:3, :] = jnp.sum(dn * xh, axis=0, keepdims=True)
        red_ref[3:, :] = jnp.zeros((SUBLANES - 3, D), F32)
        dxh = dn * g
        dx = r * (dxh - xh * jnp.mean(dxh * xh, axis=-1, keepdims=True))
        dx_ref[...] = dx + jnp.where(t >= nct, dy1_ref[...], 0.0)

    def mrow(b, t):
        return jnp.where(t < nct, B, b)

    def seg(s):
        return pl.BlockSpec((None, None, tm, SEG), lambda b, t, s=s: (b, s, t, 0))

    return _grid_call(
        body, name="inproj_bwd", grid=(B, nt),
        out_shape=(jax.ShapeDtypeStruct((B, N, D), F32), jax.ShapeDtypeStruct((B, nt, SUBLANES, D), F32)),
        in_specs=[seg(s) for s in range(nseg_r)] + [seg(s) for s in range(nseg_n)]
                 + [ctx_spec(D), lat_spec(D), lat_spec(D),
                    pl.BlockSpec((None, None, 1, D), lambda b, t: (mrow(b, t), 1, 0, 0)),
                    pl.BlockSpec((1, D), lambda b, t: (0, 0)),
                    pl.BlockSpec((nw, D), lambda b, t: (0, 0))],
        out_specs=(lat_spec(D), pl.BlockSpec((None, None, SUBLANES, D), lambda b, t: (b, t, 0, 0))),
        scratch_shapes=[], args=(*([dret] * nseg_r), *([dna] * nseg_n), ctx, x, dy1, modl, g1, w_in_t))


def _tn_matmul(lhs, rhs, name, rows_before=0, rows_after=0, into=None):
    B, S, T, W = lhs.shape
    nn = rhs.shape[-1]
    tk = _div_tile(T, 2304, LANES)
    bm = _div_tile(W, 1024, LANES)
    bn = _div_tile(nn, 1024, LANES)
    nkt = T // tk
    nk = B * nkt

    def body(l_ref, r_ref, *rest):
        o_ref, acc = rest[-2:]
        k = pl.program_id(3)

        @pl.when(k == 0)
        def _():
            acc[...] = jnp.zeros(acc.shape, F32)

        acc[...] += _dot_tn(l_ref[...].astype(BF16), r_ref[...].astype(BF16))

        @pl.when(k == nk - 1)
        def _():
            o_ref[...] = acc[...].astype(BF16)

    nwb = W // bm
    first = rows_before // bm
    return pl.pallas_call(
        functools.partial(body), name=name, grid=(S, nwb, nn // bn, nk),
        out_shape=jax.ShapeDtypeStruct((rows_before + S * W + rows_after, nn), BF16),
        in_specs=[pl.BlockSpec((None, None, tk, bm), lambda s, i, j, k: (k // nkt, s, k % nkt, i)),
                  pl.BlockSpec((None, tk, bn), lambda s, i, j, k: (k // nkt, k % nkt, j))]
                 + ([] if into is None else [_any()]),
        out_specs=pl.BlockSpec((bm, bn), lambda s, i, j, k: (first + s * nwb + i, j)),
        scratch_shapes=[pltpu.VMEM((bm, bn), F32)],
        input_output_aliases={} if into is None else {2: 0},
        compiler_params=_params("parallel", "parallel", "parallel", "arbitrary"),
    )(lhs, rhs, *([] if into is None else [into]))


class _SplitScatter:
    def __init__(self, gs, block_ofs, land_shapes, name, kind="scatter", masks=ALL_PEERS):
        self.n = n = len(gs)
        self.block_ofs, self.kind, self.masks = block_ofs, kind, masks
        if kind == "scatter":
            land_shapes = [(N_DEV,) + tuple(bs) for bs in land_shapes]
        hbm = pl.BlockSpec(memory_space=pltpu.HBM)
        sem = pl.BlockSpec(memory_space=pltpu.SEMAPHORE)

        def body(*refs):
            g_refs, land_refs = refs[:n], refs[n:2 * n]
            send_sems, recv_sems, own_sems = refs[2 * n:2 * n + 3]
            token = refs[-1]
            for own, pushes in self._copies(g_refs, land_refs, send_sems, recv_sems, own_sems, landing="sender"):
                own.start()
                for cp in pushes:
                    cp.start()
            token[...] = jnp.zeros_like(token)

        outs = pl.pallas_call(
            body, name=name,
            out_shape=(pltpu.SemaphoreType.DMA((n * (N_DEV - 1),)), pltpu.SemaphoreType.DMA((n * (N_DEV - 1),)),
                       pltpu.SemaphoreType.DMA((n,)))
                      + tuple(pltpu.HBM(g.shape, g.dtype) for g in gs)
                      + tuple(pltpu.HBM(s, g.dtype) for s, g in zip(land_shapes, gs))
                      + (jax.ShapeDtypeStruct((SUBLANES, LANES), F32),),
            in_specs=(hbm,) * (2 * n), out_specs=(sem,) * 3 + (hbm,) * (2 * n) + (_vmem(),),
            input_output_aliases={k: 3 + k for k in range(2 * n)},
            compiler_params=pltpu.CompilerParams(has_side_effects=pltpu.SideEffectType.DATAFLOW_SIDE_EFFECTING),
        )(*[pltpu.with_memory_space_constraint(g, pltpu.HBM) for g in gs],
          *[pltpu.with_memory_space_constraint(lax.empty(s, g.dtype), pltpu.HBM) for s, g in zip(land_shapes, gs)])
        self.sems, self.thru, self.token = outs[:3], outs[3:3 + 2 * n], outs[-1]

    def _copies(self, g_refs, land_refs, send_sems, recv_sems, own_sems, landing):
        me, peers = _me_and_peers()
        out = []
        for k in range(self.n):
            if self.kind == "scatter":
                src, dst = self.block_ofs[k](g_refs[k]), _slot(land_refs[k])
            else:
                src, dst = (lambda p, k=k: g_refs[k]), self.block_ofs[k](land_refs[k])
            own = pltpu.make_async_copy(src(me), dst(me), own_sems.at[k])
            pushes = []
            for m in self.masks:
                dev, pid = peers[m - 1]
                i = k * (N_DEV - 1) + m - 1
                pushes.append(_remote(src(pid), dst(me if landing == "sender" else pid),
                                      send_sems.at[i], recv_sems.at[i], dev))
            out.append((own, pushes))
        return out


def _scatter_wait(scatters, after, name):
    hbm = pl.BlockSpec(memory_space=pltpu.HBM)
    sem = pl.BlockSpec(memory_space=pltpu.SEMAPHORE)
    n_arr = [2 * sc.n for sc in scatters]
    total = sum(n_arr)

    def body(*refs):
        arrs, sems = refs[:total], refs[total:total + 3 * len(scatters)]
        a0 = 0
        for j, sc in enumerate(scatters):
            g_refs, land_refs = arrs[a0:a0 + sc.n], arrs[a0 + sc.n:a0 + 2 * sc.n]
            a0 += 2 * sc.n
            send_sems, recv_sems, own_sems = sems[3 * j:3 * j + 3]
            for (own, sent), (_, got) in zip(sc._copies(g_refs, land_refs, send_sems, recv_sems, own_sems, "sender"),
                                             sc._copies(g_refs, land_refs, send_sems, recv_sems, own_sems, "receiver")):
                own.wait()
                for cp in sent:
                    cp.wait_send()
                for cp in got:
                    cp.wait_recv()

    operands = [a for sc in scatters for a in sc.thru]
    outs = pl.pallas_call(
        body, name=name,
        out_shape=tuple(pltpu.HBM(a.shape, a.dtype) for a in operands),
        in_specs=(hbm,) * total + (sem,) * (3 * len(scatters)) + (pl.BlockSpec(memory_space=pl.ANY),),
        out_specs=(hbm,) * total, input_output_aliases={k: k for k in range(total)},
        compiler_params=pltpu.CompilerParams(has_side_effects=pltpu.SideEffectType.DATAFLOW_SIDE_EFFECTING),
    )(*operands, *[s for sc in scatters for s in sc.sems], after)
    lands, a0 = [], 0
    for sc in scatters:
        lands.extend(outs[a0 + sc.n:a0 + 2 * sc.n])
        a0 += 2 * sc.n
    return lands


def _small_ar(vbuf, mbuf, silu_all, w_ada, c_ctx):
    rv = vbuf.shape[1]
    D = silu_all.shape[1]
    ncol = w_ada.shape[1]
    nm = mbuf.shape[2]
    srows = silu_all.shape[0]

    def body(vbuf, mbuf, s_ref, w_ref, cc_ref, tot_ref, gb_ref, gw_ref, gc_ref, tbuf, dmx, cmrow, send3, recv3):
        me, _ = _me_and_peers()
        tot = vbuf[0]
        msum = mbuf[0]
        for k in range(1, N_DEV):
            tot = tot + vbuf[k]
            msum = msum + mbuf[k]
        tot_ref[...] = tot
        gb_ref[...] = jnp.sum(msum, axis=0, keepdims=True)
        loc = pl.ds(pl.multiple_of(me * ncol, ncol), ncol)
        for k in range(N_DEV):
            dmx[k * SUBLANES:(k + 1) * SUBLANES, :] = mbuf[k, :, loc]
        cmrow[...] = msum
        cm_loc = cmrow[2:3, loc]
        dmx[N_DEV * SUBLANES:, :] = jnp.concatenate([cm_loc, jnp.zeros((SUBLANES - 1, ncol), F32)], axis=0)
        gw_ref[...] = _dot_tn(s_ref[...], dmx[...])
        tbuf[me] = _dot_nt(dmx[N_DEV * SUBLANES:, :], w_ref[...])
        _exchange(lambda p: tbuf.at[me], lambda p: tbuf.at[p], send3, recv3)
        tsum = tbuf[0]
        for k in range(1, N_DEV):
            tsum = tsum + tbuf[k]
        cc = cc_ref[...]
        sg = _sigmoid(cc)
        gc_ref[...] = tsum[0:1, :] * (sg * (1.0 + cc * (1.0 - sg)))

    return pl.pallas_call(
        body, name="small_ar",
        out_shape=(jax.ShapeDtypeStruct((rv, LANES), F32), jax.ShapeDtypeStruct((1, nm), F32),
                   jax.ShapeDtypeStruct((D, ncol), F32), jax.ShapeDtypeStruct((1, D), F32)),
        in_specs=[_vmem()] * 5, out_specs=(_vmem(),) * 4,
        scratch_shapes=[pltpu.VMEM((N_DEV, SUBLANES, D), F32), pltpu.VMEM((srows, ncol), F32),
                        pltpu.VMEM((SUBLANES, nm), F32)] + [pltpu.SemaphoreType.DMA((N_DEV - 1,))] * 2,
        compiler_params=pltpu.CompilerParams(vmem_limit_bytes=VMEM_LIMIT),
    )(vbuf, mbuf, silu_all, w_ada, c_ctx.reshape(1, D))


def _adam_update(w, g, m, v):
    mn = ADAM_B1 * m + (1.0 - ADAM_B1) * g
    vn = ADAM_B2 * v + (1.0 - ADAM_B2) * (g * g)
    m_hat = mn / (1.0 - ADAM_B1 ** ADAM_STEP)
    v_hat = vn / (1.0 - ADAM_B2 ** ADAM_STEP)
    return -ADAM_LR * (m_hat / (jnp.sqrt(v_hat) + ADAM_EPS) + ADAM_WD * w), mn, vn


def _adamw(w, g, m, v, name):
    rows, cols = w.shape
    tr = _div_tile(rows, 256, SUBLANES) if rows * cols > 65536 else rows

    def body(w_ref, g_ref, m_ref, v_ref, d_ref, nm_ref, nv_ref):
        d_ref[...], nm_ref[...], nv_ref[...] = _adam_update(w_ref[...], g_ref[...], m_ref[...], v_ref[...])

    spec = pl.BlockSpec((tr, cols), lambda i: (i, 0))
    return pl.pallas_call(
        functools.partial(body), name=name, grid=(rows // tr,),
        out_shape=(jax.ShapeDtypeStruct((rows, cols), F32),) * 3,
        in_specs=[spec] * 4, out_specs=(spec,) * 3,
        compiler_params=_params("parallel"),
    )(w, g, m, v)


def _adamw_small(items, name):
    n = len(items)

    def body(*refs):
        ins, outs = refs[:4 * n], refs[4 * n:]
        for i in range(n):
            w_ref, g_ref, m_ref, v_ref = ins[4 * i:4 * i + 4]
            outs[3 * i][...], outs[3 * i + 1][...], outs[3 * i + 2][...] = _adam_update(
                w_ref[...], g_ref[...], m_ref[...], v_ref[...])

    outs = pl.pallas_call(
        body, name=name,
        out_shape=tuple(jax.ShapeDtypeStruct(it[0].shape, F32) for it in items for _ in range(3)),
        in_specs=[_vmem()] * (4 * n), out_specs=(_vmem(),) * (3 * n),
        compiler_params=pltpu.CompilerParams(vmem_limit_bytes=VMEM_LIMIT),
    )(*[a for it in items for a in it])
    return [tuple(outs[3 * i:3 * i + 3]) for i in range(n)]


def _sum_adamw(buf, w, m, v, name):
    _, rows, cols = buf.shape
    tr = _div_tile(rows, 256, 2 * SUBLANES)

    def body(b_ref, w_ref, m_ref, v_ref, g_ref, d_ref, nm_ref, nv_ref):
        g = b_ref[0].astype(F32)
        for k in range(1, N_DEV):
            g = g + b_ref[k].astype(F32)
        g_ref[...] = g
        d_ref[...], nm_ref[...], nv_ref[...] = _adam_update(w_ref[...], g, m_ref[...], v_ref[...])

    spec = pl.BlockSpec((tr, cols), lambda i: (i, 0))
    return pl.pallas_call(
        functools.partial(body), name=name, grid=(rows // tr,),
        out_shape=(jax.ShapeDtypeStruct((rows, cols), F32),) * 4,
        in_specs=[pl.BlockSpec((N_DEV, tr, cols), lambda i: (0, i, 0))] + [spec] * 3, out_specs=(spec,) * 4,
        compiler_params=_params("parallel"),
    )(buf, w, m, v)


def _rope_tables(n_ctx, n):
    n_freq = RET_DIM // 4
    inv = np.float32(ROPE_BASE) ** (-np.arange(n_freq, dtype=np.float32) / np.float32(n_freq))
    tok = np.arange(n)
    pos_r = (tok // GRID_W).astype(np.float32)
    pos_c = (tok % GRID_W).astype(np.float32)
    ang_r = (pos_r[:, None] * inv[None, :]).astype(np.float32)
    ang_c = (pos_c[:, None] * inv[None, :]).astype(np.float32)
    cos = np.concatenate([np.cos(ang_r), np.cos(ang_r), np.cos(ang_c), np.cos(ang_c)], axis=-1)
    sin = np.concatenate([-np.sin(ang_r), np.sin(ang_r), -np.sin(ang_c), np.sin(ang_c)], axis=-1)
    cos = np.concatenate([np.ones((n_ctx, RET_DIM), np.float32), cos], axis=0)
    sin = np.concatenate([np.zeros((n_ctx, RET_DIM), np.float32), sin], axis=0)
    return jnp.asarray(cos, F32), jnp.asarray(sin, F32)


def _na_tables():
    q = np.arange(GRID_W)[:, None]
    k = np.arange(GRID_W)[None, :]
    start = np.clip(q - NA_KW // 2, 0, GRID_W - NA_KW)
    valid = (k >= start) & (k < start + NA_KW)
    dc = np.clip(k - q + (NA_KW - 1), 0, 2 * NA_KW - 2)
    ncls = 2 * NA_KW - 1
    onehot = (dc[None] == np.arange(ncls)[:, None, None]) & valid[None]
    oh2 = np.zeros((GRID_W, LANES, LANES), np.float32)
    for c in range(ncls):
        oh2[:, :GRID_W, c] = onehot[c]
        oh2[:, GRID_W:, 32 + c] = onehot[c]
    return onehot.astype(np.float32), valid, oh2.reshape(GRID_W * LANES, LANES)


def _paired_bias(rpb, onehot, valid):
    t = jnp.einsum("hdc,cqk->hdqk", rpb, jnp.asarray(onehot), precision=lax.Precision.HIGHEST)
    t = jnp.where(jnp.asarray(valid)[None, None], t, NEG_INF)
    return jnp.concatenate([t[:, :-1], t[:, 1:]], axis=-1)


def kernel(x, c, ctx, c_ctx, w_ada, b_ada, g_pre_mix, g_post_mix, g_pre_mlp, g_post_mlp, w_in, ret_decay, ret_gn, na_rpb, w_out, w_mlp1, w_mlp2, loss_target, m_c_ctx, m_w_ada, m_b_ada, m_g_pre_mix, m_g_post_mix, m_g_pre_mlp, m_g_post_mlp, m_w_in, m_ret_decay, m_ret_gn, m_na_rpb, m_w_out, m_w_mlp1, m_w_mlp2, v_c_ctx, v_w_ada, v_b_ada, v_g_pre_mix, v_g_post_mix, v_g_pre_mlp, v_g_post_mlp, v_w_in, v_ret_decay, v_ret_gn, v_na_rpb, v_w_out, v_w_mlp1, v_w_mlp2):
    B, N, D = x.shape
    C = ctx.shape[1]
    T = C + N

    silu_all, mods_g, win_b, wout_l, w1_l, w2_l = _mod_gather(c, c_ctx, w_ada[0], b_ada, w_in[0].T, w_out[0],
                                                             w_mlp1[0], w_mlp2[0])
    mods_mine = mods_g.transpose(1, 0, 2).reshape(mods_g.shape[1], N_MOD * D)
    modl = jnp.concatenate([mods_mine[:B], mods_mine[SUBLANES:SUBLANES + 1]], axis=0)
    modl = modl.reshape(B + 1, N_MOD, 1, D)
    rin = w_in.shape[2]
    rout, c1, r2 = wout_l.shape[0], w1_l.shape[1], w2_l.shape[0]

    def rows_of(n):
        return lambda ref: _row_block(ref, n)

    def cols_of(n):
        return lambda ref: _col_block(ref, n)

    cos, sin = _rope_tables(C, N)
    onehot, valid, oh2 = _na_tables()
    bias2 = _paired_bias(na_rpb[0], onehot, valid)
    lg = jax.nn.log_sigmoid(ret_decay[0].astype(F32))

    ag = _SplitScatter([wout_l, w1_l, w2_l], [rows_of(rout), cols_of(c1), rows_of(r2)],
                       [(N_DEV * rout, D), (D, N_DEV * c1), (N_DEV * r2, D)], "ag_mlp_start",
                       kind="gather", masks=SIBLING + ICI_SAME_CORE)
    h_all, proj = _inproj_fwd(x, ctx, modl, g_pre_mix + ag.token[0, 0], win_b)
    o_ret, lat_ret = _ret_fwd(proj, cos, sin, lg, ret_gn, C)
    (lat_na,) = _na_fwd(proj, bias2, C)
    wout_part, w1_part, w2_part = _scatter_wait([ag], lat_na, "ag_mlp_wait")

    (dy1, dlat_ret, dlat_na, dmix, h2, act, du, dz, red_d) = _dense_core(
        lat_ret, lat_na, x, loss_target, modl, g_post_mix, g_pre_mlp, g_post_mlp, wout_part, w1_part, w2_part)

    gw_out_p = _tn_matmul(lat_ret[:, None], dmix, "gw_out_ret", rows_after=lat_na.shape[-1])
    gw_out_p = _tn_matmul(lat_na[:, None], dmix, "gw_out_na", rows_before=lat_ret.shape[-1], into=gw_out_p)
    gw1_p = _tn_matmul(h2[:, None], du, "gw_mlp1")
    gw2_p = _tn_matmul(act[:, None], dz, "gw_mlp2")
    rs_mlp = _SplitScatter([gw_out_p, gw1_p, gw2_p], [rows_of(rout), cols_of(c1), rows_of(r2)],
                           [(rout, D), (D, c1), (r2, D)], "rs_mlp_start")

    dret, dgn_p, dlg_p = _ret_bwd(proj, cos, sin, lg, ret_gn + rs_mlp.token[0, 0], o_ret, dlat_ret, C)
    dna, dbias2 = _na_bwd(proj, bias2, dlat_na, C)
    ret_cols, na_cols = dret.shape[1] * dret.shape[3], dna.shape[1] * dna.shape[3]
    gwin_t_p = _tn_matmul(dret, h_all, "gw_in_ret", rows_after=na_cols)
    gwin_t_p = _tn_matmul(dna, h_all, "gw_in_na", rows_before=ret_cols, into=gwin_t_p)
    rs_in = _SplitScatter([gwin_t_p], [rows_of(rin)], [(rin, D)], "rs_w_in_start")
    grad_x, red_i = _inproj_bwd(dret, dna, x, ctx, dy1, modl, g_pre_mix + rs_in.token[0, 0], win_b)

    rd = red_d
    nct = red_i.shape[1] * C // T
    ri_ctx = red_i[:, :nct].sum(axis=(0, 1))
    ri_lat = red_i[:, nct:].sum(axis=1)
    d_mods = jnp.concatenate([ri_lat[:, 0], ri_lat[:, 1], rd[:, 0], rd[:, 4], rd[:, 3], rd[:, 2]], axis=-1)
    d_cmods = jnp.concatenate([ri_ctx[0], ri_ctx[1], jnp.zeros(((N_MOD - 2) * D,), F32)])[None]
    dm_slot = jnp.concatenate([d_mods, d_cmods, jnp.zeros((SUBLANES - B - 1, N_MOD * D), F32)], axis=0)
    dg_pre_mix = ri_lat[:, 2].sum(axis=0) + ri_ctx[2]
    dg_post_mix = rd[:, 1].sum(axis=0)
    dg_pre_mlp = rd[:, 5].sum(axis=0)
    dg_post_mlp = rd[:, 6].sum(axis=0)
    loss_p = rd[:, 7, 0].sum()
    d_gn = dgn_p[:, 0].sum(axis=0)
    d_lg = dlg_p[:, :, :2, 0].sum(axis=0).T
    d_decay = d_lg * jax.nn.sigmoid(-ret_decay[0].astype(F32))
    rr = _rpb_reduce(dbias2, jnp.asarray(oh2, BF16)).reshape(NA_HEADS, 2 * NA_KH - 2, LANES)
    ncls = 2 * NA_KW - 1
    d_rpb = (jnp.pad(rr[:, :, :ncls], ((0, 0), (0, 1), (0, 0))) + jnp.pad(rr[:, :, 32:32 + ncls], ((0, 0), (1, 0), (0, 0))))
    d_rpb32 = jnp.pad(d_rpb, ((0, 0), (0, 0), (0, 32 - ncls)))
    pieces = [dg_pre_mix, dg_post_mix, dg_pre_mlp, dg_post_mlp, d_gn, d_rpb32.reshape(-1),
              jnp.pad(d_decay.reshape(-1), (0, LANES - d_decay.size)), jnp.full((LANES,), loss_p, F32)]
    vec = jnp.concatenate(pieces)
    pad = (-vec.shape[0]) % (SUBLANES * LANES)
    vec = jnp.pad(vec, (0, pad)).reshape(-1, LANES)
    def whole(ref):
        return lambda p: ref

    small = _SplitScatter([vec, dm_slot], [whole, whole], [vec.shape, dm_slot.shape], "small_start")
    land_out, land_1, land_2, land_in = _scatter_wait([rs_mlp, rs_in], small.token, "rs_wait")
    fused = {"w_in": [a.T for a in _sum_adamw(land_in, w_in[0].T, m_w_in[0].T, v_w_in[0].T, "sum_adamw_w_in")],
             "w_out": _sum_adamw(land_out, w_out[0], m_w_out[0], v_w_out[0], "sum_adamw_w_out"),
             "w_mlp1": _sum_adamw(land_1, w_mlp1[0], m_w_mlp1[0], v_w_mlp1[0], "sum_adamw_w_mlp1"),
             "w_mlp2": _sum_adamw(land_2, w_mlp2[0], m_w_mlp2[0], v_w_mlp2[0], "sum_adamw_w_mlp2")}
    vbuf, mbuf = _scatter_wait([small], fused["w_mlp2"][0], "small_wait")
    tot, g_b_ada, g_w_ada, g_c_ctx = _small_ar(vbuf, mbuf, silu_all, w_ada[0], c_ctx)
    flat = tot.reshape(-1)
    o0 = 0
    g_pre_mix_g = flat[o0:o0 + D]; o0 += D
    g_post_mix_g = flat[o0:o0 + D]; o0 += D
    g_pre_mlp_g = flat[o0:o0 + D]; o0 += D
    g_post_mlp_g = flat[o0:o0 + D]; o0 += D
    g_gn = flat[o0:o0 + RET_WIDTH]; o0 += RET_WIDTH
    nrpb = NA_HEADS * (2 * NA_KH - 1) * 32
    g_rpb = flat[o0:o0 + nrpb].reshape(NA_HEADS, 2 * NA_KH - 1, 32)[:, :, :ncls]; o0 += nrpb
    g_decay = flat[o0:o0 + 2 * RET_HEADS].reshape(2, RET_HEADS); o0 += LANES
    loss = flat[o0]

    grads = {
        "c_ctx": g_c_ctx.reshape(c_ctx.shape), "w_ada": g_w_ada[None], "b_ada": g_b_ada.reshape(b_ada.shape),
        "g_pre_mix": g_pre_mix_g[None], "g_post_mix": g_post_mix_g[None], "g_pre_mlp": g_pre_mlp_g[None],
        "g_post_mlp": g_post_mlp_g[None], "w_in": fused["w_in"][0][None], "ret_decay": g_decay[None], "ret_gn": g_gn[None],
        "na_rpb": g_rpb[None], "w_out": fused["w_out"][0][None], "w_mlp1": fused["w_mlp1"][0][None],
        "w_mlp2": fused["w_mlp2"][0][None],
    }
    weights = dict(c_ctx=c_ctx, w_ada=w_ada, b_ada=b_ada, g_pre_mix=g_pre_mix, g_post_mix=g_post_mix,
                   g_pre_mlp=g_pre_mlp, g_post_mlp=g_post_mlp, w_in=w_in, ret_decay=ret_decay, ret_gn=ret_gn,
                   na_rpb=na_rpb, w_out=w_out, w_mlp1=w_mlp1, w_mlp2=w_mlp2)
    m_in = dict(c_ctx=m_c_ctx, w_ada=m_w_ada, b_ada=m_b_ada, g_pre_mix=m_g_pre_mix, g_post_mix=m_g_post_mix,
                g_pre_mlp=m_g_pre_mlp, g_post_mlp=m_g_post_mlp, w_in=m_w_in, ret_decay=m_ret_decay,
                ret_gn=m_ret_gn, na_rpb=m_na_rpb, w_out=m_w_out, w_mlp1=m_w_mlp1, w_mlp2=m_w_mlp2)
    v_in = dict(c_ctx=v_c_ctx, w_ada=v_w_ada, b_ada=v_b_ada, g_pre_mix=v_g_pre_mix, g_post_mix=v_g_post_mix,
                g_pre_mlp=v_g_pre_mlp, g_post_mlp=v_g_post_mlp, w_in=v_w_in, ret_decay=v_ret_decay,
                ret_gn=v_ret_gn, na_rpb=v_na_rpb, w_out=v_w_out, w_mlp1=v_w_mlp1, w_mlp2=v_w_mlp2)
    names = list(weights)
    deltas, new_m, new_v = {}, {}, {}
    def as_2d(n):
        shp = weights[n].shape
        two_d = (-1, shp[-1]) if len(shp) > 1 else (1, shp[0])
        return [a.reshape(two_d) for a in (weights[n], grads[n], m_in[n], v_in[n])]

    small = [n for n in names if n not in fused and weights[n].size <= 65536]
    updated = dict(zip(small, _adamw_small([as_2d(n) for n in small], "adamw_small")))
    for n in names:
        if n in fused:
            updated[n] = fused[n][1:]
        elif n not in updated:
            updated[n] = _adamw(*as_2d(n), "adamw_" + n)
        deltas[n], new_m[n], new_v[n] = (a.reshape(weights[n].shape) for a in updated[n])
    return (loss, grad_x, *[grads[n] for n in names], *[deltas[n] for n in names],
            *[new_m[n] for n in names], *[new_v[n] for n in names])
```

```python
import functools
import math

import numpy as np
import jax
import jax.numpy as jnp
from jax import lax
from jax.experimental import pallas as pl
from jax.experimental.pallas import tpu as pltpu

F32 = jnp.float32
BF16 = jnp.bfloat16
MESH = pl.DeviceIdType.MESH

N_DEV = 8
LANES = 128
SUBLANES = 8
VMEM_LIMIT = 60 * 1024 * 1024

GRID_W = 64
RET_HEADS = 4
RET_DIM = 128
RET_WIDTH = RET_HEADS * RET_DIM
NA_HEADS = 8
NA_DIM = 64
NA_WIDTH = NA_HEADS * NA_DIM
NA_PAIRS = NA_HEADS // 2
NA_KH = 8
NA_KW = 16
NA_GROUP = 8
SEG = 512
ROPE_BASE = 10000.0
NORM_EPS = 1e-6
NEG_INF = -1e30
N_MOD = 6

ADAM_LR = 0.001
ADAM_B1 = 0.9
ADAM_B2 = 0.999
ADAM_EPS = 1e-08
ADAM_WD = 0.01
ADAM_STEP = 10


def _dot(a, b):
    return lax.dot_general(a, b, (((1,), (0,)), ((), ())), preferred_element_type=F32)


def _dot_nt(a, b):
    return lax.dot_general(a, b, (((1,), (1,)), ((), ())), preferred_element_type=F32)


def _dot_tn(a, b):
    return lax.dot_general(a, b, (((0,), (0,)), ((), ())), preferred_element_type=F32)


def _sigmoid(x):
    return 1.0 / (1.0 + jnp.exp(-x))


def _div_tile(n, cap, mult):
    if n <= cap:
        return n
    for t in range(cap - cap % mult, 0, -mult):
        if n % t == 0:
            return t
    raise ValueError(f"no tile for {n}")


def _params(*sem):
    return pltpu.CompilerParams(dimension_semantics=tuple(sem) if sem else None,
                                vmem_limit_bytes=VMEM_LIMIT)


def _vmem():
    return pl.BlockSpec(memory_space=pltpu.VMEM)


def _any():
    return pl.BlockSpec(memory_space=pl.ANY)


def _me_and_peers():
    x, y, c = lax.axis_index("x"), lax.axis_index("y"), lax.axis_index("c")
    me = 4 * x + 2 * y + c
    peers = []
    for m in range(1, N_DEV):
        px = 1 - x if (m >> 2) & 1 else x
        py = 1 - y if (m >> 1) & 1 else y
        pc = 1 - c if m & 1 else c
        peers.append(((px, py, pc), 4 * px + 2 * py + pc))
    return me, peers


def _exchange(src_for, dst_from, send_sems, recv_sems):
    me, peers = _me_and_peers()
    sent = []
    for i, (dev, pid) in enumerate(peers):
        cp = pltpu.make_async_remote_copy(src_ref=src_for(pid), dst_ref=dst_from(me),
                                          send_sem=send_sems.at[i], recv_sem=recv_sems.at[i],
                                          device_id=dev, device_id_type=MESH)
        cp.start()
        sent.append(cp)
    for i, (dev, pid) in enumerate(peers):
        pltpu.make_async_remote_copy(src_ref=src_for(pid), dst_ref=dst_from(pid),
                                     send_sem=send_sems.at[i], recv_sem=recv_sems.at[i],
                                     device_id=dev, device_id_type=MESH).wait_recv()
    for cp in sent:
        cp.wait_send()


SIBLING = (1,)
ICI_SAME_CORE = (2, 4, 6)
ALL_PEERS = tuple(range(1, N_DEV))


def _remote(src, dst, send_sem, recv_sem, dev):
    return pltpu.make_async_remote_copy(src_ref=src, dst_ref=dst, send_sem=send_sem, recv_sem=recv_sem,
                                        device_id=dev, device_id_type=MESH)


def _push_start(items, masks, send_sems, recv_sems):
    me, peers = _me_and_peers()
    for k, (src_for, dst_from) in enumerate(items):
        for m in masks:
            dev, pid = peers[m - 1]
            _remote(src_for(pid), dst_from(me), send_sems.at[k, m - 1], recv_sems.at[k, m - 1], dev).start()


def _push_wait_recv(items, masks, send_sems, recv_sems):
    me, peers = _me_and_peers()
    for k, (src_for, dst_from) in enumerate(items):
        for m in masks:
            dev, pid = peers[m - 1]
            _remote(src_for(pid), dst_from(pid), send_sems.at[k, m - 1], recv_sems.at[k, m - 1], dev).wait_recv()


def _push_wait_send(items, masks, send_sems, recv_sems):
    me, peers = _me_and_peers()
    for k, (src_for, dst_from) in enumerate(items):
        for m in masks:
            dev, pid = peers[m - 1]
            _remote(src_for(pid), dst_from(me), send_sems.at[k, m - 1], recv_sems.at[k, m - 1], dev).wait_send()


def _forward_start(items, send_sems, recv_sems):
    me, peers = _me_and_peers()
    sib = peers[0][0]
    for k, (blk_in, blk_out) in enumerate(items):
        for j, m in enumerate(ICI_SAME_CORE):
            pid = peers[m - 1][1]
            _remote(blk_in(pid), blk_out(pid), send_sems.at[k, j], recv_sems.at[k, j], sib).start()


def _forward_wait(items, send_sems, recv_sems):
    me, peers = _me_and_peers()
    sib = peers[0][0]
    for k, (blk_in, blk_out) in enumerate(items):
        for j, m in enumerate(ICI_SAME_CORE):
            got = peers[(m | 1) - 1][1]
            _remote(blk_in(got), blk_out(got), send_sems.at[k, j], recv_sems.at[k, j], sib).wait_recv()
    for k, (blk_in, blk_out) in enumerate(items):
        for j, m in enumerate(ICI_SAME_CORE):
            pid = peers[m - 1][1]
            _remote(blk_in(pid), blk_out(pid), send_sems.at[k, j], recv_sems.at[k, j], sib).wait_send()


def _mod_gather(c, c_ctx, w_ada, b_ada, w_in_t, w_out, w1, w2):
    B, D = c.shape
    ncol = w_ada.shape[1]
    rows = SUBLANES * N_DEV + SUBLANES

    def body(c_ref, cc_ref, w_ref, b_ref, win_ref, wout_ref, w1_ref, w2_ref,
             s_ref, m_ref, gin_ref, wout_b, w1_b, w2_b,
             win_b, msend, send1, recv1, send2, recv2, wsend, wrecv, fsend, frecv, lsem):
        me, _ = _me_and_peers()
        win_b[...] = win_ref[...].astype(BF16)
        block = _row_block(gin_ref, w_in_t.shape[0])
        gather = [(lambda p: win_b, block)]
        own = pltpu.make_async_copy(win_b, block(me), lsem.at[0])
        cv = c_ref[...]
        slot = jnp.concatenate([cv * _sigmoid(cv), jnp.zeros((SUBLANES - B, D), F32)], axis=0)
        my_rows = pl.ds(pl.multiple_of(me * SUBLANES, SUBLANES), SUBLANES)
        s_ref[my_rows, :] = slot
        ccv = cc_ref[...]
        s_ref[SUBLANES * N_DEV:, :] = jnp.concatenate(
            [ccv * _sigmoid(ccv), jnp.zeros((SUBLANES - 1, D), F32)], axis=0)

        def rows_of(p):
            return s_ref.at[pl.ds(pl.multiple_of(p * SUBLANES, SUBLANES), SUBLANES), :]

        _exchange(lambda p: rows_of(me), rows_of, send1, recv1)
        own.start()
        _push_start(gather, SIBLING + ICI_SAME_CORE, wsend, wrecv)
        wout_b[...] = wout_ref[...].astype(BF16)
        w1_b[...] = w1_ref[...].astype(BF16)
        w2_b[...] = w2_ref[...].astype(BF16)
        b_loc = b_ref[:, pl.ds(pl.multiple_of(me * ncol, ncol), ncol)]
        mods = _dot(s_ref[...], w_ref[...]) + b_loc
        for p in range(N_DEV):
            msend[p] = jnp.concatenate([mods[p * SUBLANES:(p + 1) * SUBLANES], mods[N_DEV * SUBLANES:]], axis=0)
        m_ref[me] = msend[me]
        columns = [(lambda p: msend.at[p], lambda p: m_ref.at[p])]
        _push_start(columns, ALL_PEERS, send2, recv2)
        _push_wait_recv(gather, ICI_SAME_CORE, wsend, wrecv)
        relay = [(block, block)]
        _forward_start(relay, fsend, frecv)
        _push_wait_recv(columns, ALL_PEERS, send2, recv2)
        _push_wait_recv(gather, SIBLING, wsend, wrecv)
        _forward_wait(relay, fsend, frecv)
        _push_wait_send(columns, ALL_PEERS, send2, recv2)
        _push_wait_send(gather, SIBLING + ICI_SAME_CORE, wsend, wrecv)
        own.wait()

    return pl.pallas_call(
        body, name="mod_gather",
        out_shape=(jax.ShapeDtypeStruct((rows, D), F32), jax.ShapeDtypeStruct((N_DEV, 2 * SUBLANES, ncol), F32),
                   jax.ShapeDtypeStruct((N_DEV * w_in_t.shape[0], D), BF16),
                   jax.ShapeDtypeStruct(w_out.shape, BF16), jax.ShapeDtypeStruct(w1.shape, BF16),
                   jax.ShapeDtypeStruct(w2.shape, BF16)),
        in_specs=[_vmem()] * 8, out_specs=(_vmem(), _vmem(), _any(), _vmem(), _vmem(), _vmem()),
        scratch_shapes=[pltpu.VMEM(w_in_t.shape, BF16), pltpu.VMEM((N_DEV, 2 * SUBLANES, ncol), F32)]
                       + [pltpu.SemaphoreType.DMA((N_DEV - 1,))] * 2
                       + [pltpu.SemaphoreType.DMA((1, N_DEV - 1))] * 4 + [pltpu.SemaphoreType.DMA((1, 3))] * 2
                       + [pltpu.SemaphoreType.DMA((1,))],
        compiler_params=pltpu.CompilerParams(vmem_limit_bytes=VMEM_LIMIT),
    )(c, c_ctx.reshape(1, D), w_ada, b_ada, w_in_t, w_out, w1, w2)


def _row_block(ref, rows):
    return lambda p: ref.at[pl.ds(pl.multiple_of(p * rows, 2 * SUBLANES), rows), :]


def _col_block(ref, cols):
    return lambda p: ref.at[:, pl.ds(pl.multiple_of(p * cols, LANES), cols)]


def _slot(ref):
    return lambda p: ref.at[p]


def _grid_call(body, *, name, grid, out_shape, in_specs, out_specs, scratch_shapes, args):
    return pl.pallas_call(
        body, name=name, grid=grid, out_shape=tuple(out_shape), in_specs=list(in_specs), out_specs=tuple(out_specs),
        scratch_shapes=list(scratch_shapes), compiler_params=_params(*(("arbitrary",) * len(grid))),
    )(*args)


def _token_tiles(n_ctx, tm):
    nct = n_ctx // tm

    def ctx_spec(D):
        return pl.BlockSpec((None, tm, D), lambda b, t: (b, jnp.minimum(t, nct - 1), 0))

    def lat_spec(D):
        return pl.BlockSpec((None, tm, D), lambda b, t: (b, jnp.maximum(t - nct, 0), 0))

    return nct, ctx_spec, lat_spec


def _inproj_fwd(x, ctx, modl, g1, w_in_t):
    B, N, D = x.shape
    n_ctx = ctx.shape[1]
    T = n_ctx + N
    nw = w_in_t.shape[0]
    tm = _div_tile(n_ctx, 256, 16)
    nct, ctx_spec, lat_spec = _token_tiles(n_ctx, tm)

    def body(c_ref, x_ref, sh_ref, sc_ref, g_ref, w_ref, h_ref, p_ref):
        x = jnp.where(pl.program_id(1) < nct, c_ref[...], x_ref[...])
        r = lax.rsqrt(jnp.mean(x * x, axis=-1, keepdims=True) + NORM_EPS)
        h = ((x * r) * g_ref[...]) * (1.0 + sc_ref[...]) + sh_ref[...]
        hb = h.astype(BF16)
        h_ref[...] = hb
        p_ref[...] = _dot_nt(hb, w_ref[...]).astype(BF16)

    def mrow(b, t):
        return jnp.where(t < nct, B, b)

    return _grid_call(
        body, name="inproj_fwd", grid=(B, T // tm),
        out_shape=(jax.ShapeDtypeStruct((B, T, D), BF16), jax.ShapeDtypeStruct((B, T, nw), BF16)),
        in_specs=[ctx_spec(D), lat_spec(D),
                  pl.BlockSpec((None, None, 1, D), lambda b, t: (mrow(b, t), 0, 0, 0)),
                  pl.BlockSpec((None, None, 1, D), lambda b, t: (mrow(b, t), 1, 0, 0)),
                  pl.BlockSpec((1, D), lambda b, t: (0, 0)),
                  pl.BlockSpec((nw, D), lambda b, t: (0, 0))],
        out_specs=(pl.BlockSpec((None, tm, D), lambda b, t: (b, t, 0)),
                   pl.BlockSpec((None, tm, nw), lambda b, t: (b, t, 0))),
        scratch_shapes=[], args=(ctx, x, modl, modl, g1, w_in_t))


def _swap32(x):
    lane = lax.broadcasted_iota(jnp.int32, x.shape, 1)
    return jnp.where((lane % 64) < 32, pltpu.roll(x, 96, 1), pltpu.roll(x, 32, 1))


def _rope(x, cos, sin):
    return x * cos + _swap32(x) * sin


def _unrope(dy, cos, sin):
    return dy * cos + _swap32(dy * sin)


def _ret_weights(lgf, lgb, dist):
    return jnp.exp(jnp.where(dist >= 0.0, lgf * dist, -lgb * dist))


class _RetDecay:
    def __init__(self, lgf, lgb, rows):
        r = lax.broadcasted_iota(jnp.int32, (rows, RET_DIM), 0).astype(F32)
        self.head = r + 1.0
        self.tail = (rows - 1.0) - r
        self.q_f = jnp.exp(lgf * self.head)
        self.k_f = jnp.exp(lgf * self.tail)
        self.q_b = jnp.exp(lgb * self.tail)
        self.k_b = jnp.exp(lgb * self.head)


def _ret_states(kf32, vs, lgf, lgb, C, c, nt, hf, hb, hfa=None, hba=None):
    dec = _RetDecay(lgf, lgb, c)
    dec_c = _RetDecay(lgf, lgb, C)
    step_f = jnp.exp(jnp.zeros((RET_DIM, RET_DIM), F32) + lgf * c)
    step_b = jnp.exp(jnp.zeros((RET_DIM, RET_DIM), F32) + lgb * c)

    def upd(rows, kdec):
        return _dot_tn((kf32[rows, :] * kdec).astype(BF16), vs[rows, :])

    def lat(t):
        return slice(C + t * c, C + (t + 1) * c)

    state = upd(slice(0, C), dec_c.k_f)
    aged = jnp.zeros_like(state)
    for t in range(nt):
        hf[t] = state.astype(BF16)
        if hfa is not None:
            hfa[t] = aged
        if t < nt - 1:
            aged = step_f * (aged + c * state)
            state = step_f * state + upd(lat(t), dec.k_f)
    state = upd(slice(0, C), dec_c.k_b)
    aged = jnp.zeros_like(state)
    for t in range(nt - 1, -1, -1):
        hb[t] = state.astype(BF16)
        if hba is not None:
            hba[t] = aged
        if t > 0:
            aged = step_b * (aged + c * state)
            state = step_b * state + upd(lat(t), dec.k_b)
    return dec, dec_c, step_f, step_b


def _ret_fwd(proj, cos, sin, lg, gn, n_ctx):
    B, T, _ = proj.shape
    C = n_ctx
    N = T - C
    c = _div_tile(N, 256, 16)
    nt = N // c
    scale = RET_DIM ** -0.5

    def body(lg_ref, q_ref, k_ref, v_ref, g_ref, cos_ref, sin_ref, gn_ref, o_ref, lat_ref, qs, ks, vs, kf32, hf, hb):
        h = pl.program_id(1)
        lgf = lg_ref[0, h]
        lgb = lg_ref[1, h]
        for rows in [slice(0, C)] + [slice(C + t * c, C + (t + 1) * c) for t in range(nt)]:
            cosb = cos_ref[rows, :]
            sinb = sin_ref[rows, :]
            qs[rows, :] = (_rope(q_ref[rows, :].astype(F32), cosb, sinb) * scale).astype(BF16)
            kr = _rope(k_ref[rows, :].astype(F32), cosb, sinb)
            kf32[rows, :] = kr
            ks[rows, :] = kr.astype(BF16)
            vs[rows, :] = v_ref[rows, :].astype(BF16)
        gnv = gn_ref[...]
        dec, _, _, _ = _ret_states(kf32, vs, lgf, lgb, C, c, nt, hf, hb)
        rc = (lax.broadcasted_iota(jnp.int32, (c, c), 0) - lax.broadcasted_iota(jnp.int32, (c, c), 1)).astype(F32)
        w_diag = _ret_weights(lgf, lgb, rc)
        for t in range(nt):
            rows = slice(C + t * c, C + (t + 1) * c)
            qt = qs[rows, :]
            s = _dot_nt(qt, ks[rows, :])
            o = (_dot((s * w_diag).astype(BF16), vs[rows, :])
                 + dec.q_f * _dot(qt, hf[t]) + dec.q_b * _dot(qt, hb[t]))
            o_ref[t * c:(t + 1) * c, :] = o
            mu = jnp.mean(o, axis=-1, keepdims=True)
            oc = o - mu
            var = jnp.mean(oc * oc, axis=-1, keepdims=True)
            yh = oc * lax.rsqrt(var + NORM_EPS)
            g = g_ref[rows, :].astype(F32)
            lat_ref[t * c:(t + 1) * c, :] = ((yh * gnv) * (g * _sigmoid(g))).astype(BF16)

    def col(seg):
        return pl.BlockSpec((None, T, RET_DIM), lambda b, h, seg=seg: (b, 0, seg * RET_HEADS + h))

    return _grid_call(
        body, name="ret_fwd", grid=(B, RET_HEADS),
        out_shape=(jax.ShapeDtypeStruct((B, N, RET_WIDTH), F32), jax.ShapeDtypeStruct((B, N, RET_WIDTH), BF16)),
        in_specs=[pl.BlockSpec(memory_space=pltpu.SMEM), col(0), col(1), col(2), col(3),
                  pl.BlockSpec((T, RET_DIM), lambda b, h: (0, 0)), pl.BlockSpec((T, RET_DIM), lambda b, h: (0, 0)),
                  pl.BlockSpec((1, RET_DIM), lambda b, h: (0, h))],
        out_specs=(pl.BlockSpec((None, N, RET_DIM), lambda b, h: (b, 0, h)),
                   pl.BlockSpec((None, N, RET_DIM), lambda b, h: (b, 0, h))),
        scratch_shapes=[pltpu.VMEM((T, RET_DIM), BF16)] * 3 + [pltpu.VMEM((T, RET_DIM), F32)]
                       + [pltpu.VMEM((nt, RET_DIM, RET_DIM), BF16)] * 2,
        args=(lg, proj, proj, proj, proj, cos, sin, gn))


def _ret_bwd(proj, cos, sin, lg, gn, o, dlat, n_ctx):
    B, T, _ = proj.shape
    C = n_ctx
    N = T - C
    c = _div_tile(N, 256, 16)
    nt = N // c
    scale = RET_DIM ** -0.5

    def lat(t):
        return slice(C + t * c, C + (t + 1) * c)

    def body(lg_ref, q_ref, k_ref, v_ref, g_ref, cos_ref, sin_ref, gn_ref, o_ref, dl_ref,
             d_ref, dgn_ref, dlg_ref, qs, ks, vs, dos, qf32, kf32, hf, hb, hfa, hba, gf_s, gb_s):
        h = pl.program_id(1)
        lgf = lg_ref[0, h]
        lgb = lg_ref[1, h]
        gnv = gn_ref[...]

        def fold(a):
            return jnp.sum(a.reshape(a.shape[0] // SUBLANES, SUBLANES, a.shape[1]), axis=0)

        for rows in [slice(0, C)] + [lat(t) for t in range(nt)]:
            cosb = cos_ref[rows, :]
            sinb = sin_ref[rows, :]
            qr = _rope(q_ref[rows, :].astype(F32), cosb, sinb) * scale
            qf32[rows, :] = qr
            qs[rows, :] = qr.astype(BF16)
            kr = _rope(k_ref[rows, :].astype(F32), cosb, sinb)
            kf32[rows, :] = kr
            ks[rows, :] = kr.astype(BF16)
            vs[rows, :] = v_ref[rows, :].astype(BF16)

        dgn = jnp.zeros((1, RET_DIM), F32)
        for t in range(nt):
            lrows = slice(t * c, (t + 1) * c)
            ov = o_ref[lrows, :]
            mu = jnp.mean(ov, axis=-1, keepdims=True)
            oc = ov - mu
            var = jnp.mean(oc * oc, axis=-1, keepdims=True)
            rstd = lax.rsqrt(var + NORM_EPS)
            yh = oc * rstd
            g = g_ref[lat(t), :].astype(F32)
            sg = _sigmoid(g)
            dl = dl_ref[lrows, :]
            d_ref[3, lat(t), :] = (dl * (yh * gnv) * (sg * (1.0 + g * (1.0 - sg)))).astype(BF16)
            dls = dl * (g * sg)
            dgn = dgn + jnp.sum(dls * yh, axis=0, keepdims=True)
            dyh = dls * gnv
            do = rstd * (dyh - jnp.mean(dyh, axis=-1, keepdims=True)
                         - yh * jnp.mean(dyh * yh, axis=-1, keepdims=True))
            dos[lrows, :] = do.astype(BF16)
        dgn_ref[...] = jnp.concatenate([dgn, jnp.zeros((SUBLANES - 1, RET_DIM), F32)], axis=0)
        d_ref[3, 0:C, :] = jnp.zeros((C, RET_DIM), BF16)
        d_ref[0, 0:C, :] = jnp.zeros((C, RET_DIM), BF16)

        dec, dec_c, step_f, step_b = _ret_states(kf32, vs, lgf, lgb, C, c, nt, hf, hb, hfa, hba)

        def zmat(t, qdec):
            return _dot_tn((qf32[lat(t), :] * qdec).astype(BF16), dos[t * c:(t + 1) * c, :])

        acc3f = jnp.zeros((RET_DIM, RET_DIM), F32)
        acc3b = jnp.zeros((RET_DIM, RET_DIM), F32)
        state = jnp.zeros((RET_DIM, RET_DIM), F32)
        for t in range(nt - 1, -1, -1):
            gf_s[t] = state.astype(BF16)
            z = zmat(t, dec.q_f)
            acc3f = acc3f + hfa[t] * z
            state = step_f * state + z
        gctx_f = state.astype(BF16)
        state = jnp.zeros((RET_DIM, RET_DIM), F32)
        for t in range(nt):
            gb_s[t] = state.astype(BF16)
            z = zmat(t, dec.q_b)
            acc3b = acc3b + hba[t] * z
            state = step_b * state + z
        gctx_b = state.astype(BF16)

        rc = (lax.broadcasted_iota(jnp.int32, (c, c), 0) - lax.broadcasted_iota(jnp.int32, (c, c), 1)).astype(F32)
        w_diag = _ret_weights(lgf, lgb, rc)
        wg_f = jnp.where(rc >= 0.0, w_diag * rc, 0.0)
        wg_b = jnp.where(rc < 0.0, -w_diag * rc, 0.0)
        accf = jnp.zeros((SUBLANES, RET_DIM), F32)
        accb = jnp.zeros((SUBLANES, RET_DIM), F32)
        gdf = jnp.zeros((SUBLANES, c), F32)
        gdb = jnp.zeros((SUBLANES, c), F32)
        for t in range(nt):
            rows = lat(t)
            qt = qs[rows, :]
            kt = ks[rows, :]
            vt = vs[rows, :]
            dot = dos[t * c:(t + 1) * c, :]
            s = _dot_nt(qt, kt)
            dp = _dot_nt(dot, vt)
            dv = _dot_tn((s * w_diag).astype(BF16), dot)
            ds = (dp * w_diag).astype(BF16)
            dq = _dot(ds, kt)
            dk = _dot_tn(ds, qt)
            gs = dp * s
            gdf = gdf + fold(gs * wg_f)
            gdb = gdb + fold(gs * wg_b)
            qv = qf32[rows, :]
            kv = kf32[rows, :]
            dq_f = dec.q_f * _dot_nt(dot, hf[t])
            dq_b = dec.q_b * _dot_nt(dot, hb[t])
            dk_f = dec.k_f * _dot_nt(vt, gf_s[t])
            dk_b = dec.k_b * _dot_nt(vt, gb_s[t])
            accf = accf + fold(dec.head * dq_f * qv) + fold(dec.tail * dk_f * kv)
            accb = accb + fold(dec.tail * dq_b * qv) + fold(dec.head * dk_b * kv)
            dv = dv + dec.k_f * _dot(kt, gf_s[t]) + dec.k_b * _dot(kt, gb_s[t])
            cosb = cos_ref[rows, :]
            sinb = sin_ref[rows, :]
            d_ref[0, rows, :] = _unrope((dq + dq_f + dq_b) * scale, cosb, sinb).astype(BF16)
            d_ref[1, rows, :] = _unrope(dk + dk_f + dk_b, cosb, sinb).astype(BF16)
            d_ref[2, rows, :] = dv.astype(BF16)
        kc = ks[0:C, :]
        vc = vs[0:C, :]
        kcv = kf32[0:C, :]
        dkc_f = dec_c.k_f * _dot_nt(vc, gctx_f)
        dkc_b = dec_c.k_b * _dot_nt(vc, gctx_b)
        accf = accf + fold(dec_c.tail * dkc_f * kcv)
        accb = accb + fold(dec_c.head * dkc_b * kcv)
        d_ref[1, 0:C, :] = (dkc_f + dkc_b).astype(BF16)
        d_ref[2, 0:C, :] = (dec_c.k_f * _dot(kc, gctx_f) + dec_c.k_b * _dot(kc, gctx_b)).astype(BF16)
        gf = jnp.sum(gdf) + jnp.sum(accf) + jnp.sum(acc3f)
        gb = jnp.sum(gdb) + jnp.sum(accb) + jnp.sum(acc3b)
        row = lax.broadcasted_iota(jnp.int32, (SUBLANES, LANES), 0)
        dlg_ref[...] = jnp.where(row == 0, gf, jnp.where(row == 1, gb, 0.0))

    def col(seg):
        return pl.BlockSpec((None, T, RET_DIM), lambda b, h, seg=seg: (b, 0, seg * RET_HEADS + h))

    return _grid_call(
        body, name="ret_bwd", grid=(B, RET_HEADS),
        out_shape=(jax.ShapeDtypeStruct((B, 4, T, RET_WIDTH), BF16),
                   jax.ShapeDtypeStruct((B, SUBLANES, RET_WIDTH), F32),
                   jax.ShapeDtypeStruct((B, RET_HEADS, SUBLANES, LANES), F32)),
        in_specs=[pl.BlockSpec(memory_space=pltpu.SMEM), col(0), col(1), col(2), col(3),
                  pl.BlockSpec((T, RET_DIM), lambda b, h: (0, 0)), pl.BlockSpec((T, RET_DIM), lambda b, h: (0, 0)),
                  pl.BlockSpec((1, RET_DIM), lambda b, h: (0, h)),
                  pl.BlockSpec((None, N, RET_DIM), lambda b, h: (b, 0, h)),
                  pl.BlockSpec((None, N, RET_DIM), lambda b, h: (b, 0, h))],
        out_specs=(pl.BlockSpec((None, 4, T, RET_DIM), lambda b, h: (b, 0, 0, h)),
                   pl.BlockSpec((None, SUBLANES, RET_DIM), lambda b, h: (b, 0, h)),
                   pl.BlockSpec((None, None, SUBLANES, LANES), lambda b, h: (b, h, 0, 0))),
        scratch_shapes=[pltpu.VMEM((T, RET_DIM), BF16)] * 3 + [pltpu.VMEM((N, RET_DIM), BF16)]
                       + [pltpu.VMEM((T, RET_DIM), F32)] * 2
                       + [pltpu.VMEM((nt, RET_DIM, RET_DIM), BF16)] * 2 + [pltpu.VMEM((nt, RET_DIM, RET_DIM), F32)] * 2
                       + [pltpu.VMEM((nt, RET_DIM, RET_DIM), BF16)] * 2,
        args=(lg, proj, proj, proj, proj, cos, sin, gn, o, dlat))


def _na_geometry(rows):
    kh = min(NA_KH, rows)
    return kh, kh * GRID_W


def _pair_select():
    lane = lax.broadcasted_iota(jnp.int32, (2 * GRID_W, LANES), 1)
    row = lax.broadcasted_iota(jnp.int32, (2 * GRID_W, LANES), 0)
    return (lane >= NA_DIM) == (row >= GRID_W)


def _pair_bias(bias_ref, dr0, kh):
    return jnp.concatenate(
        [jnp.concatenate([bias_ref[e, pl.ds(dr0 + 2 * m, 1)].reshape(GRID_W, LANES) for m in range(kh // 2)], axis=1)
         for e in range(2)], axis=0)


def _na_softmax(s_loc, s_ctx):
    mx = jnp.maximum(jnp.max(s_loc, axis=-1, keepdims=True), jnp.max(s_ctx, axis=-1, keepdims=True))
    p_loc = jnp.exp(s_loc - mx)
    p_ctx = jnp.exp(s_ctx - mx)
    den = jnp.sum(p_loc, axis=-1, keepdims=True) + jnp.sum(p_ctx, axis=-1, keepdims=True)
    return p_loc, p_ctx, den


def _na_fwd(proj, bias2, n_ctx):
    B, T, _ = proj.shape
    C = n_ctx
    N = T - C
    R = N // GRID_W
    kh, nk = _na_geometry(R)
    scale = NA_DIM ** -0.5
    base = (4 * RET_WIDTH) // LANES

    def body(q_ref, k_ref, v_ref, bias_ref, out_ref, kb16, vb16):
        kb16[...] = k_ref[...].astype(BF16)
        vb16[...] = v_ref[...].astype(BF16)
        kc = kb16[0:C, :]
        vc = vb16[0:C, :]
        lane = lax.broadcasted_iota(jnp.int32, (GRID_W, LANES), 1)
        sel2 = _pair_select()

        def group(gi, carry):
            pre = []
            for u in range(NA_GROUP):
                r = gi * NA_GROUP + u
                bs = jnp.clip(r - kh // 2, 0, R - kh)
                dr0 = bs - r + (NA_KH - 1)
                q = q_ref[pl.ds(pl.multiple_of(C + r * GRID_W, GRID_W), GRID_W), :].astype(F32) * scale
                q2 = jnp.where(sel2, jnp.concatenate([q, q], axis=0), 0.0).astype(BF16)
                band = pl.ds(pl.multiple_of(C + bs * GRID_W, GRID_W), nk)
                s_loc = _dot_nt(q2, kb16[band, :]) + _pair_bias(bias_ref, dr0, kh)
                s_ctx = _dot_nt(q2, kc)
                pre.append((r, band, s_loc, s_ctx))
            mid = [(r, band) + _na_softmax(s_loc, s_ctx) for r, band, s_loc, s_ctx in pre]
            for r, band, p_loc, p_ctx, den in mid:
                o2 = (_dot(p_loc.astype(BF16), vb16[band, :]) + _dot(p_ctx.astype(BF16), vc)) / den
                out_ref[pl.ds(pl.multiple_of(r * GRID_W, GRID_W), GRID_W), :] = jnp.where(
                    lane < NA_DIM, o2[:GRID_W], o2[GRID_W:]).astype(BF16)
            return carry

        lax.fori_loop(0, R // NA_GROUP, group, 0)

    def col(seg):
        return pl.BlockSpec((None, T, LANES), lambda b, p, seg=seg: (b, 0, base + seg * NA_PAIRS + p))

    return _grid_call(
        body, name="na_fwd", grid=(B, NA_PAIRS),
        out_shape=(jax.ShapeDtypeStruct((B, N, NA_WIDTH), BF16),),
        in_specs=[col(0), col(1), col(2),
                  pl.BlockSpec((2, 2 * NA_KH - 2, GRID_W, LANES), lambda b, p: (p, 0, 0, 0))],
        out_specs=(pl.BlockSpec((None, N, LANES), lambda b, p: (b, 0, p)),),
        scratch_shapes=[pltpu.VMEM((T, LANES), BF16)] * 2,
        args=(proj, proj, proj, bias2))


def _na_bwd(proj, bias2, dlat, n_ctx):
    B, T, _ = proj.shape
    C = n_ctx
    N = T - C
    R = N // GRID_W
    kh, nk = _na_geometry(R)
    scale = NA_DIM ** -0.5
    base = (4 * RET_WIDTH) // LANES

    def body(q_ref, k_ref, v_ref, bias_ref, dl_ref, d_ref, db_ref, kb16, vb16, dkv):
        b = pl.program_id(1)
        kb16[...] = k_ref[...].astype(BF16)
        vb16[...] = v_ref[...].astype(BF16)
        kc = kb16[0:C, :]
        vc = vb16[0:C, :]
        lane = lax.broadcasted_iota(jnp.int32, (GRID_W, LANES), 1)
        dkv[...] = jnp.zeros(dkv.shape, F32)
        d_ref[0, 0:C, :] = jnp.zeros((C, LANES), BF16)

        @pl.when(b == 0)
        def _():
            db_ref[...] = jnp.zeros(db_ref.shape, F32)

        sel2 = _pair_select()

        def group(gi, carry):
            pre = []
            for u in range(NA_GROUP):
                r = gi * NA_GROUP + u
                bs = jnp.clip(r - kh // 2, 0, R - kh)
                dr0 = bs - r + (NA_KH - 1)
                q = q_ref[pl.ds(pl.multiple_of(C + r * GRID_W, GRID_W), GRID_W), :].astype(F32) * scale
                do = dl_ref[pl.ds(pl.multiple_of(r * GRID_W, GRID_W), GRID_W), :]
                q2 = jnp.where(sel2, jnp.concatenate([q, q], axis=0), 0.0).astype(BF16)
                do2 = jnp.where(sel2, jnp.concatenate([do, do], axis=0), 0.0).astype(BF16)
                band = pl.ds(pl.multiple_of(C + bs * GRID_W, GRID_W), nk)
                s_loc = _dot_nt(q2, kb16[band, :]) + _pair_bias(bias_ref, dr0, kh)
                s_ctx = _dot_nt(q2, kc)
                dp_loc = _dot_nt(do2, vb16[band, :])
                dp_ctx = _dot_nt(do2, vc)
                pre.append((r, dr0, band, q2, do2, s_loc, s_ctx, dp_loc, dp_ctx))
            mid = []
            for r, dr0, band, q2, do2, s_loc, s_ctx, dp_loc, dp_ctx in pre:
                p_loc, p_ctx, den = _na_softmax(s_loc, s_ctx)
                inv = 1.0 / den
                p_loc = p_loc * inv
                p_ctx = p_ctx * inv
                delta = (jnp.sum(p_loc * dp_loc, axis=-1, keepdims=True)
                         + jnp.sum(p_ctx * dp_ctx, axis=-1, keepdims=True))
                ds_loc = p_loc * (dp_loc - delta)
                ds_ctx = p_ctx * (dp_ctx - delta)
                mid.append((r, dr0, band, q2, do2, p_loc.astype(BF16), p_ctx.astype(BF16), ds_loc, ds_ctx))
            for r, dr0, band, q2, do2, pb_loc, pb_ctx, ds_loc, ds_ctx in mid:
                dsb_loc = ds_loc.astype(BF16)
                dsb_ctx = ds_ctx.astype(BF16)
                dq2 = _dot(dsb_loc, kb16[band, :]) + _dot(dsb_ctx, kc)
                d_ref[0, pl.ds(pl.multiple_of(C + r * GRID_W, GRID_W), GRID_W), :] = (jnp.where(
                    lane < NA_DIM, dq2[:GRID_W], dq2[GRID_W:]) * scale).astype(BF16)
                dkv[0, band, :] += _dot_tn(dsb_loc, q2)
                dkv[1, band, :] += _dot_tn(pb_loc, do2)
                dkv[0, 0:C, :] += _dot_tn(dsb_ctx, q2)
                dkv[1, 0:C, :] += _dot_tn(pb_ctx, do2)
                for e in range(2):
                    for m in range(kh // 2):
                        db_ref[e, pl.ds(dr0 + 2 * m, 1)] += ds_loc[e * GRID_W:(e + 1) * GRID_W,
                                                                   m * LANES:(m + 1) * LANES].reshape(1, GRID_W, LANES)
            return carry

        lax.fori_loop(0, R // NA_GROUP, group, 0)
        d_ref[1] = dkv[0].astype(BF16)
        d_ref[2] = dkv[1].astype(BF16)

    def col(seg):
        return pl.BlockSpec((None, T, LANES), lambda p, b, seg=seg: (b, 0, base + seg * NA_PAIRS + p))

    return _grid_call(
        body, name="na_bwd", grid=(NA_PAIRS, B),
        out_shape=(jax.ShapeDtypeStruct((B, 3, T, NA_WIDTH), BF16),
                   jax.ShapeDtypeStruct((NA_HEADS, 2 * NA_KH - 2, GRID_W, LANES), F32)),
        in_specs=[col(0), col(1), col(2),
                  pl.BlockSpec((2, 2 * NA_KH - 2, GRID_W, LANES), lambda p, b: (p, 0, 0, 0)),
                  pl.BlockSpec((None, N, LANES), lambda p, b: (b, 0, p))],
        out_specs=(pl.BlockSpec((None, 3, T, LANES), lambda p, b: (b, 0, 0, p)),
                   pl.BlockSpec((2, 2 * NA_KH - 2, GRID_W, LANES), lambda p, b: (p, 0, 0, 0))),
        scratch_shapes=[pltpu.VMEM((T, LANES), BF16)] * 2 + [pltpu.VMEM((2, T, LANES), F32)],
        args=(proj, proj, proj, bias2, dlat))


def _split3(a):
    hi = a.astype(BF16)
    r1 = a - hi.astype(F32)
    mid = r1.astype(BF16)
    lo = (r1 - mid.astype(F32)).astype(BF16)
    return hi, mid, lo


def _rpb_reduce(dbias2, onehot2):
    rows = dbias2.shape[0] * dbias2.shape[1]
    flat = dbias2.reshape(rows, GRID_W * LANES)

    def body(a_ref, oh_ref, o_ref):
        hi, mid, lo = _split3(a_ref[...])
        oh = oh_ref[...]
        o_ref[...] = _dot(hi, oh) + _dot(mid, oh) + _dot(lo, oh)

    return pl.pallas_call(
        body, name="rpb_reduce", out_shape=jax.ShapeDtypeStruct((rows, LANES), F32),
        in_specs=[_vmem(), _vmem()], out_specs=_vmem(),
        compiler_params=pltpu.CompilerParams(vmem_limit_bytes=VMEM_LIMIT),
    )(flat, onehot2)


def _dense_core(lat_ret, lat_na, x, tgt, modl, g_post_mix, g_pre_mlp, g_post_mlp, w_out, w1, w2):
    B, N, D = x.shape
    F = w1.shape[1]
    wout_rows, w1_cols, w2_rows = w_out.shape[0] // N_DEV, w1.shape[1] // N_DEV, w2.shape[0] // N_DEV
    mixw = w_out.shape[0]
    half = mixw // 2
    tm = _div_tile(N, 256, 16)
    nt = N // tm
    fc = _div_tile(F, 1024, LANES)

    def body(lr_ref, ln_ref, x_ref, t_ref, gt1_ref, sh2_ref, sc2_ref, gt2_ref, gpm_ref, gpre_ref, gpo_ref,
             wout_part, w1_part, w2_part,
             dy1_ref, dlr_ref, dln_ref, dmix_ref, h2_ref, a_ref, du_ref, dz_ref, red_ref, wout_hbm, w1_hbm, w2_hbm,
             wout_v, w1_v, w2_v, u_s, sems, fsend, frecv):
        @pl.when((pl.program_id(0) == 0) & (pl.program_id(1) == 0))
        def _():
            relay = [(_row_block(wout_part, wout_rows), _row_block(wout_hbm, wout_rows)),
                     (_col_block(w1_part, w1_cols), _col_block(w1_hbm, w1_cols)),
                     (_row_block(w2_part, w2_rows), _row_block(w2_hbm, w2_rows))]
            _forward_start(relay, fsend, frecv)
            _forward_wait(relay, fsend, frecv)
            cps = [pltpu.make_async_copy(wout_hbm, wout_v, sems.at[0]),
                   pltpu.make_async_copy(w1_hbm, w1_v, sems.at[1]),
                   pltpu.make_async_copy(w2_hbm, w2_v, sems.at[2])]
            for cp in cps:
                cp.start()
            for cp in cps:
                cp.wait()

        @pl.when(pl.program_id(1) == 0)
        def _():
            red_ref[...] = jnp.zeros(red_ref.shape, F32)

        gt1 = gt1_ref[...]
        sh2 = sh2_ref[...]
        sc2 = sc2_ref[...]
        gt2 = gt2_ref[...]
        gpm = gpm_ref[...]
        gpre = gpre_ref[...]
        gpo = gpo_ref[...]

        def rowmean(a):
            return jnp.mean(a, axis=-1, keepdims=True)

        def colsum(a):
            return jnp.sum(a, axis=0, keepdims=True)

        mix_gain = gt1 * gpm
        mlp_in_gain = gpre * (1.0 + sc2)
        mlp_out_gain = gt2 * gpo
        mix = _dot(lr_ref[...], wout_v[0:half, :]) + _dot(ln_ref[...], wout_v[half:, :])
        x = x_ref[...]
        rm = lax.rsqrt(rowmean(mix * mix) + NORM_EPS)
        mh = mix * rm
        y1 = x + mh * mix_gain
        r1 = lax.rsqrt(rowmean(y1 * y1) + NORM_EPS)
        xh = y1 * r1
        h2b = (xh * mlp_in_gain + sh2).astype(BF16)
        h2_ref[...] = h2b
        z = jnp.zeros((tm, D), F32)
        for c0 in range(0, F, fc):
            u = _dot(h2b, w1_v[:, c0:c0 + fc])
            u_s[:, c0:c0 + fc] = u
            ru = jnp.maximum(u, 0.0)
            ab = (ru * ru).astype(BF16)
            a_ref[:, c0:c0 + fc] = ab
            z = z + _dot(ab, w2_v[c0:c0 + fc, :])
        r2 = lax.rsqrt(rowmean(z * z) + NORM_EPS)
        zh = z * r2
        y2 = y1 + zh * mlp_out_gain
        err = y2 - t_ref[...]
        loss = 0.5 * jnp.sum(rowmean(err * err))
        dy2 = err * (1.0 / D)
        s_out = colsum(dy2 * zh)
        red_ref[2:3, :] += s_out * gpo
        red_ref[6:7, :] += s_out * gt2
        dzh = dy2 * mlp_out_gain
        dz = r2 * (dzh - zh * rowmean(dzh * zh))
        dzb = dz.astype(BF16)
        dz_ref[...] = dzb
        dh2 = jnp.zeros((tm, D), F32)
        for c0 in range(0, F, fc):
            da = _dot_nt(dzb, w2_v[c0:c0 + fc, :])
            dub = (da * (2.0 * jnp.maximum(u_s[:, c0:c0 + fc], 0.0))).astype(BF16)
            du_ref[:, c0:c0 + fc] = dub
            dh2 = dh2 + _dot_nt(dub, w1_v[:, c0:c0 + fc])
        s_in = colsum(dh2 * xh)
        red_ref[3:4, :] += s_in * gpre
        red_ref[4:5, :] += colsum(dh2)
        red_ref[5:6, :] += s_in * (1.0 + sc2)
        dxh = dh2 * mlp_in_gain
        dy1 = dy2 + r1 * (dxh - xh * rowmean(dxh * xh))
        dy1_ref[...] = dy1
        s_mix = colsum(dy1 * mh)
        red_ref[0:1, :] += s_mix * gpm
        red_ref[1:2, :] += s_mix * gt1
        dmh = dy1 * mix_gain
        dmix = (rm *(dmh - mh * rowmean(dmh * mh))).astype(BF16)
        dmix_ref[...] = dmix
        dlr_ref[...] = _dot_nt(dmix, wout_v[0:half, :])
        dln_ref[...] = _dot_nt(dmix, wout_v[half:, :])
        red_ref[7:8, :] += jnp.zeros((1, D), F32) + loss

    def tok(w):
        return pl.BlockSpec((None, tm, w), lambda b, t: (b, t, 0))

    def mod(k):
        return pl.BlockSpec((None, None, 1, D), lambda b, t, k=k: (b, k, 0, 0))

    def vec():
        return pl.BlockSpec((1, D), lambda b, t: (0, 0))

    return pl.pallas_call(
        body, name="dense_core", grid=(B, nt),
        out_shape=(jax.ShapeDtypeStruct((B, N, D), F32), jax.ShapeDtypeStruct((B, N, half), F32),
                   jax.ShapeDtypeStruct((B, N, half), F32), jax.ShapeDtypeStruct((B, N, D), BF16),
                   jax.ShapeDtypeStruct((B, N, D), BF16), jax.ShapeDtypeStruct((B, N, F), BF16),
                   jax.ShapeDtypeStruct((B, N, F), BF16), jax.ShapeDtypeStruct((B, N, D), BF16),
                   jax.ShapeDtypeStruct((B, SUBLANES, D), F32),
                   jax.ShapeDtypeStruct(w_out.shape, w_out.dtype), jax.ShapeDtypeStruct(w1.shape, w1.dtype),
                   jax.ShapeDtypeStruct(w2.shape, w2.dtype)),
        in_specs=[tok(half), tok(half), tok(D), tok(D), mod(2), mod(3), mod(4), mod(5), vec(), vec(), vec(),
                  _any(), _any(), _any()],
        out_specs=(tok(D), tok(half), tok(half), tok(D), tok(D), tok(F), tok(F), tok(D),
                   pl.BlockSpec((None, SUBLANES, D), lambda b, t: (b, 0, 0)), _any(), _any(), _any()),
        scratch_shapes=[pltpu.VMEM((mixw, D), BF16), pltpu.VMEM((D, F), BF16), pltpu.VMEM((F, D), BF16),
                        pltpu.VMEM((tm, F), F32), pltpu.SemaphoreType.DMA((3,)),
                        pltpu.SemaphoreType.DMA((3, 3)), pltpu.SemaphoreType.DMA((3, 3))],
        input_output_aliases={11: 9, 12: 10, 13: 11},
        compiler_params=_params("arbitrary", "arbitrary"),
    )(lat_ret, lat_na, x, tgt, modl, modl, modl, modl, g_post_mix, g_pre_mlp, g_post_mlp, w_out, w1, w2)[:9]


def _inproj_bwd(dret, dna, x, ctx, dy1, modl, g1, w_in_t):
    B, N, D = x.shape
    n_ctx = ctx.shape[1]
    T = n_ctx + N
    tm = _div_tile(n_ctx, 256, 16)
    nct, ctx_spec, lat_spec = _token_tiles(n_ctx, tm)
    nt = T // tm
    nseg_r = dret.shape[1]
    nseg_n = dna.shape[1]
    nw = w_in_t.shape[0]

    def body(*refs):
        seg_refs = refs[:nseg_r + nseg_n]
        c_ref, x_ref, dy1_ref, sc_ref, g_ref, w_ref, dx_ref, red_ref = refs[nseg_r + nseg_n:]
        t = pl.program_id(1)
        dh = jnp.zeros((tm, D), F32)
        for s, ref in enumerate(seg_refs):
            dh = dh + _dot(ref[...], w_ref[s * SEG:(s + 1) * SEG, :])
        x = jnp.where(t < nct, c_ref[...], x_ref[...])
        g = g_ref[...]
        r = lax.rsqrt(jnp.mean(x * x, axis=-1, keepdims=True) + NORM_EPS)
        xh = x * r
        red_ref[0:1, :] = jnp.sum(dh, axis=0, keepdims=True)
        red_ref[1:2, :] = jnp.sum(dh * (xh * g), axis=0, keepdims=True)
        dn = dh * (1.0 + sc_ref[...])
        red_ref[2:3, :] = jnp.sum(dn * xh, axis=0, keepdims=True)
        red_ref[3:, :] = jnp.zeros((SUBLANES - 3, D), F32)
        dxh = dn * g
        dx = r * (dxh - xh * jnp.mean(dxh * xh, axis=-1, keepdims=True))
        dx_ref[...] = dx + jnp.where(t >= nct, dy1_ref[...], 0.0)

    def mrow(b, t):
        return jnp.where(t < nct, B, b)

    def seg(s):
        return pl.BlockSpec((None, None, tm, SEG), lambda b, t, s=s: (b, s, t, 0))

    return _grid_call(
        body, name="inproj_bwd", grid=(B, nt),
        out_shape=(jax.ShapeDtypeStruct((B, N, D), F32), jax.ShapeDtypeStruct((B, nt, SUBLANES, D), F32)),
        in_specs=[seg(s) for s in range(nseg_r)] + [seg(s) for s in range(nseg_n)]
                 + [ctx_spec(D), lat_spec(D), lat_spec(D),
                    pl.BlockSpec((None, None, 1, D), lambda b, t: (mrow(b, t), 1, 0, 0)),
                    pl.BlockSpec((1, D), lambda b, t: (0, 0)),
                    pl.BlockSpec((nw, D), lambda b, t: (0, 0))],
        out_specs=(lat_spec(D), pl.BlockSpec((None, None, SUBLANES, D), lambda b, t: (b, t, 0, 0))),
        scratch_shapes=[], args=(*([dret] * nseg_r), *([dna] * nseg_n), ctx, x, dy1, modl, g1, w_in_t))


def _tn_matmul(lhs, rhs, name, rows_before=0, rows_after=0, into=None):
    B, S, T, W = lhs.shape
    nn = rhs.shape[-1]
    tk = _div_tile(T, 2304, LANES)
    bm = _div_tile(W, 1024, LANES)
    bn = _div_tile(nn, 1024, LANES)
    nkt = T // tk
    nk = B * nkt

    def body(l_ref, r_ref, *rest):
        o_ref, acc = rest[-2:]
        k = pl.program_id(3)

        @pl.when(k == 0)
        def _():
            acc[...] = jnp.zeros(acc.shape, F32)

        acc[...] += _dot_tn(l_ref[...].astype(BF16), r_ref[...].astype(BF16))

        @pl.when(k == nk - 1)
        def _():
            o_ref[...] = acc[...].astype(BF16)

    nwb = W // bm
    first = rows_before // bm
    return pl.pallas_call(
        functools.partial(body), name=name, grid=(S, nwb, nn // bn, nk),
        out_shape=jax.ShapeDtypeStruct((rows_before + S * W + rows_after, nn), BF16),
        in_specs=[pl.BlockSpec((None, None, tk, bm), lambda s, i, j, k: (k // nkt, s, k % nkt, i)),
                  pl.BlockSpec((None, tk, bn), lambda s, i, j, k: (k // nkt, k % nkt, j))]
                 + ([] if into is None else [_any()]),
        out_specs=pl.BlockSpec((bm, bn), lambda s, i, j, k: (first + s * nwb + i, j)),
        scratch_shapes=[pltpu.VMEM((bm, bn), F32)],
        input_output_aliases={} if into is None else {2: 0},
        compiler_params=_params("parallel", "parallel", "parallel", "arbitrary"),
    )(lhs, rhs, *([] if into is None else [into]))


class _SplitScatter:
    def __init__(self, gs, block_ofs, land_shapes, name, kind="scatter", masks=ALL_PEERS):
        self.n = n = len(gs)
        self.block_ofs, self.kind, self.masks = block_ofs, kind, masks
        if kind == "scatter":
            land_shapes = [(N_DEV,) + tuple(bs) for bs in land_shapes]
        hbm = pl.BlockSpec(memory_space=pltpu.HBM)
        sem = pl.BlockSpec(memory_space=pltpu.SEMAPHORE)

        def body(*refs):
            g_refs, land_refs = refs[:n], refs[n:2 * n]
            send_sems, recv_sems, own_sems = refs[2 * n:2 * n + 3]
            token = refs[-1]
            for own, pushes in self._copies(g_refs, land_refs, send_sems, recv_sems, own_sems, landing="sender"):
                own.start()
                for cp in pushes:
                    cp.start()
            token[...] = jnp.zeros_like(token)

        outs = pl.pallas_call(
            body, name=name,
            out_shape=(pltpu.SemaphoreType.DMA((n * (N_DEV - 1),)), pltpu.SemaphoreType.DMA((n * (N_DEV - 1),)),
                       pltpu.SemaphoreType.DMA((n,)))
                      + tuple(pltpu.HBM(g.shape, g.dtype) for g in gs)
                      + tuple(pltpu.HBM(s, g.dtype) for s, g in zip(land_shapes, gs))
                      + (jax.ShapeDtypeStruct((SUBLANES, LANES), F32),),
            in_specs=(hbm,) * (2 * n), out_specs=(sem,) * 3 + (hbm,) * (2 * n) + (_vmem(),),
            input_output_aliases={k: 3 + k for k in range(2 * n)},
            compiler_params=pltpu.CompilerParams(has_side_effects=pltpu.SideEffectType.DATAFLOW_SIDE_EFFECTING),
        )(*[pltpu.with_memory_space_constraint(g, pltpu.HBM) for g in gs],
          *[pltpu.with_memory_space_constraint(lax.empty(s, g.dtype), pltpu.HBM) for s, g in zip(land_shapes, gs)])
        self.sems, self.thru, self.token = outs[:3], outs[3:3 + 2 * n], outs[-1]

    def _copies(self, g_refs, land_refs, send_sems, recv_sems, own_sems, landing):
        me, peers = _me_and_peers()
        out = []
        for k in range(self.n):
            if self.kind == "scatter":
                src, dst = self.block_ofs[k](g_refs[k]), _slot(land_refs[k])
            else:
                src, dst = (lambda p, k=k: g_refs[k]), self.block_ofs[k](land_refs[k])
            own = pltpu.make_async_copy(src(me), dst(me), own_sems.at[k])
            pushes = []
            for m in self.masks:
                dev, pid = peers[m - 1]
                i = k * (N_DEV - 1) + m - 1
                pushes.append(_remote(src(pid), dst(me if landing == "sender" else pid),
                                      send_sems.at[i], recv_sems.at[i], dev))
            out.append((own, pushes))
        return out


def _scatter_wait(scatters, after, name):
    hbm = pl.BlockSpec(memory_space=pltpu.HBM)
    sem = pl.BlockSpec(memory_space=pltpu.SEMAPHORE)
    n_arr = [2 * sc.n for sc in scatters]
    total = sum(n_arr)

    def body(*refs):
        arrs, sems = refs[:total], refs[total:total + 3 * len(scatters)]
        a0 = 0
        for j, sc in enumerate(scatters):
            g_refs, land_refs = arrs[a0:a0 + sc.n], arrs[a0 + sc.n:a0 + 2 * sc.n]
            a0 += 2 * sc.n
            send_sems, recv_sems, own_sems = sems[3 * j:3 * j + 3]
            for (own, sent), (_, got) in zip(sc._copies(g_refs, land_refs, send_sems, recv_sems, own_sems, "sender"),
                                             sc._copies(g_refs, land_refs, send_sems, recv_sems, own_sems, "receiver")):
                own.wait()
                for cp in sent:
                    cp.wait_send()
                for cp in got:
                    cp.wait_recv()

    operands = [a for sc in scatters for a in sc.thru]
    outs = pl.pallas_call(
        body, name=name,
        out_shape=tuple(pltpu.HBM(a.shape, a.dtype) for a in operands),
        in_specs=(hbm,) * total + (sem,) * (3 * len(scatters)) + (pl.BlockSpec(memory_space=pl.ANY),),
        out_specs=(hbm,) * total, input_output_aliases={k: k for k in range(total)},
        compiler_params=pltpu.CompilerParams(has_side_effects=pltpu.SideEffectType.DATAFLOW_SIDE_EFFECTING),
    )(*operands, *[s for sc in scatters for s in sc.sems], after)
    lands, a0 = [], 0
    for sc in scatters:
        lands.extend(outs[a0 + sc.n:a0 + 2 * sc.n])
        a0 += 2 * sc.n
    return lands


def _small_ar(vbuf, mbuf, silu_all, w_ada, c_ctx):
    rv = vbuf.shape[1]
    D = silu_all.shape[1]
    ncol = w_ada.shape[1]
    nm = mbuf.shape[2]
    srows = silu_all.shape[0]

    def body(vbuf, mbuf, s_ref, w_ref, cc_ref, tot_ref, gb_ref, gw_ref, gc_ref, tbuf, dmx, cmrow, send3, recv3):
        me, _ = _me_and_peers()
        tot = vbuf[0]
        msum = mbuf[0]
        for k in range(1, N_DEV):
            tot = tot + vbuf[k]
            msum = msum + mbuf[k]
        tot_ref[...] = tot
        gb_ref[...] = jnp.sum(msum, axis=0, keepdims=True)
        loc = pl.ds(pl.multiple_of(me * ncol, ncol), ncol)
        for k in range(N_DEV):
            dmx[k * SUBLANES:(k + 1) * SUBLANES, :] = mbuf[k, :, loc]
        cmrow[...] = msum
        cm_loc = cmrow[2:3, loc]
        dmx[N_DEV * SUBLANES:, :] = jnp.concatenate([cm_loc, jnp.zeros((SUBLANES - 1, ncol), F32)], axis=0)
        gw_ref[...] = _dot_tn(s_ref[...], dmx[...])
        tbuf[me] = _dot_nt(dmx[N_DEV * SUBLANES:, :], w_ref[...])
        _exchange(lambda p: tbuf.at[me], lambda p: tbuf.at[p], send3, recv3)
        tsum = tbuf[0]
        for k in range(1, N_DEV):
            tsum = tsum + tbuf[k]
        cc = cc_ref[...]
        sg = _sigmoid(cc)
        gc_ref[...] = tsum[0:1, :] * (sg * (1.0 + cc * (1.0 - sg)))

    return pl.pallas_call(
        body, name="small_ar",
        out_shape=(jax.ShapeDtypeStruct((rv, LANES), F32), jax.ShapeDtypeStruct((1, nm), F32),
                   jax.ShapeDtypeStruct((D, ncol), F32), jax.ShapeDtypeStruct((1, D), F32)),
        in_specs=[_vmem()] * 5, out_specs=(_vmem(),) * 4,
        scratch_shapes=[pltpu.VMEM((N_DEV, SUBLANES, D), F32), pltpu.VMEM((srows, ncol), F32),
                        pltpu.VMEM((SUBLANES, nm), F32)] + [pltpu.SemaphoreType.DMA((N_DEV - 1,))] * 2,
        compiler_params=pltpu.CompilerParams(vmem_limit_bytes=VMEM_LIMIT),
    )(vbuf, mbuf, silu_all, w_ada, c_ctx.reshape(1, D))


def _adam_update(w, g, m, v):
    mn = ADAM_B1 * m + (1.0 - ADAM_B1) * g
    vn = ADAM_B2 * v + (1.0 - ADAM_B2) * (g * g)
    m_hat = mn / (1.0 - ADAM_B1 ** ADAM_STEP)
    v_hat = vn / (1.0 - ADAM_B2 ** ADAM_STEP)
    return -ADAM_LR * (m_hat / (jnp.sqrt(v_hat) + ADAM_EPS) + ADAM_WD * w), mn, vn


def _adamw(w, g, m, v, name):
    rows, cols = w.shape
    tr = _div_tile(rows, 256, SUBLANES) if rows * cols > 65536 else rows

    def body(w_ref, g_ref, m_ref, v_ref, d_ref, nm_ref, nv_ref):
        d_ref[...], nm_ref[...], nv_ref[...] = _adam_update(w_ref[...], g_ref[...], m_ref[...], v_ref[...])

    spec = pl.BlockSpec((tr, cols), lambda i: (i, 0))
    return pl.pallas_call(
        functools.partial(body), name=name, grid=(rows // tr,),
        out_shape=(jax.ShapeDtypeStruct((rows, cols), F32),) * 3,
        in_specs=[spec] * 4, out_specs=(spec,) * 3,
        compiler_params=_params("parallel"),
    )(w, g, m, v)


def _adamw_small(items, name):
    n = len(items)

    def body(*refs):
        ins, outs = refs[:4 * n], refs[4 * n:]
        for i in range(n):
            w_ref, g_ref, m_ref, v_ref = ins[4 * i:4 * i + 4]
            outs[3 * i][...], outs[3 * i + 1][...], outs[3 * i + 2][...] = _adam_update(
                w_ref[...], g_ref[...], m_ref[...], v_ref[...])

    outs = pl.pallas_call(
        body, name=name,
        out_shape=tuple(jax.ShapeDtypeStruct(it[0].shape, F32) for it in items for _ in range(3)),
        in_specs=[_vmem()] * (4 * n), out_specs=(_vmem(),) * (3 * n),
        compiler_params=pltpu.CompilerParams(vmem_limit_bytes=VMEM_LIMIT),
    )(*[a for it in items for a in it])
    return [tuple(outs[3 * i:3 * i + 3]) for i in range(n)]


def _sum_adamw(buf, w, m, v, name):
    _, rows, cols = buf.shape
    tr = _div_tile(rows, 256, 2 * SUBLANES)

    def body(b_ref, w_ref, m_ref, v_ref, g_ref, d_ref, nm_ref, nv_ref):
        g = b_ref[0].astype(F32)
        for k in range(1, N_DEV):
            g = g + b_ref[k].astype(F32)
        g_ref[...] = g
        d_ref[...], nm_ref[...], nv_ref[...] = _adam_update(w_ref[...], g, m_ref[...], v_ref[...])

    spec = pl.BlockSpec((tr, cols), lambda i: (i, 0))
    return pl.pallas_call(
        functools.partial(body), name=name, grid=(rows // tr,),
        out_shape=(jax.ShapeDtypeStruct((rows, cols), F32),) * 4,
        in_specs=[pl.BlockSpec((N_DEV, tr, cols), lambda i: (0, i, 0))] + [spec] * 3, out_specs=(spec,) * 4,
        compiler_params=_params("parallel"),
    )(buf, w, m, v)


def _rope_tables(n_ctx, n):
    n_freq = RET_DIM // 4
    inv = np.float32(ROPE_BASE) ** (-np.arange(n_freq, dtype=np.float32) / np.float32(n_freq))
    tok = np.arange(n)
    pos_r = (tok // GRID_W).astype(np.float32)
    pos_c = (tok % GRID_W).astype(np.float32)
    ang_r = (pos_r[:, None] * inv[None, :]).astype(np.float32)
    ang_c = (pos_c[:, None] * inv[None, :]).astype(np.float32)
    cos = np.concatenate([np.cos(ang_r), np.cos(ang_r), np.cos(ang_c), np.cos(ang_c)], axis=-1)
    sin = np.concatenate([-np.sin(ang_r), np.sin(ang_r), -np.sin(ang_c), np.sin(ang_c)], axis=-1)
    cos = np.concatenate([np.ones((n_ctx, RET_DIM), np.float32), cos], axis=0)
    sin = np.concatenate([np.zeros((n_ctx, RET_DIM), np.float32), sin], axis=0)
    return jnp.asarray(cos, F32), jnp.asarray(sin, F32)


def _na_tables():
    q = np.arange(GRID_W)[:, None]
    k = np.arange(GRID_W)[None, :]
    start = np.clip(q - NA_KW // 2, 0, GRID_W - NA_KW)
    valid = (k >= start) & (k < start + NA_KW)
    dc = np.clip(k - q + (NA_KW - 1), 0, 2 * NA_KW - 2)
    ncls = 2 * NA_KW - 1
    onehot = (dc[None] == np.arange(ncls)[:, None, None]) & valid[None]
    oh2 = np.zeros((GRID_W, LANES, LANES), np.float32)
    for c in range(ncls):
        oh2[:, :GRID_W, c] = onehot[c]
        oh2[:, GRID_W:, 32 + c] = onehot[c]
    return onehot.astype(np.float32), valid, oh2.reshape(GRID_W * LANES, LANES)


def _paired_bias(rpb, onehot, valid):
    ncls = onehot.shape[0]
    pair = np.zeros((2 * ncls, GRID_W, LANES), np.float32)
    pair[:ncls, :, :GRID_W] = onehot
    pair[ncls:, :, GRID_W:] = onehot
    rows = jnp.concatenate([rpb[:, :-1], rpb[:, 1:]], axis=-1)
    t = jnp.einsum("hdc,cqk->hdqk", rows, jnp.asarray(pair), precision=lax.Precision.HIGHEST)
    return jnp.where(jnp.asarray(np.tile(valid, (1, 2)))[None, None], t, NEG_INF)


def kernel(x, c, ctx, c_ctx, w_ada, b_ada, g_pre_mix, g_post_mix, g_pre_mlp, g_post_mlp, w_in, ret_decay, ret_gn, na_rpb, w_out, w_mlp1, w_mlp2, loss_target, m_c_ctx, m_w_ada, m_b_ada, m_g_pre_mix, m_g_post_mix, m_g_pre_mlp, m_g_post_mlp, m_w_in, m_ret_decay, m_ret_gn, m_na_rpb, m_w_out, m_w_mlp1, m_w_mlp2, v_c_ctx, v_w_ada, v_b_ada, v_g_pre_mix, v_g_post_mix, v_g_pre_mlp, v_g_post_mlp, v_w_in, v_ret_decay, v_ret_gn, v_na_rpb, v_w_out, v_w_mlp1, v_w_mlp2):
    B, N, D = x.shape
    C = ctx.shape[1]
    T = C + N

    silu_all, mods_g, win_b, wout_l, w1_l, w2_l = _mod_gather(c, c_ctx, w_ada[0], b_ada, w_in[0].T, w_out[0],
                                                             w_mlp1[0], w_mlp2[0])
    mods_mine = mods_g.transpose(1, 0, 2).reshape(mods_g.shape[1], N_MOD * D)
    modl = jnp.concatenate([mods_mine[:B], mods_mine[SUBLANES:SUBLANES + 1]], axis=0)
    modl = modl.reshape(B + 1, N_MOD, 1, D)
    rin = w_in.shape[2]
    rout, c1, r2 = wout_l.shape[0], w1_l.shape[1], w2_l.shape[0]

    def rows_of(n):
        return lambda ref: _row_block(ref, n)

    def cols_of(n):
        return lambda ref: _col_block(ref, n)

    cos, sin = _rope_tables(C, N)
    onehot, valid, oh2 = _na_tables()
    bias2 = _paired_bias(na_rpb[0], onehot, valid)
    lg = jax.nn.log_sigmoid(ret_decay[0].astype(F32))

    ag = _SplitScatter([wout_l, w1_l, w2_l], [rows_of(rout), cols_of(c1), rows_of(r2)],
                       [(N_DEV * rout, D), (D, N_DEV * c1), (N_DEV * r2, D)], "ag_mlp_start",
                       kind="gather", masks=SIBLING + ICI_SAME_CORE)
    h_all, proj = _inproj_fwd(x, ctx, modl, g_pre_mix + ag.token[0, 0], win_b)
    o_ret, lat_ret = _ret_fwd(proj, cos, sin, lg, ret_gn, C)
    (lat_na,) = _na_fwd(proj, bias2, C)
    wout_part, w1_part, w2_part = _scatter_wait([ag], lat_na, "ag_mlp_wait")

    (dy1, dlat_ret, dlat_na, dmix, h2, act, du, dz, red_d) = _dense_core(
        lat_ret, lat_na, x, loss_target, modl, g_post_mix, g_pre_mlp, g_post_mlp, wout_part, w1_part, w2_part)

    gw_out_p = _tn_matmul(lat_ret[:, None], dmix, "gw_out_ret", rows_after=lat_na.shape[-1])
    gw_out_p = _tn_matmul(lat_na[:, None], dmix, "gw_out_na", rows_before=lat_ret.shape[-1], into=gw_out_p)
    gw1_p = _tn_matmul(h2[:, None], du, "gw_mlp1")
    gw2_p = _tn_matmul(act[:, None], dz, "gw_mlp2")
    rs_mlp = _SplitScatter([gw_out_p, gw1_p, gw2_p], [rows_of(rout), cols_of(c1), rows_of(r2)],
                           [(rout, D), (D, c1), (r2, D)], "rs_mlp_start")

    dret, dgn_p, dlg_p = _ret_bwd(proj, cos, sin, lg, ret_gn + rs_mlp.token[0, 0], o_ret, dlat_ret, C)
    dna, dbias2 = _na_bwd(proj, bias2, dlat_na, C)
    ret_cols, na_cols = dret.shape[1] * dret.shape[3], dna.shape[1] * dna.shape[3]
    gwin_t_p = _tn_matmul(dret, h_all, "gw_in_ret", rows_after=na_cols)
    gwin_t_p = _tn_matmul(dna, h_all, "gw_in_na", rows_before=ret_cols, into=gwin_t_p)
    rs_in = _SplitScatter([gwin_t_p], [rows_of(rin)], [(rin, D)], "rs_w_in_start")
    grad_x, red_i = _inproj_bwd(dret, dna, x, ctx, dy1, modl, g_pre_mix + rs_in.token[0, 0], win_b)

    rd = red_d
    nct = red_i.shape[1] * C // T
    ri_ctx = red_i[:, :nct].sum(axis=(0, 1))
    ri_lat = red_i[:, nct:].sum(axis=1)
    d_mods = jnp.concatenate([ri_lat[:, 0], ri_lat[:, 1], rd[:, 0], rd[:, 4], rd[:, 3], rd[:, 2]], axis=-1)
    d_cmods = jnp.concatenate([ri_ctx[0], ri_ctx[1], jnp.zeros(((N_MOD - 2) * D,), F32)])[None]
    dm_slot = jnp.concatenate([d_mods, d_cmods, jnp.zeros((SUBLANES - B - 1, N_MOD * D), F32)], axis=0)
    dg_pre_mix = ri_lat[:, 2].sum(axis=0) + ri_ctx[2]
    dg_post_mix = rd[:, 1].sum(axis=0)
    dg_pre_mlp = rd[:, 5].sum(axis=0)
    dg_post_mlp = rd[:, 6].sum(axis=0)
    loss_p = rd[:, 7, 0].sum()
    d_gn = dgn_p[:, 0].sum(axis=0)
    d_lg = dlg_p[:, :, :2, 0].sum(axis=0).T
    d_decay = d_lg * jax.nn.sigmoid(-ret_decay[0].astype(F32))
    rr = _rpb_reduce(dbias2, jnp.asarray(oh2, BF16)).reshape(NA_HEADS, 2 * NA_KH - 2, LANES)
    ncls = 2 * NA_KW - 1
    d_rpb = (jnp.pad(rr[:, :, :ncls], ((0, 0), (0, 1), (0, 0))) + jnp.pad(rr[:, :, 32:32 + ncls], ((0, 0), (1, 0), (0, 0))))
    d_rpb32 = jnp.pad(d_rpb, ((0, 0), (0, 0), (0, 32 - ncls)))
    pieces = [dg_pre_mix, dg_post_mix, dg_pre_mlp, dg_post_mlp, d_gn, d_rpb32.reshape(-1),
              jnp.pad(d_decay.reshape(-1), (0, LANES - d_decay.size)), jnp.full((LANES,), loss_p, F32)]
    vec = jnp.concatenate(pieces)
    pad = (-vec.shape[0]) % (SUBLANES * LANES)
    vec = jnp.pad(vec, (0, pad)).reshape(-1, LANES)
    def whole(ref):
        return lambda p: ref

    small = _SplitScatter([vec, dm_slot], [whole, whole], [vec.shape, dm_slot.shape], "small_start")
    land_out, land_1, land_2, land_in = _scatter_wait([rs_mlp, rs_in], small.token, "rs_wait")
    fused = {"w_in": [a.T for a in _sum_adamw(land_in, w_in[0].T, m_w_in[0].T, v_w_in[0].T, "sum_adamw_w_in")],
             "w_out": _sum_adamw(land_out, w_out[0], m_w_out[0], v_w_out[0], "sum_adamw_w_out"),
             "w_mlp1": _sum_adamw(land_1, w_mlp1[0], m_w_mlp1[0], v_w_mlp1[0], "sum_adamw_w_mlp1"),
             "w_mlp2": _sum_adamw(land_2, w_mlp2[0], m_w_mlp2[0], v_w_mlp2[0], "sum_adamw_w_mlp2")}
    vbuf, mbuf = _scatter_wait([small], fused["w_mlp2"][0], "small_wait")
    tot, g_b_ada, g_w_ada, g_c_ctx = _small_ar(vbuf, mbuf, silu_all, w_ada[0], c_ctx)
    flat = tot.reshape(-1)
    o0 = 0
    g_pre_mix_g = flat[o0:o0 + D]; o0 += D
    g_post_mix_g = flat[o0:o0 + D]; o0 += D
    g_pre_mlp_g = flat[o0:o0 + D]; o0 += D
    g_post_mlp_g = flat[o0:o0 + D]; o0 += D
    g_gn = flat[o0:o0 + RET_WIDTH]; o0 += RET_WIDTH
    nrpb = NA_HEADS * (2 * NA_KH - 1) * 32
    g_rpb = flat[o0:o0 + nrpb].reshape(NA_HEADS, 2 * NA_KH - 1, 32)[:, :, :ncls]; o0 += nrpb
    g_decay = flat[o0:o0 + 2 * RET_HEADS].reshape(2, RET_HEADS); o0 += LANES
    loss = flat[o0]

    grads = {
        "c_ctx": g_c_ctx.reshape(c_ctx.shape), "w_ada": g_w_ada[None], "b_ada": g_b_ada.reshape(b_ada.shape),
        "g_pre_mix": g_pre_mix_g[None], "g_post_mix": g_post_mix_g[None], "g_pre_mlp": g_pre_mlp_g[None],
        "g_post_mlp": g_post_mlp_g[None], "w_in": fused["w_in"][0][None], "ret_decay": g_decay[None], "ret_gn": g_gn[None],
        "na_rpb": g_rpb[None], "w_out": fused["w_out"][0][None], "w_mlp1": fused["w_mlp1"][0][None],
        "w_mlp2": fused["w_mlp2"][0][None],
    }
    weights = dict(c_ctx=c_ctx, w_ada=w_ada, b_ada=b_ada, g_pre_mix=g_pre_mix, g_post_mix=g_post_mix,
                   g_pre_mlp=g_pre_mlp, g_post_mlp=g_post_mlp, w_in=w_in, ret_decay=ret_decay, ret_gn=ret_gn,
                   na_rpb=na_rpb, w_out=w_out, w_mlp1=w_mlp1, w_mlp2=w_mlp2)
    m_in = dict(c_ctx=m_c_ctx, w_ada=m_w_ada, b_ada=m_b_ada, g_pre_mix=m_g_pre_mix, g_post_mix=m_g_post_mix,
                g_pre_mlp=m_g_pre_mlp, g_post_mlp=m_g_post_mlp, w_in=m_w_in, ret_decay=m_ret_decay,
                ret_gn=m_ret_gn, na_rpb=m_na_rpb, w_out=m_w_out, w_mlp1=m_w_mlp1, w_mlp2=m_w_mlp2)
    v_in = dict(c_ctx=v_c_ctx, w_ada=v_w_ada, b_ada=v_b_ada, g_pre_mix=v_g_pre_mix, g_post_mix=v_g_post_mix,
                g_pre_mlp=v_g_pre_mlp, g_post_mlp=v_g_post_mlp, w_in=v_w_in, ret_decay=v_ret_decay,
                ret_gn=v_ret_gn, na_rpb=v_na_rpb, w_out=v_w_out, w_mlp1=v_w_mlp1, w_mlp2=v_w_mlp2)
    names = list(weights)
    deltas, new_m, new_v = {}, {}, {}
    def as_2d(n):
        shp = weights[n].shape
        two_d = (-1, shp[-1]) if len(shp) > 1 else (1, shp[0])
        return [a.reshape(two_d) for a in (weights[n], grads[n], m_in[n], v_in[n])]

    small = [n for n in names if n not in fused and weights[n].size <= 65536]
    updated = dict(zip(small, _adamw_small([as_2d(n) for n in small], "adamw_small")))
    for n in names:
        if n in fused:
            updated[n] = fused[n][1:]
        elif n not in updated:
            updated[n] = _adamw(*as_2d(n), "adamw_" + n)
        deltas[n], new_m[n], new_v[n] = (a.reshape(weights[n].shape) for a in updated[n])
    return (loss, grad_x, *[grads[n] for n in names], *[deltas[n] for n in names],
            *[new_m[n] for n in names], *[new_v[n] for n in names])
```

```python
import functools
import math

import numpy as np
import jax
import jax.numpy as jnp
from jax import lax
from jax.experimental import pallas as pl
from jax.experimental.pallas import tpu as pltpu

F32 = jnp.float32
BF16 = jnp.bfloat16
MESH = pl.DeviceIdType.MESH

N_DEV = 8
LANES = 128
SUBLANES = 8
VMEM_LIMIT = 60 * 1024 * 1024

GRID_W = 64
RET_HEADS = 4
RET_DIM = 128
RET_WIDTH = RET_HEADS * RET_DIM
NA_HEADS = 8
NA_DIM = 64
NA_WIDTH = NA_HEADS * NA_DIM
NA_PAIRS = NA_HEADS // 2
NA_KH = 8
NA_KW = 16
NA_GROUP = 8
SEG = 512
ROPE_BASE = 10000.0
NORM_EPS = 1e-6
NEG_INF = -1e30
N_MOD = 6

ADAM_LR = 0.001
ADAM_B1 = 0.9
ADAM_B2 = 0.999
ADAM_EPS = 1e-08
ADAM_WD = 0.01
ADAM_STEP = 10


def _dot(a, b):
    return lax.dot_general(a, b, (((1,), (0,)), ((), ())), preferred_element_type=F32)


def _dot_nt(a, b):
    return lax.dot_general(a, b, (((1,), (1,)), ((), ())), preferred_element_type=F32)


def _dot_tn(a, b):
    return lax.dot_general(a, b, (((0,), (0,)), ((), ())), preferred_element_type=F32)


def _sigmoid(x):
    return 1.0 / (1.0 + jnp.exp(-x))


def _div_tile(n, cap, mult):
    if n <= cap:
        return n
    for t in range(cap - cap % mult, 0, -mult):
        if n % t == 0:
            return t
    raise ValueError(f"no tile for {n}")


def _params(*sem):
    return pltpu.CompilerParams(dimension_semantics=tuple(sem) if sem else None,
                                vmem_limit_bytes=VMEM_LIMIT)


def _vmem():
    return pl.BlockSpec(memory_space=pltpu.VMEM)


def _any():
    return pl.BlockSpec(memory_space=pl.ANY)


def _me_and_peers():
    x, y, c = lax.axis_index("x"), lax.axis_index("y"), lax.axis_index("c")
    me = 4 * x + 2 * y + c
    peers = []
    for m in range(1, N_DEV):
        px = 1 - x if (m >> 2) & 1 else x
        py = 1 - y if (m >> 1) & 1 else y
        pc = 1 - c if m & 1 else c
        peers.append(((px, py, pc), 4 * px + 2 * py + pc))
    return me, peers


def _exchange(src_for, dst_from, send_sems, recv_sems):
    me, peers = _me_and_peers()
    sent = []
    for i, (dev, pid) in enumerate(peers):
        cp = pltpu.make_async_remote_copy(src_ref=src_for(pid), dst_ref=dst_from(me),
                                          send_sem=send_sems.at[i], recv_sem=recv_sems.at[i],
                                          device_id=dev, device_id_type=MESH)
        cp.start()
        sent.append(cp)
    for i, (dev, pid) in enumerate(peers):
        pltpu.make_async_remote_copy(src_ref=src_for(pid), dst_ref=dst_from(pid),
                                     send_sem=send_sems.at[i], recv_sem=recv_sems.at[i],
                                     device_id=dev, device_id_type=MESH).wait_recv()
    for cp in sent:
        cp.wait_send()


SIBLING = (1,)
ICI_SAME_CORE = (2, 4, 6)
ALL_PEERS = tuple(range(1, N_DEV))


def _remote(src, dst, send_sem, recv_sem, dev):
    return pltpu.make_async_remote_copy(src_ref=src, dst_ref=dst, send_sem=send_sem, recv_sem=recv_sem,
                                        device_id=dev, device_id_type=MESH)


def _push_start(items, masks, send_sems, recv_sems):
    me, peers = _me_and_peers()
    for k, (src_for, dst_from) in enumerate(items):
        for m in masks:
            dev, pid = peers[m - 1]
            _remote(src_for(pid), dst_from(me), send_sems.at[k, m - 1], recv_sems.at[k, m - 1], dev).start()


def _push_wait_recv(items, masks, send_sems, recv_sems):
    me, peers = _me_and_peers()
    for k, (src_for, dst_from) in enumerate(items):
        for m in masks:
            dev, pid = peers[m - 1]
            _remote(src_for(pid), dst_from(pid), send_sems.at[k, m - 1], recv_sems.at[k, m - 1], dev).wait_recv()


def _push_wait_send(items, masks, send_sems, recv_sems):
    me, peers = _me_and_peers()
    for k, (src_for, dst_from) in enumerate(items):
        for m in masks:
            dev, pid = peers[m - 1]
            _remote(src_for(pid), dst_from(me), send_sems.at[k, m - 1], recv_sems.at[k, m - 1], dev).wait_send()


def _forward_start(items, send_sems, recv_sems):
    me, peers = _me_and_peers()
    sib = peers[0][0]
    for k, (blk_in, blk_out) in enumerate(items):
        for j, m in enumerate(ICI_SAME_CORE):
            pid = peers[m - 1][1]
            _remote(blk_in(pid), blk_out(pid), send_sems.at[k, j], recv_sems.at[k, j], sib).start()


def _forward_wait(items, send_sems, recv_sems):
    me, peers = _me_and_peers()
    sib = peers[0][0]
    for k, (blk_in, blk_out) in enumerate(items):
        for j, m in enumerate(ICI_SAME_CORE):
            got = peers[(m | 1) - 1][1]
            _remote(blk_in(got), blk_out(got), send_sems.at[k, j], recv_sems.at[k, j], sib).wait_recv()
    for k, (blk_in, blk_out) in enumerate(items):
        for j, m in enumerate(ICI_SAME_CORE):
            pid = peers[m - 1][1]
            _remote(blk_in(pid), blk_out(pid), send_sems.at[k, j], recv_sems.at[k, j], sib).wait_send()


def _mod_gather(c, c_ctx, w_ada, b_ada, w_in_t, w_out, w1, w2):
    B, D = c.shape
    ncol = w_ada.shape[1]
    rows = SUBLANES * N_DEV + SUBLANES

    def body(c_ref, cc_ref, w_ref, b_ref, win_ref, wout_ref, w1_ref, w2_ref,
             s_ref, m_ref, gin_ref, wout_b, w1_b, w2_b,
             win_b, msend, send1, recv1, send2, recv2, wsend, wrecv, fsend, frecv, lsem):
        me, _ = _me_and_peers()
        win_b[...] = win_ref[...].astype(BF16)
        block = _row_block(gin_ref, w_in_t.shape[0])
        gather = [(lambda p: win_b, block)]
        own = pltpu.make_async_copy(win_b, block(me), lsem.at[0])
        cv = c_ref[...]
        slot = jnp.concatenate([cv * _sigmoid(cv), jnp.zeros((SUBLANES - B, D), F32)], axis=0)
        my_rows = pl.ds(pl.multiple_of(me * SUBLANES, SUBLANES), SUBLANES)
        s_ref[my_rows, :] = slot
        ccv = cc_ref[...]
        s_ref[SUBLANES * N_DEV:, :] = jnp.concatenate(
            [ccv * _sigmoid(ccv), jnp.zeros((SUBLANES - 1, D), F32)], axis=0)

        def rows_of(p):
            return s_ref.at[pl.ds(pl.multiple_of(p * SUBLANES, SUBLANES), SUBLANES), :]

        _exchange(lambda p: rows_of(me), rows_of, send1, recv1)
        own.start()
        _push_start(gather, SIBLING + ICI_SAME_CORE, wsend, wrecv)
        wout_b[...] = wout_ref[...].astype(BF16)
        w1_b[...] = w1_ref[...].astype(BF16)
        w2_b[...] = w2_ref[...].astype(BF16)
        b_loc = b_ref[:, pl.ds(pl.multiple_of(me * ncol, ncol), ncol)]
        mods = _dot(s_ref[...], w_ref[...]) + b_loc
        for p in range(N_DEV):
            msend[p] = jnp.concatenate([mods[p * SUBLANES:(p + 1) * SUBLANES], mods[N_DEV * SUBLANES:]], axis=0)
        m_ref[me] = msend[me]
        columns = [(lambda p: msend.at[p], lambda p: m_ref.at[p])]
        _push_start(columns, ALL_PEERS, send2, recv2)
        _push_wait_recv(gather, ICI_SAME_CORE, wsend, wrecv)
        relay = [(block, block)]
        _forward_start(relay, fsend, frecv)
        _push_wait_recv(columns, ALL_PEERS, send2, recv2)
        _push_wait_recv(gather, SIBLING, wsend, wrecv)
        _forward_wait(relay, fsend, frecv)
        _push_wait_send(columns, ALL_PEERS, send2, recv2)
        _push_wait_send(gather, SIBLING + ICI_SAME_CORE, wsend, wrecv)
        own.wait()

    return pl.pallas_call(
        body, name="mod_gather",
        out_shape=(jax.ShapeDtypeStruct((rows, D), F32), jax.ShapeDtypeStruct((N_DEV, 2 * SUBLANES, ncol), F32),
                   jax.ShapeDtypeStruct((N_DEV * w_in_t.shape[0], D), BF16),
                   jax.ShapeDtypeStruct(w_out.shape, BF16), jax.ShapeDtypeStruct(w1.shape, BF16),
                   jax.ShapeDtypeStruct(w2.shape, BF16)),
        in_specs=[_vmem()] * 8, out_specs=(_vmem(), _vmem(), _any(), _vmem(), _vmem(), _vmem()),
        scratch_shapes=[pltpu.VMEM(w_in_t.shape, BF16), pltpu.VMEM((N_DEV, 2 * SUBLANES, ncol), F32)]
                       + [pltpu.SemaphoreType.DMA((N_DEV - 1,))] * 2
                       + [pltpu.SemaphoreType.DMA((1, N_DEV - 1))] * 4 + [pltpu.SemaphoreType.DMA((1, 3))] * 2
                       + [pltpu.SemaphoreType.DMA((1,))],
        compiler_params=pltpu.CompilerParams(vmem_limit_bytes=VMEM_LIMIT),
    )(c, c_ctx.reshape(1, D), w_ada, b_ada, w_in_t, w_out, w1, w2)


def _row_block(ref, rows):
    return lambda p: ref.at[pl.ds(pl.multiple_of(p * rows, 2 * SUBLANES), rows), :]


def _col_block(ref, cols):
    return lambda p: ref.at[:, pl.ds(pl.multiple_of(p * cols, LANES), cols)]


def _slot(ref):
    return lambda p: ref.at[p]


def _grid_call(body, *, name, grid, out_shape, in_specs, out_specs, scratch_shapes, args):
    return pl.pallas_call(
        body, name=name, grid=grid, out_shape=tuple(out_shape), in_specs=list(in_specs), out_specs=tuple(out_specs),
        scratch_shapes=list(scratch_shapes), compiler_params=_params(*(("arbitrary",) * len(grid))),
    )(*args)


def _token_tiles(n_ctx, tm):
    nct = n_ctx // tm

    def ctx_spec(D):
        return pl.BlockSpec((None, tm, D), lambda b, t: (b, jnp.minimum(t, nct - 1), 0))

    def lat_spec(D):
        return pl.BlockSpec((None, tm, D), lambda b, t: (b, jnp.maximum(t - nct, 0), 0))

    return nct, ctx_spec, lat_spec


def _inproj_fwd(x, ctx, modl, g1, w_in_t):
    B, N, D = x.shape
    n_ctx = ctx.shape[1]
    T = n_ctx + N
    nw = w_in_t.shape[0]
    tm = _div_tile(n_ctx, 256, 16)
    nct, ctx_spec, lat_spec = _token_tiles(n_ctx, tm)

    def body(c_ref, x_ref, sh_ref, sc_ref, g_ref, w_ref, h_ref, p_ref):
        x = jnp.where(pl.program_id(1) < nct, c_ref[...], x_ref[...])
        r = lax.rsqrt(jnp.mean(x * x, axis=-1, keepdims=True) + NORM_EPS)
        h = ((x * r) * g_ref[...]) * (1.0 + sc_ref[...]) + sh_ref[...]
        hb = h.astype(BF16)
        h_ref[...] = hb
        p_ref[...] = _dot_nt(hb, w_ref[...]).astype(BF16)

    def mrow(b, t):
        return jnp.where(t < nct, B, b)

    return _grid_call(
        body, name="inproj_fwd", grid=(B, T // tm),
        out_shape=(jax.ShapeDtypeStruct((B, T, D), BF16), jax.ShapeDtypeStruct((B, T, nw), BF16)),
        in_specs=[ctx_spec(D), lat_spec(D),
                  pl.BlockSpec((None, None, 1, D), lambda b, t: (mrow(b, t), 0, 0, 0)),
                  pl.BlockSpec((None, None, 1, D), lambda b, t: (mrow(b, t), 1, 0, 0)),
                  pl.BlockSpec((1, D), lambda b, t: (0, 0)),
                  pl.BlockSpec((nw, D), lambda b, t: (0, 0))],
        out_specs=(pl.BlockSpec((None, tm, D), lambda b, t: (b, t, 0)),
                   pl.BlockSpec((None, tm, nw), lambda b, t: (b, t, 0))),
        scratch_shapes=[], args=(ctx, x, modl, modl, g1, w_in_t))


def _swap32(x):
    lane = lax.broadcasted_iota(jnp.int32, x.shape, 1)
    return jnp.where((lane % 64) < 32, pltpu.roll(x, 96, 1), pltpu.roll(x, 32, 1))


def _rope(x, cos, sin):
    return x * cos + _swap32(x) * sin


def _unrope(dy, cos, sin):
    return dy * cos + _swap32(dy * sin)


def _ret_weights(lgf, lgb, dist):
    return jnp.exp(jnp.where(dist >= 0.0, lgf * dist, -lgb * dist))


class _RetDecay:
    def __init__(self, lgf, lgb, rows):
        r = lax.broadcasted_iota(jnp.int32, (rows, RET_DIM), 0).astype(F32)
        self.head = r + 1.0
        self.tail = (rows - 1.0) - r
        self.q_f = jnp.exp(lgf * self.head)
        self.k_f = jnp.exp(lgf * self.tail)
        self.q_b = jnp.exp(lgb * self.tail)
        self.k_b = jnp.exp(lgb * self.head)


def _ret_states(kf32, vs, lgf, lgb, C, c, nt, hf, hb, hfa=None, hba=None):
    dec = _RetDecay(lgf, lgb, c)
    dec_c = _RetDecay(lgf, lgb, C)
    step_f = jnp.exp(jnp.zeros((RET_DIM, RET_DIM), F32) + lgf * c)
    step_b = jnp.exp(jnp.zeros((RET_DIM, RET_DIM), F32) + lgb * c)

    def upd(rows, kdec):
        return _dot_tn((kf32[rows, :] * kdec).astype(BF16), vs[rows, :])

    def lat(t):
        return slice(C + t * c, C + (t + 1) * c)

    state = upd(slice(0, C), dec_c.k_f)
    aged = jnp.zeros_like(state)
    for t in range(nt):
        hf[t] = state.astype(BF16)
        if hfa is not None:
            hfa[t] = aged
        if t < nt - 1:
            aged = step_f * (aged + c * state)
            state = step_f * state + upd(lat(t), dec.k_f)
    state = upd(slice(0, C), dec_c.k_b)
    aged = jnp.zeros_like(state)
    for t in range(nt - 1, -1, -1):
        hb[t] = state.astype(BF16)
        if hba is not None:
            hba[t] = aged
        if t > 0:
            aged = step_b * (aged + c * state)
            state = step_b * state + upd(lat(t), dec.k_b)
    return dec, dec_c, step_f, step_b


def _ret_fwd(proj, cos, sin, lg, gn, n_ctx):
    B, T, _ = proj.shape
    C = n_ctx
    N = T - C
    c = _div_tile(N, 256, 16)
    nt = N // c
    scale = RET_DIM ** -0.5

    def body(lg_ref, q_ref, k_ref, v_ref, g_ref, cos_ref, sin_ref, gn_ref, o_ref, lat_ref, qs, ks, vs, kf32, hf, hb):
        h = pl.program_id(1)
        lgf = lg_ref[0, h]
        lgb = lg_ref[1, h]
        for rows in [slice(0, C)] + [slice(C + t * c, C + (t + 1) * c) for t in range(nt)]:
            cosb = cos_ref[rows, :]
            sinb = sin_ref[rows, :]
            qs[rows, :] = (_rope(q_ref[rows, :].astype(F32), cosb, sinb) * scale).astype(BF16)
            kr = _rope(k_ref[rows, :].astype(F32), cosb, sinb)
            kf32[rows, :] = kr
            ks[rows, :] = kr.astype(BF16)
            vs[rows, :] = v_ref[rows, :].astype(BF16)
        gnv = gn_ref[...]
        dec, _, _, _ = _ret_states(kf32, vs, lgf, lgb, C, c, nt, hf, hb)
        rc = (lax.broadcasted_iota(jnp.int32, (c, c), 0) - lax.broadcasted_iota(jnp.int32, (c, c), 1)).astype(F32)
        w_diag = _ret_weights(lgf, lgb, rc)
        for t in range(nt):
            rows = slice(C + t * c, C + (t + 1) * c)
            qt = qs[rows, :]
            s = _dot_nt(qt, ks[rows, :])
            o = (_dot((s * w_diag).astype(BF16), vs[rows, :])
                 + dec.q_f * _dot(qt, hf[t]) + dec.q_b * _dot(qt, hb[t]))
            o_ref[t * c:(t + 1) * c, :] = o
            mu = jnp.mean(o, axis=-1, keepdims=True)
            oc = o - mu
            var = jnp.mean(oc * oc, axis=-1, keepdims=True)
            yh = oc * lax.rsqrt(var + NORM_EPS)
            g = g_ref[rows, :].astype(F32)
            lat_ref[t * c:(t + 1) * c, :] = ((yh * gnv) * (g * _sigmoid(g))).astype(BF16)

    def col(seg):
        return pl.BlockSpec((None, T, RET_DIM), lambda b, h, seg=seg: (b, 0, seg * RET_HEADS + h))

    return _grid_call(
        body, name="ret_fwd", grid=(B, RET_HEADS),
        out_shape=(jax.ShapeDtypeStruct((B, N, RET_WIDTH), F32), jax.ShapeDtypeStruct((B, N, RET_WIDTH), BF16)),
        in_specs=[pl.BlockSpec(memory_space=pltpu.SMEM), col(0), col(1), col(2), col(3),
                  pl.BlockSpec((T, RET_DIM), lambda b, h: (0, 0)), pl.BlockSpec((T, RET_DIM), lambda b, h: (0, 0)),
                  pl.BlockSpec((1, RET_DIM), lambda b, h: (0, h))],
        out_specs=(pl.BlockSpec((None, N, RET_DIM), lambda b, h: (b, 0, h)),
                   pl.BlockSpec((None, N, RET_DIM), lambda b, h: (b, 0, h))),
        scratch_shapes=[pltpu.VMEM((T, RET_DIM), BF16)] * 3 + [pltpu.VMEM((T, RET_DIM), F32)]
                       + [pltpu.VMEM((nt, RET_DIM, RET_DIM), BF16)] * 2,
        args=(lg, proj, proj, proj, proj, cos, sin, gn))


def _ret_bwd(proj, cos, sin, lg, gn, o, dlat, n_ctx):
    B, T, _ = proj.shape
    C = n_ctx
    N = T - C
    c = _div_tile(N, 256, 16)
    nt = N // c
    scale = RET_DIM ** -0.5

    def lat(t):
        return slice(C + t * c, C + (t + 1) * c)

    def body(lg_ref, q_ref, k_ref, v_ref, g_ref, cos_ref, sin_ref, gn_ref, o_ref, dl_ref,
             d_ref, dgn_ref, dlg_ref, qs, ks, vs, dos, qf32, kf32, hf, hb, hfa, hba, gf_s, gb_s):
        h = pl.program_id(1)
        lgf = lg_ref[0, h]
        lgb = lg_ref[1, h]
        gnv = gn_ref[...]

        def fold(a):
            return jnp.sum(a.reshape(a.shape[0] // SUBLANES, SUBLANES, a.shape[1]), axis=0)

        for rows in [slice(0, C)] + [lat(t) for t in range(nt)]:
            cosb = cos_ref[rows, :]
            sinb = sin_ref[rows, :]
            qr = _rope(q_ref[rows, :].astype(F32), cosb, sinb) * scale
            qf32[rows, :] = qr
            qs[rows, :] = qr.astype(BF16)
            kr = _rope(k_ref[rows, :].astype(F32), cosb, sinb)
            kf32[rows, :] = kr
            ks[rows, :] = kr.astype(BF16)
            vs[rows, :] = v_ref[rows, :].astype(BF16)

        dgn = jnp.zeros((1, RET_DIM), F32)
        for t in range(nt):
            lrows = slice(t * c, (t + 1) * c)
            ov = o_ref[lrows, :]
            mu = jnp.mean(ov, axis=-1, keepdims=True)
            oc = ov - mu
            var = jnp.mean(oc * oc, axis=-1, keepdims=True)
            rstd = lax.rsqrt(var + NORM_EPS)
            yh = oc * rstd
            g = g_ref[lat(t), :].astype(F32)
            sg = _sigmoid(g)
            dl = dl_ref[lrows, :]
            d_ref[3, lat(t), :] = (dl * (yh * gnv) * (sg * (1.0 + g * (1.0 - sg)))).astype(BF16)
            dls = dl * (g * sg)
            dgn = dgn + jnp.sum(dls * yh, axis=0, keepdims=True)
            dyh = dls * gnv
            do = rstd * (dyh - jnp.mean(dyh, axis=-1, keepdims=True)
                         - yh * jnp.mean(dyh * yh, axis=-1, keepdims=True))
            dos[lrows, :] = do.astype(BF16)
        dgn_ref[...] = jnp.concatenate([dgn, jnp.zeros((SUBLANES - 1, RET_DIM), F32)], axis=0)
        d_ref[3, 0:C, :] = jnp.zeros((C, RET_DIM), BF16)
        d_ref[0, 0:C, :] = jnp.zeros((C, RET_DIM), BF16)

        dec, dec_c, step_f, step_b = _ret_states(kf32, vs, lgf, lgb, C, c, nt, hf, hb, hfa, hba)

        def zmat(t, qdec):
            return _dot_tn((qf32[lat(t), :] * qdec).astype(BF16), dos[t * c:(t + 1) * c, :])

        acc3f = jnp.zeros((RET_DIM, RET_DIM), F32)
        acc3b = jnp.zeros((RET_DIM, RET_DIM), F32)
        state = jnp.zeros((RET_DIM, RET_DIM), F32)
        for t in range(nt - 1, -1, -1):
            gf_s[t] = state.astype(BF16)
            z = zmat(t, dec.q_f)
            acc3f = acc3f + hfa[t] * z
            state = step_f * state + z
        gctx_f = state.astype(BF16)
        state = jnp.zeros((RET_DIM, RET_DIM), F32)
        for t in range(nt):
            gb_s[t] = state.astype(BF16)
            z = zmat(t, dec.q_b)
            acc3b = acc3b + hba[t] * z
            state = step_b * state + z
        gctx_b = state.astype(BF16)

        rc = (lax.broadcasted_iota(jnp.int32, (c, c), 0) - lax.broadcasted_iota(jnp.int32, (c, c), 1)).astype(F32)
        w_diag = _ret_weights(lgf, lgb, rc)
        wg_f = jnp.where(rc >= 0.0, w_diag * rc, 0.0)
        wg_b = jnp.where(rc < 0.0, -w_diag * rc, 0.0)
        accf = jnp.zeros((SUBLANES, RET_DIM), F32)
        accb = jnp.zeros((SUBLANES, RET_DIM), F32)
        gdf = jnp.zeros((SUBLANES, c), F32)
        gdb = jnp.zeros((SUBLANES, c), F32)
        for t in range(nt):
            rows = lat(t)
            qt = qs[rows, :]
            kt = ks[rows, :]
            vt = vs[rows, :]
            dot = dos[t * c:(t + 1) * c, :]
            s = _dot_nt(qt, kt)
            dp = _dot_nt(dot, vt)
            dv = _dot_tn((s * w_diag).astype(BF16), dot)
            ds = (dp * w_diag).astype(BF16)
            dq = _dot(ds, kt)
            dk = _dot_tn(ds, qt)
            gs = dp * s
            gdf = gdf + fold(gs * wg_f)
            gdb = gdb + fold(gs * wg_b)
            qv = qf32[rows, :]
            kv = kf32[rows, :]
            dq_f = dec.q_f * _dot_nt(dot, hf[t])
            dq_b = dec.q_b * _dot_nt(dot, hb[t])
            dk_f = dec.k_f * _dot_nt(vt, gf_s[t])
            dk_b = dec.k_b * _dot_nt(vt, gb_s[t])
            accf = accf + fold(dec.head * dq_f * qv) + fold(dec.tail * dk_f * kv)
            accb = accb + fold(dec.tail * dq_b * qv) + fold(dec.head * dk_b * kv)
            dv = dv + dec.k_f * _dot(kt, gf_s[t]) + dec.k_b * _dot(kt, gb_s[t])
            cosb = cos_ref[rows, :]
            sinb = sin_ref[rows, :]
            d_ref[0, rows, :] = _unrope((dq + dq_f + dq_b) * scale, cosb, sinb).astype(BF16)
            d_ref[1, rows, :] = _unrope(dk + dk_f + dk_b, cosb, sinb).astype(BF16)
            d_ref[2, rows, :] = dv.astype(BF16)
        kc = ks[0:C, :]
        vc = vs[0:C, :]
        kcv = kf32[0:C, :]
        dkc_f = dec_c.k_f * _dot_nt(vc, gctx_f)
        dkc_b = dec_c.k_b * _dot_nt(vc, gctx_b)
        accf = accf + fold(dec_c.tail * dkc_f * kcv)
        accb = accb + fold(dec_c.head * dkc_b * kcv)
        d_ref[1, 0:C, :] = (dkc_f + dkc_b).astype(BF16)
        d_ref[2, 0:C, :] = (dec_c.k_f * _dot(kc, gctx_f) + dec_c.k_b * _dot(kc, gctx_b)).astype(BF16)
        gf = jnp.sum(gdf) + jnp.sum(accf) + jnp.sum(acc3f)
        gb = jnp.sum(gdb) + jnp.sum(accb) + jnp.sum(acc3b)
        row = lax.broadcasted_iota(jnp.int32, (SUBLANES, LANES), 0)
        dlg_ref[...] = jnp.where(row == 0, gf, jnp.where(row == 1, gb, 0.0))

    def col(seg):
        return pl.BlockSpec((None, T, RET_DIM), lambda b, h, seg=seg: (b, 0, seg * RET_HEADS + h))

    return _grid_call(
        body, name="ret_bwd", grid=(B, RET_HEADS),
        out_shape=(jax.ShapeDtypeStruct((B, 4, T, RET_WIDTH), BF16),
                   jax.ShapeDtypeStruct((B, SUBLANES, RET_WIDTH), F32),
                   jax.ShapeDtypeStruct((B, RET_HEADS, SUBLANES, LANES), F32)),
        in_specs=[pl.BlockSpec(memory_space=pltpu.SMEM), col(0), col(1), col(2), col(3),
                  pl.BlockSpec((T, RET_DIM), lambda b, h: (0, 0)), pl.BlockSpec((T, RET_DIM), lambda b, h: (0, 0)),
                  pl.BlockSpec((1, RET_DIM), lambda b, h: (0, h)),
                  pl.BlockSpec((None, N, RET_DIM), lambda b, h: (b, 0, h)),
                  pl.BlockSpec((None, N, RET_DIM), lambda b, h: (b, 0, h))],
        out_specs=(pl.BlockSpec((None, 4, T, RET_DIM), lambda b, h: (b, 0, 0, h)),
                   pl.BlockSpec((None, SUBLANES, RET_DIM), lambda b, h: (b, 0, h)),
                   pl.BlockSpec((None, None, SUBLANES, LANES), lambda b, h: (b, h, 0, 0))),
        scratch_shapes=[pltpu.VMEM((T, RET_DIM), BF16)] * 3 + [pltpu.VMEM((N, RET_DIM), BF16)]
                       + [pltpu.VMEM((T, RET_DIM), F32)] * 2
                       + [pltpu.VMEM((nt, RET_DIM, RET_DIM), BF16)] * 2 + [pltpu.VMEM((nt, RET_DIM, RET_DIM), F32)] * 2
                       + [pltpu.VMEM((nt, RET_DIM, RET_DIM), BF16)] * 2,
        args=(lg, proj, proj, proj, proj, cos, sin, gn, o, dlat))


def _na_geometry(rows):
    kh = min(NA_KH, rows)
    return kh, kh * GRID_W


def _pair_select():
    lane = lax.broadcasted_iota(jnp.int32, (2 * GRID_W, LANES), 1)
    row = lax.broadcasted_iota(jnp.int32, (2 * GRID_W, LANES), 0)
    return (lane >= NA_DIM) == (row >= GRID_W)


def _pair_bias(bias_ref, dr0, kh):
    return jnp.concatenate(
        [jnp.concatenate([bias_ref[e, pl.ds(dr0 + 2 * m, 1)].reshape(GRID_W, LANES) for m in range(kh // 2)], axis=1)
         for e in range(2)], axis=0)


def _na_softmax(s_loc, s_ctx):
    mx = jnp.maximum(jnp.max(s_loc, axis=-1, keepdims=True), jnp.max(s_ctx, axis=-1, keepdims=True))
    p_loc = jnp.exp(s_loc - mx)
    p_ctx = jnp.exp(s_ctx - mx)
    den = jnp.sum(p_loc, axis=-1, keepdims=True) + jnp.sum(p_ctx, axis=-1, keepdims=True)
    return p_loc, p_ctx, den


def _na_fwd(proj, bias2, n_ctx):
    B, T, _ = proj.shape
    C = n_ctx
    N = T - C
    R = N // GRID_W
    kh, nk = _na_geometry(R)
    scale = NA_DIM ** -0.5
    base = (4 * RET_WIDTH) // LANES

    def body(q_ref, k_ref, v_ref, bias_ref, out_ref, kb16, vb16):
        kb16[...] = k_ref[...].astype(BF16)
        vb16[...] = v_ref[...].astype(BF16)
        kc = kb16[0:C, :]
        vc = vb16[0:C, :]
        lane = lax.broadcasted_iota(jnp.int32, (GRID_W, LANES), 1)
        sel2 = _pair_select()

        def group(gi, carry):
            pre = []
            for u in range(NA_GROUP):
                r = gi * NA_GROUP + u
                bs = jnp.clip(r - kh // 2, 0, R - kh)
                dr0 = bs - r + (NA_KH - 1)
                q = q_ref[pl.ds(pl.multiple_of(C + r * GRID_W, GRID_W), GRID_W), :].astype(F32) * scale
                q2 = jnp.where(sel2, jnp.concatenate([q, q], axis=0), 0.0).astype(BF16)
                band = pl.ds(pl.multiple_of(C + bs * GRID_W, GRID_W), nk)
                s_loc = _dot_nt(q2, kb16[band, :]) + _pair_bias(bias_ref, dr0, kh)
                s_ctx = _dot_nt(q2, kc)
                pre.append((r, band, s_loc, s_ctx))
            mid = [(r, band) + _na_softmax(s_loc, s_ctx) for r, band, s_loc, s_ctx in pre]
            for r, band, p_loc, p_ctx, den in mid:
                o2 = (_dot(p_loc.astype(BF16), vb16[band, :]) + _dot(p_ctx.astype(BF16), vc)) / den
                out_ref[pl.ds(pl.multiple_of(r * GRID_W, GRID_W), GRID_W), :] = jnp.where(
                    lane < NA_DIM, o2[:GRID_W], o2[GRID_W:]).astype(BF16)
            return carry

        lax.fori_loop(0, R // NA_GROUP, group, 0)

    def col(seg):
        return pl.BlockSpec((None, T, LANES), lambda b, p, seg=seg: (b, 0, base + seg * NA_PAIRS + p))

    return _grid_call(
        body, name="na_fwd", grid=(B, NA_PAIRS),
        out_shape=(jax.ShapeDtypeStruct((B, N, NA_WIDTH), BF16),),
        in_specs=[col(0), col(1), col(2),
                  pl.BlockSpec((2, 2 * NA_KH - 2, GRID_W, LANES), lambda b, p: (p, 0, 0, 0))],
        out_specs=(pl.BlockSpec((None, N, LANES), lambda b, p: (b, 0, p)),),
        scratch_shapes=[pltpu.VMEM((T, LANES), BF16)] * 2,
        args=(proj, proj, proj, bias2))


def _na_bwd(proj, bias2, dlat, n_ctx):
    B, T, _ = proj.shape
    C = n_ctx
    N = T - C
    R = N // GRID_W
    kh, nk = _na_geometry(R)
    scale = NA_DIM ** -0.5
    base = (4 * RET_WIDTH) // LANES

    def body(q_ref, k_ref, v_ref, bias_ref, dl_ref, d_ref, db_ref, kb16, vb16, dkv):
        b = pl.program_id(1)
        kb16[...] = k_ref[...].astype(BF16)
        vb16[...] = v_ref[...].astype(BF16)
        kc = kb16[0:C, :]
        vc = vb16[0:C, :]
        lane = lax.broadcasted_iota(jnp.int32, (GRID_W, LANES), 1)
        dkv[...] = jnp.zeros(dkv.shape, F32)
        d_ref[0, 0:C, :] = jnp.zeros((C, LANES), BF16)

        @pl.when(b == 0)
        def _():
            db_ref[...] = jnp.zeros(db_ref.shape, F32)

        sel2 = _pair_select()

        def group(gi, carry):
            pre = []
            for u in range(NA_GROUP):
                r = gi * NA_GROUP + u
                bs = jnp.clip(r - kh // 2, 0, R - kh)
                dr0 = bs - r + (NA_KH - 1)
                q = q_ref[pl.ds(pl.multiple_of(C + r * GRID_W, GRID_W), GRID_W), :].astype(F32) * scale
                do = dl_ref[pl.ds(pl.multiple_of(r * GRID_W, GRID_W), GRID_W), :]
                q2 = jnp.where(sel2, jnp.concatenate([q, q], axis=0), 0.0).astype(BF16)
                do2 = jnp.where(sel2, jnp.concatenate([do, do], axis=0), 0.0).astype(BF16)
                band = pl.ds(pl.multiple_of(C + bs * GRID_W, GRID_W), nk)
                s_loc = _dot_nt(q2, kb16[band, :]) + _pair_bias(bias_ref, dr0, kh)
                s_ctx = _dot_nt(q2, kc)
                dp_loc = _dot_nt(do2, vb16[band, :])
                dp_ctx = _dot_nt(do2, vc)
                pre.append((r, dr0, band, q2, do2, s_loc, s_ctx, dp_loc, dp_ctx))
            mid = []
            for r, dr0, band, q2, do2, s_loc, s_ctx, dp_loc, dp_ctx in pre:
                p_loc, p_ctx, den = _na_softmax(s_loc, s_ctx)
                inv = 1.0 / den
                p_loc = p_loc * inv
                p_ctx = p_ctx * inv
                delta = (jnp.sum(p_loc * dp_loc, axis=-1, keepdims=True)
                         + jnp.sum(p_ctx * dp_ctx, axis=-1, keepdims=True))
                ds_loc = p_loc * (dp_loc - delta)
                ds_ctx = p_ctx * (dp_ctx - delta)
                mid.append((r, dr0, band, q2, do2, p_loc.astype(BF16), p_ctx.astype(BF16), ds_loc, ds_ctx))
            for r, dr0, band, q2, do2, pb_loc, pb_ctx, ds_loc, ds_ctx in mid:
                dsb_loc = ds_loc.astype(BF16)
                dsb_ctx = ds_ctx.astype(BF16)
                dq2 = _dot(dsb_loc, kb16[band, :]) + _dot(dsb_ctx, kc)
                d_ref[0, pl.ds(pl.multiple_of(C + r * GRID_W, GRID_W), GRID_W), :] = (jnp.where(
                    lane < NA_DIM, dq2[:GRID_W], dq2[GRID_W:]) * scale).astype(BF16)
                dkv[0, band, :] += _dot_tn(dsb_loc, q2)
                dkv[1, band, :] += _dot_tn(pb_loc, do2)
                dkv[0, 0:C, :] += _dot_tn(dsb_ctx, q2)
                dkv[1, 0:C, :] += _dot_tn(pb_ctx, do2)
                for e in range(2):
                    for m in range(kh // 2):
                        db_ref[e, pl.ds(dr0 + 2 * m, 1)] += ds_loc[e * GRID_W:(e + 1) * GRID_W,
                                                                   m * LANES:(m + 1) * LANES].reshape(1, GRID_W, LANES)
            return carry

        lax.fori_loop(0, R // NA_GROUP, group, 0)
        d_ref[1] = dkv[0].astype(BF16)
        d_ref[2] = dkv[1].astype(BF16)

    def col(seg):
        return pl.BlockSpec((None, T, LANES), lambda p, b, seg=seg: (b, 0, base + seg * NA_PAIRS + p))

    return _grid_call(
        body, name="na_bwd", grid=(NA_PAIRS, B),
        out_shape=(jax.ShapeDtypeStruct((B, 3, T, NA_WIDTH), BF16),
                   jax.ShapeDtypeStruct((NA_HEADS, 2 * NA_KH - 2, GRID_W, LANES), F32)),
        in_specs=[col(0), col(1), col(2),
                  pl.BlockSpec((2, 2 * NA_KH - 2, GRID_W, LANES), lambda p, b: (p, 0, 0, 0)),
                  pl.BlockSpec((None, N, LANES), lambda p, b: (b, 0, p))],
        out_specs=(pl.BlockSpec((None, 3, T, LANES), lambda p, b: (b, 0, 0, p)),
                   pl.BlockSpec((2, 2 * NA_KH - 2, GRID_W, LANES), lambda p, b: (p, 0, 0, 0))),
        scratch_shapes=[pltpu.VMEM((T, LANES), BF16)] * 2 + [pltpu.VMEM((2, T, LANES), F32)],
        args=(proj, proj, proj, bias2, dlat))


def _split3(a):
    hi = a.astype(BF16)
    r1 = a - hi.astype(F32)
    mid = r1.astype(BF16)
    lo = (r1 - mid.astype(F32)).astype(BF16)
    return hi, mid, lo


def _rpb_reduce(dbias2, onehot2):
    rows = dbias2.shape[0] * dbias2.shape[1]
    flat = dbias2.reshape(rows, GRID_W * LANES)

    def body(a_ref, oh_ref, o_ref):
        hi, mid, lo = _split3(a_ref[...])
        oh = oh_ref[...]
        o_ref[...] = _dot(hi, oh) + _dot(mid, oh) + _dot(lo, oh)

    return pl.pallas_call(
        body, name="rpb_reduce", out_shape=jax.ShapeDtypeStruct((rows, LANES), F32),
        in_specs=[_vmem(), _vmem()], out_specs=_vmem(),
        compiler_params=pltpu.CompilerParams(vmem_limit_bytes=VMEM_LIMIT),
    )(flat, onehot2)


def _dense_core(lat_ret, lat_na, x, tgt, modl, g_post_mix, g_pre_mlp, g_post_mlp, w_out, w1, w2):
    B, N, D = x.shape
    F = w1.shape[1]
    wout_rows, w1_cols, w2_rows = w_out.shape[0] // N_DEV, w1.shape[1] // N_DEV, w2.shape[0] // N_DEV
    mixw = w_out.shape[0]
    half = mixw // 2
    tm = _div_tile(N, 256, 16)
    nt = N // tm
    fc = _div_tile(F, 1024, LANES)

    def body(lr_ref, ln_ref, x_ref, t_ref, gt1_ref, sh2_ref, sc2_ref, gt2_ref, gpm_ref, gpre_ref, gpo_ref,
             wout_part, w1_part, w2_part,
             dy1_ref, dlr_ref, dln_ref, dmix_ref, h2_ref, a_ref, du_ref, dz_ref, red_ref, wout_hbm, w1_hbm, w2_hbm,
             wout_v, w1_v, w2_v, u_s, sems, fsend, frecv):
        @pl.when((pl.program_id(0) == 0) & (pl.program_id(1) == 0))
        def _():
            relay = [(_row_block(wout_part, wout_rows), _row_block(wout_hbm, wout_rows)),
                     (_col_block(w1_part, w1_cols), _col_block(w1_hbm, w1_cols)),
                     (_row_block(w2_part, w2_rows), _row_block(w2_hbm, w2_rows))]
            _forward_start(relay, fsend, frecv)
            _forward_wait(relay, fsend, frecv)
            cps = [pltpu.make_async_copy(wout_hbm, wout_v, sems.at[0]),
                   pltpu.make_async_copy(w1_hbm, w1_v, sems.at[1]),
                   pltpu.make_async_copy(w2_hbm, w2_v, sems.at[2])]
            for cp in cps:
                cp.start()
            for cp in cps:
                cp.wait()

        @pl.when(pl.program_id(1) == 0)
        def _():
            red_ref[...] = jnp.zeros(red_ref.shape, F32)

        gt1 = gt1_ref[...]
        sh2 = sh2_ref[...]
        sc2 = sc2_ref[...]
        gt2 = gt2_ref[...]
        gpm = gpm_ref[...]
        gpre = gpre_ref[...]
        gpo = gpo_ref[...]

        def rowmean(a):
            return jnp.mean(a, axis=-1, keepdims=True)

        def colsum(a):
            return jnp.sum(a, axis=0, keepdims=True)

        mix_gain = gt1 * gpm
        mlp_in_gain = gpre * (1.0 + sc2)
        mlp_out_gain = gt2 * gpo
        mix = _dot(lr_ref[...], wout_v[0:half, :]) + _dot(ln_ref[...], wout_v[half:, :])
        x = x_ref[...]
        rm = lax.rsqrt(rowmean(mix * mix) + NORM_EPS)
        mh = mix * rm
        y1 = x + mh * mix_gain
        r1 = lax.rsqrt(rowmean(y1 * y1) + NORM_EPS)
        xh = y1 * r1
        h2b = (xh * mlp_in_gain + sh2).astype(BF16)
        h2_ref[...] = h2b
        z = jnp.zeros((tm, D), F32)
        for c0 in range(0, F, fc):
            u = _dot(h2b, w1_v[:, c0:c0 + fc])
            u_s[:, c0:c0 + fc] = u
            ru = jnp.maximum(u, 0.0)
            ab = (ru * ru).astype(BF16)
            a_ref[:, c0:c0 + fc] = ab
            z = z + _dot(ab, w2_v[c0:c0 + fc, :])
        r2 = lax.rsqrt(rowmean(z * z) + NORM_EPS)
        zh = z * r2
        y2 = y1 + zh * mlp_out_gain
        err = y2 - t_ref[...]
        loss = 0.5 * jnp.sum(rowmean(err * err))
        dy2 = err * (1.0 / D)
        s_out = colsum(dy2 * zh)
        red_ref[2:3, :] += s_out * gpo
        red_ref[6:7, :] += s_out * gt2
        dzh = dy2 * mlp_out_gain
        dz = r2 * (dzh - zh * rowmean(dzh * zh))
        dzb = dz.astype(BF16)
        dz_ref[...] = dzb
        dh2 = jnp.zeros((tm, D), F32)
        for c0 in range(0, F, fc):
            da = _dot_nt(dzb, w2_v[c0:c0 + fc, :])
            dub = (da * (2.0 * jnp.maximum(u_s[:, c0:c0 + fc], 0.0))).astype(BF16)
            du_ref[:, c0:c0 + fc] = dub
            dh2 = dh2 + _dot_nt(dub, w1_v[:, c0:c0 + fc])
        s_in = colsum(dh2 * xh)
        red_ref[3:4, :] += s_in * gpre
        red_ref[4:5, :] += colsum(dh2)
        red_ref[5:6, :] += s_in * (1.0 + sc2)
        dxh = dh2 * mlp_in_gain
        dy1 = dy2 + r1 * (dxh - xh * rowmean(dxh * xh))
        dy1_ref[...] = dy1
        s_mix = colsum(dy1 * mh)
        red_ref[0:1, :] += s_mix * gpm
        red_ref[1:2, :] += s_mix * gt1
        dmh = dy1 * mix_gain
        dmix = (rm *(dmh - mh * rowmean(dmh * mh))).astype(BF16)
        dmix_ref[...] = dmix
        dlr_ref[...] = _dot_nt(dmix, wout_v[0:half, :])
        dln_ref[...] = _dot_nt(dmix, wout_v[half:, :])
        red_ref[7:8, :] += jnp.zeros((1, D), F32) + loss

    def tok(w):
        return pl.BlockSpec((None, tm, w), lambda b, t: (b, t, 0))

    def mod(k):
        return pl.BlockSpec((None, None, 1, D), lambda b, t, k=k: (b, k, 0, 0))

    def vec():
        return pl.BlockSpec((1, D), lambda b, t: (0, 0))

    return pl.pallas_call(
        body, name="dense_core", grid=(B, nt),
        out_shape=(jax.ShapeDtypeStruct((B, N, D), F32), jax.ShapeDtypeStruct((B, N, half), F32),
                   jax.ShapeDtypeStruct((B, N, half), F32), jax.ShapeDtypeStruct((B, N, D), BF16),
                   jax.ShapeDtypeStruct((B, N, D), BF16), jax.ShapeDtypeStruct((B, N, F), BF16),
                   jax.ShapeDtypeStruct((B, N, F), BF16), jax.ShapeDtypeStruct((B, N, D), BF16),
                   jax.ShapeDtypeStruct((B, SUBLANES, D), F32),
                   jax.ShapeDtypeStruct(w_out.shape, w_out.dtype), jax.ShapeDtypeStruct(w1.shape, w1.dtype),
                   jax.ShapeDtypeStruct(w2.shape, w2.dtype)),
        in_specs=[tok(half), tok(half), tok(D), tok(D), mod(2), mod(3), mod(4), mod(5), vec(), vec(), vec(),
                  _any(), _any(), _any()],
        out_specs=(tok(D), tok(half), tok(half), tok(D), tok(D), tok(F), tok(F), tok(D),
                   pl.BlockSpec((None, SUBLANES, D), lambda b, t: (b, 0, 0)), _any(), _any(), _any()),
        scratch_shapes=[pltpu.VMEM((mixw, D), BF16), pltpu.VMEM((D, F), BF16), pltpu.VMEM((F, D), BF16),
                        pltpu.VMEM((tm, F), F32), pltpu.SemaphoreType.DMA((3,)),
                        pltpu.SemaphoreType.DMA((3, 3)), pltpu.SemaphoreType.DMA((3, 3))],
        input_output_aliases={11: 9, 12: 10, 13: 11},
        compiler_params=_params("arbitrary", "arbitrary"),
    )(lat_ret, lat_na, x, tgt, modl, modl, modl, modl, g_post_mix, g_pre_mlp, g_post_mlp, w_out, w1, w2)[:9]


def _inproj_bwd(dret, dna, x, ctx, dy1, modl, g1, w_in_t):
    B, N, D = x.shape
    n_ctx = ctx.shape[1]
    T = n_ctx + N
    tm = _div_tile(n_ctx, 256, 16)
    nct, ctx_spec, lat_spec = _token_tiles(n_ctx, tm)
    nt = T // tm
    nseg_r = dret.shape[1]
    nseg_n = dna.shape[1]
    nw = w_in_t.shape[0]

    def body(*refs):
        seg_refs = refs[:nseg_r + nseg_n]
        c_ref, x_ref, dy1_ref, sc_ref, g_ref, w_ref, dx_ref, red_ref = refs[nseg_r + nseg_n:]
        t = pl.program_id(1)
        dh = jnp.zeros((tm, D), F32)
        for s, ref in enumerate(seg_refs):
            dh = dh + _dot(ref[...], w_ref[s * SEG:(s + 1) * SEG, :])
        x = jnp.where(t < nct, c_ref[...], x_ref[...])
        g = g_ref[...]
        r = lax.rsqrt(jnp.mean(x * x, axis=-1, keepdims=True) + NORM_EPS)
        xh = x * r
        red_ref[0:1, :] = jnp.sum(dh, axis=0, keepdims=True)
        red_ref[1:2, :] = jnp.sum(dh * (xh * g), axis=0, keepdims=True)
        dn = dh * (1.0 + sc_ref[...])
        red_ref[2:3, :] = jnp.sum(dn * xh, axis=0, keepdims=True)
        red_ref[3:, :] = jnp.zeros((SUBLANES - 3, D), F32)
        dxh = dn * g
        dx = r * (dxh - xh * jnp.mean(dxh * xh, axis=-1, keepdims=True))
        dx_ref[...] = dx + jnp.where(t >= nct, dy1_ref[...], 0.0)

    def mrow(b, t):
        return jnp.where(t < nct, B, b)

    def seg(s):
        return pl.BlockSpec((None, None, tm, SEG), lambda b, t, s=s: (b, s, t, 0))

    return _grid_call(
        body, name="inproj_bwd", grid=(B, nt),
        out_shape=(jax.ShapeDtypeStruct((B, N, D), F32), jax.ShapeDtypeStruct((B, nt, SUBLANES, D), F32)),
        in_specs=[seg(s) for s in range(nseg_r)] + [seg(s) for s in range(nseg_n)]
                 + [ctx_spec(D), lat_spec(D), lat_spec(D),
                    pl.BlockSpec((None, None, 1, D), lambda b, t: (mrow(b, t), 1, 0, 0)),
                    pl.BlockSpec((1, D), lambda b, t: (0, 0)),
                    pl.BlockSpec((nw, D), lambda b, t: (0, 0))],
        out_specs=(lat_spec(D), pl.BlockSpec((None, None, SUBLANES, D), lambda b, t: (b, t, 0, 0))),
        scratch_shapes=[], args=(*([dret] * nseg_r), *([dna] * nseg_n), ctx, x, dy1, modl, g1, w_in_t))


def _tn_matmul(lhs, rhs, name, rows_before=0, rows_after=0, into=None):
    B, S, T, W = lhs.shape
    nn = rhs.shape[-1]
    tk = _div_tile(T, 2304, LANES)
    bm = _div_tile(W, 1024, LANES)
    bn = _div_tile(nn, 1024, LANES)
    nkt = T // tk
    nk = B * nkt

    def body(l_ref, r_ref, *rest):
        o_ref, acc = rest[-2:]
        k = pl.program_id(3)

        @pl.when(k == 0)
        def _():
            acc[...] = jnp.zeros(acc.shape, F32)

        acc[...] += _dot_tn(l_ref[...].astype(BF16), r_ref[...].astype(BF16))

        @pl.when(k == nk - 1)
        def _():
            o_ref[...] = acc[...].astype(BF16)

    nwb = W // bm
    first = rows_before // bm
    return pl.pallas_call(
        functools.partial(body), name=name, grid=(S, nwb, nn // bn, nk),
        out_shape=jax.ShapeDtypeStruct((rows_before + S * W + rows_after, nn), BF16),
        in_specs=[pl.BlockSpec((None, None, tk, bm), lambda s, i, j, k: (k // nkt, s, k % nkt, i)),
                  pl.BlockSpec((None, tk, bn), lambda s, i, j, k: (k // nkt, k % nkt, j))]
                 + ([] if into is None else [_any()]),
        out_specs=pl.BlockSpec((bm, bn), lambda s, i, j, k: (first + s * nwb + i, j)),
        scratch_shapes=[pltpu.VMEM((bm, bn), F32)],
        input_output_aliases={} if into is None else {2: 0},
        compiler_params=_params("parallel", "parallel", "parallel", "arbitrary"),
    )(lhs, rhs, *([] if into is None else [into]))


class _SplitScatter:
    def __init__(self, gs, block_ofs, land_shapes, name, kind="scatter", masks=ALL_PEERS):
        self.n = n = len(gs)
        self.block_ofs, self.kind, self.masks = block_ofs, kind, masks
        if kind == "scatter":
            land_shapes = [(N_DEV,) + tuple(bs) for bs in land_shapes]
        hbm = pl.BlockSpec(memory_space=pltpu.HBM)
        sem = pl.BlockSpec(memory_space=pltpu.SEMAPHORE)

        def body(*refs):
            g_refs, land_refs = refs[:n], refs[n:2 * n]
            send_sems, recv_sems, own_sems = refs[2 * n:2 * n + 3]
            token = refs[-1]
            for own, pushes in self._copies(g_refs, land_refs, send_sems, recv_sems, own_sems, landing="sender"):
                own.start()
                for cp in pushes:
                    cp.start()
            token[...] = jnp.zeros_like(token)

        outs = pl.pallas_call(
            body, name=name,
            out_shape=(pltpu.SemaphoreType.DMA((n * (N_DEV - 1),)), pltpu.SemaphoreType.DMA((n * (N_DEV - 1),)),
                       pltpu.SemaphoreType.DMA((n,)))
                      + tuple(pltpu.HBM(g.shape, g.dtype) for g in gs)
                      + tuple(pltpu.HBM(s, g.dtype) for s, g in zip(land_shapes, gs))
                      + (jax.ShapeDtypeStruct((SUBLANES, LANES), F32),),
            in_specs=(hbm,) * (2 * n), out_specs=(sem,) * 3 + (hbm,) * (2 * n) + (_vmem(),),
            input_output_aliases={k: 3 + k for k in range(2 * n)},
            compiler_params=pltpu.CompilerParams(has_side_effects=pltpu.SideEffectType.DATAFLOW_SIDE_EFFECTING),
        )(*[pltpu.with_memory_space_constraint(g, pltpu.HBM) for g in gs],
          *[pltpu.with_memory_space_constraint(lax.empty(s, g.dtype), pltpu.HBM) for s, g in zip(land_shapes, gs)])
        self.sems, self.thru, self.token = outs[:3], outs[3:3 + 2 * n], outs[-1]

    def _copies(self, g_refs, land_refs, send_sems, recv_sems, own_sems, landing):
        me, peers = _me_and_peers()
        out = []
        for k in range(self.n):
            if self.kind == "scatter":
                src, dst = self.block_ofs[k](g_refs[k]), _slot(land_refs[k])
            else:
                src, dst = (lambda p, k=k: g_refs[k]), self.block_ofs[k](land_refs[k])
            own = pltpu.make_async_copy(src(me), dst(me), own_sems.at[k])
            pushes = []
            for m in self.masks:
                dev, pid = peers[m - 1]
                i = k * (N_DEV - 1) + m - 1
                pushes.append(_remote(src(pid), dst(me if landing == "sender" else pid),
                                      send_sems.at[i], recv_sems.at[i], dev))
            out.append((own, pushes))
        return out


def _scatter_wait(scatters, after, name):
    hbm = pl.BlockSpec(memory_space=pltpu.HBM)
    sem = pl.BlockSpec(memory_space=pltpu.SEMAPHORE)
    n_arr = [2 * sc.n for sc in scatters]
    total = sum(n_arr)

    def body(*refs):
        arrs, sems = refs[:total], refs[total:total + 3 * len(scatters)]
        a0 = 0
        for j, sc in enumerate(scatters):
            g_refs, land_refs = arrs[a0:a0 + sc.n], arrs[a0 + sc.n:a0 + 2 * sc.n]
            a0 += 2 * sc.n
            send_sems, recv_sems, own_sems = sems[3 * j:3 * j + 3]
            for (own, sent), (_, got) in zip(sc._copies(g_refs, land_refs, send_sems, recv_sems, own_sems, "sender"),
                                             sc._copies(g_refs, land_refs, send_sems, recv_sems, own_sems, "receiver")):
                own.wait()
                for cp in sent:
                    cp.wait_send()
                for cp in got:
                    cp.wait_recv()

    operands = [a for sc in scatters for a in sc.thru]
    outs = pl.pallas_call(
        body, name=name,
        out_shape=tuple(pltpu.HBM(a.shape, a.dtype) for a in operands),
        in_specs=(hbm,) * total + (sem,) * (3 * len(scatters)) + (pl.BlockSpec(memory_space=pl.ANY),),
        out_specs=(hbm,) * total, input_output_aliases={k: k for k in range(total)},
        compiler_params=pltpu.CompilerParams(has_side_effects=pltpu.SideEffectType.DATAFLOW_SIDE_EFFECTING),
    )(*operands, *[s for sc in scatters for s in sc.sems], after)
    lands, a0 = [], 0
    for sc in scatters:
        lands.extend(outs[a0 + sc.n:a0 + 2 * sc.n])
        a0 += 2 * sc.n
    return lands


def _small_ar(mbuf, silu_all, w_ada, c_ctx, n_mod_rows, n_vec_rows):
    D = silu_all.shape[1]
    ncol = w_ada.shape[1]
    nm = mbuf.shape[2]
    srows = silu_all.shape[0]

    def body(mbuf, s_ref, w_ref, cc_ref, tot_ref, gb_ref, gw_ref, gc_ref, tbuf, dmx, cmrow, send3, recv3):
        me, _ = _me_and_peers()
        msum = mbuf[0]
        for k in range(1, N_DEV):
            msum = msum + mbuf[k]
        tot_ref[...] = msum[n_mod_rows:n_mod_rows + n_vec_rows]
        gb_ref[...] = jnp.sum(msum[0:n_mod_rows], axis=0, keepdims=True)
        loc = pl.ds(pl.multiple_of(me * ncol, ncol), ncol)
        for k in range(N_DEV):
            dmx[k * SUBLANES:(k + 1) * SUBLANES, :] = mbuf[k, :, loc]
        cmrow[...] = msum
        cm_loc = cmrow[n_mod_rows - 1:n_mod_rows, loc]
        dmx[N_DEV * SUBLANES:, :] = jnp.concatenate([cm_loc, jnp.zeros((SUBLANES - 1, ncol), F32)], axis=0)
        gw_ref[...] = _dot_tn(s_ref[...], dmx[...])
        tbuf[me] = _dot_nt(dmx[N_DEV * SUBLANES:, :], w_ref[...])
        _exchange(lambda p: tbuf.at[me], lambda p: tbuf.at[p], send3, recv3)
        tsum = tbuf[0]
        for k in range(1, N_DEV):
            tsum = tsum + tbuf[k]
        cc = cc_ref[...]
        sg = _sigmoid(cc)
        gc_ref[...] = tsum[0:1, :] * (sg * (1.0 + cc * (1.0 - sg)))

    return pl.pallas_call(
        body, name="small_ar",
        out_shape=(jax.ShapeDtypeStruct((n_vec_rows, nm), F32), jax.ShapeDtypeStruct((1, nm), F32),
                   jax.ShapeDtypeStruct((D, ncol), F32), jax.ShapeDtypeStruct((1, D), F32)),
        in_specs=[_vmem()] * 4, out_specs=(_vmem(),) * 4,
        scratch_shapes=[pltpu.VMEM((N_DEV, SUBLANES, D), F32), pltpu.VMEM((srows, ncol), F32),
                        pltpu.VMEM((SUBLANES, nm), F32)] + [pltpu.SemaphoreType.DMA((N_DEV - 1,))] * 2,
        compiler_params=pltpu.CompilerParams(vmem_limit_bytes=VMEM_LIMIT),
    )(mbuf, silu_all, w_ada, c_ctx.reshape(1, D))


def _adam_update(w, g, m, v):
    mn = ADAM_B1 * m + (1.0 - ADAM_B1) * g
    vn = ADAM_B2 * v + (1.0 - ADAM_B2) * (g * g)
    m_hat = mn / (1.0 - ADAM_B1 ** ADAM_STEP)
    v_hat = vn / (1.0 - ADAM_B2 ** ADAM_STEP)
    return -ADAM_LR * (m_hat / (jnp.sqrt(v_hat) + ADAM_EPS) + ADAM_WD * w), mn, vn


def _adamw(w, g, m, v, name):
    rows, cols = w.shape
    tr = _div_tile(rows, 512, SUBLANES)

    def body(w_ref, g_ref, m_ref, v_ref, d_ref, nm_ref, nv_ref):
        d_ref[...], nm_ref[...], nv_ref[...] = _adam_update(w_ref[...], g_ref[...], m_ref[...], v_ref[...])

    spec = pl.BlockSpec((tr, cols), lambda i: (i, 0))
    return pl.pallas_call(
        functools.partial(body), name=name, grid=(rows // tr,),
        out_shape=(jax.ShapeDtypeStruct((rows, cols), F32),) * 3,
        in_specs=[spec] * 4, out_specs=(spec,) * 3,
        compiler_params=_params("parallel"),
    )(w, g, m, v)


def _adamw_small(items, name):
    n = len(items)

    def body(*refs):
        ins, outs = refs[:4 * n], refs[4 * n:]
        for i in range(n):
            w_ref, g_ref, m_ref, v_ref = ins[4 * i:4 * i + 4]
            outs[3 * i][...], outs[3 * i + 1][...], outs[3 * i + 2][...] = _adam_update(
                w_ref[...], g_ref[...], m_ref[...], v_ref[...])

    outs = pl.pallas_call(
        body, name=name,
        out_shape=tuple(jax.ShapeDtypeStruct(it[0].shape, F32) for it in items for _ in range(3)),
        in_specs=[_vmem()] * (4 * n), out_specs=(_vmem(),) * (3 * n),
        compiler_params=pltpu.CompilerParams(vmem_limit_bytes=VMEM_LIMIT),
    )(*[a for it in items for a in it])
    return [tuple(outs[3 * i:3 * i + 3]) for i in range(n)]


def _sum_adamw(buf, w, m, v, name):
    _, rows, cols = buf.shape
    tr = _div_tile(rows, 512, 2 * SUBLANES)

    def body(b_ref, w_ref, m_ref, v_ref, g_ref, d_ref, nm_ref, nv_ref):
        g = b_ref[0].astype(F32)
        for k in range(1, N_DEV):
            g = g + b_ref[k].astype(F32)
        g_ref[...] = g
        d_ref[...], nm_ref[...], nv_ref[...] = _adam_update(w_ref[...], g, m_ref[...], v_ref[...])

    spec = pl.BlockSpec((tr, cols), lambda i: (i, 0))
    return pl.pallas_call(
        functools.partial(body), name=name, grid=(rows // tr,),
        out_shape=(jax.ShapeDtypeStruct((rows, cols), F32),) * 4,
        in_specs=[pl.BlockSpec((N_DEV, tr, cols), lambda i: (0, i, 0))] + [spec] * 3, out_specs=(spec,) * 4,
        compiler_params=_params("parallel"),
    )(buf, w, m, v)


def _rope_tables(n_ctx, n):
    n_freq = RET_DIM // 4
    inv = np.float32(ROPE_BASE) ** (-np.arange(n_freq, dtype=np.float32) / np.float32(n_freq))
    tok = np.arange(n)
    pos_r = (tok // GRID_W).astype(np.float32)
    pos_c = (tok % GRID_W).astype(np.float32)
    ang_r = (pos_r[:, None] * inv[None, :]).astype(np.float32)
    ang_c = (pos_c[:, None] * inv[None, :]).astype(np.float32)
    cos = np.concatenate([np.cos(ang_r), np.cos(ang_r), np.cos(ang_c), np.cos(ang_c)], axis=-1)
    sin = np.concatenate([-np.sin(ang_r), np.sin(ang_r), -np.sin(ang_c), np.sin(ang_c)], axis=-1)
    cos = np.concatenate([np.ones((n_ctx, RET_DIM), np.float32), cos], axis=0)
    sin = np.concatenate([np.zeros((n_ctx, RET_DIM), np.float32), sin], axis=0)
    return jnp.asarray(cos, F32), jnp.asarray(sin, F32)


def _na_tables():
    q = np.arange(GRID_W)[:, None]
    k = np.arange(GRID_W)[None, :]
    start = np.clip(q - NA_KW // 2, 0, GRID_W - NA_KW)
    valid = (k >= start) & (k < start + NA_KW)
    dc = np.clip(k - q + (NA_KW - 1), 0, 2 * NA_KW - 2)
    ncls = 2 * NA_KW - 1
    onehot = (dc[None] == np.arange(ncls)[:, None, None]) & valid[None]
    oh2 = np.zeros((GRID_W, LANES, LANES), np.float32)
    for c in range(ncls):
        oh2[:, :GRID_W, c] = onehot[c]
        oh2[:, GRID_W:, 32 + c] = onehot[c]
    return onehot.astype(np.float32), valid, oh2.reshape(GRID_W * LANES, LANES)


def _paired_bias(rpb, onehot, valid):
    ncls = onehot.shape[0]
    pair = np.zeros((2 * ncls, GRID_W, LANES), np.float32)
    pair[:ncls, :, :GRID_W] = onehot
    pair[ncls:, :, GRID_W:] = onehot
    rows = jnp.concatenate([rpb[:, :-1], rpb[:, 1:]], axis=-1)
    t = jnp.einsum("hdc,cqk->hdqk", rows, jnp.asarray(pair), precision=lax.Precision.HIGHEST)
    return jnp.where(jnp.asarray(np.tile(valid, (1, 2)))[None, None], t, NEG_INF)


def kernel(x, c, ctx, c_ctx, w_ada, b_ada, g_pre_mix, g_post_mix, g_pre_mlp, g_post_mlp, w_in, ret_decay, ret_gn, na_rpb, w_out, w_mlp1, w_mlp2, loss_target, m_c_ctx, m_w_ada, m_b_ada, m_g_pre_mix, m_g_post_mix, m_g_pre_mlp, m_g_post_mlp, m_w_in, m_ret_decay, m_ret_gn, m_na_rpb, m_w_out, m_w_mlp1, m_w_mlp2, v_c_ctx, v_w_ada, v_b_ada, v_g_pre_mix, v_g_post_mix, v_g_pre_mlp, v_g_post_mlp, v_w_in, v_ret_decay, v_ret_gn, v_na_rpb, v_w_out, v_w_mlp1, v_w_mlp2):
    B, N, D = x.shape
    C = ctx.shape[1]
    T = C + N

    silu_all, mods_g, win_b, wout_l, w1_l, w2_l = _mod_gather(c, c_ctx, w_ada[0], b_ada, w_in[0].T, w_out[0],
                                                             w_mlp1[0], w_mlp2[0])
    mods_mine = mods_g.transpose(1, 0, 2).reshape(mods_g.shape[1], N_MOD * D)
    modl = jnp.concatenate([mods_mine[:B], mods_mine[SUBLANES:SUBLANES + 1]], axis=0)
    modl = modl.reshape(B + 1, N_MOD, 1, D)
    rin = w_in.shape[2]
    rout, c1, r2 = wout_l.shape[0], w1_l.shape[1], w2_l.shape[0]

    def rows_of(n):
        return lambda ref: _row_block(ref, n)

    def cols_of(n):
        return lambda ref: _col_block(ref, n)

    cos, sin = _rope_tables(C, N)
    onehot, valid, oh2 = _na_tables()
    bias2 = _paired_bias(na_rpb[0], onehot, valid)
    lg = jax.nn.log_sigmoid(ret_decay[0].astype(F32))

    ag = _SplitScatter([wout_l, w1_l, w2_l], [rows_of(rout), cols_of(c1), rows_of(r2)],
                       [(N_DEV * rout, D), (D, N_DEV * c1), (N_DEV * r2, D)], "ag_mlp_start",
                       kind="gather", masks=SIBLING + ICI_SAME_CORE)
    h_all, proj = _inproj_fwd(x, ctx, modl, g_pre_mix + ag.token[0, 0], win_b)
    o_ret, lat_ret = _ret_fwd(proj, cos, sin, lg, ret_gn, C)
    (lat_na,) = _na_fwd(proj, bias2, C)
    wout_part, w1_part, w2_part = _scatter_wait([ag], lat_na, "ag_mlp_wait")

    (dy1, dlat_ret, dlat_na, dmix, h2, act, du, dz, red_d) = _dense_core(
        lat_ret, lat_na, x, loss_target, modl, g_post_mix, g_pre_mlp, g_post_mlp, wout_part, w1_part, w2_part)

    gw_out_p = _tn_matmul(lat_ret[:, None], dmix, "gw_out_ret", rows_after=lat_na.shape[-1])
    gw_out_p = _tn_matmul(lat_na[:, None], dmix, "gw_out_na", rows_before=lat_ret.shape[-1], into=gw_out_p)
    gw1_p = _tn_matmul(h2[:, None], du, "gw_mlp1")
    gw2_p = _tn_matmul(act[:, None], dz, "gw_mlp2")
    rs_mlp = _SplitScatter([gw_out_p, gw1_p, gw2_p], [rows_of(rout), cols_of(c1), rows_of(r2)],
                           [(rout, D), (D, c1), (r2, D)], "rs_mlp_start")

    dret, dgn_p, dlg_p = _ret_bwd(proj, cos, sin, lg, ret_gn + rs_mlp.token[0, 0], o_ret, dlat_ret, C)
    dna, dbias2 = _na_bwd(proj, bias2, dlat_na, C)
    ret_cols, na_cols = dret.shape[1] * dret.shape[3], dna.shape[1] * dna.shape[3]
    gwin_t_p = _tn_matmul(dret, h_all, "gw_in_ret", rows_after=na_cols)
    gwin_t_p = _tn_matmul(dna, h_all, "gw_in_na", rows_before=ret_cols, into=gwin_t_p)
    rs_in = _SplitScatter([gwin_t_p], [rows_of(rin)], [(rin, D)], "rs_w_in_start")
    grad_x, red_i = _inproj_bwd(dret, dna, x, ctx, dy1, modl, g_pre_mix + rs_in.token[0, 0], win_b)

    rd = red_d
    nct = red_i.shape[1] * C // T
    ri_ctx = red_i[:, :nct].sum(axis=(0, 1))
    ri_lat = red_i[:, nct:].sum(axis=1)
    d_mods = jnp.concatenate([ri_lat[:, 0], ri_lat[:, 1], rd[:, 0], rd[:, 4], rd[:, 3], rd[:, 2]], axis=-1)
    d_cmods = jnp.concatenate([ri_ctx[0], ri_ctx[1], jnp.zeros(((N_MOD - 2) * D,), F32)])[None]
    dg_pre_mix = ri_lat[:, 2].sum(axis=0) + ri_ctx[2]
    dg_post_mix = rd[:, 1].sum(axis=0)
    dg_pre_mlp = rd[:, 5].sum(axis=0)
    dg_post_mlp = rd[:, 6].sum(axis=0)
    loss_p = rd[:, 7, 0].sum()
    d_gn = dgn_p[:, 0].sum(axis=0)
    d_lg = dlg_p[:, :, :2, 0].sum(axis=0).T
    d_decay = d_lg * jax.nn.sigmoid(-ret_decay[0].astype(F32))
    rr = _rpb_reduce(dbias2, jnp.asarray(oh2, BF16)).reshape(NA_HEADS, 2 * NA_KH - 2, LANES)
    ncls = 2 * NA_KW - 1
    d_rpb = (jnp.pad(rr[:, :, :ncls], ((0, 0), (0, 1), (0, 0))) + jnp.pad(rr[:, :, 32:32 + ncls], ((0, 0), (1, 0), (0, 0))))
    d_rpb32 = jnp.pad(d_rpb, ((0, 0), (0, 0), (0, 32 - ncls)))
    pieces = [dg_pre_mix, dg_post_mix, dg_pre_mlp, dg_post_mlp, d_gn, d_rpb32.reshape(-1),
              jnp.pad(d_decay.reshape(-1), (0, LANES - d_decay.size)), jnp.full((LANES,), loss_p, F32)]
    vec = jnp.concatenate(pieces)
    nm = N_MOD * D
    n_vec_rows = -(-vec.shape[0] // nm)
    assert B + 1 + n_vec_rows <= SUBLANES
    vec = jnp.pad(vec, (0, n_vec_rows * nm - vec.shape[0])).reshape(n_vec_rows, nm)
    dm_slot = jnp.concatenate([d_mods, d_cmods, vec, jnp.zeros((SUBLANES - B - 1 - n_vec_rows, nm), F32)], axis=0)
    def whole(ref):
        return lambda p: ref

    small = _SplitScatter([dm_slot], [whole], [dm_slot.shape], "small_start")
    land_out, land_1, land_2, land_in = _scatter_wait([rs_mlp, rs_in], small.token, "rs_wait")
    fused = {"w_in": [a.T for a in _sum_adamw(land_in, w_in[0].T, m_w_in[0].T, v_w_in[0].T, "sum_adamw_w_in")],
             "w_out": _sum_adamw(land_out, w_out[0], m_w_out[0], v_w_out[0], "sum_adamw_w_out"),
             "w_mlp1": _sum_adamw(land_1, w_mlp1[0], m_w_mlp1[0], v_w_mlp1[0], "sum_adamw_w_mlp1"),
             "w_mlp2": _sum_adamw(land_2, w_mlp2[0], m_w_mlp2[0], v_w_mlp2[0], "sum_adamw_w_mlp2")}
    (mbuf,) = _scatter_wait([small], fused["w_mlp2"][0], "small_wait")
    tot, g_b_ada, g_w_ada, g_c_ctx = _small_ar(mbuf, silu_all, w_ada[0], c_ctx, B + 1, n_vec_rows)
    flat = tot.reshape(-1)
    o0 = 0
    g_pre_mix_g = flat[o0:o0 + D]; o0 += D
    g_post_mix_g = flat[o0:o0 + D]; o0 += D
    g_pre_mlp_g = flat[o0:o0 + D]; o0 += D
    g_post_mlp_g = flat[o0:o0 + D]; o0 += D
    g_gn = flat[o0:o0 + RET_WIDTH]; o0 += RET_WIDTH
    nrpb = NA_HEADS * (2 * NA_KH - 1) * 32
    g_rpb = flat[o0:o0 + nrpb].reshape(NA_HEADS, 2 * NA_KH - 1, 32)[:, :, :ncls]; o0 += nrpb
    g_decay = flat[o0:o0 + 2 * RET_HEADS].reshape(2, RET_HEADS); o0 += LANES
    loss = flat[o0]

    grads = {
        "c_ctx": g_c_ctx.reshape(c_ctx.shape), "w_ada": g_w_ada[None], "b_ada": g_b_ada.reshape(b_ada.shape),
        "g_pre_mix": g_pre_mix_g[None], "g_post_mix": g_post_mix_g[None], "g_pre_mlp": g_pre_mlp_g[None],
        "g_post_mlp": g_post_mlp_g[None], "w_in": fused["w_in"][0][None], "ret_decay": g_decay[None], "ret_gn": g_gn[None],
        "na_rpb": g_rpb[None], "w_out": fused["w_out"][0][None], "w_mlp1": fused["w_mlp1"][0][None],
        "w_mlp2": fused["w_mlp2"][0][None],
    }
    weights = dict(c_ctx=c_ctx, w_ada=w_ada, b_ada=b_ada, g_pre_mix=g_pre_mix, g_post_mix=g_post_mix,
                   g_pre_mlp=g_pre_mlp, g_post_mlp=g_post_mlp, w_in=w_in, ret_decay=ret_decay, ret_gn=ret_gn,
                   na_rpb=na_rpb, w_out=w_out, w_mlp1=w_mlp1, w_mlp2=w_mlp2)
    m_in = dict(c_ctx=m_c_ctx, w_ada=m_w_ada, b_ada=m_b_ada, g_pre_mix=m_g_pre_mix, g_post_mix=m_g_post_mix,
                g_pre_mlp=m_g_pre_mlp, g_post_mlp=m_g_post_mlp, w_in=m_w_in, ret_decay=m_ret_decay,
                ret_gn=m_ret_gn, na_rpb=m_na_rpb, w_out=m_w_out, w_mlp1=m_w_mlp1, w_mlp2=m_w_mlp2)
    v_in = dict(c_ctx=v_c_ctx, w_ada=v_w_ada, b_ada=v_b_ada, g_pre_mix=v_g_pre_mix, g_post_mix=v_g_post_mix,
                g_pre_mlp=v_g_pre_mlp, g_post_mlp=v_g_post_mlp, w_in=v_w_in, ret_decay=v_ret_decay,
                ret_gn=v_ret_gn, na_rpb=v_na_rpb, w_out=v_w_out, w_mlp1=v_w_mlp1, w_mlp2=v_w_mlp2)
    names = list(weights)
    deltas, new_m, new_v = {}, {}, {}
    def as_2d(n):
        shp = weights[n].shape
        two_d = (-1, shp[-1]) if len(shp) > 1 else (1, shp[0])
        return [a.reshape(two_d) for a in (weights[n], grads[n], m_in[n], v_in[n])]

    small = [n for n in names if n not in fused and weights[n].size <= 65536]
    updated = dict(zip(small, _adamw_small([as_2d(n) for n in small], "adamw_small")))
    for n in names:
        if n in fused:
            updated[n] = fused[n][1:]
        elif n not in updated:
            updated[n] = _adamw(*as_2d(n), "adamw_" + n)
        deltas[n], new_m[n], new_v[n] = (a.reshape(weights[n].shape) for a in updated[n])
    return (loss, grad_x, *[grads[n] for n in names], *[deltas[n] for n in names],
            *[new_m[n] for n in names], *[new_v[n] for n in names])
```

```python
import functools
import math

import numpy as np
import jax
import jax.numpy as jnp
from jax import lax
from jax.experimental import pallas as pl
from jax.experimental.pallas import tpu as pltpu

F32 = jnp.float32
BF16 = jnp.bfloat16
MESH = pl.DeviceIdType.MESH

N_DEV = 8
LANES = 128
SUBLANES = 8
VMEM_LIMIT = 60 * 1024 * 1024

GRID_W = 64
RET_HEADS = 4
RET_DIM = 128
RET_WIDTH = RET_HEADS * RET_DIM
NA_HEADS = 8
NA_DIM = 64
NA_WIDTH = NA_HEADS * NA_DIM
NA_PAIRS = NA_HEADS // 2
NA_KH = 8
NA_KW = 16
NA_GROUP = 8
SEG = 512
ROPE_BASE = 10000.0
NORM_EPS = 1e-6
NEG_INF = -1e30
N_MOD = 6

ADAM_LR = 0.001
ADAM_B1 = 0.9
ADAM_B2 = 0.999
ADAM_EPS = 1e-08
ADAM_WD = 0.01
ADAM_STEP = 10


def _dot(a, b):
    return lax.dot_general(a, b, (((1,), (0,)), ((), ())), preferred_element_type=F32)


def _dot_nt(a, b):
    return lax.dot_general(a, b, (((1,), (1,)), ((), ())), preferred_element_type=F32)


def _dot_tn(a, b):
    return lax.dot_general(a, b, (((0,), (0,)), ((), ())), preferred_element_type=F32)


def _sigmoid(x):
    return 1.0 / (1.0 + jnp.exp(-x))


def _div_tile(n, cap, mult):
    if n <= cap:
        return n
    for t in range(cap - cap % mult, 0, -mult):
        if n % t == 0:
            return t
    raise ValueError(f"no tile for {n}")


def _params(*sem):
    return pltpu.CompilerParams(dimension_semantics=tuple(sem) if sem else None,
                                vmem_limit_bytes=VMEM_LIMIT)


def _vmem():
    return pl.BlockSpec(memory_space=pltpu.VMEM)


def _any():
    return pl.BlockSpec(memory_space=pl.ANY)


def _me_and_peers():
    x, y, c = lax.axis_index("x"), lax.axis_index("y"), lax.axis_index("c")
    me = 4 * x + 2 * y + c
    peers = []
    for m in range(1, N_DEV):
        px = 1 - x if (m >> 2) & 1 else x
        py = 1 - y if (m >> 1) & 1 else y
        pc = 1 - c if m & 1 else c
        peers.append(((px, py, pc), 4 * px + 2 * py + pc))
    return me, peers


def _exchange(src_for, dst_from, send_sems, recv_sems):
    me, peers = _me_and_peers()
    sent = []
    for i, (dev, pid) in enumerate(peers):
        cp = pltpu.make_async_remote_copy(src_ref=src_for(pid), dst_ref=dst_from(me),
                                          send_sem=send_sems.at[i], recv_sem=recv_sems.at[i],
                                          device_id=dev, device_id_type=MESH)
        cp.start()
        sent.append(cp)
    for i, (dev, pid) in enumerate(peers):
        pltpu.make_async_remote_copy(src_ref=src_for(pid), dst_ref=dst_from(pid),
                                     send_sem=send_sems.at[i], recv_sem=recv_sems.at[i],
                                     device_id=dev, device_id_type=MESH).wait_recv()
    for cp in sent:
        cp.wait_send()


SIBLING = (1,)
ICI_SAME_CORE = (2, 4, 6)
ALL_PEERS = tuple(range(1, N_DEV))


def _remote(src, dst, send_sem, recv_sem, dev):
    return pltpu.make_async_remote_copy(src_ref=src, dst_ref=dst, send_sem=send_sem, recv_sem=recv_sem,
                                        device_id=dev, device_id_type=MESH)


def _push_start(items, masks, send_sems, recv_sems):
    me, peers = _me_and_peers()
    for k, (src_for, dst_from) in enumerate(items):
        for m in masks:
            dev, pid = peers[m - 1]
            _remote(src_for(pid), dst_from(me), send_sems.at[k, m - 1], recv_sems.at[k, m - 1], dev).start()


def _push_wait_recv(items, masks, send_sems, recv_sems):
    me, peers = _me_and_peers()
    for k, (src_for, dst_from) in enumerate(items):
        for m in masks:
            dev, pid = peers[m - 1]
            _remote(src_for(pid), dst_from(pid), send_sems.at[k, m - 1], recv_sems.at[k, m - 1], dev).wait_recv()


def _push_wait_send(items, masks, send_sems, recv_sems):
    me, peers = _me_and_peers()
    for k, (src_for, dst_from) in enumerate(items):
        for m in masks:
            dev, pid = peers[m - 1]
            _remote(src_for(pid), dst_from(me), send_sems.at[k, m - 1], recv_sems.at[k, m - 1], dev).wait_send()


def _forward_start(items, send_sems, recv_sems):
    me, peers = _me_and_peers()
    sib = peers[0][0]
    for k, (blk_in, blk_out) in enumerate(items):
        for j, m in enumerate(ICI_SAME_CORE):
            pid = peers[m - 1][1]
            _remote(blk_in(pid), blk_out(pid), send_sems.at[k, j], recv_sems.at[k, j], sib).start()


def _forward_wait(items, send_sems, recv_sems):
    me, peers = _me_and_peers()
    sib = peers[0][0]
    for k, (blk_in, blk_out) in enumerate(items):
        for j, m in enumerate(ICI_SAME_CORE):
            got = peers[(m | 1) - 1][1]
            _remote(blk_in(got), blk_out(got), send_sems.at[k, j], recv_sems.at[k, j], sib).wait_recv()
    for k, (blk_in, blk_out) in enumerate(items):
        for j, m in enumerate(ICI_SAME_CORE):
            pid = peers[m - 1][1]
            _remote(blk_in(pid), blk_out(pid), send_sems.at[k, j], recv_sems.at[k, j], sib).wait_send()


def _mod_gather(c, c_ctx, w_ada, b_ada, w_in_t, w_out, w1, w2):
    B, D = c.shape
    ncol = w_ada.shape[1]
    rows = SUBLANES * N_DEV + SUBLANES

    def body(c_ref, cc_ref, w_ref, b_ref, win_ref, wout_ref, w1_ref, w2_ref,
             s_ref, m_ref, gin_ref, wout_b, w1_b, w2_b,
             win_b, msend, send1, recv1, send2, recv2, wsend, wrecv, fsend, frecv, lsem):
        me, _ = _me_and_peers()
        win_b[...] = win_ref[...].astype(BF16)
        block = _row_block(gin_ref, w_in_t.shape[0])
        gather = [(lambda p: win_b, block)]
        own = pltpu.make_async_copy(win_b, block(me), lsem.at[0])
        cv = c_ref[...]
        slot = jnp.concatenate([cv * _sigmoid(cv), jnp.zeros((SUBLANES - B, D), F32)], axis=0)
        my_rows = pl.ds(pl.multiple_of(me * SUBLANES, SUBLANES), SUBLANES)
        s_ref[my_rows, :] = slot
        ccv = cc_ref[...]
        s_ref[SUBLANES * N_DEV:, :] = jnp.concatenate(
            [ccv * _sigmoid(ccv), jnp.zeros((SUBLANES - 1, D), F32)], axis=0)

        def rows_of(p):
            return s_ref.at[pl.ds(pl.multiple_of(p * SUBLANES, SUBLANES), SUBLANES), :]

        _exchange(lambda p: rows_of(me), rows_of, send1, recv1)
        own.start()
        _push_start(gather, SIBLING + ICI_SAME_CORE, wsend, wrecv)
        wout_b[...] = wout_ref[...].astype(BF16)
        w1_b[...] = w1_ref[...].astype(BF16)
        w2_b[...] = w2_ref[...].astype(BF16)
        b_loc = b_ref[:, pl.ds(pl.multiple_of(me * ncol, ncol), ncol)]
        mods = _dot(s_ref[...], w_ref[...]) + b_loc
        for p in range(N_DEV):
            msend[p] = jnp.concatenate([mods[p * SUBLANES:(p + 1) * SUBLANES], mods[N_DEV * SUBLANES:]], axis=0)
        m_ref[me] = msend[me]
        columns = [(lambda p: msend.at[p], lambda p: m_ref.at[p])]
        _push_start(columns, ALL_PEERS, send2, recv2)
        _push_wait_recv(gather, ICI_SAME_CORE, wsend, wrecv)
        relay = [(block, block)]
        _forward_start(relay, fsend, frecv)
        _push_wait_recv(columns, ALL_PEERS, send2, recv2)
        _push_wait_recv(gather, SIBLING, wsend, wrecv)
        _forward_wait(relay, fsend, frecv)
        _push_wait_send(columns, ALL_PEERS, send2, recv2)
        _push_wait_send(gather, SIBLING + ICI_SAME_CORE, wsend, wrecv)
        own.wait()

    return pl.pallas_call(
        body, name="mod_gather",
        out_shape=(jax.ShapeDtypeStruct((rows, D), F32), jax.ShapeDtypeStruct((N_DEV, 2 * SUBLANES, ncol), F32),
                   jax.ShapeDtypeStruct((N_DEV * w_in_t.shape[0], D), BF16),
                   jax.ShapeDtypeStruct(w_out.shape, BF16), jax.ShapeDtypeStruct(w1.shape, BF16),
                   jax.ShapeDtypeStruct(w2.shape, BF16)),
        in_specs=[_vmem()] * 8, out_specs=(_vmem(), _vmem(), _any(), _vmem(), _vmem(), _vmem()),
        scratch_shapes=[pltpu.VMEM(w_in_t.shape, BF16), pltpu.VMEM((N_DEV, 2 * SUBLANES, ncol), F32)]
                       + [pltpu.SemaphoreType.DMA((N_DEV - 1,))] * 2
                       + [pltpu.SemaphoreType.DMA((1, N_DEV - 1))] * 4 + [pltpu.SemaphoreType.DMA((1, 3))] * 2
                       + [pltpu.SemaphoreType.DMA((1,))],
        compiler_params=pltpu.CompilerParams(vmem_limit_bytes=VMEM_LIMIT),
    )(c, c_ctx.reshape(1, D), w_ada, b_ada, w_in_t, w_out, w1, w2)


def _row_block(ref, rows):
    return lambda p: ref.at[pl.ds(pl.multiple_of(p * rows, 2 * SUBLANES), rows), :]


def _col_block(ref, cols):
    return lambda p: ref.at[:, pl.ds(pl.multiple_of(p * cols, LANES), cols)]


def _slot(ref):
    return lambda p: ref.at[p]


def _grid_call(body, *, name, grid, out_shape, in_specs, out_specs, scratch_shapes, args):
    return pl.pallas_call(
        body, name=name, grid=grid, out_shape=tuple(out_shape), in_specs=list(in_specs), out_specs=tuple(out_specs),
        scratch_shapes=list(scratch_shapes), compiler_params=_params(*(("arbitrary",) * len(grid))),
    )(*args)


def _token_tiles(n_ctx, tm):
    nct = n_ctx // tm

    def ctx_spec(D):
        return pl.BlockSpec((None, tm, D), lambda b, t: (b, jnp.minimum(t, nct - 1), 0))

    def lat_spec(D):
        return pl.BlockSpec((None, tm, D), lambda b, t: (b, jnp.maximum(t - nct, 0), 0))

    return nct, ctx_spec, lat_spec


def _inproj_fwd(x, ctx, modl, g1, w_in_t):
    B, N, D = x.shape
    n_ctx = ctx.shape[1]
    T = n_ctx + N
    nw = w_in_t.shape[0]
    tm = _div_tile(n_ctx, 256, 16)
    nct, ctx_spec, lat_spec = _token_tiles(n_ctx, tm)

    def body(c_ref, x_ref, sh_ref, sc_ref, g_ref, w_ref, h_ref, p_ref):
        x = jnp.where(pl.program_id(1) < nct, c_ref[...], x_ref[...])
        r = lax.rsqrt(jnp.mean(x * x, axis=-1, keepdims=True) + NORM_EPS)
        h = ((x * r) * g_ref[...]) * (1.0 + sc_ref[...]) + sh_ref[...]
        hb = h.astype(BF16)
        h_ref[...] = hb
        p_ref[...] = _dot_nt(hb, w_ref[...]).astype(BF16)

    def mrow(b, t):
        return jnp.where(t < nct, B, b)

    return _grid_call(
        body, name="inproj_fwd", grid=(B, T // tm),
        out_shape=(jax.ShapeDtypeStruct((B, T, D), BF16), jax.ShapeDtypeStruct((B, T, nw), BF16)),
        in_specs=[ctx_spec(D), lat_spec(D),
                  pl.BlockSpec((None, None, 1, D), lambda b, t: (mrow(b, t), 0, 0, 0)),
                  pl.BlockSpec((None, None, 1, D), lambda b, t: (mrow(b, t), 1, 0, 0)),
                  pl.BlockSpec((1, D), lambda b, t: (0, 0)),
                  pl.BlockSpec((nw, D), lambda b, t: (0, 0))],
        out_specs=(pl.BlockSpec((None, tm, D), lambda b, t: (b, t, 0)),
                   pl.BlockSpec((None, tm, nw), lambda b, t: (b, t, 0))),
        scratch_shapes=[], args=(ctx, x, modl, modl, g1, w_in_t))


def _swap32(x):
    lane = lax.broadcasted_iota(jnp.int32, x.shape, 1)
    return jnp.where((lane % 64) < 32, pltpu.roll(x, 96, 1), pltpu.roll(x, 32, 1))


def _rope(x, cos, sin):
    return x * cos + _swap32(x) * sin


def _unrope(dy, cos, sin):
    return dy * cos + _swap32(dy * sin)


def _ret_weights(lgf, lgb, dist):
    return jnp.exp(jnp.where(dist >= 0.0, lgf * dist, -lgb * dist))


class _RetDecay:
    def __init__(self, lgf, lgb, rows):
        r = lax.broadcasted_iota(jnp.int32, (rows, RET_DIM), 0).astype(F32)
        self.head = r + 1.0
        self.tail = (rows - 1.0) - r
        self.q_f = jnp.exp(lgf * self.head)
        self.k_f = jnp.exp(lgf * self.tail)
        self.q_b = jnp.exp(lgb * self.tail)
        self.k_b = jnp.exp(lgb * self.head)


def _ret_states(kf32, vs, lgf, lgb, C, c, nt, hf, hb, hfa=None, hba=None):
    dec = _RetDecay(lgf, lgb, c)
    dec_c = _RetDecay(lgf, lgb, C)
    step_f = jnp.exp(jnp.zeros((RET_DIM, RET_DIM), F32) + lgf * c)
    step_b = jnp.exp(jnp.zeros((RET_DIM, RET_DIM), F32) + lgb * c)

    def upd(rows, kdec):
        return _dot_tn((kf32[rows, :] * kdec).astype(BF16), vs[rows, :])

    def lat(t):
        return slice(C + t * c, C + (t + 1) * c)

    state = upd(slice(0, C), dec_c.k_f)
    aged = jnp.zeros_like(state)
    for t in range(nt):
        hf[t] = state.astype(BF16)
        if hfa is not None:
            hfa[t] = aged
        if t < nt - 1:
            aged = step_f * (aged + c * state)
            state = step_f * state + upd(lat(t), dec.k_f)
    state = upd(slice(0, C), dec_c.k_b)
    aged = jnp.zeros_like(state)
    for t in range(nt - 1, -1, -1):
        hb[t] = state.astype(BF16)
        if hba is not None:
            hba[t] = aged
        if t > 0:
            aged = step_b * (aged + c * state)
            state = step_b * state + upd(lat(t), dec.k_b)
    return dec, dec_c, step_f, step_b


def _ret_fwd(proj, cos, sin, lg, gn, n_ctx):
    B, T, _ = proj.shape
    C = n_ctx
    N = T - C
    c = _div_tile(N, 256, 16)
    nt = N // c
    scale = RET_DIM ** -0.5

    def body(lg_ref, q_ref, k_ref, v_ref, g_ref, cos_ref, sin_ref, gn_ref, o_ref, lat_ref, qs, ks, vs, kf32, hf, hb):
        h = pl.program_id(1)
        lgf = lg_ref[0, h]
        lgb = lg_ref[1, h]
        for rows in [slice(0, C)] + [slice(C + t * c, C + (t + 1) * c) for t in range(nt)]:
            cosb = cos_ref[rows, :]
            sinb = sin_ref[rows, :]
            qs[rows, :] = (_rope(q_ref[rows, :].astype(F32), cosb, sinb) * scale).astype(BF16)
            kr = _rope(k_ref[rows, :].astype(F32), cosb, sinb)
            kf32[rows, :] = kr
            ks[rows, :] = kr.astype(BF16)
            vs[rows, :] = v_ref[rows, :].astype(BF16)
        gnv = gn_ref[...]
        dec, _, _, _ = _ret_states(kf32, vs, lgf, lgb, C, c, nt, hf, hb)
        rc = (lax.broadcasted_iota(jnp.int32, (c, c), 0) - lax.broadcasted_iota(jnp.int32, (c, c), 1)).astype(F32)
        w_diag = _ret_weights(lgf, lgb, rc)
        for t in range(nt):
            rows = slice(C + t * c, C + (t + 1) * c)
            qt = qs[rows, :]
            s = _dot_nt(qt, ks[rows, :])
            o = (_dot((s * w_diag).astype(BF16), vs[rows, :])
                 + dec.q_f * _dot(qt, hf[t]) + dec.q_b * _dot(qt, hb[t]))
            o_ref[t * c:(t + 1) * c, :] = o
            mu = jnp.mean(o, axis=-1, keepdims=True)
            oc = o - mu
            var = jnp.mean(oc * oc, axis=-1, keepdims=True)
            yh = oc * lax.rsqrt(var + NORM_EPS)
            g = g_ref[rows, :].astype(F32)
            lat_ref[t * c:(t + 1) * c, :] = ((yh * gnv) * (g * _sigmoid(g))).astype(BF16)

    def col(seg):
        return pl.BlockSpec((None, T, RET_DIM), lambda b, h, seg=seg: (b, 0, seg * RET_HEADS + h))

    return _grid_call(
        body, name="ret_fwd", grid=(B, RET_HEADS),
        out_shape=(jax.ShapeDtypeStruct((B, N, RET_WIDTH), F32), jax.ShapeDtypeStruct((B, N, RET_WIDTH), BF16)),
        in_specs=[pl.BlockSpec(memory_space=pltpu.SMEM), col(0), col(1), col(2), col(3),
                  pl.BlockSpec((T, RET_DIM), lambda b, h: (0, 0)), pl.BlockSpec((T, RET_DIM), lambda b, h: (0, 0)),
                  pl.BlockSpec((1, RET_DIM), lambda b, h: (0, h))],
        out_specs=(pl.BlockSpec((None, N, RET_DIM), lambda b, h: (b, 0, h)),
                   pl.BlockSpec((None, N, RET_DIM), lambda b, h: (b, 0, h))),
        scratch_shapes=[pltpu.VMEM((T, RET_DIM), BF16)] * 3 + [pltpu.VMEM((T, RET_DIM), F32)]
                       + [pltpu.VMEM((nt, RET_DIM, RET_DIM), BF16)] * 2,
        args=(lg, proj, proj, proj, proj, cos, sin, gn))


def _ret_bwd(proj, cos, sin, lg, gn, o, dlat, n_ctx):
    B, T, _ = proj.shape
    C = n_ctx
    N = T - C
    c = _div_tile(N, 256, 16)
    nt = N // c
    scale = RET_DIM ** -0.5

    def lat(t):
        return slice(C + t * c, C + (t + 1) * c)

    def body(lg_ref, q_ref, k_ref, v_ref, g_ref, cos_ref, sin_ref, gn_ref, o_ref, dl_ref,
             d_ref, dgn_ref, dlg_ref, qs, ks, vs, dos, qf32, kf32, hf, hb, hfa, hba, gf_s, gb_s):
        h = pl.program_id(1)
        lgf = lg_ref[0, h]
        lgb = lg_ref[1, h]
        gnv = gn_ref[...]

        def fold(a):
            return jnp.sum(a.reshape(a.shape[0] // SUBLANES, SUBLANES, a.shape[1]), axis=0)

        for rows in [slice(0, C)] + [lat(t) for t in range(nt)]:
            cosb = cos_ref[rows, :]
            sinb = sin_ref[rows, :]
            qr = _rope(q_ref[rows, :].astype(F32), cosb, sinb) * scale
            qf32[rows, :] = qr
            qs[rows, :] = qr.astype(BF16)
            kr = _rope(k_ref[rows, :].astype(F32), cosb, sinb)
            kf32[rows, :] = kr
            ks[rows, :] = kr.astype(BF16)
            vs[rows, :] = v_ref[rows, :].astype(BF16)

        dgn = jnp.zeros((1, RET_DIM), F32)
        for t in range(nt):
            lrows = slice(t * c, (t + 1) * c)
            ov = o_ref[lrows, :]
            mu = jnp.mean(ov, axis=-1, keepdims=True)
            oc = ov - mu
            var = jnp.mean(oc * oc, axis=-1, keepdims=True)
            rstd = lax.rsqrt(var + NORM_EPS)
            yh = oc * rstd
            g = g_ref[lat(t), :].astype(F32)
            sg = _sigmoid(g)
            dl = dl_ref[lrows, :]
            d_ref[3, lat(t), :] = (dl * (yh * gnv) * (sg * (1.0 + g * (1.0 - sg)))).astype(BF16)
            dls = dl * (g * sg)
            dgn = dgn + jnp.sum(dls * yh, axis=0, keepdims=True)
            dyh = dls * gnv
            do = rstd * (dyh - jnp.mean(dyh, axis=-1, keepdims=True)
                         - yh * jnp.mean(dyh * yh, axis=-1, keepdims=True))
            dos[lrows, :] = do.astype(BF16)
        dgn_ref[...] = jnp.concatenate([dgn, jnp.zeros((SUBLANES - 1, RET_DIM), F32)], axis=0)
        d_ref[3, 0:C, :] = jnp.zeros((C, RET_DIM), BF16)
        d_ref[0, 0:C, :] = jnp.zeros((C, RET_DIM), BF16)

        dec, dec_c, step_f, step_b = _ret_states(kf32, vs, lgf, lgb, C, c, nt, hf, hb, hfa, hba)

        def zmat(t, qdec):
            return _dot_tn((qf32[lat(t), :] * qdec).astype(BF16), dos[t * c:(t + 1) * c, :])

        acc3f = jnp.zeros((RET_DIM, RET_DIM), F32)
        acc3b = jnp.zeros((RET_DIM, RET_DIM), F32)
        state = jnp.zeros((RET_DIM, RET_DIM), F32)
        for t in range(nt - 1, -1, -1):
            gf_s[t] = state.astype(BF16)
            z = zmat(t, dec.q_f)
            acc3f = acc3f + hfa[t] * z
            state = step_f * state + z
        gctx_f = state.astype(BF16)
        state = jnp.zeros((RET_DIM, RET_DIM), F32)
        for t in range(nt):
            gb_s[t] = state.astype(BF16)
            z = zmat(t, dec.q_b)
            acc3b = acc3b + hba[t] * z
            state = step_b * state + z
        gctx_b = state.astype(BF16)

        rc = (lax.broadcasted_iota(jnp.int32, (c, c), 0) - lax.broadcasted_iota(jnp.int32, (c, c), 1)).astype(F32)
        w_diag = _ret_weights(lgf, lgb, rc)
        wg_f = jnp.where(rc >= 0.0, w_diag * rc, 0.0)
        wg_b = jnp.where(rc < 0.0, -w_diag * rc, 0.0)
        accf = jnp.zeros((SUBLANES, RET_DIM), F32)
        accb = jnp.zeros((SUBLANES, RET_DIM), F32)
        gdf = jnp.zeros((SUBLANES, c), F32)
        gdb = jnp.zeros((SUBLANES, c), F32)
        for t in range(nt):
            rows = lat(t)
            qt = qs[rows, :]
            kt = ks[rows, :]
            vt = vs[rows, :]
            dot = dos[t * c:(t + 1) * c, :]
            s = _dot_nt(qt, kt)
            dp = _dot_nt(dot, vt)
            dv = _dot_tn((s * w_diag).astype(BF16), dot)
            ds = (dp * w_diag).astype(BF16)
            dq = _dot(ds, kt)
            dk = _dot_tn(ds, qt)
            gs = dp * s
            gdf = gdf + fold(gs * wg_f)
            gdb = gdb + fold(gs * wg_b)
            qv = qf32[rows, :]
            kv = kf32[rows, :]
            dq_f = dec.q_f * _dot_nt(dot, hf[t])
            dq_b = dec.q_b * _dot_nt(dot, hb[t])
            dk_f = dec.k_f * _dot_nt(vt, gf_s[t])
            dk_b = dec.k_b * _dot_nt(vt, gb_s[t])
            accf = accf + fold(dec.head * dq_f * qv) + fold(dec.tail * dk_f * kv)
            accb = accb + fold(dec.tail * dq_b * qv) + fold(dec.head * dk_b * kv)
            dv = dv + dec.k_f * _dot(kt, gf_s[t]) + dec.k_b * _dot(kt, gb_s[t])
            cosb = cos_ref[rows, :]
            sinb = sin_ref[rows, :]
            d_ref[0, rows, :] = _unrope((dq + dq_f + dq_b) * scale, cosb, sinb).astype(BF16)
            d_ref[1, rows, :] = _unrope(dk + dk_f + dk_b, cosb, sinb).astype(BF16)
            d_ref[2, rows, :] = dv.astype(BF16)
        kc = ks[0:C, :]
        vc = vs[0:C, :]
        kcv = kf32[0:C, :]
        dkc_f = dec_c.k_f * _dot_nt(vc, gctx_f)
        dkc_b = dec_c.k_b * _dot_nt(vc, gctx_b)
        accf = accf + fold(dec_c.tail * dkc_f * kcv)
        accb = accb + fold(dec_c.head * dkc_b * kcv)
        d_ref[1, 0:C, :] = (dkc_f + dkc_b).astype(BF16)
        d_ref[2, 0:C, :] = (dec_c.k_f * _dot(kc, gctx_f) + dec_c.k_b * _dot(kc, gctx_b)).astype(BF16)
        gf = jnp.sum(gdf) + jnp.sum(accf) + jnp.sum(acc3f)
        gb = jnp.sum(gdb) + jnp.sum(accb) + jnp.sum(acc3b)
        row = lax.broadcasted_iota(jnp.int32, (SUBLANES, LANES), 0)
        dlg_ref[...] = jnp.where(row == 0, gf, jnp.where(row == 1, gb, 0.0))

    def col(seg):
        return pl.BlockSpec((None, T, RET_DIM), lambda b, h, seg=seg: (b, 0, seg * RET_HEADS + h))

    return _grid_call(
        body, name="ret_bwd", grid=(B, RET_HEADS),
        out_shape=(jax.ShapeDtypeStruct((B, 4, T, RET_WIDTH), BF16),
                   jax.ShapeDtypeStruct((B, SUBLANES, RET_WIDTH), F32),
                   jax.ShapeDtypeStruct((B, RET_HEADS, SUBLANES, LANES), F32)),
        in_specs=[pl.BlockSpec(memory_space=pltpu.SMEM), col(0), col(1), col(2), col(3),
                  pl.BlockSpec((T, RET_DIM), lambda b, h: (0, 0)), pl.BlockSpec((T, RET_DIM), lambda b, h: (0, 0)),
                  pl.BlockSpec((1, RET_DIM), lambda b, h: (0, h)),
                  pl.BlockSpec((None, N, RET_DIM), lambda b, h: (b, 0, h)),
                  pl.BlockSpec((None, N, RET_DIM), lambda b, h: (b, 0, h))],
        out_specs=(pl.BlockSpec((None, 4, T, RET_DIM), lambda b, h: (b, 0, 0, h)),
                   pl.BlockSpec((None, SUBLANES, RET_DIM), lambda b, h: (b, 0, h)),
                   pl.BlockSpec((None, None, SUBLANES, LANES), lambda b, h: (b, h, 0, 0))),
        scratch_shapes=[pltpu.VMEM((T, RET_DIM), BF16)] * 3 + [pltpu.VMEM((N, RET_DIM), BF16)]
                       + [pltpu.VMEM((T, RET_DIM), F32)] * 2
                       + [pltpu.VMEM((nt, RET_DIM, RET_DIM), BF16)] * 2 + [pltpu.VMEM((nt, RET_DIM, RET_DIM), F32)] * 2
                       + [pltpu.VMEM((nt, RET_DIM, RET_DIM), BF16)] * 2,
        args=(lg, proj, proj, proj, proj, cos, sin, gn, o, dlat))


def _na_geometry(rows):
    kh = min(NA_KH, rows)
    return kh, kh * GRID_W


def _pair_select():
    lane = lax.broadcasted_iota(jnp.int32, (2 * GRID_W, LANES), 1)
    row = lax.broadcasted_iota(jnp.int32, (2 * GRID_W, LANES), 0)
    return (lane >= NA_DIM) == (row >= GRID_W)


def _pair_bias(bias_ref, dr0, kh):
    return jnp.concatenate(
        [jnp.concatenate([bias_ref[e, pl.ds(dr0 + 2 * m, 1)].reshape(GRID_W, LANES) for m in range(kh // 2)], axis=1)
         for e in range(2)], axis=0)


def _na_softmax(s_loc, s_ctx):
    mx = jnp.maximum(jnp.max(s_loc, axis=-1, keepdims=True), jnp.max(s_ctx, axis=-1, keepdims=True))
    p_loc = jnp.exp(s_loc - mx)
    p_ctx = jnp.exp(s_ctx - mx)
    den = jnp.sum(p_loc, axis=-1, keepdims=True) + jnp.sum(p_ctx, axis=-1, keepdims=True)
    return p_loc, p_ctx, den


def _na_fwd(proj, bias2, n_ctx):
    B, T, _ = proj.shape
    C = n_ctx
    N = T - C
    R = N // GRID_W
    kh, nk = _na_geometry(R)
    scale = NA_DIM ** -0.5
    base = (4 * RET_WIDTH) // LANES

    def body(q_ref, k_ref, v_ref, bias_ref, out_ref, kb16, vb16):
        kb16[...] = k_ref[...].astype(BF16)
        vb16[...] = v_ref[...].astype(BF16)
        kc = kb16[0:C, :]
        vc = vb16[0:C, :]
        lane = lax.broadcasted_iota(jnp.int32, (GRID_W, LANES), 1)
        sel2 = _pair_select()

        def group(gi, carry):
            pre = []
            for u in range(NA_GROUP):
                r = gi * NA_GROUP + u
                bs = jnp.clip(r - kh // 2, 0, R - kh)
                dr0 = bs - r + (NA_KH - 1)
                q = q_ref[pl.ds(pl.multiple_of(C + r * GRID_W, GRID_W), GRID_W), :].astype(F32) * scale
                q2 = jnp.where(sel2, jnp.concatenate([q, q], axis=0), 0.0).astype(BF16)
                band = pl.ds(pl.multiple_of(C + bs * GRID_W, GRID_W), nk)
                s_loc = _dot_nt(q2, kb16[band, :]) + _pair_bias(bias_ref, dr0, kh)
                s_ctx = _dot_nt(q2, kc)
                pre.append((r, band, s_loc, s_ctx))
            mid = [(r, band) + _na_softmax(s_loc, s_ctx) for r, band, s_loc, s_ctx in pre]
            for r, band, p_loc, p_ctx, den in mid:
                o2 = (_dot(p_loc.astype(BF16), vb16[band, :]) + _dot(p_ctx.astype(BF16), vc)) / den
                out_ref[pl.ds(pl.multiple_of(r * GRID_W, GRID_W), GRID_W), :] = jnp.where(
                    lane < NA_DIM, o2[:GRID_W], o2[GRID_W:]).astype(BF16)
            return carry

        lax.fori_loop(0, R // NA_GROUP, group, 0)

    def col(seg):
        return pl.BlockSpec((None, T, LANES), lambda b, p, seg=seg: (b, 0, base + seg * NA_PAIRS + p))

    return _grid_call(
        body, name="na_fwd", grid=(B, NA_PAIRS),
        out_shape=(jax.ShapeDtypeStruct((B, N, NA_WIDTH), BF16),),
        in_specs=[col(0), col(1), col(2),
                  pl.BlockSpec((2, 2 * NA_KH - 2, GRID_W, LANES), lambda b, p: (p, 0, 0, 0))],
        out_specs=(pl.BlockSpec((None, N, LANES), lambda b, p: (b, 0, p)),),
        scratch_shapes=[pltpu.VMEM((T, LANES), BF16)] * 2,
        args=(proj, proj, proj, bias2))


def _na_bwd(proj, bias2, dlat, n_ctx):
    B, T, _ = proj.shape
    C = n_ctx
    N = T - C
    R = N // GRID_W
    kh, nk = _na_geometry(R)
    scale = NA_DIM ** -0.5
    base = (4 * RET_WIDTH) // LANES

    def body(q_ref, k_ref, v_ref, bias_ref, dl_ref, d_ref, db_ref, kb16, vb16, dkv):
        b = pl.program_id(1)
        kb16[...] = k_ref[...].astype(BF16)
        vb16[...] = v_ref[...].astype(BF16)
        kc = kb16[0:C, :]
        vc = vb16[0:C, :]
        lane = lax.broadcasted_iota(jnp.int32, (GRID_W, LANES), 1)
        dkv[...] = jnp.zeros(dkv.shape, F32)
        d_ref[0, 0:C, :] = jnp.zeros((C, LANES), BF16)

        @pl.when(b == 0)
        def _():
            db_ref[...] = jnp.zeros(db_ref.shape, F32)

        sel2 = _pair_select()

        def group(gi, carry):
            pre = []
            for u in range(NA_GROUP):
                r = gi * NA_GROUP + u
                bs = jnp.clip(r - kh // 2, 0, R - kh)
                dr0 = bs - r + (NA_KH - 1)
                q = q_ref[pl.ds(pl.multiple_of(C + r * GRID_W, GRID_W), GRID_W), :].astype(F32) * scale
                do = dl_ref[pl.ds(pl.multiple_of(r * GRID_W, GRID_W), GRID_W), :]
                q2 = jnp.where(sel2, jnp.concatenate([q, q], axis=0), 0.0).astype(BF16)
                do2 = jnp.where(sel2, jnp.concatenate([do, do], axis=0), 0.0).astype(BF16)
                band = pl.ds(pl.multiple_of(C + bs * GRID_W, GRID_W), nk)
                s_loc = _dot_nt(q2, kb16[band, :]) + _pair_bias(bias_ref, dr0, kh)
                s_ctx = _dot_nt(q2, kc)
                dp_loc = _dot_nt(do2, vb16[band, :])
                dp_ctx = _dot_nt(do2, vc)
                pre.append((r, dr0, band, q2, do2, s_loc, s_ctx, dp_loc, dp_ctx))
            mid = []
            for r, dr0, band, q2, do2, s_loc, s_ctx, dp_loc, dp_ctx in pre:
                p_loc, p_ctx, den = _na_softmax(s_loc, s_ctx)
                inv = 1.0 / den
                p_loc = p_loc * inv
                p_ctx = p_ctx * inv
                delta = (jnp.sum(p_loc * dp_loc, axis=-1, keepdims=True)
                         + jnp.sum(p_ctx * dp_ctx, axis=-1, keepdims=True))
                ds_loc = p_loc * (dp_loc - delta)
                ds_ctx = p_ctx * (dp_ctx - delta)
                mid.append((r, dr0, band, q2, do2, p_loc.astype(BF16), p_ctx.astype(BF16), ds_loc, ds_ctx))
            for r, dr0, band, q2, do2, pb_loc, pb_ctx, ds_loc, ds_ctx in mid:
                dsb_loc = ds_loc.astype(BF16)
                dsb_ctx = ds_ctx.astype(BF16)
                dq2 = _dot(dsb_loc, kb16[band, :]) + _dot(dsb_ctx, kc)
                d_ref[0, pl.ds(pl.multiple_of(C + r * GRID_W, GRID_W), GRID_W), :] = (jnp.where(
                    lane < NA_DIM, dq2[:GRID_W], dq2[GRID_W:]) * scale).astype(BF16)
                dkv[0, band, :] += _dot_tn(dsb_loc, q2)
                dkv[1, band, :] += _dot_tn(pb_loc, do2)
                dkv[0, 0:C, :] += _dot_tn(dsb_ctx, q2)
                dkv[1, 0:C, :] += _dot_tn(pb_ctx, do2)
                for e in range(2):
                    for m in range(kh // 2):
                        db_ref[e, pl.ds(dr0 + 2 * m, 1)] += ds_loc[e * GRID_W:(e + 1) * GRID_W,
                                                                   m * LANES:(m + 1) * LANES].reshape(1, GRID_W, LANES)
            return carry

        lax.fori_loop(0, R // NA_GROUP, group, 0)
        d_ref[1] = dkv[0].astype(BF16)
        d_ref[2] = dkv[1].astype(BF16)

    def col(seg):
        return pl.BlockSpec((None, T, LANES), lambda p, b, seg=seg: (b, 0, base + seg * NA_PAIRS + p))

    return _grid_call(
        body, name="na_bwd", grid=(NA_PAIRS, B),
        out_shape=(jax.ShapeDtypeStruct((B, 3, T, NA_WIDTH), BF16),
                   jax.ShapeDtypeStruct((NA_HEADS, 2 * NA_KH - 2, GRID_W, LANES), F32)),
        in_specs=[col(0), col(1), col(2),
                  pl.BlockSpec((2, 2 * NA_KH - 2, GRID_W, LANES), lambda p, b: (p, 0, 0, 0)),
                  pl.BlockSpec((None, N, LANES), lambda p, b: (b, 0, p))],
        out_specs=(pl.BlockSpec((None, 3, T, LANES), lambda p, b: (b, 0, 0, p)),
                   pl.BlockSpec((2, 2 * NA_KH - 2, GRID_W, LANES), lambda p, b: (p, 0, 0, 0))),
        scratch_shapes=[pltpu.VMEM((T, LANES), BF16)] * 2 + [pltpu.VMEM((2, T, LANES), F32)],
        args=(proj, proj, proj, bias2, dlat))


def _split3(a):
    hi = a.astype(BF16)
    r1 = a - hi.astype(F32)
    mid = r1.astype(BF16)
    lo = (r1 - mid.astype(F32)).astype(BF16)
    return hi, mid, lo


def _rpb_reduce(dbias2, onehot2):
    rows = dbias2.shape[0] * dbias2.shape[1]
    flat = dbias2.reshape(rows, GRID_W * LANES)

    def body(a_ref, oh_ref, o_ref):
        hi, mid, lo = _split3(a_ref[...])
        oh = oh_ref[...]
        o_ref[...] = _dot(hi, oh) + _dot(mid, oh) + _dot(lo, oh)

    return pl.pallas_call(
        body, name="rpb_reduce", out_shape=jax.ShapeDtypeStruct((rows, LANES), F32),
        in_specs=[_vmem(), _vmem()], out_specs=_vmem(),
        compiler_params=pltpu.CompilerParams(vmem_limit_bytes=VMEM_LIMIT),
    )(flat, onehot2)


def _dense_core(lat_ret, lat_na, x, tgt, modl, g_post_mix, g_pre_mlp, g_post_mlp, w_out, w1, w2):
    B, N, D = x.shape
    F = w1.shape[1]
    wout_rows, w1_cols, w2_rows = w_out.shape[0] // N_DEV, w1.shape[1] // N_DEV, w2.shape[0] // N_DEV
    mixw = w_out.shape[0]
    half = mixw // 2
    tm = _div_tile(N, 256, 16)
    nt = N // tm
    fc = _div_tile(F, 1024, LANES)

    def body(lr_ref, ln_ref, x_ref, t_ref, gt1_ref, sh2_ref, sc2_ref, gt2_ref, gpm_ref, gpre_ref, gpo_ref,
             wout_part, w1_part, w2_part,
             dy1_ref, dlr_ref, dln_ref, dmix_ref, h2_ref, a_ref, du_ref, dz_ref, red_ref, wout_hbm, w1_hbm, w2_hbm,
             wout_v, w1_v, w2_v, u_s, sems, fsend, frecv):
        @pl.when((pl.program_id(0) == 0) & (pl.program_id(1) == 0))
        def _():
            relay = [(_row_block(wout_part, wout_rows), _row_block(wout_hbm, wout_rows)),
                     (_col_block(w1_part, w1_cols), _col_block(w1_hbm, w1_cols)),
                     (_row_block(w2_part, w2_rows), _row_block(w2_hbm, w2_rows))]
            _forward_start(relay, fsend, frecv)
            _forward_wait(relay, fsend, frecv)
            cps = [pltpu.make_async_copy(wout_hbm, wout_v, sems.at[0]),
                   pltpu.make_async_copy(w1_hbm, w1_v, sems.at[1]),
                   pltpu.make_async_copy(w2_hbm, w2_v, sems.at[2])]
            for cp in cps:
                cp.start()
            for cp in cps:
                cp.wait()

        @pl.when(pl.program_id(1) == 0)
        def _():
            red_ref[...] = jnp.zeros(red_ref.shape, F32)

        gt1 = gt1_ref[...]
        sh2 = sh2_ref[...]
        sc2 = sc2_ref[...]
        gt2 = gt2_ref[...]
        gpm = gpm_ref[...]
        gpre = gpre_ref[...]
        gpo = gpo_ref[...]

        def rowmean(a):
            return jnp.mean(a, axis=-1, keepdims=True)

        def colsum(a):
            return jnp.sum(a, axis=0, keepdims=True)

        mix_gain = gt1 * gpm
        mlp_in_gain = gpre * (1.0 + sc2)
        mlp_out_gain = gt2 * gpo
        mix = _dot(lr_ref[...], wout_v[0:half, :]) + _dot(ln_ref[...], wout_v[half:, :])
        x = x_ref[...]
        rm = lax.rsqrt(rowmean(mix * mix) + NORM_EPS)
        mh = mix * rm
        y1 = x + mh * mix_gain
        r1 = lax.rsqrt(rowmean(y1 * y1) + NORM_EPS)
        xh = y1 * r1
        h2b = (xh * mlp_in_gain + sh2).astype(BF16)
        h2_ref[...] = h2b
        z = jnp.zeros((tm, D), F32)
        for c0 in range(0, F, fc):
            u = _dot(h2b, w1_v[:, c0:c0 + fc])
            u_s[:, c0:c0 + fc] = u
            ru = jnp.maximum(u, 0.0)
            ab = (ru * ru).astype(BF16)
            a_ref[:, c0:c0 + fc] = ab
            z = z + _dot(ab, w2_v[c0:c0 + fc, :])
        r2 = lax.rsqrt(rowmean(z * z) + NORM_EPS)
        zh = z * r2
        y2 = y1 + zh * mlp_out_gain
        err = y2 - t_ref[...]
        loss = 0.5 * jnp.sum(rowmean(err * err))
        dy2 = err * (1.0 / D)
        s_out = colsum(dy2 * zh)
        red_ref[2:3, :] += s_out * gpo
        red_ref[6:7, :] += s_out * gt2
        dzh = dy2 * mlp_out_gain
        dz = r2 * (dzh - zh * rowmean(dzh * zh))
        dzb = dz.astype(BF16)
        dz_ref[...] = dzb
        dh2 = jnp.zeros((tm, D), F32)
        for c0 in range(0, F, fc):
            da = _dot_nt(dzb, w2_v[c0:c0 + fc, :])
            dub = (da * (2.0 * jnp.maximum(u_s[:, c0:c0 + fc], 0.0))).astype(BF16)
            du_ref[:, c0:c0 + fc] = dub
            dh2 = dh2 + _dot_nt(dub, w1_v[:, c0:c0 + fc])
        s_in = colsum(dh2 * xh)
        red_ref[3:4, :] += s_in * gpre
        red_ref[4:5, :] += colsum(dh2)
        red_ref[5:6, :] += s_in * (1.0 + sc2)
        dxh = dh2 * mlp_in_gain
        dy1 = dy2 + r1 * (dxh - xh * rowmean(dxh * xh))
        dy1_ref[...] = dy1
        s_mix = colsum(dy1 * mh)
        red_ref[0:1, :] += s_mix * gpm
        red_ref[1:2, :] += s_mix * gt1
        dmh = dy1 * mix_gain
        dmix = (rm *(dmh - mh * rowmean(dmh * mh))).astype(BF16)
        dmix_ref[...] = dmix
        dlr_ref[...] = _dot_nt(dmix, wout_v[0:half, :])
        dln_ref[...] = _dot_nt(dmix, wout_v[half:, :])
        red_ref[7:8, :] += jnp.zeros((1, D), F32) + loss

    def tok(w):
        return pl.BlockSpec((None, tm, w), lambda b, t: (b, t, 0))

    def mod(k):
        return pl.BlockSpec((None, None, 1, D), lambda b, t, k=k: (b, k, 0, 0))

    def vec():
        return pl.BlockSpec((1, D), lambda b, t: (0, 0))

    return pl.pallas_call(
        body, name="dense_core", grid=(B, nt),
        out_shape=(jax.ShapeDtypeStruct((B, N, D), F32), jax.ShapeDtypeStruct((B, N, half), F32),
                   jax.ShapeDtypeStruct((B, N, half), F32), jax.ShapeDtypeStruct((B, N, D), BF16),
                   jax.ShapeDtypeStruct((B, N, D), BF16), jax.ShapeDtypeStruct((B, N, F), BF16),
                   jax.ShapeDtypeStruct((B, N, F), BF16), jax.ShapeDtypeStruct((B, N, D), BF16),
                   jax.ShapeDtypeStruct((B, SUBLANES, D), F32),
                   jax.ShapeDtypeStruct(w_out.shape, w_out.dtype), jax.ShapeDtypeStruct(w1.shape, w1.dtype),
                   jax.ShapeDtypeStruct(w2.shape, w2.dtype)),
        in_specs=[tok(half), tok(half), tok(D), tok(D), mod(2), mod(3), mod(4), mod(5), vec(), vec(), vec(),
                  _any(), _any(), _any()],
        out_specs=(tok(D), tok(half), tok(half), tok(D), tok(D), tok(F), tok(F), tok(D),
                   pl.BlockSpec((None, SUBLANES, D), lambda b, t: (b, 0, 0)), _any(), _any(), _any()),
        scratch_shapes=[pltpu.VMEM((mixw, D), BF16), pltpu.VMEM((D, F), BF16), pltpu.VMEM((F, D), BF16),
                        pltpu.VMEM((tm, F), F32), pltpu.SemaphoreType.DMA((3,)),
                        pltpu.SemaphoreType.DMA((3, 3)), pltpu.SemaphoreType.DMA((3, 3))],
        input_output_aliases={11: 9, 12: 10, 13: 11},
        compiler_params=_params("arbitrary", "arbitrary"),
    )(lat_ret, lat_na, x, tgt, modl, modl, modl, modl, g_post_mix, g_pre_mlp, g_post_mlp, w_out, w1, w2)[:9]


def _inproj_bwd(dret, dna, x, ctx, dy1, modl, g1, w_in_t):
    B, N, D = x.shape
    n_ctx = ctx.shape[1]
    T = n_ctx + N
    tm = _div_tile(n_ctx, 256, 16)
    nct, ctx_spec, lat_spec = _token_tiles(n_ctx, tm)
    nt = T // tm
    nseg_r = dret.shape[1]
    nseg_n = dna.shape[1]
    nw = w_in_t.shape[0]

    def body(*refs):
        seg_refs = refs[:nseg_r + nseg_n]
        c_ref, x_ref, dy1_ref, sc_ref, g_ref, w_ref, dx_ref, red_ref = refs[nseg_r + nseg_n:]
        t = pl.program_id(1)
        dh = jnp.zeros((tm, D), F32)
        for s, ref in enumerate(seg_refs):
            dh = dh + _dot(ref[...], w_ref[s * SEG:(s + 1) * SEG, :])
        x = jnp.where(t < nct, c_ref[...], x_ref[...])
        g = g_ref[...]
        r = lax.rsqrt(jnp.mean(x * x, axis=-1, keepdims=True) + NORM_EPS)
        xh = x * r
        red_ref[0:1, :] = jnp.sum(dh, axis=0, keepdims=True)
        red_ref[1:2, :] = jnp.sum(dh * (xh * g), axis=0, keepdims=True)
        dn = dh * (1.0 + sc_ref[...])
        red_ref[2:3, :] = jnp.sum(dn * xh, axis=0, keepdims=True)
        red_ref[3:, :] = jnp.zeros((SUBLANES - 3, D), F32)
        dxh = dn * g
        dx = r * (dxh - xh * jnp.mean(dxh * xh, axis=-1, keepdims=True))
        dx_ref[...] = dx + jnp.where(t >= nct, dy1_ref[...], 0.0)

    def mrow(b, t):
        return jnp.where(t < nct, B, b)

    def seg(s):
        return pl.BlockSpec((None, None, tm, SEG), lambda b, t, s=s: (b, s, t, 0))

    return _grid_call(
        body, name="inproj_bwd", grid=(B, nt),
        out_shape=(jax.ShapeDtypeStruct((B, N, D), F32), jax.ShapeDtypeStruct((B, nt, SUBLANES, D), F32)),
        in_specs=[seg(s) for s in range(nseg_r)] + [seg(s) for s in range(nseg_n)]
                 + [ctx_spec(D), lat_spec(D), lat_spec(D),
                    pl.BlockSpec((None, None, 1, D), lambda b, t: (mrow(b, t), 1, 0, 0)),
                    pl.BlockSpec((1, D), lambda b, t: (0, 0)),
                    pl.BlockSpec((nw, D), lambda b, t: (0, 0))],
        out_specs=(lat_spec(D), pl.BlockSpec((None, None, SUBLANES, D), lambda b, t: (b, t, 0, 0))),
        scratch_shapes=[], args=(*([dret] * nseg_r), *([dna] * nseg_n), ctx, x, dy1, modl, g1, w_in_t))


def _tn_matmul(lhs, rhs, name, rows_before=0, rows_after=0, into=None):
    B, S, T, W = lhs.shape
    nn = rhs.shape[-1]
    tk = _div_tile(T, 2304, LANES)
    bm = _div_tile(W, 1024, LANES)
    bn = _div_tile(nn, 1024, LANES)
    nkt = T // tk
    nk = B * nkt

    def body(l_ref, r_ref, *rest):
        o_ref, acc = rest[-2:]
        k = pl.program_id(3)

        @pl.when(k == 0)
        def _():
            acc[...] = jnp.zeros(acc.shape, F32)

        acc[...] += _dot_tn(l_ref[...].astype(BF16), r_ref[...].astype(BF16))

        @pl.when(k == nk - 1)
        def _():
            o_ref[...] = acc[...].astype(BF16)

    nwb = W // bm
    first = rows_before // bm
    return pl.pallas_call(
        functools.partial(body), name=name, grid=(S, nwb, nn // bn, nk),
        out_shape=jax.ShapeDtypeStruct((rows_before + S * W + rows_after, nn), BF16),
        in_specs=[pl.BlockSpec((None, None, tk, bm), lambda s, i, j, k: (k // nkt, s, k % nkt, i)),
                  pl.BlockSpec((None, tk, bn), lambda s, i, j, k: (k // nkt, k % nkt, j))]
                 + ([] if into is None else [_any()]),
        out_specs=pl.BlockSpec((bm, bn), lambda s, i, j, k: (first + s * nwb + i, j)),
        scratch_shapes=[pltpu.VMEM((bm, bn), F32)],
        input_output_aliases={} if into is None else {2: 0},
        compiler_params=_params("parallel", "parallel", "parallel", "arbitrary"),
    )(lhs, rhs, *([] if into is None else [into]))


class _SplitScatter:
    def __init__(self, gs, block_ofs, land_shapes, name, kind="scatter", masks=ALL_PEERS):
        self.n = n = len(gs)
        self.block_ofs, self.kind, self.masks = block_ofs, kind, masks
        if kind == "scatter":
            land_shapes = [(N_DEV,) + tuple(bs) for bs in land_shapes]
        hbm = pl.BlockSpec(memory_space=pltpu.HBM)
        sem = pl.BlockSpec(memory_space=pltpu.SEMAPHORE)

        def body(*refs):
            g_refs, land_refs = refs[:n], refs[n:2 * n]
            send_sems, recv_sems, own_sems = refs[2 * n:2 * n + 3]
            token = refs[-1]
            for own, pushes in self._copies(g_refs, land_refs, send_sems, recv_sems, own_sems, landing="sender"):
                own.start()
                for cp in pushes:
                    cp.start()
            token[...] = jnp.zeros_like(token)

        outs = pl.pallas_call(
            body, name=name,
            out_shape=(pltpu.SemaphoreType.DMA((n * (N_DEV - 1),)), pltpu.SemaphoreType.DMA((n * (N_DEV - 1),)),
                       pltpu.SemaphoreType.DMA((n,)))
                      + tuple(pltpu.HBM(g.shape, g.dtype) for g in gs)
                      + tuple(pltpu.HBM(s, g.dtype) for s, g in zip(land_shapes, gs))
                      + (jax.ShapeDtypeStruct((SUBLANES, LANES), F32),),
            in_specs=(hbm,) * (2 * n), out_specs=(sem,) * 3 + (hbm,) * (2 * n) + (_vmem(),),
            input_output_aliases={k: 3 + k for k in range(2 * n)},
            compiler_params=pltpu.CompilerParams(has_side_effects=pltpu.SideEffectType.DATAFLOW_SIDE_EFFECTING),
        )(*[pltpu.with_memory_space_constraint(g, pltpu.HBM) for g in gs],
          *[pltpu.with_memory_space_constraint(lax.empty(s, g.dtype), pltpu.HBM) for s, g in zip(land_shapes, gs)])
        self.sems, self.thru, self.token = outs[:3], outs[3:3 + 2 * n], outs[-1]

    def _copies(self, g_refs, land_refs, send_sems, recv_sems, own_sems, landing):
        me, peers = _me_and_peers()
        out = []
        for k in range(self.n):
            if self.kind == "scatter":
                src, dst = self.block_ofs[k](g_refs[k]), _slot(land_refs[k])
            else:
                src, dst = (lambda p, k=k: g_refs[k]), self.block_ofs[k](land_refs[k])
            own = pltpu.make_async_copy(src(me), dst(me), own_sems.at[k])
            pushes = []
            for m in self.masks:
                dev, pid = peers[m - 1]
                i = k * (N_DEV - 1) + m - 1
                pushes.append(_remote(src(pid), dst(me if landing == "sender" else pid),
                                      send_sems.at[i], recv_sems.at[i], dev))
            out.append((own, pushes))
        return out


def _scatter_wait(scatters, after, name):
    hbm = pl.BlockSpec(memory_space=pltpu.HBM)
    sem = pl.BlockSpec(memory_space=pltpu.SEMAPHORE)
    n_arr = [2 * sc.n for sc in scatters]
    total = sum(n_arr)

    def body(*refs):
        arrs, sems = refs[:total], refs[total:total + 3 * len(scatters)]
        a0 = 0
        for j, sc in enumerate(scatters):
            g_refs, land_refs = arrs[a0:a0 + sc.n], arrs[a0 + sc.n:a0 + 2 * sc.n]
            a0 += 2 * sc.n
            send_sems, recv_sems, own_sems = sems[3 * j:3 * j + 3]
            for (own, sent), (_, got) in zip(sc._copies(g_refs, land_refs, send_sems, recv_sems, own_sems, "sender"),
                                             sc._copies(g_refs, land_refs, send_sems, recv_sems, own_sems, "receiver")):
                own.wait()
                for cp in sent:
                    cp.wait_send()
                for cp in got:
                    cp.wait_recv()

    operands = [a for sc in scatters for a in sc.thru]
    outs = pl.pallas_call(
        body, name=name,
        out_shape=tuple(pltpu.HBM(a.shape, a.dtype) for a in operands),
        in_specs=(hbm,) * total + (sem,) * (3 * len(scatters)) + (pl.BlockSpec(memory_space=pl.ANY),),
        out_specs=(hbm,) * total, input_output_aliases={k: k for k in range(total)},
        compiler_params=pltpu.CompilerParams(has_side_effects=pltpu.SideEffectType.DATAFLOW_SIDE_EFFECTING),
    )(*operands, *[s for sc in scatters for s in sc.sems], after)
    lands, a0 = [], 0
    for sc in scatters:
        lands.extend(outs[a0 + sc.n:a0 + 2 * sc.n])
        a0 += 2 * sc.n
    return lands


def _small_ar(mbuf, silu_all, w_ada, c_ctx, n_mod_rows, n_vec_rows):
    D = silu_all.shape[1]
    ncol = w_ada.shape[1]
    nm = mbuf.shape[2]
    srows = silu_all.shape[0]

    def body(mbuf, s_ref, w_ref, cc_ref, tot_ref, gb_ref, gw_ref, gc_ref, tbuf, dmx, cmrow, send3, recv3):
        me, _ = _me_and_peers()
        msum = mbuf[0]
        for k in range(1, N_DEV):
            msum = msum + mbuf[k]
        tot_ref[...] = msum[n_mod_rows:n_mod_rows + n_vec_rows]
        gb_ref[...] = jnp.sum(msum[0:n_mod_rows], axis=0, keepdims=True)
        loc = pl.ds(pl.multiple_of(me * ncol, ncol), ncol)
        for k in range(N_DEV):
            dmx[k * SUBLANES:(k + 1) * SUBLANES, :] = mbuf[k, :, loc]
        cmrow[...] = msum
        cm_loc = cmrow[n_mod_rows - 1:n_mod_rows, loc]
        dmx[N_DEV * SUBLANES:, :] = jnp.concatenate([cm_loc, jnp.zeros((SUBLANES - 1, ncol), F32)], axis=0)
        gw_ref[...] = _dot_tn(s_ref[...], dmx[...])
        tbuf[me] = _dot_nt(dmx[N_DEV * SUBLANES:, :], w_ref[...])
        _exchange(lambda p: tbuf.at[me], lambda p: tbuf.at[p], send3, recv3)
        tsum = tbuf[0]
        for k in range(1, N_DEV):
            tsum = tsum + tbuf[k]
        cc = cc_ref[...]
        sg = _sigmoid(cc)
        gc_ref[...] = tsum[0:1, :] * (sg * (1.0 + cc * (1.0 - sg)))

    return pl.pallas_call(
        body, name="small_ar",
        out_shape=(jax.ShapeDtypeStruct((n_vec_rows, nm), F32), jax.ShapeDtypeStruct((1, nm), F32),
                   jax.ShapeDtypeStruct((D, ncol), F32), jax.ShapeDtypeStruct((1, D), F32)),
        in_specs=[_vmem()] * 4, out_specs=(_vmem(),) * 4,
        scratch_shapes=[pltpu.VMEM((N_DEV, SUBLANES, D), F32), pltpu.VMEM((srows, ncol), F32),
                        pltpu.VMEM((SUBLANES, nm), F32)] + [pltpu.SemaphoreType.DMA((N_DEV - 1,))] * 2,
        compiler_params=pltpu.CompilerParams(vmem_limit_bytes=VMEM_LIMIT),
    )(mbuf, silu_all, w_ada, c_ctx.reshape(1, D))


def _adam_update(w, g, m, v):
    mn = ADAM_B1 * m + (1.0 - ADAM_B1) * g
    vn = ADAM_B2 * v + (1.0 - ADAM_B2) * (g * g)
    m_hat = mn / (1.0 - ADAM_B1 ** ADAM_STEP)
    v_hat = vn / (1.0 - ADAM_B2 ** ADAM_STEP)
    return -ADAM_LR * (m_hat / (jnp.sqrt(v_hat) + ADAM_EPS) + ADAM_WD * w), mn, vn


def _adamw(w, g, m, v, name):
    rows, cols = w.shape
    tr = _div_tile(rows, 128, SUBLANES)

    def body(w_ref, g_ref, m_ref, v_ref, d_ref, nm_ref, nv_ref):
        d_ref[...], nm_ref[...], nv_ref[...] = _adam_update(w_ref[...], g_ref[...], m_ref[...], v_ref[...])

    spec = pl.BlockSpec((tr, cols), lambda i: (i, 0))
    return pl.pallas_call(
        functools.partial(body), name=name, grid=(rows // tr,),
        out_shape=(jax.ShapeDtypeStruct((rows, cols), F32),) * 3,
        in_specs=[spec] * 4, out_specs=(spec,) * 3,
        compiler_params=_params("parallel"),
    )(w, g, m, v)


def _adamw_small(items, name):
    n = len(items)

    def body(*refs):
        ins, outs = refs[:4 * n], refs[4 * n:]
        for i in range(n):
            w_ref, g_ref, m_ref, v_ref = ins[4 * i:4 * i + 4]
            outs[3 * i][...], outs[3 * i + 1][...], outs[3 * i + 2][...] = _adam_update(
                w_ref[...], g_ref[...], m_ref[...], v_ref[...])

    outs = pl.pallas_call(
        body, name=name,
        out_shape=tuple(jax.ShapeDtypeStruct(it[0].shape, F32) for it in items for _ in range(3)),
        in_specs=[_vmem()] * (4 * n), out_specs=(_vmem(),) * (3 * n),
        compiler_params=pltpu.CompilerParams(vmem_limit_bytes=VMEM_LIMIT),
    )(*[a for it in items for a in it])
    return [tuple(outs[3 * i:3 * i + 3]) for i in range(n)]


def _sum_adamw(buf, w, m, v, name):
    _, rows, cols = buf.shape
    tr = _div_tile(rows, 128, 2 * SUBLANES)

    def body(b_ref, w_ref, m_ref, v_ref, g_ref, d_ref, nm_ref, nv_ref):
        g = b_ref[0].astype(F32)
        for k in range(1, N_DEV):
            g = g + b_ref[k].astype(F32)
        g_ref[...] = g
        d_ref[...], nm_ref[...], nv_ref[...] = _adam_update(w_ref[...], g, m_ref[...], v_ref[...])

    spec = pl.BlockSpec((tr, cols), lambda i: (i, 0))
    return pl.pallas_call(
        functools.partial(body), name=name, grid=(rows // tr,),
        out_shape=(jax.ShapeDtypeStruct((rows, cols), F32),) * 4,
        in_specs=[pl.BlockSpec((N_DEV, tr, cols), lambda i: (0, i, 0))] + [spec] * 3, out_specs=(spec,) * 4,
        compiler_params=_params("parallel"),
    )(buf, w, m, v)


def _rope_tables(n_ctx, n):
    n_freq = RET_DIM // 4
    inv = np.float32(ROPE_BASE) ** (-np.arange(n_freq, dtype=np.float32) / np.float32(n_freq))
    tok = np.arange(n)
    pos_r = (tok // GRID_W).astype(np.float32)
    pos_c = (tok % GRID_W).astype(np.float32)
    ang_r = (pos_r[:, None] * inv[None, :]).astype(np.float32)
    ang_c = (pos_c[:, None] * inv[None, :]).astype(np.float32)
    cos = np.concatenate([np.cos(ang_r), np.cos(ang_r), np.cos(ang_c), np.cos(ang_c)], axis=-1)
    sin = np.concatenate([-np.sin(ang_r), np.sin(ang_r), -np.sin(ang_c), np.sin(ang_c)], axis=-1)
    cos = np.concatenate([np.ones((n_ctx, RET_DIM), np.float32), cos], axis=0)
    sin = np.concatenate([np.zeros((n_ctx, RET_DIM), np.float32), sin], axis=0)
    return jnp.asarray(cos, F32), jnp.asarray(sin, F32)


def _na_tables():
    q = np.arange(GRID_W)[:, None]
    k = np.arange(GRID_W)[None, :]
    start = np.clip(q - NA_KW // 2, 0, GRID_W - NA_KW)
    valid = (k >= start) & (k < start + NA_KW)
    dc = np.clip(k - q + (NA_KW - 1), 0, 2 * NA_KW - 2)
    ncls = 2 * NA_KW - 1
    onehot = (dc[None] == np.arange(ncls)[:, None, None]) & valid[None]
    oh2 = np.zeros((GRID_W, LANES, LANES), np.float32)
    for c in range(ncls):
        oh2[:, :GRID_W, c] = onehot[c]
        oh2[:, GRID_W:, 32 + c] = onehot[c]
    return onehot.astype(np.float32), valid, oh2.reshape(GRID_W * LANES, LANES)


def _paired_bias(rpb, onehot, valid):
    ncls = onehot.shape[0]
    pair = np.zeros((2 * ncls, GRID_W, LANES), np.float32)
    pair[:ncls, :, :GRID_W] = onehot
    pair[ncls:, :, GRID_W:] = onehot
    rows = jnp.concatenate([rpb[:, :-1], rpb[:, 1:]], axis=-1)
    t = jnp.einsum("hdc,cqk->hdqk", rows, jnp.asarray(pair), precision=lax.Precision.HIGHEST)
    return jnp.where(jnp.asarray(np.tile(valid, (1, 2)))[None, None], t, NEG_INF)


def kernel(x, c, ctx, c_ctx, w_ada, b_ada, g_pre_mix, g_post_mix, g_pre_mlp, g_post_mlp, w_in, ret_decay, ret_gn, na_rpb, w_out, w_mlp1, w_mlp2, loss_target, m_c_ctx, m_w_ada, m_b_ada, m_g_pre_mix, m_g_post_mix, m_g_pre_mlp, m_g_post_mlp, m_w_in, m_ret_decay, m_ret_gn, m_na_rpb, m_w_out, m_w_mlp1, m_w_mlp2, v_c_ctx, v_w_ada, v_b_ada, v_g_pre_mix, v_g_post_mix, v_g_pre_mlp, v_g_post_mlp, v_w_in, v_ret_decay, v_ret_gn, v_na_rpb, v_w_out, v_w_mlp1, v_w_mlp2):
    B, N, D = x.shape
    C = ctx.shape[1]
    T = C + N

    silu_all, mods_g, win_b, wout_l, w1_l, w2_l = _mod_gather(c, c_ctx, w_ada[0], b_ada, w_in[0].T, w_out[0],
                                                             w_mlp1[0], w_mlp2[0])
    mods_mine = mods_g.transpose(1, 0, 2).reshape(mods_g.shape[1], N_MOD * D)
    modl = jnp.concatenate([mods_mine[:B], mods_mine[SUBLANES:SUBLANES + 1]], axis=0)
    modl = modl.reshape(B + 1, N_MOD, 1, D)
    rin = w_in.shape[2]
    rout, c1, r2 = wout_l.shape[0], w1_l.shape[1], w2_l.shape[0]

    def rows_of(n):
        return lambda ref: _row_block(ref, n)

    def cols_of(n):
        return lambda ref: _col_block(ref, n)

    cos, sin = _rope_tables(C, N)
    onehot, valid, oh2 = _na_tables()
    bias2 = _paired_bias(na_rpb[0], onehot, valid)
    lg = jax.nn.log_sigmoid(ret_decay[0].astype(F32))

    ag = _SplitScatter([wout_l, w1_l, w2_l], [rows_of(rout), cols_of(c1), rows_of(r2)],
                       [(N_DEV * rout, D), (D, N_DEV * c1), (N_DEV * r2, D)], "ag_mlp_start",
                       kind="gather", masks=SIBLING + ICI_SAME_CORE)
    h_all, proj = _inproj_fwd(x, ctx, modl, g_pre_mix + ag.token[0, 0], win_b)
    o_ret, lat_ret = _ret_fwd(proj, cos, sin, lg, ret_gn, C)
    (lat_na,) = _na_fwd(proj, bias2, C)
    wout_part, w1_part, w2_part = _scatter_wait([ag], lat_na, "ag_mlp_wait")

    (dy1, dlat_ret, dlat_na, dmix, h2, act, du, dz, red_d) = _dense_core(
        lat_ret, lat_na, x, loss_target, modl, g_post_mix, g_pre_mlp, g_post_mlp, wout_part, w1_part, w2_part)

    gw_out_p = _tn_matmul(lat_ret[:, None], dmix, "gw_out_ret", rows_after=lat_na.shape[-1])
    gw_out_p = _tn_matmul(lat_na[:, None], dmix, "gw_out_na", rows_before=lat_ret.shape[-1], into=gw_out_p)
    gw1_p = _tn_matmul(h2[:, None], du, "gw_mlp1")
    gw2_p = _tn_matmul(act[:, None], dz, "gw_mlp2")
    rs_mlp = _SplitScatter([gw_out_p, gw1_p, gw2_p], [rows_of(rout), cols_of(c1), rows_of(r2)],
                           [(rout, D), (D, c1), (r2, D)], "rs_mlp_start")

    dret, dgn_p, dlg_p = _ret_bwd(proj, cos, sin, lg, ret_gn + rs_mlp.token[0, 0], o_ret, dlat_ret, C)
    dna, dbias2 = _na_bwd(proj, bias2, dlat_na, C)
    ret_cols, na_cols = dret.shape[1] * dret.shape[3], dna.shape[1] * dna.shape[3]
    gwin_t_p = _tn_matmul(dret, h_all, "gw_in_ret", rows_after=na_cols)
    gwin_t_p = _tn_matmul(dna, h_all, "gw_in_na", rows_before=ret_cols, into=gwin_t_p)
    rs_in = _SplitScatter([gwin_t_p], [rows_of(rin)], [(rin, D)], "rs_w_in_start")
    grad_x, red_i = _inproj_bwd(dret, dna, x, ctx, dy1, modl, g_pre_mix + rs_in.token[0, 0], win_b)

    rd = red_d
    nct = red_i.shape[1] * C // T
    ri_ctx = red_i[:, :nct].sum(axis=(0, 1))
    ri_lat = red_i[:, nct:].sum(axis=1)
    d_mods = jnp.concatenate([ri_lat[:, 0], ri_lat[:, 1], rd[:, 0], rd[:, 4], rd[:, 3], rd[:, 2]], axis=-1)
    d_cmods = jnp.concatenate([ri_ctx[0], ri_ctx[1], jnp.zeros(((N_MOD - 2) * D,), F32)])[None]
    dg_pre_mix = ri_lat[:, 2].sum(axis=0) + ri_ctx[2]
    dg_post_mix = rd[:, 1].sum(axis=0)
    dg_pre_mlp = rd[:, 5].sum(axis=0)
    dg_post_mlp = rd[:, 6].sum(axis=0)
    loss_p = rd[:, 7, 0].sum()
    d_gn = dgn_p[:, 0].sum(axis=0)
    d_lg = dlg_p[:, :, :2, 0].sum(axis=0).T
    d_decay = d_lg * jax.nn.sigmoid(-ret_decay[0].astype(F32))
    rr = _rpb_reduce(dbias2, jnp.asarray(oh2, BF16)).reshape(NA_HEADS, 2 * NA_KH - 2, LANES)
    ncls = 2 * NA_KW - 1
    d_rpb = (jnp.pad(rr[:, :, :ncls], ((0, 0), (0, 1), (0, 0))) + jnp.pad(rr[:, :, 32:32 + ncls], ((0, 0), (1, 0), (0, 0))))
    d_rpb32 = jnp.pad(d_rpb, ((0, 0), (0, 0), (0, 32 - ncls)))
    pieces = [dg_pre_mix, dg_post_mix, dg_pre_mlp, dg_post_mlp, d_gn, d_rpb32.reshape(-1),
              jnp.pad(d_decay.reshape(-1), (0, LANES - d_decay.size)), jnp.full((LANES,), loss_p, F32)]
    vec = jnp.concatenate(pieces)
    nm = N_MOD * D
    n_vec_rows = -(-vec.shape[0] // nm)
    assert B + 1 + n_vec_rows <= SUBLANES
    vec = jnp.pad(vec, (0, n_vec_rows * nm - vec.shape[0])).reshape(n_vec_rows, nm)
    dm_slot = jnp.concatenate([d_mods, d_cmods, vec, jnp.zeros((SUBLANES - B - 1 - n_vec_rows, nm), F32)], axis=0)
    def whole(ref):
        return lambda p: ref

    small = _SplitScatter([dm_slot], [whole], [dm_slot.shape], "small_start")
    land_out, land_1, land_2, land_in = _scatter_wait([rs_mlp, rs_in], small.token, "rs_wait")
    fused = {"w_in": [a.T for a in _sum_adamw(land_in, w_in[0].T, m_w_in[0].T, v_w_in[0].T, "sum_adamw_w_in")],
             "w_out": _sum_adamw(land_out, w_out[0], m_w_out[0], v_w_out[0], "sum_adamw_w_out"),
             "w_mlp1": _sum_adamw(land_1, w_mlp1[0], m_w_mlp1[0], v_w_mlp1[0], "sum_adamw_w_mlp1"),
             "w_mlp2": _sum_adamw(land_2, w_mlp2[0], m_w_mlp2[0], v_w_mlp2[0], "sum_adamw_w_mlp2")}
    (mbuf,) = _scatter_wait([small], fused["w_mlp2"][0], "small_wait")
    tot, g_b_ada, g_w_ada, g_c_ctx = _small_ar(mbuf, silu_all, w_ada[0], c_ctx, B + 1, n_vec_rows)
    flat = tot.reshape(-1)
    o0 = 0
    g_pre_mix_g = flat[o0:o0 + D]; o0 += D
    g_post_mix_g = flat[o0:o0 + D]; o0 += D
    g_pre_mlp_g = flat[o0:o0 + D]; o0 += D
    g_post_mlp_g = flat[o0:o0 + D]; o0 += D
    g_gn = flat[o0:o0 + RET_WIDTH]; o0 += RET_WIDTH
    nrpb = NA_HEADS * (2 * NA_KH - 1) * 32
    g_rpb = flat[o0:o0 + nrpb].reshape(NA_HEADS, 2 * NA_KH - 1, 32)[:, :, :ncls]; o0 += nrpb
    g_decay = flat[o0:o0 + 2 * RET_HEADS].reshape(2, RET_HEADS); o0 += LANES
    loss = flat[o0]

    grads = {
        "c_ctx": g_c_ctx.reshape(c_ctx.shape), "w_ada": g_w_ada[None], "b_ada": g_b_ada.reshape(b_ada.shape),
        "g_pre_mix": g_pre_mix_g[None], "g_post_mix": g_post_mix_g[None], "g_pre_mlp": g_pre_mlp_g[None],
        "g_post_mlp": g_post_mlp_g[None], "w_in": fused["w_in"][0][None], "ret_decay": g_decay[None], "ret_gn": g_gn[None],
        "na_rpb": g_rpb[None], "w_out": fused["w_out"][0][None], "w_mlp1": fused["w_mlp1"][0][None],
        "w_mlp2": fused["w_mlp2"][0][None],
    }
    weights = dict(c_ctx=c_ctx, w_ada=w_ada, b_ada=b_ada, g_pre_mix=g_pre_mix, g_post_mix=g_post_mix,
                   g_pre_mlp=g_pre_mlp, g_post_mlp=g_post_mlp, w_in=w_in, ret_decay=ret_decay, ret_gn=ret_gn,
                   na_rpb=na_rpb, w_out=w_out, w_mlp1=w_mlp1, w_mlp2=w_mlp2)
    m_in = dict(c_ctx=m_c_ctx, w_ada=m_w_ada, b_ada=m_b_ada, g_pre_mix=m_g_pre_mix, g_post_mix=m_g_post_mix,
                g_pre_mlp=m_g_pre_mlp, g_post_mlp=m_g_post_mlp, w_in=m_w_in, ret_decay=m_ret_decay,
                ret_gn=m_ret_gn, na_rpb=m_na_rpb, w_out=m_w_out, w_mlp1=m_w_mlp1, w_mlp2=m_w_mlp2)
    v_in = dict(c_ctx=v_c_ctx, w_ada=v_w_ada, b_ada=v_b_ada, g_pre_mix=v_g_pre_mix, g_post_mix=v_g_post_mix,
                g_pre_mlp=v_g_pre_mlp, g_post_mlp=v_g_post_mlp, w_in=v_w_in, ret_decay=v_ret_decay,
                ret_gn=v_ret_gn, na_rpb=v_na_rpb, w_out=v_w_out, w_mlp1=v_w_mlp1, w_mlp2=v_w_mlp2)
    names = list(weights)
    deltas, new_m, new_v = {}, {}, {}
    def as_2d(n):
        shp = weights[n].shape
        two_d = (-1, shp[-1]) if len(shp) > 1 else (1, shp[0])
        return [a.reshape(two_d) for a in (weights[n], grads[n], m_in[n], v_in[n])]

    small = [n for n in names if n not in fused and weights[n].size <= 65536]
    updated = dict(zip(small, _adamw_small([as_2d(n) for n in small], "adamw_small")))
    for n in names:
        if n in fused:
            updated[n] = fused[n][1:]
        elif n not in updated:
            updated[n] = _adamw(*as_2d(n), "adamw_" + n)
        deltas[n], new_m[n], new_v[n] = (a.reshape(weights[n].shape) for a in updated[n])
    return (loss, grad_x, *[grads[n] for n in names], *[deltas[n] for n in names],
            *[new_m[n] for n in names], *[new_v[n] for n in names])
```

```python
import functools
import math

import numpy as np
import jax
import jax.numpy as jnp
from jax import lax
from jax.experimental import pallas as pl
from jax.experimental.pallas import tpu as pltpu

F32 = jnp.float32
BF16 = jnp.bfloat16
MESH = pl.DeviceIdType.MESH

N_DEV = 8
LANES = 128
SUBLANES = 8
VMEM_LIMIT = 60 * 1024 * 1024

GRID_W = 64
RET_HEADS = 4
RET_DIM = 128
RET_WIDTH = RET_HEADS * RET_DIM
NA_HEADS = 8
NA_DIM = 64
NA_WIDTH = NA_HEADS * NA_DIM
NA_PAIRS = NA_HEADS // 2
NA_KH = 8
NA_KW = 16
NA_GROUP = 8
SEG = 512
ROPE_BASE = 10000.0
NORM_EPS = 1e-6
NEG_INF = -1e30
N_MOD = 6

ADAM_LR = 0.001
ADAM_B1 = 0.9
ADAM_B2 = 0.999
ADAM_EPS = 1e-08
ADAM_WD = 0.01
ADAM_STEP = 10


def _dot(a, b):
    return lax.dot_general(a, b, (((1,), (0,)), ((), ())), preferred_element_type=F32)


def _dot_nt(a, b):
    return lax.dot_general(a, b, (((1,), (1,)), ((), ())), preferred_element_type=F32)


def _dot_tn(a, b):
    return lax.dot_general(a, b, (((0,), (0,)), ((), ())), preferred_element_type=F32)


def _sigmoid(x):
    return 1.0 / (1.0 + jnp.exp(-x))


def _div_tile(n, cap, mult):
    if n <= cap:
        return n
    for t in range(cap - cap % mult, 0, -mult):
        if n % t == 0:
            return t
    raise ValueError(f"no tile for {n}")


def _params(*sem):
    return pltpu.CompilerParams(dimension_semantics=tuple(sem) if sem else None,
                                vmem_limit_bytes=VMEM_LIMIT)


def _vmem():
    return pl.BlockSpec(memory_space=pltpu.VMEM)


def _any():
    return pl.BlockSpec(memory_space=pl.ANY)


def _me_and_peers():
    x, y, c = lax.axis_index("x"), lax.axis_index("y"), lax.axis_index("c")
    me = 4 * x + 2 * y + c
    peers = []
    for m in range(1, N_DEV):
        px = 1 - x if (m >> 2) & 1 else x
        py = 1 - y if (m >> 1) & 1 else y
        pc = 1 - c if m & 1 else c
        peers.append(((px, py, pc), 4 * px + 2 * py + pc))
    return me, peers


def _exchange(src_for, dst_from, send_sems, recv_sems):
    me, peers = _me_and_peers()
    sent = []
    for i, (dev, pid) in enumerate(peers):
        cp = pltpu.make_async_remote_copy(src_ref=src_for(pid), dst_ref=dst_from(me),
                                          send_sem=send_sems.at[i], recv_sem=recv_sems.at[i],
                                          device_id=dev, device_id_type=MESH)
        cp.start()
        sent.append(cp)
    for i, (dev, pid) in enumerate(peers):
        pltpu.make_async_remote_copy(src_ref=src_for(pid), dst_ref=dst_from(pid),
                                     send_sem=send_sems.at[i], recv_sem=recv_sems.at[i],
                                     device_id=dev, device_id_type=MESH).wait_recv()
    for cp in sent:
        cp.wait_send()


SIBLING = (1,)
ICI_SAME_CORE = (2, 4, 6)
ALL_PEERS = tuple(range(1, N_DEV))


def _remote(src, dst, send_sem, recv_sem, dev):
    return pltpu.make_async_remote_copy(src_ref=src, dst_ref=dst, send_sem=send_sem, recv_sem=recv_sem,
                                        device_id=dev, device_id_type=MESH)


def _push_start(items, masks, send_sems, recv_sems):
    me, peers = _me_and_peers()
    for k, (src_for, dst_from) in enumerate(items):
        for m in masks:
            dev, pid = peers[m - 1]
            _remote(src_for(pid), dst_from(me), send_sems.at[k, m - 1], recv_sems.at[k, m - 1], dev).start()


def _push_wait_recv(items, masks, send_sems, recv_sems):
    me, peers = _me_and_peers()
    for k, (src_for, dst_from) in enumerate(items):
        for m in masks:
            dev, pid = peers[m - 1]
            _remote(src_for(pid), dst_from(pid), send_sems.at[k, m - 1], recv_sems.at[k, m - 1], dev).wait_recv()


def _push_wait_send(items, masks, send_sems, recv_sems):
    me, peers = _me_and_peers()
    for k, (src_for, dst_from) in enumerate(items):
        for m in masks:
            dev, pid = peers[m - 1]
            _remote(src_for(pid), dst_from(me), send_sems.at[k, m - 1], recv_sems.at[k, m - 1], dev).wait_send()


def _forward_start(items, send_sems, recv_sems):
    me, peers = _me_and_peers()
    sib = peers[0][0]
    for k, (blk_in, blk_out) in enumerate(items):
        for j, m in enumerate(ICI_SAME_CORE):
            pid = peers[m - 1][1]
            _remote(blk_in(pid), blk_out(pid), send_sems.at[k, j], recv_sems.at[k, j], sib).start()


def _forward_wait(items, send_sems, recv_sems):
    me, peers = _me_and_peers()
    sib = peers[0][0]
    for k, (blk_in, blk_out) in enumerate(items):
        for j, m in enumerate(ICI_SAME_CORE):
            got = peers[(m | 1) - 1][1]
            _remote(blk_in(got), blk_out(got), send_sems.at[k, j], recv_sems.at[k, j], sib).wait_recv()
    for k, (blk_in, blk_out) in enumerate(items):
        for j, m in enumerate(ICI_SAME_CORE):
            pid = peers[m - 1][1]
            _remote(blk_in(pid), blk_out(pid), send_sems.at[k, j], recv_sems.at[k, j], sib).wait_send()


def _mod_gather(c, c_ctx, w_ada, b_ada, w_in_t, w_out, w1, w2):
    B, D = c.shape
    ncol = w_ada.shape[1]
    rows = SUBLANES * N_DEV + SUBLANES

    def body(c_ref, cc_ref, w_ref, b_ref, win_ref, wout_ref, w1_ref, w2_ref,
             s_ref, m_ref, gin_ref, wout_b, w1_b, w2_b,
             win_b, msend, send1, recv1, send2, recv2, wsend, wrecv, fsend, frecv, lsem):
        me, _ = _me_and_peers()
        win_b[...] = win_ref[...].astype(BF16)
        block = _row_block(gin_ref, w_in_t.shape[0])
        gather = [(lambda p: win_b, block)]
        own = pltpu.make_async_copy(win_b, block(me), lsem.at[0])
        cv = c_ref[...]
        slot = jnp.concatenate([cv * _sigmoid(cv), jnp.zeros((SUBLANES - B, D), F32)], axis=0)
        my_rows = pl.ds(pl.multiple_of(me * SUBLANES, SUBLANES), SUBLANES)
        s_ref[my_rows, :] = slot
        ccv = cc_ref[...]
        s_ref[SUBLANES * N_DEV:, :] = jnp.concatenate(
            [ccv * _sigmoid(ccv), jnp.zeros((SUBLANES - 1, D), F32)], axis=0)

        def rows_of(p):
            return s_ref.at[pl.ds(pl.multiple_of(p * SUBLANES, SUBLANES), SUBLANES), :]

        _exchange(lambda p: rows_of(me), rows_of, send1, recv1)
        own.start()
        _push_start(gather, SIBLING + ICI_SAME_CORE, wsend, wrecv)
        wout_b[...] = wout_ref[...].astype(BF16)
        w1_b[...] = w1_ref[...].astype(BF16)
        w2_b[...] = w2_ref[...].astype(BF16)
        b_loc = b_ref[:, pl.ds(pl.multiple_of(me * ncol, ncol), ncol)]
        mods = _dot(s_ref[...], w_ref[...]) + b_loc
        for p in range(N_DEV):
            msend[p] = jnp.concatenate([mods[p * SUBLANES:(p + 1) * SUBLANES], mods[N_DEV * SUBLANES:]], axis=0)
        m_ref[me] = msend[me]
        columns = [(lambda p: msend.at[p], lambda p: m_ref.at[p])]
        _push_start(columns, ALL_PEERS, send2, recv2)
        _push_wait_recv(gather, ICI_SAME_CORE, wsend, wrecv)
        relay = [(block, block)]
        _forward_start(relay, fsend, frecv)
        _push_wait_recv(columns, ALL_PEERS, send2, recv2)
        _push_wait_recv(gather, SIBLING, wsend, wrecv)
        _forward_wait(relay, fsend, frecv)
        _push_wait_send(columns, ALL_PEERS, send2, recv2)
        _push_wait_send(gather, SIBLING + ICI_SAME_CORE, wsend, wrecv)
        own.wait()

    return pl.pallas_call(
        body, name="mod_gather",
        out_shape=(jax.ShapeDtypeStruct((rows, D), F32), jax.ShapeDtypeStruct((N_DEV, 2 * SUBLANES, ncol), F32),
                   jax.ShapeDtypeStruct((N_DEV * w_in_t.shape[0], D), BF16),
                   jax.ShapeDtypeStruct(w_out.shape, BF16), jax.ShapeDtypeStruct(w1.shape, BF16),
                   jax.ShapeDtypeStruct(w2.shape, BF16)),
        in_specs=[_vmem()] * 8, out_specs=(_vmem(), _vmem(), _any(), _vmem(), _vmem(), _vmem()),
        scratch_shapes=[pltpu.VMEM(w_in_t.shape, BF16), pltpu.VMEM((N_DEV, 2 * SUBLANES, ncol), F32)]
                       + [pltpu.SemaphoreType.DMA((N_DEV - 1,))] * 2
                       + [pltpu.SemaphoreType.DMA((1, N_DEV - 1))] * 4 + [pltpu.SemaphoreType.DMA((1, 3))] * 2
                       + [pltpu.SemaphoreType.DMA((1,))],
        compiler_params=pltpu.CompilerParams(vmem_limit_bytes=VMEM_LIMIT),
    )(c, c_ctx.reshape(1, D), w_ada, b_ada, w_in_t, w_out, w1, w2)


def _row_block(ref, rows):
    return lambda p: ref.at[pl.ds(pl.multiple_of(p * rows, 2 * SUBLANES), rows), :]


def _col_block(ref, cols):
    return lambda p: ref.at[:, pl.ds(pl.multiple_of(p * cols, LANES), cols)]


def _slot(ref):
    return lambda p: ref.at[p]


def _grid_call(body, *, name, grid, out_shape, in_specs, out_specs, scratch_shapes, args):
    return pl.pallas_call(
        body, name=name, grid=grid, out_shape=tuple(out_shape), in_specs=list(in_specs), out_specs=tuple(out_specs),
        scratch_shapes=list(scratch_shapes), compiler_params=_params(*(("arbitrary",) * len(grid))),
    )(*args)


def _token_tiles(n_ctx, tm):
    nct = n_ctx // tm

    def ctx_spec(D):
        return pl.BlockSpec((None, tm, D), lambda b, t: (b, jnp.minimum(t, nct - 1), 0))

    def lat_spec(D):
        return pl.BlockSpec((None, tm, D), lambda b, t: (b, jnp.maximum(t - nct, 0), 0))

    return nct, ctx_spec, lat_spec


def _inproj_fwd(x, ctx, modl, g1, w_in_t):
    B, N, D = x.shape
    n_ctx = ctx.shape[1]
    T = n_ctx + N
    nw = w_in_t.shape[0]
    tm = _div_tile(n_ctx, 256, 16)
    nct, ctx_spec, lat_spec = _token_tiles(n_ctx, tm)

    def body(c_ref, x_ref, sh_ref, sc_ref, g_ref, w_ref, h_ref, p_ref):
        x = jnp.where(pl.program_id(1) < nct, c_ref[...], x_ref[...])
        r = lax.rsqrt(jnp.mean(x * x, axis=-1, keepdims=True) + NORM_EPS)
        h = ((x * r) * g_ref[...]) * (1.0 + sc_ref[...]) + sh_ref[...]
        hb = h.astype(BF16)
        h_ref[...] = hb
        p_ref[...] = _dot_nt(hb, w_ref[...]).astype(BF16)

    def mrow(b, t):
        return jnp.where(t < nct, B, b)

    return _grid_call(
        body, name="inproj_fwd", grid=(B, T // tm),
        out_shape=(jax.ShapeDtypeStruct((B, T, D), BF16), jax.ShapeDtypeStruct((B, T, nw), BF16)),
        in_specs=[ctx_spec(D), lat_spec(D),
                  pl.BlockSpec((None, None, 1, D), lambda b, t: (mrow(b, t), 0, 0, 0)),
                  pl.BlockSpec((None, None, 1, D), lambda b, t: (mrow(b, t), 1, 0, 0)),
                  pl.BlockSpec((1, D), lambda b, t: (0, 0)),
                  pl.BlockSpec((nw, D), lambda b, t: (0, 0))],
        out_specs=(pl.BlockSpec((None, tm, D), lambda b, t: (b, t, 0)),
                   pl.BlockSpec((None, tm, nw), lambda b, t: (b, t, 0))),
        scratch_shapes=[], args=(ctx, x, modl, modl, g1, w_in_t))


def _swap32(x):
    lane = lax.broadcasted_iota(jnp.int32, x.shape, 1)
    return jnp.where((lane % 64) < 32, pltpu.roll(x, 96, 1), pltpu.roll(x, 32, 1))


def _rope(x, cos, sin):
    return x * cos + _swap32(x) * sin


def _unrope(dy, cos, sin):
    return dy * cos + _swap32(dy * sin)


def _ret_weights(lgf, lgb, dist):
    return jnp.exp(jnp.where(dist >= 0.0, lgf * dist, -lgb * dist))


class _RetDecay:
    def __init__(self, lgf, lgb, rows):
        r = lax.broadcasted_iota(jnp.int32, (rows, RET_DIM), 0).astype(F32)
        self.head = r + 1.0
        self.tail = (rows - 1.0) - r
        self.q_f = jnp.exp(lgf * self.head)
        self.k_f = jnp.exp(lgf * self.tail)
        self.q_b = jnp.exp(lgb * self.tail)
        self.k_b = jnp.exp(lgb * self.head)


def _ret_states(kf32, vs, lgf, lgb, C, c, nt, hf, hb, hfa=None, hba=None):
    dec = _RetDecay(lgf, lgb, c)
    dec_c = _RetDecay(lgf, lgb, C)
    step_f = jnp.exp(jnp.zeros((RET_DIM, RET_DIM), F32) + lgf * c)
    step_b = jnp.exp(jnp.zeros((RET_DIM, RET_DIM), F32) + lgb * c)

    def upd(rows, kdec):
        return _dot_tn((kf32[rows, :] * kdec).astype(BF16), vs[rows, :])

    def lat(t):
        return slice(C + t * c, C + (t + 1) * c)

    state = upd(slice(0, C), dec_c.k_f)
    aged = jnp.zeros_like(state)
    for t in range(nt):
        hf[t] = state.astype(BF16)
        if hfa is not None:
            hfa[t] = aged
        if t < nt - 1:
            aged = step_f * (aged + c * state)
            state = step_f * state + upd(lat(t), dec.k_f)
    state = upd(slice(0, C), dec_c.k_b)
    aged = jnp.zeros_like(state)
    for t in range(nt - 1, -1, -1):
        hb[t] = state.astype(BF16)
        if hba is not None:
            hba[t] = aged
        if t > 0:
            aged = step_b * (aged + c * state)
            state = step_b * state + upd(lat(t), dec.k_b)
    return dec, dec_c, step_f, step_b


def _ret_fwd(proj, cos, sin, lg, gn, n_ctx):
    B, T, _ = proj.shape
    C = n_ctx
    N = T - C
    c = _div_tile(N, 256, 16)
    nt = N // c
    scale = RET_DIM ** -0.5

    def body(lg_ref, q_ref, k_ref, v_ref, g_ref, cos_ref, sin_ref, gn_ref, o_ref, lat_ref, qs, ks, vs, kf32, hf, hb):
        h = pl.program_id(1)
        lgf = lg_ref[0, h]
        lgb = lg_ref[1, h]
        for rows in [slice(0, C)] + [slice(C + t * c, C + (t + 1) * c) for t in range(nt)]:
            cosb = cos_ref[rows, :]
            sinb = sin_ref[rows, :]
            qs[rows, :] = (_rope(q_ref[rows, :].astype(F32), cosb, sinb) * scale).astype(BF16)
            kr = _rope(k_ref[rows, :].astype(F32), cosb, sinb)
            kf32[rows, :] = kr
            ks[rows, :] = kr.astype(BF16)
            vs[rows, :] = v_ref[rows, :].astype(BF16)
        gnv = gn_ref[...]
        dec, _, _, _ = _ret_states(kf32, vs, lgf, lgb, C, c, nt, hf, hb)
        rc = (lax.broadcasted_iota(jnp.int32, (c, c), 0) - lax.broadcasted_iota(jnp.int32, (c, c), 1)).astype(F32)
        w_diag = _ret_weights(lgf, lgb, rc)
        for t in range(nt):
            rows = slice(C + t * c, C + (t + 1) * c)
            qt = qs[rows, :]
            s = _dot_nt(qt, ks[rows, :])
            o = (_dot((s * w_diag).astype(BF16), vs[rows, :])
                 + dec.q_f * _dot(qt, hf[t]) + dec.q_b * _dot(qt, hb[t]))
            o_ref[t * c:(t + 1) * c, :] = o
            mu = jnp.mean(o, axis=-1, keepdims=True)
            oc = o - mu
            var = jnp.mean(oc * oc, axis=-1, keepdims=True)
            yh = oc * lax.rsqrt(var + NORM_EPS)
            g = g_ref[rows, :].astype(F32)
            lat_ref[t * c:(t + 1) * c, :] = ((yh * gnv) * (g * _sigmoid(g))).astype(BF16)

    def col(seg):
        return pl.BlockSpec((None, T, RET_DIM), lambda b, h, seg=seg: (b, 0, seg * RET_HEADS + h))

    return _grid_call(
        body, name="ret_fwd", grid=(B, RET_HEADS),
        out_shape=(jax.ShapeDtypeStruct((B, N, RET_WIDTH), F32), jax.ShapeDtypeStruct((B, N, RET_WIDTH), BF16)),
        in_specs=[pl.BlockSpec(memory_space=pltpu.SMEM), col(0), col(1), col(2), col(3),
                  pl.BlockSpec((T, RET_DIM), lambda b, h: (0, 0)), pl.BlockSpec((T, RET_DIM), lambda b, h: (0, 0)),
                  pl.BlockSpec((1, RET_DIM), lambda b, h: (0, h))],
        out_specs=(pl.BlockSpec((None, N, RET_DIM), lambda b, h: (b, 0, h)),
                   pl.BlockSpec((None, N, RET_DIM), lambda b, h: (b, 0, h))),
        scratch_shapes=[pltpu.VMEM((T, RET_DIM), BF16)] * 3 + [pltpu.VMEM((T, RET_DIM), F32)]
                       + [pltpu.VMEM((nt, RET_DIM, RET_DIM), BF16)] * 2,
        args=(lg, proj, proj, proj, proj, cos, sin, gn))


def _ret_bwd(proj, cos, sin, lg, gn, o, dlat, n_ctx):
    B, T, _ = proj.shape
    C = n_ctx
    N = T - C
    c = _div_tile(N, 256, 16)
    nt = N // c
    scale = RET_DIM ** -0.5

    def lat(t):
        return slice(C + t * c, C + (t + 1) * c)

    def body(lg_ref, q_ref, k_ref, v_ref, g_ref, cos_ref, sin_ref, gn_ref, o_ref, dl_ref,
             d_ref, dgn_ref, dlg_ref, qs, ks, vs, dos, qf32, kf32, hf, hb, hfa, hba, gf_s, gb_s):
        h = pl.program_id(1)
        lgf = lg_ref[0, h]
        lgb = lg_ref[1, h]
        gnv = gn_ref[...]

        def fold(a):
            return jnp.sum(a.reshape(a.shape[0] // SUBLANES, SUBLANES, a.shape[1]), axis=0)

        for rows in [slice(0, C)] + [lat(t) for t in range(nt)]:
            cosb = cos_ref[rows, :]
            sinb = sin_ref[rows, :]
            qr = _rope(q_ref[rows, :].astype(F32), cosb, sinb) * scale
            qf32[rows, :] = qr
            qs[rows, :] = qr.astype(BF16)
            kr = _rope(k_ref[rows, :].astype(F32), cosb, sinb)
            kf32[rows, :] = kr
            ks[rows, :] = kr.astype(BF16)
            vs[rows, :] = v_ref[rows, :].astype(BF16)

        dgn = jnp.zeros((1, RET_DIM), F32)
        for t in range(nt):
            lrows = slice(t * c, (t + 1) * c)
            ov = o_ref[lrows, :]
            mu = jnp.mean(ov, axis=-1, keepdims=True)
            oc = ov - mu
            var = jnp.mean(oc * oc, axis=-1, keepdims=True)
            rstd = lax.rsqrt(var + NORM_EPS)
            yh = oc * rstd
            g = g_ref[lat(t), :].astype(F32)
            sg = _sigmoid(g)
            dl = dl_ref[lrows, :]
            d_ref[3, lat(t), :] = (dl * (yh * gnv) * (sg * (1.0 + g * (1.0 - sg)))).astype(BF16)
            dls = dl * (g * sg)
            dgn = dgn + jnp.sum(dls * yh, axis=0, keepdims=True)
            dyh = dls * gnv
            do = rstd * (dyh - jnp.mean(dyh, axis=-1, keepdims=True)
                         - yh * jnp.mean(dyh * yh, axis=-1, keepdims=True))
            dos[lrows, :] = do.astype(BF16)
        dgn_ref[...] = jnp.concatenate([dgn, jnp.zeros((SUBLANES - 1, RET_DIM), F32)], axis=0)
        d_ref[3, 0:C, :] = jnp.zeros((C, RET_DIM), BF16)
        d_ref[0, 0:C, :] = jnp.zeros((C, RET_DIM), BF16)

        dec, dec_c, step_f, step_b = _ret_states(kf32, vs, lgf, lgb, C, c, nt, hf, hb, hfa, hba)

        def zmat(t, qdec):
            return _dot_tn((qf32[lat(t), :] * qdec).astype(BF16), dos[t * c:(t + 1) * c, :])

        acc3f = jnp.zeros((RET_DIM, RET_DIM), F32)
        acc3b = jnp.zeros((RET_DIM, RET_DIM), F32)
        state = jnp.zeros((RET_DIM, RET_DIM), F32)
        for t in range(nt - 1, -1, -1):
            gf_s[t] = state.astype(BF16)
            z = zmat(t, dec.q_f)
            acc3f = acc3f + hfa[t] * z
            state = step_f * state + z
        gctx_f = state.astype(BF16)
        state = jnp.zeros((RET_DIM, RET_DIM), F32)
        for t in range(nt):
            gb_s[t] = state.astype(BF16)
            z = zmat(t, dec.q_b)
            acc3b = acc3b + hba[t] * z
            state = step_b * state + z
        gctx_b = state.astype(BF16)

        rc = (lax.broadcasted_iota(jnp.int32, (c, c), 0) - lax.broadcasted_iota(jnp.int32, (c, c), 1)).astype(F32)
        w_diag = _ret_weights(lgf, lgb, rc)
        wg_f = jnp.where(rc >= 0.0, w_diag * rc, 0.0)
        wg_b = jnp.where(rc < 0.0, -w_diag * rc, 0.0)
        accf = jnp.zeros((SUBLANES, RET_DIM), F32)
        accb = jnp.zeros((SUBLANES, RET_DIM), F32)
        gdf = jnp.zeros((SUBLANES, c), F32)
        gdb = jnp.zeros((SUBLANES, c), F32)
        for t in range(nt):
            rows = lat(t)
            qt = qs[rows, :]
            kt = ks[rows, :]
            vt = vs[rows, :]
            dot = dos[t * c:(t + 1) * c, :]
            s = _dot_nt(qt, kt)
            dp = _dot_nt(dot, vt)
            dv = _dot_tn((s * w_diag).astype(BF16), dot)
            ds = (dp * w_diag).astype(BF16)
            dq = _dot(ds, kt)
            dk = _dot_tn(ds, qt)
            gs = dp * s
            gdf = gdf + fold(gs * wg_f)
            gdb = gdb + fold(gs * wg_b)
            qv = qf32[rows, :]
            kv = kf32[rows, :]
            dq_f = dec.q_f * _dot_nt(dot, hf[t])
            dq_b = dec.q_b * _dot_nt(dot, hb[t])
            dk_f = dec.k_f * _dot_nt(vt, gf_s[t])
            dk_b = dec.k_b * _dot_nt(vt, gb_s[t])
            accf = accf + fold(dec.head * dq_f * qv) + fold(dec.tail * dk_f * kv)
            accb = accb + fold(dec.tail * dq_b * qv) + fold(dec.head * dk_b * kv)
            dv = dv + dec.k_f * _dot(kt, gf_s[t]) + dec.k_b * _dot(kt, gb_s[t])
            cosb = cos_ref[rows, :]
            sinb = sin_ref[rows, :]
            d_ref[0, rows, :] = _unrope((dq + dq_f + dq_b) * scale, cosb, sinb).astype(BF16)
            d_ref[1, rows, :] = _unrope(dk + dk_f + dk_b, cosb, sinb).astype(BF16)
            d_ref[2, rows, :] = dv.astype(BF16)
        kc = ks[0:C, :]
        vc = vs[0:C, :]
        kcv = kf32[0:C, :]
        dkc_f = dec_c.k_f * _dot_nt(vc, gctx_f)
        dkc_b = dec_c.k_b * _dot_nt(vc, gctx_b)
        accf = accf + fold(dec_c.tail * dkc_f * kcv)
        accb = accb + fold(dec_c.head * dkc_b * kcv)
        d_ref[1, 0:C, :] = (dkc_f + dkc_b).astype(BF16)
        d_ref[2, 0:C, :] = (dec_c.k_f * _dot(kc, gctx_f) + dec_c.k_b * _dot(kc, gctx_b)).astype(BF16)
        gf = jnp.sum(gdf) + jnp.sum(accf) + jnp.sum(acc3f)
        gb = jnp.sum(gdb) + jnp.sum(accb) + jnp.sum(acc3b)
        row = lax.broadcasted_iota(jnp.int32, (SUBLANES, LANES), 0)
        dlg_ref[...] = jnp.where(row == 0, gf, jnp.where(row == 1, gb, 0.0))

    def col(seg):
        return pl.BlockSpec((None, T, RET_DIM), lambda b, h, seg=seg: (b, 0, seg * RET_HEADS + h))

    return _grid_call(
        body, name="ret_bwd", grid=(B, RET_HEADS),
        out_shape=(jax.ShapeDtypeStruct((B, 4, T, RET_WIDTH), BF16),
                   jax.ShapeDtypeStruct((B, SUBLANES, RET_WIDTH), F32),
                   jax.ShapeDtypeStruct((B, RET_HEADS, SUBLANES, LANES), F32)),
        in_specs=[pl.BlockSpec(memory_space=pltpu.SMEM), col(0), col(1), col(2), col(3),
                  pl.BlockSpec((T, RET_DIM), lambda b, h: (0, 0)), pl.BlockSpec((T, RET_DIM), lambda b, h: (0, 0)),
                  pl.BlockSpec((1, RET_DIM), lambda b, h: (0, h)),
                  pl.BlockSpec((None, N, RET_DIM), lambda b, h: (b, 0, h)),
                  pl.BlockSpec((None, N, RET_DIM), lambda b, h: (b, 0, h))],
        out_specs=(pl.BlockSpec((None, 4, T, RET_DIM), lambda b, h: (b, 0, 0, h)),
                   pl.BlockSpec((None, SUBLANES, RET_DIM), lambda b, h: (b, 0, h)),
                   pl.BlockSpec((None, None, SUBLANES, LANES), lambda b, h: (b, h, 0, 0))),
        scratch_shapes=[pltpu.VMEM((T, RET_DIM), BF16)] * 3 + [pltpu.VMEM((N, RET_DIM), BF16)]
                       + [pltpu.VMEM((T, RET_DIM), F32)] * 2
                       + [pltpu.VMEM((nt, RET_DIM, RET_DIM), BF16)] * 2 + [pltpu.VMEM((nt, RET_DIM, RET_DIM), F32)] * 2
                       + [pltpu.VMEM((nt, RET_DIM, RET_DIM), BF16)] * 2,
        args=(lg, proj, proj, proj, proj, cos, sin, gn, o, dlat))


def _na_geometry(rows):
    kh = min(NA_KH, rows)
    return kh, kh * GRID_W


def _pair_select():
    lane = lax.broadcasted_iota(jnp.int32, (2 * GRID_W, LANES), 1)
    row = lax.broadcasted_iota(jnp.int32, (2 * GRID_W, LANES), 0)
    return (lane >= NA_DIM) == (row >= GRID_W)


def _pair_bias(bias_ref, dr0, kh):
    return jnp.concatenate(
        [jnp.concatenate([bias_ref[e, pl.ds(dr0 + 2 * m, 1)].reshape(GRID_W, LANES) for m in range(kh // 2)], axis=1)
         for e in range(2)], axis=0)


def _na_softmax(s_loc, s_ctx):
    mx = jnp.maximum(jnp.max(s_loc, axis=-1, keepdims=True), jnp.max(s_ctx, axis=-1, keepdims=True))
    p_loc = jnp.exp(s_loc - mx)
    p_ctx = jnp.exp(s_ctx - mx)
    den = jnp.sum(p_loc, axis=-1, keepdims=True) + jnp.sum(p_ctx, axis=-1, keepdims=True)
    return p_loc, p_ctx, den


def _na_fwd(proj, bias2, n_ctx):
    B, T, _ = proj.shape
    C = n_ctx
    N = T - C
    R = N // GRID_W
    kh, nk = _na_geometry(R)
    scale = NA_DIM ** -0.5
    base = (4 * RET_WIDTH) // LANES

    def body(q_ref, k_ref, v_ref, bias_ref, out_ref, p_ref, kb16, vb16):
        kb16[...] = k_ref[...].astype(BF16)
        vb16[...] = v_ref[...].astype(BF16)
        kc = kb16[0:C, :]
        vc = vb16[0:C, :]
        lane = lax.broadcasted_iota(jnp.int32, (GRID_W, LANES), 1)
        sel2 = _pair_select()

        def group(gi, carry):
            pre = []
            for u in range(NA_GROUP):
                r = gi * NA_GROUP + u
                bs = jnp.clip(r - kh // 2, 0, R - kh)
                dr0 = bs - r + (NA_KH - 1)
                q = q_ref[pl.ds(pl.multiple_of(C + r * GRID_W, GRID_W), GRID_W), :].astype(F32) * scale
                q2 = jnp.where(sel2, jnp.concatenate([q, q], axis=0), 0.0).astype(BF16)
                band = pl.ds(pl.multiple_of(C + bs * GRID_W, GRID_W), nk)
                s_loc = _dot_nt(q2, kb16[band, :]) + _pair_bias(bias_ref, dr0, kh)
                s_ctx = _dot_nt(q2, kc)
                pre.append((r, band, s_loc, s_ctx))
            mid = [(r, band) + _na_softmax(s_loc, s_ctx) for r, band, s_loc, s_ctx in pre]
            for r, band, p_loc, p_ctx, den in mid:
                inv = 1.0 / den
                pb_loc = (p_loc * inv).astype(BF16)
                pb_ctx = (p_ctx * inv).astype(BF16)
                p_ref[r, :, 0:nk] = pb_loc
                p_ref[r, :, nk:] = pb_ctx
                o2 = _dot(pb_loc, vb16[band, :]) + _dot(pb_ctx, vc)
                out_ref[pl.ds(pl.multiple_of(r * GRID_W, GRID_W), GRID_W), :] = jnp.where(
                    lane < NA_DIM, o2[:GRID_W], o2[GRID_W:]).astype(BF16)
            return carry

        lax.fori_loop(0, R // NA_GROUP, group, 0)

    def col(seg):
        return pl.BlockSpec((None, T, LANES), lambda b, p, seg=seg: (b, 0, base + seg * NA_PAIRS + p))

    return _grid_call(
        body, name="na_fwd", grid=(B, NA_PAIRS),
        out_shape=(jax.ShapeDtypeStruct((B, N, NA_WIDTH), BF16),
                   jax.ShapeDtypeStruct((B, NA_PAIRS, R, 2 * GRID_W, nk + C), BF16)),
        in_specs=[col(0), col(1), col(2),
                  pl.BlockSpec((2, 2 * NA_KH - 2, GRID_W, LANES), lambda b, p: (p, 0, 0, 0))],
        out_specs=(pl.BlockSpec((None, N, LANES), lambda b, p: (b, 0, p)),
                   pl.BlockSpec((None, None, R, 2 * GRID_W, nk + C), lambda b, p: (b, p, 0, 0, 0))),
        scratch_shapes=[pltpu.VMEM((T, LANES), BF16)] * 2,
        args=(proj, proj, proj, bias2))


def _na_bwd(proj, probs, dlat, n_ctx):
    B, T, _ = proj.shape
    C = n_ctx
    N = T - C
    R = N // GRID_W
    kh, nk = _na_geometry(R)
    scale = NA_DIM ** -0.5
    base = (4 * RET_WIDTH) // LANES

    def body(q_ref, k_ref, v_ref, p_ref, dl_ref, d_ref, db_ref, kb16, vb16, dkv):
        b = pl.program_id(1)
        kb16[...] = k_ref[...].astype(BF16)
        vb16[...] = v_ref[...].astype(BF16)
        kc = kb16[0:C, :]
        vc = vb16[0:C, :]
        lane = lax.broadcasted_iota(jnp.int32, (GRID_W, LANES), 1)
        dkv[...] = jnp.zeros(dkv.shape, F32)
        d_ref[0, 0:C, :] = jnp.zeros((C, LANES), BF16)

        @pl.when(b == 0)
        def _():
            db_ref[...] = jnp.zeros(db_ref.shape, F32)

        sel2 = _pair_select()

        def group(gi, carry):
            pre = []
            for u in range(NA_GROUP):
                r = gi * NA_GROUP + u
                bs = jnp.clip(r - kh // 2, 0, R - kh)
                dr0 = bs - r + (NA_KH - 1)
                q = q_ref[pl.ds(pl.multiple_of(C + r * GRID_W, GRID_W), GRID_W), :].astype(F32) * scale
                do = dl_ref[pl.ds(pl.multiple_of(r * GRID_W, GRID_W), GRID_W), :]
                q2 = jnp.where(sel2, jnp.concatenate([q, q], axis=0), 0.0).astype(BF16)
                do2 = jnp.where(sel2, jnp.concatenate([do, do], axis=0), 0.0).astype(BF16)
                band = pl.ds(pl.multiple_of(C + bs * GRID_W, GRID_W), nk)
                dp_loc = _dot_nt(do2, vb16[band, :])
                dp_ctx = _dot_nt(do2, vc)
                pre.append((r, dr0, band, q2, do2, dp_loc, dp_ctx))
            mid = []
            for r, dr0, band, q2, do2, dp_loc, dp_ctx in pre:
                pb_loc = p_ref[r, :, 0:nk]
                pb_ctx = p_ref[r, :, nk:]
                p_loc = pb_loc.astype(F32)
                p_ctx = pb_ctx.astype(F32)
                delta = (jnp.sum(p_loc * dp_loc, axis=-1, keepdims=True)
                         + jnp.sum(p_ctx * dp_ctx, axis=-1, keepdims=True))
                ds_loc = p_loc * (dp_loc - delta)
                ds_ctx = p_ctx * (dp_ctx - delta)
                mid.append((r, dr0, band, q2, do2, pb_loc, pb_ctx, ds_loc, ds_ctx))
            for r, dr0, band, q2, do2, pb_loc, pb_ctx, ds_loc, ds_ctx in mid:
                dsb_loc = ds_loc.astype(BF16)
                dsb_ctx = ds_ctx.astype(BF16)
                dq2 = _dot(dsb_loc, kb16[band, :]) + _dot(dsb_ctx, kc)
                d_ref[0, pl.ds(pl.multiple_of(C + r * GRID_W, GRID_W), GRID_W), :] = (jnp.where(
                    lane < NA_DIM, dq2[:GRID_W], dq2[GRID_W:]) * scale).astype(BF16)
                dkv[0, band, :] += _dot_tn(dsb_loc, q2)
                dkv[1, band, :] += _dot_tn(pb_loc, do2)
                dkv[0, 0:C, :] += _dot_tn(dsb_ctx, q2)
                dkv[1, 0:C, :] += _dot_tn(pb_ctx, do2)
                for e in range(2):
                    for m in range(kh // 2):
                        db_ref[e, pl.ds(dr0 + 2 * m, 1)] += ds_loc[e * GRID_W:(e + 1) * GRID_W,
                                                                   m * LANES:(m + 1) * LANES].reshape(1, GRID_W, LANES)
            return carry

        lax.fori_loop(0, R // NA_GROUP, group, 0)
        d_ref[1] = dkv[0].astype(BF16)
        d_ref[2] = dkv[1].astype(BF16)

    def col(seg):
        return pl.BlockSpec((None, T, LANES), lambda p, b, seg=seg: (b, 0, base + seg * NA_PAIRS + p))

    return _grid_call(
        body, name="na_bwd", grid=(NA_PAIRS, B),
        out_shape=(jax.ShapeDtypeStruct((B, 3, T, NA_WIDTH), BF16),
                   jax.ShapeDtypeStruct((NA_HEADS, 2 * NA_KH - 2, GRID_W, LANES), F32)),
        in_specs=[col(0), col(1), col(2),
                  pl.BlockSpec((None, None, R, 2 * GRID_W, nk + C), lambda p, b: (b, p, 0, 0, 0)),
                  pl.BlockSpec((None, N, LANES), lambda p, b: (b, 0, p))],
        out_specs=(pl.BlockSpec((None, 3, T, LANES), lambda p, b: (b, 0, 0, p)),
                   pl.BlockSpec((2, 2 * NA_KH - 2, GRID_W, LANES), lambda p, b: (p, 0, 0, 0))),
        scratch_shapes=[pltpu.VMEM((T, LANES), BF16)] * 2 + [pltpu.VMEM((2, T, LANES), F32)],
        args=(proj, proj, proj, probs, dlat))


def _split3(a):
    hi = a.astype(BF16)
    r1 = a - hi.astype(F32)
    mid = r1.astype(BF16)
    lo = (r1 - mid.astype(F32)).astype(BF16)
    return hi, mid, lo


def _rpb_reduce(dbias2, onehot2):
    rows = dbias2.shape[0] * dbias2.shape[1]
    flat = dbias2.reshape(rows, GRID_W * LANES)

    def body(a_ref, oh_ref, o_ref):
        hi, mid, lo = _split3(a_ref[...])
        oh = oh_ref[...]
        o_ref[...] = _dot(hi, oh) + _dot(mid, oh) + _dot(lo, oh)

    return pl.pallas_call(
        body, name="rpb_reduce", out_shape=jax.ShapeDtypeStruct((rows, LANES), F32),
        in_specs=[_vmem(), _vmem()], out_specs=_vmem(),
        compiler_params=pltpu.CompilerParams(vmem_limit_bytes=VMEM_LIMIT),
    )(flat, onehot2)


def _dense_core(lat_ret, lat_na, x, tgt, modl, g_post_mix, g_pre_mlp, g_post_mlp, w_out, w1, w2):
    B, N, D = x.shape
    F = w1.shape[1]
    wout_rows, w1_cols, w2_rows = w_out.shape[0] // N_DEV, w1.shape[1] // N_DEV, w2.shape[0] // N_DEV
    mixw = w_out.shape[0]
    half = mixw // 2
    tm = _div_tile(N, 256, 16)
    nt = N // tm
    fc = _div_tile(F, 1024, LANES)

    def body(lr_ref, ln_ref, x_ref, t_ref, gt1_ref, sh2_ref, sc2_ref, gt2_ref, gpm_ref, gpre_ref, gpo_ref,
             wout_part, w1_part, w2_part,
             dy1_ref, dlr_ref, dln_ref, dmix_ref, h2_ref, a_ref, du_ref, dz_ref, red_ref, wout_hbm, w1_hbm, w2_hbm,
             wout_v, w1_v, w2_v, u_s, sems, fsend, frecv):
        @pl.when((pl.program_id(0) == 0) & (pl.program_id(1) == 0))
        def _():
            relay = [(_row_block(wout_part, wout_rows), _row_block(wout_hbm, wout_rows)),
                     (_col_block(w1_part, w1_cols), _col_block(w1_hbm, w1_cols)),
                     (_row_block(w2_part, w2_rows), _row_block(w2_hbm, w2_rows))]
            _forward_start(relay, fsend, frecv)
            _forward_wait(relay, fsend, frecv)
            cps = [pltpu.make_async_copy(wout_hbm, wout_v, sems.at[0]),
                   pltpu.make_async_copy(w1_hbm, w1_v, sems.at[1]),
                   pltpu.make_async_copy(w2_hbm, w2_v, sems.at[2])]
            for cp in cps:
                cp.start()
            for cp in cps:
                cp.wait()

        @pl.when(pl.program_id(1) == 0)
        def _():
            red_ref[...] = jnp.zeros(red_ref.shape, F32)

        gt1 = gt1_ref[...]
        sh2 = sh2_ref[...]
        sc2 = sc2_ref[...]
        gt2 = gt2_ref[...]
        gpm = gpm_ref[...]
        gpre = gpre_ref[...]
        gpo = gpo_ref[...]

        def rowmean(a):
            return jnp.mean(a, axis=-1, keepdims=True)

        def colsum(a):
            return jnp.sum(a, axis=0, keepdims=True)

        mix_gain = gt1 * gpm
        mlp_in_gain = gpre * (1.0 + sc2)
        mlp_out_gain = gt2 * gpo
        mix = _dot(lr_ref[...], wout_v[0:half, :]) + _dot(ln_ref[...], wout_v[half:, :])
        x = x_ref[...]
        rm = lax.rsqrt(rowmean(mix * mix) + NORM_EPS)
        mh = mix * rm
        y1 = x + mh * mix_gain
        r1 = lax.rsqrt(rowmean(y1 * y1) + NORM_EPS)
        xh = y1 * r1
        h2b = (xh * mlp_in_gain + sh2).astype(BF16)
        h2_ref[...] = h2b
        z = jnp.zeros((tm, D), F32)
        for c0 in range(0, F, fc):
            u = _dot(h2b, w1_v[:, c0:c0 + fc])
            u_s[:, c0:c0 + fc] = u
            ru = jnp.maximum(u, 0.0)
            ab = (ru * ru).astype(BF16)
            a_ref[:, c0:c0 + fc] = ab
            z = z + _dot(ab, w2_v[c0:c0 + fc, :])
        r2 = lax.rsqrt(rowmean(z * z) + NORM_EPS)
        zh = z * r2
        y2 = y1 + zh * mlp_out_gain
        err = y2 - t_ref[...]
        loss = 0.5 * jnp.sum(rowmean(err * err))
        dy2 = err * (1.0 / D)
        s_out = colsum(dy2 * zh)
        red_ref[2:3, :] += s_out * gpo
        red_ref[6:7, :] += s_out * gt2
        dzh = dy2 * mlp_out_gain
        dz = r2 * (dzh - zh * rowmean(dzh * zh))
        dzb = dz.astype(BF16)
        dz_ref[...] = dzb
        dh2 = jnp.zeros((tm, D), F32)
        for c0 in range(0, F, fc):
            da = _dot_nt(dzb, w2_v[c0:c0 + fc, :])
            dub = (da * (2.0 * jnp.maximum(u_s[:, c0:c0 + fc], 0.0))).astype(BF16)
            du_ref[:, c0:c0 + fc] = dub
            dh2 = dh2 + _dot_nt(dub, w1_v[:, c0:c0 + fc])
        s_in = colsum(dh2 * xh)
        red_ref[3:4, :] += s_in * gpre
        red_ref[4:5, :] += colsum(dh2)
        red_ref[5:6, :] += s_in * (1.0 + sc2)
        dxh = dh2 * mlp_in_gain
        dy1 = dy2 + r1 * (dxh - xh * rowmean(dxh * xh))
        dy1_ref[...] = dy1
        s_mix = colsum(dy1 * mh)
        red_ref[0:1, :] += s_mix * gpm
        red_ref[1:2, :] += s_mix * gt1
        dmh = dy1 * mix_gain
        dmix = (rm *(dmh - mh * rowmean(dmh * mh))).astype(BF16)
        dmix_ref[...] = dmix
        dlr_ref[...] = _dot_nt(dmix, wout_v[0:half, :])
        dln_ref[...] = _dot_nt(dmix, wout_v[half:, :])
        red_ref[7:8, :] += jnp.zeros((1, D), F32) + loss

    def tok(w):
        return pl.BlockSpec((None, tm, w), lambda b, t: (b, t, 0))

    def mod(k):
        return pl.BlockSpec((None, None, 1, D), lambda b, t, k=k: (b, k, 0, 0))

    def vec():
        return pl.BlockSpec((1, D), lambda b, t: (0, 0))

    return pl.pallas_call(
        body, name="dense_core", grid=(B, nt),
        out_shape=(jax.ShapeDtypeStruct((B, N, D), F32), jax.ShapeDtypeStruct((B, N, half), F32),
                   jax.ShapeDtypeStruct((B, N, half), F32), jax.ShapeDtypeStruct((B, N, D), BF16),
                   jax.ShapeDtypeStruct((B, N, D), BF16), jax.ShapeDtypeStruct((B, N, F), BF16),
                   jax.ShapeDtypeStruct((B, N, F), BF16), jax.ShapeDtypeStruct((B, N, D), BF16),
                   jax.ShapeDtypeStruct((B, SUBLANES, D), F32),
                   jax.ShapeDtypeStruct(w_out.shape, w_out.dtype), jax.ShapeDtypeStruct(w1.shape, w1.dtype),
                   jax.ShapeDtypeStruct(w2.shape, w2.dtype)),
        in_specs=[tok(half), tok(half), tok(D), tok(D), mod(2), mod(3), mod(4), mod(5), vec(), vec(), vec(),
                  _any(), _any(), _any()],
        out_specs=(tok(D), tok(half), tok(half), tok(D), tok(D), tok(F), tok(F), tok(D),
                   pl.BlockSpec((None, SUBLANES, D), lambda b, t: (b, 0, 0)), _any(), _any(), _any()),
        scratch_shapes=[pltpu.VMEM((mixw, D), BF16), pltpu.VMEM((D, F), BF16), pltpu.VMEM((F, D), BF16),
                        pltpu.VMEM((tm, F), F32), pltpu.SemaphoreType.DMA((3,)),
                        pltpu.SemaphoreType.DMA((3, 3)), pltpu.SemaphoreType.DMA((3, 3))],
        input_output_aliases={11: 9, 12: 10, 13: 11},
        compiler_params=_params("arbitrary", "arbitrary"),
    )(lat_ret, lat_na, x, tgt, modl, modl, modl, modl, g_post_mix, g_pre_mlp, g_post_mlp, w_out, w1, w2)[:9]


def _inproj_bwd(dret, dna, x, ctx, dy1, modl, g1, w_in_t):
    B, N, D = x.shape
    n_ctx = ctx.shape[1]
    T = n_ctx + N
    tm = _div_tile(n_ctx, 256, 16)
    nct, ctx_spec, lat_spec = _token_tiles(n_ctx, tm)
    nt = T // tm
    nseg_r = dret.shape[1]
    nseg_n = dna.shape[1]
    nw = w_in_t.shape[0]

    def body(*refs):
        seg_refs = refs[:nseg_r + nseg_n]
        c_ref, x_ref, dy1_ref, sc_ref, g_ref, w_ref, dx_ref, red_ref = refs[nseg_r + nseg_n:]
        t = pl.program_id(1)
        dh = jnp.zeros((tm, D), F32)
        for s, ref in enumerate(seg_refs):
            dh = dh + _dot(ref[...], w_ref[s * SEG:(s + 1) * SEG, :])
        x = jnp.where(t < nct, c_ref[...], x_ref[...])
        g = g_ref[...]
        r = lax.rsqrt(jnp.mean(x * x, axis=-1, keepdims=True) + NORM_EPS)
        xh = x * r
        red_ref[0:1, :] = jnp.sum(dh, axis=0, keepdims=True)
        red_ref[1:2, :] = jnp.sum(dh * (xh * g), axis=0, keepdims=True)
        dn = dh * (1.0 + sc_ref[...])
        red_ref[2:3, :] = jnp.sum(dn * xh, axis=0, keepdims=True)
        red_ref[3:, :] = jnp.zeros((SUBLANES - 3, D), F32)
        dxh = dn * g
        dx = r * (dxh - xh * jnp.mean(dxh * xh, axis=-1, keepdims=True))
        dx_ref[...] = dx + jnp.where(t >= nct, dy1_ref[...], 0.0)

    def mrow(b, t):
        return jnp.where(t < nct, B, b)

    def seg(s):
        return pl.BlockSpec((None, None, tm, SEG), lambda b, t, s=s: (b, s, t, 0))

    return _grid_call(
        body, name="inproj_bwd", grid=(B, nt),
        out_shape=(jax.ShapeDtypeStruct((B, N, D), F32), jax.ShapeDtypeStruct((B, nt, SUBLANES, D), F32)),
        in_specs=[seg(s) for s in range(nseg_r)] + [seg(s) for s in range(nseg_n)]
                 + [ctx_spec(D), lat_spec(D), lat_spec(D),
                    pl.BlockSpec((None, None, 1, D), lambda b, t: (mrow(b, t), 1, 0, 0)),
                    pl.BlockSpec((1, D), lambda b, t: (0, 0)),
                    pl.BlockSpec((nw, D), lambda b, t: (0, 0))],
        out_specs=(lat_spec(D), pl.BlockSpec((None, None, SUBLANES, D), lambda b, t: (b, t, 0, 0))),
        scratch_shapes=[], args=(*([dret] * nseg_r), *([dna] * nseg_n), ctx, x, dy1, modl, g1, w_in_t))


def _tn_matmul(lhs, rhs, name, rows_before=0, rows_after=0, into=None):
    B, S, T, W = lhs.shape
    nn = rhs.shape[-1]
    tk = _div_tile(T, 2304, LANES)
    bm = _div_tile(W, 1024, LANES)
    bn = _div_tile(nn, 1024, LANES)
    nkt = T // tk
    nk = B * nkt

    def body(l_ref, r_ref, *rest):
        o_ref, acc = rest[-2:]
        k = pl.program_id(3)

        @pl.when(k == 0)
        def _():
            acc[...] = jnp.zeros(acc.shape, F32)

        acc[...] += _dot_tn(l_ref[...].astype(BF16), r_ref[...].astype(BF16))

        @pl.when(k == nk - 1)
        def _():
            o_ref[...] = acc[...].astype(BF16)

    nwb = W // bm
    first = rows_before // bm
    return pl.pallas_call(
        functools.partial(body), name=name, grid=(S, nwb, nn // bn, nk),
        out_shape=jax.ShapeDtypeStruct((rows_before + S * W + rows_after, nn), BF16),
        in_specs=[pl.BlockSpec((None, None, tk, bm), lambda s, i, j, k: (k // nkt, s, k % nkt, i)),
                  pl.BlockSpec((None, tk, bn), lambda s, i, j, k: (k // nkt, k % nkt, j))]
                 + ([] if into is None else [_any()]),
        out_specs=pl.BlockSpec((bm, bn), lambda s, i, j, k: (first + s * nwb + i, j)),
        scratch_shapes=[pltpu.VMEM((bm, bn), F32)],
        input_output_aliases={} if into is None else {2: 0},
        compiler_params=_params("parallel", "parallel", "parallel", "arbitrary"),
    )(lhs, rhs, *([] if into is None else [into]))


class _SplitScatter:
    def __init__(self, gs, block_ofs, land_shapes, name, kind="scatter", masks=ALL_PEERS):
        self.n = n = len(gs)
        self.block_ofs, self.kind, self.masks = block_ofs, kind, masks
        if kind == "scatter":
            land_shapes = [(N_DEV,) + tuple(bs) for bs in land_shapes]
        hbm = pl.BlockSpec(memory_space=pltpu.HBM)
        sem = pl.BlockSpec(memory_space=pltpu.SEMAPHORE)

        def body(*refs):
            g_refs, land_refs = refs[:n], refs[n:2 * n]
            send_sems, recv_sems, own_sems = refs[2 * n:2 * n + 3]
            token = refs[-1]
            for own, pushes in self._copies(g_refs, land_refs, send_sems, recv_sems, own_sems, landing="sender"):
                own.start()
                for cp in pushes:
                    cp.start()
            token[...] = jnp.zeros_like(token)

        outs = pl.pallas_call(
            body, name=name,
            out_shape=(pltpu.SemaphoreType.DMA((n * (N_DEV - 1),)), pltpu.SemaphoreType.DMA((n * (N_DEV - 1),)),
                       pltpu.SemaphoreType.DMA((n,)))
                      + tuple(pltpu.HBM(g.shape, g.dtype) for g in gs)
                      + tuple(pltpu.HBM(s, g.dtype) for s, g in zip(land_shapes, gs))
                      + (jax.ShapeDtypeStruct((SUBLANES, LANES), F32),),
            in_specs=(hbm,) * (2 * n), out_specs=(sem,) * 3 + (hbm,) * (2 * n) + (_vmem(),),
            input_output_aliases={k: 3 + k for k in range(2 * n)},
            compiler_params=pltpu.CompilerParams(has_side_effects=pltpu.SideEffectType.DATAFLOW_SIDE_EFFECTING),
        )(*[pltpu.with_memory_space_constraint(g, pltpu.HBM) for g in gs],
          *[pltpu.with_memory_space_constraint(lax.empty(s, g.dtype), pltpu.HBM) for s, g in zip(land_shapes, gs)])
        self.sems, self.thru, self.token = outs[:3], outs[3:3 + 2 * n], outs[-1]

    def _copies(self, g_refs, land_refs, send_sems, recv_sems, own_sems, landing):
        me, peers = _me_and_peers()
        out = []
        for k in range(self.n):
            if self.kind == "scatter":
                src, dst = self.block_ofs[k](g_refs[k]), _slot(land_refs[k])
            else:
                src, dst = (lambda p, k=k: g_refs[k]), self.block_ofs[k](land_refs[k])
            own = pltpu.make_async_copy(src(me), dst(me), own_sems.at[k]) if landing == "sender" else None
            pushes = []
            for m in self.masks:
                dev, pid = peers[m - 1]
                i = k * (N_DEV - 1) + m - 1
                pushes.append(_remote(src(pid), dst(me if landing == "sender" else pid),
                                      send_sems.at[i], recv_sems.at[i], dev))
            out.append((own, pushes))
        return out


def _scatter_wait(scatters, after, name):
    hbm = pl.BlockSpec(memory_space=pltpu.HBM)
    sem = pl.BlockSpec(memory_space=pltpu.SEMAPHORE)
    n_arr = [2 * sc.n for sc in scatters]
    total = sum(n_arr)

    def body(*refs):
        arrs, sems = refs[:total], refs[total:total + 3 * len(scatters)]
        a0 = 0
        for j, sc in enumerate(scatters):
            g_refs, land_refs = arrs[a0:a0 + sc.n], arrs[a0 + sc.n:a0 + 2 * sc.n]
            a0 += 2 * sc.n
            send_sems, recv_sems, own_sems = sems[3 * j:3 * j + 3]
            for (own, sent), (_, got) in zip(sc._copies(g_refs, land_refs, send_sems, recv_sems, own_sems, "sender"),
                                             sc._copies(g_refs, land_refs, send_sems, recv_sems, own_sems, "receiver")):
                own.wait()
                for cp in sent:
                    cp.wait_send()
                for cp in got:
                    cp.wait_recv()

    operands = [a for sc in scatters for a in sc.thru]
    outs = pl.pallas_call(
        body, name=name,
        out_shape=tuple(pltpu.HBM(a.shape, a.dtype) for a in operands),
        in_specs=(hbm,) * total + (sem,) * (3 * len(scatters)) + (pl.BlockSpec(memory_space=pl.ANY),),
        out_specs=(hbm,) * total, input_output_aliases={k: k for k in range(total)},
        compiler_params=pltpu.CompilerParams(has_side_effects=pltpu.SideEffectType.DATAFLOW_SIDE_EFFECTING),
    )(*operands, *[s for sc in scatters for s in sc.sems], after)
    lands, a0 = [], 0
    for sc in scatters:
        lands.extend(outs[a0 + sc.n:a0 + 2 * sc.n])
        a0 += 2 * sc.n
    return lands


def _small_ar(mbuf, silu_all, w_ada, c_ctx, n_mod_rows, n_vec_rows):
    D = silu_all.shape[1]
    ncol = w_ada.shape[1]
    nm = mbuf.shape[2]
    srows = silu_all.shape[0]

    def body(mbuf, s_ref, w_ref, cc_ref, tot_ref, gb_ref, gw_ref, gc_ref, tbuf, dmx, cmrow, send3, recv3):
        me, _ = _me_and_peers()
        msum = mbuf[0]
        for k in range(1, N_DEV):
            msum = msum + mbuf[k]
        tot_ref[...] = msum[n_mod_rows:n_mod_rows + n_vec_rows]
        gb_ref[...] = jnp.sum(msum[0:n_mod_rows], axis=0, keepdims=True)
        loc = pl.ds(pl.multiple_of(me * ncol, ncol), ncol)
        for k in range(N_DEV):
            dmx[k * SUBLANES:(k + 1) * SUBLANES, :] = mbuf[k, :, loc]
        cmrow[...] = msum
        cm_loc = cmrow[n_mod_rows - 1:n_mod_rows, loc]
        dmx[N_DEV * SUBLANES:, :] = jnp.concatenate([cm_loc, jnp.zeros((SUBLANES - 1, ncol), F32)], axis=0)
        gw_ref[...] = _dot_tn(s_ref[...], dmx[...])
        tbuf[me] = _dot_nt(dmx[N_DEV * SUBLANES:, :], w_ref[...])
        _exchange(lambda p: tbuf.at[me], lambda p: tbuf.at[p], send3, recv3)
        tsum = tbuf[0]
        for k in range(1, N_DEV):
            tsum = tsum + tbuf[k]
        cc = cc_ref[...]
        sg = _sigmoid(cc)
        gc_ref[...] = tsum[0:1, :] * (sg * (1.0 + cc * (1.0 - sg)))

    return pl.pallas_call(
        body, name="small_ar",
        out_shape=(jax.ShapeDtypeStruct((n_vec_rows, nm), F32), jax.ShapeDtypeStruct((1, nm), F32),
                   jax.ShapeDtypeStruct((D, ncol), F32), jax.ShapeDtypeStruct((1, D), F32)),
        in_specs=[_vmem()] * 4, out_specs=(_vmem(),) * 4,
        scratch_shapes=[pltpu.VMEM((N_DEV, SUBLANES, D), F32), pltpu.VMEM((srows, ncol), F32),
                        pltpu.VMEM((SUBLANES, nm), F32)] + [pltpu.SemaphoreType.DMA((N_DEV - 1,))] * 2,
        compiler_params=pltpu.CompilerParams(vmem_limit_bytes=VMEM_LIMIT),
    )(mbuf, silu_all, w_ada, c_ctx.reshape(1, D))


def _adam_update(w, g, m, v):
    mn = ADAM_B1 * m + (1.0 - ADAM_B1) * g
    vn = ADAM_B2 * v + (1.0 - ADAM_B2) * (g * g)
    m_hat = mn / (1.0 - ADAM_B1 ** ADAM_STEP)
    v_hat = vn / (1.0 - ADAM_B2 ** ADAM_STEP)
    return -ADAM_LR * (m_hat / (jnp.sqrt(v_hat) + ADAM_EPS) + ADAM_WD * w), mn, vn


def _adamw(w, g, m, v, name):
    rows, cols = w.shape
    tr = _div_tile(rows, 512, SUBLANES)

    def body(w_ref, g_ref, m_ref, v_ref, d_ref, nm_ref, nv_ref):
        d_ref[...], nm_ref[...], nv_ref[...] = _adam_update(w_ref[...], g_ref[...], m_ref[...], v_ref[...])

    spec = pl.BlockSpec((tr, cols), lambda i: (i, 0))
    return pl.pallas_call(
        functools.partial(body), name=name, grid=(rows // tr,),
        out_shape=(jax.ShapeDtypeStruct((rows, cols), F32),) * 3,
        in_specs=[spec] * 4, out_specs=(spec,) * 3,
        compiler_params=_params("parallel"),
    )(w, g, m, v)


def _adamw_small(items, name):
    n = len(items)

    def body(*refs):
        ins, outs = refs[:4 * n], refs[4 * n:]
        for i in range(n):
            w_ref, g_ref, m_ref, v_ref = ins[4 * i:4 * i + 4]
            outs[3 * i][...], outs[3 * i + 1][...], outs[3 * i + 2][...] = _adam_update(
                w_ref[...], g_ref[...], m_ref[...], v_ref[...])

    outs = pl.pallas_call(
        body, name=name,
        out_shape=tuple(jax.ShapeDtypeStruct(it[0].shape, F32) for it in items for _ in range(3)),
        in_specs=[_vmem()] * (4 * n), out_specs=(_vmem(),) * (3 * n),
        compiler_params=pltpu.CompilerParams(vmem_limit_bytes=VMEM_LIMIT),
    )(*[a for it in items for a in it])
    return [tuple(outs[3 * i:3 * i + 3]) for i in range(n)]


def _sum_adamw(buf, w, m, v, name):
    _, rows, cols = buf.shape
    tr = _div_tile(rows, 256, 2 * SUBLANES)

    def body(b_ref, w_ref, m_ref, v_ref, g_ref, d_ref, nm_ref, nv_ref):
        g = b_ref[0].astype(F32)
        for k in range(1, N_DEV):
            g = g + b_ref[k].astype(F32)
        g_ref[...] = g
        d_ref[...], nm_ref[...], nv_ref[...] = _adam_update(w_ref[...], g, m_ref[...], v_ref[...])

    spec = pl.BlockSpec((tr, cols), lambda i: (i, 0))
    return pl.pallas_call(
        functools.partial(body), name=name, grid=(rows // tr,),
        out_shape=(jax.ShapeDtypeStruct((rows, cols), F32),) * 4,
        in_specs=[pl.BlockSpec((N_DEV, tr, cols), lambda i: (0, i, 0))] + [spec] * 3, out_specs=(spec,) * 4,
        compiler_params=_params("parallel"),
    )(buf, w, m, v)


def _rope_tables(n_ctx, n):
    n_freq = RET_DIM // 4
    inv = np.float32(ROPE_BASE) ** (-np.arange(n_freq, dtype=np.float32) / np.float32(n_freq))
    tok = np.arange(n)
    pos_r = (tok // GRID_W).astype(np.float32)
    pos_c = (tok % GRID_W).astype(np.float32)
    ang_r = (pos_r[:, None] * inv[None, :]).astype(np.float32)
    ang_c = (pos_c[:, None] * inv[None, :]).astype(np.float32)
    cos = np.concatenate([np.cos(ang_r), np.cos(ang_r), np.cos(ang_c), np.cos(ang_c)], axis=-1)
    sin = np.concatenate([-np.sin(ang_r), np.sin(ang_r), -np.sin(ang_c), np.sin(ang_c)], axis=-1)
    cos = np.concatenate([np.ones((n_ctx, RET_DIM), np.float32), cos], axis=0)
    sin = np.concatenate([np.zeros((n_ctx, RET_DIM), np.float32), sin], axis=0)
    return jnp.asarray(cos, F32), jnp.asarray(sin, F32)


def _na_tables():
    q = np.arange(GRID_W)[:, None]
    k = np.arange(GRID_W)[None, :]
    start = np.clip(q - NA_KW // 2, 0, GRID_W - NA_KW)
    valid = (k >= start) & (k < start + NA_KW)
    dc = np.clip(k - q + (NA_KW - 1), 0, 2 * NA_KW - 2)
    ncls = 2 * NA_KW - 1
    onehot = (dc[None] == np.arange(ncls)[:, None, None]) & valid[None]
    oh2 = np.zeros((GRID_W, LANES, LANES), np.float32)
    for c in range(ncls):
        oh2[:, :GRID_W, c] = onehot[c]
        oh2[:, GRID_W:, 32 + c] = onehot[c]
    return onehot.astype(np.float32), valid, oh2.reshape(GRID_W * LANES, LANES)


def _paired_bias(rpb, onehot, valid):
    ncls = onehot.shape[0]
    pair = np.zeros((2 * ncls, GRID_W, LANES), np.float32)
    pair[:ncls, :, :GRID_W] = onehot
    pair[ncls:, :, GRID_W:] = onehot
    rows = jnp.concatenate([rpb[:, :-1], rpb[:, 1:]], axis=-1)
    t = jnp.einsum("hdc,cqk->hdqk", rows, jnp.asarray(pair), precision=lax.Precision.HIGHEST)
    return jnp.where(jnp.asarray(np.tile(valid, (1, 2)))[None, None], t, NEG_INF)


def kernel(x, c, ctx, c_ctx, w_ada, b_ada, g_pre_mix, g_post_mix, g_pre_mlp, g_post_mlp, w_in, ret_decay, ret_gn, na_rpb, w_out, w_mlp1, w_mlp2, loss_target, m_c_ctx, m_w_ada, m_b_ada, m_g_pre_mix, m_g_post_mix, m_g_pre_mlp, m_g_post_mlp, m_w_in, m_ret_decay, m_ret_gn, m_na_rpb, m_w_out, m_w_mlp1, m_w_mlp2, v_c_ctx, v_w_ada, v_b_ada, v_g_pre_mix, v_g_post_mix, v_g_pre_mlp, v_g_post_mlp, v_w_in, v_ret_decay, v_ret_gn, v_na_rpb, v_w_out, v_w_mlp1, v_w_mlp2):
    B, N, D = x.shape
    C = ctx.shape[1]
    T = C + N

    silu_all, mods_g, win_b, wout_l, w1_l, w2_l = _mod_gather(c, c_ctx, w_ada[0], b_ada, w_in[0].T, w_out[0],
                                                             w_mlp1[0], w_mlp2[0])
    mods_mine = mods_g.transpose(1, 0, 2).reshape(mods_g.shape[1], N_MOD * D)
    modl = jnp.concatenate([mods_mine[:B], mods_mine[SUBLANES:SUBLANES + 1]], axis=0)
    modl = modl.reshape(B + 1, N_MOD, 1, D)
    rin = w_in.shape[2]
    rout, c1, r2 = wout_l.shape[0], w1_l.shape[1], w2_l.shape[0]

    def rows_of(n):
        return lambda ref: _row_block(ref, n)

    def cols_of(n):
        return lambda ref: _col_block(ref, n)

    cos, sin = _rope_tables(C, N)
    onehot, valid, oh2 = _na_tables()
    bias2 = _paired_bias(na_rpb[0], onehot, valid)
    lg = jax.nn.log_sigmoid(ret_decay[0].astype(F32))

    ag = _SplitScatter([wout_l, w1_l, w2_l], [rows_of(rout), cols_of(c1), rows_of(r2)],
                       [(N_DEV * rout, D), (D, N_DEV * c1), (N_DEV * r2, D)], "ag_mlp_start",
                       kind="gather", masks=SIBLING + ICI_SAME_CORE)
    h_all, proj = _inproj_fwd(x, ctx, modl, g_pre_mix + ag.token[0, 0], win_b)
    o_ret, lat_ret = _ret_fwd(proj, cos, sin, lg, ret_gn, C)
    lat_na, na_probs = _na_fwd(proj, bias2, C)
    wout_part, w1_part, w2_part = _scatter_wait([ag], lat_na, "ag_mlp_wait")

    (dy1, dlat_ret, dlat_na, dmix, h2, act, du, dz, red_d) = _dense_core(
        lat_ret, lat_na, x, loss_target, modl, g_post_mix, g_pre_mlp, g_post_mlp, wout_part, w1_part, w2_part)

    gw_out_p = _tn_matmul(lat_ret[:, None], dmix, "gw_out_ret", rows_after=lat_na.shape[-1])
    gw_out_p = _tn_matmul(lat_na[:, None], dmix, "gw_out_na", rows_before=lat_ret.shape[-1], into=gw_out_p)
    gw1_p = _tn_matmul(h2[:, None], du, "gw_mlp1")
    gw2_p = _tn_matmul(act[:, None], dz, "gw_mlp2")
    rs_mlp = _SplitScatter([gw_out_p, gw1_p, gw2_p], [rows_of(rout), cols_of(c1), rows_of(r2)],
                           [(rout, D), (D, c1), (r2, D)], "rs_mlp_start")

    dret, dgn_p, dlg_p = _ret_bwd(proj, cos, sin, lg, ret_gn + rs_mlp.token[0, 0], o_ret, dlat_ret, C)
    dna, dbias2 = _na_bwd(proj, na_probs, dlat_na, C)
    ret_cols, na_cols = dret.shape[1] * dret.shape[3], dna.shape[1] * dna.shape[3]
    gwin_t_p = _tn_matmul(dret, h_all, "gw_in_ret", rows_after=na_cols)
    gwin_t_p = _tn_matmul(dna, h_all, "gw_in_na", rows_before=ret_cols, into=gwin_t_p)
    rs_in = _SplitScatter([gwin_t_p], [rows_of(rin)], [(rin, D)], "rs_w_in_start")
    grad_x, red_i = _inproj_bwd(dret, dna, x, ctx, dy1, modl, g_pre_mix + rs_in.token[0, 0], win_b)

    rd = red_d
    nct = red_i.shape[1] * C // T
    ri_ctx = red_i[:, :nct].sum(axis=(0, 1))
    ri_lat = red_i[:, nct:].sum(axis=1)
    d_mods = jnp.concatenate([ri_lat[:, 0], ri_lat[:, 1], rd[:, 0], rd[:, 4], rd[:, 3], rd[:, 2]], axis=-1)
    d_cmods = jnp.concatenate([ri_ctx[0], ri_ctx[1], jnp.zeros(((N_MOD - 2) * D,), F32)])[None]
    dg_pre_mix = ri_lat[:, 2].sum(axis=0) + ri_ctx[2]
    dg_post_mix = rd[:, 1].sum(axis=0)
    dg_pre_mlp = rd[:, 5].sum(axis=0)
    dg_post_mlp = rd[:, 6].sum(axis=0)
    loss_p = rd[:, 7, 0].sum()
    d_gn = dgn_p[:, 0].sum(axis=0)
    d_lg = dlg_p[:, :, :2, 0].sum(axis=0).T
    d_decay = d_lg * jax.nn.sigmoid(-ret_decay[0].astype(F32))
    rr = _rpb_reduce(dbias2, jnp.asarray(oh2, BF16)).reshape(NA_HEADS, 2 * NA_KH - 2, LANES)
    ncls = 2 * NA_KW - 1
    d_rpb = (jnp.pad(rr[:, :, :ncls], ((0, 0), (0, 1), (0, 0))) + jnp.pad(rr[:, :, 32:32 + ncls], ((0, 0), (1, 0), (0, 0))))
    d_rpb32 = jnp.pad(d_rpb, ((0, 0), (0, 0), (0, 32 - ncls)))
    pieces = [dg_pre_mix, dg_post_mix, dg_pre_mlp, dg_post_mlp, d_gn, d_rpb32.reshape(-1),
              jnp.pad(d_decay.reshape(-1), (0, LANES - d_decay.size)), jnp.full((LANES,), loss_p, F32)]
    vec = jnp.concatenate(pieces)
    nm = N_MOD * D
    n_vec_rows = -(-vec.shape[0] // nm)
    assert B + 1 + n_vec_rows <= SUBLANES
    vec = jnp.pad(vec, (0, n_vec_rows * nm - vec.shape[0])).reshape(n_vec_rows, nm)
    dm_slot = jnp.concatenate([d_mods, d_cmods, vec, jnp.zeros((SUBLANES - B - 1 - n_vec_rows, nm), F32)], axis=0)
    def whole(ref):
        return lambda p: ref

    small = _SplitScatter([dm_slot], [whole], [dm_slot.shape], "small_start")
    land_out, land_1, land_2, land_in = _scatter_wait([rs_mlp, rs_in], small.token, "rs_wait")
    fused = {"w_in": [a.T for a in _sum_adamw(land_in, w_in[0].T, m_w_in[0].T, v_w_in[0].T, "sum_adamw_w_in")],
             "w_out": _sum_adamw(land_out, w_out[0], m_w_out[0], v_w_out[0], "sum_adamw_w_out"),
             "w_mlp1": _sum_adamw(land_1, w_mlp1[0], m_w_mlp1[0], v_w_mlp1[0], "sum_adamw_w_mlp1"),
             "w_mlp2": _sum_adamw(land_2, w_mlp2[0], m_w_mlp2[0], v_w_mlp2[0], "sum_adamw_w_mlp2")}
    (mbuf,) = _scatter_wait([small], fused["w_mlp2"][0], "small_wait")
    tot, g_b_ada, g_w_ada, g_c_ctx = _small_ar(mbuf, silu_all, w_ada[0], c_ctx, B + 1, n_vec_rows)
    flat = tot.reshape(-1)
    o0 = 0
    g_pre_mix_g = flat[o0:o0 + D]; o0 += D
    g_post_mix_g = flat[o0:o0 + D]; o0 += D
    g_pre_mlp_g = flat[o0:o0 + D]; o0 += D
    g_post_mlp_g = flat[o0:o0 + D]; o0 += D
    g_gn = flat[o0:o0 + RET_WIDTH]; o0 += RET_WIDTH
    nrpb = NA_HEADS * (2 * NA_KH - 1) * 32
    g_rpb = flat[o0:o0 + nrpb].reshape(NA_HEADS, 2 * NA_KH - 1, 32)[:, :, :ncls]; o0 += nrpb
    g_decay = flat[o0:o0 + 2 * RET_HEADS].reshape(2, RET_HEADS); o0 += LANES
    loss = flat[o0]

    grads = {
        "c_ctx": g_c_ctx.reshape(c_ctx.shape), "w_ada": g_w_ada[None], "b_ada": g_b_ada.reshape(b_ada.shape),
        "g_pre_mix": g_pre_mix_g[None], "g_post_mix": g_post_mix_g[None], "g_pre_mlp": g_pre_mlp_g[None],
        "g_post_mlp": g_post_mlp_g[None], "w_in": fused["w_in"][0][None], "ret_decay": g_decay[None], "ret_gn": g_gn[None],
        "na_rpb": g_rpb[None], "w_out": fused["w_out"][0][None], "w_mlp1": fused["w_mlp1"][0][None],
        "w_mlp2": fused["w_mlp2"][0][None],
    }
    weights = dict(c_ctx=c_ctx, w_ada=w_ada, b_ada=b_ada, g_pre_mix=g_pre_mix, g_post_mix=g_post_mix,
                   g_pre_mlp=g_pre_mlp, g_post_mlp=g_post_mlp, w_in=w_in, ret_decay=ret_decay, ret_gn=ret_gn,
                   na_rpb=na_rpb, w_out=w_out, w_mlp1=w_mlp1, w_mlp2=w_mlp2)
    m_in = dict(c_ctx=m_c_ctx, w_ada=m_w_ada, b_ada=m_b_ada, g_pre_mix=m_g_pre_mix, g_post_mix=m_g_post_mix,
                g_pre_mlp=m_g_pre_mlp, g_post_mlp=m_g_post_mlp, w_in=m_w_in, ret_decay=m_ret_decay,
                ret_gn=m_ret_gn, na_rpb=m_na_rpb, w_out=m_w_out, w_mlp1=m_w_mlp1, w_mlp2=m_w_mlp2)
    v_in = dict(c_ctx=v_c_ctx, w_ada=v_w_ada, b_ada=v_b_ada, g_pre_mix=v_g_pre_mix, g_post_mix=v_g_post_mix,
                g_pre_mlp=v_g_pre_mlp, g_post_mlp=v_g_post_mlp, w_in=v_w_in, ret_decay=v_ret_decay,
                ret_gn=v_ret_gn, na_rpb=v_na_rpb, w_out=v_w_out, w_mlp1=v_w_mlp1, w_mlp2=v_w_mlp2)
    names = list(weights)
    deltas, new_m, new_v = {}, {}, {}
    def as_2d(n):
        shp = weights[n].shape
        two_d = (-1, shp[-1]) if len(shp) > 1 else (1, shp[0])
        return [a.reshape(two_d) for a in (weights[n], grads[n], m_in[n], v_in[n])]

    small = [n for n in names if n not in fused and weights[n].size <= 65536]
    updated = dict(zip(small, _adamw_small([as_2d(n) for n in small], "adamw_small")))
    for n in names:
        if n in fused:
            updated[n] = fused[n][1:]
        elif n not in updated:
            updated[n] = _adamw(*as_2d(n), "adamw_" + n)
        deltas[n], new_m[n], new_v[n] = (a.reshape(weights[n].shape) for a in updated[n])
    return (loss, grad_x, *[grads[n] for n in names], *[deltas[n] for n in names],
            *[new_m[n] for n in names], *[new_v[n] for n in names])
```

```python
import functools
import math

import numpy as np
import jax
import jax.numpy as jnp
from jax import lax
from jax.experimental import pallas as pl
from jax.experimental.pallas import tpu as pltpu

F32 = jnp.float32
BF16 = jnp.bfloat16
MESH = pl.DeviceIdType.MESH

N_DEV = 8
LANES = 128
SUBLANES = 8
VMEM_LIMIT = 60 * 1024 * 1024

GRID_W = 64
RET_HEADS = 4
RET_DIM = 128
RET_WIDTH = RET_HEADS * RET_DIM
NA_HEADS = 8
NA_DIM = 64
NA_WIDTH = NA_HEADS * NA_DIM
NA_PAIRS = NA_HEADS // 2
NA_KH = 8
NA_KW = 16
NA_GROUP = 8
SEG = 512
ROPE_BASE = 10000.0
NORM_EPS = 1e-6
NEG_INF = -1e30
N_MOD = 6

ADAM_LR = 0.001
ADAM_B1 = 0.9
ADAM_B2 = 0.999
ADAM_EPS = 1e-08
ADAM_WD = 0.01
ADAM_STEP = 10


def _dot(a, b):
    return lax.dot_general(a, b, (((1,), (0,)), ((), ())), preferred_element_type=F32)


def _dot_nt(a, b):
    return lax.dot_general(a, b, (((1,), (1,)), ((), ())), preferred_element_type=F32)


def _dot_tn(a, b):
    return lax.dot_general(a, b, (((0,), (0,)), ((), ())), preferred_element_type=F32)


def _sigmoid(x):
    return 1.0 / (1.0 + jnp.exp(-x))


def _div_tile(n, cap, mult):
    if n <= cap:
        return n
    for t in range(cap - cap % mult, 0, -mult):
        if n % t == 0:
            return t
    raise ValueError(f"no tile for {n}")


def _params(*sem):
    return pltpu.CompilerParams(dimension_semantics=tuple(sem) if sem else None,
                                vmem_limit_bytes=VMEM_LIMIT)


def _vmem():
    return pl.BlockSpec(memory_space=pltpu.VMEM)


def _any():
    return pl.BlockSpec(memory_space=pl.ANY)


def _me_and_peers():
    x, y, c = lax.axis_index("x"), lax.axis_index("y"), lax.axis_index("c")
    me = 4 * x + 2 * y + c
    peers = []
    for m in range(1, N_DEV):
        px = 1 - x if (m >> 2) & 1 else x
        py = 1 - y if (m >> 1) & 1 else y
        pc = 1 - c if m & 1 else c
        peers.append(((px, py, pc), 4 * px + 2 * py + pc))
    return me, peers


def _exchange(src_for, dst_from, send_sems, recv_sems):
    me, peers = _me_and_peers()
    sent = []
    for i, (dev, pid) in enumerate(peers):
        cp = pltpu.make_async_remote_copy(src_ref=src_for(pid), dst_ref=dst_from(me),
                                          send_sem=send_sems.at[i], recv_sem=recv_sems.at[i],
                                          device_id=dev, device_id_type=MESH)
        cp.start()
        sent.append(cp)
    for i, (dev, pid) in enumerate(peers):
        pltpu.make_async_remote_copy(src_ref=src_for(pid), dst_ref=dst_from(pid),
                                     send_sem=send_sems.at[i], recv_sem=recv_sems.at[i],
                                     device_id=dev, device_id_type=MESH).wait_recv()
    for cp in sent:
        cp.wait_send()


SIBLING = (1,)
ICI_SAME_CORE = (2, 4, 6)
ALL_PEERS = tuple(range(1, N_DEV))


def _remote(src, dst, send_sem, recv_sem, dev):
    return pltpu.make_async_remote_copy(src_ref=src, dst_ref=dst, send_sem=send_sem, recv_sem=recv_sem,
                                        device_id=dev, device_id_type=MESH)


def _push_start(items, masks, send_sems, recv_sems):
    me, peers = _me_and_peers()
    for k, (src_for, dst_from) in enumerate(items):
        for m in masks:
            dev, pid = peers[m - 1]
            _remote(src_for(pid), dst_from(me), send_sems.at[k, m - 1], recv_sems.at[k, m - 1], dev).start()


def _push_wait_recv(items, masks, send_sems, recv_sems):
    me, peers = _me_and_peers()
    for k, (src_for, dst_from) in enumerate(items):
        for m in masks:
            dev, pid = peers[m - 1]
            _remote(src_for(pid), dst_from(pid), send_sems.at[k, m - 1], recv_sems.at[k, m - 1], dev).wait_recv()


def _push_wait_send(items, masks, send_sems, recv_sems):
    me, peers = _me_and_peers()
    for k, (src_for, dst_from) in enumerate(items):
        for m in masks:
            dev, pid = peers[m - 1]
            _remote(src_for(pid), dst_from(me), send_sems.at[k, m - 1], recv_sems.at[k, m - 1], dev).wait_send()


def _forward_start(items, send_sems, recv_sems):
    me, peers = _me_and_peers()
    sib = peers[0][0]
    for k, (blk_in, blk_out) in enumerate(items):
        for j, m in enumerate(ICI_SAME_CORE):
            pid = peers[m - 1][1]
            _remote(blk_in(pid), blk_out(pid), send_sems.at[k, j], recv_sems.at[k, j], sib).start()


def _forward_wait(items, send_sems, recv_sems):
    me, peers = _me_and_peers()
    sib = peers[0][0]
    for k, (blk_in, blk_out) in enumerate(items):
        for j, m in enumerate(ICI_SAME_CORE):
            got = peers[(m | 1) - 1][1]
            _remote(blk_in(got), blk_out(got), send_sems.at[k, j], recv_sems.at[k, j], sib).wait_recv()
    for k, (blk_in, blk_out) in enumerate(items):
        for j, m in enumerate(ICI_SAME_CORE):
            pid = peers[m - 1][1]
            _remote(blk_in(pid), blk_out(pid), send_sems.at[k, j], recv_sems.at[k, j], sib).wait_send()


def _mod_gather(c, c_ctx, w_ada, b_ada, w_in_t, w_out, w1, w2):
    B, D = c.shape
    ncol = w_ada.shape[1]
    rows = SUBLANES * N_DEV + SUBLANES

    def body(c_ref, cc_ref, w_ref, b_ref, win_ref, wout_ref, w1_ref, w2_ref,
             s_ref, m_ref, gin_ref, wout_b, w1_b, w2_b,
             win_b, msend, send1, recv1, send2, recv2, wsend, wrecv, fsend, frecv, lsem):
        me, _ = _me_and_peers()
        win_b[...] = win_ref[...].astype(BF16)
        block = _row_block(gin_ref, w_in_t.shape[0])
        gather = [(lambda p: win_b, block)]
        own = pltpu.make_async_copy(win_b, block(me), lsem.at[0])
        cv = c_ref[...]
        slot = jnp.concatenate([cv * _sigmoid(cv), jnp.zeros((SUBLANES - B, D), F32)], axis=0)
        my_rows = pl.ds(pl.multiple_of(me * SUBLANES, SUBLANES), SUBLANES)
        s_ref[my_rows, :] = slot
        ccv = cc_ref[...]
        s_ref[SUBLANES * N_DEV:, :] = jnp.concatenate(
            [ccv * _sigmoid(ccv), jnp.zeros((SUBLANES - 1, D), F32)], axis=0)

        def rows_of(p):
            return s_ref.at[pl.ds(pl.multiple_of(p * SUBLANES, SUBLANES), SUBLANES), :]

        _exchange(lambda p: rows_of(me), rows_of, send1, recv1)
        own.start()
        _push_start(gather, SIBLING + ICI_SAME_CORE, wsend, wrecv)
        wout_b[...] = wout_ref[...].astype(BF16)
        w1_b[...] = w1_ref[...].astype(BF16)
        w2_b[...] = w2_ref[...].astype(BF16)
        b_loc = b_ref[:, pl.ds(pl.multiple_of(me * ncol, ncol), ncol)]
        mods = _dot(s_ref[...], w_ref[...]) + b_loc
        for p in range(N_DEV):
            msend[p] = jnp.concatenate([mods[p * SUBLANES:(p + 1) * SUBLANES], mods[N_DEV * SUBLANES:]], axis=0)
        m_ref[me] = msend[me]
        columns = [(lambda p: msend.at[p], lambda p: m_ref.at[p])]
        _push_start(columns, ALL_PEERS, send2, recv2)
        _push_wait_recv(gather, ICI_SAME_CORE, wsend, wrecv)
        relay = [(block, block)]
        _forward_start(relay, fsend, frecv)
        _push_wait_recv(columns, ALL_PEERS, send2, recv2)
        _push_wait_recv(gather, SIBLING, wsend, wrecv)
        _forward_wait(relay, fsend, frecv)
        _push_wait_send(columns, ALL_PEERS, send2, recv2)
        _push_wait_send(gather, SIBLING + ICI_SAME_CORE, wsend, wrecv)
        own.wait()

    return pl.pallas_call(
        body, name="mod_gather",
        out_shape=(jax.ShapeDtypeStruct((rows, D), F32), jax.ShapeDtypeStruct((N_DEV, 2 * SUBLANES, ncol), F32),
                   jax.ShapeDtypeStruct((N_DEV * w_in_t.shape[0], D), BF16),
                   jax.ShapeDtypeStruct(w_out.shape, BF16), jax.ShapeDtypeStruct(w1.shape, BF16),
                   jax.ShapeDtypeStruct(w2.shape, BF16)),
        in_specs=[_vmem()] * 8, out_specs=(_vmem(), _vmem(), _any(), _vmem(), _vmem(), _vmem()),
        scratch_shapes=[pltpu.VMEM(w_in_t.shape, BF16), pltpu.VMEM((N_DEV, 2 * SUBLANES, ncol), F32)]
                       + [pltpu.SemaphoreType.DMA((N_DEV - 1,))] * 2
                       + [pltpu.SemaphoreType.DMA((1, N_DEV - 1))] * 4 + [pltpu.SemaphoreType.DMA((1, 3))] * 2
                       + [pltpu.SemaphoreType.DMA((1,))],
        compiler_params=pltpu.CompilerParams(vmem_limit_bytes=VMEM_LIMIT),
    )(c, c_ctx.reshape(1, D), w_ada, b_ada, w_in_t, w_out, w1, w2)


def _row_block(ref, rows):
    return lambda p: ref.at[pl.ds(pl.multiple_of(p * rows, 2 * SUBLANES), rows), :]


def _col_block(ref, cols):
    return lambda p: ref.at[:, pl.ds(pl.multiple_of(p * cols, LANES), cols)]


def _slot(ref):
    return lambda p: ref.at[p]


def _grid_call(body, *, name, grid, out_shape, in_specs, out_specs, scratch_shapes, args):
    return pl.pallas_call(
        body, name=name, grid=grid, out_shape=tuple(out_shape), in_specs=list(in_specs), out_specs=tuple(out_specs),
        scratch_shapes=list(scratch_shapes), compiler_params=_params(*(("arbitrary",) * len(grid))),
    )(*args)


def _token_tiles(n_ctx, tm):
    nct = n_ctx // tm

    def ctx_spec(D):
        return pl.BlockSpec((None, tm, D), lambda b, t: (b, jnp.minimum(t, nct - 1), 0))

    def lat_spec(D):
        return pl.BlockSpec((None, tm, D), lambda b, t: (b, jnp.maximum(t - nct, 0), 0))

    return nct, ctx_spec, lat_spec


def _inproj_fwd(x, ctx, modl, g1, w_in_t):
    B, N, D = x.shape
    n_ctx = ctx.shape[1]
    T = n_ctx + N
    nw = w_in_t.shape[0]
    tm = _div_tile(n_ctx, 256, 16)
    nct, ctx_spec, lat_spec = _token_tiles(n_ctx, tm)

    def body(c_ref, x_ref, sh_ref, sc_ref, g_ref, w_ref, h_ref, p_ref):
        x = jnp.where(pl.program_id(1) < nct, c_ref[...], x_ref[...])
        r = lax.rsqrt(jnp.mean(x * x, axis=-1, keepdims=True) + NORM_EPS)
        h = ((x * r) * g_ref[...]) * (1.0 + sc_ref[...]) + sh_ref[...]
        hb = h.astype(BF16)
        h_ref[...] = hb
        p_ref[...] = _dot_nt(hb, w_ref[...]).astype(BF16)

    def mrow(b, t):
        return jnp.where(t < nct, B, b)

    return _grid_call(
        body, name="inproj_fwd", grid=(B, T // tm),
        out_shape=(jax.ShapeDtypeStruct((B, T, D), BF16), jax.ShapeDtypeStruct((B, T, nw), BF16)),
        in_specs=[ctx_spec(D), lat_spec(D),
                  pl.BlockSpec((None, None, 1, D), lambda b, t: (mrow(b, t), 0, 0, 0)),
                  pl.BlockSpec((None, None, 1, D), lambda b, t: (mrow(b, t), 1, 0, 0)),
                  pl.BlockSpec((1, D), lambda b, t: (0, 0)),
                  pl.BlockSpec((nw, D), lambda b, t: (0, 0))],
        out_specs=(pl.BlockSpec((None, tm, D), lambda b, t: (b, t, 0)),
                   pl.BlockSpec((None, tm, nw), lambda b, t: (b, t, 0))),
        scratch_shapes=[], args=(ctx, x, modl, modl, g1, w_in_t))


def _swap32(x):
    lane = lax.broadcasted_iota(jnp.int32, x.shape, 1)
    return jnp.where((lane % 64) < 32, pltpu.roll(x, 96, 1), pltpu.roll(x, 32, 1))


def _rope(x, cos, sin):
    return x * cos + _swap32(x) * sin


def _unrope(dy, cos, sin):
    return dy * cos + _swap32(dy * sin)


def _ret_weights(lgf, lgb, dist):
    return jnp.exp(jnp.where(dist >= 0.0, lgf * dist, -lgb * dist))


class _RetDecay:
    def __init__(self, lgf, lgb, rows):
        r = lax.broadcasted_iota(jnp.int32, (rows, RET_DIM), 0).astype(F32)
        self.head = r + 1.0
        self.tail = (rows - 1.0) - r
        self.q_f = jnp.exp(lgf * self.head)
        self.k_f = jnp.exp(lgf * self.tail)
        self.q_b = jnp.exp(lgb * self.tail)
        self.k_b = jnp.exp(lgb * self.head)


def _ret_states(kf32, vs, lgf, lgb, C, c, nt, hf, hb, hfa=None, hba=None):
    dec = _RetDecay(lgf, lgb, c)
    dec_c = _RetDecay(lgf, lgb, C)
    step_f = jnp.exp(jnp.zeros((RET_DIM, RET_DIM), F32) + lgf * c)
    step_b = jnp.exp(jnp.zeros((RET_DIM, RET_DIM), F32) + lgb * c)

    def upd(rows, kdec):
        return _dot_tn((kf32[rows, :] * kdec).astype(BF16), vs[rows, :])

    def lat(t):
        return slice(C + t * c, C + (t + 1) * c)

    state = upd(slice(0, C), dec_c.k_f)
    aged = jnp.zeros_like(state)
    for t in range(nt):
        hf[t] = state.astype(BF16)
        if hfa is not None:
            hfa[t] = aged
        if t < nt - 1:
            aged = step_f * (aged + c * state)
            state = step_f * state + upd(lat(t), dec.k_f)
    state = upd(slice(0, C), dec_c.k_b)
    aged = jnp.zeros_like(state)
    for t in range(nt - 1, -1, -1):
        hb[t] = state.astype(BF16)
        if hba is not None:
            hba[t] = aged
        if t > 0:
            aged = step_b * (aged + c * state)
            state = step_b * state + upd(lat(t), dec.k_b)
    return dec, dec_c, step_f, step_b


def _ret_fwd(proj, cos, sin, lg, gn, n_ctx):
    B, T, _ = proj.shape
    C = n_ctx
    N = T - C
    c = _div_tile(N, 256, 16)
    nt = N // c
    scale = RET_DIM ** -0.5

    def body(lg_ref, q_ref, k_ref, v_ref, g_ref, cos_ref, sin_ref, gn_ref, o_ref, lat_ref, qr_ref, kf32,
             qs, ks, vs, hf, hb):
        h = pl.program_id(1)
        lgf = lg_ref[0, h]
        lgb = lg_ref[1, h]
        for rows in [slice(0, C)] + [slice(C + t * c, C + (t + 1) * c) for t in range(nt)]:
            cosb = cos_ref[rows, :]
            sinb = sin_ref[rows, :]
            qr = _rope(q_ref[rows, :].astype(F32), cosb, sinb) * scale
            qr_ref[rows, :] = qr
            qs[rows, :] = qr.astype(BF16)
            kr = _rope(k_ref[rows, :].astype(F32), cosb, sinb)
            kf32[rows, :] = kr
            ks[rows, :] = kr.astype(BF16)
            vs[rows, :] = v_ref[rows, :].astype(BF16)
        gnv = gn_ref[...]
        dec, _, _, _ = _ret_states(kf32, vs, lgf, lgb, C, c, nt, hf, hb)
        rc = (lax.broadcasted_iota(jnp.int32, (c, c), 0) - lax.broadcasted_iota(jnp.int32, (c, c), 1)).astype(F32)
        w_diag = _ret_weights(lgf, lgb, rc)
        for t in range(nt):
            rows = slice(C + t * c, C + (t + 1) * c)
            qt = qs[rows, :]
            s = _dot_nt(qt, ks[rows, :])
            o = (_dot((s * w_diag).astype(BF16), vs[rows, :])
                 + dec.q_f * _dot(qt, hf[t]) + dec.q_b * _dot(qt, hb[t]))
            o_ref[t * c:(t + 1) * c, :] = o
            mu = jnp.mean(o, axis=-1, keepdims=True)
            oc = o - mu
            var = jnp.mean(oc * oc, axis=-1, keepdims=True)
            yh = oc * lax.rsqrt(var + NORM_EPS)
            g = g_ref[rows, :].astype(F32)
            lat_ref[t * c:(t + 1) * c, :] = ((yh * gnv) * (g * _sigmoid(g))).astype(BF16)

    def col(seg):
        return pl.BlockSpec((None, T, RET_DIM), lambda b, h, seg=seg: (b, 0, seg * RET_HEADS + h))

    return _grid_call(
        body, name="ret_fwd", grid=(B, RET_HEADS),
        out_shape=(jax.ShapeDtypeStruct((B, N, RET_WIDTH), F32), jax.ShapeDtypeStruct((B, N, RET_WIDTH), BF16),
                   jax.ShapeDtypeStruct((B, T, RET_WIDTH), F32), jax.ShapeDtypeStruct((B, T, RET_WIDTH), F32)),
        in_specs=[pl.BlockSpec(memory_space=pltpu.SMEM), col(0), col(1), col(2), col(3),
                  pl.BlockSpec((T, RET_DIM), lambda b, h: (0, 0)), pl.BlockSpec((T, RET_DIM), lambda b, h: (0, 0)),
                  pl.BlockSpec((1, RET_DIM), lambda b, h: (0, h))],
        out_specs=(pl.BlockSpec((None, N, RET_DIM), lambda b, h: (b, 0, h)),
                   pl.BlockSpec((None, N, RET_DIM), lambda b, h: (b, 0, h)),
                   pl.BlockSpec((None, T, RET_DIM), lambda b, h: (b, 0, h)),
                   pl.BlockSpec((None, T, RET_DIM), lambda b, h: (b, 0, h))),
        scratch_shapes=[pltpu.VMEM((T, RET_DIM), BF16)] * 3 + [pltpu.VMEM((nt, RET_DIM, RET_DIM), BF16)] * 2,
        args=(lg, proj, proj, proj, proj, cos, sin, gn))


def _ret_bwd(proj, q_rot, k_rot, cos, sin, lg, gn, o, dlat, n_ctx):
    B, T, _ = proj.shape
    C = n_ctx
    N = T - C
    c = _div_tile(N, 256, 16)
    nt = N // c
    scale = RET_DIM ** -0.5

    def lat(t):
        return slice(C + t * c, C + (t + 1) * c)

    def body(lg_ref, qf32, kf32, v_ref, g_ref, cos_ref, sin_ref, gn_ref, o_ref, dl_ref,
             d_ref, dgn_ref, dlg_ref, qs, ks, vs, dos, hf, hb, hfa, hba, gf_s, gb_s):
        h = pl.program_id(1)
        lgf = lg_ref[0, h]
        lgb = lg_ref[1, h]
        gnv = gn_ref[...]

        def fold(a):
            return jnp.sum(a.reshape(a.shape[0] // SUBLANES, SUBLANES, a.shape[1]), axis=0)

        for rows in [slice(0, C)] + [lat(t) for t in range(nt)]:
            qs[rows, :] = qf32[rows, :].astype(BF16)
            ks[rows, :] = kf32[rows, :].astype(BF16)
            vs[rows, :] = v_ref[rows, :].astype(BF16)

        dgn = jnp.zeros((1, RET_DIM), F32)
        for t in range(nt):
            lrows = slice(t * c, (t + 1) * c)
            ov = o_ref[lrows, :]
            mu = jnp.mean(ov, axis=-1, keepdims=True)
            oc = ov - mu
            var = jnp.mean(oc * oc, axis=-1, keepdims=True)
            rstd = lax.rsqrt(var + NORM_EPS)
            yh = oc * rstd
            g = g_ref[lat(t), :].astype(F32)
            sg = _sigmoid(g)
            dl = dl_ref[lrows, :]
            d_ref[3, lat(t), :] = (dl * (yh * gnv) * (sg * (1.0 + g * (1.0 - sg)))).astype(BF16)
            dls = dl * (g * sg)
            dgn = dgn + jnp.sum(dls * yh, axis=0, keepdims=True)
            dyh = dls * gnv
            do = rstd * (dyh - jnp.mean(dyh, axis=-1, keepdims=True)
                         - yh * jnp.mean(dyh * yh, axis=-1, keepdims=True))
            dos[lrows, :] = do.astype(BF16)
        dgn_ref[...] = jnp.concatenate([dgn, jnp.zeros((SUBLANES - 1, RET_DIM), F32)], axis=0)
        d_ref[3, 0:C, :] = jnp.zeros((C, RET_DIM), BF16)
        d_ref[0, 0:C, :] = jnp.zeros((C, RET_DIM), BF16)

        dec, dec_c, step_f, step_b = _ret_states(kf32, vs, lgf, lgb, C, c, nt, hf, hb, hfa, hba)

        def zmat(t, qdec):
            return _dot_tn((qf32[lat(t), :] * qdec).astype(BF16), dos[t * c:(t + 1) * c, :])

        acc3f = jnp.zeros((RET_DIM, RET_DIM), F32)
        acc3b = jnp.zeros((RET_DIM, RET_DIM), F32)
        state = jnp.zeros((RET_DIM, RET_DIM), F32)
        for t in range(nt - 1, -1, -1):
            gf_s[t] = state.astype(BF16)
            z = zmat(t, dec.q_f)
            acc3f = acc3f + hfa[t] * z
            state = step_f * state + z
        gctx_f = state.astype(BF16)
        state = jnp.zeros((RET_DIM, RET_DIM), F32)
        for t in range(nt):
            gb_s[t] = state.astype(BF16)
            z = zmat(t, dec.q_b)
            acc3b = acc3b + hba[t] * z
            state = step_b * state + z
        gctx_b = state.astype(BF16)

        rc = (lax.broadcasted_iota(jnp.int32, (c, c), 0) - lax.broadcasted_iota(jnp.int32, (c, c), 1)).astype(F32)
        w_diag = _ret_weights(lgf, lgb, rc)
        wg_f = jnp.where(rc >= 0.0, w_diag * rc, 0.0)
        wg_b = jnp.where(rc < 0.0, -w_diag * rc, 0.0)
        accf = jnp.zeros((SUBLANES, RET_DIM), F32)
        accb = jnp.zeros((SUBLANES, RET_DIM), F32)
        gdf = jnp.zeros((SUBLANES, c), F32)
        gdb = jnp.zeros((SUBLANES, c), F32)
        for t in range(nt):
            rows = lat(t)
            qt = qs[rows, :]
            kt = ks[rows, :]
            vt = vs[rows, :]
            dot = dos[t * c:(t + 1) * c, :]
            s = _dot_nt(qt, kt)
            dp = _dot_nt(dot, vt)
            dv = _dot_tn((s * w_diag).astype(BF16), dot)
            ds = (dp * w_diag).astype(BF16)
            dq = _dot(ds, kt)
            dk = _dot_tn(ds, qt)
            gs = dp * s
            gdf = gdf + fold(gs * wg_f)
            gdb = gdb + fold(gs * wg_b)
            qv = qf32[rows, :]
            kv = kf32[rows, :]
            dq_f = dec.q_f * _dot_nt(dot, hf[t])
            dq_b = dec.q_b * _dot_nt(dot, hb[t])
            dk_f = dec.k_f * _dot_nt(vt, gf_s[t])
            dk_b = dec.k_b * _dot_nt(vt, gb_s[t])
            accf = accf + fold(dec.head * dq_f * qv) + fold(dec.tail * dk_f * kv)
            accb = accb + fold(dec.tail * dq_b * qv) + fold(dec.head * dk_b * kv)
            dv = dv + dec.k_f * _dot(kt, gf_s[t]) + dec.k_b * _dot(kt, gb_s[t])
            cosb = cos_ref[rows, :]
            sinb = sin_ref[rows, :]
            d_ref[0, rows, :] = _unrope((dq + dq_f + dq_b) * scale, cosb, sinb).astype(BF16)
            d_ref[1, rows, :] = _unrope(dk + dk_f + dk_b, cosb, sinb).astype(BF16)
            d_ref[2, rows, :] = dv.astype(BF16)
        kc = ks[0:C, :]
        vc = vs[0:C, :]
        kcv = kf32[0:C, :]
        dkc_f = dec_c.k_f * _dot_nt(vc, gctx_f)
        dkc_b = dec_c.k_b * _dot_nt(vc, gctx_b)
        accf = accf + fold(dec_c.tail * dkc_f * kcv)
        accb = accb + fold(dec_c.head * dkc_b * kcv)
        d_ref[1, 0:C, :] = (dkc_f + dkc_b).astype(BF16)
        d_ref[2, 0:C, :] = (dec_c.k_f * _dot(kc, gctx_f) + dec_c.k_b * _dot(kc, gctx_b)).astype(BF16)
        gf = jnp.sum(gdf) + jnp.sum(accf) + jnp.sum(acc3f)
        gb = jnp.sum(gdb) + jnp.sum(accb) + jnp.sum(acc3b)
        row = lax.broadcasted_iota(jnp.int32, (SUBLANES, LANES), 0)
        dlg_ref[...] = jnp.where(row == 0, gf, jnp.where(row == 1, gb, 0.0))

    def col(seg):
        return pl.BlockSpec((None, T, RET_DIM), lambda b, h, seg=seg: (b, 0, seg * RET_HEADS + h))

    def head(rows):
        return pl.BlockSpec((None, rows, RET_DIM), lambda b, h: (b, 0, h))

    return _grid_call(
        body, name="ret_bwd", grid=(B, RET_HEADS),
        out_shape=(jax.ShapeDtypeStruct((B, 4, T, RET_WIDTH), BF16),
                   jax.ShapeDtypeStruct((B, SUBLANES, RET_WIDTH), F32),
                   jax.ShapeDtypeStruct((B, RET_HEADS, SUBLANES, LANES), F32)),
        in_specs=[pl.BlockSpec(memory_space=pltpu.SMEM), head(T), head(T), col(2), col(3),
                  pl.BlockSpec((T, RET_DIM), lambda b, h: (0, 0)), pl.BlockSpec((T, RET_DIM), lambda b, h: (0, 0)),
                  pl.BlockSpec((1, RET_DIM), lambda b, h: (0, h)), head(N), head(N)],
        out_specs=(pl.BlockSpec((None, 4, T, RET_DIM), lambda b, h: (b, 0, 0, h)),
                   pl.BlockSpec((None, SUBLANES, RET_DIM), lambda b, h: (b, 0, h)),
                   pl.BlockSpec((None, None, SUBLANES, LANES), lambda b, h: (b, h, 0, 0))),
        scratch_shapes=[pltpu.VMEM((T, RET_DIM), BF16)] * 3 + [pltpu.VMEM((N, RET_DIM), BF16)]
                       + [pltpu.VMEM((nt, RET_DIM, RET_DIM), BF16)] * 2 + [pltpu.VMEM((nt, RET_DIM, RET_DIM), F32)] * 2
                       + [pltpu.VMEM((nt, RET_DIM, RET_DIM), BF16)] * 2,
        args=(lg, q_rot, k_rot, proj, proj, cos, sin, gn, o, dlat))


def _na_geometry(rows):
    kh = min(NA_KH, rows)
    return kh, kh * GRID_W


def _pair_select():
    lane = lax.broadcasted_iota(jnp.int32, (2 * GRID_W, LANES), 1)
    row = lax.broadcasted_iota(jnp.int32, (2 * GRID_W, LANES), 0)
    return (lane >= NA_DIM) == (row >= GRID_W)


def _pair_bias(bias_ref, dr0, kh):
    return jnp.concatenate(
        [jnp.concatenate([bias_ref[e, pl.ds(dr0 + 2 * m, 1)].reshape(GRID_W, LANES) for m in range(kh // 2)], axis=1)
         for e in range(2)], axis=0)


def _na_softmax(s_loc, s_ctx):
    mx = jnp.maximum(jnp.max(s_loc, axis=-1, keepdims=True), jnp.max(s_ctx, axis=-1, keepdims=True))
    p_loc = jnp.exp(s_loc - mx)
    p_ctx = jnp.exp(s_ctx - mx)
    den = jnp.sum(p_loc, axis=-1, keepdims=True) + jnp.sum(p_ctx, axis=-1, keepdims=True)
    return p_loc, p_ctx, den


def _na_fwd(proj, bias2, n_ctx):
    B, T, _ = proj.shape
    C = n_ctx
    N = T - C
    R = N // GRID_W
    kh, nk = _na_geometry(R)
    scale = NA_DIM ** -0.5
    base = (4 * RET_WIDTH) // LANES

    def body(q_ref, k_ref, v_ref, bias_ref, out_ref, p_ref, kb16, vb16):
        kb16[...] = k_ref[...].astype(BF16)
        vb16[...] = v_ref[...].astype(BF16)
        kc = kb16[0:C, :]
        vc = vb16[0:C, :]
        lane = lax.broadcasted_iota(jnp.int32, (GRID_W, LANES), 1)
        sel2 = _pair_select()

        def group(gi, carry):
            pre = []
            for u in range(NA_GROUP):
                r = gi * NA_GROUP + u
                bs = jnp.clip(r - kh // 2, 0, R - kh)
                dr0 = bs - r + (NA_KH - 1)
                q = q_ref[pl.ds(pl.multiple_of(C + r * GRID_W, GRID_W), GRID_W), :].astype(F32) * scale
                q2 = jnp.where(sel2, jnp.concatenate([q, q], axis=0), 0.0).astype(BF16)
                band = pl.ds(pl.multiple_of(C + bs * GRID_W, GRID_W), nk)
                s_loc = _dot_nt(q2, kb16[band, :]) + _pair_bias(bias_ref, dr0, kh)
                s_ctx = _dot_nt(q2, kc)
                pre.append((r, band, s_loc, s_ctx))
            mid = [(r, band) + _na_softmax(s_loc, s_ctx) for r, band, s_loc, s_ctx in pre]
            for r, band, p_loc, p_ctx, den in mid:
                inv = 1.0 / den
                pb_loc = (p_loc * inv).astype(BF16)
                pb_ctx = (p_ctx * inv).astype(BF16)
                p_ref[r, :, 0:nk] = pb_loc
                p_ref[r, :, nk:] = pb_ctx
                o2 = _dot(pb_loc, vb16[band, :]) + _dot(pb_ctx, vc)
                out_ref[pl.ds(pl.multiple_of(r * GRID_W, GRID_W), GRID_W), :] = jnp.where(
                    lane < NA_DIM, o2[:GRID_W], o2[GRID_W:]).astype(BF16)
            return carry

        lax.fori_loop(0, R // NA_GROUP, group, 0)

    def col(seg):
        return pl.BlockSpec((None, T, LANES), lambda b, p, seg=seg: (b, 0, base + seg * NA_PAIRS + p))

    return _grid_call(
        body, name="na_fwd", grid=(B, NA_PAIRS),
        out_shape=(jax.ShapeDtypeStruct((B, N, NA_WIDTH), BF16),
                   jax.ShapeDtypeStruct((B, NA_PAIRS, R, 2 * GRID_W, nk + C), BF16)),
        in_specs=[col(0), col(1), col(2),
                  pl.BlockSpec((2, 2 * NA_KH - 2, GRID_W, LANES), lambda b, p: (p, 0, 0, 0))],
        out_specs=(pl.BlockSpec((None, N, LANES), lambda b, p: (b, 0, p)),
                   pl.BlockSpec((None, None, R, 2 * GRID_W, nk + C), lambda b, p: (b, p, 0, 0, 0))),
        scratch_shapes=[pltpu.VMEM((T, LANES), BF16)] * 2,
        args=(proj, proj, proj, bias2))


def _na_bwd(proj, probs, dlat, n_ctx):
    B, T, _ = proj.shape
    C = n_ctx
    N = T - C
    R = N // GRID_W
    kh, nk = _na_geometry(R)
    scale = NA_DIM ** -0.5
    base = (4 * RET_WIDTH) // LANES

    def body(q_ref, k_ref, v_ref, p_ref, dl_ref, d_ref, db_ref, kb16, vb16, dkv):
        b = pl.program_id(1)
        kb16[...] = k_ref[...].astype(BF16)
        vb16[...] = v_ref[...].astype(BF16)
        kc = kb16[0:C, :]
        vc = vb16[0:C, :]
        lane = lax.broadcasted_iota(jnp.int32, (GRID_W, LANES), 1)
        dkv[...] = jnp.zeros(dkv.shape, F32)
        d_ref[0, 0:C, :] = jnp.zeros((C, LANES), BF16)

        @pl.when(b == 0)
        def _():
            db_ref[...] = jnp.zeros(db_ref.shape, F32)

        sel2 = _pair_select()

        def group(gi, carry):
            pre = []
            for u in range(NA_GROUP):
                r = gi * NA_GROUP + u
                bs = jnp.clip(r - kh // 2, 0, R - kh)
                dr0 = bs - r + (NA_KH - 1)
                q = q_ref[pl.ds(pl.multiple_of(C + r * GRID_W, GRID_W), GRID_W), :].astype(F32) * scale
                do = dl_ref[pl.ds(pl.multiple_of(r * GRID_W, GRID_W), GRID_W), :]
                q2 = jnp.where(sel2, jnp.concatenate([q, q], axis=0), 0.0).astype(BF16)
                do2 = jnp.where(sel2, jnp.concatenate([do, do], axis=0), 0.0).astype(BF16)
                band = pl.ds(pl.multiple_of(C + bs * GRID_W, GRID_W), nk)
                dp_loc = _dot_nt(do2, vb16[band, :])
                dp_ctx = _dot_nt(do2, vc)
                pre.append((r, dr0, band, q2, do2, dp_loc, dp_ctx))
            mid = []
            for r, dr0, band, q2, do2, dp_loc, dp_ctx in pre:
                pb_loc = p_ref[r, :, 0:nk]
                pb_ctx = p_ref[r, :, nk:]
                p_loc = pb_loc.astype(F32)
                p_ctx = pb_ctx.astype(F32)
                delta = (jnp.sum(p_loc * dp_loc, axis=-1, keepdims=True)
                         + jnp.sum(p_ctx * dp_ctx, axis=-1, keepdims=True))
                ds_loc = p_loc * (dp_loc - delta)
                ds_ctx = p_ctx * (dp_ctx - delta)
                mid.append((r, dr0, band, q2, do2, pb_loc, pb_ctx, ds_loc, ds_ctx))
            for r, dr0, band, q2, do2, pb_loc, pb_ctx, ds_loc, ds_ctx in mid:
                dsb_loc = ds_loc.astype(BF16)
                dsb_ctx = ds_ctx.astype(BF16)
                dq2 = _dot(dsb_loc, kb16[band, :]) + _dot(dsb_ctx, kc)
                d_ref[0, pl.ds(pl.multiple_of(C + r * GRID_W, GRID_W), GRID_W), :] = (jnp.where(
                    lane < NA_DIM, dq2[:GRID_W], dq2[GRID_W:]) * scale).astype(BF16)
                dkv[0, band, :] += _dot_tn(dsb_loc, q2)
                dkv[1, band, :] += _dot_tn(pb_loc, do2)
                dkv[0, 0:C, :] += _dot_tn(dsb_ctx, q2)
                dkv[1, 0:C, :] += _dot_tn(pb_ctx, do2)
                for e in range(2):
                    for m in range(kh // 2):
                        db_ref[e, pl.ds(dr0 + 2 * m, 1)] += ds_loc[e * GRID_W:(e + 1) * GRID_W,
                                                                   m * LANES:(m + 1) * LANES].reshape(1, GRID_W, LANES)
            return carry

        lax.fori_loop(0, R // NA_GROUP, group, 0)
        d_ref[1] = dkv[0].astype(BF16)
        d_ref[2] = dkv[1].astype(BF16)

    def col(seg):
        return pl.BlockSpec((None, T, LANES), lambda p, b, seg=seg: (b, 0, base + seg * NA_PAIRS + p))

    return _grid_call(
        body, name="na_bwd", grid=(NA_PAIRS, B),
        out_shape=(jax.ShapeDtypeStruct((B, 3, T, NA_WIDTH), BF16),
                   jax.ShapeDtypeStruct((NA_HEADS, 2 * NA_KH - 2, GRID_W, LANES), F32)),
        in_specs=[col(0), col(1), col(2),
                  pl.BlockSpec((None, None, R, 2 * GRID_W, nk + C), lambda p, b: (b, p, 0, 0, 0)),
                  pl.BlockSpec((None, N, LANES), lambda p, b: (b, 0, p))],
        out_specs=(pl.BlockSpec((None, 3, T, LANES), lambda p, b: (b, 0, 0, p)),
                   pl.BlockSpec((2, 2 * NA_KH - 2, GRID_W, LANES), lambda p, b: (p, 0, 0, 0))),
        scratch_shapes=[pltpu.VMEM((T, LANES), BF16)] * 2 + [pltpu.VMEM((2, T, LANES), F32)],
        args=(proj, proj, proj, probs, dlat))


def _split3(a):
    hi = a.astype(BF16)
    r1 = a - hi.astype(F32)
    mid = r1.astype(BF16)
    lo = (r1 - mid.astype(F32)).astype(BF16)
    return hi, mid, lo


def _rpb_reduce(dbias2, onehot2):
    rows = dbias2.shape[0] * dbias2.shape[1]
    flat = dbias2.reshape(rows, GRID_W * LANES)

    def body(a_ref, oh_ref, o_ref):
        hi, mid, lo = _split3(a_ref[...])
        oh = oh_ref[...]
        o_ref[...] = _dot(hi, oh) + _dot(mid, oh) + _dot(lo, oh)

    return pl.pallas_call(
        body, name="rpb_reduce", out_shape=jax.ShapeDtypeStruct((rows, LANES), F32),
        in_specs=[_vmem(), _vmem()], out_specs=_vmem(),
        compiler_params=pltpu.CompilerParams(vmem_limit_bytes=VMEM_LIMIT),
    )(flat, onehot2)


def _dense_core(lat_ret, lat_na, x, tgt, modl, g_post_mix, g_pre_mlp, g_post_mlp, w_out, w1, w2):
    B, N, D = x.shape
    F = w1.shape[1]
    wout_rows, w1_cols, w2_rows = w_out.shape[0] // N_DEV, w1.shape[1] // N_DEV, w2.shape[0] // N_DEV
    mixw = w_out.shape[0]
    half = mixw // 2
    tm = _div_tile(N, 256, 16)
    nt = N // tm
    fc = _div_tile(F, 1024, LANES)

    def body(lr_ref, ln_ref, x_ref, t_ref, gt1_ref, sh2_ref, sc2_ref, gt2_ref, gpm_ref, gpre_ref, gpo_ref,
             wout_part, w1_part, w2_part,
             dy1_ref, dlr_ref, dln_ref, dmix_ref, h2_ref, a_ref, du_ref, dz_ref, red_ref, wout_hbm, w1_hbm, w2_hbm,
             wout_v, w1_v, w2_v, u_s, sems, fsend, frecv):
        @pl.when((pl.program_id(0) == 0) & (pl.program_id(1) == 0))
        def _():
            relay = [(_row_block(wout_part, wout_rows), _row_block(wout_hbm, wout_rows)),
                     (_col_block(w1_part, w1_cols), _col_block(w1_hbm, w1_cols)),
                     (_row_block(w2_part, w2_rows), _row_block(w2_hbm, w2_rows))]
            _forward_start(relay, fsend, frecv)
            _forward_wait(relay, fsend, frecv)
            cps = [pltpu.make_async_copy(wout_hbm, wout_v, sems.at[0]),
                   pltpu.make_async_copy(w1_hbm, w1_v, sems.at[1]),
                   pltpu.make_async_copy(w2_hbm, w2_v, sems.at[2])]
            for cp in cps:
                cp.start()
            for cp in cps:
                cp.wait()

        @pl.when(pl.program_id(1) == 0)
        def _():
            red_ref[...] = jnp.zeros(red_ref.shape, F32)

        gt1 = gt1_ref[...]
        sh2 = sh2_ref[...]
        sc2 = sc2_ref[...]
        gt2 = gt2_ref[...]
        gpm = gpm_ref[...]
        gpre = gpre_ref[...]
        gpo = gpo_ref[...]

        def rowmean(a):
            return jnp.mean(a, axis=-1, keepdims=True)

        def colsum(a):
            return jnp.sum(a, axis=0, keepdims=True)

        mix_gain = gt1 * gpm
        mlp_in_gain = gpre * (1.0 + sc2)
        mlp_out_gain = gt2 * gpo
        mix = _dot(lr_ref[...], wout_v[0:half, :]) + _dot(ln_ref[...], wout_v[half:, :])
        x = x_ref[...]
        rm = lax.rsqrt(rowmean(mix * mix) + NORM_EPS)
        mh = mix * rm
        y1 = x + mh * mix_gain
        r1 = lax.rsqrt(rowmean(y1 * y1) + NORM_EPS)
        xh = y1 * r1
        h2b = (xh * mlp_in_gain + sh2).astype(BF16)
        h2_ref[...] = h2b
        z = jnp.zeros((tm, D), F32)
        for c0 in range(0, F, fc):
            u = _dot(h2b, w1_v[:, c0:c0 + fc])
            u_s[:, c0:c0 + fc] = u
            ru = jnp.maximum(u, 0.0)
            ab = (ru * ru).astype(BF16)
            a_ref[:, c0:c0 + fc] = ab
            z = z + _dot(ab, w2_v[c0:c0 + fc, :])
        r2 = lax.rsqrt(rowmean(z * z) + NORM_EPS)
        zh = z * r2
        y2 = y1 + zh * mlp_out_gain
        err = y2 - t_ref[...]
        loss = 0.5 * jnp.sum(rowmean(err * err))
        dy2 = err * (1.0 / D)
        s_out = colsum(dy2 * zh)
        red_ref[2:3, :] += s_out * gpo
        red_ref[6:7, :] += s_out * gt2
        dzh = dy2 * mlp_out_gain
        dz = r2 * (dzh - zh * rowmean(dzh * zh))
        dzb = dz.astype(BF16)
        dz_ref[...] = dzb
        dh2 = jnp.zeros((tm, D), F32)
        for c0 in range(0, F, fc):
            da = _dot_nt(dzb, w2_v[c0:c0 + fc, :])
            dub = (da * (2.0 * jnp.maximum(u_s[:, c0:c0 + fc], 0.0))).astype(BF16)
            du_ref[:, c0:c0 + fc] = dub
            dh2 = dh2 + _dot_nt(dub, w1_v[:, c0:c0 + fc])
        s_in = colsum(dh2 * xh)
        red_ref[3:4, :] += s_in * gpre
        red_ref[4:5, :] += colsum(dh2)
        red_ref[5:6, :] += s_in * (1.0 + sc2)
        dxh = dh2 * mlp_in_gain
        dy1 = dy2 + r1 * (dxh - xh * rowmean(dxh * xh))
        dy1_ref[...] = dy1
        s_mix = colsum(dy1 * mh)
        red_ref[0:1, :] += s_mix * gpm
        red_ref[1:2, :] += s_mix * gt1
        dmh = dy1 * mix_gain
        dmix = (rm *(dmh - mh * rowmean(dmh * mh))).astype(BF16)
        dmix_ref[...] = dmix
        dlr_ref[...] = _dot_nt(dmix, wout_v[0:half, :])
        dln_ref[...] = _dot_nt(dmix, wout_v[half:, :])
        red_ref[7:8, :] += jnp.zeros((1, D), F32) + loss

    def tok(w):
        return pl.BlockSpec((None, tm, w), lambda b, t: (b, t, 0))

    def mod(k):
        return pl.BlockSpec((None, None, 1, D), lambda b, t, k=k: (b, k, 0, 0))

    def vec():
        return pl.BlockSpec((1, D), lambda b, t: (0, 0))

    return pl.pallas_call(
        body, name="dense_core", grid=(B, nt),
        out_shape=(jax.ShapeDtypeStruct((B, N, D), F32), jax.ShapeDtypeStruct((B, N, half), F32),
                   jax.ShapeDtypeStruct((B, N, half), F32), jax.ShapeDtypeStruct((B, N, D), BF16),
                   jax.ShapeDtypeStruct((B, N, D), BF16), jax.ShapeDtypeStruct((B, N, F), BF16),
                   jax.ShapeDtypeStruct((B, N, F), BF16), jax.ShapeDtypeStruct((B, N, D), BF16),
                   jax.ShapeDtypeStruct((B, SUBLANES, D), F32),
                   jax.ShapeDtypeStruct(w_out.shape, w_out.dtype), jax.ShapeDtypeStruct(w1.shape, w1.dtype),
                   jax.ShapeDtypeStruct(w2.shape, w2.dtype)),
        in_specs=[tok(half), tok(half), tok(D), tok(D), mod(2), mod(3), mod(4), mod(5), vec(), vec(), vec(),
                  _any(), _any(), _any()],
        out_specs=(tok(D), tok(half), tok(half), tok(D), tok(D), tok(F), tok(F), tok(D),
                   pl.BlockSpec((None, SUBLANES, D), lambda b, t: (b, 0, 0)), _any(), _any(), _any()),
        scratch_shapes=[pltpu.VMEM((mixw, D), BF16), pltpu.VMEM((D, F), BF16), pltpu.VMEM((F, D), BF16),
                        pltpu.VMEM((tm, F), F32), pltpu.SemaphoreType.DMA((3,)),
                        pltpu.SemaphoreType.DMA((3, 3)), pltpu.SemaphoreType.DMA((3, 3))],
        input_output_aliases={11: 9, 12: 10, 13: 11},
        compiler_params=_params("arbitrary", "arbitrary"),
    )(lat_ret, lat_na, x, tgt, modl, modl, modl, modl, g_post_mix, g_pre_mlp, g_post_mlp, w_out, w1, w2)[:9]


def _inproj_bwd(dret, dna, x, ctx, dy1, modl, g1, w_in_t):
    B, N, D = x.shape
    n_ctx = ctx.shape[1]
    T = n_ctx + N
    tm = _div_tile(n_ctx, 256, 16)
    nct, ctx_spec, lat_spec = _token_tiles(n_ctx, tm)
    nt = T // tm
    nseg_r = dret.shape[1]
    nseg_n = dna.shape[1]
    nw = w_in_t.shape[0]

    def body(*refs):
        seg_refs = refs[:nseg_r + nseg_n]
        c_ref, x_ref, dy1_ref, sc_ref, g_ref, w_ref, dx_ref, red_ref = refs[nseg_r + nseg_n:]
        t = pl.program_id(1)
        dh = jnp.zeros((tm, D), F32)
        for s, ref in enumerate(seg_refs):
            dh = dh + _dot(ref[...], w_ref[s * SEG:(s + 1) * SEG, :])
        x = jnp.where(t < nct, c_ref[...], x_ref[...])
        g = g_ref[...]
        r = lax.rsqrt(jnp.mean(x * x, axis=-1, keepdims=True) + NORM_EPS)
        xh = x * r
        red_ref[0:1, :] = jnp.sum(dh, axis=0, keepdims=True)
        red_ref[1:2, :] = jnp.sum(dh * (xh * g), axis=0, keepdims=True)
        dn = dh * (1.0 + sc_ref[...])
        red_ref[2:3, :] = jnp.sum(dn * xh, axis=0, keepdims=True)
        red_ref[3:, :] = jnp.zeros((SUBLANES - 3, D), F32)
        dxh = dn * g
        dx = r * (dxh - xh * jnp.mean(dxh * xh, axis=-1, keepdims=True))
        dx_ref[...] = dx + jnp.where(t >= nct, dy1_ref[...], 0.0)

    def mrow(b, t):
        return jnp.where(t < nct, B, b)

    def seg(s):
        return pl.BlockSpec((None, None, tm, SEG), lambda b, t, s=s: (b, s, t, 0))

    return _grid_call(
        body, name="inproj_bwd", grid=(B, nt),
        out_shape=(jax.ShapeDtypeStruct((B, N, D), F32), jax.ShapeDtypeStruct((B, nt, SUBLANES, D), F32)),
        in_specs=[seg(s) for s in range(nseg_r)] + [seg(s) for s in range(nseg_n)]
                 + [ctx_spec(D), lat_spec(D), lat_spec(D),
                    pl.BlockSpec((None, None, 1, D), lambda b, t: (mrow(b, t), 1, 0, 0)),
                    pl.BlockSpec((1, D), lambda b, t: (0, 0)),
                    pl.BlockSpec((nw, D), lambda b, t: (0, 0))],
        out_specs=(lat_spec(D), pl.BlockSpec((None, None, SUBLANES, D), lambda b, t: (b, t, 0, 0))),
        scratch_shapes=[], args=(*([dret] * nseg_r), *([dna] * nseg_n), ctx, x, dy1, modl, g1, w_in_t))


def _tn_matmul(lhs, rhs, name, rows_before=0, rows_after=0, into=None):
    B, S, T, W = lhs.shape
    nn = rhs.shape[-1]
    tk = _div_tile(T, 2304, LANES)
    bm = _div_tile(W, 1024, LANES)
    bn = _div_tile(nn, 1024, LANES)
    nkt = T // tk
    nk = B * nkt

    def body(l_ref, r_ref, *rest):
        o_ref, acc = rest[-2:]
        k = pl.program_id(3)

        @pl.when(k == 0)
        def _():
            acc[...] = jnp.zeros(acc.shape, F32)

        acc[...] += _dot_tn(l_ref[...].astype(BF16), r_ref[...].astype(BF16))

        @pl.when(k == nk - 1)
        def _():
            o_ref[...] = acc[...].astype(BF16)

    nwb = W // bm
    first = rows_before // bm
    return pl.pallas_call(
        functools.partial(body), name=name, grid=(S, nwb, nn // bn, nk),
        out_shape=jax.ShapeDtypeStruct((rows_before + S * W + rows_after, nn), BF16),
        in_specs=[pl.BlockSpec((None, None, tk, bm), lambda s, i, j, k: (k // nkt, s, k % nkt, i)),
                  pl.BlockSpec((None, tk, bn), lambda s, i, j, k: (k // nkt, k % nkt, j))]
                 + ([] if into is None else [_any()]),
        out_specs=pl.BlockSpec((bm, bn), lambda s, i, j, k: (first + s * nwb + i, j)),
        scratch_shapes=[pltpu.VMEM((bm, bn), F32)],
        input_output_aliases={} if into is None else {2: 0},
        compiler_params=_params("parallel", "parallel", "parallel", "arbitrary"),
    )(lhs, rhs, *([] if into is None else [into]))


class _SplitScatter:
    def __init__(self, gs, block_ofs, land_shapes, name, kind="scatter", masks=ALL_PEERS):
        self.n = n = len(gs)
        self.block_ofs, self.kind, self.masks = block_ofs, kind, masks
        if kind == "scatter":
            land_shapes = [(N_DEV,) + tuple(bs) for bs in land_shapes]
        hbm = pl.BlockSpec(memory_space=pltpu.HBM)
        sem = pl.BlockSpec(memory_space=pltpu.SEMAPHORE)

        def body(*refs):
            g_refs, land_refs = refs[:n], refs[n:2 * n]
            send_sems, recv_sems, own_sems = refs[2 * n:2 * n + 3]
            token = refs[-1]
            for own, pushes in self._copies(g_refs, land_refs, send_sems, recv_sems, own_sems, landing="sender"):
                own.start()
                for cp in pushes:
                    cp.start()
            token[...] = jnp.zeros_like(token)

        outs = pl.pallas_call(
            body, name=name,
            out_shape=(pltpu.SemaphoreType.DMA((n * (N_DEV - 1),)), pltpu.SemaphoreType.DMA((n * (N_DEV - 1),)),
                       pltpu.SemaphoreType.DMA((n,)))
                      + tuple(pltpu.HBM(g.shape, g.dtype) for g in gs)
                      + tuple(pltpu.HBM(s, g.dtype) for s, g in zip(land_shapes, gs))
                      + (jax.ShapeDtypeStruct((SUBLANES, LANES), F32),),
            in_specs=(hbm,) * (2 * n), out_specs=(sem,) * 3 + (hbm,) * (2 * n) + (_vmem(),),
            input_output_aliases={k: 3 + k for k in range(2 * n)},
            compiler_params=pltpu.CompilerParams(has_side_effects=pltpu.SideEffectType.DATAFLOW_SIDE_EFFECTING),
        )(*[pltpu.with_memory_space_constraint(g, pltpu.HBM) for g in gs],
          *[pltpu.with_memory_space_constraint(lax.empty(s, g.dtype), pltpu.HBM) for s, g in zip(land_shapes, gs)])
        self.sems, self.thru, self.token = outs[:3], outs[3:3 + 2 * n], outs[-1]

    def _copies(self, g_refs, land_refs, send_sems, recv_sems, own_sems, landing):
        me, peers = _me_and_peers()
        out = []
        for k in range(self.n):
            if self.kind == "scatter":
                src, dst = self.block_ofs[k](g_refs[k]), _slot(land_refs[k])
            else:
                src, dst = (lambda p, k=k: g_refs[k]), self.block_ofs[k](land_refs[k])
            own = pltpu.make_async_copy(src(me), dst(me), own_sems.at[k]) if landing == "sender" else None
            pushes = []
            for m in self.masks:
                dev, pid = peers[m - 1]
                i = k * (N_DEV - 1) + m - 1
                pushes.append(_remote(src(pid), dst(me if landing == "sender" else pid),
                                      send_sems.at[i], recv_sems.at[i], dev))
            out.append((own, pushes))
        return out


def _scatter_wait(scatters, after, name):
    hbm = pl.BlockSpec(memory_space=pltpu.HBM)
    sem = pl.BlockSpec(memory_space=pltpu.SEMAPHORE)
    n_arr = [2 * sc.n for sc in scatters]
    total = sum(n_arr)

    def body(*refs):
        arrs, sems = refs[:total], refs[total:total + 3 * len(scatters)]
        a0 = 0
        for j, sc in enumerate(scatters):
            g_refs, land_refs = arrs[a0:a0 + sc.n], arrs[a0 + sc.n:a0 + 2 * sc.n]
            a0 += 2 * sc.n
            send_sems, recv_sems, own_sems = sems[3 * j:3 * j + 3]
            for (own, sent), (_, got) in zip(sc._copies(g_refs, land_refs, send_sems, recv_sems, own_sems, "sender"),
                                             sc._copies(g_refs, land_refs, send_sems, recv_sems, own_sems, "receiver")):
                own.wait()
                for cp in sent:
                    cp.wait_send()
                for cp in got:
                    cp.wait_recv()

    operands = [a for sc in scatters for a in sc.thru]
    outs = pl.pallas_call(
        body, name=name,
        out_shape=tuple(pltpu.HBM(a.shape, a.dtype) for a in operands),
        in_specs=(hbm,) * total + (sem,) * (3 * len(scatters)) + (pl.BlockSpec(memory_space=pl.ANY),),
        out_specs=(hbm,) * total, input_output_aliases={k: k for k in range(total)},
        compiler_params=pltpu.CompilerParams(has_side_effects=pltpu.SideEffectType.DATAFLOW_SIDE_EFFECTING),
    )(*operands, *[s for sc in scatters for s in sc.sems], after)
    lands, a0 = [], 0
    for sc in scatters:
        lands.extend(outs[a0 + sc.n:a0 + 2 * sc.n])
        a0 += 2 * sc.n
    return lands


def _small_ar(mbuf, silu_all, w_ada, c_ctx, n_mod_rows, n_vec_rows):
    D = silu_all.shape[1]
    ncol = w_ada.shape[1]
    nm = mbuf.shape[2]
    srows = silu_all.shape[0]

    def body(mbuf, s_ref, w_ref, cc_ref, tot_ref, gb_ref, gw_ref, gc_ref, tbuf, dmx, cmrow, send3, recv3):
        me, _ = _me_and_peers()
        msum = mbuf[0]
        for k in range(1, N_DEV):
            msum = msum + mbuf[k]
        tot_ref[...] = msum[n_mod_rows:n_mod_rows + n_vec_rows]
        gb_ref[...] = jnp.sum(msum[0:n_mod_rows], axis=0, keepdims=True)
        loc = pl.ds(pl.multiple_of(me * ncol, ncol), ncol)
        for k in range(N_DEV):
            dmx[k * SUBLANES:(k + 1) * SUBLANES, :] = mbuf[k, :, loc]
        cmrow[...] = msum
        cm_loc = cmrow[n_mod_rows - 1:n_mod_rows, loc]
        dmx[N_DEV * SUBLANES:, :] = jnp.concatenate([cm_loc, jnp.zeros((SUBLANES - 1, ncol), F32)], axis=0)
        gw_ref[...] = _dot_tn(s_ref[...], dmx[...])
        tbuf[me] = _dot_nt(dmx[N_DEV * SUBLANES:, :], w_ref[...])
        _exchange(lambda p: tbuf.at[me], lambda p: tbuf.at[p], send3, recv3)
        tsum = tbuf[0]
        for k in range(1, N_DEV):
            tsum = tsum + tbuf[k]
        cc = cc_ref[...]
        sg = _sigmoid(cc)
        gc_ref[...] = tsum[0:1, :] * (sg * (1.0 + cc * (1.0 - sg)))

    return pl.pallas_call(
        body, name="small_ar",
        out_shape=(jax.ShapeDtypeStruct((n_vec_rows, nm), F32), jax.ShapeDtypeStruct((1, nm), F32),
                   jax.ShapeDtypeStruct((D, ncol), F32), jax.ShapeDtypeStruct((1, D), F32)),
        in_specs=[_vmem()] * 4, out_specs=(_vmem(),) * 4,
        scratch_shapes=[pltpu.VMEM((N_DEV, SUBLANES, D), F32), pltpu.VMEM((srows, ncol), F32),
                        pltpu.VMEM((SUBLANES, nm), F32)] + [pltpu.SemaphoreType.DMA((N_DEV - 1,))] * 2,
        compiler_params=pltpu.CompilerParams(vmem_limit_bytes=VMEM_LIMIT),
    )(mbuf, silu_all, w_ada, c_ctx.reshape(1, D))


def _adam_update(w, g, m, v):
    mn = ADAM_B1 * m + (1.0 - ADAM_B1) * g
    vn = ADAM_B2 * v + (1.0 - ADAM_B2) * (g * g)
    m_hat = mn / (1.0 - ADAM_B1 ** ADAM_STEP)
    v_hat = vn / (1.0 - ADAM_B2 ** ADAM_STEP)
    return -ADAM_LR * (m_hat / (jnp.sqrt(v_hat) + ADAM_EPS) + ADAM_WD * w), mn, vn


def _adamw(w, g, m, v, name):
    rows, cols = w.shape
    tr = _div_tile(rows, 512, SUBLANES)

    def body(w_ref, g_ref, m_ref, v_ref, d_ref, nm_ref, nv_ref):
        d_ref[...], nm_ref[...], nv_ref[...] = _adam_update(w_ref[...], g_ref[...], m_ref[...], v_ref[...])

    spec = pl.BlockSpec((tr, cols), lambda i: (i, 0))
    return pl.pallas_call(
        functools.partial(body), name=name, grid=(rows // tr,),
        out_shape=(jax.ShapeDtypeStruct((rows, cols), F32),) * 3,
        in_specs=[spec] * 4, out_specs=(spec,) * 3,
        compiler_params=_params("parallel"),
    )(w, g, m, v)


def _adamw_small(items, name):
    n = len(items)

    def body(*refs):
        ins, outs = refs[:4 * n], refs[4 * n:]
        for i in range(n):
            w_ref, g_ref, m_ref, v_ref = ins[4 * i:4 * i + 4]
            outs[3 * i][...], outs[3 * i + 1][...], outs[3 * i + 2][...] = _adam_update(
                w_ref[...], g_ref[...], m_ref[...], v_ref[...])

    outs = pl.pallas_call(
        body, name=name,
        out_shape=tuple(jax.ShapeDtypeStruct(it[0].shape, F32) for it in items for _ in range(3)),
        in_specs=[_vmem()] * (4 * n), out_specs=(_vmem(),) * (3 * n),
        compiler_params=pltpu.CompilerParams(vmem_limit_bytes=VMEM_LIMIT),
    )(*[a for it in items for a in it])
    return [tuple(outs[3 * i:3 * i + 3]) for i in range(n)]


def _sum_adamw(buf, w, m, v, name):
    _, rows, cols = buf.shape
    tr = _div_tile(rows, 256, 2 * SUBLANES)

    def body(b_ref, w_ref, m_ref, v_ref, g_ref, d_ref, nm_ref, nv_ref):
        g = b_ref[0].astype(F32)
        for k in range(1, N_DEV):
            g = g + b_ref[k].astype(F32)
        g_ref[...] = g
        d_ref[...], nm_ref[...], nv_ref[...] = _adam_update(w_ref[...], g, m_ref[...], v_ref[...])

    spec = pl.BlockSpec((tr, cols), lambda i: (i, 0))
    return pl.pallas_call(
        functools.partial(body), name=name, grid=(rows // tr,),
        out_shape=(jax.ShapeDtypeStruct((rows, cols), F32),) * 4,
        in_specs=[pl.BlockSpec((N_DEV, tr, cols), lambda i: (0, i, 0))] + [spec] * 3, out_specs=(spec,) * 4,
        compiler_params=_params("parallel"),
    )(buf, w, m, v)


def _rope_tables(n_ctx, n):
    n_freq = RET_DIM // 4
    inv = np.float32(ROPE_BASE) ** (-np.arange(n_freq, dtype=np.float32) / np.float32(n_freq))
    tok = np.arange(n)
    pos_r = (tok // GRID_W).astype(np.float32)
    pos_c = (tok % GRID_W).astype(np.float32)
    ang_r = (pos_r[:, None] * inv[None, :]).astype(np.float32)
    ang_c = (pos_c[:, None] * inv[None, :]).astype(np.float32)
    cos = np.concatenate([np.cos(ang_r), np.cos(ang_r), np.cos(ang_c), np.cos(ang_c)], axis=-1)
    sin = np.concatenate([-np.sin(ang_r), np.sin(ang_r), -np.sin(ang_c), np.sin(ang_c)], axis=-1)
    cos = np.concatenate([np.ones((n_ctx, RET_DIM), np.float32), cos], axis=0)
    sin = np.concatenate([np.zeros((n_ctx, RET_DIM), np.float32), sin], axis=0)
    return jnp.asarray(cos, F32), jnp.asarray(sin, F32)


def _na_tables():
    q = np.arange(GRID_W)[:, None]
    k = np.arange(GRID_W)[None, :]
    start = np.clip(q - NA_KW // 2, 0, GRID_W - NA_KW)
    valid = (k >= start) & (k < start + NA_KW)
    dc = np.clip(k - q + (NA_KW - 1), 0, 2 * NA_KW - 2)
    ncls = 2 * NA_KW - 1
    onehot = (dc[None] == np.arange(ncls)[:, None, None]) & valid[None]
    oh2 = np.zeros((GRID_W, LANES, LANES), np.float32)
    for c in range(ncls):
        oh2[:, :GRID_W, c] = onehot[c]
        oh2[:, GRID_W:, 32 + c] = onehot[c]
    return onehot.astype(np.float32), valid, oh2.reshape(GRID_W * LANES, LANES)


def _paired_bias(rpb, onehot, valid):
    ncls = onehot.shape[0]
    pair = np.zeros((2 * ncls, GRID_W, LANES), np.float32)
    pair[:ncls, :, :GRID_W] = onehot
    pair[ncls:, :, GRID_W:] = onehot
    rows = jnp.concatenate([rpb[:, :-1], rpb[:, 1:]], axis=-1)
    t = jnp.einsum("hdc,cqk->hdqk", rows, jnp.asarray(pair), precision=lax.Precision.HIGHEST)
    return jnp.where(jnp.asarray(np.tile(valid, (1, 2)))[None, None], t, NEG_INF)


def kernel(x, c, ctx, c_ctx, w_ada, b_ada, g_pre_mix, g_post_mix, g_pre_mlp, g_post_mlp, w_in, ret_decay, ret_gn, na_rpb, w_out, w_mlp1, w_mlp2, loss_target, m_c_ctx, m_w_ada, m_b_ada, m_g_pre_mix, m_g_post_mix, m_g_pre_mlp, m_g_post_mlp, m_w_in, m_ret_decay, m_ret_gn, m_na_rpb, m_w_out, m_w_mlp1, m_w_mlp2, v_c_ctx, v_w_ada, v_b_ada, v_g_pre_mix, v_g_post_mix, v_g_pre_mlp, v_g_post_mlp, v_w_in, v_ret_decay, v_ret_gn, v_na_rpb, v_w_out, v_w_mlp1, v_w_mlp2):
    B, N, D = x.shape
    C = ctx.shape[1]
    T = C + N

    silu_all, mods_g, win_b, wout_l, w1_l, w2_l = _mod_gather(c, c_ctx, w_ada[0], b_ada, w_in[0].T, w_out[0],
                                                             w_mlp1[0], w_mlp2[0])
    mods_mine = mods_g.transpose(1, 0, 2).reshape(mods_g.shape[1], N_MOD * D)
    modl = jnp.concatenate([mods_mine[:B], mods_mine[SUBLANES:SUBLANES + 1]], axis=0)
    modl = modl.reshape(B + 1, N_MOD, 1, D)
    rin = w_in.shape[2]
    rout, c1, r2 = wout_l.shape[0], w1_l.shape[1], w2_l.shape[0]

    def rows_of(n):
        return lambda ref: _row_block(ref, n)

    def cols_of(n):
        return lambda ref: _col_block(ref, n)

    cos, sin = _rope_tables(C, N)
    onehot, valid, oh2 = _na_tables()
    bias2 = _paired_bias(na_rpb[0], onehot, valid)
    lg = jax.nn.log_sigmoid(ret_decay[0].astype(F32))

    ag = _SplitScatter([wout_l, w1_l, w2_l], [rows_of(rout), cols_of(c1), rows_of(r2)],
                       [(N_DEV * rout, D), (D, N_DEV * c1), (N_DEV * r2, D)], "ag_mlp_start",
                       kind="gather", masks=SIBLING + ICI_SAME_CORE)
    h_all, proj = _inproj_fwd(x, ctx, modl, g_pre_mix + ag.token[0, 0], win_b)
    o_ret, lat_ret, q_rot, k_rot = _ret_fwd(proj, cos, sin, lg, ret_gn, C)
    lat_na, na_probs = _na_fwd(proj, bias2, C)
    wout_part, w1_part, w2_part = _scatter_wait([ag], lat_na, "ag_mlp_wait")

    (dy1, dlat_ret, dlat_na, dmix, h2, act, du, dz, red_d) = _dense_core(
        lat_ret, lat_na, x, loss_target, modl, g_post_mix, g_pre_mlp, g_post_mlp, wout_part, w1_part, w2_part)

    gw_out_p = _tn_matmul(lat_ret[:, None], dmix, "gw_out_ret", rows_after=lat_na.shape[-1])
    gw_out_p = _tn_matmul(lat_na[:, None], dmix, "gw_out_na", rows_before=lat_ret.shape[-1], into=gw_out_p)
    gw1_p = _tn_matmul(h2[:, None], du, "gw_mlp1")
    gw2_p = _tn_matmul(act[:, None], dz, "gw_mlp2")
    rs_mlp = _SplitScatter([gw_out_p, gw1_p, gw2_p], [rows_of(rout), cols_of(c1), rows_of(r2)],
                           [(rout, D), (D, c1), (r2, D)], "rs_mlp_start")

    dret, dgn_p, dlg_p = _ret_bwd(proj, q_rot, k_rot, cos, sin, lg, ret_gn + rs_mlp.token[0, 0], o_ret, dlat_ret, C)
    dna, dbias2 = _na_bwd(proj, na_probs, dlat_na, C)
    ret_cols, na_cols = dret.shape[1] * dret.shape[3], dna.shape[1] * dna.shape[3]
    gwin_t_p = _tn_matmul(dret, h_all, "gw_in_ret", rows_after=na_cols)
    gwin_t_p = _tn_matmul(dna, h_all, "gw_in_na", rows_before=ret_cols, into=gwin_t_p)
    rs_in = _SplitScatter([gwin_t_p], [rows_of(rin)], [(rin, D)], "rs_w_in_start")
    grad_x, red_i = _inproj_bwd(dret, dna, x, ctx, dy1, modl, g_pre_mix + rs_in.token[0, 0], win_b)

    rd = red_d
    nct = red_i.shape[1] * C // T
    ri_ctx = red_i[:, :nct].sum(axis=(0, 1))
    ri_lat = red_i[:, nct:].sum(axis=1)
    d_mods = jnp.concatenate([ri_lat[:, 0], ri_lat[:, 1], rd[:, 0], rd[:, 4], rd[:, 3], rd[:, 2]], axis=-1)
    d_cmods = jnp.concatenate([ri_ctx[0], ri_ctx[1], jnp.zeros(((N_MOD - 2) * D,), F32)])[None]
    dg_pre_mix = ri_lat[:, 2].sum(axis=0) + ri_ctx[2]
    dg_post_mix = rd[:, 1].sum(axis=0)
    dg_pre_mlp = rd[:, 5].sum(axis=0)
    dg_post_mlp = rd[:, 6].sum(axis=0)
    loss_p = rd[:, 7, 0].sum()
    d_gn = dgn_p[:, 0].sum(axis=0)
    d_lg = dlg_p[:, :, :2, 0].sum(axis=0).T
    d_decay = d_lg * jax.nn.sigmoid(-ret_decay[0].astype(F32))
    rr = _rpb_reduce(dbias2, jnp.asarray(oh2, BF16)).reshape(NA_HEADS, 2 * NA_KH - 2, LANES)
    ncls = 2 * NA_KW - 1
    d_rpb = (jnp.pad(rr[:, :, :ncls], ((0, 0), (0, 1), (0, 0))) + jnp.pad(rr[:, :, 32:32 + ncls], ((0, 0), (1, 0), (0, 0))))
    d_rpb32 = jnp.pad(d_rpb, ((0, 0), (0, 0), (0, 32 - ncls)))
    pieces = [dg_pre_mix, dg_post_mix, dg_pre_mlp, dg_post_mlp, d_gn, d_rpb32.reshape(-1),
              jnp.pad(d_decay.reshape(-1), (0, LANES - d_decay.size)), jnp.full((LANES,), loss_p, F32)]
    vec = jnp.concatenate(pieces)
    nm = N_MOD * D
    n_vec_rows = -(-vec.shape[0] // nm)
    assert B + 1 + n_vec_rows <= SUBLANES
    vec = jnp.pad(vec, (0, n_vec_rows * nm - vec.shape[0])).reshape(n_vec_rows, nm)
    dm_slot = jnp.concatenate([d_mods, d_cmods, vec, jnp.zeros((SUBLANES - B - 1 - n_vec_rows, nm), F32)], axis=0)
    def whole(ref):
        return lambda p: ref

    small = _SplitScatter([dm_slot], [whole], [dm_slot.shape], "small_start")
    land_out, land_1, land_2, land_in = _scatter_wait([rs_mlp, rs_in], small.token, "rs_wait")
    fused = {"w_in": [a.T for a in _sum_adamw(land_in, w_in[0].T, m_w_in[0].T, v_w_in[0].T, "sum_adamw_w_in")],
             "w_out": _sum_adamw(land_out, w_out[0], m_w_out[0], v_w_out[0], "sum_adamw_w_out"),
             "w_mlp1": _sum_adamw(land_1, w_mlp1[0], m_w_mlp1[0], v_w_mlp1[0], "sum_adamw_w_mlp1"),
             "w_mlp2": _sum_adamw(land_2, w_mlp2[0], m_w_mlp2[0], v_w_mlp2[0], "sum_adamw_w_mlp2")}
    (mbuf,) = _scatter_wait([small], fused["w_mlp2"][0], "small_wait")
    tot, g_b_ada, g_w_ada, g_c_ctx = _small_ar(mbuf, silu_all, w_ada[0], c_ctx, B + 1, n_vec_rows)
    flat = tot.reshape(-1)
    o0 = 0
    g_pre_mix_g = flat[o0:o0 + D]; o0 += D
    g_post_mix_g = flat[o0:o0 + D]; o0 += D
    g_pre_mlp_g = flat[o0:o0 + D]; o0 += D
    g_post_mlp_g = flat[o0:o0 + D]; o0 += D
    g_gn = flat[o0:o0 + RET_WIDTH]; o0 += RET_WIDTH
    nrpb = NA_HEADS * (2 * NA_KH - 1) * 32
    g_rpb = flat[o0:o0 + nrpb].reshape(NA_HEADS, 2 * NA_KH - 1, 32)[:, :, :ncls]; o0 += nrpb
    g_decay = flat[o0:o0 + 2 * RET_HEADS].reshape(2, RET_HEADS); o0 += LANES
    loss = flat[o0]

    grads = {
        "c_ctx": g_c_ctx.reshape(c_ctx.shape), "w_ada": g_w_ada[None], "b_ada": g_b_ada.reshape(b_ada.shape),
        "g_pre_mix": g_pre_mix_g[None], "g_post_mix": g_post_mix_g[None], "g_pre_mlp": g_pre_mlp_g[None],
        "g_post_mlp": g_post_mlp_g[None], "w_in": fused["w_in"][0][None], "ret_decay": g_decay[None], "ret_gn": g_gn[None],
        "na_rpb": g_rpb[None], "w_out": fused["w_out"][0][None], "w_mlp1": fused["w_mlp1"][0][None],
        "w_mlp2": fused["w_mlp2"][0][None],
    }
    weights = dict(c_ctx=c_ctx, w_ada=w_ada, b_ada=b_ada, g_pre_mix=g_pre_mix, g_post_mix=g_post_mix,
                   g_pre_mlp=g_pre_mlp, g_post_mlp=g_post_mlp, w_in=w_in, ret_decay=ret_decay, ret_gn=ret_gn,
                   na_rpb=na_rpb, w_out=w_out, w_mlp1=w_mlp1, w_mlp2=w_mlp2)
    m_in = dict(c_ctx=m_c_ctx, w_ada=m_w_ada, b_ada=m_b_ada, g_pre_mix=m_g_pre_mix, g_post_mix=m_g_post_mix,
                g_pre_mlp=m_g_pre_mlp, g_post_mlp=m_g_post_mlp, w_in=m_w_in, ret_decay=m_ret_decay,
                ret_gn=m_ret_gn, na_rpb=m_na_rpb, w_out=m_w_out, w_mlp1=m_w_mlp1, w_mlp2=m_w_mlp2)
    v_in = dict(c_ctx=v_c_ctx, w_ada=v_w_ada, b_ada=v_b_ada, g_pre_mix=v_g_pre_mix, g_post_mix=v_g_post_mix,
                g_pre_mlp=v_g_pre_mlp, g_post_mlp=v_g_post_mlp, w_in=v_w_in, ret_decay=v_ret_decay,
                ret_gn=v_ret_gn, na_rpb=v_na_rpb, w_out=v_w_out, w_mlp1=v_w_mlp1, w_mlp2=v_w_mlp2)
    names = list(weights)
    deltas, new_m, new_v = {}, {}, {}
    def as_2d(n):
        shp = weights[n].shape
        two_d = (-1, shp[-1]) if len(shp) > 1 else (1, shp[0])
        return [a.reshape(two_d) for a in (weights[n], grads[n], m_in[n], v_in[n])]

    small = [n for n in names if n not in fused and weights[n].size <= 65536]
    updated = dict(zip(small, _adamw_small([as_2d(n) for n in small], "adamw_small")))
    for n in names:
        if n in fused:
            updated[n] = fused[n][1:]
        elif n not in updated:
            updated[n] = _adamw(*as_2d(n), "adamw_" + n)
        deltas[n], new_m[n], new_v[n] = (a.reshape(weights[n].shape) for a in updated[n])
    return (loss, grad_x, *[grads[n] for n in names], *[deltas[n] for n in names],
            *[new_m[n] for n in names], *[new_v[n] for n in names])
```

```python
import functools
import math

import numpy as np
import jax
import jax.numpy as jnp
from jax import lax
from jax.experimental import pallas as pl
from jax.experimental.pallas import tpu as pltpu

F32 = jnp.float32
BF16 = jnp.bfloat16
MESH = pl.DeviceIdType.MESH

N_DEV = 8
LANES = 128
SUBLANES = 8
VMEM_LIMIT = 60 * 1024 * 1024

GRID_W = 64
RET_HEADS = 4
RET_DIM = 128
RET_WIDTH = RET_HEADS * RET_DIM
NA_HEADS = 8
NA_DIM = 64
NA_WIDTH = NA_HEADS * NA_DIM
NA_PAIRS = NA_HEADS // 2
NA_KH = 8
NA_KW = 16
NA_GROUP = 8
SEG = 512
ROPE_BASE = 10000.0
NORM_EPS = 1e-6
NEG_INF = -1e30
N_MOD = 6

ADAM_LR = 0.001
ADAM_B1 = 0.9
ADAM_B2 = 0.999
ADAM_EPS = 1e-08
ADAM_WD = 0.01
ADAM_STEP = 10


def _dot(a, b):
    return lax.dot_general(a, b, (((1,), (0,)), ((), ())), preferred_element_type=F32)


def _dot_nt(a, b):
    return lax.dot_general(a, b, (((1,), (1,)), ((), ())), preferred_element_type=F32)


def _dot_tn(a, b):
    return lax.dot_general(a, b, (((0,), (0,)), ((), ())), preferred_element_type=F32)


def _sigmoid(x):
    return 1.0 / (1.0 + jnp.exp(-x))


def _div_tile(n, cap, mult):
    if n <= cap:
        return n
    for t in range(cap - cap % mult, 0, -mult):
        if n % t == 0:
            return t
    raise ValueError(f"no tile for {n}")


def _params(*sem):
    return pltpu.CompilerParams(dimension_semantics=tuple(sem) if sem else None,
                                vmem_limit_bytes=VMEM_LIMIT)


def _vmem():
    return pl.BlockSpec(memory_space=pltpu.VMEM)


def _any():
    return pl.BlockSpec(memory_space=pl.ANY)


def _me_and_peers():
    x, y, c = lax.axis_index("x"), lax.axis_index("y"), lax.axis_index("c")
    me = 4 * x + 2 * y + c
    peers = []
    for m in range(1, N_DEV):
        px = 1 - x if (m >> 2) & 1 else x
        py = 1 - y if (m >> 1) & 1 else y
        pc = 1 - c if m & 1 else c
        peers.append(((px, py, pc), 4 * px + 2 * py + pc))
    return me, peers


def _exchange(src_for, dst_from, send_sems, recv_sems):
    me, peers = _me_and_peers()
    sent = []
    for i, (dev, pid) in enumerate(peers):
        cp = pltpu.make_async_remote_copy(src_ref=src_for(pid), dst_ref=dst_from(me),
                                          send_sem=send_sems.at[i], recv_sem=recv_sems.at[i],
                                          device_id=dev, device_id_type=MESH)
        cp.start()
        sent.append(cp)
    for i, (dev, pid) in enumerate(peers):
        pltpu.make_async_remote_copy(src_ref=src_for(pid), dst_ref=dst_from(pid),
                                     send_sem=send_sems.at[i], recv_sem=recv_sems.at[i],
                                     device_id=dev, device_id_type=MESH).wait_recv()
    for cp in sent:
        cp.wait_send()


SIBLING = (1,)
ICI_SAME_CORE = (2, 4, 6)
ALL_PEERS = tuple(range(1, N_DEV))


def _remote(src, dst, send_sem, recv_sem, dev):
    return pltpu.make_async_remote_copy(src_ref=src, dst_ref=dst, send_sem=send_sem, recv_sem=recv_sem,
                                        device_id=dev, device_id_type=MESH)


def _push_start(items, masks, send_sems, recv_sems):
    me, peers = _me_and_peers()
    for k, (src_for, dst_from) in enumerate(items):
        for m in masks:
            dev, pid = peers[m - 1]
            _remote(src_for(pid), dst_from(me), send_sems.at[k, m - 1], recv_sems.at[k, m - 1], dev).start()


def _push_wait_recv(items, masks, send_sems, recv_sems):
    me, peers = _me_and_peers()
    for k, (src_for, dst_from) in enumerate(items):
        for m in masks:
            dev, pid = peers[m - 1]
            _remote(src_for(pid), dst_from(pid), send_sems.at[k, m - 1], recv_sems.at[k, m - 1], dev).wait_recv()


def _push_wait_send(items, masks, send_sems, recv_sems):
    me, peers = _me_and_peers()
    for k, (src_for, dst_from) in enumerate(items):
        for m in masks:
            dev, pid = peers[m - 1]
            _remote(src_for(pid), dst_from(me), send_sems.at[k, m - 1], recv_sems.at[k, m - 1], dev).wait_send()


def _forward_start(items, send_sems, recv_sems):
    me, peers = _me_and_peers()
    sib = peers[0][0]
    for k, (blk_in, blk_out) in enumerate(items):
        for j, m in enumerate(ICI_SAME_CORE):
            pid = peers[m - 1][1]
            _remote(blk_in(pid), blk_out(pid), send_sems.at[k, j], recv_sems.at[k, j], sib).start()


def _forward_wait(items, send_sems, recv_sems):
    me, peers = _me_and_peers()
    sib = peers[0][0]
    for k, (blk_in, blk_out) in enumerate(items):
        for j, m in enumerate(ICI_SAME_CORE):
            got = peers[(m | 1) - 1][1]
            _remote(blk_in(got), blk_out(got), send_sems.at[k, j], recv_sems.at[k, j], sib).wait_recv()
    for k, (blk_in, blk_out) in enumerate(items):
        for j, m in enumerate(ICI_SAME_CORE):
            pid = peers[m - 1][1]
            _remote(blk_in(pid), blk_out(pid), send_sems.at[k, j], recv_sems.at[k, j], sib).wait_send()


def _mod_gather(c, c_ctx, w_ada, b_ada, w_in_t, w_out, w1, w2):
    B, D = c.shape
    ncol = w_ada.shape[1]
    rows = SUBLANES * N_DEV + SUBLANES

    def body(c_ref, cc_ref, w_ref, b_ref, win_ref, wout_ref, w1_ref, w2_ref,
             s_ref, m_ref, gin_ref, wout_b, w1_b, w2_b,
             win_b, msend, send1, recv1, send2, recv2, wsend, wrecv, fsend, frecv, lsem):
        me, _ = _me_and_peers()
        win_b[...] = win_ref[...].astype(BF16)
        block = _row_block(gin_ref, w_in_t.shape[0])
        gather = [(lambda p: win_b, block)]
        own = pltpu.make_async_copy(win_b, block(me), lsem.at[0])
        cv = c_ref[...]
        slot = jnp.concatenate([cv * _sigmoid(cv), jnp.zeros((SUBLANES - B, D), F32)], axis=0)
        my_rows = pl.ds(pl.multiple_of(me * SUBLANES, SUBLANES), SUBLANES)
        s_ref[my_rows, :] = slot
        ccv = cc_ref[...]
        s_ref[SUBLANES * N_DEV:, :] = jnp.concatenate(
            [ccv * _sigmoid(ccv), jnp.zeros((SUBLANES - 1, D), F32)], axis=0)

        def rows_of(p):
            return s_ref.at[pl.ds(pl.multiple_of(p * SUBLANES, SUBLANES), SUBLANES), :]

        _exchange(lambda p: rows_of(me), rows_of, send1, recv1)
        own.start()
        _push_start(gather, SIBLING + ICI_SAME_CORE, wsend, wrecv)
        wout_b[...] = wout_ref[...].astype(BF16)
        w1_b[...] = w1_ref[...].astype(BF16)
        w2_b[...] = w2_ref[...].astype(BF16)
        b_loc = b_ref[:, pl.ds(pl.multiple_of(me * ncol, ncol), ncol)]
        mods = _dot(s_ref[...], w_ref[...]) + b_loc
        for p in range(N_DEV):
            msend[p] = jnp.concatenate([mods[p * SUBLANES:(p + 1) * SUBLANES], mods[N_DEV * SUBLANES:]], axis=0)
        m_ref[me] = msend[me]
        columns = [(lambda p: msend.at[p], lambda p: m_ref.at[p])]
        _push_start(columns, ALL_PEERS, send2, recv2)
        _push_wait_recv(gather, ICI_SAME_CORE, wsend, wrecv)
        relay = [(block, block)]
        _forward_start(relay, fsend, frecv)
        _push_wait_recv(columns, ALL_PEERS, send2, recv2)
        _push_wait_recv(gather, SIBLING, wsend, wrecv)
        _forward_wait(relay, fsend, frecv)
        _push_wait_send(columns, ALL_PEERS, send2, recv2)
        _push_wait_send(gather, SIBLING + ICI_SAME_CORE, wsend, wrecv)
        own.wait()

    return pl.pallas_call(
        body, name="mod_gather",
        out_shape=(jax.ShapeDtypeStruct((rows, D), F32), jax.ShapeDtypeStruct((N_DEV, 2 * SUBLANES, ncol), F32),
                   jax.ShapeDtypeStruct((N_DEV * w_in_t.shape[0], D), BF16),
                   jax.ShapeDtypeStruct(w_out.shape, BF16), jax.ShapeDtypeStruct(w1.shape, BF16),
                   jax.ShapeDtypeStruct(w2.shape, BF16)),
        in_specs=[_vmem()] * 8, out_specs=(_vmem(), _vmem(), _any(), _vmem(), _vmem(), _vmem()),
        scratch_shapes=[pltpu.VMEM(w_in_t.shape, BF16), pltpu.VMEM((N_DEV, 2 * SUBLANES, ncol), F32)]
                       + [pltpu.SemaphoreType.DMA((N_DEV - 1,))] * 2
                       + [pltpu.SemaphoreType.DMA((1, N_DEV - 1))] * 4 + [pltpu.SemaphoreType.DMA((1, 3))] * 2
                       + [pltpu.SemaphoreType.DMA((1,))],
        compiler_params=pltpu.CompilerParams(vmem_limit_bytes=VMEM_LIMIT),
    )(c, c_ctx.reshape(1, D), w_ada, b_ada, w_in_t, w_out, w1, w2)


def _row_block(ref, rows):
    return lambda p: ref.at[pl.ds(pl.multiple_of(p * rows, 2 * SUBLANES), rows), :]


def _col_block(ref, cols):
    return lambda p: ref.at[:, pl.ds(pl.multiple_of(p * cols, LANES), cols)]


def _slot(ref):
    return lambda p: ref.at[p]


def _grid_call(body, *, name, grid, out_shape, in_specs, out_specs, scratch_shapes, args):
    return pl.pallas_call(
        body, name=name, grid=grid, out_shape=tuple(out_shape), in_specs=list(in_specs), out_specs=tuple(out_specs),
        scratch_shapes=list(scratch_shapes), compiler_params=_params(*(("arbitrary",) * len(grid))),
    )(*args)


def _token_tiles(n_ctx, tm):
    nct = n_ctx // tm

    def ctx_spec(D):
        return pl.BlockSpec((None, tm, D), lambda b, t: (b, jnp.minimum(t, nct - 1), 0))

    def lat_spec(D):
        return pl.BlockSpec((None, tm, D), lambda b, t: (b, jnp.maximum(t - nct, 0), 0))

    return nct, ctx_spec, lat_spec


def _inproj_fwd(x, ctx, modl, g1, w_in_t):
    B, N, D = x.shape
    n_ctx = ctx.shape[1]
    T = n_ctx + N
    nw = w_in_t.shape[0]
    tm = _div_tile(n_ctx, 256, 16)
    nct, ctx_spec, lat_spec = _token_tiles(n_ctx, tm)

    def body(c_ref, x_ref, sh_ref, sc_ref, g_ref, w_ref, h_ref, p_ref):
        x = jnp.where(pl.program_id(1) < nct, c_ref[...], x_ref[...])
        r = lax.rsqrt(jnp.mean(x * x, axis=-1, keepdims=True) + NORM_EPS)
        h = ((x * r) * g_ref[...]) * (1.0 + sc_ref[...]) + sh_ref[...]
        hb = h.astype(BF16)
        h_ref[...] = hb
        p_ref[...] = _dot_nt(hb, w_ref[...]).astype(BF16)

    def mrow(b, t):
        return jnp.where(t < nct, B, b)

    return _grid_call(
        body, name="inproj_fwd", grid=(B, T // tm),
        out_shape=(jax.ShapeDtypeStruct((B, T, D), BF16), jax.ShapeDtypeStruct((B, T, nw), BF16)),
        in_specs=[ctx_spec(D), lat_spec(D),
                  pl.BlockSpec((None, None, 1, D), lambda b, t: (mrow(b, t), 0, 0, 0)),
                  pl.BlockSpec((None, None, 1, D), lambda b, t: (mrow(b, t), 1, 0, 0)),
                  pl.BlockSpec((1, D), lambda b, t: (0, 0)),
                  pl.BlockSpec((nw, D), lambda b, t: (0, 0))],
        out_specs=(pl.BlockSpec((None, tm, D), lambda b, t: (b, t, 0)),
                   pl.BlockSpec((None, tm, nw), lambda b, t: (b, t, 0))),
        scratch_shapes=[], args=(ctx, x, modl, modl, g1, w_in_t))


def _swap32(x):
    lane = lax.broadcasted_iota(jnp.int32, x.shape, 1)
    return jnp.where((lane % 64) < 32, pltpu.roll(x, 96, 1), pltpu.roll(x, 32, 1))


def _rope(x, cos, sin):
    return x * cos + _swap32(x) * sin


def _unrope(dy, cos, sin):
    return dy * cos + _swap32(dy * sin)


def _ret_weights(lgf, lgb, dist):
    return jnp.exp(jnp.where(dist >= 0.0, lgf * dist, -lgb * dist))


class _RetDecay:
    def __init__(self, lgf, lgb, rows):
        r = lax.broadcasted_iota(jnp.int32, (rows, RET_DIM), 0).astype(F32)
        self.head = r + 1.0
        self.tail = (rows - 1.0) - r
        self.q_f = jnp.exp(lgf * self.head)
        self.k_f = jnp.exp(lgf * self.tail)
        self.q_b = jnp.exp(lgb * self.tail)
        self.k_b = jnp.exp(lgb * self.head)


def _ret_states(kf32, vs, lgf, lgb, C, c, nt, hf, hb, hfa=None, hba=None):
    dec = _RetDecay(lgf, lgb, c)
    dec_c = _RetDecay(lgf, lgb, C)
    step_f = jnp.exp(jnp.zeros((RET_DIM, RET_DIM), F32) + lgf * c)
    step_b = jnp.exp(jnp.zeros((RET_DIM, RET_DIM), F32) + lgb * c)

    def upd(rows, kdec):
        return _dot_tn((kf32[rows, :] * kdec).astype(BF16), vs[rows, :])

    def lat(t):
        return slice(C + t * c, C + (t + 1) * c)

    state = upd(slice(0, C), dec_c.k_f)
    aged = jnp.zeros_like(state)
    for t in range(nt):
        hf[t] = state.astype(BF16)
        if hfa is not None:
            hfa[t] = aged
        if t < nt - 1:
            aged = step_f * (aged + c * state)
            state = step_f * state + upd(lat(t), dec.k_f)
    state = upd(slice(0, C), dec_c.k_b)
    aged = jnp.zeros_like(state)
    for t in range(nt - 1, -1, -1):
        hb[t] = state.astype(BF16)
        if hba is not None:
            hba[t] = aged
        if t > 0:
            aged = step_b * (aged + c * state)
            state = step_b * state + upd(lat(t), dec.k_b)
    return dec, dec_c, step_f, step_b


def _ret_fwd(proj, cos, sin, lg, gn, n_ctx):
    B, T, _ = proj.shape
    C = n_ctx
    N = T - C
    c = _div_tile(N, 256, 16)
    nt = N // c
    scale = RET_DIM ** -0.5

    def body(lg_ref, q_ref, k_ref, v_ref, g_ref, cos_ref, sin_ref, gn_ref, o_ref, lat_ref, qr_ref, kf32,
             qs, ks, vs, hf, hb):
        h = pl.program_id(1)
        lgf = lg_ref[0, h]
        lgb = lg_ref[1, h]
        for rows in [slice(0, C)] + [slice(C + t * c, C + (t + 1) * c) for t in range(nt)]:
            cosb = cos_ref[rows, :]
            sinb = sin_ref[rows, :]
            qr = _rope(q_ref[rows, :].astype(F32), cosb, sinb) * scale
            qr_ref[rows, :] = qr
            qs[rows, :] = qr.astype(BF16)
            kr = _rope(k_ref[rows, :].astype(F32), cosb, sinb)
            kf32[rows, :] = kr
            ks[rows, :] = kr.astype(BF16)
            vs[rows, :] = v_ref[rows, :].astype(BF16)
        gnv = gn_ref[...]
        dec, _, _, _ = _ret_states(kf32, vs, lgf, lgb, C, c, nt, hf, hb)
        rc = (lax.broadcasted_iota(jnp.int32, (c, c), 0) - lax.broadcasted_iota(jnp.int32, (c, c), 1)).astype(F32)
        w_diag = _ret_weights(lgf, lgb, rc)
        for t in range(nt):
            rows = slice(C + t * c, C + (t + 1) * c)
            qt = qs[rows, :]
            s = _dot_nt(qt, ks[rows, :])
            o = (_dot((s * w_diag).astype(BF16), vs[rows, :])
                 + dec.q_f * _dot(qt, hf[t]) + dec.q_b * _dot(qt, hb[t]))
            o_ref[t * c:(t + 1) * c, :] = o
            mu = jnp.mean(o, axis=-1, keepdims=True)
            oc = o - mu
            var = jnp.mean(oc * oc, axis=-1, keepdims=True)
            yh = oc * lax.rsqrt(var + NORM_EPS)
            g = g_ref[rows, :].astype(F32)
            lat_ref[t * c:(t + 1) * c, :] = ((yh * gnv) * (g * _sigmoid(g))).astype(BF16)

    def col(seg):
        return pl.BlockSpec((None, T, RET_DIM), lambda b, h, seg=seg: (b, 0, seg * RET_HEADS + h))

    return _grid_call(
        body, name="ret_fwd", grid=(B, RET_HEADS),
        out_shape=(jax.ShapeDtypeStruct((B, N, RET_WIDTH), F32), jax.ShapeDtypeStruct((B, N, RET_WIDTH), BF16),
                   jax.ShapeDtypeStruct((B, T, RET_WIDTH), F32), jax.ShapeDtypeStruct((B, T, RET_WIDTH), F32)),
        in_specs=[pl.BlockSpec(memory_space=pltpu.SMEM), col(0), col(1), col(2), col(3),
                  pl.BlockSpec((T, RET_DIM), lambda b, h: (0, 0)), pl.BlockSpec((T, RET_DIM), lambda b, h: (0, 0)),
                  pl.BlockSpec((1, RET_DIM), lambda b, h: (0, h))],
        out_specs=(pl.BlockSpec((None, N, RET_DIM), lambda b, h: (b, 0, h)),
                   pl.BlockSpec((None, N, RET_DIM), lambda b, h: (b, 0, h)),
                   pl.BlockSpec((None, T, RET_DIM), lambda b, h: (b, 0, h)),
                   pl.BlockSpec((None, T, RET_DIM), lambda b, h: (b, 0, h))),
        scratch_shapes=[pltpu.VMEM((T, RET_DIM), BF16)] * 3 + [pltpu.VMEM((nt, RET_DIM, RET_DIM), BF16)] * 2,
        args=(lg, proj, proj, proj, proj, cos, sin, gn))


def _ret_bwd(proj, q_rot, k_rot, cos, sin, lg, gn, o, dlat, n_ctx):
    B, T, _ = proj.shape
    C = n_ctx
    N = T - C
    c = _div_tile(N, 256, 16)
    nt = N // c
    scale = RET_DIM ** -0.5

    def lat(t):
        return slice(C + t * c, C + (t + 1) * c)

    def body(lg_ref, qf32, kf32, v_ref, g_ref, cos_ref, sin_ref, gn_ref, o_ref, dl_ref,
             d_ref, dgn_ref, dlg_ref, qs, ks, vs, dos, hf, hb, hfa, hba, gf_s, gb_s):
        h = pl.program_id(1)
        lgf = lg_ref[0, h]
        lgb = lg_ref[1, h]
        gnv = gn_ref[...]

        def fold(a):
            return jnp.sum(a.reshape(a.shape[0] // SUBLANES, SUBLANES, a.shape[1]), axis=0)

        for rows in [slice(0, C)] + [lat(t) for t in range(nt)]:
            qs[rows, :] = qf32[rows, :].astype(BF16)
            ks[rows, :] = kf32[rows, :].astype(BF16)
            vs[rows, :] = v_ref[rows, :].astype(BF16)

        dgn = jnp.zeros((1, RET_DIM), F32)
        for t in range(nt):
            lrows = slice(t * c, (t + 1) * c)
            ov = o_ref[lrows, :]
            mu = jnp.mean(ov, axis=-1, keepdims=True)
            oc = ov - mu
            var = jnp.mean(oc * oc, axis=-1, keepdims=True)
            rstd = lax.rsqrt(var + NORM_EPS)
            yh = oc * rstd
            g = g_ref[lat(t), :].astype(F32)
            sg = _sigmoid(g)
            dl = dl_ref[lrows, :]
            d_ref[3, lat(t), :] = (dl * (yh * gnv) * (sg * (1.0 + g * (1.0 - sg)))).astype(BF16)
            dls = dl * (g * sg)
            dgn = dgn + jnp.sum(dls * yh, axis=0, keepdims=True)
            dyh = dls * gnv
            do = rstd * (dyh - jnp.mean(dyh, axis=-1, keepdims=True)
                         - yh * jnp.mean(dyh * yh, axis=-1, keepdims=True))
            dos[lrows, :] = do.astype(BF16)
        dgn_ref[...] = jnp.concatenate([dgn, jnp.zeros((SUBLANES - 1, RET_DIM), F32)], axis=0)
        d_ref[3, 0:C, :] = jnp.zeros((C, RET_DIM), BF16)
        d_ref[0, 0:C, :] = jnp.zeros((C, RET_DIM), BF16)

        dec, dec_c, step_f, step_b = _ret_states(kf32, vs, lgf, lgb, C, c, nt, hf, hb, hfa, hba)

        def zmat(t, qdec):
            return _dot_tn((qf32[lat(t), :] * qdec).astype(BF16), dos[t * c:(t + 1) * c, :])

        acc3f = jnp.zeros((RET_DIM, RET_DIM), F32)
        acc3b = jnp.zeros((RET_DIM, RET_DIM), F32)
        state = jnp.zeros((RET_DIM, RET_DIM), F32)
        for t in range(nt - 1, -1, -1):
            gf_s[t] = state.astype(BF16)
            z = zmat(t, dec.q_f)
            acc3f = acc3f + hfa[t] * z
            state = step_f * state + z
        gctx_f = state.astype(BF16)
        state = jnp.zeros((RET_DIM, RET_DIM), F32)
        for t in range(nt):
            gb_s[t] = state.astype(BF16)
            z = zmat(t, dec.q_b)
            acc3b = acc3b + hba[t] * z
            state = step_b * state + z
        gctx_b = state.astype(BF16)

        rc = (lax.broadcasted_iota(jnp.int32, (c, c), 0) - lax.broadcasted_iota(jnp.int32, (c, c), 1)).astype(F32)
        w_diag = _ret_weights(lgf, lgb, rc)
        wg_f = jnp.where(rc >= 0.0, w_diag * rc, 0.0)
        wg_b = jnp.where(rc < 0.0, -w_diag * rc, 0.0)
        accf = jnp.zeros((SUBLANES, RET_DIM), F32)
        accb = jnp.zeros((SUBLANES, RET_DIM), F32)
        gdf = jnp.zeros((SUBLANES, c), F32)
        gdb = jnp.zeros((SUBLANES, c), F32)
        for t in range(nt):
            rows = lat(t)
            qt = qs[rows, :]
            kt = ks[rows, :]
            vt = vs[rows, :]
            dot = dos[t * c:(t + 1) * c, :]
            s = _dot_nt(qt, kt)
            dp = _dot_nt(dot, vt)
            dv = _dot_tn((s * w_diag).astype(BF16), dot)
            ds = (dp * w_diag).astype(BF16)
            dq = _dot(ds, kt)
            dk = _dot_tn(ds, qt)
            gs = dp * s
            gdf = gdf + fold(gs * wg_f)
            gdb = gdb + fold(gs * wg_b)
            qv = qf32[rows, :]
            kv = kf32[rows, :]
            dq_f = dec.q_f * _dot_nt(dot, hf[t])
            dq_b = dec.q_b * _dot_nt(dot, hb[t])
            dk_f = dec.k_f * _dot_nt(vt, gf_s[t])
            dk_b = dec.k_b * _dot_nt(vt, gb_s[t])
            accf = accf + fold(dec.head * dq_f * qv) + fold(dec.tail * dk_f * kv)
            accb = accb + fold(dec.tail * dq_b * qv) + fold(dec.head * dk_b * kv)
            dv = dv + dec.k_f * _dot(kt, gf_s[t]) + dec.k_b * _dot(kt, gb_s[t])
            cosb = cos_ref[rows, :]
            sinb = sin_ref[rows, :]
            d_ref[0, rows, :] = _unrope((dq + dq_f + dq_b) * scale, cosb, sinb).astype(BF16)
            d_ref[1, rows, :] = _unrope(dk + dk_f + dk_b, cosb, sinb).astype(BF16)
            d_ref[2, rows, :] = dv.astype(BF16)
        kc = ks[0:C, :]
        vc = vs[0:C, :]
        kcv = kf32[0:C, :]
        dkc_f = dec_c.k_f * _dot_nt(vc, gctx_f)
        dkc_b = dec_c.k_b * _dot_nt(vc, gctx_b)
        accf = accf + fold(dec_c.tail * dkc_f * kcv)
        accb = accb + fold(dec_c.head * dkc_b * kcv)
        d_ref[1, 0:C, :] = (dkc_f + dkc_b).astype(BF16)
        d_ref[2, 0:C, :] = (dec_c.k_f * _dot(kc, gctx_f) + dec_c.k_b * _dot(kc, gctx_b)).astype(BF16)
        gf = jnp.sum(gdf) + jnp.sum(accf) + jnp.sum(acc3f)
        gb = jnp.sum(gdb) + jnp.sum(accb) + jnp.sum(acc3b)
        row = lax.broadcasted_iota(jnp.int32, (SUBLANES, LANES), 0)
        dlg_ref[...] = jnp.where(row == 0, gf, jnp.where(row == 1, gb, 0.0))

    def col(seg):
        return pl.BlockSpec((None, T, RET_DIM), lambda b, h, seg=seg: (b, 0, seg * RET_HEADS + h))

    def head(rows):
        return pl.BlockSpec((None, rows, RET_DIM), lambda b, h: (b, 0, h))

    return _grid_call(
        body, name="ret_bwd", grid=(B, RET_HEADS),
        out_shape=(jax.ShapeDtypeStruct((B, 4, T, RET_WIDTH), BF16),
                   jax.ShapeDtypeStruct((B, SUBLANES, RET_WIDTH), F32),
                   jax.ShapeDtypeStruct((B, RET_HEADS, SUBLANES, LANES), F32)),
        in_specs=[pl.BlockSpec(memory_space=pltpu.SMEM), head(T), head(T), col(2), col(3),
                  pl.BlockSpec((T, RET_DIM), lambda b, h: (0, 0)), pl.BlockSpec((T, RET_DIM), lambda b, h: (0, 0)),
                  pl.BlockSpec((1, RET_DIM), lambda b, h: (0, h)), head(N), head(N)],
        out_specs=(pl.BlockSpec((None, 4, T, RET_DIM), lambda b, h: (b, 0, 0, h)),
                   pl.BlockSpec((None, SUBLANES, RET_DIM), lambda b, h: (b, 0, h)),
                   pl.BlockSpec((None, None, SUBLANES, LANES), lambda b, h: (b, h, 0, 0))),
        scratch_shapes=[pltpu.VMEM((T, RET_DIM), BF16)] * 3 + [pltpu.VMEM((N, RET_DIM), BF16)]
                       + [pltpu.VMEM((nt, RET_DIM, RET_DIM), BF16)] * 2 + [pltpu.VMEM((nt, RET_DIM, RET_DIM), F32)] * 2
                       + [pltpu.VMEM((nt, RET_DIM, RET_DIM), BF16)] * 2,
        args=(lg, q_rot, k_rot, proj, proj, cos, sin, gn, o, dlat))


def _na_geometry(rows):
    kh = min(NA_KH, rows)
    return kh, kh * GRID_W


def _pair_select():
    lane = lax.broadcasted_iota(jnp.int32, (2 * GRID_W, LANES), 1)
    row = lax.broadcasted_iota(jnp.int32, (2 * GRID_W, LANES), 0)
    return (lane >= NA_DIM) == (row >= GRID_W)


def _pair_bias(bias_ref, dr0, kh):
    return jnp.concatenate(
        [jnp.concatenate([bias_ref[e, pl.ds(dr0 + 2 * m, 1)].reshape(GRID_W, LANES) for m in range(kh // 2)], axis=1)
         for e in range(2)], axis=0)


def _na_softmax(s_loc, s_ctx):
    mx = jnp.maximum(jnp.max(s_loc, axis=-1, keepdims=True), jnp.max(s_ctx, axis=-1, keepdims=True))
    p_loc = jnp.exp(s_loc - mx)
    p_ctx = jnp.exp(s_ctx - mx)
    den = jnp.sum(p_loc, axis=-1, keepdims=True) + jnp.sum(p_ctx, axis=-1, keepdims=True)
    return p_loc, p_ctx, den


def _na_fwd(proj, bias2, n_ctx):
    assert proj.dtype == BF16
    B, T, _ = proj.shape
    C = n_ctx
    N = T - C
    R = N // GRID_W
    kh, nk = _na_geometry(R)
    scale = NA_DIM ** -0.5
    base = (4 * RET_WIDTH) // LANES

    def body(q_ref, kb16, vb16, bias_ref, out_ref, p_ref):
        kc = kb16[0:C, :]
        vc = vb16[0:C, :]
        lane = lax.broadcasted_iota(jnp.int32, (GRID_W, LANES), 1)
        sel2 = _pair_select()

        def group(gi, carry):
            pre = []
            for u in range(NA_GROUP):
                r = gi * NA_GROUP + u
                bs = jnp.clip(r - kh // 2, 0, R - kh)
                dr0 = bs - r + (NA_KH - 1)
                q = q_ref[pl.ds(pl.multiple_of(C + r * GRID_W, GRID_W), GRID_W), :].astype(F32) * scale
                q2 = jnp.where(sel2, jnp.concatenate([q, q], axis=0), 0.0).astype(BF16)
                band = pl.ds(pl.multiple_of(C + bs * GRID_W, GRID_W), nk)
                s_loc = _dot_nt(q2, kb16[band, :]) + _pair_bias(bias_ref, dr0, kh)
                s_ctx = _dot_nt(q2, kc)
                pre.append((r, band, s_loc, s_ctx))
            mid = [(r, band) + _na_softmax(s_loc, s_ctx) for r, band, s_loc, s_ctx in pre]
            for r, band, p_loc, p_ctx, den in mid:
                inv = 1.0 / den
                pb_loc = (p_loc * inv).astype(BF16)
                pb_ctx = (p_ctx * inv).astype(BF16)
                p_ref[r, :, 0:nk] = pb_loc
                p_ref[r, :, nk:] = pb_ctx
                o2 = _dot(pb_loc, vb16[band, :]) + _dot(pb_ctx, vc)
                out_ref[pl.ds(pl.multiple_of(r * GRID_W, GRID_W), GRID_W), :] = jnp.where(
                    lane < NA_DIM, o2[:GRID_W], o2[GRID_W:]).astype(BF16)
            return carry

        lax.fori_loop(0, R // NA_GROUP, group, 0)

    def col(seg):
        return pl.BlockSpec((None, T, LANES), lambda b, p, seg=seg: (b, 0, base + seg * NA_PAIRS + p))

    return _grid_call(
        body, name="na_fwd", grid=(B, NA_PAIRS),
        out_shape=(jax.ShapeDtypeStruct((B, N, NA_WIDTH), BF16),
                   jax.ShapeDtypeStruct((B, NA_PAIRS, R, 2 * GRID_W, nk + C), BF16)),
        in_specs=[col(0), col(1), col(2),
                  pl.BlockSpec((2, 2 * NA_KH - 2, GRID_W, LANES), lambda b, p: (p, 0, 0, 0))],
        out_specs=(pl.BlockSpec((None, N, LANES), lambda b, p: (b, 0, p)),
                   pl.BlockSpec((None, None, R, 2 * GRID_W, nk + C), lambda b, p: (b, p, 0, 0, 0))),
        scratch_shapes=[],
        args=(proj, proj, proj, bias2))


def _na_bwd(proj, probs, dlat, n_ctx):
    assert proj.dtype == BF16
    B, T, _ = proj.shape
    C = n_ctx
    N = T - C
    R = N // GRID_W
    kh, nk = _na_geometry(R)
    scale = NA_DIM ** -0.5
    base = (4 * RET_WIDTH) // LANES

    def body(q_ref, kb16, vb16, p_ref, dl_ref, d_ref, db_ref, dkv):
        b = pl.program_id(1)
        kc = kb16[0:C, :]
        vc = vb16[0:C, :]
        lane = lax.broadcasted_iota(jnp.int32, (GRID_W, LANES), 1)
        dkv[...] = jnp.zeros(dkv.shape, F32)
        d_ref[0, 0:C, :] = jnp.zeros((C, LANES), BF16)

        @pl.when(b == 0)
        def _():
            db_ref[...] = jnp.zeros(db_ref.shape, F32)

        sel2 = _pair_select()

        def group(gi, carry):
            pre = []
            for u in range(NA_GROUP):
                r = gi * NA_GROUP + u
                bs = jnp.clip(r - kh // 2, 0, R - kh)
                dr0 = bs - r + (NA_KH - 1)
                q = q_ref[pl.ds(pl.multiple_of(C + r * GRID_W, GRID_W), GRID_W), :].astype(F32) * scale
                do = dl_ref[pl.ds(pl.multiple_of(r * GRID_W, GRID_W), GRID_W), :]
                q2 = jnp.where(sel2, jnp.concatenate([q, q], axis=0), 0.0).astype(BF16)
                do2 = jnp.where(sel2, jnp.concatenate([do, do], axis=0), 0.0).astype(BF16)
                band = pl.ds(pl.multiple_of(C + bs * GRID_W, GRID_W), nk)
                dp_loc = _dot_nt(do2, vb16[band, :])
                dp_ctx = _dot_nt(do2, vc)
                pre.append((r, dr0, band, q2, do2, dp_loc, dp_ctx))
            mid = []
            for r, dr0, band, q2, do2, dp_loc, dp_ctx in pre:
                pb_loc = p_ref[r, :, 0:nk]
                pb_ctx = p_ref[r, :, nk:]
                p_loc = pb_loc.astype(F32)
                p_ctx = pb_ctx.astype(F32)
                delta = (jnp.sum(p_loc * dp_loc, axis=-1, keepdims=True)
                         + jnp.sum(p_ctx * dp_ctx, axis=-1, keepdims=True))
                ds_loc = p_loc * (dp_loc - delta)
                ds_ctx = p_ctx * (dp_ctx - delta)
                mid.append((r, dr0, band, q2, do2, pb_loc, pb_ctx, ds_loc, ds_ctx))
            for r, dr0, band, q2, do2, pb_loc, pb_ctx, ds_loc, ds_ctx in mid:
                dsb_loc = ds_loc.astype(BF16)
                dsb_ctx = ds_ctx.astype(BF16)
                dq2 = _dot(dsb_loc, kb16[band, :]) + _dot(dsb_ctx, kc)
                d_ref[0, pl.ds(pl.multiple_of(C + r * GRID_W, GRID_W), GRID_W), :] = (jnp.where(
                    lane < NA_DIM, dq2[:GRID_W], dq2[GRID_W:]) * scale).astype(BF16)
                dkv[0, band, :] += _dot_tn(dsb_loc, q2)
                dkv[1, band, :] += _dot_tn(pb_loc, do2)
                dkv[0, 0:C, :] += _dot_tn(dsb_ctx, q2)
                dkv[1, 0:C, :] += _dot_tn(pb_ctx, do2)
                for e in range(2):
                    for m in range(kh // 2):
                        db_ref[e, pl.ds(dr0 + 2 * m, 1)] += ds_loc[e * GRID_W:(e + 1) * GRID_W,
                                                                   m * LANES:(m + 1) * LANES].reshape(1, GRID_W, LANES)
            return carry

        lax.fori_loop(0, R // NA_GROUP, group, 0)
        d_ref[1] = dkv[0].astype(BF16)
        d_ref[2] = dkv[1].astype(BF16)

    def col(seg):
        return pl.BlockSpec((None, T, LANES), lambda p, b, seg=seg: (b, 0, base + seg * NA_PAIRS + p))

    return _grid_call(
        body, name="na_bwd", grid=(NA_PAIRS, B),
        out_shape=(jax.ShapeDtypeStruct((B, 3, T, NA_WIDTH), BF16),
                   jax.ShapeDtypeStruct((NA_HEADS, 2 * NA_KH - 2, GRID_W, LANES), F32)),
        in_specs=[col(0), col(1), col(2),
                  pl.BlockSpec((None, None, R, 2 * GRID_W, nk + C), lambda p, b: (b, p, 0, 0, 0)),
                  pl.BlockSpec((None, N, LANES), lambda p, b: (b, 0, p))],
        out_specs=(pl.BlockSpec((None, 3, T, LANES), lambda p, b: (b, 0, 0, p)),
                   pl.BlockSpec((2, 2 * NA_KH - 2, GRID_W, LANES), lambda p, b: (p, 0, 0, 0))),
        scratch_shapes=[pltpu.VMEM((2, T, LANES), F32)],
        args=(proj, proj, proj, probs, dlat))


def _split3(a):
    hi = a.astype(BF16)
    r1 = a - hi.astype(F32)
    mid = r1.astype(BF16)
    lo = (r1 - mid.astype(F32)).astype(BF16)
    return hi, mid, lo


def _rpb_reduce(dbias2, onehot2):
    rows = dbias2.shape[0] * dbias2.shape[1]
    flat = dbias2.reshape(rows, GRID_W * LANES)

    def body(a_ref, oh_ref, o_ref):
        hi, mid, lo = _split3(a_ref[...])
        oh = oh_ref[...]
        o_ref[...] = _dot(hi, oh) + _dot(mid, oh) + _dot(lo, oh)

    return pl.pallas_call(
        body, name="rpb_reduce", out_shape=jax.ShapeDtypeStruct((rows, LANES), F32),
        in_specs=[_vmem(), _vmem()], out_specs=_vmem(),
        compiler_params=pltpu.CompilerParams(vmem_limit_bytes=VMEM_LIMIT),
    )(flat, onehot2)


def _dense_core(lat_ret, lat_na, x, tgt, modl, g_post_mix, g_pre_mlp, g_post_mlp, w_out, w1, w2):
    B, N, D = x.shape
    F = w1.shape[1]
    wout_rows, w1_cols, w2_rows = w_out.shape[0] // N_DEV, w1.shape[1] // N_DEV, w2.shape[0] // N_DEV
    mixw = w_out.shape[0]
    half = mixw // 2
    tm = _div_tile(N, 256, 16)
    nt = N // tm
    fc = _div_tile(F, 1024, LANES)

    def body(lr_ref, ln_ref, x_ref, t_ref, gt1_ref, sh2_ref, sc2_ref, gt2_ref, gpm_ref, gpre_ref, gpo_ref,
             wout_part, w1_part, w2_part,
             dy1_ref, dlr_ref, dln_ref, dmix_ref, h2_ref, a_ref, du_ref, dz_ref, red_ref, wout_hbm, w1_hbm, w2_hbm,
             wout_v, w1_v, w2_v, u_s, sems, fsend, frecv):
        @pl.when((pl.program_id(0) == 0) & (pl.program_id(1) == 0))
        def _():
            relay = [(_row_block(wout_part, wout_rows), _row_block(wout_hbm, wout_rows)),
                     (_col_block(w1_part, w1_cols), _col_block(w1_hbm, w1_cols)),
                     (_row_block(w2_part, w2_rows), _row_block(w2_hbm, w2_rows))]
            _forward_start(relay, fsend, frecv)
            _forward_wait(relay, fsend, frecv)
            cps = [pltpu.make_async_copy(wout_hbm, wout_v, sems.at[0]),
                   pltpu.make_async_copy(w1_hbm, w1_v, sems.at[1]),
                   pltpu.make_async_copy(w2_hbm, w2_v, sems.at[2])]
            for cp in cps:
                cp.start()
            for cp in cps:
                cp.wait()

        @pl.when(pl.program_id(1) == 0)
        def _():
            red_ref[...] = jnp.zeros(red_ref.shape, F32)

        gt1 = gt1_ref[...]
        sh2 = sh2_ref[...]
        sc2 = sc2_ref[...]
        gt2 = gt2_ref[...]
        gpm = gpm_ref[...]
        gpre = gpre_ref[...]
        gpo = gpo_ref[...]

        def rowmean(a):
            return jnp.mean(a, axis=-1, keepdims=True)

        def colsum(a):
            return jnp.sum(a, axis=0, keepdims=True)

        mix_gain = gt1 * gpm
        mlp_in_gain = gpre * (1.0 + sc2)
        mlp_out_gain = gt2 * gpo
        mix = _dot(lr_ref[...], wout_v[0:half, :]) + _dot(ln_ref[...], wout_v[half:, :])
        x = x_ref[...]
        rm = lax.rsqrt(rowmean(mix * mix) + NORM_EPS)
        mh = mix * rm
        y1 = x + mh * mix_gain
        r1 = lax.rsqrt(rowmean(y1 * y1) + NORM_EPS)
        xh = y1 * r1
        h2b = (xh * mlp_in_gain + sh2).astype(BF16)
        h2_ref[...] = h2b
        z = jnp.zeros((tm, D), F32)
        for c0 in range(0, F, fc):
            u = _dot(h2b, w1_v[:, c0:c0 + fc])
            u_s[:, c0:c0 + fc] = u
            ru = jnp.maximum(u, 0.0)
            ab = (ru * ru).astype(BF16)
            a_ref[:, c0:c0 + fc] = ab
            z = z + _dot(ab, w2_v[c0:c0 + fc, :])
        r2 = lax.rsqrt(rowmean(z * z) + NORM_EPS)
        zh = z * r2
        y2 = y1 + zh * mlp_out_gain
        err = y2 - t_ref[...]
        loss = 0.5 * jnp.sum(rowmean(err * err))
        dy2 = err * (1.0 / D)
        s_out = colsum(dy2 * zh)
        red_ref[2:3, :] += s_out * gpo
        red_ref[6:7, :] += s_out * gt2
        dzh = dy2 * mlp_out_gain
        dz = r2 * (dzh - zh * rowmean(dzh * zh))
        dzb = dz.astype(BF16)
        dz_ref[...] = dzb
        dh2 = jnp.zeros((tm, D), F32)
        for c0 in range(0, F, fc):
            da = _dot_nt(dzb, w2_v[c0:c0 + fc, :])
            dub = (da * (2.0 * jnp.maximum(u_s[:, c0:c0 + fc], 0.0))).astype(BF16)
            du_ref[:, c0:c0 + fc] = dub
            dh2 = dh2 + _dot_nt(dub, w1_v[:, c0:c0 + fc])
        s_in = colsum(dh2 * xh)
        red_ref[3:4, :] += s_in * gpre
        red_ref[4:5, :] += colsum(dh2)
        red_ref[5:6, :] += s_in * (1.0 + sc2)
        dxh = dh2 * mlp_in_gain
        dy1 = dy2 + r1 * (dxh - xh * rowmean(dxh * xh))
        dy1_ref[...] = dy1
        s_mix = colsum(dy1 * mh)
        red_ref[0:1, :] += s_mix * gpm
        red_ref[1:2, :] += s_mix * gt1
        dmh = dy1 * mix_gain
        dmix = (rm *(dmh - mh * rowmean(dmh * mh))).astype(BF16)
        dmix_ref[...] = dmix
        dlr_ref[...] = _dot_nt(dmix, wout_v[0:half, :])
        dln_ref[...] = _dot_nt(dmix, wout_v[half:, :])
        red_ref[7:8, :] += jnp.zeros((1, D), F32) + loss

    def tok(w):
        return pl.BlockSpec((None, tm, w), lambda b, t: (b, t, 0))

    def mod(k):
        return pl.BlockSpec((None, None, 1, D), lambda b, t, k=k: (b, k, 0, 0))

    def vec():
        return pl.BlockSpec((1, D), lambda b, t: (0, 0))

    return pl.pallas_call(
        body, name="dense_core", grid=(B, nt),
        out_shape=(jax.ShapeDtypeStruct((B, N, D), F32), jax.ShapeDtypeStruct((B, N, half), F32),
                   jax.ShapeDtypeStruct((B, N, half), F32), jax.ShapeDtypeStruct((B, N, D), BF16),
                   jax.ShapeDtypeStruct((B, N, D), BF16), jax.ShapeDtypeStruct((B, N, F), BF16),
                   jax.ShapeDtypeStruct((B, N, F), BF16), jax.ShapeDtypeStruct((B, N, D), BF16),
                   jax.ShapeDtypeStruct((B, SUBLANES, D), F32),
                   jax.ShapeDtypeStruct(w_out.shape, w_out.dtype), jax.ShapeDtypeStruct(w1.shape, w1.dtype),
                   jax.ShapeDtypeStruct(w2.shape, w2.dtype)),
        in_specs=[tok(half), tok(half), tok(D), tok(D), mod(2), mod(3), mod(4), mod(5), vec(), vec(), vec(),
                  _any(), _any(), _any()],
        out_specs=(tok(D), tok(half), tok(half), tok(D), tok(D), tok(F), tok(F), tok(D),
                   pl.BlockSpec((None, SUBLANES, D), lambda b, t: (b, 0, 0)), _any(), _any(), _any()),
        scratch_shapes=[pltpu.VMEM((mixw, D), BF16), pltpu.VMEM((D, F), BF16), pltpu.VMEM((F, D), BF16),
                        pltpu.VMEM((tm, F), F32), pltpu.SemaphoreType.DMA((3,)),
                        pltpu.SemaphoreType.DMA((3, 3)), pltpu.SemaphoreType.DMA((3, 3))],
        input_output_aliases={11: 9, 12: 10, 13: 11},
        compiler_params=_params("arbitrary", "arbitrary"),
    )(lat_ret, lat_na, x, tgt, modl, modl, modl, modl, g_post_mix, g_pre_mlp, g_post_mlp, w_out, w1, w2)[:9]


def _inproj_bwd(dret, dna, x, ctx, dy1, modl, g1, w_in_t):
    B, N, D = x.shape
    n_ctx = ctx.shape[1]
    T = n_ctx + N
    tm = _div_tile(n_ctx, 256, 16)
    nct, ctx_spec, lat_spec = _token_tiles(n_ctx, tm)
    nt = T // tm
    nseg_r = dret.shape[1]
    nseg_n = dna.shape[1]
    nw = w_in_t.shape[0]

    def body(*refs):
        seg_refs = refs[:nseg_r + nseg_n]
        c_ref, x_ref, dy1_ref, sc_ref, g_ref, w_ref, dx_ref, red_ref = refs[nseg_r + nseg_n:]
        t = pl.program_id(1)
        dh = jnp.zeros((tm, D), F32)
        for s, ref in enumerate(seg_refs):
            dh = dh + _dot(ref[...], w_ref[s * SEG:(s + 1) * SEG, :])
        x = jnp.where(t < nct, c_ref[...], x_ref[...])
        g = g_ref[...]
        r = lax.rsqrt(jnp.mean(x * x, axis=-1, keepdims=True) + NORM_EPS)
        xh = x * r
        gain = 1.0 + sc_ref[...]
        s_in = jnp.sum(dh * xh, axis=0, keepdims=True)
        red_ref[0:1, :] = jnp.sum(dh, axis=0, keepdims=True)
        red_ref[1:2, :] = s_in * g
        red_ref[2:3, :] = s_in * gain
        red_ref[3:, :] = jnp.zeros((SUBLANES - 3, D), F32)
        dxh = dh * (g * gain)
        dx = r * (dxh - xh * jnp.mean(dxh * xh, axis=-1, keepdims=True))
        dx_ref[...] = dx + jnp.where(t >= nct, dy1_ref[...], 0.0)

    def mrow(b, t):
        return jnp.where(t < nct, B, b)

    def seg(s):
        return pl.BlockSpec((None, None, tm, SEG), lambda b, t, s=s: (b, s, t, 0))

    return _grid_call(
        body, name="inproj_bwd", grid=(B, nt),
        out_shape=(jax.ShapeDtypeStruct((B, N, D), F32), jax.ShapeDtypeStruct((B, nt, SUBLANES, D), F32)),
        in_specs=[seg(s) for s in range(nseg_r)] + [seg(s) for s in range(nseg_n)]
                 + [ctx_spec(D), lat_spec(D), lat_spec(D),
                    pl.BlockSpec((None, None, 1, D), lambda b, t: (mrow(b, t), 1, 0, 0)),
                    pl.BlockSpec((1, D), lambda b, t: (0, 0)),
                    pl.BlockSpec((nw, D), lambda b, t: (0, 0))],
        out_specs=(lat_spec(D), pl.BlockSpec((None, None, SUBLANES, D), lambda b, t: (b, t, 0, 0))),
        scratch_shapes=[], args=(*([dret] * nseg_r), *([dna] * nseg_n), ctx, x, dy1, modl, g1, w_in_t))


def _tn_matmul(lhs, rhs, name, rows_before=0, rows_after=0, into=None):
    B, S, T, W = lhs.shape
    nn = rhs.shape[-1]
    tk = _div_tile(T, 2304, LANES)
    bm = _div_tile(W, 1024, LANES)
    bn = _div_tile(nn, 1024, LANES)
    nkt = T // tk
    nk = B * nkt

    def body(l_ref, r_ref, *rest):
        o_ref, acc = rest[-2:]
        k = pl.program_id(3)

        @pl.when(k == 0)
        def _():
            acc[...] = jnp.zeros(acc.shape, F32)

        acc[...] += _dot_tn(l_ref[...].astype(BF16), r_ref[...].astype(BF16))

        @pl.when(k == nk - 1)
        def _():
            o_ref[...] = acc[...].astype(BF16)

    nwb = W // bm
    first = rows_before // bm
    return pl.pallas_call(
        functools.partial(body), name=name, grid=(S, nwb, nn // bn, nk),
        out_shape=jax.ShapeDtypeStruct((rows_before + S * W + rows_after, nn), BF16),
        in_specs=[pl.BlockSpec((None, None, tk, bm), lambda s, i, j, k: (k // nkt, s, k % nkt, i)),
                  pl.BlockSpec((None, tk, bn), lambda s, i, j, k: (k // nkt, k % nkt, j))]
                 + ([] if into is None else [_any()]),
        out_specs=pl.BlockSpec((bm, bn), lambda s, i, j, k: (first + s * nwb + i, j)),
        scratch_shapes=[pltpu.VMEM((bm, bn), F32)],
        input_output_aliases={} if into is None else {2: 0},
        compiler_params=_params("parallel", "parallel", "parallel", "arbitrary"),
    )(lhs, rhs, *([] if into is None else [into]))


class _SplitScatter:
    def __init__(self, gs, block_ofs, land_shapes, name, kind="scatter", masks=ALL_PEERS):
        self.n = n = len(gs)
        self.block_ofs, self.kind, self.masks = block_ofs, kind, masks
        if kind == "scatter":
            land_shapes = [(N_DEV,) + tuple(bs) for bs in land_shapes]
        hbm = pl.BlockSpec(memory_space=pltpu.HBM)
        sem = pl.BlockSpec(memory_space=pltpu.SEMAPHORE)

        def body(*refs):
            g_refs, land_refs = refs[:n], refs[n:2 * n]
            send_sems, recv_sems, own_sems = refs[2 * n:2 * n + 3]
            token = refs[-1]
            for own, pushes in self._copies(g_refs, land_refs, send_sems, recv_sems, own_sems, landing="sender"):
                own.start()
                for cp in pushes:
                    cp.start()
            token[...] = jnp.zeros_like(token)

        outs = pl.pallas_call(
            body, name=name,
            out_shape=(pltpu.SemaphoreType.DMA((n * (N_DEV - 1),)), pltpu.SemaphoreType.DMA((n * (N_DEV - 1),)),
                       pltpu.SemaphoreType.DMA((n,)))
                      + tuple(pltpu.HBM(g.shape, g.dtype) for g in gs)
                      + tuple(pltpu.HBM(s, g.dtype) for s, g in zip(land_shapes, gs))
                      + (jax.ShapeDtypeStruct((SUBLANES, LANES), F32),),
            in_specs=(hbm,) * (2 * n), out_specs=(sem,) * 3 + (hbm,) * (2 * n) + (_vmem(),),
            input_output_aliases={k: 3 + k for k in range(2 * n)},
            compiler_params=pltpu.CompilerParams(has_side_effects=pltpu.SideEffectType.DATAFLOW_SIDE_EFFECTING),
        )(*[pltpu.with_memory_space_constraint(g, pltpu.HBM) for g in gs],
          *[pltpu.with_memory_space_constraint(lax.empty(s, g.dtype), pltpu.HBM) for s, g in zip(land_shapes, gs)])
        self.sems, self.thru, self.token = outs[:3], outs[3:3 + 2 * n], outs[-1]

    def _copies(self, g_refs, land_refs, send_sems, recv_sems, own_sems, landing):
        me, peers = _me_and_peers()
        out = []
        for k in range(self.n):
            if self.kind == "scatter":
                src, dst = self.block_ofs[k](g_refs[k]), _slot(land_refs[k])
            else:
                src, dst = (lambda p, k=k: g_refs[k]), self.block_ofs[k](land_refs[k])
            own = pltpu.make_async_copy(src(me), dst(me), own_sems.at[k]) if landing == "sender" else None
            pushes = []
            for m in self.masks:
                dev, pid = peers[m - 1]
                i = k * (N_DEV - 1) + m - 1
                pushes.append(_remote(src(pid), dst(me if landing == "sender" else pid),
                                      send_sems.at[i], recv_sems.at[i], dev))
            out.append((own, pushes))
        return out


def _scatter_wait(scatters, after, name):
    hbm = pl.BlockSpec(memory_space=pltpu.HBM)
    sem = pl.BlockSpec(memory_space=pltpu.SEMAPHORE)
    n_arr = [2 * sc.n for sc in scatters]
    total = sum(n_arr)

    def body(*refs):
        arrs, sems = refs[:total], refs[total:total + 3 * len(scatters)]
        a0 = 0
        for j, sc in enumerate(scatters):
            g_refs, land_refs = arrs[a0:a0 + sc.n], arrs[a0 + sc.n:a0 + 2 * sc.n]
            a0 += 2 * sc.n
            send_sems, recv_sems, own_sems = sems[3 * j:3 * j + 3]
            for (own, sent), (_, got) in zip(sc._copies(g_refs, land_refs, send_sems, recv_sems, own_sems, "sender"),
                                             sc._copies(g_refs, land_refs, send_sems, recv_sems, own_sems, "receiver")):
                own.wait()
                for cp in sent:
                    cp.wait_send()
                for cp in got:
                    cp.wait_recv()

    operands = [a for sc in scatters for a in sc.thru]
    outs = pl.pallas_call(
        body, name=name,
        out_shape=tuple(pltpu.HBM(a.shape, a.dtype) for a in operands),
        in_specs=(hbm,) * total + (sem,) * (3 * len(scatters)) + (pl.BlockSpec(memory_space=pl.ANY),),
        out_specs=(hbm,) * total, input_output_aliases={k: k for k in range(total)},
        compiler_params=pltpu.CompilerParams(has_side_effects=pltpu.SideEffectType.DATAFLOW_SIDE_EFFECTING),
    )(*operands, *[s for sc in scatters for s in sc.sems], after)
    lands, a0 = [], 0
    for sc in scatters:
        lands.extend(outs[a0 + sc.n:a0 + 2 * sc.n])
        a0 += 2 * sc.n
    return lands


def _small_ar(mbuf, silu_all, w_ada, c_ctx, n_mod_rows, n_vec_rows):
    D = silu_all.shape[1]
    ncol = w_ada.shape[1]
    nm = mbuf.shape[2]
    srows = silu_all.shape[0]

    def body(mbuf, s_ref, w_ref, cc_ref, tot_ref, gb_ref, gw_ref, gc_ref, tbuf, dmx, cmrow, send3, recv3):
        me, _ = _me_and_peers()
        msum = mbuf[0]
        for k in range(1, N_DEV):
            msum = msum + mbuf[k]
        tot_ref[...] = msum[n_mod_rows:n_mod_rows + n_vec_rows]
        gb_ref[...] = jnp.sum(msum[0:n_mod_rows], axis=0, keepdims=True)
        loc = pl.ds(pl.multiple_of(me * ncol, ncol), ncol)
        for k in range(N_DEV):
            dmx[k * SUBLANES:(k + 1) * SUBLANES, :] = mbuf[k, :, loc]
        cmrow[...] = msum
        cm_loc = cmrow[n_mod_rows - 1:n_mod_rows, loc]
        dmx[N_DEV * SUBLANES:, :] = jnp.concatenate([cm_loc, jnp.zeros((SUBLANES - 1, ncol), F32)], axis=0)
        gw_ref[...] = _dot_tn(s_ref[...], dmx[...])
        tbuf[me] = _dot_nt(dmx[N_DEV * SUBLANES:, :], w_ref[...])
        _exchange(lambda p: tbuf.at[me], lambda p: tbuf.at[p], send3, recv3)
        tsum = tbuf[0]
        for k in range(1, N_DEV):
            tsum = tsum + tbuf[k]
        cc = cc_ref[...]
        sg = _sigmoid(cc)
        gc_ref[...] = tsum[0:1, :] * (sg * (1.0 + cc * (1.0 - sg)))

    return pl.pallas_call(
        body, name="small_ar",
        out_shape=(jax.ShapeDtypeStruct((n_vec_rows, nm), F32), jax.ShapeDtypeStruct((1, nm), F32),
                   jax.ShapeDtypeStruct((D, ncol), F32), jax.ShapeDtypeStruct((1, D), F32)),
        in_specs=[_vmem()] * 4, out_specs=(_vmem(),) * 4,
        scratch_shapes=[pltpu.VMEM((N_DEV, SUBLANES, D), F32), pltpu.VMEM((srows, ncol), F32),
                        pltpu.VMEM((SUBLANES, nm), F32)] + [pltpu.SemaphoreType.DMA((N_DEV - 1,))] * 2,
        compiler_params=pltpu.CompilerParams(vmem_limit_bytes=VMEM_LIMIT),
    )(mbuf, silu_all, w_ada, c_ctx.reshape(1, D))


def _adam_update(w, g, m, v):
    mn = ADAM_B1 * m + (1.0 - ADAM_B1) * g
    vn = ADAM_B2 * v + (1.0 - ADAM_B2) * (g * g)
    m_hat = mn / (1.0 - ADAM_B1 ** ADAM_STEP)
    v_hat = vn / (1.0 - ADAM_B2 ** ADAM_STEP)
    return -ADAM_LR * (m_hat / (jnp.sqrt(v_hat) + ADAM_EPS) + ADAM_WD * w), mn, vn


def _adamw(w, g, m, v, name):
    rows, cols = w.shape
    tr = _div_tile(rows, 512, SUBLANES)

    def body(w_ref, g_ref, m_ref, v_ref, d_ref, nm_ref, nv_ref):
        d_ref[...], nm_ref[...], nv_ref[...] = _adam_update(w_ref[...], g_ref[...], m_ref[...], v_ref[...])

    spec = pl.BlockSpec((tr, cols), lambda i: (i, 0))
    return pl.pallas_call(
        functools.partial(body), name=name, grid=(rows // tr,),
        out_shape=(jax.ShapeDtypeStruct((rows, cols), F32),) * 3,
        in_specs=[spec] * 4, out_specs=(spec,) * 3,
        compiler_params=_params("parallel"),
    )(w, g, m, v)


def _adamw_small(items, name):
    n = len(items)

    def body(*refs):
        ins, outs = refs[:4 * n], refs[4 * n:]
        for i in range(n):
            w_ref, g_ref, m_ref, v_ref = ins[4 * i:4 * i + 4]
            outs[3 * i][...], outs[3 * i + 1][...], outs[3 * i + 2][...] = _adam_update(
                w_ref[...], g_ref[...], m_ref[...], v_ref[...])

    outs = pl.pallas_call(
        body, name=name,
        out_shape=tuple(jax.ShapeDtypeStruct(it[0].shape, F32) for it in items for _ in range(3)),
        in_specs=[_vmem()] * (4 * n), out_specs=(_vmem(),) * (3 * n),
        compiler_params=pltpu.CompilerParams(vmem_limit_bytes=VMEM_LIMIT),
    )(*[a for it in items for a in it])
    return [tuple(outs[3 * i:3 * i + 3]) for i in range(n)]


def _sum_adamw(buf, w, m, v, name):
    _, rows, cols = buf.shape
    tr = _div_tile(rows, 256, 2 * SUBLANES)

    def body(b_ref, w_ref, m_ref, v_ref, g_ref, d_ref, nm_ref, nv_ref):
        g = b_ref[0].astype(F32)
        for k in range(1, N_DEV):
            g = g + b_ref[k].astype(F32)
        g_ref[...] = g
        d_ref[...], nm_ref[...], nv_ref[...] = _adam_update(w_ref[...], g, m_ref[...], v_ref[...])

    spec = pl.BlockSpec((tr, cols), lambda i: (i, 0))
    return pl.pallas_call(
        functools.partial(body), name=name, grid=(rows // tr,),
        out_shape=(jax.ShapeDtypeStruct((rows, cols), F32),) * 4,
        in_specs=[pl.BlockSpec((N_DEV, tr, cols), lambda i: (0, i, 0))] + [spec] * 3, out_specs=(spec,) * 4,
        compiler_params=_params("parallel"),
    )(buf, w, m, v)


def _rope_tables(n_ctx, n):
    n_freq = RET_DIM // 4
    inv = np.float32(ROPE_BASE) ** (-np.arange(n_freq, dtype=np.float32) / np.float32(n_freq))
    tok = np.arange(n)
    pos_r = (tok // GRID_W).astype(np.float32)
    pos_c = (tok % GRID_W).astype(np.float32)
    ang_r = (pos_r[:, None] * inv[None, :]).astype(np.float32)
    ang_c = (pos_c[:, None] * inv[None, :]).astype(np.float32)
    cos = np.concatenate([np.cos(ang_r), np.cos(ang_r), np.cos(ang_c), np.cos(ang_c)], axis=-1)
    sin = np.concatenate([-np.sin(ang_r), np.sin(ang_r), -np.sin(ang_c), np.sin(ang_c)], axis=-1)
    cos = np.concatenate([np.ones((n_ctx, RET_DIM), np.float32), cos], axis=0)
    sin = np.concatenate([np.zeros((n_ctx, RET_DIM), np.float32), sin], axis=0)
    return jnp.asarray(cos, F32), jnp.asarray(sin, F32)


def _na_tables():
    q = np.arange(GRID_W)[:, None]
    k = np.arange(GRID_W)[None, :]
    start = np.clip(q - NA_KW // 2, 0, GRID_W - NA_KW)
    valid = (k >= start) & (k < start + NA_KW)
    dc = np.clip(k - q + (NA_KW - 1), 0, 2 * NA_KW - 2)
    ncls = 2 * NA_KW - 1
    onehot = (dc[None] == np.arange(ncls)[:, None, None]) & valid[None]
    oh2 = np.zeros((GRID_W, LANES, LANES), np.float32)
    for c in range(ncls):
        oh2[:, :GRID_W, c] = onehot[c]
        oh2[:, GRID_W:, 32 + c] = onehot[c]
    return onehot.astype(np.float32), valid, oh2.reshape(GRID_W * LANES, LANES)


def _paired_bias(rpb, onehot, valid):
    ncls = onehot.shape[0]
    pair = np.zeros((2 * ncls, GRID_W, LANES), np.float32)
    pair[:ncls, :, :GRID_W] = onehot
    pair[ncls:, :, GRID_W:] = onehot
    rows = jnp.concatenate([rpb[:, :-1], rpb[:, 1:]], axis=-1)
    t = jnp.einsum("hdc,cqk->hdqk", rows, jnp.asarray(pair), precision=lax.Precision.HIGHEST)
    return jnp.where(jnp.asarray(np.tile(valid, (1, 2)))[None, None], t, NEG_INF)


def kernel(x, c, ctx, c_ctx, w_ada, b_ada, g_pre_mix, g_post_mix, g_pre_mlp, g_post_mlp, w_in, ret_decay, ret_gn, na_rpb, w_out, w_mlp1, w_mlp2, loss_target, m_c_ctx, m_w_ada, m_b_ada, m_g_pre_mix, m_g_post_mix, m_g_pre_mlp, m_g_post_mlp, m_w_in, m_ret_decay, m_ret_gn, m_na_rpb, m_w_out, m_w_mlp1, m_w_mlp2, v_c_ctx, v_w_ada, v_b_ada, v_g_pre_mix, v_g_post_mix, v_g_pre_mlp, v_g_post_mlp, v_w_in, v_ret_decay, v_ret_gn, v_na_rpb, v_w_out, v_w_mlp1, v_w_mlp2):
    B, N, D = x.shape
    C = ctx.shape[1]
    T = C + N

    silu_all, mods_g, win_b, wout_l, w1_l, w2_l = _mod_gather(c, c_ctx, w_ada[0], b_ada, w_in[0].T, w_out[0],
                                                             w_mlp1[0], w_mlp2[0])
    mods_mine = mods_g.transpose(1, 0, 2).reshape(mods_g.shape[1], N_MOD * D)
    modl = jnp.concatenate([mods_mine[:B], mods_mine[SUBLANES:SUBLANES + 1]], axis=0)
    modl = modl.reshape(B + 1, N_MOD, 1, D)
    rin = w_in.shape[2]
    rout, c1, r2 = wout_l.shape[0], w1_l.shape[1], w2_l.shape[0]

    def rows_of(n):
        return lambda ref: _row_block(ref, n)

    def cols_of(n):
        return lambda ref: _col_block(ref, n)

    cos, sin = _rope_tables(C, N)
    onehot, valid, oh2 = _na_tables()
    bias2 = _paired_bias(na_rpb[0], onehot, valid)
    lg = jax.nn.log_sigmoid(ret_decay[0].astype(F32))

    ag = _SplitScatter([wout_l, w1_l, w2_l], [rows_of(rout), cols_of(c1), rows_of(r2)],
                       [(N_DEV * rout, D), (D, N_DEV * c1), (N_DEV * r2, D)], "ag_mlp_start",
                       kind="gather", masks=SIBLING + ICI_SAME_CORE)
    h_all, proj = _inproj_fwd(x, ctx, modl, g_pre_mix + ag.token[0, 0], win_b)
    o_ret, lat_ret, q_rot, k_rot = _ret_fwd(proj, cos, sin, lg, ret_gn, C)
    lat_na, na_probs = _na_fwd(proj, bias2, C)
    wout_part, w1_part, w2_part = _scatter_wait([ag], lat_na, "ag_mlp_wait")

    (dy1, dlat_ret, dlat_na, dmix, h2, act, du, dz, red_d) = _dense_core(
        lat_ret, lat_na, x, loss_target, modl, g_post_mix, g_pre_mlp, g_post_mlp, wout_part, w1_part, w2_part)

    gw_out_p = _tn_matmul(lat_ret[:, None], dmix, "gw_out_ret", rows_after=lat_na.shape[-1])
    gw_out_p = _tn_matmul(lat_na[:, None], dmix, "gw_out_na", rows_before=lat_ret.shape[-1], into=gw_out_p)
    gw1_p = _tn_matmul(h2[:, None], du, "gw_mlp1")
    gw2_p = _tn_matmul(act[:, None], dz, "gw_mlp2")
    rs_mlp = _SplitScatter([gw_out_p, gw1_p, gw2_p], [rows_of(rout), cols_of(c1), rows_of(r2)],
                           [(rout, D), (D, c1), (r2, D)], "rs_mlp_start")

    dret, dgn_p, dlg_p = _ret_bwd(proj, q_rot, k_rot, cos, sin, lg, ret_gn + rs_mlp.token[0, 0], o_ret, dlat_ret, C)
    dna, dbias2 = _na_bwd(proj, na_probs, dlat_na, C)
    ret_cols, na_cols = dret.shape[1] * dret.shape[3], dna.shape[1] * dna.shape[3]
    gwin_t_p = _tn_matmul(dret, h_all, "gw_in_ret", rows_after=na_cols)
    gwin_t_p = _tn_matmul(dna, h_all, "gw_in_na", rows_before=ret_cols, into=gwin_t_p)
    rs_in = _SplitScatter([gwin_t_p], [rows_of(rin)], [(rin, D)], "rs_w_in_start")
    grad_x, red_i = _inproj_bwd(dret, dna, x, ctx, dy1, modl, g_pre_mix + rs_in.token[0, 0], win_b)

    rd = red_d
    nct = red_i.shape[1] * C // T
    ri_ctx = red_i[:, :nct].sum(axis=(0, 1))
    ri_lat = red_i[:, nct:].sum(axis=1)
    d_mods = jnp.concatenate([ri_lat[:, 0], ri_lat[:, 1], rd[:, 0], rd[:, 4], rd[:, 3], rd[:, 2]], axis=-1)
    d_cmods = jnp.concatenate([ri_ctx[0], ri_ctx[1], jnp.zeros(((N_MOD - 2) * D,), F32)])[None]
    dg_pre_mix = ri_lat[:, 2].sum(axis=0) + ri_ctx[2]
    dg_post_mix = rd[:, 1].sum(axis=0)
    dg_pre_mlp = rd[:, 5].sum(axis=0)
    dg_post_mlp = rd[:, 6].sum(axis=0)
    loss_p = rd[:, 7, 0].sum()
    d_gn = dgn_p[:, 0].sum(axis=0)
    d_lg = dlg_p[:, :, :2, 0].sum(axis=0).T
    d_decay = d_lg * jax.nn.sigmoid(-ret_decay[0].astype(F32))
    rr = _rpb_reduce(dbias2, jnp.asarray(oh2, BF16)).reshape(NA_HEADS, 2 * NA_KH - 2, LANES)
    ncls = 2 * NA_KW - 1
    d_rpb = (jnp.pad(rr[:, :, :ncls], ((0, 0), (0, 1), (0, 0))) + jnp.pad(rr[:, :, 32:32 + ncls], ((0, 0), (1, 0), (0, 0))))
    d_rpb32 = jnp.pad(d_rpb, ((0, 0), (0, 0), (0, 32 - ncls)))
    pieces = [dg_pre_mix, dg_post_mix, dg_pre_mlp, dg_post_mlp, d_gn, d_rpb32.reshape(-1),
              jnp.pad(d_decay.reshape(-1), (0, LANES - d_decay.size)), jnp.full((LANES,), loss_p, F32)]
    vec = jnp.concatenate(pieces)
    nm = N_MOD * D
    n_vec_rows = -(-vec.shape[0] // nm)
    assert B + 1 + n_vec_rows <= SUBLANES
    vec = jnp.pad(vec, (0, n_vec_rows * nm - vec.shape[0])).reshape(n_vec_rows, nm)
    dm_slot = jnp.concatenate([d_mods, d_cmods, vec, jnp.zeros((SUBLANES - B - 1 - n_vec_rows, nm), F32)], axis=0)
    def whole(ref):
        return lambda p: ref

    small = _SplitScatter([dm_slot], [whole], [dm_slot.shape], "small_start")
    land_out, land_1, land_2, land_in = _scatter_wait([rs_mlp, rs_in], small.token, "rs_wait")
    fused = {"w_in": [a.T for a in _sum_adamw(land_in, w_in[0].T, m_w_in[0].T, v_w_in[0].T, "sum_adamw_w_in")],
             "w_out": _sum_adamw(land_out, w_out[0], m_w_out[0], v_w_out[0], "sum_adamw_w_out"),
             "w_mlp1": _sum_adamw(land_1, w_mlp1[0], m_w_mlp1[0], v_w_mlp1[0], "sum_adamw_w_mlp1"),
             "w_mlp2": _sum_adamw(land_2, w_mlp2[0], m_w_mlp2[0], v_w_mlp2[0], "sum_adamw_w_mlp2")}
    (mbuf,) = _scatter_wait([small], fused["w_mlp2"][0], "small_wait")
    tot, g_b_ada, g_w_ada, g_c_ctx = _small_ar(mbuf, silu_all, w_ada[0], c_ctx, B + 1, n_vec_rows)
    flat = tot.reshape(-1)
    o0 = 0
    g_pre_mix_g = flat[o0:o0 + D]; o0 += D
    g_post_mix_g = flat[o0:o0 + D]; o0 += D
    g_pre_mlp_g = flat[o0:o0 + D]; o0 += D
    g_post_mlp_g = flat[o0:o0 + D]; o0 += D
    g_gn = flat[o0:o0 + RET_WIDTH]; o0 += RET_WIDTH
    nrpb = NA_HEADS * (2 * NA_KH - 1) * 32
    g_rpb = flat[o0:o0 + nrpb].reshape(NA_HEADS, 2 * NA_KH - 1, 32)[:, :, :ncls]; o0 += nrpb
    g_decay = flat[o0:o0 + 2 * RET_HEADS].reshape(2, RET_HEADS); o0 += LANES
    loss = flat[o0]

    grads = {
        "c_ctx": g_c_ctx.reshape(c_ctx.shape), "w_ada": g_w_ada[None], "b_ada": g_b_ada.reshape(b_ada.shape),
        "g_pre_mix": g_pre_mix_g[None], "g_post_mix": g_post_mix_g[None], "g_pre_mlp": g_pre_mlp_g[None],
        "g_post_mlp": g_post_mlp_g[None], "w_in": fused["w_in"][0][None], "ret_decay": g_decay[None], "ret_gn": g_gn[None],
        "na_rpb": g_rpb[None], "w_out": fused["w_out"][0][None], "w_mlp1": fused["w_mlp1"][0][None],
        "w_mlp2": fused["w_mlp2"][0][None],
    }
    weights = dict(c_ctx=c_ctx, w_ada=w_ada, b_ada=b_ada, g_pre_mix=g_pre_mix, g_post_mix=g_post_mix,
                   g_pre_mlp=g_pre_mlp, g_post_mlp=g_post_mlp, w_in=w_in, ret_decay=ret_decay, ret_gn=ret_gn,
                   na_rpb=na_rpb, w_out=w_out, w_mlp1=w_mlp1, w_mlp2=w_mlp2)
    m_in = dict(c_ctx=m_c_ctx, w_ada=m_w_ada, b_ada=m_b_ada, g_pre_mix=m_g_pre_mix, g_post_mix=m_g_post_mix,
                g_pre_mlp=m_g_pre_mlp, g_post_mlp=m_g_post_mlp, w_in=m_w_in, ret_decay=m_ret_decay,
                ret_gn=m_ret_gn, na_rpb=m_na_rpb, w_out=m_w_out, w_mlp1=m_w_mlp1, w_mlp2=m_w_mlp2)
    v_in = dict(c_ctx=v_c_ctx, w_ada=v_w_ada, b_ada=v_b_ada, g_pre_mix=v_g_pre_mix, g_post_mix=v_g_post_mix,
                g_pre_mlp=v_g_pre_mlp, g_post_mlp=v_g_post_mlp, w_in=v_w_in, ret_decay=v_ret_decay,
                ret_gn=v_ret_gn, na_rpb=v_na_rpb, w_out=v_w_out, w_mlp1=v_w_mlp1, w_mlp2=v_w_mlp2)
    names = list(weights)
    deltas, new_m, new_v = {}, {}, {}
    def as_2d(n):
        shp = weights[n].shape
        two_d = (-1, shp[-1]) if len(shp) > 1 else (1, shp[0])
        return [a.reshape(two_d) for a in (weights[n], grads[n], m_in[n], v_in[n])]

    small = [n for n in names if n not in fused and weights[n].size <= 65536]
    updated = dict(zip(small, _adamw_small([as_2d(n) for n in small], "adamw_small")))
    for n in names:
        if n in fused:
            updated[n] = fused[n][1:]
        elif n not in updated:
            updated[n] = _adamw(*as_2d(n), "adamw_" + n)
        deltas[n], new_m[n], new_v[n] = (a.reshape(weights[n].shape) for a in updated[n])
    return (loss, grad_x, *[grads[n] for n in names], *[deltas[n] for n in names],
            *[new_m[n] for n in names], *[new_v[n] for n in names])
```

```python
import functools
import math

import numpy as np
import jax
import jax.numpy as jnp
from jax import lax
from jax.experimental import pallas as pl
from jax.experimental.pallas import tpu as pltpu

F32 = jnp.float32
BF16 = jnp.bfloat16
MESH = pl.DeviceIdType.MESH

N_DEV = 8
LANES = 128
SUBLANES = 8
VMEM_LIMIT = 60 * 1024 * 1024

GRID_W = 64
RET_HEADS = 4
RET_DIM = 128
RET_WIDTH = RET_HEADS * RET_DIM
NA_HEADS = 8
NA_DIM = 64
NA_WIDTH = NA_HEADS * NA_DIM
NA_PAIRS = NA_HEADS // 2
NA_KH = 8
NA_KW = 16
NA_GROUP = 8
SEG = 512
ROPE_BASE = 10000.0
NORM_EPS = 1e-6
NEG_INF = -1e30
N_MOD = 6

ADAM_LR = 0.001
ADAM_B1 = 0.9
ADAM_B2 = 0.999
ADAM_EPS = 1e-08
ADAM_WD = 0.01
ADAM_STEP = 10


def _dot(a, b):
    return lax.dot_general(a, b, (((1,), (0,)), ((), ())), preferred_element_type=F32)


def _dot_nt(a, b):
    return lax.dot_general(a, b, (((1,), (1,)), ((), ())), preferred_element_type=F32)


def _dot_tn(a, b):
    return lax.dot_general(a, b, (((0,), (0,)), ((), ())), preferred_element_type=F32)


def _sigmoid(x):
    return 1.0 / (1.0 + jnp.exp(-x))


def _div_tile(n, cap, mult):
    if n <= cap:
        return n
    for t in range(cap - cap % mult, 0, -mult):
        if n % t == 0:
            return t
    raise ValueError(f"no tile for {n}")


def _params(*sem):
    return pltpu.CompilerParams(dimension_semantics=tuple(sem) if sem else None,
                                vmem_limit_bytes=VMEM_LIMIT)


def _vmem():
    return pl.BlockSpec(memory_space=pltpu.VMEM)


def _any():
    return pl.BlockSpec(memory_space=pl.ANY)


def _me_and_peers():
    x, y, c = lax.axis_index("x"), lax.axis_index("y"), lax.axis_index("c")
    me = 4 * x + 2 * y + c
    peers = []
    for m in range(1, N_DEV):
        px = 1 - x if (m >> 2) & 1 else x
        py = 1 - y if (m >> 1) & 1 else y
        pc = 1 - c if m & 1 else c
        peers.append(((px, py, pc), 4 * px + 2 * py + pc))
    return me, peers


def _exchange(src_for, dst_from, send_sems, recv_sems):
    me, peers = _me_and_peers()
    sent = []
    for i, (dev, pid) in enumerate(peers):
        cp = pltpu.make_async_remote_copy(src_ref=src_for(pid), dst_ref=dst_from(me),
                                          send_sem=send_sems.at[i], recv_sem=recv_sems.at[i],
                                          device_id=dev, device_id_type=MESH)
        cp.start()
        sent.append(cp)
    for i, (dev, pid) in enumerate(peers):
        pltpu.make_async_remote_copy(src_ref=src_for(pid), dst_ref=dst_from(pid),
                                     send_sem=send_sems.at[i], recv_sem=recv_sems.at[i],
                                     device_id=dev, device_id_type=MESH).wait_recv()
    for cp in sent:
        cp.wait_send()


SIBLING = (1,)
ICI_SAME_CORE = (2, 4, 6)
ALL_PEERS = tuple(range(1, N_DEV))


def _remote(src, dst, send_sem, recv_sem, dev):
    return pltpu.make_async_remote_copy(src_ref=src, dst_ref=dst, send_sem=send_sem, recv_sem=recv_sem,
                                        device_id=dev, device_id_type=MESH)


def _push_start(items, masks, send_sems, recv_sems):
    me, peers = _me_and_peers()
    for k, (src_for, dst_from) in enumerate(items):
        for m in masks:
            dev, pid = peers[m - 1]
            _remote(src_for(pid), dst_from(me), send_sems.at[k, m - 1], recv_sems.at[k, m - 1], dev).start()


def _push_wait_recv(items, masks, send_sems, recv_sems):
    me, peers = _me_and_peers()
    for k, (src_for, dst_from) in enumerate(items):
        for m in masks:
            dev, pid = peers[m - 1]
            _remote(src_for(pid), dst_from(pid), send_sems.at[k, m - 1], recv_sems.at[k, m - 1], dev).wait_recv()


def _push_wait_send(items, masks, send_sems, recv_sems):
    me, peers = _me_and_peers()
    for k, (src_for, dst_from) in enumerate(items):
        for m in masks:
            dev, pid = peers[m - 1]
            _remote(src_for(pid), dst_from(me), send_sems.at[k, m - 1], recv_sems.at[k, m - 1], dev).wait_send()


def _forward_start(items, send_sems, recv_sems):
    me, peers = _me_and_peers()
    sib = peers[0][0]
    for k, (blk_in, blk_out) in enumerate(items):
        for j, m in enumerate(ICI_SAME_CORE):
            pid = peers[m - 1][1]
            _remote(blk_in(pid), blk_out(pid), send_sems.at[k, j], recv_sems.at[k, j], sib).start()


def _forward_wait(items, send_sems, recv_sems):
    me, peers = _me_and_peers()
    sib = peers[0][0]
    for k, (blk_in, blk_out) in enumerate(items):
        for j, m in enumerate(ICI_SAME_CORE):
            got = peers[(m | 1) - 1][1]
            _remote(blk_in(got), blk_out(got), send_sems.at[k, j], recv_sems.at[k, j], sib).wait_recv()
    for k, (blk_in, blk_out) in enumerate(items):
        for j, m in enumerate(ICI_SAME_CORE):
            pid = peers[m - 1][1]
            _remote(blk_in(pid), blk_out(pid), send_sems.at[k, j], recv_sems.at[k, j], sib).wait_send()


def _mod_gather(c, c_ctx, w_ada, b_ada, w_in_t, w_out, w1, w2):
    B, D = c.shape
    ncol = w_ada.shape[1]
    rows = SUBLANES * N_DEV + SUBLANES

    def body(c_ref, cc_ref, w_ref, b_ref, win_ref, wout_ref, w1_ref, w2_ref,
             s_ref, m_ref, gin_ref, wout_b, w1_b, w2_b,
             win_b, msend, send1, recv1, send2, recv2, wsend, wrecv, fsend, frecv, lsem):
        me, _ = _me_and_peers()
        win_b[...] = win_ref[...].astype(BF16)
        block = _row_block(gin_ref, w_in_t.shape[0])
        gather = [(lambda p: win_b, block)]
        own = pltpu.make_async_copy(win_b, block(me), lsem.at[0])
        cv = c_ref[...]
        slot = jnp.concatenate([cv * _sigmoid(cv), jnp.zeros((SUBLANES - B, D), F32)], axis=0)
        my_rows = pl.ds(pl.multiple_of(me * SUBLANES, SUBLANES), SUBLANES)
        s_ref[my_rows, :] = slot
        ccv = cc_ref[...]
        s_ref[SUBLANES * N_DEV:, :] = jnp.concatenate(
            [ccv * _sigmoid(ccv), jnp.zeros((SUBLANES - 1, D), F32)], axis=0)

        def rows_of(p):
            return s_ref.at[pl.ds(pl.multiple_of(p * SUBLANES, SUBLANES), SUBLANES), :]

        _exchange(lambda p: rows_of(me), rows_of, send1, recv1)
        own.start()
        _push_start(gather, SIBLING + ICI_SAME_CORE, wsend, wrecv)
        wout_b[...] = wout_ref[...].astype(BF16)
        w1_b[...] = w1_ref[...].astype(BF16)
        w2_b[...] = w2_ref[...].astype(BF16)
        b_loc = b_ref[:, pl.ds(pl.multiple_of(me * ncol, ncol), ncol)]
        mods = _dot(s_ref[...], w_ref[...]) + b_loc
        for p in range(N_DEV):
            msend[p] = jnp.concatenate([mods[p * SUBLANES:(p + 1) * SUBLANES], mods[N_DEV * SUBLANES:]], axis=0)
        m_ref[me] = msend[me]
        columns = [(lambda p: msend.at[p], lambda p: m_ref.at[p])]
        _push_start(columns, ALL_PEERS, send2, recv2)
        _push_wait_recv(gather, ICI_SAME_CORE, wsend, wrecv)
        relay = [(block, block)]
        _forward_start(relay, fsend, frecv)
        _push_wait_recv(columns, ALL_PEERS, send2, recv2)
        _push_wait_recv(gather, SIBLING, wsend, wrecv)
        _forward_wait(relay, fsend, frecv)
        _push_wait_send(columns, ALL_PEERS, send2, recv2)
        _push_wait_send(gather, SIBLING + ICI_SAME_CORE, wsend, wrecv)
        own.wait()

    return pl.pallas_call(
        body, name="mod_gather",
        out_shape=(jax.ShapeDtypeStruct((rows, D), F32), jax.ShapeDtypeStruct((N_DEV, 2 * SUBLANES, ncol), F32),
                   jax.ShapeDtypeStruct((N_DEV * w_in_t.shape[0], D), BF16),
                   jax.ShapeDtypeStruct(w_out.shape, BF16), jax.ShapeDtypeStruct(w1.shape, BF16),
                   jax.ShapeDtypeStruct(w2.shape, BF16)),
        in_specs=[_vmem()] * 8, out_specs=(_vmem(), _vmem(), _any(), _vmem(), _vmem(), _vmem()),
        scratch_shapes=[pltpu.VMEM(w_in_t.shape, BF16), pltpu.VMEM((N_DEV, 2 * SUBLANES, ncol), F32)]
                       + [pltpu.SemaphoreType.DMA((N_DEV - 1,))] * 2
                       + [pltpu.SemaphoreType.DMA((1, N_DEV - 1))] * 4 + [pltpu.SemaphoreType.DMA((1, 3))] * 2
                       + [pltpu.SemaphoreType.DMA((1,))],
        compiler_params=pltpu.CompilerParams(vmem_limit_bytes=VMEM_LIMIT),
    )(c, c_ctx.reshape(1, D), w_ada, b_ada, w_in_t, w_out, w1, w2)


def _row_block(ref, rows):
    return lambda p: ref.at[pl.ds(pl.multiple_of(p * rows, 2 * SUBLANES), rows), :]


def _col_block(ref, cols):
    return lambda p: ref.at[:, pl.ds(pl.multiple_of(p * cols, LANES), cols)]


def _slot(ref):
    return lambda p: ref.at[p]


def _grid_call(body, *, name, grid, out_shape, in_specs, out_specs, scratch_shapes, args):
    return pl.pallas_call(
        body, name=name, grid=grid, out_shape=tuple(out_shape), in_specs=list(in_specs), out_specs=tuple(out_specs),
        scratch_shapes=list(scratch_shapes), compiler_params=_params(*(("arbitrary",) * len(grid))),
    )(*args)


def _token_tiles(n_ctx, tm):
    nct = n_ctx // tm

    def ctx_spec(D):
        return pl.BlockSpec((None, tm, D), lambda b, t: (b, jnp.minimum(t, nct - 1), 0))

    def lat_spec(D):
        return pl.BlockSpec((None, tm, D), lambda b, t: (b, jnp.maximum(t - nct, 0), 0))

    return nct, ctx_spec, lat_spec


def _inproj_fwd(x, ctx, modl, g1, w_in_t):
    B, N, D = x.shape
    n_ctx = ctx.shape[1]
    T = n_ctx + N
    nw = w_in_t.shape[0]
    tm = _div_tile(n_ctx, 256, 16)
    nct, ctx_spec, lat_spec = _token_tiles(n_ctx, tm)

    def body(c_ref, x_ref, sh_ref, sc_ref, g_ref, w_ref, h_ref, p_ref):
        x = jnp.where(pl.program_id(1) < nct, c_ref[...], x_ref[...])
        r = lax.rsqrt(jnp.mean(x * x, axis=-1, keepdims=True) + NORM_EPS)
        h = ((x * r) * g_ref[...]) * (1.0 + sc_ref[...]) + sh_ref[...]
        hb = h.astype(BF16)
        h_ref[...] = hb
        p_ref[...] = _dot_nt(hb, w_ref[...]).astype(BF16)

    def mrow(b, t):
        return jnp.where(t < nct, B, b)

    return _grid_call(
        body, name="inproj_fwd", grid=(B, T // tm),
        out_shape=(jax.ShapeDtypeStruct((B, T, D), BF16), jax.ShapeDtypeStruct((B, T, nw), BF16)),
        in_specs=[ctx_spec(D), lat_spec(D),
                  pl.BlockSpec((None, None, 1, D), lambda b, t: (mrow(b, t), 0, 0, 0)),
                  pl.BlockSpec((None, None, 1, D), lambda b, t: (mrow(b, t), 1, 0, 0)),
                  pl.BlockSpec((1, D), lambda b, t: (0, 0)),
                  pl.BlockSpec((nw, D), lambda b, t: (0, 0))],
        out_specs=(pl.BlockSpec((None, tm, D), lambda b, t: (b, t, 0)),
                   pl.BlockSpec((None, tm, nw), lambda b, t: (b, t, 0))),
        scratch_shapes=[], args=(ctx, x, modl, modl, g1, w_in_t))


def _swap32(x):
    lane = lax.broadcasted_iota(jnp.int32, x.shape, 1)
    return jnp.where((lane % 64) < 32, pltpu.roll(x, 96, 1), pltpu.roll(x, 32, 1))


def _rope(x, cos, sin):
    return x * cos + _swap32(x) * sin


def _unrope(dy, cos, sin):
    return dy * cos + _swap32(dy * sin)


def _ret_weights(lgf, lgb, dist):
    return jnp.exp(jnp.where(dist >= 0.0, lgf * dist, -lgb * dist))


class _RetDecay:
    def __init__(self, lgf, lgb, rows):
        r = lax.broadcasted_iota(jnp.int32, (rows, RET_DIM), 0).astype(F32)
        self.head = r + 1.0
        self.tail = (rows - 1.0) - r
        self.q_f = jnp.exp(lgf * self.head)
        self.k_f = jnp.exp(lgf * self.tail)
        self.q_b = jnp.exp(lgb * self.tail)
        self.k_b = jnp.exp(lgb * self.head)


def _ret_states(kf32, vs, lgf, lgb, C, c, nt, hf, hb, hfa=None, hba=None):
    dec = _RetDecay(lgf, lgb, c)
    dec_c = _RetDecay(lgf, lgb, C)
    step_f = jnp.exp(jnp.zeros((RET_DIM, RET_DIM), F32) + lgf * c)
    step_b = jnp.exp(jnp.zeros((RET_DIM, RET_DIM), F32) + lgb * c)

    def upd(rows, kdec):
        return _dot_tn((kf32[rows, :] * kdec).astype(BF16), vs[rows, :])

    def lat(t):
        return slice(C + t * c, C + (t + 1) * c)

    state = upd(slice(0, C), dec_c.k_f)
    aged = jnp.zeros_like(state)
    for t in range(nt):
        hf[t] = state.astype(BF16)
        if hfa is not None:
            hfa[t] = aged
        if t < nt - 1:
            aged = step_f * (aged + c * state)
            state = step_f * state + upd(lat(t), dec.k_f)
    state = upd(slice(0, C), dec_c.k_b)
    aged = jnp.zeros_like(state)
    for t in range(nt - 1, -1, -1):
        hb[t] = state.astype(BF16)
        if hba is not None:
            hba[t] = aged
        if t > 0:
            aged = step_b * (aged + c * state)
            state = step_b * state + upd(lat(t), dec.k_b)
    return dec, dec_c, step_f, step_b


def _ret_fwd(proj, cos, sin, lg, gn, n_ctx):
    B, T, _ = proj.shape
    C = n_ctx
    N = T - C
    c = _div_tile(N, 256, 16)
    nt = N // c
    scale = RET_DIM ** -0.5

    def body(lg_ref, q_ref, k_ref, v_ref, g_ref, cos_ref, sin_ref, gn_ref, o_ref, lat_ref, qr_ref, kf32,
             qs, ks, vs, hf, hb):
        h = pl.program_id(1)
        lgf = lg_ref[0, h]
        lgb = lg_ref[1, h]
        for rows in [slice(0, C)] + [slice(C + t * c, C + (t + 1) * c) for t in range(nt)]:
            cosb = cos_ref[rows, :]
            sinb = sin_ref[rows, :]
            qr = _rope(q_ref[rows, :].astype(F32), cosb, sinb) * scale
            qr_ref[rows, :] = qr
            qs[rows, :] = qr.astype(BF16)
            kr = _rope(k_ref[rows, :].astype(F32), cosb, sinb)
            kf32[rows, :] = kr
            ks[rows, :] = kr.astype(BF16)
            vs[rows, :] = v_ref[rows, :].astype(BF16)
        gnv = gn_ref[...]
        dec, _, _, _ = _ret_states(kf32, vs, lgf, lgb, C, c, nt, hf, hb)
        rc = (lax.broadcasted_iota(jnp.int32, (c, c), 0) - lax.broadcasted_iota(jnp.int32, (c, c), 1)).astype(F32)
        w_diag = _ret_weights(lgf, lgb, rc)
        for t in range(nt):
            rows = slice(C + t * c, C + (t + 1) * c)
            qt = qs[rows, :]
            s = _dot_nt(qt, ks[rows, :])
            o = (_dot((s * w_diag).astype(BF16), vs[rows, :])
                 + dec.q_f * _dot(qt, hf[t]) + dec.q_b * _dot(qt, hb[t]))
            o_ref[t * c:(t + 1) * c, :] = o
            mu = jnp.mean(o, axis=-1, keepdims=True)
            oc = o - mu
            var = jnp.mean(oc * oc, axis=-1, keepdims=True)
            yh = oc * lax.rsqrt(var + NORM_EPS)
            g = g_ref[rows, :].astype(F32)
            lat_ref[t * c:(t + 1) * c, :] = ((yh * gnv) * (g * _sigmoid(g))).astype(BF16)

    def col(seg):
        return pl.BlockSpec((None, T, RET_DIM), lambda b, h, seg=seg: (b, 0, seg * RET_HEADS + h))

    return _grid_call(
        body, name="ret_fwd", grid=(B, RET_HEADS),
        out_shape=(jax.ShapeDtypeStruct((B, N, RET_WIDTH), F32), jax.ShapeDtypeStruct((B, N, RET_WIDTH), BF16),
                   jax.ShapeDtypeStruct((B, T, RET_WIDTH), F32), jax.ShapeDtypeStruct((B, T, RET_WIDTH), F32)),
        in_specs=[pl.BlockSpec(memory_space=pltpu.SMEM), col(0), col(1), col(2), col(3),
                  pl.BlockSpec((T, RET_DIM), lambda b, h: (0, 0)), pl.BlockSpec((T, RET_DIM), lambda b, h: (0, 0)),
                  pl.BlockSpec((1, RET_DIM), lambda b, h: (0, h))],
        out_specs=(pl.BlockSpec((None, N, RET_DIM), lambda b, h: (b, 0, h)),
                   pl.BlockSpec((None, N, RET_DIM), lambda b, h: (b, 0, h)),
                   pl.BlockSpec((None, T, RET_DIM), lambda b, h: (b, 0, h)),
                   pl.BlockSpec((None, T, RET_DIM), lambda b, h: (b, 0, h))),
        scratch_shapes=[pltpu.VMEM((T, RET_DIM), BF16)] * 3 + [pltpu.VMEM((nt, RET_DIM, RET_DIM), BF16)] * 2,
        args=(lg, proj, proj, proj, proj, cos, sin, gn))


def _ret_bwd(proj, q_rot, k_rot, cos, sin, lg, gn, o, dlat, n_ctx):
    B, T, _ = proj.shape
    C = n_ctx
    N = T - C
    c = _div_tile(N, 256, 16)
    nt = N // c
    scale = RET_DIM ** -0.5

    def lat(t):
        return slice(C + t * c, C + (t + 1) * c)

    def body(lg_ref, qf32, kf32, v_ref, g_ref, cos_ref, sin_ref, gn_ref, o_ref, dl_ref,
             d_ref, dgn_ref, dlg_ref, qs, ks, vs, dos, hf, hb, hfa, hba, gf_s, gb_s):
        h = pl.program_id(1)
        lgf = lg_ref[0, h]
        lgb = lg_ref[1, h]
        gnv = gn_ref[...]

        def fold(a):
            return jnp.sum(a.reshape(a.shape[0] // SUBLANES, SUBLANES, a.shape[1]), axis=0)

        for rows in [slice(0, C)] + [lat(t) for t in range(nt)]:
            qs[rows, :] = qf32[rows, :].astype(BF16)
            ks[rows, :] = kf32[rows, :].astype(BF16)
            vs[rows, :] = v_ref[rows, :].astype(BF16)

        dgn = jnp.zeros((1, RET_DIM), F32)
        for t in range(nt):
            lrows = slice(t * c, (t + 1) * c)
            ov = o_ref[lrows, :]
            mu = jnp.mean(ov, axis=-1, keepdims=True)
            oc = ov - mu
            var = jnp.mean(oc * oc, axis=-1, keepdims=True)
            rstd = lax.rsqrt(var + NORM_EPS)
            yh = oc * rstd
            g = g_ref[lat(t), :].astype(F32)
            sg = _sigmoid(g)
            dl = dl_ref[lrows, :]
            d_ref[3, lat(t), :] = (dl * (yh * gnv) * (sg * (1.0 + g * (1.0 - sg)))).astype(BF16)
            dls = dl * (g * sg)
            dgn = dgn + jnp.sum(dls * yh, axis=0, keepdims=True)
            dyh = dls * gnv
            do = rstd * (dyh - jnp.mean(dyh, axis=-1, keepdims=True)
                         - yh * jnp.mean(dyh * yh, axis=-1, keepdims=True))
            dos[lrows, :] = do.astype(BF16)
        dgn_ref[...] = jnp.concatenate([dgn, jnp.zeros((SUBLANES - 1, RET_DIM), F32)], axis=0)
        d_ref[3, 0:C, :] = jnp.zeros((C, RET_DIM), BF16)
        d_ref[0, 0:C, :] = jnp.zeros((C, RET_DIM), BF16)

        dec, dec_c, step_f, step_b = _ret_states(kf32, vs, lgf, lgb, C, c, nt, hf, hb, hfa, hba)

        def zmat(t, qdec):
            return _dot_tn((qf32[lat(t), :] * qdec).astype(BF16), dos[t * c:(t + 1) * c, :])

        acc3f = jnp.zeros((RET_DIM, RET_DIM), F32)
        acc3b = jnp.zeros((RET_DIM, RET_DIM), F32)
        state = jnp.zeros((RET_DIM, RET_DIM), F32)
        for t in range(nt - 1, -1, -1):
            gf_s[t] = state.astype(BF16)
            z = zmat(t, dec.q_f)
            acc3f = acc3f + hfa[t] * z
            state = step_f * state + z
        gctx_f = state.astype(BF16)
        state = jnp.zeros((RET_DIM, RET_DIM), F32)
        for t in range(nt):
            gb_s[t] = state.astype(BF16)
            z = zmat(t, dec.q_b)
            acc3b = acc3b + hba[t] * z
            state = step_b * state + z
        gctx_b = state.astype(BF16)

        rc = (lax.broadcasted_iota(jnp.int32, (c, c), 0) - lax.broadcasted_iota(jnp.int32, (c, c), 1)).astype(F32)
        w_diag = _ret_weights(lgf, lgb, rc)
        wg_f = jnp.where(rc >= 0.0, w_diag * rc, 0.0)
        wg_b = jnp.where(rc < 0.0, -w_diag * rc, 0.0)
        accf = jnp.zeros((SUBLANES, RET_DIM), F32)
        accb = jnp.zeros((SUBLANES, RET_DIM), F32)
        gdf = jnp.zeros((SUBLANES, c), F32)
        gdb = jnp.zeros((SUBLANES, c), F32)
        for t in range(nt):
            rows = lat(t)
            qt = qs[rows, :]
            kt = ks[rows, :]
            vt = vs[rows, :]
            dot = dos[t * c:(t + 1) * c, :]
            s = _dot_nt(qt, kt)
            dp = _dot_nt(dot, vt)
            dv = _dot_tn((s * w_diag).astype(BF16), dot)
            ds = (dp * w_diag).astype(BF16)
            dq = _dot(ds, kt)
            dk = _dot_tn(ds, qt)
            gs = dp * s
            gdf = gdf + fold(gs * wg_f)
            gdb = gdb + fold(gs * wg_b)
            qv = qf32[rows, :]
            kv = kf32[rows, :]
            dq_f = dec.q_f * _dot_nt(dot, hf[t])
            dq_b = dec.q_b * _dot_nt(dot, hb[t])
            dk_f = dec.k_f * _dot_nt(vt, gf_s[t])
            dk_b = dec.k_b * _dot_nt(vt, gb_s[t])
            accf = accf + fold(dec.head * dq_f * qv) + fold(dec.tail * dk_f * kv)
            accb = accb + fold(dec.tail * dq_b * qv) + fold(dec.head * dk_b * kv)
            dv = dv + dec.k_f * _dot(kt, gf_s[t]) + dec.k_b * _dot(kt, gb_s[t])
            cosb = cos_ref[rows, :]
            sinb = sin_ref[rows, :]
            d_ref[0, rows, :] = _unrope((dq + dq_f + dq_b) * scale, cosb, sinb).astype(BF16)
            d_ref[1, rows, :] = _unrope(dk + dk_f + dk_b, cosb, sinb).astype(BF16)
            d_ref[2, rows, :] = dv.astype(BF16)
        kc = ks[0:C, :]
        vc = vs[0:C, :]
        kcv = kf32[0:C, :]
        dkc_f = dec_c.k_f * _dot_nt(vc, gctx_f)
        dkc_b = dec_c.k_b * _dot_nt(vc, gctx_b)
        accf = accf + fold(dec_c.tail * dkc_f * kcv)
        accb = accb + fold(dec_c.head * dkc_b * kcv)
        d_ref[1, 0:C, :] = (dkc_f + dkc_b).astype(BF16)
        d_ref[2, 0:C, :] = (dec_c.k_f * _dot(kc, gctx_f) + dec_c.k_b * _dot(kc, gctx_b)).astype(BF16)
        gf = jnp.sum(gdf) + jnp.sum(accf) + jnp.sum(acc3f)
        gb = jnp.sum(gdb) + jnp.sum(accb) + jnp.sum(acc3b)
        row = lax.broadcasted_iota(jnp.int32, (SUBLANES, LANES), 0)
        dlg_ref[...] = jnp.where(row == 0, gf, jnp.where(row == 1, gb, 0.0))

    def col(seg):
        return pl.BlockSpec((None, T, RET_DIM), lambda b, h, seg=seg: (b, 0, seg * RET_HEADS + h))

    def head(rows):
        return pl.BlockSpec((None, rows, RET_DIM), lambda b, h: (b, 0, h))

    return _grid_call(
        body, name="ret_bwd", grid=(B, RET_HEADS),
        out_shape=(jax.ShapeDtypeStruct((B, 4, T, RET_WIDTH), BF16),
                   jax.ShapeDtypeStruct((B, SUBLANES, RET_WIDTH), F32),
                   jax.ShapeDtypeStruct((B, RET_HEADS, SUBLANES, LANES), F32)),
        in_specs=[pl.BlockSpec(memory_space=pltpu.SMEM), head(T), head(T), col(2), col(3),
                  pl.BlockSpec((T, RET_DIM), lambda b, h: (0, 0)), pl.BlockSpec((T, RET_DIM), lambda b, h: (0, 0)),
                  pl.BlockSpec((1, RET_DIM), lambda b, h: (0, h)), head(N), head(N)],
        out_specs=(pl.BlockSpec((None, 4, T, RET_DIM), lambda b, h: (b, 0, 0, h)),
                   pl.BlockSpec((None, SUBLANES, RET_DIM), lambda b, h: (b, 0, h)),
                   pl.BlockSpec((None, None, SUBLANES, LANES), lambda b, h: (b, h, 0, 0))),
        scratch_shapes=[pltpu.VMEM((T, RET_DIM), BF16)] * 3 + [pltpu.VMEM((N, RET_DIM), BF16)]
                       + [pltpu.VMEM((nt, RET_DIM, RET_DIM), BF16)] * 2 + [pltpu.VMEM((nt, RET_DIM, RET_DIM), F32)] * 2
                       + [pltpu.VMEM((nt, RET_DIM, RET_DIM), BF16)] * 2,
        args=(lg, q_rot, k_rot, proj, proj, cos, sin, gn, o, dlat))


def _na_geometry(rows):
    kh = min(NA_KH, rows)
    return kh, kh * GRID_W


def _pair_select():
    lane = lax.broadcasted_iota(jnp.int32, (2 * GRID_W, LANES), 1)
    row = lax.broadcasted_iota(jnp.int32, (2 * GRID_W, LANES), 0)
    return (lane >= NA_DIM) == (row >= GRID_W)


def _pair_bias(bias_ref, dr0, kh):
    return jnp.concatenate(
        [jnp.concatenate([bias_ref[e, pl.ds(dr0 + 2 * m, 1)].reshape(GRID_W, LANES) for m in range(kh // 2)], axis=1)
         for e in range(2)], axis=0)


def _na_softmax(s_loc, s_ctx):
    mx = jnp.maximum(jnp.max(s_loc, axis=-1, keepdims=True), jnp.max(s_ctx, axis=-1, keepdims=True))
    p_loc = jnp.exp(s_loc - mx)
    p_ctx = jnp.exp(s_ctx - mx)
    den = jnp.sum(p_loc, axis=-1, keepdims=True) + jnp.sum(p_ctx, axis=-1, keepdims=True)
    return p_loc, p_ctx, den


def _na_fwd(proj, bias2, n_ctx):
    assert proj.dtype == BF16
    B, T, _ = proj.shape
    C = n_ctx
    N = T - C
    R = N // GRID_W
    kh, nk = _na_geometry(R)
    scale = NA_DIM ** -0.5
    base = (4 * RET_WIDTH) // LANES

    def body(q_ref, kb16, vb16, bias_ref, out_ref, p_ref):
        kc = kb16[0:C, :]
        vc = vb16[0:C, :]
        lane = lax.broadcasted_iota(jnp.int32, (GRID_W, LANES), 1)
        sel2 = _pair_select()

        def group(gi, carry):
            pre = []
            for u in range(NA_GROUP):
                r = gi * NA_GROUP + u
                bs = jnp.clip(r - kh // 2, 0, R - kh)
                dr0 = bs - r + (NA_KH - 1)
                q = q_ref[pl.ds(pl.multiple_of(C + r * GRID_W, GRID_W), GRID_W), :].astype(F32) * scale
                q2 = jnp.where(sel2, jnp.concatenate([q, q], axis=0), 0.0).astype(BF16)
                band = pl.ds(pl.multiple_of(C + bs * GRID_W, GRID_W), nk)
                s_loc = _dot_nt(q2, kb16[band, :]) + _pair_bias(bias_ref, dr0, kh)
                s_ctx = _dot_nt(q2, kc)
                pre.append((r, band, s_loc, s_ctx))
            mid = [(r, band) + _na_softmax(s_loc, s_ctx) for r, band, s_loc, s_ctx in pre]
            for r, band, p_loc, p_ctx, den in mid:
                inv = 1.0 / den
                pb_loc = (p_loc * inv).astype(BF16)
                pb_ctx = (p_ctx * inv).astype(BF16)
                p_ref[r, :, 0:nk] = pb_loc
                p_ref[r, :, nk:] = pb_ctx
                o2 = _dot(pb_loc, vb16[band, :]) + _dot(pb_ctx, vc)
                out_ref[pl.ds(pl.multiple_of(r * GRID_W, GRID_W), GRID_W), :] = jnp.where(
                    lane < NA_DIM, o2[:GRID_W], o2[GRID_W:]).astype(BF16)
            return carry

        lax.fori_loop(0, R // NA_GROUP, group, 0)

    def col(seg):
        return pl.BlockSpec((None, T, LANES), lambda b, p, seg=seg: (b, 0, base + seg * NA_PAIRS + p))

    return _grid_call(
        body, name="na_fwd", grid=(B, NA_PAIRS),
        out_shape=(jax.ShapeDtypeStruct((B, N, NA_WIDTH), BF16),
                   jax.ShapeDtypeStruct((B, NA_PAIRS, R, 2 * GRID_W, nk + C), BF16)),
        in_specs=[col(0), col(1), col(2),
                  pl.BlockSpec((2, 2 * NA_KH - 2, GRID_W, LANES), lambda b, p: (p, 0, 0, 0))],
        out_specs=(pl.BlockSpec((None, N, LANES), lambda b, p: (b, 0, p)),
                   pl.BlockSpec((None, None, R, 2 * GRID_W, nk + C), lambda b, p: (b, p, 0, 0, 0))),
        scratch_shapes=[],
        args=(proj, proj, proj, bias2))


def _na_bwd(proj, probs, dlat, n_ctx):
    assert proj.dtype == BF16
    B, T, _ = proj.shape
    C = n_ctx
    N = T - C
    R = N // GRID_W
    kh, nk = _na_geometry(R)
    scale = NA_DIM ** -0.5
    base = (4 * RET_WIDTH) // LANES

    def body(q_ref, kb16, vb16, p_ref, dl_ref, d_ref, db_ref, dkv):
        b = pl.program_id(1)
        kc = kb16[0:C, :]
        vc = vb16[0:C, :]
        lane = lax.broadcasted_iota(jnp.int32, (GRID_W, LANES), 1)
        dkv[...] = jnp.zeros(dkv.shape, F32)
        d_ref[0, 0:C, :] = jnp.zeros((C, LANES), BF16)

        @pl.when(b == 0)
        def _():
            db_ref[...] = jnp.zeros(db_ref.shape, F32)

        sel2 = _pair_select()

        def group(gi, carry):
            pre = []
            for u in range(NA_GROUP):
                r = gi * NA_GROUP + u
                bs = jnp.clip(r - kh // 2, 0, R - kh)
                dr0 = bs - r + (NA_KH - 1)
                q = q_ref[pl.ds(pl.multiple_of(C + r * GRID_W, GRID_W), GRID_W), :].astype(F32) * scale
                do = dl_ref[pl.ds(pl.multiple_of(r * GRID_W, GRID_W), GRID_W), :]
                q2 = jnp.where(sel2, jnp.concatenate([q, q], axis=0), 0.0).astype(BF16)
                do2 = jnp.where(sel2, jnp.concatenate([do, do], axis=0), 0.0).astype(BF16)
                band = pl.ds(pl.multiple_of(C + bs * GRID_W, GRID_W), nk)
                dp_loc = _dot_nt(do2, vb16[band, :])
                dp_ctx = _dot_nt(do2, vc)
                pre.append((r, dr0, band, q2, do2, dp_loc, dp_ctx))
            mid = []
            for r, dr0, band, q2, do2, dp_loc, dp_ctx in pre:
                pb_loc = p_ref[r, :, 0:nk]
                pb_ctx = p_ref[r, :, nk:]
                p_loc = pb_loc.astype(F32)
                p_ctx = pb_ctx.astype(F32)
                delta = (jnp.sum(p_loc * dp_loc, axis=-1, keepdims=True)
                         + jnp.sum(p_ctx * dp_ctx, axis=-1, keepdims=True))
                ds_loc = p_loc * (dp_loc - delta)
                ds_ctx = p_ctx * (dp_ctx - delta)
                mid.append((r, dr0, band, q2, do2, pb_loc, pb_ctx, ds_loc, ds_ctx))
            for r, dr0, band, q2, do2, pb_loc, pb_ctx, ds_loc, ds_ctx in mid:
                dsb_loc = ds_loc.astype(BF16)
                dsb_ctx = ds_ctx.astype(BF16)
                dq2 = _dot(dsb_loc, kb16[band, :]) + _dot(dsb_ctx, kc)
                d_ref[0, pl.ds(pl.multiple_of(C + r * GRID_W, GRID_W), GRID_W), :] = (jnp.where(
                    lane < NA_DIM, dq2[:GRID_W], dq2[GRID_W:]) * scale).astype(BF16)
                dkv[0, band, :] += _dot_tn(dsb_loc, q2)
                dkv[1, band, :] += _dot_tn(pb_loc, do2)
                dkv[0, 0:C, :] += _dot_tn(dsb_ctx, q2)
                dkv[1, 0:C, :] += _dot_tn(pb_ctx, do2)
                for e in range(2):
                    for m in range(kh // 2):
                        db_ref[e, pl.ds(dr0 + 2 * m, 1)] += ds_loc[e * GRID_W:(e + 1) * GRID_W,
                                                                   m * LANES:(m + 1) * LANES].reshape(1, GRID_W, LANES)
            return carry

        lax.fori_loop(0, R // NA_GROUP, group, 0)
        d_ref[1] = dkv[0].astype(BF16)
        d_ref[2] = dkv[1].astype(BF16)

    def col(seg):
        return pl.BlockSpec((None, T, LANES), lambda p, b, seg=seg: (b, 0, base + seg * NA_PAIRS + p))

    return _grid_call(
        body, name="na_bwd", grid=(NA_PAIRS, B),
        out_shape=(jax.ShapeDtypeStruct((B, 3, T, NA_WIDTH), BF16),
                   jax.ShapeDtypeStruct((NA_HEADS, 2 * NA_KH - 2, GRID_W, LANES), F32)),
        in_specs=[col(0), col(1), col(2),
                  pl.BlockSpec((None, None, R, 2 * GRID_W, nk + C), lambda p, b: (b, p, 0, 0, 0)),
                  pl.BlockSpec((None, N, LANES), lambda p, b: (b, 0, p))],
        out_specs=(pl.BlockSpec((None, 3, T, LANES), lambda p, b: (b, 0, 0, p)),
                   pl.BlockSpec((2, 2 * NA_KH - 2, GRID_W, LANES), lambda p, b: (p, 0, 0, 0))),
        scratch_shapes=[pltpu.VMEM((2, T, LANES), F32)],
        args=(proj, proj, proj, probs, dlat))


def _split3(a):
    hi = a.astype(BF16)
    r1 = a - hi.astype(F32)
    mid = r1.astype(BF16)
    lo = (r1 - mid.astype(F32)).astype(BF16)
    return hi, mid, lo


def _rpb_reduce(dbias2, onehot2):
    rows = dbias2.shape[0] * dbias2.shape[1]
    flat = dbias2.reshape(rows, GRID_W * LANES)

    def body(a_ref, oh_ref, o_ref):
        hi, mid, lo = _split3(a_ref[...])
        oh = oh_ref[...]
        o_ref[...] = _dot(hi, oh) + _dot(mid, oh) + _dot(lo, oh)

    return pl.pallas_call(
        body, name="rpb_reduce", out_shape=jax.ShapeDtypeStruct((rows, LANES), F32),
        in_specs=[_vmem(), _vmem()], out_specs=_vmem(),
        compiler_params=pltpu.CompilerParams(vmem_limit_bytes=VMEM_LIMIT),
    )(flat, onehot2)


def _dense_core(lat_ret, lat_na, x, tgt, modl, g_post_mix, g_pre_mlp, g_post_mlp, w_out, w1, w2):
    B, N, D = x.shape
    F = w1.shape[1]
    wout_rows, w1_cols, w2_rows = w_out.shape[0] // N_DEV, w1.shape[1] // N_DEV, w2.shape[0] // N_DEV
    mixw = w_out.shape[0]
    half = mixw // 2
    tm = _div_tile(N, 256, 16)
    nt = N // tm
    fc = _div_tile(F, 1024, LANES)

    def body(lr_ref, ln_ref, x_ref, t_ref, gt1_ref, sh2_ref, sc2_ref, gt2_ref, gpm_ref, gpre_ref, gpo_ref,
             wout_part, w1_part, w2_part,
             dy1_ref, dlr_ref, dln_ref, dmix_ref, h2_ref, a_ref, du_ref, dz_ref, red_ref, wout_hbm, w1_hbm, w2_hbm,
             wout_v, w1_v, w2_v, u_s, sems, fsend, frecv):
        @pl.when((pl.program_id(0) == 0) & (pl.program_id(1) == 0))
        def _():
            relay = [(_row_block(wout_part, wout_rows), _row_block(wout_hbm, wout_rows)),
                     (_col_block(w1_part, w1_cols), _col_block(w1_hbm, w1_cols)),
                     (_row_block(w2_part, w2_rows), _row_block(w2_hbm, w2_rows))]
            _forward_start(relay, fsend, frecv)
            _forward_wait(relay, fsend, frecv)
            cps = [pltpu.make_async_copy(wout_hbm, wout_v, sems.at[0]),
                   pltpu.make_async_copy(w1_hbm, w1_v, sems.at[1]),
                   pltpu.make_async_copy(w2_hbm, w2_v, sems.at[2])]
            for cp in cps:
                cp.start()
            for cp in cps:
                cp.wait()

        @pl.when(pl.program_id(1) == 0)
        def _():
            red_ref[...] = jnp.zeros(red_ref.shape, F32)

        gt1 = gt1_ref[...]
        sh2 = sh2_ref[...]
        sc2 = sc2_ref[...]
        gt2 = gt2_ref[...]
        gpm = gpm_ref[...]
        gpre = gpre_ref[...]
        gpo = gpo_ref[...]

        def rowmean(a):
            return jnp.mean(a, axis=-1, keepdims=True)

        def colsum(a):
            return jnp.sum(a, axis=0, keepdims=True)

        mix_gain = gt1 * gpm
        mlp_in_gain = gpre * (1.0 + sc2)
        mlp_out_gain = gt2 * gpo
        mix = _dot(lr_ref[...], wout_v[0:half, :]) + _dot(ln_ref[...], wout_v[half:, :])
        x = x_ref[...]
        rm = lax.rsqrt(rowmean(mix * mix) + NORM_EPS)
        mh = mix * rm
        y1 = x + mh * mix_gain
        r1 = lax.rsqrt(rowmean(y1 * y1) + NORM_EPS)
        xh = y1 * r1
        h2b = (xh * mlp_in_gain + sh2).astype(BF16)
        h2_ref[...] = h2b
        z = jnp.zeros((tm, D), F32)
        for c0 in range(0, F, fc):
            u = _dot(h2b, w1_v[:, c0:c0 + fc])
            u_s[:, c0:c0 + fc] = u
            ru = jnp.maximum(u, 0.0)
            ab = (ru * ru).astype(BF16)
            a_ref[:, c0:c0 + fc] = ab
            z = z + _dot(ab, w2_v[c0:c0 + fc, :])
        r2 = lax.rsqrt(rowmean(z * z) + NORM_EPS)
        zh = z * r2
        y2 = y1 + zh * mlp_out_gain
        err = y2 - t_ref[...]
        loss = 0.5 * jnp.sum(rowmean(err * err))
        dy2 = err * (1.0 / D)
        s_out = colsum(dy2 * zh)
        red_ref[2:3, :] += s_out * gpo
        red_ref[6:7, :] += s_out * gt2
        dzh = dy2 * mlp_out_gain
        dz = r2 * (dzh - zh * rowmean(dzh * zh))
        dzb = dz.astype(BF16)
        dz_ref[...] = dzb
        dh2 = jnp.zeros((tm, D), F32)
        for c0 in range(0, F, fc):
            da = _dot_nt(dzb, w2_v[c0:c0 + fc, :])
            dub = (da * (2.0 * jnp.maximum(u_s[:, c0:c0 + fc], 0.0))).astype(BF16)
            du_ref[:, c0:c0 + fc] = dub
            dh2 = dh2 + _dot_nt(dub, w1_v[:, c0:c0 + fc])
        s_in = colsum(dh2 * xh)
        red_ref[3:4, :] += s_in * gpre
        red_ref[4:5, :] += colsum(dh2)
        red_ref[5:6, :] += s_in * (1.0 + sc2)
        dxh = dh2 * mlp_in_gain
        dy1 = dy2 + r1 * (dxh - xh * rowmean(dxh * xh))
        dy1_ref[...] = dy1
        s_mix = colsum(dy1 * mh)
        red_ref[0:1, :] += s_mix * gpm
        red_ref[1:2, :] += s_mix * gt1
        dmh = dy1 * mix_gain
        dmix = (rm *(dmh - mh * rowmean(dmh * mh))).astype(BF16)
        dmix_ref[...] = dmix
        dlr_ref[...] = _dot_nt(dmix, wout_v[0:half, :])
        dln_ref[...] = _dot_nt(dmix, wout_v[half:, :])
        red_ref[7:8, :] += jnp.zeros((1, D), F32) + loss

    def tok(w):
        return pl.BlockSpec((None, tm, w), lambda b, t: (b, t, 0))

    def mod(k):
        return pl.BlockSpec((None, None, 1, D), lambda b, t, k=k: (b, k, 0, 0))

    def vec():
        return pl.BlockSpec((1, D), lambda b, t: (0, 0))

    return pl.pallas_call(
        body, name="dense_core", grid=(B, nt),
        out_shape=(jax.ShapeDtypeStruct((B, N, D), F32), jax.ShapeDtypeStruct((B, N, half), F32),
                   jax.ShapeDtypeStruct((B, N, half), F32), jax.ShapeDtypeStruct((B, N, D), BF16),
                   jax.ShapeDtypeStruct((B, N, D), BF16), jax.ShapeDtypeStruct((B, N, F), BF16),
                   jax.ShapeDtypeStruct((B, N, F), BF16), jax.ShapeDtypeStruct((B, N, D), BF16),
                   jax.ShapeDtypeStruct((B, SUBLANES, D), F32),
                   jax.ShapeDtypeStruct(w_out.shape, w_out.dtype), jax.ShapeDtypeStruct(w1.shape, w1.dtype),
                   jax.ShapeDtypeStruct(w2.shape, w2.dtype)),
        in_specs=[tok(half), tok(half), tok(D), tok(D), mod(2), mod(3), mod(4), mod(5), vec(), vec(), vec(),
                  _any(), _any(), _any()],
        out_specs=(tok(D), tok(half), tok(half), tok(D), tok(D), tok(F), tok(F), tok(D),
                   pl.BlockSpec((None, SUBLANES, D), lambda b, t: (b, 0, 0)), _any(), _any(), _any()),
        scratch_shapes=[pltpu.VMEM((mixw, D), BF16), pltpu.VMEM((D, F), BF16), pltpu.VMEM((F, D), BF16),
                        pltpu.VMEM((tm, F), F32), pltpu.SemaphoreType.DMA((3,)),
                        pltpu.SemaphoreType.DMA((3, 3)), pltpu.SemaphoreType.DMA((3, 3))],
        input_output_aliases={11: 9, 12: 10, 13: 11},
        compiler_params=_params("arbitrary", "arbitrary"),
    )(lat_ret, lat_na, x, tgt, modl, modl, modl, modl, g_post_mix, g_pre_mlp, g_post_mlp, w_out, w1, w2)[:9]


def _dense_core_skewed(lat_ret, lat_na, x, tgt, modl, g_post_mix, g_pre_mlp, g_post_mlp, w_out, w1, w2):
    B, N, D = x.shape
    F = w1.shape[1]
    wout_rows, w1_cols, w2_rows = w_out.shape[0] // N_DEV, w1.shape[1] // N_DEV, w2.shape[0] // N_DEV
    mixw = w_out.shape[0]
    half = mixw // 2
    tm = _div_tile(N, 256, 16)
    nt = N // tm
    n_tiles = B * nt
    fc = _div_tile(F, 1024, LANES)

    def first(s):
        tile = jnp.minimum(s, n_tiles - 1)
        return tile // nt, tile % nt

    def second(s):
        tile = jnp.maximum(s - 1, 0)
        return tile // nt, tile % nt

    def body(lr_ref, ln_ref, x_ref, gt1a_ref, sh2a_ref, sc2a_ref, t_ref, gt1b_ref, sc2b_ref, gt2b_ref,
             gpm_ref, gpre_ref, gpo_ref, wout_part, w1_part, w2_part,
             h2_ref, a_ref, dy1_ref, dlr_ref, dln_ref, dmix_ref, du_ref, dz_ref, red_ref, wout_hbm, w1_hbm, w2_hbm,
             wout_v, w1_v, w2_v, z_c, y1_c, mh_c, rm_c, r1_c, u_c, sems, fsend, frecv):
        s = pl.program_id(0)
        p = s % 2
        q = 1 - p

        @pl.when(s == 0)
        def _():
            relay = [(_row_block(wout_part, wout_rows), _row_block(wout_hbm, wout_rows)),
                     (_col_block(w1_part, w1_cols), _col_block(w1_hbm, w1_cols)),
                     (_row_block(w2_part, w2_rows), _row_block(w2_hbm, w2_rows))]
            _forward_start(relay, fsend, frecv)
            _forward_wait(relay, fsend, frecv)
            cps = [pltpu.make_async_copy(wout_hbm, wout_v, sems.at[0]),
                   pltpu.make_async_copy(w1_hbm, w1_v, sems.at[1]),
                   pltpu.make_async_copy(w2_hbm, w2_v, sems.at[2])]
            for cp in cps:
                cp.start()
            for cp in cps:
                cp.wait()

        gpm = gpm_ref[...]
        gpre = gpre_ref[...]
        gpo = gpo_ref[...]
        t2 = jnp.maximum(s - 1, 0) % nt

        def rowmean(a):
            return jnp.mean(a, axis=-1, keepdims=True)

        def colsum(a):
            return jnp.sum(a, axis=0, keepdims=True)

        def add_red(row, val):
            red_ref[row:row + 1, :] = jnp.where(t2 == 0, 0.0, red_ref[row:row + 1, :]) + val

        def first_half():
            mix = _dot(lr_ref[...], wout_v[0:half, :]) + _dot(ln_ref[...], wout_v[half:, :])
            yield
            rm = lax.rsqrt(rowmean(mix * mix) + NORM_EPS)
            mh = mix * rm
            y1 = x_ref[...] + mh * (gt1a_ref[...] * gpm)
            r1 = lax.rsqrt(rowmean(y1 * y1) + NORM_EPS)
            h2b = ((y1 * r1) * (gpre * (1.0 + sc2a_ref[...])) + sh2a_ref[...]).astype(BF16)
            h2_ref[...] = h2b
            y1_c[p] = y1
            mh_c[p] = mh
            rm_c[p] = rm
            r1_c[p] = r1
            yield
            z1 = jnp.zeros((tm, D), F32)
            for c0 in range(0, F, fc):
                u = _dot(h2b, w1_v[:, c0:c0 + fc])
                u_c[p, :, c0:c0 + fc] = u.astype(BF16)
                ru = jnp.maximum(u, 0.0)
                ab = (ru * ru).astype(BF16)
                a_ref[:, c0:c0 + fc] = ab
                z1 = z1 + _dot(ab, w2_v[c0:c0 + fc, :])
            z_c[p] = z1
            yield

        def second_half():
            gt2 = gt2b_ref[...]
            mlp_out_gain = gt2 * gpo
            z = z_c[q]
            y1b = y1_c[q]
            r2 = lax.rsqrt(rowmean(z * z) + NORM_EPS)
            zh = z * r2
            err = y1b + zh * mlp_out_gain - t_ref[...]
            loss = 0.5 * jnp.sum(rowmean(err * err))
            dy2 = err * (1.0 / D)
            s_out = colsum(dy2 * zh)
            add_red(2, s_out * gpo)
            add_red(6, s_out * gt2)
            dzh = dy2 * mlp_out_gain
            dzb = (r2 * (dzh - zh * rowmean(dzh * zh))).astype(BF16)
            dz_ref[...] = dzb
            yield
            dh2 = jnp.zeros((tm, D), F32)
            for c0 in range(0, F, fc):
                da = _dot_nt(dzb, w2_v[c0:c0 + fc, :])
                dub = (da * (2.0 * jnp.maximum(u_c[q, :, c0:c0 + fc].astype(F32), 0.0))).astype(BF16)
                du_ref[:, c0:c0 + fc] = dub
                dh2 = dh2 + _dot_nt(dub, w1_v[:, c0:c0 + fc])
            yield
            sc2 = sc2b_ref[...]
            r1b = r1_c[q]
            xh = y1b * r1b
            s_in = colsum(dh2 * xh)
            add_red(3, s_in * gpre)
            add_red(4, colsum(dh2))
            add_red(5, s_in * (1.0 + sc2))
            dxh = dh2 * (gpre * (1.0 + sc2))
            dy1 = dy2 + r1b * (dxh - xh * rowmean(dxh * xh))
            dy1_ref[...] = dy1
            gt1 = gt1b_ref[...]
            mhb = mh_c[q]
            s_mix = colsum(dy1 * mhb)
            add_red(0, s_mix * gpm)
            add_red(1, s_mix * gt1)
            dmh = dy1 * (gt1 * gpm)
            dmix = (rm_c[q] * (dmh - mhb * rowmean(dmh * mhb))).astype(BF16)
            dmix_ref[...] = dmix
            dlr_ref[...] = _dot_nt(dmix, wout_v[0:half, :])
            dln_ref[...] = _dot_nt(dmix, wout_v[half:, :])
            add_red(7, jnp.zeros((1, D), F32) + loss)
            yield

        def interleave(*halves):
            runs = [h() for h in halves]
            for _ in range(3):
                for r in runs:
                    next(r)

        @pl.when(s == 0)
        def _():
            interleave(first_half)

        @pl.when((s > 0) & (s < n_tiles))
        def _():
            interleave(first_half, second_half)

        @pl.when(s == n_tiles)
        def _():
            interleave(second_half)

    def tok(w, which):
        return pl.BlockSpec((None, tm, w), lambda s: (*which(s), 0))

    def mod(k, which):
        return pl.BlockSpec((None, None, 1, D), lambda s, k=k: (which(s)[0], k, 0, 0))

    def vec():
        return pl.BlockSpec((1, D), lambda s: (0, 0))

    outs = pl.pallas_call(
        body, name="dense_core", grid=(n_tiles + 1,),
        out_shape=(jax.ShapeDtypeStruct((B, N, D), BF16), jax.ShapeDtypeStruct((B, N, F), BF16),
                   jax.ShapeDtypeStruct((B, N, D), F32), jax.ShapeDtypeStruct((B, N, half), F32),
                   jax.ShapeDtypeStruct((B, N, half), F32), jax.ShapeDtypeStruct((B, N, D), BF16),
                   jax.ShapeDtypeStruct((B, N, F), BF16), jax.ShapeDtypeStruct((B, N, D), BF16),
                   jax.ShapeDtypeStruct((B, SUBLANES, D), F32),
                   jax.ShapeDtypeStruct(w_out.shape, w_out.dtype), jax.ShapeDtypeStruct(w1.shape, w1.dtype),
                   jax.ShapeDtypeStruct(w2.shape, w2.dtype)),
        in_specs=[tok(half, first), tok(half, first), tok(D, first), mod(2, first), mod(3, first), mod(4, first),
                  tok(D, second), mod(2, second), mod(4, second), mod(5, second),
                  vec(), vec(), vec(), _any(), _any(), _any()],
        out_specs=(tok(D, first), tok(F, first),
                   tok(D, second), tok(half, second), tok(half, second), tok(D, second), tok(F, second), tok(D, second),
                   pl.BlockSpec((None, SUBLANES, D), lambda s: (second(s)[0], 0, 0)), _any(), _any(), _any()),
        scratch_shapes=[pltpu.VMEM((mixw, D), BF16), pltpu.VMEM((D, F), BF16), pltpu.VMEM((F, D), BF16),
                        pltpu.VMEM((2, tm, D), F32), pltpu.VMEM((2, tm, D), F32), pltpu.VMEM((2, tm, D), F32),
                        pltpu.VMEM((2, tm, 1), F32), pltpu.VMEM((2, tm, 1), F32), pltpu.VMEM((2, tm, F), BF16),
                        pltpu.SemaphoreType.DMA((3,)), pltpu.SemaphoreType.DMA((3, 3)), pltpu.SemaphoreType.DMA((3, 3))],
        input_output_aliases={13: 9, 14: 10, 15: 11},
        compiler_params=_params("arbitrary"),
    )(lat_ret, lat_na, x, modl, modl, modl, tgt, modl, modl, modl, g_post_mix, g_pre_mlp, g_post_mlp, w_out, w1, w2)
    h2, act, dy1, dlat_ret, dlat_na, dmix, du, dz, red = outs[:9]
    return dy1, dlat_ret, dlat_na, dmix, h2, act, du, dz, red


def _inproj_bwd(dret, dna, x, ctx, dy1, modl, g1, w_in_t):
    B, N, D = x.shape
    n_ctx = ctx.shape[1]
    T = n_ctx + N
    tm = _div_tile(n_ctx, 256, 16)
    nct, ctx_spec, lat_spec = _token_tiles(n_ctx, tm)
    nt = T // tm
    nseg_r = dret.shape[1]
    nseg_n = dna.shape[1]
    nw = w_in_t.shape[0]

    def body(*refs):
        seg_refs = refs[:nseg_r + nseg_n]
        c_ref, x_ref, dy1_ref, sc_ref, g_ref, w_ref, dx_ref, red_ref = refs[nseg_r + nseg_n:]
        t = pl.program_id(1)
        dh = jnp.zeros((tm, D), F32)
        for s, ref in enumerate(seg_refs):
            dh = dh + _dot(ref[...], w_ref[s * SEG:(s + 1) * SEG, :])
        x = jnp.where(t < nct, c_ref[...], x_ref[...])
        g = g_ref[...]
        r = lax.rsqrt(jnp.mean(x * x, axis=-1, keepdims=True) + NORM_EPS)
        xh = x * r
        gain = 1.0 + sc_ref[...]
        s_in = jnp.sum(dh * xh, axis=0, keepdims=True)
        red_ref[0:1, :] = jnp.sum(dh, axis=0, keepdims=True)
        red_ref[1:2, :] = s_in * g
        red_ref[2:3, :] = s_in * gain
        red_ref[3:, :] = jnp.zeros((SUBLANES - 3, D), F32)
        dxh = dh * (g * gain)
        dx = r * (dxh - xh * jnp.mean(dxh * xh, axis=-1, keepdims=True))
        dx_ref[...] = dx + jnp.where(t >= nct, dy1_ref[...], 0.0)

    def mrow(b, t):
        return jnp.where(t < nct, B, b)

    def seg(s):
        return pl.BlockSpec((None, None, tm, SEG), lambda b, t, s=s: (b, s, t, 0))

    return _grid_call(
        body, name="inproj_bwd", grid=(B, nt),
        out_shape=(jax.ShapeDtypeStruct((B, N, D), F32), jax.ShapeDtypeStruct((B, nt, SUBLANES, D), F32)),
        in_specs=[seg(s) for s in range(nseg_r)] + [seg(s) for s in range(nseg_n)]
                 + [ctx_spec(D), lat_spec(D), lat_spec(D),
                    pl.BlockSpec((None, None, 1, D), lambda b, t: (mrow(b, t), 1, 0, 0)),
                    pl.BlockSpec((1, D), lambda b, t: (0, 0)),
                    pl.BlockSpec((nw, D), lambda b, t: (0, 0))],
        out_specs=(lat_spec(D), pl.BlockSpec((None, None, SUBLANES, D), lambda b, t: (b, t, 0, 0))),
        scratch_shapes=[], args=(*([dret] * nseg_r), *([dna] * nseg_n), ctx, x, dy1, modl, g1, w_in_t))


def _tn_matmul(lhs, rhs, name, rows_before=0, rows_after=0, into=None):
    B, S, T, W = lhs.shape
    nn = rhs.shape[-1]
    tk = _div_tile(T, 2304, LANES)
    bm = _div_tile(W, 1024, LANES)
    bn = _div_tile(nn, 1024, LANES)
    nkt = T // tk
    nk = B * nkt

    def body(l_ref, r_ref, *rest):
        o_ref, acc = rest[-2:]
        k = pl.program_id(3)

        @pl.when(k == 0)
        def _():
            acc[...] = jnp.zeros(acc.shape, F32)

        acc[...] += _dot_tn(l_ref[...].astype(BF16), r_ref[...].astype(BF16))

        @pl.when(k == nk - 1)
        def _():
            o_ref[...] = acc[...].astype(BF16)

    nwb = W // bm
    first = rows_before // bm
    return pl.pallas_call(
        functools.partial(body), name=name, grid=(S, nwb, nn // bn, nk),
        out_shape=jax.ShapeDtypeStruct((rows_before + S * W + rows_after, nn), BF16),
        in_specs=[pl.BlockSpec((None, None, tk, bm), lambda s, i, j, k: (k // nkt, s, k % nkt, i)),
                  pl.BlockSpec((None, tk, bn), lambda s, i, j, k: (k // nkt, k % nkt, j))]
                 + ([] if into is None else [_any()]),
        out_specs=pl.BlockSpec((bm, bn), lambda s, i, j, k: (first + s * nwb + i, j)),
        scratch_shapes=[pltpu.VMEM((bm, bn), F32)],
        input_output_aliases={} if into is None else {2: 0},
        compiler_params=_params("parallel", "parallel", "parallel", "arbitrary"),
    )(lhs, rhs, *([] if into is None else [into]))


class _SplitScatter:
    def __init__(self, gs, block_ofs, land_shapes, name, kind="scatter", masks=ALL_PEERS):
        self.n = n = len(gs)
        self.block_ofs, self.kind, self.masks = block_ofs, kind, masks
        if kind == "scatter":
            land_shapes = [(N_DEV,) + tuple(bs) for bs in land_shapes]
        hbm = pl.BlockSpec(memory_space=pltpu.HBM)
        sem = pl.BlockSpec(memory_space=pltpu.SEMAPHORE)

        def body(*refs):
            g_refs, land_refs = refs[:n], refs[n:2 * n]
            send_sems, recv_sems, own_sems = refs[2 * n:2 * n + 3]
            token = refs[-1]
            for own, pushes in self._copies(g_refs, land_refs, send_sems, recv_sems, own_sems, landing="sender"):
                own.start()
                for cp in pushes:
                    cp.start()
            token[...] = jnp.zeros_like(token)

        outs = pl.pallas_call(
            body, name=name,
            out_shape=(pltpu.SemaphoreType.DMA((n * (N_DEV - 1),)), pltpu.SemaphoreType.DMA((n * (N_DEV - 1),)),
                       pltpu.SemaphoreType.DMA((n,)))
                      + tuple(pltpu.HBM(g.shape, g.dtype) for g in gs)
                      + tuple(pltpu.HBM(s, g.dtype) for s, g in zip(land_shapes, gs))
                      + (jax.ShapeDtypeStruct((SUBLANES, LANES), F32),),
            in_specs=(hbm,) * (2 * n), out_specs=(sem,) * 3 + (hbm,) * (2 * n) + (_vmem(),),
            input_output_aliases={k: 3 + k for k in range(2 * n)},
            compiler_params=pltpu.CompilerParams(has_side_effects=pltpu.SideEffectType.DATAFLOW_SIDE_EFFECTING),
        )(*[pltpu.with_memory_space_constraint(g, pltpu.HBM) for g in gs],
          *[pltpu.with_memory_space_constraint(lax.empty(s, g.dtype), pltpu.HBM) for s, g in zip(land_shapes, gs)])
        self.sems, self.thru, self.token = outs[:3], outs[3:3 + 2 * n], outs[-1]

    def _copies(self, g_refs, land_refs, send_sems, recv_sems, own_sems, landing):
        me, peers = _me_and_peers()
        out = []
        for k in range(self.n):
            if self.kind == "scatter":
                src, dst = self.block_ofs[k](g_refs[k]), _slot(land_refs[k])
            else:
                src, dst = (lambda p, k=k: g_refs[k]), self.block_ofs[k](land_refs[k])
            own = pltpu.make_async_copy(src(me), dst(me), own_sems.at[k]) if landing == "sender" else None
            pushes = []
            for m in self.masks:
                dev, pid = peers[m - 1]
                i = k * (N_DEV - 1) + m - 1
                pushes.append(_remote(src(pid), dst(me if landing == "sender" else pid),
                                      send_sems.at[i], recv_sems.at[i], dev))
            out.append((own, pushes))
        return out


def _scatter_wait(scatters, after, name):
    hbm = pl.BlockSpec(memory_space=pltpu.HBM)
    sem = pl.BlockSpec(memory_space=pltpu.SEMAPHORE)
    n_arr = [2 * sc.n for sc in scatters]
    total = sum(n_arr)

    def body(*refs):
        arrs, sems = refs[:total], refs[total:total + 3 * len(scatters)]
        a0 = 0
        for j, sc in enumerate(scatters):
            g_refs, land_refs = arrs[a0:a0 + sc.n], arrs[a0 + sc.n:a0 + 2 * sc.n]
            a0 += 2 * sc.n
            send_sems, recv_sems, own_sems = sems[3 * j:3 * j + 3]
            for (own, sent), (_, got) in zip(sc._copies(g_refs, land_refs, send_sems, recv_sems, own_sems, "sender"),
                                             sc._copies(g_refs, land_refs, send_sems, recv_sems, own_sems, "receiver")):
                own.wait()
                for cp in sent:
                    cp.wait_send()
                for cp in got:
                    cp.wait_recv()

    operands = [a for sc in scatters for a in sc.thru]
    outs = pl.pallas_call(
        body, name=name,
        out_shape=tuple(pltpu.HBM(a.shape, a.dtype) for a in operands),
        in_specs=(hbm,) * total + (sem,) * (3 * len(scatters)) + (pl.BlockSpec(memory_space=pl.ANY),),
        out_specs=(hbm,) * total, input_output_aliases={k: k for k in range(total)},
        compiler_params=pltpu.CompilerParams(has_side_effects=pltpu.SideEffectType.DATAFLOW_SIDE_EFFECTING),
    )(*operands, *[s for sc in scatters for s in sc.sems], after)
    lands, a0 = [], 0
    for sc in scatters:
        lands.extend(outs[a0 + sc.n:a0 + 2 * sc.n])
        a0 += 2 * sc.n
    return lands


def _small_ar(mbuf, silu_all, w_ada, c_ctx, n_mod_rows, n_vec_rows):
    D = silu_all.shape[1]
    ncol = w_ada.shape[1]
    nm = mbuf.shape[2]
    srows = silu_all.shape[0]

    def body(mbuf, s_ref, w_ref, cc_ref, tot_ref, gb_ref, gw_ref, gc_ref, tbuf, dmx, cmrow, send3, recv3):
        me, _ = _me_and_peers()
        msum = mbuf[0]
        for k in range(1, N_DEV):
            msum = msum + mbuf[k]
        tot_ref[...] = msum[n_mod_rows:n_mod_rows + n_vec_rows]
        gb_ref[...] = jnp.sum(msum[0:n_mod_rows], axis=0, keepdims=True)
        loc = pl.ds(pl.multiple_of(me * ncol, ncol), ncol)
        for k in range(N_DEV):
            dmx[k * SUBLANES:(k + 1) * SUBLANES, :] = mbuf[k, :, loc]
        cmrow[...] = msum
        cm_loc = cmrow[n_mod_rows - 1:n_mod_rows, loc]
        dmx[N_DEV * SUBLANES:, :] = jnp.concatenate([cm_loc, jnp.zeros((SUBLANES - 1, ncol), F32)], axis=0)
        gw_ref[...] = _dot_tn(s_ref[...], dmx[...])
        tbuf[me] = _dot_nt(dmx[N_DEV * SUBLANES:, :], w_ref[...])
        _exchange(lambda p: tbuf.at[me], lambda p: tbuf.at[p], send3, recv3)
        tsum = tbuf[0]
        for k in range(1, N_DEV):
            tsum = tsum + tbuf[k]
        cc = cc_ref[...]
        sg = _sigmoid(cc)
        gc_ref[...] = tsum[0:1, :] * (sg * (1.0 + cc * (1.0 - sg)))

    return pl.pallas_call(
        body, name="small_ar",
        out_shape=(jax.ShapeDtypeStruct((n_vec_rows, nm), F32), jax.ShapeDtypeStruct((1, nm), F32),
                   jax.ShapeDtypeStruct((D, ncol), F32), jax.ShapeDtypeStruct((1, D), F32)),
        in_specs=[_vmem()] * 4, out_specs=(_vmem(),) * 4,
        scratch_shapes=[pltpu.VMEM((N_DEV, SUBLANES, D), F32), pltpu.VMEM((srows, ncol), F32),
                        pltpu.VMEM((SUBLANES, nm), F32)] + [pltpu.SemaphoreType.DMA((N_DEV - 1,))] * 2,
        compiler_params=pltpu.CompilerParams(vmem_limit_bytes=VMEM_LIMIT),
    )(mbuf, silu_all, w_ada, c_ctx.reshape(1, D))


def _adam_update(w, g, m, v):
    mn = ADAM_B1 * m + (1.0 - ADAM_B1) * g
    vn = ADAM_B2 * v + (1.0 - ADAM_B2) * (g * g)
    m_hat = mn / (1.0 - ADAM_B1 ** ADAM_STEP)
    v_hat = vn / (1.0 - ADAM_B2 ** ADAM_STEP)
    return -ADAM_LR * (m_hat / (jnp.sqrt(v_hat) + ADAM_EPS) + ADAM_WD * w), mn, vn


def _adamw(w, g, m, v, name):
    rows, cols = w.shape
    tr = _div_tile(rows, 512, SUBLANES)

    def body(w_ref, g_ref, m_ref, v_ref, d_ref, nm_ref, nv_ref):
        d_ref[...], nm_ref[...], nv_ref[...] = _adam_update(w_ref[...], g_ref[...], m_ref[...], v_ref[...])

    spec = pl.BlockSpec((tr, cols), lambda i: (i, 0))
    return pl.pallas_call(
        functools.partial(body), name=name, grid=(rows // tr,),
        out_shape=(jax.ShapeDtypeStruct((rows, cols), F32),) * 3,
        in_specs=[spec] * 4, out_specs=(spec,) * 3,
        compiler_params=_params("parallel"),
    )(w, g, m, v)


def _adamw_small(items, name):
    n = len(items)

    def body(*refs):
        ins, outs = refs[:4 * n], refs[4 * n:]
        for i in range(n):
            w_ref, g_ref, m_ref, v_ref = ins[4 * i:4 * i + 4]
            outs[3 * i][...], outs[3 * i + 1][...], outs[3 * i + 2][...] = _adam_update(
                w_ref[...], g_ref[...], m_ref[...], v_ref[...])

    outs = pl.pallas_call(
        body, name=name,
        out_shape=tuple(jax.ShapeDtypeStruct(it[0].shape, F32) for it in items for _ in range(3)),
        in_specs=[_vmem()] * (4 * n), out_specs=(_vmem(),) * (3 * n),
        compiler_params=pltpu.CompilerParams(vmem_limit_bytes=VMEM_LIMIT),
    )(*[a for it in items for a in it])
    return [tuple(outs[3 * i:3 * i + 3]) for i in range(n)]


def _sum_adamw(buf, w, m, v, name):
    _, rows, cols = buf.shape
    tr = _div_tile(rows, 256, 2 * SUBLANES)

    def body(b_ref, w_ref, m_ref, v_ref, g_ref, d_ref, nm_ref, nv_ref):
        g = b_ref[0].astype(F32)
        for k in range(1, N_DEV):
            g = g + b_ref[k].astype(F32)
        g_ref[...] = g
        d_ref[...], nm_ref[...], nv_ref[...] = _adam_update(w_ref[...], g, m_ref[...], v_ref[...])

    spec = pl.BlockSpec((tr, cols), lambda i: (i, 0))
    return pl.pallas_call(
        functools.partial(body), name=name, grid=(rows // tr,),
        out_shape=(jax.ShapeDtypeStruct((rows, cols), F32),) * 4,
        in_specs=[pl.BlockSpec((N_DEV, tr, cols), lambda i: (0, i, 0))] + [spec] * 3, out_specs=(spec,) * 4,
        compiler_params=_params("parallel"),
    )(buf, w, m, v)


def _rope_tables(n_ctx, n):
    n_freq = RET_DIM // 4
    inv = np.float32(ROPE_BASE) ** (-np.arange(n_freq, dtype=np.float32) / np.float32(n_freq))
    tok = np.arange(n)
    pos_r = (tok // GRID_W).astype(np.float32)
    pos_c = (tok % GRID_W).astype(np.float32)
    ang_r = (pos_r[:, None] * inv[None, :]).astype(np.float32)
    ang_c = (pos_c[:, None] * inv[None, :]).astype(np.float32)
    cos = np.concatenate([np.cos(ang_r), np.cos(ang_r), np.cos(ang_c), np.cos(ang_c)], axis=-1)
    sin = np.concatenate([-np.sin(ang_r), np.sin(ang_r), -np.sin(ang_c), np.sin(ang_c)], axis=-1)
    cos = np.concatenate([np.ones((n_ctx, RET_DIM), np.float32), cos], axis=0)
    sin = np.concatenate([np.zeros((n_ctx, RET_DIM), np.float32), sin], axis=0)
    return jnp.asarray(cos, F32), jnp.asarray(sin, F32)


def _na_tables():
    q = np.arange(GRID_W)[:, None]
    k = np.arange(GRID_W)[None, :]
    start = np.clip(q - NA_KW // 2, 0, GRID_W - NA_KW)
    valid = (k >= start) & (k < start + NA_KW)
    dc = np.clip(k - q + (NA_KW - 1), 0, 2 * NA_KW - 2)
    ncls = 2 * NA_KW - 1
    onehot = (dc[None] == np.arange(ncls)[:, None, None]) & valid[None]
    oh2 = np.zeros((GRID_W, LANES, LANES), np.float32)
    for c in range(ncls):
        oh2[:, :GRID_W, c] = onehot[c]
        oh2[:, GRID_W:, 32 + c] = onehot[c]
    return onehot.astype(np.float32), valid, oh2.reshape(GRID_W * LANES, LANES)


def _paired_bias(rpb, onehot, valid):
    ncls = onehot.shape[0]
    pair = np.zeros((2 * ncls, GRID_W, LANES), np.float32)
    pair[:ncls, :, :GRID_W] = onehot
    pair[ncls:, :, GRID_W:] = onehot
    rows = jnp.concatenate([rpb[:, :-1], rpb[:, 1:]], axis=-1)
    t = jnp.einsum("hdc,cqk->hdqk", rows, jnp.asarray(pair), precision=lax.Precision.HIGHEST)
    return jnp.where(jnp.asarray(np.tile(valid, (1, 2)))[None, None], t, NEG_INF)


def kernel(x, c, ctx, c_ctx, w_ada, b_ada, g_pre_mix, g_post_mix, g_pre_mlp, g_post_mlp, w_in, ret_decay, ret_gn, na_rpb, w_out, w_mlp1, w_mlp2, loss_target, m_c_ctx, m_w_ada, m_b_ada, m_g_pre_mix, m_g_post_mix, m_g_pre_mlp, m_g_post_mlp, m_w_in, m_ret_decay, m_ret_gn, m_na_rpb, m_w_out, m_w_mlp1, m_w_mlp2, v_c_ctx, v_w_ada, v_b_ada, v_g_pre_mix, v_g_post_mix, v_g_pre_mlp, v_g_post_mlp, v_w_in, v_ret_decay, v_ret_gn, v_na_rpb, v_w_out, v_w_mlp1, v_w_mlp2):
    B, N, D = x.shape
    C = ctx.shape[1]
    T = C + N

    silu_all, mods_g, win_b, wout_l, w1_l, w2_l = _mod_gather(c, c_ctx, w_ada[0], b_ada, w_in[0].T, w_out[0],
                                                             w_mlp1[0], w_mlp2[0])
    mods_mine = mods_g.transpose(1, 0, 2).reshape(mods_g.shape[1], N_MOD * D)
    modl = jnp.concatenate([mods_mine[:B], mods_mine[SUBLANES:SUBLANES + 1]], axis=0)
    modl = modl.reshape(B + 1, N_MOD, 1, D)
    rin = w_in.shape[2]
    rout, c1, r2 = wout_l.shape[0], w1_l.shape[1], w2_l.shape[0]

    def rows_of(n):
        return lambda ref: _row_block(ref, n)

    def cols_of(n):
        return lambda ref: _col_block(ref, n)

    cos, sin = _rope_tables(C, N)
    onehot, valid, oh2 = _na_tables()
    bias2 = _paired_bias(na_rpb[0], onehot, valid)
    lg = jax.nn.log_sigmoid(ret_decay[0].astype(F32))

    ag = _SplitScatter([wout_l, w1_l, w2_l], [rows_of(rout), cols_of(c1), rows_of(r2)],
                       [(N_DEV * rout, D), (D, N_DEV * c1), (N_DEV * r2, D)], "ag_mlp_start",
                       kind="gather", masks=SIBLING + ICI_SAME_CORE)
    h_all, proj = _inproj_fwd(x, ctx, modl, g_pre_mix + ag.token[0, 0], win_b)
    o_ret, lat_ret, q_rot, k_rot = _ret_fwd(proj, cos, sin, lg, ret_gn, C)
    lat_na, na_probs = _na_fwd(proj, bias2, C)
    wout_part, w1_part, w2_part = _scatter_wait([ag], lat_na, "ag_mlp_wait")

    (dy1, dlat_ret, dlat_na, dmix, h2, act, du, dz, red_d) = _dense_core_skewed(
        lat_ret, lat_na, x, loss_target, modl, g_post_mix, g_pre_mlp, g_post_mlp, wout_part, w1_part, w2_part)

    gw_out_p = _tn_matmul(lat_ret[:, None], dmix, "gw_out_ret", rows_after=lat_na.shape[-1])
    gw_out_p = _tn_matmul(lat_na[:, None], dmix, "gw_out_na", rows_before=lat_ret.shape[-1], into=gw_out_p)
    gw1_p = _tn_matmul(h2[:, None], du, "gw_mlp1")
    gw2_p = _tn_matmul(act[:, None], dz, "gw_mlp2")
    rs_mlp = _SplitScatter([gw_out_p, gw1_p, gw2_p], [rows_of(rout), cols_of(c1), rows_of(r2)],
                           [(rout, D), (D, c1), (r2, D)], "rs_mlp_start")

    dret, dgn_p, dlg_p = _ret_bwd(proj, q_rot, k_rot, cos, sin, lg, ret_gn + rs_mlp.token[0, 0], o_ret, dlat_ret, C)
    dna, dbias2 = _na_bwd(proj, na_probs, dlat_na, C)
    ret_cols, na_cols = dret.shape[1] * dret.shape[3], dna.shape[1] * dna.shape[3]
    gwin_t_p = _tn_matmul(dret, h_all, "gw_in_ret", rows_after=na_cols)
    gwin_t_p = _tn_matmul(dna, h_all, "gw_in_na", rows_before=ret_cols, into=gwin_t_p)
    rs_in = _SplitScatter([gwin_t_p], [rows_of(rin)], [(rin, D)], "rs_w_in_start")
    grad_x, red_i = _inproj_bwd(dret, dna, x, ctx, dy1, modl, g_pre_mix + rs_in.token[0, 0], win_b)

    rd = red_d
    nct = red_i.shape[1] * C // T
    ri_ctx = red_i[:, :nct].sum(axis=(0, 1))
    ri_lat = red_i[:, nct:].sum(axis=1)
    d_mods = jnp.concatenate([ri_lat[:, 0], ri_lat[:, 1], rd[:, 0], rd[:, 4], rd[:, 3], rd[:, 2]], axis=-1)
    d_cmods = jnp.concatenate([ri_ctx[0], ri_ctx[1], jnp.zeros(((N_MOD - 2) * D,), F32)])[None]
    dg_pre_mix = ri_lat[:, 2].sum(axis=0) + ri_ctx[2]
    dg_post_mix = rd[:, 1].sum(axis=0)
    dg_pre_mlp = rd[:, 5].sum(axis=0)
    dg_post_mlp = rd[:, 6].sum(axis=0)
    loss_p = rd[:, 7, 0].sum()
    d_gn = dgn_p[:, 0].sum(axis=0)
    d_lg = dlg_p[:, :, :2, 0].sum(axis=0).T
    d_decay = d_lg * jax.nn.sigmoid(-ret_decay[0].astype(F32))
    rr = _rpb_reduce(dbias2, jnp.asarray(oh2, BF16)).reshape(NA_HEADS, 2 * NA_KH - 2, LANES)
    ncls = 2 * NA_KW - 1
    d_rpb = (jnp.pad(rr[:, :, :ncls], ((0, 0), (0, 1), (0, 0))) + jnp.pad(rr[:, :, 32:32 + ncls], ((0, 0), (1, 0), (0, 0))))
    d_rpb32 = jnp.pad(d_rpb, ((0, 0), (0, 0), (0, 32 - ncls)))
    pieces = [dg_pre_mix, dg_post_mix, dg_pre_mlp, dg_post_mlp, d_gn, d_rpb32.reshape(-1),
              jnp.pad(d_decay.reshape(-1), (0, LANES - d_decay.size)), jnp.full((LANES,), loss_p, F32)]
    vec = jnp.concatenate(pieces)
    nm = N_MOD * D
    n_vec_rows = -(-vec.shape[0] // nm)
    assert B + 1 + n_vec_rows <= SUBLANES
    vec = jnp.pad(vec, (0, n_vec_rows * nm - vec.shape[0])).reshape(n_vec_rows, nm)
    dm_slot = jnp.concatenate([d_mods, d_cmods, vec, jnp.zeros((SUBLANES - B - 1 - n_vec_rows, nm), F32)], axis=0)
    def whole(ref):
        return lambda p: ref

    small = _SplitScatter([dm_slot], [whole], [dm_slot.shape], "small_start")
    land_out, land_1, land_2, land_in = _scatter_wait([rs_mlp, rs_in], small.token, "rs_wait")
    fused = {"w_in": [a.T for a in _sum_adamw(land_in, w_in[0].T, m_w_in[0].T, v_w_in[0].T, "sum_adamw_w_in")],
             "w_out": _sum_adamw(land_out, w_out[0], m_w_out[0], v_w_out[0], "sum_adamw_w_out"),
             "w_mlp1": _sum_adamw(land_1, w_mlp1[0], m_w_mlp1[0], v_w_mlp1[0], "sum_adamw_w_mlp1"),
             "w_mlp2": _sum_adamw(land_2, w_mlp2[0], m_w_mlp2[0], v_w_mlp2[0], "sum_adamw_w_mlp2")}
    (mbuf,) = _scatter_wait([small], fused["w_mlp2"][0], "small_wait")
    tot, g_b_ada, g_w_ada, g_c_ctx = _small_ar(mbuf, silu_all, w_ada[0], c_ctx, B + 1, n_vec_rows)
    flat = tot.reshape(-1)
    o0 = 0
    g_pre_mix_g = flat[o0:o0 + D]; o0 += D
    g_post_mix_g = flat[o0:o0 + D]; o0 += D
    g_pre_mlp_g = flat[o0:o0 + D]; o0 += D
    g_post_mlp_g = flat[o0:o0 + D]; o0 += D
    g_gn = flat[o0:o0 + RET_WIDTH]; o0 += RET_WIDTH
    nrpb = NA_HEADS * (2 * NA_KH - 1) * 32
    g_rpb = flat[o0:o0 + nrpb].reshape(NA_HEADS, 2 * NA_KH - 1, 32)[:, :, :ncls]; o0 += nrpb
    g_decay = flat[o0:o0 + 2 * RET_HEADS].reshape(2, RET_HEADS); o0 += LANES
    loss = flat[o0]

    grads = {
        "c_ctx": g_c_ctx.reshape(c_ctx.shape), "w_ada": g_w_ada[None], "b_ada": g_b_ada.reshape(b_ada.shape),
        "g_pre_mix": g_pre_mix_g[None], "g_post_mix": g_post_mix_g[None], "g_pre_mlp": g_pre_mlp_g[None],
        "g_post_mlp": g_post_mlp_g[None], "w_in": fused["w_in"][0][None], "ret_decay": g_decay[None], "ret_gn": g_gn[None],
        "na_rpb": g_rpb[None], "w_out": fused["w_out"][0][None], "w_mlp1": fused["w_mlp1"][0][None],
        "w_mlp2": fused["w_mlp2"][0][None],
    }
    weights = dict(c_ctx=c_ctx, w_ada=w_ada, b_ada=b_ada, g_pre_mix=g_pre_mix, g_post_mix=g_post_mix,
                   g_pre_mlp=g_pre_mlp, g_post_mlp=g_post_mlp, w_in=w_in, ret_decay=ret_decay, ret_gn=ret_gn,
                   na_rpb=na_rpb, w_out=w_out, w_mlp1=w_mlp1, w_mlp2=w_mlp2)
    m_in = dict(c_ctx=m_c_ctx, w_ada=m_w_ada, b_ada=m_b_ada, g_pre_mix=m_g_pre_mix, g_post_mix=m_g_post_mix,
                g_pre_mlp=m_g_pre_mlp, g_post_mlp=m_g_post_mlp, w_in=m_w_in, ret_decay=m_ret_decay,
                ret_gn=m_ret_gn, na_rpb=m_na_rpb, w_out=m_w_out, w_mlp1=m_w_mlp1, w_mlp2=m_w_mlp2)
    v_in = dict(c_ctx=v_c_ctx, w_ada=v_w_ada, b_ada=v_b_ada, g_pre_mix=v_g_pre_mix, g_post_mix=v_g_post_mix,
                g_pre_mlp=v_g_pre_mlp, g_post_mlp=v_g_post_mlp, w_in=v_w_in, ret_decay=v_ret_decay,
                ret_gn=v_ret_gn, na_rpb=v_na_rpb, w_out=v_w_out, w_mlp1=v_w_mlp1, w_mlp2=v_w_mlp2)
    names = list(weights)
    deltas, new_m, new_v = {}, {}, {}
    def as_2d(n):
        shp = weights[n].shape
        two_d = (-1, shp[-1]) if len(shp) > 1 else (1, shp[0])
        return [a.reshape(two_d) for a in (weights[n], grads[n], m_in[n], v_in[n])]

    small = [n for n in names if n not in fused and weights[n].size <= 65536]
    updated = dict(zip(small, _adamw_small([as_2d(n) for n in small], "adamw_small")))
    for n in names:
        if n in fused:
            updated[n] = fused[n][1:]
        elif n not in updated:
            updated[n] = _adamw(*as_2d(n), "adamw_" + n)
        deltas[n], new_m[n], new_v[n] = (a.reshape(weights[n].shape) for a in updated[n])
    return (loss, grad_x, *[grads[n] for n in names], *[deltas[n] for n in names],
            *[new_m[n] for n in names], *[new_v[n] for n in names])
```

```python
import functools

import numpy as np
import jax
import jax.numpy as jnp
from jax import lax
from jax.experimental import pallas as pl
from jax.experimental.pallas import tpu as pltpu

F32 = jnp.float32
BF16 = jnp.bfloat16
MESH = pl.DeviceIdType.MESH

N_DEV = 8
LANES = 128
SUBLANES = 8
VMEM_LIMIT = 60 * 1024 * 1024

GRID_W = 64
RET_HEADS = 4
RET_DIM = 128
RET_WIDTH = RET_HEADS * RET_DIM
NA_HEADS = 8
NA_DIM = 64
NA_WIDTH = NA_HEADS * NA_DIM
NA_PAIRS = NA_HEADS // 2
NA_KH = 8
NA_KW = 16
NA_GROUP = 8
SEG = 512
ROPE_BASE = 10000.0
NORM_EPS = 1e-6
NEG_INF = -1e30
N_MOD = 6

ADAM_LR = 0.001
ADAM_B1 = 0.9
ADAM_B2 = 0.999
ADAM_EPS = 1e-08
ADAM_WD = 0.01
ADAM_STEP = 10


def _dot(a, b):
    return lax.dot_general(a, b, (((1,), (0,)), ((), ())), preferred_element_type=F32)


def _dot_nt(a, b):
    return lax.dot_general(a, b, (((1,), (1,)), ((), ())), preferred_element_type=F32)


def _dot_tn(a, b):
    return lax.dot_general(a, b, (((0,), (0,)), ((), ())), preferred_element_type=F32)


def _sigmoid(x):
    return 1.0 / (1.0 + jnp.exp(-x))


def _div_tile(n, cap, mult):
    if n <= cap:
        return n
    for t in range(cap - cap % mult, 0, -mult):
        if n % t == 0:
            return t
    raise ValueError(f"no tile for {n}")


def _params(*sem):
    return pltpu.CompilerParams(dimension_semantics=tuple(sem) if sem else None,
                                vmem_limit_bytes=VMEM_LIMIT)


def _vmem():
    return pl.BlockSpec(memory_space=pltpu.VMEM)


def _any():
    return pl.BlockSpec(memory_space=pl.ANY)


def _me_and_peers():
    x, y, c = lax.axis_index("x"), lax.axis_index("y"), lax.axis_index("c")
    me = 4 * x + 2 * y + c
    peers = []
    for m in range(1, N_DEV):
        px = 1 - x if (m >> 2) & 1 else x
        py = 1 - y if (m >> 1) & 1 else y
        pc = 1 - c if m & 1 else c
        peers.append(((px, py, pc), 4 * px + 2 * py + pc))
    return me, peers


def _exchange(src_for, dst_from, send_sems, recv_sems):
    me, peers = _me_and_peers()
    sent = []
    for i, (dev, pid) in enumerate(peers):
        cp = pltpu.make_async_remote_copy(src_ref=src_for(pid), dst_ref=dst_from(me),
                                          send_sem=send_sems.at[i], recv_sem=recv_sems.at[i],
                                          device_id=dev, device_id_type=MESH)
        cp.start()
        sent.append(cp)
    for i, (dev, pid) in enumerate(peers):
        pltpu.make_async_remote_copy(src_ref=src_for(pid), dst_ref=dst_from(pid),
                                     send_sem=send_sems.at[i], recv_sem=recv_sems.at[i],
                                     device_id=dev, device_id_type=MESH).wait_recv()
    for cp in sent:
        cp.wait_send()


SIBLING = (1,)
ICI_SAME_CORE = (2, 4, 6)
ALL_PEERS = tuple(range(1, N_DEV))


def _remote(src, dst, send_sem, recv_sem, dev):
    return pltpu.make_async_remote_copy(src_ref=src, dst_ref=dst, send_sem=send_sem, recv_sem=recv_sem,
                                        device_id=dev, device_id_type=MESH)


def _push_start(items, masks, send_sems, recv_sems):
    me, peers = _me_and_peers()
    for k, (src_for, dst_from) in enumerate(items):
        for m in masks:
            dev, pid = peers[m - 1]
            _remote(src_for(pid), dst_from(me), send_sems.at[k, m - 1], recv_sems.at[k, m - 1], dev).start()


def _push_wait_recv(items, masks, send_sems, recv_sems):
    me, peers = _me_and_peers()
    for k, (src_for, dst_from) in enumerate(items):
        for m in masks:
            dev, pid = peers[m - 1]
            _remote(src_for(pid), dst_from(pid), send_sems.at[k, m - 1], recv_sems.at[k, m - 1], dev).wait_recv()


def _push_wait_send(items, masks, send_sems, recv_sems):
    me, peers = _me_and_peers()
    for k, (src_for, dst_from) in enumerate(items):
        for m in masks:
            dev, pid = peers[m - 1]
            _remote(src_for(pid), dst_from(me), send_sems.at[k, m - 1], recv_sems.at[k, m - 1], dev).wait_send()


def _forward_start(items, send_sems, recv_sems):
    me, peers = _me_and_peers()
    sib = peers[0][0]
    for k, (blk_in, blk_out) in enumerate(items):
        for j, m in enumerate(ICI_SAME_CORE):
            pid = peers[m - 1][1]
            _remote(blk_in(pid), blk_out(pid), send_sems.at[k, j], recv_sems.at[k, j], sib).start()


def _forward_wait(items, send_sems, recv_sems):
    me, peers = _me_and_peers()
    sib = peers[0][0]
    for k, (blk_in, blk_out) in enumerate(items):
        for j, m in enumerate(ICI_SAME_CORE):
            got = peers[(m | 1) - 1][1]
            _remote(blk_in(got), blk_out(got), send_sems.at[k, j], recv_sems.at[k, j], sib).wait_recv()
    for k, (blk_in, blk_out) in enumerate(items):
        for j, m in enumerate(ICI_SAME_CORE):
            pid = peers[m - 1][1]
            _remote(blk_in(pid), blk_out(pid), send_sems.at[k, j], recv_sems.at[k, j], sib).wait_send()


def _mod_gather(c, c_ctx, w_ada, b_ada, w_in_t, w_out, w1, w2):
    B, D = c.shape
    ncol = w_ada.shape[1]
    rows = SUBLANES * N_DEV + SUBLANES

    def body(c_ref, cc_ref, w_ref, b_ref, win_ref, wout_ref, w1_ref, w2_ref,
             s_ref, m_ref, gin_ref, wout_b, w1_b, w2_b,
             win_b, msend, send1, recv1, send2, recv2, wsend, wrecv, fsend, frecv, lsem):
        me, _ = _me_and_peers()
        win_b[...] = win_ref[...].astype(BF16)
        block = _row_block(gin_ref, w_in_t.shape[0])
        gather = [(lambda p: win_b, block)]
        own = pltpu.make_async_copy(win_b, block(me), lsem.at[0])
        cv = c_ref[...]
        slot = jnp.concatenate([cv * _sigmoid(cv), jnp.zeros((SUBLANES - B, D), F32)], axis=0)
        my_rows = pl.ds(pl.multiple_of(me * SUBLANES, SUBLANES), SUBLANES)
        s_ref[my_rows, :] = slot
        ccv = cc_ref[...]
        s_ref[SUBLANES * N_DEV:, :] = jnp.concatenate(
            [ccv * _sigmoid(ccv), jnp.zeros((SUBLANES - 1, D), F32)], axis=0)

        def rows_of(p):
            return s_ref.at[pl.ds(pl.multiple_of(p * SUBLANES, SUBLANES), SUBLANES), :]

        _exchange(lambda p: rows_of(me), rows_of, send1, recv1)
        own.start()
        _push_start(gather, SIBLING + ICI_SAME_CORE, wsend, wrecv)
        wout_b[...] = wout_ref[...].astype(BF16)
        w1_b[...] = w1_ref[...].astype(BF16)
        w2_b[...] = w2_ref[...].astype(BF16)
        b_loc = b_ref[:, pl.ds(pl.multiple_of(me * ncol, ncol), ncol)]
        mods = _dot(s_ref[...], w_ref[...]) + b_loc
        for p in range(N_DEV):
            msend[p] = jnp.concatenate([mods[p * SUBLANES:(p + 1) * SUBLANES], mods[N_DEV * SUBLANES:]], axis=0)
        m_ref[me] = msend[me]
        columns = [(lambda p: msend.at[p], lambda p: m_ref.at[p])]
        _push_start(columns, ALL_PEERS, send2, recv2)
        _push_wait_recv(gather, ICI_SAME_CORE, wsend, wrecv)
        relay = [(block, block)]
        _forward_start(relay, fsend, frecv)
        _push_wait_recv(columns, ALL_PEERS, send2, recv2)
        _push_wait_recv(gather, SIBLING, wsend, wrecv)
        _forward_wait(relay, fsend, frecv)
        _push_wait_send(columns, ALL_PEERS, send2, recv2)
        _push_wait_send(gather, SIBLING + ICI_SAME_CORE, wsend, wrecv)
        own.wait()

    return pl.pallas_call(
        body, name="mod_gather",
        out_shape=(jax.ShapeDtypeStruct((rows, D), F32), jax.ShapeDtypeStruct((N_DEV, 2 * SUBLANES, ncol), F32),
                   jax.ShapeDtypeStruct((N_DEV * w_in_t.shape[0], D), BF16),
                   jax.ShapeDtypeStruct(w_out.shape, BF16), jax.ShapeDtypeStruct(w1.shape, BF16),
                   jax.ShapeDtypeStruct(w2.shape, BF16)),
        in_specs=[_vmem()] * 8, out_specs=(_vmem(), _vmem(), _any(), _vmem(), _vmem(), _vmem()),
        scratch_shapes=[pltpu.VMEM(w_in_t.shape, BF16), pltpu.VMEM((N_DEV, 2 * SUBLANES, ncol), F32)]
                       + [pltpu.SemaphoreType.DMA((N_DEV - 1,))] * 2
                       + [pltpu.SemaphoreType.DMA((1, N_DEV - 1))] * 4 + [pltpu.SemaphoreType.DMA((1, 3))] * 2
                       + [pltpu.SemaphoreType.DMA((1,))],
        compiler_params=pltpu.CompilerParams(vmem_limit_bytes=VMEM_LIMIT),
    )(c, c_ctx.reshape(1, D), w_ada, b_ada, w_in_t, w_out, w1, w2)


def _row_block(ref, rows):
    return lambda p: ref.at[pl.ds(pl.multiple_of(p * rows, 2 * SUBLANES), rows), :]


def _col_block(ref, cols):
    return lambda p: ref.at[:, pl.ds(pl.multiple_of(p * cols, LANES), cols)]


def _slot(ref):
    return lambda p: ref.at[p]


def _grid_call(body, *, name, grid, out_shape, in_specs, out_specs, scratch_shapes, args):
    return pl.pallas_call(
        body, name=name, grid=grid, out_shape=tuple(out_shape), in_specs=list(in_specs), out_specs=tuple(out_specs),
        scratch_shapes=list(scratch_shapes), compiler_params=_params(*(("arbitrary",) * len(grid))),
    )(*args)


def _token_tiles(n_ctx, tm):
    nct = n_ctx // tm

    def ctx_spec(D):
        return pl.BlockSpec((None, tm, D), lambda b, t: (b, jnp.minimum(t, nct - 1), 0))

    def lat_spec(D):
        return pl.BlockSpec((None, tm, D), lambda b, t: (b, jnp.maximum(t - nct, 0), 0))

    return nct, ctx_spec, lat_spec


def _inproj_fwd(x, ctx, modl, g1, w_in_t):
    B, N, D = x.shape
    n_ctx = ctx.shape[1]
    T = n_ctx + N
    nw = w_in_t.shape[0]
    tm = _div_tile(n_ctx, 256, 16)
    nct, ctx_spec, lat_spec = _token_tiles(n_ctx, tm)

    def body(c_ref, x_ref, sh_ref, sc_ref, g_ref, w_ref, h_ref, p_ref):
        x = jnp.where(pl.program_id(1) < nct, c_ref[...], x_ref[...])
        r = lax.rsqrt(jnp.mean(x * x, axis=-1, keepdims=True) + NORM_EPS)
        h = ((x * r) * g_ref[...]) * (1.0 + sc_ref[...]) + sh_ref[...]
        hb = h.astype(BF16)
        h_ref[...] = hb
        p_ref[...] = _dot_nt(hb, w_ref[...]).astype(BF16)

    def mrow(b, t):
        return jnp.where(t < nct, B, b)

    return _grid_call(
        body, name="inproj_fwd", grid=(B, T // tm),
        out_shape=(jax.ShapeDtypeStruct((B, T, D), BF16), jax.ShapeDtypeStruct((B, T, nw), BF16)),
        in_specs=[ctx_spec(D), lat_spec(D),
                  pl.BlockSpec((None, None, 1, D), lambda b, t: (mrow(b, t), 0, 0, 0)),
                  pl.BlockSpec((None, None, 1, D), lambda b, t: (mrow(b, t), 1, 0, 0)),
                  pl.BlockSpec((1, D), lambda b, t: (0, 0)),
                  pl.BlockSpec((nw, D), lambda b, t: (0, 0))],
        out_specs=(pl.BlockSpec((None, tm, D), lambda b, t: (b, t, 0)),
                   pl.BlockSpec((None, tm, nw), lambda b, t: (b, t, 0))),
        scratch_shapes=[], args=(ctx, x, modl, modl, g1, w_in_t))


def _swap32(x):
    lane = lax.broadcasted_iota(jnp.int32, x.shape, 1)
    return jnp.where((lane % 64) < 32, pltpu.roll(x, 96, 1), pltpu.roll(x, 32, 1))


def _rope(x, cos, sin):
    return x * cos + _swap32(x) * sin


def _unrope(dy, cos, sin):
    return dy * cos + _swap32(dy * sin)


def _ret_weights(lgf, lgb, dist):
    return jnp.exp(jnp.where(dist >= 0.0, lgf * dist, -lgb * dist))


class _RetDecay:
    def __init__(self, lgf, lgb, rows):
        r = lax.broadcasted_iota(jnp.int32, (rows, RET_DIM), 0).astype(F32)
        self.head = r + 1.0
        self.tail = (rows - 1.0) - r
        self.q_f = jnp.exp(lgf * self.head)
        self.k_f = jnp.exp(lgf * self.tail)
        self.q_b = jnp.exp(lgb * self.tail)
        self.k_b = jnp.exp(lgb * self.head)


def _ret_states(kf32, vs, lgf, lgb, C, c, nt, hf, hb, hfa=None, hba=None):
    dec = _RetDecay(lgf, lgb, c)
    dec_c = _RetDecay(lgf, lgb, C)
    step_f = jnp.exp(jnp.zeros((RET_DIM, RET_DIM), F32) + lgf * c)
    step_b = jnp.exp(jnp.zeros((RET_DIM, RET_DIM), F32) + lgb * c)

    def upd(rows, kdec):
        return _dot_tn((kf32[rows, :] * kdec).astype(BF16), vs[rows, :])

    def lat(t):
        return slice(C + t * c, C + (t + 1) * c)

    state = upd(slice(0, C), dec_c.k_f)
    aged = jnp.zeros_like(state)
    for t in range(nt):
        hf[t] = state.astype(BF16)
        if hfa is not None:
            hfa[t] = aged
        if t < nt - 1:
            aged = step_f * (aged + c * state)
            state = step_f * state + upd(lat(t), dec.k_f)
    state = upd(slice(0, C), dec_c.k_b)
    aged = jnp.zeros_like(state)
    for t in range(nt - 1, -1, -1):
        hb[t] = state.astype(BF16)
        if hba is not None:
            hba[t] = aged
        if t > 0:
            aged = step_b * (aged + c * state)
            state = step_b * state + upd(lat(t), dec.k_b)
    return dec, dec_c, step_f, step_b


def _ret_fwd(proj, cos, sin, lg, gn, n_ctx):
    B, T, _ = proj.shape
    C = n_ctx
    N = T - C
    c = _div_tile(N, 256, 16)
    nt = N // c
    scale = RET_DIM ** -0.5

    def body(lg_ref, q_ref, k_ref, vs, g_ref, cos_ref, sin_ref, gn_ref, o_ref, lat_ref, qr_ref, kf32,
             qs, ks, hf, hb):
        h = pl.program_id(1)
        lgf = lg_ref[0, h]
        lgb = lg_ref[1, h]
        for rows in [slice(0, C)] + [slice(C + t * c, C + (t + 1) * c) for t in range(nt)]:
            cosb = cos_ref[rows, :]
            sinb = sin_ref[rows, :]
            qr = _rope(q_ref[rows, :].astype(F32), cosb, sinb) * scale
            qr_ref[rows, :] = qr
            qs[rows, :] = qr.astype(BF16)
            kr = _rope(k_ref[rows, :].astype(F32), cosb, sinb)
            kf32[rows, :] = kr
            ks[rows, :] = kr.astype(BF16)
        gnv = gn_ref[...]
        dec, _, _, _ = _ret_states(kf32, vs, lgf, lgb, C, c, nt, hf, hb)
        rc = (lax.broadcasted_iota(jnp.int32, (c, c), 0) - lax.broadcasted_iota(jnp.int32, (c, c), 1)).astype(F32)
        w_diag = _ret_weights(lgf, lgb, rc)
        for t in range(nt):
            rows = slice(C + t * c, C + (t + 1) * c)
            qt = qs[rows, :]
            s = _dot_nt(qt, ks[rows, :])
            o = (_dot((s * w_diag).astype(BF16), vs[rows, :])
                 + dec.q_f * _dot(qt, hf[t]) + dec.q_b * _dot(qt, hb[t]))
            o_ref[t * c:(t + 1) * c, :] = o
            mu = jnp.mean(o, axis=-1, keepdims=True)
            oc = o - mu
            var = jnp.mean(oc * oc, axis=-1, keepdims=True)
            yh = oc * lax.rsqrt(var + NORM_EPS)
            g = g_ref[rows, :].astype(F32)
            lat_ref[t * c:(t + 1) * c, :] = ((yh * gnv) * (g * _sigmoid(g))).astype(BF16)

    def col(seg):
        return pl.BlockSpec((None, T, RET_DIM), lambda b, h, seg=seg: (b, 0, seg * RET_HEADS + h))

    return _grid_call(
        body, name="ret_fwd", grid=(B, RET_HEADS),
        out_shape=(jax.ShapeDtypeStruct((B, N, RET_WIDTH), F32), jax.ShapeDtypeStruct((B, N, RET_WIDTH), BF16),
                   jax.ShapeDtypeStruct((B, T, RET_WIDTH), F32), jax.ShapeDtypeStruct((B, T, RET_WIDTH), F32)),
        in_specs=[pl.BlockSpec(memory_space=pltpu.SMEM), col(0), col(1), col(2), col(3),
                  pl.BlockSpec((T, RET_DIM), lambda b, h: (0, 0)), pl.BlockSpec((T, RET_DIM), lambda b, h: (0, 0)),
                  pl.BlockSpec((1, RET_DIM), lambda b, h: (0, h))],
        out_specs=(pl.BlockSpec((None, N, RET_DIM), lambda b, h: (b, 0, h)),
                   pl.BlockSpec((None, N, RET_DIM), lambda b, h: (b, 0, h)),
                   pl.BlockSpec((None, T, RET_DIM), lambda b, h: (b, 0, h)),
                   pl.BlockSpec((None, T, RET_DIM), lambda b, h: (b, 0, h))),
        scratch_shapes=[pltpu.VMEM((T, RET_DIM), BF16)] * 2 + [pltpu.VMEM((nt, RET_DIM, RET_DIM), BF16)] * 2,
        args=(lg, proj, proj, proj, proj, cos, sin, gn))


def _ret_bwd(proj, q_rot, k_rot, cos, sin, lg, gn, o, dlat, n_ctx):
    B, T, _ = proj.shape
    C = n_ctx
    N = T - C
    c = _div_tile(N, 256, 16)
    nt = N // c
    scale = RET_DIM ** -0.5

    def lat(t):
        return slice(C + t * c, C + (t + 1) * c)

    def body(lg_ref, qf32, kf32, vs, g_ref, cos_ref, sin_ref, gn_ref, o_ref, dl_ref,
             d_ref, dgn_ref, dlg_ref, qs, ks, dos, hf, hb, hfa, hba, gf_s, gb_s):
        h = pl.program_id(1)
        lgf = lg_ref[0, h]
        lgb = lg_ref[1, h]
        gnv = gn_ref[...]

        def fold(a):
            return jnp.sum(a.reshape(a.shape[0] // SUBLANES, SUBLANES, a.shape[1]), axis=0)

        for rows in [slice(0, C)] + [lat(t) for t in range(nt)]:
            qs[rows, :] = qf32[rows, :].astype(BF16)
            ks[rows, :] = kf32[rows, :].astype(BF16)

        dgn = jnp.zeros((1, RET_DIM), F32)
        for t in range(nt):
            lrows = slice(t * c, (t + 1) * c)
            ov = o_ref[lrows, :]
            mu = jnp.mean(ov, axis=-1, keepdims=True)
            oc = ov - mu
            var = jnp.mean(oc * oc, axis=-1, keepdims=True)
            rstd = lax.rsqrt(var + NORM_EPS)
            yh = oc * rstd
            g = g_ref[lat(t), :].astype(F32)
            sg = _sigmoid(g)
            dl = dl_ref[lrows, :]
            d_ref[3, lat(t), :] = (dl * (yh * gnv) * (sg * (1.0 + g * (1.0 - sg)))).astype(BF16)
            dls = dl * (g * sg)
            dgn = dgn + jnp.sum(dls * yh, axis=0, keepdims=True)
            dyh = dls * gnv
            do = rstd * (dyh - jnp.mean(dyh, axis=-1, keepdims=True)
                         - yh * jnp.mean(dyh * yh, axis=-1, keepdims=True))
            dos[lrows, :] = do.astype(BF16)
        dgn_ref[...] = jnp.concatenate([dgn, jnp.zeros((SUBLANES - 1, RET_DIM), F32)], axis=0)
        d_ref[3, 0:C, :] = jnp.zeros((C, RET_DIM), BF16)
        d_ref[0, 0:C, :] = jnp.zeros((C, RET_DIM), BF16)

        dec, dec_c, step_f, step_b = _ret_states(kf32, vs, lgf, lgb, C, c, nt, hf, hb, hfa, hba)

        def zmat(t, qdec):
            return _dot_tn((qf32[lat(t), :] * qdec).astype(BF16), dos[t * c:(t + 1) * c, :])

        acc3f = jnp.zeros((RET_DIM, RET_DIM), F32)
        acc3b = jnp.zeros((RET_DIM, RET_DIM), F32)
        state = jnp.zeros((RET_DIM, RET_DIM), F32)
        for t in range(nt - 1, -1, -1):
            gf_s[t] = state.astype(BF16)
            z = zmat(t, dec.q_f)
            acc3f = acc3f + hfa[t] * z
            state = step_f * state + z
        gctx_f = state.astype(BF16)
        state = jnp.zeros((RET_DIM, RET_DIM), F32)
        for t in range(nt):
            gb_s[t] = state.astype(BF16)
            z = zmat(t, dec.q_b)
            acc3b = acc3b + hba[t] * z
            state = step_b * state + z
        gctx_b = state.astype(BF16)

        rc = (lax.broadcasted_iota(jnp.int32, (c, c), 0) - lax.broadcasted_iota(jnp.int32, (c, c), 1)).astype(F32)
        w_diag = _ret_weights(lgf, lgb, rc)
        wg_f = jnp.where(rc >= 0.0, w_diag * rc, 0.0)
        wg_b = jnp.where(rc < 0.0, -w_diag * rc, 0.0)
        accf = jnp.zeros((SUBLANES, RET_DIM), F32)
        accb = jnp.zeros((SUBLANES, RET_DIM), F32)
        gdf = jnp.zeros((SUBLANES, c), F32)
        gdb = jnp.zeros((SUBLANES, c), F32)
        for t in range(nt):
            rows = lat(t)
            qt = qs[rows, :]
            kt = ks[rows, :]
            vt = vs[rows, :]
            dot = dos[t * c:(t + 1) * c, :]
            s = _dot_nt(qt, kt)
            dp = _dot_nt(dot, vt)
            dv = _dot_tn((s * w_diag).astype(BF16), dot)
            ds = (dp * w_diag).astype(BF16)
            dq = _dot(ds, kt)
            dk = _dot_tn(ds, qt)
            gs = dp * s
            gdf = gdf + fold(gs * wg_f)
            gdb = gdb + fold(gs * wg_b)
            qv = qf32[rows, :]
            kv = kf32[rows, :]
            dq_f = dec.q_f * _dot_nt(dot, hf[t])
            dq_b = dec.q_b * _dot_nt(dot, hb[t])
            dk_f = dec.k_f * _dot_nt(vt, gf_s[t])
            dk_b = dec.k_b * _dot_nt(vt, gb_s[t])
            accf = accf + fold(dec.head * dq_f * qv) + fold(dec.tail * dk_f * kv)
            accb = accb + fold(dec.tail * dq_b * qv) + fold(dec.head * dk_b * kv)
            dv = dv + dec.k_f * _dot(kt, gf_s[t]) + dec.k_b * _dot(kt, gb_s[t])
            cosb = cos_ref[rows, :]
            sinb = sin_ref[rows, :]
            d_ref[0, rows, :] = _unrope((dq + dq_f + dq_b) * scale, cosb, sinb).astype(BF16)
            d_ref[1, rows, :] = _unrope(dk + dk_f + dk_b, cosb, sinb).astype(BF16)
            d_ref[2, rows, :] = dv.astype(BF16)
        kc = ks[0:C, :]
        vc = vs[0:C, :]
        kcv = kf32[0:C, :]
        dkc_f = dec_c.k_f * _dot_nt(vc, gctx_f)
        dkc_b = dec_c.k_b * _dot_nt(vc, gctx_b)
        accf = accf + fold(dec_c.tail * dkc_f * kcv)
        accb = accb + fold(dec_c.head * dkc_b * kcv)
        d_ref[1, 0:C, :] = (dkc_f + dkc_b).astype(BF16)
        d_ref[2, 0:C, :] = (dec_c.k_f * _dot(kc, gctx_f) + dec_c.k_b * _dot(kc, gctx_b)).astype(BF16)
        gf = jnp.sum(gdf) + jnp.sum(accf) + jnp.sum(acc3f)
        gb = jnp.sum(gdb) + jnp.sum(accb) + jnp.sum(acc3b)
        row = lax.broadcasted_iota(jnp.int32, (SUBLANES, LANES), 0)
        dlg_ref[...] = jnp.where(row == 0, gf, jnp.where(row == 1, gb, 0.0))

    def col(seg):
        return pl.BlockSpec((None, T, RET_DIM), lambda b, h, seg=seg: (b, 0, seg * RET_HEADS + h))

    def head(rows):
        return pl.BlockSpec((None, rows, RET_DIM), lambda b, h: (b, 0, h))

    return _grid_call(
        body, name="ret_bwd", grid=(B, RET_HEADS),
        out_shape=(jax.ShapeDtypeStruct((B, 4, T, RET_WIDTH), BF16),
                   jax.ShapeDtypeStruct((B, SUBLANES, RET_WIDTH), F32),
                   jax.ShapeDtypeStruct((B, RET_HEADS, SUBLANES, LANES), F32)),
        in_specs=[pl.BlockSpec(memory_space=pltpu.SMEM), head(T), head(T), col(2), col(3),
                  pl.BlockSpec((T, RET_DIM), lambda b, h: (0, 0)), pl.BlockSpec((T, RET_DIM), lambda b, h: (0, 0)),
                  pl.BlockSpec((1, RET_DIM), lambda b, h: (0, h)), head(N), head(N)],
        out_specs=(pl.BlockSpec((None, 4, T, RET_DIM), lambda b, h: (b, 0, 0, h)),
                   pl.BlockSpec((None, SUBLANES, RET_DIM), lambda b, h: (b, 0, h)),
                   pl.BlockSpec((None, None, SUBLANES, LANES), lambda b, h: (b, h, 0, 0))),
        scratch_shapes=[pltpu.VMEM((T, RET_DIM), BF16)] * 2 + [pltpu.VMEM((N, RET_DIM), BF16)]
                       + [pltpu.VMEM((nt, RET_DIM, RET_DIM), BF16)] * 2 + [pltpu.VMEM((nt, RET_DIM, RET_DIM), F32)] * 2
                       + [pltpu.VMEM((nt, RET_DIM, RET_DIM), BF16)] * 2,
        args=(lg, q_rot, k_rot, proj, proj, cos, sin, gn, o, dlat))


def _na_geometry(rows):
    kh = min(NA_KH, rows)
    return kh, kh * GRID_W


def _pair_select():
    lane = lax.broadcasted_iota(jnp.int32, (2 * GRID_W, LANES), 1)
    row = lax.broadcasted_iota(jnp.int32, (2 * GRID_W, LANES), 0)
    return (lane >= NA_DIM) == (row >= GRID_W)


def _pair_bias(bias_ref, dr0, kh):
    return jnp.concatenate(
        [jnp.concatenate([bias_ref[e, pl.ds(dr0 + 2 * m, 1)].reshape(GRID_W, LANES) for m in range(kh // 2)], axis=1)
         for e in range(2)], axis=0)


def _na_softmax(s_loc, s_ctx):
    mx = jnp.maximum(jnp.max(s_loc, axis=-1, keepdims=True), jnp.max(s_ctx, axis=-1, keepdims=True))
    p_loc = jnp.exp(s_loc - mx)
    p_ctx = jnp.exp(s_ctx - mx)
    den = jnp.sum(p_loc, axis=-1, keepdims=True) + jnp.sum(p_ctx, axis=-1, keepdims=True)
    return p_loc, p_ctx, den


def _na_fwd(proj, bias2, n_ctx):
    assert proj.dtype == BF16
    B, T, _ = proj.shape
    C = n_ctx
    N = T - C
    R = N // GRID_W
    kh, nk = _na_geometry(R)
    scale = NA_DIM ** -0.5
    base = (4 * RET_WIDTH) // LANES

    def body(q_ref, kb16, vb16, bias_ref, out_ref, p_ref):
        kc = kb16[0:C, :]
        vc = vb16[0:C, :]
        lane = lax.broadcasted_iota(jnp.int32, (GRID_W, LANES), 1)
        sel2 = _pair_select()

        def group(gi, carry):
            pre = []
            for u in range(NA_GROUP):
                r = gi * NA_GROUP + u
                bs = jnp.clip(r - kh // 2, 0, R - kh)
                dr0 = bs - r + (NA_KH - 1)
                q = q_ref[pl.ds(pl.multiple_of(C + r * GRID_W, GRID_W), GRID_W), :].astype(F32) * scale
                q2 = jnp.where(sel2, jnp.concatenate([q, q], axis=0), 0.0).astype(BF16)
                band = pl.ds(pl.multiple_of(C + bs * GRID_W, GRID_W), nk)
                s_loc = _dot_nt(q2, kb16[band, :]) + _pair_bias(bias_ref, dr0, kh)
                s_ctx = _dot_nt(q2, kc)
                pre.append((r, band, s_loc, s_ctx))
            mid = [(r, band) + _na_softmax(s_loc, s_ctx) for r, band, s_loc, s_ctx in pre]
            for r, band, p_loc, p_ctx, den in mid:
                inv = 1.0 / den
                pb_loc = (p_loc * inv).astype(BF16)
                pb_ctx = (p_ctx * inv).astype(BF16)
                p_ref[r, :, 0:nk] = pb_loc
                p_ref[r, :, nk:] = pb_ctx
                o2 = _dot(pb_loc, vb16[band, :]) + _dot(pb_ctx, vc)
                out_ref[pl.ds(pl.multiple_of(r * GRID_W, GRID_W), GRID_W), :] = jnp.where(
                    lane < NA_DIM, o2[:GRID_W], o2[GRID_W:]).astype(BF16)
            return carry

        lax.fori_loop(0, R // NA_GROUP, group, 0)

    def col(seg):
        return pl.BlockSpec((None, T, LANES), lambda b, p, seg=seg: (b, 0, base + seg * NA_PAIRS + p))

    return _grid_call(
        body, name="na_fwd", grid=(B, NA_PAIRS),
        out_shape=(jax.ShapeDtypeStruct((B, N, NA_WIDTH), BF16),
                   jax.ShapeDtypeStruct((B, NA_PAIRS, R, 2 * GRID_W, nk + C), BF16)),
        in_specs=[col(0), col(1), col(2),
                  pl.BlockSpec((2, 2 * NA_KH - 2, GRID_W, LANES), lambda b, p: (p, 0, 0, 0))],
        out_specs=(pl.BlockSpec((None, N, LANES), lambda b, p: (b, 0, p)),
                   pl.BlockSpec((None, None, R, 2 * GRID_W, nk + C), lambda b, p: (b, p, 0, 0, 0))),
        scratch_shapes=[],
        args=(proj, proj, proj, bias2))


def _na_bwd(proj, probs, dlat, n_ctx):
    assert proj.dtype == BF16
    B, T, _ = proj.shape
    C = n_ctx
    N = T - C
    R = N // GRID_W
    kh, nk = _na_geometry(R)
    scale = NA_DIM ** -0.5
    base = (4 * RET_WIDTH) // LANES

    def body(q_ref, kb16, vb16, p_ref, dl_ref, d_ref, db_ref, dkv):
        b = pl.program_id(1)
        kc = kb16[0:C, :]
        vc = vb16[0:C, :]
        lane = lax.broadcasted_iota(jnp.int32, (GRID_W, LANES), 1)
        dkv[...] = jnp.zeros(dkv.shape, F32)
        d_ref[0, 0:C, :] = jnp.zeros((C, LANES), BF16)

        @pl.when(b == 0)
        def _():
            db_ref[...] = jnp.zeros(db_ref.shape, F32)

        sel2 = _pair_select()

        def group(gi, carry):
            pre = []
            for u in range(NA_GROUP):
                r = gi * NA_GROUP + u
                bs = jnp.clip(r - kh // 2, 0, R - kh)
                dr0 = bs - r + (NA_KH - 1)
                q = q_ref[pl.ds(pl.multiple_of(C + r * GRID_W, GRID_W), GRID_W), :].astype(F32) * scale
                do = dl_ref[pl.ds(pl.multiple_of(r * GRID_W, GRID_W), GRID_W), :]
                q2 = jnp.where(sel2, jnp.concatenate([q, q], axis=0), 0.0).astype(BF16)
                do2 = jnp.where(sel2, jnp.concatenate([do, do], axis=0), 0.0).astype(BF16)
                band = pl.ds(pl.multiple_of(C + bs * GRID_W, GRID_W), nk)
                dp_loc = _dot_nt(do2, vb16[band, :])
                dp_ctx = _dot_nt(do2, vc)
                pre.append((r, dr0, band, q2, do2, dp_loc, dp_ctx))
            mid = []
            for r, dr0, band, q2, do2, dp_loc, dp_ctx in pre:
                pb_loc = p_ref[r, :, 0:nk]
                pb_ctx = p_ref[r, :, nk:]
                p_loc = pb_loc.astype(F32)
                p_ctx = pb_ctx.astype(F32)
                delta = (jnp.sum(p_loc * dp_loc, axis=-1, keepdims=True)
                         + jnp.sum(p_ctx * dp_ctx, axis=-1, keepdims=True))
                ds_loc = p_loc * (dp_loc - delta)
                ds_ctx = p_ctx * (dp_ctx - delta)
                mid.append((r, dr0, band, q2, do2, pb_loc, pb_ctx, ds_loc, ds_ctx))
            for r, dr0, band, q2, do2, pb_loc, pb_ctx, ds_loc, ds_ctx in mid:
                dsb_loc = ds_loc.astype(BF16)
                dsb_ctx = ds_ctx.astype(BF16)
                dq2 = _dot(dsb_loc, kb16[band, :]) + _dot(dsb_ctx, kc)
                d_ref[0, pl.ds(pl.multiple_of(C + r * GRID_W, GRID_W), GRID_W), :] = (jnp.where(
                    lane < NA_DIM, dq2[:GRID_W], dq2[GRID_W:]) * scale).astype(BF16)
                dkv[0, band, :] += _dot_tn(dsb_loc, q2)
                dkv[1, band, :] += _dot_tn(pb_loc, do2)
                dkv[0, 0:C, :] += _dot_tn(dsb_ctx, q2)
                dkv[1, 0:C, :] += _dot_tn(pb_ctx, do2)
                for e in range(2):
                    for m in range(kh // 2):
                        db_ref[e, pl.ds(dr0 + 2 * m, 1)] += ds_loc[e * GRID_W:(e + 1) * GRID_W,
                                                                   m * LANES:(m + 1) * LANES].reshape(1, GRID_W, LANES)
            return carry

        lax.fori_loop(0, R // NA_GROUP, group, 0)
        d_ref[1] = dkv[0].astype(BF16)
        d_ref[2] = dkv[1].astype(BF16)

    def col(seg):
        return pl.BlockSpec((None, T, LANES), lambda p, b, seg=seg: (b, 0, base + seg * NA_PAIRS + p))

    return _grid_call(
        body, name="na_bwd", grid=(NA_PAIRS, B),
        out_shape=(jax.ShapeDtypeStruct((B, 3, T, NA_WIDTH), BF16),
                   jax.ShapeDtypeStruct((NA_HEADS, 2 * NA_KH - 2, GRID_W, LANES), F32)),
        in_specs=[col(0), col(1), col(2),
                  pl.BlockSpec((None, None, R, 2 * GRID_W, nk + C), lambda p, b: (b, p, 0, 0, 0)),
                  pl.BlockSpec((None, N, LANES), lambda p, b: (b, 0, p))],
        out_specs=(pl.BlockSpec((None, 3, T, LANES), lambda p, b: (b, 0, 0, p)),
                   pl.BlockSpec((2, 2 * NA_KH - 2, GRID_W, LANES), lambda p, b: (p, 0, 0, 0))),
        scratch_shapes=[pltpu.VMEM((2, T, LANES), F32)],
        args=(proj, proj, proj, probs, dlat))


def _split3(a):
    hi = a.astype(BF16)
    r1 = a - hi.astype(F32)
    mid = r1.astype(BF16)
    lo = (r1 - mid.astype(F32)).astype(BF16)
    return hi, mid, lo


def _rpb_reduce(dbias2, onehot2):
    rows = dbias2.shape[0] * dbias2.shape[1]
    flat = dbias2.reshape(rows, GRID_W * LANES)

    def body(a_ref, oh_ref, o_ref):
        hi, mid, lo = _split3(a_ref[...])
        oh = oh_ref[...]
        o_ref[...] = _dot(hi, oh) + _dot(mid, oh) + _dot(lo, oh)

    return pl.pallas_call(
        body, name="rpb_reduce", out_shape=jax.ShapeDtypeStruct((rows, LANES), F32),
        in_specs=[_vmem(), _vmem()], out_specs=_vmem(),
        compiler_params=pltpu.CompilerParams(vmem_limit_bytes=VMEM_LIMIT),
    )(flat, onehot2)


def _dense_core(lat_ret, lat_na, x, tgt, modl, g_post_mix, g_pre_mlp, g_post_mlp, w_out, w1, w2):
    B, N, D = x.shape
    F = w1.shape[1]
    wout_rows, w1_cols, w2_rows = w_out.shape[0] // N_DEV, w1.shape[1] // N_DEV, w2.shape[0] // N_DEV
    mixw = w_out.shape[0]
    half = mixw // 2
    tm = _div_tile(N, 256, 16)
    nt = N // tm
    fc = _div_tile(F, 1024, LANES)

    def body(lr_ref, ln_ref, x_ref, t_ref, gt1_ref, sh2_ref, sc2_ref, gt2_ref, gpm_ref, gpre_ref, gpo_ref,
             wout_part, w1_part, w2_part,
             dy1_ref, dlr_ref, dln_ref, dmix_ref, h2_ref, a_ref, du_ref, dz_ref, red_ref, wout_hbm, w1_hbm, w2_hbm,
             wout_v, w1_v, w2_v, u_s, sems, fsend, frecv):
        @pl.when((pl.program_id(0) == 0) & (pl.program_id(1) == 0))
        def _():
            relay = [(_row_block(wout_part, wout_rows), _row_block(wout_hbm, wout_rows)),
                     (_col_block(w1_part, w1_cols), _col_block(w1_hbm, w1_cols)),
                     (_row_block(w2_part, w2_rows), _row_block(w2_hbm, w2_rows))]
            _forward_start(relay, fsend, frecv)
            _forward_wait(relay, fsend, frecv)
            cps = [pltpu.make_async_copy(wout_hbm, wout_v, sems.at[0]),
                   pltpu.make_async_copy(w1_hbm, w1_v, sems.at[1]),
                   pltpu.make_async_copy(w2_hbm, w2_v, sems.at[2])]
            for cp in cps:
                cp.start()
            for cp in cps:
                cp.wait()

        @pl.when(pl.program_id(1) == 0)
        def _():
            red_ref[...] = jnp.zeros(red_ref.shape, F32)

        gt1 = gt1_ref[...]
        sh2 = sh2_ref[...]
        sc2 = sc2_ref[...]
        gt2 = gt2_ref[...]
        gpm = gpm_ref[...]
        gpre = gpre_ref[...]
        gpo = gpo_ref[...]

        def rowmean(a):
            return jnp.mean(a, axis=-1, keepdims=True)

        def colsum(a):
            return jnp.sum(a, axis=0, keepdims=True)

        mix_gain = gt1 * gpm
        mlp_in_gain = gpre * (1.0 + sc2)
        mlp_out_gain = gt2 * gpo
        mix = _dot(lr_ref[...], wout_v[0:half, :]) + _dot(ln_ref[...], wout_v[half:, :])
        x = x_ref[...]
        rm = lax.rsqrt(rowmean(mix * mix) + NORM_EPS)
        mh = mix * rm
        y1 = x + mh * mix_gain
        r1 = lax.rsqrt(rowmean(y1 * y1) + NORM_EPS)
        xh = y1 * r1
        h2b = (xh * mlp_in_gain + sh2).astype(BF16)
        h2_ref[...] = h2b
        z = jnp.zeros((tm, D), F32)
        for c0 in range(0, F, fc):
            u = _dot(h2b, w1_v[:, c0:c0 + fc])
            u_s[:, c0:c0 + fc] = u
            ru = jnp.maximum(u, 0.0)
            ab = (ru * ru).astype(BF16)
            a_ref[:, c0:c0 + fc] = ab
            z = z + _dot(ab, w2_v[c0:c0 + fc, :])
        r2 = lax.rsqrt(rowmean(z * z) + NORM_EPS)
        zh = z * r2
        y2 = y1 + zh * mlp_out_gain
        err = y2 - t_ref[...]
        loss = 0.5 * jnp.sum(rowmean(err * err))
        dy2 = err * (1.0 / D)
        s_out = colsum(dy2 * zh)
        red_ref[2:3, :] += s_out * gpo
        red_ref[6:7, :] += s_out * gt2
        dzh = dy2 * mlp_out_gain
        dz = r2 * (dzh - zh * rowmean(dzh * zh))
        dzb = dz.astype(BF16)
        dz_ref[...] = dzb
        dh2 = jnp.zeros((tm, D), F32)
        for c0 in range(0, F, fc):
            da = _dot_nt(dzb, w2_v[c0:c0 + fc, :])
            dub = (da * (2.0 * jnp.maximum(u_s[:, c0:c0 + fc], 0.0))).astype(BF16)
            du_ref[:, c0:c0 + fc] = dub
            dh2 = dh2 + _dot_nt(dub, w1_v[:, c0:c0 + fc])
        s_in = colsum(dh2 * xh)
        red_ref[3:4, :] += s_in * gpre
        red_ref[4:5, :] += colsum(dh2)
        red_ref[5:6, :] += s_in * (1.0 + sc2)
        dxh = dh2 * mlp_in_gain
        dy1 = dy2 + r1 * (dxh - xh * rowmean(dxh * xh))
        dy1_ref[...] = dy1
        s_mix = colsum(dy1 * mh)
        red_ref[0:1, :] += s_mix * gpm
        red_ref[1:2, :] += s_mix * gt1
        dmh = dy1 * mix_gain
        dmix = (rm *(dmh - mh * rowmean(dmh * mh))).astype(BF16)
        dmix_ref[...] = dmix
        dlr_ref[...] = _dot_nt(dmix, wout_v[0:half, :])
        dln_ref[...] = _dot_nt(dmix, wout_v[half:, :])
        red_ref[7:8, :] += jnp.zeros((1, D), F32) + loss

    def tok(w):
        return pl.BlockSpec((None, tm, w), lambda b, t: (b, t, 0))

    def mod(k):
        return pl.BlockSpec((None, None, 1, D), lambda b, t, k=k: (b, k, 0, 0))

    def vec():
        return pl.BlockSpec((1, D), lambda b, t: (0, 0))

    return pl.pallas_call(
        body, name="dense_core", grid=(B, nt),
        out_shape=(jax.ShapeDtypeStruct((B, N, D), F32), jax.ShapeDtypeStruct((B, N, half), F32),
                   jax.ShapeDtypeStruct((B, N, half), F32), jax.ShapeDtypeStruct((B, N, D), BF16),
                   jax.ShapeDtypeStruct((B, N, D), BF16), jax.ShapeDtypeStruct((B, N, F), BF16),
                   jax.ShapeDtypeStruct((B, N, F), BF16), jax.ShapeDtypeStruct((B, N, D), BF16),
                   jax.ShapeDtypeStruct((B, SUBLANES, D), F32),
                   jax.ShapeDtypeStruct(w_out.shape, w_out.dtype), jax.ShapeDtypeStruct(w1.shape, w1.dtype),
                   jax.ShapeDtypeStruct(w2.shape, w2.dtype)),
        in_specs=[tok(half), tok(half), tok(D), tok(D), mod(2), mod(3), mod(4), mod(5), vec(), vec(), vec(),
                  _any(), _any(), _any()],
        out_specs=(tok(D), tok(half), tok(half), tok(D), tok(D), tok(F), tok(F), tok(D),
                   pl.BlockSpec((None, SUBLANES, D), lambda b, t: (b, 0, 0)), _any(), _any(), _any()),
        scratch_shapes=[pltpu.VMEM((mixw, D), BF16), pltpu.VMEM((D, F), BF16), pltpu.VMEM((F, D), BF16),
                        pltpu.VMEM((tm, F), F32), pltpu.SemaphoreType.DMA((3,)),
                        pltpu.SemaphoreType.DMA((3, 3)), pltpu.SemaphoreType.DMA((3, 3))],
        input_output_aliases={11: 9, 12: 10, 13: 11},
        compiler_params=_params("arbitrary", "arbitrary"),
    )(lat_ret, lat_na, x, tgt, modl, modl, modl, modl, g_post_mix, g_pre_mlp, g_post_mlp, w_out, w1, w2)[:9]


def _inproj_bwd(dret, dna, x, ctx, dy1, modl, g1, w_in_t):
    B, N, D = x.shape
    n_ctx = ctx.shape[1]
    T = n_ctx + N
    tm = _div_tile(n_ctx, 256, 16)
    nct, ctx_spec, lat_spec = _token_tiles(n_ctx, tm)
    nt = T // tm
    nseg_r = dret.shape[1]
    nseg_n = dna.shape[1]
    nw = w_in_t.shape[0]

    def body(*refs):
        seg_refs = refs[:nseg_r + nseg_n]
        c_ref, x_ref, dy1_ref, sc_ref, g_ref, w_ref, dx_ref, red_ref = refs[nseg_r + nseg_n:]
        t = pl.program_id(1)
        dh = jnp.zeros((tm, D), F32)
        for s, ref in enumerate(seg_refs):
            dh = dh + _dot(ref[...], w_ref[s * SEG:(s + 1) * SEG, :])
        x = jnp.where(t < nct, c_ref[...], x_ref[...])
        g = g_ref[...]
        r = lax.rsqrt(jnp.mean(x * x, axis=-1, keepdims=True) + NORM_EPS)
        xh = x * r
        gain = 1.0 + sc_ref[...]
        s_in = jnp.sum(dh * xh, axis=0, keepdims=True)
        red_ref[0:1, :] = jnp.sum(dh, axis=0, keepdims=True)
        red_ref[1:2, :] = s_in * g
        red_ref[2:3, :] = s_in * gain
        red_ref[3:, :] = jnp.zeros((SUBLANES - 3, D), F32)
        dxh = dh * (g * gain)
        dx = r * (dxh - xh * jnp.mean(dxh * xh, axis=-1, keepdims=True))
        dx_ref[...] = dx + jnp.where(t >= nct, dy1_ref[...], 0.0)

    def mrow(b, t):
        return jnp.where(t < nct, B, b)

    def seg(s):
        return pl.BlockSpec((None, None, tm, SEG), lambda b, t, s=s: (b, s, t, 0))

    return _grid_call(
        body, name="inproj_bwd", grid=(B, nt),
        out_shape=(jax.ShapeDtypeStruct((B, N, D), F32), jax.ShapeDtypeStruct((B, nt, SUBLANES, D), F32)),
        in_specs=[seg(s) for s in range(nseg_r)] + [seg(s) for s in range(nseg_n)]
                 + [ctx_spec(D), lat_spec(D), lat_spec(D),
                    pl.BlockSpec((None, None, 1, D), lambda b, t: (mrow(b, t), 1, 0, 0)),
                    pl.BlockSpec((1, D), lambda b, t: (0, 0)),
                    pl.BlockSpec((nw, D), lambda b, t: (0, 0))],
        out_specs=(lat_spec(D), pl.BlockSpec((None, None, SUBLANES, D), lambda b, t: (b, t, 0, 0))),
        scratch_shapes=[], args=(*([dret] * nseg_r), *([dna] * nseg_n), ctx, x, dy1, modl, g1, w_in_t))


def _tn_matmul(lhs, rhs, name, rows_before=0, rows_after=0, into=None):
    B, S, T, W = lhs.shape
    nn = rhs.shape[-1]
    tk = _div_tile(T, 2304, LANES)
    bm = _div_tile(W, 1024, LANES)
    bn = _div_tile(nn, 1024, LANES)
    nkt = T // tk
    nk = B * nkt

    def body(l_ref, r_ref, *rest):
        o_ref, acc = rest[-2:]
        k = pl.program_id(3)

        @pl.when(k == 0)
        def _():
            acc[...] = jnp.zeros(acc.shape, F32)

        acc[...] += _dot_tn(l_ref[...].astype(BF16), r_ref[...].astype(BF16))

        @pl.when(k == nk - 1)
        def _():
            o_ref[...] = acc[...].astype(BF16)

    nwb = W // bm
    first = rows_before // bm
    return pl.pallas_call(
        functools.partial(body), name=name, grid=(S, nwb, nn // bn, nk),
        out_shape=jax.ShapeDtypeStruct((rows_before + S * W + rows_after, nn), BF16),
        in_specs=[pl.BlockSpec((None, None, tk, bm), lambda s, i, j, k: (k // nkt, s, k % nkt, i)),
                  pl.BlockSpec((None, tk, bn), lambda s, i, j, k: (k // nkt, k % nkt, j))]
                 + ([] if into is None else [_any()]),
        out_specs=pl.BlockSpec((bm, bn), lambda s, i, j, k: (first + s * nwb + i, j)),
        scratch_shapes=[pltpu.VMEM((bm, bn), F32)],
        input_output_aliases={} if into is None else {2: 0},
        compiler_params=_params("parallel", "parallel", "parallel", "arbitrary"),
    )(lhs, rhs, *([] if into is None else [into]))


class _SplitScatter:
    def __init__(self, gs, block_ofs, land_shapes, name, kind="scatter", masks=ALL_PEERS):
        self.n = n = len(gs)
        self.block_ofs, self.kind, self.masks = block_ofs, kind, masks
        if kind == "scatter":
            land_shapes = [(N_DEV,) + tuple(bs) for bs in land_shapes]
        hbm = pl.BlockSpec(memory_space=pltpu.HBM)
        sem = pl.BlockSpec(memory_space=pltpu.SEMAPHORE)

        def body(*refs):
            g_refs, land_refs = refs[:n], refs[n:2 * n]
            send_sems, recv_sems, own_sems = refs[2 * n:2 * n + 3]
            token = refs[-1]
            for own, pushes in self._copies(g_refs, land_refs, send_sems, recv_sems, own_sems, landing="sender"):
                own.start()
                for cp in pushes:
                    cp.start()
            token[...] = jnp.zeros_like(token)

        outs = pl.pallas_call(
            body, name=name,
            out_shape=(pltpu.SemaphoreType.DMA((n * (N_DEV - 1),)), pltpu.SemaphoreType.DMA((n * (N_DEV - 1),)),
                       pltpu.SemaphoreType.DMA((n,)))
                      + tuple(pltpu.HBM(g.shape, g.dtype) for g in gs)
                      + tuple(pltpu.HBM(s, g.dtype) for s, g in zip(land_shapes, gs))
                      + (jax.ShapeDtypeStruct((SUBLANES, LANES), F32),),
            in_specs=(hbm,) * (2 * n), out_specs=(sem,) * 3 + (hbm,) * (2 * n) + (_vmem(),),
            input_output_aliases={k: 3 + k for k in range(2 * n)},
            compiler_params=pltpu.CompilerParams(has_side_effects=pltpu.SideEffectType.DATAFLOW_SIDE_EFFECTING),
        )(*[pltpu.with_memory_space_constraint(g, pltpu.HBM) for g in gs],
          *[pltpu.with_memory_space_constraint(lax.empty(s, g.dtype), pltpu.HBM) for s, g in zip(land_shapes, gs)])
        self.sems, self.thru, self.token = outs[:3], outs[3:3 + 2 * n], outs[-1]

    def _copies(self, g_refs, land_refs, send_sems, recv_sems, own_sems, landing):
        me, peers = _me_and_peers()
        out = []
        for k in range(self.n):
            if self.kind == "scatter":
                src, dst = self.block_ofs[k](g_refs[k]), _slot(land_refs[k])
            else:
                src, dst = (lambda p, k=k: g_refs[k]), self.block_ofs[k](land_refs[k])
            own = pltpu.make_async_copy(src(me), dst(me), own_sems.at[k]) if landing == "sender" else None
            pushes = []
            for m in self.masks:
                dev, pid = peers[m - 1]
                i = k * (N_DEV - 1) + m - 1
                pushes.append(_remote(src(pid), dst(me if landing == "sender" else pid),
                                      send_sems.at[i], recv_sems.at[i], dev))
            out.append((own, pushes))
        return out


def _scatter_wait(scatters, after, name):
    hbm = pl.BlockSpec(memory_space=pltpu.HBM)
    sem = pl.BlockSpec(memory_space=pltpu.SEMAPHORE)
    n_arr = [2 * sc.n for sc in scatters]
    total = sum(n_arr)

    def body(*refs):
        arrs, sems = refs[:total], refs[total:total + 3 * len(scatters)]
        a0 = 0
        for j, sc in enumerate(scatters):
            g_refs, land_refs = arrs[a0:a0 + sc.n], arrs[a0 + sc.n:a0 + 2 * sc.n]
            a0 += 2 * sc.n
            send_sems, recv_sems, own_sems = sems[3 * j:3 * j + 3]
            for (own, sent), (_, got) in zip(sc._copies(g_refs, land_refs, send_sems, recv_sems, own_sems, "sender"),
                                             sc._copies(g_refs, land_refs, send_sems, recv_sems, own_sems, "receiver")):
                own.wait()
                for cp in sent:
                    cp.wait_send()
                for cp in got:
                    cp.wait_recv()

    operands = [a for sc in scatters for a in sc.thru]
    outs = pl.pallas_call(
        body, name=name,
        out_shape=tuple(pltpu.HBM(a.shape, a.dtype) for a in operands),
        in_specs=(hbm,) * total + (sem,) * (3 * len(scatters)) + (pl.BlockSpec(memory_space=pl.ANY),),
        out_specs=(hbm,) * total, input_output_aliases={k: k for k in range(total)},
        compiler_params=pltpu.CompilerParams(has_side_effects=pltpu.SideEffectType.DATAFLOW_SIDE_EFFECTING),
    )(*operands, *[s for sc in scatters for s in sc.sems], after)
    lands, a0 = [], 0
    for sc in scatters:
        lands.extend(outs[a0 + sc.n:a0 + 2 * sc.n])
        a0 += 2 * sc.n
    return lands


def _small_ar(mbuf, silu_all, w_ada, c_ctx, n_mod_rows, n_vec_rows):
    D = silu_all.shape[1]
    ncol = w_ada.shape[1]
    nm = mbuf.shape[2]
    srows = silu_all.shape[0]

    def body(mbuf, s_ref, w_ref, cc_ref, tot_ref, gb_ref, gw_ref, gc_ref, tbuf, dmx, cmrow, send3, recv3):
        me, _ = _me_and_peers()
        msum = mbuf[0]
        for k in range(1, N_DEV):
            msum = msum + mbuf[k]
        tot_ref[...] = msum[n_mod_rows:n_mod_rows + n_vec_rows]
        gb_ref[...] = jnp.sum(msum[0:n_mod_rows], axis=0, keepdims=True)
        loc = pl.ds(pl.multiple_of(me * ncol, ncol), ncol)
        for k in range(N_DEV):
            dmx[k * SUBLANES:(k + 1) * SUBLANES, :] = mbuf[k, :, loc]
        cmrow[...] = msum
        cm_loc = cmrow[n_mod_rows - 1:n_mod_rows, loc]
        dmx[N_DEV * SUBLANES:, :] = jnp.concatenate([cm_loc, jnp.zeros((SUBLANES - 1, ncol), F32)], axis=0)
        gw_ref[...] = _dot_tn(s_ref[...], dmx[...])
        tbuf[me] = _dot_nt(dmx[N_DEV * SUBLANES:, :], w_ref[...])
        _exchange(lambda p: tbuf.at[me], lambda p: tbuf.at[p], send3, recv3)
        tsum = tbuf[0]
        for k in range(1, N_DEV):
            tsum = tsum + tbuf[k]
        cc = cc_ref[...]
        sg = _sigmoid(cc)
        gc_ref[...] = tsum[0:1, :] * (sg * (1.0 + cc * (1.0 - sg)))

    return pl.pallas_call(
        body, name="small_ar",
        out_shape=(jax.ShapeDtypeStruct((n_vec_rows, nm), F32), jax.ShapeDtypeStruct((1, nm), F32),
                   jax.ShapeDtypeStruct((D, ncol), F32), jax.ShapeDtypeStruct((1, D), F32)),
        in_specs=[_vmem()] * 4, out_specs=(_vmem(),) * 4,
        scratch_shapes=[pltpu.VMEM((N_DEV, SUBLANES, D), F32), pltpu.VMEM((srows, ncol), F32),
                        pltpu.VMEM((SUBLANES, nm), F32)] + [pltpu.SemaphoreType.DMA((N_DEV - 1,))] * 2,
        compiler_params=pltpu.CompilerParams(vmem_limit_bytes=VMEM_LIMIT),
    )(mbuf, silu_all, w_ada, c_ctx.reshape(1, D))


def _adam_update(w, g, m, v):
    mn = ADAM_B1 * m + (1.0 - ADAM_B1) * g
    vn = ADAM_B2 * v + (1.0 - ADAM_B2) * (g * g)
    m_hat = mn / (1.0 - ADAM_B1 ** ADAM_STEP)
    v_hat = vn / (1.0 - ADAM_B2 ** ADAM_STEP)
    return -ADAM_LR * (m_hat / (jnp.sqrt(v_hat) + ADAM_EPS) + ADAM_WD * w), mn, vn


def _adamw(w, g, m, v, name):
    rows, cols = w.shape
    tr = _div_tile(rows, 512, SUBLANES)

    def body(w_ref, g_ref, m_ref, v_ref, d_ref, nm_ref, nv_ref):
        d_ref[...], nm_ref[...], nv_ref[...] = _adam_update(w_ref[...], g_ref[...], m_ref[...], v_ref[...])

    spec = pl.BlockSpec((tr, cols), lambda i: (i, 0))
    return pl.pallas_call(
        functools.partial(body), name=name, grid=(rows // tr,),
        out_shape=(jax.ShapeDtypeStruct((rows, cols), F32),) * 3,
        in_specs=[spec] * 4, out_specs=(spec,) * 3,
        compiler_params=_params("parallel"),
    )(w, g, m, v)


def _adamw_small(items, name):
    n = len(items)

    def body(*refs):
        ins, outs = refs[:4 * n], refs[4 * n:]
        for i in range(n):
            w_ref, g_ref, m_ref, v_ref = ins[4 * i:4 * i + 4]
            outs[3 * i][...], outs[3 * i + 1][...], outs[3 * i + 2][...] = _adam_update(
                w_ref[...], g_ref[...], m_ref[...], v_ref[...])

    outs = pl.pallas_call(
        body, name=name,
        out_shape=tuple(jax.ShapeDtypeStruct(it[0].shape, F32) for it in items for _ in range(3)),
        in_specs=[_vmem()] * (4 * n), out_specs=(_vmem(),) * (3 * n),
        compiler_params=pltpu.CompilerParams(vmem_limit_bytes=VMEM_LIMIT),
    )(*[a for it in items for a in it])
    return [tuple(outs[3 * i:3 * i + 3]) for i in range(n)]


def _sum_adamw(buf, w, m, v, name):
    _, rows, cols = buf.shape
    tr = _div_tile(rows, 256, 2 * SUBLANES)

    def body(b_ref, w_ref, m_ref, v_ref, g_ref, d_ref, nm_ref, nv_ref):
        g = b_ref[0].astype(F32)
        for k in range(1, N_DEV):
            g = g + b_ref[k].astype(F32)
        g_ref[...] = g
        d_ref[...], nm_ref[...], nv_ref[...] = _adam_update(w_ref[...], g, m_ref[...], v_ref[...])

    spec = pl.BlockSpec((tr, cols), lambda i: (i, 0))
    return pl.pallas_call(
        functools.partial(body), name=name, grid=(rows // tr,),
        out_shape=(jax.ShapeDtypeStruct((rows, cols), F32),) * 4,
        in_specs=[pl.BlockSpec((N_DEV, tr, cols), lambda i: (0, i, 0))] + [spec] * 3, out_specs=(spec,) * 4,
        compiler_params=_params("parallel"),
    )(buf, w, m, v)


def _rope_tables(n_ctx, n):
    n_freq = RET_DIM // 4
    inv = np.float32(ROPE_BASE) ** (-np.arange(n_freq, dtype=np.float32) / np.float32(n_freq))
    tok = np.arange(n)
    pos_r = (tok // GRID_W).astype(np.float32)
    pos_c = (tok % GRID_W).astype(np.float32)
    ang_r = (pos_r[:, None] * inv[None, :]).astype(np.float32)
    ang_c = (pos_c[:, None] * inv[None, :]).astype(np.float32)
    cos = np.concatenate([np.cos(ang_r), np.cos(ang_r), np.cos(ang_c), np.cos(ang_c)], axis=-1)
    sin = np.concatenate([-np.sin(ang_r), np.sin(ang_r), -np.sin(ang_c), np.sin(ang_c)], axis=-1)
    cos = np.concatenate([np.ones((n_ctx, RET_DIM), np.float32), cos], axis=0)
    sin = np.concatenate([np.zeros((n_ctx, RET_DIM), np.float32), sin], axis=0)
    return jnp.asarray(cos, F32), jnp.asarray(sin, F32)


def _na_tables():
    q = np.arange(GRID_W)[:, None]
    k = np.arange(GRID_W)[None, :]
    start = np.clip(q - NA_KW // 2, 0, GRID_W - NA_KW)
    valid = (k >= start) & (k < start + NA_KW)
    dc = np.clip(k - q + (NA_KW - 1), 0, 2 * NA_KW - 2)
    ncls = 2 * NA_KW - 1
    onehot = (dc[None] == np.arange(ncls)[:, None, None]) & valid[None]
    oh2 = np.zeros((GRID_W, LANES, LANES), np.float32)
    for c in range(ncls):
        oh2[:, :GRID_W, c] = onehot[c]
        oh2[:, GRID_W:, 32 + c] = onehot[c]
    return onehot.astype(np.float32), valid, oh2.reshape(GRID_W * LANES, LANES)


def _paired_bias(rpb, onehot, valid):
    ncls = onehot.shape[0]
    pair = np.zeros((2 * ncls, GRID_W, LANES), np.float32)
    pair[:ncls, :, :GRID_W] = onehot
    pair[ncls:, :, GRID_W:] = onehot
    rows = jnp.concatenate([rpb[:, :-1], rpb[:, 1:]], axis=-1)
    t = jnp.einsum("hdc,cqk->hdqk", rows, jnp.asarray(pair), precision=lax.Precision.HIGHEST)
    return jnp.where(jnp.asarray(np.tile(valid, (1, 2)))[None, None], t, NEG_INF)


def kernel(x, c, ctx, c_ctx, w_ada, b_ada, g_pre_mix, g_post_mix, g_pre_mlp, g_post_mlp, w_in, ret_decay, ret_gn, na_rpb, w_out, w_mlp1, w_mlp2, loss_target, m_c_ctx, m_w_ada, m_b_ada, m_g_pre_mix, m_g_post_mix, m_g_pre_mlp, m_g_post_mlp, m_w_in, m_ret_decay, m_ret_gn, m_na_rpb, m_w_out, m_w_mlp1, m_w_mlp2, v_c_ctx, v_w_ada, v_b_ada, v_g_pre_mix, v_g_post_mix, v_g_pre_mlp, v_g_post_mlp, v_w_in, v_ret_decay, v_ret_gn, v_na_rpb, v_w_out, v_w_mlp1, v_w_mlp2):
    B, N, D = x.shape
    C = ctx.shape[1]
    T = C + N

    silu_all, mods_g, win_b, wout_l, w1_l, w2_l = _mod_gather(c, c_ctx, w_ada[0], b_ada, w_in[0].T, w_out[0],
                                                             w_mlp1[0], w_mlp2[0])
    mods_mine = mods_g.transpose(1, 0, 2).reshape(mods_g.shape[1], N_MOD * D)
    modl = jnp.concatenate([mods_mine[:B], mods_mine[SUBLANES:SUBLANES + 1]], axis=0)
    modl = modl.reshape(B + 1, N_MOD, 1, D)
    rin = w_in.shape[2]
    rout, c1, r2 = wout_l.shape[0], w1_l.shape[1], w2_l.shape[0]

    def rows_of(n):
        return lambda ref: _row_block(ref, n)

    def cols_of(n):
        return lambda ref: _col_block(ref, n)

    cos, sin = _rope_tables(C, N)
    onehot, valid, oh2 = _na_tables()
    bias2 = _paired_bias(na_rpb[0], onehot, valid)
    lg = jax.nn.log_sigmoid(ret_decay[0].astype(F32))

    ag = _SplitScatter([wout_l, w1_l, w2_l], [rows_of(rout), cols_of(c1), rows_of(r2)],
                       [(N_DEV * rout, D), (D, N_DEV * c1), (N_DEV * r2, D)], "ag_mlp_start",
                       kind="gather", masks=SIBLING + ICI_SAME_CORE)
    h_all, proj = _inproj_fwd(x, ctx, modl, g_pre_mix + ag.token[0, 0], win_b)
    o_ret, lat_ret, q_rot, k_rot = _ret_fwd(proj, cos, sin, lg, ret_gn, C)
    lat_na, na_probs = _na_fwd(proj, bias2, C)
    wout_part, w1_part, w2_part = _scatter_wait([ag], lat_na, "ag_mlp_wait")

    (dy1, dlat_ret, dlat_na, dmix, h2, act, du, dz, red_d) = _dense_core(
        lat_ret, lat_na, x, loss_target, modl, g_post_mix, g_pre_mlp, g_post_mlp, wout_part, w1_part, w2_part)

    gw_out_p = _tn_matmul(lat_ret[:, None], dmix, "gw_out_ret", rows_after=lat_na.shape[-1])
    gw_out_p = _tn_matmul(lat_na[:, None], dmix, "gw_out_na", rows_before=lat_ret.shape[-1], into=gw_out_p)
    gw1_p = _tn_matmul(h2[:, None], du, "gw_mlp1")
    gw2_p = _tn_matmul(act[:, None], dz, "gw_mlp2")
    rs_mlp = _SplitScatter([gw_out_p, gw1_p, gw2_p], [rows_of(rout), cols_of(c1), rows_of(r2)],
                           [(rout, D), (D, c1), (r2, D)], "rs_mlp_start")

    dret, dgn_p, dlg_p = _ret_bwd(proj, q_rot, k_rot, cos, sin, lg, ret_gn + rs_mlp.token[0, 0], o_ret, dlat_ret, C)
    dna, dbias2 = _na_bwd(proj, na_probs, dlat_na, C)
    ret_cols, na_cols = dret.shape[1] * dret.shape[3], dna.shape[1] * dna.shape[3]
    gwin_t_p = _tn_matmul(dret, h_all, "gw_in_ret", rows_after=na_cols)
    gwin_t_p = _tn_matmul(dna, h_all, "gw_in_na", rows_before=ret_cols, into=gwin_t_p)
    rs_in = _SplitScatter([gwin_t_p], [rows_of(rin)], [(rin, D)], "rs_w_in_start")
    grad_x, red_i = _inproj_bwd(dret, dna, x, ctx, dy1, modl, g_pre_mix + rs_in.token[0, 0], win_b)

    rd = red_d
    nct = red_i.shape[1] * C // T
    ri_ctx = red_i[:, :nct].sum(axis=(0, 1))
    ri_lat = red_i[:, nct:].sum(axis=1)
    d_mods = jnp.concatenate([ri_lat[:, 0], ri_lat[:, 1], rd[:, 0], rd[:, 4], rd[:, 3], rd[:, 2]], axis=-1)
    d_cmods = jnp.concatenate([ri_ctx[0], ri_ctx[1], jnp.zeros(((N_MOD - 2) * D,), F32)])[None]
    dg_pre_mix = ri_lat[:, 2].sum(axis=0) + ri_ctx[2]
    dg_post_mix = rd[:, 1].sum(axis=0)
    dg_pre_mlp = rd[:, 5].sum(axis=0)
    dg_post_mlp = rd[:, 6].sum(axis=0)
    loss_p = rd[:, 7, 0].sum()
    d_gn = dgn_p[:, 0].sum(axis=0)
    d_lg = dlg_p[:, :, :2, 0].sum(axis=0).T
    d_decay = d_lg * jax.nn.sigmoid(-ret_decay[0].astype(F32))
    rr = _rpb_reduce(dbias2, jnp.asarray(oh2, BF16)).reshape(NA_HEADS, 2 * NA_KH - 2, LANES)
    ncls = 2 * NA_KW - 1
    d_rpb = (jnp.pad(rr[:, :, :ncls], ((0, 0), (0, 1), (0, 0))) + jnp.pad(rr[:, :, 32:32 + ncls], ((0, 0), (1, 0), (0, 0))))
    d_rpb32 = jnp.pad(d_rpb, ((0, 0), (0, 0), (0, 32 - ncls)))
    pieces = [dg_pre_mix, dg_post_mix, dg_pre_mlp, dg_post_mlp, d_gn, d_rpb32.reshape(-1),
              jnp.pad(d_decay.reshape(-1), (0, LANES - d_decay.size)), jnp.full((LANES,), loss_p, F32)]
    vec = jnp.concatenate(pieces)
    nm = N_MOD * D
    n_vec_rows = -(-vec.shape[0] // nm)
    assert B + 1 + n_vec_rows <= SUBLANES
    vec = jnp.pad(vec, (0, n_vec_rows * nm - vec.shape[0])).reshape(n_vec_rows, nm)
    dm_slot = jnp.concatenate([d_mods, d_cmods, vec, jnp.zeros((SUBLANES - B - 1 - n_vec_rows, nm), F32)], axis=0)
    def whole(ref):
        return lambda p: ref

    small = _SplitScatter([dm_slot], [whole], [dm_slot.shape], "small_start")
    land_out, land_1, land_2, land_in = _scatter_wait([rs_mlp, rs_in], small.token, "rs_wait")
    fused = {"w_in": [a.T for a in _sum_adamw(land_in, w_in[0].T, m_w_in[0].T, v_w_in[0].T, "sum_adamw_w_in")],
             "w_out": _sum_adamw(land_out, w_out[0], m_w_out[0], v_w_out[0], "sum_adamw_w_out"),
             "w_mlp1": _sum_adamw(land_1, w_mlp1[0], m_w_mlp1[0], v_w_mlp1[0], "sum_adamw_w_mlp1"),
             "w_mlp2": _sum_adamw(land_2, w_mlp2[0], m_w_mlp2[0], v_w_mlp2[0], "sum_adamw_w_mlp2")}
    (mbuf,) = _scatter_wait([small], fused["w_mlp2"][0], "small_wait")
    tot, g_b_ada, g_w_ada, g_c_ctx = _small_ar(mbuf, silu_all, w_ada[0], c_ctx, B + 1, n_vec_rows)
    flat = tot.reshape(-1)
    o0 = 0
    g_pre_mix_g = flat[o0:o0 + D]; o0 += D
    g_post_mix_g = flat[o0:o0 + D]; o0 += D
    g_pre_mlp_g = flat[o0:o0 + D]; o0 += D
    g_post_mlp_g = flat[o0:o0 + D]; o0 += D
    g_gn = flat[o0:o0 + RET_WIDTH]; o0 += RET_WIDTH
    nrpb = NA_HEADS * (2 * NA_KH - 1) * 32
    g_rpb = flat[o0:o0 + nrpb].reshape(NA_HEADS, 2 * NA_KH - 1, 32)[:, :, :ncls]; o0 += nrpb
    g_decay = flat[o0:o0 + 2 * RET_HEADS].reshape(2, RET_HEADS); o0 += LANES
    loss = flat[o0]

    grads = {
        "c_ctx": g_c_ctx.reshape(c_ctx.shape), "w_ada": g_w_ada[None], "b_ada": g_b_ada.reshape(b_ada.shape),
        "g_pre_mix": g_pre_mix_g[None], "g_post_mix": g_post_mix_g[None], "g_pre_mlp": g_pre_mlp_g[None],
        "g_post_mlp": g_post_mlp_g[None], "w_in": fused["w_in"][0][None], "ret_decay": g_decay[None], "ret_gn": g_gn[None],
        "na_rpb": g_rpb[None], "w_out": fused["w_out"][0][None], "w_mlp1": fused["w_mlp1"][0][None],
        "w_mlp2": fused["w_mlp2"][0][None],
    }
    weights = dict(c_ctx=c_ctx, w_ada=w_ada, b_ada=b_ada, g_pre_mix=g_pre_mix, g_post_mix=g_post_mix,
                   g_pre_mlp=g_pre_mlp, g_post_mlp=g_post_mlp, w_in=w_in, ret_decay=ret_decay, ret_gn=ret_gn,
                   na_rpb=na_rpb, w_out=w_out, w_mlp1=w_mlp1, w_mlp2=w_mlp2)
    m_in = dict(c_ctx=m_c_ctx, w_ada=m_w_ada, b_ada=m_b_ada, g_pre_mix=m_g_pre_mix, g_post_mix=m_g_post_mix,
                g_pre_mlp=m_g_pre_mlp, g_post_mlp=m_g_post_mlp, w_in=m_w_in, ret_decay=m_ret_decay,
                ret_gn=m_ret_gn, na_rpb=m_na_rpb, w_out=m_w_out, w_mlp1=m_w_mlp1, w_mlp2=m_w_mlp2)
    v_in = dict(c_ctx=v_c_ctx, w_ada=v_w_ada, b_ada=v_b_ada, g_pre_mix=v_g_pre_mix, g_post_mix=v_g_post_mix,
                g_pre_mlp=v_g_pre_mlp, g_post_mlp=v_g_post_mlp, w_in=v_w_in, ret_decay=v_ret_decay,
                ret_gn=v_ret_gn, na_rpb=v_na_rpb, w_out=v_w_out, w_mlp1=v_w_mlp1, w_mlp2=v_w_mlp2)
    names = list(weights)
    deltas, new_m, new_v = {}, {}, {}
    def as_2d(n):
        shp = weights[n].shape
        two_d = (-1, shp[-1]) if len(shp) > 1 else (1, shp[0])
        return [a.reshape(two_d) for a in (weights[n], grads[n], m_in[n], v_in[n])]

    small = [n for n in names if n not in fused and weights[n].size <= 65536]
    updated = dict(zip(small, _adamw_small([as_2d(n) for n in small], "adamw_small")))
    for n in names:
        if n in fused:
            updated[n] = fused[n][1:]
        elif n not in updated:
            updated[n] = _adamw(*as_2d(n), "adamw_" + n)
        deltas[n], new_m[n], new_v[n] = (a.reshape(weights[n].shape) for a in updated[n])
    return (loss, grad_x, *[grads[n] for n in names], *[deltas[n] for n in names],
            *[new_m[n] for n in names], *[new_v[n] for n in names])
```

```python
import functools

import numpy as np
import jax
import jax.numpy as jnp
from jax import lax
from jax.experimental import pallas as pl
from jax.experimental.pallas import tpu as pltpu

F32 = jnp.float32
BF16 = jnp.bfloat16
MESH = pl.DeviceIdType.MESH

N_DEV = 8
LANES = 128
SUBLANES = 8
VMEM_LIMIT = 60 * 1024 * 1024

GRID_W = 64
RET_HEADS = 4
RET_DIM = 128
RET_WIDTH = RET_HEADS * RET_DIM
NA_HEADS = 8
NA_DIM = 64
NA_WIDTH = NA_HEADS * NA_DIM
NA_PAIRS = NA_HEADS // 2
NA_KH = 8
NA_KW = 16
NA_GROUP = 8
SEG = 512
ROPE_BASE = 10000.0
NORM_EPS = 1e-6
NEG_INF = -1e30
N_MOD = 6

ADAM_LR = 0.001
ADAM_B1 = 0.9
ADAM_B2 = 0.999
ADAM_EPS = 1e-08
ADAM_WD = 0.01
ADAM_STEP = 10


def _dot(a, b):
    return lax.dot_general(a, b, (((1,), (0,)), ((), ())), preferred_element_type=F32)


def _dot_nt(a, b):
    return lax.dot_general(a, b, (((1,), (1,)), ((), ())), preferred_element_type=F32)


def _dot_tn(a, b):
    return lax.dot_general(a, b, (((0,), (0,)), ((), ())), preferred_element_type=F32)


def _sigmoid(x):
    return 1.0 / (1.0 + jnp.exp(-x))


def _div_tile(n, cap, mult):
    if n <= cap:
        return n
    for t in range(cap - cap % mult, 0, -mult):
        if n % t == 0:
            return t
    raise ValueError(f"no tile for {n}")


def _params(*sem):
    return pltpu.CompilerParams(dimension_semantics=tuple(sem) if sem else None,
                                vmem_limit_bytes=VMEM_LIMIT)


def _vmem():
    return pl.BlockSpec(memory_space=pltpu.VMEM)


def _any():
    return pl.BlockSpec(memory_space=pl.ANY)


def _me_and_peers():
    x, y, c = lax.axis_index("x"), lax.axis_index("y"), lax.axis_index("c")
    me = 4 * x + 2 * y + c
    peers = []
    for m in range(1, N_DEV):
        px = 1 - x if (m >> 2) & 1 else x
        py = 1 - y if (m >> 1) & 1 else y
        pc = 1 - c if m & 1 else c
        peers.append(((px, py, pc), 4 * px + 2 * py + pc))
    return me, peers


def _exchange(src_for, dst_from, send_sems, recv_sems):
    me, peers = _me_and_peers()
    sent = []
    for i, (dev, pid) in enumerate(peers):
        cp = pltpu.make_async_remote_copy(src_ref=src_for(pid), dst_ref=dst_from(me),
                                          send_sem=send_sems.at[i], recv_sem=recv_sems.at[i],
                                          device_id=dev, device_id_type=MESH)
        cp.start()
        sent.append(cp)
    for i, (dev, pid) in enumerate(peers):
        pltpu.make_async_remote_copy(src_ref=src_for(pid), dst_ref=dst_from(pid),
                                     send_sem=send_sems.at[i], recv_sem=recv_sems.at[i],
                                     device_id=dev, device_id_type=MESH).wait_recv()
    for cp in sent:
        cp.wait_send()


SIBLING = (1,)
ICI_SAME_CORE = (2, 4, 6)
ALL_PEERS = tuple(range(1, N_DEV))


def _remote(src, dst, send_sem, recv_sem, dev):
    return pltpu.make_async_remote_copy(src_ref=src, dst_ref=dst, send_sem=send_sem, recv_sem=recv_sem,
                                        device_id=dev, device_id_type=MESH)


def _push_start(items, masks, send_sems, recv_sems):
    me, peers = _me_and_peers()
    for k, (src_for, dst_from) in enumerate(items):
        for m in masks:
            dev, pid = peers[m - 1]
            _remote(src_for(pid), dst_from(me), send_sems.at[k, m - 1], recv_sems.at[k, m - 1], dev).start()


def _push_wait_recv(items, masks, send_sems, recv_sems):
    me, peers = _me_and_peers()
    for k, (src_for, dst_from) in enumerate(items):
        for m in masks:
            dev, pid = peers[m - 1]
            _remote(src_for(pid), dst_from(pid), send_sems.at[k, m - 1], recv_sems.at[k, m - 1], dev).wait_recv()


def _push_wait_send(items, masks, send_sems, recv_sems):
    me, peers = _me_and_peers()
    for k, (src_for, dst_from) in enumerate(items):
        for m in masks:
            dev, pid = peers[m - 1]
            _remote(src_for(pid), dst_from(me), send_sems.at[k, m - 1], recv_sems.at[k, m - 1], dev).wait_send()


def _forward_start(items, send_sems, recv_sems):
    me, peers = _me_and_peers()
    sib = peers[0][0]
    for k, (blk_in, blk_out) in enumerate(items):
        for j, m in enumerate(ICI_SAME_CORE):
            pid = peers[m - 1][1]
            _remote(blk_in(pid), blk_out(pid), send_sems.at[k, j], recv_sems.at[k, j], sib).start()


def _forward_wait(items, send_sems, recv_sems):
    me, peers = _me_and_peers()
    sib = peers[0][0]
    for k, (blk_in, blk_out) in enumerate(items):
        for j, m in enumerate(ICI_SAME_CORE):
            got = peers[(m | 1) - 1][1]
            _remote(blk_in(got), blk_out(got), send_sems.at[k, j], recv_sems.at[k, j], sib).wait_recv()
    for k, (blk_in, blk_out) in enumerate(items):
        for j, m in enumerate(ICI_SAME_CORE):
            pid = peers[m - 1][1]
            _remote(blk_in(pid), blk_out(pid), send_sems.at[k, j], recv_sems.at[k, j], sib).wait_send()


def _mod_gather(c, c_ctx, w_ada, b_ada, w_in_t, w_out, w1, w2):
    B, D = c.shape
    ncol = w_ada.shape[1]
    rows = SUBLANES * N_DEV + SUBLANES

    def body(c_ref, cc_ref, w_ref, b_ref, win_ref, wout_ref, w1_ref, w2_ref,
             s_ref, m_ref, gin_ref, wout_b, w1_b, w2_b,
             win_b, msend, send1, recv1, send2, recv2, wsend, wrecv, fsend, frecv, lsem):
        me, _ = _me_and_peers()
        win_b[...] = win_ref[...].astype(BF16)
        block = _row_block(gin_ref, w_in_t.shape[0])
        gather = [(lambda p: win_b, block)]
        own = pltpu.make_async_copy(win_b, block(me), lsem.at[0])
        cv = c_ref[...]
        slot = jnp.concatenate([cv * _sigmoid(cv), jnp.zeros((SUBLANES - B, D), F32)], axis=0)
        my_rows = pl.ds(pl.multiple_of(me * SUBLANES, SUBLANES), SUBLANES)
        s_ref[my_rows, :] = slot
        ccv = cc_ref[...]
        s_ref[SUBLANES * N_DEV:, :] = jnp.concatenate(
            [ccv * _sigmoid(ccv), jnp.zeros((SUBLANES - 1, D), F32)], axis=0)

        def rows_of(p):
            return s_ref.at[pl.ds(pl.multiple_of(p * SUBLANES, SUBLANES), SUBLANES), :]

        _exchange(lambda p: rows_of(me), rows_of, send1, recv1)
        own.start()
        _push_start(gather, SIBLING + ICI_SAME_CORE, wsend, wrecv)
        wout_b[...] = wout_ref[...].astype(BF16)
        w1_b[...] = w1_ref[...].astype(BF16)
        w2_b[...] = w2_ref[...].astype(BF16)
        b_loc = b_ref[:, pl.ds(pl.multiple_of(me * ncol, ncol), ncol)]
        mods = _dot(s_ref[...], w_ref[...]) + b_loc
        for p in range(N_DEV):
            msend[p] = jnp.concatenate([mods[p * SUBLANES:(p + 1) * SUBLANES], mods[N_DEV * SUBLANES:]], axis=0)
        m_ref[me] = msend[me]
        columns = [(lambda p: msend.at[p], lambda p: m_ref.at[p])]
        _push_start(columns, ALL_PEERS, send2, recv2)
        _push_wait_recv(gather, ICI_SAME_CORE, wsend, wrecv)
        relay = [(block, block)]
        _forward_start(relay, fsend, frecv)
        _push_wait_recv(columns, ALL_PEERS, send2, recv2)
        _push_wait_recv(gather, SIBLING, wsend, wrecv)
        _forward_wait(relay, fsend, frecv)
        _push_wait_send(columns, ALL_PEERS, send2, recv2)
        _push_wait_send(gather, SIBLING + ICI_SAME_CORE, wsend, wrecv)
        own.wait()

    return pl.pallas_call(
        body, name="mod_gather",
        out_shape=(jax.ShapeDtypeStruct((rows, D), F32), jax.ShapeDtypeStruct((N_DEV, 2 * SUBLANES, ncol), F32),
                   jax.ShapeDtypeStruct((N_DEV * w_in_t.shape[0], D), BF16),
                   jax.ShapeDtypeStruct(w_out.shape, BF16), jax.ShapeDtypeStruct(w1.shape, BF16),
                   jax.ShapeDtypeStruct(w2.shape, BF16)),
        in_specs=[_vmem()] * 8, out_specs=(_vmem(), _vmem(), _any(), _vmem(), _vmem(), _vmem()),
        scratch_shapes=[pltpu.VMEM(w_in_t.shape, BF16), pltpu.VMEM((N_DEV, 2 * SUBLANES, ncol), F32)]
                       + [pltpu.SemaphoreType.DMA((N_DEV - 1,))] * 2
                       + [pltpu.SemaphoreType.DMA((1, N_DEV - 1))] * 4 + [pltpu.SemaphoreType.DMA((1, 3))] * 2
                       + [pltpu.SemaphoreType.DMA((1,))],
        compiler_params=pltpu.CompilerParams(vmem_limit_bytes=VMEM_LIMIT),
    )(c, c_ctx.reshape(1, D), w_ada, b_ada, w_in_t, w_out, w1, w2)


def _row_block(ref, rows):
    return lambda p: ref.at[pl.ds(pl.multiple_of(p * rows, 2 * SUBLANES), rows), :]


def _col_block(ref, cols):
    return lambda p: ref.at[:, pl.ds(pl.multiple_of(p * cols, LANES), cols)]


def _slot(ref):
    return lambda p: ref.at[p]


def _grid_call(body, *, name, grid, out_shape, in_specs, out_specs, scratch_shapes, args, after=None):
    n_in = len(args)

    def ordered_body(*refs):
        body(*refs[:n_in], *refs[n_in + 1:])

    return pl.pallas_call(
        body if after is None else ordered_body, name=name, grid=grid, out_shape=tuple(out_shape),
        in_specs=list(in_specs) + ([] if after is None else [_any()]), out_specs=tuple(out_specs),
        scratch_shapes=list(scratch_shapes), compiler_params=_params(*(("arbitrary",) * len(grid))),
    )(*args, *([] if after is None else [after]))


def _token_tiles(n_ctx, tm):
    nct = n_ctx // tm

    def ctx_spec(D):
        return pl.BlockSpec((None, tm, D), lambda b, t: (b, jnp.minimum(t, nct - 1), 0))

    def lat_spec(D):
        return pl.BlockSpec((None, tm, D), lambda b, t: (b, jnp.maximum(t - nct, 0), 0))

    return nct, ctx_spec, lat_spec


def _inproj_fwd(x, ctx, modl, g1, w_in_t, after):
    B, N, D = x.shape
    n_ctx = ctx.shape[1]
    T = n_ctx + N
    nw = w_in_t.shape[0]
    tm = _div_tile(n_ctx, 256, 16)
    nct, ctx_spec, lat_spec = _token_tiles(n_ctx, tm)

    def body(c_ref, x_ref, sh_ref, sc_ref, g_ref, w_ref, h_ref, p_ref):
        x = jnp.where(pl.program_id(1) < nct, c_ref[...], x_ref[...])
        r = lax.rsqrt(jnp.mean(x * x, axis=-1, keepdims=True) + NORM_EPS)
        h = ((x * r) * g_ref[...]) * (1.0 + sc_ref[...]) + sh_ref[...]
        hb = h.astype(BF16)
        h_ref[...] = hb
        p_ref[...] = _dot_nt(hb, w_ref[...]).astype(BF16)

    def mrow(b, t):
        return jnp.where(t < nct, B, b)

    return _grid_call(
        body, name="inproj_fwd", grid=(B, T // tm),
        out_shape=(jax.ShapeDtypeStruct((B, T, D), BF16), jax.ShapeDtypeStruct((B, T, nw), BF16)),
        in_specs=[ctx_spec(D), lat_spec(D),
                  pl.BlockSpec((None, None, 1, D), lambda b, t: (mrow(b, t), 0, 0, 0)),
                  pl.BlockSpec((None, None, 1, D), lambda b, t: (mrow(b, t), 1, 0, 0)),
                  pl.BlockSpec((1, D), lambda b, t: (0, 0)),
                  pl.BlockSpec((nw, D), lambda b, t: (0, 0))],
        out_specs=(pl.BlockSpec((None, tm, D), lambda b, t: (b, t, 0)),
                   pl.BlockSpec((None, tm, nw), lambda b, t: (b, t, 0))),
        scratch_shapes=[], args=(ctx, x, modl, modl, g1, w_in_t), after=after)


def _swap32(x):
    lane = lax.broadcasted_iota(jnp.int32, x.shape, 1)
    return jnp.where((lane % 64) < 32, pltpu.roll(x, 96, 1), pltpu.roll(x, 32, 1))


def _rope(x, cos, sin):
    return x * cos + _swap32(x) * sin


def _unrope(dy, cos, sin):
    return dy * cos + _swap32(dy * sin)


def _ret_weights(lgf, lgb, dist):
    return jnp.exp(jnp.where(dist >= 0.0, lgf * dist, -lgb * dist))


class _RetDecay:
    def __init__(self, lgf, lgb, rows):
        r = lax.broadcasted_iota(jnp.int32, (rows, RET_DIM), 0).astype(F32)
        self.head = r + 1.0
        self.tail = (rows - 1.0) - r
        self.q_f = jnp.exp(lgf * self.head)
        self.k_f = jnp.exp(lgf * self.tail)
        self.q_b = jnp.exp(lgb * self.tail)
        self.k_b = jnp.exp(lgb * self.head)


def _ret_states(kf32, vs, lgf, lgb, C, c, nt, hf, hb, hfa=None, hba=None):
    dec = _RetDecay(lgf, lgb, c)
    dec_c = _RetDecay(lgf, lgb, C)
    step_f = jnp.exp(jnp.zeros((RET_DIM, RET_DIM), F32) + lgf * c)
    step_b = jnp.exp(jnp.zeros((RET_DIM, RET_DIM), F32) + lgb * c)

    def upd(rows, kdec):
        return _dot_tn((kf32[rows, :] * kdec).astype(BF16), vs[rows, :])

    def lat(t):
        return slice(C + t * c, C + (t + 1) * c)

    state = upd(slice(0, C), dec_c.k_f)
    aged = jnp.zeros_like(state)
    for t in range(nt):
        hf[t] = state.astype(BF16)
        if hfa is not None:
            hfa[t] = aged
        if t < nt - 1:
            aged = step_f * (aged + c * state)
            state = step_f * state + upd(lat(t), dec.k_f)
    state = upd(slice(0, C), dec_c.k_b)
    aged = jnp.zeros_like(state)
    for t in range(nt - 1, -1, -1):
        hb[t] = state.astype(BF16)
        if hba is not None:
            hba[t] = aged
        if t > 0:
            aged = step_b * (aged + c * state)
            state = step_b * state + upd(lat(t), dec.k_b)
    return dec, dec_c, step_f, step_b


def _ret_fwd(proj, cos, sin, lg, gn, n_ctx):
    B, T, _ = proj.shape
    C = n_ctx
    N = T - C
    c = _div_tile(N, 256, 16)
    nt = N // c
    scale = RET_DIM ** -0.5

    def body(lg_ref, q_ref, k_ref, vs, g_ref, cos_ref, sin_ref, gn_ref, o_ref, lat_ref, qr_ref, kf32,
             qs, ks, hf, hb):
        h = pl.program_id(1)
        lgf = lg_ref[0, h]
        lgb = lg_ref[1, h]
        for rows in [slice(0, C)] + [slice(C + t * c, C + (t + 1) * c) for t in range(nt)]:
            cosb = cos_ref[rows, :]
            sinb = sin_ref[rows, :]
            qr = _rope(q_ref[rows, :].astype(F32), cosb, sinb) * scale
            qr_ref[rows, :] = qr
            qs[rows, :] = qr.astype(BF16)
            kr = _rope(k_ref[rows, :].astype(F32), cosb, sinb)
            kf32[rows, :] = kr
            ks[rows, :] = kr.astype(BF16)
        gnv = gn_ref[...]
        dec, _, _, _ = _ret_states(kf32, vs, lgf, lgb, C, c, nt, hf, hb)
        rc = (lax.broadcasted_iota(jnp.int32, (c, c), 0) - lax.broadcasted_iota(jnp.int32, (c, c), 1)).astype(F32)
        w_diag = _ret_weights(lgf, lgb, rc)
        for t in range(nt):
            rows = slice(C + t * c, C + (t + 1) * c)
            qt = qs[rows, :]
            s = _dot_nt(qt, ks[rows, :])
            o = (_dot((s * w_diag).astype(BF16), vs[rows, :])
                 + dec.q_f * _dot(qt, hf[t]) + dec.q_b * _dot(qt, hb[t]))
            o_ref[t * c:(t + 1) * c, :] = o
            mu = jnp.mean(o, axis=-1, keepdims=True)
            oc = o - mu
            var = jnp.mean(oc * oc, axis=-1, keepdims=True)
            yh = oc * lax.rsqrt(var + NORM_EPS)
            g = g_ref[rows, :].astype(F32)
            lat_ref[t * c:(t + 1) * c, :] = ((yh * gnv) * (g * _sigmoid(g))).astype(BF16)

    def col(seg):
        return pl.BlockSpec((None, T, RET_DIM), lambda b, h, seg=seg: (b, 0, seg * RET_HEADS + h))

    return _grid_call(
        body, name="ret_fwd", grid=(B, RET_HEADS),
        out_shape=(jax.ShapeDtypeStruct((B, N, RET_WIDTH), F32), jax.ShapeDtypeStruct((B, N, RET_WIDTH), BF16),
                   jax.ShapeDtypeStruct((B, T, RET_WIDTH), F32), jax.ShapeDtypeStruct((B, T, RET_WIDTH), F32)),
        in_specs=[pl.BlockSpec(memory_space=pltpu.SMEM), col(0), col(1), col(2), col(3),
                  pl.BlockSpec((T, RET_DIM), lambda b, h: (0, 0)), pl.BlockSpec((T, RET_DIM), lambda b, h: (0, 0)),
                  pl.BlockSpec((1, RET_DIM), lambda b, h: (0, h))],
        out_specs=(pl.BlockSpec((None, N, RET_DIM), lambda b, h: (b, 0, h)),
                   pl.BlockSpec((None, N, RET_DIM), lambda b, h: (b, 0, h)),
                   pl.BlockSpec((None, T, RET_DIM), lambda b, h: (b, 0, h)),
                   pl.BlockSpec((None, T, RET_DIM), lambda b, h: (b, 0, h))),
        scratch_shapes=[pltpu.VMEM((T, RET_DIM), BF16)] * 2 + [pltpu.VMEM((nt, RET_DIM, RET_DIM), BF16)] * 2,
        args=(lg, proj, proj, proj, proj, cos, sin, gn))


def _ret_bwd(proj, q_rot, k_rot, cos, sin, lg, gn, o, dlat, n_ctx, after):
    B, T, _ = proj.shape
    C = n_ctx
    N = T - C
    c = _div_tile(N, 256, 16)
    nt = N // c
    scale = RET_DIM ** -0.5

    def lat(t):
        return slice(C + t * c, C + (t + 1) * c)

    def body(lg_ref, qf32, kf32, vs, g_ref, cos_ref, sin_ref, gn_ref, o_ref, dl_ref,
             d_ref, dgn_ref, dlg_ref, qs, ks, dos, hf, hb, hfa, hba, gf_s, gb_s):
        h = pl.program_id(1)
        lgf = lg_ref[0, h]
        lgb = lg_ref[1, h]
        gnv = gn_ref[...]

        def fold(a):
            return jnp.sum(a.reshape(a.shape[0] // SUBLANES, SUBLANES, a.shape[1]), axis=0)

        for rows in [slice(0, C)] + [lat(t) for t in range(nt)]:
            qs[rows, :] = qf32[rows, :].astype(BF16)
            ks[rows, :] = kf32[rows, :].astype(BF16)

        dgn = jnp.zeros((1, RET_DIM), F32)
        for t in range(nt):
            lrows = slice(t * c, (t + 1) * c)
            ov = o_ref[lrows, :]
            mu = jnp.mean(ov, axis=-1, keepdims=True)
            oc = ov - mu
            var = jnp.mean(oc * oc, axis=-1, keepdims=True)
            rstd = lax.rsqrt(var + NORM_EPS)
            yh = oc * rstd
            g = g_ref[lat(t), :].astype(F32)
            sg = _sigmoid(g)
            dl = dl_ref[lrows, :]
            d_ref[3, lat(t), :] = (dl * (yh * gnv) * (sg * (1.0 + g * (1.0 - sg)))).astype(BF16)
            dls = dl * (g * sg)
            dgn = dgn + jnp.sum(dls * yh, axis=0, keepdims=True)
            dyh = dls * gnv
            do = rstd * (dyh - jnp.mean(dyh, axis=-1, keepdims=True)
                         - yh * jnp.mean(dyh * yh, axis=-1, keepdims=True))
            dos[lrows, :] = do.astype(BF16)
        dgn_ref[...] = jnp.concatenate([dgn, jnp.zeros((SUBLANES - 1, RET_DIM), F32)], axis=0)
        d_ref[3, 0:C, :] = jnp.zeros((C, RET_DIM), BF16)
        d_ref[0, 0:C, :] = jnp.zeros((C, RET_DIM), BF16)

        dec, dec_c, step_f, step_b = _ret_states(kf32, vs, lgf, lgb, C, c, nt, hf, hb, hfa, hba)

        def zmat(t, qdec):
            return _dot_tn((qf32[lat(t), :] * qdec).astype(BF16), dos[t * c:(t + 1) * c, :])

        acc3f = jnp.zeros((RET_DIM, RET_DIM), F32)
        acc3b = jnp.zeros((RET_DIM, RET_DIM), F32)
        state = jnp.zeros((RET_DIM, RET_DIM), F32)
        for t in range(nt - 1, -1, -1):
            gf_s[t] = state.astype(BF16)
            z = zmat(t, dec.q_f)
            acc3f = acc3f + hfa[t] * z
            state = step_f * state + z
        gctx_f = state.astype(BF16)
        state = jnp.zeros((RET_DIM, RET_DIM), F32)
        for t in range(nt):
            gb_s[t] = state.astype(BF16)
            z = zmat(t, dec.q_b)
            acc3b = acc3b + hba[t] * z
            state = step_b * state + z
        gctx_b = state.astype(BF16)

        rc = (lax.broadcasted_iota(jnp.int32, (c, c), 0) - lax.broadcasted_iota(jnp.int32, (c, c), 1)).astype(F32)
        w_diag = _ret_weights(lgf, lgb, rc)
        wg_f = jnp.where(rc >= 0.0, w_diag * rc, 0.0)
        wg_b = jnp.where(rc < 0.0, -w_diag * rc, 0.0)
        accf = jnp.zeros((SUBLANES, RET_DIM), F32)
        accb = jnp.zeros((SUBLANES, RET_DIM), F32)
        gdf = jnp.zeros((SUBLANES, c), F32)
        gdb = jnp.zeros((SUBLANES, c), F32)
        for t in range(nt):
            rows = lat(t)
            qt = qs[rows, :]
            kt = ks[rows, :]
            vt = vs[rows, :]
            dot = dos[t * c:(t + 1) * c, :]
            s = _dot_nt(qt, kt)
            dp = _dot_nt(dot, vt)
            dv = _dot_tn((s * w_diag).astype(BF16), dot)
            ds = (dp * w_diag).astype(BF16)
            dq = _dot(ds, kt)
            dk = _dot_tn(ds, qt)
            gs = dp * s
            gdf = gdf + fold(gs * wg_f)
            gdb = gdb + fold(gs * wg_b)
            qv = qf32[rows, :]
            kv = kf32[rows, :]
            dq_f = dec.q_f * _dot_nt(dot, hf[t])
            dq_b = dec.q_b * _dot_nt(dot, hb[t])
            dk_f = dec.k_f * _dot_nt(vt, gf_s[t])
            dk_b = dec.k_b * _dot_nt(vt, gb_s[t])
            accf = accf + fold(dec.head * dq_f * qv) + fold(dec.tail * dk_f * kv)
            accb = accb + fold(dec.tail * dq_b * qv) + fold(dec.head * dk_b * kv)
            dv = dv + dec.k_f * _dot(kt, gf_s[t]) + dec.k_b * _dot(kt, gb_s[t])
            cosb = cos_ref[rows, :]
            sinb = sin_ref[rows, :]
            d_ref[0, rows, :] = _unrope((dq + dq_f + dq_b) * scale, cosb, sinb).astype(BF16)
            d_ref[1, rows, :] = _unrope(dk + dk_f + dk_b, cosb, sinb).astype(BF16)
            d_ref[2, rows, :] = dv.astype(BF16)
        kc = ks[0:C, :]
        vc = vs[0:C, :]
        kcv = kf32[0:C, :]
        dkc_f = dec_c.k_f * _dot_nt(vc, gctx_f)
        dkc_b = dec_c.k_b * _dot_nt(vc, gctx_b)
        accf = accf + fold(dec_c.tail * dkc_f * kcv)
        accb = accb + fold(dec_c.head * dkc_b * kcv)
        d_ref[1, 0:C, :] = (dkc_f + dkc_b).astype(BF16)
        d_ref[2, 0:C, :] = (dec_c.k_f * _dot(kc, gctx_f) + dec_c.k_b * _dot(kc, gctx_b)).astype(BF16)
        gf = jnp.sum(gdf) + jnp.sum(accf) + jnp.sum(acc3f)
        gb = jnp.sum(gdb) + jnp.sum(accb) + jnp.sum(acc3b)
        row = lax.broadcasted_iota(jnp.int32, (SUBLANES, LANES), 0)
        dlg_ref[...] = jnp.where(row == 0, gf, jnp.where(row == 1, gb, 0.0))

    def col(seg):
        return pl.BlockSpec((None, T, RET_DIM), lambda b, h, seg=seg: (b, 0, seg * RET_HEADS + h))

    def head(rows):
        return pl.BlockSpec((None, rows, RET_DIM), lambda b, h: (b, 0, h))

    return _grid_call(
        body, name="ret_bwd", grid=(B, RET_HEADS),
        out_shape=(jax.ShapeDtypeStruct((B, 4, T, RET_WIDTH), BF16),
                   jax.ShapeDtypeStruct((B, SUBLANES, RET_WIDTH), F32),
                   jax.ShapeDtypeStruct((B, RET_HEADS, SUBLANES, LANES), F32)),
        in_specs=[pl.BlockSpec(memory_space=pltpu.SMEM), head(T), head(T), col(2), col(3),
                  pl.BlockSpec((T, RET_DIM), lambda b, h: (0, 0)), pl.BlockSpec((T, RET_DIM), lambda b, h: (0, 0)),
                  pl.BlockSpec((1, RET_DIM), lambda b, h: (0, h)), head(N), head(N)],
        out_specs=(pl.BlockSpec((None, 4, T, RET_DIM), lambda b, h: (b, 0, 0, h)),
                   pl.BlockSpec((None, SUBLANES, RET_DIM), lambda b, h: (b, 0, h)),
                   pl.BlockSpec((None, None, SUBLANES, LANES), lambda b, h: (b, h, 0, 0))),
        scratch_shapes=[pltpu.VMEM((T, RET_DIM), BF16)] * 2 + [pltpu.VMEM((N, RET_DIM), BF16)]
                       + [pltpu.VMEM((nt, RET_DIM, RET_DIM), BF16)] * 2 + [pltpu.VMEM((nt, RET_DIM, RET_DIM), F32)] * 2
                       + [pltpu.VMEM((nt, RET_DIM, RET_DIM), BF16)] * 2,
        args=(lg, q_rot, k_rot, proj, proj, cos, sin, gn, o, dlat), after=after)


def _na_geometry(rows):
    kh = min(NA_KH, rows)
    return kh, kh * GRID_W


def _pair_select():
    lane = lax.broadcasted_iota(jnp.int32, (2 * GRID_W, LANES), 1)
    row = lax.broadcasted_iota(jnp.int32, (2 * GRID_W, LANES), 0)
    return (lane >= NA_DIM) == (row >= GRID_W)


def _pair_bias(bias_ref, dr0, kh):
    return jnp.concatenate(
        [jnp.concatenate([bias_ref[e, pl.ds(dr0 + 2 * m, 1)].reshape(GRID_W, LANES) for m in range(kh // 2)], axis=1)
         for e in range(2)], axis=0)


def _na_softmax(s_loc, s_ctx):
    mx = jnp.maximum(jnp.max(s_loc, axis=-1, keepdims=True), jnp.max(s_ctx, axis=-1, keepdims=True))
    p_loc = jnp.exp(s_loc - mx)
    p_ctx = jnp.exp(s_ctx - mx)
    den = jnp.sum(p_loc, axis=-1, keepdims=True) + jnp.sum(p_ctx, axis=-1, keepdims=True)
    return p_loc, p_ctx, den


def _na_fwd(proj, bias2, n_ctx):
    assert proj.dtype == BF16
    B, T, _ = proj.shape
    C = n_ctx
    N = T - C
    R = N // GRID_W
    kh, nk = _na_geometry(R)
    scale = NA_DIM ** -0.5
    base = (4 * RET_WIDTH) // LANES

    def body(q_ref, kb16, vb16, bias_ref, out_ref, p_ref):
        kc = kb16[0:C, :]
        vc = vb16[0:C, :]
        lane = lax.broadcasted_iota(jnp.int32, (GRID_W, LANES), 1)
        sel2 = _pair_select()

        def group(gi, carry):
            pre = []
            for u in range(NA_GROUP):
                r = gi * NA_GROUP + u
                bs = jnp.clip(r - kh // 2, 0, R - kh)
                dr0 = bs - r + (NA_KH - 1)
                q = q_ref[pl.ds(pl.multiple_of(C + r * GRID_W, GRID_W), GRID_W), :].astype(F32) * scale
                q2 = jnp.where(sel2, jnp.concatenate([q, q], axis=0), 0.0).astype(BF16)
                band = pl.ds(pl.multiple_of(C + bs * GRID_W, GRID_W), nk)
                s_loc = _dot_nt(q2, kb16[band, :]) + _pair_bias(bias_ref, dr0, kh)
                s_ctx = _dot_nt(q2, kc)
                pre.append((r, band, s_loc, s_ctx))
            mid = [(r, band) + _na_softmax(s_loc, s_ctx) for r, band, s_loc, s_ctx in pre]
            for r, band, p_loc, p_ctx, den in mid:
                inv = 1.0 / den
                pb_loc = (p_loc * inv).astype(BF16)
                pb_ctx = (p_ctx * inv).astype(BF16)
                p_ref[r, :, 0:nk] = pb_loc
                p_ref[r, :, nk:] = pb_ctx
                o2 = _dot(pb_loc, vb16[band, :]) + _dot(pb_ctx, vc)
                out_ref[pl.ds(pl.multiple_of(r * GRID_W, GRID_W), GRID_W), :] = jnp.where(
                    lane < NA_DIM, o2[:GRID_W], o2[GRID_W:]).astype(BF16)
            return carry

        lax.fori_loop(0, R // NA_GROUP, group, 0)

    def col(seg):
        return pl.BlockSpec((None, T, LANES), lambda b, p, seg=seg: (b, 0, base + seg * NA_PAIRS + p))

    return _grid_call(
        body, name="na_fwd", grid=(B, NA_PAIRS),
        out_shape=(jax.ShapeDtypeStruct((B, N, NA_WIDTH), BF16),
                   jax.ShapeDtypeStruct((B, NA_PAIRS, R, 2 * GRID_W, nk + C), BF16)),
        in_specs=[col(0), col(1), col(2),
                  pl.BlockSpec((2, 2 * NA_KH - 2, GRID_W, LANES), lambda b, p: (p, 0, 0, 0))],
        out_specs=(pl.BlockSpec((None, N, LANES), lambda b, p: (b, 0, p)),
                   pl.BlockSpec((None, None, R, 2 * GRID_W, nk + C), lambda b, p: (b, p, 0, 0, 0))),
        scratch_shapes=[],
        args=(proj, proj, proj, bias2))


def _na_bwd(proj, probs, dlat, n_ctx):
    assert proj.dtype == BF16
    B, T, _ = proj.shape
    C = n_ctx
    N = T - C
    R = N // GRID_W
    kh, nk = _na_geometry(R)
    scale = NA_DIM ** -0.5
    base = (4 * RET_WIDTH) // LANES

    def body(q_ref, kb16, vb16, p_ref, dl_ref, d_ref, db_ref, dkv):
        b = pl.program_id(1)
        kc = kb16[0:C, :]
        vc = vb16[0:C, :]
        lane = lax.broadcasted_iota(jnp.int32, (GRID_W, LANES), 1)
        dkv[...] = jnp.zeros(dkv.shape, F32)
        d_ref[0, 0:C, :] = jnp.zeros((C, LANES), BF16)

        @pl.when(b == 0)
        def _():
            db_ref[...] = jnp.zeros(db_ref.shape, F32)

        sel2 = _pair_select()

        def group(gi, carry):
            pre = []
            for u in range(NA_GROUP):
                r = gi * NA_GROUP + u
                bs = jnp.clip(r - kh // 2, 0, R - kh)
                dr0 = bs - r + (NA_KH - 1)
                q = q_ref[pl.ds(pl.multiple_of(C + r * GRID_W, GRID_W), GRID_W), :].astype(F32) * scale
                do = dl_ref[pl.ds(pl.multiple_of(r * GRID_W, GRID_W), GRID_W), :]
                q2 = jnp.where(sel2, jnp.concatenate([q, q], axis=0), 0.0).astype(BF16)
                do2 = jnp.where(sel2, jnp.concatenate([do, do], axis=0), 0.0).astype(BF16)
                band = pl.ds(pl.multiple_of(C + bs * GRID_W, GRID_W), nk)
                dp_loc = _dot_nt(do2, vb16[band, :])
                dp_ctx = _dot_nt(do2, vc)
                pre.append((r, dr0, band, q2, do2, dp_loc, dp_ctx))
            mid = []
            for r, dr0, band, q2, do2, dp_loc, dp_ctx in pre:
                pb_loc = p_ref[r, :, 0:nk]
                pb_ctx = p_ref[r, :, nk:]
                p_loc = pb_loc.astype(F32)
                p_ctx = pb_ctx.astype(F32)
                delta = (jnp.sum(p_loc * dp_loc, axis=-1, keepdims=True)
                         + jnp.sum(p_ctx * dp_ctx, axis=-1, keepdims=True))
                ds_loc = p_loc * (dp_loc - delta)
                ds_ctx = p_ctx * (dp_ctx - delta)
                mid.append((r, dr0, band, q2, do2, pb_loc, pb_ctx, ds_loc, ds_ctx))
            for r, dr0, band, q2, do2, pb_loc, pb_ctx, ds_loc, ds_ctx in mid:
                dsb_loc = ds_loc.astype(BF16)
                dsb_ctx = ds_ctx.astype(BF16)
                dq2 = _dot(dsb_loc, kb16[band, :]) + _dot(dsb_ctx, kc)
                d_ref[0, pl.ds(pl.multiple_of(C + r * GRID_W, GRID_W), GRID_W), :] = (jnp.where(
                    lane < NA_DIM, dq2[:GRID_W], dq2[GRID_W:]) * scale).astype(BF16)
                dkv[0, band, :] += _dot_tn(dsb_loc, q2)
                dkv[1, band, :] += _dot_tn(pb_loc, do2)
                dkv[0, 0:C, :] += _dot_tn(dsb_ctx, q2)
                dkv[1, 0:C, :] += _dot_tn(pb_ctx, do2)
                for e in range(2):
                    for m in range(kh // 2):
                        db_ref[e, pl.ds(dr0 + 2 * m, 1)] += ds_loc[e * GRID_W:(e + 1) * GRID_W,
                                                                   m * LANES:(m + 1) * LANES].reshape(1, GRID_W, LANES)
            return carry

        lax.fori_loop(0, R // NA_GROUP, group, 0)
        d_ref[1] = dkv[0].astype(BF16)
        d_ref[2] = dkv[1].astype(BF16)

    def col(seg):
        return pl.BlockSpec((None, T, LANES), lambda p, b, seg=seg: (b, 0, base + seg * NA_PAIRS + p))

    return _grid_call(
        body, name="na_bwd", grid=(NA_PAIRS, B),
        out_shape=(jax.ShapeDtypeStruct((B, 3, T, NA_WIDTH), BF16),
                   jax.ShapeDtypeStruct((NA_HEADS, 2 * NA_KH - 2, GRID_W, LANES), F32)),
        in_specs=[col(0), col(1), col(2),
                  pl.BlockSpec((None, None, R, 2 * GRID_W, nk + C), lambda p, b: (b, p, 0, 0, 0)),
                  pl.BlockSpec((None, N, LANES), lambda p, b: (b, 0, p))],
        out_specs=(pl.BlockSpec((None, 3, T, LANES), lambda p, b: (b, 0, 0, p)),
                   pl.BlockSpec((2, 2 * NA_KH - 2, GRID_W, LANES), lambda p, b: (p, 0, 0, 0))),
        scratch_shapes=[pltpu.VMEM((2, T, LANES), F32)],
        args=(proj, proj, proj, probs, dlat))


def _split3(a):
    hi = a.astype(BF16)
    r1 = a - hi.astype(F32)
    mid = r1.astype(BF16)
    lo = (r1 - mid.astype(F32)).astype(BF16)
    return hi, mid, lo


def _rpb_reduce(dbias2, onehot2):
    rows = dbias2.shape[0] * dbias2.shape[1]
    flat = dbias2.reshape(rows, GRID_W * LANES)

    def body(a_ref, oh_ref, o_ref):
        hi, mid, lo = _split3(a_ref[...])
        oh = oh_ref[...]
        o_ref[...] = _dot(hi, oh) + _dot(mid, oh) + _dot(lo, oh)

    return pl.pallas_call(
        body, name="rpb_reduce", out_shape=jax.ShapeDtypeStruct((rows, LANES), F32),
        in_specs=[_vmem(), _vmem()], out_specs=_vmem(),
        compiler_params=pltpu.CompilerParams(vmem_limit_bytes=VMEM_LIMIT),
    )(flat, onehot2)


def _dense_core(lat_ret, lat_na, x, tgt, modl, g_post_mix, g_pre_mlp, g_post_mlp, w_out, w1, w2):
    B, N, D = x.shape
    F = w1.shape[1]
    wout_rows, w1_cols, w2_rows = w_out.shape[0] // N_DEV, w1.shape[1] // N_DEV, w2.shape[0] // N_DEV
    mixw = w_out.shape[0]
    half = mixw // 2
    tm = _div_tile(N, 256, 16)
    nt = N // tm
    fc = _div_tile(F, 1024, LANES)

    def body(lr_ref, ln_ref, x_ref, t_ref, gt1_ref, sh2_ref, sc2_ref, gt2_ref, gpm_ref, gpre_ref, gpo_ref,
             wout_part, w1_part, w2_part,
             dy1_ref, dlr_ref, dln_ref, dmix_ref, h2_ref, a_ref, du_ref, dz_ref, red_ref, wout_hbm, w1_hbm, w2_hbm,
             wout_v, w1_v, w2_v, u_s, sems, fsend, frecv):
        @pl.when((pl.program_id(0) == 0) & (pl.program_id(1) == 0))
        def _():
            relay = [(_row_block(wout_part, wout_rows), _row_block(wout_hbm, wout_rows)),
                     (_col_block(w1_part, w1_cols), _col_block(w1_hbm, w1_cols)),
                     (_row_block(w2_part, w2_rows), _row_block(w2_hbm, w2_rows))]
            _forward_start(relay, fsend, frecv)
            _forward_wait(relay, fsend, frecv)
            cps = [pltpu.make_async_copy(wout_hbm, wout_v, sems.at[0]),
                   pltpu.make_async_copy(w1_hbm, w1_v, sems.at[1]),
                   pltpu.make_async_copy(w2_hbm, w2_v, sems.at[2])]
            for cp in cps:
                cp.start()
            for cp in cps:
                cp.wait()

        @pl.when(pl.program_id(1) == 0)
        def _():
            red_ref[...] = jnp.zeros(red_ref.shape, F32)

        gt1 = gt1_ref[...]
        sh2 = sh2_ref[...]
        sc2 = sc2_ref[...]
        gt2 = gt2_ref[...]
        gpm = gpm_ref[...]
        gpre = gpre_ref[...]
        gpo = gpo_ref[...]

        def rowmean(a):
            return jnp.mean(a, axis=-1, keepdims=True)

        def colsum(a):
            return jnp.sum(a, axis=0, keepdims=True)

        mix_gain = gt1 * gpm
        mlp_in_gain = gpre * (1.0 + sc2)
        mlp_out_gain = gt2 * gpo
        mix = _dot(lr_ref[...], wout_v[0:half, :]) + _dot(ln_ref[...], wout_v[half:, :])
        x = x_ref[...]
        rm = lax.rsqrt(rowmean(mix * mix) + NORM_EPS)
        mh = mix * rm
        y1 = x + mh * mix_gain
        r1 = lax.rsqrt(rowmean(y1 * y1) + NORM_EPS)
        xh = y1 * r1
        h2b = (xh * mlp_in_gain + sh2).astype(BF16)
        h2_ref[...] = h2b
        z = jnp.zeros((tm, D), F32)
        for c0 in range(0, F, fc):
            u = _dot(h2b, w1_v[:, c0:c0 + fc])
            u_s[:, c0:c0 + fc] = u
            ru = jnp.maximum(u, 0.0)
            ab = (ru * ru).astype(BF16)
            a_ref[:, c0:c0 + fc] = ab
            z = z + _dot(ab, w2_v[c0:c0 + fc, :])
        r2 = lax.rsqrt(rowmean(z * z) + NORM_EPS)
        zh = z * r2
        y2 = y1 + zh * mlp_out_gain
        err = y2 - t_ref[...]
        loss = 0.5 * jnp.sum(rowmean(err * err))
        dy2 = err * (1.0 / D)
        s_out = colsum(dy2 * zh)
        red_ref[2:3, :] += s_out * gpo
        red_ref[6:7, :] += s_out * gt2
        dzh = dy2 * mlp_out_gain
        dz = r2 * (dzh - zh * rowmean(dzh * zh))
        dzb = dz.astype(BF16)
        dz_ref[...] = dzb
        dh2 = jnp.zeros((tm, D), F32)
        for c0 in range(0, F, fc):
            da = _dot_nt(dzb, w2_v[c0:c0 + fc, :])
            dub = (da * (2.0 * jnp.maximum(u_s[:, c0:c0 + fc], 0.0))).astype(BF16)
            du_ref[:, c0:c0 + fc] = dub
            dh2 = dh2 + _dot_nt(dub, w1_v[:, c0:c0 + fc])
        s_in = colsum(dh2 * xh)
        red_ref[3:4, :] += s_in * gpre
        red_ref[4:5, :] += colsum(dh2)
        red_ref[5:6, :] += s_in * (1.0 + sc2)
        dxh = dh2 * mlp_in_gain
        dy1 = dy2 + r1 * (dxh - xh * rowmean(dxh * xh))
        dy1_ref[...] = dy1
        s_mix = colsum(dy1 * mh)
        red_ref[0:1, :] += s_mix * gpm
        red_ref[1:2, :] += s_mix * gt1
        dmh = dy1 * mix_gain
        dmix = (rm *(dmh - mh * rowmean(dmh * mh))).astype(BF16)
        dmix_ref[...] = dmix
        dlr_ref[...] = _dot_nt(dmix, wout_v[0:half, :])
        dln_ref[...] = _dot_nt(dmix, wout_v[half:, :])
        red_ref[7:8, :] += jnp.zeros((1, D), F32) + loss

    def tok(w):
        return pl.BlockSpec((None, tm, w), lambda b, t: (b, t, 0))

    def mod(k):
        return pl.BlockSpec((None, None, 1, D), lambda b, t, k=k: (b, k, 0, 0))

    def vec():
        return pl.BlockSpec((1, D), lambda b, t: (0, 0))

    return pl.pallas_call(
        body, name="dense_core", grid=(B, nt),
        out_shape=(jax.ShapeDtypeStruct((B, N, D), F32), jax.ShapeDtypeStruct((B, N, half), F32),
                   jax.ShapeDtypeStruct((B, N, half), F32), jax.ShapeDtypeStruct((B, N, D), BF16),
                   jax.ShapeDtypeStruct((B, N, D), BF16), jax.ShapeDtypeStruct((B, N, F), BF16),
                   jax.ShapeDtypeStruct((B, N, F), BF16), jax.ShapeDtypeStruct((B, N, D), BF16),
                   jax.ShapeDtypeStruct((B, SUBLANES, D), F32),
                   jax.ShapeDtypeStruct(w_out.shape, w_out.dtype), jax.ShapeDtypeStruct(w1.shape, w1.dtype),
                   jax.ShapeDtypeStruct(w2.shape, w2.dtype)),
        in_specs=[tok(half), tok(half), tok(D), tok(D), mod(2), mod(3), mod(4), mod(5), vec(), vec(), vec(),
                  _any(), _any(), _any()],
        out_specs=(tok(D), tok(half), tok(half), tok(D), tok(D), tok(F), tok(F), tok(D),
                   pl.BlockSpec((None, SUBLANES, D), lambda b, t: (b, 0, 0)), _any(), _any(), _any()),
        scratch_shapes=[pltpu.VMEM((mixw, D), BF16), pltpu.VMEM((D, F), BF16), pltpu.VMEM((F, D), BF16),
                        pltpu.VMEM((tm, F), F32), pltpu.SemaphoreType.DMA((3,)),
                        pltpu.SemaphoreType.DMA((3, 3)), pltpu.SemaphoreType.DMA((3, 3))],
        input_output_aliases={11: 9, 12: 10, 13: 11},
        compiler_params=_params("arbitrary", "arbitrary"),
    )(lat_ret, lat_na, x, tgt, modl, modl, modl, modl, g_post_mix, g_pre_mlp, g_post_mlp, w_out, w1, w2)[:9]


def _inproj_bwd(dret, dna, x, ctx, dy1, modl, g1, w_in_t, after):
    B, N, D = x.shape
    n_ctx = ctx.shape[1]
    T = n_ctx + N
    tm = _div_tile(n_ctx, 256, 16)
    nct, ctx_spec, lat_spec = _token_tiles(n_ctx, tm)
    nt = T // tm
    nseg_r = dret.shape[1]
    nseg_n = dna.shape[1]
    nw = w_in_t.shape[0]

    def body(*refs):
        seg_refs = refs[:nseg_r + nseg_n]
        c_ref, x_ref, dy1_ref, sc_ref, g_ref, w_ref, dx_ref, red_ref = refs[nseg_r + nseg_n:]
        t = pl.program_id(1)
        dh = jnp.zeros((tm, D), F32)
        for s, ref in enumerate(seg_refs):
            dh = dh + _dot(ref[...], w_ref[s * SEG:(s + 1) * SEG, :])
        x = jnp.where(t < nct, c_ref[...], x_ref[...])
        g = g_ref[...]
        r = lax.rsqrt(jnp.mean(x * x, axis=-1, keepdims=True) + NORM_EPS)
        xh = x * r
        gain = 1.0 + sc_ref[...]
        s_in = jnp.sum(dh * xh, axis=0, keepdims=True)
        red_ref[0:1, :] = jnp.sum(dh, axis=0, keepdims=True)
        red_ref[1:2, :] = s_in * g
        red_ref[2:3, :] = s_in * gain
        red_ref[3:, :] = jnp.zeros((SUBLANES - 3, D), F32)
        dxh = dh * (g * gain)
        dx = r * (dxh - xh * jnp.mean(dxh * xh, axis=-1, keepdims=True))
        dx_ref[...] = dx + jnp.where(t >= nct, dy1_ref[...], 0.0)

    def mrow(b, t):
        return jnp.where(t < nct, B, b)

    def seg(s):
        return pl.BlockSpec((None, None, tm, SEG), lambda b, t, s=s: (b, s, t, 0))

    return _grid_call(
        body, name="inproj_bwd", grid=(B, nt),
        out_shape=(jax.ShapeDtypeStruct((B, N, D), F32), jax.ShapeDtypeStruct((B, nt, SUBLANES, D), F32)),
        in_specs=[seg(s) for s in range(nseg_r)] + [seg(s) for s in range(nseg_n)]
                 + [ctx_spec(D), lat_spec(D), lat_spec(D),
                    pl.BlockSpec((None, None, 1, D), lambda b, t: (mrow(b, t), 1, 0, 0)),
                    pl.BlockSpec((1, D), lambda b, t: (0, 0)),
                    pl.BlockSpec((nw, D), lambda b, t: (0, 0))],
        out_specs=(lat_spec(D), pl.BlockSpec((None, None, SUBLANES, D), lambda b, t: (b, t, 0, 0))),
        scratch_shapes=[], args=(*([dret] * nseg_r), *([dna] * nseg_n), ctx, x, dy1, modl, g1, w_in_t), after=after)


def _tn_matmul(lhs, rhs, name, rows_before=0, rows_after=0, into=None):
    B, S, T, W = lhs.shape
    nn = rhs.shape[-1]
    tk = _div_tile(T, 2304, LANES)
    bm = _div_tile(W, 1024, LANES)
    bn = _div_tile(nn, 1024, LANES)
    nkt = T // tk
    nk = B * nkt

    def body(l_ref, r_ref, *rest):
        o_ref, acc = rest[-2:]
        k = pl.program_id(3)

        @pl.when(k == 0)
        def _():
            acc[...] = jnp.zeros(acc.shape, F32)

        acc[...] += _dot_tn(l_ref[...].astype(BF16), r_ref[...].astype(BF16))

        @pl.when(k == nk - 1)
        def _():
            o_ref[...] = acc[...].astype(BF16)

    nwb = W // bm
    first = rows_before // bm
    return pl.pallas_call(
        functools.partial(body), name=name, grid=(S, nwb, nn // bn, nk),
        out_shape=jax.ShapeDtypeStruct((rows_before + S * W + rows_after, nn), BF16),
        in_specs=[pl.BlockSpec((None, None, tk, bm), lambda s, i, j, k: (k // nkt, s, k % nkt, i)),
                  pl.BlockSpec((None, tk, bn), lambda s, i, j, k: (k // nkt, k % nkt, j))]
                 + ([] if into is None else [_any()]),
        out_specs=pl.BlockSpec((bm, bn), lambda s, i, j, k: (first + s * nwb + i, j)),
        scratch_shapes=[pltpu.VMEM((bm, bn), F32)],
        input_output_aliases={} if into is None else {2: 0},
        compiler_params=_params("parallel", "parallel", "parallel", "arbitrary"),
    )(lhs, rhs, *([] if into is None else [into]))


class _SplitScatter:
    def __init__(self, gs, block_ofs, land_shapes, name, kind="scatter", masks=ALL_PEERS):
        self.n = n = len(gs)
        self.block_ofs, self.kind, self.masks = block_ofs, kind, masks
        if kind == "scatter":
            land_shapes = [(N_DEV,) + tuple(bs) for bs in land_shapes]
        hbm = pl.BlockSpec(memory_space=pltpu.HBM)
        sem = pl.BlockSpec(memory_space=pltpu.SEMAPHORE)

        def body(*refs):
            g_refs, land_refs = refs[:n], refs[n:2 * n]
            send_sems, recv_sems, own_sems = refs[2 * n:2 * n + 3]
            token = refs[-1]
            for own, pushes in self._copies(g_refs, land_refs, send_sems, recv_sems, own_sems, landing="sender"):
                own.start()
                for cp in pushes:
                    cp.start()
            token[...] = jnp.zeros_like(token)

        outs = pl.pallas_call(
            body, name=name,
            out_shape=(pltpu.SemaphoreType.DMA((n * (N_DEV - 1),)), pltpu.SemaphoreType.DMA((n * (N_DEV - 1),)),
                       pltpu.SemaphoreType.DMA((n,)))
                      + tuple(pltpu.HBM(g.shape, g.dtype) for g in gs)
                      + tuple(pltpu.HBM(s, g.dtype) for s, g in zip(land_shapes, gs))
                      + (jax.ShapeDtypeStruct((SUBLANES, LANES), F32),),
            in_specs=(hbm,) * (2 * n), out_specs=(sem,) * 3 + (hbm,) * (2 * n) + (_vmem(),),
            input_output_aliases={k: 3 + k for k in range(2 * n)},
            compiler_params=pltpu.CompilerParams(has_side_effects=pltpu.SideEffectType.DATAFLOW_SIDE_EFFECTING),
        )(*[pltpu.with_memory_space_constraint(g, pltpu.HBM) for g in gs],
          *[pltpu.with_memory_space_constraint(lax.empty(s, g.dtype), pltpu.HBM) for s, g in zip(land_shapes, gs)])
        self.sems, self.thru, self.token = outs[:3], outs[3:3 + 2 * n], outs[-1]

    def _copies(self, g_refs, land_refs, send_sems, recv_sems, own_sems, landing):
        me, peers = _me_and_peers()
        out = []
        for k in range(self.n):
            if self.kind == "scatter":
                src, dst = self.block_ofs[k](g_refs[k]), _slot(land_refs[k])
            else:
                src, dst = (lambda p, k=k: g_refs[k]), self.block_ofs[k](land_refs[k])
            own = pltpu.make_async_copy(src(me), dst(me), own_sems.at[k]) if landing == "sender" else None
            pushes = []
            for m in self.masks:
                dev, pid = peers[m - 1]
                i = k * (N_DEV - 1) + m - 1
                pushes.append(_remote(src(pid), dst(me if landing == "sender" else pid),
                                      send_sems.at[i], recv_sems.at[i], dev))
            out.append((own, pushes))
        return out


def _scatter_wait(scatters, after, name):
    hbm = pl.BlockSpec(memory_space=pltpu.HBM)
    sem = pl.BlockSpec(memory_space=pltpu.SEMAPHORE)
    n_arr = [2 * sc.n for sc in scatters]
    total = sum(n_arr)

    def body(*refs):
        arrs, sems = refs[:total], refs[total:total + 3 * len(scatters)]
        a0 = 0
        for j, sc in enumerate(scatters):
            g_refs, land_refs = arrs[a0:a0 + sc.n], arrs[a0 + sc.n:a0 + 2 * sc.n]
            a0 += 2 * sc.n
            send_sems, recv_sems, own_sems = sems[3 * j:3 * j + 3]
            for (own, sent), (_, got) in zip(sc._copies(g_refs, land_refs, send_sems, recv_sems, own_sems, "sender"),
                                             sc._copies(g_refs, land_refs, send_sems, recv_sems, own_sems, "receiver")):
                own.wait()
                for cp in sent:
                    cp.wait_send()
                for cp in got:
                    cp.wait_recv()

    operands = [a for sc in scatters for a in sc.thru]
    outs = pl.pallas_call(
        body, name=name,
        out_shape=tuple(pltpu.HBM(a.shape, a.dtype) for a in operands),
        in_specs=(hbm,) * total + (sem,) * (3 * len(scatters)) + (pl.BlockSpec(memory_space=pl.ANY),),
        out_specs=(hbm,) * total, input_output_aliases={k: k for k in range(total)},
        compiler_params=pltpu.CompilerParams(has_side_effects=pltpu.SideEffectType.DATAFLOW_SIDE_EFFECTING),
    )(*operands, *[s for sc in scatters for s in sc.sems], after)
    lands, a0 = [], 0
    for sc in scatters:
        lands.extend(outs[a0 + sc.n:a0 + 2 * sc.n])
        a0 += 2 * sc.n
    return lands


def _small_ar(mbuf, silu_all, w_ada, c_ctx, n_mod_rows, n_vec_rows):
    D = silu_all.shape[1]
    ncol = w_ada.shape[1]
    nm = mbuf.shape[2]
    srows = silu_all.shape[0]

    def body(mbuf, s_ref, w_ref, cc_ref, tot_ref, gb_ref, gw_ref, gc_ref, tbuf, dmx, cmrow, send3, recv3):
        me, _ = _me_and_peers()
        msum = mbuf[0]
        for k in range(1, N_DEV):
            msum = msum + mbuf[k]
        tot_ref[...] = msum[n_mod_rows:n_mod_rows + n_vec_rows]
        gb_ref[...] = jnp.sum(msum[0:n_mod_rows], axis=0, keepdims=True)
        loc = pl.ds(pl.multiple_of(me * ncol, ncol), ncol)
        for k in range(N_DEV):
            dmx[k * SUBLANES:(k + 1) * SUBLANES, :] = mbuf[k, :, loc]
        cmrow[...] = msum
        cm_loc = cmrow[n_mod_rows - 1:n_mod_rows, loc]
        dmx[N_DEV * SUBLANES:, :] = jnp.concatenate([cm_loc, jnp.zeros((SUBLANES - 1, ncol), F32)], axis=0)
        gw_ref[...] = _dot_tn(s_ref[...], dmx[...])
        tbuf[me] = _dot_nt(dmx[N_DEV * SUBLANES:, :], w_ref[...])
        _exchange(lambda p: tbuf.at[me], lambda p: tbuf.at[p], send3, recv3)
        tsum = tbuf[0]
        for k in range(1, N_DEV):
            tsum = tsum + tbuf[k]
        cc = cc_ref[...]
        sg = _sigmoid(cc)
        gc_ref[...] = tsum[0:1, :] * (sg * (1.0 + cc * (1.0 - sg)))

    return pl.pallas_call(
        body, name="small_ar",
        out_shape=(jax.ShapeDtypeStruct((n_vec_rows, nm), F32), jax.ShapeDtypeStruct((1, nm), F32),
                   jax.ShapeDtypeStruct((D, ncol), F32), jax.ShapeDtypeStruct((1, D), F32)),
        in_specs=[_vmem()] * 4, out_specs=(_vmem(),) * 4,
        scratch_shapes=[pltpu.VMEM((N_DEV, SUBLANES, D), F32), pltpu.VMEM((srows, ncol), F32),
                        pltpu.VMEM((SUBLANES, nm), F32)] + [pltpu.SemaphoreType.DMA((N_DEV - 1,))] * 2,
        compiler_params=pltpu.CompilerParams(vmem_limit_bytes=VMEM_LIMIT),
    )(mbuf, silu_all, w_ada, c_ctx.reshape(1, D))


def _adam_update(w, g, m, v):
    mn = ADAM_B1 * m + (1.0 - ADAM_B1) * g
    vn = ADAM_B2 * v + (1.0 - ADAM_B2) * (g * g)
    m_hat = mn / (1.0 - ADAM_B1 ** ADAM_STEP)
    v_hat = vn / (1.0 - ADAM_B2 ** ADAM_STEP)
    return -ADAM_LR * (m_hat / (jnp.sqrt(v_hat) + ADAM_EPS) + ADAM_WD * w), mn, vn


def _adamw(w, g, m, v, name):
    rows, cols = w.shape
    tr = _div_tile(rows, 512, SUBLANES)

    def body(w_ref, g_ref, m_ref, v_ref, d_ref, nm_ref, nv_ref):
        d_ref[...], nm_ref[...], nv_ref[...] = _adam_update(w_ref[...], g_ref[...], m_ref[...], v_ref[...])

    spec = pl.BlockSpec((tr, cols), lambda i: (i, 0))
    return pl.pallas_call(
        functools.partial(body), name=name, grid=(rows // tr,),
        out_shape=(jax.ShapeDtypeStruct((rows, cols), F32),) * 3,
        in_specs=[spec] * 4, out_specs=(spec,) * 3,
        compiler_params=_params("parallel"),
    )(w, g, m, v)


def _adamw_small(items, name):
    n = len(items)

    def body(*refs):
        ins, outs = refs[:4 * n], refs[4 * n:]
        for i in range(n):
            w_ref, g_ref, m_ref, v_ref = ins[4 * i:4 * i + 4]
            outs[3 * i][...], outs[3 * i + 1][...], outs[3 * i + 2][...] = _adam_update(
                w_ref[...], g_ref[...], m_ref[...], v_ref[...])

    outs = pl.pallas_call(
        body, name=name,
        out_shape=tuple(jax.ShapeDtypeStruct(it[0].shape, F32) for it in items for _ in range(3)),
        in_specs=[_vmem()] * (4 * n), out_specs=(_vmem(),) * (3 * n),
        compiler_params=pltpu.CompilerParams(vmem_limit_bytes=VMEM_LIMIT),
    )(*[a for it in items for a in it])
    return [tuple(outs[3 * i:3 * i + 3]) for i in range(n)]


def _sum_adamw(buf, w, m, v, name):
    _, rows, cols = buf.shape
    tr = _div_tile(rows, 256, 2 * SUBLANES)

    def body(b_ref, w_ref, m_ref, v_ref, g_ref, d_ref, nm_ref, nv_ref):
        g = b_ref[0].astype(F32)
        for k in range(1, N_DEV):
            g = g + b_ref[k].astype(F32)
        g_ref[...] = g
        d_ref[...], nm_ref[...], nv_ref[...] = _adam_update(w_ref[...], g, m_ref[...], v_ref[...])

    spec = pl.BlockSpec((tr, cols), lambda i: (i, 0))
    return pl.pallas_call(
        functools.partial(body), name=name, grid=(rows // tr,),
        out_shape=(jax.ShapeDtypeStruct((rows, cols), F32),) * 4,
        in_specs=[pl.BlockSpec((N_DEV, tr, cols), lambda i: (0, i, 0))] + [spec] * 3, out_specs=(spec,) * 4,
        compiler_params=_params("parallel"),
    )(buf, w, m, v)


def _rope_tables(n_ctx, n):
    n_freq = RET_DIM // 4
    inv = np.float32(ROPE_BASE) ** (-np.arange(n_freq, dtype=np.float32) / np.float32(n_freq))
    tok = np.arange(n)
    pos_r = (tok // GRID_W).astype(np.float32)
    pos_c = (tok % GRID_W).astype(np.float32)
    ang_r = (pos_r[:, None] * inv[None, :]).astype(np.float32)
    ang_c = (pos_c[:, None] * inv[None, :]).astype(np.float32)
    cos = np.concatenate([np.cos(ang_r), np.cos(ang_r), np.cos(ang_c), np.cos(ang_c)], axis=-1)
    sin = np.concatenate([-np.sin(ang_r), np.sin(ang_r), -np.sin(ang_c), np.sin(ang_c)], axis=-1)
    cos = np.concatenate([np.ones((n_ctx, RET_DIM), np.float32), cos], axis=0)
    sin = np.concatenate([np.zeros((n_ctx, RET_DIM), np.float32), sin], axis=0)
    return jnp.asarray(cos, F32), jnp.asarray(sin, F32)


def _na_tables():
    q = np.arange(GRID_W)[:, None]
    k = np.arange(GRID_W)[None, :]
    start = np.clip(q - NA_KW // 2, 0, GRID_W - NA_KW)
    valid = (k >= start) & (k < start + NA_KW)
    dc = np.clip(k - q + (NA_KW - 1), 0, 2 * NA_KW - 2)
    ncls = 2 * NA_KW - 1
    onehot = (dc[None] == np.arange(ncls)[:, None, None]) & valid[None]
    oh2 = np.zeros((GRID_W, LANES, LANES), np.float32)
    for c in range(ncls):
        oh2[:, :GRID_W, c] = onehot[c]
        oh2[:, GRID_W:, 32 + c] = onehot[c]
    return onehot.astype(np.float32), valid, oh2.reshape(GRID_W * LANES, LANES)


def _paired_bias(rpb, onehot, valid):
    ncls = onehot.shape[0]
    pair = np.zeros((2 * ncls, GRID_W, LANES), np.float32)
    pair[:ncls, :, :GRID_W] = onehot
    pair[ncls:, :, GRID_W:] = onehot
    rows = jnp.concatenate([rpb[:, :-1], rpb[:, 1:]], axis=-1)
    t = jnp.einsum("hdc,cqk->hdqk", rows, jnp.asarray(pair), precision=lax.Precision.HIGHEST)
    return jnp.where(jnp.asarray(np.tile(valid, (1, 2)))[None, None], t, NEG_INF)


def kernel(x, c, ctx, c_ctx, w_ada, b_ada, g_pre_mix, g_post_mix, g_pre_mlp, g_post_mlp, w_in, ret_decay, ret_gn, na_rpb, w_out, w_mlp1, w_mlp2, loss_target, m_c_ctx, m_w_ada, m_b_ada, m_g_pre_mix, m_g_post_mix, m_g_pre_mlp, m_g_post_mlp, m_w_in, m_ret_decay, m_ret_gn, m_na_rpb, m_w_out, m_w_mlp1, m_w_mlp2, v_c_ctx, v_w_ada, v_b_ada, v_g_pre_mix, v_g_post_mix, v_g_pre_mlp, v_g_post_mlp, v_w_in, v_ret_decay, v_ret_gn, v_na_rpb, v_w_out, v_w_mlp1, v_w_mlp2):
    B, N, D = x.shape
    C = ctx.shape[1]
    T = C + N

    silu_all, mods_g, win_b, wout_l, w1_l, w2_l = _mod_gather(c, c_ctx, w_ada[0], b_ada, w_in[0].T, w_out[0],
                                                             w_mlp1[0], w_mlp2[0])
    mods_mine = mods_g.transpose(1, 0, 2).reshape(mods_g.shape[1], N_MOD * D)
    modl = jnp.concatenate([mods_mine[:B], mods_mine[SUBLANES:SUBLANES + 1]], axis=0)
    modl = modl.reshape(B + 1, N_MOD, 1, D)
    rin = w_in.shape[2]
    rout, c1, r2 = wout_l.shape[0], w1_l.shape[1], w2_l.shape[0]

    def rows_of(n):
        return lambda ref: _row_block(ref, n)

    def cols_of(n):
        return lambda ref: _col_block(ref, n)

    cos, sin = _rope_tables(C, N)
    onehot, valid, oh2 = _na_tables()
    bias2 = _paired_bias(na_rpb[0], onehot, valid)
    lg = jax.nn.log_sigmoid(ret_decay[0].astype(F32))

    ag = _SplitScatter([wout_l, w1_l, w2_l], [rows_of(rout), cols_of(c1), rows_of(r2)],
                       [(N_DEV * rout, D), (D, N_DEV * c1), (N_DEV * r2, D)], "ag_mlp_start",
                       kind="gather", masks=SIBLING + ICI_SAME_CORE)
    h_all, proj = _inproj_fwd(x, ctx, modl, g_pre_mix, win_b, after=ag.token)
    o_ret, lat_ret, q_rot, k_rot = _ret_fwd(proj, cos, sin, lg, ret_gn, C)
    lat_na, na_probs = _na_fwd(proj, bias2, C)
    wout_part, w1_part, w2_part = _scatter_wait([ag], lat_na, "ag_mlp_wait")

    (dy1, dlat_ret, dlat_na, dmix, h2, act, du, dz, red_d) = _dense_core(
        lat_ret, lat_na, x, loss_target, modl, g_post_mix, g_pre_mlp, g_post_mlp, wout_part, w1_part, w2_part)

    gw_out_p = _tn_matmul(lat_ret[:, None], dmix, "gw_out_ret", rows_after=lat_na.shape[-1])
    gw_out_p = _tn_matmul(lat_na[:, None], dmix, "gw_out_na", rows_before=lat_ret.shape[-1], into=gw_out_p)
    gw1_p = _tn_matmul(h2[:, None], du, "gw_mlp1")
    gw2_p = _tn_matmul(act[:, None], dz, "gw_mlp2")
    rs_mlp = _SplitScatter([gw_out_p, gw1_p, gw2_p], [rows_of(rout), cols_of(c1), rows_of(r2)],
                           [(rout, D), (D, c1), (r2, D)], "rs_mlp_start")

    dret, dgn_p, dlg_p = _ret_bwd(proj, q_rot, k_rot, cos, sin, lg, ret_gn, o_ret, dlat_ret, C, after=rs_mlp.token)
    dna, dbias2 = _na_bwd(proj, na_probs, dlat_na, C)
    ret_cols, na_cols = dret.shape[1] * dret.shape[3], dna.shape[1] * dna.shape[3]
    gwin_t_p = _tn_matmul(dret, h_all, "gw_in_ret", rows_after=na_cols)
    gwin_t_p = _tn_matmul(dna, h_all, "gw_in_na", rows_before=ret_cols, into=gwin_t_p)
    rs_in = _SplitScatter([gwin_t_p], [rows_of(rin)], [(rin, D)], "rs_w_in_start")
    grad_x, red_i = _inproj_bwd(dret, dna, x, ctx, dy1, modl, g_pre_mix, win_b, after=rs_in.token)

    rd = red_d
    nct = red_i.shape[1] * C // T
    ri_ctx = red_i[:, :nct].sum(axis=(0, 1))
    ri_lat = red_i[:, nct:].sum(axis=1)
    d_mods = jnp.concatenate([ri_lat[:, 0], ri_lat[:, 1], rd[:, 0], rd[:, 4], rd[:, 3], rd[:, 2]], axis=-1)
    d_cmods = jnp.concatenate([ri_ctx[0], ri_ctx[1], jnp.zeros(((N_MOD - 2) * D,), F32)])[None]
    dg_pre_mix = ri_lat[:, 2].sum(axis=0) + ri_ctx[2]
    dg_post_mix = rd[:, 1].sum(axis=0)
    dg_pre_mlp = rd[:, 5].sum(axis=0)
    dg_post_mlp = rd[:, 6].sum(axis=0)
    loss_p = rd[:, 7, 0].sum()
    d_gn = dgn_p[:, 0].sum(axis=0)
    d_lg = dlg_p[:, :, :2, 0].sum(axis=0).T
    d_decay = d_lg * jax.nn.sigmoid(-ret_decay[0].astype(F32))
    rr = _rpb_reduce(dbias2, jnp.asarray(oh2, BF16)).reshape(NA_HEADS, 2 * NA_KH - 2, LANES)
    ncls = 2 * NA_KW - 1
    d_rpb = (jnp.pad(rr[:, :, :ncls], ((0, 0), (0, 1), (0, 0))) + jnp.pad(rr[:, :, 32:32 + ncls], ((0, 0), (1, 0), (0, 0))))
    d_rpb32 = jnp.pad(d_rpb, ((0, 0), (0, 0), (0, 32 - ncls)))
    pieces = [dg_pre_mix, dg_post_mix, dg_pre_mlp, dg_post_mlp, d_gn, d_rpb32.reshape(-1),
              jnp.pad(d_decay.reshape(-1), (0, LANES - d_decay.size)), jnp.full((LANES,), loss_p, F32)]
    vec = jnp.concatenate(pieces)
    nm = N_MOD * D
    n_vec_rows = -(-vec.shape[0] // nm)
    assert B + 1 + n_vec_rows <= SUBLANES
    vec = jnp.pad(vec, (0, n_vec_rows * nm - vec.shape[0])).reshape(n_vec_rows, nm)
    dm_slot = jnp.concatenate([d_mods, d_cmods, vec, jnp.zeros((SUBLANES - B - 1 - n_vec_rows, nm), F32)], axis=0)
    def whole(ref):
        return lambda p: ref

    small = _SplitScatter([dm_slot], [whole], [dm_slot.shape], "small_start")
    land_out, land_1, land_2, land_in = _scatter_wait([rs_mlp, rs_in], small.token, "rs_wait")
    fused = {"w_in": [a.T for a in _sum_adamw(land_in, w_in[0].T, m_w_in[0].T, v_w_in[0].T, "sum_adamw_w_in")],
             "w_out": _sum_adamw(land_out, w_out[0], m_w_out[0], v_w_out[0], "sum_adamw_w_out"),
             "w_mlp1": _sum_adamw(land_1, w_mlp1[0], m_w_mlp1[0], v_w_mlp1[0], "sum_adamw_w_mlp1"),
             "w_mlp2": _sum_adamw(land_2, w_mlp2[0], m_w_mlp2[0], v_w_mlp2[0], "sum_adamw_w_mlp2")}
    (mbuf,) = _scatter_wait([small], fused["w_mlp2"][0], "small_wait")
    tot, g_b_ada, g_w_ada, g_c_ctx = _small_ar(mbuf, silu_all, w_ada[0], c_ctx, B + 1, n_vec_rows)
    flat = tot.reshape(-1)
    o0 = 0
    g_pre_mix_g = flat[o0:o0 + D]; o0 += D
    g_post_mix_g = flat[o0:o0 + D]; o0 += D
    g_pre_mlp_g = flat[o0:o0 + D]; o0 += D
    g_post_mlp_g = flat[o0:o0 + D]; o0 += D
    g_gn = flat[o0:o0 + RET_WIDTH]; o0 += RET_WIDTH
    nrpb = NA_HEADS * (2 * NA_KH - 1) * 32
    g_rpb = flat[o0:o0 + nrpb].reshape(NA_HEADS, 2 * NA_KH - 1, 32)[:, :, :ncls]; o0 += nrpb
    g_decay = flat[o0:o0 + 2 * RET_HEADS].reshape(2, RET_HEADS); o0 += LANES
    loss = flat[o0]

    grads = {
        "c_ctx": g_c_ctx.reshape(c_ctx.shape), "w_ada": g_w_ada[None], "b_ada": g_b_ada.reshape(b_ada.shape),
        "g_pre_mix": g_pre_mix_g[None], "g_post_mix": g_post_mix_g[None], "g_pre_mlp": g_pre_mlp_g[None],
        "g_post_mlp": g_post_mlp_g[None], "w_in": fused["w_in"][0][None], "ret_decay": g_decay[None], "ret_gn": g_gn[None],
        "na_rpb": g_rpb[None], "w_out": fused["w_out"][0][None], "w_mlp1": fused["w_mlp1"][0][None],
        "w_mlp2": fused["w_mlp2"][0][None],
    }
    weights = dict(c_ctx=c_ctx, w_ada=w_ada, b_ada=b_ada, g_pre_mix=g_pre_mix, g_post_mix=g_post_mix,
                   g_pre_mlp=g_pre_mlp, g_post_mlp=g_post_mlp, w_in=w_in, ret_decay=ret_decay, ret_gn=ret_gn,
                   na_rpb=na_rpb, w_out=w_out, w_mlp1=w_mlp1, w_mlp2=w_mlp2)
    m_in = dict(c_ctx=m_c_ctx, w_ada=m_w_ada, b_ada=m_b_ada, g_pre_mix=m_g_pre_mix, g_post_mix=m_g_post_mix,
                g_pre_mlp=m_g_pre_mlp, g_post_mlp=m_g_post_mlp, w_in=m_w_in, ret_decay=m_ret_decay,
                ret_gn=m_ret_gn, na_rpb=m_na_rpb, w_out=m_w_out, w_mlp1=m_w_mlp1, w_mlp2=m_w_mlp2)
    v_in = dict(c_ctx=v_c_ctx, w_ada=v_w_ada, b_ada=v_b_ada, g_pre_mix=v_g_pre_mix, g_post_mix=v_g_post_mix,
                g_pre_mlp=v_g_pre_mlp, g_post_mlp=v_g_post_mlp, w_in=v_w_in, ret_decay=v_ret_decay,
                ret_gn=v_ret_gn, na_rpb=v_na_rpb, w_out=v_w_out, w_mlp1=v_w_mlp1, w_mlp2=v_w_mlp2)
    names = list(weights)
    deltas, new_m, new_v = {}, {}, {}
    def as_2d(n):
        shp = weights[n].shape
        two_d = (-1, shp[-1]) if len(shp) > 1 else (1, shp[0])
        return [a.reshape(two_d) for a in (weights[n], grads[n], m_in[n], v_in[n])]

    small = [n for n in names if n not in fused and weights[n].size <= 65536]
    updated = dict(zip(small, _adamw_small([as_2d(n) for n in small], "adamw_small")))
    for n in names:
        if n in fused:
            updated[n] = fused[n][1:]
        elif n not in updated:
            updated[n] = _adamw(*as_2d(n), "adamw_" + n)
        deltas[n], new_m[n], new_v[n] = (a.reshape(weights[n].shape) for a in updated[n])
    return (loss, grad_x, *[grads[n] for n in names], *[deltas[n] for n in names],
            *[new_m[n] for n in names], *[new_v[n] for n in names])
```

```python
import functools

import numpy as np
import jax
import jax.numpy as jnp
from jax import lax
from jax.experimental import pallas as pl
from jax.experimental.pallas import tpu as pltpu

F32 = jnp.float32
BF16 = jnp.bfloat16
MESH = pl.DeviceIdType.MESH

N_DEV = 8
LANES = 128
SUBLANES = 8
VMEM_LIMIT = 60 * 1024 * 1024

GRID_W = 64
RET_HEADS = 4
RET_DIM = 128
RET_WIDTH = RET_HEADS * RET_DIM
NA_HEADS = 8
NA_DIM = 64
NA_WIDTH = NA_HEADS * NA_DIM
NA_PAIRS = NA_HEADS // 2
NA_KH = 8
NA_KW = 16
NA_GROUP = 8
SEG = 512
ROPE_BASE = 10000.0
NORM_EPS = 1e-6
NEG_INF = -1e30
N_MOD = 6

ADAM_LR = 0.001
ADAM_B1 = 0.9
ADAM_B2 = 0.999
ADAM_EPS = 1e-08
ADAM_WD = 0.01
ADAM_STEP = 10


def _dot(a, b):
    return lax.dot_general(a, b, (((1,), (0,)), ((), ())), preferred_element_type=F32)


def _dot_nt(a, b):
    return lax.dot_general(a, b, (((1,), (1,)), ((), ())), preferred_element_type=F32)


def _dot_tn(a, b):
    return lax.dot_general(a, b, (((0,), (0,)), ((), ())), preferred_element_type=F32)


def _sigmoid(x):
    return 1.0 / (1.0 + jnp.exp(-x))


def _div_tile(n, cap, mult):
    if n <= cap:
        return n
    for t in range(cap - cap % mult, 0, -mult):
        if n % t == 0:
            return t
    raise ValueError(f"no tile for {n}")


def _params(*sem):
    return pltpu.CompilerParams(dimension_semantics=tuple(sem) if sem else None,
                                vmem_limit_bytes=VMEM_LIMIT)


def _vmem():
    return pl.BlockSpec(memory_space=pltpu.VMEM)


def _any():
    return pl.BlockSpec(memory_space=pl.ANY)


def _me_and_peers():
    x, y, c = lax.axis_index("x"), lax.axis_index("y"), lax.axis_index("c")
    me = 4 * x + 2 * y + c
    peers = []
    for m in range(1, N_DEV):
        px = 1 - x if (m >> 2) & 1 else x
        py = 1 - y if (m >> 1) & 1 else y
        pc = 1 - c if m & 1 else c
        peers.append(((px, py, pc), 4 * px + 2 * py + pc))
    return me, peers


def _exchange(src_for, dst_from, send_sems, recv_sems):
    me, peers = _me_and_peers()
    sent = []
    for i, (dev, pid) in enumerate(peers):
        cp = pltpu.make_async_remote_copy(src_ref=src_for(pid), dst_ref=dst_from(me),
                                          send_sem=send_sems.at[i], recv_sem=recv_sems.at[i],
                                          device_id=dev, device_id_type=MESH)
        cp.start()
        sent.append(cp)
    for i, (dev, pid) in enumerate(peers):
        pltpu.make_async_remote_copy(src_ref=src_for(pid), dst_ref=dst_from(pid),
                                     send_sem=send_sems.at[i], recv_sem=recv_sems.at[i],
                                     device_id=dev, device_id_type=MESH).wait_recv()
    for cp in sent:
        cp.wait_send()


SIBLING = (1,)
ICI_SAME_CORE = (2, 4, 6)
ALL_PEERS = tuple(range(1, N_DEV))


def _remote(src, dst, send_sem, recv_sem, dev):
    return pltpu.make_async_remote_copy(src_ref=src, dst_ref=dst, send_sem=send_sem, recv_sem=recv_sem,
                                        device_id=dev, device_id_type=MESH)


def _push_start(items, masks, send_sems, recv_sems):
    me, peers = _me_and_peers()
    for k, (src_for, dst_from) in enumerate(items):
        for m in masks:
            dev, pid = peers[m - 1]
            _remote(src_for(pid), dst_from(me), send_sems.at[k, m - 1], recv_sems.at[k, m - 1], dev).start()


def _push_wait_recv(items, masks, send_sems, recv_sems):
    me, peers = _me_and_peers()
    for k, (src_for, dst_from) in enumerate(items):
        for m in masks:
            dev, pid = peers[m - 1]
            _remote(src_for(pid), dst_from(pid), send_sems.at[k, m - 1], recv_sems.at[k, m - 1], dev).wait_recv()


def _push_wait_send(items, masks, send_sems, recv_sems):
    me, peers = _me_and_peers()
    for k, (src_for, dst_from) in enumerate(items):
        for m in masks:
            dev, pid = peers[m - 1]
            _remote(src_for(pid), dst_from(me), send_sems.at[k, m - 1], recv_sems.at[k, m - 1], dev).wait_send()


def _forward_start(items, send_sems, recv_sems):
    me, peers = _me_and_peers()
    sib = peers[0][0]
    for k, (blk_in, blk_out) in enumerate(items):
        for j, m in enumerate(ICI_SAME_CORE):
            pid = peers[m - 1][1]
            _remote(blk_in(pid), blk_out(pid), send_sems.at[k, j], recv_sems.at[k, j], sib).start()


def _forward_wait(items, send_sems, recv_sems):
    me, peers = _me_and_peers()
    sib = peers[0][0]
    for k, (blk_in, blk_out) in enumerate(items):
        for j, m in enumerate(ICI_SAME_CORE):
            got = peers[(m | 1) - 1][1]
            _remote(blk_in(got), blk_out(got), send_sems.at[k, j], recv_sems.at[k, j], sib).wait_recv()
    for k, (blk_in, blk_out) in enumerate(items):
        for j, m in enumerate(ICI_SAME_CORE):
            pid = peers[m - 1][1]
            _remote(blk_in(pid), blk_out(pid), send_sems.at[k, j], recv_sems.at[k, j], sib).wait_send()


def _mod_gather(c, c_ctx, w_ada, b_ada, w_in_t, w_out, w1, w2):
    B, D = c.shape
    ncol = w_ada.shape[1]
    rows = SUBLANES * N_DEV + SUBLANES

    def body(c_ref, cc_ref, w_ref, b_ref, win_ref, wout_ref, w1_ref, w2_ref,
             s_ref, m_ref, gin_ref, wout_b, w1_b, w2_b,
             win_b, msend, send1, recv1, send2, recv2, wsend, wrecv, fsend, frecv, lsem):
        me, _ = _me_and_peers()
        win_b[...] = win_ref[...].astype(BF16)
        block = _row_block(gin_ref, w_in_t.shape[0])
        gather = [(lambda p: win_b, block)]
        own = pltpu.make_async_copy(win_b, block(me), lsem.at[0])
        cv = c_ref[...]
        slot = jnp.concatenate([cv * _sigmoid(cv), jnp.zeros((SUBLANES - B, D), F32)], axis=0)
        my_rows = pl.ds(pl.multiple_of(me * SUBLANES, SUBLANES), SUBLANES)
        s_ref[my_rows, :] = slot
        ccv = cc_ref[...]
        s_ref[SUBLANES * N_DEV:, :] = jnp.concatenate(
            [ccv * _sigmoid(ccv), jnp.zeros((SUBLANES - 1, D), F32)], axis=0)

        def rows_of(p):
            return s_ref.at[pl.ds(pl.multiple_of(p * SUBLANES, SUBLANES), SUBLANES), :]

        _exchange(lambda p: rows_of(me), rows_of, send1, recv1)
        own.start()
        _push_start(gather, SIBLING + ICI_SAME_CORE, wsend, wrecv)
        wout_b[...] = wout_ref[...].astype(BF16)
        w1_b[...] = w1_ref[...].astype(BF16)
        w2_b[...] = w2_ref[...].astype(BF16)
        b_loc = b_ref[:, pl.ds(pl.multiple_of(me * ncol, ncol), ncol)]
        mods = _dot(s_ref[...], w_ref[...]) + b_loc
        for p in range(N_DEV):
            msend[p] = jnp.concatenate([mods[p * SUBLANES:(p + 1) * SUBLANES], mods[N_DEV * SUBLANES:]], axis=0)
        m_ref[me] = msend[me]
        columns = [(lambda p: msend.at[p], lambda p: m_ref.at[p])]
        _push_start(columns, ALL_PEERS, send2, recv2)
        _push_wait_recv(gather, ICI_SAME_CORE, wsend, wrecv)
        relay = [(block, block)]
        _forward_start(relay, fsend, frecv)
        _push_wait_recv(columns, ALL_PEERS, send2, recv2)
        _push_wait_recv(gather, SIBLING, wsend, wrecv)
        _forward_wait(relay, fsend, frecv)
        _push_wait_send(columns, ALL_PEERS, send2, recv2)
        _push_wait_send(gather, SIBLING + ICI_SAME_CORE, wsend, wrecv)
        own.wait()

    return pl.pallas_call(
        body, name="mod_gather",
        out_shape=(jax.ShapeDtypeStruct((rows, D), F32), jax.ShapeDtypeStruct((N_DEV, 2 * SUBLANES, ncol), F32),
                   jax.ShapeDtypeStruct((N_DEV * w_in_t.shape[0], D), BF16),
                   jax.ShapeDtypeStruct(w_out.shape, BF16), jax.ShapeDtypeStruct(w1.shape, BF16),
                   jax.ShapeDtypeStruct(w2.shape, BF16)),
        in_specs=[_vmem()] * 8, out_specs=(_vmem(), _vmem(), _any(), _vmem(), _vmem(), _vmem()),
        scratch_shapes=[pltpu.VMEM(w_in_t.shape, BF16), pltpu.VMEM((N_DEV, 2 * SUBLANES, ncol), F32)]
                       + [pltpu.SemaphoreType.DMA((N_DEV - 1,))] * 2
                       + [pltpu.SemaphoreType.DMA((1, N_DEV - 1))] * 4 + [pltpu.SemaphoreType.DMA((1, 3))] * 2
                       + [pltpu.SemaphoreType.DMA((1,))],
        compiler_params=pltpu.CompilerParams(vmem_limit_bytes=VMEM_LIMIT),
    )(c, c_ctx.reshape(1, D), w_ada, b_ada, w_in_t, w_out, w1, w2)


def _row_block(ref, rows):
    return lambda p: ref.at[pl.ds(pl.multiple_of(p * rows, 2 * SUBLANES), rows), :]


def _col_block(ref, cols):
    return lambda p: ref.at[:, pl.ds(pl.multiple_of(p * cols, LANES), cols)]


def _slot(ref):
    return lambda p: ref.at[p]


def _grid_call(body, *, name, grid, out_shape, in_specs, out_specs, scratch_shapes, args, after=None):
    n_in = len(args)

    def ordered_body(*refs):
        body(*refs[:n_in], *refs[n_in + 1:])

    return pl.pallas_call(
        body if after is None else ordered_body, name=name, grid=grid, out_shape=tuple(out_shape),
        in_specs=list(in_specs) + ([] if after is None else [_any()]), out_specs=tuple(out_specs),
        scratch_shapes=list(scratch_shapes), compiler_params=_params(*(("arbitrary",) * len(grid))),
    )(*args, *([] if after is None else [after]))


def _token_tiles(n_ctx, tm):
    nct = n_ctx // tm

    def ctx_spec(D):
        return pl.BlockSpec((None, tm, D), lambda b, t: (b, jnp.minimum(t, nct - 1), 0))

    def lat_spec(D):
        return pl.BlockSpec((None, tm, D), lambda b, t: (b, jnp.maximum(t - nct, 0), 0))

    return nct, ctx_spec, lat_spec


def _inproj_fwd(x, ctx, modl, g1, w_in_t, after):
    B, N, D = x.shape
    n_ctx = ctx.shape[1]
    T = n_ctx + N
    nw = w_in_t.shape[0]
    tm = _div_tile(n_ctx, 256, 16)
    nct, ctx_spec, lat_spec = _token_tiles(n_ctx, tm)

    def body(c_ref, x_ref, sh_ref, sc_ref, g_ref, w_ref, h_ref, p_ref):
        x = jnp.where(pl.program_id(1) < nct, c_ref[...], x_ref[...])
        r = lax.rsqrt(jnp.mean(x * x, axis=-1, keepdims=True) + NORM_EPS)
        h = ((x * r) * g_ref[...]) * (1.0 + sc_ref[...]) + sh_ref[...]
        hb = h.astype(BF16)
        h_ref[...] = hb
        p_ref[...] = _dot_nt(hb, w_ref[...]).astype(BF16)

    def mrow(b, t):
        return jnp.where(t < nct, B, b)

    return _grid_call(
        body, name="inproj_fwd", grid=(B, T // tm),
        out_shape=(jax.ShapeDtypeStruct((B, T, D), BF16), jax.ShapeDtypeStruct((B, T, nw), BF16)),
        in_specs=[ctx_spec(D), lat_spec(D),
                  pl.BlockSpec((None, None, 1, D), lambda b, t: (mrow(b, t), 0, 0, 0)),
                  pl.BlockSpec((None, None, 1, D), lambda b, t: (mrow(b, t), 1, 0, 0)),
                  pl.BlockSpec((1, D), lambda b, t: (0, 0)),
                  pl.BlockSpec((nw, D), lambda b, t: (0, 0))],
        out_specs=(pl.BlockSpec((None, tm, D), lambda b, t: (b, t, 0)),
                   pl.BlockSpec((None, tm, nw), lambda b, t: (b, t, 0))),
        scratch_shapes=[], args=(ctx, x, modl, modl, g1, w_in_t), after=after)


def _swap32(x):
    lane = lax.broadcasted_iota(jnp.int32, x.shape, 1)
    return jnp.where((lane % 64) < 32, pltpu.roll(x, 96, 1), pltpu.roll(x, 32, 1))


def _rope(x, cos, sin):
    return x * cos + _swap32(x) * sin


def _unrope(dy, cos, sin):
    return dy * cos + _swap32(dy * sin)


def _ret_weights(lgf, lgb, dist):
    return jnp.exp(jnp.where(dist >= 0.0, lgf * dist, -lgb * dist))


class _RetDecay:
    def __init__(self, lgf, lgb, rows):
        r = lax.broadcasted_iota(jnp.int32, (rows, RET_DIM), 0).astype(F32)
        self.head = r + 1.0
        self.tail = (rows - 1.0) - r
        self.q_f = jnp.exp(lgf * self.head)
        self.k_f = jnp.exp(lgf * self.tail)
        self.q_b = jnp.exp(lgb * self.tail)
        self.k_b = jnp.exp(lgb * self.head)


def _ret_states(kf32, vs, lgf, lgb, C, c, nt, hf, hb, hfa=None, hba=None):
    dec = _RetDecay(lgf, lgb, c)
    dec_c = _RetDecay(lgf, lgb, C)
    step_f = jnp.exp(jnp.zeros((RET_DIM, RET_DIM), F32) + lgf * c)
    step_b = jnp.exp(jnp.zeros((RET_DIM, RET_DIM), F32) + lgb * c)

    def upd(rows, kdec):
        return _dot_tn((kf32[rows, :] * kdec).astype(BF16), vs[rows, :])

    def lat(t):
        return slice(C + t * c, C + (t + 1) * c)

    state = upd(slice(0, C), dec_c.k_f)
    aged = jnp.zeros_like(state)
    for t in range(nt):
        hf[t] = state.astype(BF16)
        if hfa is not None:
            hfa[t] = aged
        if t < nt - 1:
            aged = step_f * (aged + c * state)
            state = step_f * state + upd(lat(t), dec.k_f)
    state = upd(slice(0, C), dec_c.k_b)
    aged = jnp.zeros_like(state)
    for t in range(nt - 1, -1, -1):
        hb[t] = state.astype(BF16)
        if hba is not None:
            hba[t] = aged
        if t > 0:
            aged = step_b * (aged + c * state)
            state = step_b * state + upd(lat(t), dec.k_b)
    return dec, dec_c, step_f, step_b


def _ret_fwd(proj, cos, sin, lg, gn, n_ctx):
    B, T, _ = proj.shape
    C = n_ctx
    N = T - C
    c = _div_tile(N, 256, 16)
    nt = N // c
    scale = RET_DIM ** -0.5

    def body(lg_ref, q_ref, k_ref, vs, g_ref, cos_ref, sin_ref, gn_ref, o_ref, lat_ref, qr_ref, kf32,
             qs, ks, hf, hb):
        h = pl.program_id(1)
        lgf = lg_ref[0, h]
        lgb = lg_ref[1, h]
        for rows in [slice(0, C)] + [slice(C + t * c, C + (t + 1) * c) for t in range(nt)]:
            cosb = cos_ref[rows, :]
            sinb = sin_ref[rows, :]
            qr = _rope(q_ref[rows, :].astype(F32), cosb, sinb) * scale
            qr_ref[rows, :] = qr
            qs[rows, :] = qr.astype(BF16)
            kr = _rope(k_ref[rows, :].astype(F32), cosb, sinb)
            kf32[rows, :] = kr
            ks[rows, :] = kr.astype(BF16)
        gnv = gn_ref[...]
        dec, _, _, _ = _ret_states(kf32, vs, lgf, lgb, C, c, nt, hf, hb)
        rc = (lax.broadcasted_iota(jnp.int32, (c, c), 0) - lax.broadcasted_iota(jnp.int32, (c, c), 1)).astype(F32)
        w_diag = _ret_weights(lgf, lgb, rc)
        for t in range(nt):
            rows = slice(C + t * c, C + (t + 1) * c)
            qt = qs[rows, :]
            s = _dot_nt(qt, ks[rows, :])
            o = (_dot((s * w_diag).astype(BF16), vs[rows, :])
                 + dec.q_f * _dot(qt, hf[t]) + dec.q_b * _dot(qt, hb[t]))
            o_ref[t * c:(t + 1) * c, :] = o
            mu = jnp.mean(o, axis=-1, keepdims=True)
            oc = o - mu
            var = jnp.mean(oc * oc, axis=-1, keepdims=True)
            yh = oc * lax.rsqrt(var + NORM_EPS)
            g = g_ref[rows, :].astype(F32)
            lat_ref[t * c:(t + 1) * c, :] = ((yh * gnv) * (g * _sigmoid(g))).astype(BF16)

    def col(seg):
        return pl.BlockSpec((None, T, RET_DIM), lambda b, h, seg=seg: (b, 0, seg * RET_HEADS + h))

    return _grid_call(
        body, name="ret_fwd", grid=(B, RET_HEADS),
        out_shape=(jax.ShapeDtypeStruct((B, N, RET_WIDTH), F32), jax.ShapeDtypeStruct((B, N, RET_WIDTH), BF16),
                   jax.ShapeDtypeStruct((B, T, RET_WIDTH), F32), jax.ShapeDtypeStruct((B, T, RET_WIDTH), F32)),
        in_specs=[pl.BlockSpec(memory_space=pltpu.SMEM), col(0), col(1), col(2), col(3),
                  pl.BlockSpec((T, RET_DIM), lambda b, h: (0, 0)), pl.BlockSpec((T, RET_DIM), lambda b, h: (0, 0)),
                  pl.BlockSpec((1, RET_DIM), lambda b, h: (0, h))],
        out_specs=(pl.BlockSpec((None, N, RET_DIM), lambda b, h: (b, 0, h)),
                   pl.BlockSpec((None, N, RET_DIM), lambda b, h: (b, 0, h)),
                   pl.BlockSpec((None, T, RET_DIM), lambda b, h: (b, 0, h)),
                   pl.BlockSpec((None, T, RET_DIM), lambda b, h: (b, 0, h))),
        scratch_shapes=[pltpu.VMEM((T, RET_DIM), BF16)] * 2 + [pltpu.VMEM((nt, RET_DIM, RET_DIM), BF16)] * 2,
        args=(lg, proj, proj, proj, proj, cos, sin, gn))


def _ret_bwd(proj, q_rot, k_rot, cos, sin, lg, gn, o, dlat, n_ctx, after):
    B, T, _ = proj.shape
    C = n_ctx
    N = T - C
    c = _div_tile(N, 256, 16)
    nt = N // c
    scale = RET_DIM ** -0.5

    def lat(t):
        return slice(C + t * c, C + (t + 1) * c)

    def body(lg_ref, qf32, kf32, vs, g_ref, cos_ref, sin_ref, gn_ref, o_ref, dl_ref,
             d_ref, dgn_ref, dlg_ref, qs, ks, dos, hf, hb, hfa, hba, gf_s, gb_s):
        h = pl.program_id(1)
        lgf = lg_ref[0, h]
        lgb = lg_ref[1, h]
        gnv = gn_ref[...]

        def fold(a):
            return jnp.sum(a.reshape(a.shape[0] // SUBLANES, SUBLANES, a.shape[1]), axis=0)

        for rows in [slice(0, C)] + [lat(t) for t in range(nt)]:
            qs[rows, :] = qf32[rows, :].astype(BF16)
            ks[rows, :] = kf32[rows, :].astype(BF16)

        dgn = jnp.zeros((1, RET_DIM), F32)
        for t in range(nt):
            lrows = slice(t * c, (t + 1) * c)
            ov = o_ref[lrows, :]
            mu = jnp.mean(ov, axis=-1, keepdims=True)
            oc = ov - mu
            var = jnp.mean(oc * oc, axis=-1, keepdims=True)
            rstd = lax.rsqrt(var + NORM_EPS)
            yh = oc * rstd
            g = g_ref[lat(t), :].astype(F32)
            sg = _sigmoid(g)
            dl = dl_ref[lrows, :]
            d_ref[3, lat(t), :] = (dl * (yh * gnv) * (sg * (1.0 + g * (1.0 - sg)))).astype(BF16)
            dls = dl * (g * sg)
            dgn = dgn + jnp.sum(dls * yh, axis=0, keepdims=True)
            dyh = dls * gnv
            do = rstd * (dyh - jnp.mean(dyh, axis=-1, keepdims=True)
                         - yh * jnp.mean(dyh * yh, axis=-1, keepdims=True))
            dos[lrows, :] = do.astype(BF16)
        dgn_ref[...] = jnp.concatenate([dgn, jnp.zeros((SUBLANES - 1, RET_DIM), F32)], axis=0)
        d_ref[3, 0:C, :] = jnp.zeros((C, RET_DIM), BF16)
        d_ref[0, 0:C, :] = jnp.zeros((C, RET_DIM), BF16)

        dec, dec_c, step_f, step_b = _ret_states(kf32, vs, lgf, lgb, C, c, nt, hf, hb, hfa, hba)

        def zmat(t, qdec):
            return _dot_tn((qf32[lat(t), :] * qdec).astype(BF16), dos[t * c:(t + 1) * c, :])

        acc3f = jnp.zeros((RET_DIM, RET_DIM), F32)
        acc3b = jnp.zeros((RET_DIM, RET_DIM), F32)
        state = jnp.zeros((RET_DIM, RET_DIM), F32)
        for t in range(nt - 1, -1, -1):
            gf_s[t] = state.astype(BF16)
            z = zmat(t, dec.q_f)
            acc3f = acc3f + hfa[t] * z
            state = step_f * state + z
        gctx_f = state.astype(BF16)
        state = jnp.zeros((RET_DIM, RET_DIM), F32)
        for t in range(nt):
            gb_s[t] = state.astype(BF16)
            z = zmat(t, dec.q_b)
            acc3b = acc3b + hba[t] * z
            state = step_b * state + z
        gctx_b = state.astype(BF16)

        rc = (lax.broadcasted_iota(jnp.int32, (c, c), 0) - lax.broadcasted_iota(jnp.int32, (c, c), 1)).astype(F32)
        w_diag = _ret_weights(lgf, lgb, rc)
        wg_f = jnp.where(rc >= 0.0, w_diag * rc, 0.0)
        wg_b = jnp.where(rc < 0.0, -w_diag * rc, 0.0)
        accf = jnp.zeros((SUBLANES, RET_DIM), F32)
        accb = jnp.zeros((SUBLANES, RET_DIM), F32)
        gdf = jnp.zeros((SUBLANES, c), F32)
        gdb = jnp.zeros((SUBLANES, c), F32)
        for t in range(nt):
            rows = lat(t)
            qt = qs[rows, :]
            kt = ks[rows, :]
            vt = vs[rows, :]
            dot = dos[t * c:(t + 1) * c, :]
            s = _dot_nt(qt, kt)
            dp = _dot_nt(dot, vt)
            dv = _dot_tn((s * w_diag).astype(BF16), dot)
            ds = (dp * w_diag).astype(BF16)
            dq = _dot(ds, kt)
            dk = _dot_tn(ds, qt)
            gs = dp * s
            gdf = gdf + fold(gs * wg_f)
            gdb = gdb + fold(gs * wg_b)
            qv = qf32[rows, :]
            kv = kf32[rows, :]
            dq_f = dec.q_f * _dot_nt(dot, hf[t])
            dq_b = dec.q_b * _dot_nt(dot, hb[t])
            dk_f = dec.k_f * _dot_nt(vt, gf_s[t])
            dk_b = dec.k_b * _dot_nt(vt, gb_s[t])
            accf = accf + fold(dec.head * dq_f * qv) + fold(dec.tail * dk_f * kv)
            accb = accb + fold(dec.tail * dq_b * qv) + fold(dec.head * dk_b * kv)
            dv = dv + dec.k_f * _dot(kt, gf_s[t]) + dec.k_b * _dot(kt, gb_s[t])
            cosb = cos_ref[rows, :]
            sinb = sin_ref[rows, :]
            d_ref[0, rows, :] = _unrope((dq + dq_f + dq_b) * scale, cosb, sinb).astype(BF16)
            d_ref[1, rows, :] = _unrope(dk + dk_f + dk_b, cosb, sinb).astype(BF16)
            d_ref[2, rows, :] = dv.astype(BF16)
        kc = ks[0:C, :]
        vc = vs[0:C, :]
        kcv = kf32[0:C, :]
        dkc_f = dec_c.k_f * _dot_nt(vc, gctx_f)
        dkc_b = dec_c.k_b * _dot_nt(vc, gctx_b)
        accf = accf + fold(dec_c.tail * dkc_f * kcv)
        accb = accb + fold(dec_c.head * dkc_b * kcv)
        d_ref[1, 0:C, :] = (dkc_f + dkc_b).astype(BF16)
        d_ref[2, 0:C, :] = (dec_c.k_f * _dot(kc, gctx_f) + dec_c.k_b * _dot(kc, gctx_b)).astype(BF16)
        gf = jnp.sum(gdf) + jnp.sum(accf) + jnp.sum(acc3f)
        gb = jnp.sum(gdb) + jnp.sum(accb) + jnp.sum(acc3b)
        row = lax.broadcasted_iota(jnp.int32, (SUBLANES, LANES), 0)
        dlg_ref[...] = jnp.where(row == 0, gf, jnp.where(row == 1, gb, 0.0))

    def col(seg):
        return pl.BlockSpec((None, T, RET_DIM), lambda b, h, seg=seg: (b, 0, seg * RET_HEADS + h))

    def head(rows):
        return pl.BlockSpec((None, rows, RET_DIM), lambda b, h: (b, 0, h))

    return _grid_call(
        body, name="ret_bwd", grid=(B, RET_HEADS),
        out_shape=(jax.ShapeDtypeStruct((B, 4, T, RET_WIDTH), BF16),
                   jax.ShapeDtypeStruct((B, SUBLANES, RET_WIDTH), F32),
                   jax.ShapeDtypeStruct((B, RET_HEADS, SUBLANES, LANES), F32)),
        in_specs=[pl.BlockSpec(memory_space=pltpu.SMEM), head(T), head(T), col(2), col(3),
                  pl.BlockSpec((T, RET_DIM), lambda b, h: (0, 0)), pl.BlockSpec((T, RET_DIM), lambda b, h: (0, 0)),
                  pl.BlockSpec((1, RET_DIM), lambda b, h: (0, h)), head(N), head(N)],
        out_specs=(pl.BlockSpec((None, 4, T, RET_DIM), lambda b, h: (b, 0, 0, h)),
                   pl.BlockSpec((None, SUBLANES, RET_DIM), lambda b, h: (b, 0, h)),
                   pl.BlockSpec((None, None, SUBLANES, LANES), lambda b, h: (b, h, 0, 0))),
        scratch_shapes=[pltpu.VMEM((T, RET_DIM), BF16)] * 2 + [pltpu.VMEM((N, RET_DIM), BF16)]
                       + [pltpu.VMEM((nt, RET_DIM, RET_DIM), BF16)] * 2 + [pltpu.VMEM((nt, RET_DIM, RET_DIM), F32)] * 2
                       + [pltpu.VMEM((nt, RET_DIM, RET_DIM), BF16)] * 2,
        args=(lg, q_rot, k_rot, proj, proj, cos, sin, gn, o, dlat), after=after)


def _na_geometry(rows):
    kh = min(NA_KH, rows)
    return kh, kh * GRID_W


def _pair_select():
    lane = lax.broadcasted_iota(jnp.int32, (2 * GRID_W, LANES), 1)
    row = lax.broadcasted_iota(jnp.int32, (2 * GRID_W, LANES), 0)
    return (lane >= NA_DIM) == (row >= GRID_W)


def _pair_bias(bias_ref, dr0, kh):
    return jnp.concatenate(
        [jnp.concatenate([bias_ref[e, pl.ds(dr0 + 2 * m, 1)].reshape(GRID_W, LANES) for m in range(kh // 2)], axis=1)
         for e in range(2)], axis=0)


def _na_softmax(s_loc, s_ctx):
    mx = jnp.maximum(jnp.max(s_loc, axis=-1, keepdims=True), jnp.max(s_ctx, axis=-1, keepdims=True))
    p_loc = jnp.exp(s_loc - mx)
    p_ctx = jnp.exp(s_ctx - mx)
    den = jnp.sum(p_loc, axis=-1, keepdims=True) + jnp.sum(p_ctx, axis=-1, keepdims=True)
    return p_loc, p_ctx, den


def _na_fwd(proj, bias2, n_ctx):
    assert proj.dtype == BF16
    B, T, _ = proj.shape
    C = n_ctx
    N = T - C
    R = N // GRID_W
    kh, nk = _na_geometry(R)
    scale = NA_DIM ** -0.5
    base = (4 * RET_WIDTH) // LANES

    def body(q_ref, kb16, vb16, bias_ref, out_ref, p_ref):
        kc = kb16[0:C, :]
        vc = vb16[0:C, :]
        lane = lax.broadcasted_iota(jnp.int32, (GRID_W, LANES), 1)
        sel2 = _pair_select()

        def group(gi, carry):
            pre = []
            for u in range(NA_GROUP):
                r = gi * NA_GROUP + u
                bs = jnp.clip(r - kh // 2, 0, R - kh)
                dr0 = bs - r + (NA_KH - 1)
                q = q_ref[pl.ds(pl.multiple_of(C + r * GRID_W, GRID_W), GRID_W), :].astype(F32) * scale
                q2 = jnp.where(sel2, jnp.concatenate([q, q], axis=0), 0.0).astype(BF16)
                band = pl.ds(pl.multiple_of(C + bs * GRID_W, GRID_W), nk)
                s_loc = _dot_nt(q2, kb16[band, :]) + _pair_bias(bias_ref, dr0, kh)
                s_ctx = _dot_nt(q2, kc)
                pre.append((r, band, s_loc, s_ctx))
            mid = [(r, band) + _na_softmax(s_loc, s_ctx) for r, band, s_loc, s_ctx in pre]
            for r, band, p_loc, p_ctx, den in mid:
                inv = 1.0 / den
                pb_loc = (p_loc * inv).astype(BF16)
                pb_ctx = (p_ctx * inv).astype(BF16)
                p_ref[r, :, 0:nk] = pb_loc
                p_ref[r, :, nk:] = pb_ctx
                o2 = _dot(pb_loc, vb16[band, :]) + _dot(pb_ctx, vc)
                out_ref[pl.ds(pl.multiple_of(r * GRID_W, GRID_W), GRID_W), :] = jnp.where(
                    lane < NA_DIM, o2[:GRID_W], o2[GRID_W:]).astype(BF16)
            return carry

        lax.fori_loop(0, R // NA_GROUP, group, 0)

    def col(seg):
        return pl.BlockSpec((None, T, LANES), lambda b, p, seg=seg: (b, 0, base + seg * NA_PAIRS + p))

    return _grid_call(
        body, name="na_fwd", grid=(B, NA_PAIRS),
        out_shape=(jax.ShapeDtypeStruct((B, N, NA_WIDTH), BF16),
                   jax.ShapeDtypeStruct((B, NA_PAIRS, R, 2 * GRID_W, nk + C), BF16)),
        in_specs=[col(0), col(1), col(2),
                  pl.BlockSpec((2, 2 * NA_KH - 2, GRID_W, LANES), lambda b, p: (p, 0, 0, 0))],
        out_specs=(pl.BlockSpec((None, N, LANES), lambda b, p: (b, 0, p)),
                   pl.BlockSpec((None, None, R, 2 * GRID_W, nk + C), lambda b, p: (b, p, 0, 0, 0))),
        scratch_shapes=[],
        args=(proj, proj, proj, bias2))


def _na_bwd(proj, probs, dlat, n_ctx):
    assert proj.dtype == BF16
    B, T, _ = proj.shape
    C = n_ctx
    N = T - C
    R = N // GRID_W
    kh, nk = _na_geometry(R)
    scale = NA_DIM ** -0.5
    base = (4 * RET_WIDTH) // LANES

    def class_sums(tile):
        acc = pltpu.roll(tile[0:SUBLANES], NA_KW - 1, 1)
        for v in range(1, GRID_W // SUBLANES):
            acc = acc + pltpu.roll(tile[v * SUBLANES:(v + 1) * SUBLANES], (NA_KW - 1 - v * SUBLANES) % LANES, 1)
        sub = lax.broadcasted_iota(jnp.int32, acc.shape, 0)
        for bit in (1, 2, 4):
            acc = jnp.where((sub & bit) != 0, pltpu.roll(acc, LANES - bit, 1), acc)
        return jnp.sum(acc, axis=0, keepdims=True)

    def body(q_ref, kb16, vb16, p_ref, dl_ref, d_ref, rr_ref, dkv, db_ref):
        b = pl.program_id(1)
        kc = kb16[0:C, :]
        vc = vb16[0:C, :]
        lane = lax.broadcasted_iota(jnp.int32, (GRID_W, LANES), 1)
        dkv[...] = jnp.zeros(dkv.shape, F32)
        d_ref[0, 0:C, :] = jnp.zeros((C, LANES), BF16)

        @pl.when(b == 0)
        def _():
            db_ref[...] = jnp.zeros(db_ref.shape, F32)

        sel2 = _pair_select()

        def group(gi, carry):
            pre = []
            for u in range(NA_GROUP):
                r = gi * NA_GROUP + u
                bs = jnp.clip(r - kh // 2, 0, R - kh)
                dr0 = bs - r + (NA_KH - 1)
                q = q_ref[pl.ds(pl.multiple_of(C + r * GRID_W, GRID_W), GRID_W), :].astype(F32) * scale
                do = dl_ref[pl.ds(pl.multiple_of(r * GRID_W, GRID_W), GRID_W), :]
                q2 = jnp.where(sel2, jnp.concatenate([q, q], axis=0), 0.0).astype(BF16)
                do2 = jnp.where(sel2, jnp.concatenate([do, do], axis=0), 0.0).astype(BF16)
                band = pl.ds(pl.multiple_of(C + bs * GRID_W, GRID_W), nk)
                dp_loc = _dot_nt(do2, vb16[band, :])
                dp_ctx = _dot_nt(do2, vc)
                pre.append((r, dr0, band, q2, do2, dp_loc, dp_ctx))
            mid = []
            for r, dr0, band, q2, do2, dp_loc, dp_ctx in pre:
                pb_loc = p_ref[r, :, 0:nk]
                pb_ctx = p_ref[r, :, nk:]
                p_loc = pb_loc.astype(F32)
                p_ctx = pb_ctx.astype(F32)
                delta = (jnp.sum(p_loc * dp_loc, axis=-1, keepdims=True)
                         + jnp.sum(p_ctx * dp_ctx, axis=-1, keepdims=True))
                ds_loc = p_loc * (dp_loc - delta)
                ds_ctx = p_ctx * (dp_ctx - delta)
                mid.append((r, dr0, band, q2, do2, pb_loc, pb_ctx, ds_loc, ds_ctx))
            for r, dr0, band, q2, do2, pb_loc, pb_ctx, ds_loc, ds_ctx in mid:
                dsb_loc = ds_loc.astype(BF16)
                dsb_ctx = ds_ctx.astype(BF16)
                dq2 = _dot(dsb_loc, kb16[band, :]) + _dot(dsb_ctx, kc)
                d_ref[0, pl.ds(pl.multiple_of(C + r * GRID_W, GRID_W), GRID_W), :] = (jnp.where(
                    lane < NA_DIM, dq2[:GRID_W], dq2[GRID_W:]) * scale).astype(BF16)
                dkv[0, band, :] += _dot_tn(dsb_loc, q2)
                dkv[1, band, :] += _dot_tn(pb_loc, do2)
                dkv[0, 0:C, :] += _dot_tn(dsb_ctx, q2)
                dkv[1, 0:C, :] += _dot_tn(pb_ctx, do2)
                for e in range(2):
                    for m in range(kh // 2):
                        db_ref[e, pl.ds(dr0 + 2 * m, 1)] += ds_loc[e * GRID_W:(e + 1) * GRID_W,
                                                                   m * LANES:(m + 1) * LANES].reshape(1, GRID_W, LANES)
            return carry

        lax.fori_loop(0, R // NA_GROUP, group, 0)
        d_ref[1] = dkv[0].astype(BF16)
        d_ref[2] = dkv[1].astype(BF16)

        @pl.when(b == B - 1)
        def _():
            for e in range(2):
                for d in range(2 * NA_KH - 2):
                    rr_ref[e, pl.ds(d, 1), :] = class_sums(db_ref[e, d])

    def col(seg):
        return pl.BlockSpec((None, T, LANES), lambda p, b, seg=seg: (b, 0, base + seg * NA_PAIRS + p))

    return _grid_call(
        body, name="na_bwd", grid=(NA_PAIRS, B),
        out_shape=(jax.ShapeDtypeStruct((B, 3, T, NA_WIDTH), BF16),
                   jax.ShapeDtypeStruct((NA_HEADS, 2 * NA_KH - 2, LANES), F32)),
        in_specs=[col(0), col(1), col(2),
                  pl.BlockSpec((None, None, R, 2 * GRID_W, nk + C), lambda p, b: (b, p, 0, 0, 0)),
                  pl.BlockSpec((None, N, LANES), lambda p, b: (b, 0, p))],
        out_specs=(pl.BlockSpec((None, 3, T, LANES), lambda p, b: (b, 0, 0, p)),
                   pl.BlockSpec((2, 2 * NA_KH - 2, LANES), lambda p, b: (p, 0, 0))),
        scratch_shapes=[pltpu.VMEM((2, T, LANES), F32), pltpu.VMEM((2, 2 * NA_KH - 2, GRID_W, LANES), F32)],
        args=(proj, proj, proj, probs, dlat))


def _dense_core(lat_ret, lat_na, x, tgt, modl, g_post_mix, g_pre_mlp, g_post_mlp, w_out, w1, w2):
    B, N, D = x.shape
    F = w1.shape[1]
    wout_rows, w1_cols, w2_rows = w_out.shape[0] // N_DEV, w1.shape[1] // N_DEV, w2.shape[0] // N_DEV
    mixw = w_out.shape[0]
    half = mixw // 2
    tm = _div_tile(N, 256, 16)
    nt = N // tm
    fc = _div_tile(F, 1024, LANES)

    def body(lr_ref, ln_ref, x_ref, t_ref, gt1_ref, sh2_ref, sc2_ref, gt2_ref, gpm_ref, gpre_ref, gpo_ref,
             wout_part, w1_part, w2_part,
             dy1_ref, dlr_ref, dln_ref, dmix_ref, h2_ref, a_ref, du_ref, dz_ref, red_ref, wout_hbm, w1_hbm, w2_hbm,
             wout_v, w1_v, w2_v, u_s, sems, fsend, frecv):
        @pl.when((pl.program_id(0) == 0) & (pl.program_id(1) == 0))
        def _():
            relay = [(_row_block(wout_part, wout_rows), _row_block(wout_hbm, wout_rows)),
                     (_col_block(w1_part, w1_cols), _col_block(w1_hbm, w1_cols)),
                     (_row_block(w2_part, w2_rows), _row_block(w2_hbm, w2_rows))]
            _forward_start(relay, fsend, frecv)
            _forward_wait(relay, fsend, frecv)
            cps = [pltpu.make_async_copy(wout_hbm, wout_v, sems.at[0]),
                   pltpu.make_async_copy(w1_hbm, w1_v, sems.at[1]),
                   pltpu.make_async_copy(w2_hbm, w2_v, sems.at[2])]
            for cp in cps:
                cp.start()
            for cp in cps:
                cp.wait()

        @pl.when(pl.program_id(1) == 0)
        def _():
            red_ref[...] = jnp.zeros(red_ref.shape, F32)

        gt1 = gt1_ref[...]
        sh2 = sh2_ref[...]
        sc2 = sc2_ref[...]
        gt2 = gt2_ref[...]
        gpm = gpm_ref[...]
        gpre = gpre_ref[...]
        gpo = gpo_ref[...]

        def rowmean(a):
            return jnp.mean(a, axis=-1, keepdims=True)

        def colsum(a):
            return jnp.sum(a, axis=0, keepdims=True)

        mix_gain = gt1 * gpm
        mlp_in_gain = gpre * (1.0 + sc2)
        mlp_out_gain = gt2 * gpo
        mix = _dot(lr_ref[...], wout_v[0:half, :]) + _dot(ln_ref[...], wout_v[half:, :])
        x = x_ref[...]
        rm = lax.rsqrt(rowmean(mix * mix) + NORM_EPS)
        mh = mix * rm
        y1 = x + mh * mix_gain
        r1 = lax.rsqrt(rowmean(y1 * y1) + NORM_EPS)
        xh = y1 * r1
        h2b = (xh * mlp_in_gain + sh2).astype(BF16)
        h2_ref[...] = h2b
        z = jnp.zeros((tm, D), F32)
        for c0 in range(0, F, fc):
            u = _dot(h2b, w1_v[:, c0:c0 + fc])
            u_s[:, c0:c0 + fc] = u
            ru = jnp.maximum(u, 0.0)
            ab = (ru * ru).astype(BF16)
            a_ref[:, c0:c0 + fc] = ab
            z = z + _dot(ab, w2_v[c0:c0 + fc, :])
        r2 = lax.rsqrt(rowmean(z * z) + NORM_EPS)
        zh = z * r2
        y2 = y1 + zh * mlp_out_gain
        err = y2 - t_ref[...]
        loss = 0.5 * jnp.sum(rowmean(err * err))
        dy2 = err * (1.0 / D)
        s_out = colsum(dy2 * zh)
        red_ref[2:3, :] += s_out * gpo
        red_ref[6:7, :] += s_out * gt2
        dzh = dy2 * mlp_out_gain
        dz = r2 * (dzh - zh * rowmean(dzh * zh))
        dzb = dz.astype(BF16)
        dz_ref[...] = dzb
        dh2 = jnp.zeros((tm, D), F32)
        for c0 in range(0, F, fc):
            da = _dot_nt(dzb, w2_v[c0:c0 + fc, :])
            dub = (da * (2.0 * jnp.maximum(u_s[:, c0:c0 + fc], 0.0))).astype(BF16)
            du_ref[:, c0:c0 + fc] = dub
            dh2 = dh2 + _dot_nt(dub, w1_v[:, c0:c0 + fc])
        s_in = colsum(dh2 * xh)
        red_ref[3:4, :] += s_in * gpre
        red_ref[4:5, :] += colsum(dh2)
        red_ref[5:6, :] += s_in * (1.0 + sc2)
        dxh = dh2 * mlp_in_gain
        dy1 = dy2 + r1 * (dxh - xh * rowmean(dxh * xh))
        dy1_ref[...] = dy1
        s_mix = colsum(dy1 * mh)
        red_ref[0:1, :] += s_mix * gpm
        red_ref[1:2, :] += s_mix * gt1
        dmh = dy1 * mix_gain
        dmix = (rm *(dmh - mh * rowmean(dmh * mh))).astype(BF16)
        dmix_ref[...] = dmix
        dlr_ref[...] = _dot_nt(dmix, wout_v[0:half, :])
        dln_ref[...] = _dot_nt(dmix, wout_v[half:, :])
        red_ref[7:8, :] += jnp.zeros((1, D), F32) + loss

    def tok(w):
        return pl.BlockSpec((None, tm, w), lambda b, t: (b, t, 0))

    def mod(k):
        return pl.BlockSpec((None, None, 1, D), lambda b, t, k=k: (b, k, 0, 0))

    def vec():
        return pl.BlockSpec((1, D), lambda b, t: (0, 0))

    return pl.pallas_call(
        body, name="dense_core", grid=(B, nt),
        out_shape=(jax.ShapeDtypeStruct((B, N, D), F32), jax.ShapeDtypeStruct((B, N, half), F32),
                   jax.ShapeDtypeStruct((B, N, half), F32), jax.ShapeDtypeStruct((B, N, D), BF16),
                   jax.ShapeDtypeStruct((B, N, D), BF16), jax.ShapeDtypeStruct((B, N, F), BF16),
                   jax.ShapeDtypeStruct((B, N, F), BF16), jax.ShapeDtypeStruct((B, N, D), BF16),
                   jax.ShapeDtypeStruct((B, SUBLANES, D), F32),
                   jax.ShapeDtypeStruct(w_out.shape, w_out.dtype), jax.ShapeDtypeStruct(w1.shape, w1.dtype),
                   jax.ShapeDtypeStruct(w2.shape, w2.dtype)),
        in_specs=[tok(half), tok(half), tok(D), tok(D), mod(2), mod(3), mod(4), mod(5), vec(), vec(), vec(),
                  _any(), _any(), _any()],
        out_specs=(tok(D), tok(half), tok(half), tok(D), tok(D), tok(F), tok(F), tok(D),
                   pl.BlockSpec((None, SUBLANES, D), lambda b, t: (b, 0, 0)), _any(), _any(), _any()),
        scratch_shapes=[pltpu.VMEM((mixw, D), BF16), pltpu.VMEM((D, F), BF16), pltpu.VMEM((F, D), BF16),
                        pltpu.VMEM((tm, F), F32), pltpu.SemaphoreType.DMA((3,)),
                        pltpu.SemaphoreType.DMA((3, 3)), pltpu.SemaphoreType.DMA((3, 3))],
        input_output_aliases={11: 9, 12: 10, 13: 11},
        compiler_params=_params("arbitrary", "arbitrary"),
    )(lat_ret, lat_na, x, tgt, modl, modl, modl, modl, g_post_mix, g_pre_mlp, g_post_mlp, w_out, w1, w2)[:9]


def _inproj_bwd(dret, dna, x, ctx, dy1, modl, g1, w_in_t, after):
    B, N, D = x.shape
    n_ctx = ctx.shape[1]
    T = n_ctx + N
    tm = _div_tile(n_ctx, 256, 16)
    nct, ctx_spec, lat_spec = _token_tiles(n_ctx, tm)
    nt = T // tm
    nseg_r = dret.shape[1]
    nseg_n = dna.shape[1]
    nw = w_in_t.shape[0]

    def body(*refs):
        seg_refs = refs[:nseg_r + nseg_n]
        c_ref, x_ref, dy1_ref, sc_ref, g_ref, w_ref, dx_ref, red_ref = refs[nseg_r + nseg_n:]
        t = pl.program_id(1)
        dh = jnp.zeros((tm, D), F32)
        for s, ref in enumerate(seg_refs):
            dh = dh + _dot(ref[...], w_ref[s * SEG:(s + 1) * SEG, :])
        x = jnp.where(t < nct, c_ref[...], x_ref[...])
        g = g_ref[...]
        r = lax.rsqrt(jnp.mean(x * x, axis=-1, keepdims=True) + NORM_EPS)
        xh = x * r
        gain = 1.0 + sc_ref[...]
        s_in = jnp.sum(dh * xh, axis=0, keepdims=True)
        red_ref[0:1, :] = jnp.sum(dh, axis=0, keepdims=True)
        red_ref[1:2, :] = s_in * g
        red_ref[2:3, :] = s_in * gain
        red_ref[3:, :] = jnp.zeros((SUBLANES - 3, D), F32)
        dxh = dh * (g * gain)
        dx = r * (dxh - xh * jnp.mean(dxh * xh, axis=-1, keepdims=True))
        dx_ref[...] = dx + jnp.where(t >= nct, dy1_ref[...], 0.0)

    def mrow(b, t):
        return jnp.where(t < nct, B, b)

    def seg(s):
        return pl.BlockSpec((None, None, tm, SEG), lambda b, t, s=s: (b, s, t, 0))

    return _grid_call(
        body, name="inproj_bwd", grid=(B, nt),
        out_shape=(jax.ShapeDtypeStruct((B, N, D), F32), jax.ShapeDtypeStruct((B, nt, SUBLANES, D), F32)),
        in_specs=[seg(s) for s in range(nseg_r)] + [seg(s) for s in range(nseg_n)]
                 + [ctx_spec(D), lat_spec(D), lat_spec(D),
                    pl.BlockSpec((None, None, 1, D), lambda b, t: (mrow(b, t), 1, 0, 0)),
                    pl.BlockSpec((1, D), lambda b, t: (0, 0)),
                    pl.BlockSpec((nw, D), lambda b, t: (0, 0))],
        out_specs=(lat_spec(D), pl.BlockSpec((None, None, SUBLANES, D), lambda b, t: (b, t, 0, 0))),
        scratch_shapes=[], args=(*([dret] * nseg_r), *([dna] * nseg_n), ctx, x, dy1, modl, g1, w_in_t), after=after)


def _tn_matmul(lhs, rhs, name, rows_before=0, rows_after=0, into=None):
    B, S, T, W = lhs.shape
    nn = rhs.shape[-1]
    tk = _div_tile(T, 2304, LANES)
    bm = _div_tile(W, 1024, LANES)
    bn = _div_tile(nn, 1024, LANES)
    nkt = T // tk
    nk = B * nkt

    def body(l_ref, r_ref, *rest):
        o_ref, acc = rest[-2:]
        k = pl.program_id(3)

        @pl.when(k == 0)
        def _():
            acc[...] = jnp.zeros(acc.shape, F32)

        acc[...] += _dot_tn(l_ref[...].astype(BF16), r_ref[...].astype(BF16))

        @pl.when(k == nk - 1)
        def _():
            o_ref[...] = acc[...].astype(BF16)

    nwb = W // bm
    first = rows_before // bm
    return pl.pallas_call(
        functools.partial(body), name=name, grid=(S, nwb, nn // bn, nk),
        out_shape=jax.ShapeDtypeStruct((rows_before + S * W + rows_after, nn), BF16),
        in_specs=[pl.BlockSpec((None, None, tk, bm), lambda s, i, j, k: (k // nkt, s, k % nkt, i)),
                  pl.BlockSpec((None, tk, bn), lambda s, i, j, k: (k // nkt, k % nkt, j))]
                 + ([] if into is None else [_any()]),
        out_specs=pl.BlockSpec((bm, bn), lambda s, i, j, k: (first + s * nwb + i, j)),
        scratch_shapes=[pltpu.VMEM((bm, bn), F32)],
        input_output_aliases={} if into is None else {2: 0},
        compiler_params=_params("parallel", "parallel", "parallel", "arbitrary"),
    )(lhs, rhs, *([] if into is None else [into]))


class _SplitScatter:
    def __init__(self, gs, block_ofs, land_shapes, name, kind="scatter", masks=ALL_PEERS):
        self.n = n = len(gs)
        self.block_ofs, self.kind, self.masks = block_ofs, kind, masks
        if kind == "scatter":
            land_shapes = [(N_DEV,) + tuple(bs) for bs in land_shapes]
        hbm = pl.BlockSpec(memory_space=pltpu.HBM)
        sem = pl.BlockSpec(memory_space=pltpu.SEMAPHORE)

        def body(*refs):
            g_refs, land_refs = refs[:n], refs[n:2 * n]
            send_sems, recv_sems, own_sems = refs[2 * n:2 * n + 3]
            token = refs[-1]
            for own, pushes in self._copies(g_refs, land_refs, send_sems, recv_sems, own_sems, landing="sender"):
                own.start()
                for cp in pushes:
                    cp.start()
            token[...] = jnp.zeros_like(token)

        outs = pl.pallas_call(
            body, name=name,
            out_shape=(pltpu.SemaphoreType.DMA((n * (N_DEV - 1),)), pltpu.SemaphoreType.DMA((n * (N_DEV - 1),)),
                       pltpu.SemaphoreType.DMA((n,)))
                      + tuple(pltpu.HBM(g.shape, g.dtype) for g in gs)
                      + tuple(pltpu.HBM(s, g.dtype) for s, g in zip(land_shapes, gs))
                      + (jax.ShapeDtypeStruct((SUBLANES, LANES), F32),),
            in_specs=(hbm,) * (2 * n), out_specs=(sem,) * 3 + (hbm,) * (2 * n) + (_vmem(),),
            input_output_aliases={k: 3 + k for k in range(2 * n)},
            compiler_params=pltpu.CompilerParams(has_side_effects=pltpu.SideEffectType.DATAFLOW_SIDE_EFFECTING),
        )(*[pltpu.with_memory_space_constraint(g, pltpu.HBM) for g in gs],
          *[pltpu.with_memory_space_constraint(lax.empty(s, g.dtype), pltpu.HBM) for s, g in zip(land_shapes, gs)])
        self.sems, self.thru, self.token = outs[:3], outs[3:3 + 2 * n], outs[-1]

    def _copies(self, g_refs, land_refs, send_sems, recv_sems, own_sems, landing):
        me, peers = _me_and_peers()
        out = []
        for k in range(self.n):
            if self.kind == "scatter":
                src, dst = self.block_ofs[k](g_refs[k]), _slot(land_refs[k])
            else:
                src, dst = (lambda p, k=k: g_refs[k]), self.block_ofs[k](land_refs[k])
            own = pltpu.make_async_copy(src(me), dst(me), own_sems.at[k]) if landing == "sender" else None
            pushes = []
            for m in self.masks:
                dev, pid = peers[m - 1]
                i = k * (N_DEV - 1) + m - 1
                pushes.append(_remote(src(pid), dst(me if landing == "sender" else pid),
                                      send_sems.at[i], recv_sems.at[i], dev))
            out.append((own, pushes))
        return out


def _scatter_wait(scatters, after, name):
    hbm = pl.BlockSpec(memory_space=pltpu.HBM)
    sem = pl.BlockSpec(memory_space=pltpu.SEMAPHORE)
    n_arr = [2 * sc.n for sc in scatters]
    total = sum(n_arr)

    def body(*refs):
        arrs, sems = refs[:total], refs[total:total + 3 * len(scatters)]
        a0 = 0
        for j, sc in enumerate(scatters):
            g_refs, land_refs = arrs[a0:a0 + sc.n], arrs[a0 + sc.n:a0 + 2 * sc.n]
            a0 += 2 * sc.n
            send_sems, recv_sems, own_sems = sems[3 * j:3 * j + 3]
            for (own, sent), (_, got) in zip(sc._copies(g_refs, land_refs, send_sems, recv_sems, own_sems, "sender"),
                                             sc._copies(g_refs, land_refs, send_sems, recv_sems, own_sems, "receiver")):
                own.wait()
                for cp in sent:
                    cp.wait_send()
                for cp in got:
                    cp.wait_recv()

    operands = [a for sc in scatters for a in sc.thru]
    outs = pl.pallas_call(
        body, name=name,
        out_shape=tuple(pltpu.HBM(a.shape, a.dtype) for a in operands),
        in_specs=(hbm,) * total + (sem,) * (3 * len(scatters)) + (pl.BlockSpec(memory_space=pl.ANY),),
        out_specs=(hbm,) * total, input_output_aliases={k: k for k in range(total)},
        compiler_params=pltpu.CompilerParams(has_side_effects=pltpu.SideEffectType.DATAFLOW_SIDE_EFFECTING),
    )(*operands, *[s for sc in scatters for s in sc.sems], after)
    lands, a0 = [], 0
    for sc in scatters:
        lands.extend(outs[a0 + sc.n:a0 + 2 * sc.n])
        a0 += 2 * sc.n
    return lands


def _small_ar(mbuf, silu_all, w_ada, c_ctx, n_mod_rows, n_vec_rows):
    D = silu_all.shape[1]
    ncol = w_ada.shape[1]
    nm = mbuf.shape[2]
    srows = silu_all.shape[0]

    def body(mbuf, s_ref, w_ref, cc_ref, tot_ref, gb_ref, gw_ref, gc_ref, tbuf, dmx, cmrow, send3, recv3):
        me, _ = _me_and_peers()
        msum = mbuf[0]
        for k in range(1, N_DEV):
            msum = msum + mbuf[k]
        tot_ref[...] = msum[n_mod_rows:n_mod_rows + n_vec_rows]
        gb_ref[...] = jnp.sum(msum[0:n_mod_rows], axis=0, keepdims=True)
        loc = pl.ds(pl.multiple_of(me * ncol, ncol), ncol)
        for k in range(N_DEV):
            dmx[k * SUBLANES:(k + 1) * SUBLANES, :] = mbuf[k, :, loc]
        cmrow[...] = msum
        cm_loc = cmrow[n_mod_rows - 1:n_mod_rows, loc]
        dmx[N_DEV * SUBLANES:, :] = jnp.concatenate([cm_loc, jnp.zeros((SUBLANES - 1, ncol), F32)], axis=0)
        gw_ref[...] = _dot_tn(s_ref[...], dmx[...])
        tbuf[me] = _dot_nt(dmx[N_DEV * SUBLANES:, :], w_ref[...])
        _exchange(lambda p: tbuf.at[me], lambda p: tbuf.at[p], send3, recv3)
        tsum = tbuf[0]
        for k in range(1, N_DEV):
            tsum = tsum + tbuf[k]
        cc = cc_ref[...]
        sg = _sigmoid(cc)
        gc_ref[...] = tsum[0:1, :] * (sg * (1.0 + cc * (1.0 - sg)))

    return pl.pallas_call(
        body, name="small_ar",
        out_shape=(jax.ShapeDtypeStruct((n_vec_rows, nm), F32), jax.ShapeDtypeStruct((1, nm), F32),
                   jax.ShapeDtypeStruct((D, ncol), F32), jax.ShapeDtypeStruct((1, D), F32)),
        in_specs=[_vmem()] * 4, out_specs=(_vmem(),) * 4,
        scratch_shapes=[pltpu.VMEM((N_DEV, SUBLANES, D), F32), pltpu.VMEM((srows, ncol), F32),
                        pltpu.VMEM((SUBLANES, nm), F32)] + [pltpu.SemaphoreType.DMA((N_DEV - 1,))] * 2,
        compiler_params=pltpu.CompilerParams(vmem_limit_bytes=VMEM_LIMIT),
    )(mbuf, silu_all, w_ada, c_ctx.reshape(1, D))


def _adam_update(w, g, m, v):
    mn = ADAM_B1 * m + (1.0 - ADAM_B1) * g
    vn = ADAM_B2 * v + (1.0 - ADAM_B2) * (g * g)
    m_hat = mn / (1.0 - ADAM_B1 ** ADAM_STEP)
    v_hat = vn / (1.0 - ADAM_B2 ** ADAM_STEP)
    return -ADAM_LR * (m_hat / (jnp.sqrt(v_hat) + ADAM_EPS) + ADAM_WD * w), mn, vn


def _adamw(w, g, m, v, name):
    rows, cols = w.shape
    tr = _div_tile(rows, 512, SUBLANES)

    def body(w_ref, g_ref, m_ref, v_ref, d_ref, nm_ref, nv_ref):
        d_ref[...], nm_ref[...], nv_ref[...] = _adam_update(w_ref[...], g_ref[...], m_ref[...], v_ref[...])

    spec = pl.BlockSpec((tr, cols), lambda i: (i, 0))
    return pl.pallas_call(
        functools.partial(body), name=name, grid=(rows // tr,),
        out_shape=(jax.ShapeDtypeStruct((rows, cols), F32),) * 3,
        in_specs=[spec] * 4, out_specs=(spec,) * 3,
        compiler_params=_params("parallel"),
    )(w, g, m, v)


def _adamw_small(items, name):
    n = len(items)

    def body(*refs):
        ins, outs = refs[:4 * n], refs[4 * n:]
        for i in range(n):
            w_ref, g_ref, m_ref, v_ref = ins[4 * i:4 * i + 4]
            outs[3 * i][...], outs[3 * i + 1][...], outs[3 * i + 2][...] = _adam_update(
                w_ref[...], g_ref[...], m_ref[...], v_ref[...])

    outs = pl.pallas_call(
        body, name=name,
        out_shape=tuple(jax.ShapeDtypeStruct(it[0].shape, F32) for it in items for _ in range(3)),
        in_specs=[_vmem()] * (4 * n), out_specs=(_vmem(),) * (3 * n),
        compiler_params=pltpu.CompilerParams(vmem_limit_bytes=VMEM_LIMIT),
    )(*[a for it in items for a in it])
    return [tuple(outs[3 * i:3 * i + 3]) for i in range(n)]


def _sum_adamw(buf, w, m, v, name):
    _, rows, cols = buf.shape
    tr = _div_tile(rows, 256, 2 * SUBLANES)

    def body(b_ref, w_ref, m_ref, v_ref, g_ref, d_ref, nm_ref, nv_ref):
        g = b_ref[0].astype(F32)
        for k in range(1, N_DEV):
            g = g + b_ref[k].astype(F32)
        g_ref[...] = g
        d_ref[...], nm_ref[...], nv_ref[...] = _adam_update(w_ref[...], g, m_ref[...], v_ref[...])

    spec = pl.BlockSpec((tr, cols), lambda i: (i, 0))
    return pl.pallas_call(
        functools.partial(body), name=name, grid=(rows // tr,),
        out_shape=(jax.ShapeDtypeStruct((rows, cols), F32),) * 4,
        in_specs=[pl.BlockSpec((N_DEV, tr, cols), lambda i: (0, i, 0))] + [spec] * 3, out_specs=(spec,) * 4,
        compiler_params=_params("parallel"),
    )(buf, w, m, v)


def _rope_tables(n_ctx, n):
    n_freq = RET_DIM // 4
    inv = np.float32(ROPE_BASE) ** (-np.arange(n_freq, dtype=np.float32) / np.float32(n_freq))
    tok = np.arange(n)
    pos_r = (tok // GRID_W).astype(np.float32)
    pos_c = (tok % GRID_W).astype(np.float32)
    ang_r = (pos_r[:, None] * inv[None, :]).astype(np.float32)
    ang_c = (pos_c[:, None] * inv[None, :]).astype(np.float32)
    cos = np.concatenate([np.cos(ang_r), np.cos(ang_r), np.cos(ang_c), np.cos(ang_c)], axis=-1)
    sin = np.concatenate([-np.sin(ang_r), np.sin(ang_r), -np.sin(ang_c), np.sin(ang_c)], axis=-1)
    cos = np.concatenate([np.ones((n_ctx, RET_DIM), np.float32), cos], axis=0)
    sin = np.concatenate([np.zeros((n_ctx, RET_DIM), np.float32), sin], axis=0)
    return jnp.asarray(cos, F32), jnp.asarray(sin, F32)


def _na_tables():
    q = np.arange(GRID_W)[:, None]
    k = np.arange(GRID_W)[None, :]
    start = np.clip(q - NA_KW // 2, 0, GRID_W - NA_KW)
    valid = (k >= start) & (k < start + NA_KW)
    dc = np.clip(k - q + (NA_KW - 1), 0, 2 * NA_KW - 2)
    ncls = 2 * NA_KW - 1
    onehot = (dc[None] == np.arange(ncls)[:, None, None]) & valid[None]
    return onehot.astype(np.float32), valid


def _paired_bias(rpb, onehot, valid):
    ncls = onehot.shape[0]
    pair = np.zeros((2 * ncls, GRID_W, LANES), np.float32)
    pair[:ncls, :, :GRID_W] = onehot
    pair[ncls:, :, GRID_W:] = onehot
    rows = jnp.concatenate([rpb[:, :-1], rpb[:, 1:]], axis=-1)
    t = jnp.einsum("hdc,cqk->hdqk", rows, jnp.asarray(pair), precision=lax.Precision.HIGHEST)
    return jnp.where(jnp.asarray(np.tile(valid, (1, 2)))[None, None], t, NEG_INF)


def kernel(x, c, ctx, c_ctx, w_ada, b_ada, g_pre_mix, g_post_mix, g_pre_mlp, g_post_mlp, w_in, ret_decay, ret_gn, na_rpb, w_out, w_mlp1, w_mlp2, loss_target, m_c_ctx, m_w_ada, m_b_ada, m_g_pre_mix, m_g_post_mix, m_g_pre_mlp, m_g_post_mlp, m_w_in, m_ret_decay, m_ret_gn, m_na_rpb, m_w_out, m_w_mlp1, m_w_mlp2, v_c_ctx, v_w_ada, v_b_ada, v_g_pre_mix, v_g_post_mix, v_g_pre_mlp, v_g_post_mlp, v_w_in, v_ret_decay, v_ret_gn, v_na_rpb, v_w_out, v_w_mlp1, v_w_mlp2):
    B, N, D = x.shape
    C = ctx.shape[1]
    T = C + N

    silu_all, mods_g, win_b, wout_l, w1_l, w2_l = _mod_gather(c, c_ctx, w_ada[0], b_ada, w_in[0].T, w_out[0],
                                                             w_mlp1[0], w_mlp2[0])
    mods_mine = mods_g.transpose(1, 0, 2).reshape(mods_g.shape[1], N_MOD * D)
    modl = jnp.concatenate([mods_mine[:B], mods_mine[SUBLANES:SUBLANES + 1]], axis=0)
    modl = modl.reshape(B + 1, N_MOD, 1, D)
    rin = w_in.shape[2]
    rout, c1, r2 = wout_l.shape[0], w1_l.shape[1], w2_l.shape[0]

    def rows_of(n):
        return lambda ref: _row_block(ref, n)

    def cols_of(n):
        return lambda ref: _col_block(ref, n)

    cos, sin = _rope_tables(C, N)
    onehot, valid = _na_tables()
    bias2 = _paired_bias(na_rpb[0], onehot, valid)
    lg = jax.nn.log_sigmoid(ret_decay[0].astype(F32))

    ag = _SplitScatter([wout_l, w1_l, w2_l], [rows_of(rout), cols_of(c1), rows_of(r2)],
                       [(N_DEV * rout, D), (D, N_DEV * c1), (N_DEV * r2, D)], "ag_mlp_start",
                       kind="gather", masks=SIBLING + ICI_SAME_CORE)
    h_all, proj = _inproj_fwd(x, ctx, modl, g_pre_mix, win_b, after=ag.token)
    o_ret, lat_ret, q_rot, k_rot = _ret_fwd(proj, cos, sin, lg, ret_gn, C)
    lat_na, na_probs = _na_fwd(proj, bias2, C)
    wout_part, w1_part, w2_part = _scatter_wait([ag], lat_na, "ag_mlp_wait")

    (dy1, dlat_ret, dlat_na, dmix, h2, act, du, dz, red_d) = _dense_core(
        lat_ret, lat_na, x, loss_target, modl, g_post_mix, g_pre_mlp, g_post_mlp, wout_part, w1_part, w2_part)

    gw_out_p = _tn_matmul(lat_ret[:, None], dmix, "gw_out_ret", rows_after=lat_na.shape[-1])
    gw_out_p = _tn_matmul(lat_na[:, None], dmix, "gw_out_na", rows_before=lat_ret.shape[-1], into=gw_out_p)
    gw1_p = _tn_matmul(h2[:, None], du, "gw_mlp1")
    gw2_p = _tn_matmul(act[:, None], dz, "gw_mlp2")
    rs_mlp = _SplitScatter([gw_out_p, gw1_p, gw2_p], [rows_of(rout), cols_of(c1), rows_of(r2)],
                           [(rout, D), (D, c1), (r2, D)], "rs_mlp_start")

    dret, dgn_p, dlg_p = _ret_bwd(proj, q_rot, k_rot, cos, sin, lg, ret_gn, o_ret, dlat_ret, C, after=rs_mlp.token)
    dna, rr = _na_bwd(proj, na_probs, dlat_na, C)
    ret_cols, na_cols = dret.shape[1] * dret.shape[3], dna.shape[1] * dna.shape[3]
    gwin_t_p = _tn_matmul(dret, h_all, "gw_in_ret", rows_after=na_cols)
    gwin_t_p = _tn_matmul(dna, h_all, "gw_in_na", rows_before=ret_cols, into=gwin_t_p)
    rs_in = _SplitScatter([gwin_t_p], [rows_of(rin)], [(rin, D)], "rs_w_in_start")
    grad_x, red_i = _inproj_bwd(dret, dna, x, ctx, dy1, modl, g_pre_mix, win_b, after=rs_in.token)

    rd = red_d
    nct = red_i.shape[1] * C // T
    ri_ctx = red_i[:, :nct].sum(axis=(0, 1))
    ri_lat = red_i[:, nct:].sum(axis=1)
    d_mods = jnp.concatenate([ri_lat[:, 0], ri_lat[:, 1], rd[:, 0], rd[:, 4], rd[:, 3], rd[:, 2]], axis=-1)
    d_cmods = jnp.concatenate([ri_ctx[0], ri_ctx[1], jnp.zeros(((N_MOD - 2) * D,), F32)])[None]
    dg_pre_mix = ri_lat[:, 2].sum(axis=0) + ri_ctx[2]
    dg_post_mix = rd[:, 1].sum(axis=0)
    dg_pre_mlp = rd[:, 5].sum(axis=0)
    dg_post_mlp = rd[:, 6].sum(axis=0)
    loss_p = rd[:, 7, 0].sum()
    d_gn = dgn_p[:, 0].sum(axis=0)
    d_lg = dlg_p[:, :, :2, 0].sum(axis=0).T
    d_decay = d_lg * jax.nn.sigmoid(-ret_decay[0].astype(F32))
    ncls = 2 * NA_KW - 1
    d_rpb = (jnp.pad(rr[:, :, :ncls], ((0, 0), (0, 1), (0, 0)))
             + jnp.pad(rr[:, :, GRID_W:GRID_W + ncls], ((0, 0), (1, 0), (0, 0))))
    d_rpb32 = jnp.pad(d_rpb, ((0, 0), (0, 0), (0, 32 - ncls)))
    pieces = [dg_pre_mix, dg_post_mix, dg_pre_mlp, dg_post_mlp, d_gn, d_rpb32.reshape(-1),
              jnp.pad(d_decay.reshape(-1), (0, LANES - d_decay.size)), jnp.full((LANES,), loss_p, F32)]
    vec = jnp.concatenate(pieces)
    nm = N_MOD * D
    n_vec_rows = -(-vec.shape[0] // nm)
    assert B + 1 + n_vec_rows <= SUBLANES
    vec = jnp.pad(vec, (0, n_vec_rows * nm - vec.shape[0])).reshape(n_vec_rows, nm)
    dm_slot = jnp.concatenate([d_mods, d_cmods, vec, jnp.zeros((SUBLANES - B - 1 - n_vec_rows, nm), F32)], axis=0)
    def whole(ref):
        return lambda p: ref

    small = _SplitScatter([dm_slot], [whole], [dm_slot.shape], "small_start")
    land_out, land_1, land_2, land_in = _scatter_wait([rs_mlp, rs_in], small.token, "rs_wait")
    fused = {"w_in": [a.T for a in _sum_adamw(land_in, w_in[0].T, m_w_in[0].T, v_w_in[0].T, "sum_adamw_w_in")],
             "w_out": _sum_adamw(land_out, w_out[0], m_w_out[0], v_w_out[0], "sum_adamw_w_out"),
             "w_mlp1": _sum_adamw(land_1, w_mlp1[0], m_w_mlp1[0], v_w_mlp1[0], "sum_adamw_w_mlp1"),
             "w_mlp2": _sum_adamw(land_2, w_mlp2[0], m_w_mlp2[0], v_w_mlp2[0], "sum_adamw_w_mlp2")}
    (mbuf,) = _scatter_wait([small], fused["w_mlp2"][0], "small_wait")
    tot, g_b_ada, g_w_ada, g_c_ctx = _small_ar(mbuf, silu_all, w_ada[0], c_ctx, B + 1, n_vec_rows)
    flat = tot.reshape(-1)
    o0 = 0
    g_pre_mix_g = flat[o0:o0 + D]; o0 += D
    g_post_mix_g = flat[o0:o0 + D]; o0 += D
    g_pre_mlp_g = flat[o0:o0 + D]; o0 += D
    g_post_mlp_g = flat[o0:o0 + D]; o0 += D
    g_gn = flat[o0:o0 + RET_WIDTH]; o0 += RET_WIDTH
    nrpb = NA_HEADS * (2 * NA_KH - 1) * 32
    g_rpb = flat[o0:o0 + nrpb].reshape(NA_HEADS, 2 * NA_KH - 1, 32)[:, :, :ncls]; o0 += nrpb
    g_decay = flat[o0:o0 + 2 * RET_HEADS].reshape(2, RET_HEADS); o0 += LANES
    loss = flat[o0]

    grads = {
        "c_ctx": g_c_ctx.reshape(c_ctx.shape), "w_ada": g_w_ada[None], "b_ada": g_b_ada.reshape(b_ada.shape),
        "g_pre_mix": g_pre_mix_g[None], "g_post_mix": g_post_mix_g[None], "g_pre_mlp": g_pre_mlp_g[None],
        "g_post_mlp": g_post_mlp_g[None], "w_in": fused["w_in"][0][None], "ret_decay": g_decay[None], "ret_gn": g_gn[None],
        "na_rpb": g_rpb[None], "w_out": fused["w_out"][0][None], "w_mlp1": fused["w_mlp1"][0][None],
        "w_mlp2": fused["w_mlp2"][0][None],
    }
    weights = dict(c_ctx=c_ctx, w_ada=w_ada, b_ada=b_ada, g_pre_mix=g_pre_mix, g_post_mix=g_post_mix,
                   g_pre_mlp=g_pre_mlp, g_post_mlp=g_post_mlp, w_in=w_in, ret_decay=ret_decay, ret_gn=ret_gn,
                   na_rpb=na_rpb, w_out=w_out, w_mlp1=w_mlp1, w_mlp2=w_mlp2)
    m_in = dict(c_ctx=m_c_ctx, w_ada=m_w_ada, b_ada=m_b_ada, g_pre_mix=m_g_pre_mix, g_post_mix=m_g_post_mix,
                g_pre_mlp=m_g_pre_mlp, g_post_mlp=m_g_post_mlp, w_in=m_w_in, ret_decay=m_ret_decay,
                ret_gn=m_ret_gn, na_rpb=m_na_rpb, w_out=m_w_out, w_mlp1=m_w_mlp1, w_mlp2=m_w_mlp2)
    v_in = dict(c_ctx=v_c_ctx, w_ada=v_w_ada, b_ada=v_b_ada, g_pre_mix=v_g_pre_mix, g_post_mix=v_g_post_mix,
                g_pre_mlp=v_g_pre_mlp, g_post_mlp=v_g_post_mlp, w_in=v_w_in, ret_decay=v_ret_decay,
                ret_gn=v_ret_gn, na_rpb=v_na_rpb, w_out=v_w_out, w_mlp1=v_w_mlp1, w_mlp2=v_w_mlp2)
    names = list(weights)
    deltas, new_m, new_v = {}, {}, {}
    def as_2d(n):
        shp = weights[n].shape
        two_d = (-1, shp[-1]) if len(shp) > 1 else (1, shp[0])
        return [a.reshape(two_d) for a in (weights[n], grads[n], m_in[n], v_in[n])]

    small = [n for n in names if n not in fused and weights[n].size <= 65536]
    updated = dict(zip(small, _adamw_small([as_2d(n) for n in small], "adamw_small")))
    for n in names:
        if n in fused:
            updated[n] = fused[n][1:]
        elif n not in updated:
            updated[n] = _adamw(*as_2d(n), "adamw_" + n)
        deltas[n], new_m[n], new_v[n] = (a.reshape(weights[n].shape) for a in updated[n])
    return (loss, grad_x, *[grads[n] for n in names], *[deltas[n] for n in names],
            *[new_m[n] for n in names], *[new_v[n] for n in names])
```

```python
import functools

import numpy as np
import jax
import jax.numpy as jnp
from jax import lax
from jax.experimental import pallas as pl
from jax.experimental.pallas import tpu as pltpu

F32 = jnp.float32
BF16 = jnp.bfloat16
MESH = pl.DeviceIdType.MESH

N_DEV = 8
LANES = 128
SUBLANES = 8
VMEM_LIMIT = 60 * 1024 * 1024

GRID_W = 64
RET_HEADS = 4
RET_DIM = 128
RET_WIDTH = RET_HEADS * RET_DIM
NA_HEADS = 8
NA_DIM = 64
NA_WIDTH = NA_HEADS * NA_DIM
NA_PAIRS = NA_HEADS // 2
NA_KH = 8
NA_KW = 16
NA_GROUP = 8
SEG = 512
ROPE_BASE = 10000.0
NORM_EPS = 1e-6
NEG_INF = -1e30
N_MOD = 6

ADAM_LR = 0.001
ADAM_B1 = 0.9
ADAM_B2 = 0.999
ADAM_EPS = 1e-08
ADAM_WD = 0.01
ADAM_STEP = 10


def _dot(a, b):
    return lax.dot_general(a, b, (((1,), (0,)), ((), ())), preferred_element_type=F32)


def _dot_nt(a, b):
    return lax.dot_general(a, b, (((1,), (1,)), ((), ())), preferred_element_type=F32)


def _dot_tn(a, b):
    return lax.dot_general(a, b, (((0,), (0,)), ((), ())), preferred_element_type=F32)


def _sigmoid(x):
    return 1.0 / (1.0 + jnp.exp(-x))


def _div_tile(n, cap, mult):
    if n <= cap:
        return n
    for t in range(cap - cap % mult, 0, -mult):
        if n % t == 0:
            return t
    raise ValueError(f"no tile for {n}")


def _params(*sem):
    return pltpu.CompilerParams(dimension_semantics=tuple(sem) if sem else None,
                                vmem_limit_bytes=VMEM_LIMIT)


def _vmem():
    return pl.BlockSpec(memory_space=pltpu.VMEM)


def _any():
    return pl.BlockSpec(memory_space=pl.ANY)


def _me_and_peers():
    x, y, c = lax.axis_index("x"), lax.axis_index("y"), lax.axis_index("c")
    me = 4 * x + 2 * y + c
    peers = []
    for m in range(1, N_DEV):
        px = 1 - x if (m >> 2) & 1 else x
        py = 1 - y if (m >> 1) & 1 else y
        pc = 1 - c if m & 1 else c
        peers.append(((px, py, pc), 4 * px + 2 * py + pc))
    return me, peers


def _exchange(src_for, dst_from, send_sems, recv_sems):
    me, peers = _me_and_peers()
    sent = []
    for i, (dev, pid) in enumerate(peers):
        cp = pltpu.make_async_remote_copy(src_ref=src_for(pid), dst_ref=dst_from(me),
                                          send_sem=send_sems.at[i], recv_sem=recv_sems.at[i],
                                          device_id=dev, device_id_type=MESH)
        cp.start()
        sent.append(cp)
    for i, (dev, pid) in enumerate(peers):
        pltpu.make_async_remote_copy(src_ref=src_for(pid), dst_ref=dst_from(pid),
                                     send_sem=send_sems.at[i], recv_sem=recv_sems.at[i],
                                     device_id=dev, device_id_type=MESH).wait_recv()
    for cp in sent:
        cp.wait_send()


SIBLING = (1,)
ICI_SAME_CORE = (2, 4, 6)
ALL_PEERS = tuple(range(1, N_DEV))


def _remote(src, dst, send_sem, recv_sem, dev):
    return pltpu.make_async_remote_copy(src_ref=src, dst_ref=dst, send_sem=send_sem, recv_sem=recv_sem,
                                        device_id=dev, device_id_type=MESH)


def _push_start(items, masks, send_sems, recv_sems):
    me, peers = _me_and_peers()
    for k, (src_for, dst_from) in enumerate(items):
        for m in masks:
            dev, pid = peers[m - 1]
            _remote(src_for(pid), dst_from(me), send_sems.at[k, m - 1], recv_sems.at[k, m - 1], dev).start()


def _push_wait_recv(items, masks, send_sems, recv_sems):
    me, peers = _me_and_peers()
    for k, (src_for, dst_from) in enumerate(items):
        for m in masks:
            dev, pid = peers[m - 1]
            _remote(src_for(pid), dst_from(pid), send_sems.at[k, m - 1], recv_sems.at[k, m - 1], dev).wait_recv()


def _push_wait_send(items, masks, send_sems, recv_sems):
    me, peers = _me_and_peers()
    for k, (src_for, dst_from) in enumerate(items):
        for m in masks:
            dev, pid = peers[m - 1]
            _remote(src_for(pid), dst_from(me), send_sems.at[k, m - 1], recv_sems.at[k, m - 1], dev).wait_send()


def _forward_start(items, send_sems, recv_sems):
    me, peers = _me_and_peers()
    sib = peers[0][0]
    for k, (blk_in, blk_out) in enumerate(items):
        for j, m in enumerate(ICI_SAME_CORE):
            pid = peers[m - 1][1]
            _remote(blk_in(pid), blk_out(pid), send_sems.at[k, j], recv_sems.at[k, j], sib).start()


def _forward_wait(items, send_sems, recv_sems):
    me, peers = _me_and_peers()
    sib = peers[0][0]
    for k, (blk_in, blk_out) in enumerate(items):
        for j, m in enumerate(ICI_SAME_CORE):
            got = peers[(m | 1) - 1][1]
            _remote(blk_in(got), blk_out(got), send_sems.at[k, j], recv_sems.at[k, j], sib).wait_recv()
    for k, (blk_in, blk_out) in enumerate(items):
        for j, m in enumerate(ICI_SAME_CORE):
            pid = peers[m - 1][1]
            _remote(blk_in(pid), blk_out(pid), send_sems.at[k, j], recv_sems.at[k, j], sib).wait_send()


def _mod_gather(c, c_ctx, w_ada, b_ada, w_in_t, w_out, w1, w2):
    B, D = c.shape
    ncol = w_ada.shape[1]
    rows = SUBLANES * N_DEV + SUBLANES

    def body(c_ref, cc_ref, w_ref, b_ref, win_ref, wout_ref, w1_ref, w2_ref,
             s_ref, m_ref, gin_ref, wout_b, w1_b, w2_b,
             win_b, msend, send1, recv1, send2, recv2, wsend, wrecv, fsend, frecv, lsem):
        me, _ = _me_and_peers()
        win_b[...] = win_ref[...].astype(BF16)
        block = _row_block(gin_ref, w_in_t.shape[0])
        gather = [(lambda p: win_b, block)]
        own = pltpu.make_async_copy(win_b, block(me), lsem.at[0])
        cv = c_ref[...]
        slot = jnp.concatenate([cv * _sigmoid(cv), jnp.zeros((SUBLANES - B, D), F32)], axis=0)
        my_rows = pl.ds(pl.multiple_of(me * SUBLANES, SUBLANES), SUBLANES)
        s_ref[my_rows, :] = slot
        ccv = cc_ref[...]
        s_ref[SUBLANES * N_DEV:, :] = jnp.concatenate(
            [ccv * _sigmoid(ccv), jnp.zeros((SUBLANES - 1, D), F32)], axis=0)

        def rows_of(p):
            return s_ref.at[pl.ds(pl.multiple_of(p * SUBLANES, SUBLANES), SUBLANES), :]

        _exchange(lambda p: rows_of(me), rows_of, send1, recv1)
        own.start()
        _push_start(gather, SIBLING + ICI_SAME_CORE, wsend, wrecv)
        wout_b[...] = wout_ref[...].astype(BF16)
        w1_b[...] = w1_ref[...].astype(BF16)
        w2_b[...] = w2_ref[...].astype(BF16)
        b_loc = b_ref[:, pl.ds(pl.multiple_of(me * ncol, ncol), ncol)]
        mods = _dot(s_ref[...], w_ref[...]) + b_loc
        for p in range(N_DEV):
            msend[p] = jnp.concatenate([mods[p * SUBLANES:(p + 1) * SUBLANES], mods[N_DEV * SUBLANES:]], axis=0)
        m_ref[me] = msend[me]
        columns = [(lambda p: msend.at[p], lambda p: m_ref.at[p])]
        _push_start(columns, ALL_PEERS, send2, recv2)
        _push_wait_recv(gather, ICI_SAME_CORE, wsend, wrecv)
        relay = [(block, block)]
        _forward_start(relay, fsend, frecv)
        _push_wait_recv(columns, ALL_PEERS, send2, recv2)
        _push_wait_recv(gather, SIBLING, wsend, wrecv)
        _forward_wait(relay, fsend, frecv)
        _push_wait_send(columns, ALL_PEERS, send2, recv2)
        _push_wait_send(gather, SIBLING + ICI_SAME_CORE, wsend, wrecv)
        own.wait()

    return pl.pallas_call(
        body, name="mod_gather",
        out_shape=(jax.ShapeDtypeStruct((rows, D), F32), jax.ShapeDtypeStruct((N_DEV, 2 * SUBLANES, ncol), F32),
                   jax.ShapeDtypeStruct((N_DEV * w_in_t.shape[0], D), BF16),
                   jax.ShapeDtypeStruct(w_out.shape, BF16), jax.ShapeDtypeStruct(w1.shape, BF16),
                   jax.ShapeDtypeStruct(w2.shape, BF16)),
        in_specs=[_vmem()] * 8, out_specs=(_vmem(), _vmem(), _any(), _vmem(), _vmem(), _vmem()),
        scratch_shapes=[pltpu.VMEM(w_in_t.shape, BF16), pltpu.VMEM((N_DEV, 2 * SUBLANES, ncol), F32)]
                       + [pltpu.SemaphoreType.DMA((N_DEV - 1,))] * 2
                       + [pltpu.SemaphoreType.DMA((1, N_DEV - 1))] * 4 + [pltpu.SemaphoreType.DMA((1, 3))] * 2
                       + [pltpu.SemaphoreType.DMA((1,))],
        compiler_params=pltpu.CompilerParams(vmem_limit_bytes=VMEM_LIMIT),
    )(c, c_ctx.reshape(1, D), w_ada, b_ada, w_in_t, w_out, w1, w2)


def _row_block(ref, rows):
    return lambda p: ref.at[pl.ds(pl.multiple_of(p * rows, 2 * SUBLANES), rows), :]


def _col_block(ref, cols):
    return lambda p: ref.at[:, pl.ds(pl.multiple_of(p * cols, LANES), cols)]


def _slot(ref):
    return lambda p: ref.at[p]


def _grid_call(body, *, name, grid, out_shape, in_specs, out_specs, scratch_shapes, args, after=None):
    n_in = len(args)

    def ordered_body(*refs):
        body(*refs[:n_in], *refs[n_in + 1:])

    return pl.pallas_call(
        body if after is None else ordered_body, name=name, grid=grid, out_shape=tuple(out_shape),
        in_specs=list(in_specs) + ([] if after is None else [_any()]), out_specs=tuple(out_specs),
        scratch_shapes=list(scratch_shapes), compiler_params=_params(*(("arbitrary",) * len(grid))),
    )(*args, *([] if after is None else [after]))


def _token_tiles(n_ctx, tm):
    nct = n_ctx // tm

    def ctx_spec(D):
        return pl.BlockSpec((None, tm, D), lambda b, t: (b, jnp.minimum(t, nct - 1), 0))

    def lat_spec(D):
        return pl.BlockSpec((None, tm, D), lambda b, t: (b, jnp.maximum(t - nct, 0), 0))

    return nct, ctx_spec, lat_spec


def _inproj_fwd(x, ctx, modl, g1, w_in_t, after):
    B, N, D = x.shape
    n_ctx = ctx.shape[1]
    T = n_ctx + N
    nw = w_in_t.shape[0]
    tm = _div_tile(n_ctx, 256, 16)
    nct, ctx_spec, lat_spec = _token_tiles(n_ctx, tm)

    def body(c_ref, x_ref, sh_ref, sc_ref, g_ref, w_ref, h_ref, p_ref):
        x = jnp.where(pl.program_id(1) < nct, c_ref[...], x_ref[...])
        r = lax.rsqrt(jnp.mean(x * x, axis=-1, keepdims=True) + NORM_EPS)
        h = ((x * r) * g_ref[...]) * (1.0 + sc_ref[...]) + sh_ref[...]
        hb = h.astype(BF16)
        h_ref[...] = hb
        p_ref[...] = _dot_nt(hb, w_ref[...]).astype(BF16)

    def mrow(b, t):
        return jnp.where(t < nct, B, b)

    return _grid_call(
        body, name="inproj_fwd", grid=(B, T // tm),
        out_shape=(jax.ShapeDtypeStruct((B, T, D), BF16), jax.ShapeDtypeStruct((B, T, nw), BF16)),
        in_specs=[ctx_spec(D), lat_spec(D),
                  pl.BlockSpec((None, None, 1, D), lambda b, t: (mrow(b, t), 0, 0, 0)),
                  pl.BlockSpec((None, None, 1, D), lambda b, t: (mrow(b, t), 1, 0, 0)),
                  pl.BlockSpec((1, D), lambda b, t: (0, 0)),
                  pl.BlockSpec((nw, D), lambda b, t: (0, 0))],
        out_specs=(pl.BlockSpec((None, tm, D), lambda b, t: (b, t, 0)),
                   pl.BlockSpec((None, tm, nw), lambda b, t: (b, t, 0))),
        scratch_shapes=[], args=(ctx, x, modl, modl, g1, w_in_t), after=after)


def _swap32(x):
    lane = lax.broadcasted_iota(jnp.int32, x.shape, 1)
    return jnp.where((lane % 64) < 32, pltpu.roll(x, 96, 1), pltpu.roll(x, 32, 1))


def _rope(x, cos, sin):
    return x * cos + _swap32(x) * sin


def _unrope(dy, cos, sin):
    return dy * cos + _swap32(dy * sin)


def _ret_weights(lgf, lgb, dist):
    return jnp.exp(jnp.where(dist >= 0.0, lgf * dist, -lgb * dist))


class _RetDecay:
    def __init__(self, lgf, lgb, rows):
        r = lax.broadcasted_iota(jnp.int32, (rows, RET_DIM), 0).astype(F32)
        self.head = r + 1.0
        self.tail = (rows - 1.0) - r
        self.q_f = jnp.exp(lgf * self.head)
        self.k_f = jnp.exp(lgf * self.tail)
        self.q_b = jnp.exp(lgb * self.tail)
        self.k_b = jnp.exp(lgb * self.head)


def _ret_states(kf32, vs, lgf, lgb, C, c, nt, hf, hb, hfa=None, hba=None):
    dec = _RetDecay(lgf, lgb, c)
    dec_c = _RetDecay(lgf, lgb, C)
    step_f = jnp.exp(jnp.zeros((RET_DIM, RET_DIM), F32) + lgf * c)
    step_b = jnp.exp(jnp.zeros((RET_DIM, RET_DIM), F32) + lgb * c)

    def upd(rows, kdec):
        return _dot_tn((kf32[rows, :] * kdec).astype(BF16), vs[rows, :])

    def lat(t):
        return slice(C + t * c, C + (t + 1) * c)

    state = upd(slice(0, C), dec_c.k_f)
    aged = jnp.zeros_like(state)
    for t in range(nt):
        hf[t] = state.astype(BF16)
        if hfa is not None:
            hfa[t] = aged
        if t < nt - 1:
            aged = step_f * (aged + c * state)
            state = step_f * state + upd(lat(t), dec.k_f)
    state = upd(slice(0, C), dec_c.k_b)
    aged = jnp.zeros_like(state)
    for t in range(nt - 1, -1, -1):
        hb[t] = state.astype(BF16)
        if hba is not None:
            hba[t] = aged
        if t > 0:
            aged = step_b * (aged + c * state)
            state = step_b * state + upd(lat(t), dec.k_b)
    return dec, dec_c, step_f, step_b


def _ret_fwd(proj, cos, sin, lg, gn, n_ctx):
    B, T, _ = proj.shape
    C = n_ctx
    N = T - C
    c = _div_tile(N, 256, 16)
    nt = N // c
    scale = RET_DIM ** -0.5

    def body(lg_ref, q_ref, k_ref, vs, g_ref, cos_ref, sin_ref, gn_ref, o_ref, lat_ref, qr_ref, kf32,
             qs, ks, hf, hb):
        h = pl.program_id(1)
        lgf = lg_ref[0, h]
        lgb = lg_ref[1, h]
        for rows in [slice(0, C)] + [slice(C + t * c, C + (t + 1) * c) for t in range(nt)]:
            cosb = cos_ref[rows, :]
            sinb = sin_ref[rows, :]
            qr = _rope(q_ref[rows, :].astype(F32), cosb, sinb) * scale
            qr_ref[rows, :] = qr
            qs[rows, :] = qr.astype(BF16)
            kr = _rope(k_ref[rows, :].astype(F32), cosb, sinb)
            kf32[rows, :] = kr
            ks[rows, :] = kr.astype(BF16)
        gnv = gn_ref[...]
        dec, _, _, _ = _ret_states(kf32, vs, lgf, lgb, C, c, nt, hf, hb)
        rc = (lax.broadcasted_iota(jnp.int32, (c, c), 0) - lax.broadcasted_iota(jnp.int32, (c, c), 1)).astype(F32)
        w_diag = _ret_weights(lgf, lgb, rc)
        for t in range(nt):
            rows = slice(C + t * c, C + (t + 1) * c)
            qt = qs[rows, :]
            s = _dot_nt(qt, ks[rows, :])
            o = (_dot((s * w_diag).astype(BF16), vs[rows, :])
                 + dec.q_f * _dot(qt, hf[t]) + dec.q_b * _dot(qt, hb[t]))
            o_ref[t * c:(t + 1) * c, :] = o
            mu = jnp.mean(o, axis=-1, keepdims=True)
            oc = o - mu
            var = jnp.mean(oc * oc, axis=-1, keepdims=True)
            yh = oc * lax.rsqrt(var + NORM_EPS)
            g = g_ref[rows, :].astype(F32)
            lat_ref[t * c:(t + 1) * c, :] = ((yh * gnv) * (g * _sigmoid(g))).astype(BF16)

    def col(seg):
        return pl.BlockSpec((None, T, RET_DIM), lambda b, h, seg=seg: (b, 0, seg * RET_HEADS + h))

    return _grid_call(
        body, name="ret_fwd", grid=(B, RET_HEADS),
        out_shape=(jax.ShapeDtypeStruct((B, N, RET_WIDTH), F32), jax.ShapeDtypeStruct((B, N, RET_WIDTH), BF16),
                   jax.ShapeDtypeStruct((B, T, RET_WIDTH), F32), jax.ShapeDtypeStruct((B, T, RET_WIDTH), F32)),
        in_specs=[pl.BlockSpec(memory_space=pltpu.SMEM), col(0), col(1), col(2), col(3),
                  pl.BlockSpec((T, RET_DIM), lambda b, h: (0, 0)), pl.BlockSpec((T, RET_DIM), lambda b, h: (0, 0)),
                  pl.BlockSpec((1, RET_DIM), lambda b, h: (0, h))],
        out_specs=(pl.BlockSpec((None, N, RET_DIM), lambda b, h: (b, 0, h)),
                   pl.BlockSpec((None, N, RET_DIM), lambda b, h: (b, 0, h)),
                   pl.BlockSpec((None, T, RET_DIM), lambda b, h: (b, 0, h)),
                   pl.BlockSpec((None, T, RET_DIM), lambda b, h: (b, 0, h))),
        scratch_shapes=[pltpu.VMEM((T, RET_DIM), BF16)] * 2 + [pltpu.VMEM((nt, RET_DIM, RET_DIM), BF16)] * 2,
        args=(lg, proj, proj, proj, proj, cos, sin, gn))


def _ret_bwd(proj, q_rot, k_rot, cos, sin, lg, gn, o, dlat, n_ctx, after):
    B, T, _ = proj.shape
    C = n_ctx
    N = T - C
    c = _div_tile(N, 256, 16)
    nt = N // c
    scale = RET_DIM ** -0.5

    def lat(t):
        return slice(C + t * c, C + (t + 1) * c)

    def body(lg_ref, qf32, kf32, vs, g_ref, cos_ref, sin_ref, gn_ref, o_ref, dl_ref,
             d_ref, dgn_ref, dlg_ref, qs, ks, dos, hf, hb, hfa, hba, gf_s, gb_s):
        h = pl.program_id(1)
        lgf = lg_ref[0, h]
        lgb = lg_ref[1, h]
        gnv = gn_ref[...]

        def fold(a):
            return jnp.sum(a.reshape(a.shape[0] // SUBLANES, SUBLANES, a.shape[1]), axis=0)

        for rows in [slice(0, C)] + [lat(t) for t in range(nt)]:
            qs[rows, :] = qf32[rows, :].astype(BF16)
            ks[rows, :] = kf32[rows, :].astype(BF16)

        dgn = jnp.zeros((1, RET_DIM), F32)
        for t in range(nt):
            lrows = slice(t * c, (t + 1) * c)
            ov = o_ref[lrows, :]
            mu = jnp.mean(ov, axis=-1, keepdims=True)
            oc = ov - mu
            var = jnp.mean(oc * oc, axis=-1, keepdims=True)
            rstd = lax.rsqrt(var + NORM_EPS)
            yh = oc * rstd
            g = g_ref[lat(t), :].astype(F32)
            sg = _sigmoid(g)
            dl = dl_ref[lrows, :]
            d_ref[3, lat(t), :] = (dl * (yh * gnv) * (sg * (1.0 + g * (1.0 - sg)))).astype(BF16)
            dls = dl * (g * sg)
            dgn = dgn + jnp.sum(dls * yh, axis=0, keepdims=True)
            dyh = dls * gnv
            do = rstd * (dyh - jnp.mean(dyh, axis=-1, keepdims=True)
                         - yh * jnp.mean(dyh * yh, axis=-1, keepdims=True))
            dos[lrows, :] = do.astype(BF16)
        dgn_ref[...] = jnp.concatenate([dgn, jnp.zeros((SUBLANES - 1, RET_DIM), F32)], axis=0)
        d_ref[3, 0:C, :] = jnp.zeros((C, RET_DIM), BF16)
        d_ref[0, 0:C, :] = jnp.zeros((C, RET_DIM), BF16)

        dec, dec_c, step_f, step_b = _ret_states(kf32, vs, lgf, lgb, C, c, nt, hf, hb, hfa, hba)

        def zmat(t, qdec):
            return _dot_tn((qf32[lat(t), :] * qdec).astype(BF16), dos[t * c:(t + 1) * c, :])

        acc3f = jnp.zeros((RET_DIM, RET_DIM), F32)
        acc3b = jnp.zeros((RET_DIM, RET_DIM), F32)
        state = jnp.zeros((RET_DIM, RET_DIM), F32)
        for t in range(nt - 1, -1, -1):
            gf_s[t] = state.astype(BF16)
            z = zmat(t, dec.q_f)
            acc3f = acc3f + hfa[t] * z
            state = step_f * state + z
        gctx_f = state.astype(BF16)
        state = jnp.zeros((RET_DIM, RET_DIM), F32)
        for t in range(nt):
            gb_s[t] = state.astype(BF16)
            z = zmat(t, dec.q_b)
            acc3b = acc3b + hba[t] * z
            state = step_b * state + z
        gctx_b = state.astype(BF16)

        rc = (lax.broadcasted_iota(jnp.int32, (c, c), 0) - lax.broadcasted_iota(jnp.int32, (c, c), 1)).astype(F32)
        w_diag = _ret_weights(lgf, lgb, rc)
        wg_f = jnp.where(rc >= 0.0, w_diag * rc, 0.0)
        wg_b = jnp.where(rc < 0.0, -w_diag * rc, 0.0)
        accf = jnp.zeros((SUBLANES, RET_DIM), F32)
        accb = jnp.zeros((SUBLANES, RET_DIM), F32)
        gdf = jnp.zeros((SUBLANES, c), F32)
        gdb = jnp.zeros((SUBLANES, c), F32)
        for t in range(nt):
            rows = lat(t)
            qt = qs[rows, :]
            kt = ks[rows, :]
            vt = vs[rows, :]
            dot = dos[t * c:(t + 1) * c, :]
            s = _dot_nt(qt, kt)
            dp = _dot_nt(dot, vt)
            dv = _dot_tn((s * w_diag).astype(BF16), dot)
            ds = (dp * w_diag).astype(BF16)
            dq = _dot(ds, kt)
            dk = _dot_tn(ds, qt)
            gs = dp * s
            gdf = gdf + fold(gs * wg_f)
            gdb = gdb + fold(gs * wg_b)
            qv = qf32[rows, :]
            kv = kf32[rows, :]
            dq_f = dec.q_f * _dot_nt(dot, hf[t])
            dq_b = dec.q_b * _dot_nt(dot, hb[t])
            dk_f = dec.k_f * _dot_nt(vt, gf_s[t])
            dk_b = dec.k_b * _dot_nt(vt, gb_s[t])
            accf = accf + fold(dec.head * dq_f * qv) + fold(dec.tail * dk_f * kv)
            accb = accb + fold(dec.tail * dq_b * qv) + fold(dec.head * dk_b * kv)
            dv = dv + dec.k_f * _dot(kt, gf_s[t]) + dec.k_b * _dot(kt, gb_s[t])
            cosb = cos_ref[rows, :]
            sinb = sin_ref[rows, :]
            d_ref[0, rows, :] = _unrope((dq + dq_f + dq_b) * scale, cosb, sinb).astype(BF16)
            d_ref[1, rows, :] = _unrope(dk + dk_f + dk_b, cosb, sinb).astype(BF16)
            d_ref[2, rows, :] = dv.astype(BF16)
        kc = ks[0:C, :]
        vc = vs[0:C, :]
        kcv = kf32[0:C, :]
        dkc_f = dec_c.k_f * _dot_nt(vc, gctx_f)
        dkc_b = dec_c.k_b * _dot_nt(vc, gctx_b)
        accf = accf + fold(dec_c.tail * dkc_f * kcv)
        accb = accb + fold(dec_c.head * dkc_b * kcv)
        d_ref[1, 0:C, :] = (dkc_f + dkc_b).astype(BF16)
        d_ref[2, 0:C, :] = (dec_c.k_f * _dot(kc, gctx_f) + dec_c.k_b * _dot(kc, gctx_b)).astype(BF16)
        gf = jnp.sum(gdf) + jnp.sum(accf) + jnp.sum(acc3f)
        gb = jnp.sum(gdb) + jnp.sum(accb) + jnp.sum(acc3b)
        row = lax.broadcasted_iota(jnp.int32, (SUBLANES, LANES), 0)
        dlg_ref[...] = jnp.where(row == 0, gf, jnp.where(row == 1, gb, 0.0))

    def col(seg):
        return pl.BlockSpec((None, T, RET_DIM), lambda b, h, seg=seg: (b, 0, seg * RET_HEADS + h))

    def head(rows):
        return pl.BlockSpec((None, rows, RET_DIM), lambda b, h: (b, 0, h))

    return _grid_call(
        body, name="ret_bwd", grid=(B, RET_HEADS),
        out_shape=(jax.ShapeDtypeStruct((B, 4, T, RET_WIDTH), BF16),
                   jax.ShapeDtypeStruct((B, SUBLANES, RET_WIDTH), F32),
                   jax.ShapeDtypeStruct((B, RET_HEADS, SUBLANES, LANES), F32)),
        in_specs=[pl.BlockSpec(memory_space=pltpu.SMEM), head(T), head(T), col(2), col(3),
                  pl.BlockSpec((T, RET_DIM), lambda b, h: (0, 0)), pl.BlockSpec((T, RET_DIM), lambda b, h: (0, 0)),
                  pl.BlockSpec((1, RET_DIM), lambda b, h: (0, h)), head(N), head(N)],
        out_specs=(pl.BlockSpec((None, 4, T, RET_DIM), lambda b, h: (b, 0, 0, h)),
                   pl.BlockSpec((None, SUBLANES, RET_DIM), lambda b, h: (b, 0, h)),
                   pl.BlockSpec((None, None, SUBLANES, LANES), lambda b, h: (b, h, 0, 0))),
        scratch_shapes=[pltpu.VMEM((T, RET_DIM), BF16)] * 2 + [pltpu.VMEM((N, RET_DIM), BF16)]
                       + [pltpu.VMEM((nt, RET_DIM, RET_DIM), BF16)] * 2 + [pltpu.VMEM((nt, RET_DIM, RET_DIM), F32)] * 2
                       + [pltpu.VMEM((nt, RET_DIM, RET_DIM), BF16)] * 2,
        args=(lg, q_rot, k_rot, proj, proj, cos, sin, gn, o, dlat), after=after)


def _na_geometry(rows):
    kh = min(NA_KH, rows)
    return kh, kh * GRID_W


def _pair_select():
    lane = lax.broadcasted_iota(jnp.int32, (2 * GRID_W, LANES), 1)
    row = lax.broadcasted_iota(jnp.int32, (2 * GRID_W, LANES), 0)
    return (lane >= NA_DIM) == (row >= GRID_W)


def _pair_bias(bias_ref, dr0, kh):
    return jnp.concatenate(
        [jnp.concatenate([bias_ref[e, pl.ds(dr0 + 2 * m, 1)].reshape(GRID_W, LANES) for m in range(kh // 2)], axis=1)
         for e in range(2)], axis=0)


def _na_softmax(s_loc, s_ctx):
    mx = jnp.maximum(jnp.max(s_loc, axis=-1, keepdims=True), jnp.max(s_ctx, axis=-1, keepdims=True))
    p_loc = jnp.exp(s_loc - mx)
    p_ctx = jnp.exp(s_ctx - mx)
    den = jnp.sum(p_loc, axis=-1, keepdims=True) + jnp.sum(p_ctx, axis=-1, keepdims=True)
    return p_loc, p_ctx, den


def _na_fwd(proj, bias2, n_ctx):
    assert proj.dtype == BF16
    B, T, _ = proj.shape
    C = n_ctx
    N = T - C
    R = N // GRID_W
    kh, nk = _na_geometry(R)
    scale = NA_DIM ** -0.5
    base = (4 * RET_WIDTH) // LANES

    def body(q_ref, kb16, vb16, bias_ref, out_ref, p_ref):
        kc = kb16[0:C, :]
        vc = vb16[0:C, :]
        lane = lax.broadcasted_iota(jnp.int32, (GRID_W, LANES), 1)
        sel2 = _pair_select()

        def group(gi, carry):
            pre = []
            for u in range(NA_GROUP):
                r = gi * NA_GROUP + u
                bs = jnp.clip(r - kh // 2, 0, R - kh)
                dr0 = bs - r + (NA_KH - 1)
                q = q_ref[pl.ds(pl.multiple_of(C + r * GRID_W, GRID_W), GRID_W), :].astype(F32) * scale
                q2 = jnp.where(sel2, jnp.concatenate([q, q], axis=0), 0.0).astype(BF16)
                band = pl.ds(pl.multiple_of(C + bs * GRID_W, GRID_W), nk)
                s_loc = _dot_nt(q2, kb16[band, :]) + _pair_bias(bias_ref, dr0, kh)
                s_ctx = _dot_nt(q2, kc)
                pre.append((r, band, s_loc, s_ctx))
            mid = [(r, band) + _na_softmax(s_loc, s_ctx) for r, band, s_loc, s_ctx in pre]
            for r, band, p_loc, p_ctx, den in mid:
                inv = 1.0 / den
                pb_loc = (p_loc * inv).astype(BF16)
                pb_ctx = (p_ctx * inv).astype(BF16)
                p_ref[r, :, 0:nk] = pb_loc
                p_ref[r, :, nk:] = pb_ctx
                o2 = _dot(pb_loc, vb16[band, :]) + _dot(pb_ctx, vc)
                out_ref[pl.ds(pl.multiple_of(r * GRID_W, GRID_W), GRID_W), :] = jnp.where(
                    lane < NA_DIM, o2[:GRID_W], o2[GRID_W:]).astype(BF16)
            return carry

        lax.fori_loop(0, R // NA_GROUP, group, 0)

    def col(seg):
        return pl.BlockSpec((None, T, LANES), lambda b, p, seg=seg: (b, 0, base + seg * NA_PAIRS + p))

    return _grid_call(
        body, name="na_fwd", grid=(B, NA_PAIRS),
        out_shape=(jax.ShapeDtypeStruct((B, N, NA_WIDTH), BF16),
                   jax.ShapeDtypeStruct((B, NA_PAIRS, R, 2 * GRID_W, nk + C), BF16)),
        in_specs=[col(0), col(1), col(2),
                  pl.BlockSpec((2, 2 * NA_KH - 2, GRID_W, LANES), lambda b, p: (p, 0, 0, 0))],
        out_specs=(pl.BlockSpec((None, N, LANES), lambda b, p: (b, 0, p)),
                   pl.BlockSpec((None, None, R, 2 * GRID_W, nk + C), lambda b, p: (b, p, 0, 0, 0))),
        scratch_shapes=[],
        args=(proj, proj, proj, bias2))


def _na_bwd(proj, probs, dlat, n_ctx):
    assert proj.dtype == BF16
    B, T, _ = proj.shape
    C = n_ctx
    N = T - C
    R = N // GRID_W
    kh, nk = _na_geometry(R)
    scale = NA_DIM ** -0.5
    base = (4 * RET_WIDTH) // LANES

    def class_sums(tiles):
        n = tiles.shape[0]
        acc = None
        for v in range(GRID_W // SUBLANES):
            part = tiles[:, v * SUBLANES:(v + 1) * SUBLANES, :].reshape(n * SUBLANES, LANES)
            part = pltpu.roll(part, (NA_KW - 1 - v * SUBLANES) % LANES, 1)
            acc = part if acc is None else acc + part
        row = lax.broadcasted_iota(jnp.int32, acc.shape, 0)
        for bit in (1, 2, 4):
            acc = jnp.where((row & bit) != 0, pltpu.roll(acc, LANES - bit, 1), acc)
        return jnp.sum(acc.reshape(n, SUBLANES, LANES), axis=1)

    def body(q_ref, kb16, vb16, p_ref, dl_ref, d_ref, rr_ref, dkv, db_ref):
        b = pl.program_id(1)
        kc = kb16[0:C, :]
        vc = vb16[0:C, :]
        lane = lax.broadcasted_iota(jnp.int32, (GRID_W, LANES), 1)
        dkv[...] = jnp.zeros(dkv.shape, F32)
        d_ref[0, 0:C, :] = jnp.zeros((C, LANES), BF16)

        @pl.when(b == 0)
        def _():
            db_ref[...] = jnp.zeros(db_ref.shape, F32)

        sel2 = _pair_select()

        def group(gi, carry):
            pre = []
            for u in range(NA_GROUP):
                r = gi * NA_GROUP + u
                bs = jnp.clip(r - kh // 2, 0, R - kh)
                dr0 = bs - r + (NA_KH - 1)
                q = q_ref[pl.ds(pl.multiple_of(C + r * GRID_W, GRID_W), GRID_W), :].astype(F32) * scale
                do = dl_ref[pl.ds(pl.multiple_of(r * GRID_W, GRID_W), GRID_W), :]
                q2 = jnp.where(sel2, jnp.concatenate([q, q], axis=0), 0.0).astype(BF16)
                do2 = jnp.where(sel2, jnp.concatenate([do, do], axis=0), 0.0).astype(BF16)
                band = pl.ds(pl.multiple_of(C + bs * GRID_W, GRID_W), nk)
                dp_loc = _dot_nt(do2, vb16[band, :])
                dp_ctx = _dot_nt(do2, vc)
                pre.append((r, dr0, band, q2, do2, dp_loc, dp_ctx))
            mid = []
            for r, dr0, band, q2, do2, dp_loc, dp_ctx in pre:
                pb_loc = p_ref[r, :, 0:nk]
                pb_ctx = p_ref[r, :, nk:]
                p_loc = pb_loc.astype(F32)
                p_ctx = pb_ctx.astype(F32)
                delta = (jnp.sum(p_loc * dp_loc, axis=-1, keepdims=True)
                         + jnp.sum(p_ctx * dp_ctx, axis=-1, keepdims=True))
                ds_loc = p_loc * (dp_loc - delta)
                ds_ctx = p_ctx * (dp_ctx - delta)
                mid.append((r, dr0, band, q2, do2, pb_loc, pb_ctx, ds_loc, ds_ctx))
            for r, dr0, band, q2, do2, pb_loc, pb_ctx, ds_loc, ds_ctx in mid:
                dsb_loc = ds_loc.astype(BF16)
                dsb_ctx = ds_ctx.astype(BF16)
                dq2 = _dot(dsb_loc, kb16[band, :]) + _dot(dsb_ctx, kc)
                d_ref[0, pl.ds(pl.multiple_of(C + r * GRID_W, GRID_W), GRID_W), :] = (jnp.where(
                    lane < NA_DIM, dq2[:GRID_W], dq2[GRID_W:]) * scale).astype(BF16)
                dkv[0, band, :] += _dot_tn(dsb_loc, q2)
                dkv[1, band, :] += _dot_tn(pb_loc, do2)
                dkv[0, 0:C, :] += _dot_tn(dsb_ctx, q2)
                dkv[1, 0:C, :] += _dot_tn(pb_ctx, do2)
                for e in range(2):
                    for m in range(kh // 2):
                        db_ref[e, pl.ds(dr0 + 2 * m, 1)] += ds_loc[e * GRID_W:(e + 1) * GRID_W,
                                                                   m * LANES:(m + 1) * LANES].reshape(1, GRID_W, LANES)
            return carry

        lax.fori_loop(0, R // NA_GROUP, group, 0)
        d_ref[1] = dkv[0].astype(BF16)
        d_ref[2] = dkv[1].astype(BF16)

        @pl.when(b == B - 1)
        def _():
            for e in range(2):
                rr_ref[e] = class_sums(db_ref[e])

    def col(seg):
        return pl.BlockSpec((None, T, LANES), lambda p, b, seg=seg: (b, 0, base + seg * NA_PAIRS + p))

    return _grid_call(
        body, name="na_bwd", grid=(NA_PAIRS, B),
        out_shape=(jax.ShapeDtypeStruct((B, 3, T, NA_WIDTH), BF16),
                   jax.ShapeDtypeStruct((NA_HEADS, 2 * NA_KH - 2, LANES), F32)),
        in_specs=[col(0), col(1), col(2),
                  pl.BlockSpec((None, None, R, 2 * GRID_W, nk + C), lambda p, b: (b, p, 0, 0, 0)),
                  pl.BlockSpec((None, N, LANES), lambda p, b: (b, 0, p))],
        out_specs=(pl.BlockSpec((None, 3, T, LANES), lambda p, b: (b, 0, 0, p)),
                   pl.BlockSpec((2, 2 * NA_KH - 2, LANES), lambda p, b: (p, 0, 0))),
        scratch_shapes=[pltpu.VMEM((2, T, LANES), F32), pltpu.VMEM((2, 2 * NA_KH - 2, GRID_W, LANES), F32)],
        args=(proj, proj, proj, probs, dlat))


def _dense_core(lat_ret, lat_na, x, tgt, modl, g_post_mix, g_pre_mlp, g_post_mlp, w_out, w1, w2):
    B, N, D = x.shape
    F = w1.shape[1]
    wout_rows, w1_cols, w2_rows = w_out.shape[0] // N_DEV, w1.shape[1] // N_DEV, w2.shape[0] // N_DEV
    mixw = w_out.shape[0]
    half = mixw // 2
    tm = _div_tile(N, 256, 16)
    nt = N // tm
    fc = _div_tile(F, 1024, LANES)

    def body(lr_ref, ln_ref, x_ref, t_ref, gt1_ref, sh2_ref, sc2_ref, gt2_ref, gpm_ref, gpre_ref, gpo_ref,
             wout_part, w1_part, w2_part,
             dy1_ref, dlr_ref, dln_ref, dmix_ref, h2_ref, a_ref, du_ref, dz_ref, red_ref, wout_hbm, w1_hbm, w2_hbm,
             wout_v, w1_v, w2_v, u_s, sems, fsend, frecv):
        @pl.when((pl.program_id(0) == 0) & (pl.program_id(1) == 0))
        def _():
            relay = [(_row_block(wout_part, wout_rows), _row_block(wout_hbm, wout_rows)),
                     (_col_block(w1_part, w1_cols), _col_block(w1_hbm, w1_cols)),
                     (_row_block(w2_part, w2_rows), _row_block(w2_hbm, w2_rows))]
            _forward_start(relay, fsend, frecv)
            _forward_wait(relay, fsend, frecv)
            cps = [pltpu.make_async_copy(wout_hbm, wout_v, sems.at[0]),
                   pltpu.make_async_copy(w1_hbm, w1_v, sems.at[1]),
                   pltpu.make_async_copy(w2_hbm, w2_v, sems.at[2])]
            for cp in cps:
                cp.start()
            for cp in cps:
                cp.wait()

        @pl.when(pl.program_id(1) == 0)
        def _():
            red_ref[...] = jnp.zeros(red_ref.shape, F32)

        gt1 = gt1_ref[...]
        sh2 = sh2_ref[...]
        sc2 = sc2_ref[...]
        gt2 = gt2_ref[...]
        gpm = gpm_ref[...]
        gpre = gpre_ref[...]
        gpo = gpo_ref[...]

        def rowmean(a):
            return jnp.mean(a, axis=-1, keepdims=True)

        def colsum(a):
            return jnp.sum(a, axis=0, keepdims=True)

        mix_gain = gt1 * gpm
        mlp_in_gain = gpre * (1.0 + sc2)
        mlp_out_gain = gt2 * gpo
        mix = _dot(lr_ref[...], wout_v[0:half, :]) + _dot(ln_ref[...], wout_v[half:, :])
        x = x_ref[...]
        rm = lax.rsqrt(rowmean(mix * mix) + NORM_EPS)
        mh = mix * rm
        y1 = x + mh * mix_gain
        r1 = lax.rsqrt(rowmean(y1 * y1) + NORM_EPS)
        xh = y1 * r1
        h2b = (xh * mlp_in_gain + sh2).astype(BF16)
        h2_ref[...] = h2b
        z = jnp.zeros((tm, D), F32)
        for c0 in range(0, F, fc):
            u = _dot(h2b, w1_v[:, c0:c0 + fc])
            u_s[:, c0:c0 + fc] = u
            ru = jnp.maximum(u, 0.0)
            ab = (ru * ru).astype(BF16)
            a_ref[:, c0:c0 + fc] = ab
            z = z + _dot(ab, w2_v[c0:c0 + fc, :])
        r2 = lax.rsqrt(rowmean(z * z) + NORM_EPS)
        zh = z * r2
        y2 = y1 + zh * mlp_out_gain
        err = y2 - t_ref[...]
        loss = 0.5 * jnp.sum(rowmean(err * err))
        dy2 = err * (1.0 / D)
        s_out = colsum(dy2 * zh)
        red_ref[2:3, :] += s_out * gpo
        red_ref[6:7, :] += s_out * gt2
        dzh = dy2 * mlp_out_gain
        dz = r2 * (dzh - zh * rowmean(dzh * zh))
        dzb = dz.astype(BF16)
        dz_ref[...] = dzb
        dh2 = jnp.zeros((tm, D), F32)
        for c0 in range(0, F, fc):
            da = _dot_nt(dzb, w2_v[c0:c0 + fc, :])
            dub = (da * (2.0 * jnp.maximum(u_s[:, c0:c0 + fc], 0.0))).astype(BF16)
            du_ref[:, c0:c0 + fc] = dub
            dh2 = dh2 + _dot_nt(dub, w1_v[:, c0:c0 + fc])
        s_in = colsum(dh2 * xh)
        red_ref[3:4, :] += s_in * gpre
        red_ref[4:5, :] += colsum(dh2)
        red_ref[5:6, :] += s_in * (1.0 + sc2)
        dxh = dh2 * mlp_in_gain
        dy1 = dy2 + r1 * (dxh - xh * rowmean(dxh * xh))
        dy1_ref[...] = dy1
        s_mix = colsum(dy1 * mh)
        red_ref[0:1, :] += s_mix * gpm
        red_ref[1:2, :] += s_mix * gt1
        dmh = dy1 * mix_gain
        dmix = (rm *(dmh - mh * rowmean(dmh * mh))).astype(BF16)
        dmix_ref[...] = dmix
        dlr_ref[...] = _dot_nt(dmix, wout_v[0:half, :])
        dln_ref[...] = _dot_nt(dmix, wout_v[half:, :])
        red_ref[7:8, :] += jnp.zeros((1, D), F32) + loss

    def tok(w):
        return pl.BlockSpec((None, tm, w), lambda b, t: (b, t, 0))

    def mod(k):
        return pl.BlockSpec((None, None, 1, D), lambda b, t, k=k: (b, k, 0, 0))

    def vec():
        return pl.BlockSpec((1, D), lambda b, t: (0, 0))

    return pl.pallas_call(
        body, name="dense_core", grid=(B, nt),
        out_shape=(jax.ShapeDtypeStruct((B, N, D), F32), jax.ShapeDtypeStruct((B, N, half), F32),
                   jax.ShapeDtypeStruct((B, N, half), F32), jax.ShapeDtypeStruct((B, N, D), BF16),
                   jax.ShapeDtypeStruct((B, N, D), BF16), jax.ShapeDtypeStruct((B, N, F), BF16),
                   jax.ShapeDtypeStruct((B, N, F), BF16), jax.ShapeDtypeStruct((B, N, D), BF16),
                   jax.ShapeDtypeStruct((B, SUBLANES, D), F32),
                   jax.ShapeDtypeStruct(w_out.shape, w_out.dtype), jax.ShapeDtypeStruct(w1.shape, w1.dtype),
                   jax.ShapeDtypeStruct(w2.shape, w2.dtype)),
        in_specs=[tok(half), tok(half), tok(D), tok(D), mod(2), mod(3), mod(4), mod(5), vec(), vec(), vec(),
                  _any(), _any(), _any()],
        out_specs=(tok(D), tok(half), tok(half), tok(D), tok(D), tok(F), tok(F), tok(D),
                   pl.BlockSpec((None, SUBLANES, D), lambda b, t: (b, 0, 0)), _any(), _any(), _any()),
        scratch_shapes=[pltpu.VMEM((mixw, D), BF16), pltpu.VMEM((D, F), BF16), pltpu.VMEM((F, D), BF16),
                        pltpu.VMEM((tm, F), F32), pltpu.SemaphoreType.DMA((3,)),
                        pltpu.SemaphoreType.DMA((3, 3)), pltpu.SemaphoreType.DMA((3, 3))],
        input_output_aliases={11: 9, 12: 10, 13: 11},
        compiler_params=_params("arbitrary", "arbitrary"),
    )(lat_ret, lat_na, x, tgt, modl, modl, modl, modl, g_post_mix, g_pre_mlp, g_post_mlp, w_out, w1, w2)[:9]


def _inproj_bwd(dret, dna, x, ctx, dy1, modl, g1, w_in_t, after):
    B, N, D = x.shape
    n_ctx = ctx.shape[1]
    T = n_ctx + N
    tm = _div_tile(n_ctx, 256, 16)
    nct, ctx_spec, lat_spec = _token_tiles(n_ctx, tm)
    nt = T // tm
    nseg_r = dret.shape[1]
    nseg_n = dna.shape[1]
    nw = w_in_t.shape[0]

    def body(*refs):
        seg_refs = refs[:nseg_r + nseg_n]
        c_ref, x_ref, dy1_ref, sc_ref, g_ref, w_ref, dx_ref, red_ref = refs[nseg_r + nseg_n:]
        t = pl.program_id(1)
        dh = jnp.zeros((tm, D), F32)
        for s, ref in enumerate(seg_refs):
            dh = dh + _dot(ref[...], w_ref[s * SEG:(s + 1) * SEG, :])
        x = jnp.where(t < nct, c_ref[...], x_ref[...])
        g = g_ref[...]
        r = lax.rsqrt(jnp.mean(x * x, axis=-1, keepdims=True) + NORM_EPS)
        xh = x * r
        gain = 1.0 + sc_ref[...]
        s_in = jnp.sum(dh * xh, axis=0, keepdims=True)
        red_ref[0:1, :] = jnp.sum(dh, axis=0, keepdims=True)
        red_ref[1:2, :] = s_in * g
        red_ref[2:3, :] = s_in * gain
        red_ref[3:, :] = jnp.zeros((SUBLANES - 3, D), F32)
        dxh = dh * (g * gain)
        dx = r * (dxh - xh * jnp.mean(dxh * xh, axis=-1, keepdims=True))
        dx_ref[...] = dx + jnp.where(t >= nct, dy1_ref[...], 0.0)

    def mrow(b, t):
        return jnp.where(t < nct, B, b)

    def seg(s):
        return pl.BlockSpec((None, None, tm, SEG), lambda b, t, s=s: (b, s, t, 0))

    return _grid_call(
        body, name="inproj_bwd", grid=(B, nt),
        out_shape=(jax.ShapeDtypeStruct((B, N, D), F32), jax.ShapeDtypeStruct((B, nt, SUBLANES, D), F32)),
        in_specs=[seg(s) for s in range(nseg_r)] + [seg(s) for s in range(nseg_n)]
                 + [ctx_spec(D), lat_spec(D), lat_spec(D),
                    pl.BlockSpec((None, None, 1, D), lambda b, t: (mrow(b, t), 1, 0, 0)),
                    pl.BlockSpec((1, D), lambda b, t: (0, 0)),
                    pl.BlockSpec((nw, D), lambda b, t: (0, 0))],
        out_specs=(lat_spec(D), pl.BlockSpec((None, None, SUBLANES, D), lambda b, t: (b, t, 0, 0))),
        scratch_shapes=[], args=(*([dret] * nseg_r), *([dna] * nseg_n), ctx, x, dy1, modl, g1, w_in_t), after=after)


def _tn_matmul(lhs, rhs, name, rows_before=0, rows_after=0, into=None):
    B, S, T, W = lhs.shape
    nn = rhs.shape[-1]
    tk = _div_tile(T, 2304, LANES)
    bm = _div_tile(W, 1024, LANES)
    bn = _div_tile(nn, 1024, LANES)
    nkt = T // tk
    nk = B * nkt

    def body(l_ref, r_ref, *rest):
        o_ref, acc = rest[-2:]
        k = pl.program_id(3)

        @pl.when(k == 0)
        def _():
            acc[...] = jnp.zeros(acc.shape, F32)

        acc[...] += _dot_tn(l_ref[...].astype(BF16), r_ref[...].astype(BF16))

        @pl.when(k == nk - 1)
        def _():
            o_ref[...] = acc[...].astype(BF16)

    nwb = W // bm
    first = rows_before // bm
    return pl.pallas_call(
        functools.partial(body), name=name, grid=(S, nwb, nn // bn, nk),
        out_shape=jax.ShapeDtypeStruct((rows_before + S * W + rows_after, nn), BF16),
        in_specs=[pl.BlockSpec((None, None, tk, bm), lambda s, i, j, k: (k // nkt, s, k % nkt, i)),
                  pl.BlockSpec((None, tk, bn), lambda s, i, j, k: (k // nkt, k % nkt, j))]
                 + ([] if into is None else [_any()]),
        out_specs=pl.BlockSpec((bm, bn), lambda s, i, j, k: (first + s * nwb + i, j)),
        scratch_shapes=[pltpu.VMEM((bm, bn), F32)],
        input_output_aliases={} if into is None else {2: 0},
        compiler_params=_params("parallel", "parallel", "parallel", "arbitrary"),
    )(lhs, rhs, *([] if into is None else [into]))


class _SplitScatter:
    def __init__(self, gs, block_ofs, land_shapes, name, kind="scatter", masks=ALL_PEERS):
        self.n = n = len(gs)
        self.block_ofs, self.kind, self.masks = block_ofs, kind, masks
        if kind == "scatter":
            land_shapes = [(N_DEV,) + tuple(bs) for bs in land_shapes]
        hbm = pl.BlockSpec(memory_space=pltpu.HBM)
        sem = pl.BlockSpec(memory_space=pltpu.SEMAPHORE)

        def body(*refs):
            g_refs, land_refs = refs[:n], refs[n:2 * n]
            send_sems, recv_sems, own_sems = refs[2 * n:2 * n + 3]
            token = refs[-1]
            for own, pushes in self._copies(g_refs, land_refs, send_sems, recv_sems, own_sems, landing="sender"):
                own.start()
                for cp in pushes:
                    cp.start()
            token[...] = jnp.zeros_like(token)

        outs = pl.pallas_call(
            body, name=name,
            out_shape=(pltpu.SemaphoreType.DMA((n * (N_DEV - 1),)), pltpu.SemaphoreType.DMA((n * (N_DEV - 1),)),
                       pltpu.SemaphoreType.DMA((n,)))
                      + tuple(pltpu.HBM(g.shape, g.dtype) for g in gs)
                      + tuple(pltpu.HBM(s, g.dtype) for s, g in zip(land_shapes, gs))
                      + (jax.ShapeDtypeStruct((SUBLANES, LANES), F32),),
            in_specs=(hbm,) * (2 * n), out_specs=(sem,) * 3 + (hbm,) * (2 * n) + (_vmem(),),
            input_output_aliases={k: 3 + k for k in range(2 * n)},
            compiler_params=pltpu.CompilerParams(has_side_effects=pltpu.SideEffectType.DATAFLOW_SIDE_EFFECTING),
        )(*[pltpu.with_memory_space_constraint(g, pltpu.HBM) for g in gs],
          *[pltpu.with_memory_space_constraint(lax.empty(s, g.dtype), pltpu.HBM) for s, g in zip(land_shapes, gs)])
        self.sems, self.thru, self.token = outs[:3], outs[3:3 + 2 * n], outs[-1]

    def _copies(self, g_refs, land_refs, send_sems, recv_sems, own_sems, landing):
        me, peers = _me_and_peers()
        out = []
        for k in range(self.n):
            if self.kind == "scatter":
                src, dst = self.block_ofs[k](g_refs[k]), _slot(land_refs[k])
            else:
                src, dst = (lambda p, k=k: g_refs[k]), self.block_ofs[k](land_refs[k])
            own = pltpu.make_async_copy(src(me), dst(me), own_sems.at[k]) if landing == "sender" else None
            pushes = []
            for m in self.masks:
                dev, pid = peers[m - 1]
                i = k * (N_DEV - 1) + m - 1
                pushes.append(_remote(src(pid), dst(me if landing == "sender" else pid),
                                      send_sems.at[i], recv_sems.at[i], dev))
            out.append((own, pushes))
        return out


def _scatter_wait(scatters, after, name):
    hbm = pl.BlockSpec(memory_space=pltpu.HBM)
    sem = pl.BlockSpec(memory_space=pltpu.SEMAPHORE)
    n_arr = [2 * sc.n for sc in scatters]
    total = sum(n_arr)

    def body(*refs):
        arrs, sems = refs[:total], refs[total:total + 3 * len(scatters)]
        a0 = 0
        for j, sc in enumerate(scatters):
            g_refs, land_refs = arrs[a0:a0 + sc.n], arrs[a0 + sc.n:a0 + 2 * sc.n]
            a0 += 2 * sc.n
            send_sems, recv_sems, own_sems = sems[3 * j:3 * j + 3]
            for (own, sent), (_, got) in zip(sc._copies(g_refs, land_refs, send_sems, recv_sems, own_sems, "sender"),
                                             sc._copies(g_refs, land_refs, send_sems, recv_sems, own_sems, "receiver")):
                own.wait()
                for cp in sent:
                    cp.wait_send()
                for cp in got:
                    cp.wait_recv()

    operands = [a for sc in scatters for a in sc.thru]
    outs = pl.pallas_call(
        body, name=name,
        out_shape=tuple(pltpu.HBM(a.shape, a.dtype) for a in operands),
        in_specs=(hbm,) * total + (sem,) * (3 * len(scatters)) + (pl.BlockSpec(memory_space=pl.ANY),),
        out_specs=(hbm,) * total, input_output_aliases={k: k for k in range(total)},
        compiler_params=pltpu.CompilerParams(has_side_effects=pltpu.SideEffectType.DATAFLOW_SIDE_EFFECTING),
    )(*operands, *[s for sc in scatters for s in sc.sems], after)
    lands, a0 = [], 0
    for sc in scatters:
        lands.extend(outs[a0 + sc.n:a0 + 2 * sc.n])
        a0 += 2 * sc.n
    return lands


def _small_ar(mbuf, silu_all, w_ada, c_ctx, n_mod_rows, n_vec_rows):
    D = silu_all.shape[1]
    ncol = w_ada.shape[1]
    nm = mbuf.shape[2]
    srows = silu_all.shape[0]

    def body(mbuf, s_ref, w_ref, cc_ref, tot_ref, gb_ref, gw_ref, gc_ref, tbuf, dmx, cmrow, send3, recv3):
        me, _ = _me_and_peers()
        msum = mbuf[0]
        for k in range(1, N_DEV):
            msum = msum + mbuf[k]
        tot_ref[...] = msum[n_mod_rows:n_mod_rows + n_vec_rows]
        gb_ref[...] = jnp.sum(msum[0:n_mod_rows], axis=0, keepdims=True)
        loc = pl.ds(pl.multiple_of(me * ncol, ncol), ncol)
        for k in range(N_DEV):
            dmx[k * SUBLANES:(k + 1) * SUBLANES, :] = mbuf[k, :, loc]
        cmrow[...] = msum
        cm_loc = cmrow[n_mod_rows - 1:n_mod_rows, loc]
        dmx[N_DEV * SUBLANES:, :] = jnp.concatenate([cm_loc, jnp.zeros((SUBLANES - 1, ncol), F32)], axis=0)
        gw_ref[...] = _dot_tn(s_ref[...], dmx[...])
        tbuf[me] = _dot_nt(dmx[N_DEV * SUBLANES:, :], w_ref[...])
        _exchange(lambda p: tbuf.at[me], lambda p: tbuf.at[p], send3, recv3)
        tsum = tbuf[0]
        for k in range(1, N_DEV):
            tsum = tsum + tbuf[k]
        cc = cc_ref[...]
        sg = _sigmoid(cc)
        gc_ref[...] = tsum[0:1, :] * (sg * (1.0 + cc * (1.0 - sg)))

    return pl.pallas_call(
        body, name="small_ar",
        out_shape=(jax.ShapeDtypeStruct((n_vec_rows, nm), F32), jax.ShapeDtypeStruct((1, nm), F32),
                   jax.ShapeDtypeStruct((D, ncol), F32), jax.ShapeDtypeStruct((1, D), F32)),
        in_specs=[_vmem()] * 4, out_specs=(_vmem(),) * 4,
        scratch_shapes=[pltpu.VMEM((N_DEV, SUBLANES, D), F32), pltpu.VMEM((srows, ncol), F32),
                        pltpu.VMEM((SUBLANES, nm), F32)] + [pltpu.SemaphoreType.DMA((N_DEV - 1,))] * 2,
        compiler_params=pltpu.CompilerParams(vmem_limit_bytes=VMEM_LIMIT),
    )(mbuf, silu_all, w_ada, c_ctx.reshape(1, D))


def _adam_update(w, g, m, v):
    mn = ADAM_B1 * m + (1.0 - ADAM_B1) * g
    vn = ADAM_B2 * v + (1.0 - ADAM_B2) * (g * g)
    m_hat = mn / (1.0 - ADAM_B1 ** ADAM_STEP)
    v_hat = vn / (1.0 - ADAM_B2 ** ADAM_STEP)
    return -ADAM_LR * (m_hat / (jnp.sqrt(v_hat) + ADAM_EPS) + ADAM_WD * w), mn, vn


def _adamw(w, g, m, v, name):
    rows, cols = w.shape
    tr = _div_tile(rows, 512, SUBLANES)

    def body(w_ref, g_ref, m_ref, v_ref, d_ref, nm_ref, nv_ref):
        d_ref[...], nm_ref[...], nv_ref[...] = _adam_update(w_ref[...], g_ref[...], m_ref[...], v_ref[...])

    spec = pl.BlockSpec((tr, cols), lambda i: (i, 0))
    return pl.pallas_call(
        functools.partial(body), name=name, grid=(rows // tr,),
        out_shape=(jax.ShapeDtypeStruct((rows, cols), F32),) * 3,
        in_specs=[spec] * 4, out_specs=(spec,) * 3,
        compiler_params=_params("parallel"),
    )(w, g, m, v)


def _adamw_small(items, name):
    n = len(items)

    def body(*refs):
        ins, outs = refs[:4 * n], refs[4 * n:]
        for i in range(n):
            w_ref, g_ref, m_ref, v_ref = ins[4 * i:4 * i + 4]
            outs[3 * i][...], outs[3 * i + 1][...], outs[3 * i + 2][...] = _adam_update(
                w_ref[...], g_ref[...], m_ref[...], v_ref[...])

    outs = pl.pallas_call(
        body, name=name,
        out_shape=tuple(jax.ShapeDtypeStruct(it[0].shape, F32) for it in items for _ in range(3)),
        in_specs=[_vmem()] * (4 * n), out_specs=(_vmem(),) * (3 * n),
        compiler_params=pltpu.CompilerParams(vmem_limit_bytes=VMEM_LIMIT),
    )(*[a for it in items for a in it])
    return [tuple(outs[3 * i:3 * i + 3]) for i in range(n)]


def _sum_adamw(buf, w, m, v, name):
    _, rows, cols = buf.shape
    tr = _div_tile(rows, 256, 2 * SUBLANES)

    def body(b_ref, w_ref, m_ref, v_ref, g_ref, d_ref, nm_ref, nv_ref):
        g = b_ref[0].astype(F32)
        for k in range(1, N_DEV):
            g = g + b_ref[k].astype(F32)
        g_ref[...] = g
        d_ref[...], nm_ref[...], nv_ref[...] = _adam_update(w_ref[...], g, m_ref[...], v_ref[...])

    spec = pl.BlockSpec((tr, cols), lambda i: (i, 0))
    return pl.pallas_call(
        functools.partial(body), name=name, grid=(rows // tr,),
        out_shape=(jax.ShapeDtypeStruct((rows, cols), F32),) * 4,
        in_specs=[pl.BlockSpec((N_DEV, tr, cols), lambda i: (0, i, 0))] + [spec] * 3, out_specs=(spec,) * 4,
        compiler_params=_params("parallel"),
    )(buf, w, m, v)


def _rope_tables(n_ctx, n):
    n_freq = RET_DIM // 4
    inv = np.float32(ROPE_BASE) ** (-np.arange(n_freq, dtype=np.float32) / np.float32(n_freq))
    tok = np.arange(n)
    pos_r = (tok // GRID_W).astype(np.float32)
    pos_c = (tok % GRID_W).astype(np.float32)
    ang_r = (pos_r[:, None] * inv[None, :]).astype(np.float32)
    ang_c = (pos_c[:, None] * inv[None, :]).astype(np.float32)
    cos = np.concatenate([np.cos(ang_r), np.cos(ang_r), np.cos(ang_c), np.cos(ang_c)], axis=-1)
    sin = np.concatenate([-np.sin(ang_r), np.sin(ang_r), -np.sin(ang_c), np.sin(ang_c)], axis=-1)
    cos = np.concatenate([np.ones((n_ctx, RET_DIM), np.float32), cos], axis=0)
    sin = np.concatenate([np.zeros((n_ctx, RET_DIM), np.float32), sin], axis=0)
    return jnp.asarray(cos, F32), jnp.asarray(sin, F32)


def _na_tables():
    q = np.arange(GRID_W)[:, None]
    k = np.arange(GRID_W)[None, :]
    start = np.clip(q - NA_KW // 2, 0, GRID_W - NA_KW)
    valid = (k >= start) & (k < start + NA_KW)
    dc = np.clip(k - q + (NA_KW - 1), 0, 2 * NA_KW - 2)
    ncls = 2 * NA_KW - 1
    onehot = (dc[None] == np.arange(ncls)[:, None, None]) & valid[None]
    return onehot.astype(np.float32), valid


def _paired_bias(rpb, onehot, valid):
    ncls = onehot.shape[0]
    pair = np.zeros((2 * ncls, GRID_W, LANES), np.float32)
    pair[:ncls, :, :GRID_W] = onehot
    pair[ncls:, :, GRID_W:] = onehot
    rows = jnp.concatenate([rpb[:, :-1], rpb[:, 1:]], axis=-1)
    t = jnp.einsum("hdc,cqk->hdqk", rows, jnp.asarray(pair), precision=lax.Precision.HIGHEST)
    return jnp.where(jnp.asarray(np.tile(valid, (1, 2)))[None, None], t, NEG_INF)


def kernel(x, c, ctx, c_ctx, w_ada, b_ada, g_pre_mix, g_post_mix, g_pre_mlp, g_post_mlp, w_in, ret_decay, ret_gn, na_rpb, w_out, w_mlp1, w_mlp2, loss_target, m_c_ctx, m_w_ada, m_b_ada, m_g_pre_mix, m_g_post_mix, m_g_pre_mlp, m_g_post_mlp, m_w_in, m_ret_decay, m_ret_gn, m_na_rpb, m_w_out, m_w_mlp1, m_w_mlp2, v_c_ctx, v_w_ada, v_b_ada, v_g_pre_mix, v_g_post_mix, v_g_pre_mlp, v_g_post_mlp, v_w_in, v_ret_decay, v_ret_gn, v_na_rpb, v_w_out, v_w_mlp1, v_w_mlp2):
    B, N, D = x.shape
    C = ctx.shape[1]
    T = C + N

    silu_all, mods_g, win_b, wout_l, w1_l, w2_l = _mod_gather(c, c_ctx, w_ada[0], b_ada, w_in[0].T, w_out[0],
                                                             w_mlp1[0], w_mlp2[0])
    mods_mine = mods_g.transpose(1, 0, 2).reshape(mods_g.shape[1], N_MOD * D)
    modl = jnp.concatenate([mods_mine[:B], mods_mine[SUBLANES:SUBLANES + 1]], axis=0)
    modl = modl.reshape(B + 1, N_MOD, 1, D)
    rin = w_in.shape[2]
    rout, c1, r2 = wout_l.shape[0], w1_l.shape[1], w2_l.shape[0]

    def rows_of(n):
        return lambda ref: _row_block(ref, n)

    def cols_of(n):
        return lambda ref: _col_block(ref, n)

    cos, sin = _rope_tables(C, N)
    onehot, valid = _na_tables()
    bias2 = _paired_bias(na_rpb[0], onehot, valid)
    lg = jax.nn.log_sigmoid(ret_decay[0].astype(F32))

    ag = _SplitScatter([wout_l, w1_l, w2_l], [rows_of(rout), cols_of(c1), rows_of(r2)],
                       [(N_DEV * rout, D), (D, N_DEV * c1), (N_DEV * r2, D)], "ag_mlp_start",
                       kind="gather", masks=SIBLING + ICI_SAME_CORE)
    h_all, proj = _inproj_fwd(x, ctx, modl, g_pre_mix, win_b, after=ag.token)
    o_ret, lat_ret, q_rot, k_rot = _ret_fwd(proj, cos, sin, lg, ret_gn, C)
    lat_na, na_probs = _na_fwd(proj, bias2, C)
    wout_part, w1_part, w2_part = _scatter_wait([ag], lat_na, "ag_mlp_wait")

    (dy1, dlat_ret, dlat_na, dmix, h2, act, du, dz, red_d) = _dense_core(
        lat_ret, lat_na, x, loss_target, modl, g_post_mix, g_pre_mlp, g_post_mlp, wout_part, w1_part, w2_part)

    gw_out_p = _tn_matmul(lat_ret[:, None], dmix, "gw_out_ret", rows_after=lat_na.shape[-1])
    gw_out_p = _tn_matmul(lat_na[:, None], dmix, "gw_out_na", rows_before=lat_ret.shape[-1], into=gw_out_p)
    gw1_p = _tn_matmul(h2[:, None], du, "gw_mlp1")
    gw2_p = _tn_matmul(act[:, None], dz, "gw_mlp2")
    rs_mlp = _SplitScatter([gw_out_p, gw1_p, gw2_p], [rows_of(rout), cols_of(c1), rows_of(r2)],
                           [(rout, D), (D, c1), (r2, D)], "rs_mlp_start")

    dret, dgn_p, dlg_p = _ret_bwd(proj, q_rot, k_rot, cos, sin, lg, ret_gn, o_ret, dlat_ret, C, after=rs_mlp.token)
    dna, rr = _na_bwd(proj, na_probs, dlat_na, C)
    ret_cols, na_cols = dret.shape[1] * dret.shape[3], dna.shape[1] * dna.shape[3]
    gwin_t_p = _tn_matmul(dret, h_all, "gw_in_ret", rows_after=na_cols)
    gwin_t_p = _tn_matmul(dna, h_all, "gw_in_na", rows_before=ret_cols, into=gwin_t_p)
    rs_in = _SplitScatter([gwin_t_p], [rows_of(rin)], [(rin, D)], "rs_w_in_start")
    grad_x, red_i = _inproj_bwd(dret, dna, x, ctx, dy1, modl, g_pre_mix, win_b, after=rs_in.token)

    rd = red_d
    nct = red_i.shape[1] * C // T
    ri_ctx = red_i[:, :nct].sum(axis=(0, 1))
    ri_lat = red_i[:, nct:].sum(axis=1)
    d_mods = jnp.concatenate([ri_lat[:, 0], ri_lat[:, 1], rd[:, 0], rd[:, 4], rd[:, 3], rd[:, 2]], axis=-1)
    d_cmods = jnp.concatenate([ri_ctx[0], ri_ctx[1], jnp.zeros(((N_MOD - 2) * D,), F32)])[None]
    dg_pre_mix = ri_lat[:, 2].sum(axis=0) + ri_ctx[2]
    dg_post_mix = rd[:, 1].sum(axis=0)
    dg_pre_mlp = rd[:, 5].sum(axis=0)
    dg_post_mlp = rd[:, 6].sum(axis=0)
    loss_p = rd[:, 7, 0].sum()
    d_gn = dgn_p[:, 0].sum(axis=0)
    d_lg = dlg_p[:, :, :2, 0].sum(axis=0).T
    d_decay = d_lg * jax.nn.sigmoid(-ret_decay[0].astype(F32))
    ncls = 2 * NA_KW - 1
    d_rpb = (jnp.pad(rr[:, :, :ncls], ((0, 0), (0, 1), (0, 0)))
             + jnp.pad(rr[:, :, GRID_W:GRID_W + ncls], ((0, 0), (1, 0), (0, 0))))
    d_rpb32 = jnp.pad(d_rpb, ((0, 0), (0, 0), (0, 32 - ncls)))
    pieces = [dg_pre_mix, dg_post_mix, dg_pre_mlp, dg_post_mlp, d_gn, d_rpb32.reshape(-1),
              jnp.pad(d_decay.reshape(-1), (0, LANES - d_decay.size)), jnp.full((LANES,), loss_p, F32)]
    vec = jnp.concatenate(pieces)
    nm = N_MOD * D
    n_vec_rows = -(-vec.shape[0] // nm)
    assert B + 1 + n_vec_rows <= SUBLANES
    vec = jnp.pad(vec, (0, n_vec_rows * nm - vec.shape[0])).reshape(n_vec_rows, nm)
    dm_slot = jnp.concatenate([d_mods, d_cmods, vec, jnp.zeros((SUBLANES - B - 1 - n_vec_rows, nm), F32)], axis=0)
    def whole(ref):
        return lambda p: ref

    small = _SplitScatter([dm_slot], [whole], [dm_slot.shape], "small_start")
    land_out, land_1, land_2, land_in = _scatter_wait([rs_mlp, rs_in], small.token, "rs_wait")
    fused = {"w_in": [a.T for a in _sum_adamw(land_in, w_in[0].T, m_w_in[0].T, v_w_in[0].T, "sum_adamw_w_in")],
             "w_out": _sum_adamw(land_out, w_out[0], m_w_out[0], v_w_out[0], "sum_adamw_w_out"),
             "w_mlp1": _sum_adamw(land_1, w_mlp1[0], m_w_mlp1[0], v_w_mlp1[0], "sum_adamw_w_mlp1"),
             "w_mlp2": _sum_adamw(land_2, w_mlp2[0], m_w_mlp2[0], v_w_mlp2[0], "sum_adamw_w_mlp2")}
    (mbuf,) = _scatter_wait([small], fused["w_mlp2"][0], "small_wait")
    tot, g_b_ada, g_w_ada, g_c_ctx = _small_ar(mbuf, silu_all, w_ada[0], c_ctx, B + 1, n_vec_rows)
    flat = tot.reshape(-1)
    o0 = 0
    g_pre_mix_g = flat[o0:o0 + D]; o0 += D
    g_post_mix_g = flat[o0:o0 + D]; o0 += D
    g_pre_mlp_g = flat[o0:o0 + D]; o0 += D
    g_post_mlp_g = flat[o0:o0 + D]; o0 += D
    g_gn = flat[o0:o0 + RET_WIDTH]; o0 += RET_WIDTH
    nrpb = NA_HEADS * (2 * NA_KH - 1) * 32
    g_rpb = flat[o0:o0 + nrpb].reshape(NA_HEADS, 2 * NA_KH - 1, 32)[:, :, :ncls]; o0 += nrpb
    g_decay = flat[o0:o0 + 2 * RET_HEADS].reshape(2, RET_HEADS); o0 += LANES
    loss = flat[o0]

    grads = {
        "c_ctx": g_c_ctx.reshape(c_ctx.shape), "w_ada": g_w_ada[None], "b_ada": g_b_ada.reshape(b_ada.shape),
        "g_pre_mix": g_pre_mix_g[None], "g_post_mix": g_post_mix_g[None], "g_pre_mlp": g_pre_mlp_g[None],
        "g_post_mlp": g_post_mlp_g[None], "w_in": fused["w_in"][0][None], "ret_decay": g_decay[None], "ret_gn": g_gn[None],
        "na_rpb": g_rpb[None], "w_out": fused["w_out"][0][None], "w_mlp1": fused["w_mlp1"][0][None],
        "w_mlp2": fused["w_mlp2"][0][None],
    }
    weights = dict(c_ctx=c_ctx, w_ada=w_ada, b_ada=b_ada, g_pre_mix=g_pre_mix, g_post_mix=g_post_mix,
                   g_pre_mlp=g_pre_mlp, g_post_mlp=g_post_mlp, w_in=w_in, ret_decay=ret_decay, ret_gn=ret_gn,
                   na_rpb=na_rpb, w_out=w_out, w_mlp1=w_mlp1, w_mlp2=w_mlp2)
    m_in = dict(c_ctx=m_c_ctx, w_ada=m_w_ada, b_ada=m_b_ada, g_pre_mix=m_g_pre_mix, g_post_mix=m_g_post_mix,
                g_pre_mlp=m_g_pre_mlp, g_post_mlp=m_g_post_mlp, w_in=m_w_in, ret_decay=m_ret_decay,
                ret_gn=m_ret_gn, na_rpb=m_na_rpb, w_out=m_w_out, w_mlp1=m_w_mlp1, w_mlp2=m_w_mlp2)
    v_in = dict(c_ctx=v_c_ctx, w_ada=v_w_ada, b_ada=v_b_ada, g_pre_mix=v_g_pre_mix, g_post_mix=v_g_post_mix,
                g_pre_mlp=v_g_pre_mlp, g_post_mlp=v_g_post_mlp, w_in=v_w_in, ret_decay=v_ret_decay,
                ret_gn=v_ret_gn, na_rpb=v_na_rpb, w_out=v_w_out, w_mlp1=v_w_mlp1, w_mlp2=v_w_mlp2)
    names = list(weights)
    deltas, new_m, new_v = {}, {}, {}
    def as_2d(n):
        shp = weights[n].shape
        two_d = (-1, shp[-1]) if len(shp) > 1 else (1, shp[0])
        return [a.reshape(two_d) for a in (weights[n], grads[n], m_in[n], v_in[n])]

    small = [n for n in names if n not in fused and weights[n].size <= 65536]
    updated = dict(zip(small, _adamw_small([as_2d(n) for n in small], "adamw_small")))
    for n in names:
        if n in fused:
            updated[n] = fused[n][1:]
        elif n not in updated:
            updated[n] = _adamw(*as_2d(n), "adamw_" + n)
        deltas[n], new_m[n], new_v[n] = (a.reshape(weights[n].shape) for a in updated[n])
    return (loss, grad_x, *[grads[n] for n in names], *[deltas[n] for n in names],
            *[new_m[n] for n in names], *[new_v[n] for n in names])
```

```python
import functools

import numpy as np
import jax
import jax.numpy as jnp
from jax import lax
from jax.experimental import pallas as pl
from jax.experimental.pallas import tpu as pltpu

F32 = jnp.float32
BF16 = jnp.bfloat16
MESH = pl.DeviceIdType.MESH

N_DEV = 8
LANES = 128
SUBLANES = 8
VMEM_LIMIT = 60 * 1024 * 1024

GRID_W = 64
RET_HEADS = 4
RET_DIM = 128
RET_WIDTH = RET_HEADS * RET_DIM
NA_HEADS = 8
NA_DIM = 64
NA_WIDTH = NA_HEADS * NA_DIM
NA_PAIRS = NA_HEADS // 2
NA_KH = 8
NA_KW = 16
NA_GROUP = 8
SEG = 512
ROPE_BASE = 10000.0
NORM_EPS = 1e-6
NEG_INF = -1e30
N_MOD = 6

ADAM_LR = 0.001
ADAM_B1 = 0.9
ADAM_B2 = 0.999
ADAM_EPS = 1e-08
ADAM_WD = 0.01
ADAM_STEP = 10


def _dot(a, b):
    return lax.dot_general(a, b, (((1,), (0,)), ((), ())), preferred_element_type=F32)


def _dot_nt(a, b):
    return lax.dot_general(a, b, (((1,), (1,)), ((), ())), preferred_element_type=F32)


def _dot_tn(a, b):
    return lax.dot_general(a, b, (((0,), (0,)), ((), ())), preferred_element_type=F32)


def _sigmoid(x):
    return 1.0 / (1.0 + jnp.exp(-x))


def _div_tile(n, cap, mult):
    if n <= cap:
        return n
    for t in range(cap - cap % mult, 0, -mult):
        if n % t == 0:
            return t
    raise ValueError(f"no tile for {n}")


def _params(*sem):
    return pltpu.CompilerParams(dimension_semantics=tuple(sem) if sem else None,
                                vmem_limit_bytes=VMEM_LIMIT)


def _vmem():
    return pl.BlockSpec(memory_space=pltpu.VMEM)


def _any():
    return pl.BlockSpec(memory_space=pl.ANY)


def _me_and_peers():
    x, y, c = lax.axis_index("x"), lax.axis_index("y"), lax.axis_index("c")
    me = 4 * x + 2 * y + c
    peers = []
    for m in range(1, N_DEV):
        px = 1 - x if (m >> 2) & 1 else x
        py = 1 - y if (m >> 1) & 1 else y
        pc = 1 - c if m & 1 else c
        peers.append(((px, py, pc), 4 * px + 2 * py + pc))
    return me, peers


def _exchange(src_for, dst_from, send_sems, recv_sems):
    me, peers = _me_and_peers()
    sent = []
    for i, (dev, pid) in enumerate(peers):
        cp = pltpu.make_async_remote_copy(src_ref=src_for(pid), dst_ref=dst_from(me),
                                          send_sem=send_sems.at[i], recv_sem=recv_sems.at[i],
                                          device_id=dev, device_id_type=MESH)
        cp.start()
        sent.append(cp)
    for i, (dev, pid) in enumerate(peers):
        pltpu.make_async_remote_copy(src_ref=src_for(pid), dst_ref=dst_from(pid),
                                     send_sem=send_sems.at[i], recv_sem=recv_sems.at[i],
                                     device_id=dev, device_id_type=MESH).wait_recv()
    for cp in sent:
        cp.wait_send()


SIBLING = (1,)
ICI_SAME_CORE = (2, 4, 6)
ALL_PEERS = tuple(range(1, N_DEV))


def _remote(src, dst, send_sem, recv_sem, dev):
    return pltpu.make_async_remote_copy(src_ref=src, dst_ref=dst, send_sem=send_sem, recv_sem=recv_sem,
                                        device_id=dev, device_id_type=MESH)


def _push_start(items, masks, send_sems, recv_sems):
    me, peers = _me_and_peers()
    for k, (src_for, dst_from) in enumerate(items):
        for m in masks:
            dev, pid = peers[m - 1]
            _remote(src_for(pid), dst_from(me), send_sems.at[k, m - 1], recv_sems.at[k, m - 1], dev).start()


def _push_wait_recv(items, masks, send_sems, recv_sems):
    me, peers = _me_and_peers()
    for k, (src_for, dst_from) in enumerate(items):
        for m in masks:
            dev, pid = peers[m - 1]
            _remote(src_for(pid), dst_from(pid), send_sems.at[k, m - 1], recv_sems.at[k, m - 1], dev).wait_recv()


def _push_wait_send(items, masks, send_sems, recv_sems):
    me, peers = _me_and_peers()
    for k, (src_for, dst_from) in enumerate(items):
        for m in masks:
            dev, pid = peers[m - 1]
            _remote(src_for(pid), dst_from(me), send_sems.at[k, m - 1], recv_sems.at[k, m - 1], dev).wait_send()


def _forward_start(items, send_sems, recv_sems):
    me, peers = _me_and_peers()
    sib = peers[0][0]
    for k, (blk_in, blk_out) in enumerate(items):
        for j, m in enumerate(ICI_SAME_CORE):
            pid = peers[m - 1][1]
            _remote(blk_in(pid), blk_out(pid), send_sems.at[k, j], recv_sems.at[k, j], sib).start()


def _forward_wait(items, send_sems, recv_sems):
    me, peers = _me_and_peers()
    sib = peers[0][0]
    for k, (blk_in, blk_out) in enumerate(items):
        for j, m in enumerate(ICI_SAME_CORE):
            got = peers[(m | 1) - 1][1]
            _remote(blk_in(got), blk_out(got), send_sems.at[k, j], recv_sems.at[k, j], sib).wait_recv()
    for k, (blk_in, blk_out) in enumerate(items):
        for j, m in enumerate(ICI_SAME_CORE):
            pid = peers[m - 1][1]
            _remote(blk_in(pid), blk_out(pid), send_sems.at[k, j], recv_sems.at[k, j], sib).wait_send()


def _mod_gather(c, c_ctx, w_ada, b_ada, w_in_t, w_out, w1, w2):
    B, D = c.shape
    ncol = w_ada.shape[1]
    rows = SUBLANES * N_DEV + SUBLANES

    def body(c_ref, cc_ref, w_ref, b_ref, win_ref, wout_ref, w1_ref, w2_ref,
             s_ref, m_ref, gin_ref, wout_b, w1_b, w2_b,
             win_b, msend, send1, recv1, send2, recv2, wsend, wrecv, fsend, frecv, lsem):
        me, _ = _me_and_peers()
        win_b[...] = win_ref[...].astype(BF16)
        block = _row_block(gin_ref, w_in_t.shape[0])
        gather = [(lambda p: win_b, block)]
        own = pltpu.make_async_copy(win_b, block(me), lsem.at[0])
        cv = c_ref[...]
        slot = jnp.concatenate([cv * _sigmoid(cv), jnp.zeros((SUBLANES - B, D), F32)], axis=0)
        my_rows = pl.ds(pl.multiple_of(me * SUBLANES, SUBLANES), SUBLANES)
        s_ref[my_rows, :] = slot
        ccv = cc_ref[...]
        s_ref[SUBLANES * N_DEV:, :] = jnp.concatenate(
            [ccv * _sigmoid(ccv), jnp.zeros((SUBLANES - 1, D), F32)], axis=0)

        def rows_of(p):
            return s_ref.at[pl.ds(pl.multiple_of(p * SUBLANES, SUBLANES), SUBLANES), :]

        _exchange(lambda p: rows_of(me), rows_of, send1, recv1)
        own.start()
        _push_start(gather, SIBLING + ICI_SAME_CORE, wsend, wrecv)
        wout_b[...] = wout_ref[...].astype(BF16)
        w1_b[...] = w1_ref[...].astype(BF16)
        w2_b[...] = w2_ref[...].astype(BF16)
        b_loc = b_ref[:, pl.ds(pl.multiple_of(me * ncol, ncol), ncol)]
        mods = _dot(s_ref[...], w_ref[...]) + b_loc
        for p in range(N_DEV):
            msend[p] = jnp.concatenate([mods[p * SUBLANES:(p + 1) * SUBLANES], mods[N_DEV * SUBLANES:]], axis=0)
        m_ref[me] = msend[me]
        columns = [(lambda p: msend.at[p], lambda p: m_ref.at[p])]
        _push_start(columns, ALL_PEERS, send2, recv2)
        _push_wait_recv(gather, ICI_SAME_CORE, wsend, wrecv)
        relay = [(block, block)]
        _forward_start(relay, fsend, frecv)
        _push_wait_recv(columns, ALL_PEERS, send2, recv2)
        _push_wait_recv(gather, SIBLING, wsend, wrecv)
        _forward_wait(relay, fsend, frecv)
        _push_wait_send(columns, ALL_PEERS, send2, recv2)
        _push_wait_send(gather, SIBLING + ICI_SAME_CORE, wsend, wrecv)
        own.wait()

    return pl.pallas_call(
        body, name="mod_gather",
        out_shape=(jax.ShapeDtypeStruct((rows, D), F32), jax.ShapeDtypeStruct((N_DEV, 2 * SUBLANES, ncol), F32),
                   jax.ShapeDtypeStruct((N_DEV * w_in_t.shape[0], D), BF16),
                   jax.ShapeDtypeStruct(w_out.shape, BF16), jax.ShapeDtypeStruct(w1.shape, BF16),
                   jax.ShapeDtypeStruct(w2.shape, BF16)),
        in_specs=[_vmem()] * 8, out_specs=(_vmem(), _vmem(), _any(), _vmem(), _vmem(), _vmem()),
        scratch_shapes=[pltpu.VMEM(w_in_t.shape, BF16), pltpu.VMEM((N_DEV, 2 * SUBLANES, ncol), F32)]
                       + [pltpu.SemaphoreType.DMA((N_DEV - 1,))] * 2
                       + [pltpu.SemaphoreType.DMA((1, N_DEV - 1))] * 4 + [pltpu.SemaphoreType.DMA((1, 3))] * 2
                       + [pltpu.SemaphoreType.DMA((1,))],
        compiler_params=pltpu.CompilerParams(vmem_limit_bytes=VMEM_LIMIT),
    )(c, c_ctx.reshape(1, D), w_ada, b_ada, w_in_t, w_out, w1, w2)


def _row_block(ref, rows):
    return lambda p: ref.at[pl.ds(pl.multiple_of(p * rows, 2 * SUBLANES), rows), :]


def _col_block(ref, cols):
    return lambda p: ref.at[:, pl.ds(pl.multiple_of(p * cols, LANES), cols)]


def _slot(ref):
    return lambda p: ref.at[p]


def _grid_call(body, *, name, grid, out_shape, in_specs, out_specs, scratch_shapes, args, after=None):
    n_in = len(args)

    def ordered_body(*refs):
        body(*refs[:n_in], *refs[n_in + 1:])

    return pl.pallas_call(
        body if after is None else ordered_body, name=name, grid=grid, out_shape=tuple(out_shape),
        in_specs=list(in_specs) + ([] if after is None else [_any()]), out_specs=tuple(out_specs),
        scratch_shapes=list(scratch_shapes), compiler_params=_params(*(("arbitrary",) * len(grid))),
    )(*args, *([] if after is None else [after]))


def _token_tiles(n_ctx, tm):
    nct = n_ctx // tm

    def ctx_spec(D):
        return pl.BlockSpec((None, tm, D), lambda b, t: (b, jnp.minimum(t, nct - 1), 0))

    def lat_spec(D):
        return pl.BlockSpec((None, tm, D), lambda b, t: (b, jnp.maximum(t - nct, 0), 0))

    return nct, ctx_spec, lat_spec


def _inproj_fwd(x, ctx, modl, g1, w_in_t, after):
    B, N, D = x.shape
    n_ctx = ctx.shape[1]
    T = n_ctx + N
    nw = w_in_t.shape[0]
    tm = _div_tile(n_ctx, 256, 16)
    nct, ctx_spec, lat_spec = _token_tiles(n_ctx, tm)

    def body(c_ref, x_ref, sh_ref, sc_ref, g_ref, w_ref, h_ref, p_ref):
        x = jnp.where(pl.program_id(1) < nct, c_ref[...], x_ref[...])
        r = lax.rsqrt(jnp.mean(x * x, axis=-1, keepdims=True) + NORM_EPS)
        h = ((x * r) * g_ref[...]) * (1.0 + sc_ref[...]) + sh_ref[...]
        hb = h.astype(BF16)
        h_ref[...] = hb
        p_ref[...] = _dot_nt(hb, w_ref[...]).astype(BF16)

    def mrow(b, t):
        return jnp.where(t < nct, B, b)

    return _grid_call(
        body, name="inproj_fwd", grid=(B, T // tm),
        out_shape=(jax.ShapeDtypeStruct((B, T, D), BF16), jax.ShapeDtypeStruct((B, T, nw), BF16)),
        in_specs=[ctx_spec(D), lat_spec(D),
                  pl.BlockSpec((None, None, 1, D), lambda b, t: (mrow(b, t), 0, 0, 0)),
                  pl.BlockSpec((None, None, 1, D), lambda b, t: (mrow(b, t), 1, 0, 0)),
                  pl.BlockSpec((1, D), lambda b, t: (0, 0)),
                  pl.BlockSpec((nw, D), lambda b, t: (0, 0))],
        out_specs=(pl.BlockSpec((None, tm, D), lambda b, t: (b, t, 0)),
                   pl.BlockSpec((None, tm, nw), lambda b, t: (b, t, 0))),
        scratch_shapes=[], args=(ctx, x, modl, modl, g1, w_in_t), after=after)


def _swap32(x):
    lane = lax.broadcasted_iota(jnp.int32, x.shape, 1)
    return jnp.where((lane % 64) < 32, pltpu.roll(x, 96, 1), pltpu.roll(x, 32, 1))


def _rope(x, cos, sin):
    return x * cos + _swap32(x) * sin


def _unrope(dy, cos, sin):
    return dy * cos + _swap32(dy * sin)


def _ret_weights(lgf, lgb, dist):
    return jnp.exp(jnp.where(dist >= 0.0, lgf * dist, -lgb * dist))


class _RetDecay:
    def __init__(self, lgf, lgb, rows):
        r = lax.broadcasted_iota(jnp.int32, (rows, RET_DIM), 0).astype(F32)
        self.head = r + 1.0
        self.tail = (rows - 1.0) - r
        self.q_f = jnp.exp(lgf * self.head)
        self.k_f = jnp.exp(lgf * self.tail)
        self.q_b = jnp.exp(lgb * self.tail)
        self.k_b = jnp.exp(lgb * self.head)


def _ret_states(kf32, vs, lgf, lgb, C, c, nt, hf, hb, hfa=None, hba=None):
    dec = _RetDecay(lgf, lgb, c)
    dec_c = _RetDecay(lgf, lgb, C)
    step_f = jnp.exp(jnp.zeros((RET_DIM, RET_DIM), F32) + lgf * c)
    step_b = jnp.exp(jnp.zeros((RET_DIM, RET_DIM), F32) + lgb * c)

    def upd(rows, kdec):
        return _dot_tn((kf32[rows, :] * kdec).astype(BF16), vs[rows, :])

    def lat(t):
        return slice(C + t * c, C + (t + 1) * c)

    state = upd(slice(0, C), dec_c.k_f)
    aged = jnp.zeros_like(state)
    for t in range(nt):
        hf[t] = state.astype(BF16)
        if hfa is not None:
            hfa[t] = aged
        if t < nt - 1:
            aged = step_f * (aged + c * state)
            state = step_f * state + upd(lat(t), dec.k_f)
    state = upd(slice(0, C), dec_c.k_b)
    aged = jnp.zeros_like(state)
    for t in range(nt - 1, -1, -1):
        hb[t] = state.astype(BF16)
        if hba is not None:
            hba[t] = aged
        if t > 0:
            aged = step_b * (aged + c * state)
            state = step_b * state + upd(lat(t), dec.k_b)
    return dec, dec_c, step_f, step_b


def _ret_fwd(proj, cos, sin, lg, gn, n_ctx):
    B, T, _ = proj.shape
    C = n_ctx
    N = T - C
    c = _div_tile(N, 256, 16)
    nt = N // c
    scale = RET_DIM ** -0.5

    def body(lg_ref, q_ref, k_ref, vs, g_ref, cos_ref, sin_ref, gn_ref, o_ref, lat_ref, qr_ref, kf32,
             qs, ks, hf, hb):
        h = pl.program_id(1)
        lgf = lg_ref[0, h]
        lgb = lg_ref[1, h]
        for rows in [slice(0, C)] + [slice(C + t * c, C + (t + 1) * c) for t in range(nt)]:
            cosb = cos_ref[rows, :]
            sinb = sin_ref[rows, :]
            qr = _rope(q_ref[rows, :].astype(F32), cosb, sinb) * scale
            qr_ref[rows, :] = qr
            qs[rows, :] = qr.astype(BF16)
            kr = _rope(k_ref[rows, :].astype(F32), cosb, sinb)
            kf32[rows, :] = kr
            ks[rows, :] = kr.astype(BF16)
        gnv = gn_ref[...]
        dec, _, _, _ = _ret_states(kf32, vs, lgf, lgb, C, c, nt, hf, hb)
        rc = (lax.broadcasted_iota(jnp.int32, (c, c), 0) - lax.broadcasted_iota(jnp.int32, (c, c), 1)).astype(F32)
        w_diag = _ret_weights(lgf, lgb, rc)
        for t in range(nt):
            rows = slice(C + t * c, C + (t + 1) * c)
            qt = qs[rows, :]
            s = _dot_nt(qt, ks[rows, :])
            o = (_dot((s * w_diag).astype(BF16), vs[rows, :])
                 + dec.q_f * _dot(qt, hf[t]) + dec.q_b * _dot(qt, hb[t]))
            o_ref[t * c:(t + 1) * c, :] = o
            mu = jnp.mean(o, axis=-1, keepdims=True)
            oc = o - mu
            var = jnp.mean(oc * oc, axis=-1, keepdims=True)
            yh = oc * lax.rsqrt(var + NORM_EPS)
            g = g_ref[rows, :].astype(F32)
            lat_ref[t * c:(t + 1) * c, :] = ((yh * gnv) * (g * _sigmoid(g))).astype(BF16)

    def col(seg):
        return pl.BlockSpec((None, T, RET_DIM), lambda b, h, seg=seg: (b, 0, seg * RET_HEADS + h))

    return _grid_call(
        body, name="ret_fwd", grid=(B, RET_HEADS),
        out_shape=(jax.ShapeDtypeStruct((B, N, RET_WIDTH), F32), jax.ShapeDtypeStruct((B, N, RET_WIDTH), BF16),
                   jax.ShapeDtypeStruct((B, T, RET_WIDTH), F32), jax.ShapeDtypeStruct((B, T, RET_WIDTH), F32)),
        in_specs=[pl.BlockSpec(memory_space=pltpu.SMEM), col(0), col(1), col(2), col(3),
                  pl.BlockSpec((T, RET_DIM), lambda b, h: (0, 0)), pl.BlockSpec((T, RET_DIM), lambda b, h: (0, 0)),
                  pl.BlockSpec((1, RET_DIM), lambda b, h: (0, h))],
        out_specs=(pl.BlockSpec((None, N, RET_DIM), lambda b, h: (b, 0, h)),
                   pl.BlockSpec((None, N, RET_DIM), lambda b, h: (b, 0, h)),
                   pl.BlockSpec((None, T, RET_DIM), lambda b, h: (b, 0, h)),
                   pl.BlockSpec((None, T, RET_DIM), lambda b, h: (b, 0, h))),
        scratch_shapes=[pltpu.VMEM((T, RET_DIM), BF16)] * 2 + [pltpu.VMEM((nt, RET_DIM, RET_DIM), BF16)] * 2,
        args=(lg, proj, proj, proj, proj, cos, sin, gn))


def _ret_bwd(proj, q_rot, k_rot, cos, sin, lg, gn, o, dlat, n_ctx, after):
    B, T, _ = proj.shape
    C = n_ctx
    N = T - C
    c = _div_tile(N, 256, 16)
    nt = N // c
    scale = RET_DIM ** -0.5

    def lat(t):
        return slice(C + t * c, C + (t + 1) * c)

    def body(lg_ref, qf32, kf32, vs, g_ref, cos_ref, sin_ref, gn_ref, o_ref, dl_ref,
             d_ref, dgn_ref, dlg_ref, qs, ks, dos, hf, hb, hfa, hba, gf_s, gb_s):
        h = pl.program_id(1)
        lgf = lg_ref[0, h]
        lgb = lg_ref[1, h]
        gnv = gn_ref[...]

        def fold(a):
            return jnp.sum(a.reshape(a.shape[0] // SUBLANES, SUBLANES, a.shape[1]), axis=0)

        for rows in [slice(0, C)] + [lat(t) for t in range(nt)]:
            qs[rows, :] = qf32[rows, :].astype(BF16)
            ks[rows, :] = kf32[rows, :].astype(BF16)

        dgn = jnp.zeros((1, RET_DIM), F32)
        for t in range(nt):
            lrows = slice(t * c, (t + 1) * c)
            ov = o_ref[lrows, :]
            mu = jnp.mean(ov, axis=-1, keepdims=True)
            oc = ov - mu
            var = jnp.mean(oc * oc, axis=-1, keepdims=True)
            rstd = lax.rsqrt(var + NORM_EPS)
            yh = oc * rstd
            g = g_ref[lat(t), :].astype(F32)
            sg = _sigmoid(g)
            dl = dl_ref[lrows, :]
            d_ref[3, lat(t), :] = (dl * (yh * gnv) * (sg * (1.0 + g * (1.0 - sg)))).astype(BF16)
            dls = dl * (g * sg)
            dgn = dgn + jnp.sum(dls * yh, axis=0, keepdims=True)
            dyh = dls * gnv
            do = rstd * (dyh - jnp.mean(dyh, axis=-1, keepdims=True)
                         - yh * jnp.mean(dyh * yh, axis=-1, keepdims=True))
            dos[lrows, :] = do.astype(BF16)
        dgn_ref[...] = jnp.concatenate([dgn, jnp.zeros((SUBLANES - 1, RET_DIM), F32)], axis=0)
        d_ref[3, 0:C, :] = jnp.zeros((C, RET_DIM), BF16)
        d_ref[0, 0:C, :] = jnp.zeros((C, RET_DIM), BF16)

        dec, dec_c, step_f, step_b = _ret_states(kf32, vs, lgf, lgb, C, c, nt, hf, hb, hfa, hba)

        def zmat(t, qdec):
            return _dot_tn((qf32[lat(t), :] * qdec).astype(BF16), dos[t * c:(t + 1) * c, :])

        acc3f = jnp.zeros((RET_DIM, RET_DIM), F32)
        acc3b = jnp.zeros((RET_DIM, RET_DIM), F32)
        state = jnp.zeros((RET_DIM, RET_DIM), F32)
        for t in range(nt - 1, -1, -1):
            gf_s[t] = state.astype(BF16)
            z = zmat(t, dec.q_f)
            acc3f = acc3f + hfa[t] * z
            state = step_f * state + z
        gctx_f = state.astype(BF16)
        state = jnp.zeros((RET_DIM, RET_DIM), F32)
        for t in range(nt):
            gb_s[t] = state.astype(BF16)
            z = zmat(t, dec.q_b)
            acc3b = acc3b + hba[t] * z
            state = step_b * state + z
        gctx_b = state.astype(BF16)

        rc = (lax.broadcasted_iota(jnp.int32, (c, c), 0) - lax.broadcasted_iota(jnp.int32, (c, c), 1)).astype(F32)
        w_diag = _ret_weights(lgf, lgb, rc)
        wg_f = jnp.where(rc >= 0.0, w_diag * rc, 0.0)
        wg_b = jnp.where(rc < 0.0, -w_diag * rc, 0.0)
        accf = jnp.zeros((SUBLANES, RET_DIM), F32)
        accb = jnp.zeros((SUBLANES, RET_DIM), F32)
        gdf = jnp.zeros((SUBLANES, c), F32)
        gdb = jnp.zeros((SUBLANES, c), F32)
        for t in range(nt):
            rows = lat(t)
            qt = qs[rows, :]
            kt = ks[rows, :]
            vt = vs[rows, :]
            dot = dos[t * c:(t + 1) * c, :]
            s = _dot_nt(qt, kt)
            dp = _dot_nt(dot, vt)
            dv = _dot_tn((s * w_diag).astype(BF16), dot)
            ds = (dp * w_diag).astype(BF16)
            dq = _dot(ds, kt)
            dk = _dot_tn(ds, qt)
            gs = dp * s
            gdf = gdf + fold(gs * wg_f)
            gdb = gdb + fold(gs * wg_b)
            qv = qf32[rows, :]
            kv = kf32[rows, :]
            dq_f = dec.q_f * _dot_nt(dot, hf[t])
            dq_b = dec.q_b * _dot_nt(dot, hb[t])
            dk_f = dec.k_f * _dot_nt(vt, gf_s[t])
            dk_b = dec.k_b * _dot_nt(vt, gb_s[t])
            accf = accf + fold(dec.head * dq_f * qv) + fold(dec.tail * dk_f * kv)
            accb = accb + fold(dec.tail * dq_b * qv) + fold(dec.head * dk_b * kv)
            dv = dv + dec.k_f * _dot(kt, gf_s[t]) + dec.k_b * _dot(kt, gb_s[t])
            cosb = cos_ref[rows, :]
            sinb = sin_ref[rows, :]
            d_ref[0, rows, :] = _unrope((dq + dq_f + dq_b) * scale, cosb, sinb).astype(BF16)
            d_ref[1, rows, :] = _unrope(dk + dk_f + dk_b, cosb, sinb).astype(BF16)
            d_ref[2, rows, :] = dv.astype(BF16)
        kc = ks[0:C, :]
        vc = vs[0:C, :]
        kcv = kf32[0:C, :]
        dkc_f = dec_c.k_f * _dot_nt(vc, gctx_f)
        dkc_b = dec_c.k_b * _dot_nt(vc, gctx_b)
        accf = accf + fold(dec_c.tail * dkc_f * kcv)
        accb = accb + fold(dec_c.head * dkc_b * kcv)
        d_ref[1, 0:C, :] = (dkc_f + dkc_b).astype(BF16)
        d_ref[2, 0:C, :] = (dec_c.k_f * _dot(kc, gctx_f) + dec_c.k_b * _dot(kc, gctx_b)).astype(BF16)
        gf = jnp.sum(gdf) + jnp.sum(accf) + jnp.sum(acc3f)
        gb = jnp.sum(gdb) + jnp.sum(accb) + jnp.sum(acc3b)
        row = lax.broadcasted_iota(jnp.int32, (SUBLANES, LANES), 0)
        dlg_ref[...] = jnp.where(row == 0, gf, jnp.where(row == 1, gb, 0.0))

    def col(seg):
        return pl.BlockSpec((None, T, RET_DIM), lambda b, h, seg=seg: (b, 0, seg * RET_HEADS + h))

    def head(rows):
        return pl.BlockSpec((None, rows, RET_DIM), lambda b, h: (b, 0, h))

    return _grid_call(
        body, name="ret_bwd", grid=(B, RET_HEADS),
        out_shape=(jax.ShapeDtypeStruct((B, 4, T, RET_WIDTH), BF16),
                   jax.ShapeDtypeStruct((B, SUBLANES, RET_WIDTH), F32),
                   jax.ShapeDtypeStruct((B, RET_HEADS, SUBLANES, LANES), F32)),
        in_specs=[pl.BlockSpec(memory_space=pltpu.SMEM), head(T), head(T), col(2), col(3),
                  pl.BlockSpec((T, RET_DIM), lambda b, h: (0, 0)), pl.BlockSpec((T, RET_DIM), lambda b, h: (0, 0)),
                  pl.BlockSpec((1, RET_DIM), lambda b, h: (0, h)), head(N), head(N)],
        out_specs=(pl.BlockSpec((None, 4, T, RET_DIM), lambda b, h: (b, 0, 0, h)),
                   pl.BlockSpec((None, SUBLANES, RET_DIM), lambda b, h: (b, 0, h)),
                   pl.BlockSpec((None, None, SUBLANES, LANES), lambda b, h: (b, h, 0, 0))),
        scratch_shapes=[pltpu.VMEM((T, RET_DIM), BF16)] * 2 + [pltpu.VMEM((N, RET_DIM), BF16)]
                       + [pltpu.VMEM((nt, RET_DIM, RET_DIM), BF16)] * 2 + [pltpu.VMEM((nt, RET_DIM, RET_DIM), F32)] * 2
                       + [pltpu.VMEM((nt, RET_DIM, RET_DIM), BF16)] * 2,
        args=(lg, q_rot, k_rot, proj, proj, cos, sin, gn, o, dlat), after=after)


def _na_geometry(rows):
    kh = min(NA_KH, rows)
    return kh, kh * GRID_W


def _pair_select():
    lane = lax.broadcasted_iota(jnp.int32, (2 * GRID_W, LANES), 1)
    row = lax.broadcasted_iota(jnp.int32, (2 * GRID_W, LANES), 0)
    return (lane >= NA_DIM) == (row >= GRID_W)


def _pair_bias(bias_ref, dr0, kh):
    return jnp.concatenate(
        [jnp.concatenate([bias_ref[e, pl.ds(dr0 + 2 * m, 1)].reshape(GRID_W, LANES) for m in range(kh // 2)], axis=1)
         for e in range(2)], axis=0)


def _na_softmax(s_loc, s_ctx):
    mx = jnp.maximum(jnp.max(s_loc, axis=-1, keepdims=True), jnp.max(s_ctx, axis=-1, keepdims=True))
    p_loc = jnp.exp(s_loc - mx)
    p_ctx = jnp.exp(s_ctx - mx)
    den = jnp.sum(p_loc, axis=-1, keepdims=True) + jnp.sum(p_ctx, axis=-1, keepdims=True)
    return p_loc, p_ctx, den


def _na_fwd(proj, bias2, n_ctx):
    assert proj.dtype == BF16
    B, T, _ = proj.shape
    C = n_ctx
    N = T - C
    R = N // GRID_W
    kh, nk = _na_geometry(R)
    scale = NA_DIM ** -0.5
    base = (4 * RET_WIDTH) // LANES

    def body(q_ref, kb16, vb16, bias_ref, out_ref, p_ref):
        kc = kb16[0:C, :]
        vc = vb16[0:C, :]
        lane = lax.broadcasted_iota(jnp.int32, (GRID_W, LANES), 1)
        sel2 = _pair_select()

        def group(gi, carry):
            pre = []
            for u in range(NA_GROUP):
                r = gi * NA_GROUP + u
                bs = jnp.clip(r - kh // 2, 0, R - kh)
                dr0 = bs - r + (NA_KH - 1)
                q = q_ref[pl.ds(pl.multiple_of(C + r * GRID_W, GRID_W), GRID_W), :].astype(F32) * scale
                q2 = jnp.where(sel2, jnp.concatenate([q, q], axis=0), 0.0).astype(BF16)
                band = pl.ds(pl.multiple_of(C + bs * GRID_W, GRID_W), nk)
                s_loc = _dot_nt(q2, kb16[band, :]) + _pair_bias(bias_ref, dr0, kh)
                s_ctx = _dot_nt(q2, kc)
                pre.append((r, band, s_loc, s_ctx))
            mid = [(r, band) + _na_softmax(s_loc, s_ctx) for r, band, s_loc, s_ctx in pre]
            for r, band, p_loc, p_ctx, den in mid:
                inv = 1.0 / den
                pb_loc = (p_loc * inv).astype(BF16)
                pb_ctx = (p_ctx * inv).astype(BF16)
                p_ref[r, :, 0:nk] = pb_loc
                p_ref[r, :, nk:] = pb_ctx
                o2 = _dot(pb_loc, vb16[band, :]) + _dot(pb_ctx, vc)
                out_ref[pl.ds(pl.multiple_of(r * GRID_W, GRID_W), GRID_W), :] = jnp.where(
                    lane < NA_DIM, o2[:GRID_W], o2[GRID_W:]).astype(BF16)
            return carry

        lax.fori_loop(0, R // NA_GROUP, group, 0)

    def col(seg):
        return pl.BlockSpec((None, T, LANES), lambda b, p, seg=seg: (b, 0, base + seg * NA_PAIRS + p))

    return _grid_call(
        body, name="na_fwd", grid=(B, NA_PAIRS),
        out_shape=(jax.ShapeDtypeStruct((B, N, NA_WIDTH), BF16),
                   jax.ShapeDtypeStruct((B, NA_PAIRS, R, 2 * GRID_W, nk + C), BF16)),
        in_specs=[col(0), col(1), col(2),
                  pl.BlockSpec((2, 2 * NA_KH - 2, GRID_W, LANES), lambda b, p: (p, 0, 0, 0))],
        out_specs=(pl.BlockSpec((None, N, LANES), lambda b, p: (b, 0, p)),
                   pl.BlockSpec((None, None, R, 2 * GRID_W, nk + C), lambda b, p: (b, p, 0, 0, 0))),
        scratch_shapes=[],
        args=(proj, proj, proj, bias2))


def _na_bwd(proj, probs, dlat, n_ctx):
    assert proj.dtype == BF16
    B, T, _ = proj.shape
    C = n_ctx
    N = T - C
    R = N // GRID_W
    kh, nk = _na_geometry(R)
    scale = NA_DIM ** -0.5
    base = (4 * RET_WIDTH) // LANES

    def class_sums(tiles):
        n = tiles.shape[0]
        acc = None
        for v in range(GRID_W // SUBLANES):
            part = tiles[:, v * SUBLANES:(v + 1) * SUBLANES, :].reshape(n * SUBLANES, LANES)
            part = pltpu.roll(part, (NA_KW - 1 - v * SUBLANES) % LANES, 1)
            acc = part if acc is None else acc + part
        row = lax.broadcasted_iota(jnp.int32, acc.shape, 0)
        for bit in (1, 2, 4):
            acc = jnp.where((row & bit) != 0, pltpu.roll(acc, LANES - bit, 1), acc)
        return jnp.sum(acc.reshape(n, SUBLANES, LANES), axis=1)

    def body(q_ref, kb16, vb16, p_ref, dl_ref, d_ref, rr_ref, dkv, db_ref):
        b = pl.program_id(1)
        kc = kb16[0:C, :]
        vc = vb16[0:C, :]
        lane = lax.broadcasted_iota(jnp.int32, (GRID_W, LANES), 1)
        dkv[...] = jnp.zeros(dkv.shape, F32)
        d_ref[0, 0:C, :] = jnp.zeros((C, LANES), BF16)

        @pl.when(b == 0)
        def _():
            db_ref[...] = jnp.zeros(db_ref.shape, F32)

        sel2 = _pair_select()

        def group(gi, carry):
            pre = []
            for u in range(NA_GROUP):
                r = gi * NA_GROUP + u
                bs = jnp.clip(r - kh // 2, 0, R - kh)
                dr0 = bs - r + (NA_KH - 1)
                q = q_ref[pl.ds(pl.multiple_of(C + r * GRID_W, GRID_W), GRID_W), :].astype(F32) * scale
                do = dl_ref[pl.ds(pl.multiple_of(r * GRID_W, GRID_W), GRID_W), :]
                q2 = jnp.where(sel2, jnp.concatenate([q, q], axis=0), 0.0).astype(BF16)
                do2 = jnp.where(sel2, jnp.concatenate([do, do], axis=0), 0.0).astype(BF16)
                band = pl.ds(pl.multiple_of(C + bs * GRID_W, GRID_W), nk)
                dp_loc = _dot_nt(do2, vb16[band, :])
                dp_ctx = _dot_nt(do2, vc)
                pre.append((r, dr0, band, q2, do2, dp_loc, dp_ctx))
            mid = []
            for r, dr0, band, q2, do2, dp_loc, dp_ctx in pre:
                pb_loc = p_ref[r, :, 0:nk]
                pb_ctx = p_ref[r, :, nk:]
                p_loc = pb_loc.astype(F32)
                p_ctx = pb_ctx.astype(F32)
                delta = (jnp.sum(p_loc * dp_loc, axis=-1, keepdims=True)
                         + jnp.sum(p_ctx * dp_ctx, axis=-1, keepdims=True))
                ds_loc = p_loc * (dp_loc - delta)
                ds_ctx = p_ctx * (dp_ctx - delta)
                mid.append((r, dr0, band, q2, do2, pb_loc, pb_ctx, ds_loc, ds_ctx))
            for r, dr0, band, q2, do2, pb_loc, pb_ctx, ds_loc, ds_ctx in mid:
                dsb_loc = ds_loc.astype(BF16)
                dsb_ctx = ds_ctx.astype(BF16)
                dq2 = _dot(dsb_loc, kb16[band, :]) + _dot(dsb_ctx, kc)
                d_ref[0, pl.ds(pl.multiple_of(C + r * GRID_W, GRID_W), GRID_W), :] = (jnp.where(
                    lane < NA_DIM, dq2[:GRID_W], dq2[GRID_W:]) * scale).astype(BF16)
                dkv[0, band, :] += _dot_tn(dsb_loc, q2)
                dkv[1, band, :] += _dot_tn(pb_loc, do2)
                dkv[0, 0:C, :] += _dot_tn(dsb_ctx, q2)
                dkv[1, 0:C, :] += _dot_tn(pb_ctx, do2)
                for e in range(2):
                    for m in range(kh // 2):
                        db_ref[e, pl.ds(dr0 + 2 * m, 1)] += ds_loc[e * GRID_W:(e + 1) * GRID_W,
                                                                   m * LANES:(m + 1) * LANES].reshape(1, GRID_W, LANES)
            return carry

        lax.fori_loop(0, R // NA_GROUP, group, 0)
        d_ref[1] = dkv[0].astype(BF16)
        d_ref[2] = dkv[1].astype(BF16)

        @pl.when(b == B - 1)
        def _():
            for e in range(2):
                rr_ref[e] = class_sums(db_ref[e])

    def col(seg):
        return pl.BlockSpec((None, T, LANES), lambda p, b, seg=seg: (b, 0, base + seg * NA_PAIRS + p))

    return _grid_call(
        body, name="na_bwd", grid=(NA_PAIRS, B),
        out_shape=(jax.ShapeDtypeStruct((B, 3, T, NA_WIDTH), BF16),
                   jax.ShapeDtypeStruct((NA_HEADS, 2 * NA_KH - 2, LANES), F32)),
        in_specs=[col(0), col(1), col(2),
                  pl.BlockSpec((None, None, R, 2 * GRID_W, nk + C), lambda p, b: (b, p, 0, 0, 0)),
                  pl.BlockSpec((None, N, LANES), lambda p, b: (b, 0, p))],
        out_specs=(pl.BlockSpec((None, 3, T, LANES), lambda p, b: (b, 0, 0, p)),
                   pl.BlockSpec((2, 2 * NA_KH - 2, LANES), lambda p, b: (p, 0, 0))),
        scratch_shapes=[pltpu.VMEM((2, T, LANES), F32), pltpu.VMEM((2, 2 * NA_KH - 2, GRID_W, LANES), F32)],
        args=(proj, proj, proj, probs, dlat))


def _dense_core(lat_ret, lat_na, x, tgt, modl, g_post_mix, g_pre_mlp, g_post_mlp, w_out, w1, w2):
    B, N, D = x.shape
    F = w1.shape[1]
    wout_rows, w1_cols, w2_rows = w_out.shape[0] // N_DEV, w1.shape[1] // N_DEV, w2.shape[0] // N_DEV
    mixw = w_out.shape[0]
    half = mixw // 2
    tm = _div_tile(N, 256, 16)
    nt = N // tm
    fc = _div_tile(F, 1024, LANES)

    def body(lr_ref, ln_ref, x_ref, t_ref, gt1_ref, sh2_ref, sc2_ref, gt2_ref, gpm_ref, gpre_ref, gpo_ref,
             wout_part, w1_part, w2_part,
             dy1_ref, dlr_ref, dln_ref, dmix_ref, h2_ref, a_ref, du_ref, dz_ref, red_ref, wout_hbm, w1_hbm, w2_hbm,
             wout_v, w1_v, w2_v, u_s, sems, fsend, frecv):
        @pl.when((pl.program_id(0) == 0) & (pl.program_id(1) == 0))
        def _():
            relay = [(_row_block(wout_part, wout_rows), _row_block(wout_hbm, wout_rows)),
                     (_col_block(w1_part, w1_cols), _col_block(w1_hbm, w1_cols)),
                     (_row_block(w2_part, w2_rows), _row_block(w2_hbm, w2_rows))]
            _forward_start(relay, fsend, frecv)
            _forward_wait(relay, fsend, frecv)
            cps = [pltpu.make_async_copy(wout_hbm, wout_v, sems.at[0]),
                   pltpu.make_async_copy(w1_hbm, w1_v, sems.at[1]),
                   pltpu.make_async_copy(w2_hbm, w2_v, sems.at[2])]
            for cp in cps:
                cp.start()
            for cp in cps:
                cp.wait()

        @pl.when(pl.program_id(1) == 0)
        def _():
            red_ref[...] = jnp.zeros(red_ref.shape, F32)

        gt1 = gt1_ref[...]
        sh2 = sh2_ref[...]
        sc2 = sc2_ref[...]
        gt2 = gt2_ref[...]
        gpm = gpm_ref[...]
        gpre = gpre_ref[...]
        gpo = gpo_ref[...]

        def rowmean(a):
            return jnp.mean(a, axis=-1, keepdims=True)

        def colsum(a):
            return jnp.sum(a, axis=0, keepdims=True)

        mix_gain = gt1 * gpm
        mlp_in_gain = gpre * (1.0 + sc2)
        mlp_out_gain = gt2 * gpo
        mix = _dot(lr_ref[...], wout_v[0:half, :]) + _dot(ln_ref[...], wout_v[half:, :])
        x = x_ref[...]
        rm = lax.rsqrt(rowmean(mix * mix) + NORM_EPS)
        mh = mix * rm
        y1 = x + mh * mix_gain
        r1 = lax.rsqrt(rowmean(y1 * y1) + NORM_EPS)
        xh = y1 * r1
        h2b = (xh * mlp_in_gain + sh2).astype(BF16)
        h2_ref[...] = h2b
        z = jnp.zeros((tm, D), F32)
        for c0 in range(0, F, fc):
            u = _dot(h2b, w1_v[:, c0:c0 + fc])
            u_s[:, c0:c0 + fc] = u
            ru = jnp.maximum(u, 0.0)
            ab = (ru * ru).astype(BF16)
            a_ref[:, c0:c0 + fc] = ab
            z = z + _dot(ab, w2_v[c0:c0 + fc, :])
        r2 = lax.rsqrt(rowmean(z * z) + NORM_EPS)
        zh = z * r2
        y2 = y1 + zh * mlp_out_gain
        err = y2 - t_ref[...]
        loss = 0.5 * jnp.sum(rowmean(err * err))
        dy2 = err * (1.0 / D)
        s_out = colsum(dy2 * zh)
        red_ref[2:3, :] += s_out * gpo
        red_ref[6:7, :] += s_out * gt2
        dzh = dy2 * mlp_out_gain
        dz = r2 * (dzh - zh * rowmean(dzh * zh))
        dzb = dz.astype(BF16)
        dz_ref[...] = dzb
        dh2 = jnp.zeros((tm, D), F32)
        for c0 in range(0, F, fc):
            da = _dot_nt(dzb, w2_v[c0:c0 + fc, :])
            dub = (da * (2.0 * jnp.maximum(u_s[:, c0:c0 + fc], 0.0))).astype(BF16)
            du_ref[:, c0:c0 + fc] = dub
            dh2 = dh2 + _dot_nt(dub, w1_v[:, c0:c0 + fc])
        s_in = colsum(dh2 * xh)
        red_ref[3:4, :] += s_in * gpre
        red_ref[4:5, :] += colsum(dh2)
        red_ref[5:6, :] += s_in * (1.0 + sc2)
        dxh = dh2 * mlp_in_gain
        dy1 = dy2 + r1 * (dxh - xh * rowmean(dxh * xh))
        dy1_ref[...] = dy1
        s_mix = colsum(dy1 * mh)
        red_ref[0:1, :] += s_mix * gpm
        red_ref[1:2, :] += s_mix * gt1
        dmh = dy1 * mix_gain
        dmix = (rm *(dmh - mh * rowmean(dmh * mh))).astype(BF16)
        dmix_ref[...] = dmix
        dlr_ref[...] = _dot_nt(dmix, wout_v[0:half, :])
        dln_ref[...] = _dot_nt(dmix, wout_v[half:, :])
        red_ref[7:8, :] += jnp.zeros((1, D), F32) + loss

    def tok(w):
        return pl.BlockSpec((None, tm, w), lambda b, t: (b, t, 0))

    def mod(k):
        return pl.BlockSpec((None, None, 1, D), lambda b, t, k=k: (b, k, 0, 0))

    def vec():
        return pl.BlockSpec((1, D), lambda b, t: (0, 0))

    return pl.pallas_call(
        body, name="dense_core", grid=(B, nt),
        out_shape=(jax.ShapeDtypeStruct((B, N, D), F32), jax.ShapeDtypeStruct((B, N, half), F32),
                   jax.ShapeDtypeStruct((B, N, half), F32), jax.ShapeDtypeStruct((B, N, D), BF16),
                   jax.ShapeDtypeStruct((B, N, D), BF16), jax.ShapeDtypeStruct((B, N, F), BF16),
                   jax.ShapeDtypeStruct((B, N, F), BF16), jax.ShapeDtypeStruct((B, N, D), BF16),
                   jax.ShapeDtypeStruct((B, SUBLANES, D), F32),
                   jax.ShapeDtypeStruct(w_out.shape, w_out.dtype), jax.ShapeDtypeStruct(w1.shape, w1.dtype),
                   jax.ShapeDtypeStruct(w2.shape, w2.dtype)),
        in_specs=[tok(half), tok(half), tok(D), tok(D), mod(2), mod(3), mod(4), mod(5), vec(), vec(), vec(),
                  _any(), _any(), _any()],
        out_specs=(tok(D), tok(half), tok(half), tok(D), tok(D), tok(F), tok(F), tok(D),
                   pl.BlockSpec((None, SUBLANES, D), lambda b, t: (b, 0, 0)), _any(), _any(), _any()),
        scratch_shapes=[pltpu.VMEM((mixw, D), BF16), pltpu.VMEM((D, F), BF16), pltpu.VMEM((F, D), BF16),
                        pltpu.VMEM((tm, F), F32), pltpu.SemaphoreType.DMA((3,)),
                        pltpu.SemaphoreType.DMA((3, 3)), pltpu.SemaphoreType.DMA((3, 3))],
        input_output_aliases={11: 9, 12: 10, 13: 11},
        compiler_params=_params("arbitrary", "arbitrary"),
    )(lat_ret, lat_na, x, tgt, modl, modl, modl, modl, g_post_mix, g_pre_mlp, g_post_mlp, w_out, w1, w2)[:9]


def _inproj_bwd(dret, dna, x, ctx, dy1, modl, g1, w_in_t, after):
    B, N, D = x.shape
    n_ctx = ctx.shape[1]
    T = n_ctx + N
    tm = _div_tile(n_ctx, 256, 16)
    nct, ctx_spec, lat_spec = _token_tiles(n_ctx, tm)
    nt = T // tm
    nseg_r = dret.shape[1]
    nseg_n = dna.shape[1]
    nw = w_in_t.shape[0]

    def body(*refs):
        seg_refs = refs[:nseg_r + nseg_n]
        c_ref, x_ref, dy1_ref, sc_ref, g_ref, w_ref, dx_ref, red_ref = refs[nseg_r + nseg_n:]
        t = pl.program_id(1)
        dh = jnp.zeros((tm, D), F32)
        for s, ref in enumerate(seg_refs):
            dh = dh + _dot(ref[...], w_ref[s * SEG:(s + 1) * SEG, :])
        x = jnp.where(t < nct, c_ref[...], x_ref[...])
        g = g_ref[...]
        r = lax.rsqrt(jnp.mean(x * x, axis=-1, keepdims=True) + NORM_EPS)
        xh = x * r
        gain = 1.0 + sc_ref[...]
        s_in = jnp.sum(dh * xh, axis=0, keepdims=True)
        red_ref[0:1, :] = jnp.sum(dh, axis=0, keepdims=True)
        red_ref[1:2, :] = s_in * g
        red_ref[2:3, :] = s_in * gain
        red_ref[3:, :] = jnp.zeros((SUBLANES - 3, D), F32)
        dxh = dh * (g * gain)
        dx = r * (dxh - xh * jnp.mean(dxh * xh, axis=-1, keepdims=True))
        dx_ref[...] = dx + jnp.where(t >= nct, dy1_ref[...], 0.0)

    def mrow(b, t):
        return jnp.where(t < nct, B, b)

    def seg(s):
        return pl.BlockSpec((None, None, tm, SEG), lambda b, t, s=s: (b, s, t, 0))

    return _grid_call(
        body, name="inproj_bwd", grid=(B, nt),
        out_shape=(jax.ShapeDtypeStruct((B, N, D), F32), jax.ShapeDtypeStruct((B, nt, SUBLANES, D), F32)),
        in_specs=[seg(s) for s in range(nseg_r)] + [seg(s) for s in range(nseg_n)]
                 + [ctx_spec(D), lat_spec(D), lat_spec(D),
                    pl.BlockSpec((None, None, 1, D), lambda b, t: (mrow(b, t), 1, 0, 0)),
                    pl.BlockSpec((1, D), lambda b, t: (0, 0)),
                    pl.BlockSpec((nw, D), lambda b, t: (0, 0))],
        out_specs=(lat_spec(D), pl.BlockSpec((None, None, SUBLANES, D), lambda b, t: (b, t, 0, 0))),
        scratch_shapes=[], args=(*([dret] * nseg_r), *([dna] * nseg_n), ctx, x, dy1, modl, g1, w_in_t), after=after)


def _tn_matmul(lhs, rhs, name, rows_before=0, rows_after=0, into=None):
    B, S, T, W = lhs.shape
    nn = rhs.shape[-1]
    tk = _div_tile(T, 2304, LANES)
    bm = _div_tile(W, 1024, LANES)
    bn = _div_tile(nn, 1024, LANES)
    nkt = T // tk
    nk = B * nkt

    def body(l_ref, r_ref, *rest):
        o_ref, acc = rest[-2:]
        k = pl.program_id(3)

        @pl.when(k == 0)
        def _():
            acc[...] = jnp.zeros(acc.shape, F32)

        acc[...] += _dot_tn(l_ref[...].astype(BF16), r_ref[...].astype(BF16))

        @pl.when(k == nk - 1)
        def _():
            o_ref[...] = acc[...].astype(BF16)

    nwb = W // bm
    first = rows_before // bm
    return pl.pallas_call(
        functools.partial(body), name=name, grid=(S, nwb, nn // bn, nk),
        out_shape=jax.ShapeDtypeStruct((rows_before + S * W + rows_after, nn), BF16),
        in_specs=[pl.BlockSpec((None, None, tk, bm), lambda s, i, j, k: (k // nkt, s, k % nkt, i)),
                  pl.BlockSpec((None, tk, bn), lambda s, i, j, k: (k // nkt, k % nkt, j))]
                 + ([] if into is None else [_any()]),
        out_specs=pl.BlockSpec((bm, bn), lambda s, i, j, k: (first + s * nwb + i, j)),
        scratch_shapes=[pltpu.VMEM((bm, bn), F32)],
        input_output_aliases={} if into is None else {2: 0},
        compiler_params=_params("parallel", "parallel", "parallel", "arbitrary"),
    )(lhs, rhs, *([] if into is None else [into]))


class _SplitScatter:
    def __init__(self, gs, block_ofs, land_shapes, name, kind="scatter", masks=ALL_PEERS):
        self.n = n = len(gs)
        self.block_ofs, self.kind, self.masks = block_ofs, kind, masks
        if kind == "scatter":
            land_shapes = [(N_DEV,) + tuple(bs) for bs in land_shapes]
        hbm = pl.BlockSpec(memory_space=pltpu.HBM)
        sem = pl.BlockSpec(memory_space=pltpu.SEMAPHORE)

        def body(*refs):
            g_refs, land_refs = refs[:n], refs[n:2 * n]
            send_sems, recv_sems, own_sems = refs[2 * n:2 * n + 3]
            token = refs[-1]
            for own, pushes in self._copies(g_refs, land_refs, send_sems, recv_sems, own_sems, landing="sender"):
                own.start()
                for cp in pushes:
                    cp.start()
            token[...] = jnp.zeros_like(token)

        outs = pl.pallas_call(
            body, name=name,
            out_shape=(pltpu.SemaphoreType.DMA((n * (N_DEV - 1),)), pltpu.SemaphoreType.DMA((n * (N_DEV - 1),)),
                       pltpu.SemaphoreType.DMA((n,)))
                      + tuple(pltpu.HBM(g.shape, g.dtype) for g in gs)
                      + tuple(pltpu.HBM(s, g.dtype) for s, g in zip(land_shapes, gs))
                      + (jax.ShapeDtypeStruct((SUBLANES, LANES), F32),),
            in_specs=(hbm,) * (2 * n), out_specs=(sem,) * 3 + (hbm,) * (2 * n) + (_vmem(),),
            input_output_aliases={k: 3 + k for k in range(2 * n)},
            compiler_params=pltpu.CompilerParams(has_side_effects=pltpu.SideEffectType.DATAFLOW_SIDE_EFFECTING),
        )(*[pltpu.with_memory_space_constraint(g, pltpu.HBM) for g in gs],
          *[pltpu.with_memory_space_constraint(lax.empty(s, g.dtype), pltpu.HBM) for s, g in zip(land_shapes, gs)])
        self.sems, self.thru, self.token = outs[:3], outs[3:3 + 2 * n], outs[-1]

    def _copies(self, g_refs, land_refs, send_sems, recv_sems, own_sems, landing):
        me, peers = _me_and_peers()
        out = []
        for k in range(self.n):
            if self.kind == "scatter":
                src, dst = self.block_ofs[k](g_refs[k]), _slot(land_refs[k])
            else:
                src, dst = (lambda p, k=k: g_refs[k]), self.block_ofs[k](land_refs[k])
            own = pltpu.make_async_copy(src(me), dst(me), own_sems.at[k]) if landing == "sender" else None
            pushes = []
            for m in self.masks:
                dev, pid = peers[m - 1]
                i = k * (N_DEV - 1) + m - 1
                pushes.append(_remote(src(pid), dst(me if landing == "sender" else pid),
                                      send_sems.at[i], recv_sems.at[i], dev))
            out.append((own, pushes))
        return out


def _scatter_wait(scatters, after, name):
    hbm = pl.BlockSpec(memory_space=pltpu.HBM)
    sem = pl.BlockSpec(memory_space=pltpu.SEMAPHORE)
    n_arr = [2 * sc.n for sc in scatters]
    total = sum(n_arr)

    def body(*refs):
        arrs, sems = refs[:total], refs[total:total + 3 * len(scatters)]
        a0 = 0
        for j, sc in enumerate(scatters):
            g_refs, land_refs = arrs[a0:a0 + sc.n], arrs[a0 + sc.n:a0 + 2 * sc.n]
            a0 += 2 * sc.n
            send_sems, recv_sems, own_sems = sems[3 * j:3 * j + 3]
            for (own, sent), (_, got) in zip(sc._copies(g_refs, land_refs, send_sems, recv_sems, own_sems, "sender"),
                                             sc._copies(g_refs, land_refs, send_sems, recv_sems, own_sems, "receiver")):
                own.wait()
                for cp in sent:
                    cp.wait_send()
                for cp in got:
                    cp.wait_recv()

    operands = [a for sc in scatters for a in sc.thru]
    outs = pl.pallas_call(
        body, name=name,
        out_shape=tuple(pltpu.HBM(a.shape, a.dtype) for a in operands),
        in_specs=(hbm,) * total + (sem,) * (3 * len(scatters)) + (pl.BlockSpec(memory_space=pl.ANY),),
        out_specs=(hbm,) * total, input_output_aliases={k: k for k in range(total)},
        compiler_params=pltpu.CompilerParams(has_side_effects=pltpu.SideEffectType.DATAFLOW_SIDE_EFFECTING),
    )(*operands, *[s for sc in scatters for s in sc.sems], after)
    lands, a0 = [], 0
    for sc in scatters:
        lands.extend(outs[a0 + sc.n:a0 + 2 * sc.n])
        a0 += 2 * sc.n
    return lands


def _small_ar(mbuf, silu_all, w_ada, c_ctx, n_mod_rows, n_vec_rows):
    D = silu_all.shape[1]
    ncol = w_ada.shape[1]
    nm = mbuf.shape[2]
    srows = silu_all.shape[0]

    def body(mbuf, s_ref, w_ref, cc_ref, tot_ref, gb_ref, gw_ref, gc_ref, tbuf, dmx, cmrow, send3, recv3):
        me, _ = _me_and_peers()
        msum = mbuf[0]
        for k in range(1, N_DEV):
            msum = msum + mbuf[k]
        tot_ref[...] = msum[n_mod_rows:n_mod_rows + n_vec_rows]
        gb_ref[...] = jnp.sum(msum[0:n_mod_rows], axis=0, keepdims=True)
        loc = pl.ds(pl.multiple_of(me * ncol, ncol), ncol)
        for k in range(N_DEV):
            dmx[k * SUBLANES:(k + 1) * SUBLANES, :] = mbuf[k, :, loc]
        cmrow[...] = msum
        cm_loc = cmrow[n_mod_rows - 1:n_mod_rows, loc]
        dmx[N_DEV * SUBLANES:, :] = jnp.concatenate([cm_loc, jnp.zeros((SUBLANES - 1, ncol), F32)], axis=0)
        gw_ref[...] = _dot_tn(s_ref[...], dmx[...])
        tbuf[me] = _dot_nt(dmx[N_DEV * SUBLANES:, :], w_ref[...])
        _exchange(lambda p: tbuf.at[me], lambda p: tbuf.at[p], send3, recv3)
        tsum = tbuf[0]
        for k in range(1, N_DEV):
            tsum = tsum + tbuf[k]
        cc = cc_ref[...]
        sg = _sigmoid(cc)
        gc_ref[...] = tsum[0:1, :] * (sg * (1.0 + cc * (1.0 - sg)))

    return pl.pallas_call(
        body, name="small_ar",
        out_shape=(jax.ShapeDtypeStruct((n_vec_rows, nm), F32), jax.ShapeDtypeStruct((1, nm), F32),
                   jax.ShapeDtypeStruct((D, ncol), F32), jax.ShapeDtypeStruct((1, D), F32)),
        in_specs=[_vmem()] * 4, out_specs=(_vmem(),) * 4,
        scratch_shapes=[pltpu.VMEM((N_DEV, SUBLANES, D), F32), pltpu.VMEM((srows, ncol), F32),
                        pltpu.VMEM((SUBLANES, nm), F32)] + [pltpu.SemaphoreType.DMA((N_DEV - 1,))] * 2,
        compiler_params=pltpu.CompilerParams(vmem_limit_bytes=VMEM_LIMIT),
    )(mbuf, silu_all, w_ada, c_ctx.reshape(1, D))


def _adam_update(w, g, m, v):
    mn = ADAM_B1 * m + (1.0 - ADAM_B1) * g
    vn = ADAM_B2 * v + (1.0 - ADAM_B2) * (g * g)
    m_hat = mn / (1.0 - ADAM_B1 ** ADAM_STEP)
    v_hat = vn / (1.0 - ADAM_B2 ** ADAM_STEP)
    return -ADAM_LR * (m_hat / (jnp.sqrt(v_hat) + ADAM_EPS) + ADAM_WD * w), mn, vn


def _adamw(w, g, m, v, name):
    rows, cols = w.shape
    tr = _div_tile(rows, 512, SUBLANES)

    def body(w_ref, g_ref, m_ref, v_ref, d_ref, nm_ref, nv_ref):
        d_ref[...], nm_ref[...], nv_ref[...] = _adam_update(w_ref[...], g_ref[...], m_ref[...], v_ref[...])

    spec = pl.BlockSpec((tr, cols), lambda i: (i, 0))
    return pl.pallas_call(
        functools.partial(body), name=name, grid=(rows // tr,),
        out_shape=(jax.ShapeDtypeStruct((rows, cols), F32),) * 3,
        in_specs=[spec] * 4, out_specs=(spec,) * 3,
        compiler_params=_params("parallel"),
    )(w, g, m, v)


def _adamw_small(items, name):
    n = len(items)

    def body(*refs):
        ins, outs = refs[:4 * n], refs[4 * n:]
        for i in range(n):
            w_ref, g_ref, m_ref, v_ref = ins[4 * i:4 * i + 4]
            outs[3 * i][...], outs[3 * i + 1][...], outs[3 * i + 2][...] = _adam_update(
                w_ref[...], g_ref[...], m_ref[...], v_ref[...])

    outs = pl.pallas_call(
        body, name=name,
        out_shape=tuple(jax.ShapeDtypeStruct(it[0].shape, F32) for it in items for _ in range(3)),
        in_specs=[_vmem()] * (4 * n), out_specs=(_vmem(),) * (3 * n),
        compiler_params=pltpu.CompilerParams(vmem_limit_bytes=VMEM_LIMIT),
    )(*[a for it in items for a in it])
    return [tuple(outs[3 * i:3 * i + 3]) for i in range(n)]


def _sum_adamw(buf, w, m, v, name):
    _, rows, cols = buf.shape
    tr = _div_tile(rows, 256, 2 * SUBLANES)

    def body(b_ref, w_ref, m_ref, v_ref, g_ref, d_ref, nm_ref, nv_ref):
        g = b_ref[0].astype(F32)
        for k in range(1, N_DEV):
            g = g + b_ref[k].astype(F32)
        g_ref[...] = g
        d_ref[...], nm_ref[...], nv_ref[...] = _adam_update(w_ref[...], g, m_ref[...], v_ref[...])

    spec = pl.BlockSpec((tr, cols), lambda i: (i, 0))
    return pl.pallas_call(
        functools.partial(body), name=name, grid=(rows // tr,),
        out_shape=(jax.ShapeDtypeStruct((rows, cols), F32),) * 4,
        in_specs=[pl.BlockSpec((N_DEV, tr, cols), lambda i: (0, i, 0))] + [spec] * 3, out_specs=(spec,) * 4,
        compiler_params=_params("parallel"),
    )(buf, w, m, v)


def _rope_tables(n_ctx, n):
    n_freq = RET_DIM // 4
    inv = np.float32(ROPE_BASE) ** (-np.arange(n_freq, dtype=np.float32) / np.float32(n_freq))
    tok = np.arange(n)
    pos_r = (tok // GRID_W).astype(np.float32)
    pos_c = (tok % GRID_W).astype(np.float32)
    ang_r = (pos_r[:, None] * inv[None, :]).astype(np.float32)
    ang_c = (pos_c[:, None] * inv[None, :]).astype(np.float32)
    cos = np.concatenate([np.cos(ang_r), np.cos(ang_r), np.cos(ang_c), np.cos(ang_c)], axis=-1)
    sin = np.concatenate([-np.sin(ang_r), np.sin(ang_r), -np.sin(ang_c), np.sin(ang_c)], axis=-1)
    cos = np.concatenate([np.ones((n_ctx, RET_DIM), np.float32), cos], axis=0)
    sin = np.concatenate([np.zeros((n_ctx, RET_DIM), np.float32), sin], axis=0)
    return jnp.asarray(cos, F32), jnp.asarray(sin, F32)


def _na_tables():
    q = np.arange(GRID_W)[:, None]
    k = np.arange(GRID_W)[None, :]
    start = np.clip(q - NA_KW // 2, 0, GRID_W - NA_KW)
    valid = (k >= start) & (k < start + NA_KW)
    dc = np.clip(k - q + (NA_KW - 1), 0, 2 * NA_KW - 2)
    ncls = 2 * NA_KW - 1
    onehot = (dc[None] == np.arange(ncls)[:, None, None]) & valid[None]
    return onehot.astype(np.float32), valid


def _paired_bias(rpb, onehot, valid):
    ncls = onehot.shape[0]
    pair = np.zeros((2 * ncls, GRID_W, LANES), np.float32)
    pair[:ncls, :, :GRID_W] = onehot
    pair[ncls:, :, GRID_W:] = onehot
    rows = jnp.concatenate([rpb[:, :-1], rpb[:, 1:]], axis=-1)
    t = jnp.einsum("hdc,cqk->hdqk", rows, jnp.asarray(pair), precision=lax.Precision.HIGHEST)
    return jnp.where(jnp.asarray(np.tile(valid, (1, 2)))[None, None], t, NEG_INF)


def kernel(x, c, ctx, c_ctx, w_ada, b_ada, g_pre_mix, g_post_mix, g_pre_mlp, g_post_mlp, w_in, ret_decay, ret_gn, na_rpb, w_out, w_mlp1, w_mlp2, loss_target, m_c_ctx, m_w_ada, m_b_ada, m_g_pre_mix, m_g_post_mix, m_g_pre_mlp, m_g_post_mlp, m_w_in, m_ret_decay, m_ret_gn, m_na_rpb, m_w_out, m_w_mlp1, m_w_mlp2, v_c_ctx, v_w_ada, v_b_ada, v_g_pre_mix, v_g_post_mix, v_g_pre_mlp, v_g_post_mlp, v_w_in, v_ret_decay, v_ret_gn, v_na_rpb, v_w_out, v_w_mlp1, v_w_mlp2):
    B, N, D = x.shape
    C = ctx.shape[1]
    T = C + N

    silu_all, mods_g, win_b, wout_l, w1_l, w2_l = _mod_gather(c, c_ctx, w_ada[0], b_ada, w_in[0].T, w_out[0],
                                                             w_mlp1[0], w_mlp2[0])
    mods_mine = mods_g.transpose(1, 0, 2).reshape(mods_g.shape[1], N_MOD * D)
    modl = jnp.concatenate([mods_mine[:B], mods_mine[SUBLANES:SUBLANES + 1]], axis=0)
    modl = modl.reshape(B + 1, N_MOD, 1, D)
    rin = w_in.shape[2]
    rout, c1, r2 = wout_l.shape[0], w1_l.shape[1], w2_l.shape[0]

    def rows_of(n):
        return lambda ref: _row_block(ref, n)

    def cols_of(n):
        return lambda ref: _col_block(ref, n)

    cos, sin = _rope_tables(C, N)
    onehot, valid = _na_tables()
    bias2 = _paired_bias(na_rpb[0], onehot, valid)
    lg = jax.nn.log_sigmoid(ret_decay[0].astype(F32))

    ag = _SplitScatter([wout_l, w1_l, w2_l], [rows_of(rout), cols_of(c1), rows_of(r2)],
                       [(N_DEV * rout, D), (D, N_DEV * c1), (N_DEV * r2, D)], "ag_mlp_start",
                       kind="gather", masks=SIBLING + ICI_SAME_CORE)
    h_all, proj = _inproj_fwd(x, ctx, modl, g_pre_mix, win_b, after=ag.token)
    o_ret, lat_ret, q_rot, k_rot = _ret_fwd(proj, cos, sin, lg, ret_gn, C)
    lat_na, na_probs = _na_fwd(proj, bias2, C)
    wout_part, w1_part, w2_part = _scatter_wait([ag], lat_na, "ag_mlp_wait")

    (dy1, dlat_ret, dlat_na, dmix, h2, act, du, dz, red_d) = _dense_core(
        lat_ret, lat_na, x, loss_target, modl, g_post_mix, g_pre_mlp, g_post_mlp, wout_part, w1_part, w2_part)

    gw_out_p = _tn_matmul(lat_ret[:, None], dmix, "gw_out_ret", rows_after=lat_na.shape[-1])
    gw_out_p = _tn_matmul(lat_na[:, None], dmix, "gw_out_na", rows_before=lat_ret.shape[-1], into=gw_out_p)
    gw1_p = _tn_matmul(h2[:, None], du, "gw_mlp1")
    gw2_p = _tn_matmul(act[:, None], dz, "gw_mlp2")
    rs_mlp = _SplitScatter([gw_out_p, gw1_p, gw2_p], [rows_of(rout), cols_of(c1), rows_of(r2)],
                           [(rout, D), (D, c1), (r2, D)], "rs_mlp_start")

    dret, dgn_p, dlg_p = _ret_bwd(proj, q_rot, k_rot, cos, sin, lg, ret_gn, o_ret, dlat_ret, C, after=rs_mlp.token)
    dna, rr = _na_bwd(proj, na_probs, dlat_na, C)
    ret_cols, na_cols = dret.shape[1] * dret.shape[3], dna.shape[1] * dna.shape[3]
    gwin_t_p = _tn_matmul(dret, h_all, "gw_in_ret", rows_after=na_cols)
    gwin_t_p = _tn_matmul(dna, h_all, "gw_in_na", rows_before=ret_cols, into=gwin_t_p)
    rs_in = _SplitScatter([gwin_t_p], [rows_of(rin)], [(rin, D)], "rs_w_in_start")
    grad_x, red_i = _inproj_bwd(dret, dna, x, ctx, dy1, modl, g_pre_mix, win_b, after=rs_in.token)

    rd = red_d
    nct = red_i.shape[1] * C // T
    ri_ctx = red_i[:, :nct].sum(axis=(0, 1))
    ri_lat = red_i[:, nct:].sum(axis=1)
    d_mods = jnp.concatenate([ri_lat[:, 0], ri_lat[:, 1], rd[:, 0], rd[:, 4], rd[:, 3], rd[:, 2]], axis=-1)
    d_cmods = jnp.concatenate([ri_ctx[0], ri_ctx[1], jnp.zeros(((N_MOD - 2) * D,), F32)])[None]
    dg_pre_mix = ri_lat[:, 2].sum(axis=0) + ri_ctx[2]
    dg_post_mix = rd[:, 1].sum(axis=0)
    dg_pre_mlp = rd[:, 5].sum(axis=0)
    dg_post_mlp = rd[:, 6].sum(axis=0)
    loss_p = rd[:, 7, 0].sum()
    d_gn = dgn_p[:, 0].sum(axis=0)
    d_lg = dlg_p[:, :, :2, 0].sum(axis=0).T
    d_decay = d_lg * jax.nn.sigmoid(-ret_decay[0].astype(F32))
    ncls = 2 * NA_KW - 1
    d_rpb = (jnp.pad(rr[:, :, :ncls], ((0, 0), (0, 1), (0, 0)))
             + jnp.pad(rr[:, :, GRID_W:GRID_W + ncls], ((0, 0), (1, 0), (0, 0))))
    d_rpb32 = jnp.pad(d_rpb, ((0, 0), (0, 0), (0, 32 - ncls)))
    pieces = [dg_pre_mix, dg_post_mix, dg_pre_mlp, dg_post_mlp, d_gn, d_rpb32.reshape(-1),
              jnp.pad(d_decay.reshape(-1), (0, LANES - d_decay.size)), jnp.full((LANES,), loss_p, F32)]
    vec = jnp.concatenate(pieces)
    nm = N_MOD * D
    n_vec_rows = -(-vec.shape[0] // nm)
    assert B + 1 + n_vec_rows <= SUBLANES
    vec = jnp.pad(vec, (0, n_vec_rows * nm - vec.shape[0])).reshape(n_vec_rows, nm)
    dm_slot = jnp.concatenate([d_mods, d_cmods, vec, jnp.zeros((SUBLANES - B - 1 - n_vec_rows, nm), F32)], axis=0)
    def whole(ref):
        return lambda p: ref

    small = _SplitScatter([dm_slot], [whole], [dm_slot.shape], "small_start")
    land_out, land_1, land_2 = _scatter_wait([rs_mlp], small.token, "rs_mlp_wait")
    fused = {"w_out": _sum_adamw(land_out, w_out[0], m_w_out[0], v_w_out[0], "sum_adamw_w_out"),
             "w_mlp1": _sum_adamw(land_1, w_mlp1[0], m_w_mlp1[0], v_w_mlp1[0], "sum_adamw_w_mlp1"),
             "w_mlp2": _sum_adamw(land_2, w_mlp2[0], m_w_mlp2[0], v_w_mlp2[0], "sum_adamw_w_mlp2")}
    (land_in,) = _scatter_wait([rs_in], fused["w_mlp2"][0], "rs_w_in_wait")
    win_upd = _sum_adamw(land_in, w_in[0].T, m_w_in[0].T, v_w_in[0].T, "sum_adamw_w_in")
    fused["w_in"] = [a.T for a in win_upd]
    (mbuf,) = _scatter_wait([small], win_upd[0], "small_wait")
    tot, g_b_ada, g_w_ada, g_c_ctx = _small_ar(mbuf, silu_all, w_ada[0], c_ctx, B + 1, n_vec_rows)
    flat = tot.reshape(-1)
    o0 = 0
    g_pre_mix_g = flat[o0:o0 + D]; o0 += D
    g_post_mix_g = flat[o0:o0 + D]; o0 += D
    g_pre_mlp_g = flat[o0:o0 + D]; o0 += D
    g_post_mlp_g = flat[o0:o0 + D]; o0 += D
    g_gn = flat[o0:o0 + RET_WIDTH]; o0 += RET_WIDTH
    nrpb = NA_HEADS * (2 * NA_KH - 1) * 32
    g_rpb = flat[o0:o0 + nrpb].reshape(NA_HEADS, 2 * NA_KH - 1, 32)[:, :, :ncls]; o0 += nrpb
    g_decay = flat[o0:o0 + 2 * RET_HEADS].reshape(2, RET_HEADS); o0 += LANES
    loss = flat[o0]

    grads = {
        "c_ctx": g_c_ctx.reshape(c_ctx.shape), "w_ada": g_w_ada[None], "b_ada": g_b_ada.reshape(b_ada.shape),
        "g_pre_mix": g_pre_mix_g[None], "g_post_mix": g_post_mix_g[None], "g_pre_mlp": g_pre_mlp_g[None],
        "g_post_mlp": g_post_mlp_g[None], "w_in": fused["w_in"][0][None], "ret_decay": g_decay[None], "ret_gn": g_gn[None],
        "na_rpb": g_rpb[None], "w_out": fused["w_out"][0][None], "w_mlp1": fused["w_mlp1"][0][None],
        "w_mlp2": fused["w_mlp2"][0][None],
    }
    weights = dict(c_ctx=c_ctx, w_ada=w_ada, b_ada=b_ada, g_pre_mix=g_pre_mix, g_post_mix=g_post_mix,
                   g_pre_mlp=g_pre_mlp, g_post_mlp=g_post_mlp, w_in=w_in, ret_decay=ret_decay, ret_gn=ret_gn,
                   na_rpb=na_rpb, w_out=w_out, w_mlp1=w_mlp1, w_mlp2=w_mlp2)
    m_in = dict(c_ctx=m_c_ctx, w_ada=m_w_ada, b_ada=m_b_ada, g_pre_mix=m_g_pre_mix, g_post_mix=m_g_post_mix,
                g_pre_mlp=m_g_pre_mlp, g_post_mlp=m_g_post_mlp, w_in=m_w_in, ret_decay=m_ret_decay,
                ret_gn=m_ret_gn, na_rpb=m_na_rpb, w_out=m_w_out, w_mlp1=m_w_mlp1, w_mlp2=m_w_mlp2)
    v_in = dict(c_ctx=v_c_ctx, w_ada=v_w_ada, b_ada=v_b_ada, g_pre_mix=v_g_pre_mix, g_post_mix=v_g_post_mix,
                g_pre_mlp=v_g_pre_mlp, g_post_mlp=v_g_post_mlp, w_in=v_w_in, ret_decay=v_ret_decay,
                ret_gn=v_ret_gn, na_rpb=v_na_rpb, w_out=v_w_out, w_mlp1=v_w_mlp1, w_mlp2=v_w_mlp2)
    names = list(weights)
    deltas, new_m, new_v = {}, {}, {}
    def as_2d(n):
        shp = weights[n].shape
        two_d = (-1, shp[-1]) if len(shp) > 1 else (1, shp[0])
        return [a.reshape(two_d) for a in (weights[n], grads[n], m_in[n], v_in[n])]

    small = [n for n in names if n not in fused and weights[n].size <= 65536]
    updated = dict(zip(small, _adamw_small([as_2d(n) for n in small], "adamw_small")))
    for n in names:
        if n in fused:
            updated[n] = fused[n][1:]
        elif n not in updated:
            updated[n] = _adamw(*as_2d(n), "adamw_" + n)
        deltas[n], new_m[n], new_v[n] = (a.reshape(weights[n].shape) for a in updated[n])
    return (loss, grad_x, *[grads[n] for n in names], *[deltas[n] for n in names],
            *[new_m[n] for n in names], *[new_v[n] for n in names])
```

```python
import functools

import numpy as np
import jax
import jax.numpy as jnp
from jax import lax
from jax.experimental import pallas as pl
from jax.experimental.pallas import tpu as pltpu

F32 = jnp.float32
BF16 = jnp.bfloat16
MESH = pl.DeviceIdType.MESH

N_DEV = 8
LANES = 128
SUBLANES = 8
VMEM_LIMIT = 60 * 1024 * 1024

GRID_W = 64
RET_HEADS = 4
RET_DIM = 128
RET_WIDTH = RET_HEADS * RET_DIM
NA_HEADS = 8
NA_DIM = 64
NA_WIDTH = NA_HEADS * NA_DIM
NA_PAIRS = NA_HEADS // 2
NA_KH = 8
NA_KW = 16
NA_GROUP = 8
SEG = 512
ROPE_BASE = 10000.0
NORM_EPS = 1e-6
NEG_INF = -1e30
N_MOD = 6

ADAM_LR = 0.001
ADAM_B1 = 0.9
ADAM_B2 = 0.999
ADAM_EPS = 1e-08
ADAM_WD = 0.01
ADAM_STEP = 10


def _dot(a, b):
    return lax.dot_general(a, b, (((1,), (0,)), ((), ())), preferred_element_type=F32)


def _dot_nt(a, b):
    return lax.dot_general(a, b, (((1,), (1,)), ((), ())), preferred_element_type=F32)


def _dot_tn(a, b):
    return lax.dot_general(a, b, (((0,), (0,)), ((), ())), preferred_element_type=F32)


def _sigmoid(x):
    return 1.0 / (1.0 + jnp.exp(-x))


def _div_tile(n, cap, mult):
    if n <= cap:
        return n
    for t in range(cap - cap % mult, 0, -mult):
        if n % t == 0:
            return t
    raise ValueError(f"no tile for {n}")


def _params(*sem):
    return pltpu.CompilerParams(dimension_semantics=tuple(sem) if sem else None,
                                vmem_limit_bytes=VMEM_LIMIT)


def _vmem():
    return pl.BlockSpec(memory_space=pltpu.VMEM)


def _any():
    return pl.BlockSpec(memory_space=pl.ANY)


def _me_and_peers():
    x, y, c = lax.axis_index("x"), lax.axis_index("y"), lax.axis_index("c")
    me = 4 * x + 2 * y + c
    peers = []
    for m in range(1, N_DEV):
        px = 1 - x if (m >> 2) & 1 else x
        py = 1 - y if (m >> 1) & 1 else y
        pc = 1 - c if m & 1 else c
        peers.append(((px, py, pc), 4 * px + 2 * py + pc))
    return me, peers


def _exchange(src_for, dst_from, send_sems, recv_sems):
    me, peers = _me_and_peers()
    sent = []
    for i, (dev, pid) in enumerate(peers):
        cp = pltpu.make_async_remote_copy(src_ref=src_for(pid), dst_ref=dst_from(me),
                                          send_sem=send_sems.at[i], recv_sem=recv_sems.at[i],
                                          device_id=dev, device_id_type=MESH)
        cp.start()
        sent.append(cp)
    for i, (dev, pid) in enumerate(peers):
        pltpu.make_async_remote_copy(src_ref=src_for(pid), dst_ref=dst_from(pid),
                                     send_sem=send_sems.at[i], recv_sem=recv_sems.at[i],
                                     device_id=dev, device_id_type=MESH).wait_recv()
    for cp in sent:
        cp.wait_send()


SIBLING = (1,)
ICI_SAME_CORE = (2, 4, 6)
ALL_PEERS = tuple(range(1, N_DEV))


def _remote(src, dst, send_sem, recv_sem, dev):
    return pltpu.make_async_remote_copy(src_ref=src, dst_ref=dst, send_sem=send_sem, recv_sem=recv_sem,
                                        device_id=dev, device_id_type=MESH)


def _push_start(items, masks, send_sems, recv_sems):
    me, peers = _me_and_peers()
    for k, (src_for, dst_from) in enumerate(items):
        for m in masks:
            dev, pid = peers[m - 1]
            _remote(src_for(pid), dst_from(me), send_sems.at[k, m - 1], recv_sems.at[k, m - 1], dev).start()


def _push_wait_recv(items, masks, send_sems, recv_sems):
    me, peers = _me_and_peers()
    for k, (src_for, dst_from) in enumerate(items):
        for m in masks:
            dev, pid = peers[m - 1]
            _remote(src_for(pid), dst_from(pid), send_sems.at[k, m - 1], recv_sems.at[k, m - 1], dev).wait_recv()


def _push_wait_send(items, masks, send_sems, recv_sems):
    me, peers = _me_and_peers()
    for k, (src_for, dst_from) in enumerate(items):
        for m in masks:
            dev, pid = peers[m - 1]
            _remote(src_for(pid), dst_from(me), send_sems.at[k, m - 1], recv_sems.at[k, m - 1], dev).wait_send()


def _forward_start(items, send_sems, recv_sems):
    me, peers = _me_and_peers()
    sib = peers[0][0]
    for k, (blk_in, blk_out) in enumerate(items):
        for j, m in enumerate(ICI_SAME_CORE):
            pid = peers[m - 1][1]
            _remote(blk_in(pid), blk_out(pid), send_sems.at[k, j], recv_sems.at[k, j], sib).start()


def _forward_wait(items, send_sems, recv_sems):
    me, peers = _me_and_peers()
    sib = peers[0][0]
    for k, (blk_in, blk_out) in enumerate(items):
        for j, m in enumerate(ICI_SAME_CORE):
            got = peers[(m | 1) - 1][1]
            _remote(blk_in(got), blk_out(got), send_sems.at[k, j], recv_sems.at[k, j], sib).wait_recv()
    for k, (blk_in, blk_out) in enumerate(items):
        for j, m in enumerate(ICI_SAME_CORE):
            pid = peers[m - 1][1]
            _remote(blk_in(pid), blk_out(pid), send_sems.at[k, j], recv_sems.at[k, j], sib).wait_send()


def _mod_gather(c, c_ctx, w_ada, b_ada, w_in_t, w_out, w1, w2):
    B, D = c.shape
    ncol = w_ada.shape[1]
    rows = SUBLANES * N_DEV + SUBLANES

    def body(c_ref, cc_ref, w_ref, b_ref, win_ref, wout_ref, w1_ref, w2_ref,
             s_ref, m_ref, gin_ref, wout_b, w1_b, w2_b,
             win_b, msend, send1, recv1, send2, recv2, wsend, wrecv, fsend, frecv, lsem):
        me, _ = _me_and_peers()
        win_b[...] = win_ref[...].astype(BF16)
        block = _row_block(gin_ref, w_in_t.shape[0])
        gather = [(lambda p: win_b, block)]
        own = pltpu.make_async_copy(win_b, block(me), lsem.at[0])
        cv = c_ref[...]
        slot = jnp.concatenate([cv * _sigmoid(cv), jnp.zeros((SUBLANES - B, D), F32)], axis=0)
        my_rows = pl.ds(pl.multiple_of(me * SUBLANES, SUBLANES), SUBLANES)
        s_ref[my_rows, :] = slot
        ccv = cc_ref[...]
        s_ref[SUBLANES * N_DEV:, :] = jnp.concatenate(
            [ccv * _sigmoid(ccv), jnp.zeros((SUBLANES - 1, D), F32)], axis=0)

        def rows_of(p):
            return s_ref.at[pl.ds(pl.multiple_of(p * SUBLANES, SUBLANES), SUBLANES), :]

        _exchange(lambda p: rows_of(me), rows_of, send1, recv1)
        own.start()
        _push_start(gather, SIBLING + ICI_SAME_CORE, wsend, wrecv)
        wout_b[...] = wout_ref[...].astype(BF16)
        w1_b[...] = w1_ref[...].astype(BF16)
        w2_b[...] = w2_ref[...].astype(BF16)
        b_loc = b_ref[:, pl.ds(pl.multiple_of(me * ncol, ncol), ncol)]
        mods = _dot(s_ref[...], w_ref[...]) + b_loc
        for p in range(N_DEV):
            msend[p] = jnp.concatenate([mods[p * SUBLANES:(p + 1) * SUBLANES], mods[N_DEV * SUBLANES:]], axis=0)
        m_ref[me] = msend[me]
        columns = [(lambda p: msend.at[p], lambda p: m_ref.at[p])]
        _push_start(columns, ALL_PEERS, send2, recv2)
        _push_wait_recv(gather, ICI_SAME_CORE, wsend, wrecv)
        relay = [(block, block)]
        _forward_start(relay, fsend, frecv)
        _push_wait_recv(columns, ALL_PEERS, send2, recv2)
        _push_wait_recv(gather, SIBLING, wsend, wrecv)
        _forward_wait(relay, fsend, frecv)
        _push_wait_send(columns, ALL_PEERS, send2, recv2)
        _push_wait_send(gather, SIBLING + ICI_SAME_CORE, wsend, wrecv)
        own.wait()

    return pl.pallas_call(
        body, name="mod_gather",
        out_shape=(jax.ShapeDtypeStruct((rows, D), F32), jax.ShapeDtypeStruct((N_DEV, 2 * SUBLANES, ncol), F32),
                   jax.ShapeDtypeStruct((N_DEV * w_in_t.shape[0], D), BF16),
                   jax.ShapeDtypeStruct(w_out.shape, BF16), jax.ShapeDtypeStruct(w1.shape, BF16),
                   jax.ShapeDtypeStruct(w2.shape, BF16)),
        in_specs=[_vmem()] * 8, out_specs=(_vmem(), _vmem(), _any(), _vmem(), _vmem(), _vmem()),
        scratch_shapes=[pltpu.VMEM(w_in_t.shape, BF16), pltpu.VMEM((N_DEV, 2 * SUBLANES, ncol), F32)]
                       + [pltpu.SemaphoreType.DMA((N_DEV - 1,))] * 2
                       + [pltpu.SemaphoreType.DMA((1, N_DEV - 1))] * 4 + [pltpu.SemaphoreType.DMA((1, 3))] * 2
                       + [pltpu.SemaphoreType.DMA((1,))],
        compiler_params=pltpu.CompilerParams(vmem_limit_bytes=VMEM_LIMIT),
    )(c, c_ctx.reshape(1, D), w_ada, b_ada, w_in_t, w_out, w1, w2)


def _row_block(ref, rows):
    return lambda p: ref.at[pl.ds(pl.multiple_of(p * rows, 2 * SUBLANES), rows), :]


def _col_block(ref, cols):
    return lambda p: ref.at[:, pl.ds(pl.multiple_of(p * cols, LANES), cols)]


def _slot(ref):
    return lambda p: ref.at[p]


def _grid_call(body, *, name, grid, out_shape, in_specs, out_specs, scratch_shapes, args, after=None):
    n_in = len(args)

    def ordered_body(*refs):
        body(*refs[:n_in], *refs[n_in + 1:])

    return pl.pallas_call(
        body if after is None else ordered_body, name=name, grid=grid, out_shape=tuple(out_shape),
        in_specs=list(in_specs) + ([] if after is None else [_any()]), out_specs=tuple(out_specs),
        scratch_shapes=list(scratch_shapes), compiler_params=_params(*(("arbitrary",) * len(grid))),
    )(*args, *([] if after is None else [after]))


def _token_tiles(n_ctx, tm):
    nct = n_ctx // tm

    def ctx_spec(D):
        return pl.BlockSpec((None, tm, D), lambda b, t: (b, jnp.minimum(t, nct - 1), 0))

    def lat_spec(D):
        return pl.BlockSpec((None, tm, D), lambda b, t: (b, jnp.maximum(t - nct, 0), 0))

    return nct, ctx_spec, lat_spec


def _inproj_fwd(x, ctx, modl, g1, w_in_t, after):
    B, N, D = x.shape
    n_ctx = ctx.shape[1]
    T = n_ctx + N
    nw = w_in_t.shape[0]
    tm = _div_tile(n_ctx, 256, 16)
    nct, ctx_spec, lat_spec = _token_tiles(n_ctx, tm)

    def body(c_ref, x_ref, sh_ref, sc_ref, g_ref, w_ref, h_ref, p_ref):
        x = jnp.where(pl.program_id(1) < nct, c_ref[...], x_ref[...])
        r = lax.rsqrt(jnp.mean(x * x, axis=-1, keepdims=True) + NORM_EPS)
        h = ((x * r) * g_ref[...]) * (1.0 + sc_ref[...]) + sh_ref[...]
        hb = h.astype(BF16)
        h_ref[...] = hb
        p_ref[...] = _dot_nt(hb, w_ref[...]).astype(BF16)

    def mrow(b, t):
        return jnp.where(t < nct, B, b)

    return _grid_call(
        body, name="inproj_fwd", grid=(B, T // tm),
        out_shape=(jax.ShapeDtypeStruct((B, T, D), BF16), jax.ShapeDtypeStruct((B, T, nw), BF16)),
        in_specs=[ctx_spec(D), lat_spec(D),
                  pl.BlockSpec((None, None, 1, D), lambda b, t: (mrow(b, t), 0, 0, 0)),
                  pl.BlockSpec((None, None, 1, D), lambda b, t: (mrow(b, t), 1, 0, 0)),
                  pl.BlockSpec((1, D), lambda b, t: (0, 0)),
                  pl.BlockSpec((nw, D), lambda b, t: (0, 0))],
        out_specs=(pl.BlockSpec((None, tm, D), lambda b, t: (b, t, 0)),
                   pl.BlockSpec((None, tm, nw), lambda b, t: (b, t, 0))),
        scratch_shapes=[], args=(ctx, x, modl, modl, g1, w_in_t), after=after)


def _swap32(x):
    lane = lax.broadcasted_iota(jnp.int32, x.shape, 1)
    return jnp.where((lane % 64) < 32, pltpu.roll(x, 96, 1), pltpu.roll(x, 32, 1))


def _rope(x, cos, sin):
    return x * cos + _swap32(x) * sin


def _unrope(dy, cos, sin):
    return dy * cos + _swap32(dy * sin)


def _ret_weights(lgf, lgb, dist):
    return jnp.exp(jnp.where(dist >= 0.0, lgf * dist, -lgb * dist))


class _RetDecay:
    def __init__(self, lgf, lgb, rows):
        r = lax.broadcasted_iota(jnp.int32, (rows, RET_DIM), 0).astype(F32)
        self.head = r + 1.0
        self.tail = (rows - 1.0) - r
        self.q_f = jnp.exp(lgf * self.head)
        self.k_f = jnp.exp(lgf * self.tail)
        self.q_b = jnp.exp(lgb * self.tail)
        self.k_b = jnp.exp(lgb * self.head)


def _ret_states(kf32, vs, lgf, lgb, C, c, nt, hf, hb, hfa=None, hba=None):
    dec = _RetDecay(lgf, lgb, c)
    dec_c = _RetDecay(lgf, lgb, C)
    step_f = jnp.exp(jnp.zeros((RET_DIM, RET_DIM), F32) + lgf * c)
    step_b = jnp.exp(jnp.zeros((RET_DIM, RET_DIM), F32) + lgb * c)

    def upd(rows, kdec):
        return _dot_tn((kf32[rows, :] * kdec).astype(BF16), vs[rows, :])

    def lat(t):
        return slice(C + t * c, C + (t + 1) * c)

    state = upd(slice(0, C), dec_c.k_f)
    aged = jnp.zeros_like(state)
    for t in range(nt):
        hf[t] = state.astype(BF16)
        if hfa is not None:
            hfa[t] = aged
        if t < nt - 1:
            aged = step_f * (aged + c * state)
            state = step_f * state + upd(lat(t), dec.k_f)
    state = upd(slice(0, C), dec_c.k_b)
    aged = jnp.zeros_like(state)
    for t in range(nt - 1, -1, -1):
        hb[t] = state.astype(BF16)
        if hba is not None:
            hba[t] = aged
        if t > 0:
            aged = step_b * (aged + c * state)
            state = step_b * state + upd(lat(t), dec.k_b)
    return dec, dec_c, step_f, step_b


def _ret_fwd(proj, cos, sin, lg, gn, n_ctx):
    B, T, _ = proj.shape
    C = n_ctx
    N = T - C
    c = _div_tile(N, 256, 16)
    nt = N // c
    scale = RET_DIM ** -0.5

    def body(lg_ref, q_ref, k_ref, vs, g_ref, cos_ref, sin_ref, gn_ref, o_ref, lat_ref, qr_ref, kf32,
             qs, ks, hf, hb):
        h = pl.program_id(1)
        lgf = lg_ref[0, h]
        lgb = lg_ref[1, h]
        for rows in [slice(0, C)] + [slice(C + t * c, C + (t + 1) * c) for t in range(nt)]:
            cosb = cos_ref[rows, :]
            sinb = sin_ref[rows, :]
            qr = _rope(q_ref[rows, :].astype(F32), cosb, sinb) * scale
            qr_ref[rows, :] = qr
            qs[rows, :] = qr.astype(BF16)
            kr = _rope(k_ref[rows, :].astype(F32), cosb, sinb)
            kf32[rows, :] = kr
            ks[rows, :] = kr.astype(BF16)
        gnv = gn_ref[...]
        dec, _, _, _ = _ret_states(kf32, vs, lgf, lgb, C, c, nt, hf, hb)
        rc = (lax.broadcasted_iota(jnp.int32, (c, c), 0) - lax.broadcasted_iota(jnp.int32, (c, c), 1)).astype(F32)
        w_diag = _ret_weights(lgf, lgb, rc)
        for t in range(nt):
            rows = slice(C + t * c, C + (t + 1) * c)
            qt = qs[rows, :]
            s = _dot_nt(qt, ks[rows, :])
            o = (_dot((s * w_diag).astype(BF16), vs[rows, :])
                 + dec.q_f * _dot(qt, hf[t]) + dec.q_b * _dot(qt, hb[t]))
            o_ref[t * c:(t + 1) * c, :] = o
            mu = jnp.mean(o, axis=-1, keepdims=True)
            oc = o - mu
            var = jnp.mean(oc * oc, axis=-1, keepdims=True)
            yh = oc * lax.rsqrt(var + NORM_EPS)
            g = g_ref[rows, :].astype(F32)
            lat_ref[t * c:(t + 1) * c, :] = ((yh * gnv) * (g * _sigmoid(g))).astype(BF16)

    def col(seg):
        return pl.BlockSpec((None, T, RET_DIM), lambda b, h, seg=seg: (b, 0, seg * RET_HEADS + h))

    return _grid_call(
        body, name="ret_fwd", grid=(B, RET_HEADS),
        out_shape=(jax.ShapeDtypeStruct((B, N, RET_WIDTH), F32), jax.ShapeDtypeStruct((B, N, RET_WIDTH), BF16),
                   jax.ShapeDtypeStruct((B, T, RET_WIDTH), F32), jax.ShapeDtypeStruct((B, T, RET_WIDTH), F32)),
        in_specs=[pl.BlockSpec(memory_space=pltpu.SMEM), col(0), col(1), col(2), col(3),
                  pl.BlockSpec((T, RET_DIM), lambda b, h: (0, 0)), pl.BlockSpec((T, RET_DIM), lambda b, h: (0, 0)),
                  pl.BlockSpec((1, RET_DIM), lambda b, h: (0, h))],
        out_specs=(pl.BlockSpec((None, N, RET_DIM), lambda b, h: (b, 0, h)),
                   pl.BlockSpec((None, N, RET_DIM), lambda b, h: (b, 0, h)),
                   pl.BlockSpec((None, T, RET_DIM), lambda b, h: (b, 0, h)),
                   pl.BlockSpec((None, T, RET_DIM), lambda b, h: (b, 0, h))),
        scratch_shapes=[pltpu.VMEM((T, RET_DIM), BF16)] * 2 + [pltpu.VMEM((nt, RET_DIM, RET_DIM), BF16)] * 2,
        args=(lg, proj, proj, proj, proj, cos, sin, gn))


def _ret_bwd(proj, q_rot, k_rot, cos, sin, lg, gn, o, dlat, n_ctx, after):
    B, T, _ = proj.shape
    C = n_ctx
    N = T - C
    c = _div_tile(N, 256, 16)
    nt = N // c
    scale = RET_DIM ** -0.5

    def lat(t):
        return slice(C + t * c, C + (t + 1) * c)

    def body(lg_ref, qf32, kf32, vs, g_ref, cos_ref, sin_ref, gn_ref, o_ref, dl_ref,
             d_ref, dgn_ref, dlg_ref, qs, ks, dos, hf, hb, hfa, hba, gf_s, gb_s):
        h = pl.program_id(1)
        lgf = lg_ref[0, h]
        lgb = lg_ref[1, h]
        gnv = gn_ref[...]

        def fold(a):
            return jnp.sum(a.reshape(a.shape[0] // SUBLANES, SUBLANES, a.shape[1]), axis=0)

        for rows in [slice(0, C)] + [lat(t) for t in range(nt)]:
            qs[rows, :] = qf32[rows, :].astype(BF16)
            ks[rows, :] = kf32[rows, :].astype(BF16)

        dgn = jnp.zeros((1, RET_DIM), F32)
        for t in range(nt):
            lrows = slice(t * c, (t + 1) * c)
            ov = o_ref[lrows, :]
            mu = jnp.mean(ov, axis=-1, keepdims=True)
            oc = ov - mu
            var = jnp.mean(oc * oc, axis=-1, keepdims=True)
            rstd = lax.rsqrt(var + NORM_EPS)
            yh = oc * rstd
            g = g_ref[lat(t), :].astype(F32)
            sg = _sigmoid(g)
            dl = dl_ref[lrows, :]
            d_ref[3, lat(t), :] = (dl * (yh * gnv) * (sg * (1.0 + g * (1.0 - sg)))).astype(BF16)
            dls = dl * (g * sg)
            dgn = dgn + jnp.sum(dls * yh, axis=0, keepdims=True)
            dyh = dls * gnv
            do = rstd * (dyh - jnp.mean(dyh, axis=-1, keepdims=True)
                         - yh * jnp.mean(dyh * yh, axis=-1, keepdims=True))
            dos[lrows, :] = do.astype(BF16)
        dgn_ref[...] = jnp.concatenate([dgn, jnp.zeros((SUBLANES - 1, RET_DIM), F32)], axis=0)
        d_ref[3, 0:C, :] = jnp.zeros((C, RET_DIM), BF16)
        d_ref[0, 0:C, :] = jnp.zeros((C, RET_DIM), BF16)

        dec, dec_c, step_f, step_b = _ret_states(kf32, vs, lgf, lgb, C, c, nt, hf, hb, hfa, hba)

        def zmat(t, qdec):
            return _dot_tn((qf32[lat(t), :] * qdec).astype(BF16), dos[t * c:(t + 1) * c, :])

        acc3f = jnp.zeros((RET_DIM, RET_DIM), F32)
        acc3b = jnp.zeros((RET_DIM, RET_DIM), F32)
        state = jnp.zeros((RET_DIM, RET_DIM), F32)
        for t in range(nt - 1, -1, -1):
            gf_s[t] = state.astype(BF16)
            z = zmat(t, dec.q_f)
            acc3f = acc3f + hfa[t] * z
            state = step_f * state + z
        gctx_f = state.astype(BF16)
        state = jnp.zeros((RET_DIM, RET_DIM), F32)
        for t in range(nt):
            gb_s[t] = state.astype(BF16)
            z = zmat(t, dec.q_b)
            acc3b = acc3b + hba[t] * z
            state = step_b * state + z
        gctx_b = state.astype(BF16)

        rc = (lax.broadcasted_iota(jnp.int32, (c, c), 0) - lax.broadcasted_iota(jnp.int32, (c, c), 1)).astype(F32)
        w_diag = _ret_weights(lgf, lgb, rc)
        wg_f = jnp.where(rc >= 0.0, w_diag * rc, 0.0)
        wg_b = jnp.where(rc < 0.0, -w_diag * rc, 0.0)
        accf = jnp.zeros((SUBLANES, RET_DIM), F32)
        accb = jnp.zeros((SUBLANES, RET_DIM), F32)
        gdf = jnp.zeros((SUBLANES, c), F32)
        gdb = jnp.zeros((SUBLANES, c), F32)
        for t in range(nt):
            rows = lat(t)
            qt = qs[rows, :]
            kt = ks[rows, :]
            vt = vs[rows, :]
            dot = dos[t * c:(t + 1) * c, :]
            s = _dot_nt(qt, kt)
            dp = _dot_nt(dot, vt)
            dv = _dot_tn((s * w_diag).astype(BF16), dot)
            ds = (dp * w_diag).astype(BF16)
            dq = _dot(ds, kt)
            dk = _dot_tn(ds, qt)
            gs = dp * s
            gdf = gdf + fold(gs * wg_f)
            gdb = gdb + fold(gs * wg_b)
            qv = qf32[rows, :]
            kv = kf32[rows, :]
            dq_f = dec.q_f * _dot_nt(dot, hf[t])
            dq_b = dec.q_b * _dot_nt(dot, hb[t])
            dk_f = dec.k_f * _dot_nt(vt, gf_s[t])
            dk_b = dec.k_b * _dot_nt(vt, gb_s[t])
            accf = accf + fold(dec.head * dq_f * qv) + fold(dec.tail * dk_f * kv)
            accb = accb + fold(dec.tail * dq_b * qv) + fold(dec.head * dk_b * kv)
            dv = dv + dec.k_f * _dot(kt, gf_s[t]) + dec.k_b * _dot(kt, gb_s[t])
            cosb = cos_ref[rows, :]
            sinb = sin_ref[rows, :]
            d_ref[0, rows, :] = _unrope((dq + dq_f + dq_b) * scale, cosb, sinb).astype(BF16)
            d_ref[1, rows, :] = _unrope(dk + dk_f + dk_b, cosb, sinb).astype(BF16)
            d_ref[2, rows, :] = dv.astype(BF16)
        kc = ks[0:C, :]
        vc = vs[0:C, :]
        kcv = kf32[0:C, :]
        dkc_f = dec_c.k_f * _dot_nt(vc, gctx_f)
        dkc_b = dec_c.k_b * _dot_nt(vc, gctx_b)
        accf = accf + fold(dec_c.tail * dkc_f * kcv)
        accb = accb + fold(dec_c.head * dkc_b * kcv)
        d_ref[1, 0:C, :] = (dkc_f + dkc_b).astype(BF16)
        d_ref[2, 0:C, :] = (dec_c.k_f * _dot(kc, gctx_f) + dec_c.k_b * _dot(kc, gctx_b)).astype(BF16)
        gf = jnp.sum(gdf) + jnp.sum(accf) + jnp.sum(acc3f)
        gb = jnp.sum(gdb) + jnp.sum(accb) + jnp.sum(acc3b)
        row = lax.broadcasted_iota(jnp.int32, (SUBLANES, LANES), 0)
        dlg_ref[...] = jnp.where(row == 0, gf, jnp.where(row == 1, gb, 0.0))

    def col(seg):
        return pl.BlockSpec((None, T, RET_DIM), lambda b, h, seg=seg: (b, 0, seg * RET_HEADS + h))

    def head(rows):
        return pl.BlockSpec((None, rows, RET_DIM), lambda b, h: (b, 0, h))

    return _grid_call(
        body, name="ret_bwd", grid=(B, RET_HEADS),
        out_shape=(jax.ShapeDtypeStruct((B, 4, T, RET_WIDTH), BF16),
                   jax.ShapeDtypeStruct((B, SUBLANES, RET_WIDTH), F32),
                   jax.ShapeDtypeStruct((B, RET_HEADS, SUBLANES, LANES), F32)),
        in_specs=[pl.BlockSpec(memory_space=pltpu.SMEM), head(T), head(T), col(2), col(3),
                  pl.BlockSpec((T, RET_DIM), lambda b, h: (0, 0)), pl.BlockSpec((T, RET_DIM), lambda b, h: (0, 0)),
                  pl.BlockSpec((1, RET_DIM), lambda b, h: (0, h)), head(N), head(N)],
        out_specs=(pl.BlockSpec((None, 4, T, RET_DIM), lambda b, h: (b, 0, 0, h)),
                   pl.BlockSpec((None, SUBLANES, RET_DIM), lambda b, h: (b, 0, h)),
                   pl.BlockSpec((None, None, SUBLANES, LANES), lambda b, h: (b, h, 0, 0))),
        scratch_shapes=[pltpu.VMEM((T, RET_DIM), BF16)] * 2 + [pltpu.VMEM((N, RET_DIM), BF16)]
                       + [pltpu.VMEM((nt, RET_DIM, RET_DIM), BF16)] * 2 + [pltpu.VMEM((nt, RET_DIM, RET_DIM), F32)] * 2
                       + [pltpu.VMEM((nt, RET_DIM, RET_DIM), BF16)] * 2,
        args=(lg, q_rot, k_rot, proj, proj, cos, sin, gn, o, dlat), after=after)


def _na_geometry(rows):
    kh = min(NA_KH, rows)
    return kh, kh * GRID_W


def _pair_select():
    lane = lax.broadcasted_iota(jnp.int32, (2 * GRID_W, LANES), 1)
    row = lax.broadcasted_iota(jnp.int32, (2 * GRID_W, LANES), 0)
    return (lane >= NA_DIM) == (row >= GRID_W)


def _pair_bias(bias_ref, dr0, kh):
    return jnp.concatenate(
        [jnp.concatenate([bias_ref[e, pl.ds(dr0 + 2 * m, 1)].reshape(GRID_W, LANES) for m in range(kh // 2)], axis=1)
         for e in range(2)], axis=0)


def _na_softmax(s_loc, s_ctx):
    mx = jnp.maximum(jnp.max(s_loc, axis=-1, keepdims=True), jnp.max(s_ctx, axis=-1, keepdims=True))
    p_loc = jnp.exp(s_loc - mx)
    p_ctx = jnp.exp(s_ctx - mx)
    den = jnp.sum(p_loc, axis=-1, keepdims=True) + jnp.sum(p_ctx, axis=-1, keepdims=True)
    return p_loc, p_ctx, den


def _na_fwd(proj, bias2, n_ctx):
    assert proj.dtype == BF16
    B, T, _ = proj.shape
    C = n_ctx
    N = T - C
    R = N // GRID_W
    kh, nk = _na_geometry(R)
    scale = NA_DIM ** -0.5
    base = (4 * RET_WIDTH) // LANES

    def body(q_ref, kb16, vb16, bias_ref, out_ref, p_ref):
        kc = kb16[0:C, :]
        vc = vb16[0:C, :]
        lane = lax.broadcasted_iota(jnp.int32, (GRID_W, LANES), 1)
        sel2 = _pair_select()

        def group(gi, carry):
            pre = []
            for u in range(NA_GROUP):
                r = gi * NA_GROUP + u
                bs = jnp.clip(r - kh // 2, 0, R - kh)
                dr0 = bs - r + (NA_KH - 1)
                q = q_ref[pl.ds(pl.multiple_of(C + r * GRID_W, GRID_W), GRID_W), :].astype(F32) * scale
                q2 = jnp.where(sel2, jnp.concatenate([q, q], axis=0), 0.0).astype(BF16)
                band = pl.ds(pl.multiple_of(C + bs * GRID_W, GRID_W), nk)
                s_loc = _dot_nt(q2, kb16[band, :]) + _pair_bias(bias_ref, dr0, kh)
                s_ctx = _dot_nt(q2, kc)
                pre.append((r, band, s_loc, s_ctx))
            mid = [(r, band) + _na_softmax(s_loc, s_ctx) for r, band, s_loc, s_ctx in pre]
            for r, band, p_loc, p_ctx, den in mid:
                inv = 1.0 / den
                pb_loc = (p_loc * inv).astype(BF16)
                pb_ctx = (p_ctx * inv).astype(BF16)
                p_ref[r, :, 0:nk] = pb_loc
                p_ref[r, :, nk:] = pb_ctx
                o2 = _dot(pb_loc, vb16[band, :]) + _dot(pb_ctx, vc)
                out_ref[pl.ds(pl.multiple_of(r * GRID_W, GRID_W), GRID_W), :] = jnp.where(
                    lane < NA_DIM, o2[:GRID_W], o2[GRID_W:]).astype(BF16)
            return carry

        lax.fori_loop(0, R // NA_GROUP, group, 0)

    def col(seg):
        return pl.BlockSpec((None, T, LANES), lambda b, p, seg=seg: (b, 0, base + seg * NA_PAIRS + p))

    return _grid_call(
        body, name="na_fwd", grid=(B, NA_PAIRS),
        out_shape=(jax.ShapeDtypeStruct((B, N, NA_WIDTH), BF16),
                   jax.ShapeDtypeStruct((B, NA_PAIRS, R, 2 * GRID_W, nk + C), BF16)),
        in_specs=[col(0), col(1), col(2),
                  pl.BlockSpec((2, 2 * NA_KH - 2, GRID_W, LANES), lambda b, p: (p, 0, 0, 0))],
        out_specs=(pl.BlockSpec((None, N, LANES), lambda b, p: (b, 0, p)),
                   pl.BlockSpec((None, None, R, 2 * GRID_W, nk + C), lambda b, p: (b, p, 0, 0, 0))),
        scratch_shapes=[],
        args=(proj, proj, proj, bias2))


def _na_bwd(proj, probs, dlat, n_ctx):
    assert proj.dtype == BF16
    B, T, _ = proj.shape
    C = n_ctx
    N = T - C
    R = N // GRID_W
    kh, nk = _na_geometry(R)
    scale = NA_DIM ** -0.5
    base = (4 * RET_WIDTH) // LANES

    def class_sums(tiles):
        n = tiles.shape[0]
        acc = None
        for v in range(GRID_W // SUBLANES):
            part = tiles[:, v * SUBLANES:(v + 1) * SUBLANES, :].reshape(n * SUBLANES, LANES)
            part = pltpu.roll(part, (NA_KW - 1 - v * SUBLANES) % LANES, 1)
            acc = part if acc is None else acc + part
        row = lax.broadcasted_iota(jnp.int32, acc.shape, 0)
        for bit in (1, 2, 4):
            acc = jnp.where((row & bit) != 0, pltpu.roll(acc, LANES - bit, 1), acc)
        return jnp.sum(acc.reshape(n, SUBLANES, LANES), axis=1)

    def body(q_ref, kb16, vb16, p_ref, dl_ref, d_ref, rr_ref, dkv, db_ref):
        b = pl.program_id(1)
        kc = kb16[0:C, :]
        vc = vb16[0:C, :]
        lane = lax.broadcasted_iota(jnp.int32, (GRID_W, LANES), 1)
        dkv[...] = jnp.zeros(dkv.shape, F32)
        d_ref[0, 0:C, :] = jnp.zeros((C, LANES), BF16)

        @pl.when(b == 0)
        def _():
            db_ref[...] = jnp.zeros(db_ref.shape, F32)

        sel2 = _pair_select()

        def group(gi, carry):
            pre = []
            for u in range(NA_GROUP):
                r = gi * NA_GROUP + u
                bs = jnp.clip(r - kh // 2, 0, R - kh)
                dr0 = bs - r + (NA_KH - 1)
                q = q_ref[pl.ds(pl.multiple_of(C + r * GRID_W, GRID_W), GRID_W), :].astype(F32) * scale
                do = dl_ref[pl.ds(pl.multiple_of(r * GRID_W, GRID_W), GRID_W), :]
                q2 = jnp.where(sel2, jnp.concatenate([q, q], axis=0), 0.0).astype(BF16)
                do2 = jnp.where(sel2, jnp.concatenate([do, do], axis=0), 0.0).astype(BF16)
                band = pl.ds(pl.multiple_of(C + bs * GRID_W, GRID_W), nk)
                dp_loc = _dot_nt(do2, vb16[band, :])
                dp_ctx = _dot_nt(do2, vc)
                pre.append((r, dr0, band, q2, do2, dp_loc, dp_ctx))
            mid = []
            for r, dr0, band, q2, do2, dp_loc, dp_ctx in pre:
                pb_loc = p_ref[r, :, 0:nk]
                pb_ctx = p_ref[r, :, nk:]
                p_loc = pb_loc.astype(F32)
                p_ctx = pb_ctx.astype(F32)
                delta = (jnp.sum(p_loc * dp_loc, axis=-1, keepdims=True)
                         + jnp.sum(p_ctx * dp_ctx, axis=-1, keepdims=True))
                ds_loc = p_loc * (dp_loc - delta)
                ds_ctx = p_ctx * (dp_ctx - delta)
                mid.append((r, dr0, band, q2, do2, pb_loc, pb_ctx, ds_loc, ds_ctx))
            for r, dr0, band, q2, do2, pb_loc, pb_ctx, ds_loc, ds_ctx in mid:
                dsb_loc = ds_loc.astype(BF16)
                dsb_ctx = ds_ctx.astype(BF16)
                dq2 = _dot(dsb_loc, kb16[band, :]) + _dot(dsb_ctx, kc)
                d_ref[0, pl.ds(pl.multiple_of(C + r * GRID_W, GRID_W), GRID_W), :] = (jnp.where(
                    lane < NA_DIM, dq2[:GRID_W], dq2[GRID_W:]) * scale).astype(BF16)
                dkv[0, band, :] += _dot_tn(dsb_loc, q2)
                dkv[1, band, :] += _dot_tn(pb_loc, do2)
                dkv[0, 0:C, :] += _dot_tn(dsb_ctx, q2)
                dkv[1, 0:C, :] += _dot_tn(pb_ctx, do2)
                for e in range(2):
                    for m in range(kh // 2):
                        db_ref[e, pl.ds(dr0 + 2 * m, 1)] += ds_loc[e * GRID_W:(e + 1) * GRID_W,
                                                                   m * LANES:(m + 1) * LANES].reshape(1, GRID_W, LANES)
            return carry

        lax.fori_loop(0, R // NA_GROUP, group, 0)
        d_ref[1] = dkv[0].astype(BF16)
        d_ref[2] = dkv[1].astype(BF16)

        @pl.when(b == B - 1)
        def _():
            for e in range(2):
                rr_ref[e] = class_sums(db_ref[e])

    def col(seg):
        return pl.BlockSpec((None, T, LANES), lambda p, b, seg=seg: (b, 0, base + seg * NA_PAIRS + p))

    return _grid_call(
        body, name="na_bwd", grid=(NA_PAIRS, B),
        out_shape=(jax.ShapeDtypeStruct((B, 3, T, NA_WIDTH), BF16),
                   jax.ShapeDtypeStruct((NA_HEADS, 2 * NA_KH - 2, LANES), F32)),
        in_specs=[col(0), col(1), col(2),
                  pl.BlockSpec((None, None, R, 2 * GRID_W, nk + C), lambda p, b: (b, p, 0, 0, 0)),
                  pl.BlockSpec((None, N, LANES), lambda p, b: (b, 0, p))],
        out_specs=(pl.BlockSpec((None, 3, T, LANES), lambda p, b: (b, 0, 0, p)),
                   pl.BlockSpec((2, 2 * NA_KH - 2, LANES), lambda p, b: (p, 0, 0))),
        scratch_shapes=[pltpu.VMEM((2, T, LANES), F32), pltpu.VMEM((2, 2 * NA_KH - 2, GRID_W, LANES), F32)],
        args=(proj, proj, proj, probs, dlat))


def _dense_core(lat_ret, lat_na, x, tgt, modl, g_post_mix, g_pre_mlp, g_post_mlp, w_out, w1, w2):
    B, N, D = x.shape
    F = w1.shape[1]
    wout_rows, w1_cols, w2_rows = w_out.shape[0] // N_DEV, w1.shape[1] // N_DEV, w2.shape[0] // N_DEV
    mixw = w_out.shape[0]
    half = mixw // 2
    tm = _div_tile(N, 256, 16)
    nt = N // tm
    fc = _div_tile(F, 1024, LANES)

    def body(lr_ref, ln_ref, x_ref, t_ref, gt1_ref, sh2_ref, sc2_ref, gt2_ref, gpm_ref, gpre_ref, gpo_ref,
             wout_part, w1_part, w2_part,
             dy1_ref, dlr_ref, dln_ref, dmix_ref, h2_ref, a_ref, du_ref, dz_ref, red_ref, wout_hbm, w1_hbm, w2_hbm,
             wout_v, w1_v, w2_v, u_s, sems, fsend, frecv):
        @pl.when((pl.program_id(0) == 0) & (pl.program_id(1) == 0))
        def _():
            relay = [(_row_block(wout_part, wout_rows), _row_block(wout_hbm, wout_rows)),
                     (_col_block(w1_part, w1_cols), _col_block(w1_hbm, w1_cols)),
                     (_row_block(w2_part, w2_rows), _row_block(w2_hbm, w2_rows))]
            _forward_start(relay, fsend, frecv)
            _forward_wait(relay, fsend, frecv)
            cps = [pltpu.make_async_copy(wout_hbm, wout_v, sems.at[0]),
                   pltpu.make_async_copy(w1_hbm, w1_v, sems.at[1]),
                   pltpu.make_async_copy(w2_hbm, w2_v, sems.at[2])]
            for cp in cps:
                cp.start()
            for cp in cps:
                cp.wait()

        @pl.when(pl.program_id(1) == 0)
        def _():
            red_ref[...] = jnp.zeros(red_ref.shape, F32)

        gt1 = gt1_ref[...]
        sh2 = sh2_ref[...]
        sc2 = sc2_ref[...]
        gt2 = gt2_ref[...]
        gpm = gpm_ref[...]
        gpre = gpre_ref[...]
        gpo = gpo_ref[...]

        def rowmean(a):
            return jnp.mean(a, axis=-1, keepdims=True)

        def colsum(a):
            return jnp.sum(a, axis=0, keepdims=True)

        mix_gain = gt1 * gpm
        mlp_in_gain = gpre * (1.0 + sc2)
        mlp_out_gain = gt2 * gpo
        mix = _dot(lr_ref[...], wout_v[0:half, :]) + _dot(ln_ref[...], wout_v[half:, :])
        x = x_ref[...]
        rm = lax.rsqrt(rowmean(mix * mix) + NORM_EPS)
        mh = mix * rm
        y1 = x + mh * mix_gain
        r1 = lax.rsqrt(rowmean(y1 * y1) + NORM_EPS)
        xh = y1 * r1
        h2b = (xh * mlp_in_gain + sh2).astype(BF16)
        h2_ref[...] = h2b
        z = jnp.zeros((tm, D), F32)
        for c0 in range(0, F, fc):
            u = _dot(h2b, w1_v[:, c0:c0 + fc])
            u_s[:, c0:c0 + fc] = u
            ru = jnp.maximum(u, 0.0)
            ab = (ru * ru).astype(BF16)
            a_ref[:, c0:c0 + fc] = ab
            z = z + _dot(ab, w2_v[c0:c0 + fc, :])
        r2 = lax.rsqrt(rowmean(z * z) + NORM_EPS)
        zh = z * r2
        y2 = y1 + zh * mlp_out_gain
        err = y2 - t_ref[...]
        loss = 0.5 * jnp.sum(rowmean(err * err))
        dy2 = err * (1.0 / D)
        s_out = colsum(dy2 * zh)
        red_ref[2:3, :] += s_out * gpo
        red_ref[6:7, :] += s_out * gt2
        dzh = dy2 * mlp_out_gain
        dz = r2 * (dzh - zh * rowmean(dzh * zh))
        dzb = dz.astype(BF16)
        dz_ref[...] = dzb
        dh2 = jnp.zeros((tm, D), F32)
        for c0 in range(0, F, fc):
            da = _dot_nt(dzb, w2_v[c0:c0 + fc, :])
            dub = (da * (2.0 * jnp.maximum(u_s[:, c0:c0 + fc], 0.0))).astype(BF16)
            du_ref[:, c0:c0 + fc] = dub
            dh2 = dh2 + _dot_nt(dub, w1_v[:, c0:c0 + fc])
        s_in = colsum(dh2 * xh)
        red_ref[3:4, :] += s_in * gpre
        red_ref[4:5, :] += colsum(dh2)
        red_ref[5:6, :] += s_in * (1.0 + sc2)
        dxh = dh2 * mlp_in_gain
        dy1 = dy2 + r1 * (dxh - xh * rowmean(dxh * xh))
        dy1_ref[...] = dy1
        s_mix = colsum(dy1 * mh)
        red_ref[0:1, :] += s_mix * gpm
        red_ref[1:2, :] += s_mix * gt1
        dmh = dy1 * mix_gain
        dmix = (rm *(dmh - mh * rowmean(dmh * mh))).astype(BF16)
        dmix_ref[...] = dmix
        dlr_ref[...] = _dot_nt(dmix, wout_v[0:half, :])
        dln_ref[...] = _dot_nt(dmix, wout_v[half:, :])
        red_ref[7:8, :] += jnp.zeros((1, D), F32) + loss

    def tok(w):
        return pl.BlockSpec((None, tm, w), lambda b, t: (b, t, 0))

    def mod(k):
        return pl.BlockSpec((None, None, 1, D), lambda b, t, k=k: (b, k, 0, 0))

    def vec():
        return pl.BlockSpec((1, D), lambda b, t: (0, 0))

    return pl.pallas_call(
        body, name="dense_core", grid=(B, nt),
        out_shape=(jax.ShapeDtypeStruct((B, N, D), F32), jax.ShapeDtypeStruct((B, N, half), F32),
                   jax.ShapeDtypeStruct((B, N, half), F32), jax.ShapeDtypeStruct((B, N, D), BF16),
                   jax.ShapeDtypeStruct((B, N, D), BF16), jax.ShapeDtypeStruct((B, N, F), BF16),
                   jax.ShapeDtypeStruct((B, N, F), BF16), jax.ShapeDtypeStruct((B, N, D), BF16),
                   jax.ShapeDtypeStruct((B, SUBLANES, D), F32),
                   jax.ShapeDtypeStruct(w_out.shape, w_out.dtype), jax.ShapeDtypeStruct(w1.shape, w1.dtype),
                   jax.ShapeDtypeStruct(w2.shape, w2.dtype)),
        in_specs=[tok(half), tok(half), tok(D), tok(D), mod(2), mod(3), mod(4), mod(5), vec(), vec(), vec(),
                  _any(), _any(), _any()],
        out_specs=(tok(D), tok(half), tok(half), tok(D), tok(D), tok(F), tok(F), tok(D),
                   pl.BlockSpec((None, SUBLANES, D), lambda b, t: (b, 0, 0)), _any(), _any(), _any()),
        scratch_shapes=[pltpu.VMEM((mixw, D), BF16), pltpu.VMEM((D, F), BF16), pltpu.VMEM((F, D), BF16),
                        pltpu.VMEM((tm, F), F32), pltpu.SemaphoreType.DMA((3,)),
                        pltpu.SemaphoreType.DMA((3, 3)), pltpu.SemaphoreType.DMA((3, 3))],
        input_output_aliases={11: 9, 12: 10, 13: 11},
        compiler_params=_params("arbitrary", "arbitrary"),
    )(lat_ret, lat_na, x, tgt, modl, modl, modl, modl, g_post_mix, g_pre_mlp, g_post_mlp, w_out, w1, w2)[:9]


def _inproj_bwd(dret, dna, x, ctx, dy1, modl, g1, w_in_t, after):
    B, N, D = x.shape
    n_ctx = ctx.shape[1]
    T = n_ctx + N
    tm = _div_tile(n_ctx, 256, 16)
    nct, ctx_spec, lat_spec = _token_tiles(n_ctx, tm)
    nt = T // tm
    nseg_r = dret.shape[1]
    nseg_n = dna.shape[1]
    nw = w_in_t.shape[0]

    def body(*refs):
        seg_refs = refs[:nseg_r + nseg_n]
        c_ref, x_ref, dy1_ref, sc_ref, g_ref, w_ref, dx_ref, red_ref = refs[nseg_r + nseg_n:]
        t = pl.program_id(1)
        dh = jnp.zeros((tm, D), F32)
        for s, ref in enumerate(seg_refs):
            dh = dh + _dot(ref[...], w_ref[s * SEG:(s + 1) * SEG, :])
        x = jnp.where(t < nct, c_ref[...], x_ref[...])
        g = g_ref[...]
        r = lax.rsqrt(jnp.mean(x * x, axis=-1, keepdims=True) + NORM_EPS)
        xh = x * r
        gain = 1.0 + sc_ref[...]
        s_in = jnp.sum(dh * xh, axis=0, keepdims=True)
        red_ref[0:1, :] = jnp.sum(dh, axis=0, keepdims=True)
        red_ref[1:2, :] = s_in * g
        red_ref[2:3, :] = s_in * gain
        red_ref[3:, :] = jnp.zeros((SUBLANES - 3, D), F32)
        dxh = dh * (g * gain)
        dx = r * (dxh - xh * jnp.mean(dxh * xh, axis=-1, keepdims=True))
        dx_ref[...] = dx + jnp.where(t >= nct, dy1_ref[...], 0.0)

    def mrow(b, t):
        return jnp.where(t < nct, B, b)

    def seg(s):
        return pl.BlockSpec((None, None, tm, SEG), lambda b, t, s=s: (b, s, t, 0))

    return _grid_call(
        body, name="inproj_bwd", grid=(B, nt),
        out_shape=(jax.ShapeDtypeStruct((B, N, D), F32), jax.ShapeDtypeStruct((B, nt, SUBLANES, D), F32)),
        in_specs=[seg(s) for s in range(nseg_r)] + [seg(s) for s in range(nseg_n)]
                 + [ctx_spec(D), lat_spec(D), lat_spec(D),
                    pl.BlockSpec((None, None, 1, D), lambda b, t: (mrow(b, t), 1, 0, 0)),
                    pl.BlockSpec((1, D), lambda b, t: (0, 0)),
                    pl.BlockSpec((nw, D), lambda b, t: (0, 0))],
        out_specs=(lat_spec(D), pl.BlockSpec((None, None, SUBLANES, D), lambda b, t: (b, t, 0, 0))),
        scratch_shapes=[], args=(*([dret] * nseg_r), *([dna] * nseg_n), ctx, x, dy1, modl, g1, w_in_t), after=after)


def _tn_matmul(lhs, rhs, name, rows_before=0, rows_after=0, into=None):
    B, S, T, W = lhs.shape
    nn = rhs.shape[-1]
    tk = _div_tile(T, 2304, LANES)
    bm = _div_tile(W, 1024, LANES)
    bn = _div_tile(nn, 1024, LANES)
    nkt = T // tk
    nk = B * nkt

    def body(l_ref, r_ref, *rest):
        o_ref, acc = rest[-2:]
        k = pl.program_id(3)

        @pl.when(k == 0)
        def _():
            acc[...] = jnp.zeros(acc.shape, F32)

        acc[...] += _dot_tn(l_ref[...].astype(BF16), r_ref[...].astype(BF16))

        @pl.when(k == nk - 1)
        def _():
            o_ref[...] = acc[...].astype(BF16)

    nwb = W // bm
    first = rows_before // bm
    return pl.pallas_call(
        functools.partial(body), name=name, grid=(S, nwb, nn // bn, nk),
        out_shape=jax.ShapeDtypeStruct((rows_before + S * W + rows_after, nn), BF16),
        in_specs=[pl.BlockSpec((None, None, tk, bm), lambda s, i, j, k: (k // nkt, s, k % nkt, i)),
                  pl.BlockSpec((None, tk, bn), lambda s, i, j, k: (k // nkt, k % nkt, j))]
                 + ([] if into is None else [_any()]),
        out_specs=pl.BlockSpec((bm, bn), lambda s, i, j, k: (first + s * nwb + i, j)),
        scratch_shapes=[pltpu.VMEM((bm, bn), F32)],
        input_output_aliases={} if into is None else {2: 0},
        compiler_params=_params("parallel", "parallel", "parallel", "arbitrary"),
    )(lhs, rhs, *([] if into is None else [into]))


class _SplitScatter:
    def __init__(self, gs, block_ofs, land_shapes, name, kind="scatter", masks=ALL_PEERS):
        self.n = n = len(gs)
        self.block_ofs, self.kind, self.masks = block_ofs, kind, masks
        if kind == "scatter":
            land_shapes = [(N_DEV,) + tuple(bs) for bs in land_shapes]
        hbm = pl.BlockSpec(memory_space=pltpu.HBM)
        sem = pl.BlockSpec(memory_space=pltpu.SEMAPHORE)

        def body(*refs):
            g_refs, land_refs = refs[:n], refs[n:2 * n]
            send_sems, recv_sems, own_sems = refs[2 * n:2 * n + 3]
            token = refs[-1]
            for own, pushes in self._copies(g_refs, land_refs, send_sems, recv_sems, own_sems, landing="sender"):
                own.start()
                for cp in pushes:
                    cp.start()
            token[...] = jnp.zeros_like(token)

        outs = pl.pallas_call(
            body, name=name,
            out_shape=(pltpu.SemaphoreType.DMA((n * (N_DEV - 1),)), pltpu.SemaphoreType.DMA((n * (N_DEV - 1),)),
                       pltpu.SemaphoreType.DMA((n,)))
                      + tuple(pltpu.HBM(g.shape, g.dtype) for g in gs)
                      + tuple(pltpu.HBM(s, g.dtype) for s, g in zip(land_shapes, gs))
                      + (jax.ShapeDtypeStruct((SUBLANES, LANES), F32),),
            in_specs=(hbm,) * (2 * n), out_specs=(sem,) * 3 + (hbm,) * (2 * n) + (_vmem(),),
            input_output_aliases={k: 3 + k for k in range(2 * n)},
            compiler_params=pltpu.CompilerParams(has_side_effects=pltpu.SideEffectType.DATAFLOW_SIDE_EFFECTING),
        )(*[pltpu.with_memory_space_constraint(g, pltpu.HBM) for g in gs],
          *[pltpu.with_memory_space_constraint(lax.empty(s, g.dtype), pltpu.HBM) for s, g in zip(land_shapes, gs)])
        self.sems, self.thru, self.token = outs[:3], outs[3:3 + 2 * n], outs[-1]

    def _copies(self, g_refs, land_refs, send_sems, recv_sems, own_sems, landing):
        me, peers = _me_and_peers()
        out = []
        for k in range(self.n):
            if self.kind == "scatter":
                src, dst = self.block_ofs[k](g_refs[k]), _slot(land_refs[k])
            else:
                src, dst = (lambda p, k=k: g_refs[k]), self.block_ofs[k](land_refs[k])
            own = pltpu.make_async_copy(src(me), dst(me), own_sems.at[k]) if landing == "sender" else None
            pushes = []
            for m in self.masks:
                dev, pid = peers[m - 1]
                i = k * (N_DEV - 1) + m - 1
                pushes.append(_remote(src(pid), dst(me if landing == "sender" else pid),
                                      send_sems.at[i], recv_sems.at[i], dev))
            out.append((own, pushes))
        return out


def _scatter_wait(scatters, after, name):
    hbm = pl.BlockSpec(memory_space=pltpu.HBM)
    sem = pl.BlockSpec(memory_space=pltpu.SEMAPHORE)
    n_arr = [2 * sc.n for sc in scatters]
    total = sum(n_arr)

    def body(*refs):
        arrs, sems = refs[:total], refs[total:total + 3 * len(scatters)]
        a0 = 0
        for j, sc in enumerate(scatters):
            g_refs, land_refs = arrs[a0:a0 + sc.n], arrs[a0 + sc.n:a0 + 2 * sc.n]
            a0 += 2 * sc.n
            send_sems, recv_sems, own_sems = sems[3 * j:3 * j + 3]
            for (own, sent), (_, got) in zip(sc._copies(g_refs, land_refs, send_sems, recv_sems, own_sems, "sender"),
                                             sc._copies(g_refs, land_refs, send_sems, recv_sems, own_sems, "receiver")):
                own.wait()
                for cp in sent:
                    cp.wait_send()
                for cp in got:
                    cp.wait_recv()

    operands = [a for sc in scatters for a in sc.thru]
    outs = pl.pallas_call(
        body, name=name,
        out_shape=tuple(pltpu.HBM(a.shape, a.dtype) for a in operands),
        in_specs=(hbm,) * total + (sem,) * (3 * len(scatters)) + (pl.BlockSpec(memory_space=pl.ANY),),
        out_specs=(hbm,) * total, input_output_aliases={k: k for k in range(total)},
        compiler_params=pltpu.CompilerParams(has_side_effects=pltpu.SideEffectType.DATAFLOW_SIDE_EFFECTING),
    )(*operands, *[s for sc in scatters for s in sc.sems], after)
    lands, a0 = [], 0
    for sc in scatters:
        lands.extend(outs[a0 + sc.n:a0 + 2 * sc.n])
        a0 += 2 * sc.n
    return lands


def _small_ar(mbuf, silu_all, w_ada, c_ctx, n_mod_rows, n_vec_rows):
    D = silu_all.shape[1]
    ncol = w_ada.shape[1]
    nm = mbuf.shape[2]
    srows = silu_all.shape[0]

    def body(mbuf, s_ref, w_ref, cc_ref, tot_ref, gb_ref, gw_ref, gc_ref, tbuf, dmx, cmrow, send3, recv3):
        me, _ = _me_and_peers()
        msum = mbuf[0]
        for k in range(1, N_DEV):
            msum = msum + mbuf[k]
        tot_ref[...] = msum[n_mod_rows:n_mod_rows + n_vec_rows]
        gb_ref[...] = jnp.sum(msum[0:n_mod_rows], axis=0, keepdims=True)
        loc = pl.ds(pl.multiple_of(me * ncol, ncol), ncol)
        for k in range(N_DEV):
            dmx[k * SUBLANES:(k + 1) * SUBLANES, :] = mbuf[k, :, loc]
        cmrow[...] = msum
        cm_loc = cmrow[n_mod_rows - 1:n_mod_rows, loc]
        dmx[N_DEV * SUBLANES:, :] = jnp.concatenate([cm_loc, jnp.zeros((SUBLANES - 1, ncol), F32)], axis=0)
        gw_ref[...] = _dot_tn(s_ref[...], dmx[...])
        tbuf[me] = _dot_nt(dmx[N_DEV * SUBLANES:, :], w_ref[...])
        _exchange(lambda p: tbuf.at[me], lambda p: tbuf.at[p], send3, recv3)
        tsum = tbuf[0]
        for k in range(1, N_DEV):
            tsum = tsum + tbuf[k]
        cc = cc_ref[...]
        sg = _sigmoid(cc)
        gc_ref[...] = tsum[0:1, :] * (sg * (1.0 + cc * (1.0 - sg)))

    return pl.pallas_call(
        body, name="small_ar",
        out_shape=(jax.ShapeDtypeStruct((n_vec_rows, nm), F32), jax.ShapeDtypeStruct((1, nm), F32),
                   jax.ShapeDtypeStruct((D, ncol), F32), jax.ShapeDtypeStruct((1, D), F32)),
        in_specs=[_vmem()] * 4, out_specs=(_vmem(),) * 4,
        scratch_shapes=[pltpu.VMEM((N_DEV, SUBLANES, D), F32), pltpu.VMEM((srows, ncol), F32),
                        pltpu.VMEM((SUBLANES, nm), F32)] + [pltpu.SemaphoreType.DMA((N_DEV - 1,))] * 2,
        compiler_params=pltpu.CompilerParams(vmem_limit_bytes=VMEM_LIMIT),
    )(mbuf, silu_all, w_ada, c_ctx.reshape(1, D))


def _adam_update(w, g, m, v):
    mn = ADAM_B1 * m + (1.0 - ADAM_B1) * g
    vn = ADAM_B2 * v + (1.0 - ADAM_B2) * (g * g)
    m_hat = mn / (1.0 - ADAM_B1 ** ADAM_STEP)
    v_hat = vn / (1.0 - ADAM_B2 ** ADAM_STEP)
    return -ADAM_LR * (m_hat / (jnp.sqrt(v_hat) + ADAM_EPS) + ADAM_WD * w), mn, vn


def _adamw(w, g, m, v, name):
    rows, cols = w.shape
    tr = _div_tile(rows, 512, SUBLANES)

    def body(w_ref, g_ref, m_ref, v_ref, d_ref, nm_ref, nv_ref):
        d_ref[...], nm_ref[...], nv_ref[...] = _adam_update(w_ref[...], g_ref[...], m_ref[...], v_ref[...])

    spec = pl.BlockSpec((tr, cols), lambda i: (i, 0))
    return pl.pallas_call(
        functools.partial(body), name=name, grid=(rows // tr,),
        out_shape=(jax.ShapeDtypeStruct((rows, cols), F32),) * 3,
        in_specs=[spec] * 4, out_specs=(spec,) * 3,
        compiler_params=_params("parallel"),
    )(w, g, m, v)


def _adamw_small(items, name):
    n = len(items)

    def body(*refs):
        ins, outs = refs[:4 * n], refs[4 * n:]
        for i in range(n):
            w_ref, g_ref, m_ref, v_ref = ins[4 * i:4 * i + 4]
            outs[3 * i][...], outs[3 * i + 1][...], outs[3 * i + 2][...] = _adam_update(
                w_ref[...], g_ref[...], m_ref[...], v_ref[...])

    outs = pl.pallas_call(
        body, name=name,
        out_shape=tuple(jax.ShapeDtypeStruct(it[0].shape, F32) for it in items for _ in range(3)),
        in_specs=[_vmem()] * (4 * n), out_specs=(_vmem(),) * (3 * n),
        compiler_params=pltpu.CompilerParams(vmem_limit_bytes=VMEM_LIMIT),
    )(*[a for it in items for a in it])
    return [tuple(outs[3 * i:3 * i + 3]) for i in range(n)]


def _sum_adamw(buf, w, m, v, name):
    _, rows, cols = buf.shape
    tr = _div_tile(rows, 256, 2 * SUBLANES)

    def body(b_ref, w_ref, m_ref, v_ref, g_ref, d_ref, nm_ref, nv_ref):
        g = b_ref[0].astype(F32)
        for k in range(1, N_DEV):
            g = g + b_ref[k].astype(F32)
        g_ref[...] = g
        d_ref[...], nm_ref[...], nv_ref[...] = _adam_update(w_ref[...], g, m_ref[...], v_ref[...])

    spec = pl.BlockSpec((tr, cols), lambda i: (i, 0))
    return pl.pallas_call(
        functools.partial(body), name=name, grid=(rows // tr,),
        out_shape=(jax.ShapeDtypeStruct((rows, cols), F32),) * 4,
        in_specs=[pl.BlockSpec((N_DEV, tr, cols), lambda i: (0, i, 0))] + [spec] * 3, out_specs=(spec,) * 4,
        compiler_params=_params("parallel"),
    )(buf, w, m, v)


def _rope_tables(n_ctx, n):
    n_freq = RET_DIM // 4
    inv = np.float32(ROPE_BASE) ** (-np.arange(n_freq, dtype=np.float32) / np.float32(n_freq))
    tok = np.arange(n)
    pos_r = (tok // GRID_W).astype(np.float32)
    pos_c = (tok % GRID_W).astype(np.float32)
    ang_r = (pos_r[:, None] * inv[None, :]).astype(np.float32)
    ang_c = (pos_c[:, None] * inv[None, :]).astype(np.float32)
    cos = np.concatenate([np.cos(ang_r), np.cos(ang_r), np.cos(ang_c), np.cos(ang_c)], axis=-1)
    sin = np.concatenate([-np.sin(ang_r), np.sin(ang_r), -np.sin(ang_c), np.sin(ang_c)], axis=-1)
    cos = np.concatenate([np.ones((n_ctx, RET_DIM), np.float32), cos], axis=0)
    sin = np.concatenate([np.zeros((n_ctx, RET_DIM), np.float32), sin], axis=0)
    return jnp.asarray(cos, F32), jnp.asarray(sin, F32)


def _na_tables():
    q = np.arange(GRID_W)[:, None]
    k = np.arange(GRID_W)[None, :]
    start = np.clip(q - NA_KW // 2, 0, GRID_W - NA_KW)
    valid = (k >= start) & (k < start + NA_KW)
    dc = np.clip(k - q + (NA_KW - 1), 0, 2 * NA_KW - 2)
    ncls = 2 * NA_KW - 1
    onehot = (dc[None] == np.arange(ncls)[:, None, None]) & valid[None]
    return onehot.astype(np.float32), valid


def _paired_bias(rpb, onehot, valid):
    ncls = onehot.shape[0]
    pair = np.zeros((2 * ncls, GRID_W, LANES), np.float32)
    pair[:ncls, :, :GRID_W] = onehot
    pair[ncls:, :, GRID_W:] = onehot
    rows = jnp.concatenate([rpb[:, :-1], rpb[:, 1:]], axis=-1)
    t = jnp.einsum("hdc,cqk->hdqk", rows, jnp.asarray(pair), precision=lax.Precision.HIGHEST)
    return jnp.where(jnp.asarray(np.tile(valid, (1, 2)))[None, None], t, NEG_INF)


def kernel(x, c, ctx, c_ctx, w_ada, b_ada, g_pre_mix, g_post_mix, g_pre_mlp, g_post_mlp, w_in, ret_decay, ret_gn, na_rpb, w_out, w_mlp1, w_mlp2, loss_target, m_c_ctx, m_w_ada, m_b_ada, m_g_pre_mix, m_g_post_mix, m_g_pre_mlp, m_g_post_mlp, m_w_in, m_ret_decay, m_ret_gn, m_na_rpb, m_w_out, m_w_mlp1, m_w_mlp2, v_c_ctx, v_w_ada, v_b_ada, v_g_pre_mix, v_g_post_mix, v_g_pre_mlp, v_g_post_mlp, v_w_in, v_ret_decay, v_ret_gn, v_na_rpb, v_w_out, v_w_mlp1, v_w_mlp2):
    B, N, D = x.shape
    C = ctx.shape[1]
    T = C + N

    silu_all, mods_g, win_b, wout_l, w1_l, w2_l = _mod_gather(c, c_ctx, w_ada[0], b_ada, w_in[0].T, w_out[0],
                                                             w_mlp1[0], w_mlp2[0])
    mods_mine = mods_g.transpose(1, 0, 2).reshape(mods_g.shape[1], N_MOD * D)
    modl = jnp.concatenate([mods_mine[:B], mods_mine[SUBLANES:SUBLANES + 1]], axis=0)
    modl = modl.reshape(B + 1, N_MOD, 1, D)
    rin = w_in.shape[2]
    rout, c1, r2 = wout_l.shape[0], w1_l.shape[1], w2_l.shape[0]

    def rows_of(n):
        return lambda ref: _row_block(ref, n)

    def cols_of(n):
        return lambda ref: _col_block(ref, n)

    cos, sin = _rope_tables(C, N)
    onehot, valid = _na_tables()
    bias2 = _paired_bias(na_rpb[0], onehot, valid)
    lg = jax.nn.log_sigmoid(ret_decay[0].astype(F32))

    ag = _SplitScatter([wout_l, w1_l, w2_l], [rows_of(rout), cols_of(c1), rows_of(r2)],
                       [(N_DEV * rout, D), (D, N_DEV * c1), (N_DEV * r2, D)], "ag_mlp_start",
                       kind="gather", masks=SIBLING + ICI_SAME_CORE)
    h_all, proj = _inproj_fwd(x, ctx, modl, g_pre_mix, win_b, after=ag.token)
    o_ret, lat_ret, q_rot, k_rot = _ret_fwd(proj, cos, sin, lg, ret_gn, C)
    lat_na, na_probs = _na_fwd(proj, bias2, C)
    wout_part, w1_part, w2_part = _scatter_wait([ag], lat_na, "ag_mlp_wait")

    (dy1, dlat_ret, dlat_na, dmix, h2, act, du, dz, red_d) = _dense_core(
        lat_ret, lat_na, x, loss_target, modl, g_post_mix, g_pre_mlp, g_post_mlp, wout_part, w1_part, w2_part)

    gw_out_p = _tn_matmul(lat_ret[:, None], dmix, "gw_out_ret", rows_after=lat_na.shape[-1])
    gw_out_p = _tn_matmul(lat_na[:, None], dmix, "gw_out_na", rows_before=lat_ret.shape[-1], into=gw_out_p)
    gw1_p = _tn_matmul(h2[:, None], du, "gw_mlp1")
    gw2_p = _tn_matmul(act[:, None], dz, "gw_mlp2")
    rs_mlp = _SplitScatter([gw_out_p, gw1_p, gw2_p], [rows_of(rout), cols_of(c1), rows_of(r2)],
                           [(rout, D), (D, c1), (r2, D)], "rs_mlp_start")

    dret, dgn_p, dlg_p = _ret_bwd(proj, q_rot, k_rot, cos, sin, lg, ret_gn, o_ret, dlat_ret, C, after=rs_mlp.token)
    dna, rr = _na_bwd(proj, na_probs, dlat_na, C)
    ret_cols, na_cols = dret.shape[1] * dret.shape[3], dna.shape[1] * dna.shape[3]
    gwin_t_p = _tn_matmul(dret, h_all, "gw_in_ret", rows_after=na_cols)
    gwin_t_p = _tn_matmul(dna, h_all, "gw_in_na", rows_before=ret_cols, into=gwin_t_p)
    rs_in = _SplitScatter([gwin_t_p], [rows_of(rin)], [(rin, D)], "rs_w_in_start")
    grad_x, red_i = _inproj_bwd(dret, dna, x, ctx, dy1, modl, g_pre_mix, win_b, after=rs_in.token)

    rd = red_d
    nct = red_i.shape[1] * C // T
    ri_ctx = red_i[:, :nct].sum(axis=(0, 1))
    ri_lat = red_i[:, nct:].sum(axis=1)
    d_mods = jnp.concatenate([ri_lat[:, 0], ri_lat[:, 1], rd[:, 0], rd[:, 4], rd[:, 3], rd[:, 2]], axis=-1)
    d_cmods = jnp.concatenate([ri_ctx[0], ri_ctx[1], jnp.zeros(((N_MOD - 2) * D,), F32)])[None]
    dg_pre_mix = ri_lat[:, 2].sum(axis=0) + ri_ctx[2]
    dg_post_mix = rd[:, 1].sum(axis=0)
    dg_pre_mlp = rd[:, 5].sum(axis=0)
    dg_post_mlp = rd[:, 6].sum(axis=0)
    loss_p = rd[:, 7, 0].sum()
    d_gn = dgn_p[:, 0].sum(axis=0)
    d_lg = dlg_p[:, :, :2, 0].sum(axis=0).T
    d_decay = d_lg * jax.nn.sigmoid(-ret_decay[0].astype(F32))
    ncls = 2 * NA_KW - 1
    d_rpb = (jnp.pad(rr[:, :, :ncls], ((0, 0), (0, 1), (0, 0)))
             + jnp.pad(rr[:, :, GRID_W:GRID_W + ncls], ((0, 0), (1, 0), (0, 0))))
    d_rpb32 = jnp.pad(d_rpb, ((0, 0), (0, 0), (0, 32 - ncls)))
    nm = N_MOD * D
    vec_rows = [jnp.concatenate([dg_pre_mix, dg_post_mix, dg_pre_mlp, dg_post_mlp, d_gn,
                                 jnp.pad(d_decay.reshape(-1), (0, LANES - d_decay.size)),
                                 jnp.full((LANES,), loss_p, F32)]),
                d_rpb32.reshape(-1)]
    n_vec_rows = len(vec_rows)
    assert B + 1 + n_vec_rows <= SUBLANES and all(r.shape[0] <= nm for r in vec_rows)
    vec = jnp.stack([jnp.pad(r, (0, nm - r.shape[0])) for r in vec_rows])
    dm_slot = jnp.concatenate([d_mods, d_cmods, vec, jnp.zeros((SUBLANES - B - 1 - n_vec_rows, nm), F32)], axis=0)
    def whole(ref):
        return lambda p: ref

    small = _SplitScatter([dm_slot], [whole], [dm_slot.shape], "small_start")
    land_out, land_1, land_2 = _scatter_wait([rs_mlp], small.token, "rs_mlp_wait")
    fused = {"w_out": _sum_adamw(land_out, w_out[0], m_w_out[0], v_w_out[0], "sum_adamw_w_out"),
             "w_mlp1": _sum_adamw(land_1, w_mlp1[0], m_w_mlp1[0], v_w_mlp1[0], "sum_adamw_w_mlp1"),
             "w_mlp2": _sum_adamw(land_2, w_mlp2[0], m_w_mlp2[0], v_w_mlp2[0], "sum_adamw_w_mlp2")}
    (land_in,) = _scatter_wait([rs_in], fused["w_mlp2"][0], "rs_w_in_wait")
    win_upd = _sum_adamw(land_in, w_in[0].T, m_w_in[0].T, v_w_in[0].T, "sum_adamw_w_in")
    fused["w_in"] = [a.T for a in win_upd]
    (mbuf,) = _scatter_wait([small], win_upd[0], "small_wait")
    tot, g_b_ada, g_w_ada, g_c_ctx = _small_ar(mbuf, silu_all, w_ada[0], c_ctx, B + 1, n_vec_rows)
    flat = tot[0]
    o0 = 0
    g_pre_mix_g = flat[o0:o0 + D]; o0 += D
    g_post_mix_g = flat[o0:o0 + D]; o0 += D
    g_pre_mlp_g = flat[o0:o0 + D]; o0 += D
    g_post_mlp_g = flat[o0:o0 + D]; o0 += D
    g_gn = flat[o0:o0 + RET_WIDTH]; o0 += RET_WIDTH
    g_decay = flat[o0:o0 + 2 * RET_HEADS].reshape(2, RET_HEADS); o0 += LANES
    loss = flat[o0]
    g_rpb = tot[1, :d_rpb32.size].reshape(d_rpb32.shape)[:, :, :ncls]

    grads = {
        "c_ctx": g_c_ctx.reshape(c_ctx.shape), "w_ada": g_w_ada[None], "b_ada": g_b_ada.reshape(b_ada.shape),
        "g_pre_mix": g_pre_mix_g[None], "g_post_mix": g_post_mix_g[None], "g_pre_mlp": g_pre_mlp_g[None],
        "g_post_mlp": g_post_mlp_g[None], "w_in": fused["w_in"][0][None], "ret_decay": g_decay[None], "ret_gn": g_gn[None],
        "na_rpb": g_rpb[None], "w_out": fused["w_out"][0][None], "w_mlp1": fused["w_mlp1"][0][None],
        "w_mlp2": fused["w_mlp2"][0][None],
    }
    weights = dict(c_ctx=c_ctx, w_ada=w_ada, b_ada=b_ada, g_pre_mix=g_pre_mix, g_post_mix=g_post_mix,
                   g_pre_mlp=g_pre_mlp, g_post_mlp=g_post_mlp, w_in=w_in, ret_decay=ret_decay, ret_gn=ret_gn,
                   na_rpb=na_rpb, w_out=w_out, w_mlp1=w_mlp1, w_mlp2=w_mlp2)
    m_in = dict(c_ctx=m_c_ctx, w_ada=m_w_ada, b_ada=m_b_ada, g_pre_mix=m_g_pre_mix, g_post_mix=m_g_post_mix,
                g_pre_mlp=m_g_pre_mlp, g_post_mlp=m_g_post_mlp, w_in=m_w_in, ret_decay=m_ret_decay,
                ret_gn=m_ret_gn, na_rpb=m_na_rpb, w_out=m_w_out, w_mlp1=m_w_mlp1, w_mlp2=m_w_mlp2)
    v_in = dict(c_ctx=v_c_ctx, w_ada=v_w_ada, b_ada=v_b_ada, g_pre_mix=v_g_pre_mix, g_post_mix=v_g_post_mix,
                g_pre_mlp=v_g_pre_mlp, g_post_mlp=v_g_post_mlp, w_in=v_w_in, ret_decay=v_ret_decay,
                ret_gn=v_ret_gn, na_rpb=v_na_rpb, w_out=v_w_out, w_mlp1=v_w_mlp1, w_mlp2=v_w_mlp2)
    names = list(weights)
    deltas, new_m, new_v = {}, {}, {}
    def as_2d(n):
        shp = weights[n].shape
        two_d = (-1, shp[-1]) if len(shp) > 1 else (1, shp[0])
        return [a.reshape(two_d) for a in (weights[n], grads[n], m_in[n], v_in[n])]

    def unsharded(n):
        shp = weights[n].shape
        view = shp[1:] if len(shp) > 2 else (shp if len(shp) == 2 else (1, shp[0]))
        return [a.reshape(view) for a in (weights[n], grads[n], m_in[n], v_in[n])]

    small = [n for n in names if n not in fused and weights[n].size <= 65536]
    updated = dict(zip(small, _adamw_small([unsharded(n) for n in small], "adamw_small")))
    for n in names:
        if n in fused:
            updated[n] = fused[n][1:]
        elif n not in updated:
            updated[n] = _adamw(*as_2d(n), "adamw_" + n)
        deltas[n], new_m[n], new_v[n] = (a.reshape(weights[n].shape) for a in updated[n])
    return (loss, grad_x, *[grads[n] for n in names], *[deltas[n] for n in names],
            *[new_m[n] for n in names], *[new_v[n] for n in names])
```

```python
import functools

import numpy as np
import jax
import jax.numpy as jnp
from jax import lax
from jax.experimental import pallas as pl
from jax.experimental.pallas import tpu as pltpu

F32 = jnp.float32
BF16 = jnp.bfloat16
MESH = pl.DeviceIdType.MESH

N_DEV = 8
LANES = 128
SUBLANES = 8
VMEM_LIMIT = 60 * 1024 * 1024

GRID_W = 64
RET_HEADS = 4
RET_DIM = 128
RET_WIDTH = RET_HEADS * RET_DIM
NA_HEADS = 8
NA_DIM = 64
NA_WIDTH = NA_HEADS * NA_DIM
NA_PAIRS = NA_HEADS // 2
NA_KH = 8
NA_KW = 16
NA_GROUP = 8
SEG = 512
ROPE_BASE = 10000.0
NORM_EPS = 1e-6
NEG_INF = -1e30
N_MOD = 6

ADAM_LR = 0.001
ADAM_B1 = 0.9
ADAM_B2 = 0.999
ADAM_EPS = 1e-08
ADAM_WD = 0.01
ADAM_STEP = 10


def _dot(a, b):
    return lax.dot_general(a, b, (((1,), (0,)), ((), ())), preferred_element_type=F32)


def _dot_nt(a, b):
    return lax.dot_general(a, b, (((1,), (1,)), ((), ())), preferred_element_type=F32)


def _dot_tn(a, b):
    return lax.dot_general(a, b, (((0,), (0,)), ((), ())), preferred_element_type=F32)


def _sigmoid(x):
    return 1.0 / (1.0 + jnp.exp(-x))


def _div_tile(n, cap, mult):
    if n <= cap:
        return n
    for t in range(cap - cap % mult, 0, -mult):
        if n % t == 0:
            return t
    raise ValueError(f"no tile for {n}")


def _params(*sem):
    return pltpu.CompilerParams(dimension_semantics=tuple(sem) if sem else None,
                                vmem_limit_bytes=VMEM_LIMIT)


def _vmem():
    return pl.BlockSpec(memory_space=pltpu.VMEM)


def _any():
    return pl.BlockSpec(memory_space=pl.ANY)


def _me_and_peers():
    x, y, c = lax.axis_index("x"), lax.axis_index("y"), lax.axis_index("c")
    me = 4 * x + 2 * y + c
    peers = []
    for m in range(1, N_DEV):
        px = 1 - x if (m >> 2) & 1 else x
        py = 1 - y if (m >> 1) & 1 else y
        pc = 1 - c if m & 1 else c
        peers.append(((px, py, pc), 4 * px + 2 * py + pc))
    return me, peers


def _exchange(src_for, dst_from, send_sems, recv_sems):
    me, peers = _me_and_peers()
    sent = []
    for i, (dev, pid) in enumerate(peers):
        cp = pltpu.make_async_remote_copy(src_ref=src_for(pid), dst_ref=dst_from(me),
                                          send_sem=send_sems.at[i], recv_sem=recv_sems.at[i],
                                          device_id=dev, device_id_type=MESH)
        cp.start()
        sent.append(cp)
    for i, (dev, pid) in enumerate(peers):
        pltpu.make_async_remote_copy(src_ref=src_for(pid), dst_ref=dst_from(pid),
                                     send_sem=send_sems.at[i], recv_sem=recv_sems.at[i],
                                     device_id=dev, device_id_type=MESH).wait_recv()
    for cp in sent:
        cp.wait_send()


SIBLING = (1,)
ICI_SAME_CORE = (2, 4, 6)
ALL_PEERS = tuple(range(1, N_DEV))


def _remote(src, dst, send_sem, recv_sem, dev):
    return pltpu.make_async_remote_copy(src_ref=src, dst_ref=dst, send_sem=send_sem, recv_sem=recv_sem,
                                        device_id=dev, device_id_type=MESH)


def _push_start(items, masks, send_sems, recv_sems):
    me, peers = _me_and_peers()
    for k, (src_for, dst_from) in enumerate(items):
        for m in masks:
            dev, pid = peers[m - 1]
            _remote(src_for(pid), dst_from(me), send_sems.at[k, m - 1], recv_sems.at[k, m - 1], dev).start()


def _push_wait_recv(items, masks, send_sems, recv_sems):
    me, peers = _me_and_peers()
    for k, (src_for, dst_from) in enumerate(items):
        for m in masks:
            dev, pid = peers[m - 1]
            _remote(src_for(pid), dst_from(pid), send_sems.at[k, m - 1], recv_sems.at[k, m - 1], dev).wait_recv()


def _push_wait_send(items, masks, send_sems, recv_sems):
    me, peers = _me_and_peers()
    for k, (src_for, dst_from) in enumerate(items):
        for m in masks:
            dev, pid = peers[m - 1]
            _remote(src_for(pid), dst_from(me), send_sems.at[k, m - 1], recv_sems.at[k, m - 1], dev).wait_send()


def _forward_start(items, send_sems, recv_sems):
    me, peers = _me_and_peers()
    sib = peers[0][0]
    for k, (blk_in, blk_out) in enumerate(items):
        for j, m in enumerate(ICI_SAME_CORE):
            pid = peers[m - 1][1]
            _remote(blk_in(pid), blk_out(pid), send_sems.at[k, j], recv_sems.at[k, j], sib).start()


def _forward_wait(items, send_sems, recv_sems):
    me, peers = _me_and_peers()
    sib = peers[0][0]
    for k, (blk_in, blk_out) in enumerate(items):
        for j, m in enumerate(ICI_SAME_CORE):
            got = peers[(m | 1) - 1][1]
            _remote(blk_in(got), blk_out(got), send_sems.at[k, j], recv_sems.at[k, j], sib).wait_recv()
    for k, (blk_in, blk_out) in enumerate(items):
        for j, m in enumerate(ICI_SAME_CORE):
            pid = peers[m - 1][1]
            _remote(blk_in(pid), blk_out(pid), send_sems.at[k, j], recv_sems.at[k, j], sib).wait_send()


def _mod_gather(c, c_ctx, w_ada, b_ada, w_in_t, w_out, w1, w2):
    B, D = c.shape
    ncol = w_ada.shape[1]
    rows = SUBLANES * N_DEV + SUBLANES

    def body(c_ref, cc_ref, w_ref, b_ref, win_ref, wout_ref, w1_ref, w2_ref,
             s_ref, m_ref, gin_ref, wout_b, w1_b, w2_b,
             win_b, msend, send1, recv1, send2, recv2, wsend, wrecv, fsend, frecv, lsem):
        me, _ = _me_and_peers()
        win_b[...] = win_ref[...].astype(BF16)
        block = _row_block(gin_ref, w_in_t.shape[0])
        gather = [(lambda p: win_b, block)]
        own = pltpu.make_async_copy(win_b, block(me), lsem.at[0])
        cv = c_ref[...]
        slot = jnp.concatenate([cv * _sigmoid(cv), jnp.zeros((SUBLANES - B, D), F32)], axis=0)
        my_rows = pl.ds(pl.multiple_of(me * SUBLANES, SUBLANES), SUBLANES)
        s_ref[my_rows, :] = slot
        ccv = cc_ref[...]
        s_ref[SUBLANES * N_DEV:, :] = jnp.concatenate(
            [ccv * _sigmoid(ccv), jnp.zeros((SUBLANES - 1, D), F32)], axis=0)

        def rows_of(p):
            return s_ref.at[pl.ds(pl.multiple_of(p * SUBLANES, SUBLANES), SUBLANES), :]

        _exchange(lambda p: rows_of(me), rows_of, send1, recv1)
        own.start()
        _push_start(gather, SIBLING + ICI_SAME_CORE, wsend, wrecv)
        wout_b[...] = wout_ref[...].astype(BF16)
        w1_b[...] = w1_ref[...].astype(BF16)
        w2_b[...] = w2_ref[...].astype(BF16)
        b_loc = b_ref[:, pl.ds(pl.multiple_of(me * ncol, ncol), ncol)]
        mods = _dot(s_ref[...], w_ref[...]) + b_loc
        for p in range(N_DEV):
            msend[p] = jnp.concatenate([mods[p * SUBLANES:(p + 1) * SUBLANES], mods[N_DEV * SUBLANES:]], axis=0)
        m_ref[me] = msend[me]
        columns = [(lambda p: msend.at[p], lambda p: m_ref.at[p])]
        _push_start(columns, ALL_PEERS, send2, recv2)
        _push_wait_recv(gather, ICI_SAME_CORE, wsend, wrecv)
        relay = [(block, block)]
        _forward_start(relay, fsend, frecv)
        _push_wait_recv(columns, ALL_PEERS, send2, recv2)
        _push_wait_recv(gather, SIBLING, wsend, wrecv)
        _forward_wait(relay, fsend, frecv)
        _push_wait_send(columns, ALL_PEERS, send2, recv2)
        _push_wait_send(gather, SIBLING + ICI_SAME_CORE, wsend, wrecv)
        own.wait()

    return pl.pallas_call(
        body, name="mod_gather",
        out_shape=(jax.ShapeDtypeStruct((rows, D), F32), jax.ShapeDtypeStruct((N_DEV, 2 * SUBLANES, ncol), F32),
                   jax.ShapeDtypeStruct((N_DEV * w_in_t.shape[0], D), BF16),
                   jax.ShapeDtypeStruct(w_out.shape, BF16), jax.ShapeDtypeStruct(w1.shape, BF16),
                   jax.ShapeDtypeStruct(w2.shape, BF16)),
        in_specs=[_vmem()] * 8, out_specs=(_vmem(), _vmem(), _any(), _vmem(), _vmem(), _vmem()),
        scratch_shapes=[pltpu.VMEM(w_in_t.shape, BF16), pltpu.VMEM((N_DEV, 2 * SUBLANES, ncol), F32)]
                       + [pltpu.SemaphoreType.DMA((N_DEV - 1,))] * 2
                       + [pltpu.SemaphoreType.DMA((1, N_DEV - 1))] * 4 + [pltpu.SemaphoreType.DMA((1, 3))] * 2
                       + [pltpu.SemaphoreType.DMA((1,))],
        compiler_params=pltpu.CompilerParams(vmem_limit_bytes=VMEM_LIMIT),
    )(c, c_ctx.reshape(1, D), w_ada, b_ada, w_in_t, w_out, w1, w2)


def _row_block(ref, rows):
    return lambda p: ref.at[pl.ds(pl.multiple_of(p * rows, 2 * SUBLANES), rows), :]


def _col_block(ref, cols):
    return lambda p: ref.at[:, pl.ds(pl.multiple_of(p * cols, LANES), cols)]


def _slot(ref):
    return lambda p: ref.at[p]


def _grid_call(body, *, name, grid, out_shape, in_specs, out_specs, scratch_shapes, args, after=None):
    n_in = len(args)

    def ordered_body(*refs):
        body(*refs[:n_in], *refs[n_in + 1:])

    return pl.pallas_call(
        body if after is None else ordered_body, name=name, grid=grid, out_shape=tuple(out_shape),
        in_specs=list(in_specs) + ([] if after is None else [_any()]), out_specs=tuple(out_specs),
        scratch_shapes=list(scratch_shapes), compiler_params=_params(*(("arbitrary",) * len(grid))),
    )(*args, *([] if after is None else [after]))


def _token_tiles(n_ctx, tm):
    nct = n_ctx // tm

    def ctx_spec(D):
        return pl.BlockSpec((None, tm, D), lambda b, t: (b, jnp.minimum(t, nct - 1), 0))

    def lat_spec(D):
        return pl.BlockSpec((None, tm, D), lambda b, t: (b, jnp.maximum(t - nct, 0), 0))

    return nct, ctx_spec, lat_spec


def _inproj_fwd(x, ctx, modl, g1, w_in_t, after):
    B, N, D = x.shape
    n_ctx = ctx.shape[1]
    T = n_ctx + N
    nw = w_in_t.shape[0]
    tm = _div_tile(n_ctx, 256, 16)
    nct, ctx_spec, lat_spec = _token_tiles(n_ctx, tm)

    def body(c_ref, x_ref, sh_ref, sc_ref, g_ref, w_ref, h_ref, p_ref):
        x = jnp.where(pl.program_id(1) < nct, c_ref[...], x_ref[...])
        r = lax.rsqrt(jnp.mean(x * x, axis=-1, keepdims=True) + NORM_EPS)
        h = ((x * r) * g_ref[...]) * (1.0 + sc_ref[...]) + sh_ref[...]
        hb = h.astype(BF16)
        h_ref[...] = hb
        p_ref[...] = _dot_nt(hb, w_ref[...]).astype(BF16)

    def mrow(b, t):
        return jnp.where(t < nct, B, b)

    return _grid_call(
        body, name="inproj_fwd", grid=(B, T // tm),
        out_shape=(jax.ShapeDtypeStruct((B, T, D), BF16), jax.ShapeDtypeStruct((B, T, nw), BF16)),
        in_specs=[ctx_spec(D), lat_spec(D),
                  pl.BlockSpec((None, None, 1, D), lambda b, t: (mrow(b, t), 0, 0, 0)),
                  pl.BlockSpec((None, None, 1, D), lambda b, t: (mrow(b, t), 1, 0, 0)),
                  pl.BlockSpec((1, D), lambda b, t: (0, 0)),
                  pl.BlockSpec((nw, D), lambda b, t: (0, 0))],
        out_specs=(pl.BlockSpec((None, tm, D), lambda b, t: (b, t, 0)),
                   pl.BlockSpec((None, tm, nw), lambda b, t: (b, t, 0))),
        scratch_shapes=[], args=(ctx, x, modl, modl, g1, w_in_t), after=after)


def _swap32(x):
    lane = lax.broadcasted_iota(jnp.int32, x.shape, 1)
    return jnp.where((lane % 64) < 32, pltpu.roll(x, 96, 1), pltpu.roll(x, 32, 1))


def _rope(x, cos, sin):
    return x * cos + _swap32(x) * sin


def _unrope(dy, cos, sin):
    return dy * cos + _swap32(dy * sin)


def _ret_weights(lgf, lgb, dist):
    return jnp.exp(jnp.where(dist >= 0.0, lgf * dist, -lgb * dist))


class _RetDecay:
    def __init__(self, lgf, lgb, rows):
        r = lax.broadcasted_iota(jnp.int32, (rows, RET_DIM), 0).astype(F32)
        self.head = r + 1.0
        self.tail = (rows - 1.0) - r
        self.q_f = jnp.exp(lgf * self.head)
        self.k_f = jnp.exp(lgf * self.tail)
        self.q_b = jnp.exp(lgb * self.tail)
        self.k_b = jnp.exp(lgb * self.head)


def _ret_states(kf32, vs, lgf, lgb, C, c, nt, hf, hb, hfa=None, hba=None):
    dec = _RetDecay(lgf, lgb, c)
    dec_c = _RetDecay(lgf, lgb, C)
    step_f = jnp.exp(jnp.zeros((RET_DIM, RET_DIM), F32) + lgf * c)
    step_b = jnp.exp(jnp.zeros((RET_DIM, RET_DIM), F32) + lgb * c)

    def upd(rows, kdec):
        return _dot_tn((kf32[rows, :] * kdec).astype(BF16), vs[rows, :])

    def lat(t):
        return slice(C + t * c, C + (t + 1) * c)

    state = upd(slice(0, C), dec_c.k_f)
    aged = jnp.zeros_like(state)
    for t in range(nt):
        hf[t] = state.astype(BF16)
        if hfa is not None:
            hfa[t] = aged
        if t < nt - 1:
            aged = step_f * (aged + c * state)
            state = step_f * state + upd(lat(t), dec.k_f)
    state = upd(slice(0, C), dec_c.k_b)
    aged = jnp.zeros_like(state)
    for t in range(nt - 1, -1, -1):
        hb[t] = state.astype(BF16)
        if hba is not None:
            hba[t] = aged
        if t > 0:
            aged = step_b * (aged + c * state)
            state = step_b * state + upd(lat(t), dec.k_b)
    return dec, dec_c, step_f, step_b


def _ret_fwd(proj, cos, sin, lg, gn, n_ctx):
    B, T, _ = proj.shape
    C = n_ctx
    N = T - C
    c = _div_tile(N, 256, 16)
    nt = N // c
    scale = RET_DIM ** -0.5

    def body(lg_ref, q_ref, k_ref, vs, g_ref, cos_ref, sin_ref, gn_ref, o_ref, lat_ref, qr_ref, kf32,
             qs, ks, hf, hb):
        h = pl.program_id(1)
        lgf = lg_ref[0, h]
        lgb = lg_ref[1, h]
        for rows in [slice(0, C)] + [slice(C + t * c, C + (t + 1) * c) for t in range(nt)]:
            cosb = cos_ref[rows, :]
            sinb = sin_ref[rows, :]
            qr = _rope(q_ref[rows, :].astype(F32), cosb, sinb) * scale
            qr_ref[rows, :] = qr
            qs[rows, :] = qr.astype(BF16)
            kr = _rope(k_ref[rows, :].astype(F32), cosb, sinb)
            kf32[rows, :] = kr
            ks[rows, :] = kr.astype(BF16)
        gnv = gn_ref[...]
        dec, _, _, _ = _ret_states(kf32, vs, lgf, lgb, C, c, nt, hf, hb)
        rc = (lax.broadcasted_iota(jnp.int32, (c, c), 0) - lax.broadcasted_iota(jnp.int32, (c, c), 1)).astype(F32)
        w_diag = _ret_weights(lgf, lgb, rc)
        for t in range(nt):
            rows = slice(C + t * c, C + (t + 1) * c)
            qt = qs[rows, :]
            s = _dot_nt(qt, ks[rows, :])
            o = (_dot((s * w_diag).astype(BF16), vs[rows, :])
                 + dec.q_f * _dot(qt, hf[t]) + dec.q_b * _dot(qt, hb[t]))
            o_ref[t * c:(t + 1) * c, :] = o
            mu = jnp.mean(o, axis=-1, keepdims=True)
            oc = o - mu
            var = jnp.mean(oc * oc, axis=-1, keepdims=True)
            yh = oc * lax.rsqrt(var + NORM_EPS)
            g = g_ref[rows, :].astype(F32)
            lat_ref[t * c:(t + 1) * c, :] = ((yh * gnv) * (g * _sigmoid(g))).astype(BF16)

    def col(seg):
        return pl.BlockSpec((None, T, RET_DIM), lambda b, h, seg=seg: (b, 0, seg * RET_HEADS + h))

    return _grid_call(
        body, name="ret_fwd", grid=(B, RET_HEADS),
        out_shape=(jax.ShapeDtypeStruct((B, N, RET_WIDTH), F32), jax.ShapeDtypeStruct((B, N, RET_WIDTH), BF16),
                   jax.ShapeDtypeStruct((B, T, RET_WIDTH), F32), jax.ShapeDtypeStruct((B, T, RET_WIDTH), F32)),
        in_specs=[pl.BlockSpec(memory_space=pltpu.SMEM), col(0), col(1), col(2), col(3),
                  pl.BlockSpec((T, RET_DIM), lambda b, h: (0, 0)), pl.BlockSpec((T, RET_DIM), lambda b, h: (0, 0)),
                  pl.BlockSpec((1, RET_DIM), lambda b, h: (0, h))],
        out_specs=(pl.BlockSpec((None, N, RET_DIM), lambda b, h: (b, 0, h)),
                   pl.BlockSpec((None, N, RET_DIM), lambda b, h: (b, 0, h)),
                   pl.BlockSpec((None, T, RET_DIM), lambda b, h: (b, 0, h)),
                   pl.BlockSpec((None, T, RET_DIM), lambda b, h: (b, 0, h))),
        scratch_shapes=[pltpu.VMEM((T, RET_DIM), BF16)] * 2 + [pltpu.VMEM((nt, RET_DIM, RET_DIM), BF16)] * 2,
        args=(lg, proj, proj, proj, proj, cos, sin, gn))


def _ret_bwd(proj, q_rot, k_rot, cos, sin, lg, gn, o, dlat, n_ctx, after):
    B, T, _ = proj.shape
    C = n_ctx
    N = T - C
    c = _div_tile(N, 256, 16)
    nt = N // c
    scale = RET_DIM ** -0.5

    def lat(t):
        return slice(C + t * c, C + (t + 1) * c)

    def body(lg_ref, qf32, kf32, vs, g_ref, cos_ref, sin_ref, gn_ref, o_ref, dl_ref,
             d_ref, dgn_ref, dlg_ref, qs, ks, dos, hf, hb, hfa, hba, gf_s, gb_s):
        h = pl.program_id(1)
        lgf = lg_ref[0, h]
        lgb = lg_ref[1, h]
        gnv = gn_ref[...]

        def fold(a):
            return jnp.sum(a.reshape(a.shape[0] // SUBLANES, SUBLANES, a.shape[1]), axis=0)

        for rows in [slice(0, C)] + [lat(t) for t in range(nt)]:
            qs[rows, :] = qf32[rows, :].astype(BF16)
            ks[rows, :] = kf32[rows, :].astype(BF16)

        dgn = jnp.zeros((1, RET_DIM), F32)
        for t in range(nt):
            lrows = slice(t * c, (t + 1) * c)
            ov = o_ref[lrows, :]
            mu = jnp.mean(ov, axis=-1, keepdims=True)
            oc = ov - mu
            var = jnp.mean(oc * oc, axis=-1, keepdims=True)
            rstd = lax.rsqrt(var + NORM_EPS)
            yh = oc * rstd
            g = g_ref[lat(t), :].astype(F32)
            sg = _sigmoid(g)
            dl = dl_ref[lrows, :]
            d_ref[3, lat(t), :] = (dl * (yh * gnv) * (sg * (1.0 + g * (1.0 - sg)))).astype(BF16)
            dls = dl * (g * sg)
            dgn = dgn + jnp.sum(dls * yh, axis=0, keepdims=True)
            dyh = dls * gnv
            do = rstd * (dyh - jnp.mean(dyh, axis=-1, keepdims=True)
                         - yh * jnp.mean(dyh * yh, axis=-1, keepdims=True))
            dos[lrows, :] = do.astype(BF16)
        dgn_ref[...] = jnp.concatenate([dgn, jnp.zeros((SUBLANES - 1, RET_DIM), F32)], axis=0)
        d_ref[3, 0:C, :] = jnp.zeros((C, RET_DIM), BF16)
        d_ref[0, 0:C, :] = jnp.zeros((C, RET_DIM), BF16)

        dec, dec_c, step_f, step_b = _ret_states(kf32, vs, lgf, lgb, C, c, nt, hf, hb, hfa, hba)

        def zmat(t, qdec):
            return _dot_tn((qf32[lat(t), :] * qdec).astype(BF16), dos[t * c:(t + 1) * c, :])

        acc3f = jnp.zeros((RET_DIM, RET_DIM), F32)
        acc3b = jnp.zeros((RET_DIM, RET_DIM), F32)
        state = jnp.zeros((RET_DIM, RET_DIM), F32)
        for t in range(nt - 1, -1, -1):
            gf_s[t] = state.astype(BF16)
            z = zmat(t, dec.q_f)
            acc3f = acc3f + hfa[t] * z
            state = step_f * state + z
        gctx_f = state.astype(BF16)
        state = jnp.zeros((RET_DIM, RET_DIM), F32)
        for t in range(nt):
            gb_s[t] = state.astype(BF16)
            z = zmat(t, dec.q_b)
            acc3b = acc3b + hba[t] * z
            state = step_b * state + z
        gctx_b = state.astype(BF16)

        rc = (lax.broadcasted_iota(jnp.int32, (c, c), 0) - lax.broadcasted_iota(jnp.int32, (c, c), 1)).astype(F32)
        w_diag = _ret_weights(lgf, lgb, rc)
        wg_f = jnp.where(rc >= 0.0, w_diag * rc, 0.0)
        wg_b = jnp.where(rc < 0.0, -w_diag * rc, 0.0)
        accf = jnp.zeros((SUBLANES, RET_DIM), F32)
        accb = jnp.zeros((SUBLANES, RET_DIM), F32)
        gdf = jnp.zeros((SUBLANES, c), F32)
        gdb = jnp.zeros((SUBLANES, c), F32)
        for t in range(nt):
            rows = lat(t)
            qt = qs[rows, :]
            kt = ks[rows, :]
            vt = vs[rows, :]
            dot = dos[t * c:(t + 1) * c, :]
            s = _dot_nt(qt, kt)
            dp = _dot_nt(dot, vt)
            dv = _dot_tn((s * w_diag).astype(BF16), dot)
            ds = (dp * w_diag).astype(BF16)
            dq = _dot(ds, kt)
            dk = _dot_tn(ds, qt)
            gs = dp * s
            gdf = gdf + fold(gs * wg_f)
            gdb = gdb + fold(gs * wg_b)
            qv = qf32[rows, :]
            kv = kf32[rows, :]
            dq_f = dec.q_f * _dot_nt(dot, hf[t])
            dq_b = dec.q_b * _dot_nt(dot, hb[t])
            dk_f = dec.k_f * _dot_nt(vt, gf_s[t])
            dk_b = dec.k_b * _dot_nt(vt, gb_s[t])
            accf = accf + fold(dec.head * dq_f * qv) + fold(dec.tail * dk_f * kv)
            accb = accb + fold(dec.tail * dq_b * qv) + fold(dec.head * dk_b * kv)
            dv = dv + dec.k_f * _dot(kt, gf_s[t]) + dec.k_b * _dot(kt, gb_s[t])
            cosb = cos_ref[rows, :]
            sinb = sin_ref[rows, :]
            d_ref[0, rows, :] = _unrope((dq + dq_f + dq_b) * scale, cosb, sinb).astype(BF16)
            d_ref[1, rows, :] = _unrope(dk + dk_f + dk_b, cosb, sinb).astype(BF16)
            d_ref[2, rows, :] = dv.astype(BF16)
        kc = ks[0:C, :]
        vc = vs[0:C, :]
        kcv = kf32[0:C, :]
        dkc_f = dec_c.k_f * _dot_nt(vc, gctx_f)
        dkc_b = dec_c.k_b * _dot_nt(vc, gctx_b)
        accf = accf + fold(dec_c.tail * dkc_f * kcv)
        accb = accb + fold(dec_c.head * dkc_b * kcv)
        d_ref[1, 0:C, :] = (dkc_f + dkc_b).astype(BF16)
        d_ref[2, 0:C, :] = (dec_c.k_f * _dot(kc, gctx_f) + dec_c.k_b * _dot(kc, gctx_b)).astype(BF16)
        gf = jnp.sum(gdf) + jnp.sum(accf) + jnp.sum(acc3f)
        gb = jnp.sum(gdb) + jnp.sum(accb) + jnp.sum(acc3b)
        row = lax.broadcasted_iota(jnp.int32, (SUBLANES, LANES), 0)
        dlg_ref[...] = jnp.where(row == 0, gf, jnp.where(row == 1, gb, 0.0))

    def col(seg):
        return pl.BlockSpec((None, T, RET_DIM), lambda b, h, seg=seg: (b, 0, seg * RET_HEADS + h))

    def head(rows):
        return pl.BlockSpec((None, rows, RET_DIM), lambda b, h: (b, 0, h))

    return _grid_call(
        body, name="ret_bwd", grid=(B, RET_HEADS),
        out_shape=(jax.ShapeDtypeStruct((B, 4, T, RET_WIDTH), BF16),
                   jax.ShapeDtypeStruct((B, SUBLANES, RET_WIDTH), F32),
                   jax.ShapeDtypeStruct((B, RET_HEADS, SUBLANES, LANES), F32)),
        in_specs=[pl.BlockSpec(memory_space=pltpu.SMEM), head(T), head(T), col(2), col(3),
                  pl.BlockSpec((T, RET_DIM), lambda b, h: (0, 0)), pl.BlockSpec((T, RET_DIM), lambda b, h: (0, 0)),
                  pl.BlockSpec((1, RET_DIM), lambda b, h: (0, h)), head(N), head(N)],
        out_specs=(pl.BlockSpec((None, 4, T, RET_DIM), lambda b, h: (b, 0, 0, h)),
                   pl.BlockSpec((None, SUBLANES, RET_DIM), lambda b, h: (b, 0, h)),
                   pl.BlockSpec((None, None, SUBLANES, LANES), lambda b, h: (b, h, 0, 0))),
        scratch_shapes=[pltpu.VMEM((T, RET_DIM), BF16)] * 2 + [pltpu.VMEM((N, RET_DIM), BF16)]
                       + [pltpu.VMEM((nt, RET_DIM, RET_DIM), BF16)] * 2 + [pltpu.VMEM((nt, RET_DIM, RET_DIM), F32)] * 2
                       + [pltpu.VMEM((nt, RET_DIM, RET_DIM), BF16)] * 2,
        args=(lg, q_rot, k_rot, proj, proj, cos, sin, gn, o, dlat), after=after)


def _na_geometry(rows):
    kh = min(NA_KH, rows)
    return kh, kh * GRID_W


def _pair_select():
    lane = lax.broadcasted_iota(jnp.int32, (2 * GRID_W, LANES), 1)
    row = lax.broadcasted_iota(jnp.int32, (2 * GRID_W, LANES), 0)
    return (lane >= NA_DIM) == (row >= GRID_W)


def _pair_bias(bias_ref, dr0, kh):
    return jnp.concatenate(
        [jnp.concatenate([bias_ref[e, pl.ds(dr0 + 2 * m, 1)].reshape(GRID_W, LANES) for m in range(kh // 2)], axis=1)
         for e in range(2)], axis=0)


def _na_softmax(s_loc, s_ctx):
    mx = jnp.maximum(jnp.max(s_loc, axis=-1, keepdims=True), jnp.max(s_ctx, axis=-1, keepdims=True))
    p_loc = jnp.exp(s_loc - mx)
    p_ctx = jnp.exp(s_ctx - mx)
    den = jnp.sum(p_loc, axis=-1, keepdims=True) + jnp.sum(p_ctx, axis=-1, keepdims=True)
    return p_loc, p_ctx, den


def _na_fwd(proj, bias2, n_ctx):
    assert proj.dtype == BF16
    B, T, _ = proj.shape
    C = n_ctx
    N = T - C
    R = N // GRID_W
    kh, nk = _na_geometry(R)
    scale = NA_DIM ** -0.5
    base = (4 * RET_WIDTH) // LANES

    def body(q_ref, kb16, vb16, bias_ref, out_ref, p_ref):
        kc = kb16[0:C, :]
        vc = vb16[0:C, :]
        lane = lax.broadcasted_iota(jnp.int32, (GRID_W, LANES), 1)
        sel2 = _pair_select()

        def group(gi, carry):
            pre = []
            for u in range(NA_GROUP):
                r = gi * NA_GROUP + u
                bs = jnp.clip(r - kh // 2, 0, R - kh)
                dr0 = bs - r + (NA_KH - 1)
                q = q_ref[pl.ds(pl.multiple_of(C + r * GRID_W, GRID_W), GRID_W), :].astype(F32) * scale
                q2 = jnp.where(sel2, jnp.concatenate([q, q], axis=0), 0.0).astype(BF16)
                band = pl.ds(pl.multiple_of(C + bs * GRID_W, GRID_W), nk)
                s_loc = _dot_nt(q2, kb16[band, :]) + _pair_bias(bias_ref, dr0, kh)
                s_ctx = _dot_nt(q2, kc)
                pre.append((r, band, s_loc, s_ctx))
            mid = [(r, band) + _na_softmax(s_loc, s_ctx) for r, band, s_loc, s_ctx in pre]
            for r, band, p_loc, p_ctx, den in mid:
                inv = 1.0 / den
                pb_loc = (p_loc * inv).astype(BF16)
                pb_ctx = (p_ctx * inv).astype(BF16)
                p_ref[r, :, 0:nk] = pb_loc
                p_ref[r, :, nk:] = pb_ctx
                o2 = _dot(pb_loc, vb16[band, :]) + _dot(pb_ctx, vc)
                out_ref[pl.ds(pl.multiple_of(r * GRID_W, GRID_W), GRID_W), :] = jnp.where(
                    lane < NA_DIM, o2[:GRID_W], o2[GRID_W:]).astype(BF16)
            return carry

        lax.fori_loop(0, R // NA_GROUP, group, 0)

    def col(seg):
        return pl.BlockSpec((None, T, LANES), lambda b, p, seg=seg: (b, 0, base + seg * NA_PAIRS + p))

    return _grid_call(
        body, name="na_fwd", grid=(B, NA_PAIRS),
        out_shape=(jax.ShapeDtypeStruct((B, N, NA_WIDTH), BF16),
                   jax.ShapeDtypeStruct((B, NA_PAIRS, R, 2 * GRID_W, nk + C), BF16)),
        in_specs=[col(0), col(1), col(2),
                  pl.BlockSpec((2, 2 * NA_KH - 2, GRID_W, LANES), lambda b, p: (p, 0, 0, 0))],
        out_specs=(pl.BlockSpec((None, N, LANES), lambda b, p: (b, 0, p)),
                   pl.BlockSpec((None, None, R, 2 * GRID_W, nk + C), lambda b, p: (b, p, 0, 0, 0))),
        scratch_shapes=[],
        args=(proj, proj, proj, bias2))


def _na_bwd(proj, probs, dlat, n_ctx):
    assert proj.dtype == BF16
    B, T, _ = proj.shape
    C = n_ctx
    N = T - C
    R = N // GRID_W
    kh, nk = _na_geometry(R)
    scale = NA_DIM ** -0.5
    base = (4 * RET_WIDTH) // LANES

    def class_sums(tiles):
        n = tiles.shape[0]
        acc = None
        for v in range(GRID_W // SUBLANES):
            part = tiles[:, v * SUBLANES:(v + 1) * SUBLANES, :].reshape(n * SUBLANES, LANES)
            part = pltpu.roll(part, (NA_KW - 1 - v * SUBLANES) % LANES, 1)
            acc = part if acc is None else acc + part
        row = lax.broadcasted_iota(jnp.int32, acc.shape, 0)
        for bit in (1, 2, 4):
            acc = jnp.where((row & bit) != 0, pltpu.roll(acc, LANES - bit, 1), acc)
        return jnp.sum(acc.reshape(n, SUBLANES, LANES), axis=1)

    def body(q_ref, kb16, vb16, p_ref, dl_ref, d_ref, rr_ref, dkv, db_ref):
        b = pl.program_id(1)
        kc = kb16[0:C, :]
        vc = vb16[0:C, :]
        lane = lax.broadcasted_iota(jnp.int32, (GRID_W, LANES), 1)
        dkv[...] = jnp.zeros(dkv.shape, F32)
        d_ref[0, 0:C, :] = jnp.zeros((C, LANES), BF16)

        @pl.when(b == 0)
        def _():
            db_ref[...] = jnp.zeros(db_ref.shape, F32)

        sel2 = _pair_select()

        def group(gi, carry):
            pre = []
            for u in range(NA_GROUP):
                r = gi * NA_GROUP + u
                bs = jnp.clip(r - kh // 2, 0, R - kh)
                dr0 = bs - r + (NA_KH - 1)
                q = q_ref[pl.ds(pl.multiple_of(C + r * GRID_W, GRID_W), GRID_W), :].astype(F32) * scale
                do = dl_ref[pl.ds(pl.multiple_of(r * GRID_W, GRID_W), GRID_W), :]
                q2 = jnp.where(sel2, jnp.concatenate([q, q], axis=0), 0.0).astype(BF16)
                do2 = jnp.where(sel2, jnp.concatenate([do, do], axis=0), 0.0).astype(BF16)
                band = pl.ds(pl.multiple_of(C + bs * GRID_W, GRID_W), nk)
                dp_loc = _dot_nt(do2, vb16[band, :])
                dp_ctx = _dot_nt(do2, vc)
                pre.append((r, dr0, band, q2, do2, dp_loc, dp_ctx))
            mid = []
            for r, dr0, band, q2, do2, dp_loc, dp_ctx in pre:
                pb_loc = p_ref[r, :, 0:nk]
                pb_ctx = p_ref[r, :, nk:]
                p_loc = pb_loc.astype(F32)
                p_ctx = pb_ctx.astype(F32)
                delta = (jnp.sum(p_loc * dp_loc, axis=-1, keepdims=True)
                         + jnp.sum(p_ctx * dp_ctx, axis=-1, keepdims=True))
                ds_loc = p_loc * (dp_loc - delta)
                ds_ctx = p_ctx * (dp_ctx - delta)
                mid.append((r, dr0, band, q2, do2, pb_loc, pb_ctx, ds_loc, ds_ctx))
            for r, dr0, band, q2, do2, pb_loc, pb_ctx, ds_loc, ds_ctx in mid:
                dsb_loc = ds_loc.astype(BF16)
                dsb_ctx = ds_ctx.astype(BF16)
                dq2 = _dot(dsb_loc, kb16[band, :]) + _dot(dsb_ctx, kc)
                d_ref[0, pl.ds(pl.multiple_of(C + r * GRID_W, GRID_W), GRID_W), :] = (jnp.where(
                    lane < NA_DIM, dq2[:GRID_W], dq2[GRID_W:]) * scale).astype(BF16)
                dkv[0, band, :] += _dot_tn(dsb_loc, q2)
                dkv[1, band, :] += _dot_tn(pb_loc, do2)
                dkv[0, 0:C, :] += _dot_tn(dsb_ctx, q2)
                dkv[1, 0:C, :] += _dot_tn(pb_ctx, do2)
                for e in range(2):
                    for m in range(kh // 2):
                        db_ref[e, pl.ds(dr0 + 2 * m, 1)] += ds_loc[e * GRID_W:(e + 1) * GRID_W,
                                                                   m * LANES:(m + 1) * LANES].reshape(1, GRID_W, LANES)
            return carry

        lax.fori_loop(0, R // NA_GROUP, group, 0)
        d_ref[1] = dkv[0].astype(BF16)
        d_ref[2] = dkv[1].astype(BF16)

        @pl.when(b == B - 1)
        def _():
            for e in range(2):
                rr_ref[e] = class_sums(db_ref[e])

    def col(seg):
        return pl.BlockSpec((None, T, LANES), lambda p, b, seg=seg: (b, 0, base + seg * NA_PAIRS + p))

    return _grid_call(
        body, name="na_bwd", grid=(NA_PAIRS, B),
        out_shape=(jax.ShapeDtypeStruct((B, 3, T, NA_WIDTH), BF16),
                   jax.ShapeDtypeStruct((NA_HEADS, 2 * NA_KH - 2, LANES), F32)),
        in_specs=[col(0), col(1), col(2),
                  pl.BlockSpec((None, None, R, 2 * GRID_W, nk + C), lambda p, b: (b, p, 0, 0, 0)),
                  pl.BlockSpec((None, N, LANES), lambda p, b: (b, 0, p))],
        out_specs=(pl.BlockSpec((None, 3, T, LANES), lambda p, b: (b, 0, 0, p)),
                   pl.BlockSpec((2, 2 * NA_KH - 2, LANES), lambda p, b: (p, 0, 0))),
        scratch_shapes=[pltpu.VMEM((2, T, LANES), F32), pltpu.VMEM((2, 2 * NA_KH - 2, GRID_W, LANES), F32)],
        args=(proj, proj, proj, probs, dlat))


def _dense_core(lat_ret, lat_na, x, tgt, modl, g_post_mix, g_pre_mlp, g_post_mlp, w_out, w1, w2):
    B, N, D = x.shape
    F = w1.shape[1]
    wout_rows, w1_cols, w2_rows = w_out.shape[0] // N_DEV, w1.shape[1] // N_DEV, w2.shape[0] // N_DEV
    mixw = w_out.shape[0]
    half = mixw // 2
    tm = _div_tile(N, 256, 16)
    nt = N // tm
    fc = _div_tile(F, 1024, LANES)

    def body(lr_ref, ln_ref, x_ref, t_ref, gt1_ref, sh2_ref, sc2_ref, gt2_ref, gpm_ref, gpre_ref, gpo_ref,
             wout_part, w1_part, w2_part,
             dy1_ref, dlr_ref, dln_ref, dmix_ref, h2_ref, a_ref, du_ref, dz_ref, red_ref, wout_hbm, w1_hbm, w2_hbm,
             wout_v, w1_v, w2_v, u_s, sems, fsend, frecv):
        @pl.when((pl.program_id(0) == 0) & (pl.program_id(1) == 0))
        def _():
            relay = [(_row_block(wout_part, wout_rows), _row_block(wout_hbm, wout_rows)),
                     (_col_block(w1_part, w1_cols), _col_block(w1_hbm, w1_cols)),
                     (_row_block(w2_part, w2_rows), _row_block(w2_hbm, w2_rows))]
            _forward_start(relay, fsend, frecv)
            _forward_wait(relay, fsend, frecv)
            cps = [pltpu.make_async_copy(wout_hbm, wout_v, sems.at[0]),
                   pltpu.make_async_copy(w1_hbm, w1_v, sems.at[1]),
                   pltpu.make_async_copy(w2_hbm, w2_v, sems.at[2])]
            for cp in cps:
                cp.start()
            for cp in cps:
                cp.wait()

        @pl.when(pl.program_id(1) == 0)
        def _():
            red_ref[...] = jnp.zeros(red_ref.shape, F32)

        gt1 = gt1_ref[...]
        sh2 = sh2_ref[...]
        sc2 = sc2_ref[...]
        gt2 = gt2_ref[...]
        gpm = gpm_ref[...]
        gpre = gpre_ref[...]
        gpo = gpo_ref[...]

        def rowmean(a):
            return jnp.mean(a, axis=-1, keepdims=True)

        def colsum(a):
            return jnp.sum(a, axis=0, keepdims=True)

        mix_gain = gt1 * gpm
        mlp_in_gain = gpre * (1.0 + sc2)
        mlp_out_gain = gt2 * gpo
        mix = _dot(lr_ref[...], wout_v[0:half, :]) + _dot(ln_ref[...], wout_v[half:, :])
        x = x_ref[...]
        rm = lax.rsqrt(rowmean(mix * mix) + NORM_EPS)
        mh = mix * rm
        y1 = x + mh * mix_gain
        r1 = lax.rsqrt(rowmean(y1 * y1) + NORM_EPS)
        xh = y1 * r1
        h2b = (xh * mlp_in_gain + sh2).astype(BF16)
        h2_ref[...] = h2b
        z = jnp.zeros((tm, D), F32)
        for c0 in range(0, F, fc):
            u = _dot(h2b, w1_v[:, c0:c0 + fc])
            u_s[:, c0:c0 + fc] = u
            ru = jnp.maximum(u, 0.0)
            ab = (ru * ru).astype(BF16)
            a_ref[:, c0:c0 + fc] = ab
            z = z + _dot(ab, w2_v[c0:c0 + fc, :])
        r2 = lax.rsqrt(rowmean(z * z) + NORM_EPS)
        zh = z * r2
        y2 = y1 + zh * mlp_out_gain
        err = y2 - t_ref[...]
        loss = 0.5 * jnp.sum(rowmean(err * err))
        dy2 = err * (1.0 / D)
        s_out = colsum(dy2 * zh)
        red_ref[2:3, :] += s_out * gpo
        red_ref[6:7, :] += s_out * gt2
        dzh = dy2 * mlp_out_gain
        dz = r2 * (dzh - zh * rowmean(dzh * zh))
        dzb = dz.astype(BF16)
        dz_ref[...] = dzb
        dh2 = jnp.zeros((tm, D), F32)
        for c0 in range(0, F, fc):
            da = _dot_nt(dzb, w2_v[c0:c0 + fc, :])
            dub = (da * (2.0 * jnp.maximum(u_s[:, c0:c0 + fc], 0.0))).astype(BF16)
            du_ref[:, c0:c0 + fc] = dub
            dh2 = dh2 + _dot_nt(dub, w1_v[:, c0:c0 + fc])
        s_in = colsum(dh2 * xh)
        red_ref[3:4, :] += s_in * gpre
        red_ref[4:5, :] += colsum(dh2)
        red_ref[5:6, :] += s_in * (1.0 + sc2)
        dxh = dh2 * mlp_in_gain
        dy1 = dy2 + r1 * (dxh - xh * rowmean(dxh * xh))
        dy1_ref[...] = dy1
        s_mix = colsum(dy1 * mh)
        red_ref[0:1, :] += s_mix * gpm
        red_ref[1:2, :] += s_mix * gt1
        dmh = dy1 * mix_gain
        dmix = (rm *(dmh - mh * rowmean(dmh * mh))).astype(BF16)
        dmix_ref[...] = dmix
        dlr_ref[...] = _dot_nt(dmix, wout_v[0:half, :])
        dln_ref[...] = _dot_nt(dmix, wout_v[half:, :])
        red_ref[7:8, :] += jnp.zeros((1, D), F32) + loss

    def tok(w):
        return pl.BlockSpec((None, tm, w), lambda b, t: (b, t, 0))

    def mod(k):
        return pl.BlockSpec((None, None, 1, D), lambda b, t, k=k: (b, k, 0, 0))

    def vec():
        return pl.BlockSpec((1, D), lambda b, t: (0, 0))

    return pl.pallas_call(
        body, name="dense_core", grid=(B, nt),
        out_shape=(jax.ShapeDtypeStruct((B, N, D), F32), jax.ShapeDtypeStruct((B, N, half), F32),
                   jax.ShapeDtypeStruct((B, N, half), F32), jax.ShapeDtypeStruct((B, N, D), BF16),
                   jax.ShapeDtypeStruct((B, N, D), BF16), jax.ShapeDtypeStruct((B, N, F), BF16),
                   jax.ShapeDtypeStruct((B, N, F), BF16), jax.ShapeDtypeStruct((B, N, D), BF16),
                   jax.ShapeDtypeStruct((B, SUBLANES, D), F32),
                   jax.ShapeDtypeStruct(w_out.shape, w_out.dtype), jax.ShapeDtypeStruct(w1.shape, w1.dtype),
                   jax.ShapeDtypeStruct(w2.shape, w2.dtype)),
        in_specs=[tok(half), tok(half), tok(D), tok(D), mod(2), mod(3), mod(4), mod(5), vec(), vec(), vec(),
                  _any(), _any(), _any()],
        out_specs=(tok(D), tok(half), tok(half), tok(D), tok(D), tok(F), tok(F), tok(D),
                   pl.BlockSpec((None, SUBLANES, D), lambda b, t: (b, 0, 0)), _any(), _any(), _any()),
        scratch_shapes=[pltpu.VMEM((mixw, D), BF16), pltpu.VMEM((D, F), BF16), pltpu.VMEM((F, D), BF16),
                        pltpu.VMEM((tm, F), F32), pltpu.SemaphoreType.DMA((3,)),
                        pltpu.SemaphoreType.DMA((3, 3)), pltpu.SemaphoreType.DMA((3, 3))],
        input_output_aliases={11: 9, 12: 10, 13: 11},
        compiler_params=_params("arbitrary", "arbitrary"),
    )(lat_ret, lat_na, x, tgt, modl, modl, modl, modl, g_post_mix, g_pre_mlp, g_post_mlp, w_out, w1, w2)[:9]


def _inproj_bwd(dret, dna, x, ctx, dy1, modl, g1, w_in_t, after):
    B, N, D = x.shape
    n_ctx = ctx.shape[1]
    T = n_ctx + N
    tm = _div_tile(n_ctx, 256, 16)
    nct, ctx_spec, lat_spec = _token_tiles(n_ctx, tm)
    nt = T // tm
    nseg_r = dret.shape[1]
    nseg_n = dna.shape[1]
    nw = w_in_t.shape[0]

    def body(*refs):
        seg_refs = refs[:nseg_r + nseg_n]
        c_ref, x_ref, dy1_ref, sc_ref, g_ref, w_ref, dx_ref, red_ref = refs[nseg_r + nseg_n:]
        t = pl.program_id(1)
        dh = jnp.zeros((tm, D), F32)
        for s, ref in enumerate(seg_refs):
            dh = dh + _dot(ref[...], w_ref[s * SEG:(s + 1) * SEG, :])
        x = jnp.where(t < nct, c_ref[...], x_ref[...])
        g = g_ref[...]
        r = lax.rsqrt(jnp.mean(x * x, axis=-1, keepdims=True) + NORM_EPS)
        xh = x * r
        gain = 1.0 + sc_ref[...]
        s_in = jnp.sum(dh * xh, axis=0, keepdims=True)
        red_ref[0:1, :] = jnp.sum(dh, axis=0, keepdims=True)
        red_ref[1:2, :] = s_in * g
        red_ref[2:3, :] = s_in * gain
        red_ref[3:, :] = jnp.zeros((SUBLANES - 3, D), F32)
        dxh = dh * (g * gain)
        dx = r * (dxh - xh * jnp.mean(dxh * xh, axis=-1, keepdims=True))
        dx_ref[...] = dx + jnp.where(t >= nct, dy1_ref[...], 0.0)

    def mrow(b, t):
        return jnp.where(t < nct, B, b)

    def seg(s):
        return pl.BlockSpec((None, None, tm, SEG), lambda b, t, s=s: (b, s, t, 0))

    return _grid_call(
        body, name="inproj_bwd", grid=(B, nt),
        out_shape=(jax.ShapeDtypeStruct((B, N, D), F32), jax.ShapeDtypeStruct((B, nt, SUBLANES, D), F32)),
        in_specs=[seg(s) for s in range(nseg_r)] + [seg(s) for s in range(nseg_n)]
                 + [ctx_spec(D), lat_spec(D), lat_spec(D),
                    pl.BlockSpec((None, None, 1, D), lambda b, t: (mrow(b, t), 1, 0, 0)),
                    pl.BlockSpec((1, D), lambda b, t: (0, 0)),
                    pl.BlockSpec((nw, D), lambda b, t: (0, 0))],
        out_specs=(lat_spec(D), pl.BlockSpec((None, None, SUBLANES, D), lambda b, t: (b, t, 0, 0))),
        scratch_shapes=[], args=(*([dret] * nseg_r), *([dna] * nseg_n), ctx, x, dy1, modl, g1, w_in_t), after=after)


def _tn_matmul(lhs, rhs, name, rows_before=0, rows_after=0, into=None, after=None):
    B, S, T, W = lhs.shape
    nn = rhs.shape[-1]
    tk = _div_tile(T, 2304, LANES)
    bm = _div_tile(W, 1024, LANES)
    bn = _div_tile(nn, 1024, LANES)
    nkt = T // tk
    nk = B * nkt

    def body(l_ref, r_ref, *rest):
        o_ref, acc = rest[-2:]
        k = pl.program_id(3)

        @pl.when(k == 0)
        def _():
            acc[...] = jnp.zeros(acc.shape, F32)

        acc[...] += _dot_tn(l_ref[...].astype(BF16), r_ref[...].astype(BF16))

        @pl.when(k == nk - 1)
        def _():
            o_ref[...] = acc[...].astype(BF16)

    nwb = W // bm
    first = rows_before // bm
    extra = [a for a in (into, after) if a is not None]
    return pl.pallas_call(
        functools.partial(body), name=name, grid=(S, nwb, nn // bn, nk),
        out_shape=jax.ShapeDtypeStruct((rows_before + S * W + rows_after, nn), BF16),
        in_specs=[pl.BlockSpec((None, None, tk, bm), lambda s, i, j, k: (k // nkt, s, k % nkt, i)),
                  pl.BlockSpec((None, tk, bn), lambda s, i, j, k: (k // nkt, k % nkt, j))]
                 + [_any()] * len(extra),
        out_specs=pl.BlockSpec((bm, bn), lambda s, i, j, k: (first + s * nwb + i, j)),
        scratch_shapes=[pltpu.VMEM((bm, bn), F32)],
        input_output_aliases={} if into is None else {2: 0},
        compiler_params=_params("parallel", "parallel", "parallel", "arbitrary"),
    )(lhs, rhs, *extra)


class _SplitScatter:
    def __init__(self, gs, block_ofs, land_shapes, name, kind="scatter", masks=ALL_PEERS):
        self.n = n = len(gs)
        self.block_ofs, self.kind, self.masks = block_ofs, kind, masks
        if kind == "scatter":
            land_shapes = [(N_DEV,) + tuple(bs) for bs in land_shapes]
        hbm = pl.BlockSpec(memory_space=pltpu.HBM)
        sem = pl.BlockSpec(memory_space=pltpu.SEMAPHORE)

        def body(*refs):
            g_refs, land_refs = refs[:n], refs[n:2 * n]
            send_sems, recv_sems, own_sems = refs[2 * n:2 * n + 3]
            token = refs[-1]
            for own, pushes in self._copies(g_refs, land_refs, send_sems, recv_sems, own_sems, landing="sender"):
                own.start()
                for cp in pushes:
                    cp.start()
            token[...] = jnp.zeros_like(token)

        outs = pl.pallas_call(
            body, name=name,
            out_shape=(pltpu.SemaphoreType.DMA((n * (N_DEV - 1),)), pltpu.SemaphoreType.DMA((n * (N_DEV - 1),)),
                       pltpu.SemaphoreType.DMA((n,)))
                      + tuple(pltpu.HBM(g.shape, g.dtype) for g in gs)
                      + tuple(pltpu.HBM(s, g.dtype) for s, g in zip(land_shapes, gs))
                      + (jax.ShapeDtypeStruct((SUBLANES, LANES), F32),),
            in_specs=(hbm,) * (2 * n), out_specs=(sem,) * 3 + (hbm,) * (2 * n) + (_vmem(),),
            input_output_aliases={k: 3 + k for k in range(2 * n)},
            compiler_params=pltpu.CompilerParams(has_side_effects=pltpu.SideEffectType.DATAFLOW_SIDE_EFFECTING),
        )(*[pltpu.with_memory_space_constraint(g, pltpu.HBM) for g in gs],
          *[pltpu.with_memory_space_constraint(lax.empty(s, g.dtype), pltpu.HBM) for s, g in zip(land_shapes, gs)])
        self.sems, self.thru, self.token = outs[:3], outs[3:3 + 2 * n], outs[-1]

    def _copies(self, g_refs, land_refs, send_sems, recv_sems, own_sems, landing):
        me, peers = _me_and_peers()
        out = []
        for k in range(self.n):
            if self.kind == "scatter":
                src, dst = self.block_ofs[k](g_refs[k]), _slot(land_refs[k])
            else:
                src, dst = (lambda p, k=k: g_refs[k]), self.block_ofs[k](land_refs[k])
            own = pltpu.make_async_copy(src(me), dst(me), own_sems.at[k]) if landing == "sender" else None
            pushes = []
            for m in self.masks:
                dev, pid = peers[m - 1]
                i = k * (N_DEV - 1) + m - 1
                pushes.append(_remote(src(pid), dst(me if landing == "sender" else pid),
                                      send_sems.at[i], recv_sems.at[i], dev))
            out.append((own, pushes))
        return out


def _scatter_wait(scatters, after, name):
    hbm = pl.BlockSpec(memory_space=pltpu.HBM)
    sem = pl.BlockSpec(memory_space=pltpu.SEMAPHORE)
    n_arr = [2 * sc.n for sc in scatters]
    total = sum(n_arr)

    def body(*refs):
        arrs, sems = refs[:total], refs[total:total + 3 * len(scatters)]
        a0 = 0
        for j, sc in enumerate(scatters):
            g_refs, land_refs = arrs[a0:a0 + sc.n], arrs[a0 + sc.n:a0 + 2 * sc.n]
            a0 += 2 * sc.n
            send_sems, recv_sems, own_sems = sems[3 * j:3 * j + 3]
            for (own, sent), (_, got) in zip(sc._copies(g_refs, land_refs, send_sems, recv_sems, own_sems, "sender"),
                                             sc._copies(g_refs, land_refs, send_sems, recv_sems, own_sems, "receiver")):
                own.wait()
                for cp in sent:
                    cp.wait_send()
                for cp in got:
                    cp.wait_recv()

    operands = [a for sc in scatters for a in sc.thru]
    outs = pl.pallas_call(
        body, name=name,
        out_shape=tuple(pltpu.HBM(a.shape, a.dtype) for a in operands),
        in_specs=(hbm,) * total + (sem,) * (3 * len(scatters)) + (pl.BlockSpec(memory_space=pl.ANY),),
        out_specs=(hbm,) * total, input_output_aliases={k: k for k in range(total)},
        compiler_params=pltpu.CompilerParams(has_side_effects=pltpu.SideEffectType.DATAFLOW_SIDE_EFFECTING),
    )(*operands, *[s for sc in scatters for s in sc.sems], after)
    lands, a0 = [], 0
    for sc in scatters:
        lands.extend(outs[a0 + sc.n:a0 + 2 * sc.n])
        a0 += 2 * sc.n
    return lands


def _small_ar(mbuf, silu_all, w_ada, c_ctx, n_mod_rows, n_vec_rows):
    D = silu_all.shape[1]
    ncol = w_ada.shape[1]
    nm = mbuf.shape[2]
    srows = silu_all.shape[0]

    def body(mbuf, s_ref, w_ref, cc_ref, tot_ref, gb_ref, gw_ref, gc_ref, tbuf, dmx, cmrow, send3, recv3):
        me, _ = _me_and_peers()
        msum = mbuf[0]
        for k in range(1, N_DEV):
            msum = msum + mbuf[k]
        tot_ref[...] = msum[n_mod_rows:n_mod_rows + n_vec_rows]
        gb_ref[...] = jnp.sum(msum[0:n_mod_rows], axis=0, keepdims=True)
        loc = pl.ds(pl.multiple_of(me * ncol, ncol), ncol)
        for k in range(N_DEV):
            dmx[k * SUBLANES:(k + 1) * SUBLANES, :] = mbuf[k, :, loc]
        cmrow[...] = msum
        cm_loc = cmrow[n_mod_rows - 1:n_mod_rows, loc]
        dmx[N_DEV * SUBLANES:, :] = jnp.concatenate([cm_loc, jnp.zeros((SUBLANES - 1, ncol), F32)], axis=0)
        gw_ref[...] = _dot_tn(s_ref[...], dmx[...])
        tbuf[me] = _dot_nt(dmx[N_DEV * SUBLANES:, :], w_ref[...])
        _exchange(lambda p: tbuf.at[me], lambda p: tbuf.at[p], send3, recv3)
        tsum = tbuf[0]
        for k in range(1, N_DEV):
            tsum = tsum + tbuf[k]
        cc = cc_ref[...]
        sg = _sigmoid(cc)
        gc_ref[...] = tsum[0:1, :] * (sg * (1.0 + cc * (1.0 - sg)))

    return pl.pallas_call(
        body, name="small_ar",
        out_shape=(jax.ShapeDtypeStruct((n_vec_rows, nm), F32), jax.ShapeDtypeStruct((1, nm), F32),
                   jax.ShapeDtypeStruct((D, ncol), F32), jax.ShapeDtypeStruct((1, D), F32)),
        in_specs=[_vmem()] * 4, out_specs=(_vmem(),) * 4,
        scratch_shapes=[pltpu.VMEM((N_DEV, SUBLANES, D), F32), pltpu.VMEM((srows, ncol), F32),
                        pltpu.VMEM((SUBLANES, nm), F32)] + [pltpu.SemaphoreType.DMA((N_DEV - 1,))] * 2,
        compiler_params=pltpu.CompilerParams(vmem_limit_bytes=VMEM_LIMIT),
    )(mbuf, silu_all, w_ada, c_ctx.reshape(1, D))


def _adam_update(w, g, m, v):
    mn = ADAM_B1 * m + (1.0 - ADAM_B1) * g
    vn = ADAM_B2 * v + (1.0 - ADAM_B2) * (g * g)
    m_hat = mn / (1.0 - ADAM_B1 ** ADAM_STEP)
    v_hat = vn / (1.0 - ADAM_B2 ** ADAM_STEP)
    return -ADAM_LR * (m_hat / (jnp.sqrt(v_hat) + ADAM_EPS) + ADAM_WD * w), mn, vn


def _adamw(w, g, m, v, name):
    rows, cols = w.shape
    tr = _div_tile(rows, 512, SUBLANES)

    def body(w_ref, g_ref, m_ref, v_ref, d_ref, nm_ref, nv_ref):
        d_ref[...], nm_ref[...], nv_ref[...] = _adam_update(w_ref[...], g_ref[...], m_ref[...], v_ref[...])

    spec = pl.BlockSpec((tr, cols), lambda i: (i, 0))
    return pl.pallas_call(
        functools.partial(body), name=name, grid=(rows // tr,),
        out_shape=(jax.ShapeDtypeStruct((rows, cols), F32),) * 3,
        in_specs=[spec] * 4, out_specs=(spec,) * 3,
        compiler_params=_params("parallel"),
    )(w, g, m, v)


def _adamw_small(items, name):
    n = len(items)

    def body(*refs):
        ins, outs = refs[:4 * n], refs[4 * n:]
        for i in range(n):
            w_ref, g_ref, m_ref, v_ref = ins[4 * i:4 * i + 4]
            outs[3 * i][...], outs[3 * i + 1][...], outs[3 * i + 2][...] = _adam_update(
                w_ref[...], g_ref[...], m_ref[...], v_ref[...])

    outs = pl.pallas_call(
        body, name=name,
        out_shape=tuple(jax.ShapeDtypeStruct(it[0].shape, F32) for it in items for _ in range(3)),
        in_specs=[_vmem()] * (4 * n), out_specs=(_vmem(),) * (3 * n),
        compiler_params=pltpu.CompilerParams(vmem_limit_bytes=VMEM_LIMIT),
    )(*[a for it in items for a in it])
    return [tuple(outs[3 * i:3 * i + 3]) for i in range(n)]


def _sum_adamw(buf, w, m, v, name):
    _, rows, cols = buf.shape
    tr = _div_tile(rows, 256, 2 * SUBLANES)

    def body(b_ref, w_ref, m_ref, v_ref, g_ref, d_ref, nm_ref, nv_ref):
        g = b_ref[0].astype(F32)
        for k in range(1, N_DEV):
            g = g + b_ref[k].astype(F32)
        g_ref[...] = g
        d_ref[...], nm_ref[...], nv_ref[...] = _adam_update(w_ref[...], g, m_ref[...], v_ref[...])

    spec = pl.BlockSpec((tr, cols), lambda i: (i, 0))
    return pl.pallas_call(
        functools.partial(body), name=name, grid=(rows // tr,),
        out_shape=(jax.ShapeDtypeStruct((rows, cols), F32),) * 4,
        in_specs=[pl.BlockSpec((N_DEV, tr, cols), lambda i: (0, i, 0))] + [spec] * 3, out_specs=(spec,) * 4,
        compiler_params=_params("parallel"),
    )(buf, w, m, v)


def _rope_tables(n_ctx, n):
    n_freq = RET_DIM // 4
    inv = np.float32(ROPE_BASE) ** (-np.arange(n_freq, dtype=np.float32) / np.float32(n_freq))
    tok = np.arange(n)
    pos_r = (tok // GRID_W).astype(np.float32)
    pos_c = (tok % GRID_W).astype(np.float32)
    ang_r = (pos_r[:, None] * inv[None, :]).astype(np.float32)
    ang_c = (pos_c[:, None] * inv[None, :]).astype(np.float32)
    cos = np.concatenate([np.cos(ang_r), np.cos(ang_r), np.cos(ang_c), np.cos(ang_c)], axis=-1)
    sin = np.concatenate([-np.sin(ang_r), np.sin(ang_r), -np.sin(ang_c), np.sin(ang_c)], axis=-1)
    cos = np.concatenate([np.ones((n_ctx, RET_DIM), np.float32), cos], axis=0)
    sin = np.concatenate([np.zeros((n_ctx, RET_DIM), np.float32), sin], axis=0)
    return jnp.asarray(cos, F32), jnp.asarray(sin, F32)


def _na_tables():
    q = np.arange(GRID_W)[:, None]
    k = np.arange(GRID_W)[None, :]
    start = np.clip(q - NA_KW // 2, 0, GRID_W - NA_KW)
    valid = (k >= start) & (k < start + NA_KW)
    dc = np.clip(k - q + (NA_KW - 1), 0, 2 * NA_KW - 2)
    ncls = 2 * NA_KW - 1
    onehot = (dc[None] == np.arange(ncls)[:, None, None]) & valid[None]
    return onehot.astype(np.float32), valid


def _paired_bias(rpb, onehot, valid):
    ncls = onehot.shape[0]
    pair = np.zeros((2 * ncls, GRID_W, LANES), np.float32)
    pair[:ncls, :, :GRID_W] = onehot
    pair[ncls:, :, GRID_W:] = onehot
    rows = jnp.concatenate([rpb[:, :-1], rpb[:, 1:]], axis=-1)
    t = jnp.einsum("hdc,cqk->hdqk", rows, jnp.asarray(pair), precision=lax.Precision.HIGHEST)
    return jnp.where(jnp.asarray(np.tile(valid, (1, 2)))[None, None], t, NEG_INF)


def kernel(x, c, ctx, c_ctx, w_ada, b_ada, g_pre_mix, g_post_mix, g_pre_mlp, g_post_mlp, w_in, ret_decay, ret_gn, na_rpb, w_out, w_mlp1, w_mlp2, loss_target, m_c_ctx, m_w_ada, m_b_ada, m_g_pre_mix, m_g_post_mix, m_g_pre_mlp, m_g_post_mlp, m_w_in, m_ret_decay, m_ret_gn, m_na_rpb, m_w_out, m_w_mlp1, m_w_mlp2, v_c_ctx, v_w_ada, v_b_ada, v_g_pre_mix, v_g_post_mix, v_g_pre_mlp, v_g_post_mlp, v_w_in, v_ret_decay, v_ret_gn, v_na_rpb, v_w_out, v_w_mlp1, v_w_mlp2):
    B, N, D = x.shape
    C = ctx.shape[1]
    T = C + N

    silu_all, mods_g, win_b, wout_l, w1_l, w2_l = _mod_gather(c, c_ctx, w_ada[0], b_ada, w_in[0].T, w_out[0],
                                                             w_mlp1[0], w_mlp2[0])
    mods_mine = mods_g.transpose(1, 0, 2).reshape(mods_g.shape[1], N_MOD * D)
    modl = jnp.concatenate([mods_mine[:B], mods_mine[SUBLANES:SUBLANES + 1]], axis=0)
    modl = modl.reshape(B + 1, N_MOD, 1, D)
    rin = w_in.shape[2]
    rout, c1, r2 = wout_l.shape[0], w1_l.shape[1], w2_l.shape[0]

    def rows_of(n):
        return lambda ref: _row_block(ref, n)

    def cols_of(n):
        return lambda ref: _col_block(ref, n)

    cos, sin = _rope_tables(C, N)
    onehot, valid = _na_tables()
    bias2 = _paired_bias(na_rpb[0], onehot, valid)
    lg = jax.nn.log_sigmoid(ret_decay[0].astype(F32))

    ag = _SplitScatter([wout_l, w1_l, w2_l], [rows_of(rout), cols_of(c1), rows_of(r2)],
                       [(N_DEV * rout, D), (D, N_DEV * c1), (N_DEV * r2, D)], "ag_mlp_start",
                       kind="gather", masks=SIBLING + ICI_SAME_CORE)
    h_all, proj = _inproj_fwd(x, ctx, modl, g_pre_mix, win_b, after=ag.token)
    o_ret, lat_ret, q_rot, k_rot = _ret_fwd(proj, cos, sin, lg, ret_gn, C)
    lat_na, na_probs = _na_fwd(proj, bias2, C)
    wout_part, w1_part, w2_part = _scatter_wait([ag], lat_na, "ag_mlp_wait")

    (dy1, dlat_ret, dlat_na, dmix, h2, act, du, dz, red_d) = _dense_core(
        lat_ret, lat_na, x, loss_target, modl, g_post_mix, g_pre_mlp, g_post_mlp, wout_part, w1_part, w2_part)

    gw1_p = _tn_matmul(h2[:, None], du, "gw_mlp1")
    gw2_p = _tn_matmul(act[:, None], dz, "gw_mlp2")
    rs_mlp = _SplitScatter([gw1_p, gw2_p], [cols_of(c1), rows_of(r2)], [(D, c1), (r2, D)], "rs_mlp_start")
    gw_out_p = _tn_matmul(lat_ret[:, None], dmix, "gw_out_ret", rows_after=lat_na.shape[-1], after=rs_mlp.token)
    gw_out_p = _tn_matmul(lat_na[:, None], dmix, "gw_out_na", rows_before=lat_ret.shape[-1], into=gw_out_p)

    dret, dgn_p, dlg_p = _ret_bwd(proj, q_rot, k_rot, cos, sin, lg, ret_gn, o_ret, dlat_ret, C, after=rs_mlp.token)
    dna, rr = _na_bwd(proj, na_probs, dlat_na, C)
    ret_cols, na_cols = dret.shape[1] * dret.shape[3], dna.shape[1] * dna.shape[3]
    gwin_t_p = _tn_matmul(dret, h_all, "gw_in_ret", rows_after=na_cols)
    gwin_t_p = _tn_matmul(dna, h_all, "gw_in_na", rows_before=ret_cols, into=gwin_t_p)
    rs_in = _SplitScatter([gwin_t_p, gw_out_p], [rows_of(rin), rows_of(rout)], [(rin, D), (rout, D)], "rs_w_in_start")
    grad_x, red_i = _inproj_bwd(dret, dna, x, ctx, dy1, modl, g_pre_mix, win_b, after=rs_in.token)

    rd = red_d
    nct = red_i.shape[1] * C // T
    ri_ctx = red_i[:, :nct].sum(axis=(0, 1))
    ri_lat = red_i[:, nct:].sum(axis=1)
    d_mods = jnp.concatenate([ri_lat[:, 0], ri_lat[:, 1], rd[:, 0], rd[:, 4], rd[:, 3], rd[:, 2]], axis=-1)
    d_cmods = jnp.concatenate([ri_ctx[0], ri_ctx[1], jnp.zeros(((N_MOD - 2) * D,), F32)])[None]
    dg_pre_mix = ri_lat[:, 2].sum(axis=0) + ri_ctx[2]
    dg_post_mix = rd[:, 1].sum(axis=0)
    dg_pre_mlp = rd[:, 5].sum(axis=0)
    dg_post_mlp = rd[:, 6].sum(axis=0)
    loss_p = rd[:, 7, 0].sum()
    d_gn = dgn_p[:, 0].sum(axis=0)
    d_lg = dlg_p[:, :, :2, 0].sum(axis=0).T
    d_decay = d_lg * jax.nn.sigmoid(-ret_decay[0].astype(F32))
    ncls = 2 * NA_KW - 1
    d_rpb = (jnp.pad(rr[:, :, :ncls], ((0, 0), (0, 1), (0, 0)))
             + jnp.pad(rr[:, :, GRID_W:GRID_W + ncls], ((0, 0), (1, 0), (0, 0))))
    d_rpb32 = jnp.pad(d_rpb, ((0, 0), (0, 0), (0, 32 - ncls)))
    nm = N_MOD * D
    vec_rows = [jnp.concatenate([dg_pre_mix, dg_post_mix, dg_pre_mlp, dg_post_mlp, d_gn,
                                 jnp.pad(d_decay.reshape(-1), (0, LANES - d_decay.size)),
                                 jnp.full((LANES,), loss_p, F32)]),
                d_rpb32.reshape(-1)]
    n_vec_rows = len(vec_rows)
    assert B + 1 + n_vec_rows <= SUBLANES and all(r.shape[0] <= nm for r in vec_rows)
    vec = jnp.stack([jnp.pad(r, (0, nm - r.shape[0])) for r in vec_rows])
    dm_slot = jnp.concatenate([d_mods, d_cmods, vec, jnp.zeros((SUBLANES - B - 1 - n_vec_rows, nm), F32)], axis=0)
    def whole(ref):
        return lambda p: ref

    small = _SplitScatter([dm_slot], [whole], [dm_slot.shape], "small_start")
    land_1, land_2 = _scatter_wait([rs_mlp], small.token, "rs_mlp_wait")
    fused = {"w_mlp1": _sum_adamw(land_1, w_mlp1[0], m_w_mlp1[0], v_w_mlp1[0], "sum_adamw_w_mlp1"),
             "w_mlp2": _sum_adamw(land_2, w_mlp2[0], m_w_mlp2[0], v_w_mlp2[0], "sum_adamw_w_mlp2")}
    land_in, land_out = _scatter_wait([rs_in], fused["w_mlp2"][0], "rs_w_in_wait")
    fused["w_out"] = _sum_adamw(land_out, w_out[0], m_w_out[0], v_w_out[0], "sum_adamw_w_out")
    win_upd = _sum_adamw(land_in, w_in[0].T, m_w_in[0].T, v_w_in[0].T, "sum_adamw_w_in")
    fused["w_in"] = [a.T for a in win_upd]
    (mbuf,) = _scatter_wait([small], win_upd[0], "small_wait")
    tot, g_b_ada, g_w_ada, g_c_ctx = _small_ar(mbuf, silu_all, w_ada[0], c_ctx, B + 1, n_vec_rows)
    flat = tot[0]
    o0 = 0
    g_pre_mix_g = flat[o0:o0 + D]; o0 += D
    g_post_mix_g = flat[o0:o0 + D]; o0 += D
    g_pre_mlp_g = flat[o0:o0 + D]; o0 += D
    g_post_mlp_g = flat[o0:o0 + D]; o0 += D
    g_gn = flat[o0:o0 + RET_WIDTH]; o0 += RET_WIDTH
    g_decay = flat[o0:o0 + 2 * RET_HEADS].reshape(2, RET_HEADS); o0 += LANES
    loss = flat[o0]
    g_rpb = tot[1, :d_rpb32.size].reshape(d_rpb32.shape)[:, :, :ncls]

    grads = {
        "c_ctx": g_c_ctx.reshape(c_ctx.shape), "w_ada": g_w_ada[None], "b_ada": g_b_ada.reshape(b_ada.shape),
        "g_pre_mix": g_pre_mix_g[None], "g_post_mix": g_post_mix_g[None], "g_pre_mlp": g_pre_mlp_g[None],
        "g_post_mlp": g_post_mlp_g[None], "w_in": fused["w_in"][0][None], "ret_decay": g_decay[None], "ret_gn": g_gn[None],
        "na_rpb": g_rpb[None], "w_out": fused["w_out"][0][None], "w_mlp1": fused["w_mlp1"][0][None],
        "w_mlp2": fused["w_mlp2"][0][None],
    }
    weights = dict(c_ctx=c_ctx, w_ada=w_ada, b_ada=b_ada, g_pre_mix=g_pre_mix, g_post_mix=g_post_mix,
                   g_pre_mlp=g_pre_mlp, g_post_mlp=g_post_mlp, w_in=w_in, ret_decay=ret_decay, ret_gn=ret_gn,
                   na_rpb=na_rpb, w_out=w_out, w_mlp1=w_mlp1, w_mlp2=w_mlp2)
    m_in = dict(c_ctx=m_c_ctx, w_ada=m_w_ada, b_ada=m_b_ada, g_pre_mix=m_g_pre_mix, g_post_mix=m_g_post_mix,
                g_pre_mlp=m_g_pre_mlp, g_post_mlp=m_g_post_mlp, w_in=m_w_in, ret_decay=m_ret_decay,
                ret_gn=m_ret_gn, na_rpb=m_na_rpb, w_out=m_w_out, w_mlp1=m_w_mlp1, w_mlp2=m_w_mlp2)
    v_in = dict(c_ctx=v_c_ctx, w_ada=v_w_ada, b_ada=v_b_ada, g_pre_mix=v_g_pre_mix, g_post_mix=v_g_post_mix,
                g_pre_mlp=v_g_pre_mlp, g_post_mlp=v_g_post_mlp, w_in=v_w_in, ret_decay=v_ret_decay,
                ret_gn=v_ret_gn, na_rpb=v_na_rpb, w_out=v_w_out, w_mlp1=v_w_mlp1, w_mlp2=v_w_mlp2)
    names = list(weights)
    deltas, new_m, new_v = {}, {}, {}
    def as_2d(n):
        shp = weights[n].shape
        two_d = (-1, shp[-1]) if len(shp) > 1 else (1, shp[0])
        return [a.reshape(two_d) for a in (weights[n], grads[n], m_in[n], v_in[n])]

    def unsharded(n):
        shp = weights[n].shape
        view = shp[1:] if len(shp) > 2 else (shp if len(shp) == 2 else (1, shp[0]))
        return [a.reshape(view) for a in (weights[n], grads[n], m_in[n], v_in[n])]

    small = [n for n in names if n not in fused and weights[n].size <= 65536]
    updated = dict(zip(small, _adamw_small([unsharded(n) for n in small], "adamw_small")))
    for n in names:
        if n in fused:
            updated[n] = fused[n][1:]
        elif n not in updated:
            updated[n] = _adamw(*as_2d(n), "adamw_" + n)
        deltas[n], new_m[n], new_v[n] = (a.reshape(weights[n].shape) for a in updated[n])
    return (loss, grad_x, *[grads[n] for n in names], *[deltas[n] for n in names],
            *[new_m[n] for n in names], *[new_v[n] for n in names])
```

```python
import functools

import numpy as np
import jax
import jax.numpy as jnp
from jax import lax
from jax.experimental import pallas as pl
from jax.experimental.pallas import tpu as pltpu

F32 = jnp.float32
BF16 = jnp.bfloat16
MESH = pl.DeviceIdType.MESH

N_DEV = 8
LANES = 128
SUBLANES = 8
VMEM_LIMIT = 60 * 1024 * 1024

GRID_W = 64
RET_HEADS = 4
RET_DIM = 128
RET_WIDTH = RET_HEADS * RET_DIM
NA_HEADS = 8
NA_DIM = 64
NA_WIDTH = NA_HEADS * NA_DIM
NA_PAIRS = NA_HEADS // 2
NA_KH = 8
NA_KW = 16
NA_GROUP = 8
SEG = 512
ROPE_BASE = 10000.0
NORM_EPS = 1e-6
NEG_INF = -1e30
N_MOD = 6

ADAM_LR = 0.001
ADAM_B1 = 0.9
ADAM_B2 = 0.999
ADAM_EPS = 1e-08
ADAM_WD = 0.01
ADAM_STEP = 10


def _dot(a, b):
    return lax.dot_general(a, b, (((1,), (0,)), ((), ())), preferred_element_type=F32)


def _dot_nt(a, b):
    return lax.dot_general(a, b, (((1,), (1,)), ((), ())), preferred_element_type=F32)


def _dot_tn(a, b):
    return lax.dot_general(a, b, (((0,), (0,)), ((), ())), preferred_element_type=F32)


def _sigmoid(x):
    return 1.0 / (1.0 + jnp.exp(-x))


def _div_tile(n, cap, mult):
    if n <= cap:
        return n
    for t in range(cap - cap % mult, 0, -mult):
        if n % t == 0:
            return t
    raise ValueError(f"no tile for {n}")


def _params(*sem):
    return pltpu.CompilerParams(dimension_semantics=tuple(sem) if sem else None,
                                vmem_limit_bytes=VMEM_LIMIT)


def _vmem():
    return pl.BlockSpec(memory_space=pltpu.VMEM)


def _any():
    return pl.BlockSpec(memory_space=pl.ANY)


def _me_and_peers():
    x, y, c = lax.axis_index("x"), lax.axis_index("y"), lax.axis_index("c")
    me = 4 * x + 2 * y + c
    peers = []
    for m in range(1, N_DEV):
        px = 1 - x if (m >> 2) & 1 else x
        py = 1 - y if (m >> 1) & 1 else y
        pc = 1 - c if m & 1 else c
        peers.append(((px, py, pc), 4 * px + 2 * py + pc))
    return me, peers


def _exchange(src_for, dst_from, send_sems, recv_sems):
    me, peers = _me_and_peers()
    sent = []
    for i, (dev, pid) in enumerate(peers):
        cp = pltpu.make_async_remote_copy(src_ref=src_for(pid), dst_ref=dst_from(me),
                                          send_sem=send_sems.at[i], recv_sem=recv_sems.at[i],
                                          device_id=dev, device_id_type=MESH)
        cp.start()
        sent.append(cp)
    for i, (dev, pid) in enumerate(peers):
        pltpu.make_async_remote_copy(src_ref=src_for(pid), dst_ref=dst_from(pid),
                                     send_sem=send_sems.at[i], recv_sem=recv_sems.at[i],
                                     device_id=dev, device_id_type=MESH).wait_recv()
    for cp in sent:
        cp.wait_send()


SIBLING = (1,)
ICI_SAME_CORE = (2, 4, 6)
ALL_PEERS = tuple(range(1, N_DEV))


def _remote(src, dst, send_sem, recv_sem, dev):
    return pltpu.make_async_remote_copy(src_ref=src, dst_ref=dst, send_sem=send_sem, recv_sem=recv_sem,
                                        device_id=dev, device_id_type=MESH)


def _push_start(items, masks, send_sems, recv_sems):
    me, peers = _me_and_peers()
    for k, (src_for, dst_from) in enumerate(items):
        for m in masks:
            dev, pid = peers[m - 1]
            _remote(src_for(pid), dst_from(me), send_sems.at[k, m - 1], recv_sems.at[k, m - 1], dev).start()


def _push_wait_recv(items, masks, send_sems, recv_sems):
    me, peers = _me_and_peers()
    for k, (src_for, dst_from) in enumerate(items):
        for m in masks:
            dev, pid = peers[m - 1]
            _remote(src_for(pid), dst_from(pid), send_sems.at[k, m - 1], recv_sems.at[k, m - 1], dev).wait_recv()


def _push_wait_send(items, masks, send_sems, recv_sems):
    me, peers = _me_and_peers()
    for k, (src_for, dst_from) in enumerate(items):
        for m in masks:
            dev, pid = peers[m - 1]
            _remote(src_for(pid), dst_from(me), send_sems.at[k, m - 1], recv_sems.at[k, m - 1], dev).wait_send()


def _forward_start(items, send_sems, recv_sems):
    me, peers = _me_and_peers()
    sib = peers[0][0]
    for k, (blk_in, blk_out) in enumerate(items):
        for j, m in enumerate(ICI_SAME_CORE):
            pid = peers[m - 1][1]
            _remote(blk_in(pid), blk_out(pid), send_sems.at[k, j], recv_sems.at[k, j], sib).start()


def _forward_wait(items, send_sems, recv_sems):
    me, peers = _me_and_peers()
    sib = peers[0][0]
    for k, (blk_in, blk_out) in enumerate(items):
        for j, m in enumerate(ICI_SAME_CORE):
            got = peers[(m | 1) - 1][1]
            _remote(blk_in(got), blk_out(got), send_sems.at[k, j], recv_sems.at[k, j], sib).wait_recv()
    for k, (blk_in, blk_out) in enumerate(items):
        for j, m in enumerate(ICI_SAME_CORE):
            pid = peers[m - 1][1]
            _remote(blk_in(pid), blk_out(pid), send_sems.at[k, j], recv_sems.at[k, j], sib).wait_send()


def _mod_gather(c, c_ctx, w_ada, b_ada, w_in_t, w_out, w1, w2):
    B, D = c.shape
    ncol = w_ada.shape[1]
    rows = SUBLANES * N_DEV + SUBLANES

    def body(c_ref, cc_ref, w_ref, b_ref, win_ref, wout_ref, w1_ref, w2_ref,
             s_ref, m_ref, gin_ref, wout_b, w1_b, w2_b,
             win_b, msend, send1, recv1, send2, recv2, wsend, wrecv, fsend, frecv, lsem):
        me, _ = _me_and_peers()
        win_b[...] = win_ref[...].astype(BF16)
        block = _row_block(gin_ref, w_in_t.shape[0])
        gather = [(lambda p: win_b, block)]
        own = pltpu.make_async_copy(win_b, block(me), lsem.at[0])
        cv = c_ref[...]
        slot = jnp.concatenate([cv * _sigmoid(cv), jnp.zeros((SUBLANES - B, D), F32)], axis=0)
        my_rows = pl.ds(pl.multiple_of(me * SUBLANES, SUBLANES), SUBLANES)
        s_ref[my_rows, :] = slot
        ccv = cc_ref[...]
        s_ref[SUBLANES * N_DEV:, :] = jnp.concatenate(
            [ccv * _sigmoid(ccv), jnp.zeros((SUBLANES - 1, D), F32)], axis=0)

        def rows_of(p):
            return s_ref.at[pl.ds(pl.multiple_of(p * SUBLANES, SUBLANES), SUBLANES), :]

        _exchange(lambda p: rows_of(me), rows_of, send1, recv1)
        own.start()
        _push_start(gather, SIBLING + ICI_SAME_CORE, wsend, wrecv)
        wout_b[...] = wout_ref[...].astype(BF16)
        w1_b[...] = w1_ref[...].astype(BF16)
        w2_b[...] = w2_ref[...].astype(BF16)
        b_loc = b_ref[:, pl.ds(pl.multiple_of(me * ncol, ncol), ncol)]
        mods = _dot(s_ref[...], w_ref[...]) + b_loc
        for p in range(N_DEV):
            msend[p] = jnp.concatenate([mods[p * SUBLANES:(p + 1) * SUBLANES], mods[N_DEV * SUBLANES:]], axis=0)
        m_ref[me] = msend[me]
        columns = [(lambda p: msend.at[p], lambda p: m_ref.at[p])]
        _push_start(columns, ALL_PEERS, send2, recv2)
        _push_wait_recv(gather, ICI_SAME_CORE, wsend, wrecv)
        relay = [(block, block)]
        _forward_start(relay, fsend, frecv)
        _push_wait_recv(columns, ALL_PEERS, send2, recv2)
        _push_wait_recv(gather, SIBLING, wsend, wrecv)
        _forward_wait(relay, fsend, frecv)
        _push_wait_send(columns, ALL_PEERS, send2, recv2)
        _push_wait_send(gather, SIBLING + ICI_SAME_CORE, wsend, wrecv)
        own.wait()

    return pl.pallas_call(
        body, name="mod_gather",
        out_shape=(jax.ShapeDtypeStruct((rows, D), F32), jax.ShapeDtypeStruct((N_DEV, 2 * SUBLANES, ncol), F32),
                   jax.ShapeDtypeStruct((N_DEV * w_in_t.shape[0], D), BF16),
                   jax.ShapeDtypeStruct(w_out.shape, BF16), jax.ShapeDtypeStruct(w1.shape, BF16),
                   jax.ShapeDtypeStruct(w2.shape, BF16)),
        in_specs=[_vmem()] * 8, out_specs=(_vmem(), _vmem(), _any(), _vmem(), _vmem(), _vmem()),
        scratch_shapes=[pltpu.VMEM(w_in_t.shape, BF16), pltpu.VMEM((N_DEV, 2 * SUBLANES, ncol), F32)]
                       + [pltpu.SemaphoreType.DMA((N_DEV - 1,))] * 2
                       + [pltpu.SemaphoreType.DMA((1, N_DEV - 1))] * 4 + [pltpu.SemaphoreType.DMA((1, 3))] * 2
                       + [pltpu.SemaphoreType.DMA((1,))],
        compiler_params=pltpu.CompilerParams(vmem_limit_bytes=VMEM_LIMIT),
    )(c, c_ctx.reshape(1, D), w_ada, b_ada, w_in_t, w_out, w1, w2)


def _row_block(ref, rows):
    return lambda p: ref.at[pl.ds(pl.multiple_of(p * rows, 2 * SUBLANES), rows), :]


def _col_block(ref, cols):
    return lambda p: ref.at[:, pl.ds(pl.multiple_of(p * cols, LANES), cols)]


def _slot(ref):
    return lambda p: ref.at[p]


def _grid_call(body, *, name, grid, out_shape, in_specs, out_specs, scratch_shapes, args, after=None):
    n_in = len(args)

    def ordered_body(*refs):
        body(*refs[:n_in], *refs[n_in + 1:])

    return pl.pallas_call(
        body if after is None else ordered_body, name=name, grid=grid, out_shape=tuple(out_shape),
        in_specs=list(in_specs) + ([] if after is None else [_any()]), out_specs=tuple(out_specs),
        scratch_shapes=list(scratch_shapes), compiler_params=_params(*(("arbitrary",) * len(grid))),
    )(*args, *([] if after is None else [after]))


def _token_tiles(n_ctx, tm):
    nct = n_ctx // tm

    def ctx_spec(D):
        return pl.BlockSpec((None, tm, D), lambda b, t: (b, jnp.minimum(t, nct - 1), 0))

    def lat_spec(D):
        return pl.BlockSpec((None, tm, D), lambda b, t: (b, jnp.maximum(t - nct, 0), 0))

    return nct, ctx_spec, lat_spec


def _inproj_fwd(x, ctx, modl, g1, w_in_t, after):
    B, N, D = x.shape
    n_ctx = ctx.shape[1]
    T = n_ctx + N
    nw = w_in_t.shape[0]
    tm = _div_tile(n_ctx, 256, 16)
    nct, ctx_spec, lat_spec = _token_tiles(n_ctx, tm)

    def body(c_ref, x_ref, sh_ref, sc_ref, g_ref, w_ref, h_ref, p_ref):
        x = jnp.where(pl.program_id(1) < nct, c_ref[...], x_ref[...])
        r = lax.rsqrt(jnp.mean(x * x, axis=-1, keepdims=True) + NORM_EPS)
        h = ((x * r) * g_ref[...]) * (1.0 + sc_ref[...]) + sh_ref[...]
        hb = h.astype(BF16)
        h_ref[...] = hb
        p_ref[...] = _dot_nt(hb, w_ref[...]).astype(BF16)

    def mrow(b, t):
        return jnp.where(t < nct, B, b)

    return _grid_call(
        body, name="inproj_fwd", grid=(B, T // tm),
        out_shape=(jax.ShapeDtypeStruct((B, T, D), BF16), jax.ShapeDtypeStruct((B, T, nw), BF16)),
        in_specs=[ctx_spec(D), lat_spec(D),
                  pl.BlockSpec((None, None, 1, D), lambda b, t: (mrow(b, t), 0, 0, 0)),
                  pl.BlockSpec((None, None, 1, D), lambda b, t: (mrow(b, t), 1, 0, 0)),
                  pl.BlockSpec((1, D), lambda b, t: (0, 0)),
                  pl.BlockSpec((nw, D), lambda b, t: (0, 0))],
        out_specs=(pl.BlockSpec((None, tm, D), lambda b, t: (b, t, 0)),
                   pl.BlockSpec((None, tm, nw), lambda b, t: (b, t, 0))),
        scratch_shapes=[], args=(ctx, x, modl, modl, g1, w_in_t), after=after)


def _swap32(x):
    lane = lax.broadcasted_iota(jnp.int32, x.shape, 1)
    return jnp.where((lane % 64) < 32, pltpu.roll(x, 96, 1), pltpu.roll(x, 32, 1))


def _rope(x, cos, sin):
    return x * cos + _swap32(x) * sin


def _unrope(dy, cos, sin):
    return dy * cos + _swap32(dy * sin)


def _ret_weights(lgf, lgb, dist):
    return jnp.exp(jnp.where(dist >= 0.0, lgf * dist, -lgb * dist))


class _RetDecay:
    def __init__(self, lgf, lgb, rows):
        r = lax.broadcasted_iota(jnp.int32, (rows, RET_DIM), 0).astype(F32)
        self.head = r + 1.0
        self.tail = (rows - 1.0) - r
        self.q_f = jnp.exp(lgf * self.head)
        self.k_f = jnp.exp(lgf * self.tail)
        self.q_b = jnp.exp(lgb * self.tail)
        self.k_b = jnp.exp(lgb * self.head)


def _ret_states(kf32, vs, lgf, lgb, C, c, nt, hf, hb, hfa=None, hba=None):
    dec = _RetDecay(lgf, lgb, c)
    dec_c = _RetDecay(lgf, lgb, C)
    step_f = jnp.exp(jnp.zeros((RET_DIM, RET_DIM), F32) + lgf * c)
    step_b = jnp.exp(jnp.zeros((RET_DIM, RET_DIM), F32) + lgb * c)

    def upd(rows, kdec):
        return _dot_tn((kf32[rows, :] * kdec).astype(BF16), vs[rows, :])

    def lat(t):
        return slice(C + t * c, C + (t + 1) * c)

    state = upd(slice(0, C), dec_c.k_f)
    aged = jnp.zeros_like(state)
    for t in range(nt):
        hf[t] = state.astype(BF16)
        if hfa is not None:
            hfa[t] = aged
        if t < nt - 1:
            aged = step_f * (aged + c * state)
            state = step_f * state + upd(lat(t), dec.k_f)
    state = upd(slice(0, C), dec_c.k_b)
    aged = jnp.zeros_like(state)
    for t in range(nt - 1, -1, -1):
        hb[t] = state.astype(BF16)
        if hba is not None:
            hba[t] = aged
        if t > 0:
            aged = step_b * (aged + c * state)
            state = step_b * state + upd(lat(t), dec.k_b)
    return dec, dec_c, step_f, step_b


def _ret_fwd(proj, cos, sin, lg, gn, n_ctx):
    B, T, _ = proj.shape
    C = n_ctx
    N = T - C
    c = _div_tile(N, 256, 16)
    nt = N // c
    scale = RET_DIM ** -0.5

    def body(lg_ref, q_ref, k_ref, vs, g_ref, cos_ref, sin_ref, gn_ref, o_ref, lat_ref, qr_ref, kf32,
             qs, ks, hf, hb):
        h = pl.program_id(1)
        lgf = lg_ref[0, h]
        lgb = lg_ref[1, h]
        for rows in [slice(0, C)] + [slice(C + t * c, C + (t + 1) * c) for t in range(nt)]:
            cosb = cos_ref[rows, :]
            sinb = sin_ref[rows, :]
            qr = _rope(q_ref[rows, :].astype(F32), cosb, sinb) * scale
            qr_ref[rows, :] = qr
            qs[rows, :] = qr.astype(BF16)
            kr = _rope(k_ref[rows, :].astype(F32), cosb, sinb)
            kf32[rows, :] = kr
            ks[rows, :] = kr.astype(BF16)
        gnv = gn_ref[...]
        dec, _, _, _ = _ret_states(kf32, vs, lgf, lgb, C, c, nt, hf, hb)
        rc = (lax.broadcasted_iota(jnp.int32, (c, c), 0) - lax.broadcasted_iota(jnp.int32, (c, c), 1)).astype(F32)
        w_diag = _ret_weights(lgf, lgb, rc)
        for t in range(nt):
            rows = slice(C + t * c, C + (t + 1) * c)
            qt = qs[rows, :]
            s = _dot_nt(qt, ks[rows, :])
            o = (_dot((s * w_diag).astype(BF16), vs[rows, :])
                 + dec.q_f * _dot(qt, hf[t]) + dec.q_b * _dot(qt, hb[t]))
            o_ref[t * c:(t + 1) * c, :] = o
            mu = jnp.mean(o, axis=-1, keepdims=True)
            oc = o - mu
            var = jnp.mean(oc * oc, axis=-1, keepdims=True)
            yh = oc * lax.rsqrt(var + NORM_EPS)
            g = g_ref[rows, :].astype(F32)
            lat_ref[t * c:(t + 1) * c, :] = ((yh * gnv) * (g * _sigmoid(g))).astype(BF16)

    def col(seg):
        return pl.BlockSpec((None, T, RET_DIM), lambda b, h, seg=seg: (b, 0, seg * RET_HEADS + h))

    return _grid_call(
        body, name="ret_fwd", grid=(B, RET_HEADS),
        out_shape=(jax.ShapeDtypeStruct((B, N, RET_WIDTH), F32), jax.ShapeDtypeStruct((B, N, RET_WIDTH), BF16),
                   jax.ShapeDtypeStruct((B, T, RET_WIDTH), F32), jax.ShapeDtypeStruct((B, T, RET_WIDTH), F32)),
        in_specs=[pl.BlockSpec(memory_space=pltpu.SMEM), col(0), col(1), col(2), col(3),
                  pl.BlockSpec((T, RET_DIM), lambda b, h: (0, 0)), pl.BlockSpec((T, RET_DIM), lambda b, h: (0, 0)),
                  pl.BlockSpec((1, RET_DIM), lambda b, h: (0, h))],
        out_specs=(pl.BlockSpec((None, N, RET_DIM), lambda b, h: (b, 0, h)),
                   pl.BlockSpec((None, N, RET_DIM), lambda b, h: (b, 0, h)),
                   pl.BlockSpec((None, T, RET_DIM), lambda b, h: (b, 0, h)),
                   pl.BlockSpec((None, T, RET_DIM), lambda b, h: (b, 0, h))),
        scratch_shapes=[pltpu.VMEM((T, RET_DIM), BF16)] * 2 + [pltpu.VMEM((nt, RET_DIM, RET_DIM), BF16)] * 2,
        args=(lg, proj, proj, proj, proj, cos, sin, gn))


def _ret_bwd(proj, q_rot, k_rot, cos, sin, lg, gn, o, dlat, n_ctx, after):
    B, T, _ = proj.shape
    C = n_ctx
    N = T - C
    c = _div_tile(N, 256, 16)
    nt = N // c
    scale = RET_DIM ** -0.5

    def lat(t):
        return slice(C + t * c, C + (t + 1) * c)

    def body(lg_ref, qf32, kf32, vs, g_ref, cos_ref, sin_ref, gn_ref, o_ref, dl_ref,
             d_ref, dgn_ref, dlg_ref, qs, ks, dos, hf, hb, hfa, hba, gf_s, gb_s):
        h = pl.program_id(1)
        lgf = lg_ref[0, h]
        lgb = lg_ref[1, h]
        gnv = gn_ref[...]

        def fold(a):
            return jnp.sum(a.reshape(a.shape[0] // SUBLANES, SUBLANES, a.shape[1]), axis=0)

        for rows in [slice(0, C)] + [lat(t) for t in range(nt)]:
            qs[rows, :] = qf32[rows, :].astype(BF16)
            ks[rows, :] = kf32[rows, :].astype(BF16)

        dgn = jnp.zeros((1, RET_DIM), F32)
        for t in range(nt):
            lrows = slice(t * c, (t + 1) * c)
            ov = o_ref[lrows, :]
            mu = jnp.mean(ov, axis=-1, keepdims=True)
            oc = ov - mu
            var = jnp.mean(oc * oc, axis=-1, keepdims=True)
            rstd = lax.rsqrt(var + NORM_EPS)
            yh = oc * rstd
            g = g_ref[lat(t), :].astype(F32)
            sg = _sigmoid(g)
            dl = dl_ref[lrows, :]
            d_ref[3, lat(t), :] = (dl * (yh * gnv) * (sg * (1.0 + g * (1.0 - sg)))).astype(BF16)
            dls = dl * (g * sg)
            dgn = dgn + jnp.sum(dls * yh, axis=0, keepdims=True)
            dyh = dls * gnv
            do = rstd * (dyh - jnp.mean(dyh, axis=-1, keepdims=True)
                         - yh * jnp.mean(dyh * yh, axis=-1, keepdims=True))
            dos[lrows, :] = do.astype(BF16)
        dgn_ref[...] = jnp.concatenate([dgn, jnp.zeros((SUBLANES - 1, RET_DIM), F32)], axis=0)
        d_ref[3, 0:C, :] = jnp.zeros((C, RET_DIM), BF16)
        d_ref[0, 0:C, :] = jnp.zeros((C, RET_DIM), BF16)

        dec, dec_c, step_f, step_b = _ret_states(kf32, vs, lgf, lgb, C, c, nt, hf, hb, hfa, hba)

        def zmat(t, qdec):
            return _dot_tn((qf32[lat(t), :] * qdec).astype(BF16), dos[t * c:(t + 1) * c, :])

        acc3f = jnp.zeros((RET_DIM, RET_DIM), F32)
        acc3b = jnp.zeros((RET_DIM, RET_DIM), F32)
        state = jnp.zeros((RET_DIM, RET_DIM), F32)
        for t in range(nt - 1, -1, -1):
            gf_s[t] = state.astype(BF16)
            z = zmat(t, dec.q_f)
            acc3f = acc3f + hfa[t] * z
            state = step_f * state + z
        gctx_f = state.astype(BF16)
        state = jnp.zeros((RET_DIM, RET_DIM), F32)
        for t in range(nt):
            gb_s[t] = state.astype(BF16)
            z = zmat(t, dec.q_b)
            acc3b = acc3b + hba[t] * z
            state = step_b * state + z
        gctx_b = state.astype(BF16)

        rc = (lax.broadcasted_iota(jnp.int32, (c, c), 0) - lax.broadcasted_iota(jnp.int32, (c, c), 1)).astype(F32)
        w_diag = _ret_weights(lgf, lgb, rc)
        wg_f = jnp.where(rc >= 0.0, w_diag * rc, 0.0)
        wg_b = jnp.where(rc < 0.0, -w_diag * rc, 0.0)
        accf = jnp.zeros((SUBLANES, RET_DIM), F32)
        accb = jnp.zeros((SUBLANES, RET_DIM), F32)
        gdf = jnp.zeros((SUBLANES, c), F32)
        gdb = jnp.zeros((SUBLANES, c), F32)
        for t in range(nt):
            rows = lat(t)
            qt = qs[rows, :]
            kt = ks[rows, :]
            vt = vs[rows, :]
            dot = dos[t * c:(t + 1) * c, :]
            s = _dot_nt(qt, kt)
            dp = _dot_nt(dot, vt)
            dv = _dot_tn((s * w_diag).astype(BF16), dot)
            ds = (dp * w_diag).astype(BF16)
            dq = _dot(ds, kt)
            dk = _dot_tn(ds, qt)
            gs = dp * s
            gdf = gdf + fold(gs * wg_f)
            gdb = gdb + fold(gs * wg_b)
            qv = qf32[rows, :]
            kv = kf32[rows, :]
            dq_f = dec.q_f * _dot_nt(dot, hf[t])
            dq_b = dec.q_b * _dot_nt(dot, hb[t])
            dk_f = dec.k_f * _dot_nt(vt, gf_s[t])
            dk_b = dec.k_b * _dot_nt(vt, gb_s[t])
            accf = accf + fold(dec.head * dq_f * qv) + fold(dec.tail * dk_f * kv)
            accb = accb + fold(dec.tail * dq_b * qv) + fold(dec.head * dk_b * kv)
            dv = dv + dec.k_f * _dot(kt, gf_s[t]) + dec.k_b * _dot(kt, gb_s[t])
            cosb = cos_ref[rows, :]
            sinb = sin_ref[rows, :]
            d_ref[0, rows, :] = _unrope((dq + dq_f + dq_b) * scale, cosb, sinb).astype(BF16)
            d_ref[1, rows, :] = _unrope(dk + dk_f + dk_b, cosb, sinb).astype(BF16)
            d_ref[2, rows, :] = dv.astype(BF16)
        kc = ks[0:C, :]
        vc = vs[0:C, :]
        kcv = kf32[0:C, :]
        dkc_f = dec_c.k_f * _dot_nt(vc, gctx_f)
        dkc_b = dec_c.k_b * _dot_nt(vc, gctx_b)
        accf = accf + fold(dec_c.tail * dkc_f * kcv)
        accb = accb + fold(dec_c.head * dkc_b * kcv)
        d_ref[1, 0:C, :] = (dkc_f + dkc_b).astype(BF16)
        d_ref[2, 0:C, :] = (dec_c.k_f * _dot(kc, gctx_f) + dec_c.k_b * _dot(kc, gctx_b)).astype(BF16)
        gf = jnp.sum(gdf) + jnp.sum(accf) + jnp.sum(acc3f)
        gb = jnp.sum(gdb) + jnp.sum(accb) + jnp.sum(acc3b)
        row = lax.broadcasted_iota(jnp.int32, (SUBLANES, LANES), 0)
        dlg_ref[...] = jnp.where(row == 0, gf, jnp.where(row == 1, gb, 0.0))

    def col(seg):
        return pl.BlockSpec((None, T, RET_DIM), lambda b, h, seg=seg: (b, 0, seg * RET_HEADS + h))

    def head(rows):
        return pl.BlockSpec((None, rows, RET_DIM), lambda b, h: (b, 0, h))

    return _grid_call(
        body, name="ret_bwd", grid=(B, RET_HEADS),
        out_shape=(jax.ShapeDtypeStruct((B, 4, T, RET_WIDTH), BF16),
                   jax.ShapeDtypeStruct((B, SUBLANES, RET_WIDTH), F32),
                   jax.ShapeDtypeStruct((B, RET_HEADS, SUBLANES, LANES), F32)),
        in_specs=[pl.BlockSpec(memory_space=pltpu.SMEM), head(T), head(T), col(2), col(3),
                  pl.BlockSpec((T, RET_DIM), lambda b, h: (0, 0)), pl.BlockSpec((T, RET_DIM), lambda b, h: (0, 0)),
                  pl.BlockSpec((1, RET_DIM), lambda b, h: (0, h)), head(N), head(N)],
        out_specs=(pl.BlockSpec((None, 4, T, RET_DIM), lambda b, h: (b, 0, 0, h)),
                   pl.BlockSpec((None, SUBLANES, RET_DIM), lambda b, h: (b, 0, h)),
                   pl.BlockSpec((None, None, SUBLANES, LANES), lambda b, h: (b, h, 0, 0))),
        scratch_shapes=[pltpu.VMEM((T, RET_DIM), BF16)] * 2 + [pltpu.VMEM((N, RET_DIM), BF16)]
                       + [pltpu.VMEM((nt, RET_DIM, RET_DIM), BF16)] * 2 + [pltpu.VMEM((nt, RET_DIM, RET_DIM), F32)] * 2
                       + [pltpu.VMEM((nt, RET_DIM, RET_DIM), BF16)] * 2,
        args=(lg, q_rot, k_rot, proj, proj, cos, sin, gn, o, dlat), after=after)


def _na_geometry(rows):
    kh = min(NA_KH, rows)
    return kh, kh * GRID_W


def _pair_select():
    lane = lax.broadcasted_iota(jnp.int32, (2 * GRID_W, LANES), 1)
    row = lax.broadcasted_iota(jnp.int32, (2 * GRID_W, LANES), 0)
    return (lane >= NA_DIM) == (row >= GRID_W)


def _pair_bias(bias_ref, dr0, kh):
    return jnp.concatenate(
        [jnp.concatenate([bias_ref[e, pl.ds(dr0 + 2 * m, 1)].reshape(GRID_W, LANES) for m in range(kh // 2)], axis=1)
         for e in range(2)], axis=0)


def _na_softmax(s_loc, s_ctx):
    mx = jnp.maximum(jnp.max(s_loc, axis=-1, keepdims=True), jnp.max(s_ctx, axis=-1, keepdims=True))
    p_loc = jnp.exp(s_loc - mx)
    p_ctx = jnp.exp(s_ctx - mx)
    den = jnp.sum(p_loc, axis=-1, keepdims=True) + jnp.sum(p_ctx, axis=-1, keepdims=True)
    return p_loc, p_ctx, den


def _na_fwd(proj, bias2, n_ctx):
    assert proj.dtype == BF16
    B, T, _ = proj.shape
    C = n_ctx
    N = T - C
    R = N // GRID_W
    kh, nk = _na_geometry(R)
    scale = NA_DIM ** -0.5
    base = (4 * RET_WIDTH) // LANES

    def body(q_ref, kb16, vb16, bias_ref, out_ref, p_ref):
        kc = kb16[0:C, :]
        vc = vb16[0:C, :]
        lane = lax.broadcasted_iota(jnp.int32, (GRID_W, LANES), 1)
        sel2 = _pair_select()

        def group(gi, carry):
            pre = []
            for u in range(NA_GROUP):
                r = gi * NA_GROUP + u
                bs = jnp.clip(r - kh // 2, 0, R - kh)
                dr0 = bs - r + (NA_KH - 1)
                q = q_ref[pl.ds(pl.multiple_of(C + r * GRID_W, GRID_W), GRID_W), :].astype(F32) * scale
                q2 = jnp.where(sel2, jnp.concatenate([q, q], axis=0), 0.0).astype(BF16)
                band = pl.ds(pl.multiple_of(C + bs * GRID_W, GRID_W), nk)
                s_loc = _dot_nt(q2, kb16[band, :]) + _pair_bias(bias_ref, dr0, kh)
                s_ctx = _dot_nt(q2, kc)
                pre.append((r, band, s_loc, s_ctx))
            mid = [(r, band) + _na_softmax(s_loc, s_ctx) for r, band, s_loc, s_ctx in pre]
            for r, band, p_loc, p_ctx, den in mid:
                inv = 1.0 / den
                pb_loc = (p_loc * inv).astype(BF16)
                pb_ctx = (p_ctx * inv).astype(BF16)
                p_ref[r, :, 0:nk] = pb_loc
                p_ref[r, :, nk:] = pb_ctx
                o2 = _dot(pb_loc, vb16[band, :]) + _dot(pb_ctx, vc)
                out_ref[pl.ds(pl.multiple_of(r * GRID_W, GRID_W), GRID_W), :] = jnp.where(
                    lane < NA_DIM, o2[:GRID_W], o2[GRID_W:]).astype(BF16)
            return carry

        lax.fori_loop(0, R // NA_GROUP, group, 0)

    def col(seg):
        return pl.BlockSpec((None, T, LANES), lambda b, p, seg=seg: (b, 0, base + seg * NA_PAIRS + p))

    return _grid_call(
        body, name="na_fwd", grid=(B, NA_PAIRS),
        out_shape=(jax.ShapeDtypeStruct((B, N, NA_WIDTH), BF16),
                   jax.ShapeDtypeStruct((B, NA_PAIRS, R, 2 * GRID_W, nk + C), BF16)),
        in_specs=[col(0), col(1), col(2),
                  pl.BlockSpec((2, 2 * NA_KH - 2, GRID_W, LANES), lambda b, p: (p, 0, 0, 0))],
        out_specs=(pl.BlockSpec((None, N, LANES), lambda b, p: (b, 0, p)),
                   pl.BlockSpec((None, None, R, 2 * GRID_W, nk + C), lambda b, p: (b, p, 0, 0, 0))),
        scratch_shapes=[],
        args=(proj, proj, proj, bias2))


def _na_bwd(proj, probs, dlat, n_ctx):
    assert proj.dtype == BF16
    B, T, _ = proj.shape
    C = n_ctx
    N = T - C
    R = N // GRID_W
    kh, nk = _na_geometry(R)
    scale = NA_DIM ** -0.5
    base = (4 * RET_WIDTH) // LANES

    def class_sums(tiles):
        n = tiles.shape[0]
        acc = None
        for v in range(GRID_W // SUBLANES):
            part = tiles[:, v * SUBLANES:(v + 1) * SUBLANES, :].reshape(n * SUBLANES, LANES)
            part = pltpu.roll(part, (NA_KW - 1 - v * SUBLANES) % LANES, 1)
            acc = part if acc is None else acc + part
        row = lax.broadcasted_iota(jnp.int32, acc.shape, 0)
        for bit in (1, 2, 4):
            acc = jnp.where((row & bit) != 0, pltpu.roll(acc, LANES - bit, 1), acc)
        return jnp.sum(acc.reshape(n, SUBLANES, LANES), axis=1)

    def body(q_ref, kb16, vb16, p_ref, dl_ref, d_ref, rr_ref, dkv, db_ref):
        b = pl.program_id(1)
        kc = kb16[0:C, :]
        vc = vb16[0:C, :]
        lane = lax.broadcasted_iota(jnp.int32, (GRID_W, LANES), 1)
        dkv[...] = jnp.zeros(dkv.shape, F32)
        d_ref[0, 0:C, :] = jnp.zeros((C, LANES), BF16)

        @pl.when(b == 0)
        def _():
            db_ref[...] = jnp.zeros(db_ref.shape, F32)

        sel2 = _pair_select()

        def group(gi, carry):
            pre = []
            for u in range(NA_GROUP):
                r = gi * NA_GROUP + u
                bs = jnp.clip(r - kh // 2, 0, R - kh)
                dr0 = bs - r + (NA_KH - 1)
                q = q_ref[pl.ds(pl.multiple_of(C + r * GRID_W, GRID_W), GRID_W), :].astype(F32) * scale
                do = dl_ref[pl.ds(pl.multiple_of(r * GRID_W, GRID_W), GRID_W), :]
                q2 = jnp.where(sel2, jnp.concatenate([q, q], axis=0), 0.0).astype(BF16)
                do2 = jnp.where(sel2, jnp.concatenate([do, do], axis=0), 0.0).astype(BF16)
                band = pl.ds(pl.multiple_of(C + bs * GRID_W, GRID_W), nk)
                dp_loc = _dot_nt(do2, vb16[band, :])
                dp_ctx = _dot_nt(do2, vc)
                pre.append((r, dr0, band, q2, do2, dp_loc, dp_ctx))
            mid = []
            for r, dr0, band, q2, do2, dp_loc, dp_ctx in pre:
                pb_loc = p_ref[r, :, 0:nk]
                pb_ctx = p_ref[r, :, nk:]
                p_loc = pb_loc.astype(F32)
                p_ctx = pb_ctx.astype(F32)
                delta = (jnp.sum(p_loc * dp_loc, axis=-1, keepdims=True)
                         + jnp.sum(p_ctx * dp_ctx, axis=-1, keepdims=True))
                ds_loc = p_loc * (dp_loc - delta)
                ds_ctx = p_ctx * (dp_ctx - delta)
                mid.append((r, dr0, band, q2, do2, pb_loc, pb_ctx, ds_loc, ds_ctx))
            for r, dr0, band, q2, do2, pb_loc, pb_ctx, ds_loc, ds_ctx in mid:
                dsb_loc = ds_loc.astype(BF16)
                dsb_ctx = ds_ctx.astype(BF16)
                dq2 = _dot(dsb_loc, kb16[band, :]) + _dot(dsb_ctx, kc)
                d_ref[0, pl.ds(pl.multiple_of(C + r * GRID_W, GRID_W), GRID_W), :] = (jnp.where(
                    lane < NA_DIM, dq2[:GRID_W], dq2[GRID_W:]) * scale).astype(BF16)
                dkv[0, band, :] += _dot_tn(dsb_loc, q2)
                dkv[1, band, :] += _dot_tn(pb_loc, do2)
                dkv[0, 0:C, :] += _dot_tn(dsb_ctx, q2)
                dkv[1, 0:C, :] += _dot_tn(pb_ctx, do2)
                for e in range(2):
                    for m in range(kh // 2):
                        db_ref[e, pl.ds(dr0 + 2 * m, 1)] += ds_loc[e * GRID_W:(e + 1) * GRID_W,
                                                                   m * LANES:(m + 1) * LANES].reshape(1, GRID_W, LANES)
            return carry

        lax.fori_loop(0, R // NA_GROUP, group, 0)
        d_ref[1] = dkv[0].astype(BF16)
        d_ref[2] = dkv[1].astype(BF16)

        @pl.when(b == B - 1)
        def _():
            for e in range(2):
                rr_ref[e] = class_sums(db_ref[e])

    def col(seg):
        return pl.BlockSpec((None, T, LANES), lambda p, b, seg=seg: (b, 0, base + seg * NA_PAIRS + p))

    return _grid_call(
        body, name="na_bwd", grid=(NA_PAIRS, B),
        out_shape=(jax.ShapeDtypeStruct((B, 3, T, NA_WIDTH), BF16),
                   jax.ShapeDtypeStruct((NA_HEADS, 2 * NA_KH - 2, LANES), F32)),
        in_specs=[col(0), col(1), col(2),
                  pl.BlockSpec((None, None, R, 2 * GRID_W, nk + C), lambda p, b: (b, p, 0, 0, 0)),
                  pl.BlockSpec((None, N, LANES), lambda p, b: (b, 0, p))],
        out_specs=(pl.BlockSpec((None, 3, T, LANES), lambda p, b: (b, 0, 0, p)),
                   pl.BlockSpec((2, 2 * NA_KH - 2, LANES), lambda p, b: (p, 0, 0))),
        scratch_shapes=[pltpu.VMEM((2, T, LANES), F32), pltpu.VMEM((2, 2 * NA_KH - 2, GRID_W, LANES), F32)],
        args=(proj, proj, proj, probs, dlat))


def _dense_core(lat_ret, lat_na, x, tgt, modl, g_post_mix, g_pre_mlp, g_post_mlp, w_out, w1, w2):
    B, N, D = x.shape
    F = w1.shape[1]
    wout_rows, w1_cols, w2_rows = w_out.shape[0] // N_DEV, w1.shape[1] // N_DEV, w2.shape[0] // N_DEV
    mixw = w_out.shape[0]
    half = mixw // 2
    tm = _div_tile(N, 256, 16)
    nt = N // tm
    fc = _div_tile(F, 1024, LANES)

    def body(lr_ref, ln_ref, x_ref, t_ref, gt1_ref, sh2_ref, sc2_ref, gt2_ref, gpm_ref, gpre_ref, gpo_ref,
             wout_part, w1_part, w2_part,
             dy1_ref, dlr_ref, dln_ref, dmix_ref, h2_ref, a_ref, du_ref, dz_ref, red_ref, wout_hbm, w1_hbm, w2_hbm,
             wout_v, w1_v, w2_v, u_s, sems, fsend, frecv):
        @pl.when((pl.program_id(0) == 0) & (pl.program_id(1) == 0))
        def _():
            relay = [(_row_block(wout_part, wout_rows), _row_block(wout_hbm, wout_rows)),
                     (_col_block(w1_part, w1_cols), _col_block(w1_hbm, w1_cols)),
                     (_row_block(w2_part, w2_rows), _row_block(w2_hbm, w2_rows))]
            _forward_start(relay, fsend, frecv)
            _forward_wait(relay, fsend, frecv)
            cps = [pltpu.make_async_copy(wout_hbm, wout_v, sems.at[0]),
                   pltpu.make_async_copy(w1_hbm, w1_v, sems.at[1]),
                   pltpu.make_async_copy(w2_hbm, w2_v, sems.at[2])]
            for cp in cps:
                cp.start()
            for cp in cps:
                cp.wait()

        @pl.when(pl.program_id(1) == 0)
        def _():
            red_ref[...] = jnp.zeros(red_ref.shape, F32)

        gt1 = gt1_ref[...]
        sh2 = sh2_ref[...]
        sc2 = sc2_ref[...]
        gt2 = gt2_ref[...]
        gpm = gpm_ref[...]
        gpre = gpre_ref[...]
        gpo = gpo_ref[...]

        def rowmean(a):
            return jnp.mean(a, axis=-1, keepdims=True)

        def colsum(a):
            return jnp.sum(a, axis=0, keepdims=True)

        mix_gain = gt1 * gpm
        mlp_in_gain = gpre * (1.0 + sc2)
        mlp_out_gain = gt2 * gpo
        mix = _dot(lr_ref[...], wout_v[0:half, :]) + _dot(ln_ref[...], wout_v[half:, :])
        x = x_ref[...]
        rm = lax.rsqrt(rowmean(mix * mix) + NORM_EPS)
        mh = mix * rm
        y1 = x + mh * mix_gain
        r1 = lax.rsqrt(rowmean(y1 * y1) + NORM_EPS)
        xh = y1 * r1
        h2b = (xh * mlp_in_gain + sh2).astype(BF16)
        h2_ref[...] = h2b
        z = jnp.zeros((tm, D), F32)
        for c0 in range(0, F, fc):
            u = _dot(h2b, w1_v[:, c0:c0 + fc])
            u_s[:, c0:c0 + fc] = u
            ru = jnp.maximum(u, 0.0)
            ab = (ru * ru).astype(BF16)
            a_ref[:, c0:c0 + fc] = ab
            z = z + _dot(ab, w2_v[c0:c0 + fc, :])
        r2 = lax.rsqrt(rowmean(z * z) + NORM_EPS)
        zh = z * r2
        y2 = y1 + zh * mlp_out_gain
        err = y2 - t_ref[...]
        loss = 0.5 * jnp.sum(rowmean(err * err))
        dy2 = err * (1.0 / D)
        s_out = colsum(dy2 * zh)
        red_ref[2:3, :] += s_out * gpo
        red_ref[6:7, :] += s_out * gt2
        dzh = dy2 * mlp_out_gain
        dz = r2 * (dzh - zh * rowmean(dzh * zh))
        dzb = dz.astype(BF16)
        dz_ref[...] = dzb
        dh2 = jnp.zeros((tm, D), F32)
        for c0 in range(0, F, fc):
            da = _dot_nt(dzb, w2_v[c0:c0 + fc, :])
            dub = (da * (2.0 * jnp.maximum(u_s[:, c0:c0 + fc], 0.0))).astype(BF16)
            du_ref[:, c0:c0 + fc] = dub
            dh2 = dh2 + _dot_nt(dub, w1_v[:, c0:c0 + fc])
        s_in = colsum(dh2 * xh)
        red_ref[3:4, :] += s_in * gpre
        red_ref[4:5, :] += colsum(dh2)
        red_ref[5:6, :] += s_in * (1.0 + sc2)
        dxh = dh2 * mlp_in_gain
        dy1 = dy2 + r1 * (dxh - xh * rowmean(dxh * xh))
        dy1_ref[...] = dy1
        s_mix = colsum(dy1 * mh)
        red_ref[0:1, :] += s_mix * gpm
        red_ref[1:2, :] += s_mix * gt1
        dmh = dy1 * mix_gain
        dmix = (rm *(dmh - mh * rowmean(dmh * mh))).astype(BF16)
        dmix_ref[...] = dmix
        dlr_ref[...] = _dot_nt(dmix, wout_v[0:half, :])
        dln_ref[...] = _dot_nt(dmix, wout_v[half:, :])
        red_ref[7:8, :] += jnp.zeros((1, D), F32) + loss

    def tok(w):
        return pl.BlockSpec((None, tm, w), lambda b, t: (b, t, 0))

    def mod(k):
        return pl.BlockSpec((None, None, 1, D), lambda b, t, k=k: (b, k, 0, 0))

    def vec():
        return pl.BlockSpec((1, D), lambda b, t: (0, 0))

    return pl.pallas_call(
        body, name="dense_core", grid=(B, nt),
        out_shape=(jax.ShapeDtypeStruct((B, N, D), F32), jax.ShapeDtypeStruct((B, N, half), F32),
                   jax.ShapeDtypeStruct((B, N, half), F32), jax.ShapeDtypeStruct((B, N, D), BF16),
                   jax.ShapeDtypeStruct((B, N, D), BF16), jax.ShapeDtypeStruct((B, N, F), BF16),
                   jax.ShapeDtypeStruct((B, N, F), BF16), jax.ShapeDtypeStruct((B, N, D), BF16),
                   jax.ShapeDtypeStruct((B, SUBLANES, D), F32),
                   jax.ShapeDtypeStruct(w_out.shape, w_out.dtype), jax.ShapeDtypeStruct(w1.shape, w1.dtype),
                   jax.ShapeDtypeStruct(w2.shape, w2.dtype)),
        in_specs=[tok(half), tok(half), tok(D), tok(D), mod(2), mod(3), mod(4), mod(5), vec(), vec(), vec(),
                  _any(), _any(), _any()],
        out_specs=(tok(D), tok(half), tok(half), tok(D), tok(D), tok(F), tok(F), tok(D),
                   pl.BlockSpec((None, SUBLANES, D), lambda b, t: (b, 0, 0)), _any(), _any(), _any()),
        scratch_shapes=[pltpu.VMEM((mixw, D), BF16), pltpu.VMEM((D, F), BF16), pltpu.VMEM((F, D), BF16),
                        pltpu.VMEM((tm, F), F32), pltpu.SemaphoreType.DMA((3,)),
                        pltpu.SemaphoreType.DMA((3, 3)), pltpu.SemaphoreType.DMA((3, 3))],
        input_output_aliases={11: 9, 12: 10, 13: 11},
        compiler_params=_params("arbitrary", "arbitrary"),
    )(lat_ret, lat_na, x, tgt, modl, modl, modl, modl, g_post_mix, g_pre_mlp, g_post_mlp, w_out, w1, w2)[:9]


def _inproj_bwd(dret, dna, x, ctx, dy1, modl, g1, w_in_t, after):
    B, N, D = x.shape
    n_ctx = ctx.shape[1]
    T = n_ctx + N
    tm = _div_tile(n_ctx, 256, 16)
    nct, ctx_spec, lat_spec = _token_tiles(n_ctx, tm)
    nt = T // tm
    nseg_r = dret.shape[1]
    nseg_n = dna.shape[1]
    nw = w_in_t.shape[0]

    def body(*refs):
        seg_refs = refs[:nseg_r + nseg_n]
        c_ref, x_ref, dy1_ref, sc_ref, g_ref, w_ref, dx_ref, red_ref = refs[nseg_r + nseg_n:]
        t = pl.program_id(1)
        dh = jnp.zeros((tm, D), F32)
        for s, ref in enumerate(seg_refs):
            dh = dh + _dot(ref[...], w_ref[s * SEG:(s + 1) * SEG, :])
        x = jnp.where(t < nct, c_ref[...], x_ref[...])
        g = g_ref[...]
        r = lax.rsqrt(jnp.mean(x * x, axis=-1, keepdims=True) + NORM_EPS)
        xh = x * r
        gain = 1.0 + sc_ref[...]
        s_in = jnp.sum(dh * xh, axis=0, keepdims=True)
        red_ref[0:1, :] = jnp.sum(dh, axis=0, keepdims=True)
        red_ref[1:2, :] = s_in * g
        red_ref[2:3, :] = s_in * gain
        red_ref[3:, :] = jnp.zeros((SUBLANES - 3, D), F32)
        dxh = dh * (g * gain)
        dx = r * (dxh - xh * jnp.mean(dxh * xh, axis=-1, keepdims=True))
        dx_ref[...] = dx + jnp.where(t >= nct, dy1_ref[...], 0.0)

    def mrow(b, t):
        return jnp.where(t < nct, B, b)

    def seg(s):
        return pl.BlockSpec((None, None, tm, SEG), lambda b, t, s=s: (b, s, t, 0))

    return _grid_call(
        body, name="inproj_bwd", grid=(B, nt),
        out_shape=(jax.ShapeDtypeStruct((B, N, D), F32), jax.ShapeDtypeStruct((B, nt, SUBLANES, D), F32)),
        in_specs=[seg(s) for s in range(nseg_r)] + [seg(s) for s in range(nseg_n)]
                 + [ctx_spec(D), lat_spec(D), lat_spec(D),
                    pl.BlockSpec((None, None, 1, D), lambda b, t: (mrow(b, t), 1, 0, 0)),
                    pl.BlockSpec((1, D), lambda b, t: (0, 0)),
                    pl.BlockSpec((nw, D), lambda b, t: (0, 0))],
        out_specs=(lat_spec(D), pl.BlockSpec((None, None, SUBLANES, D), lambda b, t: (b, t, 0, 0))),
        scratch_shapes=[], args=(*([dret] * nseg_r), *([dna] * nseg_n), ctx, x, dy1, modl, g1, w_in_t), after=after)


def _tn_matmul(lhs, rhs, name, rows_before=0, rows_after=0, into=None):
    B, S, T, W = lhs.shape
    nn = rhs.shape[-1]
    tk = _div_tile(T, 2304, LANES)
    bm = _div_tile(W, 1024, LANES)
    bn = _div_tile(nn, 1024, LANES)
    nkt = T // tk
    nk = B * nkt

    def body(l_ref, r_ref, *rest):
        o_ref, acc = rest[-2:]
        k, s = pl.program_id(2), pl.program_id(3)

        def part():
            return _dot_tn(l_ref[...].astype(BF16), r_ref[...].astype(BF16))

        @pl.when(k == 0)
        def _():
            acc[s] = part()

        @pl.when(k > 0)
        def _():
            acc[s] += part()

        @pl.when(k == nk - 1)
        def _():
            o_ref[...] = acc[s].astype(BF16)

    nwb = W // bm
    first = rows_before // bm

    def out_block(i, j, k, s):
        return first + jnp.where(k == nk - 1, s, 0) * nwb + i, j
    return pl.pallas_call(
        functools.partial(body), name=name, grid=(nwb, nn // bn, nk, S),
        out_shape=jax.ShapeDtypeStruct((rows_before + S * W + rows_after, nn), BF16),
        in_specs=[pl.BlockSpec((None, None, tk, bm), lambda i, j, k, s: (k // nkt, s, k % nkt, i)),
                  pl.BlockSpec((None, tk, bn), lambda i, j, k, s: (k // nkt, k % nkt, j))]
                 + ([] if into is None else [_any()]),
        out_specs=pl.BlockSpec((bm, bn), out_block),
        scratch_shapes=[pltpu.VMEM((S, bm, bn), F32)],
        input_output_aliases={} if into is None else {2: 0},
        compiler_params=_params("parallel", "parallel", "arbitrary", "arbitrary"),
    )(lhs, rhs, *([] if into is None else [into]))


class _SplitScatter:
    def __init__(self, gs, block_ofs, land_shapes, name, kind="scatter", masks=ALL_PEERS):
        self.n = n = len(gs)
        self.block_ofs, self.kind, self.masks = block_ofs, kind, masks
        if kind == "scatter":
            land_shapes = [(N_DEV,) + tuple(bs) for bs in land_shapes]
        hbm = pl.BlockSpec(memory_space=pltpu.HBM)
        sem = pl.BlockSpec(memory_space=pltpu.SEMAPHORE)

        def body(*refs):
            g_refs, land_refs = refs[:n], refs[n:2 * n]
            send_sems, recv_sems, own_sems = refs[2 * n:2 * n + 3]
            token = refs[-1]
            for own, pushes in self._copies(g_refs, land_refs, send_sems, recv_sems, own_sems, landing="sender"):
                own.start()
                for cp in pushes:
                    cp.start()
            token[...] = jnp.zeros_like(token)

        outs = pl.pallas_call(
            body, name=name,
            out_shape=(pltpu.SemaphoreType.DMA((n * (N_DEV - 1),)), pltpu.SemaphoreType.DMA((n * (N_DEV - 1),)),
                       pltpu.SemaphoreType.DMA((n,)))
                      + tuple(pltpu.HBM(g.shape, g.dtype) for g in gs)
                      + tuple(pltpu.HBM(s, g.dtype) for s, g in zip(land_shapes, gs))
                      + (jax.ShapeDtypeStruct((SUBLANES, LANES), F32),),
            in_specs=(hbm,) * (2 * n), out_specs=(sem,) * 3 + (hbm,) * (2 * n) + (_vmem(),),
            input_output_aliases={k: 3 + k for k in range(2 * n)},
            compiler_params=pltpu.CompilerParams(has_side_effects=pltpu.SideEffectType.DATAFLOW_SIDE_EFFECTING),
        )(*[pltpu.with_memory_space_constraint(g, pltpu.HBM) for g in gs],
          *[pltpu.with_memory_space_constraint(lax.empty(s, g.dtype), pltpu.HBM) for s, g in zip(land_shapes, gs)])
        self.sems, self.thru, self.token = outs[:3], outs[3:3 + 2 * n], outs[-1]

    def _copies(self, g_refs, land_refs, send_sems, recv_sems, own_sems, landing):
        me, peers = _me_and_peers()
        out = []
        for k in range(self.n):
            if self.kind == "scatter":
                src, dst = self.block_ofs[k](g_refs[k]), _slot(land_refs[k])
            else:
                src, dst = (lambda p, k=k: g_refs[k]), self.block_ofs[k](land_refs[k])
            own = pltpu.make_async_copy(src(me), dst(me), own_sems.at[k]) if landing == "sender" else None
            pushes = []
            for m in self.masks:
                dev, pid = peers[m - 1]
                i = k * (N_DEV - 1) + m - 1
                pushes.append(_remote(src(pid), dst(me if landing == "sender" else pid),
                                      send_sems.at[i], recv_sems.at[i], dev))
            out.append((own, pushes))
        return out


def _scatter_wait(scatters, after, name):
    hbm = pl.BlockSpec(memory_space=pltpu.HBM)
    sem = pl.BlockSpec(memory_space=pltpu.SEMAPHORE)
    n_arr = [2 * sc.n for sc in scatters]
    total = sum(n_arr)

    def body(*refs):
        arrs, sems = refs[:total], refs[total:total + 3 * len(scatters)]
        a0 = 0
        for j, sc in enumerate(scatters):
            g_refs, land_refs = arrs[a0:a0 + sc.n], arrs[a0 + sc.n:a0 + 2 * sc.n]
            a0 += 2 * sc.n
            send_sems, recv_sems, own_sems = sems[3 * j:3 * j + 3]
            for (own, sent), (_, got) in zip(sc._copies(g_refs, land_refs, send_sems, recv_sems, own_sems, "sender"),
                                             sc._copies(g_refs, land_refs, send_sems, recv_sems, own_sems, "receiver")):
                own.wait()
                for cp in sent:
                    cp.wait_send()
                for cp in got:
                    cp.wait_recv()

    operands = [a for sc in scatters for a in sc.thru]
    outs = pl.pallas_call(
        body, name=name,
        out_shape=tuple(pltpu.HBM(a.shape, a.dtype) for a in operands),
        in_specs=(hbm,) * total + (sem,) * (3 * len(scatters)) + (pl.BlockSpec(memory_space=pl.ANY),),
        out_specs=(hbm,) * total, input_output_aliases={k: k for k in range(total)},
        compiler_params=pltpu.CompilerParams(has_side_effects=pltpu.SideEffectType.DATAFLOW_SIDE_EFFECTING),
    )(*operands, *[s for sc in scatters for s in sc.sems], after)
    lands, a0 = [], 0
    for sc in scatters:
        lands.extend(outs[a0 + sc.n:a0 + 2 * sc.n])
        a0 += 2 * sc.n
    return lands


def _small_ar(mbuf, silu_all, w_ada, c_ctx, n_mod_rows, n_vec_rows):
    D = silu_all.shape[1]
    ncol = w_ada.shape[1]
    nm = mbuf.shape[2]
    srows = silu_all.shape[0]

    def body(mbuf, s_ref, w_ref, cc_ref, tot_ref, gb_ref, gw_ref, gc_ref, tbuf, dmx, cmrow, send3, recv3):
        me, _ = _me_and_peers()
        msum = mbuf[0]
        for k in range(1, N_DEV):
            msum = msum + mbuf[k]
        tot_ref[...] = msum[n_mod_rows:n_mod_rows + n_vec_rows]
        gb_ref[...] = jnp.sum(msum[0:n_mod_rows], axis=0, keepdims=True)
        loc = pl.ds(pl.multiple_of(me * ncol, ncol), ncol)
        for k in range(N_DEV):
            dmx[k * SUBLANES:(k + 1) * SUBLANES, :] = mbuf[k, :, loc]
        cmrow[...] = msum
        cm_loc = cmrow[n_mod_rows - 1:n_mod_rows, loc]
        dmx[N_DEV * SUBLANES:, :] = jnp.concatenate([cm_loc, jnp.zeros((SUBLANES - 1, ncol), F32)], axis=0)
        gw_ref[...] = _dot_tn(s_ref[...], dmx[...])
        tbuf[me] = _dot_nt(dmx[N_DEV * SUBLANES:, :], w_ref[...])
        _exchange(lambda p: tbuf.at[me], lambda p: tbuf.at[p], send3, recv3)
        tsum = tbuf[0]
        for k in range(1, N_DEV):
            tsum = tsum + tbuf[k]
        cc = cc_ref[...]
        sg = _sigmoid(cc)
        gc_ref[...] = tsum[0:1, :] * (sg * (1.0 + cc * (1.0 - sg)))

    return pl.pallas_call(
        body, name="small_ar",
        out_shape=(jax.ShapeDtypeStruct((n_vec_rows, nm), F32), jax.ShapeDtypeStruct((1, nm), F32),
                   jax.ShapeDtypeStruct((D, ncol), F32), jax.ShapeDtypeStruct((1, D), F32)),
        in_specs=[_vmem()] * 4, out_specs=(_vmem(),) * 4,
        scratch_shapes=[pltpu.VMEM((N_DEV, SUBLANES, D), F32), pltpu.VMEM((srows, ncol), F32),
                        pltpu.VMEM((SUBLANES, nm), F32)] + [pltpu.SemaphoreType.DMA((N_DEV - 1,))] * 2,
        compiler_params=pltpu.CompilerParams(vmem_limit_bytes=VMEM_LIMIT),
    )(mbuf, silu_all, w_ada, c_ctx.reshape(1, D))


def _adam_update(w, g, m, v):
    mn = ADAM_B1 * m + (1.0 - ADAM_B1) * g
    vn = ADAM_B2 * v + (1.0 - ADAM_B2) * (g * g)
    m_hat = mn / (1.0 - ADAM_B1 ** ADAM_STEP)
    v_hat = vn / (1.0 - ADAM_B2 ** ADAM_STEP)
    return -ADAM_LR * (m_hat / (jnp.sqrt(v_hat) + ADAM_EPS) + ADAM_WD * w), mn, vn


def _adamw(w, g, m, v, name):
    rows, cols = w.shape
    tr = _div_tile(rows, 512, SUBLANES)

    def body(w_ref, g_ref, m_ref, v_ref, d_ref, nm_ref, nv_ref):
        d_ref[...], nm_ref[...], nv_ref[...] = _adam_update(w_ref[...], g_ref[...], m_ref[...], v_ref[...])

    spec = pl.BlockSpec((tr, cols), lambda i: (i, 0))
    return pl.pallas_call(
        functools.partial(body), name=name, grid=(rows // tr,),
        out_shape=(jax.ShapeDtypeStruct((rows, cols), F32),) * 3,
        in_specs=[spec] * 4, out_specs=(spec,) * 3,
        compiler_params=_params("parallel"),
    )(w, g, m, v)


def _adamw_small(items, name):
    n = len(items)

    def body(*refs):
        ins, outs = refs[:4 * n], refs[4 * n:]
        for i in range(n):
            w_ref, g_ref, m_ref, v_ref = ins[4 * i:4 * i + 4]
            outs[3 * i][...], outs[3 * i + 1][...], outs[3 * i + 2][...] = _adam_update(
                w_ref[...], g_ref[...], m_ref[...], v_ref[...])

    outs = pl.pallas_call(
        body, name=name,
        out_shape=tuple(jax.ShapeDtypeStruct(it[0].shape, F32) for it in items for _ in range(3)),
        in_specs=[_vmem()] * (4 * n), out_specs=(_vmem(),) * (3 * n),
        compiler_params=pltpu.CompilerParams(vmem_limit_bytes=VMEM_LIMIT),
    )(*[a for it in items for a in it])
    return [tuple(outs[3 * i:3 * i + 3]) for i in range(n)]


def _sum_adamw(buf, w, m, v, name):
    _, rows, cols = buf.shape
    tr = _div_tile(rows, 256, 2 * SUBLANES)

    def body(b_ref, w_ref, m_ref, v_ref, g_ref, d_ref, nm_ref, nv_ref):
        g = b_ref[0].astype(F32)
        for k in range(1, N_DEV):
            g = g + b_ref[k].astype(F32)
        g_ref[...] = g
        d_ref[...], nm_ref[...], nv_ref[...] = _adam_update(w_ref[...], g, m_ref[...], v_ref[...])

    spec = pl.BlockSpec((tr, cols), lambda i: (i, 0))
    return pl.pallas_call(
        functools.partial(body), name=name, grid=(rows // tr,),
        out_shape=(jax.ShapeDtypeStruct((rows, cols), F32),) * 4,
        in_specs=[pl.BlockSpec((N_DEV, tr, cols), lambda i: (0, i, 0))] + [spec] * 3, out_specs=(spec,) * 4,
        compiler_params=_params("parallel"),
    )(buf, w, m, v)


def _rope_tables(n_ctx, n):
    n_freq = RET_DIM // 4
    inv = np.float32(ROPE_BASE) ** (-np.arange(n_freq, dtype=np.float32) / np.float32(n_freq))
    tok = np.arange(n)
    pos_r = (tok // GRID_W).astype(np.float32)
    pos_c = (tok % GRID_W).astype(np.float32)
    ang_r = (pos_r[:, None] * inv[None, :]).astype(np.float32)
    ang_c = (pos_c[:, None] * inv[None, :]).astype(np.float32)
    cos = np.concatenate([np.cos(ang_r), np.cos(ang_r), np.cos(ang_c), np.cos(ang_c)], axis=-1)
    sin = np.concatenate([-np.sin(ang_r), np.sin(ang_r), -np.sin(ang_c), np.sin(ang_c)], axis=-1)
    cos = np.concatenate([np.ones((n_ctx, RET_DIM), np.float32), cos], axis=0)
    sin = np.concatenate([np.zeros((n_ctx, RET_DIM), np.float32), sin], axis=0)
    return jnp.asarray(cos, F32), jnp.asarray(sin, F32)


def _na_tables():
    q = np.arange(GRID_W)[:, None]
    k = np.arange(GRID_W)[None, :]
    start = np.clip(q - NA_KW // 2, 0, GRID_W - NA_KW)
    valid = (k >= start) & (k < start + NA_KW)
    dc = np.clip(k - q + (NA_KW - 1), 0, 2 * NA_KW - 2)
    ncls = 2 * NA_KW - 1
    onehot = (dc[None] == np.arange(ncls)[:, None, None]) & valid[None]
    return onehot.astype(np.float32), valid


def _paired_bias(rpb, onehot, valid):
    ncls = onehot.shape[0]
    pair = np.zeros((2 * ncls, GRID_W, LANES), np.float32)
    pair[:ncls, :, :GRID_W] = onehot
    pair[ncls:, :, GRID_W:] = onehot
    rows = jnp.concatenate([rpb[:, :-1], rpb[:, 1:]], axis=-1)
    t = jnp.einsum("hdc,cqk->hdqk", rows, jnp.asarray(pair), precision=lax.Precision.HIGHEST)
    return jnp.where(jnp.asarray(np.tile(valid, (1, 2)))[None, None], t, NEG_INF)


def kernel(x, c, ctx, c_ctx, w_ada, b_ada, g_pre_mix, g_post_mix, g_pre_mlp, g_post_mlp, w_in, ret_decay, ret_gn, na_rpb, w_out, w_mlp1, w_mlp2, loss_target, m_c_ctx, m_w_ada, m_b_ada, m_g_pre_mix, m_g_post_mix, m_g_pre_mlp, m_g_post_mlp, m_w_in, m_ret_decay, m_ret_gn, m_na_rpb, m_w_out, m_w_mlp1, m_w_mlp2, v_c_ctx, v_w_ada, v_b_ada, v_g_pre_mix, v_g_post_mix, v_g_pre_mlp, v_g_post_mlp, v_w_in, v_ret_decay, v_ret_gn, v_na_rpb, v_w_out, v_w_mlp1, v_w_mlp2):
    B, N, D = x.shape
    C = ctx.shape[1]
    T = C + N

    silu_all, mods_g, win_b, wout_l, w1_l, w2_l = _mod_gather(c, c_ctx, w_ada[0], b_ada, w_in[0].T, w_out[0],
                                                             w_mlp1[0], w_mlp2[0])
    mods_mine = mods_g.transpose(1, 0, 2).reshape(mods_g.shape[1], N_MOD * D)
    modl = jnp.concatenate([mods_mine[:B], mods_mine[SUBLANES:SUBLANES + 1]], axis=0)
    modl = modl.reshape(B + 1, N_MOD, 1, D)
    rin = w_in.shape[2]
    rout, c1, r2 = wout_l.shape[0], w1_l.shape[1], w2_l.shape[0]

    def rows_of(n):
        return lambda ref: _row_block(ref, n)

    def cols_of(n):
        return lambda ref: _col_block(ref, n)

    cos, sin = _rope_tables(C, N)
    onehot, valid = _na_tables()
    bias2 = _paired_bias(na_rpb[0], onehot, valid)
    lg = jax.nn.log_sigmoid(ret_decay[0].astype(F32))

    ag = _SplitScatter([wout_l, w1_l, w2_l], [rows_of(rout), cols_of(c1), rows_of(r2)],
                       [(N_DEV * rout, D), (D, N_DEV * c1), (N_DEV * r2, D)], "ag_mlp_start",
                       kind="gather", masks=SIBLING + ICI_SAME_CORE)
    h_all, proj = _inproj_fwd(x, ctx, modl, g_pre_mix, win_b, after=ag.token)
    o_ret, lat_ret, q_rot, k_rot = _ret_fwd(proj, cos, sin, lg, ret_gn, C)
    lat_na, na_probs = _na_fwd(proj, bias2, C)
    wout_part, w1_part, w2_part = _scatter_wait([ag], lat_na, "ag_mlp_wait")

    (dy1, dlat_ret, dlat_na, dmix, h2, act, du, dz, red_d) = _dense_core(
        lat_ret, lat_na, x, loss_target, modl, g_post_mix, g_pre_mlp, g_post_mlp, wout_part, w1_part, w2_part)

    gw_out_p = _tn_matmul(lat_ret[:, None], dmix, "gw_out_ret", rows_after=lat_na.shape[-1])
    gw_out_p = _tn_matmul(lat_na[:, None], dmix, "gw_out_na", rows_before=lat_ret.shape[-1], into=gw_out_p)
    gw1_p = _tn_matmul(h2[:, None], du, "gw_mlp1")
    gw2_p = _tn_matmul(act[:, None], dz, "gw_mlp2")
    rs_mlp = _SplitScatter([gw_out_p, gw1_p, gw2_p], [rows_of(rout), cols_of(c1), rows_of(r2)],
                           [(rout, D), (D, c1), (r2, D)], "rs_mlp_start")

    dret, dgn_p, dlg_p = _ret_bwd(proj, q_rot, k_rot, cos, sin, lg, ret_gn, o_ret, dlat_ret, C, after=rs_mlp.token)
    dna, rr = _na_bwd(proj, na_probs, dlat_na, C)
    ret_cols, na_cols = dret.shape[1] * dret.shape[3], dna.shape[1] * dna.shape[3]
    gwin_t_p = _tn_matmul(dret, h_all, "gw_in_ret", rows_after=na_cols)
    gwin_t_p = _tn_matmul(dna, h_all, "gw_in_na", rows_before=ret_cols, into=gwin_t_p)
    rs_in = _SplitScatter([gwin_t_p], [rows_of(rin)], [(rin, D)], "rs_w_in_start")
    grad_x, red_i = _inproj_bwd(dret, dna, x, ctx, dy1, modl, g_pre_mix, win_b, after=rs_in.token)

    rd = red_d
    nct = red_i.shape[1] * C // T
    ri_ctx = red_i[:, :nct].sum(axis=(0, 1))
    ri_lat = red_i[:, nct:].sum(axis=1)
    d_mods = jnp.concatenate([ri_lat[:, 0], ri_lat[:, 1], rd[:, 0], rd[:, 4], rd[:, 3], rd[:, 2]], axis=-1)
    d_cmods = jnp.concatenate([ri_ctx[0], ri_ctx[1], jnp.zeros(((N_MOD - 2) * D,), F32)])[None]
    dg_pre_mix = ri_lat[:, 2].sum(axis=0) + ri_ctx[2]
    dg_post_mix = rd[:, 1].sum(axis=0)
    dg_pre_mlp = rd[:, 5].sum(axis=0)
    dg_post_mlp = rd[:, 6].sum(axis=0)
    loss_p = rd[:, 7, 0].sum()
    d_gn = dgn_p[:, 0].sum(axis=0)
    d_lg = dlg_p[:, :, :2, 0].sum(axis=0).T
    d_decay = d_lg * jax.nn.sigmoid(-ret_decay[0].astype(F32))
    ncls = 2 * NA_KW - 1
    d_rpb = (jnp.pad(rr[:, :, :ncls], ((0, 0), (0, 1), (0, 0)))
             + jnp.pad(rr[:, :, GRID_W:GRID_W + ncls], ((0, 0), (1, 0), (0, 0))))
    d_rpb32 = jnp.pad(d_rpb, ((0, 0), (0, 0), (0, 32 - ncls)))
    pieces = [dg_pre_mix, dg_post_mix, dg_pre_mlp, dg_post_mlp, d_gn, d_rpb32.reshape(-1),
              jnp.pad(d_decay.reshape(-1), (0, LANES - d_decay.size)), jnp.full((LANES,), loss_p, F32)]
    vec = jnp.concatenate(pieces)
    nm = N_MOD * D
    n_vec_rows = -(-vec.shape[0] // nm)
    assert B + 1 + n_vec_rows <= SUBLANES
    vec = jnp.pad(vec, (0, n_vec_rows * nm - vec.shape[0])).reshape(n_vec_rows, nm)
    dm_slot = jnp.concatenate([d_mods, d_cmods, vec, jnp.zeros((SUBLANES - B - 1 - n_vec_rows, nm), F32)], axis=0)
    def whole(ref):
        return lambda p: ref

    small = _SplitScatter([dm_slot], [whole], [dm_slot.shape], "small_start")
    land_out, land_1, land_2 = _scatter_wait([rs_mlp], small.token, "rs_mlp_wait")
    fused = {"w_out": _sum_adamw(land_out, w_out[0], m_w_out[0], v_w_out[0], "sum_adamw_w_out"),
             "w_mlp1": _sum_adamw(land_1, w_mlp1[0], m_w_mlp1[0], v_w_mlp1[0], "sum_adamw_w_mlp1"),
             "w_mlp2": _sum_adamw(land_2, w_mlp2[0], m_w_mlp2[0], v_w_mlp2[0], "sum_adamw_w_mlp2")}
    (land_in,) = _scatter_wait([rs_in], fused["w_mlp2"][0], "rs_w_in_wait")
    win_upd = _sum_adamw(land_in, w_in[0].T, m_w_in[0].T, v_w_in[0].T, "sum_adamw_w_in")
    fused["w_in"] = [a.T for a in win_upd]
    (mbuf,) = _scatter_wait([small], win_upd[0], "small_wait")
    tot, g_b_ada, g_w_ada, g_c_ctx = _small_ar(mbuf, silu_all, w_ada[0], c_ctx, B + 1, n_vec_rows)
    flat = tot.reshape(-1)
    o0 = 0
    g_pre_mix_g = flat[o0:o0 + D]; o0 += D
    g_post_mix_g = flat[o0:o0 + D]; o0 += D
    g_pre_mlp_g = flat[o0:o0 + D]; o0 += D
    g_post_mlp_g = flat[o0:o0 + D]; o0 += D
    g_gn = flat[o0:o0 + RET_WIDTH]; o0 += RET_WIDTH
    nrpb = NA_HEADS * (2 * NA_KH - 1) * 32
    g_rpb = flat[o0:o0 + nrpb].reshape(NA_HEADS, 2 * NA_KH - 1, 32)[:, :, :ncls]; o0 += nrpb
    g_decay = flat[o0:o0 + 2 * RET_HEADS].reshape(2, RET_HEADS); o0 += LANES
    loss = flat[o0]

    grads = {
        "c_ctx": g_c_ctx.reshape(c_ctx.shape), "w_ada": g_w_ada[None], "b_ada": g_b_ada.reshape(b_ada.shape),
        "g_pre_mix": g_pre_mix_g[None], "g_post_mix": g_post_mix_g[None], "g_pre_mlp": g_pre_mlp_g[None],
        "g_post_mlp": g_post_mlp_g[None], "w_in": fused["w_in"][0][None], "ret_decay": g_decay[None], "ret_gn": g_gn[None],
        "na_rpb": g_rpb[None], "w_out": fused["w_out"][0][None], "w_mlp1": fused["w_mlp1"][0][None],
        "w_mlp2": fused["w_mlp2"][0][None],
    }
    weights = dict(c_ctx=c_ctx, w_ada=w_ada, b_ada=b_ada, g_pre_mix=g_pre_mix, g_post_mix=g_post_mix,
                   g_pre_mlp=g_pre_mlp, g_post_mlp=g_post_mlp, w_in=w_in, ret_decay=ret_decay, ret_gn=ret_gn,
                   na_rpb=na_rpb, w_out=w_out, w_mlp1=w_mlp1, w_mlp2=w_mlp2)
    m_in = dict(c_ctx=m_c_ctx, w_ada=m_w_ada, b_ada=m_b_ada, g_pre_mix=m_g_pre_mix, g_post_mix=m_g_post_mix,
                g_pre_mlp=m_g_pre_mlp, g_post_mlp=m_g_post_mlp, w_in=m_w_in, ret_decay=m_ret_decay,
                ret_gn=m_ret_gn, na_rpb=m_na_rpb, w_out=m_w_out, w_mlp1=m_w_mlp1, w_mlp2=m_w_mlp2)
    v_in = dict(c_ctx=v_c_ctx, w_ada=v_w_ada, b_ada=v_b_ada, g_pre_mix=v_g_pre_mix, g_post_mix=v_g_post_mix,
                g_pre_mlp=v_g_pre_mlp, g_post_mlp=v_g_post_mlp, w_in=v_w_in, ret_decay=v_ret_decay,
                ret_gn=v_ret_gn, na_rpb=v_na_rpb, w_out=v_w_out, w_mlp1=v_w_mlp1, w_mlp2=v_w_mlp2)
    names = list(weights)
    deltas, new_m, new_v = {}, {}, {}
    def as_2d(n):
        shp = weights[n].shape
        two_d = (-1, shp[-1]) if len(shp) > 1 else (1, shp[0])
        return [a.reshape(two_d) for a in (weights[n], grads[n], m_in[n], v_in[n])]

    small = [n for n in names if n not in fused and weights[n].size <= 65536]
    updated = dict(zip(small, _adamw_small([as_2d(n) for n in small], "adamw_small")))
    for n in names:
        if n in fused:
            updated[n] = fused[n][1:]
        elif n not in updated:
            updated[n] = _adamw(*as_2d(n), "adamw_" + n)
        deltas[n], new_m[n], new_v[n] = (a.reshape(weights[n].shape) for a in updated[n])
    return (loss, grad_x, *[grads[n] for n in names], *[deltas[n] for n in names],
            *[new_m[n] for n in names], *[new_v[n] for n in names])
```

```python
import functools

import numpy as np
import jax
import jax.numpy as jnp
from jax import lax
from jax.experimental import pallas as pl
from jax.experimental.pallas import tpu as pltpu

F32 = jnp.float32
BF16 = jnp.bfloat16
MESH = pl.DeviceIdType.MESH

N_DEV = 8
LANES = 128
SUBLANES = 8
VMEM_LIMIT = 60 * 1024 * 1024

GRID_W = 64
RET_HEADS = 4
RET_DIM = 128
RET_WIDTH = RET_HEADS * RET_DIM
NA_HEADS = 8
NA_DIM = 64
NA_WIDTH = NA_HEADS * NA_DIM
NA_PAIRS = NA_HEADS // 2
NA_KH = 8
NA_KW = 16
NA_GROUP = 8
SEG = 512
ROPE_BASE = 10000.0
NORM_EPS = 1e-6
NEG_INF = -1e30
N_MOD = 6

ADAM_LR = 0.001
ADAM_B1 = 0.9
ADAM_B2 = 0.999
ADAM_EPS = 1e-08
ADAM_WD = 0.01
ADAM_STEP = 10


def _dot(a, b):
    return lax.dot_general(a, b, (((1,), (0,)), ((), ())), preferred_element_type=F32)


def _dot_nt(a, b):
    return lax.dot_general(a, b, (((1,), (1,)), ((), ())), preferred_element_type=F32)


def _dot_tn(a, b):
    return lax.dot_general(a, b, (((0,), (0,)), ((), ())), preferred_element_type=F32)


def _sigmoid(x):
    return 1.0 / (1.0 + jnp.exp(-x))


def _div_tile(n, cap, mult):
    if n <= cap:
        return n
    for t in range(cap - cap % mult, 0, -mult):
        if n % t == 0:
            return t
    raise ValueError(f"no tile for {n}")


def _params(*sem):
    return pltpu.CompilerParams(dimension_semantics=tuple(sem) if sem else None,
                                vmem_limit_bytes=VMEM_LIMIT)


def _vmem():
    return pl.BlockSpec(memory_space=pltpu.VMEM)


def _any():
    return pl.BlockSpec(memory_space=pl.ANY)


def _me_and_peers():
    x, y, c = lax.axis_index("x"), lax.axis_index("y"), lax.axis_index("c")
    me = 4 * x + 2 * y + c
    peers = []
    for m in range(1, N_DEV):
        px = 1 - x if (m >> 2) & 1 else x
        py = 1 - y if (m >> 1) & 1 else y
        pc = 1 - c if m & 1 else c
        peers.append(((px, py, pc), 4 * px + 2 * py + pc))
    return me, peers


def _exchange(src_for, dst_from, send_sems, recv_sems):
    me, peers = _me_and_peers()
    sent = []
    for i, (dev, pid) in enumerate(peers):
        cp = pltpu.make_async_remote_copy(src_ref=src_for(pid), dst_ref=dst_from(me),
                                          send_sem=send_sems.at[i], recv_sem=recv_sems.at[i],
                                          device_id=dev, device_id_type=MESH)
        cp.start()
        sent.append(cp)
    for i, (dev, pid) in enumerate(peers):
        pltpu.make_async_remote_copy(src_ref=src_for(pid), dst_ref=dst_from(pid),
                                     send_sem=send_sems.at[i], recv_sem=recv_sems.at[i],
                                     device_id=dev, device_id_type=MESH).wait_recv()
    for cp in sent:
        cp.wait_send()


SIBLING = (1,)
ICI_SAME_CORE = (2, 4, 6)
ALL_PEERS = tuple(range(1, N_DEV))


def _remote(src, dst, send_sem, recv_sem, dev):
    return pltpu.make_async_remote_copy(src_ref=src, dst_ref=dst, send_sem=send_sem, recv_sem=recv_sem,
                                        device_id=dev, device_id_type=MESH)


def _push_start(items, masks, send_sems, recv_sems):
    me, peers = _me_and_peers()
    for k, (src_for, dst_from) in enumerate(items):
        for m in masks:
            dev, pid = peers[m - 1]
            _remote(src_for(pid), dst_from(me), send_sems.at[k, m - 1], recv_sems.at[k, m - 1], dev).start()


def _push_wait_recv(items, masks, send_sems, recv_sems):
    me, peers = _me_and_peers()
    for k, (src_for, dst_from) in enumerate(items):
        for m in masks:
            dev, pid = peers[m - 1]
            _remote(src_for(pid), dst_from(pid), send_sems.at[k, m - 1], recv_sems.at[k, m - 1], dev).wait_recv()


def _push_wait_send(items, masks, send_sems, recv_sems):
    me, peers = _me_and_peers()
    for k, (src_for, dst_from) in enumerate(items):
        for m in masks:
            dev, pid = peers[m - 1]
            _remote(src_for(pid), dst_from(me), send_sems.at[k, m - 1], recv_sems.at[k, m - 1], dev).wait_send()


def _forward_start(items, send_sems, recv_sems):
    me, peers = _me_and_peers()
    sib = peers[0][0]
    for k, (blk_in, blk_out) in enumerate(items):
        for j, m in enumerate(ICI_SAME_CORE):
            pid = peers[m - 1][1]
            _remote(blk_in(pid), blk_out(pid), send_sems.at[k, j], recv_sems.at[k, j], sib).start()


def _forward_wait(items, send_sems, recv_sems):
    me, peers = _me_and_peers()
    sib = peers[0][0]
    for k, (blk_in, blk_out) in enumerate(items):
        for j, m in enumerate(ICI_SAME_CORE):
            got = peers[(m | 1) - 1][1]
            _remote(blk_in(got), blk_out(got), send_sems.at[k, j], recv_sems.at[k, j], sib).wait_recv()
    for k, (blk_in, blk_out) in enumerate(items):
        for j, m in enumerate(ICI_SAME_CORE):
            pid = peers[m - 1][1]
            _remote(blk_in(pid), blk_out(pid), send_sems.at[k, j], recv_sems.at[k, j], sib).wait_send()


def _mod_gather(c, c_ctx, w_ada, b_ada, w_in_t, w_out, w1, w2):
    B, D = c.shape
    ncol = w_ada.shape[1]
    rows = SUBLANES * N_DEV + SUBLANES

    def body(c_ref, cc_ref, w_ref, b_ref, win_ref, wout_ref, w1_ref, w2_ref,
             s_ref, m_ref, gin_ref, wout_b, w1_b, w2_b,
             win_b, msend, send1, recv1, send2, recv2, wsend, wrecv, fsend, frecv, lsem):
        me, _ = _me_and_peers()
        win_b[...] = win_ref[...].astype(BF16)
        block = _row_block(gin_ref, w_in_t.shape[0])
        gather = [(lambda p: win_b, block)]
        own = pltpu.make_async_copy(win_b, block(me), lsem.at[0])
        cv = c_ref[...]
        slot = jnp.concatenate([cv * _sigmoid(cv), jnp.zeros((SUBLANES - B, D), F32)], axis=0)
        my_rows = pl.ds(pl.multiple_of(me * SUBLANES, SUBLANES), SUBLANES)
        s_ref[my_rows, :] = slot
        ccv = cc_ref[...]
        s_ref[SUBLANES * N_DEV:, :] = jnp.concatenate(
            [ccv * _sigmoid(ccv), jnp.zeros((SUBLANES - 1, D), F32)], axis=0)

        def rows_of(p):
            return s_ref.at[pl.ds(pl.multiple_of(p * SUBLANES, SUBLANES), SUBLANES), :]

        _exchange(lambda p: rows_of(me), rows_of, send1, recv1)
        own.start()
        _push_start(gather, SIBLING + ICI_SAME_CORE, wsend, wrecv)
        wout_b[...] = wout_ref[...].astype(BF16)
        w1_b[...] = w1_ref[...].astype(BF16)
        w2_b[...] = w2_ref[...].astype(BF16)
        b_loc = b_ref[:, pl.ds(pl.multiple_of(me * ncol, ncol), ncol)]
        mods = _dot(s_ref[...], w_ref[...]) + b_loc
        for p in range(N_DEV):
            msend[p] = jnp.concatenate([mods[p * SUBLANES:(p + 1) * SUBLANES], mods[N_DEV * SUBLANES:]], axis=0)
        m_ref[me] = msend[me]
        columns = [(lambda p: msend.at[p], lambda p: m_ref.at[p])]
        _push_start(columns, ALL_PEERS, send2, recv2)
        _push_wait_recv(gather, ICI_SAME_CORE, wsend, wrecv)
        relay = [(block, block)]
        _forward_start(relay, fsend, frecv)
        _push_wait_recv(columns, ALL_PEERS, send2, recv2)
        _push_wait_recv(gather, SIBLING, wsend, wrecv)
        _forward_wait(relay, fsend, frecv)
        _push_wait_send(columns, ALL_PEERS, send2, recv2)
        _push_wait_send(gather, SIBLING + ICI_SAME_CORE, wsend, wrecv)
        own.wait()

    return pl.pallas_call(
        body, name="mod_gather",
        out_shape=(jax.ShapeDtypeStruct((rows, D), F32), jax.ShapeDtypeStruct((N_DEV, 2 * SUBLANES, ncol), F32),
                   jax.ShapeDtypeStruct((N_DEV * w_in_t.shape[0], D), BF16),
                   jax.ShapeDtypeStruct(w_out.shape, BF16), jax.ShapeDtypeStruct(w1.shape, BF16),
                   jax.ShapeDtypeStruct(w2.shape, BF16)),
        in_specs=[_vmem()] * 8, out_specs=(_vmem(), _vmem(), _any(), _vmem(), _vmem(), _vmem()),
        scratch_shapes=[pltpu.VMEM(w_in_t.shape, BF16), pltpu.VMEM((N_DEV, 2 * SUBLANES, ncol), F32)]
                       + [pltpu.SemaphoreType.DMA((N_DEV - 1,))] * 2
                       + [pltpu.SemaphoreType.DMA((1, N_DEV - 1))] * 4 + [pltpu.SemaphoreType.DMA((1, 3))] * 2
                       + [pltpu.SemaphoreType.DMA((1,))],
        compiler_params=pltpu.CompilerParams(vmem_limit_bytes=VMEM_LIMIT),
    )(c, c_ctx.reshape(1, D), w_ada, b_ada, w_in_t, w_out, w1, w2)


def _row_block(ref, rows):
    return lambda p: ref.at[pl.ds(pl.multiple_of(p * rows, 2 * SUBLANES), rows), :]


def _col_block(ref, cols):
    return lambda p: ref.at[:, pl.ds(pl.multiple_of(p * cols, LANES), cols)]


def _slot(ref):
    return lambda p: ref.at[p]


def _grid_call(body, *, name, grid, out_shape, in_specs, out_specs, scratch_shapes, args, after=None):
    n_in = len(args)

    def ordered_body(*refs):
        body(*refs[:n_in], *refs[n_in + 1:])

    return pl.pallas_call(
        body if after is None else ordered_body, name=name, grid=grid, out_shape=tuple(out_shape),
        in_specs=list(in_specs) + ([] if after is None else [_any()]), out_specs=tuple(out_specs),
        scratch_shapes=list(scratch_shapes), compiler_params=_params(*(("arbitrary",) * len(grid))),
    )(*args, *([] if after is None else [after]))


def _token_tiles(n_ctx, tm):
    nct = n_ctx // tm

    def ctx_spec(D):
        return pl.BlockSpec((None, tm, D), lambda b, t: (b, jnp.minimum(t, nct - 1), 0))

    def lat_spec(D):
        return pl.BlockSpec((None, tm, D), lambda b, t: (b, jnp.maximum(t - nct, 0), 0))

    return nct, ctx_spec, lat_spec


def _inproj_fwd(x, ctx, modl, g1, w_in_t, after):
    B, N, D = x.shape
    n_ctx = ctx.shape[1]
    T = n_ctx + N
    nw = w_in_t.shape[0]
    tm = _div_tile(n_ctx, 256, 16)
    nct, ctx_spec, lat_spec = _token_tiles(n_ctx, tm)

    def body(c_ref, x_ref, sh_ref, sc_ref, g_ref, w_ref, h_ref, p_ref):
        x = jnp.where(pl.program_id(1) < nct, c_ref[...], x_ref[...])
        r = lax.rsqrt(jnp.mean(x * x, axis=-1, keepdims=True) + NORM_EPS)
        h = ((x * r) * g_ref[...]) * (1.0 + sc_ref[...]) + sh_ref[...]
        hb = h.astype(BF16)
        h_ref[...] = hb
        p_ref[...] = _dot_nt(hb, w_ref[...]).astype(BF16)

    def mrow(b, t):
        return jnp.where(t < nct, B, b)

    return _grid_call(
        body, name="inproj_fwd", grid=(B, T // tm),
        out_shape=(jax.ShapeDtypeStruct((B, T, D), BF16), jax.ShapeDtypeStruct((B, T, nw), BF16)),
        in_specs=[ctx_spec(D), lat_spec(D),
                  pl.BlockSpec((None, None, 1, D), lambda b, t: (mrow(b, t), 0, 0, 0)),
                  pl.BlockSpec((None, None, 1, D), lambda b, t: (mrow(b, t), 1, 0, 0)),
                  pl.BlockSpec((1, D), lambda b, t: (0, 0)),
                  pl.BlockSpec((nw, D), lambda b, t: (0, 0))],
        out_specs=(pl.BlockSpec((None, tm, D), lambda b, t: (b, t, 0)),
                   pl.BlockSpec((None, tm, nw), lambda b, t: (b, t, 0))),
        scratch_shapes=[], args=(ctx, x, modl, modl, g1, w_in_t), after=after)


def _swap32(x):
    lane = lax.broadcasted_iota(jnp.int32, x.shape, 1)
    return jnp.where((lane % 64) < 32, pltpu.roll(x, 96, 1), pltpu.roll(x, 32, 1))


def _rope(x, cos, sin):
    return x * cos + _swap32(x) * sin


def _unrope(dy, cos, sin):
    return dy * cos + _swap32(dy * sin)


def _ret_weights(lgf, lgb, dist):
    return jnp.exp(jnp.where(dist >= 0.0, lgf * dist, -lgb * dist))


class _RetDecay:
    def __init__(self, lgf, lgb, rows):
        r = lax.broadcasted_iota(jnp.int32, (rows, RET_DIM), 0).astype(F32)
        self.head = r + 1.0
        self.tail = (rows - 1.0) - r
        self.q_f = jnp.exp(lgf * self.head)
        self.k_f = jnp.exp(lgf * self.tail)
        self.q_b = jnp.exp(lgb * self.tail)
        self.k_b = jnp.exp(lgb * self.head)


def _ret_states(kf32, vs, lgf, lgb, C, c, nt, hf, hb, hfa=None, hba=None):
    dec = _RetDecay(lgf, lgb, c)
    dec_c = _RetDecay(lgf, lgb, C)
    step_f = jnp.exp(jnp.zeros((RET_DIM, RET_DIM), F32) + lgf * c)
    step_b = jnp.exp(jnp.zeros((RET_DIM, RET_DIM), F32) + lgb * c)

    def upd(rows, kdec):
        return _dot_tn((kf32[rows, :] * kdec).astype(BF16), vs[rows, :])

    def lat(t):
        return slice(C + t * c, C + (t + 1) * c)

    state = upd(slice(0, C), dec_c.k_f)
    aged = jnp.zeros_like(state)
    for t in range(nt):
        hf[t] = state.astype(BF16)
        if hfa is not None:
            hfa[t] = aged
        if t < nt - 1:
            aged = step_f * (aged + c * state)
            state = step_f * state + upd(lat(t), dec.k_f)
    state = upd(slice(0, C), dec_c.k_b)
    aged = jnp.zeros_like(state)
    for t in range(nt - 1, -1, -1):
        hb[t] = state.astype(BF16)
        if hba is not None:
            hba[t] = aged
        if t > 0:
            aged = step_b * (aged + c * state)
            state = step_b * state + upd(lat(t), dec.k_b)
    return dec, dec_c, step_f, step_b


def _ret_fwd(proj, cos, sin, lg, gn, n_ctx):
    B, T, _ = proj.shape
    C = n_ctx
    N = T - C
    c = _div_tile(N, 256, 16)
    nt = N // c
    scale = RET_DIM ** -0.5

    def body(lg_ref, q_ref, k_ref, vs, g_ref, cos_ref, sin_ref, gn_ref, o_ref, lat_ref, qr_ref, kf32,
             qs, ks, hf, hb):
        h = pl.program_id(1)
        lgf = lg_ref[0, h]
        lgb = lg_ref[1, h]
        for rows in [slice(0, C)] + [slice(C + t * c, C + (t + 1) * c) for t in range(nt)]:
            cosb = cos_ref[rows, :]
            sinb = sin_ref[rows, :]
            qr = _rope(q_ref[rows, :].astype(F32), cosb, sinb) * scale
            qr_ref[rows, :] = qr
            qs[rows, :] = qr.astype(BF16)
            kr = _rope(k_ref[rows, :].astype(F32), cosb, sinb)
            kf32[rows, :] = kr
            ks[rows, :] = kr.astype(BF16)
        gnv = gn_ref[...]
        dec, _, _, _ = _ret_states(kf32, vs, lgf, lgb, C, c, nt, hf, hb)
        rc = (lax.broadcasted_iota(jnp.int32, (c, c), 0) - lax.broadcasted_iota(jnp.int32, (c, c), 1)).astype(F32)
        w_diag = _ret_weights(lgf, lgb, rc)
        for t in range(nt):
            rows = slice(C + t * c, C + (t + 1) * c)
            qt = qs[rows, :]
            s = _dot_nt(qt, ks[rows, :])
            o = (_dot((s * w_diag).astype(BF16), vs[rows, :])
                 + dec.q_f * _dot(qt, hf[t]) + dec.q_b * _dot(qt, hb[t]))
            o_ref[t * c:(t + 1) * c, :] = o
            mu = jnp.mean(o, axis=-1, keepdims=True)
            oc = o - mu
            var = jnp.mean(oc * oc, axis=-1, keepdims=True)
            yh = oc * lax.rsqrt(var + NORM_EPS)
            g = g_ref[rows, :].astype(F32)
            lat_ref[t * c:(t + 1) * c, :] = ((yh * gnv) * (g * _sigmoid(g))).astype(BF16)

    def col(seg):
        return pl.BlockSpec((None, T, RET_DIM), lambda b, h, seg=seg: (b, 0, seg * RET_HEADS + h))

    return _grid_call(
        body, name="ret_fwd", grid=(B, RET_HEADS),
        out_shape=(jax.ShapeDtypeStruct((B, N, RET_WIDTH), F32), jax.ShapeDtypeStruct((B, N, RET_WIDTH), BF16),
                   jax.ShapeDtypeStruct((B, T, RET_WIDTH), F32), jax.ShapeDtypeStruct((B, T, RET_WIDTH), F32)),
        in_specs=[pl.BlockSpec(memory_space=pltpu.SMEM), col(0), col(1), col(2), col(3),
                  pl.BlockSpec((T, RET_DIM), lambda b, h: (0, 0)), pl.BlockSpec((T, RET_DIM), lambda b, h: (0, 0)),
                  pl.BlockSpec((1, RET_DIM), lambda b, h: (0, h))],
        out_specs=(pl.BlockSpec((None, N, RET_DIM), lambda b, h: (b, 0, h)),
                   pl.BlockSpec((None, N, RET_DIM), lambda b, h: (b, 0, h)),
                   pl.BlockSpec((None, T, RET_DIM), lambda b, h: (b, 0, h)),
                   pl.BlockSpec((None, T, RET_DIM), lambda b, h: (b, 0, h))),
        scratch_shapes=[pltpu.VMEM((T, RET_DIM), BF16)] * 2 + [pltpu.VMEM((nt, RET_DIM, RET_DIM), BF16)] * 2,
        args=(lg, proj, proj, proj, proj, cos, sin, gn))


def _ret_bwd(proj, q_rot, k_rot, cos, sin, lg, gn, o, dlat, n_ctx, after):
    B, T, _ = proj.shape
    C = n_ctx
    N = T - C
    c = _div_tile(N, 256, 16)
    nt = N // c
    scale = RET_DIM ** -0.5

    def lat(t):
        return slice(C + t * c, C + (t + 1) * c)

    def body(lg_ref, qf32, kf32, vs, g_ref, cos_ref, sin_ref, gn_ref, o_ref, dl_ref,
             d_ref, dgn_ref, dlg_ref, qs, ks, dos, hf, hb, hfa, hba, gf_s, gb_s):
        h = pl.program_id(1)
        lgf = lg_ref[0, h]
        lgb = lg_ref[1, h]
        gnv = gn_ref[...]

        def fold(a):
            return jnp.sum(a.reshape(a.shape[0] // SUBLANES, SUBLANES, a.shape[1]), axis=0)

        for rows in [slice(0, C)] + [lat(t) for t in range(nt)]:
            qs[rows, :] = qf32[rows, :].astype(BF16)
            ks[rows, :] = kf32[rows, :].astype(BF16)

        dgn = jnp.zeros((1, RET_DIM), F32)
        for t in range(nt):
            lrows = slice(t * c, (t + 1) * c)
            ov = o_ref[lrows, :]
            mu = jnp.mean(ov, axis=-1, keepdims=True)
            oc = ov - mu
            var = jnp.mean(oc * oc, axis=-1, keepdims=True)
            rstd = lax.rsqrt(var + NORM_EPS)
            yh = oc * rstd
            g = g_ref[lat(t), :].astype(F32)
            sg = _sigmoid(g)
            dl = dl_ref[lrows, :]
            d_ref[3, lat(t), :] = (dl * (yh * gnv) * (sg * (1.0 + g * (1.0 - sg)))).astype(BF16)
            dls = dl * (g * sg)
            dgn = dgn + jnp.sum(dls * yh, axis=0, keepdims=True)
            dyh = dls * gnv
            do = rstd * (dyh - jnp.mean(dyh, axis=-1, keepdims=True)
                         - yh * jnp.mean(dyh * yh, axis=-1, keepdims=True))
            dos[lrows, :] = do.astype(BF16)
        dgn_ref[...] = jnp.concatenate([dgn, jnp.zeros((SUBLANES - 1, RET_DIM), F32)], axis=0)
        d_ref[3, 0:C, :] = jnp.zeros((C, RET_DIM), BF16)
        d_ref[0, 0:C, :] = jnp.zeros((C, RET_DIM), BF16)

        dec, dec_c, step_f, step_b = _ret_states(kf32, vs, lgf, lgb, C, c, nt, hf, hb, hfa, hba)

        def zmat(t, qdec):
            return _dot_tn((qf32[lat(t), :] * qdec).astype(BF16), dos[t * c:(t + 1) * c, :])

        acc3f = jnp.zeros((RET_DIM, RET_DIM), F32)
        acc3b = jnp.zeros((RET_DIM, RET_DIM), F32)
        state = jnp.zeros((RET_DIM, RET_DIM), F32)
        for t in range(nt - 1, -1, -1):
            gf_s[t] = state.astype(BF16)
            z = zmat(t, dec.q_f)
            acc3f = acc3f + hfa[t] * z
            state = step_f * state + z
        gctx_f = state.astype(BF16)
        state = jnp.zeros((RET_DIM, RET_DIM), F32)
        for t in range(nt):
            gb_s[t] = state.astype(BF16)
            z = zmat(t, dec.q_b)
            acc3b = acc3b + hba[t] * z
            state = step_b * state + z
        gctx_b = state.astype(BF16)

        rc = (lax.broadcasted_iota(jnp.int32, (c, c), 0) - lax.broadcasted_iota(jnp.int32, (c, c), 1)).astype(F32)
        w_diag = _ret_weights(lgf, lgb, rc)
        wg_f = jnp.where(rc >= 0.0, w_diag * rc, 0.0)
        wg_b = jnp.where(rc < 0.0, -w_diag * rc, 0.0)
        accf = jnp.zeros((SUBLANES, RET_DIM), F32)
        accb = jnp.zeros((SUBLANES, RET_DIM), F32)
        gdf = jnp.zeros((SUBLANES, c), F32)
        gdb = jnp.zeros((SUBLANES, c), F32)
        for t in range(nt):
            rows = lat(t)
            qt = qs[rows, :]
            kt = ks[rows, :]
            vt = vs[rows, :]
            dot = dos[t * c:(t + 1) * c, :]
            s = _dot_nt(qt, kt)
            dp = _dot_nt(dot, vt)
            dv = _dot_tn((s * w_diag).astype(BF16), dot)
            ds = (dp * w_diag).astype(BF16)
            dq = _dot(ds, kt)
            dk = _dot_tn(ds, qt)
            gs = dp * s
            gdf = gdf + fold(gs * wg_f)
            gdb = gdb + fold(gs * wg_b)
            qv = qf32[rows, :]
            kv = kf32[rows, :]
            dq_f = dec.q_f * _dot_nt(dot, hf[t])
            dq_b = dec.q_b * _dot_nt(dot, hb[t])
            dk_f = dec.k_f * _dot_nt(vt, gf_s[t])
            dk_b = dec.k_b * _dot_nt(vt, gb_s[t])
            accf = accf + fold(dec.head * dq_f * qv) + fold(dec.tail * dk_f * kv)
            accb = accb + fold(dec.tail * dq_b * qv) + fold(dec.head * dk_b * kv)
            dv = dv + dec.k_f * _dot(kt, gf_s[t]) + dec.k_b * _dot(kt, gb_s[t])
            cosb = cos_ref[rows, :]
            sinb = sin_ref[rows, :]
            d_ref[0, rows, :] = _unrope((dq + dq_f + dq_b) * scale, cosb, sinb).astype(BF16)
            d_ref[1, rows, :] = _unrope(dk + dk_f + dk_b, cosb, sinb).astype(BF16)
            d_ref[2, rows, :] = dv.astype(BF16)
        kc = ks[0:C, :]
        vc = vs[0:C, :]
        kcv = kf32[0:C, :]
        dkc_f = dec_c.k_f * _dot_nt(vc, gctx_f)
        dkc_b = dec_c.k_b * _dot_nt(vc, gctx_b)
        accf = accf + fold(dec_c.tail * dkc_f * kcv)
        accb = accb + fold(dec_c.head * dkc_b * kcv)
        d_ref[1, 0:C, :] = (dkc_f + dkc_b).astype(BF16)
        d_ref[2, 0:C, :] = (dec_c.k_f * _dot(kc, gctx_f) + dec_c.k_b * _dot(kc, gctx_b)).astype(BF16)
        gf = jnp.sum(gdf) + jnp.sum(accf) + jnp.sum(acc3f)
        gb = jnp.sum(gdb) + jnp.sum(accb) + jnp.sum(acc3b)
        row = lax.broadcasted_iota(jnp.int32, (SUBLANES, LANES), 0)
        dlg_ref[...] = jnp.where(row == 0, gf, jnp.where(row == 1, gb, 0.0))

    def col(seg):
        return pl.BlockSpec((None, T, RET_DIM), lambda b, h, seg=seg: (b, 0, seg * RET_HEADS + h))

    def head(rows):
        return pl.BlockSpec((None, rows, RET_DIM), lambda b, h: (b, 0, h))

    return _grid_call(
        body, name="ret_bwd", grid=(B, RET_HEADS),
        out_shape=(jax.ShapeDtypeStruct((B, 4, T, RET_WIDTH), BF16),
                   jax.ShapeDtypeStruct((B, SUBLANES, RET_WIDTH), F32),
                   jax.ShapeDtypeStruct((B, RET_HEADS, SUBLANES, LANES), F32)),
        in_specs=[pl.BlockSpec(memory_space=pltpu.SMEM), head(T), head(T), col(2), col(3),
                  pl.BlockSpec((T, RET_DIM), lambda b, h: (0, 0)), pl.BlockSpec((T, RET_DIM), lambda b, h: (0, 0)),
                  pl.BlockSpec((1, RET_DIM), lambda b, h: (0, h)), head(N), head(N)],
        out_specs=(pl.BlockSpec((None, 4, T, RET_DIM), lambda b, h: (b, 0, 0, h)),
                   pl.BlockSpec((None, SUBLANES, RET_DIM), lambda b, h: (b, 0, h)),
                   pl.BlockSpec((None, None, SUBLANES, LANES), lambda b, h: (b, h, 0, 0))),
        scratch_shapes=[pltpu.VMEM((T, RET_DIM), BF16)] * 2 + [pltpu.VMEM((N, RET_DIM), BF16)]
                       + [pltpu.VMEM((nt, RET_DIM, RET_DIM), BF16)] * 2 + [pltpu.VMEM((nt, RET_DIM, RET_DIM), F32)] * 2
                       + [pltpu.VMEM((nt, RET_DIM, RET_DIM), BF16)] * 2,
        args=(lg, q_rot, k_rot, proj, proj, cos, sin, gn, o, dlat), after=after)


def _na_geometry(rows):
    kh = min(NA_KH, rows)
    return kh, kh * GRID_W


def _pair_select():
    lane = lax.broadcasted_iota(jnp.int32, (2 * GRID_W, LANES), 1)
    row = lax.broadcasted_iota(jnp.int32, (2 * GRID_W, LANES), 0)
    return (lane >= NA_DIM) == (row >= GRID_W)


def _pair_bias(bias_ref, dr0, kh):
    return jnp.concatenate(
        [jnp.concatenate([bias_ref[e, pl.ds(dr0 + 2 * m, 1)].reshape(GRID_W, LANES) for m in range(kh // 2)], axis=1)
         for e in range(2)], axis=0)


def _na_softmax(s_loc, s_ctx):
    mx = jnp.maximum(jnp.max(s_loc, axis=-1, keepdims=True), jnp.max(s_ctx, axis=-1, keepdims=True))
    p_loc = jnp.exp(s_loc - mx)
    p_ctx = jnp.exp(s_ctx - mx)
    den = jnp.sum(p_loc, axis=-1, keepdims=True) + jnp.sum(p_ctx, axis=-1, keepdims=True)
    return p_loc, p_ctx, den


def _na_fwd(proj, bias2, n_ctx):
    assert proj.dtype == BF16
    B, T, _ = proj.shape
    C = n_ctx
    N = T - C
    R = N // GRID_W
    kh, nk = _na_geometry(R)
    scale = NA_DIM ** -0.5
    base = (4 * RET_WIDTH) // LANES

    def body(q_ref, kb16, vb16, bias_ref, out_ref, p_ref):
        kc = kb16[0:C, :]
        vc = vb16[0:C, :]
        lane = lax.broadcasted_iota(jnp.int32, (GRID_W, LANES), 1)
        sel2 = _pair_select()

        def group(gi, carry):
            pre = []
            for u in range(NA_GROUP):
                r = gi * NA_GROUP + u
                bs = jnp.clip(r - kh // 2, 0, R - kh)
                dr0 = bs - r + (NA_KH - 1)
                q = q_ref[pl.ds(pl.multiple_of(C + r * GRID_W, GRID_W), GRID_W), :].astype(F32) * scale
                q2 = jnp.where(sel2, jnp.concatenate([q, q], axis=0), 0.0).astype(BF16)
                band = pl.ds(pl.multiple_of(C + bs * GRID_W, GRID_W), nk)
                s_loc = _dot_nt(q2, kb16[band, :]) + _pair_bias(bias_ref, dr0, kh)
                s_ctx = _dot_nt(q2, kc)
                pre.append((r, band, s_loc, s_ctx))
            mid = [(r, band) + _na_softmax(s_loc, s_ctx) for r, band, s_loc, s_ctx in pre]
            for r, band, p_loc, p_ctx, den in mid:
                inv = 1.0 / den
                pb_loc = (p_loc * inv).astype(BF16)
                pb_ctx = (p_ctx * inv).astype(BF16)
                p_ref[r, :, 0:nk] = pb_loc
                p_ref[r, :, nk:] = pb_ctx
                o2 = _dot(pb_loc, vb16[band, :]) + _dot(pb_ctx, vc)
                out_ref[pl.ds(pl.multiple_of(r * GRID_W, GRID_W), GRID_W), :] = jnp.where(
                    lane < NA_DIM, o2[:GRID_W], o2[GRID_W:]).astype(BF16)
            return carry

        lax.fori_loop(0, R // NA_GROUP, group, 0)

    def col(seg):
        return pl.BlockSpec((None, T, LANES), lambda b, p, seg=seg: (b, 0, base + seg * NA_PAIRS + p))

    return _grid_call(
        body, name="na_fwd", grid=(B, NA_PAIRS),
        out_shape=(jax.ShapeDtypeStruct((B, N, NA_WIDTH), BF16),
                   jax.ShapeDtypeStruct((B, NA_PAIRS, R, 2 * GRID_W, nk + C), BF16)),
        in_specs=[col(0), col(1), col(2),
                  pl.BlockSpec((2, 2 * NA_KH - 2, GRID_W, LANES), lambda b, p: (p, 0, 0, 0))],
        out_specs=(pl.BlockSpec((None, N, LANES), lambda b, p: (b, 0, p)),
                   pl.BlockSpec((None, None, R, 2 * GRID_W, nk + C), lambda b, p: (b, p, 0, 0, 0))),
        scratch_shapes=[],
        args=(proj, proj, proj, bias2))


def _na_bwd(proj, probs, dlat, n_ctx):
    assert proj.dtype == BF16
    B, T, _ = proj.shape
    C = n_ctx
    N = T - C
    R = N // GRID_W
    kh, nk = _na_geometry(R)
    scale = NA_DIM ** -0.5
    base = (4 * RET_WIDTH) // LANES

    def class_sums(tiles):
        n = tiles.shape[0]
        acc = None
        for v in range(GRID_W // SUBLANES):
            part = tiles[:, v * SUBLANES:(v + 1) * SUBLANES, :].reshape(n * SUBLANES, LANES)
            part = pltpu.roll(part, (NA_KW - 1 - v * SUBLANES) % LANES, 1)
            acc = part if acc is None else acc + part
        row = lax.broadcasted_iota(jnp.int32, acc.shape, 0)
        for bit in (1, 2, 4):
            acc = jnp.where((row & bit) != 0, pltpu.roll(acc, LANES - bit, 1), acc)
        return jnp.sum(acc.reshape(n, SUBLANES, LANES), axis=1)

    def body(q_ref, kb16, vb16, p_ref, dl_ref, d_ref, rr_ref, dkv, db_ref):
        b = pl.program_id(1)
        kc = kb16[0:C, :]
        vc = vb16[0:C, :]
        lane = lax.broadcasted_iota(jnp.int32, (GRID_W, LANES), 1)
        dkv[...] = jnp.zeros(dkv.shape, F32)
        d_ref[0, 0:C, :] = jnp.zeros((C, LANES), BF16)

        @pl.when(b == 0)
        def _():
            db_ref[...] = jnp.zeros(db_ref.shape, F32)

        sel2 = _pair_select()

        def group(gi, carry):
            pre = []
            for u in range(NA_GROUP):
                r = gi * NA_GROUP + u
                bs = jnp.clip(r - kh // 2, 0, R - kh)
                dr0 = bs - r + (NA_KH - 1)
                q = q_ref[pl.ds(pl.multiple_of(C + r * GRID_W, GRID_W), GRID_W), :].astype(F32) * scale
                do = dl_ref[pl.ds(pl.multiple_of(r * GRID_W, GRID_W), GRID_W), :]
                q2 = jnp.where(sel2, jnp.concatenate([q, q], axis=0), 0.0).astype(BF16)
                do2 = jnp.where(sel2, jnp.concatenate([do, do], axis=0), 0.0).astype(BF16)
                band = pl.ds(pl.multiple_of(C + bs * GRID_W, GRID_W), nk)
                dp_loc = _dot_nt(do2, vb16[band, :])
                dp_ctx = _dot_nt(do2, vc)
                pre.append((r, dr0, band, q2, do2, dp_loc, dp_ctx))
            mid = []
            for r, dr0, band, q2, do2, dp_loc, dp_ctx in pre:
                pb_loc = p_ref[r, :, 0:nk]
                pb_ctx = p_ref[r, :, nk:]
                p_loc = pb_loc.astype(F32)
                p_ctx = pb_ctx.astype(F32)
                delta = (jnp.sum(p_loc * dp_loc, axis=-1, keepdims=True)
                         + jnp.sum(p_ctx * dp_ctx, axis=-1, keepdims=True))
                ds_loc = p_loc * (dp_loc - delta)
                ds_ctx = p_ctx * (dp_ctx - delta)
                mid.append((r, dr0, band, q2, do2, pb_loc, pb_ctx, ds_loc, ds_ctx))
            for r, dr0, band, q2, do2, pb_loc, pb_ctx, ds_loc, ds_ctx in mid:
                dsb_loc = ds_loc.astype(BF16)
                dsb_ctx = ds_ctx.astype(BF16)
                dq2 = _dot(dsb_loc, kb16[band, :]) + _dot(dsb_ctx, kc)
                d_ref[0, pl.ds(pl.multiple_of(C + r * GRID_W, GRID_W), GRID_W), :] = (jnp.where(
                    lane < NA_DIM, dq2[:GRID_W], dq2[GRID_W:]) * scale).astype(BF16)
                dkv[0, band, :] += _dot_tn(dsb_loc, q2)
                dkv[1, band, :] += _dot_tn(pb_loc, do2)
                dkv[0, 0:C, :] += _dot_tn(dsb_ctx, q2)
                dkv[1, 0:C, :] += _dot_tn(pb_ctx, do2)
                for e in range(2):
                    for m in range(kh // 2):
                        db_ref[e, pl.ds(dr0 + 2 * m, 1)] += ds_loc[e * GRID_W:(e + 1) * GRID_W,
                                                                   m * LANES:(m + 1) * LANES].reshape(1, GRID_W, LANES)
            return carry

        lax.fori_loop(0, R // NA_GROUP, group, 0)
        d_ref[1] = dkv[0].astype(BF16)
        d_ref[2] = dkv[1].astype(BF16)

        @pl.when(b == B - 1)
        def _():
            for e in range(2):
                rr_ref[e] = class_sums(db_ref[e])

    def col(seg):
        return pl.BlockSpec((None, T, LANES), lambda p, b, seg=seg: (b, 0, base + seg * NA_PAIRS + p))

    return _grid_call(
        body, name="na_bwd", grid=(NA_PAIRS, B),
        out_shape=(jax.ShapeDtypeStruct((B, 3, T, NA_WIDTH), BF16),
                   jax.ShapeDtypeStruct((NA_HEADS, 2 * NA_KH - 2, LANES), F32)),
        in_specs=[col(0), col(1), col(2),
                  pl.BlockSpec((None, None, R, 2 * GRID_W, nk + C), lambda p, b: (b, p, 0, 0, 0)),
                  pl.BlockSpec((None, N, LANES), lambda p, b: (b, 0, p))],
        out_specs=(pl.BlockSpec((None, 3, T, LANES), lambda p, b: (b, 0, 0, p)),
                   pl.BlockSpec((2, 2 * NA_KH - 2, LANES), lambda p, b: (p, 0, 0))),
        scratch_shapes=[pltpu.VMEM((2, T, LANES), F32), pltpu.VMEM((2, 2 * NA_KH - 2, GRID_W, LANES), F32)],
        args=(proj, proj, proj, probs, dlat))


def _dense_core(lat_ret, lat_na, x, tgt, modl, g_post_mix, g_pre_mlp, g_post_mlp, w_out, w1, w2):
    B, N, D = x.shape
    F = w1.shape[1]
    wout_rows, w1_cols, w2_rows = w_out.shape[0] // N_DEV, w1.shape[1] // N_DEV, w2.shape[0] // N_DEV
    mixw = w_out.shape[0]
    half = mixw // 2
    tm = _div_tile(N, 256, 16)
    nt = N // tm
    fc = _div_tile(F, 1024, LANES)

    def body(lr_ref, ln_ref, x_ref, t_ref, gt1_ref, sh2_ref, sc2_ref, gt2_ref, gpm_ref, gpre_ref, gpo_ref,
             wout_part, w1_part, w2_part,
             dy1_ref, dlr_ref, dln_ref, dmix_ref, h2_ref, a_ref, du_ref, dz_ref, red_ref, wout_hbm, w1_hbm, w2_hbm,
             wout_v, w1_v, w2_v, u_s, sems, fsend, frecv):
        @pl.when((pl.program_id(0) == 0) & (pl.program_id(1) == 0))
        def _():
            relay = [(_row_block(wout_part, wout_rows), _row_block(wout_hbm, wout_rows)),
                     (_col_block(w1_part, w1_cols), _col_block(w1_hbm, w1_cols)),
                     (_row_block(w2_part, w2_rows), _row_block(w2_hbm, w2_rows))]
            _forward_start(relay, fsend, frecv)
            _forward_wait(relay, fsend, frecv)
            cps = [pltpu.make_async_copy(wout_hbm, wout_v, sems.at[0]),
                   pltpu.make_async_copy(w1_hbm, w1_v, sems.at[1]),
                   pltpu.make_async_copy(w2_hbm, w2_v, sems.at[2])]
            for cp in cps:
                cp.start()
            for cp in cps:
                cp.wait()

        @pl.when(pl.program_id(1) == 0)
        def _():
            red_ref[...] = jnp.zeros(red_ref.shape, F32)

        gt1 = gt1_ref[...]
        sh2 = sh2_ref[...]
        sc2 = sc2_ref[...]
        gt2 = gt2_ref[...]
        gpm = gpm_ref[...]
        gpre = gpre_ref[...]
        gpo = gpo_ref[...]

        def rowmean(a):
            return jnp.mean(a, axis=-1, keepdims=True)

        def colsum(a):
            return jnp.sum(a, axis=0, keepdims=True)

        mix_gain = gt1 * gpm
        mlp_in_gain = gpre * (1.0 + sc2)
        mlp_out_gain = gt2 * gpo
        mix = _dot(lr_ref[...], wout_v[0:half, :]) + _dot(ln_ref[...], wout_v[half:, :])
        x = x_ref[...]
        rm = lax.rsqrt(rowmean(mix * mix) + NORM_EPS)
        mh = mix * rm
        y1 = x + mh * mix_gain
        r1 = lax.rsqrt(rowmean(y1 * y1) + NORM_EPS)
        xh = y1 * r1
        h2b = (xh * mlp_in_gain + sh2).astype(BF16)
        h2_ref[...] = h2b
        z = jnp.zeros((tm, D), F32)
        for c0 in range(0, F, fc):
            u = _dot(h2b, w1_v[:, c0:c0 + fc])
            u_s[:, c0:c0 + fc] = u
            ru = jnp.maximum(u, 0.0)
            ab = (ru * ru).astype(BF16)
            a_ref[:, c0:c0 + fc] = ab
            z = z + _dot(ab, w2_v[c0:c0 + fc, :])
        r2 = lax.rsqrt(rowmean(z * z) + NORM_EPS)
        zh = z * r2
        y2 = y1 + zh * mlp_out_gain
        err = y2 - t_ref[...]
        loss = 0.5 * jnp.sum(rowmean(err * err))
        dy2 = err * (1.0 / D)
        s_out = colsum(dy2 * zh)
        red_ref[2:3, :] += s_out * gpo
        red_ref[6:7, :] += s_out * gt2
        dzh = dy2 * mlp_out_gain
        dz = r2 * (dzh - zh * rowmean(dzh * zh))
        dzb = dz.astype(BF16)
        dz_ref[...] = dzb
        dh2 = jnp.zeros((tm, D), F32)
        for c0 in range(0, F, fc):
            da = _dot_nt(dzb, w2_v[c0:c0 + fc, :])
            dub = (da * (2.0 * jnp.maximum(u_s[:, c0:c0 + fc], 0.0))).astype(BF16)
            du_ref[:, c0:c0 + fc] = dub
            dh2 = dh2 + _dot_nt(dub, w1_v[:, c0:c0 + fc])
        s_in = colsum(dh2 * xh)
        red_ref[3:4, :] += s_in * gpre
        red_ref[4:5, :] += colsum(dh2)
        red_ref[5:6, :] += s_in * (1.0 + sc2)
        dxh = dh2 * mlp_in_gain
        dy1 = dy2 + r1 * (dxh - xh * rowmean(dxh * xh))
        dy1_ref[...] = dy1
        s_mix = colsum(dy1 * mh)
        red_ref[0:1, :] += s_mix * gpm
        red_ref[1:2, :] += s_mix * gt1
        dmh = dy1 * mix_gain
        dmix = (rm *(dmh - mh * rowmean(dmh * mh))).astype(BF16)
        dmix_ref[...] = dmix
        dlr_ref[...] = _dot_nt(dmix, wout_v[0:half, :])
        dln_ref[...] = _dot_nt(dmix, wout_v[half:, :])
        red_ref[7:8, :] += jnp.zeros((1, D), F32) + loss

    def tok(w):
        return pl.BlockSpec((None, tm, w), lambda b, t: (b, t, 0))

    def mod(k):
        return pl.BlockSpec((None, None, 1, D), lambda b, t, k=k: (b, k, 0, 0))

    def vec():
        return pl.BlockSpec((1, D), lambda b, t: (0, 0))

    return pl.pallas_call(
        body, name="dense_core", grid=(B, nt),
        out_shape=(jax.ShapeDtypeStruct((B, N, D), F32), jax.ShapeDtypeStruct((B, N, half), F32),
                   jax.ShapeDtypeStruct((B, N, half), F32), jax.ShapeDtypeStruct((B, N, D), BF16),
                   jax.ShapeDtypeStruct((B, N, D), BF16), jax.ShapeDtypeStruct((B, N, F), BF16),
                   jax.ShapeDtypeStruct((B, N, F), BF16), jax.ShapeDtypeStruct((B, N, D), BF16),
                   jax.ShapeDtypeStruct((B, SUBLANES, D), F32),
                   jax.ShapeDtypeStruct(w_out.shape, w_out.dtype), jax.ShapeDtypeStruct(w1.shape, w1.dtype),
                   jax.ShapeDtypeStruct(w2.shape, w2.dtype)),
        in_specs=[tok(half), tok(half), tok(D), tok(D), mod(2), mod(3), mod(4), mod(5), vec(), vec(), vec(),
                  _any(), _any(), _any()],
        out_specs=(tok(D), tok(half), tok(half), tok(D), tok(D), tok(F), tok(F), tok(D),
                   pl.BlockSpec((None, SUBLANES, D), lambda b, t: (b, 0, 0)), _any(), _any(), _any()),
        scratch_shapes=[pltpu.VMEM((mixw, D), BF16), pltpu.VMEM((D, F), BF16), pltpu.VMEM((F, D), BF16),
                        pltpu.VMEM((tm, F), F32), pltpu.SemaphoreType.DMA((3,)),
                        pltpu.SemaphoreType.DMA((3, 3)), pltpu.SemaphoreType.DMA((3, 3))],
        input_output_aliases={11: 9, 12: 10, 13: 11},
        compiler_params=_params("arbitrary", "arbitrary"),
    )(lat_ret, lat_na, x, tgt, modl, modl, modl, modl, g_post_mix, g_pre_mlp, g_post_mlp, w_out, w1, w2)[:9]


def _inproj_bwd(dret, dna, x, ctx, dy1, modl, g1, w_in_t, after):
    B, N, D = x.shape
    n_ctx = ctx.shape[1]
    T = n_ctx + N
    tm = _div_tile(n_ctx, 256, 16)
    nct, ctx_spec, lat_spec = _token_tiles(n_ctx, tm)
    nt = T // tm
    nseg_r = dret.shape[1]
    nseg_n = dna.shape[1]
    nw = w_in_t.shape[0]

    def body(*refs):
        seg_refs = refs[:nseg_r + nseg_n]
        c_ref, x_ref, dy1_ref, sc_ref, g_ref, w_ref, dx_ref, red_ref = refs[nseg_r + nseg_n:]
        t = pl.program_id(1)
        dh = jnp.zeros((tm, D), F32)
        for s, ref in enumerate(seg_refs):
            dh = dh + _dot(ref[...], w_ref[s * SEG:(s + 1) * SEG, :])
        x = jnp.where(t < nct, c_ref[...], x_ref[...])
        g = g_ref[...]
        r = lax.rsqrt(jnp.mean(x * x, axis=-1, keepdims=True) + NORM_EPS)
        xh = x * r
        gain = 1.0 + sc_ref[...]
        s_in = jnp.sum(dh * xh, axis=0, keepdims=True)
        red_ref[0:1, :] = jnp.sum(dh, axis=0, keepdims=True)
        red_ref[1:2, :] = s_in * g
        red_ref[2:3, :] = s_in * gain
        red_ref[3:, :] = jnp.zeros((SUBLANES - 3, D), F32)
        dxh = dh * (g * gain)
        dx = r * (dxh - xh * jnp.mean(dxh * xh, axis=-1, keepdims=True))
        dx_ref[...] = dx + jnp.where(t >= nct, dy1_ref[...], 0.0)

    def mrow(b, t):
        return jnp.where(t < nct, B, b)

    def seg(s):
        return pl.BlockSpec((None, None, tm, SEG), lambda b, t, s=s: (b, s, t, 0))

    return _grid_call(
        body, name="inproj_bwd", grid=(B, nt),
        out_shape=(jax.ShapeDtypeStruct((B, N, D), F32), jax.ShapeDtypeStruct((B, nt, SUBLANES, D), F32)),
        in_specs=[seg(s) for s in range(nseg_r)] + [seg(s) for s in range(nseg_n)]
                 + [ctx_spec(D), lat_spec(D), lat_spec(D),
                    pl.BlockSpec((None, None, 1, D), lambda b, t: (mrow(b, t), 1, 0, 0)),
                    pl.BlockSpec((1, D), lambda b, t: (0, 0)),
                    pl.BlockSpec((nw, D), lambda b, t: (0, 0))],
        out_specs=(lat_spec(D), pl.BlockSpec((None, None, SUBLANES, D), lambda b, t: (b, t, 0, 0))),
        scratch_shapes=[], args=(*([dret] * nseg_r), *([dna] * nseg_n), ctx, x, dy1, modl, g1, w_in_t), after=after)


def _tn_matmul(lhs, rhs, name, rows_before=0, rows_after=0, into=None):
    B, S, T, W = lhs.shape
    nn = rhs.shape[-1]
    cap = 1024
    tk = _div_tile(T, 2304, LANES)
    bm = _div_tile(W, cap, LANES)
    bn = _div_tile(nn, cap, LANES)
    nkt = T // tk
    nk = B * nkt
    nwb = W // bm
    sg = 2 if (nwb == 1 and S % 2 == 0 and 2 * bm <= cap and rows_before % (2 * bm) == 0) else 1
    mo = sg * bm

    def body(l_ref, r_ref, *rest):
        o_ref, acc = rest[-2:]
        k, s = pl.program_id(2), pl.program_id(3)

        def part():
            l = l_ref[...] if sg == 1 else jnp.concatenate([l_ref[g] for g in range(sg)], axis=1)
            return _dot_tn(l.astype(BF16), r_ref[...].astype(BF16))

        @pl.when(k == 0)
        def _():
            acc[s] = part()

        @pl.when(k > 0)
        def _():
            acc[s] += part()

        @pl.when(k == nk - 1)
        def _():
            o_ref[...] = acc[s].astype(BF16)

    first = rows_before // mo

    def out_block(i, j, k, s):
        return first + jnp.where(k == nk - 1, s, 0) * nwb + i, j
    return pl.pallas_call(
        functools.partial(body), name=name, grid=(nwb, nn // bn, nk, S // sg),
        out_shape=jax.ShapeDtypeStruct((rows_before + S * W + rows_after, nn), BF16),
        in_specs=[pl.BlockSpec((None, None if sg == 1 else sg, tk, bm), lambda i, j, k, s: (k // nkt, s, k % nkt, i)),
                  pl.BlockSpec((None, tk, bn), lambda i, j, k, s: (k // nkt, k % nkt, j))]
                 + ([] if into is None else [_any()]),
        out_specs=pl.BlockSpec((mo, bn), out_block),
        scratch_shapes=[pltpu.VMEM((S // sg, mo, bn), F32)],
        input_output_aliases={} if into is None else {2: 0},
        compiler_params=_params("parallel", "parallel", "arbitrary", "arbitrary"),
    )(lhs, rhs, *([] if into is None else [into]))


class _SplitScatter:
    def __init__(self, gs, block_ofs, land_shapes, name, kind="scatter", masks=ALL_PEERS):
        self.n = n = len(gs)
        self.block_ofs, self.kind, self.masks = block_ofs, kind, masks
        if kind == "scatter":
            land_shapes = [(N_DEV,) + tuple(bs) for bs in land_shapes]
        hbm = pl.BlockSpec(memory_space=pltpu.HBM)
        sem = pl.BlockSpec(memory_space=pltpu.SEMAPHORE)

        def body(*refs):
            g_refs, land_refs = refs[:n], refs[n:2 * n]
            send_sems, recv_sems, own_sems = refs[2 * n:2 * n + 3]
            token = refs[-1]
            for own, pushes in self._copies(g_refs, land_refs, send_sems, recv_sems, own_sems, landing="sender"):
                own.start()
                for cp in pushes:
                    cp.start()
            token[...] = jnp.zeros_like(token)

        outs = pl.pallas_call(
            body, name=name,
            out_shape=(pltpu.SemaphoreType.DMA((n * (N_DEV - 1),)), pltpu.SemaphoreType.DMA((n * (N_DEV - 1),)),
                       pltpu.SemaphoreType.DMA((n,)))
                      + tuple(pltpu.HBM(g.shape, g.dtype) for g in gs)
                      + tuple(pltpu.HBM(s, g.dtype) for s, g in zip(land_shapes, gs))
                      + (jax.ShapeDtypeStruct((SUBLANES, LANES), F32),),
            in_specs=(hbm,) * (2 * n), out_specs=(sem,) * 3 + (hbm,) * (2 * n) + (_vmem(),),
            input_output_aliases={k: 3 + k for k in range(2 * n)},
            compiler_params=pltpu.CompilerParams(has_side_effects=pltpu.SideEffectType.DATAFLOW_SIDE_EFFECTING),
        )(*[pltpu.with_memory_space_constraint(g, pltpu.HBM) for g in gs],
          *[pltpu.with_memory_space_constraint(lax.empty(s, g.dtype), pltpu.HBM) for s, g in zip(land_shapes, gs)])
        self.sems, self.thru, self.token = outs[:3], outs[3:3 + 2 * n], outs[-1]

    def _copies(self, g_refs, land_refs, send_sems, recv_sems, own_sems, landing):
        me, peers = _me_and_peers()
        out = []
        for k in range(self.n):
            if self.kind == "scatter":
                src, dst = self.block_ofs[k](g_refs[k]), _slot(land_refs[k])
            else:
                src, dst = (lambda p, k=k: g_refs[k]), self.block_ofs[k](land_refs[k])
            own = pltpu.make_async_copy(src(me), dst(me), own_sems.at[k]) if landing == "sender" else None
            pushes = []
            for m in self.masks:
                dev, pid = peers[m - 1]
                i = k * (N_DEV - 1) + m - 1
                pushes.append(_remote(src(pid), dst(me if landing == "sender" else pid),
                                      send_sems.at[i], recv_sems.at[i], dev))
            out.append((own, pushes))
        return out


def _scatter_wait(scatters, after, name):
    hbm = pl.BlockSpec(memory_space=pltpu.HBM)
    sem = pl.BlockSpec(memory_space=pltpu.SEMAPHORE)
    n_arr = [2 * sc.n for sc in scatters]
    total = sum(n_arr)

    def body(*refs):
        arrs, sems = refs[:total], refs[total:total + 3 * len(scatters)]
        a0 = 0
        for j, sc in enumerate(scatters):
            g_refs, land_refs = arrs[a0:a0 + sc.n], arrs[a0 + sc.n:a0 + 2 * sc.n]
            a0 += 2 * sc.n
            send_sems, recv_sems, own_sems = sems[3 * j:3 * j + 3]
            for (own, sent), (_, got) in zip(sc._copies(g_refs, land_refs, send_sems, recv_sems, own_sems, "sender"),
                                             sc._copies(g_refs, land_refs, send_sems, recv_sems, own_sems, "receiver")):
                own.wait()
                for cp in sent:
                    cp.wait_send()
                for cp in got:
                    cp.wait_recv()

    operands = [a for sc in scatters for a in sc.thru]
    outs = pl.pallas_call(
        body, name=name,
        out_shape=tuple(pltpu.HBM(a.shape, a.dtype) for a in operands),
        in_specs=(hbm,) * total + (sem,) * (3 * len(scatters)) + (pl.BlockSpec(memory_space=pl.ANY),),
        out_specs=(hbm,) * total, input_output_aliases={k: k for k in range(total)},
        compiler_params=pltpu.CompilerParams(has_side_effects=pltpu.SideEffectType.DATAFLOW_SIDE_EFFECTING),
    )(*operands, *[s for sc in scatters for s in sc.sems], after)
    lands, a0 = [], 0
    for sc in scatters:
        lands.extend(outs[a0 + sc.n:a0 + 2 * sc.n])
        a0 += 2 * sc.n
    return lands


def _small_ar(mbuf, silu_all, w_ada, c_ctx, n_mod_rows, n_vec_rows):
    D = silu_all.shape[1]
    ncol = w_ada.shape[1]
    nm = mbuf.shape[2]
    srows = silu_all.shape[0]

    def body(mbuf, s_ref, w_ref, cc_ref, tot_ref, gb_ref, gw_ref, gc_ref, tbuf, dmx, cmrow, send3, recv3):
        me, _ = _me_and_peers()
        msum = mbuf[0]
        for k in range(1, N_DEV):
            msum = msum + mbuf[k]
        tot_ref[...] = msum[n_mod_rows:n_mod_rows + n_vec_rows]
        gb_ref[...] = jnp.sum(msum[0:n_mod_rows], axis=0, keepdims=True)
        loc = pl.ds(pl.multiple_of(me * ncol, ncol), ncol)
        for k in range(N_DEV):
            dmx[k * SUBLANES:(k + 1) * SUBLANES, :] = mbuf[k, :, loc]
        cmrow[...] = msum
        cm_loc = cmrow[n_mod_rows - 1:n_mod_rows, loc]
        dmx[N_DEV * SUBLANES:, :] = jnp.concatenate([cm_loc, jnp.zeros((SUBLANES - 1, ncol), F32)], axis=0)
        gw_ref[...] = _dot_tn(s_ref[...], dmx[...])
        tbuf[me] = _dot_nt(dmx[N_DEV * SUBLANES:, :], w_ref[...])
        _exchange(lambda p: tbuf.at[me], lambda p: tbuf.at[p], send3, recv3)
        tsum = tbuf[0]
        for k in range(1, N_DEV):
            tsum = tsum + tbuf[k]
        cc = cc_ref[...]
        sg = _sigmoid(cc)
        gc_ref[...] = tsum[0:1, :] * (sg * (1.0 + cc * (1.0 - sg)))

    return pl.pallas_call(
        body, name="small_ar",
        out_shape=(jax.ShapeDtypeStruct((n_vec_rows, nm), F32), jax.ShapeDtypeStruct((1, nm), F32),
                   jax.ShapeDtypeStruct((D, ncol), F32), jax.ShapeDtypeStruct((1, D), F32)),
        in_specs=[_vmem()] * 4, out_specs=(_vmem(),) * 4,
        scratch_shapes=[pltpu.VMEM((N_DEV, SUBLANES, D), F32), pltpu.VMEM((srows, ncol), F32),
                        pltpu.VMEM((SUBLANES, nm), F32)] + [pltpu.SemaphoreType.DMA((N_DEV - 1,))] * 2,
        compiler_params=pltpu.CompilerParams(vmem_limit_bytes=VMEM_LIMIT),
    )(mbuf, silu_all, w_ada, c_ctx.reshape(1, D))


def _adam_update(w, g, m, v):
    mn = ADAM_B1 * m + (1.0 - ADAM_B1) * g
    vn = ADAM_B2 * v + (1.0 - ADAM_B2) * (g * g)
    m_hat = mn / (1.0 - ADAM_B1 ** ADAM_STEP)
    v_hat = vn / (1.0 - ADAM_B2 ** ADAM_STEP)
    return -ADAM_LR * (m_hat / (jnp.sqrt(v_hat) + ADAM_EPS) + ADAM_WD * w), mn, vn


def _adamw(w, g, m, v, name):
    rows, cols = w.shape
    tr = _div_tile(rows, 512, SUBLANES)

    def body(w_ref, g_ref, m_ref, v_ref, d_ref, nm_ref, nv_ref):
        d_ref[...], nm_ref[...], nv_ref[...] = _adam_update(w_ref[...], g_ref[...], m_ref[...], v_ref[...])

    spec = pl.BlockSpec((tr, cols), lambda i: (i, 0))
    return pl.pallas_call(
        functools.partial(body), name=name, grid=(rows // tr,),
        out_shape=(jax.ShapeDtypeStruct((rows, cols), F32),) * 3,
        in_specs=[spec] * 4, out_specs=(spec,) * 3,
        compiler_params=_params("parallel"),
    )(w, g, m, v)


def _adamw_small(items, name):
    n = len(items)

    def body(*refs):
        ins, outs = refs[:4 * n], refs[4 * n:]
        for i in range(n):
            w_ref, g_ref, m_ref, v_ref = ins[4 * i:4 * i + 4]
            outs[3 * i][...], outs[3 * i + 1][...], outs[3 * i + 2][...] = _adam_update(
                w_ref[...], g_ref[...], m_ref[...], v_ref[...])

    outs = pl.pallas_call(
        body, name=name,
        out_shape=tuple(jax.ShapeDtypeStruct(it[0].shape, F32) for it in items for _ in range(3)),
        in_specs=[_vmem()] * (4 * n), out_specs=(_vmem(),) * (3 * n),
        compiler_params=pltpu.CompilerParams(vmem_limit_bytes=VMEM_LIMIT),
    )(*[a for it in items for a in it])
    return [tuple(outs[3 * i:3 * i + 3]) for i in range(n)]


def _sum_adamw(buf, w, m, v, name):
    _, rows, cols = buf.shape
    tr = _div_tile(rows, 256, 2 * SUBLANES)

    def body(b_ref, w_ref, m_ref, v_ref, g_ref, d_ref, nm_ref, nv_ref):
        g = b_ref[0].astype(F32)
        for k in range(1, N_DEV):
            g = g + b_ref[k].astype(F32)
        g_ref[...] = g
        d_ref[...], nm_ref[...], nv_ref[...] = _adam_update(w_ref[...], g, m_ref[...], v_ref[...])

    spec = pl.BlockSpec((tr, cols), lambda i: (i, 0))
    return pl.pallas_call(
        functools.partial(body), name=name, grid=(rows // tr,),
        out_shape=(jax.ShapeDtypeStruct((rows, cols), F32),) * 4,
        in_specs=[pl.BlockSpec((N_DEV, tr, cols), lambda i: (0, i, 0))] + [spec] * 3, out_specs=(spec,) * 4,
        compiler_params=_params("parallel"),
    )(buf, w, m, v)


def _rope_tables(n_ctx, n):
    n_freq = RET_DIM // 4
    inv = np.float32(ROPE_BASE) ** (-np.arange(n_freq, dtype=np.float32) / np.float32(n_freq))
    tok = np.arange(n)
    pos_r = (tok // GRID_W).astype(np.float32)
    pos_c = (tok % GRID_W).astype(np.float32)
    ang_r = (pos_r[:, None] * inv[None, :]).astype(np.float32)
    ang_c = (pos_c[:, None] * inv[None, :]).astype(np.float32)
    cos = np.concatenate([np.cos(ang_r), np.cos(ang_r), np.cos(ang_c), np.cos(ang_c)], axis=-1)
    sin = np.concatenate([-np.sin(ang_r), np.sin(ang_r), -np.sin(ang_c), np.sin(ang_c)], axis=-1)
    cos = np.concatenate([np.ones((n_ctx, RET_DIM), np.float32), cos], axis=0)
    sin = np.concatenate([np.zeros((n_ctx, RET_DIM), np.float32), sin], axis=0)
    return jnp.asarray(cos, F32), jnp.asarray(sin, F32)


def _na_tables():
    q = np.arange(GRID_W)[:, None]
    k = np.arange(GRID_W)[None, :]
    start = np.clip(q - NA_KW // 2, 0, GRID_W - NA_KW)
    valid = (k >= start) & (k < start + NA_KW)
    dc = np.clip(k - q + (NA_KW - 1), 0, 2 * NA_KW - 2)
    ncls = 2 * NA_KW - 1
    onehot = (dc[None] == np.arange(ncls)[:, None, None]) & valid[None]
    return onehot.astype(np.float32), valid


def _paired_bias(rpb, onehot, valid):
    ncls = onehot.shape[0]
    pair = np.zeros((2 * ncls, GRID_W, LANES), np.float32)
    pair[:ncls, :, :GRID_W] = onehot
    pair[ncls:, :, GRID_W:] = onehot
    rows = jnp.concatenate([rpb[:, :-1], rpb[:, 1:]], axis=-1)
    t = jnp.einsum("hdc,cqk->hdqk", rows, jnp.asarray(pair), precision=lax.Precision.HIGHEST)
    return jnp.where(jnp.asarray(np.tile(valid, (1, 2)))[None, None], t, NEG_INF)


def kernel(x, c, ctx, c_ctx, w_ada, b_ada, g_pre_mix, g_post_mix, g_pre_mlp, g_post_mlp, w_in, ret_decay, ret_gn, na_rpb, w_out, w_mlp1, w_mlp2, loss_target, m_c_ctx, m_w_ada, m_b_ada, m_g_pre_mix, m_g_post_mix, m_g_pre_mlp, m_g_post_mlp, m_w_in, m_ret_decay, m_ret_gn, m_na_rpb, m_w_out, m_w_mlp1, m_w_mlp2, v_c_ctx, v_w_ada, v_b_ada, v_g_pre_mix, v_g_post_mix, v_g_pre_mlp, v_g_post_mlp, v_w_in, v_ret_decay, v_ret_gn, v_na_rpb, v_w_out, v_w_mlp1, v_w_mlp2):
    B, N, D = x.shape
    C = ctx.shape[1]
    T = C + N

    silu_all, mods_g, win_b, wout_l, w1_l, w2_l = _mod_gather(c, c_ctx, w_ada[0], b_ada, w_in[0].T, w_out[0],
                                                             w_mlp1[0], w_mlp2[0])
    mods_mine = mods_g.transpose(1, 0, 2).reshape(mods_g.shape[1], N_MOD * D)
    modl = jnp.concatenate([mods_mine[:B], mods_mine[SUBLANES:SUBLANES + 1]], axis=0)
    modl = modl.reshape(B + 1, N_MOD, 1, D)
    rin = w_in.shape[2]
    rout, c1, r2 = wout_l.shape[0], w1_l.shape[1], w2_l.shape[0]

    def rows_of(n):
        return lambda ref: _row_block(ref, n)

    def cols_of(n):
        return lambda ref: _col_block(ref, n)

    cos, sin = _rope_tables(C, N)
    onehot, valid = _na_tables()
    bias2 = _paired_bias(na_rpb[0], onehot, valid)
    lg = jax.nn.log_sigmoid(ret_decay[0].astype(F32))

    ag = _SplitScatter([wout_l, w1_l, w2_l], [rows_of(rout), cols_of(c1), rows_of(r2)],
                       [(N_DEV * rout, D), (D, N_DEV * c1), (N_DEV * r2, D)], "ag_mlp_start",
                       kind="gather", masks=SIBLING + ICI_SAME_CORE)
    h_all, proj = _inproj_fwd(x, ctx, modl, g_pre_mix, win_b, after=ag.token)
    o_ret, lat_ret, q_rot, k_rot = _ret_fwd(proj, cos, sin, lg, ret_gn, C)
    lat_na, na_probs = _na_fwd(proj, bias2, C)
    wout_part, w1_part, w2_part = _scatter_wait([ag], lat_na, "ag_mlp_wait")

    (dy1, dlat_ret, dlat_na, dmix, h2, act, du, dz, red_d) = _dense_core(
        lat_ret, lat_na, x, loss_target, modl, g_post_mix, g_pre_mlp, g_post_mlp, wout_part, w1_part, w2_part)

    gw_out_p = _tn_matmul(lat_ret[:, None], dmix, "gw_out_ret", rows_after=lat_na.shape[-1])
    gw_out_p = _tn_matmul(lat_na[:, None], dmix, "gw_out_na", rows_before=lat_ret.shape[-1], into=gw_out_p)
    gw1_p = _tn_matmul(h2[:, None], du, "gw_mlp1")
    gw2_p = _tn_matmul(act[:, None], dz, "gw_mlp2")
    rs_mlp = _SplitScatter([gw_out_p, gw1_p, gw2_p], [rows_of(rout), cols_of(c1), rows_of(r2)],
                           [(rout, D), (D, c1), (r2, D)], "rs_mlp_start")

    dret, dgn_p, dlg_p = _ret_bwd(proj, q_rot, k_rot, cos, sin, lg, ret_gn, o_ret, dlat_ret, C, after=rs_mlp.token)
    dna, rr = _na_bwd(proj, na_probs, dlat_na, C)
    ret_cols, na_cols = dret.shape[1] * dret.shape[3], dna.shape[1] * dna.shape[3]
    gwin_t_p = _tn_matmul(dret, h_all, "gw_in_ret", rows_after=na_cols)
    gwin_t_p = _tn_matmul(dna, h_all, "gw_in_na", rows_before=ret_cols, into=gwin_t_p)
    rs_in = _SplitScatter([gwin_t_p], [rows_of(rin)], [(rin, D)], "rs_w_in_start")
    grad_x, red_i = _inproj_bwd(dret, dna, x, ctx, dy1, modl, g_pre_mix, win_b, after=rs_in.token)

    rd = red_d
    nct = red_i.shape[1] * C // T
    ri_ctx = red_i[:, :nct].sum(axis=(0, 1))
    ri_lat = red_i[:, nct:].sum(axis=1)
    d_mods = jnp.concatenate([ri_lat[:, 0], ri_lat[:, 1], rd[:, 0], rd[:, 4], rd[:, 3], rd[:, 2]], axis=-1)
    d_cmods = jnp.concatenate([ri_ctx[0], ri_ctx[1], jnp.zeros(((N_MOD - 2) * D,), F32)])[None]
    dg_pre_mix = ri_lat[:, 2].sum(axis=0) + ri_ctx[2]
    dg_post_mix = rd[:, 1].sum(axis=0)
    dg_pre_mlp = rd[:, 5].sum(axis=0)
    dg_post_mlp = rd[:, 6].sum(axis=0)
    loss_p = rd[:, 7, 0].sum()
    d_gn = dgn_p[:, 0].sum(axis=0)
    d_lg = dlg_p[:, :, :2, 0].sum(axis=0).T
    d_decay = d_lg * jax.nn.sigmoid(-ret_decay[0].astype(F32))
    ncls = 2 * NA_KW - 1
    d_rpb = (jnp.pad(rr[:, :, :ncls], ((0, 0), (0, 1), (0, 0)))
             + jnp.pad(rr[:, :, GRID_W:GRID_W + ncls], ((0, 0), (1, 0), (0, 0))))
    d_rpb32 = jnp.pad(d_rpb, ((0, 0), (0, 0), (0, 32 - ncls)))
    pieces = [dg_pre_mix, dg_post_mix, dg_pre_mlp, dg_post_mlp, d_gn, d_rpb32.reshape(-1),
              jnp.pad(d_decay.reshape(-1), (0, LANES - d_decay.size)), jnp.full((LANES,), loss_p, F32)]
    vec = jnp.concatenate(pieces)
    nm = N_MOD * D
    n_vec_rows = -(-vec.shape[0] // nm)
    assert B + 1 + n_vec_rows <= SUBLANES
    vec = jnp.pad(vec, (0, n_vec_rows * nm - vec.shape[0])).reshape(n_vec_rows, nm)
    dm_slot = jnp.concatenate([d_mods, d_cmods, vec, jnp.zeros((SUBLANES - B - 1 - n_vec_rows, nm), F32)], axis=0)
    def whole(ref):
        return lambda p: ref

    small = _SplitScatter([dm_slot], [whole], [dm_slot.shape], "small_start")
    land_out, land_1, land_2 = _scatter_wait([rs_mlp], small.token, "rs_mlp_wait")
    fused = {"w_out": _sum_adamw(land_out, w_out[0], m_w_out[0], v_w_out[0], "sum_adamw_w_out"),
             "w_mlp1": _sum_adamw(land_1, w_mlp1[0], m_w_mlp1[0], v_w_mlp1[0], "sum_adamw_w_mlp1"),
             "w_mlp2": _sum_adamw(land_2, w_mlp2[0], m_w_mlp2[0], v_w_mlp2[0], "sum_adamw_w_mlp2")}
    (land_in,) = _scatter_wait([rs_in], fused["w_mlp2"][0], "rs_w_in_wait")
    win_upd = _sum_adamw(land_in, w_in[0].T, m_w_in[0].T, v_w_in[0].T, "sum_adamw_w_in")
    fused["w_in"] = [a.T for a in win_upd]
    (mbuf,) = _scatter_wait([small], win_upd[0], "small_wait")
    tot, g_b_ada, g_w_ada, g_c_ctx = _small_ar(mbuf, silu_all, w_ada[0], c_ctx, B + 1, n_vec_rows)
    flat = tot.reshape(-1)
    o0 = 0
    g_pre_mix_g = flat[o0:o0 + D]; o0 += D
    g_post_mix_g = flat[o0:o0 + D]; o0 += D
    g_pre_mlp_g = flat[o0:o0 + D]; o0 += D
    g_post_mlp_g = flat[o0:o0 + D]; o0 += D
    g_gn = flat[o0:o0 + RET_WIDTH]; o0 += RET_WIDTH
    nrpb = NA_HEADS * (2 * NA_KH - 1) * 32
    g_rpb = flat[o0:o0 + nrpb].reshape(NA_HEADS, 2 * NA_KH - 1, 32)[:, :, :ncls]; o0 += nrpb
    g_decay = flat[o0:o0 + 2 * RET_HEADS].reshape(2, RET_HEADS); o0 += LANES
    loss = flat[o0]

    grads = {
        "c_ctx": g_c_ctx.reshape(c_ctx.shape), "w_ada": g_w_ada[None], "b_ada": g_b_ada.reshape(b_ada.shape),
        "g_pre_mix": g_pre_mix_g[None], "g_post_mix": g_post_mix_g[None], "g_pre_mlp": g_pre_mlp_g[None],
        "g_post_mlp": g_post_mlp_g[None], "w_in": fused["w_in"][0][None], "ret_decay": g_decay[None], "ret_gn": g_gn[None],
        "na_rpb": g_rpb[None], "w_out": fused["w_out"][0][None], "w_mlp1": fused["w_mlp1"][0][None],
        "w_mlp2": fused["w_mlp2"][0][None],
    }
    weights = dict(c_ctx=c_ctx, w_ada=w_ada, b_ada=b_ada, g_pre_mix=g_pre_mix, g_post_mix=g_post_mix,
                   g_pre_mlp=g_pre_mlp, g_post_mlp=g_post_mlp, w_in=w_in, ret_decay=ret_decay, ret_gn=ret_gn,
                   na_rpb=na_rpb, w_out=w_out, w_mlp1=w_mlp1, w_mlp2=w_mlp2)
    m_in = dict(c_ctx=m_c_ctx, w_ada=m_w_ada, b_ada=m_b_ada, g_pre_mix=m_g_pre_mix, g_post_mix=m_g_post_mix,
                g_pre_mlp=m_g_pre_mlp, g_post_mlp=m_g_post_mlp, w_in=m_w_in, ret_decay=m_ret_decay,
                ret_gn=m_ret_gn, na_rpb=m_na_rpb, w_out=m_w_out, w_mlp1=m_w_mlp1, w_mlp2=m_w_mlp2)
    v_in = dict(c_ctx=v_c_ctx, w_ada=v_w_ada, b_ada=v_b_ada, g_pre_mix=v_g_pre_mix, g_post_mix=v_g_post_mix,
                g_pre_mlp=v_g_pre_mlp, g_post_mlp=v_g_post_mlp, w_in=v_w_in, ret_decay=v_ret_decay,
                ret_gn=v_ret_gn, na_rpb=v_na_rpb, w_out=v_w_out, w_mlp1=v_w_mlp1, w_mlp2=v_w_mlp2)
    names = list(weights)
    deltas, new_m, new_v = {}, {}, {}
    def as_2d(n):
        shp = weights[n].shape
        two_d = (-1, shp[-1]) if len(shp) > 1 else (1, shp[0])
        return [a.reshape(two_d) for a in (weights[n], grads[n], m_in[n], v_in[n])]

    small = [n for n in names if n not in fused and weights[n].size <= 65536]
    updated = dict(zip(small, _adamw_small([as_2d(n) for n in small], "adamw_small")))
    for n in names:
        if n in fused:
            updated[n] = fused[n][1:]
        elif n not in updated:
            updated[n] = _adamw(*as_2d(n), "adamw_" + n)
        deltas[n], new_m[n], new_v[n] = (a.reshape(weights[n].shape) for a in updated[n])
    return (loss, grad_x, *[grads[n] for n in names], *[deltas[n] for n in names],
            *[new_m[n] for n in names], *[new_v[n] for n in names])
```

```python
import functools

import numpy as np
import jax
import jax.numpy as jnp
from jax import lax
from jax.experimental import pallas as pl
from jax.experimental.pallas import tpu as pltpu

F32 = jnp.float32
BF16 = jnp.bfloat16
MESH = pl.DeviceIdType.MESH

N_DEV = 8
LANES = 128
SUBLANES = 8
VMEM_LIMIT = 60 * 1024 * 1024

GRID_W = 64
RET_HEADS = 4
RET_DIM = 128
RET_WIDTH = RET_HEADS * RET_DIM
NA_HEADS = 8
NA_DIM = 64
NA_WIDTH = NA_HEADS * NA_DIM
NA_PAIRS = NA_HEADS // 2
NA_KH = 8
NA_KW = 16
NA_GROUP = 8
SEG = 512
ROPE_BASE = 10000.0
NORM_EPS = 1e-6
NEG_INF = -1e30
N_MOD = 6

ADAM_LR = 0.001
ADAM_B1 = 0.9
ADAM_B2 = 0.999
ADAM_EPS = 1e-08
ADAM_WD = 0.01
ADAM_STEP = 10


def _dot(a, b):
    return lax.dot_general(a, b, (((1,), (0,)), ((), ())), preferred_element_type=F32)


def _dot_nt(a, b):
    return lax.dot_general(a, b, (((1,), (1,)), ((), ())), preferred_element_type=F32)


def _dot_tn(a, b):
    return lax.dot_general(a, b, (((0,), (0,)), ((), ())), preferred_element_type=F32)


def _sigmoid(x):
    return 1.0 / (1.0 + jnp.exp(-x))


def _div_tile(n, cap, mult):
    if n <= cap:
        return n
    for t in range(cap - cap % mult, 0, -mult):
        if n % t == 0:
            return t
    raise ValueError(f"no tile for {n}")


def _params(*sem):
    return pltpu.CompilerParams(dimension_semantics=tuple(sem) if sem else None,
                                vmem_limit_bytes=VMEM_LIMIT)


def _vmem():
    return pl.BlockSpec(memory_space=pltpu.VMEM)


def _any():
    return pl.BlockSpec(memory_space=pl.ANY)


def _me_and_peers():
    x, y, c = lax.axis_index("x"), lax.axis_index("y"), lax.axis_index("c")
    me = 4 * x + 2 * y + c
    peers = []
    for m in range(1, N_DEV):
        px = 1 - x if (m >> 2) & 1 else x
        py = 1 - y if (m >> 1) & 1 else y
        pc = 1 - c if m & 1 else c
        peers.append(((px, py, pc), 4 * px + 2 * py + pc))
    return me, peers


def _exchange(src_for, dst_from, send_sems, recv_sems):
    me, peers = _me_and_peers()
    sent = []
    for i, (dev, pid) in enumerate(peers):
        cp = pltpu.make_async_remote_copy(src_ref=src_for(pid), dst_ref=dst_from(me),
                                          send_sem=send_sems.at[i], recv_sem=recv_sems.at[i],
                                          device_id=dev, device_id_type=MESH)
        cp.start()
        sent.append(cp)
    for i, (dev, pid) in enumerate(peers):
        pltpu.make_async_remote_copy(src_ref=src_for(pid), dst_ref=dst_from(pid),
                                     send_sem=send_sems.at[i], recv_sem=recv_sems.at[i],
                                     device_id=dev, device_id_type=MESH).wait_recv()
    for cp in sent:
        cp.wait_send()


SIBLING = (1,)
ICI_SAME_CORE = (2, 4, 6)
ALL_PEERS = tuple(range(1, N_DEV))


def _remote(src, dst, send_sem, recv_sem, dev):
    return pltpu.make_async_remote_copy(src_ref=src, dst_ref=dst, send_sem=send_sem, recv_sem=recv_sem,
                                        device_id=dev, device_id_type=MESH)


def _push_start(items, masks, send_sems, recv_sems):
    me, peers = _me_and_peers()
    for k, (src_for, dst_from) in enumerate(items):
        for m in masks:
            dev, pid = peers[m - 1]
            _remote(src_for(pid), dst_from(me), send_sems.at[k, m - 1], recv_sems.at[k, m - 1], dev).start()


def _push_wait_recv(items, masks, send_sems, recv_sems):
    me, peers = _me_and_peers()
    for k, (src_for, dst_from) in enumerate(items):
        for m in masks:
            dev, pid = peers[m - 1]
            _remote(src_for(pid), dst_from(pid), send_sems.at[k, m - 1], recv_sems.at[k, m - 1], dev).wait_recv()


def _push_wait_send(items, masks, send_sems, recv_sems):
    me, peers = _me_and_peers()
    for k, (src_for, dst_from) in enumerate(items):
        for m in masks:
            dev, pid = peers[m - 1]
            _remote(src_for(pid), dst_from(me), send_sems.at[k, m - 1], recv_sems.at[k, m - 1], dev).wait_send()


def _forward_start(items, send_sems, recv_sems):
    me, peers = _me_and_peers()
    sib = peers[0][0]
    for k, (blk_in, blk_out) in enumerate(items):
        for j, m in enumerate(ICI_SAME_CORE):
            pid = peers[m - 1][1]
            _remote(blk_in(pid), blk_out(pid), send_sems.at[k, j], recv_sems.at[k, j], sib).start()


def _forward_wait(items, send_sems, recv_sems):
    me, peers = _me_and_peers()
    sib = peers[0][0]
    for k, (blk_in, blk_out) in enumerate(items):
        for j, m in enumerate(ICI_SAME_CORE):
            got = peers[(m | 1) - 1][1]
            _remote(blk_in(got), blk_out(got), send_sems.at[k, j], recv_sems.at[k, j], sib).wait_recv()
    for k, (blk_in, blk_out) in enumerate(items):
        for j, m in enumerate(ICI_SAME_CORE):
            pid = peers[m - 1][1]
            _remote(blk_in(pid), blk_out(pid), send_sems.at[k, j], recv_sems.at[k, j], sib).wait_send()


def _mod_gather(c, c_ctx, w_ada, b_ada, w_in_t, w_out, w1, w2):
    B, D = c.shape
    ncol = w_ada.shape[1]
    rows = SUBLANES * N_DEV + SUBLANES

    def body(c_ref, cc_ref, w_ref, b_ref, win_ref, wout_ref, w1_ref, w2_ref,
             s_ref, m_ref, gin_ref, wout_b, w1_b, w2_b,
             win_b, msend, send1, recv1, send2, recv2, wsend, wrecv, fsend, frecv, lsem):
        me, _ = _me_and_peers()
        win_b[...] = win_ref[...].astype(BF16)
        block = _row_block(gin_ref, w_in_t.shape[0])
        gather = [(lambda p: win_b, block)]
        own = pltpu.make_async_copy(win_b, block(me), lsem.at[0])
        cv = c_ref[...]
        slot = jnp.concatenate([cv * _sigmoid(cv), jnp.zeros((SUBLANES - B, D), F32)], axis=0)
        my_rows = pl.ds(pl.multiple_of(me * SUBLANES, SUBLANES), SUBLANES)
        s_ref[my_rows, :] = slot
        ccv = cc_ref[...]
        s_ref[SUBLANES * N_DEV:, :] = jnp.concatenate(
            [ccv * _sigmoid(ccv), jnp.zeros((SUBLANES - 1, D), F32)], axis=0)

        def rows_of(p):
            return s_ref.at[pl.ds(pl.multiple_of(p * SUBLANES, SUBLANES), SUBLANES), :]

        _exchange(lambda p: rows_of(me), rows_of, send1, recv1)
        own.start()
        _push_start(gather, SIBLING + ICI_SAME_CORE, wsend, wrecv)
        wout_b[...] = wout_ref[...].astype(BF16)
        w1_b[...] = w1_ref[...].astype(BF16)
        w2_b[...] = w2_ref[...].astype(BF16)
        b_loc = b_ref[:, pl.ds(pl.multiple_of(me * ncol, ncol), ncol)]
        mods = _dot(s_ref[...], w_ref[...]) + b_loc
        for p in range(N_DEV):
            msend[p] = jnp.concatenate([mods[p * SUBLANES:(p + 1) * SUBLANES], mods[N_DEV * SUBLANES:]], axis=0)
        m_ref[me] = msend[me]
        columns = [(lambda p: msend.at[p], lambda p: m_ref.at[p])]
        _push_start(columns, ALL_PEERS, send2, recv2)
        _push_wait_recv(gather, ICI_SAME_CORE, wsend, wrecv)
        relay = [(block, block)]
        _forward_start(relay, fsend, frecv)
        _push_wait_recv(columns, ALL_PEERS, send2, recv2)
        _push_wait_recv(gather, SIBLING, wsend, wrecv)
        _forward_wait(relay, fsend, frecv)
        _push_wait_send(columns, ALL_PEERS, send2, recv2)
        _push_wait_send(gather, SIBLING + ICI_SAME_CORE, wsend, wrecv)
        own.wait()

    return pl.pallas_call(
        body, name="mod_gather",
        out_shape=(jax.ShapeDtypeStruct((rows, D), F32), jax.ShapeDtypeStruct((N_DEV, 2 * SUBLANES, ncol), F32),
                   jax.ShapeDtypeStruct((N_DEV * w_in_t.shape[0], D), BF16),
                   jax.ShapeDtypeStruct(w_out.shape, BF16), jax.ShapeDtypeStruct(w1.shape, BF16),
                   jax.ShapeDtypeStruct(w2.shape, BF16)),
        in_specs=[_vmem()] * 8, out_specs=(_vmem(), _vmem(), _any(), _vmem(), _vmem(), _vmem()),
        scratch_shapes=[pltpu.VMEM(w_in_t.shape, BF16), pltpu.VMEM((N_DEV, 2 * SUBLANES, ncol), F32)]
                       + [pltpu.SemaphoreType.DMA((N_DEV - 1,))] * 2
                       + [pltpu.SemaphoreType.DMA((1, N_DEV - 1))] * 4 + [pltpu.SemaphoreType.DMA((1, 3))] * 2
                       + [pltpu.SemaphoreType.DMA((1,))],
        compiler_params=pltpu.CompilerParams(vmem_limit_bytes=VMEM_LIMIT),
    )(c, c_ctx.reshape(1, D), w_ada, b_ada, w_in_t, w_out, w1, w2)


def _row_block(ref, rows):
    return lambda p: ref.at[pl.ds(pl.multiple_of(p * rows, 2 * SUBLANES), rows), :]


def _col_block(ref, cols):
    return lambda p: ref.at[:, pl.ds(pl.multiple_of(p * cols, LANES), cols)]


def _slot(ref):
    return lambda p: ref.at[p]


def _grid_call(body, *, name, grid, out_shape, in_specs, out_specs, scratch_shapes, args, after=None):
    n_in = len(args)

    def ordered_body(*refs):
        body(*refs[:n_in], *refs[n_in + 1:])

    return pl.pallas_call(
        body if after is None else ordered_body, name=name, grid=grid, out_shape=tuple(out_shape),
        in_specs=list(in_specs) + ([] if after is None else [_any()]), out_specs=tuple(out_specs),
        scratch_shapes=list(scratch_shapes), compiler_params=_params(*(("arbitrary",) * len(grid))),
    )(*args, *([] if after is None else [after]))


def _token_tiles(n_ctx, tm):
    nct = n_ctx // tm

    def ctx_spec(D):
        return pl.BlockSpec((None, tm, D), lambda b, t: (b, jnp.minimum(t, nct - 1), 0))

    def lat_spec(D):
        return pl.BlockSpec((None, tm, D), lambda b, t: (b, jnp.maximum(t - nct, 0), 0))

    return nct, ctx_spec, lat_spec


def _inproj_fwd(x, ctx, modl, g1, w_in_t, after):
    B, N, D = x.shape
    n_ctx = ctx.shape[1]
    T = n_ctx + N
    nw = w_in_t.shape[0]
    tm = _div_tile(n_ctx, 256, 16)
    nct, ctx_spec, lat_spec = _token_tiles(n_ctx, tm)

    def body(c_ref, x_ref, sh_ref, sc_ref, g_ref, w_ref, h_ref, p_ref):
        x = jnp.where(pl.program_id(1) < nct, c_ref[...], x_ref[...])
        r = lax.rsqrt(jnp.mean(x * x, axis=-1, keepdims=True) + NORM_EPS)
        h = ((x * r) * g_ref[...]) * (1.0 + sc_ref[...]) + sh_ref[...]
        hb = h.astype(BF16)
        h_ref[...] = hb
        p_ref[...] = _dot_nt(hb, w_ref[...]).astype(BF16)

    def mrow(b, t):
        return jnp.where(t < nct, B, b)

    return _grid_call(
        body, name="inproj_fwd", grid=(B, T // tm),
        out_shape=(jax.ShapeDtypeStruct((B, T, D), BF16), jax.ShapeDtypeStruct((B, T, nw), BF16)),
        in_specs=[ctx_spec(D), lat_spec(D),
                  pl.BlockSpec((None, None, 1, D), lambda b, t: (mrow(b, t), 0, 0, 0)),
                  pl.BlockSpec((None, None, 1, D), lambda b, t: (mrow(b, t), 1, 0, 0)),
                  pl.BlockSpec((1, D), lambda b, t: (0, 0)),
                  pl.BlockSpec((nw, D), lambda b, t: (0, 0))],
        out_specs=(pl.BlockSpec((None, tm, D), lambda b, t: (b, t, 0)),
                   pl.BlockSpec((None, tm, nw), lambda b, t: (b, t, 0))),
        scratch_shapes=[], args=(ctx, x, modl, modl, g1, w_in_t), after=after)


def _swap32(x):
    lane = lax.broadcasted_iota(jnp.int32, x.shape, 1)
    return jnp.where((lane % 64) < 32, pltpu.roll(x, 96, 1), pltpu.roll(x, 32, 1))


def _rope(x, cos, sin):
    return x * cos + _swap32(x) * sin


def _unrope(dy, cos, sin):
    return dy * cos + _swap32(dy * sin)


def _ret_weights(lgf, lgb, dist):
    return jnp.exp(jnp.where(dist >= 0.0, lgf * dist, -lgb * dist))


class _RetDecay:
    def __init__(self, lgf, lgb, rows):
        r = lax.broadcasted_iota(jnp.int32, (rows, RET_DIM), 0).astype(F32)
        self.head = r + 1.0
        self.tail = (rows - 1.0) - r
        self.q_f = jnp.exp(lgf * self.head)
        self.k_f = jnp.exp(lgf * self.tail)
        self.q_b = jnp.exp(lgb * self.tail)
        self.k_b = jnp.exp(lgb * self.head)


def _ret_states(kf32, vs, lgf, lgb, C, c, nt, hf, hb, hfa=None, hba=None):
    dec = _RetDecay(lgf, lgb, c)
    dec_c = _RetDecay(lgf, lgb, C)
    step_f = jnp.exp(jnp.zeros((RET_DIM, RET_DIM), F32) + lgf * c)
    step_b = jnp.exp(jnp.zeros((RET_DIM, RET_DIM), F32) + lgb * c)

    def upd(rows, kdec):
        return _dot_tn((kf32[rows, :] * kdec).astype(BF16), vs[rows, :])

    def lat(t):
        return slice(C + t * c, C + (t + 1) * c)

    state = upd(slice(0, C), dec_c.k_f)
    aged = jnp.zeros_like(state)
    for t in range(nt):
        hf[t] = state.astype(BF16)
        if hfa is not None:
            hfa[t] = aged
        if t < nt - 1:
            aged = step_f * (aged + c * state)
            state = step_f * state + upd(lat(t), dec.k_f)
    state = upd(slice(0, C), dec_c.k_b)
    aged = jnp.zeros_like(state)
    for t in range(nt - 1, -1, -1):
        hb[t] = state.astype(BF16)
        if hba is not None:
            hba[t] = aged
        if t > 0:
            aged = step_b * (aged + c * state)
            state = step_b * state + upd(lat(t), dec.k_b)
    return dec, dec_c, step_f, step_b


def _ret_fwd(proj, cos, sin, lg, gn, n_ctx):
    B, T, _ = proj.shape
    C = n_ctx
    N = T - C
    c = _div_tile(N, 256, 16)
    nt = N // c
    scale = RET_DIM ** -0.5

    def body(lg_ref, q_ref, k_ref, vs, g_ref, cos_ref, sin_ref, gn_ref, o_ref, lat_ref, qr_ref, kf32,
             qs, ks, hf, hb):
        h = pl.program_id(1)
        lgf = lg_ref[0, h]
        lgb = lg_ref[1, h]
        for rows in [slice(0, C)] + [slice(C + t * c, C + (t + 1) * c) for t in range(nt)]:
            cosb = cos_ref[rows, :]
            sinb = sin_ref[rows, :]
            qr = _rope(q_ref[rows, :].astype(F32), cosb, sinb) * scale
            qr_ref[rows, :] = qr
            qs[rows, :] = qr.astype(BF16)
            kr = _rope(k_ref[rows, :].astype(F32), cosb, sinb)
            kf32[rows, :] = kr
            ks[rows, :] = kr.astype(BF16)
        gnv = gn_ref[...]
        dec, _, _, _ = _ret_states(kf32, vs, lgf, lgb, C, c, nt, hf, hb)
        rc = (lax.broadcasted_iota(jnp.int32, (c, c), 0) - lax.broadcasted_iota(jnp.int32, (c, c), 1)).astype(F32)
        w_diag = _ret_weights(lgf, lgb, rc)
        for t in range(nt):
            rows = slice(C + t * c, C + (t + 1) * c)
            qt = qs[rows, :]
            s = _dot_nt(qt, ks[rows, :])
            o = (_dot((s * w_diag).astype(BF16), vs[rows, :])
                 + dec.q_f * _dot(qt, hf[t]) + dec.q_b * _dot(qt, hb[t]))
            o_ref[t * c:(t + 1) * c, :] = o
            mu = jnp.mean(o, axis=-1, keepdims=True)
            oc = o - mu
            var = jnp.mean(oc * oc, axis=-1, keepdims=True)
            yh = oc * lax.rsqrt(var + NORM_EPS)
            g = g_ref[rows, :].astype(F32)
            lat_ref[t * c:(t + 1) * c, :] = ((yh * gnv) * (g * _sigmoid(g))).astype(BF16)

    def col(seg):
        return pl.BlockSpec((None, T, RET_DIM), lambda b, h, seg=seg: (b, 0, seg * RET_HEADS + h))

    return _grid_call(
        body, name="ret_fwd", grid=(B, RET_HEADS),
        out_shape=(jax.ShapeDtypeStruct((B, N, RET_WIDTH), F32), jax.ShapeDtypeStruct((B, N, RET_WIDTH), BF16),
                   jax.ShapeDtypeStruct((B, T, RET_WIDTH), F32), jax.ShapeDtypeStruct((B, T, RET_WIDTH), F32)),
        in_specs=[pl.BlockSpec(memory_space=pltpu.SMEM), col(0), col(1), col(2), col(3),
                  pl.BlockSpec((T, RET_DIM), lambda b, h: (0, 0)), pl.BlockSpec((T, RET_DIM), lambda b, h: (0, 0)),
                  pl.BlockSpec((1, RET_DIM), lambda b, h: (0, h))],
        out_specs=(pl.BlockSpec((None, N, RET_DIM), lambda b, h: (b, 0, h)),
                   pl.BlockSpec((None, N, RET_DIM), lambda b, h: (b, 0, h)),
                   pl.BlockSpec((None, T, RET_DIM), lambda b, h: (b, 0, h)),
                   pl.BlockSpec((None, T, RET_DIM), lambda b, h: (b, 0, h))),
        scratch_shapes=[pltpu.VMEM((T, RET_DIM), BF16)] * 2 + [pltpu.VMEM((nt, RET_DIM, RET_DIM), BF16)] * 2,
        args=(lg, proj, proj, proj, proj, cos, sin, gn))


def _ret_bwd(proj, q_rot, k_rot, cos, sin, lg, gn, o, dlat, n_ctx, after):
    B, T, _ = proj.shape
    C = n_ctx
    N = T - C
    c = _div_tile(N, 256, 16)
    nt = N // c
    scale = RET_DIM ** -0.5

    def lat(t):
        return slice(C + t * c, C + (t + 1) * c)

    def body(lg_ref, qf32, kf32, vs, g_ref, cos_ref, sin_ref, gn_ref, o_ref, dl_ref,
             d_ref, dgn_ref, dlg_ref, qs, ks, dos, hf, hb, hfa, hba, gf_s, gb_s):
        h = pl.program_id(1)
        lgf = lg_ref[0, h]
        lgb = lg_ref[1, h]
        gnv = gn_ref[...]

        def fold(a):
            return jnp.sum(a.reshape(a.shape[0] // SUBLANES, SUBLANES, a.shape[1]), axis=0)

        for rows in [slice(0, C)] + [lat(t) for t in range(nt)]:
            qs[rows, :] = qf32[rows, :].astype(BF16)
            ks[rows, :] = kf32[rows, :].astype(BF16)

        dgn = jnp.zeros((1, RET_DIM), F32)
        for t in range(nt):
            lrows = slice(t * c, (t + 1) * c)
            ov = o_ref[lrows, :]
            mu = jnp.mean(ov, axis=-1, keepdims=True)
            oc = ov - mu
            var = jnp.mean(oc * oc, axis=-1, keepdims=True)
            rstd = lax.rsqrt(var + NORM_EPS)
            yh = oc * rstd
            g = g_ref[lat(t), :].astype(F32)
            sg = _sigmoid(g)
            dl = dl_ref[lrows, :]
            d_ref[3, lat(t), :] = (dl * (yh * gnv) * (sg * (1.0 + g * (1.0 - sg)))).astype(BF16)
            dls = dl * (g * sg)
            dgn = dgn + jnp.sum(dls * yh, axis=0, keepdims=True)
            dyh = dls * gnv
            do = rstd * (dyh - jnp.mean(dyh, axis=-1, keepdims=True)
                         - yh * jnp.mean(dyh * yh, axis=-1, keepdims=True))
            dos[lrows, :] = do.astype(BF16)
        dgn_ref[...] = jnp.concatenate([dgn, jnp.zeros((SUBLANES - 1, RET_DIM), F32)], axis=0)
        d_ref[3, 0:C, :] = jnp.zeros((C, RET_DIM), BF16)
        d_ref[0, 0:C, :] = jnp.zeros((C, RET_DIM), BF16)

        dec, dec_c, step_f, step_b = _ret_states(kf32, vs, lgf, lgb, C, c, nt, hf, hb, hfa, hba)

        def zmat(t, qdec):
            return _dot_tn((qf32[lat(t), :] * qdec).astype(BF16), dos[t * c:(t + 1) * c, :])

        acc3f = jnp.zeros((RET_DIM, RET_DIM), F32)
        acc3b = jnp.zeros((RET_DIM, RET_DIM), F32)
        state = jnp.zeros((RET_DIM, RET_DIM), F32)
        for t in range(nt - 1, -1, -1):
            gf_s[t] = state.astype(BF16)
            z = zmat(t, dec.q_f)
            acc3f = acc3f + hfa[t] * z
            state = step_f * state + z
        gctx_f = state.astype(BF16)
        state = jnp.zeros((RET_DIM, RET_DIM), F32)
        for t in range(nt):
            gb_s[t] = state.astype(BF16)
            z = zmat(t, dec.q_b)
            acc3b = acc3b + hba[t] * z
            state = step_b * state + z
        gctx_b = state.astype(BF16)

        rc = (lax.broadcasted_iota(jnp.int32, (c, c), 0) - lax.broadcasted_iota(jnp.int32, (c, c), 1)).astype(F32)
        w_diag = _ret_weights(lgf, lgb, rc)
        wg_f = jnp.where(rc >= 0.0, w_diag * rc, 0.0)
        wg_b = jnp.where(rc < 0.0, -w_diag * rc, 0.0)
        accf = jnp.zeros((SUBLANES, RET_DIM), F32)
        accb = jnp.zeros((SUBLANES, RET_DIM), F32)
        gdf = jnp.zeros((SUBLANES, c), F32)
        gdb = jnp.zeros((SUBLANES, c), F32)
        for t in range(nt):
            rows = lat(t)
            qt = qs[rows, :]
            kt = ks[rows, :]
            vt = vs[rows, :]
            dot = dos[t * c:(t + 1) * c, :]
            s = _dot_nt(qt, kt)
            dp = _dot_nt(dot, vt)
            dv = _dot_tn((s * w_diag).astype(BF16), dot)
            ds = (dp * w_diag).astype(BF16)
            dq = _dot(ds, kt)
            dk = _dot_tn(ds, qt)
            gs = dp * s
            gdf = gdf + fold(gs * wg_f)
            gdb = gdb + fold(gs * wg_b)
            qv = qf32[rows, :]
            kv = kf32[rows, :]
            dq_f = dec.q_f * _dot_nt(dot, hf[t])
            dq_b = dec.q_b * _dot_nt(dot, hb[t])
            dk_f = dec.k_f * _dot_nt(vt, gf_s[t])
            dk_b = dec.k_b * _dot_nt(vt, gb_s[t])
            accf = accf + fold(dec.head * dq_f * qv) + fold(dec.tail * dk_f * kv)
            accb = accb + fold(dec.tail * dq_b * qv) + fold(dec.head * dk_b * kv)
            dv = dv + dec.k_f * _dot(kt, gf_s[t]) + dec.k_b * _dot(kt, gb_s[t])
            cosb = cos_ref[rows, :]
            sinb = sin_ref[rows, :]
            d_ref[0, rows, :] = _unrope((dq + dq_f + dq_b) * scale, cosb, sinb).astype(BF16)
            d_ref[1, rows, :] = _unrope(dk + dk_f + dk_b, cosb, sinb).astype(BF16)
            d_ref[2, rows, :] = dv.astype(BF16)
        kc = ks[0:C, :]
        vc = vs[0:C, :]
        kcv = kf32[0:C, :]
        dkc_f = dec_c.k_f * _dot_nt(vc, gctx_f)
        dkc_b = dec_c.k_b * _dot_nt(vc, gctx_b)
        accf = accf + fold(dec_c.tail * dkc_f * kcv)
        accb = accb + fold(dec_c.head * dkc_b * kcv)
        d_ref[1, 0:C, :] = (dkc_f + dkc_b).astype(BF16)
        d_ref[2, 0:C, :] = (dec_c.k_f * _dot(kc, gctx_f) + dec_c.k_b * _dot(kc, gctx_b)).astype(BF16)
        gf = jnp.sum(gdf) + jnp.sum(accf) + jnp.sum(acc3f)
        gb = jnp.sum(gdb) + jnp.sum(accb) + jnp.sum(acc3b)
        row = lax.broadcasted_iota(jnp.int32, (SUBLANES, LANES), 0)
        dlg_ref[...] = jnp.where(row == 0, gf, jnp.where(row == 1, gb, 0.0))

    def col(seg):
        return pl.BlockSpec((None, T, RET_DIM), lambda b, h, seg=seg: (b, 0, seg * RET_HEADS + h))

    def head(rows):
        return pl.BlockSpec((None, rows, RET_DIM), lambda b, h: (b, 0, h))

    return _grid_call(
        body, name="ret_bwd", grid=(B, RET_HEADS),
        out_shape=(jax.ShapeDtypeStruct((B, 4, T, RET_WIDTH), BF16),
                   jax.ShapeDtypeStruct((B, SUBLANES, RET_WIDTH), F32),
                   jax.ShapeDtypeStruct((B, RET_HEADS, SUBLANES, LANES), F32)),
        in_specs=[pl.BlockSpec(memory_space=pltpu.SMEM), head(T), head(T), col(2), col(3),
                  pl.BlockSpec((T, RET_DIM), lambda b, h: (0, 0)), pl.BlockSpec((T, RET_DIM), lambda b, h: (0, 0)),
                  pl.BlockSpec((1, RET_DIM), lambda b, h: (0, h)), head(N), head(N)],
        out_specs=(pl.BlockSpec((None, 4, T, RET_DIM), lambda b, h: (b, 0, 0, h)),
                   pl.BlockSpec((None, SUBLANES, RET_DIM), lambda b, h: (b, 0, h)),
                   pl.BlockSpec((None, None, SUBLANES, LANES), lambda b, h: (b, h, 0, 0))),
        scratch_shapes=[pltpu.VMEM((T, RET_DIM), BF16)] * 2 + [pltpu.VMEM((N, RET_DIM), BF16)]
                       + [pltpu.VMEM((nt, RET_DIM, RET_DIM), BF16)] * 2 + [pltpu.VMEM((nt, RET_DIM, RET_DIM), F32)] * 2
                       + [pltpu.VMEM((nt, RET_DIM, RET_DIM), BF16)] * 2,
        args=(lg, q_rot, k_rot, proj, proj, cos, sin, gn, o, dlat), after=after)


def _na_geometry(rows):
    kh = min(NA_KH, rows)
    return kh, kh * GRID_W


def _pair_select():
    lane = lax.broadcasted_iota(jnp.int32, (2 * GRID_W, LANES), 1)
    row = lax.broadcasted_iota(jnp.int32, (2 * GRID_W, LANES), 0)
    return (lane >= NA_DIM) == (row >= GRID_W)


def _pair_bias(bias_ref, dr0, kh):
    return jnp.concatenate(
        [jnp.concatenate([bias_ref[e, pl.ds(dr0 + 2 * m, 1)].reshape(GRID_W, LANES) for m in range(kh // 2)], axis=1)
         for e in range(2)], axis=0)


def _na_softmax(s_loc, s_ctx):
    mx = jnp.maximum(jnp.max(s_loc, axis=-1, keepdims=True), jnp.max(s_ctx, axis=-1, keepdims=True))
    p_loc = jnp.exp(s_loc - mx)
    p_ctx = jnp.exp(s_ctx - mx)
    den = jnp.sum(p_loc, axis=-1, keepdims=True) + jnp.sum(p_ctx, axis=-1, keepdims=True)
    return p_loc, p_ctx, den


def _na_fwd(proj, bias2, n_ctx):
    assert proj.dtype == BF16
    B, T, _ = proj.shape
    C = n_ctx
    N = T - C
    R = N // GRID_W
    kh, nk = _na_geometry(R)
    scale = NA_DIM ** -0.5
    base = (4 * RET_WIDTH) // LANES

    def body(q_ref, kb16, vb16, bias_ref, out_ref, p_ref):
        kc = kb16[0:C, :]
        vc = vb16[0:C, :]
        lane = lax.broadcasted_iota(jnp.int32, (GRID_W, LANES), 1)
        sel2 = _pair_select()

        def group(gi, carry):
            pre = []
            for u in range(NA_GROUP):
                r = gi * NA_GROUP + u
                bs = jnp.clip(r - kh // 2, 0, R - kh)
                dr0 = bs - r + (NA_KH - 1)
                q = q_ref[pl.ds(pl.multiple_of(C + r * GRID_W, GRID_W), GRID_W), :].astype(F32) * scale
                q2 = jnp.where(sel2, jnp.concatenate([q, q], axis=0), 0.0).astype(BF16)
                band = pl.ds(pl.multiple_of(C + bs * GRID_W, GRID_W), nk)
                s_loc = _dot_nt(q2, kb16[band, :]) + _pair_bias(bias_ref, dr0, kh)
                s_ctx = _dot_nt(q2, kc)
                pre.append((r, band, s_loc, s_ctx))
            mid = [(r, band) + _na_softmax(s_loc, s_ctx) for r, band, s_loc, s_ctx in pre]
            for r, band, p_loc, p_ctx, den in mid:
                inv = 1.0 / den
                pb_loc = (p_loc * inv).astype(BF16)
                pb_ctx = (p_ctx * inv).astype(BF16)
                p_ref[r, :, 0:nk] = pb_loc
                p_ref[r, :, nk:] = pb_ctx
                o2 = _dot(pb_loc, vb16[band, :]) + _dot(pb_ctx, vc)
                out_ref[pl.ds(pl.multiple_of(r * GRID_W, GRID_W), GRID_W), :] = jnp.where(
                    lane < NA_DIM, o2[:GRID_W], o2[GRID_W:]).astype(BF16)
            return carry

        lax.fori_loop(0, R // NA_GROUP, group, 0)

    def col(seg):
        return pl.BlockSpec((None, T, LANES), lambda b, p, seg=seg: (b, 0, base + seg * NA_PAIRS + p))

    return _grid_call(
        body, name="na_fwd", grid=(B, NA_PAIRS),
        out_shape=(jax.ShapeDtypeStruct((B, N, NA_WIDTH), BF16),
                   jax.ShapeDtypeStruct((B, NA_PAIRS, R, 2 * GRID_W, nk + C), BF16)),
        in_specs=[col(0), col(1), col(2),
                  pl.BlockSpec((2, 2 * NA_KH - 2, GRID_W, LANES), lambda b, p: (p, 0, 0, 0))],
        out_specs=(pl.BlockSpec((None, N, LANES), lambda b, p: (b, 0, p)),
                   pl.BlockSpec((None, None, R, 2 * GRID_W, nk + C), lambda b, p: (b, p, 0, 0, 0))),
        scratch_shapes=[],
        args=(proj, proj, proj, bias2))


def _na_bwd(proj, probs, dlat, n_ctx):
    assert proj.dtype == BF16
    B, T, _ = proj.shape
    C = n_ctx
    N = T - C
    R = N // GRID_W
    kh, nk = _na_geometry(R)
    scale = NA_DIM ** -0.5
    base = (4 * RET_WIDTH) // LANES

    def class_sums(tiles):
        n = tiles.shape[0]
        acc = None
        for v in range(GRID_W // SUBLANES):
            part = tiles[:, v * SUBLANES:(v + 1) * SUBLANES, :].reshape(n * SUBLANES, LANES)
            part = pltpu.roll(part, (NA_KW - 1 - v * SUBLANES) % LANES, 1)
            acc = part if acc is None else acc + part
        row = lax.broadcasted_iota(jnp.int32, acc.shape, 0)
        for bit in (1, 2, 4):
            acc = jnp.where((row & bit) != 0, pltpu.roll(acc, LANES - bit, 1), acc)
        return jnp.sum(acc.reshape(n, SUBLANES, LANES), axis=1)

    def body(q_ref, kb16, vb16, p_ref, dl_ref, d_ref, rr_ref, dkv, db_ref):
        b = pl.program_id(1)
        kc = kb16[0:C, :]
        vc = vb16[0:C, :]
        lane = lax.broadcasted_iota(jnp.int32, (GRID_W, LANES), 1)
        dkv[...] = jnp.zeros(dkv.shape, F32)
        d_ref[0, 0:C, :] = jnp.zeros((C, LANES), BF16)

        @pl.when(b == 0)
        def _():
            db_ref[...] = jnp.zeros(db_ref.shape, F32)

        sel2 = _pair_select()

        def group(gi, carry):
            pre = []
            for u in range(NA_GROUP):
                r = gi * NA_GROUP + u
                bs = jnp.clip(r - kh // 2, 0, R - kh)
                dr0 = bs - r + (NA_KH - 1)
                q = q_ref[pl.ds(pl.multiple_of(C + r * GRID_W, GRID_W), GRID_W), :].astype(F32) * scale
                do = dl_ref[pl.ds(pl.multiple_of(r * GRID_W, GRID_W), GRID_W), :]
                q2 = jnp.where(sel2, jnp.concatenate([q, q], axis=0), 0.0).astype(BF16)
                do2 = jnp.where(sel2, jnp.concatenate([do, do], axis=0), 0.0).astype(BF16)
                band = pl.ds(pl.multiple_of(C + bs * GRID_W, GRID_W), nk)
                dp_loc = _dot_nt(do2, vb16[band, :])
                dp_ctx = _dot_nt(do2, vc)
                pre.append((r, dr0, band, q2, do2, dp_loc, dp_ctx))
            mid = []
            for r, dr0, band, q2, do2, dp_loc, dp_ctx in pre:
                pb_loc = p_ref[r, :, 0:nk]
                pb_ctx = p_ref[r, :, nk:]
                p_loc = pb_loc.astype(F32)
                p_ctx = pb_ctx.astype(F32)
                delta = (jnp.sum(p_loc * dp_loc, axis=-1, keepdims=True)
                         + jnp.sum(p_ctx * dp_ctx, axis=-1, keepdims=True))
                ds_loc = p_loc * (dp_loc - delta)
                ds_ctx = p_ctx * (dp_ctx - delta)
                mid.append((r, dr0, band, q2, do2, pb_loc, pb_ctx, ds_loc, ds_ctx))
            for r, dr0, band, q2, do2, pb_loc, pb_ctx, ds_loc, ds_ctx in mid:
                dsb_loc = ds_loc.astype(BF16)
                dsb_ctx = ds_ctx.astype(BF16)
                dq2 = _dot(dsb_loc, kb16[band, :]) + _dot(dsb_ctx, kc)
                d_ref[0, pl.ds(pl.multiple_of(C + r * GRID_W, GRID_W), GRID_W), :] = (jnp.where(
                    lane < NA_DIM, dq2[:GRID_W], dq2[GRID_W:]) * scale).astype(BF16)
                dkv[0, band, :] += _dot_tn(dsb_loc, q2)
                dkv[1, band, :] += _dot_tn(pb_loc, do2)
                dkv[0, 0:C, :] += _dot_tn(dsb_ctx, q2)
                dkv[1, 0:C, :] += _dot_tn(pb_ctx, do2)
                for e in range(2):
                    for m in range(kh // 2):
                        db_ref[e, pl.ds(dr0 + 2 * m, 1)] += ds_loc[e * GRID_W:(e + 1) * GRID_W,
                                                                   m * LANES:(m + 1) * LANES].reshape(1, GRID_W, LANES)
            return carry

        lax.fori_loop(0, R // NA_GROUP, group, 0)
        d_ref[1] = dkv[0].astype(BF16)
        d_ref[2] = dkv[1].astype(BF16)

        @pl.when(b == B - 1)
        def _():
            for e in range(2):
                rr_ref[e] = class_sums(db_ref[e])

    def col(seg):
        return pl.BlockSpec((None, T, LANES), lambda p, b, seg=seg: (b, 0, base + seg * NA_PAIRS + p))

    return _grid_call(
        body, name="na_bwd", grid=(NA_PAIRS, B),
        out_shape=(jax.ShapeDtypeStruct((B, 3, T, NA_WIDTH), BF16),
                   jax.ShapeDtypeStruct((NA_HEADS, 2 * NA_KH - 2, LANES), F32)),
        in_specs=[col(0), col(1), col(2),
                  pl.BlockSpec((None, None, R, 2 * GRID_W, nk + C), lambda p, b: (b, p, 0, 0, 0)),
                  pl.BlockSpec((None, N, LANES), lambda p, b: (b, 0, p))],
        out_specs=(pl.BlockSpec((None, 3, T, LANES), lambda p, b: (b, 0, 0, p)),
                   pl.BlockSpec((2, 2 * NA_KH - 2, LANES), lambda p, b: (p, 0, 0))),
        scratch_shapes=[pltpu.VMEM((2, T, LANES), F32), pltpu.VMEM((2, 2 * NA_KH - 2, GRID_W, LANES), F32)],
        args=(proj, proj, proj, probs, dlat))


def _dense_core(lat_ret, lat_na, x, tgt, modl, g_post_mix, g_pre_mlp, g_post_mlp, w_out, w1, w2):
    B, N, D = x.shape
    F = w1.shape[1]
    wout_rows, w1_cols, w2_rows = w_out.shape[0] // N_DEV, w1.shape[1] // N_DEV, w2.shape[0] // N_DEV
    mixw = w_out.shape[0]
    half = mixw // 2
    tm = _div_tile(N, 256, 16)
    nt = N // tm
    fc = _div_tile(F, 1024, LANES)

    def body(lr_ref, ln_ref, x_ref, t_ref, gt1_ref, sh2_ref, sc2_ref, gt2_ref, gpm_ref, gpre_ref, gpo_ref,
             wout_part, w1_part, w2_part,
             dy1_ref, dlr_ref, dln_ref, dmix_ref, h2_ref, a_ref, du_ref, dz_ref, red_ref, wout_hbm, w1_hbm, w2_hbm,
             wout_v, w1_v, w2_v, u_s, sems, fsend, frecv):
        @pl.when((pl.program_id(0) == 0) & (pl.program_id(1) == 0))
        def _():
            relay = [(_row_block(wout_part, wout_rows), _row_block(wout_hbm, wout_rows)),
                     (_col_block(w1_part, w1_cols), _col_block(w1_hbm, w1_cols)),
                     (_row_block(w2_part, w2_rows), _row_block(w2_hbm, w2_rows))]
            _forward_start(relay, fsend, frecv)
            _forward_wait(relay, fsend, frecv)
            cps = [pltpu.make_async_copy(wout_hbm, wout_v, sems.at[0]),
                   pltpu.make_async_copy(w1_hbm, w1_v, sems.at[1]),
                   pltpu.make_async_copy(w2_hbm, w2_v, sems.at[2])]
            for cp in cps:
                cp.start()
            for cp in cps:
                cp.wait()

        @pl.when(pl.program_id(1) == 0)
        def _():
            red_ref[...] = jnp.zeros(red_ref.shape, F32)

        gt1 = gt1_ref[...]
        sh2 = sh2_ref[...]
        sc2 = sc2_ref[...]
        gt2 = gt2_ref[...]
        gpm = gpm_ref[...]
        gpre = gpre_ref[...]
        gpo = gpo_ref[...]

        def rowmean(a):
            return jnp.mean(a, axis=-1, keepdims=True)

        def colsum(a):
            return jnp.sum(a, axis=0, keepdims=True)

        mix_gain = gt1 * gpm
        mlp_in_gain = gpre * (1.0 + sc2)
        mlp_out_gain = gt2 * gpo
        mix = _dot(lr_ref[...], wout_v[0:half, :]) + _dot(ln_ref[...], wout_v[half:, :])
        x = x_ref[...]
        rm = lax.rsqrt(rowmean(mix * mix) + NORM_EPS)
        mh = mix * rm
        y1 = x + mh * mix_gain
        r1 = lax.rsqrt(rowmean(y1 * y1) + NORM_EPS)
        xh = y1 * r1
        h2b = (xh * mlp_in_gain + sh2).astype(BF16)
        h2_ref[...] = h2b
        z = jnp.zeros((tm, D), F32)
        for c0 in range(0, F, fc):
            u = _dot(h2b, w1_v[:, c0:c0 + fc])
            u_s[:, c0:c0 + fc] = u
            ru = jnp.maximum(u, 0.0)
            ab = (ru * ru).astype(BF16)
            a_ref[:, c0:c0 + fc] = ab
            z = z + _dot(ab, w2_v[c0:c0 + fc, :])
        r2 = lax.rsqrt(rowmean(z * z) + NORM_EPS)
        zh = z * r2
        y2 = y1 + zh * mlp_out_gain
        err = y2 - t_ref[...]
        loss = 0.5 * jnp.sum(rowmean(err * err))
        dy2 = err * (1.0 / D)
        s_out = colsum(dy2 * zh)
        red_ref[2:3, :] += s_out * gpo
        red_ref[6:7, :] += s_out * gt2
        dzh = dy2 * mlp_out_gain
        dz = r2 * (dzh - zh * rowmean(dzh * zh))
        dzb = dz.astype(BF16)
        dz_ref[...] = dzb
        dh2 = jnp.zeros((tm, D), F32)
        for c0 in range(0, F, fc):
            da = _dot_nt(dzb, w2_v[c0:c0 + fc, :])
            dub = (da * (2.0 * jnp.maximum(u_s[:, c0:c0 + fc], 0.0))).astype(BF16)
            du_ref[:, c0:c0 + fc] = dub
            dh2 = dh2 + _dot_nt(dub, w1_v[:, c0:c0 + fc])
        s_in = colsum(dh2 * xh)
        red_ref[3:4, :] += s_in * gpre
        red_ref[4:5, :] += colsum(dh2)
        red_ref[5:6, :] += s_in * (1.0 + sc2)
        dxh = dh2 * mlp_in_gain
        dy1 = dy2 + r1 * (dxh - xh * rowmean(dxh * xh))
        dy1_ref[...] = dy1
        s_mix = colsum(dy1 * mh)
        red_ref[0:1, :] += s_mix * gpm
        red_ref[1:2, :] += s_mix * gt1
        dmh = dy1 * mix_gain
        dmix = (rm *(dmh - mh * rowmean(dmh * mh))).astype(BF16)
        dmix_ref[...] = dmix
        dlr_ref[...] = _dot_nt(dmix, wout_v[0:half, :])
        dln_ref[...] = _dot_nt(dmix, wout_v[half:, :])
        red_ref[7:8, :] += jnp.zeros((1, D), F32) + loss

    def tok(w):
        return pl.BlockSpec((None, tm, w), lambda b, t: (b, t, 0))

    def mod(k):
        return pl.BlockSpec((None, None, 1, D), lambda b, t, k=k: (b, k, 0, 0))

    def vec():
        return pl.BlockSpec((1, D), lambda b, t: (0, 0))

    return pl.pallas_call(
        body, name="dense_core", grid=(B, nt),
        out_shape=(jax.ShapeDtypeStruct((B, N, D), F32), jax.ShapeDtypeStruct((B, N, half), F32),
                   jax.ShapeDtypeStruct((B, N, half), F32), jax.ShapeDtypeStruct((B, N, D), BF16),
                   jax.ShapeDtypeStruct((B, N, D), BF16), jax.ShapeDtypeStruct((B, N, F), BF16),
                   jax.ShapeDtypeStruct((B, N, F), BF16), jax.ShapeDtypeStruct((B, N, D), BF16),
                   jax.ShapeDtypeStruct((B, SUBLANES, D), F32),
                   jax.ShapeDtypeStruct(w_out.shape, w_out.dtype), jax.ShapeDtypeStruct(w1.shape, w1.dtype),
                   jax.ShapeDtypeStruct(w2.shape, w2.dtype)),
        in_specs=[tok(half), tok(half), tok(D), tok(D), mod(2), mod(3), mod(4), mod(5), vec(), vec(), vec(),
                  _any(), _any(), _any()],
        out_specs=(tok(D), tok(half), tok(half), tok(D), tok(D), tok(F), tok(F), tok(D),
                   pl.BlockSpec((None, SUBLANES, D), lambda b, t: (b, 0, 0)), _any(), _any(), _any()),
        scratch_shapes=[pltpu.VMEM((mixw, D), BF16), pltpu.VMEM((D, F), BF16), pltpu.VMEM((F, D), BF16),
                        pltpu.VMEM((tm, F), F32), pltpu.SemaphoreType.DMA((3,)),
                        pltpu.SemaphoreType.DMA((3, 3)), pltpu.SemaphoreType.DMA((3, 3))],
        input_output_aliases={11: 9, 12: 10, 13: 11},
        compiler_params=_params("arbitrary", "arbitrary"),
    )(lat_ret, lat_na, x, tgt, modl, modl, modl, modl, g_post_mix, g_pre_mlp, g_post_mlp, w_out, w1, w2)[:9]


def _inproj_bwd(dret, dna, x, ctx, dy1, modl, g1, w_in_t, after):
    B, N, D = x.shape
    n_ctx = ctx.shape[1]
    T = n_ctx + N
    tm = _div_tile(n_ctx, 256, 16)
    nct, ctx_spec, lat_spec = _token_tiles(n_ctx, tm)
    nt = T // tm
    nseg_r = dret.shape[1]
    nseg_n = dna.shape[1]
    nw = w_in_t.shape[0]

    def body(*refs):
        seg_refs = refs[:nseg_r + nseg_n]
        c_ref, x_ref, dy1_ref, sc_ref, g_ref, w_ref, dx_ref, red_ref = refs[nseg_r + nseg_n:]
        t = pl.program_id(1)
        dh = jnp.zeros((tm, D), F32)
        for s, ref in enumerate(seg_refs):
            dh = dh + _dot(ref[...], w_ref[s * SEG:(s + 1) * SEG, :])
        x = jnp.where(t < nct, c_ref[...], x_ref[...])
        g = g_ref[...]
        r = lax.rsqrt(jnp.mean(x * x, axis=-1, keepdims=True) + NORM_EPS)
        xh = x * r
        gain = 1.0 + sc_ref[...]
        s_in = jnp.sum(dh * xh, axis=0, keepdims=True)
        red_ref[0:1, :] = jnp.sum(dh, axis=0, keepdims=True)
        red_ref[1:2, :] = s_in * g
        red_ref[2:3, :] = s_in * gain
        red_ref[3:, :] = jnp.zeros((SUBLANES - 3, D), F32)
        dxh = dh * (g * gain)
        dx = r * (dxh - xh * jnp.mean(dxh * xh, axis=-1, keepdims=True))
        dx_ref[...] = dx + jnp.where(t >= nct, dy1_ref[...], 0.0)

    def mrow(b, t):
        return jnp.where(t < nct, B, b)

    def seg(s):
        return pl.BlockSpec((None, None, tm, SEG), lambda b, t, s=s: (b, s, t, 0))

    return _grid_call(
        body, name="inproj_bwd", grid=(B, nt),
        out_shape=(jax.ShapeDtypeStruct((B, N, D), F32), jax.ShapeDtypeStruct((B, nt, SUBLANES, D), F32)),
        in_specs=[seg(s) for s in range(nseg_r)] + [seg(s) for s in range(nseg_n)]
                 + [ctx_spec(D), lat_spec(D), lat_spec(D),
                    pl.BlockSpec((None, None, 1, D), lambda b, t: (mrow(b, t), 1, 0, 0)),
                    pl.BlockSpec((1, D), lambda b, t: (0, 0)),
                    pl.BlockSpec((nw, D), lambda b, t: (0, 0))],
        out_specs=(lat_spec(D), pl.BlockSpec((None, None, SUBLANES, D), lambda b, t: (b, t, 0, 0))),
        scratch_shapes=[], args=(*([dret] * nseg_r), *([dna] * nseg_n), ctx, x, dy1, modl, g1, w_in_t), after=after)


def _tn_matmul(lhs, rhs, name, rows_before=0, rows_after=0, into=None):
    B, S, T, W = lhs.shape
    nn = rhs.shape[-1]
    cap = 1024
    tk = _div_tile(T, 2304, LANES)
    bm = _div_tile(W, cap, LANES)
    bn = _div_tile(nn, cap, LANES)
    nkt = T // tk
    nk = B * nkt
    nwb = W // bm
    sg = 2 if (nwb == 1 and S % 2 == 0 and 2 * bm <= cap and rows_before % (2 * bm) == 0) else 1
    mo = sg * bm

    def body(l_ref, r_ref, *rest):
        o_ref, acc = rest[-2:]
        k, s = pl.program_id(2), pl.program_id(3)

        def part():
            l = l_ref[...] if sg == 1 else jnp.concatenate([l_ref[g] for g in range(sg)], axis=1)
            return _dot_tn(l.astype(BF16), r_ref[...].astype(BF16))

        @pl.when(k == 0)
        def _():
            acc[s] = part()

        @pl.when(k > 0)
        def _():
            acc[s] += part()

        @pl.when(k == nk - 1)
        def _():
            o_ref[...] = acc[s].astype(BF16)

    first = rows_before // mo

    def out_block(i, j, k, s):
        return first + jnp.where(k == nk - 1, s, 0) * nwb + i, j
    return pl.pallas_call(
        functools.partial(body), name=name, grid=(nwb, nn // bn, nk, S // sg),
        out_shape=jax.ShapeDtypeStruct((rows_before + S * W + rows_after, nn), BF16),
        in_specs=[pl.BlockSpec((None, None if sg == 1 else sg, tk, bm), lambda i, j, k, s: (k // nkt, s, k % nkt, i)),
                  pl.BlockSpec((None, tk, bn), lambda i, j, k, s: (k // nkt, k % nkt, j))]
                 + ([] if into is None else [_any()]),
        out_specs=pl.BlockSpec((mo, bn), out_block),
        scratch_shapes=[pltpu.VMEM((S // sg, mo, bn), F32)],
        input_output_aliases={} if into is None else {2: 0},
        compiler_params=_params("parallel", "parallel", "arbitrary", "arbitrary"),
    )(lhs, rhs, *([] if into is None else [into]))


class _SplitScatter:
    def __init__(self, gs, block_ofs, land_shapes, name, kind="scatter", masks=ALL_PEERS):
        self.n = n = len(gs)
        self.block_ofs, self.kind, self.masks = block_ofs, kind, masks
        if kind == "scatter":
            land_shapes = [(N_DEV,) + tuple(bs) for bs in land_shapes]
        hbm = pl.BlockSpec(memory_space=pltpu.HBM)
        sem = pl.BlockSpec(memory_space=pltpu.SEMAPHORE)

        def body(*refs):
            g_refs, land_refs = refs[:n], refs[n:2 * n]
            send_sems, recv_sems, own_sems = refs[2 * n:2 * n + 3]
            token = refs[-1]
            for own, pushes in self._copies(g_refs, land_refs, send_sems, recv_sems, own_sems, landing="sender"):
                own.start()
                for cp in pushes:
                    cp.start()
            token[...] = jnp.zeros_like(token)

        outs = pl.pallas_call(
            body, name=name,
            out_shape=(pltpu.SemaphoreType.DMA((n * (N_DEV - 1),)), pltpu.SemaphoreType.DMA((n * (N_DEV - 1),)),
                       pltpu.SemaphoreType.DMA((n,)))
                      + tuple(pltpu.HBM(g.shape, g.dtype) for g in gs)
                      + tuple(pltpu.HBM(s, g.dtype) for s, g in zip(land_shapes, gs))
                      + (jax.ShapeDtypeStruct((SUBLANES, LANES), F32),),
            in_specs=(hbm,) * (2 * n), out_specs=(sem,) * 3 + (hbm,) * (2 * n) + (_vmem(),),
            input_output_aliases={k: 3 + k for k in range(2 * n)},
            compiler_params=pltpu.CompilerParams(has_side_effects=pltpu.SideEffectType.DATAFLOW_SIDE_EFFECTING),
        )(*[pltpu.with_memory_space_constraint(g, pltpu.HBM) for g in gs],
          *[pltpu.with_memory_space_constraint(lax.empty(s, g.dtype), pltpu.HBM) for s, g in zip(land_shapes, gs)])
        self.sems, self.thru, self.token = outs[:3], outs[3:3 + 2 * n], outs[-1]

    def _copies(self, g_refs, land_refs, send_sems, recv_sems, own_sems, landing):
        me, peers = _me_and_peers()
        out = []
        for k in range(self.n):
            if self.kind == "scatter":
                src, dst = self.block_ofs[k](g_refs[k]), _slot(land_refs[k])
            else:
                src, dst = (lambda p, k=k: g_refs[k]), self.block_ofs[k](land_refs[k])
            own = pltpu.make_async_copy(src(me), dst(me), own_sems.at[k]) if landing == "sender" else None
            pushes = []
            for m in self.masks:
                dev, pid = peers[m - 1]
                i = k * (N_DEV - 1) + m - 1
                pushes.append(_remote(src(pid), dst(me if landing == "sender" else pid),
                                      send_sems.at[i], recv_sems.at[i], dev))
            out.append((own, pushes))
        return out


def _scatter_wait(scatters, after, name):
    hbm = pl.BlockSpec(memory_space=pltpu.HBM)
    sem = pl.BlockSpec(memory_space=pltpu.SEMAPHORE)
    n_arr = [2 * sc.n for sc in scatters]
    total = sum(n_arr)

    def body(*refs):
        arrs, sems = refs[:total], refs[total:total + 3 * len(scatters)]
        a0 = 0
        for j, sc in enumerate(scatters):
            g_refs, land_refs = arrs[a0:a0 + sc.n], arrs[a0 + sc.n:a0 + 2 * sc.n]
            a0 += 2 * sc.n
            send_sems, recv_sems, own_sems = sems[3 * j:3 * j + 3]
            for (own, sent), (_, got) in zip(sc._copies(g_refs, land_refs, send_sems, recv_sems, own_sems, "sender"),
                                             sc._copies(g_refs, land_refs, send_sems, recv_sems, own_sems, "receiver")):
                own.wait()
                for cp in sent:
                    cp.wait_send()
                for cp in got:
                    cp.wait_recv()

    operands = [a for sc in scatters for a in sc.thru]
    outs = pl.pallas_call(
        body, name=name,
        out_shape=tuple(pltpu.HBM(a.shape, a.dtype) for a in operands),
        in_specs=(hbm,) * total + (sem,) * (3 * len(scatters)) + (pl.BlockSpec(memory_space=pl.ANY),),
        out_specs=(hbm,) * total, input_output_aliases={k: k for k in range(total)},
        compiler_params=pltpu.CompilerParams(has_side_effects=pltpu.SideEffectType.DATAFLOW_SIDE_EFFECTING),
    )(*operands, *[s for sc in scatters for s in sc.sems], after)
    lands, a0 = [], 0
    for sc in scatters:
        lands.extend(outs[a0 + sc.n:a0 + 2 * sc.n])
        a0 += 2 * sc.n
    return lands


def _small_ar(mbuf, silu_all, w_ada, c_ctx, n_mod_rows, n_vec_rows):
    D = silu_all.shape[1]
    ncol = w_ada.shape[1]
    nm = mbuf.shape[2]
    srows = silu_all.shape[0]

    def body(mbuf, s_ref, w_ref, cc_ref, tot_ref, gb_ref, gw_ref, gc_ref, tbuf, dmx, cmrow, send3, recv3):
        me, _ = _me_and_peers()
        msum = mbuf[0]
        for k in range(1, N_DEV):
            msum = msum + mbuf[k]
        tot_ref[...] = msum[n_mod_rows:n_mod_rows + n_vec_rows]
        gb_ref[...] = jnp.sum(msum[0:n_mod_rows], axis=0, keepdims=True)
        loc = pl.ds(pl.multiple_of(me * ncol, ncol), ncol)
        for k in range(N_DEV):
            dmx[k * SUBLANES:(k + 1) * SUBLANES, :] = mbuf[k, :, loc]
        cmrow[...] = msum
        cm_loc = cmrow[n_mod_rows - 1:n_mod_rows, loc]
        dmx[N_DEV * SUBLANES:, :] = jnp.concatenate([cm_loc, jnp.zeros((SUBLANES - 1, ncol), F32)], axis=0)
        tbuf[me] = _dot_nt(dmx[N_DEV * SUBLANES:, :], w_ref[...])
        partials = [(lambda p: tbuf.at[me], lambda p: tbuf.at[p])]
        _push_start(partials, ALL_PEERS, send3, recv3)
        gw_ref[...] = _dot_tn(s_ref[...], dmx[...])
        _push_wait_recv(partials, ALL_PEERS, send3, recv3)
        _push_wait_send(partials, ALL_PEERS, send3, recv3)
        tsum = tbuf[0]
        for k in range(1, N_DEV):
            tsum = tsum + tbuf[k]
        cc = cc_ref[...]
        sg = _sigmoid(cc)
        gc_ref[...] = tsum[0:1, :] * (sg * (1.0 + cc * (1.0 - sg)))

    return pl.pallas_call(
        body, name="small_ar",
        out_shape=(jax.ShapeDtypeStruct((n_vec_rows, nm), F32), jax.ShapeDtypeStruct((1, nm), F32),
                   jax.ShapeDtypeStruct((D, ncol), F32), jax.ShapeDtypeStruct((1, D), F32)),
        in_specs=[_vmem()] * 4, out_specs=(_vmem(),) * 4,
        scratch_shapes=[pltpu.VMEM((N_DEV, SUBLANES, D), F32), pltpu.VMEM((srows, ncol), F32),
                        pltpu.VMEM((SUBLANES, nm), F32)] + [pltpu.SemaphoreType.DMA((1, N_DEV - 1))] * 2,
        compiler_params=pltpu.CompilerParams(vmem_limit_bytes=VMEM_LIMIT),
    )(mbuf, silu_all, w_ada, c_ctx.reshape(1, D))


def _adam_update(w, g, m, v):
    mn = ADAM_B1 * m + (1.0 - ADAM_B1) * g
    vn = ADAM_B2 * v + (1.0 - ADAM_B2) * (g * g)
    m_hat = mn / (1.0 - ADAM_B1 ** ADAM_STEP)
    v_hat = vn / (1.0 - ADAM_B2 ** ADAM_STEP)
    return -ADAM_LR * (m_hat / (jnp.sqrt(v_hat) + ADAM_EPS) + ADAM_WD * w), mn, vn


def _adamw(w, g, m, v, name):
    rows, cols = w.shape
    tr = _div_tile(rows, 512, SUBLANES)

    def body(w_ref, g_ref, m_ref, v_ref, d_ref, nm_ref, nv_ref):
        d_ref[...], nm_ref[...], nv_ref[...] = _adam_update(w_ref[...], g_ref[...], m_ref[...], v_ref[...])

    spec = pl.BlockSpec((tr, cols), lambda i: (i, 0))
    return pl.pallas_call(
        functools.partial(body), name=name, grid=(rows // tr,),
        out_shape=(jax.ShapeDtypeStruct((rows, cols), F32),) * 3,
        in_specs=[spec] * 4, out_specs=(spec,) * 3,
        compiler_params=_params("parallel"),
    )(w, g, m, v)


def _adamw_small(items, name):
    n = len(items)

    def body(*refs):
        ins, outs = refs[:4 * n], refs[4 * n:]
        for i in range(n):
            w_ref, g_ref, m_ref, v_ref = ins[4 * i:4 * i + 4]
            outs[3 * i][...], outs[3 * i + 1][...], outs[3 * i + 2][...] = _adam_update(
                w_ref[...], g_ref[...], m_ref[...], v_ref[...])

    outs = pl.pallas_call(
        body, name=name,
        out_shape=tuple(jax.ShapeDtypeStruct(it[0].shape, F32) for it in items for _ in range(3)),
        in_specs=[_vmem()] * (4 * n), out_specs=(_vmem(),) * (3 * n),
        compiler_params=pltpu.CompilerParams(vmem_limit_bytes=VMEM_LIMIT),
    )(*[a for it in items for a in it])
    return [tuple(outs[3 * i:3 * i + 3]) for i in range(n)]


def _sum_adamw(buf, w, m, v, name):
    _, rows, cols = buf.shape
    tr = _div_tile(rows, 256, 2 * SUBLANES)

    def body(b_ref, w_ref, m_ref, v_ref, g_ref, d_ref, nm_ref, nv_ref):
        g = b_ref[0].astype(F32)
        for k in range(1, N_DEV):
            g = g + b_ref[k].astype(F32)
        g_ref[...] = g
        d_ref[...], nm_ref[...], nv_ref[...] = _adam_update(w_ref[...], g, m_ref[...], v_ref[...])

    spec = pl.BlockSpec((tr, cols), lambda i: (i, 0))
    return pl.pallas_call(
        functools.partial(body), name=name, grid=(rows // tr,),
        out_shape=(jax.ShapeDtypeStruct((rows, cols), F32),) * 4,
        in_specs=[pl.BlockSpec((N_DEV, tr, cols), lambda i: (0, i, 0))] + [spec] * 3, out_specs=(spec,) * 4,
        compiler_params=_params("parallel"),
    )(buf, w, m, v)


def _rope_tables(n_ctx, n):
    n_freq = RET_DIM // 4
    inv = np.float32(ROPE_BASE) ** (-np.arange(n_freq, dtype=np.float32) / np.float32(n_freq))
    tok = np.arange(n)
    pos_r = (tok // GRID_W).astype(np.float32)
    pos_c = (tok % GRID_W).astype(np.float32)
    ang_r = (pos_r[:, None] * inv[None, :]).astype(np.float32)
    ang_c = (pos_c[:, None] * inv[None, :]).astype(np.float32)
    cos = np.concatenate([np.cos(ang_r), np.cos(ang_r), np.cos(ang_c), np.cos(ang_c)], axis=-1)
    sin = np.concatenate([-np.sin(ang_r), np.sin(ang_r), -np.sin(ang_c), np.sin(ang_c)], axis=-1)
    cos = np.concatenate([np.ones((n_ctx, RET_DIM), np.float32), cos], axis=0)
    sin = np.concatenate([np.zeros((n_ctx, RET_DIM), np.float32), sin], axis=0)
    return jnp.asarray(cos, F32), jnp.asarray(sin, F32)


def _na_tables():
    q = np.arange(GRID_W)[:, None]
    k = np.arange(GRID_W)[None, :]
    start = np.clip(q - NA_KW // 2, 0, GRID_W - NA_KW)
    valid = (k >= start) & (k < start + NA_KW)
    dc = np.clip(k - q + (NA_KW - 1), 0, 2 * NA_KW - 2)
    ncls = 2 * NA_KW - 1
    onehot = (dc[None] == np.arange(ncls)[:, None, None]) & valid[None]
    return onehot.astype(np.float32), valid


def _paired_bias(rpb, onehot, valid):
    ncls = onehot.shape[0]
    pair = np.zeros((2 * ncls, GRID_W, LANES), np.float32)
    pair[:ncls, :, :GRID_W] = onehot
    pair[ncls:, :, GRID_W:] = onehot
    rows = jnp.concatenate([rpb[:, :-1], rpb[:, 1:]], axis=-1)
    t = jnp.einsum("hdc,cqk->hdqk", rows, jnp.asarray(pair), precision=lax.Precision.HIGHEST)
    return jnp.where(jnp.asarray(np.tile(valid, (1, 2)))[None, None], t, NEG_INF)


def kernel(x, c, ctx, c_ctx, w_ada, b_ada, g_pre_mix, g_post_mix, g_pre_mlp, g_post_mlp, w_in, ret_decay, ret_gn, na_rpb, w_out, w_mlp1, w_mlp2, loss_target, m_c_ctx, m_w_ada, m_b_ada, m_g_pre_mix, m_g_post_mix, m_g_pre_mlp, m_g_post_mlp, m_w_in, m_ret_decay, m_ret_gn, m_na_rpb, m_w_out, m_w_mlp1, m_w_mlp2, v_c_ctx, v_w_ada, v_b_ada, v_g_pre_mix, v_g_post_mix, v_g_pre_mlp, v_g_post_mlp, v_w_in, v_ret_decay, v_ret_gn, v_na_rpb, v_w_out, v_w_mlp1, v_w_mlp2):
    B, N, D = x.shape
    C = ctx.shape[1]
    T = C + N

    silu_all, mods_g, win_b, wout_l, w1_l, w2_l = _mod_gather(c, c_ctx, w_ada[0], b_ada, w_in[0].T, w_out[0],
                                                             w_mlp1[0], w_mlp2[0])
    mods_mine = mods_g.transpose(1, 0, 2).reshape(mods_g.shape[1], N_MOD * D)
    modl = jnp.concatenate([mods_mine[:B], mods_mine[SUBLANES:SUBLANES + 1]], axis=0)
    modl = modl.reshape(B + 1, N_MOD, 1, D)
    rin = w_in.shape[2]
    rout, c1, r2 = wout_l.shape[0], w1_l.shape[1], w2_l.shape[0]

    def rows_of(n):
        return lambda ref: _row_block(ref, n)

    def cols_of(n):
        return lambda ref: _col_block(ref, n)

    cos, sin = _rope_tables(C, N)
    onehot, valid = _na_tables()
    bias2 = _paired_bias(na_rpb[0], onehot, valid)
    lg = jax.nn.log_sigmoid(ret_decay[0].astype(F32))

    ag = _SplitScatter([wout_l, w1_l, w2_l], [rows_of(rout), cols_of(c1), rows_of(r2)],
                       [(N_DEV * rout, D), (D, N_DEV * c1), (N_DEV * r2, D)], "ag_mlp_start",
                       kind="gather", masks=SIBLING + ICI_SAME_CORE)
    h_all, proj = _inproj_fwd(x, ctx, modl, g_pre_mix, win_b, after=ag.token)
    o_ret, lat_ret, q_rot, k_rot = _ret_fwd(proj, cos, sin, lg, ret_gn, C)
    lat_na, na_probs = _na_fwd(proj, bias2, C)
    wout_part, w1_part, w2_part = _scatter_wait([ag], lat_na, "ag_mlp_wait")

    (dy1, dlat_ret, dlat_na, dmix, h2, act, du, dz, red_d) = _dense_core(
        lat_ret, lat_na, x, loss_target, modl, g_post_mix, g_pre_mlp, g_post_mlp, wout_part, w1_part, w2_part)

    gw_out_p = _tn_matmul(lat_ret[:, None], dmix, "gw_out_ret", rows_after=lat_na.shape[-1])
    gw_out_p = _tn_matmul(lat_na[:, None], dmix, "gw_out_na", rows_before=lat_ret.shape[-1], into=gw_out_p)
    gw1_p = _tn_matmul(h2[:, None], du, "gw_mlp1")
    gw2_p = _tn_matmul(act[:, None], dz, "gw_mlp2")
    rs_mlp = _SplitScatter([gw_out_p, gw1_p, gw2_p], [rows_of(rout), cols_of(c1), rows_of(r2)],
                           [(rout, D), (D, c1), (r2, D)], "rs_mlp_start")

    dret, dgn_p, dlg_p = _ret_bwd(proj, q_rot, k_rot, cos, sin, lg, ret_gn, o_ret, dlat_ret, C, after=rs_mlp.token)
    dna, rr = _na_bwd(proj, na_probs, dlat_na, C)
    ret_cols, na_cols = dret.shape[1] * dret.shape[3], dna.shape[1] * dna.shape[3]
    gwin_t_p = _tn_matmul(dret, h_all, "gw_in_ret", rows_after=na_cols)
    gwin_t_p = _tn_matmul(dna, h_all, "gw_in_na", rows_before=ret_cols, into=gwin_t_p)
    rs_in = _SplitScatter([gwin_t_p], [rows_of(rin)], [(rin, D)], "rs_w_in_start")
    grad_x, red_i = _inproj_bwd(dret, dna, x, ctx, dy1, modl, g_pre_mix, win_b, after=rs_in.token)

    rd = red_d
    nct = red_i.shape[1] * C // T
    ri_ctx = red_i[:, :nct].sum(axis=(0, 1))
    ri_lat = red_i[:, nct:].sum(axis=1)
    d_mods = jnp.concatenate([ri_lat[:, 0], ri_lat[:, 1], rd[:, 0], rd[:, 4], rd[:, 3], rd[:, 2]], axis=-1)
    d_cmods = jnp.concatenate([ri_ctx[0], ri_ctx[1], jnp.zeros(((N_MOD - 2) * D,), F32)])[None]
    dg_pre_mix = ri_lat[:, 2].sum(axis=0) + ri_ctx[2]
    dg_post_mix = rd[:, 1].sum(axis=0)
    dg_pre_mlp = rd[:, 5].sum(axis=0)
    dg_post_mlp = rd[:, 6].sum(axis=0)
    loss_p = rd[:, 7, 0].sum()
    d_gn = dgn_p[:, 0].sum(axis=0)
    d_lg = dlg_p[:, :, :2, 0].sum(axis=0).T
    d_decay = d_lg * jax.nn.sigmoid(-ret_decay[0].astype(F32))
    ncls = 2 * NA_KW - 1
    d_rpb = (jnp.pad(rr[:, :, :ncls], ((0, 0), (0, 1), (0, 0)))
             + jnp.pad(rr[:, :, GRID_W:GRID_W + ncls], ((0, 0), (1, 0), (0, 0))))
    d_rpb32 = jnp.pad(d_rpb, ((0, 0), (0, 0), (0, 32 - ncls)))
    pieces = [dg_pre_mix, dg_post_mix, dg_pre_mlp, dg_post_mlp, d_gn, d_rpb32.reshape(-1),
              jnp.pad(d_decay.reshape(-1), (0, LANES - d_decay.size)), jnp.full((LANES,), loss_p, F32)]
    vec = jnp.concatenate(pieces)
    nm = N_MOD * D
    n_vec_rows = -(-vec.shape[0] // nm)
    assert B + 1 + n_vec_rows <= SUBLANES
    vec = jnp.pad(vec, (0, n_vec_rows * nm - vec.shape[0])).reshape(n_vec_rows, nm)
    dm_slot = jnp.concatenate([d_mods, d_cmods, vec, jnp.zeros((SUBLANES - B - 1 - n_vec_rows, nm), F32)], axis=0)
    def whole(ref):
        return lambda p: ref

    small = _SplitScatter([dm_slot], [whole], [dm_slot.shape], "small_start")
    land_out, land_1, land_2 = _scatter_wait([rs_mlp], small.token, "rs_mlp_wait")
    fused = {"w_out": _sum_adamw(land_out, w_out[0], m_w_out[0], v_w_out[0], "sum_adamw_w_out"),
             "w_mlp1": _sum_adamw(land_1, w_mlp1[0], m_w_mlp1[0], v_w_mlp1[0], "sum_adamw_w_mlp1"),
             "w_mlp2": _sum_adamw(land_2, w_mlp2[0], m_w_mlp2[0], v_w_mlp2[0], "sum_adamw_w_mlp2")}
    (land_in,) = _scatter_wait([rs_in], fused["w_mlp2"][0], "rs_w_in_wait")
    win_upd = _sum_adamw(land_in, w_in[0].T, m_w_in[0].T, v_w_in[0].T, "sum_adamw_w_in")
    fused["w_in"] = [a.T for a in win_upd]
    (mbuf,) = _scatter_wait([small], win_upd[0], "small_wait")
    tot, g_b_ada, g_w_ada, g_c_ctx = _small_ar(mbuf, silu_all, w_ada[0], c_ctx, B + 1, n_vec_rows)
    flat = tot.reshape(-1)
    o0 = 0
    g_pre_mix_g = flat[o0:o0 + D]; o0 += D
    g_post_mix_g = flat[o0:o0 + D]; o0 += D
    g_pre_mlp_g = flat[o0:o0 + D]; o0 += D
    g_post_mlp_g = flat[o0:o0 + D]; o0 += D
    g_gn = flat[o0:o0 + RET_WIDTH]; o0 += RET_WIDTH
    nrpb = NA_HEADS * (2 * NA_KH - 1) * 32
    g_rpb = flat[o0:o0 + nrpb].reshape(NA_HEADS, 2 * NA_KH - 1, 32)[:, :, :ncls]; o0 += nrpb
    g_decay = flat[o0:o0 + 2 * RET_HEADS].reshape(2, RET_HEADS); o0 += LANES
    loss = flat[o0]

    grads = {
        "c_ctx": g_c_ctx.reshape(c_ctx.shape), "w_ada": g_w_ada[None], "b_ada": g_b_ada.reshape(b_ada.shape),
        "g_pre_mix": g_pre_mix_g[None], "g_post_mix": g_post_mix_g[None], "g_pre_mlp": g_pre_mlp_g[None],
        "g_post_mlp": g_post_mlp_g[None], "w_in": fused["w_in"][0][None], "ret_decay": g_decay[None], "ret_gn": g_gn[None],
        "na_rpb": g_rpb[None], "w_out": fused["w_out"][0][None], "w_mlp1": fused["w_mlp1"][0][None],
        "w_mlp2": fused["w_mlp2"][0][None],
    }
    weights = dict(c_ctx=c_ctx, w_ada=w_ada, b_ada=b_ada, g_pre_mix=g_pre_mix, g_post_mix=g_post_mix,
                   g_pre_mlp=g_pre_mlp, g_post_mlp=g_post_mlp, w_in=w_in, ret_decay=ret_decay, ret_gn=ret_gn,
                   na_rpb=na_rpb, w_out=w_out, w_mlp1=w_mlp1, w_mlp2=w_mlp2)
    m_in = dict(c_ctx=m_c_ctx, w_ada=m_w_ada, b_ada=m_b_ada, g_pre_mix=m_g_pre_mix, g_post_mix=m_g_post_mix,
                g_pre_mlp=m_g_pre_mlp, g_post_mlp=m_g_post_mlp, w_in=m_w_in, ret_decay=m_ret_decay,
                ret_gn=m_ret_gn, na_rpb=m_na_rpb, w_out=m_w_out, w_mlp1=m_w_mlp1, w_mlp2=m_w_mlp2)
    v_in = dict(c_ctx=v_c_ctx, w_ada=v_w_ada, b_ada=v_b_ada, g_pre_mix=v_g_pre_mix, g_post_mix=v_g_post_mix,
                g_pre_mlp=v_g_pre_mlp, g_post_mlp=v_g_post_mlp, w_in=v_w_in, ret_decay=v_ret_decay,
                ret_gn=v_ret_gn, na_rpb=v_na_rpb, w_out=v_w_out, w_mlp1=v_w_mlp1, w_mlp2=v_w_mlp2)
    names = list(weights)
    deltas, new_m, new_v = {}, {}, {}
    def as_2d(n):
        shp = weights[n].shape
        two_d = (-1, shp[-1]) if len(shp) > 1 else (1, shp[0])
        return [a.reshape(two_d) for a in (weights[n], grads[n], m_in[n], v_in[n])]

    small = [n for n in names if n not in fused and weights[n].size <= 65536]
    updated = dict(zip(small, _adamw_small([as_2d(n) for n in small], "adamw_small")))
    for n in names:
        if n in fused:
            updated[n] = fused[n][1:]
        elif n not in updated:
            updated[n] = _adamw(*as_2d(n), "adamw_" + n)
        deltas[n], new_m[n], new_v[n] = (a.reshape(weights[n].shape) for a in updated[n])
    return (loss, grad_x, *[grads[n] for n in names], *[deltas[n] for n in names],
            *[new_m[n] for n in names], *[new_v[n] for n in names])
```
